```python
import jax, jax.numpy as jnp
from jax import lax
import numpy as np

D_MODEL = 1024
BATCH = 8
SEQ = 2048
DEPTH = 1

MIX_WIDTH = D_MODEL
HEAD_DIM = 64
RWKV_WIDTH = MIX_WIDTH // 2
ATTN_WIDTH = MIX_WIDTH - RWKV_WIDTH
RWKV_HEADS = RWKV_WIDTH // HEAD_DIM
ATTN_HEADS = ATTN_WIDTH // HEAD_DIM
DECAY_LORA = 64
AAA_LORA = 64
GATE_LORA = 128
DILATED_PAIRS = ((128, 1), (512, 4), (2048, 16))
ATTN_BLOCK = 128
D_FF = -(-8 * D_MODEL // (3 * 256)) * 256
NORM_EPS = 1e-6
GN_EPS = 64e-5
RWKV_SHIFT_COLS = 3 * RWKV_WIDTH + DECAY_LORA + AAA_LORA + GATE_LORA
IN_COLS = RWKV_SHIFT_COLS + 3 * ATTN_WIDTH

kernel_name = 'hybrid_rwkv7_dilated_attn'


def rmsnorm(x, g):
    xf = x.astype(jnp.float32)
    y = xf * lax.rsqrt(jnp.mean(xf * xf, axis=-1, keepdims=True) + NORM_EPS)
    return (y * g.astype(jnp.float32)).astype(x.dtype)


def wkv7_scan(r, w, k, v, kk, a):
    B, S, H, N = r.shape

    def step(state, inp):
        r_t, w_t, k_t, v_t, kk_t, a_t = inp
        sa = jnp.einsum('bhij,bhj->bhi', state, -kk_t)
        state = (state * w_t[:, :, None, :]
                 + sa[..., :, None] * (kk_t * a_t)[..., None, :]
                 + v_t[..., :, None] * k_t[..., None, :])
        y = jnp.einsum('bhij,bhj->bhi', state, r_t)
        return state, y

    xs = tuple(t.transpose(1, 0, 2, 3) for t in (r, w, k, v, kk, a))
    init = jnp.zeros((B, H, N, N), jnp.float32)
    _, ys = lax.scan(step, init, xs)
    return ys.transpose(1, 0, 2, 3)


def rwkv7_mixer(p, mu, w0, w2, a0, a2, g2, k_k, k_a, r_k, ln_w, ln_b):
    B, S, _ = p.shape
    prev = jnp.pad(p[:, :-1], ((0, 0), (1, 0), (0, 0)))
    p = p + (prev - p) * mu
    W = RWKV_WIDTH
    r, k, v, xw, xa, xg = jnp.split(
        p, [W, 2 * W, 3 * W, 3 * W + DECAY_LORA, 3 * W + DECAY_LORA + AAA_LORA], axis=-1)
    w = -jax.nn.softplus(-(w0 + jnp.tanh(xw) @ w2)) - 0.5
    decay = jnp.exp(-jnp.exp(w.astype(jnp.float32)))
    a = jax.nn.sigmoid(a0 + xa @ a2)
    g = jax.nn.sigmoid(xg) @ g2

    def hd(t):
        return t.reshape(B, S, RWKV_HEADS, HEAD_DIM).astype(jnp.float32)

    kk = hd(k * k_k)
    kk = kk / jnp.maximum(jnp.sqrt(jnp.sum(kk * kk, axis=-1, keepdims=True)), 1e-12)
    k = k * (1 + (a - 1) * k_a)
    rh, kh, vh, ah, wh = hd(r), hd(k), hd(v), hd(a), hd(decay)
    y = wkv7_scan(rh, wh, kh, vh, kk, ah)
    mean = jnp.mean(y, axis=-1, keepdims=True)
    var = jnp.mean(jnp.square(y - mean), axis=-1, keepdims=True)
    y = ((y - mean) * lax.rsqrt(var + GN_EPS)).reshape(B, S, W)
    y = y * ln_w.astype(jnp.float32) + ln_b.astype(jnp.float32)
    bonus = jnp.sum(rh * kh * r_k.astype(jnp.float32), axis=-1, keepdims=True) * vh
    y = y + bonus.reshape(B, S, W)
    return (y * g.astype(jnp.float32)).astype(p.dtype)


def dilated_branch(q, k, v, window, dilation):
    B, H, S, Dh = q.shape
    L = S // dilation
    span = window // dilation
    nb = -(-L // ATTN_BLOCK)
    Lp = nb * ATTN_BLOCK

    def to_sub(t):
        t = t.reshape(B, H, L, dilation, Dh).transpose(0, 1, 3, 2, 4)
        t = jnp.pad(t, ((0, 0), (0, 0), (0, 0), (0, Lp - L), (0, 0)))
        return t.reshape(B, H, dilation, nb, ATTN_BLOCK, Dh)

    def with_prev(t):
        prv = jnp.pad(t[:, :, :, :-1], ((0, 0), (0, 0), (0, 0), (1, 0), (0, 0), (0, 0)))
        return jnp.concatenate([prv, t], axis=4)

    qb = to_sub(q)
    kc = with_prev(to_sub(k))
    vc = with_prev(to_sub(v))
    s = jnp.einsum('bhrnqe,bhrnke->bhrnqk', qb, kc) * (Dh ** -0.5)
    qi = jnp.arange(ATTN_BLOCK)[:, None] + ATTN_BLOCK
    kj = jnp.arange(2 * ATTN_BLOCK)[None, :]
    rel = qi - kj
    blk = jnp.arange(nb)[:, None, None]
    valid = (rel >= 0) & (rel <= span) & ((blk - 1) * ATTN_BLOCK + kj >= 0)
    s = jnp.where(valid, s, -jnp.inf)
    m = jnp.max(s, axis=-1, keepdims=True)
    pe = jnp.exp(s - m)
    den = jnp.sum(pe, axis=-1, keepdims=True)
    o = jnp.einsum('bhrnqk,bhrnke->bhrnqe', pe, vc) / den
    lse = (m + jnp.log(den))[..., 0]
    o = o.reshape(B, H, dilation, Lp, Dh)[:, :, :, :L].transpose(0, 1, 3, 2, 4).reshape(B, H, S, Dh)
    lse = lse.reshape(B, H, dilation, Lp)[..., :L].transpose(0, 1, 3, 2).reshape(B, H, S)
    return o, lse


def dilated_attention(q, k, v, out_g):
    B, S, _ = q.shape

    def heads(t):
        return t.reshape(B, S, ATTN_HEADS, HEAD_DIM).transpose(0, 2, 1, 3).astype(jnp.float32)

    qh, kh, vh = heads(q), heads(k), heads(v)
    outs, lses = [], []
    for window, dilation in DILATED_PAIRS:
        o, l = dilated_branch(qh, kh, vh, window, dilation)
        outs.append(o)
        lses.append(l)
    alpha = jax.nn.softmax(jnp.stack(lses), axis=0)
    o = jnp.sum(alpha[..., None] * jnp.stack(outs), axis=0).transpose(0, 2, 1, 3)
    o = o * lax.rsqrt(jnp.mean(o * o, axis=-1, keepdims=True) + NORM_EPS)
    o = o.reshape(B, S, ATTN_WIDTH) * out_g.astype(jnp.float32)
    return o.astype(q.dtype)


def _fwd_setup_inputs(seed: int = 0) -> dict:
    key = jax.random.key(seed)
    ks = jax.random.split(key, 24)
    nrm = jax.random.normal
    f32 = jnp.float32
    return {
        'x': nrm(ks[0], (BATCH, SEQ, D_MODEL), f32),
        'mix_norm_g': 1.0 + 0.02 * nrm(ks[1], (DEPTH, D_MODEL), f32),
        'w_in': nrm(ks[2], (DEPTH, D_MODEL, IN_COLS), f32) * D_MODEL ** -0.5,
        'mu_shift': jax.random.uniform(ks[3], (DEPTH, RWKV_SHIFT_COLS), f32),
        'decay_w0': jax.random.uniform(ks[4], (DEPTH, RWKV_WIDTH), f32, minval=-6.0, maxval=-1.0),
        'decay_w2': nrm(ks[5], (DEPTH, DECAY_LORA, RWKV_WIDTH), f32) * 0.1 * DECAY_LORA ** -0.5,
        'iclr_a0': 0.1 * nrm(ks[6], (DEPTH, RWKV_WIDTH), f32),
        'iclr_a2': nrm(ks[7], (DEPTH, AAA_LORA, RWKV_WIDTH), f32) * 0.1 * AAA_LORA ** -0.5,
        'gate_g2': nrm(ks[8], (DEPTH, GATE_LORA, RWKV_WIDTH), f32) * GATE_LORA ** -0.5,
        'k_k': 0.85 + 0.02 * nrm(ks[9], (DEPTH, RWKV_WIDTH), f32),
        'k_a': 1.0 + 0.02 * nrm(ks[10], (DEPTH, RWKV_WIDTH), f32),
        'r_k': 0.1 * nrm(ks[11], (DEPTH, RWKV_HEADS, HEAD_DIM), f32),
        'ln_x_w': 1.0 + 0.02 * nrm(ks[12], (DEPTH, RWKV_WIDTH), f32),
        'ln_x_b': 0.02 * nrm(ks[13], (DEPTH, RWKV_WIDTH), f32),
        'attn_out_g': 1.0 + 0.02 * nrm(ks[14], (DEPTH, ATTN_WIDTH), f32),
        'w_out': nrm(ks[15], (DEPTH, MIX_WIDTH, D_MODEL), f32) * MIX_WIDTH ** -0.5,
        'ffn_norm_g': 1.0 + 0.02 * nrm(ks[16], (DEPTH, D_MODEL), f32),
        'w_gate': nrm(ks[17], (DEPTH, D_MODEL, D_FF), f32) * D_MODEL ** -0.5,
        'w_up': nrm(ks[18], (DEPTH, D_MODEL, D_FF), f32) * D_MODEL ** -0.5,
        'w_down': nrm(ks[19], (DEPTH, D_FF, D_MODEL), f32) * D_FF ** -0.5,
        'final_norm_g': 1.0 + 0.02 * nrm(ks[20], (D_MODEL,), f32),
    }


def _fwd_reference(x, mix_norm_g, w_in, mu_shift, decay_w0, decay_w2, iclr_a0, iclr_a2, gate_g2,
              k_k, k_a, r_k, ln_x_w, ln_x_b, attn_out_g, w_out, ffn_norm_g, w_gate, w_up,
              w_down, final_norm_g):
    c0 = RWKV_SHIFT_COLS
    for i in range(DEPTH):
        h = rmsnorm(x, mix_norm_g[i])
        proj = h @ w_in[i]
        p_a, q, k, v = jnp.split(proj, [c0, c0 + ATTN_WIDTH, c0 + 2 * ATTN_WIDTH], axis=-1)
        y_a = rwkv7_mixer(p_a, mu_shift[i], decay_w0[i], decay_w2[i], iclr_a0[i], iclr_a2[i],
                          gate_g2[i], k_k[i], k_a[i], r_k[i], ln_x_w[i], ln_x_b[i])
        y_b = dilated_attention(q, k, v, attn_out_g[i])
        x = x + jnp.concatenate([y_a, y_b.astype(y_a.dtype)], axis=-1) @ w_out[i]
        h = rmsnorm(x, ffn_norm_g[i])
        x = x + (jax.nn.silu(h @ w_gate[i]) * (h @ w_up[i])) @ w_down[i]
    return rmsnorm(x, final_norm_g)


import jax as _jax
import jax.numpy as _jnp

TWIN_FORMAT = 'train_step'
FWD_PARAMS = ['x', 'mix_norm_g', 'w_in', 'mu_shift', 'decay_w0', 'decay_w2', 'iclr_a0', 'iclr_a2', 'gate_g2', 'k_k', 'k_a', 'r_k', 'ln_x_w', 'ln_x_b', 'attn_out_g', 'w_out', 'ffn_norm_g', 'w_gate', 'w_up', 'w_down', 'final_norm_g']
TWIN_WEIGHTS = ['mix_norm_g', 'w_in', 'mu_shift', 'decay_w0', 'decay_w2', 'iclr_a0', 'iclr_a2', 'gate_g2', 'k_k', 'k_a', 'r_k', 'ln_x_w', 'ln_x_b', 'attn_out_g', 'w_out', 'ffn_norm_g', 'w_gate', 'w_up', 'w_down', 'final_norm_g']
TWIN_DIFF_INPUT = 'x'
TWIN_INPUTS = ['x', 'mix_norm_g', 'w_in', 'mu_shift', 'decay_w0', 'decay_w2', 'iclr_a0', 'iclr_a2', 'gate_g2', 'k_k', 'k_a', 'r_k', 'ln_x_w', 'ln_x_b', 'attn_out_g', 'w_out', 'ffn_norm_g', 'w_gate', 'w_up', 'w_down', 'final_norm_g', 'loss_target', 'm_mix_norm_g', 'm_w_in', 'm_mu_shift', 'm_decay_w0', 'm_decay_w2', 'm_iclr_a0', 'm_iclr_a2', 'm_gate_g2', 'm_k_k', 'm_k_a', 'm_r_k', 'm_ln_x_w', 'm_ln_x_b', 'm_attn_out_g', 'm_w_out', 'm_ffn_norm_g', 'm_w_gate', 'm_w_up', 'm_w_down', 'm_final_norm_g', 'v_mix_norm_g', 'v_w_in', 'v_mu_shift', 'v_decay_w0', 'v_decay_w2', 'v_iclr_a0', 'v_iclr_a2', 'v_gate_g2', 'v_k_k', 'v_k_a', 'v_r_k', 'v_ln_x_w', 'v_ln_x_b', 'v_attn_out_g', 'v_w_out', 'v_ffn_norm_g', 'v_w_gate', 'v_w_up', 'v_w_down', 'v_final_norm_g']
TWIN_OUTPUTS = ['loss', 'grad_x', 'grad_mix_norm_g', 'grad_w_in', 'grad_mu_shift', 'grad_decay_w0', 'grad_decay_w2', 'grad_iclr_a0', 'grad_iclr_a2', 'grad_gate_g2', 'grad_k_k', 'grad_k_a', 'grad_r_k', 'grad_ln_x_w', 'grad_ln_x_b', 'grad_attn_out_g', 'grad_w_out', 'grad_ffn_norm_g', 'grad_w_gate', 'grad_w_up', 'grad_w_down', 'grad_final_norm_g', 'delta_mix_norm_g', 'delta_w_in', 'delta_mu_shift', 'delta_decay_w0', 'delta_decay_w2', 'delta_iclr_a0', 'delta_iclr_a2', 'delta_gate_g2', 'delta_k_k', 'delta_k_a', 'delta_r_k', 'delta_ln_x_w', 'delta_ln_x_b', 'delta_attn_out_g', 'delta_w_out', 'delta_ffn_norm_g', 'delta_w_gate', 'delta_w_up', 'delta_w_down', 'delta_final_norm_g', 'new_m_mix_norm_g', 'new_m_w_in', 'new_m_mu_shift', 'new_m_decay_w0', 'new_m_decay_w2', 'new_m_iclr_a0', 'new_m_iclr_a2', 'new_m_gate_g2', 'new_m_k_k', 'new_m_k_a', 'new_m_r_k', 'new_m_ln_x_w', 'new_m_ln_x_b', 'new_m_attn_out_g', 'new_m_w_out', 'new_m_ffn_norm_g', 'new_m_w_gate', 'new_m_w_up', 'new_m_w_down', 'new_m_final_norm_g', 'new_v_mix_norm_g', 'new_v_w_in', 'new_v_mu_shift', 'new_v_decay_w0', 'new_v_decay_w2', 'new_v_iclr_a0', 'new_v_iclr_a2', 'new_v_gate_g2', 'new_v_k_k', 'new_v_k_a', 'new_v_r_k', 'new_v_ln_x_w', 'new_v_ln_x_b', 'new_v_attn_out_g', 'new_v_w_out', 'new_v_ffn_norm_g', 'new_v_w_gate', 'new_v_w_up', 'new_v_w_down', 'new_v_final_norm_g']
TWIN_LEAF_KINDS = {'loss': 'loss', 'grad_x': 'grad_x', 'grad_mix_norm_g': 'grad_w', 'grad_w_in': 'grad_w', 'grad_mu_shift': 'grad_w', 'grad_decay_w0': 'grad_w', 'grad_decay_w2': 'grad_w', 'grad_iclr_a0': 'grad_w', 'grad_iclr_a2': 'grad_w', 'grad_gate_g2': 'grad_w', 'grad_k_k': 'grad_w', 'grad_k_a': 'grad_w', 'grad_r_k': 'grad_w', 'grad_ln_x_w': 'grad_w', 'grad_ln_x_b': 'grad_w', 'grad_attn_out_g': 'grad_w', 'grad_w_out': 'grad_w', 'grad_ffn_norm_g': 'grad_w', 'grad_w_gate': 'grad_w', 'grad_w_up': 'grad_w', 'grad_w_down': 'grad_w', 'grad_final_norm_g': 'grad_w', 'delta_mix_norm_g': 'delta_w', 'delta_w_in': 'delta_w', 'delta_mu_shift': 'delta_w', 'delta_decay_w0': 'delta_w', 'delta_decay_w2': 'delta_w', 'delta_iclr_a0': 'delta_w', 'delta_iclr_a2': 'delta_w', 'delta_gate_g2': 'delta_w', 'delta_k_k': 'delta_w', 'delta_k_a': 'delta_w', 'delta_r_k': 'delta_w', 'delta_ln_x_w': 'delta_w', 'delta_ln_x_b': 'delta_w', 'delta_attn_out_g': 'delta_w', 'delta_w_out': 'delta_w', 'delta_ffn_norm_g': 'delta_w', 'delta_w_gate': 'delta_w', 'delta_w_up': 'delta_w', 'delta_w_down': 'delta_w', 'delta_final_norm_g': 'delta_w', 'new_m_mix_norm_g': 'new_m', 'new_m_w_in': 'new_m', 'new_m_mu_shift': 'new_m', 'new_m_decay_w0': 'new_m', 'new_m_decay_w2': 'new_m', 'new_m_iclr_a0': 'new_m', 'new_m_iclr_a2': 'new_m', 'new_m_gate_g2': 'new_m', 'new_m_k_k': 'new_m', 'new_m_k_a': 'new_m', 'new_m_r_k': 'new_m', 'new_m_ln_x_w': 'new_m', 'new_m_ln_x_b': 'new_m', 'new_m_attn_out_g': 'new_m', 'new_m_w_out': 'new_m', 'new_m_ffn_norm_g': 'new_m', 'new_m_w_gate': 'new_m', 'new_m_w_up': 'new_m', 'new_m_w_down': 'new_m', 'new_m_final_norm_g': 'new_m', 'new_v_mix_norm_g': 'new_v', 'new_v_w_in': 'new_v', 'new_v_mu_shift': 'new_v', 'new_v_decay_w0': 'new_v', 'new_v_decay_w2': 'new_v', 'new_v_iclr_a0': 'new_v', 'new_v_iclr_a2': 'new_v', 'new_v_gate_g2': 'new_v', 'new_v_k_k': 'new_v', 'new_v_k_a': 'new_v', 'new_v_r_k': 'new_v', 'new_v_ln_x_w': 'new_v', 'new_v_ln_x_b': 'new_v', 'new_v_attn_out_g': 'new_v', 'new_v_w_out': 'new_v', 'new_v_ffn_norm_g': 'new_v', 'new_v_w_gate': 'new_v', 'new_v_w_up': 'new_v', 'new_v_w_down': 'new_v', 'new_v_final_norm_g': 'new_v'}


def _forward(args):
    return _fwd_reference(*[args[k] for k in FWD_PARAMS])


def _output_shape():
    out = _jax.eval_shape(lambda: _forward(_fwd_setup_inputs(0)))
    return out.shape, out.dtype

N_MICROBATCH = 1
ADAM_LR = 0.001
ADAM_B1 = 0.9
ADAM_B2 = 0.999
ADAM_EPS = 1e-08
ADAM_WD = 0.01
ADAM_STEP = 10
PER_EXAMPLE_BATCH_AXIS = {'x': 0, 'loss_target': 0}
SHARED_INPUTS = []
_WEIGHT_DTYPES = {'mix_norm_g': _jnp.float32, 'w_in': _jnp.float32, 'mu_shift': _jnp.float32, 'decay_w0': _jnp.float32, 'decay_w2': _jnp.float32, 'iclr_a0': _jnp.float32, 'iclr_a2': _jnp.float32, 'gate_g2': _jnp.float32, 'k_k': _jnp.float32, 'k_a': _jnp.float32, 'r_k': _jnp.float32, 'ln_x_w': _jnp.float32, 'ln_x_b': _jnp.float32, 'attn_out_g': _jnp.float32, 'w_out': _jnp.float32, 'ffn_norm_g': _jnp.float32, 'w_gate': _jnp.float32, 'w_up': _jnp.float32, 'w_down': _jnp.float32, 'final_norm_g': _jnp.float32}
MOMENT_SCALE = {'mix_norm_g': 1.514336e-01, 'w_in': 8.351615e-02, 'mu_shift': 8.946420e-02, 'decay_w0': 2.140318e-02, 'decay_w2': 2.554141e-03, 'iclr_a0': 2.508563e-02, 'iclr_a2': 2.410391e-02, 'gate_g2': 5.892251e-02, 'k_k': 6.917141e-02, 'k_a': 6.088699e-02, 'r_k': 1.226547e-01, 'ln_x_w': 6.177607e-02, 'ln_x_b': 5.605014e-02, 'attn_out_g': 1.073678e-01, 'w_out': 8.612736e-02, 'ffn_norm_g': 7.834596e-02, 'w_gate': 3.294075e-02, 'w_up': 3.193100e-02, 'w_down': 5.287230e-02, 'final_norm_g': 1.600074e+01}


def _to_microbatches(a, axis):
    t = _jnp.moveaxis(a, axis, 0)
    t = t.reshape((N_MICROBATCH, t.shape[0] // N_MICROBATCH) + t.shape[1:])
    return _jnp.moveaxis(t, 1, axis + 1)


def setup_inputs(seed: int = 0) -> dict:
    inp = _fwd_setup_inputs(seed)
    key = _jax.random.fold_in(_jax.random.key(seed), 7919)
    shape, _ = _output_shape()
    out = dict(inp)
    out["loss_target"] = _jax.random.normal(_jax.random.fold_in(key, 0), shape, _jnp.float32)
    for i, name in enumerate(TWIN_WEIGHTS):
        w = inp[name].astype(_jnp.float32)
        if MOMENT_SCALE is None:
            s = _jnp.sqrt(_jnp.mean(_jnp.square(w)) + 1e-30)
        else:
            s = MOMENT_SCALE[name]
        km, kv = _jax.random.split(_jax.random.fold_in(key, i + 1))
        out[name] = w
        out["m_" + name] = s * _jax.random.normal(km, w.shape, _jnp.float32)
        out["v_" + name] = (s * s) * _jax.random.uniform(kv, w.shape, _jnp.float32, 0.5, 1.5)
    if N_MICROBATCH > 1:
        for name, axis in PER_EXAMPLE_BATCH_AXIS.items():
            out[name] = _to_microbatches(out[name], axis)
    return {'x': out['x'], 'mix_norm_g': out['mix_norm_g'], 'w_in': out['w_in'], 'mu_shift': out['mu_shift'], 'decay_w0': out['decay_w0'], 'decay_w2': out['decay_w2'], 'iclr_a0': out['iclr_a0'], 'iclr_a2': out['iclr_a2'], 'gate_g2': out['gate_g2'], 'k_k': out['k_k'], 'k_a': out['k_a'], 'r_k': out['r_k'], 'ln_x_w': out['ln_x_w'], 'ln_x_b': out['ln_x_b'], 'attn_out_g': out['attn_out_g'], 'w_out': out['w_out'], 'ffn_norm_g': out['ffn_norm_g'], 'w_gate': out['w_gate'], 'w_up': out['w_up'], 'w_down': out['w_down'], 'final_norm_g': out['final_norm_g'], 'loss_target': out['loss_target'], 'm_mix_norm_g': out['m_mix_norm_g'], 'm_w_in': out['m_w_in'], 'm_mu_shift': out['m_mu_shift'], 'm_decay_w0': out['m_decay_w0'], 'm_decay_w2': out['m_decay_w2'], 'm_iclr_a0': out['m_iclr_a0'], 'm_iclr_a2': out['m_iclr_a2'], 'm_gate_g2': out['m_gate_g2'], 'm_k_k': out['m_k_k'], 'm_k_a': out['m_k_a'], 'm_r_k': out['m_r_k'], 'm_ln_x_w': out['m_ln_x_w'], 'm_ln_x_b': out['m_ln_x_b'], 'm_attn_out_g': out['m_attn_out_g'], 'm_w_out': out['m_w_out'], 'm_ffn_norm_g': out['m_ffn_norm_g'], 'm_w_gate': out['m_w_gate'], 'm_w_up': out['m_w_up'], 'm_w_down': out['m_w_down'], 'm_final_norm_g': out['m_final_norm_g'], 'v_mix_norm_g': out['v_mix_norm_g'], 'v_w_in': out['v_w_in'], 'v_mu_shift': out['v_mu_shift'], 'v_decay_w0': out['v_decay_w0'], 'v_decay_w2': out['v_decay_w2'], 'v_iclr_a0': out['v_iclr_a0'], 'v_iclr_a2': out['v_iclr_a2'], 'v_gate_g2': out['v_gate_g2'], 'v_k_k': out['v_k_k'], 'v_k_a': out['v_k_a'], 'v_r_k': out['v_r_k'], 'v_ln_x_w': out['v_ln_x_w'], 'v_ln_x_b': out['v_ln_x_b'], 'v_attn_out_g': out['v_attn_out_g'], 'v_w_out': out['v_w_out'], 'v_ffn_norm_g': out['v_ffn_norm_g'], 'v_w_gate': out['v_w_gate'], 'v_w_up': out['v_w_up'], 'v_w_down': out['v_w_down'], 'v_final_norm_g': out['v_final_norm_g']}


def _loss(weights, diff, rest, loss_target):
    with _jax.named_scope("forward"):
        args = {**rest, TWIN_DIFF_INPUT: diff, **{k: w.astype(_WEIGHT_DTYPES[k]) for k, w in weights.items()}}
        y = _forward(args)
    with _jax.named_scope("loss_head"):
        err = _jnp.square(y.astype(_jnp.float32) - loss_target)
        return 0.5 * _jnp.sum(_jnp.mean(err, axis=-1)) if err.ndim else 0.5 * err


def _adamw(w, g, m, v):
    m = ADAM_B1 * m + (1.0 - ADAM_B1) * g
    v = ADAM_B2 * v + (1.0 - ADAM_B2) * _jnp.square(g)
    m_hat = m / (1.0 - ADAM_B1 ** ADAM_STEP)
    v_hat = v / (1.0 - ADAM_B2 ** ADAM_STEP)
    delta = -ADAM_LR * (m_hat / (_jnp.sqrt(v_hat) + ADAM_EPS) + ADAM_WD * w)
    return delta, m, v


def reference(x, mix_norm_g, w_in, mu_shift, decay_w0, decay_w2, iclr_a0, iclr_a2, gate_g2, k_k, k_a, r_k, ln_x_w, ln_x_b, attn_out_g, w_out, ffn_norm_g, w_gate, w_up, w_down, final_norm_g, loss_target, m_mix_norm_g, m_w_in, m_mu_shift, m_decay_w0, m_decay_w2, m_iclr_a0, m_iclr_a2, m_gate_g2, m_k_k, m_k_a, m_r_k, m_ln_x_w, m_ln_x_b, m_attn_out_g, m_w_out, m_ffn_norm_g, m_w_gate, m_w_up, m_w_down, m_final_norm_g, v_mix_norm_g, v_w_in, v_mu_shift, v_decay_w0, v_decay_w2, v_iclr_a0, v_iclr_a2, v_gate_g2, v_k_k, v_k_a, v_r_k, v_ln_x_w, v_ln_x_b, v_attn_out_g, v_w_out, v_ffn_norm_g, v_w_gate, v_w_up, v_w_down, v_final_norm_g):
    given = dict(x=x, mix_norm_g=mix_norm_g, w_in=w_in, mu_shift=mu_shift, decay_w0=decay_w0, decay_w2=decay_w2, iclr_a0=iclr_a0, iclr_a2=iclr_a2, gate_g2=gate_g2, k_k=k_k, k_a=k_a, r_k=r_k, ln_x_w=ln_x_w, ln_x_b=ln_x_b, attn_out_g=attn_out_g, w_out=w_out, ffn_norm_g=ffn_norm_g, w_gate=w_gate, w_up=w_up, w_down=w_down, final_norm_g=final_norm_g, loss_target=loss_target, m_mix_norm_g=m_mix_norm_g, m_w_in=m_w_in, m_mu_shift=m_mu_shift, m_decay_w0=m_decay_w0, m_decay_w2=m_decay_w2, m_iclr_a0=m_iclr_a0, m_iclr_a2=m_iclr_a2, m_gate_g2=m_gate_g2, m_k_k=m_k_k, m_k_a=m_k_a, m_r_k=m_r_k, m_ln_x_w=m_ln_x_w, m_ln_x_b=m_ln_x_b, m_attn_out_g=m_attn_out_g, m_w_out=m_w_out, m_ffn_norm_g=m_ffn_norm_g, m_w_gate=m_w_gate, m_w_up=m_w_up, m_w_down=m_w_down, m_final_norm_g=m_final_norm_g, v_mix_norm_g=v_mix_norm_g, v_w_in=v_w_in, v_mu_shift=v_mu_shift, v_decay_w0=v_decay_w0, v_decay_w2=v_decay_w2, v_iclr_a0=v_iclr_a0, v_iclr_a2=v_iclr_a2, v_gate_g2=v_gate_g2, v_k_k=v_k_k, v_k_a=v_k_a, v_r_k=v_r_k, v_ln_x_w=v_ln_x_w, v_ln_x_b=v_ln_x_b, v_attn_out_g=v_attn_out_g, v_w_out=v_w_out, v_ffn_norm_g=v_ffn_norm_g, v_w_gate=v_w_gate, v_w_up=v_w_up, v_w_down=v_w_down, v_final_norm_g=v_final_norm_g)
    weights = {n: given[n] for n in TWIN_WEIGHTS}
    shared = {n: given[n] for n in SHARED_INPUTS}
    per_example = {n: given[n] for n in ['x']}
    grad_fn = _jax.value_and_grad(_loss, argnums=(0, 1))

    def one_microbatch(ex, loss_target):
        ex = dict(ex)
        diff = ex.pop(TWIN_DIFF_INPUT)
        return grad_fn(weights, diff, {**shared, **ex}, loss_target)

    if N_MICROBATCH == 1:
        loss, (grad_w, grad_x) = one_microbatch(per_example, given["loss_target"])
    else:
        def body(carry, xs):
            loss_sum, grad_sum = carry
            l_k, (gw_k, gx_k) = one_microbatch(xs[0], xs[1])
            with _jax.named_scope("update"):
                return (loss_sum + l_k, _jax.tree.map(_jnp.add, grad_sum, gw_k)), gx_k

        init = (_jnp.zeros((), _jnp.float32), _jax.tree.map(_jnp.zeros_like, weights))
        (loss, grad_w), grad_x = _jax.lax.scan(body, init, (per_example, given["loss_target"]))
    with _jax.named_scope("update"):
        delta_w, new_m, new_v = {}, {}, {}
        for n in TWIN_WEIGHTS:
            delta_w[n], new_m[n], new_v[n] = _adamw(weights[n], grad_w[n], given["m_" + n], given["v_" + n])
    return (loss, grad_x, *[grad_w[n] for n in TWIN_WEIGHTS], *[delta_w[n] for n in TWIN_WEIGHTS],
            *[new_m[n] for n in TWIN_WEIGHTS], *[new_v[n] for n in TWIN_WEIGHTS])
```

```python
import jax
import jax.numpy as jnp
from jax import lax
from jax.experimental import pallas as pl
from jax.experimental.pallas import tpu as pltpu

F32 = jnp.float32
BF16 = jnp.bfloat16
HI = lax.Precision.HIGHEST

D_MODEL = 1024
HEAD_DIM = 64
RW = 512
N_PAIR = RW // 128
SHIFT_COLS = 1792
IN_COLS = 3328
D_FF = 2816
NORM_EPS = 1e-6
GN_EPS = 64e-5
CHUNK = 64
SUB = 16
ATTN_BLOCK = 128
DILATIONS = (1, 4, 16)
NEG = -1e30
ADAM_LR, ADAM_B1, ADAM_B2, ADAM_EPS, ADAM_WD, ADAM_STEP = 0.001, 0.9, 0.999, 1e-08, 0.01, 10
VMEM_LIMIT = 56 * 1024 * 1024
MESH = pl.DeviceIdType.MESH


def _params(sem=None, **kw):
    return pltpu.CompilerParams(dimension_semantics=sem, vmem_limit_bytes=VMEM_LIMIT, **kw)


def _dot(a, b, prec=None):
    return lax.dot_general(a, b, (((1,), (0,)), ((), ())), preferred_element_type=F32, precision=prec)


def _dot_nt(a, b, prec=None):
    return lax.dot_general(a, b, (((1,), (1,)), ((), ())), preferred_element_type=F32, precision=prec)


def _dot_tn(a, b, prec=None):
    return lax.dot_general(a, b, (((0,), (0,)), ((), ())), preferred_element_type=F32, precision=prec)


def _seg_ones(n):
    r = lax.broadcasted_iota(jnp.int32, (n, n), 0) // HEAD_DIM
    c = lax.broadcasted_iota(jnp.int32, (n, n), 1) // HEAD_DIM
    return (r == c).astype(F32)


def _segsum(x, seg):
    return _dot(x, seg, HI)


def _rms_fwd(x, g):
    rstd = lax.rsqrt(jnp.mean(x * x, axis=-1, keepdims=True) + NORM_EPS)
    return x * rstd * g


def _rms_bwd(dy, x, g):
    rstd = lax.rsqrt(jnp.mean(x * x, axis=-1, keepdims=True) + NORM_EPS)
    xn = x * rstd
    dxn = dy * g
    dx = rstd * (dxn - xn * jnp.mean(dxn * xn, axis=-1, keepdims=True))
    return dx, dy * xn


def _sigmoid(x):
    return 1.0 / (1.0 + jnp.exp(-x))


def _softplus(x):
    return jnp.maximum(x, 0.0) + jnp.log(1.0 + jnp.exp(-jnp.abs(x)))


def _acc(ref, val, first):
    @pl.when(first)
    def _():
        ref[...] = val

    @pl.when(jnp.logical_not(first))
    def _():
        ref[...] += val


def _colsum8(v):
    rows, n = v.shape
    return jnp.sum(v.reshape(rows // 8, 8, n), axis=0)


def _prep_fn(p, pprev, mu, w0, w2p, a0, a2p, g2, k_k, k_a):
    seg = _seg_ones(RW)
    ps = p + (pprev - p) * mu
    r = ps[:, 0:RW]
    k = ps[:, RW:2 * RW]
    v = ps[:, 2 * RW:3 * RW]
    xwa = ps[:, 3 * RW:3 * RW + 128]
    xg = ps[:, 3 * RW + 128:3 * RW + 256]
    wraw = -_softplus(-(w0 + _dot(jnp.tanh(xwa), w2p, HI))) - 0.5
    lw = -jnp.exp(wraw)
    a = _sigmoid(a0 + _dot(xwa, a2p, HI))
    g = _dot(_sigmoid(xg), g2, HI)
    kk = k * k_k
    kk = kk / jnp.maximum(jnp.sqrt(_segsum(kk * kk, seg)), 1e-12)
    k2 = k * (1.0 + (a - 1.0) * k_a)
    return r, lw, k2, v, kk, a, g


def _solve_unit_lower(lmat, rhs):
    c = lmat.shape[0]
    row = lax.broadcasted_iota(jnp.int32, (c, c), 0)
    col = lax.broadcasted_iota(jnp.int32, (c, c), 1)
    eye = (row == col).astype(F32)
    ld = jnp.where(row // SUB == col // SUB, lmat, 0.0)
    lo = lmat - ld
    x = eye + ld
    m = ld
    for _ in range(3):
        m = _dot(m, m, HI)
        x = x + _dot(x, m, HI)
    g = _dot(x, lo, HI)
    g2 = _dot(g, g, HI)
    w = _dot(x, rhs, HI)
    w = w + _dot(g2, w, HI)
    return w + _dot(g, w, HI)


def _wkv_chunk_fn(s0, r, lw, k, v, kk, a):
    c = r.shape[0]
    n = 2 * c
    row = lax.broadcasted_iota(jnp.int32, (n, n), 0)
    col = lax.broadcasted_iota(jnp.int32, (n, n), 1)
    same = (row // c) == (col // c)
    incl = jnp.logical_and(row >= col, same)
    strict = jnp.logical_and(row > col, same)
    sel = (lax.broadcasted_iota(jnp.int32, (n, 128), 0) // c) == (lax.broadcasted_iota(jnp.int32, (n, 128), 1) // HEAD_DIM)
    two = lambda z: jnp.concatenate([z, z], axis=0)
    lw2 = two(lw)
    cl = _dot(incl.astype(F32), lw2, HI)
    p = jnp.exp(cl)
    pinv = jnp.exp(-cl)
    pprev = jnp.exp(cl - lw2)
    kk2 = two(kk)
    at = jnp.where(sel, -kk2 * pprev, 0.0)
    bt = jnp.where(sel, kk2 * two(a) * pinv, 0.0)
    kt = jnp.where(sel, two(k) * pinv, 0.0)
    rt = jnp.where(sel, two(r) * p, 0.0)
    vt = jnp.where(sel, two(v), 0.0)
    ab = jnp.where(strict, _dot_nt(at, bt, HI), 0.0)
    ak = jnp.where(strict, _dot_nt(at, kt, HI), 0.0)
    rb = jnp.where(incl, _dot_nt(rt, bt, HI), 0.0)
    rk = jnp.where(incl, _dot_nt(rt, kt, HI), 0.0)
    u = _solve_unit_lower(ab, _dot_nt(at, s0, HI) + _dot(ak, vt, HI))
    y2 = _dot_nt(rt, s0, HI) + _dot(rb, u, HI) + _dot(rk, vt, HI)
    plast = jnp.exp(jnp.sum(lw, axis=0, keepdims=True))
    s1 = (s0 + _dot_tn(u, bt, HI) + _dot_tn(vt, kt, HI)) * plast
    r2 = lax.broadcasted_iota(jnp.int32, (128, 128), 0) // HEAD_DIM
    c2 = lax.broadcasted_iota(jnp.int32, (128, 128), 1) // HEAD_DIM
    return y2[:c] + y2[c:], jnp.where(r2 == c2, s1, 0.0)


def _post_fn(y, r, k2, v, g, lnw, lnb, rk):
    seg = _seg_ones(RW)
    mean = _segsum(y, seg) * (1.0 / HEAD_DIM)
    yc = y - mean
    var = _segsum(yc * yc, seg) * (1.0 / HEAD_DIM)
    yn = yc * lax.rsqrt(var + GN_EPS)
    out = yn * lnw + lnb + _segsum(r * k2 * rk, seg) * v
    return out * g


def _attn_block_fn(q, kc, vc, kp=None, vp=None):
    n = ATTN_BLOCK
    qi = lax.broadcasted_iota(jnp.int32, (n, n), 0)
    kj = lax.broadcasted_iota(jnp.int32, (n, n), 1)
    lane = lax.broadcasted_iota(jnp.int32, (1, 128), 1)
    scale = HEAD_DIM ** -0.5
    os_, ls_ = [], []
    for h in range(2):
        mh = (lane // HEAD_DIM) == h
        qh = jnp.where(mh, q, 0.0)
        sc = jnp.where(kj <= qi, _dot_nt(qh, kc, HI) * scale, NEG)
        m = jnp.max(sc, axis=-1, keepdims=True)
        if kp is not None:
            sp = jnp.where(kj >= qi, _dot_nt(qh, kp, HI) * scale, NEG)
            m = jnp.maximum(m, jnp.max(sp, axis=-1, keepdims=True))
        pc = jnp.exp(sc - m)
        den = jnp.sum(pc, axis=-1, keepdims=True)
        num = _dot(pc, vc, HI)
        if kp is not None:
            pp = jnp.exp(sp - m)
            den = den + jnp.sum(pp, axis=-1, keepdims=True)
            num = num + _dot(pp, vp, HI)
        os_.append(num / den)
        ls_.append(m + jnp.log(den))
    m0 = (lane // HEAD_DIM) == 0
    return jnp.where(m0, os_[0], os_[1]), jnp.where(m0, ls_[0], ls_[1])


def _combine_fn(o1, o2, o3, l1, l2, l3, og):
    seg = _seg_ones(RW)
    m = jnp.maximum(jnp.maximum(l1, l2), l3)
    e1, e2, e3 = jnp.exp(l1 - m), jnp.exp(l2 - m), jnp.exp(l3 - m)
    o = (e1 * o1 + e2 * o2 + e3 * o3) / (e1 + e2 + e3)
    o = o * lax.rsqrt(_segsum(o * o, seg) * (1.0 / HEAD_DIM) + NORM_EPS)
    return o * og


def _in_proj(x, g1, win):
    t = x.shape[0]
    tm = 256

    def body(x_ref, g_ref, w_ref, h_ref, p_ref):
        h = _rms_fwd(x_ref[...], g_ref[...]).astype(BF16)
        h_ref[...] = h
        p_ref[...] = _dot(h, w_ref[...])

    return pl.pallas_call(
        body, name="in_proj", grid=(t // tm,),
        in_specs=[pl.BlockSpec((tm, D_MODEL), lambda i: (i, 0)), pl.BlockSpec((1, D_MODEL), lambda i: (0, 0)),
                  pl.BlockSpec((D_MODEL, IN_COLS), lambda i: (0, 0))],
        out_specs=[pl.BlockSpec((tm, D_MODEL), lambda i: (i, 0)), pl.BlockSpec((tm, IN_COLS), lambda i: (i, 0))],
        out_shape=[jax.ShapeDtypeStruct((t, D_MODEL), BF16), jax.ShapeDtypeStruct((t, IN_COLS), F32)],
        compiler_params=_params(("parallel",)),
    )(x, g1, win)


def _shifted(p, last8, first):
    prow = jnp.where(first, 0.0, last8[7:8, :])
    rolled = pltpu.roll(p, 1, axis=0)
    rid = lax.broadcasted_iota(jnp.int32, p.shape, 0)
    return jnp.where(rid == 0, prow, rolled)


_PREP_TM = 256


def _prep_specs(tm):
    vec = lambda n: pl.BlockSpec((1, n), lambda i: (0, 0))
    mat = lambda r, n: pl.BlockSpec((r, n), lambda i: (0, 0))
    return [vec(SHIFT_COLS), vec(RW), mat(128, RW), vec(RW), mat(128, RW), mat(128, RW), vec(RW), vec(RW)]


def _prep_fwd(proj, pw):
    t = proj.shape[0]
    tm = _PREP_TM

    def body(p_ref, l8_ref, mu, w0, w2p, a0, a2p, g2, k_k, k_a, *outs):
        p = p_ref[...]
        pprev = _shifted(p, l8_ref[...], pl.program_id(0) == 0)
        res = _prep_fn(p, pprev, mu[...], w0[...], w2p[...], a0[...], a2p[...], g2[...], k_k[...], k_a[...])
        for o_ref, val in zip(outs, res):
            o_ref[...] = val

    row = pl.BlockSpec((tm, RW), lambda i: (i, 0))
    return pl.pallas_call(
        body, name="rwkv_prep", grid=(t // tm,),
        in_specs=[pl.BlockSpec((tm, SHIFT_COLS), lambda i: (i, 0)),
                  pl.BlockSpec((8, SHIFT_COLS), lambda i: (jnp.maximum(i * (tm // 8) - 1, 0), 0))] + _prep_specs(tm),
        out_specs=[row] * 7,
        out_shape=[jax.ShapeDtypeStruct((t, RW), F32)] * 7,
        compiler_params=_params(("parallel",)),
    )(proj, proj, *pw)


def _wkv_fwd(r, lw, k2, v, kk, a):
    t = r.shape[0]
    nc = t // CHUNK

    def body(r_ref, lw_ref, k_ref, v_ref, kk_ref, a_ref, y_ref, s_ref, st):
        @pl.when(pl.program_id(1) == 0)
        def _():
            st[...] = jnp.zeros_like(st)

        s0 = st[...]
        s_ref[0, 0] = s0
        y, s1 = _wkv_chunk_fn(s0, r_ref[...], lw_ref[...], k_ref[...], v_ref[...], kk_ref[...], a_ref[...])
        y_ref[...] = y
        st[...] = s1

    blk = pl.BlockSpec((CHUNK, 128), lambda p, c: (c, p))
    return pl.pallas_call(
        body, name="wkv_fwd", grid=(N_PAIR, nc),
        in_specs=[blk] * 6,
        out_specs=[blk, pl.BlockSpec((1, 1, 128, 128), lambda p, c: (c, p, 0, 0))],
        out_shape=[jax.ShapeDtypeStruct((t, RW), F32), jax.ShapeDtypeStruct((nc, N_PAIR, 128, 128), F32)],
        scratch_shapes=[pltpu.VMEM((128, 128), F32)],
        compiler_params=_params(("parallel", "arbitrary")),
    )(r, lw, k2, v, kk, a)


_POST_TM = 256


def _post_fwd(y, r, k2, v, g, lnw, lnb, rk):
    t = y.shape[0]
    tm = _POST_TM

    def body(y_ref, r_ref, k_ref, v_ref, g_ref, lnw_ref, lnb_ref, rk_ref, o_ref):
        o_ref[...] = _post_fn(y_ref[...], r_ref[...], k_ref[...], v_ref[...], g_ref[...],
                              lnw_ref[...], lnb_ref[...], rk_ref[...]).astype(BF16)

    row = pl.BlockSpec((tm, RW), lambda i: (i, 0))
    vec = pl.BlockSpec((1, RW), lambda i: (0, 0))
    return pl.pallas_call(
        body, name="rwkv_post", grid=(t // tm,),
        in_specs=[row] * 5 + [vec] * 3, out_specs=row,
        out_shape=jax.ShapeDtypeStruct((t, RW), BF16),
        compiler_params=_params(("parallel",)),
    )(y, r, k2, v, g, lnw, lnb, rk)


def _attn_fwd(qa, ka, va, qo, ko, vo, ncol, name):
    length = qa.shape[0]
    nb = length // ATTN_BLOCK

    def body(q_ref, k_ref, v_ref, o_ref, l_ref):
        first = pl.ds(0, ATTN_BLOCK)
        o_ref[first, :], l_ref[first, :] = _attn_block_fn(q_ref[first, :], k_ref[first, :], v_ref[first, :])

        def blk(n, carry):
            cur = pl.ds(pl.multiple_of(n * ATTN_BLOCK, ATTN_BLOCK), ATTN_BLOCK)
            prv = pl.ds(pl.multiple_of((n - 1) * ATTN_BLOCK, ATTN_BLOCK), ATTN_BLOCK)
            o, lse = _attn_block_fn(q_ref[cur, :], k_ref[cur, :], v_ref[cur, :], k_ref[prv, :], v_ref[prv, :])
            o_ref[cur, :] = o
            l_ref[cur, :] = lse
            return carry

        if nb > 1:
            lax.fori_loop(1, nb, blk, 0)

    spec = lambda off: pl.BlockSpec((length, 128), lambda c: (0, off + c))
    return pl.pallas_call(
        body, name=name, grid=(ncol,),
        in_specs=[spec(qo), spec(ko), spec(vo)], out_specs=[spec(0), spec(0)],
        out_shape=[jax.ShapeDtypeStruct((length, ncol * 128), F32)] * 2,
        compiler_params=_params(("parallel",)),
    )(qa, ka, va)


_COMB_TM = 256


def _combine_fwd(os_, ls_, og):
    t = os_[0].shape[0]
    tm = _COMB_TM

    def body(o1, o2, o3, l1, l2, l3, og_ref, y_ref):
        y_ref[...] = _combine_fn(o1[...], o2[...], o3[...], l1[...], l2[...], l3[...], og_ref[...]).astype(BF16)

    row = pl.BlockSpec((tm, RW), lambda i: (i, 0))
    return pl.pallas_call(
        body, name="attn_combine", grid=(t // tm,),
        in_specs=[row] * 6 + [pl.BlockSpec((1, RW), lambda i: (0, 0))], out_specs=row,
        out_shape=jax.ShapeDtypeStruct((t, RW), BF16),
        compiler_params=_params(("parallel",)),
    )(*os_, *ls_, og)


def _out_proj(x, ycat, wout, g2):
    t = x.shape[0]
    tm = 256

    def body(x_ref, y_ref, w_ref, g_ref, x1_ref, h_ref):
        x1 = x_ref[...] + _dot(y_ref[...], w_ref[...])
        x1_ref[...] = x1
        h_ref[...] = _rms_fwd(x1, g_ref[...]).astype(BF16)

    row = pl.BlockSpec((tm, D_MODEL), lambda i: (i, 0))
    return pl.pallas_call(
        body, name="out_proj", grid=(t // tm,),
        in_specs=[row, row, pl.BlockSpec((D_MODEL, D_MODEL), lambda i: (0, 0)), pl.BlockSpec((1, D_MODEL), lambda i: (0, 0))],
        out_specs=[row, row],
        out_shape=[jax.ShapeDtypeStruct((t, D_MODEL), F32), jax.ShapeDtypeStruct((t, D_MODEL), BF16)],
        compiler_params=_params(("parallel",)),
    )(x, ycat, wout, g2)


def _ffn_up(h2, wg, wu):
    t = h2.shape[0]
    tm = 256

    def body(h_ref, wg_ref, wu_ref, gt_ref, up_ref, act_ref):
        h = h_ref[...]
        gt = _dot(h, wg_ref[...])
        up = _dot(h, wu_ref[...])
        gt_ref[...] = gt
        up_ref[...] = up
        act_ref[...] = (gt * _sigmoid(gt) * up).astype(BF16)

    wide = pl.BlockSpec((tm, D_FF), lambda i: (i, 0))
    wsp = pl.BlockSpec((D_MODEL, D_FF), lambda i: (0, 0))
    return pl.pallas_call(
        body, name="ffn_up", grid=(t // tm,),
        in_specs=[pl.BlockSpec((tm, D_MODEL), lambda i: (i, 0)), wsp, wsp],
        out_specs=[wide, wide, wide],
        out_shape=[jax.ShapeDtypeStruct((t, D_FF), F32)] * 2 + [jax.ShapeDtypeStruct((t, D_FF), BF16)],
        compiler_params=_params(("parallel",)),
    )(h2, wg, wu)


def _ffn_down_loss(x1, act, wd, gf, tgt):
    t = x1.shape[0]
    tm = 256

    def body(x1_ref, a_ref, w_ref, g_ref, t_ref, dx_ref, dxb_ref, loss_ref, dg_ref):
        first = pl.program_id(0) == 0
        x2 = x1_ref[...] + _dot(a_ref[...], w_ref[...])
        g = g_ref[...]
        diff = _rms_fwd(x2, g) - t_ref[...]
        lrow = 0.5 * jnp.sum(_colsum8(diff * diff), axis=1, keepdims=True) * (1.0 / D_MODEL)
        _acc(loss_ref, jnp.broadcast_to(lrow, (8, 128)), first)
        dx2, dgr = _rms_bwd(diff * (1.0 / D_MODEL), x2, g)
        dx_ref[...] = dx2
        dxb_ref[...] = dx2.astype(BF16)
        _acc(dg_ref, _colsum8(dgr), first)

    row = pl.BlockSpec((tm, D_MODEL), lambda i: (i, 0))
    return pl.pallas_call(
        body, name="ffn_down_loss", grid=(t // tm,),
        in_specs=[row, pl.BlockSpec((tm, D_FF), lambda i: (i, 0)), pl.BlockSpec((D_FF, D_MODEL), lambda i: (0, 0)),
                  pl.BlockSpec((1, D_MODEL), lambda i: (0, 0)), row],
        out_specs=[row, row, pl.BlockSpec((8, 128), lambda i: (0, 0)), pl.BlockSpec((8, D_MODEL), lambda i: (0, 0))],
        out_shape=[jax.ShapeDtypeStruct((t, D_MODEL), F32), jax.ShapeDtypeStruct((t, D_MODEL), BF16),
                   jax.ShapeDtypeStruct((8, 128), F32), jax.ShapeDtypeStruct((8, D_MODEL), F32)],
        compiler_params=_params(("arbitrary",)),
    )(x1, act, wd, gf, tgt)


def _ffn_bwd_act(dx2b, wd, gt, up):
    t = dx2b.shape[0]
    tm = 256

    def body(dx_ref, w_ref, gt_ref, up_ref, dgt_ref, dup_ref):
        dact = _dot_nt(dx_ref[...], w_ref[...])
        gt = gt_ref[...]
        sg = _sigmoid(gt)
        dgt_ref[...] = (dact * up_ref[...] * sg * (1.0 + gt * (1.0 - sg))).astype(BF16)
        dup_ref[...] = (dact * gt * sg).astype(BF16)

    wide = pl.BlockSpec((tm, D_FF), lambda i: (i, 0))
    return pl.pallas_call(
        body, name="ffn_bwd_act", grid=(t // tm,),
        in_specs=[pl.BlockSpec((tm, D_MODEL), lambda i: (i, 0)), pl.BlockSpec((D_FF, D_MODEL), lambda i: (0, 0)), wide, wide],
        out_specs=[wide, wide],
        out_shape=[jax.ShapeDtypeStruct((t, D_FF), BF16)] * 2,
        compiler_params=_params(("parallel",)),
    )(dx2b, wd, gt, up)


def _ffn_bwd_h(dgt, dup, wg, wu, dx2, x1, g2, wout):
    t = dgt.shape[0]
    tm = 256

    def body(dgt_ref, dup_ref, wg_ref, wu_ref, dx2_ref, x1_ref, g_ref, wo_ref, dx1_ref, dx1b_ref, dya_ref, dyb_ref, dg_ref):
        dh = _dot_nt(dgt_ref[...], wg_ref[...]) + _dot_nt(dup_ref[...], wu_ref[...])
        dxn, dgr = _rms_bwd(dh, x1_ref[...], g_ref[...])
        dx1 = dx2_ref[...] + dxn
        dx1_ref[...] = dx1
        dx1b = dx1.astype(BF16)
        dx1b_ref[...] = dx1b
        dy = _dot_nt(dx1b, wo_ref[...])
        dya_ref[...] = dy[:, :RW]
        dyb_ref[...] = dy[:, RW:]
        _acc(dg_ref, _colsum8(dgr), pl.program_id(0) == 0)

    wide = pl.BlockSpec((tm, D_FF), lambda i: (i, 0))
    row = pl.BlockSpec((tm, D_MODEL), lambda i: (i, 0))
    half = pl.BlockSpec((tm, RW), lambda i: (i, 0))
    wsp = pl.BlockSpec((D_MODEL, D_FF), lambda i: (0, 0))
    return pl.pallas_call(
        body, name="ffn_bwd_h", grid=(t // tm,),
        in_specs=[wide, wide, wsp, wsp, row, row, pl.BlockSpec((1, D_MODEL), lambda i: (0, 0)),
                  pl.BlockSpec((D_MODEL, D_MODEL), lambda i: (0, 0))],
        out_specs=[row, row, half, half, pl.BlockSpec((8, D_MODEL), lambda i: (0, 0))],
        out_shape=[jax.ShapeDtypeStruct((t, D_MODEL), F32), jax.ShapeDtypeStruct((t, D_MODEL), BF16),
                   jax.ShapeDtypeStruct((t, RW), F32), jax.ShapeDtypeStruct((t, RW), F32),
                   jax.ShapeDtypeStruct((8, D_MODEL), F32)],
        compiler_params=_params(("arbitrary",)),
    )(dgt, dup, wg, wu, dx2, x1, g2, wout)


def _wgrad(a, b, tk, tn, name):
    t, kdim = a.shape
    ndim = b.shape[1]

    def body(a_ref, b_ref, o_ref):
        o_ref[...] = _dot_tn(a_ref[...], b_ref[...])

    return pl.pallas_call(
        body, name=name, grid=(kdim // tk, ndim // tn),
        in_specs=[pl.BlockSpec((t, tk), lambda i, j: (0, i)), pl.BlockSpec((t, tn), lambda i, j: (0, j))],
        out_specs=pl.BlockSpec((tk, tn), lambda i, j: (i, j)),
        out_shape=jax.ShapeDtypeStruct((kdim, ndim), F32),
        compiler_params=_params(("parallel", "parallel")),
    )(a, b)


def _post_bwd(dya, y, r, k2, v, g, lnw, lnb, rk):
    t = y.shape[0]
    tm = _POST_TM

    def body(d_ref, y_ref, r_ref, k_ref, v_ref, g_ref, lnw_ref, lnb_ref, rk_ref,
             dy_ref, dr_ref, dk_ref, dv_ref, dg_ref, dlnw_ref, dlnb_ref, drk_ref):
        first = pl.program_id(0) == 0
        ones = jnp.ones((tm, 1), F32)
        prim = (y_ref[...], r_ref[...], k_ref[...], v_ref[...], g_ref[...],
                ones * lnw_ref[...], ones * lnb_ref[...], ones * rk_ref[...])
        _, vjp = jax.vjp(_post_fn, *prim)
        dy, dr, dk, dv, dg, dlnw, dlnb, drk = vjp(d_ref[...])
        dy_ref[...] = dy
        dr_ref[...] = dr
        dk_ref[...] = dk
        dv_ref[...] = dv
        dg_ref[...] = dg
        _acc(dlnw_ref, _colsum8(dlnw), first)
        _acc(dlnb_ref, _colsum8(dlnb), first)
        _acc(drk_ref, _colsum8(drk), first)

    row = pl.BlockSpec((tm, RW), lambda i: (i, 0))
    vec = pl.BlockSpec((1, RW), lambda i: (0, 0))
    part = pl.BlockSpec((8, RW), lambda i: (0, 0))
    return pl.pallas_call(
        body, name="rwkv_post_bwd", grid=(t // tm,),
        in_specs=[row] * 6 + [vec] * 3, out_specs=[row] * 5 + [part] * 3,
        out_shape=[jax.ShapeDtypeStruct((t, RW), F32)] * 5 + [jax.ShapeDtypeStruct((8, RW), F32)] * 3,
        compiler_params=_params(("arbitrary",)),
    )(dya, y, r, k2, v, g, lnw, lnb, rk)


def _wkv_bwd(dy, s0s, r, lw, k2, v, kk, a):
    t = r.shape[0]
    nc = t // CHUNK

    def body(dy_ref, s_ref, r_ref, lw_ref, k_ref, v_ref, kk_ref, a_ref,
             dr_ref, dlw_ref, dk_ref, dv_ref, dkk_ref, da_ref, ds):
        @pl.when(pl.program_id(1) == 0)
        def _():
            ds[...] = jnp.zeros_like(ds)

        _, vjp = jax.vjp(_wkv_chunk_fn, s_ref[0, 0], r_ref[...], lw_ref[...], k_ref[...], v_ref[...],
                         kk_ref[...], a_ref[...])
        ds0, dr, dlw, dk, dv, dkk, da = vjp((dy_ref[...], ds[...]))
        ds[...] = ds0
        dr_ref[...] = dr
        dlw_ref[...] = dlw
        dk_ref[...] = dk
        dv_ref[...] = dv
        dkk_ref[...] = dkk
        da_ref[...] = da

    blk = pl.BlockSpec((CHUNK, 128), lambda p, c: (nc - 1 - c, p))
    return pl.pallas_call(
        body, name="wkv_bwd", grid=(N_PAIR, nc),
        in_specs=[blk, pl.BlockSpec((1, 1, 128, 128), lambda p, c: (nc - 1 - c, p, 0, 0))] + [blk] * 6,
        out_specs=[blk] * 6,
        out_shape=[jax.ShapeDtypeStruct((t, RW), F32)] * 6,
        scratch_shapes=[pltpu.VMEM((128, 128), F32)],
        compiler_params=_params(("parallel", "arbitrary")),
    )(dy, s0s, r, lw, k2, v, kk, a)


def _prep_bwd(proj, pw, douts):
    t = proj.shape[0]
    tm = _PREP_TM
    nt = t // tm

    def body(p_ref, l8_ref, mu, w0, w2p, a0, a2p, g2, k_k, k_a, dr, dr2, dlw, dk2, dk22, dv, dv2, dkk, da, dg,
             dp_ref, dmu_ref, dw0_ref, dw2_ref, da0_ref, da2_ref, dg2_ref, dkk_ref, dka_ref, carry):
        i = pl.program_id(0)
        first = i == 0

        @pl.when(first)
        def _():
            carry[...] = jnp.zeros_like(carry)

        p = p_ref[...]
        pprev = _shifted(p, l8_ref[...], i == nt - 1)
        ones = jnp.ones((tm, 1), F32)
        prim = (p, pprev, ones * mu[...], ones * w0[...], w2p[...], ones * a0[...], a2p[...], g2[...],
                ones * k_k[...], ones * k_a[...])
        _, vjp = jax.vjp(_prep_fn, *prim)
        dp, dpp, dmu, dw0, dw2, da0, da2, dg2, dkk_, dka = vjp(
            (dr[...] + dr2[...], dlw[...], dk2[...] + dk22[...], dv[...] + dv2[...], dkk[...], da[...], dg[...]))
        up = pltpu.roll(dpp, tm - 1, axis=0)
        rid = lax.broadcasted_iota(jnp.int32, dpp.shape, 0)
        dp_ref[...] = dp + jnp.where(rid == tm - 1, carry[0:1, :], up)
        carry[...] = jnp.broadcast_to(dpp[0:1, :], carry.shape)
        _acc(dmu_ref, _colsum8(dmu), first)
        _acc(dw0_ref, _colsum8(dw0), first)
        _acc(dw2_ref, dw2, first)
        _acc(da0_ref, _colsum8(da0), first)
        _acc(da2_ref, da2, first)
        _acc(dg2_ref, dg2, first)
        _acc(dkk_ref, _colsum8(dkk_), first)
        _acc(dka_ref, _colsum8(dka), first)

    rev = lambda i: (nt - 1 - i, 0)
    row = pl.BlockSpec((tm, RW), rev)
    part = lambda n: pl.BlockSpec((8, n), lambda i: (0, 0))
    mat = pl.BlockSpec((128, RW), lambda i: (0, 0))
    return pl.pallas_call(
        body, name="rwkv_prep_bwd", grid=(nt,),
        in_specs=[pl.BlockSpec((tm, SHIFT_COLS), rev),
                  pl.BlockSpec((8, SHIFT_COLS), lambda i: (jnp.maximum((nt - 1 - i) * (tm // 8) - 1, 0), 0))]
                 + _prep_specs(tm) + [row] * 10,
        out_specs=[pl.BlockSpec((tm, SHIFT_COLS), rev), part(SHIFT_COLS), part(RW), mat, part(RW), mat, mat,
                   part(RW), part(RW)],
        out_shape=[jax.ShapeDtypeStruct((t, SHIFT_COLS), F32), jax.ShapeDtypeStruct((8, SHIFT_COLS), F32),
                   jax.ShapeDtypeStruct((8, RW), F32), jax.ShapeDtypeStruct((128, RW), F32),
                   jax.ShapeDtypeStruct((8, RW), F32), jax.ShapeDtypeStruct((128, RW), F32),
                   jax.ShapeDtypeStruct((128, RW), F32), jax.ShapeDtypeStruct((8, RW), F32),
                   jax.ShapeDtypeStruct((8, RW), F32)],
        scratch_shapes=[pltpu.VMEM((8, SHIFT_COLS), F32)],
        compiler_params=_params(("arbitrary",)),
    )(proj, proj, *pw, *douts)


def _combine_bwd(dyb, os_, ls_, og):
    t = dyb.shape[0]
    tm = _COMB_TM

    def body(d_ref, o1, o2, o3, l1, l2, l3, og_ref, do1, do2, do3, dl1, dl2, dl3, dog_ref):
        ones = jnp.ones((tm, 1), F32)
        _, vjp = jax.vjp(_combine_fn, o1[...], o2[...], o3[...], l1[...], l2[...], l3[...], ones * og_ref[...])
        res = vjp(d_ref[...])
        for ref, val in zip((do1, do2, do3, dl1, dl2, dl3), res[:6]):
            ref[...] = val
        _acc(dog_ref, _colsum8(res[6]), pl.program_id(0) == 0)

    row = pl.BlockSpec((tm, RW), lambda i: (i, 0))
    return pl.pallas_call(
        body, name="attn_combine_bwd", grid=(t // tm,),
        in_specs=[row] * 7 + [pl.BlockSpec((1, RW), lambda i: (0, 0))],
        out_specs=[row] * 6 + [pl.BlockSpec((8, RW), lambda i: (0, 0))],
        out_shape=[jax.ShapeDtypeStruct((t, RW), F32)] * 6 + [jax.ShapeDtypeStruct((8, RW), F32)],
        compiler_params=_params(("arbitrary",)),
    )(dyb, *os_, *ls_, og)


def _attn_bwd(do, dl, qa, ka, va, qo, ko, vo, ncol, name):
    length = qa.shape[0]
    nb = length // ATTN_BLOCK

    def body(do_ref, dl_ref, q_ref, k_ref, v_ref, dq_ref, dk_ref, dv_ref):
        dk_ref[...] = jnp.zeros_like(dk_ref)
        dv_ref[...] = jnp.zeros_like(dv_ref)

        first = pl.ds(0, ATTN_BLOCK)
        _, vjp0 = jax.vjp(_attn_block_fn, q_ref[first, :], k_ref[first, :], v_ref[first, :])
        dq0, dk0, dv0 = vjp0((do_ref[first, :], dl_ref[first, :]))
        dq_ref[first, :] = dq0
        dk_ref[first, :] += dk0
        dv_ref[first, :] += dv0

        def blk(n, carry):
            cur = pl.ds(pl.multiple_of(n * ATTN_BLOCK, ATTN_BLOCK), ATTN_BLOCK)
            prv = pl.ds(pl.multiple_of((n - 1) * ATTN_BLOCK, ATTN_BLOCK), ATTN_BLOCK)
            _, vjp = jax.vjp(_attn_block_fn, q_ref[cur, :], k_ref[cur, :], v_ref[cur, :], k_ref[prv, :], v_ref[prv, :])
            dq, dkc, dvc, dkp, dvp = vjp((do_ref[cur, :], dl_ref[cur, :]))
            dq_ref[cur, :] = dq
            dk_ref[cur, :] += dkc
            dv_ref[cur, :] += dvc
            dk_ref[prv, :] += dkp
            dv_ref[prv, :] += dvp
            return carry

        if nb > 1:
            lax.fori_loop(1, nb, blk, 0)

    spec = lambda off: pl.BlockSpec((length, 128), lambda c: (0, off + c))
    return pl.pallas_call(
        body, name=name, grid=(ncol,),
        in_specs=[spec(0), spec(0), spec(qo), spec(ko), spec(vo)], out_specs=[spec(0)] * 3,
        out_shape=[jax.ShapeDtypeStruct((length, ncol * 128), F32)] * 3,
        compiler_params=_params(("parallel",)),
    )(do, dl, qa, ka, va)


def _in_proj_bwd(dpa, dqs, dks, dvs, win, x, g1, dx1):
    t = x.shape[0]
    tm = 256

    def body(dpa_ref, q1, q2, q3, k1, k2, k3, v1, v2, v3, w_ref, x_ref, g_ref, dx1_ref, dproj_ref, dx_ref, dg_ref):
        parts = (dpa_ref[...], q1[...] + q2[...] + q3[...], k1[...] + k2[...] + k3[...], v1[...] + v2[...] + v3[...])
        dproj = jnp.concatenate([z.astype(BF16) for z in parts], axis=1)
        dproj_ref[...] = dproj
        dh = _dot_nt(dproj, w_ref[...])
        dxn, dgr = _rms_bwd(dh, x_ref[...], g_ref[...])
        dx_ref[...] = dx1_ref[...] + dxn
        _acc(dg_ref, _colsum8(dgr), pl.program_id(0) == 0)

    row = pl.BlockSpec((tm, D_MODEL), lambda i: (i, 0))
    half = pl.BlockSpec((tm, RW), lambda i: (i, 0))
    return pl.pallas_call(
        body, name="in_proj_bwd", grid=(t // tm,),
        in_specs=[pl.BlockSpec((tm, SHIFT_COLS), lambda i: (i, 0))] + [half] * 9
                 + [pl.BlockSpec((D_MODEL, IN_COLS), lambda i: (0, 0)), row, pl.BlockSpec((1, D_MODEL), lambda i: (0, 0)), row],
        out_specs=[pl.BlockSpec((tm, IN_COLS), lambda i: (i, 0)), row, pl.BlockSpec((8, D_MODEL), lambda i: (0, 0))],
        out_shape=[jax.ShapeDtypeStruct((t, IN_COLS), BF16), jax.ShapeDtypeStruct((t, D_MODEL), F32),
                   jax.ShapeDtypeStruct((8, D_MODEL), F32)],
        compiler_params=_params(("arbitrary",)),
    )(dpa, *dqs, *dks, *dvs, win, x, g1, dx1)


def _pad_lora(w, lo):
    z = jnp.zeros((64, RW), F32)
    return jnp.concatenate([w, z], axis=0) if lo == 0 else jnp.concatenate([z, w], axis=0)


def _dilate(z, d):
    return z if d == 1 else z.reshape(z.shape[0] // d, z.shape[1] * d)


def _local_step(x, tgt, win, wout, wg, wu, wd, vecs, w2, a2, g2m):
    t = x.shape[0]
    pw = (vecs["mu_shift"], vecs["decay_w0"], _pad_lora(w2, 0), vecs["iclr_a0"], _pad_lora(a2, 64), g2m,
          vecs["k_k"], vecs["k_a"])
    h, proj = _in_proj(x, vecs["mix_norm_g"], win)
    r, lw, k2, v, kk, a, g = _prep_fwd(proj, pw)
    y, s0s = _wkv_fwd(r, lw, k2, v, kk, a)
    ya = _post_fwd(y, r, k2, v, g, vecs["ln_x_w"], vecs["ln_x_b"], vecs["r_k"])

    qoff = SHIFT_COLS // 128
    qkv = {}
    os_, ls_ = [], []
    for d in DILATIONS:
        if d == 1:
            arrs, offs, ncol = (proj, proj, proj), (qoff, qoff + N_PAIR, qoff + 2 * N_PAIR), N_PAIR
        else:
            arrs = tuple(_dilate(proj[:, SHIFT_COLS + j * RW:SHIFT_COLS + (j + 1) * RW], d) for j in range(3))
            offs, ncol = (0, 0, 0), N_PAIR * d
        qkv[d] = (arrs, offs, ncol)
        o, l = _attn_fwd(*arrs, *offs, ncol, "attn_fwd_d%d" % d)
        os_.append(o.reshape(t, RW))
        ls_.append(l.reshape(t, RW))
    yb = _combine_fwd(os_, ls_, vecs["attn_out_g"])

    ycat = jnp.concatenate([ya, yb], axis=1)
    x1, h2 = _out_proj(x, ycat, wout, vecs["ffn_norm_g"])
    gt, up, act = _ffn_up(h2, wg, wu)
    dx2, dx2b, loss8, dgf = _ffn_down_loss(x1, act, wd, vecs["final_norm_g"], tgt)

    dgt, dup = _ffn_bwd_act(dx2b, wd, gt, up)
    dx1, dx1b, dya, dyb, dg2n = _ffn_bwd_h(dgt, dup, wg, wu, dx2, x1, vecs["ffn_norm_g"], wout)
    gw = {
        "w_down": _wgrad(act, dx2b, 1408, 1024, "wgrad_down"),
        "w_gate": _wgrad(h2, dgt, 1024, 1408, "wgrad_gate"),
        "w_up": _wgrad(h2, dup, 1024, 1408, "wgrad_up"),
        "w_out": _wgrad(ycat, dx1b, 1024, 1024, "wgrad_out"),
    }

    dy, dr_p, dk2_p, dv_p, dg, dlnw, dlnb, drk = _post_bwd(dya, y, r, k2, v, g, vecs["ln_x_w"], vecs["ln_x_b"], vecs["r_k"])
    dr_s, dlw, dk2_s, dv_s, dkk, da = _wkv_bwd(dy, s0s, r, lw, k2, v, kk, a)
    dpa, dmu, dw0, dw2p, da0, da2p, dg2m, dk_k, dk_a = _prep_bwd(
        proj, pw, (dr_p, dr_s, dlw, dk2_p, dk2_s, dv_p, dv_s, dkk, da, dg))

    cb = _combine_bwd(dyb, os_, ls_, vecs["attn_out_g"])
    dqs, dks, dvs = [], [], []
    for j, d in enumerate(DILATIONS):
        arrs, offs, ncol = qkv[d]
        dq, dk, dv = _attn_bwd(_dilate(cb[j], d), _dilate(cb[3 + j], d), *arrs, *offs, ncol, "attn_bwd_d%d" % d)
        dqs.append(dq.reshape(t, RW))
        dks.append(dk.reshape(t, RW))
        dvs.append(dv.reshape(t, RW))
    dproj, dx, dg1 = _in_proj_bwd(dpa, dqs, dks, dvs, win, x, vecs["mix_norm_g"], dx1)
    gw["w_in"] = _wgrad(h, dproj, 1024, 1664, "wgrad_in")
    gw["decay_w2"] = dw2p[:64]
    gw["iclr_a2"] = da2p[64:]
    gw["gate_g2"] = dg2m
    gv = {"mix_norm_g": dg1, "mu_shift": dmu, "decay_w0": dw0, "iclr_a0": da0, "k_k": dk_k, "k_a": dk_a, "r_k": drk,
          "ln_x_w": dlnw, "ln_x_b": dlnb, "attn_out_g": cb[6], "ffn_norm_g": dg2n, "final_norm_g": dgf}
    return loss8, dx, gw, gv


N_CHIP = 4
N_DEV = 8
MATS = ("w_in", "w_out", "w_gate", "w_up", "w_down")
LORAS = ("decay_w2", "iclr_a2", "gate_g2")
VECS = (("mix_norm_g", 1024), ("mu_shift", 1792), ("decay_w0", 512), ("iclr_a0", 512), ("k_k", 512), ("k_a", 512),
        ("r_k", 512), ("ln_x_w", 512), ("ln_x_b", 512), ("attn_out_g", 512), ("ffn_norm_g", 1024),
        ("final_norm_g", 1024))
N_VEC = sum(n for _, n in VECS)
N_SMALL = N_VEC + 128
ROWS_PAD = 3328
ANY = pl.BlockSpec(memory_space=pl.ANY)


def _flip(v, f):
    return 1 - v if f else v


def _gather_weights(wpack, lpack):
    def body(w_ref, l_ref, wo_ref, lo_ref, send_sems, recv_sems, local_sems):
        x, y, c = lax.axis_index("x"), lax.axis_index("y"), lax.axis_index("c")
        me = 2 * x + y
        pairs = ((w_ref, wo_ref), (l_ref, lo_ref))
        local = [pltpu.make_async_copy(src, dst.at[me], local_sems.at[a]) for a, (src, dst) in enumerate(pairs)]
        for cp in local:
            cp.start()
        chips = [(1 - x, y), (x, 1 - y), (1 - x, 1 - y)]

        def copy(j, a, slot):
            src, dst = pairs[a]
            return pltpu.make_async_remote_copy(src_ref=src, dst_ref=dst.at[slot], send_sem=send_sems.at[2 * j + a],
                                                recv_sem=recv_sems.at[2 * j + a], device_id=(*chips[j], c),
                                                device_id_type=MESH)

        sends = [copy(j, a, me) for j in range(3) for a in range(2)]
        for cp in sends:
            cp.start()
        for j, (px, py) in enumerate(chips):
            for a in range(2):
                copy(j, a, 2 * px + py).wait_recv()
        for cp in sends:
            cp.wait_send()
        for cp in local:
            cp.wait()

    return pl.pallas_call(
        body, name="gather_weights", in_specs=[ANY, ANY], out_specs=[ANY, ANY],
        out_shape=[jax.ShapeDtypeStruct((N_CHIP,) + wpack.shape, wpack.dtype),
                   jax.ShapeDtypeStruct((N_CHIP,) + lpack.shape, lpack.dtype)],
        scratch_shapes=[pltpu.SemaphoreType.DMA((6,)), pltpu.SemaphoreType.DMA((6,)), pltpu.SemaphoreType.DMA((2,))],
    )(wpack, lpack)


def _exchange_grads(gpack, small):
    def body(g_ref, s_ref, go_ref, so_ref, send_sems, recv_sems, local_sems):
        x, y, c = lax.axis_index("x"), lax.axis_index("y"), lax.axis_index("c")
        me = 4 * x + 2 * y + c
        local = [pltpu.make_async_copy(g_ref.at[2 * x + y], go_ref.at[me], local_sems.at[0]),
                 pltpu.make_async_copy(s_ref, so_ref.at[me], local_sems.at[1])]
        for cp in local:
            cp.start()
        peers = []
        for k in range(1, N_DEV):
            peers.append((_flip(x, k & 4), _flip(y, k & 2), _flip(c, k & 1)))

        def copies(k, slot):
            px, py, pc = peers[k]
            kw = dict(device_id=(px, py, pc), device_id_type=MESH)
            return [pltpu.make_async_remote_copy(src_ref=g_ref.at[2 * px + py], dst_ref=go_ref.at[slot],
                                                 send_sem=send_sems.at[2 * k], recv_sem=recv_sems.at[2 * k], **kw),
                    pltpu.make_async_remote_copy(src_ref=s_ref, dst_ref=so_ref.at[slot],
                                                 send_sem=send_sems.at[2 * k + 1], recv_sem=recv_sems.at[2 * k + 1], **kw)]

        sends = [cp for k in range(N_DEV - 1) for cp in copies(k, me)]
        for cp in sends:
            cp.start()
        for k, (px, py, pc) in enumerate(peers):
            for cp in copies(k, 4 * px + 2 * py + pc):
                cp.wait_recv()
        for cp in sends:
            cp.wait_send()
        for cp in local:
            cp.wait()

    ns = 2 * (N_DEV - 1)
    return pl.pallas_call(
        body, name="exchange_grads", in_specs=[ANY, ANY], out_specs=[ANY, ANY],
        out_shape=[jax.ShapeDtypeStruct((N_DEV,) + gpack.shape[1:], gpack.dtype),
                   jax.ShapeDtypeStruct((N_DEV,) + small.shape, small.dtype)],
        scratch_shapes=[pltpu.SemaphoreType.DMA((ns,)), pltpu.SemaphoreType.DMA((ns,)), pltpu.SemaphoreType.DMA((2,))],
    )(gpack, small)


def _adamw(w, g, m, v):
    m = ADAM_B1 * m + (1.0 - ADAM_B1) * g
    v = ADAM_B2 * v + (1.0 - ADAM_B2) * (g * g)
    m_hat = m / (1.0 - ADAM_B1 ** ADAM_STEP)
    v_hat = v / (1.0 - ADAM_B2 ** ADAM_STEP)
    delta = -ADAM_LR * (m_hat / (jnp.sqrt(v_hat) + ADAM_EPS) + ADAM_WD * w)
    return delta, m, v


def _reduce_adamw(rbuf, w, m, v):
    rows = w.shape[0]
    tr = 256

    def body(r_ref, w_ref, m_ref, v_ref, g_ref, d_ref, nm_ref, nv_ref):
        g = r_ref[0].astype(F32)
        for s in range(1, N_DEV):
            g = g + r_ref[s].astype(F32)
        g_ref[...] = g
        d_ref[...], nm_ref[...], nv_ref[...] = _adamw(w_ref[...], g, m_ref[...], v_ref[...])

    row = pl.BlockSpec((tr, 1024), lambda i: (i, 0))
    return pl.pallas_call(
        body, name="reduce_adamw", grid=(rows // tr,),
        in_specs=[pl.BlockSpec((N_DEV, tr, 1024), lambda i: (0, i, 0)), row, row, row], out_specs=[row] * 4,
        out_shape=[jax.ShapeDtypeStruct((rows, 1024), F32)] * 4,
        compiler_params=_params(("parallel",)),
    )(rbuf, w, m, v)


def _reduce_adamw_small(sbuf, w, m, v):
    def body(s_ref, w_ref, m_ref, v_ref, g_ref, d_ref, nm_ref, nv_ref, loss_ref):
        tot = s_ref[0]
        for s in range(1, N_DEV):
            tot = tot + s_ref[s]
        tot = jnp.sum(tot, axis=0, keepdims=True)
        g = tot[:, :N_VEC]
        g_ref[...] = g
        d_ref[...], nm_ref[...], nv_ref[...] = _adamw(w_ref[...], g, m_ref[...], v_ref[...])
        loss_ref[...] = tot[:, N_VEC:]

    return pl.pallas_call(
        body, name="reduce_adamw_small",
        out_shape=[jax.ShapeDtypeStruct((1, N_VEC), F32)] * 4 + [jax.ShapeDtypeStruct((1, 128), F32)],
    )(sbuf, w, m, v)


def _rows(z):
    return z.reshape(-1, 1024)


def _pack_shard(d):
    parts = [_rows(d[n]) for n in MATS + LORAS]
    used = sum(p.shape[0] for p in parts)
    return jnp.concatenate(parts + [jnp.zeros((ROWS_PAD - used, 1024), parts[0].dtype)], axis=0)


_SHARD_SHAPES = {"w_in": (1024, 832), "w_out": (256, 1024), "w_gate": (1024, 704), "w_up": (1024, 704),
                 "w_down": (704, 1024), "decay_w2": (64, 128), "iclr_a2": (64, 128), "gate_g2": (128, 128)}
_ROW_SHARDED = ("w_out", "w_down")


def _unpack_shard(pack, names):
    out, r0 = {}, 0
    for n in names:
        shp = _SHARD_SHAPES[n]
        nr = shp[0] * shp[1] // 1024
        out[n] = pack[r0:r0 + nr].reshape(shp)
        r0 += nr
    return out


def _full_from_gathered(gathered, names):
    per_chip = [_unpack_shard(gathered[p], names) for p in range(N_CHIP)]
    return {n: jnp.concatenate([pc[n] for pc in per_chip], axis=0 if n in _ROW_SHARDED else 1) for n in names}


def _shard_of_full(full, n, p):
    shp = _SHARD_SHAPES[n]
    if n in _ROW_SHARDED:
        return full[p * shp[0]:(p + 1) * shp[0]]
    return full[:, p * shp[1]:(p + 1) * shp[1]]


def kernel(x, mix_norm_g, w_in, mu_shift, decay_w0, decay_w2, iclr_a0, iclr_a2, gate_g2, k_k, k_a, r_k, ln_x_w, ln_x_b, attn_out_g, w_out, ffn_norm_g, w_gate, w_up, w_down, final_norm_g, loss_target, m_mix_norm_g, m_w_in, m_mu_shift, m_decay_w0, m_decay_w2, m_iclr_a0, m_iclr_a2, m_gate_g2, m_k_k, m_k_a, m_r_k, m_ln_x_w, m_ln_x_b, m_attn_out_g, m_w_out, m_ffn_norm_g, m_w_gate, m_w_up, m_w_down, m_final_norm_g, v_mix_norm_g, v_w_in, v_mu_shift, v_decay_w0, v_decay_w2, v_iclr_a0, v_iclr_a2, v_gate_g2, v_k_k, v_k_a, v_r_k, v_ln_x_w, v_ln_x_b, v_attn_out_g, v_w_out, v_ffn_norm_g, v_w_gate, v_w_up, v_w_down, v_final_norm_g):
    names = ("mix_norm_g", "w_in", "mu_shift", "decay_w0", "decay_w2", "iclr_a0", "iclr_a2", "gate_g2", "k_k", "k_a",
             "r_k", "ln_x_w", "ln_x_b", "attn_out_g", "w_out", "ffn_norm_g", "w_gate", "w_up", "w_down", "final_norm_g")
    w = dict(zip(names, (mix_norm_g, w_in, mu_shift, decay_w0, decay_w2, iclr_a0, iclr_a2, gate_g2, k_k, k_a, r_k,
                         ln_x_w, ln_x_b, attn_out_g, w_out, ffn_norm_g, w_gate, w_up, w_down, final_norm_g)))
    m = dict(zip(names, (m_mix_norm_g, m_w_in, m_mu_shift, m_decay_w0, m_decay_w2, m_iclr_a0, m_iclr_a2, m_gate_g2,
                         m_k_k, m_k_a, m_r_k, m_ln_x_w, m_ln_x_b, m_attn_out_g, m_w_out, m_ffn_norm_g, m_w_gate,
                         m_w_up, m_w_down, m_final_norm_g)))
    v = dict(zip(names, (v_mix_norm_g, v_w_in, v_mu_shift, v_decay_w0, v_decay_w2, v_iclr_a0, v_iclr_a2, v_gate_g2,
                         v_k_k, v_k_a, v_r_k, v_ln_x_w, v_ln_x_b, v_attn_out_g, v_w_out, v_ffn_norm_g, v_w_gate,
                         v_w_up, v_w_down, v_final_norm_g)))
    big = MATS + LORAS

    wpack = jnp.concatenate([_rows(w[n][0]) for n in MATS], axis=0).astype(BF16)
    lpack = jnp.concatenate([_rows(w[n][0]) for n in LORAS], axis=0)
    wg_all, lg_all = _gather_weights(wpack, lpack)
    fullw = _full_from_gathered(wg_all, MATS)
    fulll = _full_from_gathered(lg_all, LORAS)

    vecs = {n: w[n].reshape(1, sz) for n, sz in VECS}
    loss8, dx, gw, gv = _local_step(x[0], loss_target[0], fullw["w_in"], fullw["w_out"], fullw["w_gate"], fullw["w_up"],
                                    fullw["w_down"], vecs, fulll["decay_w2"], fulll["iclr_a2"], fulll["gate_g2"])

    gpack = jnp.stack([_pack_shard({n: _shard_of_full(gw[n], n, p) for n in big}) for p in range(N_CHIP)]).astype(BF16)
    small = jnp.concatenate([gv[n] for n, _ in VECS] + [loss8], axis=1)
    rbuf, sbuf = _exchange_grads(gpack, small)

    g_p, d_p, m_p, v_p = _reduce_adamw(rbuf, _pack_shard({n: w[n][0] for n in big}), _pack_shard({n: m[n][0] for n in big}),
                                       _pack_shard({n: v[n][0] for n in big}))
    cat = lambda d: jnp.concatenate([d[n].reshape(1, sz) for n, sz in VECS], axis=1)
    g_s, d_s, m_s, v_s, loss = _reduce_adamw_small(sbuf, cat(w), cat(m), cat(v))

    outs = []
    for pack, small_pack in ((g_p, g_s), (d_p, d_s), (m_p, m_s), (v_p, v_s)):
        got = {n: z.reshape(w[n].shape) for n, z in _unpack_shard(pack, big).items()}
        c0 = 0
        for n, sz in VECS:
            got[n] = small_pack[0, c0:c0 + sz].reshape(w[n].shape)
            c0 += sz
        outs.extend(got[n] for n in names)
    return (loss[0, 0], dx[None], *outs)
```

```python
import jax
import jax.numpy as jnp
from jax import lax
from jax.experimental import pallas as pl
from jax.experimental.pallas import tpu as pltpu

F32 = jnp.float32
BF16 = jnp.bfloat16
HI = lax.Precision.HIGHEST

D_MODEL = 1024
HEAD_DIM = 64
RW = 512
N_PAIR = RW // 128
SHIFT_COLS = 1792
IN_COLS = 3328
D_FF = 2816
NORM_EPS = 1e-6
GN_EPS = 64e-5
CHUNK = 64
SUB = 16
WKV_PASSES = 3
ATTN_PASSES = 1
ATTN_BLOCK = 128
DILATIONS = (1, 4, 16)
NEG = -1e30
ADAM_LR, ADAM_B1, ADAM_B2, ADAM_EPS, ADAM_WD, ADAM_STEP = 0.001, 0.9, 0.999, 1e-08, 0.01, 10
VMEM_LIMIT = 56 * 1024 * 1024
MESH = pl.DeviceIdType.MESH


def _params(sem=None, **kw):
    return pltpu.CompilerParams(dimension_semantics=sem, vmem_limit_bytes=VMEM_LIMIT, **kw)


def _dot(a, b, prec=None):
    return lax.dot_general(a, b, (((1,), (0,)), ((), ())), preferred_element_type=F32, precision=prec)


def _dot_nt(a, b, prec=None):
    return lax.dot_general(a, b, (((1,), (1,)), ((), ())), preferred_element_type=F32, precision=prec)


def _dot_tn(a, b, prec=None):
    return lax.dot_general(a, b, (((0,), (0,)), ((), ())), preferred_element_type=F32, precision=prec)


_FORMS = {"nn": ((1,), (0,)), "nt": ((1,), (1,)), "tn": ((0,), (0,))}


def _dg(a, b, form):
    if a.ndim == 3 or b.ndim == 3:
        nb = a.shape[0] if a.ndim == 3 else b.shape[0]
        return jnp.stack([_dg(a[i] if a.ndim == 3 else a, b[i] if b.ndim == 3 else b, form) for i in range(nb)], axis=0)
    return lax.dot_general(a, b, (_FORMS[form], ((), ())), preferred_element_type=F32)


def _split2(x):
    hi = x.astype(BF16)
    return hi, (x - hi.astype(F32)).astype(BF16)


def _split3(x):
    hi = x.astype(BF16)
    rest = x - hi.astype(F32)
    mid = rest.astype(BF16)
    return hi, mid, (rest - mid.astype(F32)).astype(BF16)


def _mm_raw(a, b, form, mode):
    if mode == 1:
        return _dg(a.astype(BF16), b.astype(BF16), form)
    if mode == 3:
        ah, al = _split2(a)
        bh, bl = _split2(b)
        return _dg(ah, bh, form) + (_dg(ah, bl, form) + _dg(al, bh, form))
    if mode == "L3":
        ab = a.astype(BF16)
        b1, b2, b3 = _split3(b)
        return _dg(ab, b1, form) + (_dg(ab, b2, form) + _dg(ab, b3, form))
    assert mode == "R3", mode
    bb = b.astype(BF16)
    a1, a2, a3 = _split3(a)
    return _dg(a1, bb, form) + (_dg(a2, bb, form) + _dg(a3, bb, form))


def _mm(a, b, form, mode):
    @jax.custom_vjp
    def f(a, b):
        return _mm_raw(a, b, form, mode)

    def fwd(a, b):
        return _mm_raw(a, b, form, mode), (a, b)

    def bwd(res, ct):
        a, b = res
        la = {1: 1, 3: 3, "L3": None, "R3": "R3"}[mode]
        lb = {1: 1, 3: 3, "L3": "L3", "R3": None}[mode]
        if form == "nn":
            da = None if la is None else _mm_raw(ct, b, "nt", la)
            db = None if lb is None else _mm_raw(a, ct, "tn", lb)
        elif form == "nt":
            da = None if la is None else _mm_raw(ct, b, "nn", la)
            db = None if lb is None else _mm_raw(ct, a, "tn", "R3" if lb == "L3" else lb)
        else:
            da = None if la is None else _mm_raw(b, ct, "nt", "L3" if la == "R3" else la)
            db = None if lb is None else _mm_raw(a, ct, "nn", lb)
        return (jnp.zeros_like(a) if da is None else da, jnp.zeros_like(b) if db is None else db)

    f.defvjp(fwd, bwd)
    return f(a, b)


def _seg_ones(n):
    r = lax.broadcasted_iota(jnp.int32, (n, n), 0) // HEAD_DIM
    c = lax.broadcasted_iota(jnp.int32, (n, n), 1) // HEAD_DIM
    return (r == c).astype(F32)


def _segsum(x, seg):
    return _mm(x, seg, "nn", "R3")


def _rms_fwd(x, g):
    rstd = lax.rsqrt(jnp.mean(x * x, axis=-1, keepdims=True) + NORM_EPS)
    return x * rstd * g


def _rms_bwd(dy, x, g):
    rstd = lax.rsqrt(jnp.mean(x * x, axis=-1, keepdims=True) + NORM_EPS)
    xn = x * rstd
    dxn = dy * g
    dx = rstd * (dxn - xn * jnp.mean(dxn * xn, axis=-1, keepdims=True))
    return dx, dy * xn


def _sigmoid(x):
    return 1.0 / (1.0 + jnp.exp(-x))


def _softplus(x):
    return jnp.maximum(x, 0.0) + jnp.log(1.0 + jnp.exp(-jnp.abs(x)))


def _acc(ref, val, first):
    @pl.when(first)
    def _():
        ref[...] = val

    @pl.when(jnp.logical_not(first))
    def _():
        ref[...] += val


def _colsum8(v):
    rows, n = v.shape
    return jnp.sum(v.reshape(rows // 8, 8, n), axis=0)


def _prep_fn(p, pprev, mu, w0, w2p, a0, a2p, g2, k_k, k_a):
    seg = _seg_ones(RW)
    ps = p + (pprev - p) * mu
    r = ps[:, 0:RW]
    k = ps[:, RW:2 * RW]
    v = ps[:, 2 * RW:3 * RW]
    xwa = ps[:, 3 * RW:3 * RW + 128]
    xg = ps[:, 3 * RW + 128:3 * RW + 256]
    wraw = -_softplus(-(w0 + _mm(jnp.tanh(xwa), w2p, "nn", 3))) - 0.5
    lw = -jnp.exp(wraw)
    a = _sigmoid(a0 + _mm(xwa, a2p, "nn", 3))
    g = _mm(_sigmoid(xg), g2, "nn", 3)
    kk = k * k_k
    kk = kk / jnp.maximum(jnp.sqrt(_segsum(kk * kk, seg)), 1e-12)
    k2 = k * (1.0 + (a - 1.0) * k_a)
    return r, lw, k2, v, kk, a, g


def _solve_unit_lower(lmat, rhs):
    c = lmat.shape[-1]
    row = lax.broadcasted_iota(jnp.int32, (c, c), 0)
    col = lax.broadcasted_iota(jnp.int32, (c, c), 1)
    eye = (row == col).astype(F32)
    ld = jnp.where(row // SUB == col // SUB, lmat, 0.0)
    lo = lmat - ld
    x = eye + ld
    m = ld
    mm = lambda p, q: _mm(p, q, "nn", WKV_PASSES)
    for _ in range(3):
        m = mm(m, m)
        x = x + mm(x, m)
    g = mm(x, lo)
    g2 = mm(g, g)
    w = mm(x, rhs)
    w = w + mm(g2, w)
    return w + mm(g, w)


def _wkv_chunk_fn(s0, r, lw, k, v, kk, a):
    c = r.shape[-2]
    n = 2 * c
    row = lax.broadcasted_iota(jnp.int32, (n, n), 0)
    col = lax.broadcasted_iota(jnp.int32, (n, n), 1)
    same = (row // c) == (col // c)
    incl = jnp.logical_and(row >= col, same)
    strict = jnp.logical_and(row > col, same)
    sel = (lax.broadcasted_iota(jnp.int32, (n, 128), 0) // c) == (lax.broadcasted_iota(jnp.int32, (n, 128), 1) // HEAD_DIM)
    two = lambda z: jnp.concatenate([z, z], axis=-2)
    lw2 = two(lw)
    mm = lambda p_, q_, form: _mm(p_, q_, form, WKV_PASSES)
    cl = _mm(incl.astype(F32), lw2, "nn", "L3")
    p = jnp.exp(cl)
    pinv = jnp.exp(-cl)
    pprev = jnp.exp(cl - lw2)
    kk2 = two(kk)
    at = jnp.where(sel, -kk2 * pprev, 0.0)
    bt = jnp.where(sel, kk2 * two(a) * pinv, 0.0)
    kt = jnp.where(sel, two(k) * pinv, 0.0)
    rt = jnp.where(sel, two(r) * p, 0.0)
    vt = jnp.where(sel, two(v), 0.0)
    ab = jnp.where(strict, mm(at, bt, "nt"), 0.0)
    ak = jnp.where(strict, mm(at, kt, "nt"), 0.0)
    rb = jnp.where(incl, mm(rt, bt, "nt"), 0.0)
    rk = jnp.where(incl, mm(rt, kt, "nt"), 0.0)
    u = _solve_unit_lower(ab, mm(at, s0, "nt") + mm(ak, vt, "nn"))
    y2 = mm(rt, s0, "nt") + mm(rb, u, "nn") + mm(rk, vt, "nn")
    plast = jnp.exp(jnp.sum(lw, axis=-2, keepdims=True))
    s1 = (s0 + mm(u, bt, "tn") + mm(vt, kt, "tn")) * plast
    r2 = lax.broadcasted_iota(jnp.int32, (128, 128), 0) // HEAD_DIM
    c2 = lax.broadcasted_iota(jnp.int32, (128, 128), 1) // HEAD_DIM
    return y2[..., :c, :] + y2[..., c:, :], jnp.where(r2 == c2, s1, 0.0)


def _post_fn(y, r, k2, v, g, lnw, lnb, rk):
    seg = _seg_ones(RW)
    mean = _segsum(y, seg) * (1.0 / HEAD_DIM)
    yc = y - mean
    var = _segsum(yc * yc, seg) * (1.0 / HEAD_DIM)
    yn = yc * lax.rsqrt(var + GN_EPS)
    out = yn * lnw + lnb + _segsum(r * k2 * rk, seg) * v
    return out * g


def _attn_block_fn(q, kc, vc, kp=None, vp=None):
    n = ATTN_BLOCK
    qi = lax.broadcasted_iota(jnp.int32, (n, n), 0)
    kj = lax.broadcasted_iota(jnp.int32, (n, n), 1)
    lane = lax.broadcasted_iota(jnp.int32, (1, 128), 1)
    scale = HEAD_DIM ** -0.5
    os_, ls_ = [], []
    for h in range(2):
        mh = (lane // HEAD_DIM) == h
        qh = jnp.where(mh, q, 0.0)
        sc = jnp.where(kj <= qi, _mm(qh, kc, "nt", ATTN_PASSES) * scale, NEG)
        m = jnp.max(sc, axis=-1, keepdims=True)
        if kp is not None:
            sp = jnp.where(kj >= qi, _mm(qh, kp, "nt", ATTN_PASSES) * scale, NEG)
            m = jnp.maximum(m, jnp.max(sp, axis=-1, keepdims=True))
        pc = jnp.exp(sc - m)
        den = jnp.sum(pc, axis=-1, keepdims=True)
        num = _mm(pc, vc, "nn", ATTN_PASSES)
        if kp is not None:
            pp = jnp.exp(sp - m)
            den = den + jnp.sum(pp, axis=-1, keepdims=True)
            num = num + _mm(pp, vp, "nn", ATTN_PASSES)
        os_.append(num / den)
        ls_.append(m + jnp.log(den))
    m0 = (lane // HEAD_DIM) == 0
    return jnp.where(m0, os_[0], os_[1]), jnp.where(m0, ls_[0], ls_[1])


def _combine_fn(o1, o2, o3, l1, l2, l3, og):
    seg = _seg_ones(RW)
    m = jnp.maximum(jnp.maximum(l1, l2), l3)
    e1, e2, e3 = jnp.exp(l1 - m), jnp.exp(l2 - m), jnp.exp(l3 - m)
    o = (e1 * o1 + e2 * o2 + e3 * o3) / (e1 + e2 + e3)
    o = o * lax.rsqrt(_segsum(o * o, seg) * (1.0 / HEAD_DIM) + NORM_EPS)
    return o * og


def _in_proj(x, g1, win):
    t = x.shape[0]
    tm = 256

    def body(x_ref, g_ref, w_ref, h_ref, p_ref):
        h = _rms_fwd(x_ref[...], g_ref[...]).astype(BF16)
        h_ref[...] = h
        p_ref[...] = _dot(h, w_ref[...])

    return pl.pallas_call(
        body, name="in_proj", grid=(t // tm,),
        in_specs=[pl.BlockSpec((tm, D_MODEL), lambda i: (i, 0)), pl.BlockSpec((1, D_MODEL), lambda i: (0, 0)),
                  pl.BlockSpec((D_MODEL, IN_COLS), lambda i: (0, 0))],
        out_specs=[pl.BlockSpec((tm, D_MODEL), lambda i: (i, 0)), pl.BlockSpec((tm, IN_COLS), lambda i: (i, 0))],
        out_shape=[jax.ShapeDtypeStruct((t, D_MODEL), BF16), jax.ShapeDtypeStruct((t, IN_COLS), F32)],
        compiler_params=_params(("parallel",)),
    )(x, g1, win)


def _shifted(p, last8, first):
    prow = jnp.where(first, 0.0, last8[7:8, :])
    rolled = pltpu.roll(p, 1, axis=0)
    rid = lax.broadcasted_iota(jnp.int32, p.shape, 0)
    return jnp.where(rid == 0, prow, rolled)


_PREP_TM = 256


def _prep_specs(tm):
    vec = lambda n: pl.BlockSpec((1, n), lambda i: (0, 0))
    mat = lambda r, n: pl.BlockSpec((r, n), lambda i: (0, 0))
    return [vec(SHIFT_COLS), vec(RW), mat(128, RW), vec(RW), mat(128, RW), mat(128, RW), vec(RW), vec(RW)]


def _prep_fwd(proj, pw):
    t = proj.shape[0]
    tm = _PREP_TM

    def body(p_ref, l8_ref, mu, w0, w2p, a0, a2p, g2, k_k, k_a, *outs):
        p = p_ref[...]
        pprev = _shifted(p, l8_ref[...], pl.program_id(0) == 0)
        res = _prep_fn(p, pprev, mu[...], w0[...], w2p[...], a0[...], a2p[...], g2[...], k_k[...], k_a[...])
        for o_ref, val in zip(outs, res):
            o_ref[...] = val

    row = pl.BlockSpec((tm, RW), lambda i: (i, 0))
    return pl.pallas_call(
        body, name="rwkv_prep", grid=(t // tm,),
        in_specs=[pl.BlockSpec((tm, SHIFT_COLS), lambda i: (i, 0)),
                  pl.BlockSpec((8, SHIFT_COLS), lambda i: (jnp.maximum(i * (tm // 8) - 1, 0), 0))] + _prep_specs(tm),
        out_specs=[row] * 7,
        out_shape=[jax.ShapeDtypeStruct((t, RW), F32)] * 7,
        compiler_params=_params(("parallel",)),
    )(proj, proj, *pw)


def _pairs(ref):
    return jnp.stack([ref[:, 128 * p:128 * (p + 1)] for p in range(N_PAIR)], axis=0)


def _wkv_fwd(r, lw, k2, v, kk, a):
    t = r.shape[0]
    nc = t // CHUNK

    def body(r_ref, lw_ref, k_ref, v_ref, kk_ref, a_ref, y_ref, s_ref, st):
        @pl.when(pl.program_id(0) == 0)
        def _():
            st[...] = jnp.zeros_like(st)

        s0 = st[...]
        s_ref[0] = s0
        y, s1 = _wkv_chunk_fn(s0, *[_pairs(ref) for ref in (r_ref, lw_ref, k_ref, v_ref, kk_ref, a_ref)])
        for p in range(N_PAIR):
            y_ref[:, 128 * p:128 * (p + 1)] = y[p]
        st[...] = s1

    blk = pl.BlockSpec((CHUNK, RW), lambda c: (c, 0))
    return pl.pallas_call(
        body, name="wkv_fwd", grid=(nc,),
        in_specs=[blk] * 6,
        out_specs=[blk, pl.BlockSpec((1, N_PAIR, 128, 128), lambda c: (c, 0, 0, 0))],
        out_shape=[jax.ShapeDtypeStruct((t, RW), F32), jax.ShapeDtypeStruct((nc, N_PAIR, 128, 128), F32)],
        scratch_shapes=[pltpu.VMEM((N_PAIR, 128, 128), F32)],
        compiler_params=_params(("arbitrary",)),
    )(r, lw, k2, v, kk, a)


_POST_TM = 256


def _post_fwd(y, r, k2, v, g, lnw, lnb, rk):
    t = y.shape[0]
    tm = _POST_TM

    def body(y_ref, r_ref, k_ref, v_ref, g_ref, lnw_ref, lnb_ref, rk_ref, o_ref):
        o_ref[...] = _post_fn(y_ref[...], r_ref[...], k_ref[...], v_ref[...], g_ref[...],
                              lnw_ref[...], lnb_ref[...], rk_ref[...]).astype(BF16)

    row = pl.BlockSpec((tm, RW), lambda i: (i, 0))
    vec = pl.BlockSpec((1, RW), lambda i: (0, 0))
    return pl.pallas_call(
        body, name="rwkv_post", grid=(t // tm,),
        in_specs=[row] * 5 + [vec] * 3, out_specs=row,
        out_shape=jax.ShapeDtypeStruct((t, RW), BF16),
        compiler_params=_params(("parallel",)),
    )(y, r, k2, v, g, lnw, lnb, rk)


def _attn_fwd(qa, ka, va, qo, ko, vo, ncol, name):
    length = qa.shape[0]
    nb = length // ATTN_BLOCK

    def body(q_ref, k_ref, v_ref, o_ref, l_ref):
        first = pl.ds(0, ATTN_BLOCK)
        o_ref[first, :], l_ref[first, :] = _attn_block_fn(q_ref[first, :], k_ref[first, :], v_ref[first, :])

        def blk(n, carry):
            cur = pl.ds(pl.multiple_of(n * ATTN_BLOCK, ATTN_BLOCK), ATTN_BLOCK)
            prv = pl.ds(pl.multiple_of((n - 1) * ATTN_BLOCK, ATTN_BLOCK), ATTN_BLOCK)
            o, lse = _attn_block_fn(q_ref[cur, :], k_ref[cur, :], v_ref[cur, :], k_ref[prv, :], v_ref[prv, :])
            o_ref[cur, :] = o
            l_ref[cur, :] = lse
            return carry

        if nb > 1:
            lax.fori_loop(1, nb, blk, 0)

    spec = lambda off: pl.BlockSpec((length, 128), lambda c: (0, off + c))
    return pl.pallas_call(
        body, name=name, grid=(ncol,),
        in_specs=[spec(qo), spec(ko), spec(vo)], out_specs=[spec(0), spec(0)],
        out_shape=[jax.ShapeDtypeStruct((length, ncol * 128), F32)] * 2,
        compiler_params=_params(("parallel",)),
    )(qa, ka, va)


_COMB_TM = 256


def _combine_fwd(os_, ls_, og):
    t = os_[0].shape[0]
    tm = _COMB_TM

    def body(o1, o2, o3, l1, l2, l3, og_ref, y_ref):
        y_ref[...] = _combine_fn(o1[...], o2[...], o3[...], l1[...], l2[...], l3[...], og_ref[...]).astype(BF16)

    row = pl.BlockSpec((tm, RW), lambda i: (i, 0))
    return pl.pallas_call(
        body, name="attn_combine", grid=(t // tm,),
        in_specs=[row] * 6 + [pl.BlockSpec((1, RW), lambda i: (0, 0))], out_specs=row,
        out_shape=jax.ShapeDtypeStruct((t, RW), BF16),
        compiler_params=_params(("parallel",)),
    )(*os_, *ls_, og)


def _out_proj(x, ycat, wout, g2):
    t = x.shape[0]
    tm = 256

    def body(x_ref, y_ref, w_ref, g_ref, x1_ref, h_ref):
        x1 = x_ref[...] + _dot(y_ref[...], w_ref[...])
        x1_ref[...] = x1
        h_ref[...] = _rms_fwd(x1, g_ref[...]).astype(BF16)

    row = pl.BlockSpec((tm, D_MODEL), lambda i: (i, 0))
    return pl.pallas_call(
        body, name="out_proj", grid=(t // tm,),
        in_specs=[row, row, pl.BlockSpec((D_MODEL, D_MODEL), lambda i: (0, 0)), pl.BlockSpec((1, D_MODEL), lambda i: (0, 0))],
        out_specs=[row, row],
        out_shape=[jax.ShapeDtypeStruct((t, D_MODEL), F32), jax.ShapeDtypeStruct((t, D_MODEL), BF16)],
        compiler_params=_params(("parallel",)),
    )(x, ycat, wout, g2)


def _ffn_up(h2, wg, wu):
    t = h2.shape[0]
    tm = 256

    def body(h_ref, wg_ref, wu_ref, gt_ref, up_ref, act_ref):
        h = h_ref[...]
        gt = _dot(h, wg_ref[...])
        up = _dot(h, wu_ref[...])
        gt_ref[...] = gt
        up_ref[...] = up
        act_ref[...] = (gt * _sigmoid(gt) * up).astype(BF16)

    wide = pl.BlockSpec((tm, D_FF), lambda i: (i, 0))
    wsp = pl.BlockSpec((D_MODEL, D_FF), lambda i: (0, 0))
    return pl.pallas_call(
        body, name="ffn_up", grid=(t // tm,),
        in_specs=[pl.BlockSpec((tm, D_MODEL), lambda i: (i, 0)), wsp, wsp],
        out_specs=[wide, wide, wide],
        out_shape=[jax.ShapeDtypeStruct((t, D_FF), F32)] * 2 + [jax.ShapeDtypeStruct((t, D_FF), BF16)],
        compiler_params=_params(("parallel",)),
    )(h2, wg, wu)


def _ffn_down_loss(x1, act, wd, gf, tgt):
    t = x1.shape[0]
    tm = 256

    def body(x1_ref, a_ref, w_ref, g_ref, t_ref, dx_ref, dxb_ref, loss_ref, dg_ref):
        first = pl.program_id(0) == 0
        x2 = x1_ref[...] + _dot(a_ref[...], w_ref[...])
        g = g_ref[...]
        diff = _rms_fwd(x2, g) - t_ref[...]
        lrow = 0.5 * jnp.sum(_colsum8(diff * diff), axis=1, keepdims=True) * (1.0 / D_MODEL)
        _acc(loss_ref, jnp.broadcast_to(lrow, (8, 128)), first)
        dx2, dgr = _rms_bwd(diff * (1.0 / D_MODEL), x2, g)
        dx_ref[...] = dx2
        dxb_ref[...] = dx2.astype(BF16)
        _acc(dg_ref, _colsum8(dgr), first)

    row = pl.BlockSpec((tm, D_MODEL), lambda i: (i, 0))
    return pl.pallas_call(
        body, name="ffn_down_loss", grid=(t // tm,),
        in_specs=[row, pl.BlockSpec((tm, D_FF), lambda i: (i, 0)), pl.BlockSpec((D_FF, D_MODEL), lambda i: (0, 0)),
                  pl.BlockSpec((1, D_MODEL), lambda i: (0, 0)), row],
        out_specs=[row, row, pl.BlockSpec((8, 128), lambda i: (0, 0)), pl.BlockSpec((8, D_MODEL), lambda i: (0, 0))],
        out_shape=[jax.ShapeDtypeStruct((t, D_MODEL), F32), jax.ShapeDtypeStruct((t, D_MODEL), BF16),
                   jax.ShapeDtypeStruct((8, 128), F32), jax.ShapeDtypeStruct((8, D_MODEL), F32)],
        compiler_params=_params(("arbitrary",)),
    )(x1, act, wd, gf, tgt)


def _ffn_bwd_act(dx2b, wd, gt, up):
    t = dx2b.shape[0]
    tm = 256

    def body(dx_ref, w_ref, gt_ref, up_ref, dgt_ref, dup_ref):
        dact = _dot_nt(dx_ref[...], w_ref[...])
        gt = gt_ref[...]
        sg = _sigmoid(gt)
        dgt_ref[...] = (dact * up_ref[...] * sg * (1.0 + gt * (1.0 - sg))).astype(BF16)
        dup_ref[...] = (dact * gt * sg).astype(BF16)

    wide = pl.BlockSpec((tm, D_FF), lambda i: (i, 0))
    return pl.pallas_call(
        body, name="ffn_bwd_act", grid=(t // tm,),
        in_specs=[pl.BlockSpec((tm, D_MODEL), lambda i: (i, 0)), pl.BlockSpec((D_FF, D_MODEL), lambda i: (0, 0)), wide, wide],
        out_specs=[wide, wide],
        out_shape=[jax.ShapeDtypeStruct((t, D_FF), BF16)] * 2,
        compiler_params=_params(("parallel",)),
    )(dx2b, wd, gt, up)


def _ffn_bwd_h(dgt, dup, wg, wu, dx2, x1, g2, wout):
    t = dgt.shape[0]
    tm = 256

    def body(dgt_ref, dup_ref, wg_ref, wu_ref, dx2_ref, x1_ref, g_ref, wo_ref, dx1_ref, dx1b_ref, dya_ref, dyb_ref, dg_ref):
        dh = _dot_nt(dgt_ref[...], wg_ref[...]) + _dot_nt(dup_ref[...], wu_ref[...])
        dxn, dgr = _rms_bwd(dh, x1_ref[...], g_ref[...])
        dx1 = dx2_ref[...] + dxn
        dx1_ref[...] = dx1
        dx1b = dx1.astype(BF16)
        dx1b_ref[...] = dx1b
        dy = _dot_nt(dx1b, wo_ref[...])
        dya_ref[...] = dy[:, :RW]
        dyb_ref[...] = dy[:, RW:]
        _acc(dg_ref, _colsum8(dgr), pl.program_id(0) == 0)

    wide = pl.BlockSpec((tm, D_FF), lambda i: (i, 0))
    row = pl.BlockSpec((tm, D_MODEL), lambda i: (i, 0))
    half = pl.BlockSpec((tm, RW), lambda i: (i, 0))
    wsp = pl.BlockSpec((D_MODEL, D_FF), lambda i: (0, 0))
    return pl.pallas_call(
        body, name="ffn_bwd_h", grid=(t // tm,),
        in_specs=[wide, wide, wsp, wsp, row, row, pl.BlockSpec((1, D_MODEL), lambda i: (0, 0)),
                  pl.BlockSpec((D_MODEL, D_MODEL), lambda i: (0, 0))],
        out_specs=[row, row, half, half, pl.BlockSpec((8, D_MODEL), lambda i: (0, 0))],
        out_shape=[jax.ShapeDtypeStruct((t, D_MODEL), F32), jax.ShapeDtypeStruct((t, D_MODEL), BF16),
                   jax.ShapeDtypeStruct((t, RW), F32), jax.ShapeDtypeStruct((t, RW), F32),
                   jax.ShapeDtypeStruct((8, D_MODEL), F32)],
        compiler_params=_params(("arbitrary",)),
    )(dgt, dup, wg, wu, dx2, x1, g2, wout)


def _wgrad(a, b, tk, tn, name):
    t, kdim = a.shape
    ndim = b.shape[1]

    def body(a_ref, b_ref, o_ref):
        o_ref[...] = _dot_tn(a_ref[...], b_ref[...])

    return pl.pallas_call(
        body, name=name, grid=(kdim // tk, ndim // tn),
        in_specs=[pl.BlockSpec((t, tk), lambda i, j: (0, i)), pl.BlockSpec((t, tn), lambda i, j: (0, j))],
        out_specs=pl.BlockSpec((tk, tn), lambda i, j: (i, j)),
        out_shape=jax.ShapeDtypeStruct((kdim, ndim), F32),
        compiler_params=_params(("parallel", "parallel")),
    )(a, b)


def _post_bwd(dya, y, r, k2, v, g, lnw, lnb, rk):
    t = y.shape[0]
    tm = _POST_TM

    def body(d_ref, y_ref, r_ref, k_ref, v_ref, g_ref, lnw_ref, lnb_ref, rk_ref,
             dy_ref, dr_ref, dk_ref, dv_ref, dg_ref, dlnw_ref, dlnb_ref, drk_ref):
        first = pl.program_id(0) == 0
        ones = jnp.ones((tm, 1), F32)
        prim = (y_ref[...], r_ref[...], k_ref[...], v_ref[...], g_ref[...],
                ones * lnw_ref[...], ones * lnb_ref[...], ones * rk_ref[...])
        _, vjp = jax.vjp(_post_fn, *prim)
        dy, dr, dk, dv, dg, dlnw, dlnb, drk = vjp(d_ref[...])
        dy_ref[...] = dy
        dr_ref[...] = dr
        dk_ref[...] = dk
        dv_ref[...] = dv
        dg_ref[...] = dg
        _acc(dlnw_ref, _colsum8(dlnw), first)
        _acc(dlnb_ref, _colsum8(dlnb), first)
        _acc(drk_ref, _colsum8(drk), first)

    row = pl.BlockSpec((tm, RW), lambda i: (i, 0))
    vec = pl.BlockSpec((1, RW), lambda i: (0, 0))
    part = pl.BlockSpec((8, RW), lambda i: (0, 0))
    return pl.pallas_call(
        body, name="rwkv_post_bwd", grid=(t // tm,),
        in_specs=[row] * 6 + [vec] * 3, out_specs=[row] * 5 + [part] * 3,
        out_shape=[jax.ShapeDtypeStruct((t, RW), F32)] * 5 + [jax.ShapeDtypeStruct((8, RW), F32)] * 3,
        compiler_params=_params(("arbitrary",)),
    )(dya, y, r, k2, v, g, lnw, lnb, rk)


def _wkv_bwd(dy, s0s, r, lw, k2, v, kk, a):
    t = r.shape[0]
    nc = t // CHUNK

    def body(dy_ref, s_ref, r_ref, lw_ref, k_ref, v_ref, kk_ref, a_ref,
             dr_ref, dlw_ref, dk_ref, dv_ref, dkk_ref, da_ref, ds):
        @pl.when(pl.program_id(0) == 0)
        def _():
            ds[...] = jnp.zeros_like(ds)

        _, vjp = jax.vjp(_wkv_chunk_fn, s_ref[0],
                         *[_pairs(ref) for ref in (r_ref, lw_ref, k_ref, v_ref, kk_ref, a_ref)])
        res = vjp((_pairs(dy_ref), ds[...]))
        ds[...] = res[0]
        for ref, val in zip((dr_ref, dlw_ref, dk_ref, dv_ref, dkk_ref, da_ref), res[1:]):
            for p in range(N_PAIR):
                ref[:, 128 * p:128 * (p + 1)] = val[p]

    blk = pl.BlockSpec((CHUNK, RW), lambda c: (nc - 1 - c, 0))
    return pl.pallas_call(
        body, name="wkv_bwd", grid=(nc,),
        in_specs=[blk, pl.BlockSpec((1, N_PAIR, 128, 128), lambda c: (nc - 1 - c, 0, 0, 0))] + [blk] * 6,
        out_specs=[blk] * 6,
        out_shape=[jax.ShapeDtypeStruct((t, RW), F32)] * 6,
        scratch_shapes=[pltpu.VMEM((N_PAIR, 128, 128), F32)],
        compiler_params=_params(("arbitrary",)),
    )(dy, s0s, r, lw, k2, v, kk, a)


def _prep_bwd(proj, pw, douts):
    t = proj.shape[0]
    tm = _PREP_TM
    nt = t // tm

    def body(p_ref, l8_ref, mu, w0, w2p, a0, a2p, g2, k_k, k_a, dr, dr2, dlw, dk2, dk22, dv, dv2, dkk, da, dg,
             dp_ref, dmu_ref, dw0_ref, dw2_ref, da0_ref, da2_ref, dg2_ref, dkk_ref, dka_ref, carry):
        i = pl.program_id(0)
        first = i == 0

        @pl.when(first)
        def _():
            carry[...] = jnp.zeros_like(carry)

        p = p_ref[...]
        pprev = _shifted(p, l8_ref[...], i == nt - 1)
        ones = jnp.ones((tm, 1), F32)
        prim = (p, pprev, ones * mu[...], ones * w0[...], w2p[...], ones * a0[...], a2p[...], g2[...],
                ones * k_k[...], ones * k_a[...])
        _, vjp = jax.vjp(_prep_fn, *prim)
        dp, dpp, dmu, dw0, dw2, da0, da2, dg2, dkk_, dka = vjp(
            (dr[...] + dr2[...], dlw[...], dk2[...] + dk22[...], dv[...] + dv2[...], dkk[...], da[...], dg[...]))
        up = pltpu.roll(dpp, tm - 1, axis=0)
        rid = lax.broadcasted_iota(jnp.int32, dpp.shape, 0)
        dp_ref[...] = dp + jnp.where(rid == tm - 1, carry[0:1, :], up)
        carry[...] = jnp.broadcast_to(dpp[0:1, :], carry.shape)
        _acc(dmu_ref, _colsum8(dmu), first)
        _acc(dw0_ref, _colsum8(dw0), first)
        _acc(dw2_ref, dw2, first)
        _acc(da0_ref, _colsum8(da0), first)
        _acc(da2_ref, da2, first)
        _acc(dg2_ref, dg2, first)
        _acc(dkk_ref, _colsum8(dkk_), first)
        _acc(dka_ref, _colsum8(dka), first)

    rev = lambda i: (nt - 1 - i, 0)
    row = pl.BlockSpec((tm, RW), rev)
    part = lambda n: pl.BlockSpec((8, n), lambda i: (0, 0))
    mat = pl.BlockSpec((128, RW), lambda i: (0, 0))
    return pl.pallas_call(
        body, name="rwkv_prep_bwd", grid=(nt,),
        in_specs=[pl.BlockSpec((tm, SHIFT_COLS), rev),
                  pl.BlockSpec((8, SHIFT_COLS), lambda i: (jnp.maximum((nt - 1 - i) * (tm // 8) - 1, 0), 0))]
                 + _prep_specs(tm) + [row] * 10,
        out_specs=[pl.BlockSpec((tm, SHIFT_COLS), rev), part(SHIFT_COLS), part(RW), mat, part(RW), mat, mat,
                   part(RW), part(RW)],
        out_shape=[jax.ShapeDtypeStruct((t, SHIFT_COLS), F32), jax.ShapeDtypeStruct((8, SHIFT_COLS), F32),
                   jax.ShapeDtypeStruct((8, RW), F32), jax.ShapeDtypeStruct((128, RW), F32),
                   jax.ShapeDtypeStruct((8, RW), F32), jax.ShapeDtypeStruct((128, RW), F32),
                   jax.ShapeDtypeStruct((128, RW), F32), jax.ShapeDtypeStruct((8, RW), F32),
                   jax.ShapeDtypeStruct((8, RW), F32)],
        scratch_shapes=[pltpu.VMEM((8, SHIFT_COLS), F32)],
        compiler_params=_params(("arbitrary",)),
    )(proj, proj, *pw, *douts)


def _combine_bwd(dyb, os_, ls_, og):
    t = dyb.shape[0]
    tm = _COMB_TM

    def body(d_ref, o1, o2, o3, l1, l2, l3, og_ref, do1, do2, do3, dl1, dl2, dl3, dog_ref):
        ones = jnp.ones((tm, 1), F32)
        _, vjp = jax.vjp(_combine_fn, o1[...], o2[...], o3[...], l1[...], l2[...], l3[...], ones * og_ref[...])
        res = vjp(d_ref[...])
        for ref, val in zip((do1, do2, do3, dl1, dl2, dl3), res[:6]):
            ref[...] = val
        _acc(dog_ref, _colsum8(res[6]), pl.program_id(0) == 0)

    row = pl.BlockSpec((tm, RW), lambda i: (i, 0))
    return pl.pallas_call(
        body, name="attn_combine_bwd", grid=(t // tm,),
        in_specs=[row] * 7 + [pl.BlockSpec((1, RW), lambda i: (0, 0))],
        out_specs=[row] * 6 + [pl.BlockSpec((8, RW), lambda i: (0, 0))],
        out_shape=[jax.ShapeDtypeStruct((t, RW), F32)] * 6 + [jax.ShapeDtypeStruct((8, RW), F32)],
        compiler_params=_params(("arbitrary",)),
    )(dyb, *os_, *ls_, og)


def _attn_bwd(do, dl, qa, ka, va, qo, ko, vo, ncol, name):
    length = qa.shape[0]
    nb = length // ATTN_BLOCK

    def body(do_ref, dl_ref, q_ref, k_ref, v_ref, dq_ref, dk_ref, dv_ref):
        dk_ref[...] = jnp.zeros_like(dk_ref)
        dv_ref[...] = jnp.zeros_like(dv_ref)

        first = pl.ds(0, ATTN_BLOCK)
        _, vjp0 = jax.vjp(_attn_block_fn, q_ref[first, :], k_ref[first, :], v_ref[first, :])
        dq0, dk0, dv0 = vjp0((do_ref[first, :], dl_ref[first, :]))
        dq_ref[first, :] = dq0
        dk_ref[first, :] += dk0
        dv_ref[first, :] += dv0

        def blk(n, carry):
            cur = pl.ds(pl.multiple_of(n * ATTN_BLOCK, ATTN_BLOCK), ATTN_BLOCK)
            prv = pl.ds(pl.multiple_of((n - 1) * ATTN_BLOCK, ATTN_BLOCK), ATTN_BLOCK)
            _, vjp = jax.vjp(_attn_block_fn, q_ref[cur, :], k_ref[cur, :], v_ref[cur, :], k_ref[prv, :], v_ref[prv, :])
            dq, dkc, dvc, dkp, dvp = vjp((do_ref[cur, :], dl_ref[cur, :]))
            dq_ref[cur, :] = dq
            dk_ref[cur, :] += dkc
            dv_ref[cur, :] += dvc
            dk_ref[prv, :] += dkp
            dv_ref[prv, :] += dvp
            return carry

        if nb > 1:
            lax.fori_loop(1, nb, blk, 0)

    spec = lambda off: pl.BlockSpec((length, 128), lambda c: (0, off + c))
    return pl.pallas_call(
        body, name=name, grid=(ncol,),
        in_specs=[spec(0), spec(0), spec(qo), spec(ko), spec(vo)], out_specs=[spec(0)] * 3,
        out_shape=[jax.ShapeDtypeStruct((length, ncol * 128), F32)] * 3,
        compiler_params=_params(("parallel",)),
    )(do, dl, qa, ka, va)


def _in_proj_bwd(dpa, dqs, dks, dvs, win, x, g1, dx1):
    t = x.shape[0]
    tm = 256

    def body(dpa_ref, q1, q2, q3, k1, k2, k3, v1, v2, v3, w_ref, x_ref, g_ref, dx1_ref, dproj_ref, dx_ref, dg_ref):
        parts = (dpa_ref[...], q1[...] + q2[...] + q3[...], k1[...] + k2[...] + k3[...], v1[...] + v2[...] + v3[...])
        dproj = jnp.concatenate([z.astype(BF16) for z in parts], axis=1)
        dproj_ref[...] = dproj
        dh = _dot_nt(dproj, w_ref[...])
        dxn, dgr = _rms_bwd(dh, x_ref[...], g_ref[...])
        dx_ref[...] = dx1_ref[...] + dxn
        _acc(dg_ref, _colsum8(dgr), pl.program_id(0) == 0)

    row = pl.BlockSpec((tm, D_MODEL), lambda i: (i, 0))
    half = pl.BlockSpec((tm, RW), lambda i: (i, 0))
    return pl.pallas_call(
        body, name="in_proj_bwd", grid=(t // tm,),
        in_specs=[pl.BlockSpec((tm, SHIFT_COLS), lambda i: (i, 0))] + [half] * 9
                 + [pl.BlockSpec((D_MODEL, IN_COLS), lambda i: (0, 0)), row, pl.BlockSpec((1, D_MODEL), lambda i: (0, 0)), row],
        out_specs=[pl.BlockSpec((tm, IN_COLS), lambda i: (i, 0)), row, pl.BlockSpec((8, D_MODEL), lambda i: (0, 0))],
        out_shape=[jax.ShapeDtypeStruct((t, IN_COLS), BF16), jax.ShapeDtypeStruct((t, D_MODEL), F32),
                   jax.ShapeDtypeStruct((8, D_MODEL), F32)],
        compiler_params=_params(("arbitrary",)),
    )(dpa, *dqs, *dks, *dvs, win, x, g1, dx1)


def _pad_lora(w, lo):
    z = jnp.zeros((64, RW), F32)
    return jnp.concatenate([w, z], axis=0) if lo == 0 else jnp.concatenate([z, w], axis=0)


def _dilate(z, d):
    return z if d == 1 else z.reshape(z.shape[0] // d, z.shape[1] * d)


def _local_step(x, tgt, win, wout, wg, wu, wd, vecs, w2, a2, g2m):
    t = x.shape[0]
    pw = (vecs["mu_shift"], vecs["decay_w0"], _pad_lora(w2, 0), vecs["iclr_a0"], _pad_lora(a2, 64), g2m,
          vecs["k_k"], vecs["k_a"])
    h, proj = _in_proj(x, vecs["mix_norm_g"], win)
    r, lw, k2, v, kk, a, g = _prep_fwd(proj, pw)
    y, s0s = _wkv_fwd(r, lw, k2, v, kk, a)
    ya = _post_fwd(y, r, k2, v, g, vecs["ln_x_w"], vecs["ln_x_b"], vecs["r_k"])

    qoff = SHIFT_COLS // 128
    qkv = {}
    os_, ls_ = [], []
    for d in DILATIONS:
        if d == 1:
            arrs, offs, ncol = (proj, proj, proj), (qoff, qoff + N_PAIR, qoff + 2 * N_PAIR), N_PAIR
        else:
            arrs = tuple(_dilate(proj[:, SHIFT_COLS + j * RW:SHIFT_COLS + (j + 1) * RW], d) for j in range(3))
            offs, ncol = (0, 0, 0), N_PAIR * d
        qkv[d] = (arrs, offs, ncol)
        o, l = _attn_fwd(*arrs, *offs, ncol, "attn_fwd_d%d" % d)
        os_.append(o.reshape(t, RW))
        ls_.append(l.reshape(t, RW))
    yb = _combine_fwd(os_, ls_, vecs["attn_out_g"])

    ycat = jnp.concatenate([ya, yb], axis=1)
    x1, h2 = _out_proj(x, ycat, wout, vecs["ffn_norm_g"])
    gt, up, act = _ffn_up(h2, wg, wu)
    dx2, dx2b, loss8, dgf = _ffn_down_loss(x1, act, wd, vecs["final_norm_g"], tgt)

    dgt, dup = _ffn_bwd_act(dx2b, wd, gt, up)
    dx1, dx1b, dya, dyb, dg2n = _ffn_bwd_h(dgt, dup, wg, wu, dx2, x1, vecs["ffn_norm_g"], wout)
    gw = {
        "w_down": _wgrad(act, dx2b, 1408, 1024, "wgrad_down"),
        "w_gate": _wgrad(h2, dgt, 1024, 1408, "wgrad_gate"),
        "w_up": _wgrad(h2, dup, 1024, 1408, "wgrad_up"),
        "w_out": _wgrad(ycat, dx1b, 1024, 1024, "wgrad_out"),
    }

    dy, dr_p, dk2_p, dv_p, dg, dlnw, dlnb, drk = _post_bwd(dya, y, r, k2, v, g, vecs["ln_x_w"], vecs["ln_x_b"], vecs["r_k"])
    dr_s, dlw, dk2_s, dv_s, dkk, da = _wkv_bwd(dy, s0s, r, lw, k2, v, kk, a)
    dpa, dmu, dw0, dw2p, da0, da2p, dg2m, dk_k, dk_a = _prep_bwd(
        proj, pw, (dr_p, dr_s, dlw, dk2_p, dk2_s, dv_p, dv_s, dkk, da, dg))

    cb = _combine_bwd(dyb, os_, ls_, vecs["attn_out_g"])
    dqs, dks, dvs = [], [], []
    for j, d in enumerate(DILATIONS):
        arrs, offs, ncol = qkv[d]
        dq, dk, dv = _attn_bwd(_dilate(cb[j], d), _dilate(cb[3 + j], d), *arrs, *offs, ncol, "attn_bwd_d%d" % d)
        dqs.append(dq.reshape(t, RW))
        dks.append(dk.reshape(t, RW))
        dvs.append(dv.reshape(t, RW))
    dproj, dx, dg1 = _in_proj_bwd(dpa, dqs, dks, dvs, win, x, vecs["mix_norm_g"], dx1)
    gw["w_in"] = _wgrad(h, dproj, 1024, 1664, "wgrad_in")
    gw["decay_w2"] = dw2p[:64]
    gw["iclr_a2"] = da2p[64:]
    gw["gate_g2"] = dg2m
    gv = {"mix_norm_g": dg1, "mu_shift": dmu, "decay_w0": dw0, "iclr_a0": da0, "k_k": dk_k, "k_a": dk_a, "r_k": drk,
          "ln_x_w": dlnw, "ln_x_b": dlnb, "attn_out_g": cb[6], "ffn_norm_g": dg2n, "final_norm_g": dgf}
    return loss8, dx, gw, gv


N_CHIP = 4
N_DEV = 8
MATS = ("w_in", "w_out", "w_gate", "w_up", "w_down")
LORAS = ("decay_w2", "iclr_a2", "gate_g2")
VECS = (("mix_norm_g", 1024), ("mu_shift", 1792), ("decay_w0", 512), ("iclr_a0", 512), ("k_k", 512), ("k_a", 512),
        ("r_k", 512), ("ln_x_w", 512), ("ln_x_b", 512), ("attn_out_g", 512), ("ffn_norm_g", 1024),
        ("final_norm_g", 1024))
N_VEC = sum(n for _, n in VECS)
N_SMALL = N_VEC + 128
ROWS_PAD = 3328
ANY = pl.BlockSpec(memory_space=pl.ANY)


def _flip(v, f):
    return 1 - v if f else v


def _gather_weights(wpack, lpack):
    def body(w_ref, l_ref, wo_ref, lo_ref, send_sems, recv_sems, local_sems):
        x, y, c = lax.axis_index("x"), lax.axis_index("y"), lax.axis_index("c")
        me = 2 * x + y
        pairs = ((w_ref, wo_ref), (l_ref, lo_ref))
        local = [pltpu.make_async_copy(src, dst.at[me], local_sems.at[a]) for a, (src, dst) in enumerate(pairs)]
        for cp in local:
            cp.start()
        chips = [(1 - x, y), (x, 1 - y), (1 - x, 1 - y)]

        def copy(j, a, slot):
            src, dst = pairs[a]
            return pltpu.make_async_remote_copy(src_ref=src, dst_ref=dst.at[slot], send_sem=send_sems.at[2 * j + a],
                                                recv_sem=recv_sems.at[2 * j + a], device_id=(*chips[j], c),
                                                device_id_type=MESH)

        sends = [copy(j, a, me) for j in range(3) for a in range(2)]
        for cp in sends:
            cp.start()
        for j, (px, py) in enumerate(chips):
            for a in range(2):
                copy(j, a, 2 * px + py).wait_recv()
        for cp in sends:
            cp.wait_send()
        for cp in local:
            cp.wait()

    return pl.pallas_call(
        body, name="gather_weights", in_specs=[ANY, ANY], out_specs=[ANY, ANY],
        out_shape=[jax.ShapeDtypeStruct((N_CHIP,) + wpack.shape, wpack.dtype),
                   jax.ShapeDtypeStruct((N_CHIP,) + lpack.shape, lpack.dtype)],
        scratch_shapes=[pltpu.SemaphoreType.DMA((6,)), pltpu.SemaphoreType.DMA((6,)), pltpu.SemaphoreType.DMA((2,))],
    )(wpack, lpack)


def _exchange_grads(gpack, small):
    def body(g_ref, s_ref, go_ref, so_ref, send_sems, recv_sems, local_sems):
        x, y, c = lax.axis_index("x"), lax.axis_index("y"), lax.axis_index("c")
        me = 4 * x + 2 * y + c
        local = [pltpu.make_async_copy(g_ref.at[2 * x + y], go_ref.at[me], local_sems.at[0]),
                 pltpu.make_async_copy(s_ref, so_ref.at[me], local_sems.at[1])]
        for cp in local:
            cp.start()
        peers = []
        for k in range(1, N_DEV):
            peers.append((_flip(x, k & 4), _flip(y, k & 2), _flip(c, k & 1)))

        def copies(k, slot):
            px, py, pc = peers[k]
            kw = dict(device_id=(px, py, pc), device_id_type=MESH)
            return [pltpu.make_async_remote_copy(src_ref=g_ref.at[2 * px + py], dst_ref=go_ref.at[slot],
                                                 send_sem=send_sems.at[2 * k], recv_sem=recv_sems.at[2 * k], **kw),
                    pltpu.make_async_remote_copy(src_ref=s_ref, dst_ref=so_ref.at[slot],
                                                 send_sem=send_sems.at[2 * k + 1], recv_sem=recv_sems.at[2 * k + 1], **kw)]

        sends = [cp for k in range(N_DEV - 1) for cp in copies(k, me)]
        for cp in sends:
            cp.start()
        for k, (px, py, pc) in enumerate(peers):
            for cp in copies(k, 4 * px + 2 * py + pc):
                cp.wait_recv()
        for cp in sends:
            cp.wait_send()
        for cp in local:
            cp.wait()

    ns = 2 * (N_DEV - 1)
    return pl.pallas_call(
        body, name="exchange_grads", in_specs=[ANY, ANY], out_specs=[ANY, ANY],
        out_shape=[jax.ShapeDtypeStruct((N_DEV,) + gpack.shape[1:], gpack.dtype),
                   jax.ShapeDtypeStruct((N_DEV,) + small.shape, small.dtype)],
        scratch_shapes=[pltpu.SemaphoreType.DMA((ns,)), pltpu.SemaphoreType.DMA((ns,)), pltpu.SemaphoreType.DMA((2,))],
    )(gpack, small)


def _adamw(w, g, m, v):
    m = ADAM_B1 * m + (1.0 - ADAM_B1) * g
    v = ADAM_B2 * v + (1.0 - ADAM_B2) * (g * g)
    m_hat = m / (1.0 - ADAM_B1 ** ADAM_STEP)
    v_hat = v / (1.0 - ADAM_B2 ** ADAM_STEP)
    delta = -ADAM_LR * (m_hat / (jnp.sqrt(v_hat) + ADAM_EPS) + ADAM_WD * w)
    return delta, m, v


def _reduce_adamw(rbuf, w, m, v):
    rows = w.shape[0]
    tr = 256

    def body(r_ref, w_ref, m_ref, v_ref, g_ref, d_ref, nm_ref, nv_ref):
        g = r_ref[0].astype(F32)
        for s in range(1, N_DEV):
            g = g + r_ref[s].astype(F32)
        g_ref[...] = g
        d_ref[...], nm_ref[...], nv_ref[...] = _adamw(w_ref[...], g, m_ref[...], v_ref[...])

    row = pl.BlockSpec((tr, 1024), lambda i: (i, 0))
    return pl.pallas_call(
        body, name="reduce_adamw", grid=(rows // tr,),
        in_specs=[pl.BlockSpec((N_DEV, tr, 1024), lambda i: (0, i, 0)), row, row, row], out_specs=[row] * 4,
        out_shape=[jax.ShapeDtypeStruct((rows, 1024), F32)] * 4,
        compiler_params=_params(("parallel",)),
    )(rbuf, w, m, v)


def _reduce_adamw_small(sbuf, w, m, v):
    def body(s_ref, w_ref, m_ref, v_ref, g_ref, d_ref, nm_ref, nv_ref, loss_ref):
        tot = s_ref[0]
        for s in range(1, N_DEV):
            tot = tot + s_ref[s]
        tot = jnp.sum(tot, axis=0, keepdims=True)
        g = tot[:, :N_VEC]
        g_ref[...] = g
        d_ref[...], nm_ref[...], nv_ref[...] = _adamw(w_ref[...], g, m_ref[...], v_ref[...])
        loss_ref[...] = tot[:, N_VEC:]

    return pl.pallas_call(
        body, name="reduce_adamw_small",
        out_shape=[jax.ShapeDtypeStruct((1, N_VEC), F32)] * 4 + [jax.ShapeDtypeStruct((1, 128), F32)],
    )(sbuf, w, m, v)


def _rows(z):
    return z.reshape(-1, 1024)


def _pack_shard(d):
    parts = [_rows(d[n]) for n in MATS + LORAS]
    used = sum(p.shape[0] for p in parts)
    return jnp.concatenate(parts + [jnp.zeros((ROWS_PAD - used, 1024), parts[0].dtype)], axis=0)


_SHARD_SHAPES = {"w_in": (1024, 832), "w_out": (256, 1024), "w_gate": (1024, 704), "w_up": (1024, 704),
                 "w_down": (704, 1024), "decay_w2": (64, 128), "iclr_a2": (64, 128), "gate_g2": (128, 128)}
_ROW_SHARDED = ("w_out", "w_down")


def _unpack_shard(pack, names):
    out, r0 = {}, 0
    for n in names:
        shp = _SHARD_SHAPES[n]
        nr = shp[0] * shp[1] // 1024
        out[n] = pack[r0:r0 + nr].reshape(shp)
        r0 += nr
    return out


def _full_from_gathered(gathered, names):
    per_chip = [_unpack_shard(gathered[p], names) for p in range(N_CHIP)]
    return {n: jnp.concatenate([pc[n] for pc in per_chip], axis=0 if n in _ROW_SHARDED else 1) for n in names}


def _shard_of_full(full, n, p):
    shp = _SHARD_SHAPES[n]
    if n in _ROW_SHARDED:
        return full[p * shp[0]:(p + 1) * shp[0]]
    return full[:, p * shp[1]:(p + 1) * shp[1]]


def kernel(x, mix_norm_g, w_in, mu_shift, decay_w0, decay_w2, iclr_a0, iclr_a2, gate_g2, k_k, k_a, r_k, ln_x_w, ln_x_b, attn_out_g, w_out, ffn_norm_g, w_gate, w_up, w_down, final_norm_g, loss_target, m_mix_norm_g, m_w_in, m_mu_shift, m_decay_w0, m_decay_w2, m_iclr_a0, m_iclr_a2, m_gate_g2, m_k_k, m_k_a, m_r_k, m_ln_x_w, m_ln_x_b, m_attn_out_g, m_w_out, m_ffn_norm_g, m_w_gate, m_w_up, m_w_down, m_final_norm_g, v_mix_norm_g, v_w_in, v_mu_shift, v_decay_w0, v_decay_w2, v_iclr_a0, v_iclr_a2, v_gate_g2, v_k_k, v_k_a, v_r_k, v_ln_x_w, v_ln_x_b, v_attn_out_g, v_w_out, v_ffn_norm_g, v_w_gate, v_w_up, v_w_down, v_final_norm_g):
    names = ("mix_norm_g", "w_in", "mu_shift", "decay_w0", "decay_w2", "iclr_a0", "iclr_a2", "gate_g2", "k_k", "k_a",
             "r_k", "ln_x_w", "ln_x_b", "attn_out_g", "w_out", "ffn_norm_g", "w_gate", "w_up", "w_down", "final_norm_g")
    w = dict(zip(names, (mix_norm_g, w_in, mu_shift, decay_w0, decay_w2, iclr_a0, iclr_a2, gate_g2, k_k, k_a, r_k,
                         ln_x_w, ln_x_b, attn_out_g, w_out, ffn_norm_g, w_gate, w_up, w_down, final_norm_g)))
    m = dict(zip(names, (m_mix_norm_g, m_w_in, m_mu_shift, m_decay_w0, m_decay_w2, m_iclr_a0, m_iclr_a2, m_gate_g2,
                         m_k_k, m_k_a, m_r_k, m_ln_x_w, m_ln_x_b, m_attn_out_g, m_w_out, m_ffn_norm_g, m_w_gate,
                         m_w_up, m_w_down, m_final_norm_g)))
    v = dict(zip(names, (v_mix_norm_g, v_w_in, v_mu_shift, v_decay_w0, v_decay_w2, v_iclr_a0, v_iclr_a2, v_gate_g2,
                         v_k_k, v_k_a, v_r_k, v_ln_x_w, v_ln_x_b, v_attn_out_g, v_w_out, v_ffn_norm_g, v_w_gate,
                         v_w_up, v_w_down, v_final_norm_g)))
    big = MATS + LORAS

    wpack = jnp.concatenate([_rows(w[n][0]) for n in MATS], axis=0).astype(BF16)
    lpack = jnp.concatenate([_rows(w[n][0]) for n in LORAS], axis=0)
    wg_all, lg_all = _gather_weights(wpack, lpack)
    fullw = _full_from_gathered(wg_all, MATS)
    fulll = _full_from_gathered(lg_all, LORAS)

    vecs = {n: w[n].reshape(1, sz) for n, sz in VECS}
    loss8, dx, gw, gv = _local_step(x[0], loss_target[0], fullw["w_in"], fullw["w_out"], fullw["w_gate"], fullw["w_up"],
                                    fullw["w_down"], vecs, fulll["decay_w2"], fulll["iclr_a2"], fulll["gate_g2"])

    gpack = jnp.stack([_pack_shard({n: _shard_of_full(gw[n], n, p) for n in big}) for p in range(N_CHIP)]).astype(BF16)
    small = jnp.concatenate([gv[n] for n, _ in VECS] + [loss8], axis=1)
    rbuf, sbuf = _exchange_grads(gpack, small)

    g_p, d_p, m_p, v_p = _reduce_adamw(rbuf, _pack_shard({n: w[n][0] for n in big}), _pack_shard({n: m[n][0] for n in big}),
                                       _pack_shard({n: v[n][0] for n in big}))
    cat = lambda d: jnp.concatenate([d[n].reshape(1, sz) for n, sz in VECS], axis=1)
    g_s, d_s, m_s, v_s, loss = _reduce_adamw_small(sbuf, cat(w), cat(m), cat(v))

    outs = []
    for pack, small_pack in ((g_p, g_s), (d_p, d_s), (m_p, m_s), (v_p, v_s)):
        got = {n: z.reshape(w[n].shape) for n, z in _unpack_shard(pack, big).items()}
        c0 = 0
        for n, sz in VECS:
            got[n] = small_pack[0, c0:c0 + sz].reshape(w[n].shape)
            c0 += sz
        outs.extend(got[n] for n in names)
    return (loss[0, 0], dx[None], *outs)
```

```python
import jax
import jax.numpy as jnp
from jax import lax
from jax.experimental import pallas as pl
from jax.experimental.pallas import tpu as pltpu

F32 = jnp.float32
BF16 = jnp.bfloat16
HI = lax.Precision.HIGHEST

D_MODEL = 1024
HEAD_DIM = 64
RW = 512
N_PAIR = RW // 128
SHIFT_COLS = 1792
IN_COLS = 3328
D_FF = 2816
NORM_EPS = 1e-6
GN_EPS = 64e-5
CHUNK = 64
SUB = 16
WKV_PASSES = 3
ATTN_PASSES = 1
ATTN_BLOCK = 128
DILATIONS = (1, 4, 16)
NEG = -1e30
ADAM_LR, ADAM_B1, ADAM_B2, ADAM_EPS, ADAM_WD, ADAM_STEP = 0.001, 0.9, 0.999, 1e-08, 0.01, 10
VMEM_LIMIT = 56 * 1024 * 1024
MESH = pl.DeviceIdType.MESH


def _params(sem=None, **kw):
    return pltpu.CompilerParams(dimension_semantics=sem, vmem_limit_bytes=VMEM_LIMIT, **kw)


def _dot(a, b, prec=None):
    return lax.dot_general(a, b, (((1,), (0,)), ((), ())), preferred_element_type=F32, precision=prec)


def _dot_nt(a, b, prec=None):
    return lax.dot_general(a, b, (((1,), (1,)), ((), ())), preferred_element_type=F32, precision=prec)


def _dot_tn(a, b, prec=None):
    return lax.dot_general(a, b, (((0,), (0,)), ((), ())), preferred_element_type=F32, precision=prec)


_FORMS = {"nn": ((1,), (0,)), "nt": ((1,), (1,)), "tn": ((0,), (0,))}


def _dg(a, b, form):
    if a.ndim == 3 or b.ndim == 3:
        nb = a.shape[0] if a.ndim == 3 else b.shape[0]
        return jnp.stack([_dg(a[i] if a.ndim == 3 else a, b[i] if b.ndim == 3 else b, form) for i in range(nb)], axis=0)
    return lax.dot_general(a, b, (_FORMS[form], ((), ())), preferred_element_type=F32)


def _split2(x):
    hi = x.astype(BF16)
    return hi, (x - hi.astype(F32)).astype(BF16)


def _split3(x):
    hi = x.astype(BF16)
    rest = x - hi.astype(F32)
    mid = rest.astype(BF16)
    return hi, mid, (rest - mid.astype(F32)).astype(BF16)


def _mm_raw(a, b, form, mode):
    if mode == 1:
        return _dg(a.astype(BF16), b.astype(BF16), form)
    if mode == 3:
        ah, al = _split2(a)
        bh, bl = _split2(b)
        return _dg(ah, bh, form) + (_dg(ah, bl, form) + _dg(al, bh, form))
    if mode == "L3":
        ab = a.astype(BF16)
        b1, b2, b3 = _split3(b)
        return _dg(ab, b1, form) + (_dg(ab, b2, form) + _dg(ab, b3, form))
    assert mode == "R3", mode
    bb = b.astype(BF16)
    a1, a2, a3 = _split3(a)
    return _dg(a1, bb, form) + (_dg(a2, bb, form) + _dg(a3, bb, form))


def _mm(a, b, form, mode):
    @jax.custom_vjp
    def f(a, b):
        return _mm_raw(a, b, form, mode)

    def fwd(a, b):
        return _mm_raw(a, b, form, mode), (a, b)

    def bwd(res, ct):
        a, b = res
        la = {1: 1, 3: 3, "L3": None, "R3": "R3"}[mode]
        lb = {1: 1, 3: 3, "L3": "L3", "R3": None}[mode]
        if form == "nn":
            da = None if la is None else _mm_raw(ct, b, "nt", la)
            db = None if lb is None else _mm_raw(a, ct, "tn", lb)
        elif form == "nt":
            da = None if la is None else _mm_raw(ct, b, "nn", la)
            db = None if lb is None else _mm_raw(ct, a, "tn", "R3" if lb == "L3" else lb)
        else:
            da = None if la is None else _mm_raw(b, ct, "nt", "L3" if la == "R3" else la)
            db = None if lb is None else _mm_raw(a, ct, "nn", lb)
        return (jnp.zeros_like(a) if da is None else da, jnp.zeros_like(b) if db is None else db)

    f.defvjp(fwd, bwd)
    return f(a, b)


def _seg_ones(n):
    r = lax.broadcasted_iota(jnp.int32, (n, n), 0) // HEAD_DIM
    c = lax.broadcasted_iota(jnp.int32, (n, n), 1) // HEAD_DIM
    return (r == c).astype(F32)


def _segsum(x, seg):
    return _mm(x, seg, "nn", "R3")


def _rms_fwd(x, g):
    rstd = lax.rsqrt(jnp.mean(x * x, axis=-1, keepdims=True) + NORM_EPS)
    return x * rstd * g


def _rms_bwd(dy, x, g):
    rstd = lax.rsqrt(jnp.mean(x * x, axis=-1, keepdims=True) + NORM_EPS)
    xn = x * rstd
    dxn = dy * g
    dx = rstd * (dxn - xn * jnp.mean(dxn * xn, axis=-1, keepdims=True))
    return dx, dy * xn


def _sigmoid(x):
    return 1.0 / (1.0 + jnp.exp(-x))


def _softplus(x):
    return jnp.maximum(x, 0.0) + jnp.log(1.0 + jnp.exp(-jnp.abs(x)))


def _acc(ref, val, first):
    @pl.when(first)
    def _():
        ref[...] = val

    @pl.when(jnp.logical_not(first))
    def _():
        ref[...] += val


def _colsum8(v):
    rows, n = v.shape
    return jnp.sum(v.reshape(rows // 8, 8, n), axis=0)


def _prep_fn(p, pprev, mu, w0, w2p, a0, a2p, g2, k_k, k_a):
    seg = _seg_ones(RW)
    ps = p + (pprev - p) * mu
    r = ps[:, 0:RW]
    k = ps[:, RW:2 * RW]
    v = ps[:, 2 * RW:3 * RW]
    xwa = ps[:, 3 * RW:3 * RW + 128]
    xg = ps[:, 3 * RW + 128:3 * RW + 256]
    wraw = -_softplus(-(w0 + _mm(jnp.tanh(xwa), w2p, "nn", 3))) - 0.5
    lw = -jnp.exp(wraw)
    a = _sigmoid(a0 + _mm(xwa, a2p, "nn", 3))
    g = _mm(_sigmoid(xg), g2, "nn", 3)
    kk = k * k_k
    kk = kk / jnp.maximum(jnp.sqrt(_segsum(kk * kk, seg)), 1e-12)
    k2 = k * (1.0 + (a - 1.0) * k_a)
    return r, lw, k2, v, kk, a, g


def _solve_unit_lower(lmat, rhs):
    c = lmat.shape[-1]
    row = lax.broadcasted_iota(jnp.int32, (c, c), 0)
    col = lax.broadcasted_iota(jnp.int32, (c, c), 1)
    eye = (row == col).astype(F32)
    ld = jnp.where(row // SUB == col // SUB, lmat, 0.0)
    lo = lmat - ld
    x = eye + ld
    m = ld
    mm = lambda p, q: _mm(p, q, "nn", WKV_PASSES)
    for _ in range(3):
        m = mm(m, m)
        x = x + mm(x, m)
    g = mm(x, lo)
    g2 = mm(g, g)
    w = mm(x, rhs)
    w = w + mm(g2, w)
    return w + mm(g, w)


def _wkv_chunk_fn(s0, r, lw, k, v, kk, a):
    c = r.shape[-2]
    n = 2 * c
    row = lax.broadcasted_iota(jnp.int32, (n, n), 0)
    col = lax.broadcasted_iota(jnp.int32, (n, n), 1)
    same = (row // c) == (col // c)
    incl = jnp.logical_and(row >= col, same)
    strict = jnp.logical_and(row > col, same)
    sel = (lax.broadcasted_iota(jnp.int32, (n, 128), 0) // c) == (lax.broadcasted_iota(jnp.int32, (n, 128), 1) // HEAD_DIM)
    two = lambda z: jnp.concatenate([z, z], axis=-2)
    lw2 = two(lw)
    mm = lambda p_, q_, form: _mm(p_, q_, form, WKV_PASSES)
    cl = _mm(incl.astype(F32), lw2, "nn", "L3")
    p = jnp.exp(cl)
    pinv = jnp.exp(-cl)
    pprev = jnp.exp(cl - lw2)
    kk2 = two(kk)
    at = jnp.where(sel, -kk2 * pprev, 0.0)
    bt = jnp.where(sel, kk2 * two(a) * pinv, 0.0)
    kt = jnp.where(sel, two(k) * pinv, 0.0)
    rt = jnp.where(sel, two(r) * p, 0.0)
    vt = jnp.where(sel, two(v), 0.0)
    ab = jnp.where(strict, mm(at, bt, "nt"), 0.0)
    ak = jnp.where(strict, mm(at, kt, "nt"), 0.0)
    rb = jnp.where(incl, mm(rt, bt, "nt"), 0.0)
    rk = jnp.where(incl, mm(rt, kt, "nt"), 0.0)
    u = _solve_unit_lower(ab, mm(at, s0, "nt") + mm(ak, vt, "nn"))
    y2 = mm(rt, s0, "nt") + mm(rb, u, "nn") + mm(rk, vt, "nn")
    plast = jnp.exp(jnp.sum(lw, axis=-2, keepdims=True))
    s1 = (s0 + mm(u, bt, "tn") + mm(vt, kt, "tn")) * plast
    r2 = lax.broadcasted_iota(jnp.int32, (128, 128), 0) // HEAD_DIM
    c2 = lax.broadcasted_iota(jnp.int32, (128, 128), 1) // HEAD_DIM
    return y2[..., :c, :] + y2[..., c:, :], jnp.where(r2 == c2, s1, 0.0)


def _post_fn(y, r, k2, v, g, lnw, lnb, rk):
    seg = _seg_ones(RW)
    mean = _segsum(y, seg) * (1.0 / HEAD_DIM)
    yc = y - mean
    var = _segsum(yc * yc, seg) * (1.0 / HEAD_DIM)
    yn = yc * lax.rsqrt(var + GN_EPS)
    out = yn * lnw + lnb + _segsum(r * k2 * rk, seg) * v
    return out * g


def _attn_block_fn(q, kc, vc, kp=None, vp=None):
    n = ATTN_BLOCK
    qi = lax.broadcasted_iota(jnp.int32, (n, n), 0)
    kj = lax.broadcasted_iota(jnp.int32, (n, n), 1)
    lane = lax.broadcasted_iota(jnp.int32, (1, 128), 1)
    scale = HEAD_DIM ** -0.5
    os_, ls_ = [], []
    for h in range(2):
        mh = (lane // HEAD_DIM) == h
        qh = jnp.where(mh, q, 0.0)
        sc = jnp.where(kj <= qi, _mm(qh, kc, "nt", ATTN_PASSES) * scale, NEG)
        m = jnp.max(sc, axis=-1, keepdims=True)
        if kp is not None:
            sp = jnp.where(kj >= qi, _mm(qh, kp, "nt", ATTN_PASSES) * scale, NEG)
            m = jnp.maximum(m, jnp.max(sp, axis=-1, keepdims=True))
        pc = jnp.exp(sc - m)
        den = jnp.sum(pc, axis=-1, keepdims=True)
        num = _mm(pc, vc, "nn", ATTN_PASSES)
        if kp is not None:
            pp = jnp.exp(sp - m)
            den = den + jnp.sum(pp, axis=-1, keepdims=True)
            num = num + _mm(pp, vp, "nn", ATTN_PASSES)
        os_.append(num / den)
        ls_.append(m + jnp.log(den))
    m0 = (lane // HEAD_DIM) == 0
    return jnp.where(m0, os_[0], os_[1]), jnp.where(m0, ls_[0], ls_[1])


def _combine_fn(o1, o2, o3, l1, l2, l3, og):
    seg = _seg_ones(RW)
    m = jnp.maximum(jnp.maximum(l1, l2), l3)
    e1, e2, e3 = jnp.exp(l1 - m), jnp.exp(l2 - m), jnp.exp(l3 - m)
    o = (e1 * o1 + e2 * o2 + e3 * o3) / (e1 + e2 + e3)
    o = o * lax.rsqrt(_segsum(o * o, seg) * (1.0 / HEAD_DIM) + NORM_EPS)
    return o * og


def _in_proj(x, g1, win):
    t = x.shape[0]
    tm = 256

    def body(x_ref, g_ref, w_ref, h_ref, p_ref):
        h = _rms_fwd(x_ref[...], g_ref[...]).astype(BF16)
        h_ref[...] = h
        p_ref[...] = _dot(h, w_ref[...])

    return pl.pallas_call(
        body, name="in_proj", grid=(t // tm,),
        in_specs=[pl.BlockSpec((tm, D_MODEL), lambda i: (i, 0)), pl.BlockSpec((1, D_MODEL), lambda i: (0, 0)),
                  pl.BlockSpec((D_MODEL, IN_COLS), lambda i: (0, 0))],
        out_specs=[pl.BlockSpec((tm, D_MODEL), lambda i: (i, 0)), pl.BlockSpec((tm, IN_COLS), lambda i: (i, 0))],
        out_shape=[jax.ShapeDtypeStruct((t, D_MODEL), BF16), jax.ShapeDtypeStruct((t, IN_COLS), F32)],
        compiler_params=_params(("parallel",)),
    )(x, g1, win)


def _shifted(p, last8, first):
    prow = jnp.where(first, 0.0, last8[7:8, :])
    rolled = pltpu.roll(p, 1, axis=0)
    rid = lax.broadcasted_iota(jnp.int32, p.shape, 0)
    return jnp.where(rid == 0, prow, rolled)


_PREP_TM = 256


def _prep_specs(tm):
    vec = lambda n: pl.BlockSpec((1, n), lambda i: (0, 0))
    mat = lambda r, n: pl.BlockSpec((r, n), lambda i: (0, 0))
    return [vec(SHIFT_COLS), vec(RW), mat(128, RW), vec(RW), mat(128, RW), mat(128, RW), vec(RW), vec(RW)]


def _prep_fwd(proj, pw):
    t = proj.shape[0]
    tm = _PREP_TM

    def body(p_ref, l8_ref, mu, w0, w2p, a0, a2p, g2, k_k, k_a, *outs):
        p = p_ref[...]
        pprev = _shifted(p, l8_ref[...], pl.program_id(0) == 0)
        res = _prep_fn(p, pprev, mu[...], w0[...], w2p[...], a0[...], a2p[...], g2[...], k_k[...], k_a[...])
        for o_ref, val in zip(outs, res):
            o_ref[...] = val

    row = pl.BlockSpec((tm, RW), lambda i: (i, 0))
    return pl.pallas_call(
        body, name="rwkv_prep", grid=(t // tm,),
        in_specs=[pl.BlockSpec((tm, SHIFT_COLS), lambda i: (i, 0)),
                  pl.BlockSpec((8, SHIFT_COLS), lambda i: (jnp.maximum(i * (tm // 8) - 1, 0), 0))] + _prep_specs(tm),
        out_specs=[row] * 7,
        out_shape=[jax.ShapeDtypeStruct((t, RW), F32)] * 7,
        compiler_params=_params(("parallel",)),
    )(proj, proj, *pw)


def _pairs(ref):
    return jnp.stack([ref[:, 128 * p:128 * (p + 1)] for p in range(N_PAIR)], axis=0)


def _wkv_fwd(r, lw, k2, v, kk, a):
    t = r.shape[0]
    nc = t // CHUNK

    def body(r_ref, lw_ref, k_ref, v_ref, kk_ref, a_ref, y_ref, s_ref, st):
        @pl.when(pl.program_id(0) == 0)
        def _():
            st[...] = jnp.zeros_like(st)

        s0 = st[...]
        s_ref[0] = s0
        y, s1 = _wkv_chunk_fn(s0, *[_pairs(ref) for ref in (r_ref, lw_ref, k_ref, v_ref, kk_ref, a_ref)])
        for p in range(N_PAIR):
            y_ref[:, 128 * p:128 * (p + 1)] = y[p]
        st[...] = s1

    blk = pl.BlockSpec((CHUNK, RW), lambda c: (c, 0))
    return pl.pallas_call(
        body, name="wkv_fwd", grid=(nc,),
        in_specs=[blk] * 6,
        out_specs=[blk, pl.BlockSpec((1, N_PAIR, 128, 128), lambda c: (c, 0, 0, 0))],
        out_shape=[jax.ShapeDtypeStruct((t, RW), F32), jax.ShapeDtypeStruct((nc, N_PAIR, 128, 128), F32)],
        scratch_shapes=[pltpu.VMEM((N_PAIR, 128, 128), F32)],
        compiler_params=_params(("arbitrary",)),
    )(r, lw, k2, v, kk, a)


_POST_TM = 256


def _post_fwd(y, r, k2, v, g, lnw, lnb, rk):
    t = y.shape[0]
    tm = _POST_TM

    def body(y_ref, r_ref, k_ref, v_ref, g_ref, lnw_ref, lnb_ref, rk_ref, o_ref):
        o_ref[...] = _post_fn(y_ref[...], r_ref[...], k_ref[...], v_ref[...], g_ref[...],
                              lnw_ref[...], lnb_ref[...], rk_ref[...]).astype(BF16)

    row = pl.BlockSpec((tm, RW), lambda i: (i, 0))
    vec = pl.BlockSpec((1, RW), lambda i: (0, 0))
    return pl.pallas_call(
        body, name="rwkv_post", grid=(t // tm,),
        in_specs=[row] * 5 + [vec] * 3, out_specs=row,
        out_shape=jax.ShapeDtypeStruct((t, RW), BF16),
        compiler_params=_params(("parallel",)),
    )(y, r, k2, v, g, lnw, lnb, rk)


def _attn_fwd(qa, ka, va, qo, ko, vo, ncol, name):
    length = qa.shape[0]
    nb = length // ATTN_BLOCK

    def body(q_ref, k_ref, v_ref, o_ref, l_ref):
        first = pl.ds(0, ATTN_BLOCK)
        o_ref[first, :], l_ref[first, :] = _attn_block_fn(q_ref[first, :], k_ref[first, :], v_ref[first, :])

        def blk(n, carry):
            cur = pl.ds(pl.multiple_of(n * ATTN_BLOCK, ATTN_BLOCK), ATTN_BLOCK)
            prv = pl.ds(pl.multiple_of((n - 1) * ATTN_BLOCK, ATTN_BLOCK), ATTN_BLOCK)
            o, lse = _attn_block_fn(q_ref[cur, :], k_ref[cur, :], v_ref[cur, :], k_ref[prv, :], v_ref[prv, :])
            o_ref[cur, :] = o
            l_ref[cur, :] = lse
            return carry

        if nb > 1:
            lax.fori_loop(1, nb, blk, 0)

    spec = lambda off: pl.BlockSpec((length, 128), lambda c: (0, off + c))
    return pl.pallas_call(
        body, name=name, grid=(ncol,),
        in_specs=[spec(qo), spec(ko), spec(vo)], out_specs=[spec(0), spec(0)],
        out_shape=[jax.ShapeDtypeStruct((length, ncol * 128), F32)] * 2,
        compiler_params=_params(("parallel",)),
    )(qa, ka, va)


_COMB_TM = 256


def _combine_fwd(os_, ls_, og):
    t = os_[0].shape[0]
    tm = _COMB_TM

    def body(o1, o2, o3, l1, l2, l3, og_ref, y_ref):
        y_ref[...] = _combine_fn(o1[...], o2[...], o3[...], l1[...], l2[...], l3[...], og_ref[...]).astype(BF16)

    row = pl.BlockSpec((tm, RW), lambda i: (i, 0))
    return pl.pallas_call(
        body, name="attn_combine", grid=(t // tm,),
        in_specs=[row] * 6 + [pl.BlockSpec((1, RW), lambda i: (0, 0))], out_specs=row,
        out_shape=jax.ShapeDtypeStruct((t, RW), BF16),
        compiler_params=_params(("parallel",)),
    )(*os_, *ls_, og)


def _out_proj(x, ycat, wout, g2):
    t = x.shape[0]
    tm = 256

    def body(x_ref, y_ref, w_ref, g_ref, x1_ref, h_ref):
        x1 = x_ref[...] + _dot(y_ref[...], w_ref[...])
        x1_ref[...] = x1
        h_ref[...] = _rms_fwd(x1, g_ref[...]).astype(BF16)

    row = pl.BlockSpec((tm, D_MODEL), lambda i: (i, 0))
    return pl.pallas_call(
        body, name="out_proj", grid=(t // tm,),
        in_specs=[row, row, pl.BlockSpec((D_MODEL, D_MODEL), lambda i: (0, 0)), pl.BlockSpec((1, D_MODEL), lambda i: (0, 0))],
        out_specs=[row, row],
        out_shape=[jax.ShapeDtypeStruct((t, D_MODEL), F32), jax.ShapeDtypeStruct((t, D_MODEL), BF16)],
        compiler_params=_params(("parallel",)),
    )(x, ycat, wout, g2)


def _ffn_up(h2, wg, wu):
    t = h2.shape[0]
    tm = 256

    def body(h_ref, wg_ref, wu_ref, gt_ref, up_ref, act_ref):
        h = h_ref[...]
        gt = _dot(h, wg_ref[...])
        up = _dot(h, wu_ref[...])
        gt_ref[...] = gt
        up_ref[...] = up
        act_ref[...] = (gt * _sigmoid(gt) * up).astype(BF16)

    wide = pl.BlockSpec((tm, D_FF), lambda i: (i, 0))
    wsp = pl.BlockSpec((D_MODEL, D_FF), lambda i: (0, 0))
    return pl.pallas_call(
        body, name="ffn_up", grid=(t // tm,),
        in_specs=[pl.BlockSpec((tm, D_MODEL), lambda i: (i, 0)), wsp, wsp],
        out_specs=[wide, wide, wide],
        out_shape=[jax.ShapeDtypeStruct((t, D_FF), F32)] * 2 + [jax.ShapeDtypeStruct((t, D_FF), BF16)],
        compiler_params=_params(("parallel",)),
    )(h2, wg, wu)


def _ffn_down_loss(x1, act, wd, gf, tgt):
    t = x1.shape[0]
    tm = 256

    def body(x1_ref, a_ref, w_ref, g_ref, t_ref, dx_ref, dxb_ref, loss_ref, dg_ref):
        first = pl.program_id(0) == 0
        x2 = x1_ref[...] + _dot(a_ref[...], w_ref[...])
        g = g_ref[...]
        diff = _rms_fwd(x2, g) - t_ref[...]
        lrow = 0.5 * jnp.sum(_colsum8(diff * diff), axis=1, keepdims=True) * (1.0 / D_MODEL)
        _acc(loss_ref, jnp.broadcast_to(lrow, (8, 128)), first)
        dx2, dgr = _rms_bwd(diff * (1.0 / D_MODEL), x2, g)
        dx_ref[...] = dx2
        dxb_ref[...] = dx2.astype(BF16)
        _acc(dg_ref, _colsum8(dgr), first)

    row = pl.BlockSpec((tm, D_MODEL), lambda i: (i, 0))
    return pl.pallas_call(
        body, name="ffn_down_loss", grid=(t // tm,),
        in_specs=[row, pl.BlockSpec((tm, D_FF), lambda i: (i, 0)), pl.BlockSpec((D_FF, D_MODEL), lambda i: (0, 0)),
                  pl.BlockSpec((1, D_MODEL), lambda i: (0, 0)), row],
        out_specs=[row, row, pl.BlockSpec((8, 128), lambda i: (0, 0)), pl.BlockSpec((8, D_MODEL), lambda i: (0, 0))],
        out_shape=[jax.ShapeDtypeStruct((t, D_MODEL), F32), jax.ShapeDtypeStruct((t, D_MODEL), BF16),
                   jax.ShapeDtypeStruct((8, 128), F32), jax.ShapeDtypeStruct((8, D_MODEL), F32)],
        compiler_params=_params(("arbitrary",)),
    )(x1, act, wd, gf, tgt)


def _ffn_bwd_act(dx2b, wd, gt, up):
    t = dx2b.shape[0]
    tm = 256

    def body(dx_ref, w_ref, gt_ref, up_ref, dgt_ref, dup_ref):
        dact = _dot_nt(dx_ref[...], w_ref[...])
        gt = gt_ref[...]
        sg = _sigmoid(gt)
        dgt_ref[...] = (dact * up_ref[...] * sg * (1.0 + gt * (1.0 - sg))).astype(BF16)
        dup_ref[...] = (dact * gt * sg).astype(BF16)

    wide = pl.BlockSpec((tm, D_FF), lambda i: (i, 0))
    return pl.pallas_call(
        body, name="ffn_bwd_act", grid=(t // tm,),
        in_specs=[pl.BlockSpec((tm, D_MODEL), lambda i: (i, 0)), pl.BlockSpec((D_FF, D_MODEL), lambda i: (0, 0)), wide, wide],
        out_specs=[wide, wide],
        out_shape=[jax.ShapeDtypeStruct((t, D_FF), BF16)] * 2,
        compiler_params=_params(("parallel",)),
    )(dx2b, wd, gt, up)


def _ffn_bwd_h(dgt, dup, wg, wu, dx2, x1, g2, wout):
    t = dgt.shape[0]
    tm = 256

    def body(dgt_ref, dup_ref, wg_ref, wu_ref, dx2_ref, x1_ref, g_ref, wo_ref, dx1_ref, dx1b_ref, dya_ref, dyb_ref, dg_ref):
        dh = _dot_nt(dgt_ref[...], wg_ref[...]) + _dot_nt(dup_ref[...], wu_ref[...])
        dxn, dgr = _rms_bwd(dh, x1_ref[...], g_ref[...])
        dx1 = dx2_ref[...] + dxn
        dx1_ref[...] = dx1
        dx1b = dx1.astype(BF16)
        dx1b_ref[...] = dx1b
        dy = _dot_nt(dx1b, wo_ref[...])
        dya_ref[...] = dy[:, :RW]
        dyb_ref[...] = dy[:, RW:]
        _acc(dg_ref, _colsum8(dgr), pl.program_id(0) == 0)

    wide = pl.BlockSpec((tm, D_FF), lambda i: (i, 0))
    row = pl.BlockSpec((tm, D_MODEL), lambda i: (i, 0))
    half = pl.BlockSpec((tm, RW), lambda i: (i, 0))
    wsp = pl.BlockSpec((D_MODEL, D_FF), lambda i: (0, 0))
    return pl.pallas_call(
        body, name="ffn_bwd_h", grid=(t // tm,),
        in_specs=[wide, wide, wsp, wsp, row, row, pl.BlockSpec((1, D_MODEL), lambda i: (0, 0)),
                  pl.BlockSpec((D_MODEL, D_MODEL), lambda i: (0, 0))],
        out_specs=[row, row, half, half, pl.BlockSpec((8, D_MODEL), lambda i: (0, 0))],
        out_shape=[jax.ShapeDtypeStruct((t, D_MODEL), F32), jax.ShapeDtypeStruct((t, D_MODEL), BF16),
                   jax.ShapeDtypeStruct((t, RW), F32), jax.ShapeDtypeStruct((t, RW), F32),
                   jax.ShapeDtypeStruct((8, D_MODEL), F32)],
        compiler_params=_params(("arbitrary",)),
    )(dgt, dup, wg, wu, dx2, x1, g2, wout)


def _wgrad(a, b, tk, tn, name):
    t, kdim = a.shape
    ndim = b.shape[1]

    def body(a_ref, b_ref, o_ref):
        o_ref[...] = _dot_tn(a_ref[...], b_ref[...])

    return pl.pallas_call(
        body, name=name, grid=(kdim // tk, ndim // tn),
        in_specs=[pl.BlockSpec((t, tk), lambda i, j: (0, i)), pl.BlockSpec((t, tn), lambda i, j: (0, j))],
        out_specs=pl.BlockSpec((tk, tn), lambda i, j: (i, j)),
        out_shape=jax.ShapeDtypeStruct((kdim, ndim), F32),
        compiler_params=_params(("parallel", "parallel")),
    )(a, b)


def _post_bwd(dya, y, r, k2, v, g, lnw, lnb, rk):
    t = y.shape[0]
    tm = _POST_TM

    def body(d_ref, y_ref, r_ref, k_ref, v_ref, g_ref, lnw_ref, lnb_ref, rk_ref,
             dy_ref, dr_ref, dk_ref, dv_ref, dg_ref, dlnw_ref, dlnb_ref, drk_ref):
        first = pl.program_id(0) == 0
        ones = jnp.ones((tm, 1), F32)
        prim = (y_ref[...], r_ref[...], k_ref[...], v_ref[...], g_ref[...],
                ones * lnw_ref[...], ones * lnb_ref[...], ones * rk_ref[...])
        _, vjp = jax.vjp(_post_fn, *prim)
        dy, dr, dk, dv, dg, dlnw, dlnb, drk = vjp(d_ref[...])
        dy_ref[...] = dy
        dr_ref[...] = dr
        dk_ref[...] = dk
        dv_ref[...] = dv
        dg_ref[...] = dg
        _acc(dlnw_ref, _colsum8(dlnw), first)
        _acc(dlnb_ref, _colsum8(dlnb), first)
        _acc(drk_ref, _colsum8(drk), first)

    row = pl.BlockSpec((tm, RW), lambda i: (i, 0))
    vec = pl.BlockSpec((1, RW), lambda i: (0, 0))
    part = pl.BlockSpec((8, RW), lambda i: (0, 0))
    return pl.pallas_call(
        body, name="rwkv_post_bwd", grid=(t // tm,),
        in_specs=[row] * 6 + [vec] * 3, out_specs=[row] * 5 + [part] * 3,
        out_shape=[jax.ShapeDtypeStruct((t, RW), F32)] * 5 + [jax.ShapeDtypeStruct((8, RW), F32)] * 3,
        compiler_params=_params(("arbitrary",)),
    )(dya, y, r, k2, v, g, lnw, lnb, rk)


def _wkv_bwd(dy, s0s, r, lw, k2, v, kk, a):
    t = r.shape[0]
    nc = t // CHUNK

    def body(dy_ref, s_ref, r_ref, lw_ref, k_ref, v_ref, kk_ref, a_ref,
             dr_ref, dlw_ref, dk_ref, dv_ref, dkk_ref, da_ref, ds):
        @pl.when(pl.program_id(0) == 0)
        def _():
            ds[...] = jnp.zeros_like(ds)

        _, vjp = jax.vjp(_wkv_chunk_fn, s_ref[0],
                         *[_pairs(ref) for ref in (r_ref, lw_ref, k_ref, v_ref, kk_ref, a_ref)])
        res = vjp((_pairs(dy_ref), ds[...]))
        ds[...] = res[0]
        for ref, val in zip((dr_ref, dlw_ref, dk_ref, dv_ref, dkk_ref, da_ref), res[1:]):
            for p in range(N_PAIR):
                ref[:, 128 * p:128 * (p + 1)] = val[p]

    blk = pl.BlockSpec((CHUNK, RW), lambda c: (nc - 1 - c, 0))
    return pl.pallas_call(
        body, name="wkv_bwd", grid=(nc,),
        in_specs=[blk, pl.BlockSpec((1, N_PAIR, 128, 128), lambda c: (nc - 1 - c, 0, 0, 0))] + [blk] * 6,
        out_specs=[blk] * 6,
        out_shape=[jax.ShapeDtypeStruct((t, RW), F32)] * 6,
        scratch_shapes=[pltpu.VMEM((N_PAIR, 128, 128), F32)],
        compiler_params=_params(("arbitrary",)),
    )(dy, s0s, r, lw, k2, v, kk, a)


def _prep_bwd(proj, pw, douts):
    t = proj.shape[0]
    tm = _PREP_TM
    nt = t // tm

    def body(p_ref, l8_ref, mu, w0, w2p, a0, a2p, g2, k_k, k_a, dr, dr2, dlw, dk2, dk22, dv, dv2, dkk, da, dg,
             dp_ref, dmu_ref, dw0_ref, dw2_ref, da0_ref, da2_ref, dg2_ref, dkk_ref, dka_ref, carry):
        i = pl.program_id(0)
        first = i == 0

        @pl.when(first)
        def _():
            carry[...] = jnp.zeros_like(carry)

        p = p_ref[...]
        pprev = _shifted(p, l8_ref[...], i == nt - 1)
        ones = jnp.ones((tm, 1), F32)
        prim = (p, pprev, ones * mu[...], ones * w0[...], w2p[...], ones * a0[...], a2p[...], g2[...],
                ones * k_k[...], ones * k_a[...])
        _, vjp = jax.vjp(_prep_fn, *prim)
        dp, dpp, dmu, dw0, dw2, da0, da2, dg2, dkk_, dka = vjp(
            (dr[...] + dr2[...], dlw[...], dk2[...] + dk22[...], dv[...] + dv2[...], dkk[...], da[...], dg[...]))
        up = pltpu.roll(dpp, tm - 1, axis=0)
        rid = lax.broadcasted_iota(jnp.int32, dpp.shape, 0)
        dp_ref[...] = dp + jnp.where(rid == tm - 1, carry[0:1, :], up)
        carry[...] = jnp.broadcast_to(dpp[0:1, :], carry.shape)
        _acc(dmu_ref, _colsum8(dmu), first)
        _acc(dw0_ref, _colsum8(dw0), first)
        _acc(dw2_ref, dw2, first)
        _acc(da0_ref, _colsum8(da0), first)
        _acc(da2_ref, da2, first)
        _acc(dg2_ref, dg2, first)
        _acc(dkk_ref, _colsum8(dkk_), first)
        _acc(dka_ref, _colsum8(dka), first)

    rev = lambda i: (nt - 1 - i, 0)
    row = pl.BlockSpec((tm, RW), rev)
    part = lambda n: pl.BlockSpec((8, n), lambda i: (0, 0))
    mat = pl.BlockSpec((128, RW), lambda i: (0, 0))
    return pl.pallas_call(
        body, name="rwkv_prep_bwd", grid=(nt,),
        in_specs=[pl.BlockSpec((tm, SHIFT_COLS), rev),
                  pl.BlockSpec((8, SHIFT_COLS), lambda i: (jnp.maximum((nt - 1 - i) * (tm // 8) - 1, 0), 0))]
                 + _prep_specs(tm) + [row] * 10,
        out_specs=[pl.BlockSpec((tm, SHIFT_COLS), rev), part(SHIFT_COLS), part(RW), mat, part(RW), mat, mat,
                   part(RW), part(RW)],
        out_shape=[jax.ShapeDtypeStruct((t, SHIFT_COLS), F32), jax.ShapeDtypeStruct((8, SHIFT_COLS), F32),
                   jax.ShapeDtypeStruct((8, RW), F32), jax.ShapeDtypeStruct((128, RW), F32),
                   jax.ShapeDtypeStruct((8, RW), F32), jax.ShapeDtypeStruct((128, RW), F32),
                   jax.ShapeDtypeStruct((128, RW), F32), jax.ShapeDtypeStruct((8, RW), F32),
                   jax.ShapeDtypeStruct((8, RW), F32)],
        scratch_shapes=[pltpu.VMEM((8, SHIFT_COLS), F32)],
        compiler_params=_params(("arbitrary",)),
    )(proj, proj, *pw, *douts)


def _combine_bwd(dyb, os_, ls_, og):
    t = dyb.shape[0]
    tm = _COMB_TM

    def body(d_ref, o1, o2, o3, l1, l2, l3, og_ref, do1, do2, do3, dl1, dl2, dl3, dog_ref):
        ones = jnp.ones((tm, 1), F32)
        _, vjp = jax.vjp(_combine_fn, o1[...], o2[...], o3[...], l1[...], l2[...], l3[...], ones * og_ref[...])
        res = vjp(d_ref[...])
        for ref, val in zip((do1, do2, do3, dl1, dl2, dl3), res[:6]):
            ref[...] = val
        _acc(dog_ref, _colsum8(res[6]), pl.program_id(0) == 0)

    row = pl.BlockSpec((tm, RW), lambda i: (i, 0))
    return pl.pallas_call(
        body, name="attn_combine_bwd", grid=(t // tm,),
        in_specs=[row] * 7 + [pl.BlockSpec((1, RW), lambda i: (0, 0))],
        out_specs=[row] * 6 + [pl.BlockSpec((8, RW), lambda i: (0, 0))],
        out_shape=[jax.ShapeDtypeStruct((t, RW), F32)] * 6 + [jax.ShapeDtypeStruct((8, RW), F32)],
        compiler_params=_params(("arbitrary",)),
    )(dyb, *os_, *ls_, og)


def _attn_bwd(do, dl, qa, ka, va, qo, ko, vo, ncol, name):
    length = qa.shape[0]
    nb = length // ATTN_BLOCK

    def body(do_ref, dl_ref, q_ref, k_ref, v_ref, dq_ref, dk_ref, dv_ref):
        dk_ref[...] = jnp.zeros_like(dk_ref)
        dv_ref[...] = jnp.zeros_like(dv_ref)

        first = pl.ds(0, ATTN_BLOCK)
        _, vjp0 = jax.vjp(_attn_block_fn, q_ref[first, :], k_ref[first, :], v_ref[first, :])
        dq0, dk0, dv0 = vjp0((do_ref[first, :], dl_ref[first, :]))
        dq_ref[first, :] = dq0
        dk_ref[first, :] += dk0
        dv_ref[first, :] += dv0

        def blk(n, carry):
            cur = pl.ds(pl.multiple_of(n * ATTN_BLOCK, ATTN_BLOCK), ATTN_BLOCK)
            prv = pl.ds(pl.multiple_of((n - 1) * ATTN_BLOCK, ATTN_BLOCK), ATTN_BLOCK)
            _, vjp = jax.vjp(_attn_block_fn, q_ref[cur, :], k_ref[cur, :], v_ref[cur, :], k_ref[prv, :], v_ref[prv, :])
            dq, dkc, dvc, dkp, dvp = vjp((do_ref[cur, :], dl_ref[cur, :]))
            dq_ref[cur, :] = dq
            dk_ref[cur, :] += dkc
            dv_ref[cur, :] += dvc
            dk_ref[prv, :] += dkp
            dv_ref[prv, :] += dvp
            return carry

        if nb > 1:
            lax.fori_loop(1, nb, blk, 0)

    spec = lambda off: pl.BlockSpec((length, 128), lambda c: (0, off + c))
    return pl.pallas_call(
        body, name=name, grid=(ncol,),
        in_specs=[spec(0), spec(0), spec(qo), spec(ko), spec(vo)], out_specs=[spec(0)] * 3,
        out_shape=[jax.ShapeDtypeStruct((length, ncol * 128), F32)] * 3,
        compiler_params=_params(("parallel",)),
    )(do, dl, qa, ka, va)


def _in_proj_bwd(dpa, dqs, dks, dvs, win, x, g1, dx1):
    t = x.shape[0]
    tm = 256

    def body(dpa_ref, q1, q2, q3, k1, k2, k3, v1, v2, v3, w_ref, x_ref, g_ref, dx1_ref, dproj_ref, dx_ref, dg_ref):
        parts = (dpa_ref[...], q1[...] + q2[...] + q3[...], k1[...] + k2[...] + k3[...], v1[...] + v2[...] + v3[...])
        dproj = jnp.concatenate([z.astype(BF16) for z in parts], axis=1)
        dproj_ref[...] = dproj
        dh = _dot_nt(dproj, w_ref[...])
        dxn, dgr = _rms_bwd(dh, x_ref[...], g_ref[...])
        dx_ref[...] = dx1_ref[...] + dxn
        _acc(dg_ref, _colsum8(dgr), pl.program_id(0) == 0)

    row = pl.BlockSpec((tm, D_MODEL), lambda i: (i, 0))
    half = pl.BlockSpec((tm, RW), lambda i: (i, 0))
    return pl.pallas_call(
        body, name="in_proj_bwd", grid=(t // tm,),
        in_specs=[pl.BlockSpec((tm, SHIFT_COLS), lambda i: (i, 0))] + [half] * 9
                 + [pl.BlockSpec((D_MODEL, IN_COLS), lambda i: (0, 0)), row, pl.BlockSpec((1, D_MODEL), lambda i: (0, 0)), row],
        out_specs=[pl.BlockSpec((tm, IN_COLS), lambda i: (i, 0)), row, pl.BlockSpec((8, D_MODEL), lambda i: (0, 0))],
        out_shape=[jax.ShapeDtypeStruct((t, IN_COLS), BF16), jax.ShapeDtypeStruct((t, D_MODEL), F32),
                   jax.ShapeDtypeStruct((8, D_MODEL), F32)],
        compiler_params=_params(("arbitrary",)),
    )(dpa, *dqs, *dks, *dvs, win, x, g1, dx1)


def _pad_lora(w, lo):
    z = jnp.zeros((64, RW), F32)
    return jnp.concatenate([w, z], axis=0) if lo == 0 else jnp.concatenate([z, w], axis=0)


def _dilate(z, d):
    return z if d == 1 else z.reshape(z.shape[0] // d, z.shape[1] * d)


def _local_step(x, tgt, win, vecs, w2, a2, g2m, get_rest, send_rest):
    t = x.shape[0]
    pw = (vecs["mu_shift"], vecs["decay_w0"], _pad_lora(w2, 0), vecs["iclr_a0"], _pad_lora(a2, 64), g2m,
          vecs["k_k"], vecs["k_a"])
    h, proj = _in_proj(x, vecs["mix_norm_g"], win)
    r, lw, k2, v, kk, a, g = _prep_fwd(proj, pw)
    y, s0s = _wkv_fwd(r, lw, k2, v, kk, a)
    ya = _post_fwd(y, r, k2, v, g, vecs["ln_x_w"], vecs["ln_x_b"], vecs["r_k"])

    qoff = SHIFT_COLS // 128
    qkv = {}
    os_, ls_ = [], []
    for d in DILATIONS:
        if d == 1:
            arrs, offs, ncol = (proj, proj, proj), (qoff, qoff + N_PAIR, qoff + 2 * N_PAIR), N_PAIR
        else:
            arrs = tuple(_dilate(proj[:, SHIFT_COLS + j * RW:SHIFT_COLS + (j + 1) * RW], d) for j in range(3))
            offs, ncol = (0, 0, 0), N_PAIR * d
        qkv[d] = (arrs, offs, ncol)
        o, l = _attn_fwd(*arrs, *offs, ncol, "attn_fwd_d%d" % d)
        os_.append(o.reshape(t, RW))
        ls_.append(l.reshape(t, RW))
    yb = _combine_fwd(os_, ls_, vecs["attn_out_g"])

    wout, wg, wu, wd = get_rest(yb)
    ycat = jnp.concatenate([ya, yb], axis=1)
    x1, h2 = _out_proj(x, ycat, wout, vecs["ffn_norm_g"])
    gt, up, act = _ffn_up(h2, wg, wu)
    dx2, dx2b, loss8, dgf = _ffn_down_loss(x1, act, wd, vecs["final_norm_g"], tgt)

    dgt, dup = _ffn_bwd_act(dx2b, wd, gt, up)
    dx1, dx1b, dya, dyb, dg2n = _ffn_bwd_h(dgt, dup, wg, wu, dx2, x1, vecs["ffn_norm_g"], wout)
    gw = {
        "w_down": _wgrad(act, dx2b, 1408, 1024, "wgrad_down"),
        "w_gate": _wgrad(h2, dgt, 1024, 1408, "wgrad_gate"),
        "w_up": _wgrad(h2, dup, 1024, 1408, "wgrad_up"),
        "w_out": _wgrad(ycat, dx1b, 1024, 1024, "wgrad_out"),
    }

    lnw = vecs["ln_x_w"] + send_rest(gw)[0, 0]
    dy, dr_p, dk2_p, dv_p, dg, dlnw, dlnb, drk = _post_bwd(dya, y, r, k2, v, g, lnw, vecs["ln_x_b"], vecs["r_k"])
    dr_s, dlw, dk2_s, dv_s, dkk, da = _wkv_bwd(dy, s0s, r, lw, k2, v, kk, a)
    dpa, dmu, dw0, dw2p, da0, da2p, dg2m, dk_k, dk_a = _prep_bwd(
        proj, pw, (dr_p, dr_s, dlw, dk2_p, dk2_s, dv_p, dv_s, dkk, da, dg))

    cb = _combine_bwd(dyb, os_, ls_, vecs["attn_out_g"])
    dqs, dks, dvs = [], [], []
    for j, d in enumerate(DILATIONS):
        arrs, offs, ncol = qkv[d]
        dq, dk, dv = _attn_bwd(_dilate(cb[j], d), _dilate(cb[3 + j], d), *arrs, *offs, ncol, "attn_bwd_d%d" % d)
        dqs.append(dq.reshape(t, RW))
        dks.append(dk.reshape(t, RW))
        dvs.append(dv.reshape(t, RW))
    dproj, dx, dg1 = _in_proj_bwd(dpa, dqs, dks, dvs, win, x, vecs["mix_norm_g"], dx1)
    gw["w_in"] = _wgrad(h, dproj, 1024, 1664, "wgrad_in")
    gw["decay_w2"] = dw2p[:64]
    gw["iclr_a2"] = da2p[64:]
    gw["gate_g2"] = dg2m
    gv = {"mix_norm_g": dg1, "mu_shift": dmu, "decay_w0": dw0, "iclr_a0": da0, "k_k": dk_k, "k_a": dk_a, "r_k": drk,
          "ln_x_w": dlnw, "ln_x_b": dlnb, "attn_out_g": cb[6], "ffn_norm_g": dg2n, "final_norm_g": dgf}
    return loss8, dx, gw, gv


N_CHIP = 4
N_DEV = 8
MATS = ("w_in", "w_out", "w_gate", "w_up", "w_down")
LORAS = ("decay_w2", "iclr_a2", "gate_g2")
VECS = (("mix_norm_g", 1024), ("mu_shift", 1792), ("decay_w0", 512), ("iclr_a0", 512), ("k_k", 512), ("k_a", 512),
        ("r_k", 512), ("ln_x_w", 512), ("ln_x_b", 512), ("attn_out_g", 512), ("ffn_norm_g", 1024),
        ("final_norm_g", 1024))
N_VEC = sum(n for _, n in VECS)
N_SMALL = N_VEC + 128
ROWS_PAD = 3328
ANY = pl.BlockSpec(memory_space=pl.ANY)


def _flip(v, f):
    return 1 - v if f else v


def _plan(mode):
    x, y, c = lax.axis_index("x"), lax.axis_index("y"), lax.axis_index("c")
    if mode == "chips":
        return 2 * x + y, [((px, py, c), 2 * px + py, 2 * x + y, 2 * px + py) for px, py in ((1 - x, y), (x, 1 - y), (1 - x, 1 - y))]
    peers = [(_flip(x, k & 4), _flip(y, k & 2), _flip(c, k & 1)) for k in range(1, N_DEV)]
    return 2 * x + y, [((px, py, pc), 2 * px + py, 4 * x + 2 * y + c, 4 * px + 2 * py + pc) for px, py, pc in peers]


def _peer_copy(srcs, dsts, sliced, send_sems, recv_sems, n, peer, j, i, incoming):
    dev, chip, out_slot, in_slot = peer
    src = srcs[i].at[chip] if sliced[i] else srcs[i]
    return pltpu.make_async_remote_copy(src_ref=src, dst_ref=dsts[i].at[in_slot if incoming else out_slot],
                                        send_sem=send_sems.at[n * j + i], recv_sem=recv_sems.at[n * j + i],
                                        device_id=dev, device_id_type=MESH)


def _slots(mode):
    return N_CHIP if mode == "chips" else N_DEV


def _piece(a, sliced):
    return a.shape[1:] if sliced else a.shape


def _swap_blocking(arrs, sliced, mode, name):
    n = len(arrs)

    def body(*refs):
        srcs, dsts = refs[:n], refs[n:2 * n]
        send_sems, recv_sems, local_sems = refs[2 * n:]
        my_chip, peers = _plan(mode)
        own_slot = peers[0][2]
        local = [pltpu.make_async_copy(srcs[i].at[my_chip] if sliced[i] else srcs[i], dsts[i].at[own_slot], local_sems.at[i])
                 for i in range(n)]
        for cp in local:
            cp.start()
        sends = [_peer_copy(srcs, dsts, sliced, send_sems, recv_sems, n, peer, j, i, False)
                 for j, peer in enumerate(peers) for i in range(n)]
        for cp in sends:
            cp.start()
        for j, peer in enumerate(peers):
            for i in range(n):
                _peer_copy(srcs, dsts, sliced, send_sems, recv_sems, n, peer, j, i, True).wait_recv()
        for cp in sends:
            cp.wait_send()
        for cp in local:
            cp.wait()

    npeer = _slots(mode) - 1
    return pl.pallas_call(
        body, name=name, in_specs=[ANY] * n, out_specs=[ANY] * n,
        out_shape=[jax.ShapeDtypeStruct((_slots(mode),) + _piece(a, s), a.dtype) for a, s in zip(arrs, sliced)],
        scratch_shapes=[pltpu.SemaphoreType.DMA((npeer * n,)), pltpu.SemaphoreType.DMA((npeer * n,)),
                        pltpu.SemaphoreType.DMA((n,))],
    )(*arrs)


HBM = pl.BlockSpec(memory_space=pltpu.HBM)
SEM = pl.BlockSpec(memory_space=pltpu.SEMAPHORE)
EFFECT = pltpu.SideEffectType.DATAFLOW_SIDE_EFFECTING


def _swap_start(arrs, lands, sliced, mode, name):
    n = len(arrs)

    def body(*refs):
        srcs, dsts, send_sems, recv_sems, token = refs[:n], refs[n:2 * n], refs[2 * n], refs[2 * n + 1], refs[-1]
        _, peers = _plan(mode)
        for j, peer in enumerate(peers):
            for i in range(n):
                _peer_copy(srcs, dsts, sliced, send_sems, recv_sems, n, peer, j, i, False).start()
        token[...] = jnp.zeros_like(token)

    ns = (_slots(mode) - 1) * n
    outs = pl.pallas_call(
        body, name=name,
        out_shape=(pltpu.SemaphoreType.DMA((ns,)), pltpu.SemaphoreType.DMA((ns,)),
                   *[pltpu.HBM(a.shape, a.dtype) for a in arrs], *[pltpu.HBM(l.shape, l.dtype) for l in lands],
                   jax.ShapeDtypeStruct((8, 128), F32)),
        in_specs=[HBM] * (2 * n), out_specs=(SEM, SEM, *[HBM] * (2 * n), pl.BlockSpec(memory_space=pltpu.VMEM)),
        input_output_aliases={k: 2 + k for k in range(2 * n)},
        compiler_params=pltpu.CompilerParams(has_side_effects=EFFECT),
    )(*[pltpu.with_memory_space_constraint(a, pltpu.HBM) for a in arrs],
      *[pltpu.with_memory_space_constraint(l, pltpu.HBM) for l in lands])
    return outs[0], outs[1], outs[2:2 + n], outs[2 + n:2 + 2 * n], outs[-1]


def _swap_wait(send_sems, recv_sems, srcs_thru, lands_thru, after, sliced, mode, name):
    n = len(srcs_thru)

    def body(*refs):
        srcs, dsts, s_sems, r_sems = refs[:n], refs[n:2 * n], refs[2 * n], refs[2 * n + 1]
        _, peers = _plan(mode)
        for j, peer in enumerate(peers):
            for i in range(n):
                cp = _peer_copy(srcs, dsts, sliced, s_sems, r_sems, n, peer, j, i, True)
                cp.wait_send()
                cp.wait_recv()

    outs = pl.pallas_call(
        body, name=name,
        out_shape=tuple(pltpu.HBM(a.shape, a.dtype) for a in (*srcs_thru, *lands_thru)),
        in_specs=[HBM] * (2 * n) + [SEM, SEM, ANY], out_specs=tuple([HBM] * (2 * n)),
        input_output_aliases={k: k for k in range(2 * n)},
        compiler_params=pltpu.CompilerParams(has_side_effects=EFFECT),
    )(*srcs_thru, *lands_thru, send_sems, recv_sems, after)
    return outs[n:]


def _adamw(w, g, m, v):
    m = ADAM_B1 * m + (1.0 - ADAM_B1) * g
    v = ADAM_B2 * v + (1.0 - ADAM_B2) * (g * g)
    m_hat = m / (1.0 - ADAM_B1 ** ADAM_STEP)
    v_hat = v / (1.0 - ADAM_B2 ** ADAM_STEP)
    delta = -ADAM_LR * (m_hat / (jnp.sqrt(v_hat) + ADAM_EPS) + ADAM_WD * w)
    return delta, m, v


def _reduce_adamw(rbuf, w, m, v, tr, name):
    _, rows, cols = w.shape

    def body(r_ref, w_ref, m_ref, v_ref, g_ref, d_ref, nm_ref, nv_ref):
        g = r_ref[0].astype(F32)
        for s in range(1, N_DEV):
            g = g + r_ref[s].astype(F32)
        g_ref[0] = g
        d_ref[0], nm_ref[0], nv_ref[0] = _adamw(w_ref[0], g, m_ref[0], v_ref[0])

    row = pl.BlockSpec((1, tr, cols), lambda i: (0, i, 0))
    return pl.pallas_call(
        body, name=name, grid=(rows // tr,),
        in_specs=[pl.BlockSpec((N_DEV, tr, cols), lambda i: (0, i, 0)), row, row, row], out_specs=[row] * 4,
        out_shape=[jax.ShapeDtypeStruct(w.shape, F32)] * 4,
        compiler_params=_params(("parallel",)),
    )(rbuf, w, m, v)


def _reduce_adamw_small(sbuf, w, m, v):
    def body(s_ref, w_ref, m_ref, v_ref, g_ref, d_ref, nm_ref, nv_ref, loss_ref):
        tot = s_ref[0]
        for s in range(1, N_DEV):
            tot = tot + s_ref[s]
        tot = jnp.sum(tot, axis=0, keepdims=True)
        g = tot[:, :N_VEC]
        g_ref[...] = g
        d_ref[...], nm_ref[...], nv_ref[...] = _adamw(w_ref[...], g, m_ref[...], v_ref[...])
        loss_ref[...] = tot[:, N_VEC:]

    return pl.pallas_call(
        body, name="reduce_adamw_small",
        out_shape=[jax.ShapeDtypeStruct((1, N_VEC), F32)] * 4 + [jax.ShapeDtypeStruct((1, 128), F32)],
    )(sbuf, w, m, v)


_ROW_SHARDED = ("w_out", "w_down")
_ADAM_TILE = {"w_in": 256, "w_out": 256, "w_gate": 256, "w_up": 256, "w_down": 176, "decay_w2": 64, "iclr_a2": 64,
              "gate_g2": 128}


def _full(n, stacked):
    p, r, c = stacked.shape
    if n in _ROW_SHARDED:
        return stacked.reshape(p * r, c)
    return jnp.transpose(stacked, (1, 0, 2)).reshape(r, p * c)


def _by_chip(n, full):
    if n in _ROW_SHARDED:
        return full.reshape(N_CHIP, full.shape[0] // N_CHIP, full.shape[1])
    r, c = full.shape
    return jnp.transpose(full.reshape(r, N_CHIP, c // N_CHIP), (1, 0, 2))


def _with_own(land_shape, dtype, own, slot):
    return lax.dynamic_update_slice(lax.empty(land_shape, dtype), own[None], (slot,) + (0,) * own.ndim)


def kernel(x, mix_norm_g, w_in, mu_shift, decay_w0, decay_w2, iclr_a0, iclr_a2, gate_g2, k_k, k_a, r_k, ln_x_w, ln_x_b, attn_out_g, w_out, ffn_norm_g, w_gate, w_up, w_down, final_norm_g, loss_target, m_mix_norm_g, m_w_in, m_mu_shift, m_decay_w0, m_decay_w2, m_iclr_a0, m_iclr_a2, m_gate_g2, m_k_k, m_k_a, m_r_k, m_ln_x_w, m_ln_x_b, m_attn_out_g, m_w_out, m_ffn_norm_g, m_w_gate, m_w_up, m_w_down, m_final_norm_g, v_mix_norm_g, v_w_in, v_mu_shift, v_decay_w0, v_decay_w2, v_iclr_a0, v_iclr_a2, v_gate_g2, v_k_k, v_k_a, v_r_k, v_ln_x_w, v_ln_x_b, v_attn_out_g, v_w_out, v_ffn_norm_g, v_w_gate, v_w_up, v_w_down, v_final_norm_g):
    names = ("mix_norm_g", "w_in", "mu_shift", "decay_w0", "decay_w2", "iclr_a0", "iclr_a2", "gate_g2", "k_k", "k_a",
             "r_k", "ln_x_w", "ln_x_b", "attn_out_g", "w_out", "ffn_norm_g", "w_gate", "w_up", "w_down", "final_norm_g")
    w = dict(zip(names, (mix_norm_g, w_in, mu_shift, decay_w0, decay_w2, iclr_a0, iclr_a2, gate_g2, k_k, k_a, r_k,
                         ln_x_w, ln_x_b, attn_out_g, w_out, ffn_norm_g, w_gate, w_up, w_down, final_norm_g)))
    m = dict(zip(names, (m_mix_norm_g, m_w_in, m_mu_shift, m_decay_w0, m_decay_w2, m_iclr_a0, m_iclr_a2, m_gate_g2,
                         m_k_k, m_k_a, m_r_k, m_ln_x_w, m_ln_x_b, m_attn_out_g, m_w_out, m_ffn_norm_g, m_w_gate,
                         m_w_up, m_w_down, m_final_norm_g)))
    v = dict(zip(names, (v_mix_norm_g, v_w_in, v_mu_shift, v_decay_w0, v_decay_w2, v_iclr_a0, v_iclr_a2, v_gate_g2,
                         v_k_k, v_k_a, v_r_k, v_ln_x_w, v_ln_x_b, v_attn_out_g, v_w_out, v_ffn_norm_g, v_w_gate,
                         v_w_up, v_w_down, v_final_norm_g)))
    first = ("w_in",) + LORAS
    rest = ("w_out", "w_gate", "w_up", "w_down")
    xi, yi, ci = lax.axis_index("x"), lax.axis_index("y"), lax.axis_index("c")
    my_chip, my_dev = 2 * xi + yi, 4 * xi + 2 * yi + ci
    whole, sliced = (False,) * 4, (True,) * 4

    wb = {n: w[n][0].astype(BF16) for n in MATS}
    lands = [_with_own((N_CHIP,) + wb[n].shape, BF16, wb[n], my_chip) for n in rest]
    ssem, rsem, srcs_thru, lands_thru, tok = _swap_start([wb[n] for n in rest], lands, whole, "chips", "gather_rest_start")
    got = _swap_blocking([wb["w_in"]] + [w[n][0] for n in LORAS], whole, "chips", "gather_first")
    win, w2, a2, g2m = (_full(n, z) for n, z in zip(first, got))

    vecs = {n: w[n].reshape(1, sz) for n, sz in VECS}
    vecs["mix_norm_g"] = vecs["mix_norm_g"] + tok[0, 0]

    def get_rest(after):
        full = _swap_wait(ssem, rsem, srcs_thru, lands_thru, after, whole, "chips", "gather_rest_wait")
        return [_full(n, z) for n, z in zip(rest, full)]

    flight = []

    def send_rest(gw):
        gs = [_by_chip(n, gw[n]).astype(BF16) for n in rest]
        own = [lax.dynamic_index_in_dim(g, my_chip, 0, keepdims=False) for g in gs]
        into = [_with_own((N_DEV,) + o.shape, BF16, o, my_dev) for o in own]
        flight.extend(_swap_start(gs, into, sliced, "devs", "exchange_rest_start"))
        return flight[4]

    loss8, dx, gw, gv = _local_step(x[0], loss_target[0], win, vecs, w2, a2, g2m, get_rest, send_rest)

    small = jnp.concatenate([gv[n] for n, _ in VECS] + [loss8], axis=1)
    got = _swap_blocking([_by_chip(n, gw[n]).astype(BF16) for n in first] + [small], sliced + (False,), "devs",
                         "exchange_first")
    rbuf = dict(zip(first, got[:4]))
    rbuf.update(zip(rest, _swap_wait(flight[0], flight[1], flight[2], flight[3], got[0], sliced, "devs",
                                     "exchange_rest_wait")))

    res = {n: _reduce_adamw(rbuf[n], w[n], m[n], v[n], _ADAM_TILE[n], "adamw_" + n) for n in MATS + LORAS}
    cat = lambda d: jnp.concatenate([d[n].reshape(1, sz) for n, sz in VECS], axis=1)
    small_res = _reduce_adamw_small(got[4], cat(w), cat(m), cat(v))

    outs = []
    for k in range(4):
        piece = {n: r[k] for n, r in res.items()}
        c0 = 0
        for n, sz in VECS:
            piece[n] = small_res[k][0, c0:c0 + sz].reshape(w[n].shape)
            c0 += sz
        outs.extend(piece[n] for n in names)
    return (small_res[4][0, 0], dx[None], *outs)
```

```python
import jax
import jax.numpy as jnp
from jax import lax
from jax.experimental import pallas as pl
from jax.experimental.pallas import tpu as pltpu

F32 = jnp.float32
BF16 = jnp.bfloat16
HI = lax.Precision.HIGHEST

D_MODEL = 1024
HEAD_DIM = 64
RW = 512
N_PAIR = RW // 128
SHIFT_COLS = 1792
IN_COLS = 3328
D_FF = 2816
NORM_EPS = 1e-6
GN_EPS = 64e-5
CHUNK = 64
SUB = 16
WKV_PASSES = 1
ATTN_PASSES = 1
ATTN_BLOCK = 128
DILATIONS = (1, 4, 16)
NEG = -1e30
ADAM_LR, ADAM_B1, ADAM_B2, ADAM_EPS, ADAM_WD, ADAM_STEP = 0.001, 0.9, 0.999, 1e-08, 0.01, 10
VMEM_LIMIT = 56 * 1024 * 1024
MESH = pl.DeviceIdType.MESH


def _params(sem=None, **kw):
    return pltpu.CompilerParams(dimension_semantics=sem, vmem_limit_bytes=VMEM_LIMIT, **kw)


def _dot(a, b, prec=None):
    return lax.dot_general(a, b, (((1,), (0,)), ((), ())), preferred_element_type=F32, precision=prec)


def _dot_nt(a, b, prec=None):
    return lax.dot_general(a, b, (((1,), (1,)), ((), ())), preferred_element_type=F32, precision=prec)


def _dot_tn(a, b, prec=None):
    return lax.dot_general(a, b, (((0,), (0,)), ((), ())), preferred_element_type=F32, precision=prec)


_FORMS = {"nn": ((1,), (0,)), "nt": ((1,), (1,)), "tn": ((0,), (0,))}


def _dg(a, b, form):
    if a.ndim == 3 or b.ndim == 3:
        nb = a.shape[0] if a.ndim == 3 else b.shape[0]
        return jnp.stack([_dg(a[i] if a.ndim == 3 else a, b[i] if b.ndim == 3 else b, form) for i in range(nb)], axis=0)
    return lax.dot_general(a, b, (_FORMS[form], ((), ())), preferred_element_type=F32)


def _split2(x):
    hi = x.astype(BF16)
    return hi, (x - hi.astype(F32)).astype(BF16)


def _split3(x):
    hi = x.astype(BF16)
    rest = x - hi.astype(F32)
    mid = rest.astype(BF16)
    return hi, mid, (rest - mid.astype(F32)).astype(BF16)


def _mm_raw(a, b, form, mode):
    if mode == 1:
        return _dg(a.astype(BF16), b.astype(BF16), form)
    if mode == 3:
        ah, al = _split2(a)
        bh, bl = _split2(b)
        return _dg(ah, bh, form) + (_dg(ah, bl, form) + _dg(al, bh, form))
    if mode == "L3":
        ab = a.astype(BF16)
        b1, b2, b3 = _split3(b)
        return _dg(ab, b1, form) + (_dg(ab, b2, form) + _dg(ab, b3, form))
    assert mode == "R3", mode
    bb = b.astype(BF16)
    a1, a2, a3 = _split3(a)
    return _dg(a1, bb, form) + (_dg(a2, bb, form) + _dg(a3, bb, form))


def _mm(a, b, form, mode):
    @jax.custom_vjp
    def f(a, b):
        return _mm_raw(a, b, form, mode)

    def fwd(a, b):
        return _mm_raw(a, b, form, mode), (a, b)

    def bwd(res, ct):
        a, b = res
        la = {1: 1, 3: 3, "L3": None, "R3": "R3"}[mode]
        lb = {1: 1, 3: 3, "L3": "L3", "R3": None}[mode]
        if form == "nn":
            da = None if la is None else _mm_raw(ct, b, "nt", la)
            db = None if lb is None else _mm_raw(a, ct, "tn", lb)
        elif form == "nt":
            da = None if la is None else _mm_raw(ct, b, "nn", la)
            db = None if lb is None else _mm_raw(ct, a, "tn", "R3" if lb == "L3" else lb)
        else:
            da = None if la is None else _mm_raw(b, ct, "nt", "L3" if la == "R3" else la)
            db = None if lb is None else _mm_raw(a, ct, "nn", lb)
        return (jnp.zeros_like(a) if da is None else da, jnp.zeros_like(b) if db is None else db)

    f.defvjp(fwd, bwd)
    return f(a, b)


def _seg_ones(n):
    r = lax.broadcasted_iota(jnp.int32, (n, n), 0) // HEAD_DIM
    c = lax.broadcasted_iota(jnp.int32, (n, n), 1) // HEAD_DIM
    return (r == c).astype(F32)


def _segsum(x, seg):
    return _mm(x, seg, "nn", "R3")


def _rms_fwd(x, g):
    rstd = lax.rsqrt(jnp.mean(x * x, axis=-1, keepdims=True) + NORM_EPS)
    return x * rstd * g


def _rms_bwd(dy, x, g):
    rstd = lax.rsqrt(jnp.mean(x * x, axis=-1, keepdims=True) + NORM_EPS)
    xn = x * rstd
    dxn = dy * g
    dx = rstd * (dxn - xn * jnp.mean(dxn * xn, axis=-1, keepdims=True))
    return dx, dy * xn


def _sigmoid(x):
    return 1.0 / (1.0 + jnp.exp(-x))


def _softplus(x):
    return jnp.maximum(x, 0.0) + jnp.log(1.0 + jnp.exp(-jnp.abs(x)))


def _acc(ref, val, first):
    @pl.when(first)
    def _():
        ref[...] = val

    @pl.when(jnp.logical_not(first))
    def _():
        ref[...] += val


def _colsum8(v):
    rows, n = v.shape
    return jnp.sum(v.reshape(rows // 8, 8, n), axis=0)


def _prep_fn(p, pprev, mu, w0, w2p, a0, a2p, g2, k_k, k_a):
    seg = _seg_ones(RW)
    ps = p + (pprev - p) * mu
    r = ps[:, 0:RW]
    k = ps[:, RW:2 * RW]
    v = ps[:, 2 * RW:3 * RW]
    xwa = ps[:, 3 * RW:3 * RW + 128]
    xg = ps[:, 3 * RW + 128:3 * RW + 256]
    wraw = -_softplus(-(w0 + _mm(jnp.tanh(xwa), w2p, "nn", 3))) - 0.5
    lw = -jnp.exp(wraw)
    a = _sigmoid(a0 + _mm(xwa, a2p, "nn", 3))
    g = _mm(_sigmoid(xg), g2, "nn", 3)
    kk = k * k_k
    kk = kk / jnp.maximum(jnp.sqrt(_segsum(kk * kk, seg)), 1e-12)
    k2 = k * (1.0 + (a - 1.0) * k_a)
    return r, lw, k2, v, kk, a, g


def _solve_unit_lower(lmat, rhs):
    c = lmat.shape[-1]
    row = lax.broadcasted_iota(jnp.int32, (c, c), 0)
    col = lax.broadcasted_iota(jnp.int32, (c, c), 1)
    eye = (row == col).astype(F32)
    ld = jnp.where(row // SUB == col // SUB, lmat, 0.0)
    lo = lmat - ld
    x = eye + ld
    m = ld
    mm = lambda p, q: _mm(p, q, "nn", WKV_PASSES)
    for _ in range(3):
        m = mm(m, m)
        x = x + mm(x, m)
    g = mm(x, lo)
    g2 = mm(g, g)
    w = mm(x, rhs)
    w = w + mm(g2, w)
    return w + mm(g, w)


def _wkv_chunk_fn(s0, r, lw, k, v, kk, a):
    c = r.shape[-2]
    n = 2 * c
    row = lax.broadcasted_iota(jnp.int32, (n, n), 0)
    col = lax.broadcasted_iota(jnp.int32, (n, n), 1)
    same = (row // c) == (col // c)
    incl = jnp.logical_and(row >= col, same)
    strict = jnp.logical_and(row > col, same)
    sel = (lax.broadcasted_iota(jnp.int32, (n, 128), 0) // c) == (lax.broadcasted_iota(jnp.int32, (n, 128), 1) // HEAD_DIM)
    two = lambda z: jnp.concatenate([z, z], axis=-2)
    lw2 = two(lw)
    mm = lambda p_, q_, form: _mm(p_, q_, form, WKV_PASSES)
    cl = _mm(incl.astype(F32), lw2, "nn", "L3")
    p = jnp.exp(cl)
    pinv = jnp.exp(-cl)
    pprev = jnp.exp(cl - lw2)
    kk2 = two(kk)
    at = jnp.where(sel, -kk2 * pprev, 0.0)
    bt = jnp.where(sel, kk2 * two(a) * pinv, 0.0)
    kt = jnp.where(sel, two(k) * pinv, 0.0)
    rt = jnp.where(sel, two(r) * p, 0.0)
    vt = jnp.where(sel, two(v), 0.0)
    ab = jnp.where(strict, mm(at, bt, "nt"), 0.0)
    ak = jnp.where(strict, mm(at, kt, "nt"), 0.0)
    rb = jnp.where(incl, mm(rt, bt, "nt"), 0.0)
    rk = jnp.where(incl, mm(rt, kt, "nt"), 0.0)
    u = _solve_unit_lower(ab, mm(at, s0, "nt") + mm(ak, vt, "nn"))
    y2 = mm(rt, s0, "nt") + mm(rb, u, "nn") + mm(rk, vt, "nn")
    plast = jnp.exp(jnp.sum(lw, axis=-2, keepdims=True))
    s1 = (s0 + mm(u, bt, "tn") + mm(vt, kt, "tn")) * plast
    r2 = lax.broadcasted_iota(jnp.int32, (128, 128), 0) // HEAD_DIM
    c2 = lax.broadcasted_iota(jnp.int32, (128, 128), 1) // HEAD_DIM
    return y2[..., :c, :] + y2[..., c:, :], jnp.where(r2 == c2, s1, 0.0)


def _post_fn(y, r, k2, v, g, lnw, lnb, rk):
    seg = _seg_ones(RW)
    mean = _segsum(y, seg) * (1.0 / HEAD_DIM)
    yc = y - mean
    var = _segsum(yc * yc, seg) * (1.0 / HEAD_DIM)
    yn = yc * lax.rsqrt(var + GN_EPS)
    out = yn * lnw + lnb + _segsum(r * k2 * rk, seg) * v
    return out * g


def _attn_block_fn(q, kc, vc, kp=None, vp=None):
    n = ATTN_BLOCK
    qi = lax.broadcasted_iota(jnp.int32, (n, n), 0)
    kj = lax.broadcasted_iota(jnp.int32, (n, n), 1)
    lane = lax.broadcasted_iota(jnp.int32, (1, 128), 1)
    scale = HEAD_DIM ** -0.5
    os_, ls_ = [], []
    for h in range(2):
        mh = (lane // HEAD_DIM) == h
        qh = jnp.where(mh, q, 0.0)
        sc = jnp.where(kj <= qi, _mm(qh, kc, "nt", ATTN_PASSES) * scale, NEG)
        m = jnp.max(sc, axis=-1, keepdims=True)
        if kp is not None:
            sp = jnp.where(kj >= qi, _mm(qh, kp, "nt", ATTN_PASSES) * scale, NEG)
            m = jnp.maximum(m, jnp.max(sp, axis=-1, keepdims=True))
        pc = jnp.exp(sc - m)
        den = jnp.sum(pc, axis=-1, keepdims=True)
        num = _mm(pc, vc, "nn", ATTN_PASSES)
        if kp is not None:
            pp = jnp.exp(sp - m)
            den = den + jnp.sum(pp, axis=-1, keepdims=True)
            num = num + _mm(pp, vp, "nn", ATTN_PASSES)
        os_.append(num / den)
        ls_.append(m + jnp.log(den))
    m0 = (lane // HEAD_DIM) == 0
    return jnp.where(m0, os_[0], os_[1]), jnp.where(m0, ls_[0], ls_[1])


def _combine_fn(o1, o2, o3, l1, l2, l3, og):
    seg = _seg_ones(o1.shape[-1])
    m = jnp.maximum(jnp.maximum(l1, l2), l3)
    e1, e2, e3 = jnp.exp(l1 - m), jnp.exp(l2 - m), jnp.exp(l3 - m)
    o = (e1 * o1 + e2 * o2 + e3 * o3) / (e1 + e2 + e3)
    o = o * lax.rsqrt(_segsum(o * o, seg) * (1.0 / HEAD_DIM) + NORM_EPS)
    return o * og


def _in_proj(x, g1, win):
    t = x.shape[0]
    tm = 256

    def body(x_ref, g_ref, w_ref, h_ref, pa_ref, qkv_ref):
        h = _rms_fwd(x_ref[...], g_ref[...]).astype(BF16)
        h_ref[...] = h
        proj = _dot(h, w_ref[...])
        pa_ref[...] = proj[:, :SHIFT_COLS]
        for j in range(3):
            for p in range(N_PAIR):
                c0 = SHIFT_COLS + j * RW + p * 128
                qkv_ref[j, p] = proj[:, c0:c0 + 128]

    return pl.pallas_call(
        body, name="in_proj", grid=(t // tm,),
        in_specs=[pl.BlockSpec((tm, D_MODEL), lambda i: (i, 0)), pl.BlockSpec((1, D_MODEL), lambda i: (0, 0)),
                  pl.BlockSpec((D_MODEL, IN_COLS), lambda i: (0, 0))],
        out_specs=[pl.BlockSpec((tm, D_MODEL), lambda i: (i, 0)), pl.BlockSpec((tm, SHIFT_COLS), lambda i: (i, 0)),
                   pl.BlockSpec((3, N_PAIR, tm, 128), lambda i: (0, 0, i, 0))],
        out_shape=[jax.ShapeDtypeStruct((t, D_MODEL), BF16), jax.ShapeDtypeStruct((t, SHIFT_COLS), F32),
                   jax.ShapeDtypeStruct((3, N_PAIR, t, 128), F32)],
        compiler_params=_params(("parallel",)),
    )(x, g1, win)


def _shifted(p, last8, first):
    prow = jnp.where(first, 0.0, last8[7:8, :])
    rolled = pltpu.roll(p, 1, axis=0)
    rid = lax.broadcasted_iota(jnp.int32, p.shape, 0)
    return jnp.where(rid == 0, prow, rolled)


_PREP_TM = 256


def _prep_specs(tm):
    vec = lambda n: pl.BlockSpec((1, n), lambda i: (0, 0))
    mat = lambda r, n: pl.BlockSpec((r, n), lambda i: (0, 0))
    return [vec(SHIFT_COLS), vec(RW), mat(128, RW), vec(RW), mat(128, RW), mat(128, RW), vec(RW), vec(RW)]


def _prep_fwd(proj, pw):
    t = proj.shape[0]
    tm = _PREP_TM

    def body(p_ref, l8_ref, mu, w0, w2p, a0, a2p, g2, k_k, k_a, *outs):
        p = p_ref[...]
        pprev = _shifted(p, l8_ref[...], pl.program_id(0) == 0)
        res = _prep_fn(p, pprev, mu[...], w0[...], w2p[...], a0[...], a2p[...], g2[...], k_k[...], k_a[...])
        for o_ref, val in zip(outs, res):
            o_ref[...] = val

    row = pl.BlockSpec((tm, RW), lambda i: (i, 0))
    return pl.pallas_call(
        body, name="rwkv_prep", grid=(t // tm,),
        in_specs=[pl.BlockSpec((tm, SHIFT_COLS), lambda i: (i, 0)),
                  pl.BlockSpec((8, SHIFT_COLS), lambda i: (jnp.maximum(i * (tm // 8) - 1, 0), 0))] + _prep_specs(tm),
        out_specs=[row] * 7,
        out_shape=[jax.ShapeDtypeStruct((t, RW), F32)] * 7,
        compiler_params=_params(("parallel",)),
    )(proj, proj, *pw)


def _pairs(ref):
    return jnp.stack([ref[:, 128 * p:128 * (p + 1)] for p in range(N_PAIR)], axis=0)


def _wkv_fwd(r, lw, k2, v, kk, a):
    t = r.shape[0]
    nc = t // CHUNK

    def body(r_ref, lw_ref, k_ref, v_ref, kk_ref, a_ref, y_ref, s_ref, st):
        @pl.when(pl.program_id(0) == 0)
        def _():
            st[...] = jnp.zeros_like(st)

        s0 = st[...]
        s_ref[0] = s0
        y, s1 = _wkv_chunk_fn(s0, *[_pairs(ref) for ref in (r_ref, lw_ref, k_ref, v_ref, kk_ref, a_ref)])
        for p in range(N_PAIR):
            y_ref[:, 128 * p:128 * (p + 1)] = y[p]
        st[...] = s1

    blk = pl.BlockSpec((CHUNK, RW), lambda c: (c, 0))
    return pl.pallas_call(
        body, name="wkv_fwd", grid=(nc,),
        in_specs=[blk] * 6,
        out_specs=[blk, pl.BlockSpec((1, N_PAIR, 128, 128), lambda c: (c, 0, 0, 0))],
        out_shape=[jax.ShapeDtypeStruct((t, RW), F32), jax.ShapeDtypeStruct((nc, N_PAIR, 128, 128), F32)],
        scratch_shapes=[pltpu.VMEM((N_PAIR, 128, 128), F32)],
        compiler_params=_params(("arbitrary",)),
    )(r, lw, k2, v, kk, a)


_POST_TM = 256


def _post_fwd(y, r, k2, v, g, lnw, lnb, rk):
    t = y.shape[0]
    tm = _POST_TM

    def body(y_ref, r_ref, k_ref, v_ref, g_ref, lnw_ref, lnb_ref, rk_ref, o_ref):
        o_ref[...] = _post_fn(y_ref[...], r_ref[...], k_ref[...], v_ref[...], g_ref[...],
                              lnw_ref[...], lnb_ref[...], rk_ref[...]).astype(BF16)

    row = pl.BlockSpec((tm, RW), lambda i: (i, 0))
    vec = pl.BlockSpec((1, RW), lambda i: (0, 0))
    return pl.pallas_call(
        body, name="rwkv_post", grid=(t // tm,),
        in_specs=[row] * 5 + [vec] * 3, out_specs=row,
        out_shape=jax.ShapeDtypeStruct((t, RW), BF16),
        compiler_params=_params(("parallel",)),
    )(y, r, k2, v, g, lnw, lnb, rk)


ATTN_GROUP = 2


def _dilated_rows(d, r, n):
    if d == 1:
        return pl.ds(pl.multiple_of(n * ATTN_BLOCK, ATTN_BLOCK), ATTN_BLOCK)
    return pl.ds(r + n * (ATTN_BLOCK * d), ATTN_BLOCK, stride=d)


def _for_each_sequence(t, unit):
    for di, d in enumerate(DILATIONS):
        nb = t // (ATTN_BLOCK * d)

        def residue(r, carry, di=di, d=d, nb=nb):
            unit(di, d, r, 0, False)
            if nb > 1:
                lax.fori_loop(1, nb, lambda n, c: (unit(di, d, r, n, True), c)[1], 0)
            return carry

        if d == 1:
            residue(0, 0)
        else:
            lax.fori_loop(0, d, residue, 0)


def _take(ref, lead, rows):
    return jnp.stack([ref.at[(*lead, g)][rows, :] for g in range(ATTN_GROUP)], axis=0)


def _attn_fwd(qkv):
    t = qkv.shape[2]

    def body(q_ref, k_ref, v_ref, o_ref, l_ref):
        def unit(di, d, r, n, has_prev):
            cur = _dilated_rows(d, r, n)
            args = [_take(ref, (0,), cur) for ref in (q_ref, k_ref, v_ref)]
            if has_prev:
                prv = _dilated_rows(d, r, n - 1)
                args += [_take(ref, (0,), prv) for ref in (k_ref, v_ref)]
            o, lse = _attn_block_fn(*args)
            for g in range(ATTN_GROUP):
                o_ref.at[di, g][cur, :] = o[g]
                l_ref.at[di, g][cur, :] = lse[g]

        _for_each_sequence(t, unit)

    spec = lambda j: pl.BlockSpec((1, ATTN_GROUP, t, 128), lambda i: (j, i, 0, 0))
    out = pl.BlockSpec((3, ATTN_GROUP, t, 128), lambda i: (0, i, 0, 0))
    return pl.pallas_call(
        body, name="attn_fwd", grid=(N_PAIR // ATTN_GROUP,),
        in_specs=[spec(0), spec(1), spec(2)], out_specs=[out, out],
        out_shape=[jax.ShapeDtypeStruct((3, N_PAIR, t, 128), F32)] * 2,
        compiler_params=_params(("parallel",)),
    )(qkv, qkv, qkv)


_COMB_TM = 256


def _combine_fwd(o, l, og):
    t = o.shape[2]
    tm = _COMB_TM

    def body(o_ref, l_ref, og_ref, y_ref):
        for p in range(N_PAIR):
            cols = slice(128 * p, 128 * (p + 1))
            y_ref[:, cols] = _combine_fn(o_ref[0, p], o_ref[1, p], o_ref[2, p], l_ref[0, p], l_ref[1, p], l_ref[2, p],
                                         og_ref[:, cols]).astype(BF16)

    blk = pl.BlockSpec((3, N_PAIR, tm, 128), lambda i: (0, 0, i, 0))
    return pl.pallas_call(
        body, name="attn_combine", grid=(t // tm,),
        in_specs=[blk, blk, pl.BlockSpec((1, RW), lambda i: (0, 0))], out_specs=pl.BlockSpec((tm, RW), lambda i: (i, 0)),
        out_shape=jax.ShapeDtypeStruct((t, RW), BF16),
        compiler_params=_params(("parallel",)),
    )(o, l, og)


def _out_proj(x, ycat, wout, g2):
    t = x.shape[0]
    tm = 256

    def body(x_ref, y_ref, w_ref, g_ref, x1_ref, h_ref):
        x1 = x_ref[...] + _dot(y_ref[...], w_ref[...])
        x1_ref[...] = x1
        h_ref[...] = _rms_fwd(x1, g_ref[...]).astype(BF16)

    row = pl.BlockSpec((tm, D_MODEL), lambda i: (i, 0))
    return pl.pallas_call(
        body, name="out_proj", grid=(t // tm,),
        in_specs=[row, row, pl.BlockSpec((D_MODEL, D_MODEL), lambda i: (0, 0)), pl.BlockSpec((1, D_MODEL), lambda i: (0, 0))],
        out_specs=[row, row],
        out_shape=[jax.ShapeDtypeStruct((t, D_MODEL), F32), jax.ShapeDtypeStruct((t, D_MODEL), BF16)],
        compiler_params=_params(("parallel",)),
    )(x, ycat, wout, g2)


def _ffn_up(h2, wg, wu):
    t = h2.shape[0]
    tm = 256

    def body(h_ref, wg_ref, wu_ref, gt_ref, up_ref, act_ref):
        h = h_ref[...]
        gt = _dot(h, wg_ref[...])
        up = _dot(h, wu_ref[...])
        gt_ref[...] = gt
        up_ref[...] = up
        act_ref[...] = (gt * _sigmoid(gt) * up).astype(BF16)

    wide = pl.BlockSpec((tm, D_FF), lambda i: (i, 0))
    wsp = pl.BlockSpec((D_MODEL, D_FF), lambda i: (0, 0))
    return pl.pallas_call(
        body, name="ffn_up", grid=(t // tm,),
        in_specs=[pl.BlockSpec((tm, D_MODEL), lambda i: (i, 0)), wsp, wsp],
        out_specs=[wide, wide, wide],
        out_shape=[jax.ShapeDtypeStruct((t, D_FF), F32)] * 2 + [jax.ShapeDtypeStruct((t, D_FF), BF16)],
        compiler_params=_params(("parallel",)),
    )(h2, wg, wu)


def _ffn_down_loss(x1, act, wd, gf, tgt):
    t = x1.shape[0]
    tm = 256

    def body(x1_ref, a_ref, w_ref, g_ref, t_ref, dx_ref, dxb_ref, loss_ref, dg_ref):
        first = pl.program_id(0) == 0
        x2 = x1_ref[...] + _dot(a_ref[...], w_ref[...])
        g = g_ref[...]
        diff = _rms_fwd(x2, g) - t_ref[...]
        lrow = 0.5 * jnp.sum(_colsum8(diff * diff), axis=1, keepdims=True) * (1.0 / D_MODEL)
        _acc(loss_ref, jnp.broadcast_to(lrow, (8, 128)), first)
        dx2, dgr = _rms_bwd(diff * (1.0 / D_MODEL), x2, g)
        dx_ref[...] = dx2
        dxb_ref[...] = dx2.astype(BF16)
        _acc(dg_ref, _colsum8(dgr), first)

    row = pl.BlockSpec((tm, D_MODEL), lambda i: (i, 0))
    return pl.pallas_call(
        body, name="ffn_down_loss", grid=(t // tm,),
        in_specs=[row, pl.BlockSpec((tm, D_FF), lambda i: (i, 0)), pl.BlockSpec((D_FF, D_MODEL), lambda i: (0, 0)),
                  pl.BlockSpec((1, D_MODEL), lambda i: (0, 0)), row],
        out_specs=[row, row, pl.BlockSpec((8, 128), lambda i: (0, 0)), pl.BlockSpec((8, D_MODEL), lambda i: (0, 0))],
        out_shape=[jax.ShapeDtypeStruct((t, D_MODEL), F32), jax.ShapeDtypeStruct((t, D_MODEL), BF16),
                   jax.ShapeDtypeStruct((8, 128), F32), jax.ShapeDtypeStruct((8, D_MODEL), F32)],
        compiler_params=_params(("arbitrary",)),
    )(x1, act, wd, gf, tgt)


def _ffn_bwd_act(dx2b, wd, gt, up):
    t = dx2b.shape[0]
    tm = 256

    def body(dx_ref, w_ref, gt_ref, up_ref, dgt_ref, dup_ref):
        dact = _dot_nt(dx_ref[...], w_ref[...])
        gt = gt_ref[...]
        sg = _sigmoid(gt)
        dgt_ref[...] = (dact * up_ref[...] * sg * (1.0 + gt * (1.0 - sg))).astype(BF16)
        dup_ref[...] = (dact * gt * sg).astype(BF16)

    wide = pl.BlockSpec((tm, D_FF), lambda i: (i, 0))
    return pl.pallas_call(
        body, name="ffn_bwd_act", grid=(t // tm,),
        in_specs=[pl.BlockSpec((tm, D_MODEL), lambda i: (i, 0)), pl.BlockSpec((D_FF, D_MODEL), lambda i: (0, 0)), wide, wide],
        out_specs=[wide, wide],
        out_shape=[jax.ShapeDtypeStruct((t, D_FF), BF16)] * 2,
        compiler_params=_params(("parallel",)),
    )(dx2b, wd, gt, up)


def _ffn_bwd_h(dgt, dup, wg, wu, dx2, x1, g2, wout):
    t = dgt.shape[0]
    tm = 256

    def body(dgt_ref, dup_ref, wg_ref, wu_ref, dx2_ref, x1_ref, g_ref, wo_ref, dx1_ref, dx1b_ref, dya_ref, dyb_ref, dg_ref):
        dh = _dot_nt(dgt_ref[...], wg_ref[...]) + _dot_nt(dup_ref[...], wu_ref[...])
        dxn, dgr = _rms_bwd(dh, x1_ref[...], g_ref[...])
        dx1 = dx2_ref[...] + dxn
        dx1_ref[...] = dx1
        dx1b = dx1.astype(BF16)
        dx1b_ref[...] = dx1b
        dy = _dot_nt(dx1b, wo_ref[...])
        dya_ref[...] = dy[:, :RW]
        dyb_ref[...] = dy[:, RW:]
        _acc(dg_ref, _colsum8(dgr), pl.program_id(0) == 0)

    wide = pl.BlockSpec((tm, D_FF), lambda i: (i, 0))
    row = pl.BlockSpec((tm, D_MODEL), lambda i: (i, 0))
    half = pl.BlockSpec((tm, RW), lambda i: (i, 0))
    wsp = pl.BlockSpec((D_MODEL, D_FF), lambda i: (0, 0))
    return pl.pallas_call(
        body, name="ffn_bwd_h", grid=(t // tm,),
        in_specs=[wide, wide, wsp, wsp, row, row, pl.BlockSpec((1, D_MODEL), lambda i: (0, 0)),
                  pl.BlockSpec((D_MODEL, D_MODEL), lambda i: (0, 0))],
        out_specs=[row, row, half, half, pl.BlockSpec((8, D_MODEL), lambda i: (0, 0))],
        out_shape=[jax.ShapeDtypeStruct((t, D_MODEL), F32), jax.ShapeDtypeStruct((t, D_MODEL), BF16),
                   jax.ShapeDtypeStruct((t, RW), F32), jax.ShapeDtypeStruct((t, RW), F32),
                   jax.ShapeDtypeStruct((8, D_MODEL), F32)],
        compiler_params=_params(("arbitrary",)),
    )(dgt, dup, wg, wu, dx2, x1, g2, wout)


def _wgrad(a, b, tk, tn, name):
    t, kdim = a.shape
    ndim = b.shape[1]

    def body(a_ref, b_ref, o_ref):
        o_ref[...] = _dot_tn(a_ref[...], b_ref[...])

    return pl.pallas_call(
        body, name=name, grid=(kdim // tk, ndim // tn),
        in_specs=[pl.BlockSpec((t, tk), lambda i, j: (0, i)), pl.BlockSpec((t, tn), lambda i, j: (0, j))],
        out_specs=pl.BlockSpec((tk, tn), lambda i, j: (i, j)),
        out_shape=jax.ShapeDtypeStruct((kdim, ndim), F32),
        compiler_params=_params(("parallel", "parallel")),
    )(a, b)


def _post_bwd(dya, y, r, k2, v, g, lnw, lnb, rk):
    t = y.shape[0]
    tm = _POST_TM

    def body(d_ref, y_ref, r_ref, k_ref, v_ref, g_ref, lnw_ref, lnb_ref, rk_ref,
             dy_ref, dr_ref, dk_ref, dv_ref, dg_ref, dlnw_ref, dlnb_ref, drk_ref):
        first = pl.program_id(0) == 0
        ones = jnp.ones((tm, 1), F32)
        prim = (y_ref[...], r_ref[...], k_ref[...], v_ref[...], g_ref[...],
                ones * lnw_ref[...], ones * lnb_ref[...], ones * rk_ref[...])
        _, vjp = jax.vjp(_post_fn, *prim)
        dy, dr, dk, dv, dg, dlnw, dlnb, drk = vjp(d_ref[...])
        dy_ref[...] = dy
        dr_ref[...] = dr
        dk_ref[...] = dk
        dv_ref[...] = dv
        dg_ref[...] = dg
        _acc(dlnw_ref, _colsum8(dlnw), first)
        _acc(dlnb_ref, _colsum8(dlnb), first)
        _acc(drk_ref, _colsum8(drk), first)

    row = pl.BlockSpec((tm, RW), lambda i: (i, 0))
    vec = pl.BlockSpec((1, RW), lambda i: (0, 0))
    part = pl.BlockSpec((8, RW), lambda i: (0, 0))
    return pl.pallas_call(
        body, name="rwkv_post_bwd", grid=(t // tm,),
        in_specs=[row] * 6 + [vec] * 3, out_specs=[row] * 5 + [part] * 3,
        out_shape=[jax.ShapeDtypeStruct((t, RW), F32)] * 5 + [jax.ShapeDtypeStruct((8, RW), F32)] * 3,
        compiler_params=_params(("arbitrary",)),
    )(dya, y, r, k2, v, g, lnw, lnb, rk)


def _wkv_bwd(dy, s0s, r, lw, k2, v, kk, a):
    t = r.shape[0]
    nc = t // CHUNK

    def body(dy_ref, s_ref, r_ref, lw_ref, k_ref, v_ref, kk_ref, a_ref,
             dr_ref, dlw_ref, dk_ref, dv_ref, dkk_ref, da_ref, ds):
        @pl.when(pl.program_id(0) == 0)
        def _():
            ds[...] = jnp.zeros_like(ds)

        _, vjp = jax.vjp(_wkv_chunk_fn, s_ref[0],
                         *[_pairs(ref) for ref in (r_ref, lw_ref, k_ref, v_ref, kk_ref, a_ref)])
        res = vjp((_pairs(dy_ref), ds[...]))
        ds[...] = res[0]
        for ref, val in zip((dr_ref, dlw_ref, dk_ref, dv_ref, dkk_ref, da_ref), res[1:]):
            for p in range(N_PAIR):
                ref[:, 128 * p:128 * (p + 1)] = val[p]

    blk = pl.BlockSpec((CHUNK, RW), lambda c: (nc - 1 - c, 0))
    return pl.pallas_call(
        body, name="wkv_bwd", grid=(nc,),
        in_specs=[blk, pl.BlockSpec((1, N_PAIR, 128, 128), lambda c: (nc - 1 - c, 0, 0, 0))] + [blk] * 6,
        out_specs=[blk] * 6,
        out_shape=[jax.ShapeDtypeStruct((t, RW), F32)] * 6,
        scratch_shapes=[pltpu.VMEM((N_PAIR, 128, 128), F32)],
        compiler_params=_params(("arbitrary",)),
    )(dy, s0s, r, lw, k2, v, kk, a)


def _prep_bwd(proj, pw, douts):
    t = proj.shape[0]
    tm = _PREP_TM
    nt = t // tm

    def body(p_ref, l8_ref, mu, w0, w2p, a0, a2p, g2, k_k, k_a, dr, dr2, dlw, dk2, dk22, dv, dv2, dkk, da, dg,
             dp_ref, dmu_ref, dw0_ref, dw2_ref, da0_ref, da2_ref, dg2_ref, dkk_ref, dka_ref, carry):
        i = pl.program_id(0)
        first = i == 0

        @pl.when(first)
        def _():
            carry[...] = jnp.zeros_like(carry)

        p = p_ref[...]
        pprev = _shifted(p, l8_ref[...], i == nt - 1)
        ones = jnp.ones((tm, 1), F32)
        prim = (p, pprev, ones * mu[...], ones * w0[...], w2p[...], ones * a0[...], a2p[...], g2[...],
                ones * k_k[...], ones * k_a[...])
        _, vjp = jax.vjp(_prep_fn, *prim)
        dp, dpp, dmu, dw0, dw2, da0, da2, dg2, dkk_, dka = vjp(
            (dr[...] + dr2[...], dlw[...], dk2[...] + dk22[...], dv[...] + dv2[...], dkk[...], da[...], dg[...]))
        up = pltpu.roll(dpp, tm - 1, axis=0)
        rid = lax.broadcasted_iota(jnp.int32, dpp.shape, 0)
        dp_ref[...] = dp + jnp.where(rid == tm - 1, carry[0:1, :], up)
        carry[...] = jnp.broadcast_to(dpp[0:1, :], carry.shape)
        _acc(dmu_ref, _colsum8(dmu), first)
        _acc(dw0_ref, _colsum8(dw0), first)
        _acc(dw2_ref, dw2, first)
        _acc(da0_ref, _colsum8(da0), first)
        _acc(da2_ref, da2, first)
        _acc(dg2_ref, dg2, first)
        _acc(dkk_ref, _colsum8(dkk_), first)
        _acc(dka_ref, _colsum8(dka), first)

    rev = lambda i: (nt - 1 - i, 0)
    row = pl.BlockSpec((tm, RW), rev)
    part = lambda n: pl.BlockSpec((8, n), lambda i: (0, 0))
    mat = pl.BlockSpec((128, RW), lambda i: (0, 0))
    return pl.pallas_call(
        body, name="rwkv_prep_bwd", grid=(nt,),
        in_specs=[pl.BlockSpec((tm, SHIFT_COLS), rev),
                  pl.BlockSpec((8, SHIFT_COLS), lambda i: (jnp.maximum((nt - 1 - i) * (tm // 8) - 1, 0), 0))]
                 + _prep_specs(tm) + [row] * 10,
        out_specs=[pl.BlockSpec((tm, SHIFT_COLS), rev), part(SHIFT_COLS), part(RW), mat, part(RW), mat, mat,
                   part(RW), part(RW)],
        out_shape=[jax.ShapeDtypeStruct((t, SHIFT_COLS), F32), jax.ShapeDtypeStruct((8, SHIFT_COLS), F32),
                   jax.ShapeDtypeStruct((8, RW), F32), jax.ShapeDtypeStruct((128, RW), F32),
                   jax.ShapeDtypeStruct((8, RW), F32), jax.ShapeDtypeStruct((128, RW), F32),
                   jax.ShapeDtypeStruct((128, RW), F32), jax.ShapeDtypeStruct((8, RW), F32),
                   jax.ShapeDtypeStruct((8, RW), F32)],
        scratch_shapes=[pltpu.VMEM((8, SHIFT_COLS), F32)],
        compiler_params=_params(("arbitrary",)),
    )(proj, proj, *pw, *douts)


def _combine_bwd(dyb, o, l, og):
    t = dyb.shape[0]
    tm = _COMB_TM

    def body(d_ref, o_ref, l_ref, og_ref, do_ref, dl_ref, dog_ref):
        ones = jnp.ones((tm, 1), F32)
        dog = []
        for p in range(N_PAIR):
            cols = slice(128 * p, 128 * (p + 1))
            _, vjp = jax.vjp(_combine_fn, o_ref[0, p], o_ref[1, p], o_ref[2, p], l_ref[0, p], l_ref[1, p], l_ref[2, p],
                             ones * og_ref[:, cols])
            res = vjp(d_ref[:, cols])
            for b in range(3):
                do_ref[b, p] = res[b]
                dl_ref[b, p] = res[3 + b]
            dog.append(_colsum8(res[6]))
        _acc(dog_ref, jnp.concatenate(dog, axis=1), pl.program_id(0) == 0)

    blk = pl.BlockSpec((3, N_PAIR, tm, 128), lambda i: (0, 0, i, 0))
    return pl.pallas_call(
        body, name="attn_combine_bwd", grid=(t // tm,),
        in_specs=[pl.BlockSpec((tm, RW), lambda i: (i, 0)), blk, blk, pl.BlockSpec((1, RW), lambda i: (0, 0))],
        out_specs=[blk, blk, pl.BlockSpec((8, RW), lambda i: (0, 0))],
        out_shape=[jax.ShapeDtypeStruct((3, N_PAIR, t, 128), F32)] * 2 + [jax.ShapeDtypeStruct((8, RW), F32)],
        compiler_params=_params(("arbitrary",)),
    )(dyb, o, l, og)


def _attn_bwd(do, dl, qkv):
    t = qkv.shape[2]

    def body(do_ref, dl_ref, q_ref, k_ref, v_ref, dq_ref, dk_ref, dv_ref):
        for ref in (dq_ref, dk_ref, dv_ref):
            ref[...] = jnp.zeros_like(ref)

        def add(ref, rows, val):
            for g in range(ATTN_GROUP):
                ref.at[g][rows, :] += val[g]

        def unit(di, d, r, n, has_prev):
            cur = _dilated_rows(d, r, n)
            args = [_take(ref, (0,), cur) for ref in (q_ref, k_ref, v_ref)]
            if has_prev:
                prv = _dilated_rows(d, r, n - 1)
                args += [_take(ref, (0,), prv) for ref in (k_ref, v_ref)]
            _, vjp = jax.vjp(_attn_block_fn, *args)
            res = vjp((_take(do_ref, (di,), cur), _take(dl_ref, (di,), cur)))
            add(dq_ref, cur, res[0])
            add(dk_ref, cur, res[1])
            add(dv_ref, cur, res[2])
            if has_prev:
                add(dk_ref, prv, res[3])
                add(dv_ref, prv, res[4])

        _for_each_sequence(t, unit)

    spec = lambda j: pl.BlockSpec((1, ATTN_GROUP, t, 128), lambda i: (j, i, 0, 0))
    three = pl.BlockSpec((3, ATTN_GROUP, t, 128), lambda i: (0, i, 0, 0))
    out = pl.BlockSpec((ATTN_GROUP, t, 128), lambda i: (i, 0, 0))
    return pl.pallas_call(
        body, name="attn_bwd", grid=(N_PAIR // ATTN_GROUP,),
        in_specs=[three, three, spec(0), spec(1), spec(2)], out_specs=[out] * 3,
        out_shape=[jax.ShapeDtypeStruct((N_PAIR, t, 128), F32)] * 3,
        compiler_params=_params(("parallel",)),
    )(do, dl, qkv, qkv, qkv)


def _in_proj_bwd(dpa, dq, dk, dv, win, x, g1, dx1):
    t = x.shape[0]
    tm = 256

    def body(dpa_ref, dq_ref, dk_ref, dv_ref, w_ref, x_ref, g_ref, dx1_ref, dproj_ref, dx_ref, dg_ref):
        parts = [dpa_ref[...]] + [ref[p] for ref in (dq_ref, dk_ref, dv_ref) for p in range(N_PAIR)]
        dproj = jnp.concatenate([z.astype(BF16) for z in parts], axis=1)
        dproj_ref[...] = dproj
        dh = _dot_nt(dproj, w_ref[...])
        dxn, dgr = _rms_bwd(dh, x_ref[...], g_ref[...])
        dx_ref[...] = dx1_ref[...] + dxn
        _acc(dg_ref, _colsum8(dgr), pl.program_id(0) == 0)

    row = pl.BlockSpec((tm, D_MODEL), lambda i: (i, 0))
    pair = pl.BlockSpec((N_PAIR, tm, 128), lambda i: (0, i, 0))
    return pl.pallas_call(
        body, name="in_proj_bwd", grid=(t // tm,),
        in_specs=[pl.BlockSpec((tm, SHIFT_COLS), lambda i: (i, 0))] + [pair] * 3
                 + [pl.BlockSpec((D_MODEL, IN_COLS), lambda i: (0, 0)), row, pl.BlockSpec((1, D_MODEL), lambda i: (0, 0)), row],
        out_specs=[pl.BlockSpec((tm, IN_COLS), lambda i: (i, 0)), row, pl.BlockSpec((8, D_MODEL), lambda i: (0, 0))],
        out_shape=[jax.ShapeDtypeStruct((t, IN_COLS), BF16), jax.ShapeDtypeStruct((t, D_MODEL), F32),
                   jax.ShapeDtypeStruct((8, D_MODEL), F32)],
        compiler_params=_params(("arbitrary",)),
    )(dpa, dq, dk, dv, win, x, g1, dx1)


def _pad_lora(w, lo):
    z = jnp.zeros((64, RW), F32)
    return jnp.concatenate([w, z], axis=0) if lo == 0 else jnp.concatenate([z, w], axis=0)


def _local_step(x, tgt, win, vecs, w2, a2, g2m, get_rest, send_rest):
    pw = (vecs["mu_shift"], vecs["decay_w0"], _pad_lora(w2, 0), vecs["iclr_a0"], _pad_lora(a2, 64), g2m,
          vecs["k_k"], vecs["k_a"])
    h, proj, qkv = _in_proj(x, vecs["mix_norm_g"], win)
    r, lw, k2, v, kk, a, g = _prep_fwd(proj, pw)
    y, s0s = _wkv_fwd(r, lw, k2, v, kk, a)
    ya = _post_fwd(y, r, k2, v, g, vecs["ln_x_w"], vecs["ln_x_b"], vecs["r_k"])
    o_att, l_att = _attn_fwd(qkv)
    yb = _combine_fwd(o_att, l_att, vecs["attn_out_g"])

    wout, wg, wu, wd = get_rest(yb)
    ycat = jnp.concatenate([ya, yb], axis=1)
    x1, h2 = _out_proj(x, ycat, wout, vecs["ffn_norm_g"])
    gt, up, act = _ffn_up(h2, wg, wu)
    dx2, dx2b, loss8, dgf = _ffn_down_loss(x1, act, wd, vecs["final_norm_g"], tgt)

    dgt, dup = _ffn_bwd_act(dx2b, wd, gt, up)
    dx1, dx1b, dya, dyb, dg2n = _ffn_bwd_h(dgt, dup, wg, wu, dx2, x1, vecs["ffn_norm_g"], wout)
    gw = {
        "w_down": _wgrad(act, dx2b, 1408, 1024, "wgrad_down"),
        "w_gate": _wgrad(h2, dgt, 1024, 1408, "wgrad_gate"),
        "w_up": _wgrad(h2, dup, 1024, 1408, "wgrad_up"),
        "w_out": _wgrad(ycat, dx1b, 1024, 1024, "wgrad_out"),
    }

    lnw = vecs["ln_x_w"] + send_rest(gw)[0, 0]
    dy, dr_p, dk2_p, dv_p, dg, dlnw, dlnb, drk = _post_bwd(dya, y, r, k2, v, g, lnw, vecs["ln_x_b"], vecs["r_k"])
    dr_s, dlw, dk2_s, dv_s, dkk, da = _wkv_bwd(dy, s0s, r, lw, k2, v, kk, a)
    dpa, dmu, dw0, dw2p, da0, da2p, dg2m, dk_k, dk_a = _prep_bwd(
        proj, pw, (dr_p, dr_s, dlw, dk2_p, dk2_s, dv_p, dv_s, dkk, da, dg))

    do_att, dl_att, dog = _combine_bwd(dyb, o_att, l_att, vecs["attn_out_g"])
    dq, dk, dv = _attn_bwd(do_att, dl_att, qkv)
    dproj, dx, dg1 = _in_proj_bwd(dpa, dq, dk, dv, win, x, vecs["mix_norm_g"], dx1)
    gw["w_in"] = _wgrad(h, dproj, 1024, 1664, "wgrad_in")
    gw["decay_w2"] = dw2p[:64]
    gw["iclr_a2"] = da2p[64:]
    gw["gate_g2"] = dg2m
    gv = {"mix_norm_g": dg1, "mu_shift": dmu, "decay_w0": dw0, "iclr_a0": da0, "k_k": dk_k, "k_a": dk_a, "r_k": drk,
          "ln_x_w": dlnw, "ln_x_b": dlnb, "attn_out_g": dog, "ffn_norm_g": dg2n, "final_norm_g": dgf}
    return loss8, dx, gw, gv


N_CHIP = 4
N_DEV = 8
MATS = ("w_in", "w_out", "w_gate", "w_up", "w_down")
LORAS = ("decay_w2", "iclr_a2", "gate_g2")
VECS = (("mix_norm_g", 1024), ("mu_shift", 1792), ("decay_w0", 512), ("iclr_a0", 512), ("k_k", 512), ("k_a", 512),
        ("r_k", 512), ("ln_x_w", 512), ("ln_x_b", 512), ("attn_out_g", 512), ("ffn_norm_g", 1024),
        ("final_norm_g", 1024))
N_VEC = sum(n for _, n in VECS)
N_SMALL = N_VEC + 128
ROWS_PAD = 3328
ANY = pl.BlockSpec(memory_space=pl.ANY)


def _flip(v, f):
    return 1 - v if f else v


def _plan(mode):
    x, y, c = lax.axis_index("x"), lax.axis_index("y"), lax.axis_index("c")
    if mode == "chips":
        return 2 * x + y, [((px, py, c), 2 * px + py, 2 * x + y, 2 * px + py) for px, py in ((1 - x, y), (x, 1 - y), (1 - x, 1 - y))]
    peers = [(_flip(x, k & 4), _flip(y, k & 2), _flip(c, k & 1)) for k in range(1, N_DEV)]
    return 2 * x + y, [((px, py, pc), 2 * px + py, 4 * x + 2 * y + c, 4 * px + 2 * py + pc) for px, py, pc in peers]


def _peer_copy(srcs, dsts, sliced, send_sems, recv_sems, n, peer, j, i, incoming):
    dev, chip, out_slot, in_slot = peer
    src = srcs[i].at[chip] if sliced[i] else srcs[i]
    return pltpu.make_async_remote_copy(src_ref=src, dst_ref=dsts[i].at[in_slot if incoming else out_slot],
                                        send_sem=send_sems.at[n * j + i], recv_sem=recv_sems.at[n * j + i],
                                        device_id=dev, device_id_type=MESH)


def _slots(mode):
    return N_CHIP if mode == "chips" else N_DEV


def _piece(a, sliced):
    return a.shape[1:] if sliced else a.shape


def _swap_blocking(arrs, sliced, mode, name):
    n = len(arrs)

    def body(*refs):
        srcs, dsts = refs[:n], refs[n:2 * n]
        send_sems, recv_sems, local_sems = refs[2 * n:]
        my_chip, peers = _plan(mode)
        own_slot = peers[0][2]
        local = [pltpu.make_async_copy(srcs[i].at[my_chip] if sliced[i] else srcs[i], dsts[i].at[own_slot], local_sems.at[i])
                 for i in range(n)]
        for cp in local:
            cp.start()
        sends = [_peer_copy(srcs, dsts, sliced, send_sems, recv_sems, n, peer, j, i, False)
                 for j, peer in enumerate(peers) for i in range(n)]
        for cp in sends:
            cp.start()
        for j, peer in enumerate(peers):
            for i in range(n):
                _peer_copy(srcs, dsts, sliced, send_sems, recv_sems, n, peer, j, i, True).wait_recv()
        for cp in sends:
            cp.wait_send()
        for cp in local:
            cp.wait()

    npeer = _slots(mode) - 1
    return pl.pallas_call(
        body, name=name, in_specs=[ANY] * n, out_specs=[ANY] * n,
        out_shape=[jax.ShapeDtypeStruct((_slots(mode),) + _piece(a, s), a.dtype) for a, s in zip(arrs, sliced)],
        scratch_shapes=[pltpu.SemaphoreType.DMA((npeer * n,)), pltpu.SemaphoreType.DMA((npeer * n,)),
                        pltpu.SemaphoreType.DMA((n,))],
    )(*arrs)


HBM = pl.BlockSpec(memory_space=pltpu.HBM)
SEM = pl.BlockSpec(memory_space=pltpu.SEMAPHORE)
EFFECT = pltpu.SideEffectType.DATAFLOW_SIDE_EFFECTING


def _swap_start(arrs, lands, sliced, mode, name):
    n = len(arrs)

    def body(*refs):
        srcs, dsts, send_sems, recv_sems, token = refs[:n], refs[n:2 * n], refs[2 * n], refs[2 * n + 1], refs[-1]
        _, peers = _plan(mode)
        for j, peer in enumerate(peers):
            for i in range(n):
                _peer_copy(srcs, dsts, sliced, send_sems, recv_sems, n, peer, j, i, False).start()
        token[...] = jnp.zeros_like(token)

    ns = (_slots(mode) - 1) * n
    outs = pl.pallas_call(
        body, name=name,
        out_shape=(pltpu.SemaphoreType.DMA((ns,)), pltpu.SemaphoreType.DMA((ns,)),
                   *[pltpu.HBM(a.shape, a.dtype) for a in arrs], *[pltpu.HBM(l.shape, l.dtype) for l in lands],
                   jax.ShapeDtypeStruct((8, 128), F32)),
        in_specs=[HBM] * (2 * n), out_specs=(SEM, SEM, *[HBM] * (2 * n), pl.BlockSpec(memory_space=pltpu.VMEM)),
        input_output_aliases={k: 2 + k for k in range(2 * n)},
        compiler_params=pltpu.CompilerParams(has_side_effects=EFFECT),
    )(*[pltpu.with_memory_space_constraint(a, pltpu.HBM) for a in arrs],
      *[pltpu.with_memory_space_constraint(l, pltpu.HBM) for l in lands])
    return outs[0], outs[1], outs[2:2 + n], outs[2 + n:2 + 2 * n], outs[-1]


def _swap_wait(send_sems, recv_sems, srcs_thru, lands_thru, after, sliced, mode, name):
    n = len(srcs_thru)

    def body(*refs):
        srcs, dsts, s_sems, r_sems = refs[:n], refs[n:2 * n], refs[2 * n], refs[2 * n + 1]
        _, peers = _plan(mode)
        for j, peer in enumerate(peers):
            for i in range(n):
                cp = _peer_copy(srcs, dsts, sliced, s_sems, r_sems, n, peer, j, i, True)
                cp.wait_send()
                cp.wait_recv()

    outs = pl.pallas_call(
        body, name=name,
        out_shape=tuple(pltpu.HBM(a.shape, a.dtype) for a in (*srcs_thru, *lands_thru)),
        in_specs=[HBM] * (2 * n) + [SEM, SEM, ANY], out_specs=tuple([HBM] * (2 * n)),
        input_output_aliases={k: k for k in range(2 * n)},
        compiler_params=pltpu.CompilerParams(has_side_effects=EFFECT),
    )(*srcs_thru, *lands_thru, send_sems, recv_sems, after)
    return outs[n:]


def _adamw(w, g, m, v):
    m = ADAM_B1 * m + (1.0 - ADAM_B1) * g
    v = ADAM_B2 * v + (1.0 - ADAM_B2) * (g * g)
    m_hat = m / (1.0 - ADAM_B1 ** ADAM_STEP)
    v_hat = v / (1.0 - ADAM_B2 ** ADAM_STEP)
    delta = -ADAM_LR * (m_hat / (jnp.sqrt(v_hat) + ADAM_EPS) + ADAM_WD * w)
    return delta, m, v


def _reduce_adamw(rbuf, w, m, v, tr, name):
    _, rows, cols = w.shape

    def body(r_ref, w_ref, m_ref, v_ref, g_ref, d_ref, nm_ref, nv_ref):
        g = r_ref[0].astype(F32)
        for s in range(1, N_DEV):
            g = g + r_ref[s].astype(F32)
        g_ref[0] = g
        d_ref[0], nm_ref[0], nv_ref[0] = _adamw(w_ref[0], g, m_ref[0], v_ref[0])

    row = pl.BlockSpec((1, tr, cols), lambda i: (0, i, 0))
    return pl.pallas_call(
        body, name=name, grid=(rows // tr,),
        in_specs=[pl.BlockSpec((N_DEV, tr, cols), lambda i: (0, i, 0)), row, row, row], out_specs=[row] * 4,
        out_shape=[jax.ShapeDtypeStruct(w.shape, F32)] * 4,
        compiler_params=_params(("parallel",)),
    )(rbuf, w, m, v)


def _reduce_adamw_small(sbuf, w, m, v):
    def body(s_ref, w_ref, m_ref, v_ref, g_ref, d_ref, nm_ref, nv_ref, loss_ref):
        tot = s_ref[0]
        for s in range(1, N_DEV):
            tot = tot + s_ref[s]
        tot = jnp.sum(tot, axis=0, keepdims=True)
        g = tot[:, :N_VEC]
        g_ref[...] = g
        d_ref[...], nm_ref[...], nv_ref[...] = _adamw(w_ref[...], g, m_ref[...], v_ref[...])
        loss_ref[...] = tot[:, N_VEC:]

    return pl.pallas_call(
        body, name="reduce_adamw_small",
        out_shape=[jax.ShapeDtypeStruct((1, N_VEC), F32)] * 4 + [jax.ShapeDtypeStruct((1, 128), F32)],
    )(sbuf, w, m, v)


_ROW_SHARDED = ("w_out", "w_down")
_ADAM_TILE = {"w_in": 256, "w_out": 256, "w_gate": 256, "w_up": 256, "w_down": 176, "decay_w2": 64, "iclr_a2": 64,
              "gate_g2": 128}


def _full(n, stacked):
    p, r, c = stacked.shape
    if n in _ROW_SHARDED:
        return stacked.reshape(p * r, c)
    return jnp.transpose(stacked, (1, 0, 2)).reshape(r, p * c)


def _by_chip(n, full):
    if n in _ROW_SHARDED:
        return full.reshape(N_CHIP, full.shape[0] // N_CHIP, full.shape[1])
    r, c = full.shape
    return jnp.transpose(full.reshape(r, N_CHIP, c // N_CHIP), (1, 0, 2))


def _with_own(land_shape, dtype, own, slot):
    return lax.dynamic_update_slice(lax.empty(land_shape, dtype), own[None], (slot,) + (0,) * own.ndim)


def kernel(x, mix_norm_g, w_in, mu_shift, decay_w0, decay_w2, iclr_a0, iclr_a2, gate_g2, k_k, k_a, r_k, ln_x_w, ln_x_b, attn_out_g, w_out, ffn_norm_g, w_gate, w_up, w_down, final_norm_g, loss_target, m_mix_norm_g, m_w_in, m_mu_shift, m_decay_w0, m_decay_w2, m_iclr_a0, m_iclr_a2, m_gate_g2, m_k_k, m_k_a, m_r_k, m_ln_x_w, m_ln_x_b, m_attn_out_g, m_w_out, m_ffn_norm_g, m_w_gate, m_w_up, m_w_down, m_final_norm_g, v_mix_norm_g, v_w_in, v_mu_shift, v_decay_w0, v_decay_w2, v_iclr_a0, v_iclr_a2, v_gate_g2, v_k_k, v_k_a, v_r_k, v_ln_x_w, v_ln_x_b, v_attn_out_g, v_w_out, v_ffn_norm_g, v_w_gate, v_w_up, v_w_down, v_final_norm_g):
    names = ("mix_norm_g", "w_in", "mu_shift", "decay_w0", "decay_w2", "iclr_a0", "iclr_a2", "gate_g2", "k_k", "k_a",
             "r_k", "ln_x_w", "ln_x_b", "attn_out_g", "w_out", "ffn_norm_g", "w_gate", "w_up", "w_down", "final_norm_g")
    w = dict(zip(names, (mix_norm_g, w_in, mu_shift, decay_w0, decay_w2, iclr_a0, iclr_a2, gate_g2, k_k, k_a, r_k,
                         ln_x_w, ln_x_b, attn_out_g, w_out, ffn_norm_g, w_gate, w_up, w_down, final_norm_g)))
    m = dict(zip(names, (m_mix_norm_g, m_w_in, m_mu_shift, m_decay_w0, m_decay_w2, m_iclr_a0, m_iclr_a2, m_gate_g2,
                         m_k_k, m_k_a, m_r_k, m_ln_x_w, m_ln_x_b, m_attn_out_g, m_w_out, m_ffn_norm_g, m_w_gate,
                         m_w_up, m_w_down, m_final_norm_g)))
    v = dict(zip(names, (v_mix_norm_g, v_w_in, v_mu_shift, v_decay_w0, v_decay_w2, v_iclr_a0, v_iclr_a2, v_gate_g2,
                         v_k_k, v_k_a, v_r_k, v_ln_x_w, v_ln_x_b, v_attn_out_g, v_w_out, v_ffn_norm_g, v_w_gate,
                         v_w_up, v_w_down, v_final_norm_g)))
    first = ("w_in",) + LORAS
    rest = ("w_out", "w_gate", "w_up", "w_down")
    xi, yi, ci = lax.axis_index("x"), lax.axis_index("y"), lax.axis_index("c")
    my_chip, my_dev = 2 * xi + yi, 4 * xi + 2 * yi + ci
    whole, sliced = (False,) * 4, (True,) * 4

    wb = {n: w[n][0].astype(BF16) for n in MATS}
    lands = [_with_own((N_CHIP,) + wb[n].shape, BF16, wb[n], my_chip) for n in rest]
    ssem, rsem, srcs_thru, lands_thru, tok = _swap_start([wb[n] for n in rest], lands, whole, "chips", "gather_rest_start")
    got = _swap_blocking([wb["w_in"]] + [w[n][0] for n in LORAS], whole, "chips", "gather_first")
    win, w2, a2, g2m = (_full(n, z) for n, z in zip(first, got))

    vecs = {n: w[n].reshape(1, sz) for n, sz in VECS}
    vecs["mix_norm_g"] = vecs["mix_norm_g"] + tok[0, 0]

    def get_rest(after):
        full = _swap_wait(ssem, rsem, srcs_thru, lands_thru, after, whole, "chips", "gather_rest_wait")
        return [_full(n, z) for n, z in zip(rest, full)]

    flight = []

    def send_rest(gw):
        gs = [_by_chip(n, gw[n]).astype(BF16) for n in rest]
        own = [lax.dynamic_index_in_dim(g, my_chip, 0, keepdims=False) for g in gs]
        into = [_with_own((N_DEV,) + o.shape, BF16, o, my_dev) for o in own]
        flight.extend(_swap_start(gs, into, sliced, "devs", "exchange_rest_start"))
        return flight[4]

    loss8, dx, gw, gv = _local_step(x[0], loss_target[0], win, vecs, w2, a2, g2m, get_rest, send_rest)

    small = jnp.concatenate([gv[n] for n, _ in VECS] + [loss8], axis=1)
    got = _swap_blocking([_by_chip(n, gw[n]).astype(BF16) for n in first] + [small], sliced + (False,), "devs",
                         "exchange_first")
    rbuf = dict(zip(first, got[:4]))
    rbuf.update(zip(rest, _swap_wait(flight[0], flight[1], flight[2], flight[3], got[0], sliced, "devs",
                                     "exchange_rest_wait")))

    res = {n: _reduce_adamw(rbuf[n], w[n], m[n], v[n], _ADAM_TILE[n], "adamw_" + n) for n in MATS + LORAS}
    cat = lambda d: jnp.concatenate([d[n].reshape(1, sz) for n, sz in VECS], axis=1)
    small_res = _reduce_adamw_small(got[4], cat(w), cat(m), cat(v))

    outs = []
    for k in range(4):
        piece = {n: r[k] for n, r in res.items()}
        c0 = 0
        for n, sz in VECS:
            piece[n] = small_res[k][0, c0:c0 + sz].reshape(w[n].shape)
            c0 += sz
        outs.extend(piece[n] for n in names)
    return (small_res[4][0, 0], dx[None], *outs)
```

```python
import jax
import jax.numpy as jnp
from jax import lax
from jax.experimental import pallas as pl
from jax.experimental.pallas import tpu as pltpu

F32 = jnp.float32
BF16 = jnp.bfloat16
HI = lax.Precision.HIGHEST

D_MODEL = 1024
HEAD_DIM = 64
RW = 512
N_PAIR = RW // 128
SHIFT_COLS = 1792
IN_COLS = 3328
D_FF = 2816
NORM_EPS = 1e-6
GN_EPS = 64e-5
CHUNK = 64
SUB = 16
WKV_PASSES = 1
ATTN_PASSES = 1
ATTN_BLOCK = 128
DILATIONS = (1, 4, 16)
NEG = -1e30
ADAM_LR, ADAM_B1, ADAM_B2, ADAM_EPS, ADAM_WD, ADAM_STEP = 0.001, 0.9, 0.999, 1e-08, 0.01, 10
VMEM_LIMIT = 56 * 1024 * 1024
MESH = pl.DeviceIdType.MESH


def _params(sem=None, **kw):
    return pltpu.CompilerParams(dimension_semantics=sem, vmem_limit_bytes=VMEM_LIMIT, **kw)


def _dot(a, b, prec=None):
    return lax.dot_general(a, b, (((1,), (0,)), ((), ())), preferred_element_type=F32, precision=prec)


def _dot_nt(a, b, prec=None):
    return lax.dot_general(a, b, (((1,), (1,)), ((), ())), preferred_element_type=F32, precision=prec)


def _dot_tn(a, b, prec=None):
    return lax.dot_general(a, b, (((0,), (0,)), ((), ())), preferred_element_type=F32, precision=prec)


_FORMS = {"nn": ((1,), (0,)), "nt": ((1,), (1,)), "tn": ((0,), (0,))}


def _dg(a, b, form):
    if a.ndim == 3 or b.ndim == 3:
        nb = a.shape[0] if a.ndim == 3 else b.shape[0]
        return jnp.stack([_dg(a[i] if a.ndim == 3 else a, b[i] if b.ndim == 3 else b, form) for i in range(nb)], axis=0)
    return lax.dot_general(a, b, (_FORMS[form], ((), ())), preferred_element_type=F32)


def _split2(x):
    hi = x.astype(BF16)
    return hi, (x - hi.astype(F32)).astype(BF16)


def _split3(x):
    hi = x.astype(BF16)
    rest = x - hi.astype(F32)
    mid = rest.astype(BF16)
    return hi, mid, (rest - mid.astype(F32)).astype(BF16)


def _mm_raw(a, b, form, mode):
    if mode == 1:
        return _dg(a.astype(BF16), b.astype(BF16), form)
    if mode == 3:
        ah, al = _split2(a)
        bh, bl = _split2(b)
        return _dg(ah, bh, form) + (_dg(ah, bl, form) + _dg(al, bh, form))
    if mode == "L3":
        ab = a.astype(BF16)
        b1, b2, b3 = _split3(b)
        return _dg(ab, b1, form) + (_dg(ab, b2, form) + _dg(ab, b3, form))
    assert mode == "R3", mode
    bb = b.astype(BF16)
    a1, a2, a3 = _split3(a)
    return _dg(a1, bb, form) + (_dg(a2, bb, form) + _dg(a3, bb, form))


def _mm(a, b, form, mode):
    @jax.custom_vjp
    def f(a, b):
        return _mm_raw(a, b, form, mode)

    def fwd(a, b):
        return _mm_raw(a, b, form, mode), (a, b)

    def bwd(res, ct):
        a, b = res
        la = {1: 1, 3: 3, "L3": None, "R3": "R3"}[mode]
        lb = {1: 1, 3: 3, "L3": "L3", "R3": None}[mode]
        if form == "nn":
            da = None if la is None else _mm_raw(ct, b, "nt", la)
            db = None if lb is None else _mm_raw(a, ct, "tn", lb)
        elif form == "nt":
            da = None if la is None else _mm_raw(ct, b, "nn", la)
            db = None if lb is None else _mm_raw(ct, a, "tn", "R3" if lb == "L3" else lb)
        else:
            da = None if la is None else _mm_raw(b, ct, "nt", "L3" if la == "R3" else la)
            db = None if lb is None else _mm_raw(a, ct, "nn", lb)
        return (jnp.zeros_like(a) if da is None else da, jnp.zeros_like(b) if db is None else db)

    f.defvjp(fwd, bwd)
    return f(a, b)


def _seg_ones(n):
    r = lax.broadcasted_iota(jnp.int32, (n, n), 0) // HEAD_DIM
    c = lax.broadcasted_iota(jnp.int32, (n, n), 1) // HEAD_DIM
    return (r == c).astype(F32)


def _segsum(x, seg):
    return _mm(x, seg, "nn", "R3")


def _rms_fwd(x, g):
    rstd = lax.rsqrt(jnp.mean(x * x, axis=-1, keepdims=True) + NORM_EPS)
    return x * rstd * g


def _rms_bwd(dy, x, g):
    rstd = lax.rsqrt(jnp.mean(x * x, axis=-1, keepdims=True) + NORM_EPS)
    xn = x * rstd
    dxn = dy * g
    dx = rstd * (dxn - xn * jnp.mean(dxn * xn, axis=-1, keepdims=True))
    return dx, dy * xn


def _sigmoid(x):
    return 1.0 / (1.0 + jnp.exp(-x))


def _softplus(x):
    return jnp.maximum(x, 0.0) + jnp.log(1.0 + jnp.exp(-jnp.abs(x)))


def _acc(ref, val, first):
    @pl.when(first)
    def _():
        ref[...] = val

    @pl.when(jnp.logical_not(first))
    def _():
        ref[...] += val


def _colsum8(v):
    rows, n = v.shape
    return jnp.sum(v.reshape(rows // 8, 8, n), axis=0)


def _prep_fn(p, pprev, mu, w0, w2p, a0, a2p, g2, k_k, k_a):
    seg = _seg_ones(RW)
    ps = p + (pprev - p) * mu
    r = ps[:, 0:RW]
    k = ps[:, RW:2 * RW]
    v = ps[:, 2 * RW:3 * RW]
    xwa = ps[:, 3 * RW:3 * RW + 128]
    xg = ps[:, 3 * RW + 128:3 * RW + 256]
    wraw = -_softplus(-(w0 + _mm(jnp.tanh(xwa), w2p, "nn", 3))) - 0.5
    lw = -jnp.exp(wraw)
    a = _sigmoid(a0 + _mm(xwa, a2p, "nn", 3))
    g = _mm(_sigmoid(xg), g2, "nn", 3)
    kk = k * k_k
    kk = kk / jnp.maximum(jnp.sqrt(_segsum(kk * kk, seg)), 1e-12)
    k2 = k * (1.0 + (a - 1.0) * k_a)
    return r, lw, k2, v, kk, a, g


def _solve_unit_lower(lmat, rhs):
    c = lmat.shape[-1]
    row = lax.broadcasted_iota(jnp.int32, (c, c), 0)
    col = lax.broadcasted_iota(jnp.int32, (c, c), 1)
    eye = (row == col).astype(F32)
    ld = jnp.where(row // SUB == col // SUB, lmat, 0.0)
    lo = lmat - ld
    x = eye + ld
    m = ld
    mm = lambda p, q: _mm(p, q, "nn", WKV_PASSES)
    for _ in range(3):
        m = mm(m, m)
        x = x + mm(x, m)
    g = mm(x, lo)
    g2 = mm(g, g)
    w = mm(x, rhs)
    w = w + mm(g2, w)
    return w + mm(g, w)


def _wkv_chunk_fn(s0, r, lw, k, v, kk, a):
    c = r.shape[-2]
    n = 2 * c
    row = lax.broadcasted_iota(jnp.int32, (n, n), 0)
    col = lax.broadcasted_iota(jnp.int32, (n, n), 1)
    same = (row // c) == (col // c)
    incl = jnp.logical_and(row >= col, same)
    strict = jnp.logical_and(row > col, same)
    sel = (lax.broadcasted_iota(jnp.int32, (n, 128), 0) // c) == (lax.broadcasted_iota(jnp.int32, (n, 128), 1) // HEAD_DIM)
    two = lambda z: jnp.concatenate([z, z], axis=-2)
    lw2 = two(lw)
    mm = lambda p_, q_, form: _mm(p_, q_, form, WKV_PASSES)
    cl = _mm(incl.astype(F32), lw2, "nn", "L3")
    p = jnp.exp(cl)
    pinv = jnp.exp(-cl)
    pprev = jnp.exp(cl - lw2)
    kk2 = two(kk)
    at = jnp.where(sel, -kk2 * pprev, 0.0)
    bt = jnp.where(sel, kk2 * two(a) * pinv, 0.0)
    kt = jnp.where(sel, two(k) * pinv, 0.0)
    rt = jnp.where(sel, two(r) * p, 0.0)
    vt = jnp.where(sel, two(v), 0.0)
    ab = jnp.where(strict, mm(at, bt, "nt"), 0.0)
    ak = jnp.where(strict, mm(at, kt, "nt"), 0.0)
    rb = jnp.where(incl, mm(rt, bt, "nt"), 0.0)
    rk = jnp.where(incl, mm(rt, kt, "nt"), 0.0)
    u = _solve_unit_lower(ab, mm(at, s0, "nt") + mm(ak, vt, "nn"))
    y2 = mm(rt, s0, "nt") + mm(rb, u, "nn") + mm(rk, vt, "nn")
    plast = jnp.exp(jnp.sum(lw, axis=-2, keepdims=True))
    s1 = (s0 + mm(u, bt, "tn") + mm(vt, kt, "tn")) * plast
    r2 = lax.broadcasted_iota(jnp.int32, (128, 128), 0) // HEAD_DIM
    c2 = lax.broadcasted_iota(jnp.int32, (128, 128), 1) // HEAD_DIM
    return y2[..., :c, :] + y2[..., c:, :], jnp.where(r2 == c2, s1, 0.0)


def _post_fn(y, r, k2, v, g, lnw, lnb, rk):
    seg = _seg_ones(RW)
    mean = _segsum(y, seg) * (1.0 / HEAD_DIM)
    yc = y - mean
    var = _segsum(yc * yc, seg) * (1.0 / HEAD_DIM)
    yn = yc * lax.rsqrt(var + GN_EPS)
    out = yn * lnw + lnb + _segsum(r * k2 * rk, seg) * v
    return out * g


def _attn_block_fn(q, kc, vc, kp=None, vp=None):
    n = ATTN_BLOCK
    qi = lax.broadcasted_iota(jnp.int32, (n, n), 0)
    kj = lax.broadcasted_iota(jnp.int32, (n, n), 1)
    lane = lax.broadcasted_iota(jnp.int32, (1, 128), 1)
    scale = HEAD_DIM ** -0.5
    os_, ls_ = [], []
    for h in range(2):
        mh = (lane // HEAD_DIM) == h
        qh = jnp.where(mh, q, 0.0)
        sc = jnp.where(kj <= qi, _mm(qh, kc, "nt", ATTN_PASSES) * scale, NEG)
        m = jnp.max(sc, axis=-1, keepdims=True)
        if kp is not None:
            sp = jnp.where(kj >= qi, _mm(qh, kp, "nt", ATTN_PASSES) * scale, NEG)
            m = jnp.maximum(m, jnp.max(sp, axis=-1, keepdims=True))
        pc = jnp.exp(sc - m)
        den = jnp.sum(pc, axis=-1, keepdims=True)
        num = _mm(pc, vc, "nn", ATTN_PASSES)
        if kp is not None:
            pp = jnp.exp(sp - m)
            den = den + jnp.sum(pp, axis=-1, keepdims=True)
            num = num + _mm(pp, vp, "nn", ATTN_PASSES)
        os_.append(num / den)
        ls_.append(m + jnp.log(den))
    m0 = (lane // HEAD_DIM) == 0
    return jnp.where(m0, os_[0], os_[1]), jnp.where(m0, ls_[0], ls_[1])


def _combine_fn(o1, o2, o3, l1, l2, l3, og):
    seg = _seg_ones(o1.shape[-1])
    m = jnp.maximum(jnp.maximum(l1, l2), l3)
    e1, e2, e3 = jnp.exp(l1 - m), jnp.exp(l2 - m), jnp.exp(l3 - m)
    o = (e1 * o1 + e2 * o2 + e3 * o3) / (e1 + e2 + e3)
    o = o * lax.rsqrt(_segsum(o * o, seg) * (1.0 / HEAD_DIM) + NORM_EPS)
    return o * og


def _in_proj(x, g1, win):
    t = x.shape[0]
    tm = 256

    def body(x_ref, g_ref, w_ref, h_ref, pa_ref, qkv_ref):
        h = _rms_fwd(x_ref[...], g_ref[...]).astype(BF16)
        h_ref[...] = h
        proj = _dot(h, w_ref[...])
        pa_ref[...] = proj[:, :SHIFT_COLS]
        for j in range(3):
            for p in range(N_PAIR):
                c0 = SHIFT_COLS + j * RW + p * 128
                qkv_ref[j, p] = proj[:, c0:c0 + 128]

    return pl.pallas_call(
        body, name="in_proj", grid=(t // tm,),
        in_specs=[pl.BlockSpec((tm, D_MODEL), lambda i: (i, 0)), pl.BlockSpec((1, D_MODEL), lambda i: (0, 0)),
                  pl.BlockSpec((D_MODEL, IN_COLS), lambda i: (0, 0))],
        out_specs=[pl.BlockSpec((tm, D_MODEL), lambda i: (i, 0)), pl.BlockSpec((tm, SHIFT_COLS), lambda i: (i, 0)),
                   pl.BlockSpec((3, N_PAIR, tm, 128), lambda i: (0, 0, i, 0))],
        out_shape=[jax.ShapeDtypeStruct((t, D_MODEL), BF16), jax.ShapeDtypeStruct((t, SHIFT_COLS), F32),
                   jax.ShapeDtypeStruct((3, N_PAIR, t, 128), F32)],
        compiler_params=_params(("parallel",)),
    )(x, g1, win)


def _shifted(p, last8, first):
    prow = jnp.where(first, 0.0, last8[7:8, :])
    rolled = pltpu.roll(p, 1, axis=0)
    rid = lax.broadcasted_iota(jnp.int32, p.shape, 0)
    return jnp.where(rid == 0, prow, rolled)


_PREP_TM = 256


def _prep_specs(tm):
    vec = lambda n: pl.BlockSpec((1, n), lambda i: (0, 0))
    mat = lambda r, n: pl.BlockSpec((r, n), lambda i: (0, 0))
    return [vec(SHIFT_COLS), vec(RW), mat(128, RW), vec(RW), mat(128, RW), mat(128, RW), vec(RW), vec(RW)]


def _prep_fwd(proj, pw):
    t = proj.shape[0]
    tm = _PREP_TM

    def body(p_ref, l8_ref, mu, w0, w2p, a0, a2p, g2, k_k, k_a, *outs):
        p = p_ref[...]
        pprev = _shifted(p, l8_ref[...], pl.program_id(0) == 0)
        res = _prep_fn(p, pprev, mu[...], w0[...], w2p[...], a0[...], a2p[...], g2[...], k_k[...], k_a[...])
        for o_ref, val in zip(outs, res):
            o_ref[...] = val

    row = pl.BlockSpec((tm, RW), lambda i: (i, 0))
    return pl.pallas_call(
        body, name="rwkv_prep", grid=(t // tm,),
        in_specs=[pl.BlockSpec((tm, SHIFT_COLS), lambda i: (i, 0)),
                  pl.BlockSpec((8, SHIFT_COLS), lambda i: (jnp.maximum(i * (tm // 8) - 1, 0), 0))] + _prep_specs(tm),
        out_specs=[row] * 7,
        out_shape=[jax.ShapeDtypeStruct((t, RW), F32)] * 7,
        compiler_params=_params(("parallel",)),
    )(proj, proj, *pw)


def _pairs(ref):
    return jnp.stack([ref[:, 128 * p:128 * (p + 1)] for p in range(N_PAIR)], axis=0)


def _wkv_fwd(r, lw, k2, v, kk, a):
    t = r.shape[0]
    nc = t // CHUNK

    def body(r_ref, lw_ref, k_ref, v_ref, kk_ref, a_ref, y_ref, s_ref, st):
        @pl.when(pl.program_id(0) == 0)
        def _():
            st[...] = jnp.zeros_like(st)

        s0 = st[...]
        s_ref[0] = s0
        y, s1 = _wkv_chunk_fn(s0, *[_pairs(ref) for ref in (r_ref, lw_ref, k_ref, v_ref, kk_ref, a_ref)])
        for p in range(N_PAIR):
            y_ref[:, 128 * p:128 * (p + 1)] = y[p]
        st[...] = s1

    blk = pl.BlockSpec((CHUNK, RW), lambda c: (c, 0))
    return pl.pallas_call(
        body, name="wkv_fwd", grid=(nc,),
        in_specs=[blk] * 6,
        out_specs=[blk, pl.BlockSpec((1, N_PAIR, 128, 128), lambda c: (c, 0, 0, 0))],
        out_shape=[jax.ShapeDtypeStruct((t, RW), F32), jax.ShapeDtypeStruct((nc, N_PAIR, 128, 128), F32)],
        scratch_shapes=[pltpu.VMEM((N_PAIR, 128, 128), F32)],
        compiler_params=_params(("arbitrary",)),
    )(r, lw, k2, v, kk, a)


_POST_TM = 256


def _post_fwd(y, r, k2, v, g, lnw, lnb, rk):
    t = y.shape[0]
    tm = _POST_TM

    def body(y_ref, r_ref, k_ref, v_ref, g_ref, lnw_ref, lnb_ref, rk_ref, o_ref):
        o_ref[...] = _post_fn(y_ref[...], r_ref[...], k_ref[...], v_ref[...], g_ref[...],
                              lnw_ref[...], lnb_ref[...], rk_ref[...]).astype(BF16)

    row = pl.BlockSpec((tm, RW), lambda i: (i, 0))
    vec = pl.BlockSpec((1, RW), lambda i: (0, 0))
    return pl.pallas_call(
        body, name="rwkv_post", grid=(t // tm,),
        in_specs=[row] * 5 + [vec] * 3, out_specs=row,
        out_shape=jax.ShapeDtypeStruct((t, RW), BF16),
        compiler_params=_params(("parallel",)),
    )(y, r, k2, v, g, lnw, lnb, rk)


ATTN_GROUP = 2


def _dilated_rows(d, r, n):
    if d == 1:
        return pl.ds(pl.multiple_of(n * ATTN_BLOCK, ATTN_BLOCK), ATTN_BLOCK)
    return pl.ds(r + n * (ATTN_BLOCK * d), ATTN_BLOCK, stride=d)


def _for_each_sequence(t, unit):
    for di, d in enumerate(DILATIONS):
        nb = t // (ATTN_BLOCK * d)

        def residue(r, carry, di=di, d=d, nb=nb):
            unit(di, d, r, 0, False)
            if nb > 1:
                lax.fori_loop(1, nb, lambda n, c: (unit(di, d, r, n, True), c)[1], 0)
            return carry

        if d == 1:
            residue(0, 0)
        else:
            lax.fori_loop(0, d, residue, 0)


def _take(ref, lead, rows):
    return jnp.stack([ref.at[(*lead, g)][rows, :] for g in range(ATTN_GROUP)], axis=0)


def _attn_fwd(qkv):
    t = qkv.shape[2]

    def body(q_ref, k_ref, v_ref, o_ref, l_ref):
        def unit(di, d, r, n, has_prev):
            cur = _dilated_rows(d, r, n)
            args = [_take(ref, (0,), cur) for ref in (q_ref, k_ref, v_ref)]
            if has_prev:
                prv = _dilated_rows(d, r, n - 1)
                args += [_take(ref, (0,), prv) for ref in (k_ref, v_ref)]
            o, lse = _attn_block_fn(*args)
            for g in range(ATTN_GROUP):
                o_ref.at[di, g][cur, :] = o[g]
                l_ref.at[di, g][cur, :] = lse[g]

        _for_each_sequence(t, unit)

    spec = lambda j: pl.BlockSpec((1, ATTN_GROUP, t, 128), lambda i: (j, i, 0, 0))
    out = pl.BlockSpec((3, ATTN_GROUP, t, 128), lambda i: (0, i, 0, 0))
    return pl.pallas_call(
        body, name="attn_fwd", grid=(N_PAIR // ATTN_GROUP,),
        in_specs=[spec(0), spec(1), spec(2)], out_specs=[out, out],
        out_shape=[jax.ShapeDtypeStruct((3, N_PAIR, t, 128), F32)] * 2,
        compiler_params=_params(("parallel",)),
    )(qkv, qkv, qkv)


_COMB_TM = 256


def _combine_fwd(o, l, og):
    t = o.shape[2]
    tm = _COMB_TM

    def body(o_ref, l_ref, og_ref, y_ref):
        for p in range(N_PAIR):
            cols = slice(128 * p, 128 * (p + 1))
            y_ref[:, cols] = _combine_fn(o_ref[0, p], o_ref[1, p], o_ref[2, p], l_ref[0, p], l_ref[1, p], l_ref[2, p],
                                         og_ref[:, cols]).astype(BF16)

    blk = pl.BlockSpec((3, N_PAIR, tm, 128), lambda i: (0, 0, i, 0))
    return pl.pallas_call(
        body, name="attn_combine", grid=(t // tm,),
        in_specs=[blk, blk, pl.BlockSpec((1, RW), lambda i: (0, 0))], out_specs=pl.BlockSpec((tm, RW), lambda i: (i, 0)),
        out_shape=jax.ShapeDtypeStruct((t, RW), BF16),
        compiler_params=_params(("parallel",)),
    )(o, l, og)


def _out_proj(x, ycat, wout, g2):
    t = x.shape[0]
    tm = 256

    def body(x_ref, y_ref, w_ref, g_ref, x1_ref, h_ref):
        x1 = x_ref[...] + _dot(y_ref[...], w_ref[...])
        x1_ref[...] = x1
        h_ref[...] = _rms_fwd(x1, g_ref[...]).astype(BF16)

    row = pl.BlockSpec((tm, D_MODEL), lambda i: (i, 0))
    return pl.pallas_call(
        body, name="out_proj", grid=(t // tm,),
        in_specs=[row, row, pl.BlockSpec((D_MODEL, D_MODEL), lambda i: (0, 0)), pl.BlockSpec((1, D_MODEL), lambda i: (0, 0))],
        out_specs=[row, row],
        out_shape=[jax.ShapeDtypeStruct((t, D_MODEL), F32), jax.ShapeDtypeStruct((t, D_MODEL), BF16)],
        compiler_params=_params(("parallel",)),
    )(x, ycat, wout, g2)


def _ffn_up(h2, wg, wu):
    t = h2.shape[0]
    tm = 256

    def body(h_ref, wg_ref, wu_ref, gt_ref, up_ref, act_ref):
        h = h_ref[...]
        gt = _dot(h, wg_ref[...])
        up = _dot(h, wu_ref[...])
        gt_ref[...] = gt
        up_ref[...] = up
        act_ref[...] = (gt * _sigmoid(gt) * up).astype(BF16)

    wide = pl.BlockSpec((tm, D_FF), lambda i: (i, 0))
    wsp = pl.BlockSpec((D_MODEL, D_FF), lambda i: (0, 0))
    return pl.pallas_call(
        body, name="ffn_up", grid=(t // tm,),
        in_specs=[pl.BlockSpec((tm, D_MODEL), lambda i: (i, 0)), wsp, wsp],
        out_specs=[wide, wide, wide],
        out_shape=[jax.ShapeDtypeStruct((t, D_FF), F32)] * 2 + [jax.ShapeDtypeStruct((t, D_FF), BF16)],
        compiler_params=_params(("parallel",)),
    )(h2, wg, wu)


def _ffn_down_loss(x1, act, wd, gf, tgt):
    t = x1.shape[0]
    tm = 256

    def body(x1_ref, a_ref, w_ref, g_ref, t_ref, dx_ref, dxb_ref, loss_ref, dg_ref):
        first = pl.program_id(0) == 0
        x2 = x1_ref[...] + _dot(a_ref[...], w_ref[...])
        g = g_ref[...]
        diff = _rms_fwd(x2, g) - t_ref[...]
        lrow = 0.5 * jnp.sum(_colsum8(diff * diff), axis=1, keepdims=True) * (1.0 / D_MODEL)
        _acc(loss_ref, jnp.broadcast_to(lrow, (8, 128)), first)
        dx2, dgr = _rms_bwd(diff * (1.0 / D_MODEL), x2, g)
        dx_ref[...] = dx2
        dxb_ref[...] = dx2.astype(BF16)
        _acc(dg_ref, _colsum8(dgr), first)

    row = pl.BlockSpec((tm, D_MODEL), lambda i: (i, 0))
    return pl.pallas_call(
        body, name="ffn_down_loss", grid=(t // tm,),
        in_specs=[row, pl.BlockSpec((tm, D_FF), lambda i: (i, 0)), pl.BlockSpec((D_FF, D_MODEL), lambda i: (0, 0)),
                  pl.BlockSpec((1, D_MODEL), lambda i: (0, 0)), row],
        out_specs=[row, row, pl.BlockSpec((8, 128), lambda i: (0, 0)), pl.BlockSpec((8, D_MODEL), lambda i: (0, 0))],
        out_shape=[jax.ShapeDtypeStruct((t, D_MODEL), F32), jax.ShapeDtypeStruct((t, D_MODEL), BF16),
                   jax.ShapeDtypeStruct((8, 128), F32), jax.ShapeDtypeStruct((8, D_MODEL), F32)],
        compiler_params=_params(("arbitrary",)),
    )(x1, act, wd, gf, tgt)


def _ffn_bwd_act(dx2b, wd, gt, up):
    t = dx2b.shape[0]
    tm = 256

    def body(dx_ref, w_ref, gt_ref, up_ref, dgt_ref, dup_ref):
        dact = _dot_nt(dx_ref[...], w_ref[...])
        gt = gt_ref[...]
        sg = _sigmoid(gt)
        dgt_ref[...] = (dact * up_ref[...] * sg * (1.0 + gt * (1.0 - sg))).astype(BF16)
        dup_ref[...] = (dact * gt * sg).astype(BF16)

    wide = pl.BlockSpec((tm, D_FF), lambda i: (i, 0))
    return pl.pallas_call(
        body, name="ffn_bwd_act", grid=(t // tm,),
        in_specs=[pl.BlockSpec((tm, D_MODEL), lambda i: (i, 0)), pl.BlockSpec((D_FF, D_MODEL), lambda i: (0, 0)), wide, wide],
        out_specs=[wide, wide],
        out_shape=[jax.ShapeDtypeStruct((t, D_FF), BF16)] * 2,
        compiler_params=_params(("parallel",)),
    )(dx2b, wd, gt, up)


def _ffn_bwd_h(dgt, dup, wg, wu, dx2, x1, g2, wout):
    t = dgt.shape[0]
    tm = 256

    def body(dgt_ref, dup_ref, wg_ref, wu_ref, dx2_ref, x1_ref, g_ref, wo_ref, dx1_ref, dx1b_ref, dya_ref, dyb_ref, dg_ref):
        dh = _dot_nt(dgt_ref[...], wg_ref[...]) + _dot_nt(dup_ref[...], wu_ref[...])
        dxn, dgr = _rms_bwd(dh, x1_ref[...], g_ref[...])
        dx1 = dx2_ref[...] + dxn
        dx1_ref[...] = dx1
        dx1b = dx1.astype(BF16)
        dx1b_ref[...] = dx1b
        dy = _dot_nt(dx1b, wo_ref[...])
        dya_ref[...] = dy[:, :RW]
        dyb_ref[...] = dy[:, RW:]
        _acc(dg_ref, _colsum8(dgr), pl.program_id(0) == 0)

    wide = pl.BlockSpec((tm, D_FF), lambda i: (i, 0))
    row = pl.BlockSpec((tm, D_MODEL), lambda i: (i, 0))
    half = pl.BlockSpec((tm, RW), lambda i: (i, 0))
    wsp = pl.BlockSpec((D_MODEL, D_FF), lambda i: (0, 0))
    return pl.pallas_call(
        body, name="ffn_bwd_h", grid=(t // tm,),
        in_specs=[wide, wide, wsp, wsp, row, row, pl.BlockSpec((1, D_MODEL), lambda i: (0, 0)),
                  pl.BlockSpec((D_MODEL, D_MODEL), lambda i: (0, 0))],
        out_specs=[row, row, half, half, pl.BlockSpec((8, D_MODEL), lambda i: (0, 0))],
        out_shape=[jax.ShapeDtypeStruct((t, D_MODEL), F32), jax.ShapeDtypeStruct((t, D_MODEL), BF16),
                   jax.ShapeDtypeStruct((t, RW), F32), jax.ShapeDtypeStruct((t, RW), F32),
                   jax.ShapeDtypeStruct((8, D_MODEL), F32)],
        compiler_params=_params(("arbitrary",)),
    )(dgt, dup, wg, wu, dx2, x1, g2, wout)


def _wgrad(a, b, tk, tn, name):
    t, kdim = a.shape
    ndim = b.shape[1]

    def body(a_ref, b_ref, o_ref):
        o_ref[...] = _dot_tn(a_ref[...], b_ref[...])

    return pl.pallas_call(
        body, name=name, grid=(kdim // tk, ndim // tn),
        in_specs=[pl.BlockSpec((t, tk), lambda i, j: (0, i)), pl.BlockSpec((t, tn), lambda i, j: (0, j))],
        out_specs=pl.BlockSpec((tk, tn), lambda i, j: (i, j)),
        out_shape=jax.ShapeDtypeStruct((kdim, ndim), F32),
        compiler_params=_params(("parallel", "parallel")),
    )(a, b)


def _post_bwd(dya, y, r, k2, v, g, lnw, lnb, rk):
    t = y.shape[0]
    tm = _POST_TM

    def body(d_ref, y_ref, r_ref, k_ref, v_ref, g_ref, lnw_ref, lnb_ref, rk_ref,
             dy_ref, dr_ref, dk_ref, dv_ref, dg_ref, dlnw_ref, dlnb_ref, drk_ref):
        first = pl.program_id(0) == 0
        ones = jnp.ones((tm, 1), F32)
        prim = (y_ref[...], r_ref[...], k_ref[...], v_ref[...], g_ref[...],
                ones * lnw_ref[...], ones * lnb_ref[...], ones * rk_ref[...])
        _, vjp = jax.vjp(_post_fn, *prim)
        dy, dr, dk, dv, dg, dlnw, dlnb, drk = vjp(d_ref[...])
        dy_ref[...] = dy
        dr_ref[...] = dr
        dk_ref[...] = dk
        dv_ref[...] = dv
        dg_ref[...] = dg
        _acc(dlnw_ref, _colsum8(dlnw), first)
        _acc(dlnb_ref, _colsum8(dlnb), first)
        _acc(drk_ref, _colsum8(drk), first)

    row = pl.BlockSpec((tm, RW), lambda i: (i, 0))
    vec = pl.BlockSpec((1, RW), lambda i: (0, 0))
    part = pl.BlockSpec((8, RW), lambda i: (0, 0))
    return pl.pallas_call(
        body, name="rwkv_post_bwd", grid=(t // tm,),
        in_specs=[row] * 6 + [vec] * 3, out_specs=[row] * 5 + [part] * 3,
        out_shape=[jax.ShapeDtypeStruct((t, RW), F32)] * 5 + [jax.ShapeDtypeStruct((8, RW), F32)] * 3,
        compiler_params=_params(("arbitrary",)),
    )(dya, y, r, k2, v, g, lnw, lnb, rk)


def _wkv_bwd(dy, s0s, r, lw, k2, v, kk, a):
    t = r.shape[0]
    nc = t // CHUNK

    def body(dy_ref, s_ref, r_ref, lw_ref, k_ref, v_ref, kk_ref, a_ref,
             dr_ref, dlw_ref, dk_ref, dv_ref, dkk_ref, da_ref, ds):
        @pl.when(pl.program_id(0) == 0)
        def _():
            ds[...] = jnp.zeros_like(ds)

        _, vjp = jax.vjp(_wkv_chunk_fn, s_ref[0],
                         *[_pairs(ref) for ref in (r_ref, lw_ref, k_ref, v_ref, kk_ref, a_ref)])
        res = vjp((_pairs(dy_ref), ds[...]))
        ds[...] = res[0]
        for ref, val in zip((dr_ref, dlw_ref, dk_ref, dv_ref, dkk_ref, da_ref), res[1:]):
            for p in range(N_PAIR):
                ref[:, 128 * p:128 * (p + 1)] = val[p]

    blk = pl.BlockSpec((CHUNK, RW), lambda c: (nc - 1 - c, 0))
    return pl.pallas_call(
        body, name="wkv_bwd", grid=(nc,),
        in_specs=[blk, pl.BlockSpec((1, N_PAIR, 128, 128), lambda c: (nc - 1 - c, 0, 0, 0))] + [blk] * 6,
        out_specs=[blk] * 6,
        out_shape=[jax.ShapeDtypeStruct((t, RW), F32)] * 6,
        scratch_shapes=[pltpu.VMEM((N_PAIR, 128, 128), F32)],
        compiler_params=_params(("arbitrary",)),
    )(dy, s0s, r, lw, k2, v, kk, a)


def _prep_bwd(proj, pw, douts):
    t = proj.shape[0]
    tm = _PREP_TM
    nt = t // tm

    def body(p_ref, l8_ref, mu, w0, w2p, a0, a2p, g2, k_k, k_a, dr, dr2, dlw, dk2, dk22, dv, dv2, dkk, da, dg,
             dp_ref, dmu_ref, dw0_ref, dw2_ref, da0_ref, da2_ref, dg2_ref, dkk_ref, dka_ref, carry):
        i = pl.program_id(0)
        first = i == 0

        @pl.when(first)
        def _():
            carry[...] = jnp.zeros_like(carry)

        p = p_ref[...]
        pprev = _shifted(p, l8_ref[...], i == nt - 1)
        ones = jnp.ones((tm, 1), F32)
        prim = (p, pprev, ones * mu[...], ones * w0[...], w2p[...], ones * a0[...], a2p[...], g2[...],
                ones * k_k[...], ones * k_a[...])
        _, vjp = jax.vjp(_prep_fn, *prim)
        dp, dpp, dmu, dw0, dw2, da0, da2, dg2, dkk_, dka = vjp(
            (dr[...] + dr2[...], dlw[...], dk2[...] + dk22[...], dv[...] + dv2[...], dkk[...], da[...], dg[...]))
        up = pltpu.roll(dpp, tm - 1, axis=0)
        rid = lax.broadcasted_iota(jnp.int32, dpp.shape, 0)
        dp_ref[...] = dp + jnp.where(rid == tm - 1, carry[0:1, :], up)
        carry[...] = jnp.broadcast_to(dpp[0:1, :], carry.shape)
        _acc(dmu_ref, _colsum8(dmu), first)
        _acc(dw0_ref, _colsum8(dw0), first)
        _acc(dw2_ref, dw2, first)
        _acc(da0_ref, _colsum8(da0), first)
        _acc(da2_ref, da2, first)
        _acc(dg2_ref, dg2, first)
        _acc(dkk_ref, _colsum8(dkk_), first)
        _acc(dka_ref, _colsum8(dka), first)

    rev = lambda i: (nt - 1 - i, 0)
    row = pl.BlockSpec((tm, RW), rev)
    part = lambda n: pl.BlockSpec((8, n), lambda i: (0, 0))
    mat = pl.BlockSpec((128, RW), lambda i: (0, 0))
    return pl.pallas_call(
        body, name="rwkv_prep_bwd", grid=(nt,),
        in_specs=[pl.BlockSpec((tm, SHIFT_COLS), rev),
                  pl.BlockSpec((8, SHIFT_COLS), lambda i: (jnp.maximum((nt - 1 - i) * (tm // 8) - 1, 0), 0))]
                 + _prep_specs(tm) + [row] * 10,
        out_specs=[pl.BlockSpec((tm, SHIFT_COLS), rev), part(SHIFT_COLS), part(RW), mat, part(RW), mat, mat,
                   part(RW), part(RW)],
        out_shape=[jax.ShapeDtypeStruct((t, SHIFT_COLS), F32), jax.ShapeDtypeStruct((8, SHIFT_COLS), F32),
                   jax.ShapeDtypeStruct((8, RW), F32), jax.ShapeDtypeStruct((128, RW), F32),
                   jax.ShapeDtypeStruct((8, RW), F32), jax.ShapeDtypeStruct((128, RW), F32),
                   jax.ShapeDtypeStruct((128, RW), F32), jax.ShapeDtypeStruct((8, RW), F32),
                   jax.ShapeDtypeStruct((8, RW), F32)],
        scratch_shapes=[pltpu.VMEM((8, SHIFT_COLS), F32)],
        compiler_params=_params(("arbitrary",)),
    )(proj, proj, *pw, *douts)


def _combine_bwd(dyb, o, l, og):
    t = dyb.shape[0]
    tm = _COMB_TM

    def body(d_ref, o_ref, l_ref, og_ref, do_ref, dl_ref, dog_ref):
        ones = jnp.ones((tm, 1), F32)
        dog = []
        for p in range(N_PAIR):
            cols = slice(128 * p, 128 * (p + 1))
            _, vjp = jax.vjp(_combine_fn, o_ref[0, p], o_ref[1, p], o_ref[2, p], l_ref[0, p], l_ref[1, p], l_ref[2, p],
                             ones * og_ref[:, cols])
            res = vjp(d_ref[:, cols])
            for b in range(3):
                do_ref[b, p] = res[b]
                dl_ref[b, p] = res[3 + b]
            dog.append(_colsum8(res[6]))
        _acc(dog_ref, jnp.concatenate(dog, axis=1), pl.program_id(0) == 0)

    blk = pl.BlockSpec((3, N_PAIR, tm, 128), lambda i: (0, 0, i, 0))
    return pl.pallas_call(
        body, name="attn_combine_bwd", grid=(t // tm,),
        in_specs=[pl.BlockSpec((tm, RW), lambda i: (i, 0)), blk, blk, pl.BlockSpec((1, RW), lambda i: (0, 0))],
        out_specs=[blk, blk, pl.BlockSpec((8, RW), lambda i: (0, 0))],
        out_shape=[jax.ShapeDtypeStruct((3, N_PAIR, t, 128), F32)] * 2 + [jax.ShapeDtypeStruct((8, RW), F32)],
        compiler_params=_params(("arbitrary",)),
    )(dyb, o, l, og)


def _attn_bwd(do, dl, qkv):
    t = qkv.shape[2]

    def body(do_ref, dl_ref, q_ref, k_ref, v_ref, dq_ref, dk_ref, dv_ref):
        for ref in (dq_ref, dk_ref, dv_ref):
            ref[...] = jnp.zeros_like(ref)

        def add(ref, rows, val):
            for g in range(ATTN_GROUP):
                ref.at[g][rows, :] += val[g]

        def unit(di, d, r, n, has_prev):
            cur = _dilated_rows(d, r, n)
            args = [_take(ref, (0,), cur) for ref in (q_ref, k_ref, v_ref)]
            if has_prev:
                prv = _dilated_rows(d, r, n - 1)
                args += [_take(ref, (0,), prv) for ref in (k_ref, v_ref)]
            _, vjp = jax.vjp(_attn_block_fn, *args)
            res = vjp((_take(do_ref, (di,), cur), _take(dl_ref, (di,), cur)))
            add(dq_ref, cur, res[0])
            add(dk_ref, cur, res[1])
            add(dv_ref, cur, res[2])
            if has_prev:
                add(dk_ref, prv, res[3])
                add(dv_ref, prv, res[4])

        _for_each_sequence(t, unit)

    spec = lambda j: pl.BlockSpec((1, ATTN_GROUP, t, 128), lambda i: (j, i, 0, 0))
    three = pl.BlockSpec((3, ATTN_GROUP, t, 128), lambda i: (0, i, 0, 0))
    out = pl.BlockSpec((ATTN_GROUP, t, 128), lambda i: (i, 0, 0))
    return pl.pallas_call(
        body, name="attn_bwd", grid=(N_PAIR // ATTN_GROUP,),
        in_specs=[three, three, spec(0), spec(1), spec(2)], out_specs=[out] * 3,
        out_shape=[jax.ShapeDtypeStruct((N_PAIR, t, 128), F32)] * 3,
        compiler_params=_params(("parallel",)),
    )(do, dl, qkv, qkv, qkv)


def _in_proj_bwd(dpa, dq, dk, dv, win, x, g1, dx1):
    t = x.shape[0]
    tm = 256

    def body(dpa_ref, dq_ref, dk_ref, dv_ref, w_ref, x_ref, g_ref, dx1_ref, dproj_ref, dx_ref, dg_ref):
        parts = [dpa_ref[...]] + [ref[p] for ref in (dq_ref, dk_ref, dv_ref) for p in range(N_PAIR)]
        dproj = jnp.concatenate([z.astype(BF16) for z in parts], axis=1)
        dproj_ref[...] = dproj
        dh = _dot_nt(dproj, w_ref[...])
        dxn, dgr = _rms_bwd(dh, x_ref[...], g_ref[...])
        dx_ref[...] = dx1_ref[...] + dxn
        _acc(dg_ref, _colsum8(dgr), pl.program_id(0) == 0)

    row = pl.BlockSpec((tm, D_MODEL), lambda i: (i, 0))
    pair = pl.BlockSpec((N_PAIR, tm, 128), lambda i: (0, i, 0))
    return pl.pallas_call(
        body, name="in_proj_bwd", grid=(t // tm,),
        in_specs=[pl.BlockSpec((tm, SHIFT_COLS), lambda i: (i, 0))] + [pair] * 3
                 + [pl.BlockSpec((D_MODEL, IN_COLS), lambda i: (0, 0)), row, pl.BlockSpec((1, D_MODEL), lambda i: (0, 0)), row],
        out_specs=[pl.BlockSpec((tm, IN_COLS), lambda i: (i, 0)), row, pl.BlockSpec((8, D_MODEL), lambda i: (0, 0))],
        out_shape=[jax.ShapeDtypeStruct((t, IN_COLS), BF16), jax.ShapeDtypeStruct((t, D_MODEL), F32),
                   jax.ShapeDtypeStruct((8, D_MODEL), F32)],
        compiler_params=_params(("arbitrary",)),
    )(dpa, dq, dk, dv, win, x, g1, dx1)


def _pad_lora(w, lo):
    z = jnp.zeros((64, RW), F32)
    return jnp.concatenate([w, z], axis=0) if lo == 0 else jnp.concatenate([z, w], axis=0)


def _local_step(x, tgt, win, vecs, w2, a2, g2m, get_rest, send_rest):
    pw = (vecs["mu_shift"], vecs["decay_w0"], _pad_lora(w2, 0), vecs["iclr_a0"], _pad_lora(a2, 64), g2m,
          vecs["k_k"], vecs["k_a"])
    h, proj, qkv = _in_proj(x, vecs["mix_norm_g"], win)
    r, lw, k2, v, kk, a, g = _prep_fwd(proj, pw)
    y, s0s = _wkv_fwd(r, lw, k2, v, kk, a)
    ya = _post_fwd(y, r, k2, v, g, vecs["ln_x_w"], vecs["ln_x_b"], vecs["r_k"])
    o_att, l_att = _attn_fwd(qkv)
    yb = _combine_fwd(o_att, l_att, vecs["attn_out_g"])

    wout, wg, wu, wd = get_rest(yb)
    ycat = jnp.concatenate([ya, yb], axis=1)
    x1, h2 = _out_proj(x, ycat, wout, vecs["ffn_norm_g"])
    gt, up, act = _ffn_up(h2, wg, wu)
    dx2, dx2b, loss8, dgf = _ffn_down_loss(x1, act, wd, vecs["final_norm_g"], tgt)

    dgt, dup = _ffn_bwd_act(dx2b, wd, gt, up)
    dx1, dx1b, dya, dyb, dg2n = _ffn_bwd_h(dgt, dup, wg, wu, dx2, x1, vecs["ffn_norm_g"], wout)
    gw = {
        "w_down": _wgrad(act, dx2b, 1408, 1024, "wgrad_down"),
        "w_gate": _wgrad(h2, dgt, 1024, 1408, "wgrad_gate"),
        "w_up": _wgrad(h2, dup, 1024, 1408, "wgrad_up"),
        "w_out": _wgrad(ycat, dx1b, 1024, 1024, "wgrad_out"),
    }

    lnw = vecs["ln_x_w"] + send_rest(gw)[0, 0]
    dy, dr_p, dk2_p, dv_p, dg, dlnw, dlnb, drk = _post_bwd(dya, y, r, k2, v, g, lnw, vecs["ln_x_b"], vecs["r_k"])
    dr_s, dlw, dk2_s, dv_s, dkk, da = _wkv_bwd(dy, s0s, r, lw, k2, v, kk, a)
    dpa, dmu, dw0, dw2p, da0, da2p, dg2m, dk_k, dk_a = _prep_bwd(
        proj, pw, (dr_p, dr_s, dlw, dk2_p, dk2_s, dv_p, dv_s, dkk, da, dg))

    do_att, dl_att, dog = _combine_bwd(dyb, o_att, l_att, vecs["attn_out_g"])
    dq, dk, dv = _attn_bwd(do_att, dl_att, qkv)
    dproj, dx, dg1 = _in_proj_bwd(dpa, dq, dk, dv, win, x, vecs["mix_norm_g"], dx1)
    gw["w_in"] = _wgrad(h, dproj, 1024, 1664, "wgrad_in")
    gw["decay_w2"] = dw2p[:64]
    gw["iclr_a2"] = da2p[64:]
    gw["gate_g2"] = dg2m
    gv = {"mix_norm_g": dg1, "mu_shift": dmu, "decay_w0": dw0, "iclr_a0": da0, "k_k": dk_k, "k_a": dk_a, "r_k": drk,
          "ln_x_w": dlnw, "ln_x_b": dlnb, "attn_out_g": dog, "ffn_norm_g": dg2n, "final_norm_g": dgf}
    return loss8, dx, gw, gv


N_CHIP = 4
N_DEV = 8
MATS = ("w_in", "w_out", "w_gate", "w_up", "w_down")
LORAS = ("decay_w2", "iclr_a2", "gate_g2")
VECS = (("mix_norm_g", 1024), ("mu_shift", 1792), ("decay_w0", 512), ("iclr_a0", 512), ("k_k", 512), ("k_a", 512),
        ("r_k", 512), ("ln_x_w", 512), ("ln_x_b", 512), ("attn_out_g", 512), ("ffn_norm_g", 1024),
        ("final_norm_g", 1024))
N_VEC = sum(n for _, n in VECS)
N_SMALL = N_VEC + 128
ROWS_PAD = 3328
ANY = pl.BlockSpec(memory_space=pl.ANY)


def _flip(v, f):
    return 1 - v if f else v


class _Me:
    def __init__(self, mode):
        x, y, c = lax.axis_index("x"), lax.axis_index("y"), lax.axis_index("c")
        self.core, self.chip, self.dev = c, 2 * x + y, 4 * x + 2 * y + c
        self.sibling = (x, y, 1 - c)
        if mode == "chips":
            self.peers = [(px, py, c) for px, py in ((1 - x, y), (x, 1 - y), (1 - x, 1 - y))]
        else:
            self.peers = [(_flip(x, k & 4), _flip(y, k & 2), _flip(c, k & 1)) for k in range(1, N_DEV)]


def _half(core, rows):
    h = rows // 2
    return pl.ds(pl.multiple_of(core * h, h), h)


def _landing(a, kind):
    if kind == "gather":
        return (N_CHIP,) + a.shape
    if kind == "scatter":
        return (N_DEV, a.shape[1] // 2, a.shape[2])
    return (N_DEV,) + a.shape


def _own_piece(src, kind, me):
    if kind == "scatter":
        return src.at[me.chip, _half(me.core, src.shape[1])]
    return src


def _own_slot(dst, kind, me):
    return dst.at[me.chip] if kind == "gather" else dst.at[me.dev]


def _peer_copy(srcs, dsts, kinds, send_sems, recv_sems, me, j, i, incoming):
    px, py, pc = me.peers[j]
    pchip, pdev = 2 * px + py, 4 * px + 2 * py + pc
    src, dst, kind = srcs[i], dsts[i], kinds[i]
    if kind == "gather":
        rows = _half(me.core, src.shape[0])
        src, dst = src.at[rows], dst.at[pchip if incoming else me.chip, rows]
    elif kind == "scatter":
        src, dst = src.at[pchip, _half(pc, src.shape[1])], dst.at[pdev if incoming else me.dev]
    else:
        dst = dst.at[pdev if incoming else me.dev]
    n = len(srcs)
    return pltpu.make_async_remote_copy(src_ref=src, dst_ref=dst, send_sem=send_sems.at[n * j + i],
                                        recv_sem=recv_sems.at[n * j + i], device_id=(px, py, pc), device_id_type=MESH)


def _mode(kinds):
    return "chips" if kinds[0] == "gather" else "devs"


def _npeer(kinds):
    return N_CHIP - 1 if kinds[0] == "gather" else N_DEV - 1


def _swap_blocking(arrs, kinds, name):
    n = len(arrs)

    def body(*refs):
        srcs, dsts = refs[:n], refs[n:2 * n]
        send_sems, recv_sems, local_sems = refs[2 * n:]
        me = _Me(_mode(kinds))
        local = [pltpu.make_async_copy(_own_piece(srcs[i], kinds[i], me), _own_slot(dsts[i], kinds[i], me), local_sems.at[i])
                 for i in range(n)]
        for cp in local:
            cp.start()
        sends = [_peer_copy(srcs, dsts, kinds, send_sems, recv_sems, me, j, i, False)
                 for j in range(len(me.peers)) for i in range(n)]
        for cp in sends:
            cp.start()
        for j in range(len(me.peers)):
            for i in range(n):
                _peer_copy(srcs, dsts, kinds, send_sems, recv_sems, me, j, i, True).wait_recv()
        for cp in sends:
            cp.wait_send()
        for cp in local:
            cp.wait()

    ns = _npeer(kinds) * n
    return pl.pallas_call(
        body, name=name, in_specs=[ANY] * n, out_specs=[ANY] * n,
        out_shape=[jax.ShapeDtypeStruct(_landing(a, k), a.dtype) for a, k in zip(arrs, kinds)],
        scratch_shapes=[pltpu.SemaphoreType.DMA((ns,)), pltpu.SemaphoreType.DMA((ns,)), pltpu.SemaphoreType.DMA((n,))],
    )(*arrs)


def _swap_gathered(lands, name):
    n = len(lands)

    def body(*refs):
        dsts, send_sems, recv_sems = refs[n:2 * n], refs[2 * n], refs[2 * n + 1]
        me = _Me("chips")

        def copy(j, i, incoming):
            px, py, _ = me.peers[j]
            rows_out, rows_in = _half(me.core, dsts[i].shape[1]), _half(1 - me.core, dsts[i].shape[1])
            return pltpu.make_async_remote_copy(
                src_ref=dsts[i].at[2 * px + py, rows_out], dst_ref=dsts[i].at[2 * px + py, rows_in if incoming else rows_out],
                send_sem=send_sems.at[n * j + i], recv_sem=recv_sems.at[n * j + i], device_id=me.sibling, device_id_type=MESH)

        sends = [copy(j, i, False) for j in range(3) for i in range(n)]
        for cp in sends:
            cp.start()
        for j in range(3):
            for i in range(n):
                copy(j, i, True).wait_recv()
        for cp in sends:
            cp.wait_send()

    return pl.pallas_call(
        body, name=name, in_specs=[ANY] * n, out_specs=[ANY] * n,
        out_shape=[jax.ShapeDtypeStruct(l.shape, l.dtype) for l in lands],
        input_output_aliases={i: i for i in range(n)},
        scratch_shapes=[pltpu.SemaphoreType.DMA((3 * n,)), pltpu.SemaphoreType.DMA((3 * n,))],
    )(*lands)


def _join_halves(halves, name):
    n = len(halves)

    def body(*refs):
        srcs, dsts, send_sems, recv_sems, local_sems = refs[:n], refs[n:2 * n], refs[2 * n], refs[2 * n + 1], refs[2 * n + 2]
        me = _Me("chips")

        def copy(i, incoming):
            rows = _half(1 - me.core if incoming else me.core, dsts[i].shape[0])
            return pltpu.make_async_remote_copy(src_ref=srcs[i], dst_ref=dsts[i].at[rows], send_sem=send_sems.at[i],
                                                recv_sem=recv_sems.at[i], device_id=me.sibling, device_id_type=MESH)

        local = [pltpu.make_async_copy(srcs[i], dsts[i].at[_half(me.core, dsts[i].shape[0])], local_sems.at[i]) for i in range(n)]
        sends = [copy(i, False) for i in range(n)]
        for cp in local + sends:
            cp.start()
        for i in range(n):
            copy(i, True).wait_recv()
        for cp in sends:
            cp.wait_send()
        for cp in local:
            cp.wait()

    return pl.pallas_call(
        body, name=name, in_specs=[ANY] * n, out_specs=[ANY] * n,
        out_shape=[jax.ShapeDtypeStruct((2 * h.shape[0], h.shape[1]), h.dtype) for h in halves],
        scratch_shapes=[pltpu.SemaphoreType.DMA((n,)), pltpu.SemaphoreType.DMA((n,)), pltpu.SemaphoreType.DMA((n,))],
    )(*halves)


HBM = pl.BlockSpec(memory_space=pltpu.HBM)
SEM = pl.BlockSpec(memory_space=pltpu.SEMAPHORE)
EFFECT = pltpu.SideEffectType.DATAFLOW_SIDE_EFFECTING


def _swap_start(arrs, lands, kinds, name):
    n = len(arrs)

    def body(*refs):
        srcs, dsts, send_sems, recv_sems, token = refs[:n], refs[n:2 * n], refs[2 * n], refs[2 * n + 1], refs[-1]
        me = _Me(_mode(kinds))
        for j in range(len(me.peers)):
            for i in range(n):
                _peer_copy(srcs, dsts, kinds, send_sems, recv_sems, me, j, i, False).start()
        token[...] = jnp.zeros_like(token)

    ns = _npeer(kinds) * n
    outs = pl.pallas_call(
        body, name=name,
        out_shape=(pltpu.SemaphoreType.DMA((ns,)), pltpu.SemaphoreType.DMA((ns,)),
                   *[pltpu.HBM(a.shape, a.dtype) for a in arrs], *[pltpu.HBM(l.shape, l.dtype) for l in lands],
                   jax.ShapeDtypeStruct((8, 128), F32)),
        in_specs=[HBM] * (2 * n), out_specs=(SEM, SEM, *[HBM] * (2 * n), pl.BlockSpec(memory_space=pltpu.VMEM)),
        input_output_aliases={k: 2 + k for k in range(2 * n)},
        compiler_params=pltpu.CompilerParams(has_side_effects=EFFECT),
    )(*[pltpu.with_memory_space_constraint(a, pltpu.HBM) for a in arrs],
      *[pltpu.with_memory_space_constraint(l, pltpu.HBM) for l in lands])
    return outs[0], outs[1], outs[2:2 + n], outs[2 + n:2 + 2 * n], outs[-1]


def _swap_wait(send_sems, recv_sems, srcs_thru, lands_thru, after, kinds, name):
    n = len(srcs_thru)

    def body(*refs):
        srcs, dsts, s_sems, r_sems = refs[:n], refs[n:2 * n], refs[2 * n], refs[2 * n + 1]
        me = _Me(_mode(kinds))
        for j in range(len(me.peers)):
            for i in range(n):
                cp = _peer_copy(srcs, dsts, kinds, s_sems, r_sems, me, j, i, True)
                cp.wait_send()
                cp.wait_recv()

    outs = pl.pallas_call(
        body, name=name,
        out_shape=tuple(pltpu.HBM(a.shape, a.dtype) for a in (*srcs_thru, *lands_thru)),
        in_specs=[HBM] * (2 * n) + [SEM, SEM, ANY], out_specs=tuple([HBM] * (2 * n)),
        input_output_aliases={k: k for k in range(2 * n)},
        compiler_params=pltpu.CompilerParams(has_side_effects=EFFECT),
    )(*srcs_thru, *lands_thru, send_sems, recv_sems, after)
    return outs[n:]


def _adamw(w, g, m, v):
    m = ADAM_B1 * m + (1.0 - ADAM_B1) * g
    v = ADAM_B2 * v + (1.0 - ADAM_B2) * (g * g)
    m_hat = m / (1.0 - ADAM_B1 ** ADAM_STEP)
    v_hat = v / (1.0 - ADAM_B2 ** ADAM_STEP)
    delta = -ADAM_LR * (m_hat / (jnp.sqrt(v_hat) + ADAM_EPS) + ADAM_WD * w)
    return delta, m, v


def _reduce8(rbuf, tr, name):
    _, rows, cols = rbuf.shape

    def body(r_ref, g_ref):
        g = r_ref[0].astype(F32)
        for s in range(1, N_DEV):
            g = g + r_ref[s].astype(F32)
        g_ref[...] = g

    return pl.pallas_call(
        body, name=name, grid=(rows // tr,),
        in_specs=[pl.BlockSpec((N_DEV, tr, cols), lambda i: (0, i, 0))], out_specs=pl.BlockSpec((tr, cols), lambda i: (i, 0)),
        out_shape=jax.ShapeDtypeStruct((rows, cols), F32),
        compiler_params=_params(("parallel",)),
    )(rbuf)


def _adamw_call(g, w, m, v, tr, name):
    _, rows, cols = w.shape

    def body(g_in, w_ref, m_ref, v_ref, g_ref, d_ref, nm_ref, nv_ref):
        g = g_in[...]
        g_ref[0] = g
        d_ref[0], nm_ref[0], nv_ref[0] = _adamw(w_ref[0], g, m_ref[0], v_ref[0])

    row = pl.BlockSpec((1, tr, cols), lambda i: (0, i, 0))
    return pl.pallas_call(
        body, name=name, grid=(rows // tr,),
        in_specs=[pl.BlockSpec((tr, cols), lambda i: (i, 0)), row, row, row], out_specs=[row] * 4,
        out_shape=[jax.ShapeDtypeStruct(w.shape, F32)] * 4,
        compiler_params=_params(("parallel",)),
    )(g, w, m, v)


def _reduce_adamw_small(sbuf, w, m, v):
    def body(s_ref, w_ref, m_ref, v_ref, g_ref, d_ref, nm_ref, nv_ref, loss_ref):
        tot = s_ref[0]
        for s in range(1, N_DEV):
            tot = tot + s_ref[s]
        tot = jnp.sum(tot, axis=0, keepdims=True)
        g = tot[:, :N_VEC]
        g_ref[...] = g
        d_ref[...], nm_ref[...], nv_ref[...] = _adamw(w_ref[...], g, m_ref[...], v_ref[...])
        loss_ref[...] = tot[:, N_VEC:]

    return pl.pallas_call(
        body, name="reduce_adamw_small",
        out_shape=[jax.ShapeDtypeStruct((1, N_VEC), F32)] * 4 + [jax.ShapeDtypeStruct((1, 128), F32)],
    )(sbuf, w, m, v)


_ROW_SHARDED = ("w_out", "w_down")
_ADAM_TILE = {"w_in": 256, "w_out": 256, "w_gate": 256, "w_up": 256, "w_down": 176, "decay_w2": 64, "iclr_a2": 64,
              "gate_g2": 128}
_SUM_TILE = {"w_in": 256, "w_out": 128, "w_gate": 256, "w_up": 256, "w_down": 176, "decay_w2": 32, "iclr_a2": 32,
             "gate_g2": 64}


def _full(n, stacked):
    p, r, c = stacked.shape
    if n in _ROW_SHARDED:
        return stacked.reshape(p * r, c)
    return jnp.transpose(stacked, (1, 0, 2)).reshape(r, p * c)


def _by_chip(n, full):
    if n in _ROW_SHARDED:
        return full.reshape(N_CHIP, full.shape[0] // N_CHIP, full.shape[1])
    r, c = full.shape
    return jnp.transpose(full.reshape(r, N_CHIP, c // N_CHIP), (1, 0, 2))


def _with_own(land_shape, dtype, own, slot):
    return lax.dynamic_update_slice(lax.empty(land_shape, dtype), own[None], (slot,) + (0,) * own.ndim)


def kernel(x, mix_norm_g, w_in, mu_shift, decay_w0, decay_w2, iclr_a0, iclr_a2, gate_g2, k_k, k_a, r_k, ln_x_w, ln_x_b, attn_out_g, w_out, ffn_norm_g, w_gate, w_up, w_down, final_norm_g, loss_target, m_mix_norm_g, m_w_in, m_mu_shift, m_decay_w0, m_decay_w2, m_iclr_a0, m_iclr_a2, m_gate_g2, m_k_k, m_k_a, m_r_k, m_ln_x_w, m_ln_x_b, m_attn_out_g, m_w_out, m_ffn_norm_g, m_w_gate, m_w_up, m_w_down, m_final_norm_g, v_mix_norm_g, v_w_in, v_mu_shift, v_decay_w0, v_decay_w2, v_iclr_a0, v_iclr_a2, v_gate_g2, v_k_k, v_k_a, v_r_k, v_ln_x_w, v_ln_x_b, v_attn_out_g, v_w_out, v_ffn_norm_g, v_w_gate, v_w_up, v_w_down, v_final_norm_g):
    names = ("mix_norm_g", "w_in", "mu_shift", "decay_w0", "decay_w2", "iclr_a0", "iclr_a2", "gate_g2", "k_k", "k_a",
             "r_k", "ln_x_w", "ln_x_b", "attn_out_g", "w_out", "ffn_norm_g", "w_gate", "w_up", "w_down", "final_norm_g")
    w = dict(zip(names, (mix_norm_g, w_in, mu_shift, decay_w0, decay_w2, iclr_a0, iclr_a2, gate_g2, k_k, k_a, r_k,
                         ln_x_w, ln_x_b, attn_out_g, w_out, ffn_norm_g, w_gate, w_up, w_down, final_norm_g)))
    m = dict(zip(names, (m_mix_norm_g, m_w_in, m_mu_shift, m_decay_w0, m_decay_w2, m_iclr_a0, m_iclr_a2, m_gate_g2,
                         m_k_k, m_k_a, m_r_k, m_ln_x_w, m_ln_x_b, m_attn_out_g, m_w_out, m_ffn_norm_g, m_w_gate,
                         m_w_up, m_w_down, m_final_norm_g)))
    v = dict(zip(names, (v_mix_norm_g, v_w_in, v_mu_shift, v_decay_w0, v_decay_w2, v_iclr_a0, v_iclr_a2, v_gate_g2,
                         v_k_k, v_k_a, v_r_k, v_ln_x_w, v_ln_x_b, v_attn_out_g, v_w_out, v_ffn_norm_g, v_w_gate,
                         v_w_up, v_w_down, v_final_norm_g)))
    first = ("w_in",) + LORAS
    rest = ("w_out", "w_gate", "w_up", "w_down")
    xi, yi, ci = lax.axis_index("x"), lax.axis_index("y"), lax.axis_index("c")
    my_chip, my_dev = 2 * xi + yi, 4 * xi + 2 * yi + ci
    gather, scatter = ("gather",) * 4, ("scatter",) * 4

    wb = {n: w[n][0].astype(BF16) for n in MATS}
    lands = [_with_own((N_CHIP,) + wb[n].shape, BF16, wb[n], my_chip) for n in rest]
    ssem, rsem, srcs_thru, lands_thru, tok = _swap_start([wb[n] for n in rest], lands, gather, "gather_rest_start")
    got = _swap_blocking([wb["w_in"]] + [w[n][0] for n in LORAS], gather, "gather_first")
    win, w2, a2, g2m = (_full(n, z) for n, z in zip(first, _swap_gathered(got, "gather_first_halves")))

    vecs = {n: w[n].reshape(1, sz) for n, sz in VECS}
    vecs["mix_norm_g"] = vecs["mix_norm_g"] + tok[0, 0]

    def get_rest(after):
        halves = _swap_wait(ssem, rsem, srcs_thru, lands_thru, after, gather, "gather_rest_wait")
        return [_full(n, z) for n, z in zip(rest, _swap_gathered(halves, "gather_rest_halves"))]

    flight = []

    def my_half(g):
        h = g.shape[1] // 2
        return lax.dynamic_slice(g, (my_chip, ci * h, 0), (1, h, g.shape[2]))[0]

    def send_rest(gw):
        gs = [_by_chip(n, gw[n]).astype(BF16) for n in rest]
        into = [_with_own((N_DEV,) + my_half(g).shape, BF16, my_half(g), my_dev) for g in gs]
        flight.extend(_swap_start(gs, into, scatter, "exchange_rest_start"))
        return flight[4]

    loss8, dx, gw, gv = _local_step(x[0], loss_target[0], win, vecs, w2, a2, g2m, get_rest, send_rest)

    small = jnp.concatenate([gv[n] for n, _ in VECS] + [loss8], axis=1)
    got = _swap_blocking([_by_chip(n, gw[n]).astype(BF16) for n in first] + [small], scatter + ("all",), "exchange_first")
    rbuf = dict(zip(first, got[:4]))
    rbuf.update(zip(rest, _swap_wait(flight[0], flight[1], flight[2], flight[3], got[0], scatter, "exchange_rest_wait")))

    big = MATS + LORAS
    halves = [_reduce8(rbuf[n], _SUM_TILE[n], "reduce_" + n) for n in big]
    gsum = dict(zip(big, _join_halves(halves, "join_halves")))
    res = {n: _adamw_call(gsum[n], w[n], m[n], v[n], _ADAM_TILE[n], "adamw_" + n) for n in big}
    cat = lambda d: jnp.concatenate([d[n].reshape(1, sz) for n, sz in VECS], axis=1)
    small_res = _reduce_adamw_small(got[4], cat(w), cat(m), cat(v))

    outs = []
    for k in range(4):
        piece = {n: r[k] for n, r in res.items()}
        c0 = 0
        for n, sz in VECS:
            piece[n] = small_res[k][0, c0:c0 + sz].reshape(w[n].shape)
            c0 += sz
        outs.extend(piece[n] for n in names)
    return (small_res[4][0, 0], dx[None], *outs)
```

```python
import jax
import jax.numpy as jnp
from jax import lax
from jax.experimental import pallas as pl
from jax.experimental.pallas import tpu as pltpu

F32 = jnp.float32
BF16 = jnp.bfloat16
HI = lax.Precision.HIGHEST

D_MODEL = 1024
HEAD_DIM = 64
RW = 512
N_PAIR = RW // 128
SHIFT_COLS = 1792
IN_COLS = 3328
D_FF = 2816
NORM_EPS = 1e-6
GN_EPS = 64e-5
CHUNK = 64
SUB = 16
WKV_PASSES = 1
ATTN_PASSES = 1
ATTN_BLOCK = 128
DILATIONS = (1, 4, 16)
NEG = -1e30
ADAM_LR, ADAM_B1, ADAM_B2, ADAM_EPS, ADAM_WD, ADAM_STEP = 0.001, 0.9, 0.999, 1e-08, 0.01, 10
VMEM_LIMIT = 56 * 1024 * 1024
MESH = pl.DeviceIdType.MESH


def _params(sem=None, **kw):
    return pltpu.CompilerParams(dimension_semantics=sem, vmem_limit_bytes=VMEM_LIMIT, **kw)


def _dot(a, b, prec=None):
    return lax.dot_general(a, b, (((1,), (0,)), ((), ())), preferred_element_type=F32, precision=prec)


def _dot_nt(a, b, prec=None):
    return lax.dot_general(a, b, (((1,), (1,)), ((), ())), preferred_element_type=F32, precision=prec)


def _dot_tn(a, b, prec=None):
    return lax.dot_general(a, b, (((0,), (0,)), ((), ())), preferred_element_type=F32, precision=prec)


_FORMS = {"nn": ((1,), (0,)), "nt": ((1,), (1,)), "tn": ((0,), (0,))}


def _dg(a, b, form):
    if a.ndim == 3 or b.ndim == 3:
        nb = a.shape[0] if a.ndim == 3 else b.shape[0]
        return jnp.stack([_dg(a[i] if a.ndim == 3 else a, b[i] if b.ndim == 3 else b, form) for i in range(nb)], axis=0)
    return lax.dot_general(a, b, (_FORMS[form], ((), ())), preferred_element_type=F32)


def _split2(x):
    hi = x.astype(BF16)
    return hi, (x - hi.astype(F32)).astype(BF16)


def _split3(x):
    hi = x.astype(BF16)
    rest = x - hi.astype(F32)
    mid = rest.astype(BF16)
    return hi, mid, (rest - mid.astype(F32)).astype(BF16)


def _mm_raw(a, b, form, mode):
    if mode == 1:
        return _dg(a.astype(BF16), b.astype(BF16), form)
    if mode == 3:
        ah, al = _split2(a)
        bh, bl = _split2(b)
        return _dg(ah, bh, form) + (_dg(ah, bl, form) + _dg(al, bh, form))
    if mode == "L3":
        ab = a.astype(BF16)
        b1, b2, b3 = _split3(b)
        return _dg(ab, b1, form) + (_dg(ab, b2, form) + _dg(ab, b3, form))
    assert mode == "R3", mode
    bb = b.astype(BF16)
    a1, a2, a3 = _split3(a)
    return _dg(a1, bb, form) + (_dg(a2, bb, form) + _dg(a3, bb, form))


def _mm(a, b, form, mode):
    @jax.custom_vjp
    def f(a, b):
        return _mm_raw(a, b, form, mode)

    def fwd(a, b):
        return _mm_raw(a, b, form, mode), (a, b)

    def bwd(res, ct):
        a, b = res
        la = {1: 1, 3: 3, "L3": None, "R3": "R3"}[mode]
        lb = {1: 1, 3: 3, "L3": "L3", "R3": None}[mode]
        if form == "nn":
            da = None if la is None else _mm_raw(ct, b, "nt", la)
            db = None if lb is None else _mm_raw(a, ct, "tn", lb)
        elif form == "nt":
            da = None if la is None else _mm_raw(ct, b, "nn", la)
            db = None if lb is None else _mm_raw(ct, a, "tn", "R3" if lb == "L3" else lb)
        else:
            da = None if la is None else _mm_raw(b, ct, "nt", "L3" if la == "R3" else la)
            db = None if lb is None else _mm_raw(a, ct, "nn", lb)
        return (jnp.zeros_like(a) if da is None else da, jnp.zeros_like(b) if db is None else db)

    f.defvjp(fwd, bwd)
    return f(a, b)


def _seg_ones(n):
    r = lax.broadcasted_iota(jnp.int32, (n, n), 0) // HEAD_DIM
    c = lax.broadcasted_iota(jnp.int32, (n, n), 1) // HEAD_DIM
    return (r == c).astype(F32)


def _segsum(x, seg):
    return _mm(x, seg, "nn", "R3")


def _rms_fwd(x, g):
    rstd = lax.rsqrt(jnp.mean(x * x, axis=-1, keepdims=True) + NORM_EPS)
    return x * rstd * g


def _rms_bwd(dy, x, g):
    rstd = lax.rsqrt(jnp.mean(x * x, axis=-1, keepdims=True) + NORM_EPS)
    xn = x * rstd
    dxn = dy * g
    dx = rstd * (dxn - xn * jnp.mean(dxn * xn, axis=-1, keepdims=True))
    return dx, dy * xn


def _sigmoid(x):
    return 1.0 / (1.0 + jnp.exp(-x))


def _softplus(x):
    return jnp.maximum(x, 0.0) + jnp.log(1.0 + jnp.exp(-jnp.abs(x)))


def _acc(ref, val, first):
    @pl.when(first)
    def _():
        ref[...] = val

    @pl.when(jnp.logical_not(first))
    def _():
        ref[...] += val


def _colsum8(v):
    rows, n = v.shape
    return jnp.sum(v.reshape(rows // 8, 8, n), axis=0)


def _prep_fn(p, pprev, mu, w0, w2p, a0, a2p, g2, k_k, k_a):
    seg = _seg_ones(RW)
    ps = p + (pprev - p) * mu
    r = ps[:, 0:RW]
    k = ps[:, RW:2 * RW]
    v = ps[:, 2 * RW:3 * RW]
    xwa = ps[:, 3 * RW:3 * RW + 128]
    xg = ps[:, 3 * RW + 128:3 * RW + 256]
    wraw = -_softplus(-(w0 + _mm(jnp.tanh(xwa), w2p, "nn", 3))) - 0.5
    lw = -jnp.exp(wraw)
    a = _sigmoid(a0 + _mm(xwa, a2p, "nn", 3))
    g = _mm(_sigmoid(xg), g2, "nn", 3)
    kk = k * k_k
    kk = kk / jnp.maximum(jnp.sqrt(_segsum(kk * kk, seg)), 1e-12)
    k2 = k * (1.0 + (a - 1.0) * k_a)
    return r, lw, k2, v, kk, a, g


def _solve_unit_lower(lmat, rhs):
    c = lmat.shape[-1]
    row = lax.broadcasted_iota(jnp.int32, (c, c), 0)
    col = lax.broadcasted_iota(jnp.int32, (c, c), 1)
    eye = (row == col).astype(F32)
    ld = jnp.where(row // SUB == col // SUB, lmat, 0.0)
    lo = lmat - ld
    x = eye + ld
    m = ld
    mm = lambda p, q: _mm(p, q, "nn", WKV_PASSES)
    for _ in range(3):
        m = mm(m, m)
        x = x + mm(x, m)
    g = mm(x, lo)
    g2 = mm(g, g)
    w = mm(x, rhs)
    w = w + mm(g2, w)
    return w + mm(g, w)


def _wkv_chunk_fn(s0, r, lw, k, v, kk, a):
    c = r.shape[-2]
    n = 2 * c
    row = lax.broadcasted_iota(jnp.int32, (n, n), 0)
    col = lax.broadcasted_iota(jnp.int32, (n, n), 1)
    same = (row // c) == (col // c)
    incl = jnp.logical_and(row >= col, same)
    strict = jnp.logical_and(row > col, same)
    sel = (lax.broadcasted_iota(jnp.int32, (n, 128), 0) // c) == (lax.broadcasted_iota(jnp.int32, (n, 128), 1) // HEAD_DIM)
    two = lambda z: jnp.concatenate([z, z], axis=-2)
    lw2 = two(lw)
    mm = lambda p_, q_, form: _mm(p_, q_, form, WKV_PASSES)
    cl = _mm(incl.astype(F32), lw2, "nn", "L3")
    p = jnp.exp(cl)
    pinv = jnp.exp(-cl)
    pprev = jnp.exp(cl - lw2)
    kk2 = two(kk)
    at = jnp.where(sel, -kk2 * pprev, 0.0)
    bt = jnp.where(sel, kk2 * two(a) * pinv, 0.0)
    kt = jnp.where(sel, two(k) * pinv, 0.0)
    rt = jnp.where(sel, two(r) * p, 0.0)
    vt = jnp.where(sel, two(v), 0.0)
    ab = jnp.where(strict, mm(at, bt, "nt"), 0.0)
    ak = jnp.where(strict, mm(at, kt, "nt"), 0.0)
    rb = jnp.where(incl, mm(rt, bt, "nt"), 0.0)
    rk = jnp.where(incl, mm(rt, kt, "nt"), 0.0)
    u = _solve_unit_lower(ab, mm(at, s0, "nt") + mm(ak, vt, "nn"))
    y2 = mm(rt, s0, "nt") + mm(rb, u, "nn") + mm(rk, vt, "nn")
    plast = jnp.exp(jnp.sum(lw, axis=-2, keepdims=True))
    s1 = (s0 + mm(u, bt, "tn") + mm(vt, kt, "tn")) * plast
    r2 = lax.broadcasted_iota(jnp.int32, (128, 128), 0) // HEAD_DIM
    c2 = lax.broadcasted_iota(jnp.int32, (128, 128), 1) // HEAD_DIM
    return y2[..., :c, :] + y2[..., c:, :], jnp.where(r2 == c2, s1, 0.0)


def _post_fn(y, r, k2, v, g, lnw, lnb, rk):
    seg = _seg_ones(RW)
    mean = _segsum(y, seg) * (1.0 / HEAD_DIM)
    yc = y - mean
    var = _segsum(yc * yc, seg) * (1.0 / HEAD_DIM)
    yn = yc * lax.rsqrt(var + GN_EPS)
    out = yn * lnw + lnb + _segsum(r * k2 * rk, seg) * v
    return out * g


def _attn_block_fn(q, kc, vc, kp=None, vp=None):
    n = ATTN_BLOCK
    qi = lax.broadcasted_iota(jnp.int32, (n, n), 0)
    kj = lax.broadcasted_iota(jnp.int32, (n, n), 1)
    lane = lax.broadcasted_iota(jnp.int32, (1, 128), 1)
    scale = HEAD_DIM ** -0.5
    os_, ls_ = [], []
    for h in range(2):
        mh = (lane // HEAD_DIM) == h
        qh = jnp.where(mh, q, 0.0)
        sc = jnp.where(kj <= qi, _mm(qh, kc, "nt", ATTN_PASSES) * scale, NEG)
        m = jnp.max(sc, axis=-1, keepdims=True)
        if kp is not None:
            sp = jnp.where(kj >= qi, _mm(qh, kp, "nt", ATTN_PASSES) * scale, NEG)
            m = jnp.maximum(m, jnp.max(sp, axis=-1, keepdims=True))
        pc = jnp.exp(sc - m)
        den = jnp.sum(pc, axis=-1, keepdims=True)
        num = _mm(pc, vc, "nn", ATTN_PASSES)
        if kp is not None:
            pp = jnp.exp(sp - m)
            den = den + jnp.sum(pp, axis=-1, keepdims=True)
            num = num + _mm(pp, vp, "nn", ATTN_PASSES)
        os_.append(num / den)
        ls_.append(m + jnp.log(den))
    m0 = (lane // HEAD_DIM) == 0
    return jnp.where(m0, os_[0], os_[1]), jnp.where(m0, ls_[0], ls_[1])


def _combine_fn(o1, o2, o3, l1, l2, l3, og):
    seg = _seg_ones(o1.shape[-1])
    m = jnp.maximum(jnp.maximum(l1, l2), l3)
    e1, e2, e3 = jnp.exp(l1 - m), jnp.exp(l2 - m), jnp.exp(l3 - m)
    o = (e1 * o1 + e2 * o2 + e3 * o3) / (e1 + e2 + e3)
    o = o * lax.rsqrt(_segsum(o * o, seg) * (1.0 / HEAD_DIM) + NORM_EPS)
    return o * og


def _in_proj(x, g1, win):
    t = x.shape[0]
    tm = 256

    def body(x_ref, g_ref, w_ref, h_ref, pa_ref, qkv_ref):
        h = _rms_fwd(x_ref[...], g_ref[...]).astype(BF16)
        h_ref[...] = h
        proj = _dot(h, w_ref[...])
        pa_ref[...] = proj[:, :SHIFT_COLS]
        for j in range(3):
            for p in range(N_PAIR):
                c0 = SHIFT_COLS + j * RW + p * 128
                qkv_ref[j, p] = proj[:, c0:c0 + 128]

    return pl.pallas_call(
        body, name="in_proj", grid=(t // tm,),
        in_specs=[pl.BlockSpec((tm, D_MODEL), lambda i: (i, 0)), pl.BlockSpec((1, D_MODEL), lambda i: (0, 0)),
                  pl.BlockSpec((D_MODEL, IN_COLS), lambda i: (0, 0))],
        out_specs=[pl.BlockSpec((tm, D_MODEL), lambda i: (i, 0)), pl.BlockSpec((tm, SHIFT_COLS), lambda i: (i, 0)),
                   pl.BlockSpec((3, N_PAIR, tm, 128), lambda i: (0, 0, i, 0))],
        out_shape=[jax.ShapeDtypeStruct((t, D_MODEL), BF16), jax.ShapeDtypeStruct((t, SHIFT_COLS), F32),
                   jax.ShapeDtypeStruct((3, N_PAIR, t, 128), F32)],
        compiler_params=_params(("parallel",)),
    )(x, g1, win)


def _shifted(p, last8, first):
    prow = jnp.where(first, 0.0, last8[7:8, :])
    rolled = pltpu.roll(p, 1, axis=0)
    rid = lax.broadcasted_iota(jnp.int32, p.shape, 0)
    return jnp.where(rid == 0, prow, rolled)


_PREP_TM = 256


def _prep_specs(tm):
    vec = lambda n: pl.BlockSpec((1, n), lambda i: (0, 0))
    mat = lambda r, n: pl.BlockSpec((r, n), lambda i: (0, 0))
    return [vec(SHIFT_COLS), vec(RW), mat(128, RW), vec(RW), mat(128, RW), mat(128, RW), vec(RW), vec(RW)]


def _prep_fwd(proj, pw):
    t = proj.shape[0]
    tm = _PREP_TM

    def body(p_ref, l8_ref, mu, w0, w2p, a0, a2p, g2, k_k, k_a, *outs):
        p = p_ref[...]
        pprev = _shifted(p, l8_ref[...], pl.program_id(0) == 0)
        res = _prep_fn(p, pprev, mu[...], w0[...], w2p[...], a0[...], a2p[...], g2[...], k_k[...], k_a[...])
        for o_ref, val in zip(outs, res):
            o_ref[...] = val

    row = pl.BlockSpec((tm, RW), lambda i: (i, 0))
    return pl.pallas_call(
        body, name="rwkv_prep", grid=(t // tm,),
        in_specs=[pl.BlockSpec((tm, SHIFT_COLS), lambda i: (i, 0)),
                  pl.BlockSpec((8, SHIFT_COLS), lambda i: (jnp.maximum(i * (tm // 8) - 1, 0), 0))] + _prep_specs(tm),
        out_specs=[row] * 7,
        out_shape=[jax.ShapeDtypeStruct((t, RW), F32)] * 7,
        compiler_params=_params(("parallel",)),
    )(proj, proj, *pw)


def _pairs(ref):
    return jnp.stack([ref[:, 128 * p:128 * (p + 1)] for p in range(N_PAIR)], axis=0)


def _wkv_fwd(r, lw, k2, v, kk, a):
    t = r.shape[0]
    nc = t // CHUNK

    def body(r_ref, lw_ref, k_ref, v_ref, kk_ref, a_ref, y_ref, s_ref, st):
        @pl.when(pl.program_id(0) == 0)
        def _():
            st[...] = jnp.zeros_like(st)

        s0 = st[...]
        s_ref[0] = s0
        y, s1 = _wkv_chunk_fn(s0, *[_pairs(ref) for ref in (r_ref, lw_ref, k_ref, v_ref, kk_ref, a_ref)])
        for p in range(N_PAIR):
            y_ref[:, 128 * p:128 * (p + 1)] = y[p]
        st[...] = s1

    blk = pl.BlockSpec((CHUNK, RW), lambda c: (c, 0))
    return pl.pallas_call(
        body, name="wkv_fwd", grid=(nc,),
        in_specs=[blk] * 6,
        out_specs=[blk, pl.BlockSpec((1, N_PAIR, 128, 128), lambda c: (c, 0, 0, 0))],
        out_shape=[jax.ShapeDtypeStruct((t, RW), F32), jax.ShapeDtypeStruct((nc, N_PAIR, 128, 128), F32)],
        scratch_shapes=[pltpu.VMEM((N_PAIR, 128, 128), F32)],
        compiler_params=_params(("arbitrary",)),
    )(r, lw, k2, v, kk, a)


_POST_TM = 256


def _post_fwd(y, r, k2, v, g, lnw, lnb, rk):
    t = y.shape[0]
    tm = _POST_TM

    def body(y_ref, r_ref, k_ref, v_ref, g_ref, lnw_ref, lnb_ref, rk_ref, o_ref):
        o_ref[...] = _post_fn(y_ref[...], r_ref[...], k_ref[...], v_ref[...], g_ref[...],
                              lnw_ref[...], lnb_ref[...], rk_ref[...]).astype(BF16)

    row = pl.BlockSpec((tm, RW), lambda i: (i, 0))
    vec = pl.BlockSpec((1, RW), lambda i: (0, 0))
    return pl.pallas_call(
        body, name="rwkv_post", grid=(t // tm,),
        in_specs=[row] * 5 + [vec] * 3, out_specs=row,
        out_shape=jax.ShapeDtypeStruct((t, RW), BF16),
        compiler_params=_params(("parallel",)),
    )(y, r, k2, v, g, lnw, lnb, rk)


ATTN_GROUP = 2


def _dilated_rows(d, r, n):
    if d == 1:
        return pl.ds(pl.multiple_of(n * ATTN_BLOCK, ATTN_BLOCK), ATTN_BLOCK)
    return pl.ds(r + n * (ATTN_BLOCK * d), ATTN_BLOCK, stride=d)


def _for_each_sequence(t, unit):
    for di, d in enumerate(DILATIONS):
        nb = t // (ATTN_BLOCK * d)

        def residue(r, carry, di=di, d=d, nb=nb):
            unit(di, d, r, 0, False)
            if nb > 1:
                lax.fori_loop(1, nb, lambda n, c: (unit(di, d, r, n, True), c)[1], 0)
            return carry

        if d == 1:
            residue(0, 0)
        else:
            lax.fori_loop(0, d, residue, 0)


def _take(ref, lead, rows):
    return jnp.stack([ref.at[(*lead, g)][rows, :] for g in range(ATTN_GROUP)], axis=0)


def _attn_fwd(qkv):
    t = qkv.shape[2]

    def body(q_ref, k_ref, v_ref, o_ref, l_ref):
        def unit(di, d, r, n, has_prev):
            cur = _dilated_rows(d, r, n)
            args = [_take(ref, (0,), cur) for ref in (q_ref, k_ref, v_ref)]
            if has_prev:
                prv = _dilated_rows(d, r, n - 1)
                args += [_take(ref, (0,), prv) for ref in (k_ref, v_ref)]
            o, lse = _attn_block_fn(*args)
            for g in range(ATTN_GROUP):
                o_ref.at[di, g][cur, :] = o[g]
                l_ref.at[di, g][cur, :] = lse[g]

        _for_each_sequence(t, unit)

    spec = lambda j: pl.BlockSpec((1, ATTN_GROUP, t, 128), lambda i: (j, i, 0, 0))
    out = pl.BlockSpec((3, ATTN_GROUP, t, 128), lambda i: (0, i, 0, 0))
    return pl.pallas_call(
        body, name="attn_fwd", grid=(N_PAIR // ATTN_GROUP,),
        in_specs=[spec(0), spec(1), spec(2)], out_specs=[out, out],
        out_shape=[jax.ShapeDtypeStruct((3, N_PAIR, t, 128), F32)] * 2,
        compiler_params=_params(("parallel",)),
    )(qkv, qkv, qkv)


_COMB_TM = 256


def _combine_fwd(o, l, og):
    t = o.shape[2]
    tm = _COMB_TM

    def body(o_ref, l_ref, og_ref, y_ref):
        for p in range(N_PAIR):
            cols = slice(128 * p, 128 * (p + 1))
            y_ref[:, cols] = _combine_fn(o_ref[0, p], o_ref[1, p], o_ref[2, p], l_ref[0, p], l_ref[1, p], l_ref[2, p],
                                         og_ref[:, cols]).astype(BF16)

    blk = pl.BlockSpec((3, N_PAIR, tm, 128), lambda i: (0, 0, i, 0))
    return pl.pallas_call(
        body, name="attn_combine", grid=(t // tm,),
        in_specs=[blk, blk, pl.BlockSpec((1, RW), lambda i: (0, 0))], out_specs=pl.BlockSpec((tm, RW), lambda i: (i, 0)),
        out_shape=jax.ShapeDtypeStruct((t, RW), BF16),
        compiler_params=_params(("parallel",)),
    )(o, l, og)


def _out_proj(x, ycat, wout, g2):
    t = x.shape[0]
    tm = 256

    def body(x_ref, y_ref, w_ref, g_ref, x1_ref, h_ref):
        x1 = x_ref[...] + _dot(y_ref[...], w_ref[...])
        x1_ref[...] = x1
        h_ref[...] = _rms_fwd(x1, g_ref[...]).astype(BF16)

    row = pl.BlockSpec((tm, D_MODEL), lambda i: (i, 0))
    return pl.pallas_call(
        body, name="out_proj", grid=(t // tm,),
        in_specs=[row, row, pl.BlockSpec((D_MODEL, D_MODEL), lambda i: (0, 0)), pl.BlockSpec((1, D_MODEL), lambda i: (0, 0))],
        out_specs=[row, row],
        out_shape=[jax.ShapeDtypeStruct((t, D_MODEL), F32), jax.ShapeDtypeStruct((t, D_MODEL), BF16)],
        compiler_params=_params(("parallel",)),
    )(x, ycat, wout, g2)


def _ffn_up(h2, wg, wu):
    t = h2.shape[0]
    tm = 256

    def body(h_ref, wg_ref, wu_ref, gt_ref, up_ref, act_ref):
        h = h_ref[...]
        gt = _dot(h, wg_ref[...])
        up = _dot(h, wu_ref[...])
        gt_ref[...] = gt
        up_ref[...] = up
        act_ref[...] = (gt * _sigmoid(gt) * up).astype(BF16)

    wide = pl.BlockSpec((tm, D_FF), lambda i: (i, 0))
    wsp = pl.BlockSpec((D_MODEL, D_FF), lambda i: (0, 0))
    return pl.pallas_call(
        body, name="ffn_up", grid=(t // tm,),
        in_specs=[pl.BlockSpec((tm, D_MODEL), lambda i: (i, 0)), wsp, wsp],
        out_specs=[wide, wide, wide],
        out_shape=[jax.ShapeDtypeStruct((t, D_FF), F32)] * 2 + [jax.ShapeDtypeStruct((t, D_FF), BF16)],
        compiler_params=_params(("parallel",)),
    )(h2, wg, wu)


def _ffn_down_loss(x1, act, wd, gf, tgt):
    t = x1.shape[0]
    tm = 256

    def body(x1_ref, a_ref, w_ref, g_ref, t_ref, dx_ref, dxb_ref, loss_ref, dg_ref):
        first = pl.program_id(0) == 0
        x2 = x1_ref[...] + _dot(a_ref[...], w_ref[...])
        g = g_ref[...]
        diff = _rms_fwd(x2, g) - t_ref[...]
        lrow = 0.5 * jnp.sum(_colsum8(diff * diff), axis=1, keepdims=True) * (1.0 / D_MODEL)
        _acc(loss_ref, jnp.broadcast_to(lrow, (8, 128)), first)
        dx2, dgr = _rms_bwd(diff * (1.0 / D_MODEL), x2, g)
        dx_ref[...] = dx2
        dxb_ref[...] = dx2.astype(BF16)
        _acc(dg_ref, _colsum8(dgr), first)

    row = pl.BlockSpec((tm, D_MODEL), lambda i: (i, 0))
    return pl.pallas_call(
        body, name="ffn_down_loss", grid=(t // tm,),
        in_specs=[row, pl.BlockSpec((tm, D_FF), lambda i: (i, 0)), pl.BlockSpec((D_FF, D_MODEL), lambda i: (0, 0)),
                  pl.BlockSpec((1, D_MODEL), lambda i: (0, 0)), row],
        out_specs=[row, row, pl.BlockSpec((8, 128), lambda i: (0, 0)), pl.BlockSpec((8, D_MODEL), lambda i: (0, 0))],
        out_shape=[jax.ShapeDtypeStruct((t, D_MODEL), F32), jax.ShapeDtypeStruct((t, D_MODEL), BF16),
                   jax.ShapeDtypeStruct((8, 128), F32), jax.ShapeDtypeStruct((8, D_MODEL), F32)],
        compiler_params=_params(("arbitrary",)),
    )(x1, act, wd, gf, tgt)


def _ffn_bwd_act(dx2b, wd, gt, up):
    t = dx2b.shape[0]
    tm = 256

    def body(dx_ref, w_ref, gt_ref, up_ref, dgt_ref, dup_ref):
        dact = _dot_nt(dx_ref[...], w_ref[...])
        gt = gt_ref[...]
        sg = _sigmoid(gt)
        dgt_ref[...] = (dact * up_ref[...] * sg * (1.0 + gt * (1.0 - sg))).astype(BF16)
        dup_ref[...] = (dact * gt * sg).astype(BF16)

    wide = pl.BlockSpec((tm, D_FF), lambda i: (i, 0))
    return pl.pallas_call(
        body, name="ffn_bwd_act", grid=(t // tm,),
        in_specs=[pl.BlockSpec((tm, D_MODEL), lambda i: (i, 0)), pl.BlockSpec((D_FF, D_MODEL), lambda i: (0, 0)), wide, wide],
        out_specs=[wide, wide],
        out_shape=[jax.ShapeDtypeStruct((t, D_FF), BF16)] * 2,
        compiler_params=_params(("parallel",)),
    )(dx2b, wd, gt, up)


def _ffn_bwd_h(dgt, dup, wg, wu, dx2, x1, g2, wout):
    t = dgt.shape[0]
    tm = 256

    def body(dgt_ref, dup_ref, wg_ref, wu_ref, dx2_ref, x1_ref, g_ref, wo_ref, dx1_ref, dx1b_ref, dya_ref, dyb_ref, dg_ref):
        dh = _dot_nt(dgt_ref[...], wg_ref[...]) + _dot_nt(dup_ref[...], wu_ref[...])
        dxn, dgr = _rms_bwd(dh, x1_ref[...], g_ref[...])
        dx1 = dx2_ref[...] + dxn
        dx1_ref[...] = dx1
        dx1b = dx1.astype(BF16)
        dx1b_ref[...] = dx1b
        dy = _dot_nt(dx1b, wo_ref[...])
        dya_ref[...] = dy[:, :RW]
        dyb_ref[...] = dy[:, RW:]
        _acc(dg_ref, _colsum8(dgr), pl.program_id(0) == 0)

    wide = pl.BlockSpec((tm, D_FF), lambda i: (i, 0))
    row = pl.BlockSpec((tm, D_MODEL), lambda i: (i, 0))
    half = pl.BlockSpec((tm, RW), lambda i: (i, 0))
    wsp = pl.BlockSpec((D_MODEL, D_FF), lambda i: (0, 0))
    return pl.pallas_call(
        body, name="ffn_bwd_h", grid=(t // tm,),
        in_specs=[wide, wide, wsp, wsp, row, row, pl.BlockSpec((1, D_MODEL), lambda i: (0, 0)),
                  pl.BlockSpec((D_MODEL, D_MODEL), lambda i: (0, 0))],
        out_specs=[row, row, half, half, pl.BlockSpec((8, D_MODEL), lambda i: (0, 0))],
        out_shape=[jax.ShapeDtypeStruct((t, D_MODEL), F32), jax.ShapeDtypeStruct((t, D_MODEL), BF16),
                   jax.ShapeDtypeStruct((t, RW), F32), jax.ShapeDtypeStruct((t, RW), F32),
                   jax.ShapeDtypeStruct((8, D_MODEL), F32)],
        compiler_params=_params(("arbitrary",)),
    )(dgt, dup, wg, wu, dx2, x1, g2, wout)


def _wgrad(a, b, tk, tn, name):
    t, kdim = a.shape
    ndim = b.shape[1]

    def body(a_ref, b_ref, o_ref):
        o_ref[...] = _dot_tn(a_ref[...], b_ref[...])

    return pl.pallas_call(
        body, name=name, grid=(kdim // tk, ndim // tn),
        in_specs=[pl.BlockSpec((t, tk), lambda i, j: (0, i)), pl.BlockSpec((t, tn), lambda i, j: (0, j))],
        out_specs=pl.BlockSpec((tk, tn), lambda i, j: (i, j)),
        out_shape=jax.ShapeDtypeStruct((kdim, ndim), F32),
        compiler_params=_params(("parallel", "parallel")),
    )(a, b)


def _post_bwd(dya, y, r, k2, v, g, lnw, lnb, rk):
    t = y.shape[0]
    tm = _POST_TM

    def body(d_ref, y_ref, r_ref, k_ref, v_ref, g_ref, lnw_ref, lnb_ref, rk_ref,
             dy_ref, dr_ref, dk_ref, dv_ref, dg_ref, dlnw_ref, dlnb_ref, drk_ref):
        first = pl.program_id(0) == 0
        ones = jnp.ones((tm, 1), F32)
        prim = (y_ref[...], r_ref[...], k_ref[...], v_ref[...], g_ref[...],
                ones * lnw_ref[...], ones * lnb_ref[...], ones * rk_ref[...])
        _, vjp = jax.vjp(_post_fn, *prim)
        dy, dr, dk, dv, dg, dlnw, dlnb, drk = vjp(d_ref[...])
        dy_ref[...] = dy
        dr_ref[...] = dr
        dk_ref[...] = dk
        dv_ref[...] = dv
        dg_ref[...] = dg
        _acc(dlnw_ref, _colsum8(dlnw), first)
        _acc(dlnb_ref, _colsum8(dlnb), first)
        _acc(drk_ref, _colsum8(drk), first)

    row = pl.BlockSpec((tm, RW), lambda i: (i, 0))
    vec = pl.BlockSpec((1, RW), lambda i: (0, 0))
    part = pl.BlockSpec((8, RW), lambda i: (0, 0))
    return pl.pallas_call(
        body, name="rwkv_post_bwd", grid=(t // tm,),
        in_specs=[row] * 6 + [vec] * 3, out_specs=[row] * 5 + [part] * 3,
        out_shape=[jax.ShapeDtypeStruct((t, RW), F32)] * 5 + [jax.ShapeDtypeStruct((8, RW), F32)] * 3,
        compiler_params=_params(("arbitrary",)),
    )(dya, y, r, k2, v, g, lnw, lnb, rk)


def _wkv_bwd(dy, s0s, r, lw, k2, v, kk, a):
    t = r.shape[0]
    nc = t // CHUNK

    def body(dy_ref, s_ref, r_ref, lw_ref, k_ref, v_ref, kk_ref, a_ref,
             dr_ref, dlw_ref, dk_ref, dv_ref, dkk_ref, da_ref, ds):
        @pl.when(pl.program_id(0) == 0)
        def _():
            ds[...] = jnp.zeros_like(ds)

        _, vjp = jax.vjp(_wkv_chunk_fn, s_ref[0],
                         *[_pairs(ref) for ref in (r_ref, lw_ref, k_ref, v_ref, kk_ref, a_ref)])
        res = vjp((_pairs(dy_ref), ds[...]))
        ds[...] = res[0]
        for ref, val in zip((dr_ref, dlw_ref, dk_ref, dv_ref, dkk_ref, da_ref), res[1:]):
            for p in range(N_PAIR):
                ref[:, 128 * p:128 * (p + 1)] = val[p]

    blk = pl.BlockSpec((CHUNK, RW), lambda c: (nc - 1 - c, 0))
    return pl.pallas_call(
        body, name="wkv_bwd", grid=(nc,),
        in_specs=[blk, pl.BlockSpec((1, N_PAIR, 128, 128), lambda c: (nc - 1 - c, 0, 0, 0))] + [blk] * 6,
        out_specs=[blk] * 6,
        out_shape=[jax.ShapeDtypeStruct((t, RW), F32)] * 6,
        scratch_shapes=[pltpu.VMEM((N_PAIR, 128, 128), F32)],
        compiler_params=_params(("arbitrary",)),
    )(dy, s0s, r, lw, k2, v, kk, a)


def _prep_bwd(proj, pw, douts):
    t = proj.shape[0]
    tm = _PREP_TM
    nt = t // tm

    def body(p_ref, l8_ref, mu, w0, w2p, a0, a2p, g2, k_k, k_a, dr, dr2, dlw, dk2, dk22, dv, dv2, dkk, da, dg,
             dp_ref, dmu_ref, dw0_ref, dw2_ref, da0_ref, da2_ref, dg2_ref, dkk_ref, dka_ref, carry):
        i = pl.program_id(0)
        first = i == 0

        @pl.when(first)
        def _():
            carry[...] = jnp.zeros_like(carry)

        p = p_ref[...]
        pprev = _shifted(p, l8_ref[...], i == nt - 1)
        ones = jnp.ones((tm, 1), F32)
        prim = (p, pprev, ones * mu[...], ones * w0[...], w2p[...], ones * a0[...], a2p[...], g2[...],
                ones * k_k[...], ones * k_a[...])
        _, vjp = jax.vjp(_prep_fn, *prim)
        dp, dpp, dmu, dw0, dw2, da0, da2, dg2, dkk_, dka = vjp(
            (dr[...] + dr2[...], dlw[...], dk2[...] + dk22[...], dv[...] + dv2[...], dkk[...], da[...], dg[...]))
        up = pltpu.roll(dpp, tm - 1, axis=0)
        rid = lax.broadcasted_iota(jnp.int32, dpp.shape, 0)
        dp_ref[...] = dp + jnp.where(rid == tm - 1, carry[0:1, :], up)
        carry[...] = jnp.broadcast_to(dpp[0:1, :], carry.shape)
        _acc(dmu_ref, _colsum8(dmu), first)
        _acc(dw0_ref, _colsum8(dw0), first)
        _acc(dw2_ref, dw2, first)
        _acc(da0_ref, _colsum8(da0), first)
        _acc(da2_ref, da2, first)
        _acc(dg2_ref, dg2, first)
        _acc(dkk_ref, _colsum8(dkk_), first)
        _acc(dka_ref, _colsum8(dka), first)

    rev = lambda i: (nt - 1 - i, 0)
    row = pl.BlockSpec((tm, RW), rev)
    part = lambda n: pl.BlockSpec((8, n), lambda i: (0, 0))
    mat = pl.BlockSpec((128, RW), lambda i: (0, 0))
    return pl.pallas_call(
        body, name="rwkv_prep_bwd", grid=(nt,),
        in_specs=[pl.BlockSpec((tm, SHIFT_COLS), rev),
                  pl.BlockSpec((8, SHIFT_COLS), lambda i: (jnp.maximum((nt - 1 - i) * (tm // 8) - 1, 0), 0))]
                 + _prep_specs(tm) + [row] * 10,
        out_specs=[pl.BlockSpec((tm, SHIFT_COLS), rev), part(SHIFT_COLS), part(RW), mat, part(RW), mat, mat,
                   part(RW), part(RW)],
        out_shape=[jax.ShapeDtypeStruct((t, SHIFT_COLS), F32), jax.ShapeDtypeStruct((8, SHIFT_COLS), F32),
                   jax.ShapeDtypeStruct((8, RW), F32), jax.ShapeDtypeStruct((128, RW), F32),
                   jax.ShapeDtypeStruct((8, RW), F32), jax.ShapeDtypeStruct((128, RW), F32),
                   jax.ShapeDtypeStruct((128, RW), F32), jax.ShapeDtypeStruct((8, RW), F32),
                   jax.ShapeDtypeStruct((8, RW), F32)],
        scratch_shapes=[pltpu.VMEM((8, SHIFT_COLS), F32)],
        compiler_params=_params(("arbitrary",)),
    )(proj, proj, *pw, *douts)


def _combine_bwd(dyb, o, l, og):
    t = dyb.shape[0]
    tm = _COMB_TM

    def body(d_ref, o_ref, l_ref, og_ref, do_ref, dl_ref, dog_ref):
        ones = jnp.ones((tm, 1), F32)
        dog = []
        for p in range(N_PAIR):
            cols = slice(128 * p, 128 * (p + 1))
            _, vjp = jax.vjp(_combine_fn, o_ref[0, p], o_ref[1, p], o_ref[2, p], l_ref[0, p], l_ref[1, p], l_ref[2, p],
                             ones * og_ref[:, cols])
            res = vjp(d_ref[:, cols])
            for b in range(3):
                do_ref[b, p] = res[b]
                dl_ref[b, p] = res[3 + b]
            dog.append(_colsum8(res[6]))
        _acc(dog_ref, jnp.concatenate(dog, axis=1), pl.program_id(0) == 0)

    blk = pl.BlockSpec((3, N_PAIR, tm, 128), lambda i: (0, 0, i, 0))
    return pl.pallas_call(
        body, name="attn_combine_bwd", grid=(t // tm,),
        in_specs=[pl.BlockSpec((tm, RW), lambda i: (i, 0)), blk, blk, pl.BlockSpec((1, RW), lambda i: (0, 0))],
        out_specs=[blk, blk, pl.BlockSpec((8, RW), lambda i: (0, 0))],
        out_shape=[jax.ShapeDtypeStruct((3, N_PAIR, t, 128), F32)] * 2 + [jax.ShapeDtypeStruct((8, RW), F32)],
        compiler_params=_params(("arbitrary",)),
    )(dyb, o, l, og)


def _attn_bwd(do, dl, qkv):
    t = qkv.shape[2]

    def body(do_ref, dl_ref, q_ref, k_ref, v_ref, dq_ref, dk_ref, dv_ref):
        for ref in (dq_ref, dk_ref, dv_ref):
            ref[...] = jnp.zeros_like(ref)

        def add(ref, rows, val):
            for g in range(ATTN_GROUP):
                ref.at[g][rows, :] += val[g]

        def unit(di, d, r, n, has_prev):
            cur = _dilated_rows(d, r, n)
            args = [_take(ref, (0,), cur) for ref in (q_ref, k_ref, v_ref)]
            if has_prev:
                prv = _dilated_rows(d, r, n - 1)
                args += [_take(ref, (0,), prv) for ref in (k_ref, v_ref)]
            _, vjp = jax.vjp(_attn_block_fn, *args)
            res = vjp((_take(do_ref, (di,), cur), _take(dl_ref, (di,), cur)))
            add(dq_ref, cur, res[0])
            add(dk_ref, cur, res[1])
            add(dv_ref, cur, res[2])
            if has_prev:
                add(dk_ref, prv, res[3])
                add(dv_ref, prv, res[4])

        _for_each_sequence(t, unit)

    spec = lambda j: pl.BlockSpec((1, ATTN_GROUP, t, 128), lambda i: (j, i, 0, 0))
    three = pl.BlockSpec((3, ATTN_GROUP, t, 128), lambda i: (0, i, 0, 0))
    out = pl.BlockSpec((ATTN_GROUP, t, 128), lambda i: (i, 0, 0))
    return pl.pallas_call(
        body, name="attn_bwd", grid=(N_PAIR // ATTN_GROUP,),
        in_specs=[three, three, spec(0), spec(1), spec(2)], out_specs=[out] * 3,
        out_shape=[jax.ShapeDtypeStruct((N_PAIR, t, 128), F32)] * 3,
        compiler_params=_params(("parallel",)),
    )(do, dl, qkv, qkv, qkv)


def _in_proj_bwd(dpa, dq, dk, dv, win, x, g1, dx1):
    t = x.shape[0]
    tm = 256

    def body(dpa_ref, dq_ref, dk_ref, dv_ref, w_ref, x_ref, g_ref, dx1_ref, dproj_ref, dx_ref, dg_ref):
        parts = [dpa_ref[...]] + [ref[p] for ref in (dq_ref, dk_ref, dv_ref) for p in range(N_PAIR)]
        dproj = jnp.concatenate([z.astype(BF16) for z in parts], axis=1)
        dproj_ref[...] = dproj
        dh = _dot_nt(dproj, w_ref[...])
        dxn, dgr = _rms_bwd(dh, x_ref[...], g_ref[...])
        dx_ref[...] = dx1_ref[...] + dxn
        _acc(dg_ref, _colsum8(dgr), pl.program_id(0) == 0)

    row = pl.BlockSpec((tm, D_MODEL), lambda i: (i, 0))
    pair = pl.BlockSpec((N_PAIR, tm, 128), lambda i: (0, i, 0))
    return pl.pallas_call(
        body, name="in_proj_bwd", grid=(t // tm,),
        in_specs=[pl.BlockSpec((tm, SHIFT_COLS), lambda i: (i, 0))] + [pair] * 3
                 + [pl.BlockSpec((D_MODEL, IN_COLS), lambda i: (0, 0)), row, pl.BlockSpec((1, D_MODEL), lambda i: (0, 0)), row],
        out_specs=[pl.BlockSpec((tm, IN_COLS), lambda i: (i, 0)), row, pl.BlockSpec((8, D_MODEL), lambda i: (0, 0))],
        out_shape=[jax.ShapeDtypeStruct((t, IN_COLS), BF16), jax.ShapeDtypeStruct((t, D_MODEL), F32),
                   jax.ShapeDtypeStruct((8, D_MODEL), F32)],
        compiler_params=_params(("arbitrary",)),
    )(dpa, dq, dk, dv, win, x, g1, dx1)


def _pad_lora(w, lo):
    z = jnp.zeros((64, RW), F32)
    return jnp.concatenate([w, z], axis=0) if lo == 0 else jnp.concatenate([z, w], axis=0)


def _local_step(x, tgt, win, vecs, w2, a2, g2m, get_rest, send_rest):
    pw = (vecs["mu_shift"], vecs["decay_w0"], _pad_lora(w2, 0), vecs["iclr_a0"], _pad_lora(a2, 64), g2m,
          vecs["k_k"], vecs["k_a"])
    h, proj, qkv = _in_proj(x, vecs["mix_norm_g"], win)
    r, lw, k2, v, kk, a, g = _prep_fwd(proj, pw)
    y, s0s = _wkv_fwd(r, lw, k2, v, kk, a)
    ya = _post_fwd(y, r, k2, v, g, vecs["ln_x_w"], vecs["ln_x_b"], vecs["r_k"])
    o_att, l_att = _attn_fwd(qkv)
    yb = _combine_fwd(o_att, l_att, vecs["attn_out_g"])

    wout, wg, wu, wd = get_rest(yb)
    ycat = jnp.concatenate([ya, yb], axis=1)
    x1, h2 = _out_proj(x, ycat, wout, vecs["ffn_norm_g"])
    gt, up, act = _ffn_up(h2, wg, wu)
    dx2, dx2b, loss8, dgf = _ffn_down_loss(x1, act, wd, vecs["final_norm_g"], tgt)

    dgt, dup = _ffn_bwd_act(dx2b, wd, gt, up)
    dx1, dx1b, dya, dyb, dg2n = _ffn_bwd_h(dgt, dup, wg, wu, dx2, x1, vecs["ffn_norm_g"], wout)
    gw = {
        "w_down": _wgrad(act, dx2b, 1408, 1024, "wgrad_down"),
        "w_gate": _wgrad(h2, dgt, 1024, 1408, "wgrad_gate"),
        "w_up": _wgrad(h2, dup, 1024, 1408, "wgrad_up"),
        "w_out": _wgrad(ycat, dx1b, 1024, 1024, "wgrad_out"),
    }

    lnw = vecs["ln_x_w"] + send_rest(gw)[0, 0]
    dy, dr_p, dk2_p, dv_p, dg, dlnw, dlnb, drk = _post_bwd(dya, y, r, k2, v, g, lnw, vecs["ln_x_b"], vecs["r_k"])
    dr_s, dlw, dk2_s, dv_s, dkk, da = _wkv_bwd(dy, s0s, r, lw, k2, v, kk, a)
    dpa, dmu, dw0, dw2p, da0, da2p, dg2m, dk_k, dk_a = _prep_bwd(
        proj, pw, (dr_p, dr_s, dlw, dk2_p, dk2_s, dv_p, dv_s, dkk, da, dg))

    do_att, dl_att, dog = _combine_bwd(dyb, o_att, l_att, vecs["attn_out_g"])
    dq, dk, dv = _attn_bwd(do_att, dl_att, qkv)
    dproj, dx, dg1 = _in_proj_bwd(dpa, dq, dk, dv, win, x, vecs["mix_norm_g"], dx1)
    gw["w_in"] = _wgrad(h, dproj, 1024, 1664, "wgrad_in")
    gw["decay_w2"] = dw2p[:64]
    gw["iclr_a2"] = da2p[64:]
    gw["gate_g2"] = dg2m
    gv = {"mix_norm_g": dg1, "mu_shift": dmu, "decay_w0": dw0, "iclr_a0": da0, "k_k": dk_k, "k_a": dk_a, "r_k": drk,
          "ln_x_w": dlnw, "ln_x_b": dlnb, "attn_out_g": dog, "ffn_norm_g": dg2n, "final_norm_g": dgf}
    return loss8, dx, gw, gv


N_CHIP = 4
N_DEV = 8
MATS = ("w_in", "w_out", "w_gate", "w_up", "w_down")
LORAS = ("decay_w2", "iclr_a2", "gate_g2")
VECS = (("mix_norm_g", 1024), ("mu_shift", 1792), ("decay_w0", 512), ("iclr_a0", 512), ("k_k", 512), ("k_a", 512),
        ("r_k", 512), ("ln_x_w", 512), ("ln_x_b", 512), ("attn_out_g", 512), ("ffn_norm_g", 1024),
        ("final_norm_g", 1024))
N_VEC = sum(n for _, n in VECS)
N_SMALL = N_VEC + 128
ROWS_PAD = 3328
ANY = pl.BlockSpec(memory_space=pl.ANY)


def _flip(v, f):
    return 1 - v if f else v


class _Me:
    def __init__(self, mode):
        x, y, c = lax.axis_index("x"), lax.axis_index("y"), lax.axis_index("c")
        self.core, self.chip, self.dev = c, 2 * x + y, 4 * x + 2 * y + c
        self.sibling = (x, y, 1 - c)
        if mode == "chips":
            self.peers = [(px, py, c) for px, py in ((1 - x, y), (x, 1 - y), (1 - x, 1 - y))]
        else:
            self.peers = [(_flip(x, k & 4), _flip(y, k & 2), _flip(c, k & 1)) for k in range(1, N_DEV)]


def _half(core, rows):
    h = rows // 2
    return pl.ds(pl.multiple_of(core * h, h), h)


def _landing(a, kind):
    if kind == "gather":
        return (N_CHIP,) + a.shape
    if kind == "scatter":
        return (N_DEV, a.shape[1] // 2, a.shape[2])
    return (N_DEV,) + a.shape


def _peer_copy(srcs, dsts, kinds, send_sems, recv_sems, me, j, i, incoming):
    px, py, pc = me.peers[j]
    pchip, pdev = 2 * px + py, 4 * px + 2 * py + pc
    src, dst, kind = srcs[i], dsts[i], kinds[i]
    if kind == "gather":
        rows = _half(me.core, src.shape[0])
        src, dst = src.at[rows], dst.at[pchip if incoming else me.chip, rows]
    elif kind == "scatter":
        src, dst = src.at[pchip, _half(pc, src.shape[1])], dst.at[pdev if incoming else me.dev]
    else:
        dst = dst.at[pdev if incoming else me.dev]
    n = len(srcs)
    return pltpu.make_async_remote_copy(src_ref=src, dst_ref=dst, send_sem=send_sems.at[n * j + i],
                                        recv_sem=recv_sems.at[n * j + i], device_id=(px, py, pc), device_id_type=MESH)


def _mode(kinds):
    return "chips" if kinds[0] == "gather" else "devs"


def _npeer(kinds):
    return N_CHIP - 1 if kinds[0] == "gather" else N_DEV - 1


def _swap_blocking(arrs, lands, kinds, name):
    n = len(arrs)

    def body(*refs):
        srcs, dsts = refs[:n], refs[2 * n:3 * n]
        send_sems, recv_sems = refs[3 * n:]
        me = _Me(_mode(kinds))
        sends = [_peer_copy(srcs, dsts, kinds, send_sems, recv_sems, me, j, i, False)
                 for j in range(len(me.peers)) for i in range(n)]
        for cp in sends:
            cp.start()
        for j in range(len(me.peers)):
            for i in range(n):
                _peer_copy(srcs, dsts, kinds, send_sems, recv_sems, me, j, i, True).wait_recv()
        for cp in sends:
            cp.wait_send()

    ns = _npeer(kinds) * n
    return pl.pallas_call(
        body, name=name, in_specs=[ANY] * (2 * n), out_specs=[ANY] * n,
        out_shape=[jax.ShapeDtypeStruct(l.shape, l.dtype) for l in lands],
        input_output_aliases={n + i: i for i in range(n)},
        scratch_shapes=[pltpu.SemaphoreType.DMA((ns,)), pltpu.SemaphoreType.DMA((ns,))],
    )(*arrs, *lands)


def _swap_gathered(lands, name):
    n = len(lands)

    def body(*refs):
        dsts, send_sems, recv_sems = refs[n:2 * n], refs[2 * n], refs[2 * n + 1]
        me = _Me("chips")

        def copy(j, i, incoming):
            px, py, _ = me.peers[j]
            rows_out, rows_in = _half(me.core, dsts[i].shape[1]), _half(1 - me.core, dsts[i].shape[1])
            return pltpu.make_async_remote_copy(
                src_ref=dsts[i].at[2 * px + py, rows_out], dst_ref=dsts[i].at[2 * px + py, rows_in if incoming else rows_out],
                send_sem=send_sems.at[n * j + i], recv_sem=recv_sems.at[n * j + i], device_id=me.sibling, device_id_type=MESH)

        sends = [copy(j, i, False) for j in range(3) for i in range(n)]
        for cp in sends:
            cp.start()
        for j in range(3):
            for i in range(n):
                copy(j, i, True).wait_recv()
        for cp in sends:
            cp.wait_send()

    return pl.pallas_call(
        body, name=name, in_specs=[ANY] * n, out_specs=[ANY] * n,
        out_shape=[jax.ShapeDtypeStruct(l.shape, l.dtype) for l in lands],
        input_output_aliases={i: i for i in range(n)},
        scratch_shapes=[pltpu.SemaphoreType.DMA((3 * n,)), pltpu.SemaphoreType.DMA((3 * n,))],
    )(*lands)


def _join_halves(sums, name):
    n = len(sums)

    def body(*refs):
        dsts, send_sems, recv_sems = refs[n:2 * n], refs[2 * n], refs[2 * n + 1]
        me = _Me("chips")

        def copy(i, incoming):
            mine, other = _half(me.core, dsts[i].shape[0]), _half(1 - me.core, dsts[i].shape[0])
            return pltpu.make_async_remote_copy(src_ref=dsts[i].at[mine], dst_ref=dsts[i].at[other if incoming else mine],
                                                send_sem=send_sems.at[i], recv_sem=recv_sems.at[i],
                                                device_id=me.sibling, device_id_type=MESH)

        sends = [copy(i, False) for i in range(n)]
        for cp in sends:
            cp.start()
        for i in range(n):
            copy(i, True).wait_recv()
        for cp in sends:
            cp.wait_send()

    return pl.pallas_call(
        body, name=name, in_specs=[ANY] * n, out_specs=[ANY] * n,
        out_shape=[jax.ShapeDtypeStruct(s.shape, s.dtype) for s in sums],
        input_output_aliases={i: i for i in range(n)},
        scratch_shapes=[pltpu.SemaphoreType.DMA((n,)), pltpu.SemaphoreType.DMA((n,))],
    )(*sums)


HBM = pl.BlockSpec(memory_space=pltpu.HBM)
SEM = pl.BlockSpec(memory_space=pltpu.SEMAPHORE)
EFFECT = pltpu.SideEffectType.DATAFLOW_SIDE_EFFECTING


def _swap_start(arrs, lands, kinds, name):
    n = len(arrs)

    def body(*refs):
        srcs, dsts, send_sems, recv_sems, token = refs[:n], refs[n:2 * n], refs[2 * n], refs[2 * n + 1], refs[-1]
        me = _Me(_mode(kinds))
        for j in range(len(me.peers)):
            for i in range(n):
                _peer_copy(srcs, dsts, kinds, send_sems, recv_sems, me, j, i, False).start()
        token[...] = jnp.zeros_like(token)

    ns = _npeer(kinds) * n
    outs = pl.pallas_call(
        body, name=name,
        out_shape=(pltpu.SemaphoreType.DMA((ns,)), pltpu.SemaphoreType.DMA((ns,)),
                   *[pltpu.HBM(a.shape, a.dtype) for a in arrs], *[pltpu.HBM(l.shape, l.dtype) for l in lands],
                   jax.ShapeDtypeStruct((8, 128), F32)),
        in_specs=[HBM] * (2 * n), out_specs=(SEM, SEM, *[HBM] * (2 * n), pl.BlockSpec(memory_space=pltpu.VMEM)),
        input_output_aliases={k: 2 + k for k in range(2 * n)},
        compiler_params=pltpu.CompilerParams(has_side_effects=EFFECT),
    )(*[pltpu.with_memory_space_constraint(a, pltpu.HBM) for a in arrs],
      *[pltpu.with_memory_space_constraint(l, pltpu.HBM) for l in lands])
    return outs[0], outs[1], outs[2:2 + n], outs[2 + n:2 + 2 * n], outs[-1]


def _swap_wait(send_sems, recv_sems, srcs_thru, lands_thru, after, kinds, name):
    n = len(srcs_thru)

    def body(*refs):
        srcs, dsts, s_sems, r_sems = refs[:n], refs[n:2 * n], refs[2 * n], refs[2 * n + 1]
        me = _Me(_mode(kinds))
        for j in range(len(me.peers)):
            for i in range(n):
                cp = _peer_copy(srcs, dsts, kinds, s_sems, r_sems, me, j, i, True)
                cp.wait_send()
                cp.wait_recv()

    outs = pl.pallas_call(
        body, name=name,
        out_shape=tuple(pltpu.HBM(a.shape, a.dtype) for a in (*srcs_thru, *lands_thru)),
        in_specs=[HBM] * (2 * n) + [SEM, SEM, ANY], out_specs=tuple([HBM] * (2 * n)),
        input_output_aliases={k: k for k in range(2 * n)},
        compiler_params=pltpu.CompilerParams(has_side_effects=EFFECT),
    )(*srcs_thru, *lands_thru, send_sems, recv_sems, after)
    return outs[n:]


def _adamw(w, g, m, v):
    m = ADAM_B1 * m + (1.0 - ADAM_B1) * g
    v = ADAM_B2 * v + (1.0 - ADAM_B2) * (g * g)
    m_hat = m / (1.0 - ADAM_B1 ** ADAM_STEP)
    v_hat = v / (1.0 - ADAM_B2 ** ADAM_STEP)
    delta = -ADAM_LR * (m_hat / (jnp.sqrt(v_hat) + ADAM_EPS) + ADAM_WD * w)
    return delta, m, v


def _reduce8(rbuf, core, tr, name):
    _, h, cols = rbuf.shape

    def body(core_ref, r_ref, g_ref):
        g = r_ref[0].astype(F32)
        for s in range(1, N_DEV):
            g = g + r_ref[s].astype(F32)
        g_ref[...] = g

    return pl.pallas_call(
        body, name=name,
        grid_spec=pltpu.PrefetchScalarGridSpec(
            num_scalar_prefetch=1, grid=(h // tr,),
            in_specs=[pl.BlockSpec((N_DEV, tr, cols), lambda i, core_ref: (0, i, 0))],
            out_specs=pl.BlockSpec((tr, cols), lambda i, core_ref: (core_ref[0] * (h // tr) + i, 0))),
        out_shape=jax.ShapeDtypeStruct((2 * h, cols), F32),
        compiler_params=_params(("parallel",)),
    )(core, rbuf)


def _adamw_call(g, w, m, v, tr, name):
    _, rows, cols = w.shape

    def body(g_in, w_ref, m_ref, v_ref, g_ref, d_ref, nm_ref, nv_ref):
        g = g_in[...]
        g_ref[0] = g
        d_ref[0], nm_ref[0], nv_ref[0] = _adamw(w_ref[0], g, m_ref[0], v_ref[0])

    row = pl.BlockSpec((1, tr, cols), lambda i: (0, i, 0))
    return pl.pallas_call(
        body, name=name, grid=(rows // tr,),
        in_specs=[pl.BlockSpec((tr, cols), lambda i: (i, 0)), row, row, row], out_specs=[row] * 4,
        out_shape=[jax.ShapeDtypeStruct(w.shape, F32)] * 4,
        compiler_params=_params(("parallel",)),
    )(g, w, m, v)


def _reduce_adamw_small(sbuf, w, m, v):
    def body(s_ref, w_ref, m_ref, v_ref, g_ref, d_ref, nm_ref, nv_ref, loss_ref):
        tot = s_ref[0]
        for s in range(1, N_DEV):
            tot = tot + s_ref[s]
        tot = jnp.sum(tot, axis=0, keepdims=True)
        g = tot[:, :N_VEC]
        g_ref[...] = g
        d_ref[...], nm_ref[...], nv_ref[...] = _adamw(w_ref[...], g, m_ref[...], v_ref[...])
        loss_ref[...] = tot[:, N_VEC:]

    return pl.pallas_call(
        body, name="reduce_adamw_small",
        out_shape=[jax.ShapeDtypeStruct((1, N_VEC), F32)] * 4 + [jax.ShapeDtypeStruct((1, 128), F32)],
    )(sbuf, w, m, v)


_ROW_SHARDED = ("w_out", "w_down")
_ADAM_TILE = {"w_in": 256, "w_out": 256, "w_gate": 256, "w_up": 256, "w_down": 176, "decay_w2": 64, "iclr_a2": 64,
              "gate_g2": 128}
_SUM_TILE = {"w_in": 256, "w_out": 128, "w_gate": 256, "w_up": 256, "w_down": 176, "decay_w2": 32, "iclr_a2": 32,
             "gate_g2": 64}


def _full(n, stacked):
    p, r, c = stacked.shape
    if n in _ROW_SHARDED:
        return stacked.reshape(p * r, c)
    return jnp.transpose(stacked, (1, 0, 2)).reshape(r, p * c)


def _by_chip(n, full):
    if n in _ROW_SHARDED:
        return full.reshape(N_CHIP, full.shape[0] // N_CHIP, full.shape[1])
    r, c = full.shape
    return jnp.transpose(full.reshape(r, N_CHIP, c // N_CHIP), (1, 0, 2))


def _with_own(land_shape, dtype, own, slot):
    return lax.dynamic_update_slice(lax.empty(land_shape, dtype), own[None], (slot,) + (0,) * own.ndim)


def kernel(x, mix_norm_g, w_in, mu_shift, decay_w0, decay_w2, iclr_a0, iclr_a2, gate_g2, k_k, k_a, r_k, ln_x_w, ln_x_b, attn_out_g, w_out, ffn_norm_g, w_gate, w_up, w_down, final_norm_g, loss_target, m_mix_norm_g, m_w_in, m_mu_shift, m_decay_w0, m_decay_w2, m_iclr_a0, m_iclr_a2, m_gate_g2, m_k_k, m_k_a, m_r_k, m_ln_x_w, m_ln_x_b, m_attn_out_g, m_w_out, m_ffn_norm_g, m_w_gate, m_w_up, m_w_down, m_final_norm_g, v_mix_norm_g, v_w_in, v_mu_shift, v_decay_w0, v_decay_w2, v_iclr_a0, v_iclr_a2, v_gate_g2, v_k_k, v_k_a, v_r_k, v_ln_x_w, v_ln_x_b, v_attn_out_g, v_w_out, v_ffn_norm_g, v_w_gate, v_w_up, v_w_down, v_final_norm_g):
    names = ("mix_norm_g", "w_in", "mu_shift", "decay_w0", "decay_w2", "iclr_a0", "iclr_a2", "gate_g2", "k_k", "k_a",
             "r_k", "ln_x_w", "ln_x_b", "attn_out_g", "w_out", "ffn_norm_g", "w_gate", "w_up", "w_down", "final_norm_g")
    w = dict(zip(names, (mix_norm_g, w_in, mu_shift, decay_w0, decay_w2, iclr_a0, iclr_a2, gate_g2, k_k, k_a, r_k,
                         ln_x_w, ln_x_b, attn_out_g, w_out, ffn_norm_g, w_gate, w_up, w_down, final_norm_g)))
    m = dict(zip(names, (m_mix_norm_g, m_w_in, m_mu_shift, m_decay_w0, m_decay_w2, m_iclr_a0, m_iclr_a2, m_gate_g2,
                         m_k_k, m_k_a, m_r_k, m_ln_x_w, m_ln_x_b, m_attn_out_g, m_w_out, m_ffn_norm_g, m_w_gate,
                         m_w_up, m_w_down, m_final_norm_g)))
    v = dict(zip(names, (v_mix_norm_g, v_w_in, v_mu_shift, v_decay_w0, v_decay_w2, v_iclr_a0, v_iclr_a2, v_gate_g2,
                         v_k_k, v_k_a, v_r_k, v_ln_x_w, v_ln_x_b, v_attn_out_g, v_w_out, v_ffn_norm_g, v_w_gate,
                         v_w_up, v_w_down, v_final_norm_g)))
    first = ("w_in",) + LORAS
    rest = ("w_out", "w_gate", "w_up", "w_down")
    xi, yi, ci = lax.axis_index("x"), lax.axis_index("y"), lax.axis_index("c")
    my_chip, my_dev = 2 * xi + yi, 4 * xi + 2 * yi + ci
    gather, scatter = ("gather",) * 4, ("scatter",) * 4

    wb = {n: w[n][0].astype(BF16) for n in MATS}
    lands = [_with_own((N_CHIP,) + wb[n].shape, BF16, wb[n], my_chip) for n in rest]
    ssem, rsem, srcs_thru, lands_thru, tok = _swap_start([wb[n] for n in rest], lands, gather, "gather_rest_start")
    mine = [wb["w_in"]] + [w[n][0] for n in LORAS]
    got = _swap_blocking(mine, [_with_own((N_CHIP,) + a.shape, a.dtype, a, my_chip) for a in mine], gather, "gather_first")
    win, w2, a2, g2m = (_full(n, z) for n, z in zip(first, _swap_gathered(got, "gather_first_halves")))

    vecs = {n: w[n].reshape(1, sz) for n, sz in VECS}
    vecs["mix_norm_g"] = vecs["mix_norm_g"] + tok[0, 0]

    def get_rest(after):
        halves = _swap_wait(ssem, rsem, srcs_thru, lands_thru, after, gather, "gather_rest_wait")
        return [_full(n, z) for n, z in zip(rest, _swap_gathered(halves, "gather_rest_halves"))]

    flight = []

    def my_half(g):
        h = g.shape[1] // 2
        return lax.dynamic_slice(g, (my_chip, ci * h, 0), (1, h, g.shape[2]))[0]

    def send_rest(gw):
        gs = [_by_chip(n, gw[n]).astype(BF16) for n in rest]
        into = [_with_own((N_DEV,) + my_half(g).shape, BF16, my_half(g), my_dev) for g in gs]
        flight.extend(_swap_start(gs, into, scatter, "exchange_rest_start"))
        return flight[4]

    loss8, dx, gw, gv = _local_step(x[0], loss_target[0], win, vecs, w2, a2, g2m, get_rest, send_rest)

    small = jnp.concatenate([gv[n] for n, _ in VECS] + [loss8], axis=1)
    gs = [_by_chip(n, gw[n]).astype(BF16) for n in first]
    into = [_with_own((N_DEV,) + my_half(g).shape, BF16, my_half(g), my_dev) for g in gs]
    into.append(_with_own((N_DEV,) + small.shape, F32, small, my_dev))
    got = _swap_blocking(gs + [small], into, scatter + ("all",), "exchange_first")
    rbuf = dict(zip(first, got[:4]))
    rbuf.update(zip(rest, _swap_wait(flight[0], flight[1], flight[2], flight[3], got[0], scatter, "exchange_rest_wait")))

    big = MATS + LORAS
    core = jnp.reshape(ci, (1,)).astype(jnp.int32)
    sums = [_reduce8(rbuf[n], core, _SUM_TILE[n], "reduce_" + n) for n in big]
    gsum = dict(zip(big, _join_halves(sums, "join_halves")))
    res = {n: _adamw_call(gsum[n], w[n], m[n], v[n], _ADAM_TILE[n], "adamw_" + n) for n in big}
    cat = lambda d: jnp.concatenate([d[n].reshape(1, sz) for n, sz in VECS], axis=1)
    small_res = _reduce_adamw_small(got[4], cat(w), cat(m), cat(v))

    outs = []
    for k in range(4):
        piece = {n: r[k] for n, r in res.items()}
        c0 = 0
        for n, sz in VECS:
            piece[n] = small_res[k][0, c0:c0 + sz].reshape(w[n].shape)
            c0 += sz
        outs.extend(piece[n] for n in names)
    return (small_res[4][0, 0], dx[None], *outs)
```

```python
import jax
import jax.numpy as jnp
from jax import lax
from jax.experimental import pallas as pl
from jax.experimental.pallas import tpu as pltpu

F32 = jnp.float32
BF16 = jnp.bfloat16
HI = lax.Precision.HIGHEST

D_MODEL = 1024
HEAD_DIM = 64
RW = 512
N_PAIR = RW // 128
SHIFT_COLS = 1792
IN_COLS = 3328
D_FF = 2816
NORM_EPS = 1e-6
GN_EPS = 64e-5
CHUNK = 64
SUB = 16
WKV_PASSES = 1
ATTN_PASSES = 1
ATTN_BLOCK = 128
DILATIONS = (1, 4, 16)
NEG = -1e30
ADAM_LR, ADAM_B1, ADAM_B2, ADAM_EPS, ADAM_WD, ADAM_STEP = 0.001, 0.9, 0.999, 1e-08, 0.01, 10
VMEM_LIMIT = 56 * 1024 * 1024
MESH = pl.DeviceIdType.MESH


def _params(sem=None, **kw):
    return pltpu.CompilerParams(dimension_semantics=sem, vmem_limit_bytes=VMEM_LIMIT, **kw)


def _dot(a, b, prec=None):
    return lax.dot_general(a, b, (((1,), (0,)), ((), ())), preferred_element_type=F32, precision=prec)


def _dot_nt(a, b, prec=None):
    return lax.dot_general(a, b, (((1,), (1,)), ((), ())), preferred_element_type=F32, precision=prec)


def _dot_tn(a, b, prec=None):
    return lax.dot_general(a, b, (((0,), (0,)), ((), ())), preferred_element_type=F32, precision=prec)


_FORMS = {"nn": ((1,), (0,)), "nt": ((1,), (1,)), "tn": ((0,), (0,))}


def _dg(a, b, form):
    if a.ndim == 3 or b.ndim == 3:
        nb = a.shape[0] if a.ndim == 3 else b.shape[0]
        return jnp.stack([_dg(a[i] if a.ndim == 3 else a, b[i] if b.ndim == 3 else b, form) for i in range(nb)], axis=0)
    return lax.dot_general(a, b, (_FORMS[form], ((), ())), preferred_element_type=F32)


def _split2(x):
    hi = x.astype(BF16)
    return hi, (x - hi.astype(F32)).astype(BF16)


def _split3(x):
    hi = x.astype(BF16)
    rest = x - hi.astype(F32)
    mid = rest.astype(BF16)
    return hi, mid, (rest - mid.astype(F32)).astype(BF16)


def _mm_raw(a, b, form, mode):
    if mode == 1:
        return _dg(a.astype(BF16), b.astype(BF16), form)
    if mode == 3:
        ah, al = _split2(a)
        bh, bl = _split2(b)
        return _dg(ah, bh, form) + (_dg(ah, bl, form) + _dg(al, bh, form))
    if mode == "L3":
        ab = a.astype(BF16)
        b1, b2, b3 = _split3(b)
        return _dg(ab, b1, form) + (_dg(ab, b2, form) + _dg(ab, b3, form))
    assert mode == "R3", mode
    bb = b.astype(BF16)
    a1, a2, a3 = _split3(a)
    return _dg(a1, bb, form) + (_dg(a2, bb, form) + _dg(a3, bb, form))


def _mm(a, b, form, mode):
    @jax.custom_vjp
    def f(a, b):
        return _mm_raw(a, b, form, mode)

    def fwd(a, b):
        return _mm_raw(a, b, form, mode), (a, b)

    def bwd(res, ct):
        a, b = res
        la = {1: 1, 3: 3, "L3": None, "R3": "R3"}[mode]
        lb = {1: 1, 3: 3, "L3": "L3", "R3": None}[mode]
        if form == "nn":
            da = None if la is None else _mm_raw(ct, b, "nt", la)
            db = None if lb is None else _mm_raw(a, ct, "tn", lb)
        elif form == "nt":
            da = None if la is None else _mm_raw(ct, b, "nn", la)
            db = None if lb is None else _mm_raw(ct, a, "tn", "R3" if lb == "L3" else lb)
        else:
            da = None if la is None else _mm_raw(b, ct, "nt", "L3" if la == "R3" else la)
            db = None if lb is None else _mm_raw(a, ct, "nn", lb)
        return (jnp.zeros_like(a) if da is None else da, jnp.zeros_like(b) if db is None else db)

    f.defvjp(fwd, bwd)
    return f(a, b)


def _seg_ones(n):
    r = lax.broadcasted_iota(jnp.int32, (n, n), 0) // HEAD_DIM
    c = lax.broadcasted_iota(jnp.int32, (n, n), 1) // HEAD_DIM
    return (r == c).astype(F32)


def _segsum(x, seg):
    return _mm(x, seg, "nn", "R3")


def _rms_fwd(x, g):
    rstd = lax.rsqrt(jnp.mean(x * x, axis=-1, keepdims=True) + NORM_EPS)
    return x * rstd * g


def _rms_bwd(dy, x, g):
    rstd = lax.rsqrt(jnp.mean(x * x, axis=-1, keepdims=True) + NORM_EPS)
    xn = x * rstd
    dxn = dy * g
    dx = rstd * (dxn - xn * jnp.mean(dxn * xn, axis=-1, keepdims=True))
    return dx, dy * xn


def _sigmoid(x):
    return 1.0 / (1.0 + jnp.exp(-x))


def _softplus(x):
    return jnp.maximum(x, 0.0) + jnp.log(1.0 + jnp.exp(-jnp.abs(x)))


def _acc(ref, val, first):
    @pl.when(first)
    def _():
        ref[...] = val

    @pl.when(jnp.logical_not(first))
    def _():
        ref[...] += val


def _colsum8(v):
    rows, n = v.shape
    return jnp.sum(v.reshape(rows // 8, 8, n), axis=0)


def _prep_fn(p, pprev, mu, w0, w2p, a0, a2p, g2, k_k, k_a):
    seg = _seg_ones(RW)
    ps = p + (pprev - p) * mu
    r = ps[:, 0:RW]
    k = ps[:, RW:2 * RW]
    v = ps[:, 2 * RW:3 * RW]
    xwa = ps[:, 3 * RW:3 * RW + 128]
    xg = ps[:, 3 * RW + 128:3 * RW + 256]
    wraw = -_softplus(-(w0 + _mm(jnp.tanh(xwa), w2p, "nn", 3))) - 0.5
    lw = -jnp.exp(wraw)
    a = _sigmoid(a0 + _mm(xwa, a2p, "nn", 3))
    g = _mm(_sigmoid(xg), g2, "nn", 3)
    kk = k * k_k
    kk = kk / jnp.maximum(jnp.sqrt(_segsum(kk * kk, seg)), 1e-12)
    k2 = k * (1.0 + (a - 1.0) * k_a)
    return r, lw, k2, v, kk, a, g


def _solve_unit_lower(lmat, rhs):
    c = lmat.shape[-1]
    row = lax.broadcasted_iota(jnp.int32, (c, c), 0)
    col = lax.broadcasted_iota(jnp.int32, (c, c), 1)
    eye = (row == col).astype(F32)
    ld = jnp.where(row // SUB == col // SUB, lmat, 0.0)
    lo = lmat - ld
    x = eye + ld
    m = ld
    mm = lambda p, q: _mm(p, q, "nn", WKV_PASSES)
    for _ in range(3):
        m = mm(m, m)
        x = x + mm(x, m)
    g = mm(x, lo)
    g2 = mm(g, g)
    w = mm(x, rhs)
    w = w + mm(g2, w)
    return w + mm(g, w)


def _wkv_chunk_fn(s0, r, lw, k, v, kk, a):
    c = r.shape[-2]
    n = 2 * c
    row = lax.broadcasted_iota(jnp.int32, (n, n), 0)
    col = lax.broadcasted_iota(jnp.int32, (n, n), 1)
    same = (row // c) == (col // c)
    incl = jnp.logical_and(row >= col, same)
    strict = jnp.logical_and(row > col, same)
    sel = (lax.broadcasted_iota(jnp.int32, (n, 128), 0) // c) == (lax.broadcasted_iota(jnp.int32, (n, 128), 1) // HEAD_DIM)
    two = lambda z: jnp.concatenate([z, z], axis=-2)
    lw2 = two(lw)
    mm = lambda p_, q_, form: _mm(p_, q_, form, WKV_PASSES)
    cl = _mm(incl.astype(F32), lw2, "nn", "L3")
    p = jnp.exp(cl)
    pinv = jnp.exp(-cl)
    pprev = jnp.exp(cl - lw2)
    kk2 = two(kk)
    at = jnp.where(sel, -kk2 * pprev, 0.0)
    bt = jnp.where(sel, kk2 * two(a) * pinv, 0.0)
    kt = jnp.where(sel, two(k) * pinv, 0.0)
    rt = jnp.where(sel, two(r) * p, 0.0)
    vt = jnp.where(sel, two(v), 0.0)
    ab = jnp.where(strict, mm(at, bt, "nt"), 0.0)
    ak = jnp.where(strict, mm(at, kt, "nt"), 0.0)
    rb = jnp.where(incl, mm(rt, bt, "nt"), 0.0)
    rk = jnp.where(incl, mm(rt, kt, "nt"), 0.0)
    u = _solve_unit_lower(ab, mm(at, s0, "nt") + mm(ak, vt, "nn"))
    y2 = mm(rt, s0, "nt") + mm(rb, u, "nn") + mm(rk, vt, "nn")
    plast = jnp.exp(jnp.sum(lw, axis=-2, keepdims=True))
    s1 = (s0 + mm(u, bt, "tn") + mm(vt, kt, "tn")) * plast
    r2 = lax.broadcasted_iota(jnp.int32, (128, 128), 0) // HEAD_DIM
    c2 = lax.broadcasted_iota(jnp.int32, (128, 128), 1) // HEAD_DIM
    return y2[..., :c, :] + y2[..., c:, :], jnp.where(r2 == c2, s1, 0.0)


def _post_fn(y, r, k2, v, g, lnw, lnb, rk):
    seg = _seg_ones(RW)
    mean = _segsum(y, seg) * (1.0 / HEAD_DIM)
    yc = y - mean
    var = _segsum(yc * yc, seg) * (1.0 / HEAD_DIM)
    yn = yc * lax.rsqrt(var + GN_EPS)
    out = yn * lnw + lnb + _segsum(r * k2 * rk, seg) * v
    return out * g


def _attn_block_fn(q, kc, vc, kp=None, vp=None):
    n = ATTN_BLOCK
    qi = lax.broadcasted_iota(jnp.int32, (n, n), 0)
    kj = lax.broadcasted_iota(jnp.int32, (n, n), 1)
    lane = lax.broadcasted_iota(jnp.int32, (1, 128), 1)
    scale = HEAD_DIM ** -0.5
    os_, ls_ = [], []
    for h in range(2):
        mh = (lane // HEAD_DIM) == h
        qh = jnp.where(mh, q, 0.0)
        sc = jnp.where(kj <= qi, _mm(qh, kc, "nt", ATTN_PASSES) * scale, NEG)
        m = jnp.max(sc, axis=-1, keepdims=True)
        if kp is not None:
            sp = jnp.where(kj >= qi, _mm(qh, kp, "nt", ATTN_PASSES) * scale, NEG)
            m = jnp.maximum(m, jnp.max(sp, axis=-1, keepdims=True))
        pc = jnp.exp(sc - m)
        den = jnp.sum(pc, axis=-1, keepdims=True)
        num = _mm(pc, vc, "nn", ATTN_PASSES)
        if kp is not None:
            pp = jnp.exp(sp - m)
            den = den + jnp.sum(pp, axis=-1, keepdims=True)
            num = num + _mm(pp, vp, "nn", ATTN_PASSES)
        os_.append(num / den)
        ls_.append(m + jnp.log(den))
    m0 = (lane // HEAD_DIM) == 0
    return jnp.where(m0, os_[0], os_[1]), jnp.where(m0, ls_[0], ls_[1])


def _combine_fn(o1, o2, o3, l1, l2, l3, og):
    seg = _seg_ones(o1.shape[-1])
    m = jnp.maximum(jnp.maximum(l1, l2), l3)
    e1, e2, e3 = jnp.exp(l1 - m), jnp.exp(l2 - m), jnp.exp(l3 - m)
    o = (e1 * o1 + e2 * o2 + e3 * o3) / (e1 + e2 + e3)
    o = o * lax.rsqrt(_segsum(o * o, seg) * (1.0 / HEAD_DIM) + NORM_EPS)
    return o * og


def _in_proj(x, g1, win):
    t = x.shape[0]
    tm = 256

    def body(x_ref, g_ref, w_ref, h_ref, pa_ref, qkv_ref):
        h = _rms_fwd(x_ref[...], g_ref[...]).astype(BF16)
        h_ref[...] = h
        proj = _dot(h, w_ref[...])
        pa_ref[...] = proj[:, :SHIFT_COLS]
        for j in range(3):
            for p in range(N_PAIR):
                c0 = SHIFT_COLS + j * RW + p * 128
                qkv_ref[j, p] = proj[:, c0:c0 + 128]

    return pl.pallas_call(
        body, name="in_proj", grid=(t // tm,),
        in_specs=[pl.BlockSpec((tm, D_MODEL), lambda i: (i, 0)), pl.BlockSpec((1, D_MODEL), lambda i: (0, 0)),
                  pl.BlockSpec((D_MODEL, IN_COLS), lambda i: (0, 0))],
        out_specs=[pl.BlockSpec((tm, D_MODEL), lambda i: (i, 0)), pl.BlockSpec((tm, SHIFT_COLS), lambda i: (i, 0)),
                   pl.BlockSpec((3, N_PAIR, tm, 128), lambda i: (0, 0, i, 0))],
        out_shape=[jax.ShapeDtypeStruct((t, D_MODEL), BF16), jax.ShapeDtypeStruct((t, SHIFT_COLS), F32),
                   jax.ShapeDtypeStruct((3, N_PAIR, t, 128), F32)],
        compiler_params=_params(("parallel",)),
    )(x, g1, win)


def _shifted(p, last8, first):
    prow = jnp.where(first, 0.0, last8[7:8, :])
    rolled = pltpu.roll(p, 1, axis=0)
    rid = lax.broadcasted_iota(jnp.int32, p.shape, 0)
    return jnp.where(rid == 0, prow, rolled)


_PREP_TM = 256


def _prep_specs(tm):
    vec = lambda n: pl.BlockSpec((1, n), lambda i: (0, 0))
    mat = lambda r, n: pl.BlockSpec((r, n), lambda i: (0, 0))
    return [vec(SHIFT_COLS), vec(RW), mat(128, RW), vec(RW), mat(128, RW), mat(128, RW), vec(RW), vec(RW)]


def _prep_fwd(proj, pw):
    t = proj.shape[0]
    tm = _PREP_TM

    def body(p_ref, l8_ref, mu, w0, w2p, a0, a2p, g2, k_k, k_a, *outs):
        p = p_ref[...]
        pprev = _shifted(p, l8_ref[...], pl.program_id(0) == 0)
        res = _prep_fn(p, pprev, mu[...], w0[...], w2p[...], a0[...], a2p[...], g2[...], k_k[...], k_a[...])
        for o_ref, val in zip(outs, res):
            o_ref[...] = val

    row = pl.BlockSpec((tm, RW), lambda i: (i, 0))
    return pl.pallas_call(
        body, name="rwkv_prep", grid=(t // tm,),
        in_specs=[pl.BlockSpec((tm, SHIFT_COLS), lambda i: (i, 0)),
                  pl.BlockSpec((8, SHIFT_COLS), lambda i: (jnp.maximum(i * (tm // 8) - 1, 0), 0))] + _prep_specs(tm),
        out_specs=[row] * 7,
        out_shape=[jax.ShapeDtypeStruct((t, RW), F32)] * 7,
        compiler_params=_params(("parallel",)),
    )(proj, proj, *pw)


def _pairs(ref):
    return jnp.stack([ref[:, 128 * p:128 * (p + 1)] for p in range(N_PAIR)], axis=0)


def _wkv_fwd(r, lw, k2, v, kk, a):
    t = r.shape[0]
    nc = t // CHUNK

    def body(r_ref, lw_ref, k_ref, v_ref, kk_ref, a_ref, y_ref, s_ref, st):
        @pl.when(pl.program_id(0) == 0)
        def _():
            st[...] = jnp.zeros_like(st)

        s0 = st[...]
        s_ref[0] = s0
        y, s1 = _wkv_chunk_fn(s0, *[_pairs(ref) for ref in (r_ref, lw_ref, k_ref, v_ref, kk_ref, a_ref)])
        for p in range(N_PAIR):
            y_ref[:, 128 * p:128 * (p + 1)] = y[p]
        st[...] = s1

    blk = pl.BlockSpec((CHUNK, RW), lambda c: (c, 0))
    return pl.pallas_call(
        body, name="wkv_fwd", grid=(nc,),
        in_specs=[blk] * 6,
        out_specs=[blk, pl.BlockSpec((1, N_PAIR, 128, 128), lambda c: (c, 0, 0, 0))],
        out_shape=[jax.ShapeDtypeStruct((t, RW), F32), jax.ShapeDtypeStruct((nc, N_PAIR, 128, 128), F32)],
        scratch_shapes=[pltpu.VMEM((N_PAIR, 128, 128), F32)],
        compiler_params=_params(("arbitrary",)),
    )(r, lw, k2, v, kk, a)


_POST_TM = 256


def _post_fwd(y, r, k2, v, g, lnw, lnb, rk):
    t = y.shape[0]
    tm = _POST_TM

    def body(y_ref, r_ref, k_ref, v_ref, g_ref, lnw_ref, lnb_ref, rk_ref, o_ref):
        o_ref[...] = _post_fn(y_ref[...], r_ref[...], k_ref[...], v_ref[...], g_ref[...],
                              lnw_ref[...], lnb_ref[...], rk_ref[...]).astype(BF16)

    row = pl.BlockSpec((tm, RW), lambda i: (i, 0))
    vec = pl.BlockSpec((1, RW), lambda i: (0, 0))
    return pl.pallas_call(
        body, name="rwkv_post", grid=(t // tm,),
        in_specs=[row] * 5 + [vec] * 3, out_specs=row,
        out_shape=jax.ShapeDtypeStruct((t, RW), BF16),
        compiler_params=_params(("parallel",)),
    )(y, r, k2, v, g, lnw, lnb, rk)


ATTN_GROUP = 2


def _dilated_rows(d, r, n):
    if d == 1:
        return pl.ds(pl.multiple_of(n * ATTN_BLOCK, ATTN_BLOCK), ATTN_BLOCK)
    return pl.ds(r + n * (ATTN_BLOCK * d), ATTN_BLOCK, stride=d)


def _for_each_sequence(t, unit):
    for di, d in enumerate(DILATIONS):

        @pl.when(pl.program_id(1) == di)
        def _(di=di, d=d):
            nb = t // (ATTN_BLOCK * d)
            if d == 1:
                unit(di, [(d, 0, 0)], False)
                unit(di, [(d, 0, 1)], True)
                lax.fori_loop(1, nb // 2, lambda k, c: (unit(di, [(d, 0, 2 * k), (d, 0, 2 * k + 1)], True), c)[1], 0)
            else:

                def residues(r, carry):
                    unit(di, [(d, r, 0), (d, r + d // 2, 0)], False)
                    if nb > 1:
                        lax.fori_loop(1, nb, lambda n, c: (unit(di, [(d, r, n), (d, r + d // 2, n)], True), c)[1], 0)
                    return carry

                lax.fori_loop(0, d // 2, residues, 0)


def _take(ref, lead, rows_list):
    return jnp.stack([ref.at[(*lead, g)][rows, :] for rows in rows_list for g in range(ATTN_GROUP)], axis=0)


def _put(ref, lead, rows_list, val, add=False):
    k = 0
    for rows in rows_list:
        for g in range(ATTN_GROUP):
            if add:
                ref.at[(*lead, g)][rows, :] += val[k]
            else:
                ref.at[(*lead, g)][rows, :] = val[k]
            k += 1


def _attn_fwd(qkv):
    t = qkv.shape[2]

    def body(q_ref, k_ref, v_ref, o_ref, l_ref):
        def unit(di, places, has_prev):
            cur = [_dilated_rows(d, r, n) for d, r, n in places]
            args = [_take(ref, (0,), cur) for ref in (q_ref, k_ref, v_ref)]
            if has_prev:
                prv = [_dilated_rows(d, r, n - 1) for d, r, n in places]
                args += [_take(ref, (0,), prv) for ref in (k_ref, v_ref)]
            o, lse = _attn_block_fn(*args)
            _put(o_ref, (0,), cur, o)
            _put(l_ref, (0,), cur, lse)

        _for_each_sequence(t, unit)

    spec = lambda j: pl.BlockSpec((1, ATTN_GROUP, t, 128), lambda i, b: (j, i, 0, 0))
    out = pl.BlockSpec((1, ATTN_GROUP, t, 128), lambda i, b: (b, i, 0, 0))
    return pl.pallas_call(
        body, name="attn_fwd", grid=(N_PAIR // ATTN_GROUP, len(DILATIONS)),
        in_specs=[spec(0), spec(1), spec(2)], out_specs=[out, out],
        out_shape=[jax.ShapeDtypeStruct((3, N_PAIR, t, 128), F32)] * 2,
        compiler_params=_params(("parallel", "arbitrary")),
    )(qkv, qkv, qkv)


_COMB_TM = 256


def _combine_fwd(o, l, og):
    t = o.shape[2]
    tm = _COMB_TM

    def body(o_ref, l_ref, og_ref, y_ref):
        for p in range(N_PAIR):
            cols = slice(128 * p, 128 * (p + 1))
            y_ref[:, cols] = _combine_fn(o_ref[0, p], o_ref[1, p], o_ref[2, p], l_ref[0, p], l_ref[1, p], l_ref[2, p],
                                         og_ref[:, cols]).astype(BF16)

    blk = pl.BlockSpec((3, N_PAIR, tm, 128), lambda i: (0, 0, i, 0))
    return pl.pallas_call(
        body, name="attn_combine", grid=(t // tm,),
        in_specs=[blk, blk, pl.BlockSpec((1, RW), lambda i: (0, 0))], out_specs=pl.BlockSpec((tm, RW), lambda i: (i, 0)),
        out_shape=jax.ShapeDtypeStruct((t, RW), BF16),
        compiler_params=_params(("parallel",)),
    )(o, l, og)


def _out_proj(x, ycat, wout, g2):
    t = x.shape[0]
    tm = 256

    def body(x_ref, y_ref, w_ref, g_ref, x1_ref, h_ref):
        x1 = x_ref[...] + _dot(y_ref[...], w_ref[...])
        x1_ref[...] = x1
        h_ref[...] = _rms_fwd(x1, g_ref[...]).astype(BF16)

    row = pl.BlockSpec((tm, D_MODEL), lambda i: (i, 0))
    return pl.pallas_call(
        body, name="out_proj", grid=(t // tm,),
        in_specs=[row, row, pl.BlockSpec((D_MODEL, D_MODEL), lambda i: (0, 0)), pl.BlockSpec((1, D_MODEL), lambda i: (0, 0))],
        out_specs=[row, row],
        out_shape=[jax.ShapeDtypeStruct((t, D_MODEL), F32), jax.ShapeDtypeStruct((t, D_MODEL), BF16)],
        compiler_params=_params(("parallel",)),
    )(x, ycat, wout, g2)


def _ffn_up(h2, wg, wu):
    t = h2.shape[0]
    tm = 256

    def body(h_ref, wg_ref, wu_ref, gt_ref, up_ref, act_ref):
        h = h_ref[...]
        gt = _dot(h, wg_ref[...])
        up = _dot(h, wu_ref[...])
        gt_ref[...] = gt
        up_ref[...] = up
        act_ref[...] = (gt * _sigmoid(gt) * up).astype(BF16)

    wide = pl.BlockSpec((tm, D_FF), lambda i: (i, 0))
    wsp = pl.BlockSpec((D_MODEL, D_FF), lambda i: (0, 0))
    return pl.pallas_call(
        body, name="ffn_up", grid=(t // tm,),
        in_specs=[pl.BlockSpec((tm, D_MODEL), lambda i: (i, 0)), wsp, wsp],
        out_specs=[wide, wide, wide],
        out_shape=[jax.ShapeDtypeStruct((t, D_FF), F32)] * 2 + [jax.ShapeDtypeStruct((t, D_FF), BF16)],
        compiler_params=_params(("parallel",)),
    )(h2, wg, wu)


def _ffn_down_loss(x1, act, wd, gf, tgt):
    t = x1.shape[0]
    tm = 256

    def body(x1_ref, a_ref, w_ref, g_ref, t_ref, dx_ref, dxb_ref, loss_ref, dg_ref):
        first = pl.program_id(0) == 0
        x2 = x1_ref[...] + _dot(a_ref[...], w_ref[...])
        g = g_ref[...]
        diff = _rms_fwd(x2, g) - t_ref[...]
        lrow = 0.5 * jnp.sum(_colsum8(diff * diff), axis=1, keepdims=True) * (1.0 / D_MODEL)
        _acc(loss_ref, jnp.broadcast_to(lrow, (8, 128)), first)
        dx2, dgr = _rms_bwd(diff * (1.0 / D_MODEL), x2, g)
        dx_ref[...] = dx2
        dxb_ref[...] = dx2.astype(BF16)
        _acc(dg_ref, _colsum8(dgr), first)

    row = pl.BlockSpec((tm, D_MODEL), lambda i: (i, 0))
    return pl.pallas_call(
        body, name="ffn_down_loss", grid=(t // tm,),
        in_specs=[row, pl.BlockSpec((tm, D_FF), lambda i: (i, 0)), pl.BlockSpec((D_FF, D_MODEL), lambda i: (0, 0)),
                  pl.BlockSpec((1, D_MODEL), lambda i: (0, 0)), row],
        out_specs=[row, row, pl.BlockSpec((8, 128), lambda i: (0, 0)), pl.BlockSpec((8, D_MODEL), lambda i: (0, 0))],
        out_shape=[jax.ShapeDtypeStruct((t, D_MODEL), F32), jax.ShapeDtypeStruct((t, D_MODEL), BF16),
                   jax.ShapeDtypeStruct((8, 128), F32), jax.ShapeDtypeStruct((8, D_MODEL), F32)],
        compiler_params=_params(("arbitrary",)),
    )(x1, act, wd, gf, tgt)


def _ffn_bwd_act(dx2b, wd, gt, up):
    t = dx2b.shape[0]
    tm = 256

    def body(dx_ref, w_ref, gt_ref, up_ref, dgt_ref, dup_ref):
        dact = _dot_nt(dx_ref[...], w_ref[...])
        gt = gt_ref[...]
        sg = _sigmoid(gt)
        dgt_ref[...] = (dact * up_ref[...] * sg * (1.0 + gt * (1.0 - sg))).astype(BF16)
        dup_ref[...] = (dact * gt * sg).astype(BF16)

    wide = pl.BlockSpec((tm, D_FF), lambda i: (i, 0))
    return pl.pallas_call(
        body, name="ffn_bwd_act", grid=(t // tm,),
        in_specs=[pl.BlockSpec((tm, D_MODEL), lambda i: (i, 0)), pl.BlockSpec((D_FF, D_MODEL), lambda i: (0, 0)), wide, wide],
        out_specs=[wide, wide],
        out_shape=[jax.ShapeDtypeStruct((t, D_FF), BF16)] * 2,
        compiler_params=_params(("parallel",)),
    )(dx2b, wd, gt, up)


def _ffn_bwd_h(dgt, dup, wg, wu, dx2, x1, g2, wout):
    t = dgt.shape[0]
    tm = 256

    def body(dgt_ref, dup_ref, wg_ref, wu_ref, dx2_ref, x1_ref, g_ref, wo_ref, dx1_ref, dx1b_ref, dya_ref, dyb_ref, dg_ref):
        dh = _dot_nt(dgt_ref[...], wg_ref[...]) + _dot_nt(dup_ref[...], wu_ref[...])
        dxn, dgr = _rms_bwd(dh, x1_ref[...], g_ref[...])
        dx1 = dx2_ref[...] + dxn
        dx1_ref[...] = dx1
        dx1b = dx1.astype(BF16)
        dx1b_ref[...] = dx1b
        dy = _dot_nt(dx1b, wo_ref[...])
        dya_ref[...] = dy[:, :RW]
        dyb_ref[...] = dy[:, RW:]
        _acc(dg_ref, _colsum8(dgr), pl.program_id(0) == 0)

    wide = pl.BlockSpec((tm, D_FF), lambda i: (i, 0))
    row = pl.BlockSpec((tm, D_MODEL), lambda i: (i, 0))
    half = pl.BlockSpec((tm, RW), lambda i: (i, 0))
    wsp = pl.BlockSpec((D_MODEL, D_FF), lambda i: (0, 0))
    return pl.pallas_call(
        body, name="ffn_bwd_h", grid=(t // tm,),
        in_specs=[wide, wide, wsp, wsp, row, row, pl.BlockSpec((1, D_MODEL), lambda i: (0, 0)),
                  pl.BlockSpec((D_MODEL, D_MODEL), lambda i: (0, 0))],
        out_specs=[row, row, half, half, pl.BlockSpec((8, D_MODEL), lambda i: (0, 0))],
        out_shape=[jax.ShapeDtypeStruct((t, D_MODEL), F32), jax.ShapeDtypeStruct((t, D_MODEL), BF16),
                   jax.ShapeDtypeStruct((t, RW), F32), jax.ShapeDtypeStruct((t, RW), F32),
                   jax.ShapeDtypeStruct((8, D_MODEL), F32)],
        compiler_params=_params(("arbitrary",)),
    )(dgt, dup, wg, wu, dx2, x1, g2, wout)


def _wgrad(a, b, tk, tn, name):
    t, kdim = a.shape
    ndim = b.shape[1]

    def body(a_ref, b_ref, o_ref):
        o_ref[...] = _dot_tn(a_ref[...], b_ref[...])

    return pl.pallas_call(
        body, name=name, grid=(kdim // tk, ndim // tn),
        in_specs=[pl.BlockSpec((t, tk), lambda i, j: (0, i)), pl.BlockSpec((t, tn), lambda i, j: (0, j))],
        out_specs=pl.BlockSpec((tk, tn), lambda i, j: (i, j)),
        out_shape=jax.ShapeDtypeStruct((kdim, ndim), F32),
        compiler_params=_params(("parallel", "parallel")),
    )(a, b)


def _post_bwd(dya, y, r, k2, v, g, lnw, lnb, rk):
    t = y.shape[0]
    tm = _POST_TM

    def body(d_ref, y_ref, r_ref, k_ref, v_ref, g_ref, lnw_ref, lnb_ref, rk_ref,
             dy_ref, dr_ref, dk_ref, dv_ref, dg_ref, dlnw_ref, dlnb_ref, drk_ref):
        first = pl.program_id(0) == 0
        ones = jnp.ones((tm, 1), F32)
        prim = (y_ref[...], r_ref[...], k_ref[...], v_ref[...], g_ref[...],
                ones * lnw_ref[...], ones * lnb_ref[...], ones * rk_ref[...])
        _, vjp = jax.vjp(_post_fn, *prim)
        dy, dr, dk, dv, dg, dlnw, dlnb, drk = vjp(d_ref[...])
        dy_ref[...] = dy
        dr_ref[...] = dr
        dk_ref[...] = dk
        dv_ref[...] = dv
        dg_ref[...] = dg
        _acc(dlnw_ref, _colsum8(dlnw), first)
        _acc(dlnb_ref, _colsum8(dlnb), first)
        _acc(drk_ref, _colsum8(drk), first)

    row = pl.BlockSpec((tm, RW), lambda i: (i, 0))
    vec = pl.BlockSpec((1, RW), lambda i: (0, 0))
    part = pl.BlockSpec((8, RW), lambda i: (0, 0))
    return pl.pallas_call(
        body, name="rwkv_post_bwd", grid=(t // tm,),
        in_specs=[row] * 6 + [vec] * 3, out_specs=[row] * 5 + [part] * 3,
        out_shape=[jax.ShapeDtypeStruct((t, RW), F32)] * 5 + [jax.ShapeDtypeStruct((8, RW), F32)] * 3,
        compiler_params=_params(("arbitrary",)),
    )(dya, y, r, k2, v, g, lnw, lnb, rk)


def _wkv_bwd(dy, s0s, r, lw, k2, v, kk, a):
    t = r.shape[0]
    nc = t // CHUNK

    def body(dy_ref, s_ref, r_ref, lw_ref, k_ref, v_ref, kk_ref, a_ref,
             dr_ref, dlw_ref, dk_ref, dv_ref, dkk_ref, da_ref, ds):
        @pl.when(pl.program_id(0) == 0)
        def _():
            ds[...] = jnp.zeros_like(ds)

        _, vjp = jax.vjp(_wkv_chunk_fn, s_ref[0],
                         *[_pairs(ref) for ref in (r_ref, lw_ref, k_ref, v_ref, kk_ref, a_ref)])
        res = vjp((_pairs(dy_ref), ds[...]))
        ds[...] = res[0]
        for ref, val in zip((dr_ref, dlw_ref, dk_ref, dv_ref, dkk_ref, da_ref), res[1:]):
            for p in range(N_PAIR):
                ref[:, 128 * p:128 * (p + 1)] = val[p]

    blk = pl.BlockSpec((CHUNK, RW), lambda c: (nc - 1 - c, 0))
    return pl.pallas_call(
        body, name="wkv_bwd", grid=(nc,),
        in_specs=[blk, pl.BlockSpec((1, N_PAIR, 128, 128), lambda c: (nc - 1 - c, 0, 0, 0))] + [blk] * 6,
        out_specs=[blk] * 6,
        out_shape=[jax.ShapeDtypeStruct((t, RW), F32)] * 6,
        scratch_shapes=[pltpu.VMEM((N_PAIR, 128, 128), F32)],
        compiler_params=_params(("arbitrary",)),
    )(dy, s0s, r, lw, k2, v, kk, a)


def _prep_bwd(proj, pw, douts):
    t = proj.shape[0]
    tm = _PREP_TM
    nt = t // tm

    def body(p_ref, l8_ref, mu, w0, w2p, a0, a2p, g2, k_k, k_a, dr, dr2, dlw, dk2, dk22, dv, dv2, dkk, da, dg,
             dp_ref, dmu_ref, dw0_ref, dw2_ref, da0_ref, da2_ref, dg2_ref, dkk_ref, dka_ref, carry):
        i = pl.program_id(0)
        first = i == 0

        @pl.when(first)
        def _():
            carry[...] = jnp.zeros_like(carry)

        p = p_ref[...]
        pprev = _shifted(p, l8_ref[...], i == nt - 1)
        ones = jnp.ones((tm, 1), F32)
        prim = (p, pprev, ones * mu[...], ones * w0[...], w2p[...], ones * a0[...], a2p[...], g2[...],
                ones * k_k[...], ones * k_a[...])
        _, vjp = jax.vjp(_prep_fn, *prim)
        dp, dpp, dmu, dw0, dw2, da0, da2, dg2, dkk_, dka = vjp(
            (dr[...] + dr2[...], dlw[...], dk2[...] + dk22[...], dv[...] + dv2[...], dkk[...], da[...], dg[...]))
        up = pltpu.roll(dpp, tm - 1, axis=0)
        rid = lax.broadcasted_iota(jnp.int32, dpp.shape, 0)
        dp_ref[...] = dp + jnp.where(rid == tm - 1, carry[0:1, :], up)
        carry[...] = jnp.broadcast_to(dpp[0:1, :], carry.shape)
        _acc(dmu_ref, _colsum8(dmu), first)
        _acc(dw0_ref, _colsum8(dw0), first)
        _acc(dw2_ref, dw2, first)
        _acc(da0_ref, _colsum8(da0), first)
        _acc(da2_ref, da2, first)
        _acc(dg2_ref, dg2, first)
        _acc(dkk_ref, _colsum8(dkk_), first)
        _acc(dka_ref, _colsum8(dka), first)

    rev = lambda i: (nt - 1 - i, 0)
    row = pl.BlockSpec((tm, RW), rev)
    part = lambda n: pl.BlockSpec((8, n), lambda i: (0, 0))
    mat = pl.BlockSpec((128, RW), lambda i: (0, 0))
    return pl.pallas_call(
        body, name="rwkv_prep_bwd", grid=(nt,),
        in_specs=[pl.BlockSpec((tm, SHIFT_COLS), rev),
                  pl.BlockSpec((8, SHIFT_COLS), lambda i: (jnp.maximum((nt - 1 - i) * (tm // 8) - 1, 0), 0))]
                 + _prep_specs(tm) + [row] * 10,
        out_specs=[pl.BlockSpec((tm, SHIFT_COLS), rev), part(SHIFT_COLS), part(RW), mat, part(RW), mat, mat,
                   part(RW), part(RW)],
        out_shape=[jax.ShapeDtypeStruct((t, SHIFT_COLS), F32), jax.ShapeDtypeStruct((8, SHIFT_COLS), F32),
                   jax.ShapeDtypeStruct((8, RW), F32), jax.ShapeDtypeStruct((128, RW), F32),
                   jax.ShapeDtypeStruct((8, RW), F32), jax.ShapeDtypeStruct((128, RW), F32),
                   jax.ShapeDtypeStruct((128, RW), F32), jax.ShapeDtypeStruct((8, RW), F32),
                   jax.ShapeDtypeStruct((8, RW), F32)],
        scratch_shapes=[pltpu.VMEM((8, SHIFT_COLS), F32)],
        compiler_params=_params(("arbitrary",)),
    )(proj, proj, *pw, *douts)


def _combine_bwd(dyb, o, l, og):
    t = dyb.shape[0]
    tm = _COMB_TM

    def body(d_ref, o_ref, l_ref, og_ref, do_ref, dl_ref, dog_ref):
        ones = jnp.ones((tm, 1), F32)
        dog = []
        for p in range(N_PAIR):
            cols = slice(128 * p, 128 * (p + 1))
            _, vjp = jax.vjp(_combine_fn, o_ref[0, p], o_ref[1, p], o_ref[2, p], l_ref[0, p], l_ref[1, p], l_ref[2, p],
                             ones * og_ref[:, cols])
            res = vjp(d_ref[:, cols])
            for b in range(3):
                do_ref[b, p] = res[b]
                dl_ref[b, p] = res[3 + b]
            dog.append(_colsum8(res[6]))
        _acc(dog_ref, jnp.concatenate(dog, axis=1), pl.program_id(0) == 0)

    blk = pl.BlockSpec((3, N_PAIR, tm, 128), lambda i: (0, 0, i, 0))
    return pl.pallas_call(
        body, name="attn_combine_bwd", grid=(t // tm,),
        in_specs=[pl.BlockSpec((tm, RW), lambda i: (i, 0)), blk, blk, pl.BlockSpec((1, RW), lambda i: (0, 0))],
        out_specs=[blk, blk, pl.BlockSpec((8, RW), lambda i: (0, 0))],
        out_shape=[jax.ShapeDtypeStruct((3, N_PAIR, t, 128), F32)] * 2 + [jax.ShapeDtypeStruct((8, RW), F32)],
        compiler_params=_params(("arbitrary",)),
    )(dyb, o, l, og)


def _attn_bwd(do, dl, qkv):
    t = qkv.shape[2]

    def body(do_ref, dl_ref, q_ref, k_ref, v_ref, dq_ref, dk_ref, dv_ref):
        @pl.when(pl.program_id(1) == 0)
        def _():
            for ref in (dq_ref, dk_ref, dv_ref):
                ref[...] = jnp.zeros_like(ref)

        def unit(di, places, has_prev):
            cur = [_dilated_rows(d, r, n) for d, r, n in places]
            args = [_take(ref, (0,), cur) for ref in (q_ref, k_ref, v_ref)]
            if has_prev:
                prv = [_dilated_rows(d, r, n - 1) for d, r, n in places]
                args += [_take(ref, (0,), prv) for ref in (k_ref, v_ref)]
            _, vjp = jax.vjp(_attn_block_fn, *args)
            res = vjp((_take(do_ref, (0,), cur), _take(dl_ref, (0,), cur)))
            _put(dq_ref, (), cur, res[0], add=True)
            _put(dk_ref, (), cur, res[1], add=True)
            _put(dv_ref, (), cur, res[2], add=True)
            if has_prev:
                _put(dk_ref, (), prv, res[3], add=True)
                _put(dv_ref, (), prv, res[4], add=True)

        _for_each_sequence(t, unit)

    spec = lambda j: pl.BlockSpec((1, ATTN_GROUP, t, 128), lambda i, b: (j, i, 0, 0))
    branch = pl.BlockSpec((1, ATTN_GROUP, t, 128), lambda i, b: (b, i, 0, 0))
    out = pl.BlockSpec((ATTN_GROUP, t, 128), lambda i, b: (i, 0, 0))
    return pl.pallas_call(
        body, name="attn_bwd", grid=(N_PAIR // ATTN_GROUP, len(DILATIONS)),
        in_specs=[branch, branch, spec(0), spec(1), spec(2)], out_specs=[out] * 3,
        out_shape=[jax.ShapeDtypeStruct((N_PAIR, t, 128), F32)] * 3,
        compiler_params=_params(("parallel", "arbitrary")),
    )(do, dl, qkv, qkv, qkv)


def _in_proj_bwd(dpa, dq, dk, dv, win, x, g1, dx1):
    t = x.shape[0]
    tm = 256

    def body(dpa_ref, dq_ref, dk_ref, dv_ref, w_ref, x_ref, g_ref, dx1_ref, dproj_ref, dx_ref, dg_ref):
        parts = [dpa_ref[...]] + [ref[p] for ref in (dq_ref, dk_ref, dv_ref) for p in range(N_PAIR)]
        dproj = jnp.concatenate([z.astype(BF16) for z in parts], axis=1)
        dproj_ref[...] = dproj
        dh = _dot_nt(dproj, w_ref[...])
        dxn, dgr = _rms_bwd(dh, x_ref[...], g_ref[...])
        dx_ref[...] = dx1_ref[...] + dxn
        _acc(dg_ref, _colsum8(dgr), pl.program_id(0) == 0)

    row = pl.BlockSpec((tm, D_MODEL), lambda i: (i, 0))
    pair = pl.BlockSpec((N_PAIR, tm, 128), lambda i: (0, i, 0))
    return pl.pallas_call(
        body, name="in_proj_bwd", grid=(t // tm,),
        in_specs=[pl.BlockSpec((tm, SHIFT_COLS), lambda i: (i, 0))] + [pair] * 3
                 + [pl.BlockSpec((D_MODEL, IN_COLS), lambda i: (0, 0)), row, pl.BlockSpec((1, D_MODEL), lambda i: (0, 0)), row],
        out_specs=[pl.BlockSpec((tm, IN_COLS), lambda i: (i, 0)), row, pl.BlockSpec((8, D_MODEL), lambda i: (0, 0))],
        out_shape=[jax.ShapeDtypeStruct((t, IN_COLS), BF16), jax.ShapeDtypeStruct((t, D_MODEL), F32),
                   jax.ShapeDtypeStruct((8, D_MODEL), F32)],
        compiler_params=_params(("arbitrary",)),
    )(dpa, dq, dk, dv, win, x, g1, dx1)


def _pad_lora(w, lo):
    z = jnp.zeros((64, RW), F32)
    return jnp.concatenate([w, z], axis=0) if lo == 0 else jnp.concatenate([z, w], axis=0)


def _local_step(x, tgt, win, vecs, w2, a2, g2m, get_rest, send_rest):
    pw = (vecs["mu_shift"], vecs["decay_w0"], _pad_lora(w2, 0), vecs["iclr_a0"], _pad_lora(a2, 64), g2m,
          vecs["k_k"], vecs["k_a"])
    h, proj, qkv = _in_proj(x, vecs["mix_norm_g"], win)
    r, lw, k2, v, kk, a, g = _prep_fwd(proj, pw)
    y, s0s = _wkv_fwd(r, lw, k2, v, kk, a)
    ya = _post_fwd(y, r, k2, v, g, vecs["ln_x_w"], vecs["ln_x_b"], vecs["r_k"])
    o_att, l_att = _attn_fwd(qkv)
    yb = _combine_fwd(o_att, l_att, vecs["attn_out_g"])

    wout, wg, wu, wd = get_rest(yb)
    ycat = jnp.concatenate([ya, yb], axis=1)
    x1, h2 = _out_proj(x, ycat, wout, vecs["ffn_norm_g"])
    gt, up, act = _ffn_up(h2, wg, wu)
    dx2, dx2b, loss8, dgf = _ffn_down_loss(x1, act, wd, vecs["final_norm_g"], tgt)

    dgt, dup = _ffn_bwd_act(dx2b, wd, gt, up)
    dx1, dx1b, dya, dyb, dg2n = _ffn_bwd_h(dgt, dup, wg, wu, dx2, x1, vecs["ffn_norm_g"], wout)
    gw = {
        "w_down": _wgrad(act, dx2b, 1408, 1024, "wgrad_down"),
        "w_gate": _wgrad(h2, dgt, 1024, 1408, "wgrad_gate"),
        "w_up": _wgrad(h2, dup, 1024, 1408, "wgrad_up"),
        "w_out": _wgrad(ycat, dx1b, 1024, 1024, "wgrad_out"),
    }

    lnw = vecs["ln_x_w"] + send_rest(gw)[0, 0]
    dy, dr_p, dk2_p, dv_p, dg, dlnw, dlnb, drk = _post_bwd(dya, y, r, k2, v, g, lnw, vecs["ln_x_b"], vecs["r_k"])
    dr_s, dlw, dk2_s, dv_s, dkk, da = _wkv_bwd(dy, s0s, r, lw, k2, v, kk, a)
    dpa, dmu, dw0, dw2p, da0, da2p, dg2m, dk_k, dk_a = _prep_bwd(
        proj, pw, (dr_p, dr_s, dlw, dk2_p, dk2_s, dv_p, dv_s, dkk, da, dg))

    do_att, dl_att, dog = _combine_bwd(dyb, o_att, l_att, vecs["attn_out_g"])
    dq, dk, dv = _attn_bwd(do_att, dl_att, qkv)
    dproj, dx, dg1 = _in_proj_bwd(dpa, dq, dk, dv, win, x, vecs["mix_norm_g"], dx1)
    gw["w_in"] = _wgrad(h, dproj, 1024, 1664, "wgrad_in")
    gw["decay_w2"] = dw2p[:64]
    gw["iclr_a2"] = da2p[64:]
    gw["gate_g2"] = dg2m
    gv = {"mix_norm_g": dg1, "mu_shift": dmu, "decay_w0": dw0, "iclr_a0": da0, "k_k": dk_k, "k_a": dk_a, "r_k": drk,
          "ln_x_w": dlnw, "ln_x_b": dlnb, "attn_out_g": dog, "ffn_norm_g": dg2n, "final_norm_g": dgf}
    return loss8, dx, gw, gv


N_CHIP = 4
N_DEV = 8
MATS = ("w_in", "w_out", "w_gate", "w_up", "w_down")
LORAS = ("decay_w2", "iclr_a2", "gate_g2")
VECS = (("mix_norm_g", 1024), ("mu_shift", 1792), ("decay_w0", 512), ("iclr_a0", 512), ("k_k", 512), ("k_a", 512),
        ("r_k", 512), ("ln_x_w", 512), ("ln_x_b", 512), ("attn_out_g", 512), ("ffn_norm_g", 1024),
        ("final_norm_g", 1024))
N_VEC = sum(n for _, n in VECS)
N_SMALL = N_VEC + 128
ROWS_PAD = 3328
ANY = pl.BlockSpec(memory_space=pl.ANY)


def _flip(v, f):
    return 1 - v if f else v


class _Me:
    def __init__(self, mode):
        x, y, c = lax.axis_index("x"), lax.axis_index("y"), lax.axis_index("c")
        self.core, self.chip, self.dev = c, 2 * x + y, 4 * x + 2 * y + c
        self.sibling = (x, y, 1 - c)
        if mode == "chips":
            self.peers = [(px, py, c) for px, py in ((1 - x, y), (x, 1 - y), (1 - x, 1 - y))]
        else:
            self.peers = [(_flip(x, k & 4), _flip(y, k & 2), _flip(c, k & 1)) for k in range(1, N_DEV)]


def _half(core, rows):
    h = rows // 2
    return pl.ds(pl.multiple_of(core * h, h), h)


def _landing(a, kind):
    if kind == "gather":
        return (N_CHIP,) + a.shape
    if kind == "scatter":
        return (N_DEV, a.shape[1] // 2, a.shape[2])
    return (N_DEV,) + a.shape


def _peer_copy(srcs, dsts, kinds, send_sems, recv_sems, me, j, i, incoming):
    px, py, pc = me.peers[j]
    pchip, pdev = 2 * px + py, 4 * px + 2 * py + pc
    src, dst, kind = srcs[i], dsts[i], kinds[i]
    if kind == "gather":
        rows = _half(me.core, src.shape[0])
        src, dst = src.at[rows], dst.at[pchip if incoming else me.chip, rows]
    elif kind == "scatter":
        src, dst = src.at[pchip, _half(pc, src.shape[1])], dst.at[pdev if incoming else me.dev]
    else:
        dst = dst.at[pdev if incoming else me.dev]
    n = len(srcs)
    return pltpu.make_async_remote_copy(src_ref=src, dst_ref=dst, send_sem=send_sems.at[n * j + i],
                                        recv_sem=recv_sems.at[n * j + i], device_id=(px, py, pc), device_id_type=MESH)


def _mode(kinds):
    return "chips" if kinds[0] == "gather" else "devs"


def _npeer(kinds):
    return N_CHIP - 1 if kinds[0] == "gather" else N_DEV - 1


def _swap_blocking(arrs, lands, kinds, name):
    n = len(arrs)

    def body(*refs):
        srcs, dsts = refs[:n], refs[2 * n:3 * n]
        send_sems, recv_sems = refs[3 * n:]
        me = _Me(_mode(kinds))
        sends = [_peer_copy(srcs, dsts, kinds, send_sems, recv_sems, me, j, i, False)
                 for j in range(len(me.peers)) for i in range(n)]
        for cp in sends:
            cp.start()
        for j in range(len(me.peers)):
            for i in range(n):
                _peer_copy(srcs, dsts, kinds, send_sems, recv_sems, me, j, i, True).wait_recv()
        for cp in sends:
            cp.wait_send()

    ns = _npeer(kinds) * n
    return pl.pallas_call(
        body, name=name, in_specs=[ANY] * (2 * n), out_specs=[ANY] * n,
        out_shape=[jax.ShapeDtypeStruct(l.shape, l.dtype) for l in lands],
        input_output_aliases={n + i: i for i in range(n)},
        scratch_shapes=[pltpu.SemaphoreType.DMA((ns,)), pltpu.SemaphoreType.DMA((ns,))],
    )(*arrs, *lands)


def _swap_gathered(lands, name):
    n = len(lands)

    def body(*refs):
        dsts, send_sems, recv_sems = refs[n:2 * n], refs[2 * n], refs[2 * n + 1]
        me = _Me("chips")

        def copy(j, i, incoming):
            px, py, _ = me.peers[j]
            rows_out, rows_in = _half(me.core, dsts[i].shape[1]), _half(1 - me.core, dsts[i].shape[1])
            return pltpu.make_async_remote_copy(
                src_ref=dsts[i].at[2 * px + py, rows_out], dst_ref=dsts[i].at[2 * px + py, rows_in if incoming else rows_out],
                send_sem=send_sems.at[n * j + i], recv_sem=recv_sems.at[n * j + i], device_id=me.sibling, device_id_type=MESH)

        sends = [copy(j, i, False) for j in range(3) for i in range(n)]
        for cp in sends:
            cp.start()
        for j in range(3):
            for i in range(n):
                copy(j, i, True).wait_recv()
        for cp in sends:
            cp.wait_send()

    return pl.pallas_call(
        body, name=name, in_specs=[ANY] * n, out_specs=[ANY] * n,
        out_shape=[jax.ShapeDtypeStruct(l.shape, l.dtype) for l in lands],
        input_output_aliases={i: i for i in range(n)},
        scratch_shapes=[pltpu.SemaphoreType.DMA((3 * n,)), pltpu.SemaphoreType.DMA((3 * n,))],
    )(*lands)


def _join_halves(sums, name):
    n = len(sums)

    def body(*refs):
        dsts, send_sems, recv_sems = refs[n:2 * n], refs[2 * n], refs[2 * n + 1]
        me = _Me("chips")

        def copy(i, incoming):
            mine, other = _half(me.core, dsts[i].shape[0]), _half(1 - me.core, dsts[i].shape[0])
            return pltpu.make_async_remote_copy(src_ref=dsts[i].at[mine], dst_ref=dsts[i].at[other if incoming else mine],
                                                send_sem=send_sems.at[i], recv_sem=recv_sems.at[i],
                                                device_id=me.sibling, device_id_type=MESH)

        sends = [copy(i, False) for i in range(n)]
        for cp in sends:
            cp.start()
        for i in range(n):
            copy(i, True).wait_recv()
        for cp in sends:
            cp.wait_send()

    return pl.pallas_call(
        body, name=name, in_specs=[ANY] * n, out_specs=[ANY] * n,
        out_shape=[jax.ShapeDtypeStruct(s.shape, s.dtype) for s in sums],
        input_output_aliases={i: i for i in range(n)},
        scratch_shapes=[pltpu.SemaphoreType.DMA((n,)), pltpu.SemaphoreType.DMA((n,))],
    )(*sums)


HBM = pl.BlockSpec(memory_space=pltpu.HBM)
SEM = pl.BlockSpec(memory_space=pltpu.SEMAPHORE)
EFFECT = pltpu.SideEffectType.DATAFLOW_SIDE_EFFECTING


def _swap_start(arrs, lands, kinds, name):
    n = len(arrs)

    def body(*refs):
        srcs, dsts, send_sems, recv_sems, token = refs[:n], refs[n:2 * n], refs[2 * n], refs[2 * n + 1], refs[-1]
        me = _Me(_mode(kinds))
        for j in range(len(me.peers)):
            for i in range(n):
                _peer_copy(srcs, dsts, kinds, send_sems, recv_sems, me, j, i, False).start()
        token[...] = jnp.zeros_like(token)

    ns = _npeer(kinds) * n
    outs = pl.pallas_call(
        body, name=name,
        out_shape=(pltpu.SemaphoreType.DMA((ns,)), pltpu.SemaphoreType.DMA((ns,)),
                   *[pltpu.HBM(a.shape, a.dtype) for a in arrs], *[pltpu.HBM(l.shape, l.dtype) for l in lands],
                   jax.ShapeDtypeStruct((8, 128), F32)),
        in_specs=[HBM] * (2 * n), out_specs=(SEM, SEM, *[HBM] * (2 * n), pl.BlockSpec(memory_space=pltpu.VMEM)),
        input_output_aliases={k: 2 + k for k in range(2 * n)},
        compiler_params=pltpu.CompilerParams(has_side_effects=EFFECT),
    )(*[pltpu.with_memory_space_constraint(a, pltpu.HBM) for a in arrs],
      *[pltpu.with_memory_space_constraint(l, pltpu.HBM) for l in lands])
    return outs[0], outs[1], outs[2:2 + n], outs[2 + n:2 + 2 * n], outs[-1]


def _swap_wait(send_sems, recv_sems, srcs_thru, lands_thru, after, kinds, name):
    n = len(srcs_thru)

    def body(*refs):
        srcs, dsts, s_sems, r_sems = refs[:n], refs[n:2 * n], refs[2 * n], refs[2 * n + 1]
        me = _Me(_mode(kinds))
        for j in range(len(me.peers)):
            for i in range(n):
                cp = _peer_copy(srcs, dsts, kinds, s_sems, r_sems, me, j, i, True)
                cp.wait_send()
                cp.wait_recv()

    outs = pl.pallas_call(
        body, name=name,
        out_shape=tuple(pltpu.HBM(a.shape, a.dtype) for a in (*srcs_thru, *lands_thru)),
        in_specs=[HBM] * (2 * n) + [SEM, SEM, ANY], out_specs=tuple([HBM] * (2 * n)),
        input_output_aliases={k: k for k in range(2 * n)},
        compiler_params=pltpu.CompilerParams(has_side_effects=EFFECT),
    )(*srcs_thru, *lands_thru, send_sems, recv_sems, after)
    return outs[n:]


def _adamw(w, g, m, v):
    m = ADAM_B1 * m + (1.0 - ADAM_B1) * g
    v = ADAM_B2 * v + (1.0 - ADAM_B2) * (g * g)
    m_hat = m / (1.0 - ADAM_B1 ** ADAM_STEP)
    v_hat = v / (1.0 - ADAM_B2 ** ADAM_STEP)
    delta = -ADAM_LR * (m_hat / (jnp.sqrt(v_hat) + ADAM_EPS) + ADAM_WD * w)
    return delta, m, v


def _reduce8(rbuf, core, tr, name):
    _, h, cols = rbuf.shape

    def body(core_ref, r_ref, g_ref):
        g = r_ref[0].astype(F32)
        for s in range(1, N_DEV):
            g = g + r_ref[s].astype(F32)
        g_ref[...] = g

    return pl.pallas_call(
        body, name=name,
        grid_spec=pltpu.PrefetchScalarGridSpec(
            num_scalar_prefetch=1, grid=(h // tr,),
            in_specs=[pl.BlockSpec((N_DEV, tr, cols), lambda i, core_ref: (0, i, 0))],
            out_specs=pl.BlockSpec((tr, cols), lambda i, core_ref: (core_ref[0] * (h // tr) + i, 0))),
        out_shape=jax.ShapeDtypeStruct((2 * h, cols), F32),
        compiler_params=_params(("parallel",)),
    )(core, rbuf)


def _adamw_call(g, w, m, v, tr, name):
    _, rows, cols = w.shape

    def body(g_in, w_ref, m_ref, v_ref, g_ref, d_ref, nm_ref, nv_ref):
        g = g_in[...]
        g_ref[0] = g
        d_ref[0], nm_ref[0], nv_ref[0] = _adamw(w_ref[0], g, m_ref[0], v_ref[0])

    row = pl.BlockSpec((1, tr, cols), lambda i: (0, i, 0))
    return pl.pallas_call(
        body, name=name, grid=(rows // tr,),
        in_specs=[pl.BlockSpec((tr, cols), lambda i: (i, 0)), row, row, row], out_specs=[row] * 4,
        out_shape=[jax.ShapeDtypeStruct(w.shape, F32)] * 4,
        compiler_params=_params(("parallel",)),
    )(g, w, m, v)


def _reduce_adamw_small(sbuf, w, m, v):
    def body(s_ref, w_ref, m_ref, v_ref, g_ref, d_ref, nm_ref, nv_ref, loss_ref):
        tot = s_ref[0]
        for s in range(1, N_DEV):
            tot = tot + s_ref[s]
        tot = jnp.sum(tot, axis=0, keepdims=True)
        g = tot[:, :N_VEC]
        g_ref[...] = g
        d_ref[...], nm_ref[...], nv_ref[...] = _adamw(w_ref[...], g, m_ref[...], v_ref[...])
        loss_ref[...] = tot[:, N_VEC:]

    return pl.pallas_call(
        body, name="reduce_adamw_small",
        out_shape=[jax.ShapeDtypeStruct((1, N_VEC), F32)] * 4 + [jax.ShapeDtypeStruct((1, 128), F32)],
    )(sbuf, w, m, v)


_ROW_SHARDED = ("w_out", "w_down")
_ADAM_TILE = {"w_in": 256, "w_out": 256, "w_gate": 256, "w_up": 256, "w_down": 176, "decay_w2": 64, "iclr_a2": 64,
              "gate_g2": 128}
_SUM_TILE = {"w_in": 256, "w_out": 128, "w_gate": 256, "w_up": 256, "w_down": 176, "decay_w2": 32, "iclr_a2": 32,
             "gate_g2": 64}


def _full(n, stacked):
    p, r, c = stacked.shape
    if n in _ROW_SHARDED:
        return stacked.reshape(p * r, c)
    return jnp.transpose(stacked, (1, 0, 2)).reshape(r, p * c)


def _by_chip(n, full):
    if n in _ROW_SHARDED:
        return full.reshape(N_CHIP, full.shape[0] // N_CHIP, full.shape[1])
    r, c = full.shape
    return jnp.transpose(full.reshape(r, N_CHIP, c // N_CHIP), (1, 0, 2))


def _with_own(land_shape, dtype, own, slot):
    return lax.dynamic_update_slice(lax.empty(land_shape, dtype), own[None], (slot,) + (0,) * own.ndim)


def kernel(x, mix_norm_g, w_in, mu_shift, decay_w0, decay_w2, iclr_a0, iclr_a2, gate_g2, k_k, k_a, r_k, ln_x_w, ln_x_b, attn_out_g, w_out, ffn_norm_g, w_gate, w_up, w_down, final_norm_g, loss_target, m_mix_norm_g, m_w_in, m_mu_shift, m_decay_w0, m_decay_w2, m_iclr_a0, m_iclr_a2, m_gate_g2, m_k_k, m_k_a, m_r_k, m_ln_x_w, m_ln_x_b, m_attn_out_g, m_w_out, m_ffn_norm_g, m_w_gate, m_w_up, m_w_down, m_final_norm_g, v_mix_norm_g, v_w_in, v_mu_shift, v_decay_w0, v_decay_w2, v_iclr_a0, v_iclr_a2, v_gate_g2, v_k_k, v_k_a, v_r_k, v_ln_x_w, v_ln_x_b, v_attn_out_g, v_w_out, v_ffn_norm_g, v_w_gate, v_w_up, v_w_down, v_final_norm_g):
    names = ("mix_norm_g", "w_in", "mu_shift", "decay_w0", "decay_w2", "iclr_a0", "iclr_a2", "gate_g2", "k_k", "k_a",
             "r_k", "ln_x_w", "ln_x_b", "attn_out_g", "w_out", "ffn_norm_g", "w_gate", "w_up", "w_down", "final_norm_g")
    w = dict(zip(names, (mix_norm_g, w_in, mu_shift, decay_w0, decay_w2, iclr_a0, iclr_a2, gate_g2, k_k, k_a, r_k,
                         ln_x_w, ln_x_b, attn_out_g, w_out, ffn_norm_g, w_gate, w_up, w_down, final_norm_g)))
    m = dict(zip(names, (m_mix_norm_g, m_w_in, m_mu_shift, m_decay_w0, m_decay_w2, m_iclr_a0, m_iclr_a2, m_gate_g2,
                         m_k_k, m_k_a, m_r_k, m_ln_x_w, m_ln_x_b, m_attn_out_g, m_w_out, m_ffn_norm_g, m_w_gate,
                         m_w_up, m_w_down, m_final_norm_g)))
    v = dict(zip(names, (v_mix_norm_g, v_w_in, v_mu_shift, v_decay_w0, v_decay_w2, v_iclr_a0, v_iclr_a2, v_gate_g2,
                         v_k_k, v_k_a, v_r_k, v_ln_x_w, v_ln_x_b, v_attn_out_g, v_w_out, v_ffn_norm_g, v_w_gate,
                         v_w_up, v_w_down, v_final_norm_g)))
    first = ("w_in",) + LORAS
    rest = ("w_out", "w_gate", "w_up", "w_down")
    xi, yi, ci = lax.axis_index("x"), lax.axis_index("y"), lax.axis_index("c")
    my_chip, my_dev = 2 * xi + yi, 4 * xi + 2 * yi + ci
    gather, scatter = ("gather",) * 4, ("scatter",) * 4

    wb = {n: w[n][0].astype(BF16) for n in MATS}
    lands = [_with_own((N_CHIP,) + wb[n].shape, BF16, wb[n], my_chip) for n in rest]
    ssem, rsem, srcs_thru, lands_thru, tok = _swap_start([wb[n] for n in rest], lands, gather, "gather_rest_start")
    mine = [wb["w_in"]] + [w[n][0] for n in LORAS]
    got = _swap_blocking(mine, [_with_own((N_CHIP,) + a.shape, a.dtype, a, my_chip) for a in mine], gather, "gather_first")
    win, w2, a2, g2m = (_full(n, z) for n, z in zip(first, _swap_gathered(got, "gather_first_halves")))

    vecs = {n: w[n].reshape(1, sz) for n, sz in VECS}
    vecs["mix_norm_g"] = vecs["mix_norm_g"] + tok[0, 0]

    def get_rest(after):
        halves = _swap_wait(ssem, rsem, srcs_thru, lands_thru, after, gather, "gather_rest_wait")
        return [_full(n, z) for n, z in zip(rest, _swap_gathered(halves, "gather_rest_halves"))]

    flight = []

    def my_half(g):
        h = g.shape[1] // 2
        return lax.dynamic_slice(g, (my_chip, ci * h, 0), (1, h, g.shape[2]))[0]

    def send_rest(gw):
        gs = [_by_chip(n, gw[n]).astype(BF16) for n in rest]
        into = [_with_own((N_DEV,) + my_half(g).shape, BF16, my_half(g), my_dev) for g in gs]
        flight.extend(_swap_start(gs, into, scatter, "exchange_rest_start"))
        return flight[4]

    loss8, dx, gw, gv = _local_step(x[0], loss_target[0], win, vecs, w2, a2, g2m, get_rest, send_rest)

    small = jnp.concatenate([gv[n] for n, _ in VECS] + [loss8], axis=1)
    gs = [_by_chip(n, gw[n]).astype(BF16) for n in first]
    into = [_with_own((N_DEV,) + my_half(g).shape, BF16, my_half(g), my_dev) for g in gs]
    into.append(_with_own((N_DEV,) + small.shape, F32, small, my_dev))
    last = _swap_start(gs + [small], into, scatter + ("all",), "exchange_first_start")

    core = jnp.reshape(ci, (1,)).astype(jnp.int32)

    def update(group, rbufs, tag):
        sums = [_reduce8(rb, core, _SUM_TILE[n], "reduce_" + n) for n, rb in zip(group, rbufs)]
        gsum = _join_halves(sums, "join_halves_" + tag)
        return {n: _adamw_call(g, w[n], m[n], v[n], _ADAM_TILE[n], "adamw_" + n) for n, g in zip(group, gsum)}

    res = update(rest, _swap_wait(flight[0], flight[1], flight[2], flight[3], last[4], scatter, "exchange_rest_wait"), "rest")
    got = _swap_wait(last[0], last[1], last[2], last[3], res["w_down"][1], scatter + ("all",), "exchange_first_wait")
    res.update(update(first, got[:4], "first"))
    cat = lambda d: jnp.concatenate([d[n].reshape(1, sz) for n, sz in VECS], axis=1)
    small_res = _reduce_adamw_small(got[4], cat(w), cat(m), cat(v))

    outs = []
    for k in range(4):
        piece = {n: r[k] for n, r in res.items()}
        c0 = 0
        for n, sz in VECS:
            piece[n] = small_res[k][0, c0:c0 + sz].reshape(w[n].shape)
            c0 += sz
        outs.extend(piece[n] for n in names)
    return (small_res[4][0, 0], dx[None], *outs)
```

```python
import jax
import jax.numpy as jnp
from jax import lax
from jax.experimental import pallas as pl
from jax.experimental.pallas import tpu as pltpu

F32 = jnp.float32
BF16 = jnp.bfloat16
HI = lax.Precision.HIGHEST

D_MODEL = 1024
HEAD_DIM = 64
RW = 512
N_PAIR = RW // 128
SHIFT_COLS = 1792
IN_COLS = 3328
D_FF = 2816
NORM_EPS = 1e-6
GN_EPS = 64e-5
CHUNK = 64
SUB = 16
WKV_PASSES = 1
ATTN_PASSES = 1
ATTN_BLOCK = 128
DILATIONS = (1, 4, 16)
NEG = -1e30
ADAM_LR, ADAM_B1, ADAM_B2, ADAM_EPS, ADAM_WD, ADAM_STEP = 0.001, 0.9, 0.999, 1e-08, 0.01, 10
VMEM_LIMIT = 56 * 1024 * 1024
MESH = pl.DeviceIdType.MESH


def _params(sem=None, **kw):
    return pltpu.CompilerParams(dimension_semantics=sem, vmem_limit_bytes=VMEM_LIMIT, **kw)


def _dot(a, b, prec=None):
    return lax.dot_general(a, b, (((1,), (0,)), ((), ())), preferred_element_type=F32, precision=prec)


def _dot_nt(a, b, prec=None):
    return lax.dot_general(a, b, (((1,), (1,)), ((), ())), preferred_element_type=F32, precision=prec)


def _dot_tn(a, b, prec=None):
    return lax.dot_general(a, b, (((0,), (0,)), ((), ())), preferred_element_type=F32, precision=prec)


_FORMS = {"nn": ((1,), (0,)), "nt": ((1,), (1,)), "tn": ((0,), (0,))}


def _dg(a, b, form):
    if a.ndim == 3 or b.ndim == 3:
        nb = a.shape[0] if a.ndim == 3 else b.shape[0]
        return jnp.stack([_dg(a[i] if a.ndim == 3 else a, b[i] if b.ndim == 3 else b, form) for i in range(nb)], axis=0)
    return lax.dot_general(a, b, (_FORMS[form], ((), ())), preferred_element_type=F32)


def _split2(x):
    hi = x.astype(BF16)
    return hi, (x - hi.astype(F32)).astype(BF16)


def _split3(x):
    hi = x.astype(BF16)
    rest = x - hi.astype(F32)
    mid = rest.astype(BF16)
    return hi, mid, (rest - mid.astype(F32)).astype(BF16)


def _mm_raw(a, b, form, mode):
    if mode == 1:
        return _dg(a.astype(BF16), b.astype(BF16), form)
    if mode == 3:
        ah, al = _split2(a)
        bh, bl = _split2(b)
        return _dg(ah, bh, form) + (_dg(ah, bl, form) + _dg(al, bh, form))
    if mode == "L3":
        ab = a.astype(BF16)
        b1, b2, b3 = _split3(b)
        return _dg(ab, b1, form) + (_dg(ab, b2, form) + _dg(ab, b3, form))
    assert mode == "R3", mode
    bb = b.astype(BF16)
    a1, a2, a3 = _split3(a)
    return _dg(a1, bb, form) + (_dg(a2, bb, form) + _dg(a3, bb, form))


def _mm(a, b, form, mode):
    @jax.custom_vjp
    def f(a, b):
        return _mm_raw(a, b, form, mode)

    def fwd(a, b):
        return _mm_raw(a, b, form, mode), (a, b)

    def bwd(res, ct):
        a, b = res
        la = {1: 1, 3: 3, "L3": None, "R3": "R3"}[mode]
        lb = {1: 1, 3: 3, "L3": "L3", "R3": None}[mode]
        if form == "nn":
            da = None if la is None else _mm_raw(ct, b, "nt", la)
            db = None if lb is None else _mm_raw(a, ct, "tn", lb)
        elif form == "nt":
            da = None if la is None else _mm_raw(ct, b, "nn", la)
            db = None if lb is None else _mm_raw(ct, a, "tn", "R3" if lb == "L3" else lb)
        else:
            da = None if la is None else _mm_raw(b, ct, "nt", "L3" if la == "R3" else la)
            db = None if lb is None else _mm_raw(a, ct, "nn", lb)
        return (jnp.zeros_like(a) if da is None else da, jnp.zeros_like(b) if db is None else db)

    f.defvjp(fwd, bwd)
    return f(a, b)


def _seg_ones(n):
    r = lax.broadcasted_iota(jnp.int32, (n, n), 0) // HEAD_DIM
    c = lax.broadcasted_iota(jnp.int32, (n, n), 1) // HEAD_DIM
    return (r == c).astype(F32)


def _segsum(x, seg):
    return _mm(x, seg, "nn", "R3")


def _rms_fwd(x, g):
    rstd = lax.rsqrt(jnp.mean(x * x, axis=-1, keepdims=True) + NORM_EPS)
    return x * rstd * g


def _rms_bwd(dy, x, g):
    rstd = lax.rsqrt(jnp.mean(x * x, axis=-1, keepdims=True) + NORM_EPS)
    xn = x * rstd
    dxn = dy * g
    dx = rstd * (dxn - xn * jnp.mean(dxn * xn, axis=-1, keepdims=True))
    return dx, dy * xn


def _sigmoid(x):
    return 1.0 / (1.0 + jnp.exp(-x))


def _softplus(x):
    return jnp.maximum(x, 0.0) + jnp.log(1.0 + jnp.exp(-jnp.abs(x)))


def _acc(ref, val, first):
    @pl.when(first)
    def _():
        ref[...] = val

    @pl.when(jnp.logical_not(first))
    def _():
        ref[...] += val


def _colsum8(v):
    rows, n = v.shape
    return jnp.sum(v.reshape(rows // 8, 8, n), axis=0)


def _prep_fn(p, pprev, mu, w0, w2p, a0, a2p, g2, k_k, k_a):
    seg = _seg_ones(RW)
    ps = p + (pprev - p) * mu
    r = ps[:, 0:RW]
    k = ps[:, RW:2 * RW]
    v = ps[:, 2 * RW:3 * RW]
    xwa = ps[:, 3 * RW:3 * RW + 128]
    xg = ps[:, 3 * RW + 128:3 * RW + 256]
    wraw = -_softplus(-(w0 + _mm(jnp.tanh(xwa), w2p, "nn", 3))) - 0.5
    lw = -jnp.exp(wraw)
    a = _sigmoid(a0 + _mm(xwa, a2p, "nn", 3))
    g = _mm(_sigmoid(xg), g2, "nn", 3)
    kk = k * k_k
    kk = kk / jnp.maximum(jnp.sqrt(_segsum(kk * kk, seg)), 1e-12)
    k2 = k * (1.0 + (a - 1.0) * k_a)
    return r, lw, k2, v, kk, a, g


def _solve_unit_lower(lmat, rhs):
    c = lmat.shape[-1]
    row = lax.broadcasted_iota(jnp.int32, (c, c), 0)
    col = lax.broadcasted_iota(jnp.int32, (c, c), 1)
    eye = (row == col).astype(F32)
    ld = jnp.where(row // SUB == col // SUB, lmat, 0.0)
    lo = lmat - ld
    x = eye + ld
    m = ld
    mm = lambda p, q: _mm(p, q, "nn", WKV_PASSES)
    for _ in range(3):
        m = mm(m, m)
        x = x + mm(x, m)
    g = mm(x, lo)
    g2 = mm(g, g)
    w = mm(x, rhs)
    w = w + mm(g2, w)
    return w + mm(g, w)


def _wkv_chunk_fn(s0, r, lw, k, v, kk, a):
    c = r.shape[-2]
    n = 2 * c
    row = lax.broadcasted_iota(jnp.int32, (n, n), 0)
    col = lax.broadcasted_iota(jnp.int32, (n, n), 1)
    same = (row // c) == (col // c)
    incl = jnp.logical_and(row >= col, same)
    strict = jnp.logical_and(row > col, same)
    sel = (lax.broadcasted_iota(jnp.int32, (n, 128), 0) // c) == (lax.broadcasted_iota(jnp.int32, (n, 128), 1) // HEAD_DIM)
    two = lambda z: jnp.concatenate([z, z], axis=-2)
    lw2 = two(lw)
    mm = lambda p_, q_, form: _mm(p_, q_, form, WKV_PASSES)
    cl = _mm(incl.astype(F32), lw2, "nn", "L3")
    p = jnp.exp(cl)
    pinv = jnp.exp(-cl)
    pprev = jnp.exp(cl - lw2)
    kk2 = two(kk)
    at = jnp.where(sel, -kk2 * pprev, 0.0)
    bt = jnp.where(sel, kk2 * two(a) * pinv, 0.0)
    kt = jnp.where(sel, two(k) * pinv, 0.0)
    rt = jnp.where(sel, two(r) * p, 0.0)
    vt = jnp.where(sel, two(v), 0.0)
    ab = jnp.where(strict, mm(at, bt, "nt"), 0.0)
    ak = jnp.where(strict, mm(at, kt, "nt"), 0.0)
    rb = jnp.where(incl, mm(rt, bt, "nt"), 0.0)
    rk = jnp.where(incl, mm(rt, kt, "nt"), 0.0)
    u = _solve_unit_lower(ab, mm(at, s0, "nt") + mm(ak, vt, "nn"))
    y2 = mm(rt, s0, "nt") + mm(rb, u, "nn") + mm(rk, vt, "nn")
    plast = jnp.exp(jnp.sum(lw, axis=-2, keepdims=True))
    s1 = (s0 + mm(u, bt, "tn") + mm(vt, kt, "tn")) * plast
    r2 = lax.broadcasted_iota(jnp.int32, (128, 128), 0) // HEAD_DIM
    c2 = lax.broadcasted_iota(jnp.int32, (128, 128), 1) // HEAD_DIM
    return y2[..., :c, :] + y2[..., c:, :], jnp.where(r2 == c2, s1, 0.0)


def _post_fn(y, r, k2, v, g, lnw, lnb, rk):
    seg = _seg_ones(RW)
    mean = _segsum(y, seg) * (1.0 / HEAD_DIM)
    yc = y - mean
    var = _segsum(yc * yc, seg) * (1.0 / HEAD_DIM)
    yn = yc * lax.rsqrt(var + GN_EPS)
    out = yn * lnw + lnb + _segsum(r * k2 * rk, seg) * v
    return out * g


def _attn_block_fn(q, kc, vc, kp=None, vp=None):
    n = ATTN_BLOCK
    qi = lax.broadcasted_iota(jnp.int32, (n, n), 0)
    kj = lax.broadcasted_iota(jnp.int32, (n, n), 1)
    lane = lax.broadcasted_iota(jnp.int32, (1, 128), 1)
    scale = HEAD_DIM ** -0.5
    os_, ls_ = [], []
    for h in range(2):
        mh = (lane // HEAD_DIM) == h
        qh = jnp.where(mh, q, 0.0)
        sc = jnp.where(kj <= qi, _mm(qh, kc, "nt", ATTN_PASSES) * scale, NEG)
        m = jnp.max(sc, axis=-1, keepdims=True)
        if kp is not None:
            sp = jnp.where(kj >= qi, _mm(qh, kp, "nt", ATTN_PASSES) * scale, NEG)
            m = jnp.maximum(m, jnp.max(sp, axis=-1, keepdims=True))
        pc = jnp.exp(sc - m)
        den = jnp.sum(pc, axis=-1, keepdims=True)
        num = _mm(pc, vc, "nn", ATTN_PASSES)
        if kp is not None:
            pp = jnp.exp(sp - m)
            den = den + jnp.sum(pp, axis=-1, keepdims=True)
            num = num + _mm(pp, vp, "nn", ATTN_PASSES)
        os_.append(num / den)
        ls_.append(m + jnp.log(den))
    m0 = (lane // HEAD_DIM) == 0
    return jnp.where(m0, os_[0], os_[1]), jnp.where(m0, ls_[0], ls_[1])


def _combine_fn(o1, o2, o3, l1, l2, l3, og):
    seg = _seg_ones(o1.shape[-1])
    m = jnp.maximum(jnp.maximum(l1, l2), l3)
    e1, e2, e3 = jnp.exp(l1 - m), jnp.exp(l2 - m), jnp.exp(l3 - m)
    o = (e1 * o1 + e2 * o2 + e3 * o3) / (e1 + e2 + e3)
    o = o * lax.rsqrt(_segsum(o * o, seg) * (1.0 / HEAD_DIM) + NORM_EPS)
    return o * og


def _in_proj(x, g1, win):
    t = x.shape[0]
    tm = 256

    def body(x_ref, g_ref, w_ref, h_ref, pa_ref, qkv_ref):
        h = _rms_fwd(x_ref[...], g_ref[...]).astype(BF16)
        h_ref[...] = h
        proj = _dot_nt(h, w_ref[...])
        pa_ref[...] = proj[:, :SHIFT_COLS]
        for j in range(3):
            for p in range(N_PAIR):
                c0 = SHIFT_COLS + j * RW + p * 128
                qkv_ref[j, p] = proj[:, c0:c0 + 128]

    return pl.pallas_call(
        body, name="in_proj", grid=(t // tm,),
        in_specs=[pl.BlockSpec((tm, D_MODEL), lambda i: (i, 0)), pl.BlockSpec((1, D_MODEL), lambda i: (0, 0)),
                  pl.BlockSpec((IN_COLS, D_MODEL), lambda i: (0, 0))],
        out_specs=[pl.BlockSpec((tm, D_MODEL), lambda i: (i, 0)), pl.BlockSpec((tm, SHIFT_COLS), lambda i: (i, 0)),
                   pl.BlockSpec((3, N_PAIR, tm, 128), lambda i: (0, 0, i, 0))],
        out_shape=[jax.ShapeDtypeStruct((t, D_MODEL), BF16), jax.ShapeDtypeStruct((t, SHIFT_COLS), F32),
                   jax.ShapeDtypeStruct((3, N_PAIR, t, 128), F32)],
        compiler_params=_params(("parallel",)),
    )(x, g1, win)


def _shifted(p, last8, first):
    prow = jnp.where(first, 0.0, last8[7:8, :])
    rolled = pltpu.roll(p, 1, axis=0)
    rid = lax.broadcasted_iota(jnp.int32, p.shape, 0)
    return jnp.where(rid == 0, prow, rolled)


_PREP_TM = 256


def _prep_specs(tm):
    vec = lambda n: pl.BlockSpec((1, n), lambda i: (0, 0))
    mat = lambda r, n: pl.BlockSpec((r, n), lambda i: (0, 0))
    return [vec(SHIFT_COLS), vec(RW), mat(128, RW), vec(RW), mat(128, RW), mat(128, RW), vec(RW), vec(RW)]


def _prep_fwd(proj, pw):
    t = proj.shape[0]
    tm = _PREP_TM

    def body(p_ref, l8_ref, mu, w0, w2p, a0, a2p, g2, k_k, k_a, *outs):
        p = p_ref[...]
        pprev = _shifted(p, l8_ref[...], pl.program_id(0) == 0)
        res = _prep_fn(p, pprev, mu[...], w0[...], w2p[...], a0[...], a2p[...], g2[...], k_k[...], k_a[...])
        for o_ref, val in zip(outs, res):
            o_ref[...] = val

    row = pl.BlockSpec((tm, RW), lambda i: (i, 0))
    return pl.pallas_call(
        body, name="rwkv_prep", grid=(t // tm,),
        in_specs=[pl.BlockSpec((tm, SHIFT_COLS), lambda i: (i, 0)),
                  pl.BlockSpec((8, SHIFT_COLS), lambda i: (jnp.maximum(i * (tm // 8) - 1, 0), 0))] + _prep_specs(tm),
        out_specs=[row] * 7,
        out_shape=[jax.ShapeDtypeStruct((t, RW), F32)] * 7,
        compiler_params=_params(("parallel",)),
    )(proj, proj, *pw)


def _pairs(ref):
    return jnp.stack([ref[:, 128 * p:128 * (p + 1)] for p in range(N_PAIR)], axis=0)


def _wkv_fwd(r, lw, k2, v, kk, a):
    t = r.shape[0]
    nc = t // CHUNK

    def body(r_ref, lw_ref, k_ref, v_ref, kk_ref, a_ref, y_ref, s_ref, st):
        @pl.when(pl.program_id(0) == 0)
        def _():
            st[...] = jnp.zeros_like(st)

        s0 = st[...]
        s_ref[0] = s0
        y, s1 = _wkv_chunk_fn(s0, *[_pairs(ref) for ref in (r_ref, lw_ref, k_ref, v_ref, kk_ref, a_ref)])
        for p in range(N_PAIR):
            y_ref[:, 128 * p:128 * (p + 1)] = y[p]
        st[...] = s1

    blk = pl.BlockSpec((CHUNK, RW), lambda c: (c, 0))
    return pl.pallas_call(
        body, name="wkv_fwd", grid=(nc,),
        in_specs=[blk] * 6,
        out_specs=[blk, pl.BlockSpec((1, N_PAIR, 128, 128), lambda c: (c, 0, 0, 0))],
        out_shape=[jax.ShapeDtypeStruct((t, RW), F32), jax.ShapeDtypeStruct((nc, N_PAIR, 128, 128), F32)],
        scratch_shapes=[pltpu.VMEM((N_PAIR, 128, 128), F32)],
        compiler_params=_params(("arbitrary",)),
    )(r, lw, k2, v, kk, a)


_POST_TM = 256


def _post_fwd(y, r, k2, v, g, lnw, lnb, rk):
    t = y.shape[0]
    tm = _POST_TM

    def body(y_ref, r_ref, k_ref, v_ref, g_ref, lnw_ref, lnb_ref, rk_ref, o_ref):
        o_ref[...] = _post_fn(y_ref[...], r_ref[...], k_ref[...], v_ref[...], g_ref[...],
                              lnw_ref[...], lnb_ref[...], rk_ref[...]).astype(BF16)

    row = pl.BlockSpec((tm, RW), lambda i: (i, 0))
    vec = pl.BlockSpec((1, RW), lambda i: (0, 0))
    return pl.pallas_call(
        body, name="rwkv_post", grid=(t // tm,),
        in_specs=[row] * 5 + [vec] * 3, out_specs=row,
        out_shape=jax.ShapeDtypeStruct((t, RW), BF16),
        compiler_params=_params(("parallel",)),
    )(y, r, k2, v, g, lnw, lnb, rk)


ATTN_GROUP = 2


def _dilated_rows(d, r, n):
    if d == 1:
        return pl.ds(pl.multiple_of(n * ATTN_BLOCK, ATTN_BLOCK), ATTN_BLOCK)
    return pl.ds(r + n * (ATTN_BLOCK * d), ATTN_BLOCK, stride=d)


def _for_each_sequence(t, unit):
    for di, d in enumerate(DILATIONS):

        @pl.when(pl.program_id(1) == di)
        def _(di=di, d=d):
            nb = t // (ATTN_BLOCK * d)
            if d == 1:
                unit(di, [(d, 0, 0)], False)
                unit(di, [(d, 0, 1)], True)
                lax.fori_loop(1, nb // 2, lambda k, c: (unit(di, [(d, 0, 2 * k), (d, 0, 2 * k + 1)], True), c)[1], 0)
            else:

                def residues(r, carry):
                    unit(di, [(d, r, 0), (d, r + d // 2, 0)], False)
                    if nb > 1:
                        lax.fori_loop(1, nb, lambda n, c: (unit(di, [(d, r, n), (d, r + d // 2, n)], True), c)[1], 0)
                    return carry

                lax.fori_loop(0, d // 2, residues, 0)


def _take(ref, lead, rows_list):
    return jnp.stack([ref.at[(*lead, g)][rows, :] for rows in rows_list for g in range(ATTN_GROUP)], axis=0)


def _put(ref, lead, rows_list, val, add=False):
    k = 0
    for rows in rows_list:
        for g in range(ATTN_GROUP):
            if add:
                ref.at[(*lead, g)][rows, :] += val[k]
            else:
                ref.at[(*lead, g)][rows, :] = val[k]
            k += 1


def _attn_fwd(qkv):
    t = qkv.shape[2]

    def body(q_ref, k_ref, v_ref, o_ref, l_ref):
        def unit(di, places, has_prev):
            cur = [_dilated_rows(d, r, n) for d, r, n in places]
            args = [_take(ref, (0,), cur) for ref in (q_ref, k_ref, v_ref)]
            if has_prev:
                prv = [_dilated_rows(d, r, n - 1) for d, r, n in places]
                args += [_take(ref, (0,), prv) for ref in (k_ref, v_ref)]
            o, lse = _attn_block_fn(*args)
            _put(o_ref, (0,), cur, o)
            _put(l_ref, (0,), cur, lse)

        _for_each_sequence(t, unit)

    spec = lambda j: pl.BlockSpec((1, ATTN_GROUP, t, 128), lambda i, b: (j, i, 0, 0))
    out = pl.BlockSpec((1, ATTN_GROUP, t, 128), lambda i, b: (b, i, 0, 0))
    return pl.pallas_call(
        body, name="attn_fwd", grid=(N_PAIR // ATTN_GROUP, len(DILATIONS)),
        in_specs=[spec(0), spec(1), spec(2)], out_specs=[out, out],
        out_shape=[jax.ShapeDtypeStruct((3, N_PAIR, t, 128), F32)] * 2,
        compiler_params=_params(("parallel", "arbitrary")),
    )(qkv, qkv, qkv)


_COMB_TM = 256


def _combine_fwd(o, l, og):
    t = o.shape[2]
    tm = _COMB_TM

    def body(o_ref, l_ref, og_ref, y_ref):
        for p in range(N_PAIR):
            cols = slice(128 * p, 128 * (p + 1))
            y_ref[:, cols] = _combine_fn(o_ref[0, p], o_ref[1, p], o_ref[2, p], l_ref[0, p], l_ref[1, p], l_ref[2, p],
                                         og_ref[:, cols]).astype(BF16)

    blk = pl.BlockSpec((3, N_PAIR, tm, 128), lambda i: (0, 0, i, 0))
    return pl.pallas_call(
        body, name="attn_combine", grid=(t // tm,),
        in_specs=[blk, blk, pl.BlockSpec((1, RW), lambda i: (0, 0))], out_specs=pl.BlockSpec((tm, RW), lambda i: (i, 0)),
        out_shape=jax.ShapeDtypeStruct((t, RW), BF16),
        compiler_params=_params(("parallel",)),
    )(o, l, og)


def _out_proj(x, ycat, wout, g2):
    t = x.shape[0]
    tm = 256

    def body(x_ref, y_ref, w_ref, g_ref, x1_ref, h_ref):
        x1 = x_ref[...] + _dot(y_ref[...], w_ref[...])
        x1_ref[...] = x1
        h_ref[...] = _rms_fwd(x1, g_ref[...]).astype(BF16)

    row = pl.BlockSpec((tm, D_MODEL), lambda i: (i, 0))
    return pl.pallas_call(
        body, name="out_proj", grid=(t // tm,),
        in_specs=[row, row, pl.BlockSpec((D_MODEL, D_MODEL), lambda i: (0, 0)), pl.BlockSpec((1, D_MODEL), lambda i: (0, 0))],
        out_specs=[row, row],
        out_shape=[jax.ShapeDtypeStruct((t, D_MODEL), F32), jax.ShapeDtypeStruct((t, D_MODEL), BF16)],
        compiler_params=_params(("parallel",)),
    )(x, ycat, wout, g2)


def _ffn_up(h2, wg, wu):
    t = h2.shape[0]
    tm = 256

    def body(h_ref, wg_ref, wu_ref, gt_ref, up_ref, act_ref):
        h = h_ref[...]
        gt = _dot_nt(h, wg_ref[...])
        up = _dot_nt(h, wu_ref[...])
        gt_ref[...] = gt
        up_ref[...] = up
        act_ref[...] = (gt * _sigmoid(gt) * up).astype(BF16)

    wide = pl.BlockSpec((tm, D_FF), lambda i: (i, 0))
    wsp = pl.BlockSpec((D_FF, D_MODEL), lambda i: (0, 0))
    return pl.pallas_call(
        body, name="ffn_up", grid=(t // tm,),
        in_specs=[pl.BlockSpec((tm, D_MODEL), lambda i: (i, 0)), wsp, wsp],
        out_specs=[wide, wide, wide],
        out_shape=[jax.ShapeDtypeStruct((t, D_FF), F32)] * 2 + [jax.ShapeDtypeStruct((t, D_FF), BF16)],
        compiler_params=_params(("parallel",)),
    )(h2, wg, wu)


def _ffn_down_loss(x1, act, wd, gf, tgt):
    t = x1.shape[0]
    tm = 256

    def body(x1_ref, a_ref, w_ref, g_ref, t_ref, dx_ref, dxb_ref, loss_ref, dg_ref):
        first = pl.program_id(0) == 0
        x2 = x1_ref[...] + _dot(a_ref[...], w_ref[...])
        g = g_ref[...]
        diff = _rms_fwd(x2, g) - t_ref[...]
        lrow = 0.5 * jnp.sum(_colsum8(diff * diff), axis=1, keepdims=True) * (1.0 / D_MODEL)
        _acc(loss_ref, jnp.broadcast_to(lrow, (8, 128)), first)
        dx2, dgr = _rms_bwd(diff * (1.0 / D_MODEL), x2, g)
        dx_ref[...] = dx2
        dxb_ref[...] = dx2.astype(BF16)
        _acc(dg_ref, _colsum8(dgr), first)

    row = pl.BlockSpec((tm, D_MODEL), lambda i: (i, 0))
    return pl.pallas_call(
        body, name="ffn_down_loss", grid=(t // tm,),
        in_specs=[row, pl.BlockSpec((tm, D_FF), lambda i: (i, 0)), pl.BlockSpec((D_FF, D_MODEL), lambda i: (0, 0)),
                  pl.BlockSpec((1, D_MODEL), lambda i: (0, 0)), row],
        out_specs=[row, row, pl.BlockSpec((8, 128), lambda i: (0, 0)), pl.BlockSpec((8, D_MODEL), lambda i: (0, 0))],
        out_shape=[jax.ShapeDtypeStruct((t, D_MODEL), F32), jax.ShapeDtypeStruct((t, D_MODEL), BF16),
                   jax.ShapeDtypeStruct((8, 128), F32), jax.ShapeDtypeStruct((8, D_MODEL), F32)],
        compiler_params=_params(("arbitrary",)),
    )(x1, act, wd, gf, tgt)


def _ffn_bwd_act(dx2b, wd, gt, up):
    t = dx2b.shape[0]
    tm = 256

    def body(dx_ref, w_ref, gt_ref, up_ref, dgt_ref, dup_ref):
        dact = _dot_nt(dx_ref[...], w_ref[...])
        gt = gt_ref[...]
        sg = _sigmoid(gt)
        dgt_ref[...] = (dact * up_ref[...] * sg * (1.0 + gt * (1.0 - sg))).astype(BF16)
        dup_ref[...] = (dact * gt * sg).astype(BF16)

    wide = pl.BlockSpec((tm, D_FF), lambda i: (i, 0))
    return pl.pallas_call(
        body, name="ffn_bwd_act", grid=(t // tm,),
        in_specs=[pl.BlockSpec((tm, D_MODEL), lambda i: (i, 0)), pl.BlockSpec((D_FF, D_MODEL), lambda i: (0, 0)), wide, wide],
        out_specs=[wide, wide],
        out_shape=[jax.ShapeDtypeStruct((t, D_FF), BF16)] * 2,
        compiler_params=_params(("parallel",)),
    )(dx2b, wd, gt, up)


def _ffn_bwd_h(dgt, dup, wg, wu, dx2, x1, g2, wout):
    t = dgt.shape[0]
    tm = 256

    def body(dgt_ref, dup_ref, wg_ref, wu_ref, dx2_ref, x1_ref, g_ref, wo_ref, dx1_ref, dx1b_ref, dya_ref, dyb_ref, dg_ref):
        dh = _dot(dgt_ref[...], wg_ref[...]) + _dot(dup_ref[...], wu_ref[...])
        dxn, dgr = _rms_bwd(dh, x1_ref[...], g_ref[...])
        dx1 = dx2_ref[...] + dxn
        dx1_ref[...] = dx1
        dx1b = dx1.astype(BF16)
        dx1b_ref[...] = dx1b
        dy = _dot_nt(dx1b, wo_ref[...])
        dya_ref[...] = dy[:, :RW]
        dyb_ref[...] = dy[:, RW:]
        _acc(dg_ref, _colsum8(dgr), pl.program_id(0) == 0)

    wide = pl.BlockSpec((tm, D_FF), lambda i: (i, 0))
    row = pl.BlockSpec((tm, D_MODEL), lambda i: (i, 0))
    half = pl.BlockSpec((tm, RW), lambda i: (i, 0))
    wsp = pl.BlockSpec((D_FF, D_MODEL), lambda i: (0, 0))
    return pl.pallas_call(
        body, name="ffn_bwd_h", grid=(t // tm,),
        in_specs=[wide, wide, wsp, wsp, row, row, pl.BlockSpec((1, D_MODEL), lambda i: (0, 0)),
                  pl.BlockSpec((D_MODEL, D_MODEL), lambda i: (0, 0))],
        out_specs=[row, row, half, half, pl.BlockSpec((8, D_MODEL), lambda i: (0, 0))],
        out_shape=[jax.ShapeDtypeStruct((t, D_MODEL), F32), jax.ShapeDtypeStruct((t, D_MODEL), BF16),
                   jax.ShapeDtypeStruct((t, RW), F32), jax.ShapeDtypeStruct((t, RW), F32),
                   jax.ShapeDtypeStruct((8, D_MODEL), F32)],
        compiler_params=_params(("arbitrary",)),
    )(dgt, dup, wg, wu, dx2, x1, g2, wout)


def _wgrad(a, b, tk, tn, name):
    t, kdim = a.shape
    ndim = b.shape[1]

    def body(a_ref, b_ref, o_ref):
        o_ref[...] = _dot_tn(a_ref[...], b_ref[...])

    return pl.pallas_call(
        body, name=name, grid=(kdim // tk, ndim // tn),
        in_specs=[pl.BlockSpec((t, tk), lambda i, j: (0, i)), pl.BlockSpec((t, tn), lambda i, j: (0, j))],
        out_specs=pl.BlockSpec((tk, tn), lambda i, j: (i, j)),
        out_shape=jax.ShapeDtypeStruct((kdim, ndim), F32),
        compiler_params=_params(("parallel", "parallel")),
    )(a, b)


def _post_bwd(dya, y, r, k2, v, g, lnw, lnb, rk):
    t = y.shape[0]
    tm = _POST_TM

    def body(d_ref, y_ref, r_ref, k_ref, v_ref, g_ref, lnw_ref, lnb_ref, rk_ref,
             dy_ref, dr_ref, dk_ref, dv_ref, dg_ref, dlnw_ref, dlnb_ref, drk_ref):
        first = pl.program_id(0) == 0
        ones = jnp.ones((tm, 1), F32)
        prim = (y_ref[...], r_ref[...], k_ref[...], v_ref[...], g_ref[...],
                ones * lnw_ref[...], ones * lnb_ref[...], ones * rk_ref[...])
        _, vjp = jax.vjp(_post_fn, *prim)
        dy, dr, dk, dv, dg, dlnw, dlnb, drk = vjp(d_ref[...])
        dy_ref[...] = dy
        dr_ref[...] = dr
        dk_ref[...] = dk
        dv_ref[...] = dv
        dg_ref[...] = dg
        _acc(dlnw_ref, _colsum8(dlnw), first)
        _acc(dlnb_ref, _colsum8(dlnb), first)
        _acc(drk_ref, _colsum8(drk), first)

    row = pl.BlockSpec((tm, RW), lambda i: (i, 0))
    vec = pl.BlockSpec((1, RW), lambda i: (0, 0))
    part = pl.BlockSpec((8, RW), lambda i: (0, 0))
    return pl.pallas_call(
        body, name="rwkv_post_bwd", grid=(t // tm,),
        in_specs=[row] * 6 + [vec] * 3, out_specs=[row] * 5 + [part] * 3,
        out_shape=[jax.ShapeDtypeStruct((t, RW), F32)] * 5 + [jax.ShapeDtypeStruct((8, RW), F32)] * 3,
        compiler_params=_params(("arbitrary",)),
    )(dya, y, r, k2, v, g, lnw, lnb, rk)


def _wkv_bwd(dy, s0s, r, lw, k2, v, kk, a):
    t = r.shape[0]
    nc = t // CHUNK

    def body(dy_ref, s_ref, r_ref, lw_ref, k_ref, v_ref, kk_ref, a_ref,
             dr_ref, dlw_ref, dk_ref, dv_ref, dkk_ref, da_ref, ds):
        @pl.when(pl.program_id(0) == 0)
        def _():
            ds[...] = jnp.zeros_like(ds)

        _, vjp = jax.vjp(_wkv_chunk_fn, s_ref[0],
                         *[_pairs(ref) for ref in (r_ref, lw_ref, k_ref, v_ref, kk_ref, a_ref)])
        res = vjp((_pairs(dy_ref), ds[...]))
        ds[...] = res[0]
        for ref, val in zip((dr_ref, dlw_ref, dk_ref, dv_ref, dkk_ref, da_ref), res[1:]):
            for p in range(N_PAIR):
                ref[:, 128 * p:128 * (p + 1)] = val[p]

    blk = pl.BlockSpec((CHUNK, RW), lambda c: (nc - 1 - c, 0))
    return pl.pallas_call(
        body, name="wkv_bwd", grid=(nc,),
        in_specs=[blk, pl.BlockSpec((1, N_PAIR, 128, 128), lambda c: (nc - 1 - c, 0, 0, 0))] + [blk] * 6,
        out_specs=[blk] * 6,
        out_shape=[jax.ShapeDtypeStruct((t, RW), F32)] * 6,
        scratch_shapes=[pltpu.VMEM((N_PAIR, 128, 128), F32)],
        compiler_params=_params(("arbitrary",)),
    )(dy, s0s, r, lw, k2, v, kk, a)


def _prep_bwd(proj, pw, douts):
    t = proj.shape[0]
    tm = _PREP_TM
    nt = t // tm

    def body(p_ref, l8_ref, mu, w0, w2p, a0, a2p, g2, k_k, k_a, dr, dr2, dlw, dk2, dk22, dv, dv2, dkk, da, dg,
             dp_ref, dmu_ref, dw0_ref, dw2_ref, da0_ref, da2_ref, dg2_ref, dkk_ref, dka_ref, carry):
        i = pl.program_id(0)
        first = i == 0

        @pl.when(first)
        def _():
            carry[...] = jnp.zeros_like(carry)

        p = p_ref[...]
        pprev = _shifted(p, l8_ref[...], i == nt - 1)
        ones = jnp.ones((tm, 1), F32)
        prim = (p, pprev, ones * mu[...], ones * w0[...], w2p[...], ones * a0[...], a2p[...], g2[...],
                ones * k_k[...], ones * k_a[...])
        _, vjp = jax.vjp(_prep_fn, *prim)
        dp, dpp, dmu, dw0, dw2, da0, da2, dg2, dkk_, dka = vjp(
            (dr[...] + dr2[...], dlw[...], dk2[...] + dk22[...], dv[...] + dv2[...], dkk[...], da[...], dg[...]))
        up = pltpu.roll(dpp, tm - 1, axis=0)
        rid = lax.broadcasted_iota(jnp.int32, dpp.shape, 0)
        dp_ref[...] = dp + jnp.where(rid == tm - 1, carry[0:1, :], up)
        carry[...] = jnp.broadcast_to(dpp[0:1, :], carry.shape)
        _acc(dmu_ref, _colsum8(dmu), first)
        _acc(dw0_ref, _colsum8(dw0), first)
        _acc(dw2_ref, dw2, first)
        _acc(da0_ref, _colsum8(da0), first)
        _acc(da2_ref, da2, first)
        _acc(dg2_ref, dg2, first)
        _acc(dkk_ref, _colsum8(dkk_), first)
        _acc(dka_ref, _colsum8(dka), first)

    rev = lambda i: (nt - 1 - i, 0)
    row = pl.BlockSpec((tm, RW), rev)
    part = lambda n: pl.BlockSpec((8, n), lambda i: (0, 0))
    mat = pl.BlockSpec((128, RW), lambda i: (0, 0))
    return pl.pallas_call(
        body, name="rwkv_prep_bwd", grid=(nt,),
        in_specs=[pl.BlockSpec((tm, SHIFT_COLS), rev),
                  pl.BlockSpec((8, SHIFT_COLS), lambda i: (jnp.maximum((nt - 1 - i) * (tm // 8) - 1, 0), 0))]
                 + _prep_specs(tm) + [row] * 10,
        out_specs=[pl.BlockSpec((tm, SHIFT_COLS), rev), part(SHIFT_COLS), part(RW), mat, part(RW), mat, mat,
                   part(RW), part(RW)],
        out_shape=[jax.ShapeDtypeStruct((t, SHIFT_COLS), F32), jax.ShapeDtypeStruct((8, SHIFT_COLS), F32),
                   jax.ShapeDtypeStruct((8, RW), F32), jax.ShapeDtypeStruct((128, RW), F32),
                   jax.ShapeDtypeStruct((8, RW), F32), jax.ShapeDtypeStruct((128, RW), F32),
                   jax.ShapeDtypeStruct((128, RW), F32), jax.ShapeDtypeStruct((8, RW), F32),
                   jax.ShapeDtypeStruct((8, RW), F32)],
        scratch_shapes=[pltpu.VMEM((8, SHIFT_COLS), F32)],
        compiler_params=_params(("arbitrary",)),
    )(proj, proj, *pw, *douts)


def _combine_bwd(dyb, o, l, og):
    t = dyb.shape[0]
    tm = _COMB_TM

    def body(d_ref, o_ref, l_ref, og_ref, do_ref, dl_ref, dog_ref):
        ones = jnp.ones((tm, 1), F32)
        dog = []
        for p in range(N_PAIR):
            cols = slice(128 * p, 128 * (p + 1))
            _, vjp = jax.vjp(_combine_fn, o_ref[0, p], o_ref[1, p], o_ref[2, p], l_ref[0, p], l_ref[1, p], l_ref[2, p],
                             ones * og_ref[:, cols])
            res = vjp(d_ref[:, cols])
            for b in range(3):
                do_ref[b, p] = res[b]
                dl_ref[b, p] = res[3 + b]
            dog.append(_colsum8(res[6]))
        _acc(dog_ref, jnp.concatenate(dog, axis=1), pl.program_id(0) == 0)

    blk = pl.BlockSpec((3, N_PAIR, tm, 128), lambda i: (0, 0, i, 0))
    return pl.pallas_call(
        body, name="attn_combine_bwd", grid=(t // tm,),
        in_specs=[pl.BlockSpec((tm, RW), lambda i: (i, 0)), blk, blk, pl.BlockSpec((1, RW), lambda i: (0, 0))],
        out_specs=[blk, blk, pl.BlockSpec((8, RW), lambda i: (0, 0))],
        out_shape=[jax.ShapeDtypeStruct((3, N_PAIR, t, 128), F32)] * 2 + [jax.ShapeDtypeStruct((8, RW), F32)],
        compiler_params=_params(("arbitrary",)),
    )(dyb, o, l, og)


def _attn_bwd(do, dl, qkv):
    t = qkv.shape[2]

    def body(do_ref, dl_ref, q_ref, k_ref, v_ref, dq_ref, dk_ref, dv_ref):
        @pl.when(pl.program_id(1) == 0)
        def _():
            for ref in (dq_ref, dk_ref, dv_ref):
                ref[...] = jnp.zeros_like(ref)

        def unit(di, places, has_prev):
            cur = [_dilated_rows(d, r, n) for d, r, n in places]
            args = [_take(ref, (0,), cur) for ref in (q_ref, k_ref, v_ref)]
            if has_prev:
                prv = [_dilated_rows(d, r, n - 1) for d, r, n in places]
                args += [_take(ref, (0,), prv) for ref in (k_ref, v_ref)]
            _, vjp = jax.vjp(_attn_block_fn, *args)
            res = vjp((_take(do_ref, (0,), cur), _take(dl_ref, (0,), cur)))
            _put(dq_ref, (), cur, res[0], add=True)
            _put(dk_ref, (), cur, res[1], add=True)
            _put(dv_ref, (), cur, res[2], add=True)
            if has_prev:
                _put(dk_ref, (), prv, res[3], add=True)
                _put(dv_ref, (), prv, res[4], add=True)

        _for_each_sequence(t, unit)

    spec = lambda j: pl.BlockSpec((1, ATTN_GROUP, t, 128), lambda i, b: (j, i, 0, 0))
    branch = pl.BlockSpec((1, ATTN_GROUP, t, 128), lambda i, b: (b, i, 0, 0))
    out = pl.BlockSpec((ATTN_GROUP, t, 128), lambda i, b: (i, 0, 0))
    return pl.pallas_call(
        body, name="attn_bwd", grid=(N_PAIR // ATTN_GROUP, len(DILATIONS)),
        in_specs=[branch, branch, spec(0), spec(1), spec(2)], out_specs=[out] * 3,
        out_shape=[jax.ShapeDtypeStruct((N_PAIR, t, 128), F32)] * 3,
        compiler_params=_params(("parallel", "arbitrary")),
    )(do, dl, qkv, qkv, qkv)


def _in_proj_bwd(dpa, dq, dk, dv, win, x, g1, dx1):
    t = x.shape[0]
    tm = 256

    def body(dpa_ref, dq_ref, dk_ref, dv_ref, w_ref, x_ref, g_ref, dx1_ref, dproj_ref, dx_ref, dg_ref):
        parts = [dpa_ref[...]] + [ref[p] for ref in (dq_ref, dk_ref, dv_ref) for p in range(N_PAIR)]
        dproj = jnp.concatenate([z.astype(BF16) for z in parts], axis=1)
        dproj_ref[...] = dproj
        dh = _dot(dproj, w_ref[...])
        dxn, dgr = _rms_bwd(dh, x_ref[...], g_ref[...])
        dx_ref[...] = dx1_ref[...] + dxn
        _acc(dg_ref, _colsum8(dgr), pl.program_id(0) == 0)

    row = pl.BlockSpec((tm, D_MODEL), lambda i: (i, 0))
    pair = pl.BlockSpec((N_PAIR, tm, 128), lambda i: (0, i, 0))
    return pl.pallas_call(
        body, name="in_proj_bwd", grid=(t // tm,),
        in_specs=[pl.BlockSpec((tm, SHIFT_COLS), lambda i: (i, 0))] + [pair] * 3
                 + [pl.BlockSpec((IN_COLS, D_MODEL), lambda i: (0, 0)), row, pl.BlockSpec((1, D_MODEL), lambda i: (0, 0)), row],
        out_specs=[pl.BlockSpec((tm, IN_COLS), lambda i: (i, 0)), row, pl.BlockSpec((8, D_MODEL), lambda i: (0, 0))],
        out_shape=[jax.ShapeDtypeStruct((t, IN_COLS), BF16), jax.ShapeDtypeStruct((t, D_MODEL), F32),
                   jax.ShapeDtypeStruct((8, D_MODEL), F32)],
        compiler_params=_params(("arbitrary",)),
    )(dpa, dq, dk, dv, win, x, g1, dx1)


def _pad_lora(w, lo):
    z = jnp.zeros((64, RW), F32)
    return jnp.concatenate([w, z], axis=0) if lo == 0 else jnp.concatenate([z, w], axis=0)


def _local_step(x, tgt, win, vecs, w2, a2, g2m, get_rest, send_rest):
    pw = (vecs["mu_shift"], vecs["decay_w0"], _pad_lora(w2, 0), vecs["iclr_a0"], _pad_lora(a2, 64), g2m,
          vecs["k_k"], vecs["k_a"])
    h, proj, qkv = _in_proj(x, vecs["mix_norm_g"], win)
    r, lw, k2, v, kk, a, g = _prep_fwd(proj, pw)
    y, s0s = _wkv_fwd(r, lw, k2, v, kk, a)
    ya = _post_fwd(y, r, k2, v, g, vecs["ln_x_w"], vecs["ln_x_b"], vecs["r_k"])
    o_att, l_att = _attn_fwd(qkv)
    yb = _combine_fwd(o_att, l_att, vecs["attn_out_g"])

    wout, wg, wu, wd = get_rest(yb)
    ycat = jnp.concatenate([ya, yb], axis=1)
    x1, h2 = _out_proj(x, ycat, wout, vecs["ffn_norm_g"])
    gt, up, act = _ffn_up(h2, wg, wu)
    dx2, dx2b, loss8, dgf = _ffn_down_loss(x1, act, wd, vecs["final_norm_g"], tgt)

    dgt, dup = _ffn_bwd_act(dx2b, wd, gt, up)
    dx1, dx1b, dya, dyb, dg2n = _ffn_bwd_h(dgt, dup, wg, wu, dx2, x1, vecs["ffn_norm_g"], wout)
    gw = {
        "w_down": _wgrad(act, dx2b, 1408, 1024, "wgrad_down"),
        "w_gate": _wgrad(dgt, h2, 1408, 1024, "wgrad_gate"),
        "w_up": _wgrad(dup, h2, 1408, 1024, "wgrad_up"),
        "w_out": _wgrad(ycat, dx1b, 1024, 1024, "wgrad_out"),
    }

    lnw = vecs["ln_x_w"] + send_rest(gw)[0, 0]
    dy, dr_p, dk2_p, dv_p, dg, dlnw, dlnb, drk = _post_bwd(dya, y, r, k2, v, g, lnw, vecs["ln_x_b"], vecs["r_k"])
    dr_s, dlw, dk2_s, dv_s, dkk, da = _wkv_bwd(dy, s0s, r, lw, k2, v, kk, a)
    dpa, dmu, dw0, dw2p, da0, da2p, dg2m, dk_k, dk_a = _prep_bwd(
        proj, pw, (dr_p, dr_s, dlw, dk2_p, dk2_s, dv_p, dv_s, dkk, da, dg))

    do_att, dl_att, dog = _combine_bwd(dyb, o_att, l_att, vecs["attn_out_g"])
    dq, dk, dv = _attn_bwd(do_att, dl_att, qkv)
    dproj, dx, dg1 = _in_proj_bwd(dpa, dq, dk, dv, win, x, vecs["mix_norm_g"], dx1)
    gw["w_in"] = _wgrad(dproj, h, 1664, 1024, "wgrad_in")
    gw["decay_w2"] = dw2p[:64]
    gw["iclr_a2"] = da2p[64:]
    gw["gate_g2"] = dg2m
    gv = {"mix_norm_g": dg1, "mu_shift": dmu, "decay_w0": dw0, "iclr_a0": da0, "k_k": dk_k, "k_a": dk_a, "r_k": drk,
          "ln_x_w": dlnw, "ln_x_b": dlnb, "attn_out_g": dog, "ffn_norm_g": dg2n, "final_norm_g": dgf}
    return loss8, dx, gw, gv


N_CHIP = 4
N_DEV = 8
MATS = ("w_in", "w_out", "w_gate", "w_up", "w_down")
LORAS = ("decay_w2", "iclr_a2", "gate_g2")
VECS = (("mix_norm_g", 1024), ("mu_shift", 1792), ("decay_w0", 512), ("iclr_a0", 512), ("k_k", 512), ("k_a", 512),
        ("r_k", 512), ("ln_x_w", 512), ("ln_x_b", 512), ("attn_out_g", 512), ("ffn_norm_g", 1024),
        ("final_norm_g", 1024))
N_VEC = sum(n for _, n in VECS)
N_SMALL = N_VEC + 128
ROWS_PAD = 3328
ANY = pl.BlockSpec(memory_space=pl.ANY)


def _flip(v, f):
    return 1 - v if f else v


class _Me:
    def __init__(self, mode):
        x, y, c = lax.axis_index("x"), lax.axis_index("y"), lax.axis_index("c")
        self.core, self.chip, self.dev = c, 2 * x + y, 4 * x + 2 * y + c
        self.sibling = (x, y, 1 - c)
        if mode == "chips":
            self.peers = [(px, py, c) for px, py in ((1 - x, y), (x, 1 - y), (1 - x, 1 - y))]
        else:
            self.peers = [(_flip(x, k & 4), _flip(y, k & 2), _flip(c, k & 1)) for k in range(1, N_DEV)]


def _half(core, rows):
    h = rows // 2
    return pl.ds(pl.multiple_of(core * h, h), h)


def _landing(a, kind):
    if kind == "gather":
        return (N_CHIP,) + a.shape
    if kind == "scatter":
        return (N_DEV, a.shape[1] // 2, a.shape[2])
    return (N_DEV,) + a.shape


def _peer_copy(srcs, dsts, kinds, send_sems, recv_sems, me, j, i, incoming):
    px, py, pc = me.peers[j]
    pchip, pdev = 2 * px + py, 4 * px + 2 * py + pc
    src, dst, kind = srcs[i], dsts[i], kinds[i]
    if kind == "gather":
        rows = _half(me.core, src.shape[0])
        src, dst = src.at[rows], dst.at[pchip if incoming else me.chip, rows]
    elif kind == "scatter":
        src, dst = src.at[pchip, _half(pc, src.shape[1])], dst.at[pdev if incoming else me.dev]
    else:
        dst = dst.at[pdev if incoming else me.dev]
    n = len(srcs)
    return pltpu.make_async_remote_copy(src_ref=src, dst_ref=dst, send_sem=send_sems.at[n * j + i],
                                        recv_sem=recv_sems.at[n * j + i], device_id=(px, py, pc), device_id_type=MESH)


def _mode(kinds):
    return "chips" if kinds[0] == "gather" else "devs"


def _npeer(kinds):
    return N_CHIP - 1 if kinds[0] == "gather" else N_DEV - 1


def _swap_blocking(arrs, lands, kinds, name):
    n = len(arrs)

    def body(*refs):
        srcs, dsts = refs[:n], refs[2 * n:3 * n]
        send_sems, recv_sems = refs[3 * n:]
        me = _Me(_mode(kinds))
        sends = [_peer_copy(srcs, dsts, kinds, send_sems, recv_sems, me, j, i, False)
                 for j in range(len(me.peers)) for i in range(n)]
        for cp in sends:
            cp.start()
        for j in range(len(me.peers)):
            for i in range(n):
                _peer_copy(srcs, dsts, kinds, send_sems, recv_sems, me, j, i, True).wait_recv()
        for cp in sends:
            cp.wait_send()

    ns = _npeer(kinds) * n
    return pl.pallas_call(
        body, name=name, in_specs=[ANY] * (2 * n), out_specs=[ANY] * n,
        out_shape=[jax.ShapeDtypeStruct(l.shape, l.dtype) for l in lands],
        input_output_aliases={n + i: i for i in range(n)},
        scratch_shapes=[pltpu.SemaphoreType.DMA((ns,)), pltpu.SemaphoreType.DMA((ns,))],
    )(*arrs, *lands)


def _swap_gathered(lands, name):
    n = len(lands)

    def body(*refs):
        dsts, send_sems, recv_sems = refs[n:2 * n], refs[2 * n], refs[2 * n + 1]
        me = _Me("chips")

        def copy(j, i, incoming):
            px, py, _ = me.peers[j]
            rows_out, rows_in = _half(me.core, dsts[i].shape[1]), _half(1 - me.core, dsts[i].shape[1])
            return pltpu.make_async_remote_copy(
                src_ref=dsts[i].at[2 * px + py, rows_out], dst_ref=dsts[i].at[2 * px + py, rows_in if incoming else rows_out],
                send_sem=send_sems.at[n * j + i], recv_sem=recv_sems.at[n * j + i], device_id=me.sibling, device_id_type=MESH)

        sends = [copy(j, i, False) for j in range(3) for i in range(n)]
        for cp in sends:
            cp.start()
        for j in range(3):
            for i in range(n):
                copy(j, i, True).wait_recv()
        for cp in sends:
            cp.wait_send()

    return pl.pallas_call(
        body, name=name, in_specs=[ANY] * n, out_specs=[ANY] * n,
        out_shape=[jax.ShapeDtypeStruct(l.shape, l.dtype) for l in lands],
        input_output_aliases={i: i for i in range(n)},
        scratch_shapes=[pltpu.SemaphoreType.DMA((3 * n,)), pltpu.SemaphoreType.DMA((3 * n,))],
    )(*lands)


def _join_halves(sums, name):
    n = len(sums)

    def body(*refs):
        dsts, send_sems, recv_sems = refs[n:2 * n], refs[2 * n], refs[2 * n + 1]
        me = _Me("chips")

        def copy(i, incoming):
            mine, other = _half(me.core, dsts[i].shape[0]), _half(1 - me.core, dsts[i].shape[0])
            return pltpu.make_async_remote_copy(src_ref=dsts[i].at[mine], dst_ref=dsts[i].at[other if incoming else mine],
                                                send_sem=send_sems.at[i], recv_sem=recv_sems.at[i],
                                                device_id=me.sibling, device_id_type=MESH)

        sends = [copy(i, False) for i in range(n)]
        for cp in sends:
            cp.start()
        for i in range(n):
            copy(i, True).wait_recv()
        for cp in sends:
            cp.wait_send()

    return pl.pallas_call(
        body, name=name, in_specs=[ANY] * n, out_specs=[ANY] * n,
        out_shape=[jax.ShapeDtypeStruct(s.shape, s.dtype) for s in sums],
        input_output_aliases={i: i for i in range(n)},
        scratch_shapes=[pltpu.SemaphoreType.DMA((n,)), pltpu.SemaphoreType.DMA((n,))],
    )(*sums)


HBM = pl.BlockSpec(memory_space=pltpu.HBM)
SEM = pl.BlockSpec(memory_space=pltpu.SEMAPHORE)
EFFECT = pltpu.SideEffectType.DATAFLOW_SIDE_EFFECTING


def _swap_start(arrs, lands, kinds, name):
    n = len(arrs)

    def body(*refs):
        srcs, dsts, send_sems, recv_sems, token = refs[:n], refs[n:2 * n], refs[2 * n], refs[2 * n + 1], refs[-1]
        me = _Me(_mode(kinds))
        for j in range(len(me.peers)):
            for i in range(n):
                _peer_copy(srcs, dsts, kinds, send_sems, recv_sems, me, j, i, False).start()
        token[...] = jnp.zeros_like(token)

    ns = _npeer(kinds) * n
    outs = pl.pallas_call(
        body, name=name,
        out_shape=(pltpu.SemaphoreType.DMA((ns,)), pltpu.SemaphoreType.DMA((ns,)),
                   *[pltpu.HBM(a.shape, a.dtype) for a in arrs], *[pltpu.HBM(l.shape, l.dtype) for l in lands],
                   jax.ShapeDtypeStruct((8, 128), F32)),
        in_specs=[HBM] * (2 * n), out_specs=(SEM, SEM, *[HBM] * (2 * n), pl.BlockSpec(memory_space=pltpu.VMEM)),
        input_output_aliases={k: 2 + k for k in range(2 * n)},
        compiler_params=pltpu.CompilerParams(has_side_effects=EFFECT),
    )(*[pltpu.with_memory_space_constraint(a, pltpu.HBM) for a in arrs],
      *[pltpu.with_memory_space_constraint(l, pltpu.HBM) for l in lands])
    return outs[0], outs[1], outs[2:2 + n], outs[2 + n:2 + 2 * n], outs[-1]


def _swap_wait(send_sems, recv_sems, srcs_thru, lands_thru, after, kinds, name):
    n = len(srcs_thru)

    def body(*refs):
        srcs, dsts, s_sems, r_sems = refs[:n], refs[n:2 * n], refs[2 * n], refs[2 * n + 1]
        me = _Me(_mode(kinds))
        for j in range(len(me.peers)):
            for i in range(n):
                cp = _peer_copy(srcs, dsts, kinds, s_sems, r_sems, me, j, i, True)
                cp.wait_send()
                cp.wait_recv()

    outs = pl.pallas_call(
        body, name=name,
        out_shape=tuple(pltpu.HBM(a.shape, a.dtype) for a in (*srcs_thru, *lands_thru)),
        in_specs=[HBM] * (2 * n) + [SEM, SEM, ANY], out_specs=tuple([HBM] * (2 * n)),
        input_output_aliases={k: k for k in range(2 * n)},
        compiler_params=pltpu.CompilerParams(has_side_effects=EFFECT),
    )(*srcs_thru, *lands_thru, send_sems, recv_sems, after)
    return outs[n:]


def _adamw(w, g, m, v):
    m = ADAM_B1 * m + (1.0 - ADAM_B1) * g
    v = ADAM_B2 * v + (1.0 - ADAM_B2) * (g * g)
    m_hat = m / (1.0 - ADAM_B1 ** ADAM_STEP)
    v_hat = v / (1.0 - ADAM_B2 ** ADAM_STEP)
    delta = -ADAM_LR * (m_hat / (jnp.sqrt(v_hat) + ADAM_EPS) + ADAM_WD * w)
    return delta, m, v


def _reduce8(rbuf, core, tr, name):
    _, h, cols = rbuf.shape

    def body(core_ref, r_ref, g_ref):
        g = r_ref[0].astype(F32)
        for s in range(1, N_DEV):
            g = g + r_ref[s].astype(F32)
        g_ref[...] = g

    return pl.pallas_call(
        body, name=name,
        grid_spec=pltpu.PrefetchScalarGridSpec(
            num_scalar_prefetch=1, grid=(h // tr,),
            in_specs=[pl.BlockSpec((N_DEV, tr, cols), lambda i, core_ref: (0, i, 0))],
            out_specs=pl.BlockSpec((tr, cols), lambda i, core_ref: (core_ref[0] * (h // tr) + i, 0))),
        out_shape=jax.ShapeDtypeStruct((2 * h, cols), F32),
        compiler_params=_params(("parallel",)),
    )(core, rbuf)


def _adamw_call(g, w, m, v, tr, name):
    _, rows, cols = w.shape

    def body(g_in, w_ref, m_ref, v_ref, g_ref, d_ref, nm_ref, nv_ref):
        g = g_in[...]
        g_ref[0] = g
        d_ref[0], nm_ref[0], nv_ref[0] = _adamw(w_ref[0], g, m_ref[0], v_ref[0])

    row = pl.BlockSpec((1, tr, cols), lambda i: (0, i, 0))
    return pl.pallas_call(
        body, name=name, grid=(rows // tr,),
        in_specs=[pl.BlockSpec((tr, cols), lambda i: (i, 0)), row, row, row], out_specs=[row] * 4,
        out_shape=[jax.ShapeDtypeStruct(w.shape, F32)] * 4,
        compiler_params=_params(("parallel",)),
    )(g, w, m, v)


def _reduce_adamw_small(sbuf, w, m, v):
    def body(s_ref, w_ref, m_ref, v_ref, g_ref, d_ref, nm_ref, nv_ref, loss_ref):
        tot = s_ref[0]
        for s in range(1, N_DEV):
            tot = tot + s_ref[s]
        tot = jnp.sum(tot, axis=0, keepdims=True)
        g = tot[:, :N_VEC]
        g_ref[...] = g
        d_ref[...], nm_ref[...], nv_ref[...] = _adamw(w_ref[...], g, m_ref[...], v_ref[...])
        loss_ref[...] = tot[:, N_VEC:]

    return pl.pallas_call(
        body, name="reduce_adamw_small",
        out_shape=[jax.ShapeDtypeStruct((1, N_VEC), F32)] * 4 + [jax.ShapeDtypeStruct((1, 128), F32)],
    )(sbuf, w, m, v)


_TRANSPOSED = ("w_in", "w_gate", "w_up")
_ROW_STACKED = MATS
_ADAM_TILE = {"w_in": 208, "w_out": 256, "w_gate": 176, "w_up": 176, "w_down": 176, "decay_w2": 64, "iclr_a2": 64,
              "gate_g2": 128}
_SUM_TILE = {"w_in": 208, "w_out": 128, "w_gate": 176, "w_up": 176, "w_down": 176, "decay_w2": 32, "iclr_a2": 32,
             "gate_g2": 64}


def _full(n, stacked):
    p, r, c = stacked.shape
    if n in _ROW_STACKED:
        return stacked.reshape(p * r, c)
    return jnp.transpose(stacked, (1, 0, 2)).reshape(r, p * c)


def _by_chip(n, full):
    if n in _ROW_STACKED:
        return full.reshape(N_CHIP, full.shape[0] // N_CHIP, full.shape[1])
    r, c = full.shape
    return jnp.transpose(full.reshape(r, N_CHIP, c // N_CHIP), (1, 0, 2))


def _with_own(land_shape, dtype, own, slot):
    return lax.dynamic_update_slice(lax.empty(land_shape, dtype), own[None], (slot,) + (0,) * own.ndim)


def kernel(x, mix_norm_g, w_in, mu_shift, decay_w0, decay_w2, iclr_a0, iclr_a2, gate_g2, k_k, k_a, r_k, ln_x_w, ln_x_b, attn_out_g, w_out, ffn_norm_g, w_gate, w_up, w_down, final_norm_g, loss_target, m_mix_norm_g, m_w_in, m_mu_shift, m_decay_w0, m_decay_w2, m_iclr_a0, m_iclr_a2, m_gate_g2, m_k_k, m_k_a, m_r_k, m_ln_x_w, m_ln_x_b, m_attn_out_g, m_w_out, m_ffn_norm_g, m_w_gate, m_w_up, m_w_down, m_final_norm_g, v_mix_norm_g, v_w_in, v_mu_shift, v_decay_w0, v_decay_w2, v_iclr_a0, v_iclr_a2, v_gate_g2, v_k_k, v_k_a, v_r_k, v_ln_x_w, v_ln_x_b, v_attn_out_g, v_w_out, v_ffn_norm_g, v_w_gate, v_w_up, v_w_down, v_final_norm_g):
    names = ("mix_norm_g", "w_in", "mu_shift", "decay_w0", "decay_w2", "iclr_a0", "iclr_a2", "gate_g2", "k_k", "k_a",
             "r_k", "ln_x_w", "ln_x_b", "attn_out_g", "w_out", "ffn_norm_g", "w_gate", "w_up", "w_down", "final_norm_g")
    w = dict(zip(names, (mix_norm_g, w_in, mu_shift, decay_w0, decay_w2, iclr_a0, iclr_a2, gate_g2, k_k, k_a, r_k,
                         ln_x_w, ln_x_b, attn_out_g, w_out, ffn_norm_g, w_gate, w_up, w_down, final_norm_g)))
    m = dict(zip(names, (m_mix_norm_g, m_w_in, m_mu_shift, m_decay_w0, m_decay_w2, m_iclr_a0, m_iclr_a2, m_gate_g2,
                         m_k_k, m_k_a, m_r_k, m_ln_x_w, m_ln_x_b, m_attn_out_g, m_w_out, m_ffn_norm_g, m_w_gate,
                         m_w_up, m_w_down, m_final_norm_g)))
    v = dict(zip(names, (v_mix_norm_g, v_w_in, v_mu_shift, v_decay_w0, v_decay_w2, v_iclr_a0, v_iclr_a2, v_gate_g2,
                         v_k_k, v_k_a, v_r_k, v_ln_x_w, v_ln_x_b, v_attn_out_g, v_w_out, v_ffn_norm_g, v_w_gate,
                         v_w_up, v_w_down, v_final_norm_g)))
    first = ("w_in",) + LORAS
    rest = ("w_out", "w_gate", "w_up", "w_down")
    xi, yi, ci = lax.axis_index("x"), lax.axis_index("y"), lax.axis_index("c")
    my_chip, my_dev = 2 * xi + yi, 4 * xi + 2 * yi + ci
    gather, scatter = ("gather",) * 4, ("scatter",) * 4

    sh = lambda z, n: jnp.transpose(z[0]) if n in _TRANSPOSED else z[0]
    wb = {n: sh(w[n], n).astype(BF16) for n in MATS}
    lands = [_with_own((N_CHIP,) + wb[n].shape, BF16, wb[n], my_chip) for n in rest]
    ssem, rsem, srcs_thru, lands_thru, tok = _swap_start([wb[n] for n in rest], lands, gather, "gather_rest_start")
    mine = [wb["w_in"]] + [w[n][0] for n in LORAS]
    got = _swap_blocking(mine, [_with_own((N_CHIP,) + a.shape, a.dtype, a, my_chip) for a in mine], gather, "gather_first")
    win, w2, a2, g2m = (_full(n, z) for n, z in zip(first, _swap_gathered(got, "gather_first_halves")))

    vecs = {n: w[n].reshape(1, sz) for n, sz in VECS}
    vecs["mix_norm_g"] = vecs["mix_norm_g"] + tok[0, 0]

    def get_rest(after):
        halves = _swap_wait(ssem, rsem, srcs_thru, lands_thru, after, gather, "gather_rest_wait")
        return [_full(n, z) for n, z in zip(rest, _swap_gathered(halves, "gather_rest_halves"))]

    flight = []

    def my_half(g):
        h = g.shape[1] // 2
        return lax.dynamic_slice(g, (my_chip, ci * h, 0), (1, h, g.shape[2]))[0]

    def send_rest(gw):
        gs = [_by_chip(n, gw[n]).astype(BF16) for n in rest]
        into = [_with_own((N_DEV,) + my_half(g).shape, BF16, my_half(g), my_dev) for g in gs]
        flight.extend(_swap_start(gs, into, scatter, "exchange_rest_start"))
        return flight[4]

    loss8, dx, gw, gv = _local_step(x[0], loss_target[0], win, vecs, w2, a2, g2m, get_rest, send_rest)

    small = jnp.concatenate([gv[n] for n, _ in VECS] + [loss8], axis=1)
    gs = [_by_chip(n, gw[n]).astype(BF16) for n in first]
    into = [_with_own((N_DEV,) + my_half(g).shape, BF16, my_half(g), my_dev) for g in gs]
    into.append(_with_own((N_DEV,) + small.shape, F32, small, my_dev))
    last = _swap_start(gs + [small], into, scatter + ("all",), "exchange_first_start")

    core = jnp.reshape(ci, (1,)).astype(jnp.int32)

    def update(group, rbufs, tag):
        sums = [_reduce8(rb, core, _SUM_TILE[n], "reduce_" + n) for n, rb in zip(group, rbufs)]
        gsum = _join_halves(sums, "join_halves_" + tag)
        out = {}
        for n, g in zip(group, gsum):
            r = _adamw_call(g, sh(w[n], n)[None], sh(m[n], n)[None], sh(v[n], n)[None], _ADAM_TILE[n], "adamw_" + n)
            out[n] = [jnp.transpose(z[0])[None] for z in r] if n in _TRANSPOSED else r
        return out

    res = update(rest, _swap_wait(flight[0], flight[1], flight[2], flight[3], last[4], scatter, "exchange_rest_wait"), "rest")
    got = _swap_wait(last[0], last[1], last[2], last[3], res["w_down"][1], scatter + ("all",), "exchange_first_wait")
    res.update(update(first, got[:4], "first"))
    cat = lambda d: jnp.concatenate([d[n].reshape(1, sz) for n, sz in VECS], axis=1)
    small_res = _reduce_adamw_small(got[4], cat(w), cat(m), cat(v))

    outs = []
    for k in range(4):
        piece = {n: r[k] for n, r in res.items()}
        c0 = 0
        for n, sz in VECS:
            piece[n] = small_res[k][0, c0:c0 + sz].reshape(w[n].shape)
            c0 += sz
        outs.extend(piece[n] for n in names)
    return (small_res[4][0, 0], dx[None], *outs)
```

```python
import jax
import jax.numpy as jnp
from jax import lax
from jax.experimental import pallas as pl
from jax.experimental.pallas import tpu as pltpu

F32 = jnp.float32
BF16 = jnp.bfloat16
HI = lax.Precision.HIGHEST

D_MODEL = 1024
HEAD_DIM = 64
RW = 512
N_PAIR = RW // 128
SHIFT_COLS = 1792
IN_COLS = 3328
D_FF = 2816
NORM_EPS = 1e-6
GN_EPS = 64e-5
CHUNK = 64
SUB = 16
WKV_PASSES = 1
ATTN_PASSES = 1
ATTN_BLOCK = 128
DILATIONS = (1, 4, 16)
NEG = -1e30
ADAM_LR, ADAM_B1, ADAM_B2, ADAM_EPS, ADAM_WD, ADAM_STEP = 0.001, 0.9, 0.999, 1e-08, 0.01, 10
VMEM_LIMIT = 56 * 1024 * 1024
MESH = pl.DeviceIdType.MESH


def _params(sem=None, **kw):
    return pltpu.CompilerParams(dimension_semantics=sem, vmem_limit_bytes=VMEM_LIMIT, **kw)


def _dot(a, b, prec=None):
    return lax.dot_general(a, b, (((1,), (0,)), ((), ())), preferred_element_type=F32, precision=prec)


def _dot_nt(a, b, prec=None):
    return lax.dot_general(a, b, (((1,), (1,)), ((), ())), preferred_element_type=F32, precision=prec)


def _dot_tn(a, b, prec=None):
    return lax.dot_general(a, b, (((0,), (0,)), ((), ())), preferred_element_type=F32, precision=prec)


_FORMS = {"nn": ((1,), (0,)), "nt": ((1,), (1,)), "tn": ((0,), (0,))}


def _dg(a, b, form):
    if a.ndim == 3 or b.ndim == 3:
        nb = a.shape[0] if a.ndim == 3 else b.shape[0]
        return jnp.stack([_dg(a[i] if a.ndim == 3 else a, b[i] if b.ndim == 3 else b, form) for i in range(nb)], axis=0)
    return lax.dot_general(a, b, (_FORMS[form], ((), ())), preferred_element_type=F32)


def _split2(x):
    hi = x.astype(BF16)
    return hi, (x - hi.astype(F32)).astype(BF16)


def _split3(x):
    hi = x.astype(BF16)
    rest = x - hi.astype(F32)
    mid = rest.astype(BF16)
    return hi, mid, (rest - mid.astype(F32)).astype(BF16)


def _mm_raw(a, b, form, mode):
    if mode == 1:
        return _dg(a.astype(BF16), b.astype(BF16), form)
    if mode == 3:
        ah, al = _split2(a)
        bh, bl = _split2(b)
        return _dg(ah, bh, form) + (_dg(ah, bl, form) + _dg(al, bh, form))
    if mode == "L3":
        ab = a.astype(BF16)
        b1, b2, b3 = _split3(b)
        return _dg(ab, b1, form) + (_dg(ab, b2, form) + _dg(ab, b3, form))
    assert mode == "R3", mode
    bb = b.astype(BF16)
    a1, a2, a3 = _split3(a)
    return _dg(a1, bb, form) + (_dg(a2, bb, form) + _dg(a3, bb, form))


def _mm(a, b, form, mode):
    @jax.custom_vjp
    def f(a, b):
        return _mm_raw(a, b, form, mode)

    def fwd(a, b):
        return _mm_raw(a, b, form, mode), (a, b)

    def bwd(res, ct):
        a, b = res
        la = {1: 1, 3: 3, "L3": None, "R3": "R3"}[mode]
        lb = {1: 1, 3: 3, "L3": "L3", "R3": None}[mode]
        if form == "nn":
            da = None if la is None else _mm_raw(ct, b, "nt", la)
            db = None if lb is None else _mm_raw(a, ct, "tn", lb)
        elif form == "nt":
            da = None if la is None else _mm_raw(ct, b, "nn", la)
            db = None if lb is None else _mm_raw(ct, a, "tn", "R3" if lb == "L3" else lb)
        else:
            da = None if la is None else _mm_raw(b, ct, "nt", "L3" if la == "R3" else la)
            db = None if lb is None else _mm_raw(a, ct, "nn", lb)
        return (jnp.zeros_like(a) if da is None else da, jnp.zeros_like(b) if db is None else db)

    f.defvjp(fwd, bwd)
    return f(a, b)


def _seg_ones(n):
    r = lax.broadcasted_iota(jnp.int32, (n, n), 0) // HEAD_DIM
    c = lax.broadcasted_iota(jnp.int32, (n, n), 1) // HEAD_DIM
    return (r == c).astype(F32)


def _segsum(x, seg):
    return _mm(x, seg, "nn", "R3")


def _rms_fwd(x, g):
    rstd = lax.rsqrt(jnp.mean(x * x, axis=-1, keepdims=True) + NORM_EPS)
    return x * rstd * g


def _rms_bwd(dy, x, g):
    rstd = lax.rsqrt(jnp.mean(x * x, axis=-1, keepdims=True) + NORM_EPS)
    xn = x * rstd
    dxn = dy * g
    dx = rstd * (dxn - xn * jnp.mean(dxn * xn, axis=-1, keepdims=True))
    return dx, dy * xn


def _sigmoid(x):
    return 1.0 / (1.0 + jnp.exp(-x))


def _softplus(x):
    return jnp.maximum(x, 0.0) + jnp.log(1.0 + jnp.exp(-jnp.abs(x)))


def _acc(ref, val, first):
    @pl.when(first)
    def _():
        ref[...] = val

    @pl.when(jnp.logical_not(first))
    def _():
        ref[...] += val


def _colsum8(v):
    rows, n = v.shape
    return jnp.sum(v.reshape(rows // 8, 8, n), axis=0)


def _prep_fn(p, pprev, mu, w0, w2p, a0, a2p, g2, k_k, k_a):
    seg = _seg_ones(RW)
    ps = p + (pprev - p) * mu
    r = ps[:, 0:RW]
    k = ps[:, RW:2 * RW]
    v = ps[:, 2 * RW:3 * RW]
    xwa = ps[:, 3 * RW:3 * RW + 128]
    xg = ps[:, 3 * RW + 128:3 * RW + 256]
    wraw = -_softplus(-(w0 + _mm(jnp.tanh(xwa), w2p, "nn", 3))) - 0.5
    lw = -jnp.exp(wraw)
    a = _sigmoid(a0 + _mm(xwa, a2p, "nn", 3))
    g = _mm(_sigmoid(xg), g2, "nn", 3)
    kk = k * k_k
    kk = kk / jnp.maximum(jnp.sqrt(_segsum(kk * kk, seg)), 1e-12)
    k2 = k * (1.0 + (a - 1.0) * k_a)
    return r, lw, k2, v, kk, a, g


def _solve_unit_lower(lmat, rhs):
    c = lmat.shape[-1]
    row = lax.broadcasted_iota(jnp.int32, (c, c), 0)
    col = lax.broadcasted_iota(jnp.int32, (c, c), 1)
    eye = (row == col).astype(F32)
    ld = jnp.where(row // SUB == col // SUB, lmat, 0.0)
    lo = lmat - ld
    x = eye + ld
    m = ld
    mm = lambda p, q: _mm(p, q, "nn", WKV_PASSES)
    for _ in range(3):
        m = mm(m, m)
        x = x + mm(x, m)
    g = mm(x, lo)
    g2 = mm(g, g)
    w = mm(x, rhs)
    w = w + mm(g2, w)
    return w + mm(g, w)


def _wkv_chunk_fn(s0, r, lw, k, v, kk, a):
    c = r.shape[-2]
    n = 2 * c
    row = lax.broadcasted_iota(jnp.int32, (n, n), 0)
    col = lax.broadcasted_iota(jnp.int32, (n, n), 1)
    same = (row // c) == (col // c)
    incl = jnp.logical_and(row >= col, same)
    strict = jnp.logical_and(row > col, same)
    sel = (lax.broadcasted_iota(jnp.int32, (n, 128), 0) // c) == (lax.broadcasted_iota(jnp.int32, (n, 128), 1) // HEAD_DIM)
    two = lambda z: jnp.concatenate([z, z], axis=-2)
    lw2 = two(lw)
    mm = lambda p_, q_, form: _mm(p_, q_, form, WKV_PASSES)
    cl = _mm(incl.astype(F32), lw2, "nn", "L3")
    p = jnp.exp(cl)
    pinv = jnp.exp(-cl)
    pprev = jnp.exp(cl - lw2)
    kk2 = two(kk)
    at = jnp.where(sel, -kk2 * pprev, 0.0)
    bt = jnp.where(sel, kk2 * two(a) * pinv, 0.0)
    kt = jnp.where(sel, two(k) * pinv, 0.0)
    rt = jnp.where(sel, two(r) * p, 0.0)
    vt = jnp.where(sel, two(v), 0.0)
    ab = jnp.where(strict, mm(at, bt, "nt"), 0.0)
    ak = jnp.where(strict, mm(at, kt, "nt"), 0.0)
    rb = jnp.where(incl, mm(rt, bt, "nt"), 0.0)
    rk = jnp.where(incl, mm(rt, kt, "nt"), 0.0)
    u = _solve_unit_lower(ab, mm(at, s0, "nt") + mm(ak, vt, "nn"))
    y2 = mm(rt, s0, "nt") + mm(rb, u, "nn") + mm(rk, vt, "nn")
    plast = jnp.exp(jnp.sum(lw, axis=-2, keepdims=True))
    s1 = (s0 + mm(u, bt, "tn") + mm(vt, kt, "tn")) * plast
    r2 = lax.broadcasted_iota(jnp.int32, (128, 128), 0) // HEAD_DIM
    c2 = lax.broadcasted_iota(jnp.int32, (128, 128), 1) // HEAD_DIM
    return y2[..., :c, :] + y2[..., c:, :], jnp.where(r2 == c2, s1, 0.0)


def _post_fn(y, r, k2, v, g, lnw, lnb, rk):
    seg = _seg_ones(RW)
    mean = _segsum(y, seg) * (1.0 / HEAD_DIM)
    yc = y - mean
    var = _segsum(yc * yc, seg) * (1.0 / HEAD_DIM)
    yn = yc * lax.rsqrt(var + GN_EPS)
    out = yn * lnw + lnb + _segsum(r * k2 * rk, seg) * v
    return out * g


def _attn_block_fn(q, kc, vc, kp=None, vp=None):
    n = ATTN_BLOCK
    qi = lax.broadcasted_iota(jnp.int32, (n, n), 0)
    kj = lax.broadcasted_iota(jnp.int32, (n, n), 1)
    lane = lax.broadcasted_iota(jnp.int32, (1, 128), 1)
    scale = HEAD_DIM ** -0.5
    os_, ls_ = [], []
    for h in range(2):
        mh = (lane // HEAD_DIM) == h
        qh = jnp.where(mh, q, 0.0)
        sc = jnp.where(kj <= qi, _mm(qh, kc, "nt", ATTN_PASSES) * scale, NEG)
        m = jnp.max(sc, axis=-1, keepdims=True)
        if kp is not None:
            sp = jnp.where(kj >= qi, _mm(qh, kp, "nt", ATTN_PASSES) * scale, NEG)
            m = jnp.maximum(m, jnp.max(sp, axis=-1, keepdims=True))
        pc = jnp.exp(sc - m)
        den = jnp.sum(pc, axis=-1, keepdims=True)
        num = _mm(pc, vc, "nn", ATTN_PASSES)
        if kp is not None:
            pp = jnp.exp(sp - m)
            den = den + jnp.sum(pp, axis=-1, keepdims=True)
            num = num + _mm(pp, vp, "nn", ATTN_PASSES)
        os_.append(num / den)
        ls_.append(m + jnp.log(den))
    m0 = (lane // HEAD_DIM) == 0
    return jnp.where(m0, os_[0], os_[1]), jnp.where(m0, ls_[0], ls_[1])


def _combine_fn(o1, o2, o3, l1, l2, l3, og):
    seg = _seg_ones(o1.shape[-1])
    m = jnp.maximum(jnp.maximum(l1, l2), l3)
    e1, e2, e3 = jnp.exp(l1 - m), jnp.exp(l2 - m), jnp.exp(l3 - m)
    o = (e1 * o1 + e2 * o2 + e3 * o3) / (e1 + e2 + e3)
    o = o * lax.rsqrt(_segsum(o * o, seg) * (1.0 / HEAD_DIM) + NORM_EPS)
    return o * og


def _in_proj(x, g1, win):
    t = x.shape[0]
    tm = 256

    def body(x_ref, g_ref, w_ref, h_ref, pa_ref, qkv_ref):
        h = _rms_fwd(x_ref[...], g_ref[...]).astype(BF16)
        h_ref[...] = h
        proj = _dot_nt(h, w_ref[...])
        pa_ref[...] = proj[:, :SHIFT_COLS]
        for j in range(3):
            for p in range(N_PAIR):
                c0 = SHIFT_COLS + j * RW + p * 128
                qkv_ref[j, p] = proj[:, c0:c0 + 128]

    return pl.pallas_call(
        body, name="in_proj", grid=(t // tm,),
        in_specs=[pl.BlockSpec((tm, D_MODEL), lambda i: (i, 0)), pl.BlockSpec((1, D_MODEL), lambda i: (0, 0)),
                  pl.BlockSpec((IN_COLS, D_MODEL), lambda i: (0, 0))],
        out_specs=[pl.BlockSpec((tm, D_MODEL), lambda i: (i, 0)), pl.BlockSpec((tm, SHIFT_COLS), lambda i: (i, 0)),
                   pl.BlockSpec((3, N_PAIR, tm, 128), lambda i: (0, 0, i, 0))],
        out_shape=[jax.ShapeDtypeStruct((t, D_MODEL), BF16), jax.ShapeDtypeStruct((t, SHIFT_COLS), F32),
                   jax.ShapeDtypeStruct((3, N_PAIR, t, 128), F32)],
        compiler_params=_params(("parallel",)),
    )(x, g1, win)


def _shifted(p, last8, first):
    prow = jnp.where(first, 0.0, last8[7:8, :])
    rolled = pltpu.roll(p, 1, axis=0)
    rid = lax.broadcasted_iota(jnp.int32, p.shape, 0)
    return jnp.where(rid == 0, prow, rolled)


_PREP_TM = 256


def _prep_specs(tm):
    vec = lambda n: pl.BlockSpec((1, n), lambda i: (0, 0))
    mat = lambda r, n: pl.BlockSpec((r, n), lambda i: (0, 0))
    return [vec(SHIFT_COLS), vec(RW), mat(128, RW), vec(RW), mat(128, RW), mat(128, RW), vec(RW), vec(RW)]


def _prep_fwd(proj, pw):
    t = proj.shape[0]
    tm = _PREP_TM

    def body(p_ref, l8_ref, mu, w0, w2p, a0, a2p, g2, k_k, k_a, *outs):
        p = p_ref[...]
        pprev = _shifted(p, l8_ref[...], pl.program_id(0) == 0)
        res = _prep_fn(p, pprev, mu[...], w0[...], w2p[...], a0[...], a2p[...], g2[...], k_k[...], k_a[...])
        for o_ref, val in zip(outs, res):
            o_ref[...] = val

    row = pl.BlockSpec((tm, RW), lambda i: (i, 0))
    return pl.pallas_call(
        body, name="rwkv_prep", grid=(t // tm,),
        in_specs=[pl.BlockSpec((tm, SHIFT_COLS), lambda i: (i, 0)),
                  pl.BlockSpec((8, SHIFT_COLS), lambda i: (jnp.maximum(i * (tm // 8) - 1, 0), 0))] + _prep_specs(tm),
        out_specs=[row] * 7,
        out_shape=[jax.ShapeDtypeStruct((t, RW), F32)] * 7,
        compiler_params=_params(("parallel",)),
    )(proj, proj, *pw)


def _pairs(ref):
    return jnp.stack([ref[:, 128 * p:128 * (p + 1)] for p in range(N_PAIR)], axis=0)


def _wkv_fwd(r, lw, k2, v, kk, a):
    t = r.shape[0]
    nc = t // CHUNK

    def body(r_ref, lw_ref, k_ref, v_ref, kk_ref, a_ref, y_ref, s_ref, st):
        @pl.when(pl.program_id(0) == 0)
        def _():
            st[...] = jnp.zeros_like(st)

        s0 = st[...]
        s_ref[0] = s0
        y, s1 = _wkv_chunk_fn(s0, *[_pairs(ref) for ref in (r_ref, lw_ref, k_ref, v_ref, kk_ref, a_ref)])
        for p in range(N_PAIR):
            y_ref[:, 128 * p:128 * (p + 1)] = y[p]
        st[...] = s1

    blk = pl.BlockSpec((CHUNK, RW), lambda c: (c, 0))
    return pl.pallas_call(
        body, name="wkv_fwd", grid=(nc,),
        in_specs=[blk] * 6,
        out_specs=[blk, pl.BlockSpec((1, N_PAIR, 128, 128), lambda c: (c, 0, 0, 0))],
        out_shape=[jax.ShapeDtypeStruct((t, RW), F32), jax.ShapeDtypeStruct((nc, N_PAIR, 128, 128), F32)],
        scratch_shapes=[pltpu.VMEM((N_PAIR, 128, 128), F32)],
        compiler_params=_params(("arbitrary",)),
    )(r, lw, k2, v, kk, a)


_POST_TM = 256


def _post_fwd(y, r, k2, v, g, lnw, lnb, rk):
    t = y.shape[0]
    tm = _POST_TM

    def body(y_ref, r_ref, k_ref, v_ref, g_ref, lnw_ref, lnb_ref, rk_ref, o_ref):
        o_ref[...] = _post_fn(y_ref[...], r_ref[...], k_ref[...], v_ref[...], g_ref[...],
                              lnw_ref[...], lnb_ref[...], rk_ref[...]).astype(BF16)

    row = pl.BlockSpec((tm, RW), lambda i: (i, 0))
    vec = pl.BlockSpec((1, RW), lambda i: (0, 0))
    return pl.pallas_call(
        body, name="rwkv_post", grid=(t // tm,),
        in_specs=[row] * 5 + [vec] * 3, out_specs=row,
        out_shape=jax.ShapeDtypeStruct((t, RW), BF16),
        compiler_params=_params(("parallel",)),
    )(y, r, k2, v, g, lnw, lnb, rk)


ATTN_GROUP = 2


def _dilated_rows(d, r, n):
    if d == 1:
        return pl.ds(pl.multiple_of(n * ATTN_BLOCK, ATTN_BLOCK), ATTN_BLOCK)
    return pl.ds(r + n * (ATTN_BLOCK * d), ATTN_BLOCK, stride=d)


def _for_each_sequence(t, unit):
    for di, d in enumerate(DILATIONS):

        @pl.when(pl.program_id(1) == di)
        def _(di=di, d=d):
            nb = t // (ATTN_BLOCK * d)
            if d == 1:
                unit(di, [(d, 0, 0)], False)
                unit(di, [(d, 0, 1)], True)
                lax.fori_loop(1, nb // 2, lambda k, c: (unit(di, [(d, 0, 2 * k), (d, 0, 2 * k + 1)], True), c)[1], 0)
            else:

                def residues(r, carry):
                    unit(di, [(d, r, 0), (d, r + d // 2, 0)], False)
                    if nb > 1:
                        lax.fori_loop(1, nb, lambda n, c: (unit(di, [(d, r, n), (d, r + d // 2, n)], True), c)[1], 0)
                    return carry

                lax.fori_loop(0, d // 2, residues, 0)


def _take(ref, lead, rows_list):
    return jnp.stack([ref.at[(*lead, g)][rows, :] for rows in rows_list for g in range(ATTN_GROUP)], axis=0)


def _put(ref, lead, rows_list, val, add=False):
    k = 0
    for rows in rows_list:
        for g in range(ATTN_GROUP):
            if add:
                ref.at[(*lead, g)][rows, :] += val[k]
            else:
                ref.at[(*lead, g)][rows, :] = val[k]
            k += 1


def _attn_fwd(qkv):
    t = qkv.shape[2]

    def body(q_ref, k_ref, v_ref, o_ref, l_ref):
        def unit(di, places, has_prev):
            cur = [_dilated_rows(d, r, n) for d, r, n in places]
            args = [_take(ref, (0,), cur) for ref in (q_ref, k_ref, v_ref)]
            if has_prev:
                prv = [_dilated_rows(d, r, n - 1) for d, r, n in places]
                args += [_take(ref, (0,), prv) for ref in (k_ref, v_ref)]
            o, lse = _attn_block_fn(*args)
            _put(o_ref, (0,), cur, o)
            _put(l_ref, (0,), cur, lse)

        _for_each_sequence(t, unit)

    spec = lambda j: pl.BlockSpec((1, ATTN_GROUP, t, 128), lambda i, b: (j, i, 0, 0))
    out = pl.BlockSpec((1, ATTN_GROUP, t, 128), lambda i, b: (b, i, 0, 0))
    return pl.pallas_call(
        body, name="attn_fwd", grid=(N_PAIR // ATTN_GROUP, len(DILATIONS)),
        in_specs=[spec(0), spec(1), spec(2)], out_specs=[out, out],
        out_shape=[jax.ShapeDtypeStruct((3, N_PAIR, t, 128), F32)] * 2,
        compiler_params=_params(("parallel", "arbitrary")),
    )(qkv, qkv, qkv)


_COMB_TM = 256


def _combine_fwd(o, l, og):
    t = o.shape[2]
    tm = _COMB_TM

    def body(o_ref, l_ref, og_ref, y_ref):
        for p in range(N_PAIR):
            cols = slice(128 * p, 128 * (p + 1))
            y_ref[:, cols] = _combine_fn(o_ref[0, p], o_ref[1, p], o_ref[2, p], l_ref[0, p], l_ref[1, p], l_ref[2, p],
                                         og_ref[:, cols]).astype(BF16)

    blk = pl.BlockSpec((3, N_PAIR, tm, 128), lambda i: (0, 0, i, 0))
    return pl.pallas_call(
        body, name="attn_combine", grid=(t // tm,),
        in_specs=[blk, blk, pl.BlockSpec((1, RW), lambda i: (0, 0))], out_specs=pl.BlockSpec((tm, RW), lambda i: (i, 0)),
        out_shape=jax.ShapeDtypeStruct((t, RW), BF16),
        compiler_params=_params(("parallel",)),
    )(o, l, og)


def _out_proj(x, ycat, wout, g2):
    t = x.shape[0]
    tm = 256

    def body(x_ref, y_ref, w_ref, g_ref, x1_ref, h_ref):
        x1 = x_ref[...] + _dot(y_ref[...], w_ref[...])
        x1_ref[...] = x1
        h_ref[...] = _rms_fwd(x1, g_ref[...]).astype(BF16)

    row = pl.BlockSpec((tm, D_MODEL), lambda i: (i, 0))
    return pl.pallas_call(
        body, name="out_proj", grid=(t // tm,),
        in_specs=[row, row, pl.BlockSpec((D_MODEL, D_MODEL), lambda i: (0, 0)), pl.BlockSpec((1, D_MODEL), lambda i: (0, 0))],
        out_specs=[row, row],
        out_shape=[jax.ShapeDtypeStruct((t, D_MODEL), F32), jax.ShapeDtypeStruct((t, D_MODEL), BF16)],
        compiler_params=_params(("parallel",)),
    )(x, ycat, wout, g2)


def _ffn_up(h2, wg, wu):
    t = h2.shape[0]
    tm = 256

    def body(h_ref, wg_ref, wu_ref, gt_ref, up_ref, act_ref):
        h = h_ref[...]
        gt = _dot_nt(h, wg_ref[...])
        up = _dot_nt(h, wu_ref[...])
        gt_ref[...] = gt.astype(BF16)
        up_ref[...] = up.astype(BF16)
        act_ref[...] = (gt * _sigmoid(gt) * up).astype(BF16)

    wide = pl.BlockSpec((tm, D_FF), lambda i: (i, 0))
    wsp = pl.BlockSpec((D_FF, D_MODEL), lambda i: (0, 0))
    return pl.pallas_call(
        body, name="ffn_up", grid=(t // tm,),
        in_specs=[pl.BlockSpec((tm, D_MODEL), lambda i: (i, 0)), wsp, wsp],
        out_specs=[wide, wide, wide],
        out_shape=[jax.ShapeDtypeStruct((t, D_FF), BF16)] * 3,
        compiler_params=_params(("parallel",)),
    )(h2, wg, wu)


def _ffn_down_loss(x1, act, wd, gf, tgt):
    t = x1.shape[0]
    tm = 256

    def body(x1_ref, a_ref, w_ref, g_ref, t_ref, dx_ref, dxb_ref, loss_ref, dg_ref):
        first = pl.program_id(0) == 0
        x2 = x1_ref[...] + _dot(a_ref[...], w_ref[...])
        g = g_ref[...]
        diff = _rms_fwd(x2, g) - t_ref[...]
        lrow = 0.5 * jnp.sum(_colsum8(diff * diff), axis=1, keepdims=True) * (1.0 / D_MODEL)
        _acc(loss_ref, jnp.broadcast_to(lrow, (8, 128)), first)
        dx2, dgr = _rms_bwd(diff * (1.0 / D_MODEL), x2, g)
        dx_ref[...] = dx2
        dxb_ref[...] = dx2.astype(BF16)
        _acc(dg_ref, _colsum8(dgr), first)

    row = pl.BlockSpec((tm, D_MODEL), lambda i: (i, 0))
    return pl.pallas_call(
        body, name="ffn_down_loss", grid=(t // tm,),
        in_specs=[row, pl.BlockSpec((tm, D_FF), lambda i: (i, 0)), pl.BlockSpec((D_FF, D_MODEL), lambda i: (0, 0)),
                  pl.BlockSpec((1, D_MODEL), lambda i: (0, 0)), row],
        out_specs=[row, row, pl.BlockSpec((8, 128), lambda i: (0, 0)), pl.BlockSpec((8, D_MODEL), lambda i: (0, 0))],
        out_shape=[jax.ShapeDtypeStruct((t, D_MODEL), F32), jax.ShapeDtypeStruct((t, D_MODEL), BF16),
                   jax.ShapeDtypeStruct((8, 128), F32), jax.ShapeDtypeStruct((8, D_MODEL), F32)],
        compiler_params=_params(("arbitrary",)),
    )(x1, act, wd, gf, tgt)


def _ffn_bwd_act(dx2b, wd, gt, up):
    t = dx2b.shape[0]
    tm = 256

    def body(dx_ref, w_ref, gt_ref, up_ref, dgt_ref, dup_ref):
        dact = _dot_nt(dx_ref[...], w_ref[...])
        gt = gt_ref[...].astype(F32)
        sg = _sigmoid(gt)
        dgt_ref[...] = (dact * up_ref[...].astype(F32) * sg * (1.0 + gt * (1.0 - sg))).astype(BF16)
        dup_ref[...] = (dact * gt * sg).astype(BF16)

    wide = pl.BlockSpec((tm, D_FF), lambda i: (i, 0))
    return pl.pallas_call(
        body, name="ffn_bwd_act", grid=(t // tm,),
        in_specs=[pl.BlockSpec((tm, D_MODEL), lambda i: (i, 0)), pl.BlockSpec((D_FF, D_MODEL), lambda i: (0, 0)), wide, wide],
        out_specs=[wide, wide],
        out_shape=[jax.ShapeDtypeStruct((t, D_FF), BF16)] * 2,
        compiler_params=_params(("parallel",)),
    )(dx2b, wd, gt, up)


def _ffn_bwd_h(dgt, dup, wg, wu, dx2, x1, g2, wout):
    t = dgt.shape[0]
    tm = 256

    def body(dgt_ref, dup_ref, wg_ref, wu_ref, dx2_ref, x1_ref, g_ref, wo_ref, dx1_ref, dx1b_ref, dya_ref, dyb_ref, dg_ref):
        dh = _dot(dgt_ref[...], wg_ref[...]) + _dot(dup_ref[...], wu_ref[...])
        dxn, dgr = _rms_bwd(dh, x1_ref[...], g_ref[...])
        dx1 = dx2_ref[...] + dxn
        dx1_ref[...] = dx1
        dx1b = dx1.astype(BF16)
        dx1b_ref[...] = dx1b
        dy = _dot_nt(dx1b, wo_ref[...])
        dya_ref[...] = dy[:, :RW]
        dyb_ref[...] = dy[:, RW:]
        _acc(dg_ref, _colsum8(dgr), pl.program_id(0) == 0)

    wide = pl.BlockSpec((tm, D_FF), lambda i: (i, 0))
    row = pl.BlockSpec((tm, D_MODEL), lambda i: (i, 0))
    half = pl.BlockSpec((tm, RW), lambda i: (i, 0))
    wsp = pl.BlockSpec((D_FF, D_MODEL), lambda i: (0, 0))
    return pl.pallas_call(
        body, name="ffn_bwd_h", grid=(t // tm,),
        in_specs=[wide, wide, wsp, wsp, row, row, pl.BlockSpec((1, D_MODEL), lambda i: (0, 0)),
                  pl.BlockSpec((D_MODEL, D_MODEL), lambda i: (0, 0))],
        out_specs=[row, row, half, half, pl.BlockSpec((8, D_MODEL), lambda i: (0, 0))],
        out_shape=[jax.ShapeDtypeStruct((t, D_MODEL), F32), jax.ShapeDtypeStruct((t, D_MODEL), BF16),
                   jax.ShapeDtypeStruct((t, RW), F32), jax.ShapeDtypeStruct((t, RW), F32),
                   jax.ShapeDtypeStruct((8, D_MODEL), F32)],
        compiler_params=_params(("arbitrary",)),
    )(dgt, dup, wg, wu, dx2, x1, g2, wout)


def _wgrad(a, b, tk, tn, name):
    t, kdim = a.shape
    ndim = b.shape[1]

    def body(a_ref, b_ref, o_ref):
        o_ref[...] = _dot_tn(a_ref[...], b_ref[...])

    return pl.pallas_call(
        body, name=name, grid=(kdim // tk, ndim // tn),
        in_specs=[pl.BlockSpec((t, tk), lambda i, j: (0, i)), pl.BlockSpec((t, tn), lambda i, j: (0, j))],
        out_specs=pl.BlockSpec((tk, tn), lambda i, j: (i, j)),
        out_shape=jax.ShapeDtypeStruct((kdim, ndim), F32),
        compiler_params=_params(("parallel", "parallel")),
    )(a, b)


def _post_bwd(dya, y, r, k2, v, g, lnw, lnb, rk):
    t = y.shape[0]
    tm = _POST_TM

    def body(d_ref, y_ref, r_ref, k_ref, v_ref, g_ref, lnw_ref, lnb_ref, rk_ref,
             dy_ref, dr_ref, dk_ref, dv_ref, dg_ref, dlnw_ref, dlnb_ref, drk_ref):
        first = pl.program_id(0) == 0
        ones = jnp.ones((tm, 1), F32)
        prim = (y_ref[...], r_ref[...], k_ref[...], v_ref[...], g_ref[...],
                ones * lnw_ref[...], ones * lnb_ref[...], ones * rk_ref[...])
        _, vjp = jax.vjp(_post_fn, *prim)
        dy, dr, dk, dv, dg, dlnw, dlnb, drk = vjp(d_ref[...])
        dy_ref[...] = dy
        dr_ref[...] = dr
        dk_ref[...] = dk
        dv_ref[...] = dv
        dg_ref[...] = dg
        _acc(dlnw_ref, _colsum8(dlnw), first)
        _acc(dlnb_ref, _colsum8(dlnb), first)
        _acc(drk_ref, _colsum8(drk), first)

    row = pl.BlockSpec((tm, RW), lambda i: (i, 0))
    vec = pl.BlockSpec((1, RW), lambda i: (0, 0))
    part = pl.BlockSpec((8, RW), lambda i: (0, 0))
    return pl.pallas_call(
        body, name="rwkv_post_bwd", grid=(t // tm,),
        in_specs=[row] * 6 + [vec] * 3, out_specs=[row] * 5 + [part] * 3,
        out_shape=[jax.ShapeDtypeStruct((t, RW), F32)] * 5 + [jax.ShapeDtypeStruct((8, RW), F32)] * 3,
        compiler_params=_params(("arbitrary",)),
    )(dya, y, r, k2, v, g, lnw, lnb, rk)


def _wkv_bwd(dy, s0s, r, lw, k2, v, kk, a):
    t = r.shape[0]
    nc = t // CHUNK

    def body(dy_ref, s_ref, r_ref, lw_ref, k_ref, v_ref, kk_ref, a_ref,
             dr_ref, dlw_ref, dk_ref, dv_ref, dkk_ref, da_ref, ds):
        @pl.when(pl.program_id(0) == 0)
        def _():
            ds[...] = jnp.zeros_like(ds)

        _, vjp = jax.vjp(_wkv_chunk_fn, s_ref[0],
                         *[_pairs(ref) for ref in (r_ref, lw_ref, k_ref, v_ref, kk_ref, a_ref)])
        res = vjp((_pairs(dy_ref), ds[...]))
        ds[...] = res[0]
        for ref, val in zip((dr_ref, dlw_ref, dk_ref, dv_ref, dkk_ref, da_ref), res[1:]):
            for p in range(N_PAIR):
                ref[:, 128 * p:128 * (p + 1)] = val[p]

    blk = pl.BlockSpec((CHUNK, RW), lambda c: (nc - 1 - c, 0))
    return pl.pallas_call(
        body, name="wkv_bwd", grid=(nc,),
        in_specs=[blk, pl.BlockSpec((1, N_PAIR, 128, 128), lambda c: (nc - 1 - c, 0, 0, 0))] + [blk] * 6,
        out_specs=[blk] * 6,
        out_shape=[jax.ShapeDtypeStruct((t, RW), F32)] * 6,
        scratch_shapes=[pltpu.VMEM((N_PAIR, 128, 128), F32)],
        compiler_params=_params(("arbitrary",)),
    )(dy, s0s, r, lw, k2, v, kk, a)


def _prep_bwd(proj, pw, douts):
    t = proj.shape[0]
    tm = _PREP_TM
    nt = t // tm

    def body(p_ref, l8_ref, mu, w0, w2p, a0, a2p, g2, k_k, k_a, dr, dr2, dlw, dk2, dk22, dv, dv2, dkk, da, dg,
             dp_ref, dmu_ref, dw0_ref, dw2_ref, da0_ref, da2_ref, dg2_ref, dkk_ref, dka_ref, carry):
        i = pl.program_id(0)
        first = i == 0

        @pl.when(first)
        def _():
            carry[...] = jnp.zeros_like(carry)

        p = p_ref[...]
        pprev = _shifted(p, l8_ref[...], i == nt - 1)
        ones = jnp.ones((tm, 1), F32)
        prim = (p, pprev, ones * mu[...], ones * w0[...], w2p[...], ones * a0[...], a2p[...], g2[...],
                ones * k_k[...], ones * k_a[...])
        _, vjp = jax.vjp(_prep_fn, *prim)
        dp, dpp, dmu, dw0, dw2, da0, da2, dg2, dkk_, dka = vjp(
            (dr[...] + dr2[...], dlw[...], dk2[...] + dk22[...], dv[...] + dv2[...], dkk[...], da[...], dg[...]))
        up = pltpu.roll(dpp, tm - 1, axis=0)
        rid = lax.broadcasted_iota(jnp.int32, dpp.shape, 0)
        dp_ref[...] = dp + jnp.where(rid == tm - 1, carry[0:1, :], up)
        carry[...] = jnp.broadcast_to(dpp[0:1, :], carry.shape)
        _acc(dmu_ref, _colsum8(dmu), first)
        _acc(dw0_ref, _colsum8(dw0), first)
        _acc(dw2_ref, dw2, first)
        _acc(da0_ref, _colsum8(da0), first)
        _acc(da2_ref, da2, first)
        _acc(dg2_ref, dg2, first)
        _acc(dkk_ref, _colsum8(dkk_), first)
        _acc(dka_ref, _colsum8(dka), first)

    rev = lambda i: (nt - 1 - i, 0)
    row = pl.BlockSpec((tm, RW), rev)
    part = lambda n: pl.BlockSpec((8, n), lambda i: (0, 0))
    mat = pl.BlockSpec((128, RW), lambda i: (0, 0))
    return pl.pallas_call(
        body, name="rwkv_prep_bwd", grid=(nt,),
        in_specs=[pl.BlockSpec((tm, SHIFT_COLS), rev),
                  pl.BlockSpec((8, SHIFT_COLS), lambda i: (jnp.maximum((nt - 1 - i) * (tm // 8) - 1, 0), 0))]
                 + _prep_specs(tm) + [row] * 10,
        out_specs=[pl.BlockSpec((tm, SHIFT_COLS), rev), part(SHIFT_COLS), part(RW), mat, part(RW), mat, mat,
                   part(RW), part(RW)],
        out_shape=[jax.ShapeDtypeStruct((t, SHIFT_COLS), F32), jax.ShapeDtypeStruct((8, SHIFT_COLS), F32),
                   jax.ShapeDtypeStruct((8, RW), F32), jax.ShapeDtypeStruct((128, RW), F32),
                   jax.ShapeDtypeStruct((8, RW), F32), jax.ShapeDtypeStruct((128, RW), F32),
                   jax.ShapeDtypeStruct((128, RW), F32), jax.ShapeDtypeStruct((8, RW), F32),
                   jax.ShapeDtypeStruct((8, RW), F32)],
        scratch_shapes=[pltpu.VMEM((8, SHIFT_COLS), F32)],
        compiler_params=_params(("arbitrary",)),
    )(proj, proj, *pw, *douts)


def _combine_bwd(dyb, o, l, og):
    t = dyb.shape[0]
    tm = _COMB_TM

    def body(d_ref, o_ref, l_ref, og_ref, do_ref, dl_ref, dog_ref):
        ones = jnp.ones((tm, 1), F32)
        dog = []
        for p in range(N_PAIR):
            cols = slice(128 * p, 128 * (p + 1))
            _, vjp = jax.vjp(_combine_fn, o_ref[0, p], o_ref[1, p], o_ref[2, p], l_ref[0, p], l_ref[1, p], l_ref[2, p],
                             ones * og_ref[:, cols])
            res = vjp(d_ref[:, cols])
            for b in range(3):
                do_ref[b, p] = res[b]
                dl_ref[b, p] = res[3 + b]
            dog.append(_colsum8(res[6]))
        _acc(dog_ref, jnp.concatenate(dog, axis=1), pl.program_id(0) == 0)

    blk = pl.BlockSpec((3, N_PAIR, tm, 128), lambda i: (0, 0, i, 0))
    return pl.pallas_call(
        body, name="attn_combine_bwd", grid=(t // tm,),
        in_specs=[pl.BlockSpec((tm, RW), lambda i: (i, 0)), blk, blk, pl.BlockSpec((1, RW), lambda i: (0, 0))],
        out_specs=[blk, blk, pl.BlockSpec((8, RW), lambda i: (0, 0))],
        out_shape=[jax.ShapeDtypeStruct((3, N_PAIR, t, 128), F32)] * 2 + [jax.ShapeDtypeStruct((8, RW), F32)],
        compiler_params=_params(("arbitrary",)),
    )(dyb, o, l, og)


def _attn_bwd(do, dl, qkv):
    t = qkv.shape[2]

    def body(do_ref, dl_ref, q_ref, k_ref, v_ref, dq_ref, dk_ref, dv_ref):
        @pl.when(pl.program_id(1) == 0)
        def _():
            for ref in (dq_ref, dk_ref, dv_ref):
                ref[...] = jnp.zeros_like(ref)

        def unit(di, places, has_prev):
            cur = [_dilated_rows(d, r, n) for d, r, n in places]
            args = [_take(ref, (0,), cur) for ref in (q_ref, k_ref, v_ref)]
            if has_prev:
                prv = [_dilated_rows(d, r, n - 1) for d, r, n in places]
                args += [_take(ref, (0,), prv) for ref in (k_ref, v_ref)]
            _, vjp = jax.vjp(_attn_block_fn, *args)
            res = vjp((_take(do_ref, (0,), cur), _take(dl_ref, (0,), cur)))
            _put(dq_ref, (), cur, res[0], add=True)
            _put(dk_ref, (), cur, res[1], add=True)
            _put(dv_ref, (), cur, res[2], add=True)
            if has_prev:
                _put(dk_ref, (), prv, res[3], add=True)
                _put(dv_ref, (), prv, res[4], add=True)

        _for_each_sequence(t, unit)

    spec = lambda j: pl.BlockSpec((1, ATTN_GROUP, t, 128), lambda i, b: (j, i, 0, 0))
    branch = pl.BlockSpec((1, ATTN_GROUP, t, 128), lambda i, b: (b, i, 0, 0))
    out = pl.BlockSpec((ATTN_GROUP, t, 128), lambda i, b: (i, 0, 0))
    return pl.pallas_call(
        body, name="attn_bwd", grid=(N_PAIR // ATTN_GROUP, len(DILATIONS)),
        in_specs=[branch, branch, spec(0), spec(1), spec(2)], out_specs=[out] * 3,
        out_shape=[jax.ShapeDtypeStruct((N_PAIR, t, 128), F32)] * 3,
        compiler_params=_params(("parallel", "arbitrary")),
    )(do, dl, qkv, qkv, qkv)


def _in_proj_bwd(dpa, dq, dk, dv, win, x, g1, dx1):
    t = x.shape[0]
    tm = 256

    def body(dpa_ref, dq_ref, dk_ref, dv_ref, w_ref, x_ref, g_ref, dx1_ref, dproj_ref, dx_ref, dg_ref):
        parts = [dpa_ref[...]] + [ref[p] for ref in (dq_ref, dk_ref, dv_ref) for p in range(N_PAIR)]
        dproj = jnp.concatenate([z.astype(BF16) for z in parts], axis=1)
        dproj_ref[...] = dproj
        dh = _dot(dproj, w_ref[...])
        dxn, dgr = _rms_bwd(dh, x_ref[...], g_ref[...])
        dx_ref[...] = dx1_ref[...] + dxn
        _acc(dg_ref, _colsum8(dgr), pl.program_id(0) == 0)

    row = pl.BlockSpec((tm, D_MODEL), lambda i: (i, 0))
    pair = pl.BlockSpec((N_PAIR, tm, 128), lambda i: (0, i, 0))
    return pl.pallas_call(
        body, name="in_proj_bwd", grid=(t // tm,),
        in_specs=[pl.BlockSpec((tm, SHIFT_COLS), lambda i: (i, 0))] + [pair] * 3
                 + [pl.BlockSpec((IN_COLS, D_MODEL), lambda i: (0, 0)), row, pl.BlockSpec((1, D_MODEL), lambda i: (0, 0)), row],
        out_specs=[pl.BlockSpec((tm, IN_COLS), lambda i: (i, 0)), row, pl.BlockSpec((8, D_MODEL), lambda i: (0, 0))],
        out_shape=[jax.ShapeDtypeStruct((t, IN_COLS), BF16), jax.ShapeDtypeStruct((t, D_MODEL), F32),
                   jax.ShapeDtypeStruct((8, D_MODEL), F32)],
        compiler_params=_params(("arbitrary",)),
    )(dpa, dq, dk, dv, win, x, g1, dx1)


def _pad_lora(w, lo):
    z = jnp.zeros((64, RW), F32)
    return jnp.concatenate([w, z], axis=0) if lo == 0 else jnp.concatenate([z, w], axis=0)


def _local_step(x, tgt, win, vecs, w2, a2, g2m, get_rest, send_rest):
    pw = (vecs["mu_shift"], vecs["decay_w0"], _pad_lora(w2, 0), vecs["iclr_a0"], _pad_lora(a2, 64), g2m,
          vecs["k_k"], vecs["k_a"])
    h, proj, qkv = _in_proj(x, vecs["mix_norm_g"], win)
    r, lw, k2, v, kk, a, g = _prep_fwd(proj, pw)
    y, s0s = _wkv_fwd(r, lw, k2, v, kk, a)
    ya = _post_fwd(y, r, k2, v, g, vecs["ln_x_w"], vecs["ln_x_b"], vecs["r_k"])
    o_att, l_att = _attn_fwd(qkv)
    yb = _combine_fwd(o_att, l_att, vecs["attn_out_g"])

    wout, wg, wu, wd = get_rest(yb)
    ycat = jnp.concatenate([ya, yb], axis=1)
    x1, h2 = _out_proj(x, ycat, wout, vecs["ffn_norm_g"])
    gt, up, act = _ffn_up(h2, wg, wu)
    dx2, dx2b, loss8, dgf = _ffn_down_loss(x1, act, wd, vecs["final_norm_g"], tgt)

    dgt, dup = _ffn_bwd_act(dx2b, wd, gt, up)
    dx1, dx1b, dya, dyb, dg2n = _ffn_bwd_h(dgt, dup, wg, wu, dx2, x1, vecs["ffn_norm_g"], wout)
    gw = {
        "w_down": _wgrad(act, dx2b, 1408, 1024, "wgrad_down"),
        "w_gate": _wgrad(dgt, h2, 1408, 1024, "wgrad_gate"),
        "w_up": _wgrad(dup, h2, 1408, 1024, "wgrad_up"),
        "w_out": _wgrad(ycat, dx1b, 1024, 1024, "wgrad_out"),
    }

    lnw = vecs["ln_x_w"] + send_rest(gw)[0, 0]
    dy, dr_p, dk2_p, dv_p, dg, dlnw, dlnb, drk = _post_bwd(dya, y, r, k2, v, g, lnw, vecs["ln_x_b"], vecs["r_k"])
    dr_s, dlw, dk2_s, dv_s, dkk, da = _wkv_bwd(dy, s0s, r, lw, k2, v, kk, a)
    dpa, dmu, dw0, dw2p, da0, da2p, dg2m, dk_k, dk_a = _prep_bwd(
        proj, pw, (dr_p, dr_s, dlw, dk2_p, dk2_s, dv_p, dv_s, dkk, da, dg))

    do_att, dl_att, dog = _combine_bwd(dyb, o_att, l_att, vecs["attn_out_g"])
    dq, dk, dv = _attn_bwd(do_att, dl_att, qkv)
    dproj, dx, dg1 = _in_proj_bwd(dpa, dq, dk, dv, win, x, vecs["mix_norm_g"], dx1)
    gw["w_in"] = _wgrad(dproj, h, 1664, 1024, "wgrad_in")
    gw["decay_w2"] = dw2p[:64]
    gw["iclr_a2"] = da2p[64:]
    gw["gate_g2"] = dg2m
    gv = {"mix_norm_g": dg1, "mu_shift": dmu, "decay_w0": dw0, "iclr_a0": da0, "k_k": dk_k, "k_a": dk_a, "r_k": drk,
          "ln_x_w": dlnw, "ln_x_b": dlnb, "attn_out_g": dog, "ffn_norm_g": dg2n, "final_norm_g": dgf}
    return loss8, dx, gw, gv


N_CHIP = 4
N_DEV = 8
MATS = ("w_in", "w_out", "w_gate", "w_up", "w_down")
LORAS = ("decay_w2", "iclr_a2", "gate_g2")
VECS = (("mix_norm_g", 1024), ("mu_shift", 1792), ("decay_w0", 512), ("iclr_a0", 512), ("k_k", 512), ("k_a", 512),
        ("r_k", 512), ("ln_x_w", 512), ("ln_x_b", 512), ("attn_out_g", 512), ("ffn_norm_g", 1024),
        ("final_norm_g", 1024))
N_VEC = sum(n for _, n in VECS)
N_SMALL = N_VEC + 128
ROWS_PAD = 3328
ANY = pl.BlockSpec(memory_space=pl.ANY)


def _flip(v, f):
    return 1 - v if f else v


class _Me:
    def __init__(self, mode):
        x, y, c = lax.axis_index("x"), lax.axis_index("y"), lax.axis_index("c")
        self.core, self.chip, self.dev = c, 2 * x + y, 4 * x + 2 * y + c
        self.sibling = (x, y, 1 - c)
        if mode == "chips":
            self.peers = [(px, py, c) for px, py in ((1 - x, y), (x, 1 - y), (1 - x, 1 - y))]
        else:
            self.peers = [(_flip(x, k & 4), _flip(y, k & 2), _flip(c, k & 1)) for k in range(1, N_DEV)]


def _half(core, rows):
    h = rows // 2
    return pl.ds(pl.multiple_of(core * h, h), h)


def _landing(a, kind):
    if kind == "gather":
        return (N_CHIP,) + a.shape
    if kind == "scatter":
        return (N_DEV, a.shape[1] // 2, a.shape[2])
    return (N_DEV,) + a.shape


def _peer_copy(srcs, dsts, kinds, send_sems, recv_sems, me, j, i, incoming):
    px, py, pc = me.peers[j]
    pchip, pdev = 2 * px + py, 4 * px + 2 * py + pc
    src, dst, kind = srcs[i], dsts[i], kinds[i]
    if kind == "gather":
        rows = _half(me.core, src.shape[0])
        src, dst = src.at[rows], dst.at[pchip if incoming else me.chip, rows]
    elif kind == "scatter":
        src, dst = src.at[pchip, _half(pc, src.shape[1])], dst.at[pdev if incoming else me.dev]
    else:
        dst = dst.at[pdev if incoming else me.dev]
    n = len(srcs)
    return pltpu.make_async_remote_copy(src_ref=src, dst_ref=dst, send_sem=send_sems.at[n * j + i],
                                        recv_sem=recv_sems.at[n * j + i], device_id=(px, py, pc), device_id_type=MESH)


def _mode(kinds):
    return "chips" if kinds[0] == "gather" else "devs"


def _npeer(kinds):
    return N_CHIP - 1 if kinds[0] == "gather" else N_DEV - 1


def _swap_blocking(arrs, lands, kinds, name):
    n = len(arrs)

    def body(*refs):
        srcs, dsts = refs[:n], refs[2 * n:3 * n]
        send_sems, recv_sems = refs[3 * n:]
        me = _Me(_mode(kinds))
        sends = [_peer_copy(srcs, dsts, kinds, send_sems, recv_sems, me, j, i, False)
                 for j in range(len(me.peers)) for i in range(n)]
        for cp in sends:
            cp.start()
        for j in range(len(me.peers)):
            for i in range(n):
                _peer_copy(srcs, dsts, kinds, send_sems, recv_sems, me, j, i, True).wait_recv()
        for cp in sends:
            cp.wait_send()

    ns = _npeer(kinds) * n
    return pl.pallas_call(
        body, name=name, in_specs=[ANY] * (2 * n), out_specs=[ANY] * n,
        out_shape=[jax.ShapeDtypeStruct(l.shape, l.dtype) for l in lands],
        input_output_aliases={n + i: i for i in range(n)},
        scratch_shapes=[pltpu.SemaphoreType.DMA((ns,)), pltpu.SemaphoreType.DMA((ns,))],
    )(*arrs, *lands)


def _swap_gathered(lands, name):
    n = len(lands)

    def body(*refs):
        dsts, send_sems, recv_sems = refs[n:2 * n], refs[2 * n], refs[2 * n + 1]
        me = _Me("chips")

        def copy(j, i, incoming):
            px, py, _ = me.peers[j]
            rows_out, rows_in = _half(me.core, dsts[i].shape[1]), _half(1 - me.core, dsts[i].shape[1])
            return pltpu.make_async_remote_copy(
                src_ref=dsts[i].at[2 * px + py, rows_out], dst_ref=dsts[i].at[2 * px + py, rows_in if incoming else rows_out],
                send_sem=send_sems.at[n * j + i], recv_sem=recv_sems.at[n * j + i], device_id=me.sibling, device_id_type=MESH)

        sends = [copy(j, i, False) for j in range(3) for i in range(n)]
        for cp in sends:
            cp.start()
        for j in range(3):
            for i in range(n):
                copy(j, i, True).wait_recv()
        for cp in sends:
            cp.wait_send()

    return pl.pallas_call(
        body, name=name, in_specs=[ANY] * n, out_specs=[ANY] * n,
        out_shape=[jax.ShapeDtypeStruct(l.shape, l.dtype) for l in lands],
        input_output_aliases={i: i for i in range(n)},
        scratch_shapes=[pltpu.SemaphoreType.DMA((3 * n,)), pltpu.SemaphoreType.DMA((3 * n,))],
    )(*lands)


def _join_halves(sums, name):
    n = len(sums)

    def body(*refs):
        dsts, send_sems, recv_sems = refs[n:2 * n], refs[2 * n], refs[2 * n + 1]
        me = _Me("chips")

        def copy(i, incoming):
            mine, other = _half(me.core, dsts[i].shape[0]), _half(1 - me.core, dsts[i].shape[0])
            return pltpu.make_async_remote_copy(src_ref=dsts[i].at[mine], dst_ref=dsts[i].at[other if incoming else mine],
                                                send_sem=send_sems.at[i], recv_sem=recv_sems.at[i],
                                                device_id=me.sibling, device_id_type=MESH)

        sends = [copy(i, False) for i in range(n)]
        for cp in sends:
            cp.start()
        for i in range(n):
            copy(i, True).wait_recv()
        for cp in sends:
            cp.wait_send()

    return pl.pallas_call(
        body, name=name, in_specs=[ANY] * n, out_specs=[ANY] * n,
        out_shape=[jax.ShapeDtypeStruct(s.shape, s.dtype) for s in sums],
        input_output_aliases={i: i for i in range(n)},
        scratch_shapes=[pltpu.SemaphoreType.DMA((n,)), pltpu.SemaphoreType.DMA((n,))],
    )(*sums)


HBM = pl.BlockSpec(memory_space=pltpu.HBM)
SEM = pl.BlockSpec(memory_space=pltpu.SEMAPHORE)
EFFECT = pltpu.SideEffectType.DATAFLOW_SIDE_EFFECTING


def _swap_start(arrs, lands, kinds, name):
    n = len(arrs)

    def body(*refs):
        srcs, dsts, send_sems, recv_sems, token = refs[:n], refs[n:2 * n], refs[2 * n], refs[2 * n + 1], refs[-1]
        me = _Me(_mode(kinds))
        for j in range(len(me.peers)):
            for i in range(n):
                _peer_copy(srcs, dsts, kinds, send_sems, recv_sems, me, j, i, False).start()
        token[...] = jnp.zeros_like(token)

    ns = _npeer(kinds) * n
    outs = pl.pallas_call(
        body, name=name,
        out_shape=(pltpu.SemaphoreType.DMA((ns,)), pltpu.SemaphoreType.DMA((ns,)),
                   *[pltpu.HBM(a.shape, a.dtype) for a in arrs], *[pltpu.HBM(l.shape, l.dtype) for l in lands],
                   jax.ShapeDtypeStruct((8, 128), F32)),
        in_specs=[HBM] * (2 * n), out_specs=(SEM, SEM, *[HBM] * (2 * n), pl.BlockSpec(memory_space=pltpu.VMEM)),
        input_output_aliases={k: 2 + k for k in range(2 * n)},
        compiler_params=pltpu.CompilerParams(has_side_effects=EFFECT),
    )(*[pltpu.with_memory_space_constraint(a, pltpu.HBM) for a in arrs],
      *[pltpu.with_memory_space_constraint(l, pltpu.HBM) for l in lands])
    return outs[0], outs[1], outs[2:2 + n], outs[2 + n:2 + 2 * n], outs[-1]


def _swap_wait(send_sems, recv_sems, srcs_thru, lands_thru, after, kinds, name):
    n = len(srcs_thru)

    def body(*refs):
        srcs, dsts, s_sems, r_sems = refs[:n], refs[n:2 * n], refs[2 * n], refs[2 * n + 1]
        me = _Me(_mode(kinds))
        for j in range(len(me.peers)):
            for i in range(n):
                cp = _peer_copy(srcs, dsts, kinds, s_sems, r_sems, me, j, i, True)
                cp.wait_send()
                cp.wait_recv()

    outs = pl.pallas_call(
        body, name=name,
        out_shape=tuple(pltpu.HBM(a.shape, a.dtype) for a in (*srcs_thru, *lands_thru)),
        in_specs=[HBM] * (2 * n) + [SEM, SEM, ANY], out_specs=tuple([HBM] * (2 * n)),
        input_output_aliases={k: k for k in range(2 * n)},
        compiler_params=pltpu.CompilerParams(has_side_effects=EFFECT),
    )(*srcs_thru, *lands_thru, send_sems, recv_sems, after)
    return outs[n:]


def _adamw(w, g, m, v):
    m = ADAM_B1 * m + (1.0 - ADAM_B1) * g
    v = ADAM_B2 * v + (1.0 - ADAM_B2) * (g * g)
    m_hat = m / (1.0 - ADAM_B1 ** ADAM_STEP)
    v_hat = v / (1.0 - ADAM_B2 ** ADAM_STEP)
    delta = -ADAM_LR * (m_hat / (jnp.sqrt(v_hat) + ADAM_EPS) + ADAM_WD * w)
    return delta, m, v


def _reduce8(rbuf, core, tr, name):
    _, h, cols = rbuf.shape

    def body(core_ref, r_ref, g_ref):
        g = r_ref[0].astype(F32)
        for s in range(1, N_DEV):
            g = g + r_ref[s].astype(F32)
        g_ref[...] = g

    return pl.pallas_call(
        body, name=name,
        grid_spec=pltpu.PrefetchScalarGridSpec(
            num_scalar_prefetch=1, grid=(h // tr,),
            in_specs=[pl.BlockSpec((N_DEV, tr, cols), lambda i, core_ref: (0, i, 0))],
            out_specs=pl.BlockSpec((tr, cols), lambda i, core_ref: (core_ref[0] * (h // tr) + i, 0))),
        out_shape=jax.ShapeDtypeStruct((2 * h, cols), F32),
        compiler_params=_params(("parallel",)),
    )(core, rbuf)


def _adamw_call(g, w, m, v, tr, name):
    _, rows, cols = w.shape

    def body(g_in, w_ref, m_ref, v_ref, g_ref, d_ref, nm_ref, nv_ref):
        g = g_in[...]
        g_ref[0] = g
        d_ref[0], nm_ref[0], nv_ref[0] = _adamw(w_ref[0], g, m_ref[0], v_ref[0])

    row = pl.BlockSpec((1, tr, cols), lambda i: (0, i, 0))
    return pl.pallas_call(
        body, name=name, grid=(rows // tr,),
        in_specs=[pl.BlockSpec((tr, cols), lambda i: (i, 0)), row, row, row], out_specs=[row] * 4,
        out_shape=[jax.ShapeDtypeStruct(w.shape, F32)] * 4,
        compiler_params=_params(("parallel",)),
    )(g, w, m, v)


def _rowsum_small(parts, loss8):
    def body(*refs):
        out = refs[-1]
        c0 = 0
        for ref in refs[:-1]:
            n = ref.shape[1]
            out[:, c0:c0 + n] = jnp.sum(ref[...], axis=0, keepdims=True)
            c0 += n

    return pl.pallas_call(body, name="rowsum_small", out_shape=jax.ShapeDtypeStruct((1, N_SMALL), F32))(*parts, loss8)


def _reduce_adamw_small(sbuf, ws, ms, vs):
    nv = len(ws)

    def body(*refs):
        s_ref, ins, outs = refs[0], refs[1:1 + 3 * nv], refs[1 + 3 * nv:]
        tot = s_ref[0]
        for s in range(1, N_DEV):
            tot = tot + s_ref[s]
        c0 = 0
        for i in range(nv):
            n = ins[i].shape[1]
            g = tot[:, c0:c0 + n]
            outs[i][...] = g
            outs[nv + i][...], outs[2 * nv + i][...], outs[3 * nv + i][...] = _adamw(
                ins[i][...], g, ins[nv + i][...], ins[2 * nv + i][...])
            c0 += n
        outs[-1][...] = tot[:, c0:]

    return pl.pallas_call(
        body, name="reduce_adamw_small",
        out_shape=[jax.ShapeDtypeStruct(a.shape, F32) for a in ws] * 4 + [jax.ShapeDtypeStruct((1, 128), F32)],
    )(sbuf, *ws, *ms, *vs)


_TRANSPOSED = ("w_in", "w_gate", "w_up")
_ROW_STACKED = MATS
_ADAM_TILE = {"w_in": 208, "w_out": 256, "w_gate": 176, "w_up": 176, "w_down": 176, "decay_w2": 64, "iclr_a2": 64,
              "gate_g2": 128}
_SUM_TILE = {"w_in": 208, "w_out": 128, "w_gate": 176, "w_up": 176, "w_down": 176, "decay_w2": 32, "iclr_a2": 32,
             "gate_g2": 64}


def _full(n, stacked):
    p, r, c = stacked.shape
    if n in _ROW_STACKED:
        return stacked.reshape(p * r, c)
    return jnp.transpose(stacked, (1, 0, 2)).reshape(r, p * c)


def _by_chip(n, full):
    if n in _ROW_STACKED:
        return full.reshape(N_CHIP, full.shape[0] // N_CHIP, full.shape[1])
    r, c = full.shape
    return jnp.transpose(full.reshape(r, N_CHIP, c // N_CHIP), (1, 0, 2))


def _with_own(land_shape, dtype, own, slot):
    return lax.dynamic_update_slice(lax.empty(land_shape, dtype), own[None], (slot,) + (0,) * own.ndim)


def kernel(x, mix_norm_g, w_in, mu_shift, decay_w0, decay_w2, iclr_a0, iclr_a2, gate_g2, k_k, k_a, r_k, ln_x_w, ln_x_b, attn_out_g, w_out, ffn_norm_g, w_gate, w_up, w_down, final_norm_g, loss_target, m_mix_norm_g, m_w_in, m_mu_shift, m_decay_w0, m_decay_w2, m_iclr_a0, m_iclr_a2, m_gate_g2, m_k_k, m_k_a, m_r_k, m_ln_x_w, m_ln_x_b, m_attn_out_g, m_w_out, m_ffn_norm_g, m_w_gate, m_w_up, m_w_down, m_final_norm_g, v_mix_norm_g, v_w_in, v_mu_shift, v_decay_w0, v_decay_w2, v_iclr_a0, v_iclr_a2, v_gate_g2, v_k_k, v_k_a, v_r_k, v_ln_x_w, v_ln_x_b, v_attn_out_g, v_w_out, v_ffn_norm_g, v_w_gate, v_w_up, v_w_down, v_final_norm_g):
    names = ("mix_norm_g", "w_in", "mu_shift", "decay_w0", "decay_w2", "iclr_a0", "iclr_a2", "gate_g2", "k_k", "k_a",
             "r_k", "ln_x_w", "ln_x_b", "attn_out_g", "w_out", "ffn_norm_g", "w_gate", "w_up", "w_down", "final_norm_g")
    w = dict(zip(names, (mix_norm_g, w_in, mu_shift, decay_w0, decay_w2, iclr_a0, iclr_a2, gate_g2, k_k, k_a, r_k,
                         ln_x_w, ln_x_b, attn_out_g, w_out, ffn_norm_g, w_gate, w_up, w_down, final_norm_g)))
    m = dict(zip(names, (m_mix_norm_g, m_w_in, m_mu_shift, m_decay_w0, m_decay_w2, m_iclr_a0, m_iclr_a2, m_gate_g2,
                         m_k_k, m_k_a, m_r_k, m_ln_x_w, m_ln_x_b, m_attn_out_g, m_w_out, m_ffn_norm_g, m_w_gate,
                         m_w_up, m_w_down, m_final_norm_g)))
    v = dict(zip(names, (v_mix_norm_g, v_w_in, v_mu_shift, v_decay_w0, v_decay_w2, v_iclr_a0, v_iclr_a2, v_gate_g2,
                         v_k_k, v_k_a, v_r_k, v_ln_x_w, v_ln_x_b, v_attn_out_g, v_w_out, v_ffn_norm_g, v_w_gate,
                         v_w_up, v_w_down, v_final_norm_g)))
    first = ("w_in",) + LORAS
    rest = ("w_out", "w_gate", "w_up", "w_down")
    xi, yi, ci = lax.axis_index("x"), lax.axis_index("y"), lax.axis_index("c")
    my_chip, my_dev = 2 * xi + yi, 4 * xi + 2 * yi + ci
    gather, scatter = ("gather",) * 4, ("scatter",) * 4

    sh = lambda z, n: jnp.transpose(z[0]) if n in _TRANSPOSED else z[0]
    wb = {n: sh(w[n], n).astype(BF16) for n in MATS}
    lands = [_with_own((N_CHIP,) + wb[n].shape, BF16, wb[n], my_chip) for n in rest]
    ssem, rsem, srcs_thru, lands_thru, tok = _swap_start([wb[n] for n in rest], lands, gather, "gather_rest_start")
    mine = [wb["w_in"]] + [w[n][0] for n in LORAS]
    got = _swap_blocking(mine, [_with_own((N_CHIP,) + a.shape, a.dtype, a, my_chip) for a in mine], gather, "gather_first")
    win, w2, a2, g2m = (_full(n, z) for n, z in zip(first, _swap_gathered(got, "gather_first_halves")))

    vecs = {n: w[n].reshape(1, sz) for n, sz in VECS}
    vecs["mix_norm_g"] = vecs["mix_norm_g"] + tok[0, 0]

    def get_rest(after):
        halves = _swap_wait(ssem, rsem, srcs_thru, lands_thru, after, gather, "gather_rest_wait")
        return [_full(n, z) for n, z in zip(rest, _swap_gathered(halves, "gather_rest_halves"))]

    flight = []

    def my_half(g):
        h = g.shape[1] // 2
        return lax.dynamic_slice(g, (my_chip, ci * h, 0), (1, h, g.shape[2]))[0]

    def send_rest(gw):
        gs = [_by_chip(n, gw[n]).astype(BF16) for n in rest]
        into = [_with_own((N_DEV,) + my_half(g).shape, BF16, my_half(g), my_dev) for g in gs]
        flight.extend(_swap_start(gs, into, scatter, "exchange_rest_start"))
        return flight[4]

    loss8, dx, gw, gv = _local_step(x[0], loss_target[0], win, vecs, w2, a2, g2m, get_rest, send_rest)

    small = _rowsum_small([gv[n] for n, _ in VECS], loss8)
    gs = [_by_chip(n, gw[n]).astype(BF16) for n in first]
    into = [_with_own((N_DEV,) + my_half(g).shape, BF16, my_half(g), my_dev) for g in gs]
    into.append(_with_own((N_DEV,) + small.shape, F32, small, my_dev))
    last = _swap_start(gs + [small], into, scatter + ("all",), "exchange_first_start")

    core = jnp.reshape(ci, (1,)).astype(jnp.int32)

    def update(group, rbufs, tag):
        sums = [_reduce8(rb, core, _SUM_TILE[n], "reduce_" + n) for n, rb in zip(group, rbufs)]
        gsum = _join_halves(sums, "join_halves_" + tag)
        out = {}
        for n, g in zip(group, gsum):
            r = _adamw_call(g, sh(w[n], n)[None], sh(m[n], n)[None], sh(v[n], n)[None], _ADAM_TILE[n], "adamw_" + n)
            out[n] = [jnp.transpose(z[0])[None] for z in r] if n in _TRANSPOSED else r
        return out

    res = update(rest, _swap_wait(flight[0], flight[1], flight[2], flight[3], last[4], scatter, "exchange_rest_wait"), "rest")
    got = _swap_wait(last[0], last[1], last[2], last[3], res["w_down"][1], scatter + ("all",), "exchange_first_wait")
    res.update(update(first, got[:4], "first"))
    rows = lambda d: [d[n].reshape(1, sz) for n, sz in VECS]
    small_res = _reduce_adamw_small(got[4], rows(w), rows(m), rows(v))

    outs = []
    for k in range(4):
        piece = {n: r[k] for n, r in res.items()}
        for i, (n, _) in enumerate(VECS):
            piece[n] = small_res[k * len(VECS) + i].reshape(w[n].shape)
        outs.extend(piece[n] for n in names)
    return (small_res[-1][0, 0], dx[None], *outs)
```

```python
import jax
import jax.numpy as jnp
from jax import lax
from jax.experimental import pallas as pl
from jax.experimental.pallas import tpu as pltpu

F32 = jnp.float32
BF16 = jnp.bfloat16

D_MODEL = 1024
HEAD_DIM = 64
RW = 512
N_PAIR = RW // 128
SHIFT_COLS = 1792
IN_COLS = 3328
D_FF = 2816
NORM_EPS = 1e-6
GN_EPS = 64e-5
CHUNK = 64
SUB = 16
WKV_PASSES = 1
ATTN_PASSES = 1
ATTN_BLOCK = 128
DILATIONS = (1, 4, 16)
NEG = -1e30
ADAM_LR, ADAM_B1, ADAM_B2, ADAM_EPS, ADAM_WD, ADAM_STEP = 0.001, 0.9, 0.999, 1e-08, 0.01, 10
VMEM_LIMIT = 56 * 1024 * 1024
MESH = pl.DeviceIdType.MESH


def _params(sem=None, **kw):
    return pltpu.CompilerParams(dimension_semantics=sem, vmem_limit_bytes=VMEM_LIMIT, **kw)


def _dot(a, b, prec=None):
    return lax.dot_general(a, b, (((1,), (0,)), ((), ())), preferred_element_type=F32, precision=prec)


def _dot_nt(a, b, prec=None):
    return lax.dot_general(a, b, (((1,), (1,)), ((), ())), preferred_element_type=F32, precision=prec)


def _dot_tn(a, b, prec=None):
    return lax.dot_general(a, b, (((0,), (0,)), ((), ())), preferred_element_type=F32, precision=prec)


_FORMS = {"nn": ((1,), (0,)), "nt": ((1,), (1,)), "tn": ((0,), (0,))}


def _dg(a, b, form):
    if a.ndim == 3 or b.ndim == 3:
        nb = a.shape[0] if a.ndim == 3 else b.shape[0]
        return jnp.stack([_dg(a[i] if a.ndim == 3 else a, b[i] if b.ndim == 3 else b, form) for i in range(nb)], axis=0)
    return lax.dot_general(a, b, (_FORMS[form], ((), ())), preferred_element_type=F32)


def _split2(x):
    hi = x.astype(BF16)
    return hi, (x - hi.astype(F32)).astype(BF16)


def _split3(x):
    hi = x.astype(BF16)
    rest = x - hi.astype(F32)
    mid = rest.astype(BF16)
    return hi, mid, (rest - mid.astype(F32)).astype(BF16)


def _mm_raw(a, b, form, mode):
    if mode == 1:
        return _dg(a.astype(BF16), b.astype(BF16), form)
    if mode == 3:
        ah, al = _split2(a)
        bh, bl = _split2(b)
        return _dg(ah, bh, form) + (_dg(ah, bl, form) + _dg(al, bh, form))
    if mode == "L3":
        ab = a.astype(BF16)
        b1, b2, b3 = _split3(b)
        return _dg(ab, b1, form) + (_dg(ab, b2, form) + _dg(ab, b3, form))
    assert mode == "R3", mode
    bb = b.astype(BF16)
    a1, a2, a3 = _split3(a)
    return _dg(a1, bb, form) + (_dg(a2, bb, form) + _dg(a3, bb, form))


def _mm(a, b, form, mode):
    @jax.custom_vjp
    def f(a, b):
        return _mm_raw(a, b, form, mode)

    def fwd(a, b):
        return _mm_raw(a, b, form, mode), (a, b)

    def bwd(res, ct):
        a, b = res
        la = {1: 1, 3: 3, "L3": None, "R3": "R3"}[mode]
        lb = {1: 1, 3: 3, "L3": "L3", "R3": None}[mode]
        if form == "nn":
            da = None if la is None else _mm_raw(ct, b, "nt", la)
            db = None if lb is None else _mm_raw(a, ct, "tn", lb)
        elif form == "nt":
            da = None if la is None else _mm_raw(ct, b, "nn", la)
            db = None if lb is None else _mm_raw(ct, a, "tn", "R3" if lb == "L3" else lb)
        else:
            da = None if la is None else _mm_raw(b, ct, "nt", "L3" if la == "R3" else la)
            db = None if lb is None else _mm_raw(a, ct, "nn", lb)
        return (jnp.zeros_like(a) if da is None else da, jnp.zeros_like(b) if db is None else db)

    f.defvjp(fwd, bwd)
    return f(a, b)


def _seg_ones(n):
    r = lax.broadcasted_iota(jnp.int32, (n, n), 0) // HEAD_DIM
    c = lax.broadcasted_iota(jnp.int32, (n, n), 1) // HEAD_DIM
    return (r == c).astype(F32)


def _segsum(x, seg):
    return _mm(x, seg, "nn", "R3")


def _rms_fwd(x, g):
    rstd = lax.rsqrt(jnp.mean(x * x, axis=-1, keepdims=True) + NORM_EPS)
    return x * rstd * g


def _rms_bwd(dy, x, g):
    rstd = lax.rsqrt(jnp.mean(x * x, axis=-1, keepdims=True) + NORM_EPS)
    xn = x * rstd
    dxn = dy * g
    dx = rstd * (dxn - xn * jnp.mean(dxn * xn, axis=-1, keepdims=True))
    return dx, dy * xn


def _sigmoid(x):
    return 1.0 / (1.0 + jnp.exp(-x))


def _softplus(x):
    return jnp.maximum(x, 0.0) + jnp.log(1.0 + jnp.exp(-jnp.abs(x)))


def _acc(ref, val, first):
    @pl.when(first)
    def _():
        ref[...] = val

    @pl.when(jnp.logical_not(first))
    def _():
        ref[...] += val


def _colsum8(v):
    rows, n = v.shape
    return jnp.sum(v.reshape(rows // 8, 8, n), axis=0)


def _prep_fn(p, pprev, mu, w0, w2p, a0, a2p, g2, k_k, k_a):
    seg = _seg_ones(RW)
    ps = p + (pprev - p) * mu
    r = ps[:, 0:RW]
    k = ps[:, RW:2 * RW]
    v = ps[:, 2 * RW:3 * RW]
    xwa = ps[:, 3 * RW:3 * RW + 128]
    xg = ps[:, 3 * RW + 128:3 * RW + 256]
    wraw = -_softplus(-(w0 + _mm(jnp.tanh(xwa), w2p, "nn", 3))) - 0.5
    lw = -jnp.exp(wraw)
    a = _sigmoid(a0 + _mm(xwa, a2p, "nn", 3))
    g = _mm(_sigmoid(xg), g2, "nn", 3)
    kk = k * k_k
    kk = kk / jnp.maximum(jnp.sqrt(_segsum(kk * kk, seg)), 1e-12)
    k2 = k * (1.0 + (a - 1.0) * k_a)
    return r, lw, k2, v, kk, a, g


def _solve_unit_lower(lmat, rhs):
    c = lmat.shape[-1]
    row = lax.broadcasted_iota(jnp.int32, (c, c), 0)
    col = lax.broadcasted_iota(jnp.int32, (c, c), 1)
    eye = (row == col).astype(F32)
    ld = jnp.where(row // SUB == col // SUB, lmat, 0.0)
    lo = lmat - ld
    x = eye + ld
    m = ld
    mm = lambda p, q: _mm(p, q, "nn", WKV_PASSES)
    for _ in range(3):
        m = mm(m, m)
        x = x + mm(x, m)
    g = mm(x, lo)
    g2 = mm(g, g)
    w = mm(x, rhs)
    w = w + mm(g2, w)
    return w + mm(g, w)


def _wkv_chunk_fn(s0, r, lw, k, v, kk, a):
    c = r.shape[-2]
    n = 2 * c
    row = lax.broadcasted_iota(jnp.int32, (n, n), 0)
    col = lax.broadcasted_iota(jnp.int32, (n, n), 1)
    same = (row // c) == (col // c)
    incl = jnp.logical_and(row >= col, same)
    strict = jnp.logical_and(row > col, same)
    sel = (lax.broadcasted_iota(jnp.int32, (n, 128), 0) // c) == (lax.broadcasted_iota(jnp.int32, (n, 128), 1) // HEAD_DIM)
    two = lambda z: jnp.concatenate([z, z], axis=-2)
    lw2 = two(lw)
    mm = lambda p_, q_, form: _mm(p_, q_, form, WKV_PASSES)
    cl = _mm(incl.astype(F32), lw2, "nn", "L3")
    p = jnp.exp(cl)
    pinv = jnp.exp(-cl)
    pprev = jnp.exp(cl - lw2)
    kk2 = two(kk)
    at = jnp.where(sel, -kk2 * pprev, 0.0)
    bt = jnp.where(sel, kk2 * two(a) * pinv, 0.0)
    kt = jnp.where(sel, two(k) * pinv, 0.0)
    rt = jnp.where(sel, two(r) * p, 0.0)
    vt = jnp.where(sel, two(v), 0.0)
    ab = jnp.where(strict, mm(at, bt, "nt"), 0.0)
    ak = jnp.where(strict, mm(at, kt, "nt"), 0.0)
    rb = jnp.where(incl, mm(rt, bt, "nt"), 0.0)
    rk = jnp.where(incl, mm(rt, kt, "nt"), 0.0)
    u = _solve_unit_lower(ab, mm(at, s0, "nt") + mm(ak, vt, "nn"))
    y2 = mm(rt, s0, "nt") + mm(rb, u, "nn") + mm(rk, vt, "nn")
    plast = jnp.exp(jnp.sum(lw, axis=-2, keepdims=True))
    s1 = (s0 + mm(u, bt, "tn") + mm(vt, kt, "tn")) * plast
    r2 = lax.broadcasted_iota(jnp.int32, (128, 128), 0) // HEAD_DIM
    c2 = lax.broadcasted_iota(jnp.int32, (128, 128), 1) // HEAD_DIM
    return y2[..., :c, :] + y2[..., c:, :], jnp.where(r2 == c2, s1, 0.0)


def _post_fn(y, r, k2, v, g, lnw, lnb, rk):
    seg = _seg_ones(RW)
    mean = _segsum(y, seg) * (1.0 / HEAD_DIM)
    yc = y - mean
    var = _segsum(yc * yc, seg) * (1.0 / HEAD_DIM)
    yn = yc * lax.rsqrt(var + GN_EPS)
    out = yn * lnw + lnb + _segsum(r * k2 * rk, seg) * v
    return out * g


def _attn_block_fn(q, kc, vc, kp=None, vp=None):
    n = ATTN_BLOCK
    qi = lax.broadcasted_iota(jnp.int32, (n, n), 0)
    kj = lax.broadcasted_iota(jnp.int32, (n, n), 1)
    lane = lax.broadcasted_iota(jnp.int32, (1, 128), 1)
    scale = HEAD_DIM ** -0.5
    os_, ls_ = [], []
    for h in range(2):
        mh = (lane // HEAD_DIM) == h
        qh = jnp.where(mh, q, 0.0)
        sc = jnp.where(kj <= qi, _mm(qh, kc, "nt", ATTN_PASSES) * scale, NEG)
        m = jnp.max(sc, axis=-1, keepdims=True)
        if kp is not None:
            sp = jnp.where(kj >= qi, _mm(qh, kp, "nt", ATTN_PASSES) * scale, NEG)
            m = jnp.maximum(m, jnp.max(sp, axis=-1, keepdims=True))
        pc = jnp.exp(sc - m)
        den = jnp.sum(pc, axis=-1, keepdims=True)
        num = _mm(pc, vc, "nn", ATTN_PASSES)
        if kp is not None:
            pp = jnp.exp(sp - m)
            den = den + jnp.sum(pp, axis=-1, keepdims=True)
            num = num + _mm(pp, vp, "nn", ATTN_PASSES)
        os_.append(num / den)
        ls_.append(m + jnp.log(den))
    m0 = (lane // HEAD_DIM) == 0
    return jnp.where(m0, os_[0], os_[1]), jnp.where(m0, ls_[0], ls_[1])


def _combine_fn(o1, o2, o3, l1, l2, l3, og):
    seg = _seg_ones(o1.shape[-1])
    m = jnp.maximum(jnp.maximum(l1, l2), l3)
    e1, e2, e3 = jnp.exp(l1 - m), jnp.exp(l2 - m), jnp.exp(l3 - m)
    o = (e1 * o1 + e2 * o2 + e3 * o3) / (e1 + e2 + e3)
    o = o * lax.rsqrt(_segsum(o * o, seg) * (1.0 / HEAD_DIM) + NORM_EPS)
    return o * og


def _in_proj(x, g1, win):
    t = x.shape[0]
    tm = 512

    def body(x_ref, g_ref, w_ref, h_ref, pa_ref, qkv_ref):
        h = _rms_fwd(x_ref[...], g_ref[...]).astype(BF16)
        h_ref[...] = h
        proj = _dot_nt(h, w_ref[...])
        pa_ref[...] = proj[:, :SHIFT_COLS]
        for j in range(3):
            for p in range(N_PAIR):
                c0 = SHIFT_COLS + j * RW + p * 128
                qkv_ref[j, p] = proj[:, c0:c0 + 128]

    return pl.pallas_call(
        body, name="in_proj", grid=(t // tm,),
        in_specs=[pl.BlockSpec((tm, D_MODEL), lambda i: (i, 0)), pl.BlockSpec((1, D_MODEL), lambda i: (0, 0)),
                  pl.BlockSpec((IN_COLS, D_MODEL), lambda i: (0, 0))],
        out_specs=[pl.BlockSpec((tm, D_MODEL), lambda i: (i, 0)), pl.BlockSpec((tm, SHIFT_COLS), lambda i: (i, 0)),
                   pl.BlockSpec((3, N_PAIR, tm, 128), lambda i: (0, 0, i, 0))],
        out_shape=[jax.ShapeDtypeStruct((t, D_MODEL), BF16), jax.ShapeDtypeStruct((t, SHIFT_COLS), F32),
                   jax.ShapeDtypeStruct((3, N_PAIR, t, 128), F32)],
        compiler_params=_params(("parallel",)),
    )(x, g1, win)


def _shifted(p, last8, first):
    prow = jnp.where(first, 0.0, last8[7:8, :])
    rolled = pltpu.roll(p, 1, axis=0)
    rid = lax.broadcasted_iota(jnp.int32, p.shape, 0)
    return jnp.where(rid == 0, prow, rolled)


_PREP_TM = 256


def _prep_specs(tm):
    vec = lambda n: pl.BlockSpec((1, n), lambda i: (0, 0))
    mat = lambda r, n: pl.BlockSpec((r, n), lambda i: (0, 0))
    return [vec(SHIFT_COLS), vec(RW), mat(128, RW), vec(RW), mat(128, RW), mat(128, RW), vec(RW), vec(RW)]


def _prep_fwd(proj, pw):
    t = proj.shape[0]
    tm = _PREP_TM

    def body(p_ref, l8_ref, mu, w0, w2p, a0, a2p, g2, k_k, k_a, *outs):
        p = p_ref[...]
        pprev = _shifted(p, l8_ref[...], pl.program_id(0) == 0)
        res = _prep_fn(p, pprev, mu[...], w0[...], w2p[...], a0[...], a2p[...], g2[...], k_k[...], k_a[...])
        for o_ref, val in zip(outs, res):
            o_ref[...] = val

    row = pl.BlockSpec((tm, RW), lambda i: (i, 0))
    return pl.pallas_call(
        body, name="rwkv_prep", grid=(t // tm,),
        in_specs=[pl.BlockSpec((tm, SHIFT_COLS), lambda i: (i, 0)),
                  pl.BlockSpec((8, SHIFT_COLS), lambda i: (jnp.maximum(i * (tm // 8) - 1, 0), 0))] + _prep_specs(tm),
        out_specs=[row] * 7,
        out_shape=[jax.ShapeDtypeStruct((t, RW), F32)] * 7,
        compiler_params=_params(("parallel",)),
    )(proj, proj, *pw)


def _pairs(ref):
    return jnp.stack([ref[:, 128 * p:128 * (p + 1)] for p in range(N_PAIR)], axis=0)


def _wkv_fwd(r, lw, k2, v, kk, a):
    t = r.shape[0]
    nc = t // CHUNK

    def body(r_ref, lw_ref, k_ref, v_ref, kk_ref, a_ref, y_ref, s_ref, st):
        @pl.when(pl.program_id(0) == 0)
        def _():
            st[...] = jnp.zeros_like(st)

        s0 = st[...]
        s_ref[0] = s0
        y, s1 = _wkv_chunk_fn(s0, *[_pairs(ref) for ref in (r_ref, lw_ref, k_ref, v_ref, kk_ref, a_ref)])
        for p in range(N_PAIR):
            y_ref[:, 128 * p:128 * (p + 1)] = y[p]
        st[...] = s1

    blk = pl.BlockSpec((CHUNK, RW), lambda c: (c, 0))
    return pl.pallas_call(
        body, name="wkv_fwd", grid=(nc,),
        in_specs=[blk] * 6,
        out_specs=[blk, pl.BlockSpec((1, N_PAIR, 128, 128), lambda c: (c, 0, 0, 0))],
        out_shape=[jax.ShapeDtypeStruct((t, RW), F32), jax.ShapeDtypeStruct((nc, N_PAIR, 128, 128), F32)],
        scratch_shapes=[pltpu.VMEM((N_PAIR, 128, 128), F32)],
        compiler_params=_params(("arbitrary",)),
    )(r, lw, k2, v, kk, a)


_POST_TM = 256


def _post_fwd(y, r, k2, v, g, lnw, lnb, rk):
    t = y.shape[0]
    tm = _POST_TM

    def body(y_ref, r_ref, k_ref, v_ref, g_ref, lnw_ref, lnb_ref, rk_ref, o_ref):
        o_ref[...] = _post_fn(y_ref[...], r_ref[...], k_ref[...], v_ref[...], g_ref[...],
                              lnw_ref[...], lnb_ref[...], rk_ref[...]).astype(BF16)

    row = pl.BlockSpec((tm, RW), lambda i: (i, 0))
    vec = pl.BlockSpec((1, RW), lambda i: (0, 0))
    return pl.pallas_call(
        body, name="rwkv_post", grid=(t // tm,),
        in_specs=[row] * 5 + [vec] * 3, out_specs=row,
        out_shape=jax.ShapeDtypeStruct((t, RW), BF16),
        compiler_params=_params(("parallel",)),
    )(y, r, k2, v, g, lnw, lnb, rk)


ATTN_GROUP = 2


def _dilated_rows(d, r, n):
    if d == 1:
        return pl.ds(pl.multiple_of(n * ATTN_BLOCK, ATTN_BLOCK), ATTN_BLOCK)
    return pl.ds(r + n * (ATTN_BLOCK * d), ATTN_BLOCK, stride=d)


def _for_each_sequence(t, unit):
    for di, d in enumerate(DILATIONS):

        @pl.when(pl.program_id(1) == di)
        def _(di=di, d=d):
            nb = t // (ATTN_BLOCK * d)
            if d == 1:
                unit(di, [(d, 0, 0)], False)
                unit(di, [(d, 0, 1)], True)
                lax.fori_loop(1, nb // 2, lambda k, c: (unit(di, [(d, 0, 2 * k), (d, 0, 2 * k + 1)], True), c)[1], 0)
            else:

                def residues(r, carry):
                    unit(di, [(d, r, 0), (d, r + d // 2, 0)], False)
                    if nb > 1:
                        lax.fori_loop(1, nb, lambda n, c: (unit(di, [(d, r, n), (d, r + d // 2, n)], True), c)[1], 0)
                    return carry

                lax.fori_loop(0, d // 2, residues, 0)


def _take(ref, lead, rows_list):
    return jnp.stack([ref.at[(*lead, g)][rows, :] for rows in rows_list for g in range(ATTN_GROUP)], axis=0)


def _put(ref, lead, rows_list, val, add=False):
    k = 0
    for rows in rows_list:
        for g in range(ATTN_GROUP):
            if add:
                ref.at[(*lead, g)][rows, :] += val[k]
            else:
                ref.at[(*lead, g)][rows, :] = val[k]
            k += 1


def _attn_fwd(qkv):
    t = qkv.shape[2]

    def body(q_ref, k_ref, v_ref, o_ref, l_ref):
        def unit(di, places, has_prev):
            cur = [_dilated_rows(d, r, n) for d, r, n in places]
            args = [_take(ref, (0,), cur) for ref in (q_ref, k_ref, v_ref)]
            if has_prev:
                prv = [_dilated_rows(d, r, n - 1) for d, r, n in places]
                args += [_take(ref, (0,), prv) for ref in (k_ref, v_ref)]
            o, lse = _attn_block_fn(*args)
            _put(o_ref, (0,), cur, o)
            _put(l_ref, (0,), cur, lse)

        _for_each_sequence(t, unit)

    spec = lambda j: pl.BlockSpec((1, ATTN_GROUP, t, 128), lambda i, b: (j, i, 0, 0))
    out = pl.BlockSpec((1, ATTN_GROUP, t, 128), lambda i, b: (b, i, 0, 0))
    return pl.pallas_call(
        body, name="attn_fwd", grid=(N_PAIR // ATTN_GROUP, len(DILATIONS)),
        in_specs=[spec(0), spec(1), spec(2)], out_specs=[out, out],
        out_shape=[jax.ShapeDtypeStruct((3, N_PAIR, t, 128), F32)] * 2,
        compiler_params=_params(("parallel", "arbitrary")),
    )(qkv, qkv, qkv)


_COMB_TM = 256


def _combine_fwd(o, l, og):
    t = o.shape[2]
    tm = _COMB_TM

    def body(o_ref, l_ref, og_ref, y_ref):
        for p in range(N_PAIR):
            cols = slice(128 * p, 128 * (p + 1))
            y_ref[:, cols] = _combine_fn(o_ref[0, p], o_ref[1, p], o_ref[2, p], l_ref[0, p], l_ref[1, p], l_ref[2, p],
                                         og_ref[:, cols]).astype(BF16)

    blk = pl.BlockSpec((3, N_PAIR, tm, 128), lambda i: (0, 0, i, 0))
    return pl.pallas_call(
        body, name="attn_combine", grid=(t // tm,),
        in_specs=[blk, blk, pl.BlockSpec((1, RW), lambda i: (0, 0))], out_specs=pl.BlockSpec((tm, RW), lambda i: (i, 0)),
        out_shape=jax.ShapeDtypeStruct((t, RW), BF16),
        compiler_params=_params(("parallel",)),
    )(o, l, og)


def _out_proj(x, ycat, wout, g2):
    t = x.shape[0]
    tm = 256

    def body(x_ref, y_ref, w_ref, g_ref, x1_ref, h_ref):
        x1 = x_ref[...] + _dot(y_ref[...], w_ref[...])
        x1_ref[...] = x1
        h_ref[...] = _rms_fwd(x1, g_ref[...]).astype(BF16)

    row = pl.BlockSpec((tm, D_MODEL), lambda i: (i, 0))
    return pl.pallas_call(
        body, name="out_proj", grid=(t // tm,),
        in_specs=[row, row, pl.BlockSpec((D_MODEL, D_MODEL), lambda i: (0, 0)), pl.BlockSpec((1, D_MODEL), lambda i: (0, 0))],
        out_specs=[row, row],
        out_shape=[jax.ShapeDtypeStruct((t, D_MODEL), F32), jax.ShapeDtypeStruct((t, D_MODEL), BF16)],
        compiler_params=_params(("parallel",)),
    )(x, ycat, wout, g2)


def _ffn_up(h2, wg, wu):
    t = h2.shape[0]
    tm = 512

    def body(h_ref, wg_ref, wu_ref, gt_ref, up_ref, act_ref):
        h = h_ref[...]
        gt = _dot_nt(h, wg_ref[...])
        up = _dot_nt(h, wu_ref[...])
        gt_ref[...] = gt.astype(BF16)
        up_ref[...] = up.astype(BF16)
        act_ref[...] = (gt * _sigmoid(gt) * up).astype(BF16)

    wide = pl.BlockSpec((tm, D_FF), lambda i: (i, 0))
    wsp = pl.BlockSpec((D_FF, D_MODEL), lambda i: (0, 0))
    return pl.pallas_call(
        body, name="ffn_up", grid=(t // tm,),
        in_specs=[pl.BlockSpec((tm, D_MODEL), lambda i: (i, 0)), wsp, wsp],
        out_specs=[wide, wide, wide],
        out_shape=[jax.ShapeDtypeStruct((t, D_FF), BF16)] * 3,
        compiler_params=_params(("parallel",)),
    )(h2, wg, wu)


def _ffn_down_loss(x1, act, wd, gf, tgt):
    t = x1.shape[0]
    tm = 256

    def body(x1_ref, a_ref, w_ref, g_ref, t_ref, dx_ref, dxb_ref, loss_ref, dg_ref):
        first = pl.program_id(0) == 0
        x2 = x1_ref[...] + _dot(a_ref[...], w_ref[...])
        g = g_ref[...]
        diff = _rms_fwd(x2, g) - t_ref[...]
        lrow = 0.5 * jnp.sum(_colsum8(diff * diff), axis=1, keepdims=True) * (1.0 / D_MODEL)
        _acc(loss_ref, jnp.broadcast_to(lrow, (8, 128)), first)
        dx2, dgr = _rms_bwd(diff * (1.0 / D_MODEL), x2, g)
        dx_ref[...] = dx2
        dxb_ref[...] = dx2.astype(BF16)
        _acc(dg_ref, _colsum8(dgr), first)

    row = pl.BlockSpec((tm, D_MODEL), lambda i: (i, 0))
    return pl.pallas_call(
        body, name="ffn_down_loss", grid=(t // tm,),
        in_specs=[row, pl.BlockSpec((tm, D_FF), lambda i: (i, 0)), pl.BlockSpec((D_FF, D_MODEL), lambda i: (0, 0)),
                  pl.BlockSpec((1, D_MODEL), lambda i: (0, 0)), row],
        out_specs=[row, row, pl.BlockSpec((8, 128), lambda i: (0, 0)), pl.BlockSpec((8, D_MODEL), lambda i: (0, 0))],
        out_shape=[jax.ShapeDtypeStruct((t, D_MODEL), F32), jax.ShapeDtypeStruct((t, D_MODEL), BF16),
                   jax.ShapeDtypeStruct((8, 128), F32), jax.ShapeDtypeStruct((8, D_MODEL), F32)],
        compiler_params=_params(("arbitrary",)),
    )(x1, act, wd, gf, tgt)


def _ffn_bwd_act(dx2b, wd, gt, up):
    t = dx2b.shape[0]
    tm = 512

    def body(dx_ref, w_ref, gt_ref, up_ref, dgt_ref, dup_ref):
        dact = _dot_nt(dx_ref[...], w_ref[...])
        gt = gt_ref[...].astype(F32)
        sg = _sigmoid(gt)
        dgt_ref[...] = (dact * up_ref[...].astype(F32) * sg * (1.0 + gt * (1.0 - sg))).astype(BF16)
        dup_ref[...] = (dact * gt * sg).astype(BF16)

    wide = pl.BlockSpec((tm, D_FF), lambda i: (i, 0))
    return pl.pallas_call(
        body, name="ffn_bwd_act", grid=(t // tm,),
        in_specs=[pl.BlockSpec((tm, D_MODEL), lambda i: (i, 0)), pl.BlockSpec((D_FF, D_MODEL), lambda i: (0, 0)), wide, wide],
        out_specs=[wide, wide],
        out_shape=[jax.ShapeDtypeStruct((t, D_FF), BF16)] * 2,
        compiler_params=_params(("parallel",)),
    )(dx2b, wd, gt, up)


def _ffn_bwd_h(dgt, dup, wg, wu, dx2, x1, g2, wout):
    t = dgt.shape[0]
    tm = 256

    def body(dgt_ref, dup_ref, wg_ref, wu_ref, dx2_ref, x1_ref, g_ref, wo_ref, dx1_ref, dx1b_ref, dya_ref, dyb_ref, dg_ref):
        dh = _dot(dgt_ref[...], wg_ref[...]) + _dot(dup_ref[...], wu_ref[...])
        dxn, dgr = _rms_bwd(dh, x1_ref[...], g_ref[...])
        dx1 = dx2_ref[...] + dxn
        dx1_ref[...] = dx1
        dx1b = dx1.astype(BF16)
        dx1b_ref[...] = dx1b
        dy = _dot_nt(dx1b, wo_ref[...])
        dya_ref[...] = dy[:, :RW]
        dyb_ref[...] = dy[:, RW:]
        _acc(dg_ref, _colsum8(dgr), pl.program_id(0) == 0)

    wide = pl.BlockSpec((tm, D_FF), lambda i: (i, 0))
    row = pl.BlockSpec((tm, D_MODEL), lambda i: (i, 0))
    half = pl.BlockSpec((tm, RW), lambda i: (i, 0))
    wsp = pl.BlockSpec((D_FF, D_MODEL), lambda i: (0, 0))
    return pl.pallas_call(
        body, name="ffn_bwd_h", grid=(t // tm,),
        in_specs=[wide, wide, wsp, wsp, row, row, pl.BlockSpec((1, D_MODEL), lambda i: (0, 0)),
                  pl.BlockSpec((D_MODEL, D_MODEL), lambda i: (0, 0))],
        out_specs=[row, row, half, half, pl.BlockSpec((8, D_MODEL), lambda i: (0, 0))],
        out_shape=[jax.ShapeDtypeStruct((t, D_MODEL), F32), jax.ShapeDtypeStruct((t, D_MODEL), BF16),
                   jax.ShapeDtypeStruct((t, RW), F32), jax.ShapeDtypeStruct((t, RW), F32),
                   jax.ShapeDtypeStruct((8, D_MODEL), F32)],
        compiler_params=_params(("arbitrary",)),
    )(dgt, dup, wg, wu, dx2, x1, g2, wout)


def _wgrad(a, b, tk, tn, name):
    t, kdim = a.shape
    ndim = b.shape[1]

    def body(a_ref, b_ref, o_ref):
        o_ref[...] = _dot_tn(a_ref[...], b_ref[...])

    return pl.pallas_call(
        body, name=name, grid=(kdim // tk, ndim // tn),
        in_specs=[pl.BlockSpec((t, tk), lambda i, j: (0, i)), pl.BlockSpec((t, tn), lambda i, j: (0, j))],
        out_specs=pl.BlockSpec((tk, tn), lambda i, j: (i, j)),
        out_shape=jax.ShapeDtypeStruct((kdim, ndim), F32),
        compiler_params=_params(("parallel", "parallel")),
    )(a, b)


def _post_bwd(dya, y, r, k2, v, g, lnw, lnb, rk):
    t = y.shape[0]
    tm = _POST_TM

    def body(d_ref, y_ref, r_ref, k_ref, v_ref, g_ref, lnw_ref, lnb_ref, rk_ref,
             dy_ref, dr_ref, dk_ref, dv_ref, dg_ref, dlnw_ref, dlnb_ref, drk_ref):
        first = pl.program_id(0) == 0
        ones = jnp.ones((tm, 1), F32)
        prim = (y_ref[...], r_ref[...], k_ref[...], v_ref[...], g_ref[...],
                ones * lnw_ref[...], ones * lnb_ref[...], ones * rk_ref[...])
        _, vjp = jax.vjp(_post_fn, *prim)
        dy, dr, dk, dv, dg, dlnw, dlnb, drk = vjp(d_ref[...])
        dy_ref[...] = dy
        dr_ref[...] = dr
        dk_ref[...] = dk
        dv_ref[...] = dv
        dg_ref[...] = dg
        _acc(dlnw_ref, _colsum8(dlnw), first)
        _acc(dlnb_ref, _colsum8(dlnb), first)
        _acc(drk_ref, _colsum8(drk), first)

    row = pl.BlockSpec((tm, RW), lambda i: (i, 0))
    vec = pl.BlockSpec((1, RW), lambda i: (0, 0))
    part = pl.BlockSpec((8, RW), lambda i: (0, 0))
    return pl.pallas_call(
        body, name="rwkv_post_bwd", grid=(t // tm,),
        in_specs=[row] * 6 + [vec] * 3, out_specs=[row] * 5 + [part] * 3,
        out_shape=[jax.ShapeDtypeStruct((t, RW), F32)] * 5 + [jax.ShapeDtypeStruct((8, RW), F32)] * 3,
        compiler_params=_params(("arbitrary",)),
    )(dya, y, r, k2, v, g, lnw, lnb, rk)


def _wkv_bwd(dy, s0s, r, lw, k2, v, kk, a):
    t = r.shape[0]
    nc = t // CHUNK

    def body(dy_ref, s_ref, r_ref, lw_ref, k_ref, v_ref, kk_ref, a_ref,
             dr_ref, dlw_ref, dk_ref, dv_ref, dkk_ref, da_ref, ds):
        @pl.when(pl.program_id(0) == 0)
        def _():
            ds[...] = jnp.zeros_like(ds)

        _, vjp = jax.vjp(_wkv_chunk_fn, s_ref[0],
                         *[_pairs(ref) for ref in (r_ref, lw_ref, k_ref, v_ref, kk_ref, a_ref)])
        res = vjp((_pairs(dy_ref), ds[...]))
        ds[...] = res[0]
        for ref, val in zip((dr_ref, dlw_ref, dk_ref, dv_ref, dkk_ref, da_ref), res[1:]):
            for p in range(N_PAIR):
                ref[:, 128 * p:128 * (p + 1)] = val[p]

    blk = pl.BlockSpec((CHUNK, RW), lambda c: (nc - 1 - c, 0))
    return pl.pallas_call(
        body, name="wkv_bwd", grid=(nc,),
        in_specs=[blk, pl.BlockSpec((1, N_PAIR, 128, 128), lambda c: (nc - 1 - c, 0, 0, 0))] + [blk] * 6,
        out_specs=[blk] * 6,
        out_shape=[jax.ShapeDtypeStruct((t, RW), F32)] * 6,
        scratch_shapes=[pltpu.VMEM((N_PAIR, 128, 128), F32)],
        compiler_params=_params(("arbitrary",)),
    )(dy, s0s, r, lw, k2, v, kk, a)


def _prep_bwd(proj, pw, douts):
    t = proj.shape[0]
    tm = _PREP_TM
    nt = t // tm

    def body(p_ref, l8_ref, mu, w0, w2p, a0, a2p, g2, k_k, k_a, dr, dr2, dlw, dk2, dk22, dv, dv2, dkk, da, dg,
             dp_ref, dmu_ref, dw0_ref, dw2_ref, da0_ref, da2_ref, dg2_ref, dkk_ref, dka_ref, carry):
        i = pl.program_id(0)
        first = i == 0

        @pl.when(first)
        def _():
            carry[...] = jnp.zeros_like(carry)

        p = p_ref[...]
        pprev = _shifted(p, l8_ref[...], i == nt - 1)
        ones = jnp.ones((tm, 1), F32)
        prim = (p, pprev, ones * mu[...], ones * w0[...], w2p[...], ones * a0[...], a2p[...], g2[...],
                ones * k_k[...], ones * k_a[...])
        _, vjp = jax.vjp(_prep_fn, *prim)
        dp, dpp, dmu, dw0, dw2, da0, da2, dg2, dkk_, dka = vjp(
            (dr[...] + dr2[...], dlw[...], dk2[...] + dk22[...], dv[...] + dv2[...], dkk[...], da[...], dg[...]))
        up = pltpu.roll(dpp, tm - 1, axis=0)
        rid = lax.broadcasted_iota(jnp.int32, dpp.shape, 0)
        dp_ref[...] = dp + jnp.where(rid == tm - 1, carry[0:1, :], up)
        carry[...] = jnp.broadcast_to(dpp[0:1, :], carry.shape)
        _acc(dmu_ref, _colsum8(dmu), first)
        _acc(dw0_ref, _colsum8(dw0), first)
        _acc(dw2_ref, dw2, first)
        _acc(da0_ref, _colsum8(da0), first)
        _acc(da2_ref, da2, first)
        _acc(dg2_ref, dg2, first)
        _acc(dkk_ref, _colsum8(dkk_), first)
        _acc(dka_ref, _colsum8(dka), first)

    rev = lambda i: (nt - 1 - i, 0)
    row = pl.BlockSpec((tm, RW), rev)
    part = lambda n: pl.BlockSpec((8, n), lambda i: (0, 0))
    mat = pl.BlockSpec((128, RW), lambda i: (0, 0))
    return pl.pallas_call(
        body, name="rwkv_prep_bwd", grid=(nt,),
        in_specs=[pl.BlockSpec((tm, SHIFT_COLS), rev),
                  pl.BlockSpec((8, SHIFT_COLS), lambda i: (jnp.maximum((nt - 1 - i) * (tm // 8) - 1, 0), 0))]
                 + _prep_specs(tm) + [row] * 10,
        out_specs=[pl.BlockSpec((tm, SHIFT_COLS), rev), part(SHIFT_COLS), part(RW), mat, part(RW), mat, mat,
                   part(RW), part(RW)],
        out_shape=[jax.ShapeDtypeStruct((t, SHIFT_COLS), F32), jax.ShapeDtypeStruct((8, SHIFT_COLS), F32),
                   jax.ShapeDtypeStruct((8, RW), F32), jax.ShapeDtypeStruct((128, RW), F32),
                   jax.ShapeDtypeStruct((8, RW), F32), jax.ShapeDtypeStruct((128, RW), F32),
                   jax.ShapeDtypeStruct((128, RW), F32), jax.ShapeDtypeStruct((8, RW), F32),
                   jax.ShapeDtypeStruct((8, RW), F32)],
        scratch_shapes=[pltpu.VMEM((8, SHIFT_COLS), F32)],
        compiler_params=_params(("arbitrary",)),
    )(proj, proj, *pw, *douts)


def _combine_bwd(dyb, o, l, og):
    t = dyb.shape[0]
    tm = _COMB_TM

    def body(d_ref, o_ref, l_ref, og_ref, do_ref, dl_ref, dog_ref):
        ones = jnp.ones((tm, 1), F32)
        dog = []
        for p in range(N_PAIR):
            cols = slice(128 * p, 128 * (p + 1))
            _, vjp = jax.vjp(_combine_fn, o_ref[0, p], o_ref[1, p], o_ref[2, p], l_ref[0, p], l_ref[1, p], l_ref[2, p],
                             ones * og_ref[:, cols])
            res = vjp(d_ref[:, cols])
            for b in range(3):
                do_ref[b, p] = res[b]
                dl_ref[b, p] = res[3 + b]
            dog.append(_colsum8(res[6]))
        _acc(dog_ref, jnp.concatenate(dog, axis=1), pl.program_id(0) == 0)

    blk = pl.BlockSpec((3, N_PAIR, tm, 128), lambda i: (0, 0, i, 0))
    return pl.pallas_call(
        body, name="attn_combine_bwd", grid=(t // tm,),
        in_specs=[pl.BlockSpec((tm, RW), lambda i: (i, 0)), blk, blk, pl.BlockSpec((1, RW), lambda i: (0, 0))],
        out_specs=[blk, blk, pl.BlockSpec((8, RW), lambda i: (0, 0))],
        out_shape=[jax.ShapeDtypeStruct((3, N_PAIR, t, 128), F32)] * 2 + [jax.ShapeDtypeStruct((8, RW), F32)],
        compiler_params=_params(("arbitrary",)),
    )(dyb, o, l, og)


def _attn_bwd(do, dl, qkv):
    t = qkv.shape[2]

    def body(do_ref, dl_ref, q_ref, k_ref, v_ref, dq_ref, dk_ref, dv_ref):
        @pl.when(pl.program_id(1) == 0)
        def _():
            for ref in (dq_ref, dk_ref, dv_ref):
                ref[...] = jnp.zeros_like(ref)

        def unit(di, places, has_prev):
            cur = [_dilated_rows(d, r, n) for d, r, n in places]
            args = [_take(ref, (0,), cur) for ref in (q_ref, k_ref, v_ref)]
            if has_prev:
                prv = [_dilated_rows(d, r, n - 1) for d, r, n in places]
                args += [_take(ref, (0,), prv) for ref in (k_ref, v_ref)]
            _, vjp = jax.vjp(_attn_block_fn, *args)
            res = vjp((_take(do_ref, (0,), cur), _take(dl_ref, (0,), cur)))
            _put(dq_ref, (), cur, res[0], add=True)
            _put(dk_ref, (), cur, res[1], add=True)
            _put(dv_ref, (), cur, res[2], add=True)
            if has_prev:
                _put(dk_ref, (), prv, res[3], add=True)
                _put(dv_ref, (), prv, res[4], add=True)

        _for_each_sequence(t, unit)

    spec = lambda j: pl.BlockSpec((1, ATTN_GROUP, t, 128), lambda i, b: (j, i, 0, 0))
    branch = pl.BlockSpec((1, ATTN_GROUP, t, 128), lambda i, b: (b, i, 0, 0))
    out = pl.BlockSpec((ATTN_GROUP, t, 128), lambda i, b: (i, 0, 0))
    return pl.pallas_call(
        body, name="attn_bwd", grid=(N_PAIR // ATTN_GROUP, len(DILATIONS)),
        in_specs=[branch, branch, spec(0), spec(1), spec(2)], out_specs=[out] * 3,
        out_shape=[jax.ShapeDtypeStruct((N_PAIR, t, 128), F32)] * 3,
        compiler_params=_params(("parallel", "arbitrary")),
    )(do, dl, qkv, qkv, qkv)


def _in_proj_bwd(dpa, dq, dk, dv, win, x, g1, dx1):
    t = x.shape[0]
    tm = 256

    def body(dpa_ref, dq_ref, dk_ref, dv_ref, w_ref, x_ref, g_ref, dx1_ref, dproj_ref, dx_ref, dg_ref):
        parts = [dpa_ref[...]] + [ref[p] for ref in (dq_ref, dk_ref, dv_ref) for p in range(N_PAIR)]
        dproj = jnp.concatenate([z.astype(BF16) for z in parts], axis=1)
        dproj_ref[...] = dproj
        dh = _dot(dproj, w_ref[...])
        dxn, dgr = _rms_bwd(dh, x_ref[...], g_ref[...])
        dx_ref[...] = dx1_ref[...] + dxn
        _acc(dg_ref, _colsum8(dgr), pl.program_id(0) == 0)

    row = pl.BlockSpec((tm, D_MODEL), lambda i: (i, 0))
    pair = pl.BlockSpec((N_PAIR, tm, 128), lambda i: (0, i, 0))
    return pl.pallas_call(
        body, name="in_proj_bwd", grid=(t // tm,),
        in_specs=[pl.BlockSpec((tm, SHIFT_COLS), lambda i: (i, 0))] + [pair] * 3
                 + [pl.BlockSpec((IN_COLS, D_MODEL), lambda i: (0, 0)), row, pl.BlockSpec((1, D_MODEL), lambda i: (0, 0)), row],
        out_specs=[pl.BlockSpec((tm, IN_COLS), lambda i: (i, 0)), row, pl.BlockSpec((8, D_MODEL), lambda i: (0, 0))],
        out_shape=[jax.ShapeDtypeStruct((t, IN_COLS), BF16), jax.ShapeDtypeStruct((t, D_MODEL), F32),
                   jax.ShapeDtypeStruct((8, D_MODEL), F32)],
        compiler_params=_params(("arbitrary",)),
    )(dpa, dq, dk, dv, win, x, g1, dx1)


def _pad_lora(w, lo):
    z = jnp.zeros((64, RW), F32)
    return jnp.concatenate([w, z], axis=0) if lo == 0 else jnp.concatenate([z, w], axis=0)


def _local_step(x, tgt, win, vecs, w2, a2, g2m, get_rest, send_rest):
    pw = (vecs["mu_shift"], vecs["decay_w0"], _pad_lora(w2, 0), vecs["iclr_a0"], _pad_lora(a2, 64), g2m,
          vecs["k_k"], vecs["k_a"])
    h, proj, qkv = _in_proj(x, vecs["mix_norm_g"], win)
    r, lw, k2, v, kk, a, g = _prep_fwd(proj, pw)
    y, s0s = _wkv_fwd(r, lw, k2, v, kk, a)
    ya = _post_fwd(y, r, k2, v, g, vecs["ln_x_w"], vecs["ln_x_b"], vecs["r_k"])
    o_att, l_att = _attn_fwd(qkv)
    yb = _combine_fwd(o_att, l_att, vecs["attn_out_g"])

    wout, wg, wu, wd = get_rest(yb)
    ycat = jnp.concatenate([ya, yb], axis=1)
    x1, h2 = _out_proj(x, ycat, wout, vecs["ffn_norm_g"])
    gt, up, act = _ffn_up(h2, wg, wu)
    dx2, dx2b, loss8, dgf = _ffn_down_loss(x1, act, wd, vecs["final_norm_g"], tgt)

    dgt, dup = _ffn_bwd_act(dx2b, wd, gt, up)
    dx1, dx1b, dya, dyb, dg2n = _ffn_bwd_h(dgt, dup, wg, wu, dx2, x1, vecs["ffn_norm_g"], wout)
    gw = {
        "w_down": _wgrad(act, dx2b, 1408, 1024, "wgrad_down"),
        "w_gate": _wgrad(dgt, h2, 1408, 1024, "wgrad_gate"),
        "w_up": _wgrad(dup, h2, 1408, 1024, "wgrad_up"),
        "w_out": _wgrad(ycat, dx1b, 1024, 1024, "wgrad_out"),
    }

    lnw = vecs["ln_x_w"] + send_rest(gw)[0, 0]
    dy, dr_p, dk2_p, dv_p, dg, dlnw, dlnb, drk = _post_bwd(dya, y, r, k2, v, g, lnw, vecs["ln_x_b"], vecs["r_k"])
    dr_s, dlw, dk2_s, dv_s, dkk, da = _wkv_bwd(dy, s0s, r, lw, k2, v, kk, a)
    dpa, dmu, dw0, dw2p, da0, da2p, dg2m, dk_k, dk_a = _prep_bwd(
        proj, pw, (dr_p, dr_s, dlw, dk2_p, dk2_s, dv_p, dv_s, dkk, da, dg))

    do_att, dl_att, dog = _combine_bwd(dyb, o_att, l_att, vecs["attn_out_g"])
    dq, dk, dv = _attn_bwd(do_att, dl_att, qkv)
    dproj, dx, dg1 = _in_proj_bwd(dpa, dq, dk, dv, win, x, vecs["mix_norm_g"], dx1)
    gw["w_in"] = _wgrad(dproj, h, 1664, 1024, "wgrad_in")
    gw["decay_w2"] = dw2p[:64]
    gw["iclr_a2"] = da2p[64:]
    gw["gate_g2"] = dg2m
    gv = {"mix_norm_g": dg1, "mu_shift": dmu, "decay_w0": dw0, "iclr_a0": da0, "k_k": dk_k, "k_a": dk_a, "r_k": drk,
          "ln_x_w": dlnw, "ln_x_b": dlnb, "attn_out_g": dog, "ffn_norm_g": dg2n, "final_norm_g": dgf}
    return loss8, dx, gw, gv


N_CHIP = 4
N_DEV = 8
MATS = ("w_in", "w_out", "w_gate", "w_up", "w_down")
LORAS = ("decay_w2", "iclr_a2", "gate_g2")
VECS = (("mix_norm_g", 1024), ("mu_shift", 1792), ("decay_w0", 512), ("iclr_a0", 512), ("k_k", 512), ("k_a", 512),
        ("r_k", 512), ("ln_x_w", 512), ("ln_x_b", 512), ("attn_out_g", 512), ("ffn_norm_g", 1024),
        ("final_norm_g", 1024))
N_VEC = sum(n for _, n in VECS)
N_SMALL = N_VEC + 128
ANY = pl.BlockSpec(memory_space=pl.ANY)


def _flip(v, f):
    return 1 - v if f else v


class _Me:
    def __init__(self, mode):
        x, y, c = lax.axis_index("x"), lax.axis_index("y"), lax.axis_index("c")
        self.core, self.chip, self.dev = c, 2 * x + y, 4 * x + 2 * y + c
        self.sibling = (x, y, 1 - c)
        if mode == "chips":
            self.peers = [(px, py, c) for px, py in ((1 - x, y), (x, 1 - y), (1 - x, 1 - y))]
        else:
            self.peers = [(_flip(x, k & 4), _flip(y, k & 2), _flip(c, k & 1)) for k in range(1, N_DEV)]


def _half(core, rows):
    h = rows // 2
    return pl.ds(pl.multiple_of(core * h, h), h)


def _peer_copy(srcs, dsts, kinds, send_sems, recv_sems, me, j, i, incoming):
    px, py, pc = me.peers[j]
    pchip, pdev = 2 * px + py, 4 * px + 2 * py + pc
    src, dst, kind = srcs[i], dsts[i], kinds[i]
    if kind == "gather":
        rows = _half(me.core, src.shape[0])
        src, dst = src.at[rows], dst.at[pchip if incoming else me.chip, rows]
    elif kind == "scatter":
        src, dst = src.at[pchip, _half(pc, src.shape[1])], dst.at[pdev if incoming else me.dev]
    else:
        dst = dst.at[pdev if incoming else me.dev]
    n = len(srcs)
    return pltpu.make_async_remote_copy(src_ref=src, dst_ref=dst, send_sem=send_sems.at[n * j + i],
                                        recv_sem=recv_sems.at[n * j + i], device_id=(px, py, pc), device_id_type=MESH)


def _mode(kinds):
    return "chips" if kinds[0] == "gather" else "devs"


def _npeer(kinds):
    return N_CHIP - 1 if kinds[0] == "gather" else N_DEV - 1


def _swap_gathered(lands, name):
    n = len(lands)

    def body(*refs):
        dsts, send_sems, recv_sems = refs[n:2 * n], refs[2 * n], refs[2 * n + 1]
        me = _Me("chips")

        def copy(j, i, incoming):
            px, py, _ = me.peers[j]
            rows_out, rows_in = _half(me.core, dsts[i].shape[1]), _half(1 - me.core, dsts[i].shape[1])
            return pltpu.make_async_remote_copy(
                src_ref=dsts[i].at[2 * px + py, rows_out], dst_ref=dsts[i].at[2 * px + py, rows_in if incoming else rows_out],
                send_sem=send_sems.at[n * j + i], recv_sem=recv_sems.at[n * j + i], device_id=me.sibling, device_id_type=MESH)

        sends = [copy(j, i, False) for j in range(3) for i in range(n)]
        for cp in sends:
            cp.start()
        for j in range(3):
            for i in range(n):
                copy(j, i, True).wait_recv()
        for cp in sends:
            cp.wait_send()

    return pl.pallas_call(
        body, name=name, in_specs=[ANY] * n, out_specs=[ANY] * n,
        out_shape=[jax.ShapeDtypeStruct(l.shape, l.dtype) for l in lands],
        input_output_aliases={i: i for i in range(n)},
        scratch_shapes=[pltpu.SemaphoreType.DMA((3 * n,)), pltpu.SemaphoreType.DMA((3 * n,))],
    )(*lands)


def _join_halves(sums, name):
    n = len(sums)

    def body(*refs):
        dsts, send_sems, recv_sems = refs[n:2 * n], refs[2 * n], refs[2 * n + 1]
        me = _Me("chips")

        def copy(i, incoming):
            mine, other = _half(me.core, dsts[i].shape[0]), _half(1 - me.core, dsts[i].shape[0])
            return pltpu.make_async_remote_copy(src_ref=dsts[i].at[mine], dst_ref=dsts[i].at[other if incoming else mine],
                                                send_sem=send_sems.at[i], recv_sem=recv_sems.at[i],
                                                device_id=me.sibling, device_id_type=MESH)

        sends = [copy(i, False) for i in range(n)]
        for cp in sends:
            cp.start()
        for i in range(n):
            copy(i, True).wait_recv()
        for cp in sends:
            cp.wait_send()

    return pl.pallas_call(
        body, name=name, in_specs=[ANY] * n, out_specs=[ANY] * n,
        out_shape=[jax.ShapeDtypeStruct(s.shape, s.dtype) for s in sums],
        input_output_aliases={i: i for i in range(n)},
        scratch_shapes=[pltpu.SemaphoreType.DMA((n,)), pltpu.SemaphoreType.DMA((n,))],
    )(*sums)


HBM = pl.BlockSpec(memory_space=pltpu.HBM)
SEM = pl.BlockSpec(memory_space=pltpu.SEMAPHORE)
EFFECT = pltpu.SideEffectType.DATAFLOW_SIDE_EFFECTING


def _swap_start(arrs, lands, kinds, name):
    n = len(arrs)

    def body(*refs):
        srcs, dsts, send_sems, recv_sems, token = refs[:n], refs[n:2 * n], refs[2 * n], refs[2 * n + 1], refs[-1]
        me = _Me(_mode(kinds))
        for j in range(len(me.peers)):
            for i in range(n):
                _peer_copy(srcs, dsts, kinds, send_sems, recv_sems, me, j, i, False).start()
        token[...] = jnp.zeros_like(token)

    ns = _npeer(kinds) * n
    outs = pl.pallas_call(
        body, name=name,
        out_shape=(pltpu.SemaphoreType.DMA((ns,)), pltpu.SemaphoreType.DMA((ns,)),
                   *[pltpu.HBM(a.shape, a.dtype) for a in arrs], *[pltpu.HBM(l.shape, l.dtype) for l in lands],
                   jax.ShapeDtypeStruct((8, 128), F32)),
        in_specs=[HBM] * (2 * n), out_specs=(SEM, SEM, *[HBM] * (2 * n), pl.BlockSpec(memory_space=pltpu.VMEM)),
        input_output_aliases={k: 2 + k for k in range(2 * n)},
        compiler_params=pltpu.CompilerParams(has_side_effects=EFFECT),
    )(*[pltpu.with_memory_space_constraint(a, pltpu.HBM) for a in arrs],
      *[pltpu.with_memory_space_constraint(l, pltpu.HBM) for l in lands])
    return outs[0], outs[1], outs[2:2 + n], outs[2 + n:2 + 2 * n], outs[-1]


def _swap_wait(send_sems, recv_sems, srcs_thru, lands_thru, after, kinds, name):
    n = len(srcs_thru)

    def body(*refs):
        srcs, dsts, s_sems, r_sems = refs[:n], refs[n:2 * n], refs[2 * n], refs[2 * n + 1]
        me = _Me(_mode(kinds))
        for j in range(len(me.peers)):
            for i in range(n):
                cp = _peer_copy(srcs, dsts, kinds, s_sems, r_sems, me, j, i, True)
                cp.wait_send()
                cp.wait_recv()

    outs = pl.pallas_call(
        body, name=name,
        out_shape=tuple(pltpu.HBM(a.shape, a.dtype) for a in (*srcs_thru, *lands_thru)),
        in_specs=[HBM] * (2 * n) + [SEM, SEM, ANY], out_specs=tuple([HBM] * (2 * n)),
        input_output_aliases={k: k for k in range(2 * n)},
        compiler_params=pltpu.CompilerParams(has_side_effects=EFFECT),
    )(*srcs_thru, *lands_thru, send_sems, recv_sems, after)
    return outs[n:]


def _adamw(w, g, m, v):
    m = ADAM_B1 * m + (1.0 - ADAM_B1) * g
    v = ADAM_B2 * v + (1.0 - ADAM_B2) * (g * g)
    m_hat = m / (1.0 - ADAM_B1 ** ADAM_STEP)
    v_hat = v / (1.0 - ADAM_B2 ** ADAM_STEP)
    delta = -ADAM_LR * (m_hat / (jnp.sqrt(v_hat) + ADAM_EPS) + ADAM_WD * w)
    return delta, m, v


def _reduce8(rbuf, core, tr, name):
    _, h, cols = rbuf.shape

    def body(core_ref, r_ref, g_ref):
        g = r_ref[0].astype(F32)
        for s in range(1, N_DEV):
            g = g + r_ref[s].astype(F32)
        g_ref[...] = g

    return pl.pallas_call(
        body, name=name,
        grid_spec=pltpu.PrefetchScalarGridSpec(
            num_scalar_prefetch=1, grid=(h // tr,),
            in_specs=[pl.BlockSpec((N_DEV, tr, cols), lambda i, core_ref: (0, i, 0))],
            out_specs=pl.BlockSpec((tr, cols), lambda i, core_ref: (core_ref[0] * (h // tr) + i, 0))),
        out_shape=jax.ShapeDtypeStruct((2 * h, cols), F32),
        compiler_params=_params(("parallel",)),
    )(core, rbuf)


def _adamw_call(g, w, m, v, tr, name):
    _, rows, cols = w.shape

    def body(g_in, w_ref, m_ref, v_ref, g_ref, d_ref, nm_ref, nv_ref):
        g = g_in[...]
        g_ref[0] = g
        d_ref[0], nm_ref[0], nv_ref[0] = _adamw(w_ref[0], g, m_ref[0], v_ref[0])

    row = pl.BlockSpec((1, tr, cols), lambda i: (0, i, 0))
    return pl.pallas_call(
        body, name=name, grid=(rows // tr,),
        in_specs=[pl.BlockSpec((tr, cols), lambda i: (i, 0)), row, row, row], out_specs=[row] * 4,
        out_shape=[jax.ShapeDtypeStruct(w.shape, F32)] * 4,
        compiler_params=_params(("parallel",)),
    )(g, w, m, v)


def _rowsum_small(parts, loss8):
    def body(*refs):
        out = refs[-1]
        c0 = 0
        for ref in refs[:-1]:
            n = ref.shape[1]
            out[:, c0:c0 + n] = jnp.sum(ref[...], axis=0, keepdims=True)
            c0 += n

    return pl.pallas_call(body, name="rowsum_small", out_shape=jax.ShapeDtypeStruct((1, N_SMALL), F32))(*parts, loss8)


def _reduce_adamw_small(sbuf, ws, ms, vs):
    nv = len(ws)

    def body(*refs):
        s_ref, ins, outs = refs[0], refs[1:1 + 3 * nv], refs[1 + 3 * nv:]
        tot = s_ref[0]
        for s in range(1, N_DEV):
            tot = tot + s_ref[s]
        c0 = 0
        for i in range(nv):
            n = ins[i].shape[1]
            g = tot[:, c0:c0 + n]
            outs[i][...] = g
            outs[nv + i][...], outs[2 * nv + i][...], outs[3 * nv + i][...] = _adamw(
                ins[i][...], g, ins[nv + i][...], ins[2 * nv + i][...])
            c0 += n
        outs[-1][...] = tot[:, c0:]

    return pl.pallas_call(
        body, name="reduce_adamw_small",
        out_shape=[jax.ShapeDtypeStruct(a.shape, F32) for a in ws] * 4 + [jax.ShapeDtypeStruct((1, 128), F32)],
    )(sbuf, *ws, *ms, *vs)


_TRANSPOSED = ("w_in", "w_gate", "w_up")
_ROW_STACKED = MATS
_ADAM_TILE = {"w_in": 208, "w_out": 256, "w_gate": 176, "w_up": 176, "w_down": 176, "decay_w2": 64, "iclr_a2": 64,
              "gate_g2": 128}
_SUM_TILE = {"w_in": 208, "w_out": 128, "w_gate": 176, "w_up": 176, "w_down": 176, "decay_w2": 32, "iclr_a2": 32,
             "gate_g2": 64}


def _full(n, stacked):
    p, r, c = stacked.shape
    if n in _ROW_STACKED:
        return stacked.reshape(p * r, c)
    return jnp.transpose(stacked, (1, 0, 2)).reshape(r, p * c)


def _by_chip(n, full):
    if n in _ROW_STACKED:
        return full.reshape(N_CHIP, full.shape[0] // N_CHIP, full.shape[1])
    r, c = full.shape
    return jnp.transpose(full.reshape(r, N_CHIP, c // N_CHIP), (1, 0, 2))


def _with_own(land_shape, dtype, own, slot):
    return lax.dynamic_update_slice(lax.empty(land_shape, dtype), own[None], (slot,) + (0,) * own.ndim)


def kernel(x, mix_norm_g, w_in, mu_shift, decay_w0, decay_w2, iclr_a0, iclr_a2, gate_g2, k_k, k_a, r_k, ln_x_w, ln_x_b, attn_out_g, w_out, ffn_norm_g, w_gate, w_up, w_down, final_norm_g, loss_target, m_mix_norm_g, m_w_in, m_mu_shift, m_decay_w0, m_decay_w2, m_iclr_a0, m_iclr_a2, m_gate_g2, m_k_k, m_k_a, m_r_k, m_ln_x_w, m_ln_x_b, m_attn_out_g, m_w_out, m_ffn_norm_g, m_w_gate, m_w_up, m_w_down, m_final_norm_g, v_mix_norm_g, v_w_in, v_mu_shift, v_decay_w0, v_decay_w2, v_iclr_a0, v_iclr_a2, v_gate_g2, v_k_k, v_k_a, v_r_k, v_ln_x_w, v_ln_x_b, v_attn_out_g, v_w_out, v_ffn_norm_g, v_w_gate, v_w_up, v_w_down, v_final_norm_g):
    names = ("mix_norm_g", "w_in", "mu_shift", "decay_w0", "decay_w2", "iclr_a0", "iclr_a2", "gate_g2", "k_k", "k_a",
             "r_k", "ln_x_w", "ln_x_b", "attn_out_g", "w_out", "ffn_norm_g", "w_gate", "w_up", "w_down", "final_norm_g")
    w = dict(zip(names, (mix_norm_g, w_in, mu_shift, decay_w0, decay_w2, iclr_a0, iclr_a2, gate_g2, k_k, k_a, r_k,
                         ln_x_w, ln_x_b, attn_out_g, w_out, ffn_norm_g, w_gate, w_up, w_down, final_norm_g)))
    m = dict(zip(names, (m_mix_norm_g, m_w_in, m_mu_shift, m_decay_w0, m_decay_w2, m_iclr_a0, m_iclr_a2, m_gate_g2,
                         m_k_k, m_k_a, m_r_k, m_ln_x_w, m_ln_x_b, m_attn_out_g, m_w_out, m_ffn_norm_g, m_w_gate,
                         m_w_up, m_w_down, m_final_norm_g)))
    v = dict(zip(names, (v_mix_norm_g, v_w_in, v_mu_shift, v_decay_w0, v_decay_w2, v_iclr_a0, v_iclr_a2, v_gate_g2,
                         v_k_k, v_k_a, v_r_k, v_ln_x_w, v_ln_x_b, v_attn_out_g, v_w_out, v_ffn_norm_g, v_w_gate,
                         v_w_up, v_w_down, v_final_norm_g)))
    first = ("w_in",) + LORAS
    rest = ("w_out", "w_gate", "w_up", "w_down")
    xi, yi, ci = lax.axis_index("x"), lax.axis_index("y"), lax.axis_index("c")
    my_chip, my_dev = 2 * xi + yi, 4 * xi + 2 * yi + ci
    gather, scatter = ("gather",) * 4, ("scatter",) * 4

    sh = lambda z, n: jnp.transpose(z[0]) if n in _TRANSPOSED else z[0]
    mine = [sh(w["w_in"], "w_in").astype(BF16)] + [w[n][0] for n in LORAS]
    early = _swap_start(mine, [_with_own((N_CHIP,) + a.shape, a.dtype, a, my_chip) for a in mine], gather, "gather_first_start")
    wb = {n: (sh(w[n], n) + early[4][0, 0]).astype(BF16) for n in rest}
    lands = [_with_own((N_CHIP,) + wb[n].shape, BF16, wb[n], my_chip) for n in rest]
    ssem, rsem, srcs_thru, lands_thru, tok = _swap_start([wb[n] for n in rest], lands, gather, "gather_rest_start")
    got = _swap_wait(early[0], early[1], early[2], early[3], tok, gather, "gather_first_wait")
    win, w2, a2, g2m = (_full(n, z) for n, z in zip(first, _swap_gathered(got, "gather_first_halves")))

    vecs = {n: w[n].reshape(1, sz) for n, sz in VECS}
    vecs["mix_norm_g"] = vecs["mix_norm_g"] + tok[0, 0]

    def get_rest(after):
        halves = _swap_wait(ssem, rsem, srcs_thru, lands_thru, after, gather, "gather_rest_wait")
        return [_full(n, z) for n, z in zip(rest, _swap_gathered(halves, "gather_rest_halves"))]

    flight = []

    def my_half(g):
        h = g.shape[1] // 2
        return lax.dynamic_slice(g, (my_chip, ci * h, 0), (1, h, g.shape[2]))[0]

    def send_rest(gw):
        gs = [_by_chip(n, gw[n]).astype(BF16) for n in rest]
        into = [_with_own((N_DEV,) + my_half(g).shape, BF16, my_half(g), my_dev) for g in gs]
        flight.extend(_swap_start(gs, into, scatter, "exchange_rest_start"))
        return flight[4]

    loss8, dx, gw, gv = _local_step(x[0], loss_target[0], win, vecs, w2, a2, g2m, get_rest, send_rest)

    small = _rowsum_small([gv[n] for n, _ in VECS], loss8)
    gs = [_by_chip(n, gw[n]).astype(BF16) for n in first]
    into = [_with_own((N_DEV,) + my_half(g).shape, BF16, my_half(g), my_dev) for g in gs]
    into.append(_with_own((N_DEV,) + small.shape, F32, small, my_dev))
    last = _swap_start(gs + [small], into, scatter + ("all",), "exchange_first_start")

    core = jnp.reshape(ci, (1,)).astype(jnp.int32)

    def update(group, rbufs, tag):
        sums = [_reduce8(rb, core, _SUM_TILE[n], "reduce_" + n) for n, rb in zip(group, rbufs)]
        gsum = _join_halves(sums, "join_halves_" + tag)
        out = {}
        for n, g in zip(group, gsum):
            r = _adamw_call(g, sh(w[n], n)[None], sh(m[n], n)[None], sh(v[n], n)[None], _ADAM_TILE[n], "adamw_" + n)
            out[n] = [jnp.transpose(z[0])[None] for z in r] if n in _TRANSPOSED else r
        return out

    res = update(rest, _swap_wait(flight[0], flight[1], flight[2], flight[3], last[4], scatter, "exchange_rest_wait"), "rest")
    got = _swap_wait(last[0], last[1], last[2], last[3], res["w_down"][1], scatter + ("all",), "exchange_first_wait")
    res.update(update(first, got[:4], "first"))
    rows = lambda d: [d[n].reshape(1, sz) for n, sz in VECS]
    small_res = _reduce_adamw_small(got[4], rows(w), rows(m), rows(v))

    outs = []
    for k in range(4):
        piece = {n: r[k] for n, r in res.items()}
        for i, (n, _) in enumerate(VECS):
            piece[n] = small_res[k * len(VECS) + i].reshape(w[n].shape)
        outs.extend(piece[n] for n in names)
    return (small_res[-1][0, 0], dx[None], *outs)
```

```python
import jax
import jax.numpy as jnp
from jax import lax
from jax.experimental import pallas as pl
from jax.experimental.pallas import tpu as pltpu

F32 = jnp.float32
BF16 = jnp.bfloat16

D_MODEL = 1024
HEAD_DIM = 64
RW = 512
N_PAIR = RW // 128
SHIFT_COLS = 1792
IN_COLS = 3328
D_FF = 2816
NORM_EPS = 1e-6
GN_EPS = 64e-5
CHUNK = 64
SUB = 16
WKV_PASSES = 1
ATTN_PASSES = 1
ATTN_BLOCK = 128
DILATIONS = (1, 4, 16)
NEG = -1e30
ADAM_LR, ADAM_B1, ADAM_B2, ADAM_EPS, ADAM_WD, ADAM_STEP = 0.001, 0.9, 0.999, 1e-08, 0.01, 10
VMEM_LIMIT = 56 * 1024 * 1024
MESH = pl.DeviceIdType.MESH


def _params(sem=None, **kw):
    return pltpu.CompilerParams(dimension_semantics=sem, vmem_limit_bytes=VMEM_LIMIT, **kw)


def _dot(a, b, prec=None):
    return lax.dot_general(a, b, (((1,), (0,)), ((), ())), preferred_element_type=F32, precision=prec)


def _dot_nt(a, b, prec=None):
    return lax.dot_general(a, b, (((1,), (1,)), ((), ())), preferred_element_type=F32, precision=prec)


def _dot_tn(a, b, prec=None):
    return lax.dot_general(a, b, (((0,), (0,)), ((), ())), preferred_element_type=F32, precision=prec)


_FORMS = {"nn": ((1,), (0,)), "nt": ((1,), (1,)), "tn": ((0,), (0,))}


def _dg(a, b, form):
    if a.ndim == 3 or b.ndim == 3:
        nb = a.shape[0] if a.ndim == 3 else b.shape[0]
        return jnp.stack([_dg(a[i] if a.ndim == 3 else a, b[i] if b.ndim == 3 else b, form) for i in range(nb)], axis=0)
    return lax.dot_general(a, b, (_FORMS[form], ((), ())), preferred_element_type=F32)


def _split2(x):
    hi = x.astype(BF16)
    return hi, (x - hi.astype(F32)).astype(BF16)


def _split3(x):
    hi = x.astype(BF16)
    rest = x - hi.astype(F32)
    mid = rest.astype(BF16)
    return hi, mid, (rest - mid.astype(F32)).astype(BF16)


def _mm_raw(a, b, form, mode):
    if mode == 1:
        return _dg(a.astype(BF16), b.astype(BF16), form)
    if mode == 3:
        ah, al = _split2(a)
        bh, bl = _split2(b)
        return _dg(ah, bh, form) + (_dg(ah, bl, form) + _dg(al, bh, form))
    if mode == "L3":
        ab = a.astype(BF16)
        b1, b2, b3 = _split3(b)
        return _dg(ab, b1, form) + (_dg(ab, b2, form) + _dg(ab, b3, form))
    assert mode == "R3", mode
    bb = b.astype(BF16)
    a1, a2, a3 = _split3(a)
    return _dg(a1, bb, form) + (_dg(a2, bb, form) + _dg(a3, bb, form))


def _mm(a, b, form, mode):
    @jax.custom_vjp
    def f(a, b):
        return _mm_raw(a, b, form, mode)

    def fwd(a, b):
        return _mm_raw(a, b, form, mode), (a, b)

    def bwd(res, ct):
        a, b = res
        la = {1: 1, 3: 3, "L3": None, "R3": "R3"}[mode]
        lb = {1: 1, 3: 3, "L3": "L3", "R3": None}[mode]
        if form == "nn":
            da = None if la is None else _mm_raw(ct, b, "nt", la)
            db = None if lb is None else _mm_raw(a, ct, "tn", lb)
        elif form == "nt":
            da = None if la is None else _mm_raw(ct, b, "nn", la)
            db = None if lb is None else _mm_raw(ct, a, "tn", "R3" if lb == "L3" else lb)
        else:
            da = None if la is None else _mm_raw(b, ct, "nt", "L3" if la == "R3" else la)
            db = None if lb is None else _mm_raw(a, ct, "nn", lb)
        return (jnp.zeros_like(a) if da is None else da, jnp.zeros_like(b) if db is None else db)

    f.defvjp(fwd, bwd)
    return f(a, b)


def _seg_ones(n):
    r = lax.broadcasted_iota(jnp.int32, (n, n), 0) // HEAD_DIM
    c = lax.broadcasted_iota(jnp.int32, (n, n), 1) // HEAD_DIM
    return (r == c).astype(F32)


def _segsum(x, seg):
    return _mm(x, seg, "nn", "R3")


def _rms_fwd(x, g):
    rstd = lax.rsqrt(jnp.mean(x * x, axis=-1, keepdims=True) + NORM_EPS)
    return x * rstd * g


def _rms_bwd(dy, x, g):
    rstd = lax.rsqrt(jnp.mean(x * x, axis=-1, keepdims=True) + NORM_EPS)
    xn = x * rstd
    dxn = dy * g
    dx = rstd * (dxn - xn * jnp.mean(dxn * xn, axis=-1, keepdims=True))
    return dx, dy * xn


def _sigmoid(x):
    return 1.0 / (1.0 + jnp.exp(-x))


def _softplus(x):
    return jnp.maximum(x, 0.0) + jnp.log(1.0 + jnp.exp(-jnp.abs(x)))


def _acc(ref, val, first):
    @pl.when(first)
    def _():
        ref[...] = val

    @pl.when(jnp.logical_not(first))
    def _():
        ref[...] += val


def _colsum8(v):
    rows, n = v.shape
    return jnp.sum(v.reshape(rows // 8, 8, n), axis=0)


def _prep_fn(p, pprev, mu, w0, w2p, a0, a2p, g2, k_k, k_a):
    seg = _seg_ones(RW)
    ps = p + (pprev - p) * mu
    r = ps[:, 0:RW]
    k = ps[:, RW:2 * RW]
    v = ps[:, 2 * RW:3 * RW]
    xwa = ps[:, 3 * RW:3 * RW + 128]
    xg = ps[:, 3 * RW + 128:3 * RW + 256]
    wraw = -_softplus(-(w0 + _mm(jnp.tanh(xwa), w2p, "nn", 3))) - 0.5
    lw = -jnp.exp(wraw)
    a = _sigmoid(a0 + _mm(xwa, a2p, "nn", 3))
    g = _mm(_sigmoid(xg), g2, "nn", 3)
    kk = k * k_k
    kk = kk / jnp.maximum(jnp.sqrt(_segsum(kk * kk, seg)), 1e-12)
    k2 = k * (1.0 + (a - 1.0) * k_a)
    return r, lw, k2, v, kk, a, g


def _transposed(z):
    return jnp.stack([z[i].T for i in range(z.shape[0])], axis=0) if z.ndim == 3 else z.T


def _solve_unit_lower(lmat, rhs):
    c = lmat.shape[-1]
    row = lax.broadcasted_iota(jnp.int32, (c, c), 0)
    col = lax.broadcasted_iota(jnp.int32, (c, c), 1)
    eye = (row == col).astype(F32)
    ld = jnp.where(row // SUB == col // SUB, lmat, 0.0)
    lo = lmat - ld
    x = eye + ld
    m = ld
    mm = lambda p, q: _mm(p, q, "nn", WKV_PASSES)
    for _ in range(3):
        m = mm(m, m)
        x = x + mm(x, m)
    gw = mm(x, jnp.concatenate([lo, rhs], axis=-1))
    g, w = gw[..., :c], gw[..., c:]
    g2 = mm(g, g)
    w = w + mm(g2, w)
    return w + mm(g, w)


def _wkv_chunk_fn(s0, r, lw, k, v, kk, a):
    c = r.shape[-2]
    n = 2 * c
    row = lax.broadcasted_iota(jnp.int32, (n, n), 0)
    col = lax.broadcasted_iota(jnp.int32, (n, n), 1)
    same = (row // c) == (col // c)
    incl = jnp.logical_and(row >= col, same)
    strict = jnp.logical_and(row > col, same)
    sel = (lax.broadcasted_iota(jnp.int32, (n, 128), 0) // c) == (lax.broadcasted_iota(jnp.int32, (n, 128), 1) // HEAD_DIM)
    two = lambda z: jnp.concatenate([z, z], axis=-2)
    lw2 = two(lw)
    mm = lambda p_, q_, form: _mm(p_, q_, form, WKV_PASSES)
    cl = _mm(incl.astype(F32), lw2, "nn", "L3")
    p = jnp.exp(cl)
    pinv = jnp.exp(-cl)
    pprev = jnp.exp(cl - lw2)
    kk2 = two(kk)
    at = jnp.where(sel, -kk2 * pprev, 0.0)
    bt = jnp.where(sel, kk2 * two(a) * pinv, 0.0)
    kt = jnp.where(sel, two(k) * pinv, 0.0)
    rt = jnp.where(sel, two(r) * p, 0.0)
    vt = jnp.where(sel, two(v), 0.0)
    cat = jnp.concatenate
    bk = cat([bt, kt], axis=-2)
    abk = mm(at, bk, "nt")
    rbk = mm(rt, bk, "nt")
    ab, ak = jnp.where(strict, abk[..., :n], 0.0), jnp.where(strict, abk[..., n:], 0.0)
    rb, rk = jnp.where(incl, rbk[..., :n], 0.0), jnp.where(incl, rbk[..., n:], 0.0)
    s0t = _transposed(s0)
    u = _solve_unit_lower(ab, mm(cat([at, ak], axis=-1), cat([s0t, vt], axis=-2), "nn"))
    y2 = mm(cat([rt, rb, rk], axis=-1), cat([s0t, u, vt], axis=-2), "nn")
    plast = jnp.exp(jnp.sum(lw, axis=-2, keepdims=True))
    s1 = (s0 + mm(cat([u, vt], axis=-2), bk, "tn")) * plast
    r2 = lax.broadcasted_iota(jnp.int32, (128, 128), 0) // HEAD_DIM
    c2 = lax.broadcasted_iota(jnp.int32, (128, 128), 1) // HEAD_DIM
    return y2[..., :c, :] + y2[..., c:, :], jnp.where(r2 == c2, s1, 0.0)


def _post_fn(y, r, k2, v, g, lnw, lnb, rk):
    seg = _seg_ones(RW)
    mean = _segsum(y, seg) * (1.0 / HEAD_DIM)
    yc = y - mean
    var = _segsum(yc * yc, seg) * (1.0 / HEAD_DIM)
    yn = yc * lax.rsqrt(var + GN_EPS)
    out = yn * lnw + lnb + _segsum(r * k2 * rk, seg) * v
    return out * g


def _attn_block_fn(q, kc, vc, kp=None, vp=None):
    n = ATTN_BLOCK
    qi = lax.broadcasted_iota(jnp.int32, (n, n), 0)
    kj = lax.broadcasted_iota(jnp.int32, (n, n), 1)
    lane = lax.broadcasted_iota(jnp.int32, (1, 128), 1)
    scale = HEAD_DIM ** -0.5
    os_, ls_ = [], []
    for h in range(2):
        mh = (lane // HEAD_DIM) == h
        qh = jnp.where(mh, q, 0.0)
        sc = jnp.where(kj <= qi, _mm(qh, kc, "nt", ATTN_PASSES) * scale, NEG)
        m = jnp.max(sc, axis=-1, keepdims=True)
        if kp is not None:
            sp = jnp.where(kj >= qi, _mm(qh, kp, "nt", ATTN_PASSES) * scale, NEG)
            m = jnp.maximum(m, jnp.max(sp, axis=-1, keepdims=True))
        pc = jnp.exp(sc - m)
        den = jnp.sum(pc, axis=-1, keepdims=True)
        num = _mm(pc, vc, "nn", ATTN_PASSES)
        if kp is not None:
            pp = jnp.exp(sp - m)
            den = den + jnp.sum(pp, axis=-1, keepdims=True)
            num = num + _mm(pp, vp, "nn", ATTN_PASSES)
        os_.append(num / den)
        ls_.append(m + jnp.log(den))
    m0 = (lane // HEAD_DIM) == 0
    return jnp.where(m0, os_[0], os_[1]), jnp.where(m0, ls_[0], ls_[1])


def _combine_fn(o1, o2, o3, l1, l2, l3, og):
    seg = _seg_ones(o1.shape[-1])
    m = jnp.maximum(jnp.maximum(l1, l2), l3)
    e1, e2, e3 = jnp.exp(l1 - m), jnp.exp(l2 - m), jnp.exp(l3 - m)
    o = (e1 * o1 + e2 * o2 + e3 * o3) / (e1 + e2 + e3)
    o = o * lax.rsqrt(_segsum(o * o, seg) * (1.0 / HEAD_DIM) + NORM_EPS)
    return o * og


def _in_proj(x, g1, win):
    t = x.shape[0]
    tm = 512

    def body(x_ref, g_ref, w_ref, h_ref, pa_ref, qkv_ref):
        h = _rms_fwd(x_ref[...], g_ref[...]).astype(BF16)
        h_ref[...] = h
        proj = _dot_nt(h, w_ref[...])
        pa_ref[...] = proj[:, :SHIFT_COLS]
        for j in range(3):
            for p in range(N_PAIR):
                c0 = SHIFT_COLS + j * RW + p * 128
                qkv_ref[j, p] = proj[:, c0:c0 + 128]

    return pl.pallas_call(
        body, name="in_proj", grid=(t // tm,),
        in_specs=[pl.BlockSpec((tm, D_MODEL), lambda i: (i, 0)), pl.BlockSpec((1, D_MODEL), lambda i: (0, 0)),
                  pl.BlockSpec((IN_COLS, D_MODEL), lambda i: (0, 0))],
        out_specs=[pl.BlockSpec((tm, D_MODEL), lambda i: (i, 0)), pl.BlockSpec((tm, SHIFT_COLS), lambda i: (i, 0)),
                   pl.BlockSpec((3, N_PAIR, tm, 128), lambda i: (0, 0, i, 0))],
        out_shape=[jax.ShapeDtypeStruct((t, D_MODEL), BF16), jax.ShapeDtypeStruct((t, SHIFT_COLS), F32),
                   jax.ShapeDtypeStruct((3, N_PAIR, t, 128), F32)],
        compiler_params=_params(("parallel",)),
    )(x, g1, win)


def _shifted(p, last8, first):
    prow = jnp.where(first, 0.0, last8[7:8, :])
    rolled = pltpu.roll(p, 1, axis=0)
    rid = lax.broadcasted_iota(jnp.int32, p.shape, 0)
    return jnp.where(rid == 0, prow, rolled)


_PREP_TM = 256


def _prep_specs(tm):
    vec = lambda n: pl.BlockSpec((1, n), lambda i: (0, 0))
    mat = lambda r, n: pl.BlockSpec((r, n), lambda i: (0, 0))
    return [vec(SHIFT_COLS), vec(RW), mat(128, RW), vec(RW), mat(128, RW), mat(128, RW), vec(RW), vec(RW)]


def _prep_fwd(proj, pw):
    t = proj.shape[0]
    tm = _PREP_TM

    def body(p_ref, l8_ref, mu, w0, w2p, a0, a2p, g2, k_k, k_a, *outs):
        p = p_ref[...]
        pprev = _shifted(p, l8_ref[...], pl.program_id(0) == 0)
        res = _prep_fn(p, pprev, mu[...], w0[...], w2p[...], a0[...], a2p[...], g2[...], k_k[...], k_a[...])
        for o_ref, val in zip(outs, res):
            o_ref[...] = val

    row = pl.BlockSpec((tm, RW), lambda i: (i, 0))
    return pl.pallas_call(
        body, name="rwkv_prep", grid=(t // tm,),
        in_specs=[pl.BlockSpec((tm, SHIFT_COLS), lambda i: (i, 0)),
                  pl.BlockSpec((8, SHIFT_COLS), lambda i: (jnp.maximum(i * (tm // 8) - 1, 0), 0))] + _prep_specs(tm),
        out_specs=[row] * 7,
        out_shape=[jax.ShapeDtypeStruct((t, RW), F32)] * 7,
        compiler_params=_params(("parallel",)),
    )(proj, proj, *pw)


def _pairs(ref):
    return jnp.stack([ref[:, 128 * p:128 * (p + 1)] for p in range(N_PAIR)], axis=0)


def _wkv_fwd(r, lw, k2, v, kk, a):
    t = r.shape[0]
    nc = t // CHUNK

    def body(r_ref, lw_ref, k_ref, v_ref, kk_ref, a_ref, y_ref, s_ref, st):
        @pl.when(pl.program_id(0) == 0)
        def _():
            st[...] = jnp.zeros_like(st)

        s0 = st[...]
        s_ref[0] = s0
        y, s1 = _wkv_chunk_fn(s0, *[_pairs(ref) for ref in (r_ref, lw_ref, k_ref, v_ref, kk_ref, a_ref)])
        for p in range(N_PAIR):
            y_ref[:, 128 * p:128 * (p + 1)] = y[p]
        st[...] = s1

    blk = pl.BlockSpec((CHUNK, RW), lambda c: (c, 0))
    return pl.pallas_call(
        body, name="wkv_fwd", grid=(nc,),
        in_specs=[blk] * 6,
        out_specs=[blk, pl.BlockSpec((1, N_PAIR, 128, 128), lambda c: (c, 0, 0, 0))],
        out_shape=[jax.ShapeDtypeStruct((t, RW), F32), jax.ShapeDtypeStruct((nc, N_PAIR, 128, 128), F32)],
        scratch_shapes=[pltpu.VMEM((N_PAIR, 128, 128), F32)],
        compiler_params=_params(("arbitrary",)),
    )(r, lw, k2, v, kk, a)


_POST_TM = 256


def _post_fwd(y, r, k2, v, g, lnw, lnb, rk):
    t = y.shape[0]
    tm = _POST_TM

    def body(y_ref, r_ref, k_ref, v_ref, g_ref, lnw_ref, lnb_ref, rk_ref, o_ref):
        o_ref[...] = _post_fn(y_ref[...], r_ref[...], k_ref[...], v_ref[...], g_ref[...],
                              lnw_ref[...], lnb_ref[...], rk_ref[...]).astype(BF16)

    row = pl.BlockSpec((tm, RW), lambda i: (i, 0))
    vec = pl.BlockSpec((1, RW), lambda i: (0, 0))
    return pl.pallas_call(
        body, name="rwkv_post", grid=(t // tm,),
        in_specs=[row] * 5 + [vec] * 3, out_specs=row,
        out_shape=jax.ShapeDtypeStruct((t, RW), BF16),
        compiler_params=_params(("parallel",)),
    )(y, r, k2, v, g, lnw, lnb, rk)


ATTN_GROUP = 2


def _dilated_rows(d, r, n):
    if d == 1:
        return pl.ds(pl.multiple_of(n * ATTN_BLOCK, ATTN_BLOCK), ATTN_BLOCK)
    return pl.ds(r + n * (ATTN_BLOCK * d), ATTN_BLOCK, stride=d)


def _for_each_sequence(t, unit):
    for di, d in enumerate(DILATIONS):

        @pl.when(pl.program_id(1) == di)
        def _(di=di, d=d):
            nb = t // (ATTN_BLOCK * d)
            if d == 1:
                unit(di, [(d, 0, 0)], False)
                unit(di, [(d, 0, 1)], True)
                lax.fori_loop(1, nb // 2, lambda k, c: (unit(di, [(d, 0, 2 * k), (d, 0, 2 * k + 1)], True), c)[1], 0)
            else:

                def residues(r, carry):
                    unit(di, [(d, r, 0), (d, r + d // 2, 0)], False)
                    if nb > 1:
                        lax.fori_loop(1, nb, lambda n, c: (unit(di, [(d, r, n), (d, r + d // 2, n)], True), c)[1], 0)
                    return carry

                lax.fori_loop(0, d // 2, residues, 0)


def _take(ref, lead, rows_list):
    return jnp.stack([ref.at[(*lead, g)][rows, :] for rows in rows_list for g in range(ATTN_GROUP)], axis=0)


def _put(ref, lead, rows_list, val, add=False):
    k = 0
    for rows in rows_list:
        for g in range(ATTN_GROUP):
            if add:
                ref.at[(*lead, g)][rows, :] += val[k]
            else:
                ref.at[(*lead, g)][rows, :] = val[k]
            k += 1


def _attn_fwd(qkv):
    t = qkv.shape[2]

    def body(q_ref, k_ref, v_ref, o_ref, l_ref):
        def unit(di, places, has_prev):
            cur = [_dilated_rows(d, r, n) for d, r, n in places]
            args = [_take(ref, (0,), cur) for ref in (q_ref, k_ref, v_ref)]
            if has_prev:
                prv = [_dilated_rows(d, r, n - 1) for d, r, n in places]
                args += [_take(ref, (0,), prv) for ref in (k_ref, v_ref)]
            o, lse = _attn_block_fn(*args)
            _put(o_ref, (0,), cur, o)
            _put(l_ref, (0,), cur, lse)

        _for_each_sequence(t, unit)

    spec = lambda j: pl.BlockSpec((1, ATTN_GROUP, t, 128), lambda i, b: (j, i, 0, 0))
    out = pl.BlockSpec((1, ATTN_GROUP, t, 128), lambda i, b: (b, i, 0, 0))
    return pl.pallas_call(
        body, name="attn_fwd", grid=(N_PAIR // ATTN_GROUP, len(DILATIONS)),
        in_specs=[spec(0), spec(1), spec(2)], out_specs=[out, out],
        out_shape=[jax.ShapeDtypeStruct((3, N_PAIR, t, 128), F32)] * 2,
        compiler_params=_params(("parallel", "arbitrary")),
    )(qkv, qkv, qkv)


_COMB_TM = 256


def _combine_fwd(o, l, og):
    t = o.shape[2]
    tm = _COMB_TM

    def body(o_ref, l_ref, og_ref, y_ref):
        for p in range(N_PAIR):
            cols = slice(128 * p, 128 * (p + 1))
            y_ref[:, cols] = _combine_fn(o_ref[0, p], o_ref[1, p], o_ref[2, p], l_ref[0, p], l_ref[1, p], l_ref[2, p],
                                         og_ref[:, cols]).astype(BF16)

    blk = pl.BlockSpec((3, N_PAIR, tm, 128), lambda i: (0, 0, i, 0))
    return pl.pallas_call(
        body, name="attn_combine", grid=(t // tm,),
        in_specs=[blk, blk, pl.BlockSpec((1, RW), lambda i: (0, 0))], out_specs=pl.BlockSpec((tm, RW), lambda i: (i, 0)),
        out_shape=jax.ShapeDtypeStruct((t, RW), BF16),
        compiler_params=_params(("parallel",)),
    )(o, l, og)


def _out_proj(x, ycat, wout, g2):
    t = x.shape[0]
    tm = 256

    def body(x_ref, y_ref, w_ref, g_ref, x1_ref, h_ref):
        x1 = x_ref[...] + _dot(y_ref[...], w_ref[...])
        x1_ref[...] = x1
        h_ref[...] = _rms_fwd(x1, g_ref[...]).astype(BF16)

    row = pl.BlockSpec((tm, D_MODEL), lambda i: (i, 0))
    return pl.pallas_call(
        body, name="out_proj", grid=(t // tm,),
        in_specs=[row, row, pl.BlockSpec((D_MODEL, D_MODEL), lambda i: (0, 0)), pl.BlockSpec((1, D_MODEL), lambda i: (0, 0))],
        out_specs=[row, row],
        out_shape=[jax.ShapeDtypeStruct((t, D_MODEL), F32), jax.ShapeDtypeStruct((t, D_MODEL), BF16)],
        compiler_params=_params(("parallel",)),
    )(x, ycat, wout, g2)


def _ffn_up(h2, wg, wu):
    t = h2.shape[0]
    tm = 512

    def body(h_ref, wg_ref, wu_ref, gt_ref, up_ref, act_ref):
        h = h_ref[...]
        gt = _dot_nt(h, wg_ref[...])
        up = _dot_nt(h, wu_ref[...])
        gt_ref[...] = gt.astype(BF16)
        up_ref[...] = up.astype(BF16)
        act_ref[...] = (gt * _sigmoid(gt) * up).astype(BF16)

    wide = pl.BlockSpec((tm, D_FF), lambda i: (i, 0))
    wsp = pl.BlockSpec((D_FF, D_MODEL), lambda i: (0, 0))
    return pl.pallas_call(
        body, name="ffn_up", grid=(t // tm,),
        in_specs=[pl.BlockSpec((tm, D_MODEL), lambda i: (i, 0)), wsp, wsp],
        out_specs=[wide, wide, wide],
        out_shape=[jax.ShapeDtypeStruct((t, D_FF), BF16)] * 3,
        compiler_params=_params(("parallel",)),
    )(h2, wg, wu)


def _ffn_down_loss(x1, act, wd, gf, tgt):
    t = x1.shape[0]
    tm = 256

    def body(x1_ref, a_ref, w_ref, g_ref, t_ref, dx_ref, dxb_ref, loss_ref, dg_ref):
        first = pl.program_id(0) == 0
        x2 = x1_ref[...] + _dot(a_ref[...], w_ref[...])
        g = g_ref[...]
        diff = _rms_fwd(x2, g) - t_ref[...]
        lrow = 0.5 * jnp.sum(_colsum8(diff * diff), axis=1, keepdims=True) * (1.0 / D_MODEL)
        _acc(loss_ref, jnp.broadcast_to(lrow, (8, 128)), first)
        dx2, dgr = _rms_bwd(diff * (1.0 / D_MODEL), x2, g)
        dx_ref[...] = dx2
        dxb_ref[...] = dx2.astype(BF16)
        _acc(dg_ref, _colsum8(dgr), first)

    row = pl.BlockSpec((tm, D_MODEL), lambda i: (i, 0))
    return pl.pallas_call(
        body, name="ffn_down_loss", grid=(t // tm,),
        in_specs=[row, pl.BlockSpec((tm, D_FF), lambda i: (i, 0)), pl.BlockSpec((D_FF, D_MODEL), lambda i: (0, 0)),
                  pl.BlockSpec((1, D_MODEL), lambda i: (0, 0)), row],
        out_specs=[row, row, pl.BlockSpec((8, 128), lambda i: (0, 0)), pl.BlockSpec((8, D_MODEL), lambda i: (0, 0))],
        out_shape=[jax.ShapeDtypeStruct((t, D_MODEL), F32), jax.ShapeDtypeStruct((t, D_MODEL), BF16),
                   jax.ShapeDtypeStruct((8, 128), F32), jax.ShapeDtypeStruct((8, D_MODEL), F32)],
        compiler_params=_params(("arbitrary",)),
    )(x1, act, wd, gf, tgt)


def _ffn_bwd_act(dx2b, wd, gt, up):
    t = dx2b.shape[0]
    tm = 512

    def body(dx_ref, w_ref, gt_ref, up_ref, dgt_ref, dup_ref):
        dact = _dot_nt(dx_ref[...], w_ref[...])
        gt = gt_ref[...].astype(F32)
        sg = _sigmoid(gt)
        dgt_ref[...] = (dact * up_ref[...].astype(F32) * sg * (1.0 + gt * (1.0 - sg))).astype(BF16)
        dup_ref[...] = (dact * gt * sg).astype(BF16)

    wide = pl.BlockSpec((tm, D_FF), lambda i: (i, 0))
    return pl.pallas_call(
        body, name="ffn_bwd_act", grid=(t // tm,),
        in_specs=[pl.BlockSpec((tm, D_MODEL), lambda i: (i, 0)), pl.BlockSpec((D_FF, D_MODEL), lambda i: (0, 0)), wide, wide],
        out_specs=[wide, wide],
        out_shape=[jax.ShapeDtypeStruct((t, D_FF), BF16)] * 2,
        compiler_params=_params(("parallel",)),
    )(dx2b, wd, gt, up)


def _ffn_bwd_h(dgt, dup, wg, wu, dx2, x1, g2, wout):
    t = dgt.shape[0]
    tm = 256

    def body(dgt_ref, dup_ref, wg_ref, wu_ref, dx2_ref, x1_ref, g_ref, wo_ref, dx1_ref, dx1b_ref, dya_ref, dyb_ref, dg_ref):
        dh = _dot(dgt_ref[...], wg_ref[...]) + _dot(dup_ref[...], wu_ref[...])
        dxn, dgr = _rms_bwd(dh, x1_ref[...], g_ref[...])
        dx1 = dx2_ref[...] + dxn
        dx1_ref[...] = dx1
        dx1b = dx1.astype(BF16)
        dx1b_ref[...] = dx1b
        dy = _dot_nt(dx1b, wo_ref[...])
        dya_ref[...] = dy[:, :RW]
        dyb_ref[...] = dy[:, RW:]
        _acc(dg_ref, _colsum8(dgr), pl.program_id(0) == 0)

    wide = pl.BlockSpec((tm, D_FF), lambda i: (i, 0))
    row = pl.BlockSpec((tm, D_MODEL), lambda i: (i, 0))
    half = pl.BlockSpec((tm, RW), lambda i: (i, 0))
    wsp = pl.BlockSpec((D_FF, D_MODEL), lambda i: (0, 0))
    return pl.pallas_call(
        body, name="ffn_bwd_h", grid=(t // tm,),
        in_specs=[wide, wide, wsp, wsp, row, row, pl.BlockSpec((1, D_MODEL), lambda i: (0, 0)),
                  pl.BlockSpec((D_MODEL, D_MODEL), lambda i: (0, 0))],
        out_specs=[row, row, half, half, pl.BlockSpec((8, D_MODEL), lambda i: (0, 0))],
        out_shape=[jax.ShapeDtypeStruct((t, D_MODEL), F32), jax.ShapeDtypeStruct((t, D_MODEL), BF16),
                   jax.ShapeDtypeStruct((t, RW), F32), jax.ShapeDtypeStruct((t, RW), F32),
                   jax.ShapeDtypeStruct((8, D_MODEL), F32)],
        compiler_params=_params(("arbitrary",)),
    )(dgt, dup, wg, wu, dx2, x1, g2, wout)


def _wgrad(a, b, tk, tn, name):
    t, kdim = a.shape
    ndim = b.shape[1]

    def body(a_ref, b_ref, o_ref):
        o_ref[...] = _dot_tn(a_ref[...], b_ref[...])

    return pl.pallas_call(
        body, name=name, grid=(kdim // tk, ndim // tn),
        in_specs=[pl.BlockSpec((t, tk), lambda i, j: (0, i)), pl.BlockSpec((t, tn), lambda i, j: (0, j))],
        out_specs=pl.BlockSpec((tk, tn), lambda i, j: (i, j)),
        out_shape=jax.ShapeDtypeStruct((kdim, ndim), F32),
        compiler_params=_params(("parallel", "parallel")),
    )(a, b)


def _post_bwd(dya, y, r, k2, v, g, lnw, lnb, rk):
    t = y.shape[0]
    tm = _POST_TM

    def body(d_ref, y_ref, r_ref, k_ref, v_ref, g_ref, lnw_ref, lnb_ref, rk_ref,
             dy_ref, dr_ref, dk_ref, dv_ref, dg_ref, dlnw_ref, dlnb_ref, drk_ref):
        first = pl.program_id(0) == 0
        ones = jnp.ones((tm, 1), F32)
        prim = (y_ref[...], r_ref[...], k_ref[...], v_ref[...], g_ref[...],
                ones * lnw_ref[...], ones * lnb_ref[...], ones * rk_ref[...])
        _, vjp = jax.vjp(_post_fn, *prim)
        dy, dr, dk, dv, dg, dlnw, dlnb, drk = vjp(d_ref[...])
        dy_ref[...] = dy
        dr_ref[...] = dr
        dk_ref[...] = dk
        dv_ref[...] = dv
        dg_ref[...] = dg
        _acc(dlnw_ref, _colsum8(dlnw), first)
        _acc(dlnb_ref, _colsum8(dlnb), first)
        _acc(drk_ref, _colsum8(drk), first)

    row = pl.BlockSpec((tm, RW), lambda i: (i, 0))
    vec = pl.BlockSpec((1, RW), lambda i: (0, 0))
    part = pl.BlockSpec((8, RW), lambda i: (0, 0))
    return pl.pallas_call(
        body, name="rwkv_post_bwd", grid=(t // tm,),
        in_specs=[row] * 6 + [vec] * 3, out_specs=[row] * 5 + [part] * 3,
        out_shape=[jax.ShapeDtypeStruct((t, RW), F32)] * 5 + [jax.ShapeDtypeStruct((8, RW), F32)] * 3,
        compiler_params=_params(("arbitrary",)),
    )(dya, y, r, k2, v, g, lnw, lnb, rk)


def _wkv_bwd(dy, s0s, r, lw, k2, v, kk, a):
    t = r.shape[0]
    nc = t // CHUNK

    def body(dy_ref, s_ref, r_ref, lw_ref, k_ref, v_ref, kk_ref, a_ref,
             dr_ref, dlw_ref, dk_ref, dv_ref, dkk_ref, da_ref, ds):
        @pl.when(pl.program_id(0) == 0)
        def _():
            ds[...] = jnp.zeros_like(ds)

        _, vjp = jax.vjp(_wkv_chunk_fn, s_ref[0],
                         *[_pairs(ref) for ref in (r_ref, lw_ref, k_ref, v_ref, kk_ref, a_ref)])
        res = vjp((_pairs(dy_ref), ds[...]))
        ds[...] = res[0]
        for ref, val in zip((dr_ref, dlw_ref, dk_ref, dv_ref, dkk_ref, da_ref), res[1:]):
            for p in range(N_PAIR):
                ref[:, 128 * p:128 * (p + 1)] = val[p]

    blk = pl.BlockSpec((CHUNK, RW), lambda c: (nc - 1 - c, 0))
    return pl.pallas_call(
        body, name="wkv_bwd", grid=(nc,),
        in_specs=[blk, pl.BlockSpec((1, N_PAIR, 128, 128), lambda c: (nc - 1 - c, 0, 0, 0))] + [blk] * 6,
        out_specs=[blk] * 6,
        out_shape=[jax.ShapeDtypeStruct((t, RW), F32)] * 6,
        scratch_shapes=[pltpu.VMEM((N_PAIR, 128, 128), F32)],
        compiler_params=_params(("arbitrary",)),
    )(dy, s0s, r, lw, k2, v, kk, a)


def _prep_bwd(proj, pw, douts):
    t = proj.shape[0]
    tm = _PREP_TM
    nt = t // tm

    def body(p_ref, l8_ref, mu, w0, w2p, a0, a2p, g2, k_k, k_a, dr, dr2, dlw, dk2, dk22, dv, dv2, dkk, da, dg,
             dp_ref, dmu_ref, dw0_ref, dw2_ref, da0_ref, da2_ref, dg2_ref, dkk_ref, dka_ref, carry):
        i = pl.program_id(0)
        first = i == 0

        @pl.when(first)
        def _():
            carry[...] = jnp.zeros_like(carry)

        p = p_ref[...]
        pprev = _shifted(p, l8_ref[...], i == nt - 1)
        ones = jnp.ones((tm, 1), F32)
        prim = (p, pprev, ones * mu[...], ones * w0[...], w2p[...], ones * a0[...], a2p[...], g2[...],
                ones * k_k[...], ones * k_a[...])
        _, vjp = jax.vjp(_prep_fn, *prim)
        dp, dpp, dmu, dw0, dw2, da0, da2, dg2, dkk_, dka = vjp(
            (dr[...] + dr2[...], dlw[...], dk2[...] + dk22[...], dv[...] + dv2[...], dkk[...], da[...], dg[...]))
        up = pltpu.roll(dpp, tm - 1, axis=0)
        rid = lax.broadcasted_iota(jnp.int32, dpp.shape, 0)
        dp_ref[...] = dp + jnp.where(rid == tm - 1, carry[0:1, :], up)
        carry[...] = jnp.broadcast_to(dpp[0:1, :], carry.shape)
        _acc(dmu_ref, _colsum8(dmu), first)
        _acc(dw0_ref, _colsum8(dw0), first)
        _acc(dw2_ref, dw2, first)
        _acc(da0_ref, _colsum8(da0), first)
        _acc(da2_ref, da2, first)
        _acc(dg2_ref, dg2, first)
        _acc(dkk_ref, _colsum8(dkk_), first)
        _acc(dka_ref, _colsum8(dka), first)

    rev = lambda i: (nt - 1 - i, 0)
    row = pl.BlockSpec((tm, RW), rev)
    part = lambda n: pl.BlockSpec((8, n), lambda i: (0, 0))
    mat = pl.BlockSpec((128, RW), lambda i: (0, 0))
    return pl.pallas_call(
        body, name="rwkv_prep_bwd", grid=(nt,),
        in_specs=[pl.BlockSpec((tm, SHIFT_COLS), rev),
                  pl.BlockSpec((8, SHIFT_COLS), lambda i: (jnp.maximum((nt - 1 - i) * (tm // 8) - 1, 0), 0))]
                 + _prep_specs(tm) + [row] * 10,
        out_specs=[pl.BlockSpec((tm, SHIFT_COLS), rev), part(SHIFT_COLS), part(RW), mat, part(RW), mat, mat,
                   part(RW), part(RW)],
        out_shape=[jax.ShapeDtypeStruct((t, SHIFT_COLS), F32), jax.ShapeDtypeStruct((8, SHIFT_COLS), F32),
                   jax.ShapeDtypeStruct((8, RW), F32), jax.ShapeDtypeStruct((128, RW), F32),
                   jax.ShapeDtypeStruct((8, RW), F32), jax.ShapeDtypeStruct((128, RW), F32),
                   jax.ShapeDtypeStruct((128, RW), F32), jax.ShapeDtypeStruct((8, RW), F32),
                   jax.ShapeDtypeStruct((8, RW), F32)],
        scratch_shapes=[pltpu.VMEM((8, SHIFT_COLS), F32)],
        compiler_params=_params(("arbitrary",)),
    )(proj, proj, *pw, *douts)


def _combine_bwd(dyb, o, l, og):
    t = dyb.shape[0]
    tm = _COMB_TM

    def body(d_ref, o_ref, l_ref, og_ref, do_ref, dl_ref, dog_ref):
        ones = jnp.ones((tm, 1), F32)
        dog = []
        for p in range(N_PAIR):
            cols = slice(128 * p, 128 * (p + 1))
            _, vjp = jax.vjp(_combine_fn, o_ref[0, p], o_ref[1, p], o_ref[2, p], l_ref[0, p], l_ref[1, p], l_ref[2, p],
                             ones * og_ref[:, cols])
            res = vjp(d_ref[:, cols])
            for b in range(3):
                do_ref[b, p] = res[b]
                dl_ref[b, p] = res[3 + b]
            dog.append(_colsum8(res[6]))
        _acc(dog_ref, jnp.concatenate(dog, axis=1), pl.program_id(0) == 0)

    blk = pl.BlockSpec((3, N_PAIR, tm, 128), lambda i: (0, 0, i, 0))
    return pl.pallas_call(
        body, name="attn_combine_bwd", grid=(t // tm,),
        in_specs=[pl.BlockSpec((tm, RW), lambda i: (i, 0)), blk, blk, pl.BlockSpec((1, RW), lambda i: (0, 0))],
        out_specs=[blk, blk, pl.BlockSpec((8, RW), lambda i: (0, 0))],
        out_shape=[jax.ShapeDtypeStruct((3, N_PAIR, t, 128), F32)] * 2 + [jax.ShapeDtypeStruct((8, RW), F32)],
        compiler_params=_params(("arbitrary",)),
    )(dyb, o, l, og)


def _attn_bwd(do, dl, qkv):
    t = qkv.shape[2]

    def body(do_ref, dl_ref, q_ref, k_ref, v_ref, dq_ref, dk_ref, dv_ref):
        @pl.when(pl.program_id(1) == 0)
        def _():
            for ref in (dq_ref, dk_ref, dv_ref):
                ref[...] = jnp.zeros_like(ref)

        def unit(di, places, has_prev):
            cur = [_dilated_rows(d, r, n) for d, r, n in places]
            args = [_take(ref, (0,), cur) for ref in (q_ref, k_ref, v_ref)]
            if has_prev:
                prv = [_dilated_rows(d, r, n - 1) for d, r, n in places]
                args += [_take(ref, (0,), prv) for ref in (k_ref, v_ref)]
            _, vjp = jax.vjp(_attn_block_fn, *args)
            res = vjp((_take(do_ref, (0,), cur), _take(dl_ref, (0,), cur)))
            _put(dq_ref, (), cur, res[0], add=True)
            _put(dk_ref, (), cur, res[1], add=True)
            _put(dv_ref, (), cur, res[2], add=True)
            if has_prev:
                _put(dk_ref, (), prv, res[3], add=True)
                _put(dv_ref, (), prv, res[4], add=True)

        _for_each_sequence(t, unit)

    spec = lambda j: pl.BlockSpec((1, ATTN_GROUP, t, 128), lambda i, b: (j, i, 0, 0))
    branch = pl.BlockSpec((1, ATTN_GROUP, t, 128), lambda i, b: (b, i, 0, 0))
    out = pl.BlockSpec((ATTN_GROUP, t, 128), lambda i, b: (i, 0, 0))
    return pl.pallas_call(
        body, name="attn_bwd", grid=(N_PAIR // ATTN_GROUP, len(DILATIONS)),
        in_specs=[branch, branch, spec(0), spec(1), spec(2)], out_specs=[out] * 3,
        out_shape=[jax.ShapeDtypeStruct((N_PAIR, t, 128), F32)] * 3,
        compiler_params=_params(("parallel", "arbitrary")),
    )(do, dl, qkv, qkv, qkv)


def _in_proj_bwd(dpa, dq, dk, dv, win, x, g1, dx1):
    t = x.shape[0]
    tm = 256

    def body(dpa_ref, dq_ref, dk_ref, dv_ref, w_ref, x_ref, g_ref, dx1_ref, dproj_ref, dx_ref, dg_ref):
        parts = [dpa_ref[...]] + [ref[p] for ref in (dq_ref, dk_ref, dv_ref) for p in range(N_PAIR)]
        dproj = jnp.concatenate([z.astype(BF16) for z in parts], axis=1)
        dproj_ref[...] = dproj
        dh = _dot(dproj, w_ref[...])
        dxn, dgr = _rms_bwd(dh, x_ref[...], g_ref[...])
        dx_ref[...] = dx1_ref[...] + dxn
        _acc(dg_ref, _colsum8(dgr), pl.program_id(0) == 0)

    row = pl.BlockSpec((tm, D_MODEL), lambda i: (i, 0))
    pair = pl.BlockSpec((N_PAIR, tm, 128), lambda i: (0, i, 0))
    return pl.pallas_call(
        body, name="in_proj_bwd", grid=(t // tm,),
        in_specs=[pl.BlockSpec((tm, SHIFT_COLS), lambda i: (i, 0))] + [pair] * 3
                 + [pl.BlockSpec((IN_COLS, D_MODEL), lambda i: (0, 0)), row, pl.BlockSpec((1, D_MODEL), lambda i: (0, 0)), row],
        out_specs=[pl.BlockSpec((tm, IN_COLS), lambda i: (i, 0)), row, pl.BlockSpec((8, D_MODEL), lambda i: (0, 0))],
        out_shape=[jax.ShapeDtypeStruct((t, IN_COLS), BF16), jax.ShapeDtypeStruct((t, D_MODEL), F32),
                   jax.ShapeDtypeStruct((8, D_MODEL), F32)],
        compiler_params=_params(("arbitrary",)),
    )(dpa, dq, dk, dv, win, x, g1, dx1)


def _pad_lora(w, lo):
    z = jnp.zeros((64, RW), F32)
    return jnp.concatenate([w, z], axis=0) if lo == 0 else jnp.concatenate([z, w], axis=0)


def _local_step(x, tgt, win, vecs, w2, a2, g2m, get_rest, send_rest):
    pw = (vecs["mu_shift"], vecs["decay_w0"], _pad_lora(w2, 0), vecs["iclr_a0"], _pad_lora(a2, 64), g2m,
          vecs["k_k"], vecs["k_a"])
    h, proj, qkv = _in_proj(x, vecs["mix_norm_g"], win)
    r, lw, k2, v, kk, a, g = _prep_fwd(proj, pw)
    y, s0s = _wkv_fwd(r, lw, k2, v, kk, a)
    ya = _post_fwd(y, r, k2, v, g, vecs["ln_x_w"], vecs["ln_x_b"], vecs["r_k"])
    o_att, l_att = _attn_fwd(qkv)
    yb = _combine_fwd(o_att, l_att, vecs["attn_out_g"])

    wout, wg, wu, wd = get_rest(yb)
    ycat = jnp.concatenate([ya, yb], axis=1)
    x1, h2 = _out_proj(x, ycat, wout, vecs["ffn_norm_g"])
    gt, up, act = _ffn_up(h2, wg, wu)
    dx2, dx2b, loss8, dgf = _ffn_down_loss(x1, act, wd, vecs["final_norm_g"], tgt)

    dgt, dup = _ffn_bwd_act(dx2b, wd, gt, up)
    dx1, dx1b, dya, dyb, dg2n = _ffn_bwd_h(dgt, dup, wg, wu, dx2, x1, vecs["ffn_norm_g"], wout)
    gw = {
        "w_down": _wgrad(act, dx2b, 1408, 1024, "wgrad_down"),
        "w_gate": _wgrad(dgt, h2, 1408, 1024, "wgrad_gate"),
        "w_up": _wgrad(dup, h2, 1408, 1024, "wgrad_up"),
        "w_out": _wgrad(ycat, dx1b, 1024, 1024, "wgrad_out"),
    }

    lnw = vecs["ln_x_w"] + send_rest(gw)[0, 0]
    dy, dr_p, dk2_p, dv_p, dg, dlnw, dlnb, drk = _post_bwd(dya, y, r, k2, v, g, lnw, vecs["ln_x_b"], vecs["r_k"])
    dr_s, dlw, dk2_s, dv_s, dkk, da = _wkv_bwd(dy, s0s, r, lw, k2, v, kk, a)
    dpa, dmu, dw0, dw2p, da0, da2p, dg2m, dk_k, dk_a = _prep_bwd(
        proj, pw, (dr_p, dr_s, dlw, dk2_p, dk2_s, dv_p, dv_s, dkk, da, dg))

    do_att, dl_att, dog = _combine_bwd(dyb, o_att, l_att, vecs["attn_out_g"])
    dq, dk, dv = _attn_bwd(do_att, dl_att, qkv)
    dproj, dx, dg1 = _in_proj_bwd(dpa, dq, dk, dv, win, x, vecs["mix_norm_g"], dx1)
    gw["w_in"] = _wgrad(dproj, h, 1664, 1024, "wgrad_in")
    gw["decay_w2"] = dw2p[:64]
    gw["iclr_a2"] = da2p[64:]
    gw["gate_g2"] = dg2m
    gv = {"mix_norm_g": dg1, "mu_shift": dmu, "decay_w0": dw0, "iclr_a0": da0, "k_k": dk_k, "k_a": dk_a, "r_k": drk,
          "ln_x_w": dlnw, "ln_x_b": dlnb, "attn_out_g": dog, "ffn_norm_g": dg2n, "final_norm_g": dgf}
    return loss8, dx, gw, gv


N_CHIP = 4
N_DEV = 8
MATS = ("w_in", "w_out", "w_gate", "w_up", "w_down")
LORAS = ("decay_w2", "iclr_a2", "gate_g2")
VECS = (("mix_norm_g", 1024), ("mu_shift", 1792), ("decay_w0", 512), ("iclr_a0", 512), ("k_k", 512), ("k_a", 512),
        ("r_k", 512), ("ln_x_w", 512), ("ln_x_b", 512), ("attn_out_g", 512), ("ffn_norm_g", 1024),
        ("final_norm_g", 1024))
N_VEC = sum(n for _, n in VECS)
N_SMALL = N_VEC + 128
ANY = pl.BlockSpec(memory_space=pl.ANY)


def _flip(v, f):
    return 1 - v if f else v


class _Me:
    def __init__(self, mode):
        x, y, c = lax.axis_index("x"), lax.axis_index("y"), lax.axis_index("c")
        self.core, self.chip, self.dev = c, 2 * x + y, 4 * x + 2 * y + c
        self.sibling = (x, y, 1 - c)
        if mode == "chips":
            self.peers = [(px, py, c) for px, py in ((1 - x, y), (x, 1 - y), (1 - x, 1 - y))]
        else:
            self.peers = [(_flip(x, k & 4), _flip(y, k & 2), _flip(c, k & 1)) for k in range(1, N_DEV)]


def _half(core, rows):
    h = rows // 2
    return pl.ds(pl.multiple_of(core * h, h), h)


def _peer_copy(srcs, dsts, kinds, send_sems, recv_sems, me, j, i, incoming):
    px, py, pc = me.peers[j]
    pchip, pdev = 2 * px + py, 4 * px + 2 * py + pc
    src, dst, kind = srcs[i], dsts[i], kinds[i]
    if kind == "gather":
        rows = _half(me.core, src.shape[0])
        src, dst = src.at[rows], dst.at[pchip if incoming else me.chip, rows]
    elif kind == "scatter":
        src, dst = src.at[pchip, _half(pc, src.shape[1])], dst.at[pdev if incoming else me.dev]
    else:
        dst = dst.at[pdev if incoming else me.dev]
    n = len(srcs)
    return pltpu.make_async_remote_copy(src_ref=src, dst_ref=dst, send_sem=send_sems.at[n * j + i],
                                        recv_sem=recv_sems.at[n * j + i], device_id=(px, py, pc), device_id_type=MESH)


def _mode(kinds):
    return "chips" if kinds[0] == "gather" else "devs"


def _npeer(kinds):
    return N_CHIP - 1 if kinds[0] == "gather" else N_DEV - 1


def _swap_gathered(lands, name):
    n = len(lands)

    def body(*refs):
        dsts, send_sems, recv_sems = refs[n:2 * n], refs[2 * n], refs[2 * n + 1]
        me = _Me("chips")

        def copy(j, i, incoming):
            px, py, _ = me.peers[j]
            rows_out, rows_in = _half(me.core, dsts[i].shape[1]), _half(1 - me.core, dsts[i].shape[1])
            return pltpu.make_async_remote_copy(
                src_ref=dsts[i].at[2 * px + py, rows_out], dst_ref=dsts[i].at[2 * px + py, rows_in if incoming else rows_out],
                send_sem=send_sems.at[n * j + i], recv_sem=recv_sems.at[n * j + i], device_id=me.sibling, device_id_type=MESH)

        sends = [copy(j, i, False) for j in range(3) for i in range(n)]
        for cp in sends:
            cp.start()
        for j in range(3):
            for i in range(n):
                copy(j, i, True).wait_recv()
        for cp in sends:
            cp.wait_send()

    return pl.pallas_call(
        body, name=name, in_specs=[ANY] * n, out_specs=[ANY] * n,
        out_shape=[jax.ShapeDtypeStruct(l.shape, l.dtype) for l in lands],
        input_output_aliases={i: i for i in range(n)},
        scratch_shapes=[pltpu.SemaphoreType.DMA((3 * n,)), pltpu.SemaphoreType.DMA((3 * n,))],
    )(*lands)


def _join_halves(sums, name):
    n = len(sums)

    def body(*refs):
        dsts, send_sems, recv_sems = refs[n:2 * n], refs[2 * n], refs[2 * n + 1]
        me = _Me("chips")

        def copy(i, incoming):
            mine, other = _half(me.core, dsts[i].shape[0]), _half(1 - me.core, dsts[i].shape[0])
            return pltpu.make_async_remote_copy(src_ref=dsts[i].at[mine], dst_ref=dsts[i].at[other if incoming else mine],
                                                send_sem=send_sems.at[i], recv_sem=recv_sems.at[i],
                                                device_id=me.sibling, device_id_type=MESH)

        sends = [copy(i, False) for i in range(n)]
        for cp in sends:
            cp.start()
        for i in range(n):
            copy(i, True).wait_recv()
        for cp in sends:
            cp.wait_send()

    return pl.pallas_call(
        body, name=name, in_specs=[ANY] * n, out_specs=[ANY] * n,
        out_shape=[jax.ShapeDtypeStruct(s.shape, s.dtype) for s in sums],
        input_output_aliases={i: i for i in range(n)},
        scratch_shapes=[pltpu.SemaphoreType.DMA((n,)), pltpu.SemaphoreType.DMA((n,))],
    )(*sums)


HBM = pl.BlockSpec(memory_space=pltpu.HBM)
SEM = pl.BlockSpec(memory_space=pltpu.SEMAPHORE)
EFFECT = pltpu.SideEffectType.DATAFLOW_SIDE_EFFECTING


def _swap_start(arrs, lands, kinds, name):
    n = len(arrs)

    def body(*refs):
        srcs, dsts, send_sems, recv_sems, token = refs[:n], refs[n:2 * n], refs[2 * n], refs[2 * n + 1], refs[-1]
        me = _Me(_mode(kinds))
        for j in range(len(me.peers)):
            for i in range(n):
                _peer_copy(srcs, dsts, kinds, send_sems, recv_sems, me, j, i, False).start()
        token[...] = jnp.zeros_like(token)

    ns = _npeer(kinds) * n
    outs = pl.pallas_call(
        body, name=name,
        out_shape=(pltpu.SemaphoreType.DMA((ns,)), pltpu.SemaphoreType.DMA((ns,)),
                   *[pltpu.HBM(a.shape, a.dtype) for a in arrs], *[pltpu.HBM(l.shape, l.dtype) for l in lands],
                   jax.ShapeDtypeStruct((8, 128), F32)),
        in_specs=[HBM] * (2 * n), out_specs=(SEM, SEM, *[HBM] * (2 * n), pl.BlockSpec(memory_space=pltpu.VMEM)),
        input_output_aliases={k: 2 + k for k in range(2 * n)},
        compiler_params=pltpu.CompilerParams(has_side_effects=EFFECT),
    )(*[pltpu.with_memory_space_constraint(a, pltpu.HBM) for a in arrs],
      *[pltpu.with_memory_space_constraint(l, pltpu.HBM) for l in lands])
    return outs[0], outs[1], outs[2:2 + n], outs[2 + n:2 + 2 * n], outs[-1]


def _swap_wait(send_sems, recv_sems, srcs_thru, lands_thru, after, kinds, name):
    n = len(srcs_thru)

    def body(*refs):
        srcs, dsts, s_sems, r_sems = refs[:n], refs[n:2 * n], refs[2 * n], refs[2 * n + 1]
        me = _Me(_mode(kinds))
        for j in range(len(me.peers)):
            for i in range(n):
                cp = _peer_copy(srcs, dsts, kinds, s_sems, r_sems, me, j, i, True)
                cp.wait_send()
                cp.wait_recv()

    outs = pl.pallas_call(
        body, name=name,
        out_shape=tuple(pltpu.HBM(a.shape, a.dtype) for a in (*srcs_thru, *lands_thru)),
        in_specs=[HBM] * (2 * n) + [SEM, SEM, ANY], out_specs=tuple([HBM] * (2 * n)),
        input_output_aliases={k: k for k in range(2 * n)},
        compiler_params=pltpu.CompilerParams(has_side_effects=EFFECT),
    )(*srcs_thru, *lands_thru, send_sems, recv_sems, after)
    return outs[n:]


def _adamw(w, g, m, v):
    m = ADAM_B1 * m + (1.0 - ADAM_B1) * g
    v = ADAM_B2 * v + (1.0 - ADAM_B2) * (g * g)
    m_hat = m / (1.0 - ADAM_B1 ** ADAM_STEP)
    v_hat = v / (1.0 - ADAM_B2 ** ADAM_STEP)
    delta = -ADAM_LR * (m_hat / (jnp.sqrt(v_hat) + ADAM_EPS) + ADAM_WD * w)
    return delta, m, v


def _reduce8(rbuf, core, tr, name):
    _, h, cols = rbuf.shape

    def body(core_ref, r_ref, g_ref):
        g = r_ref[0].astype(F32)
        for s in range(1, N_DEV):
            g = g + r_ref[s].astype(F32)
        g_ref[...] = g

    return pl.pallas_call(
        body, name=name,
        grid_spec=pltpu.PrefetchScalarGridSpec(
            num_scalar_prefetch=1, grid=(h // tr,),
            in_specs=[pl.BlockSpec((N_DEV, tr, cols), lambda i, core_ref: (0, i, 0))],
            out_specs=pl.BlockSpec((tr, cols), lambda i, core_ref: (core_ref[0] * (h // tr) + i, 0))),
        out_shape=jax.ShapeDtypeStruct((2 * h, cols), F32),
        compiler_params=_params(("parallel",)),
    )(core, rbuf)


def _adamw_call(g, w, m, v, tr, name):
    _, rows, cols = w.shape

    def body(g_in, w_ref, m_ref, v_ref, g_ref, d_ref, nm_ref, nv_ref):
        g = g_in[...]
        g_ref[0] = g
        d_ref[0], nm_ref[0], nv_ref[0] = _adamw(w_ref[0], g, m_ref[0], v_ref[0])

    row = pl.BlockSpec((1, tr, cols), lambda i: (0, i, 0))
    return pl.pallas_call(
        body, name=name, grid=(rows // tr,),
        in_specs=[pl.BlockSpec((tr, cols), lambda i: (i, 0)), row, row, row], out_specs=[row] * 4,
        out_shape=[jax.ShapeDtypeStruct(w.shape, F32)] * 4,
        compiler_params=_params(("parallel",)),
    )(g, w, m, v)


def _rowsum_small(parts, loss8):
    def body(*refs):
        out = refs[-1]
        c0 = 0
        for ref in refs[:-1]:
            n = ref.shape[1]
            out[:, c0:c0 + n] = jnp.sum(ref[...], axis=0, keepdims=True)
            c0 += n

    return pl.pallas_call(body, name="rowsum_small", out_shape=jax.ShapeDtypeStruct((1, N_SMALL), F32))(*parts, loss8)


def _reduce_adamw_small(sbuf, ws, ms, vs):
    nv = len(ws)

    def body(*refs):
        s_ref, ins, outs = refs[0], refs[1:1 + 3 * nv], refs[1 + 3 * nv:]
        tot = s_ref[0]
        for s in range(1, N_DEV):
            tot = tot + s_ref[s]
        c0 = 0
        for i in range(nv):
            n = ins[i].shape[1]
            g = tot[:, c0:c0 + n]
            outs[i][...] = g
            outs[nv + i][...], outs[2 * nv + i][...], outs[3 * nv + i][...] = _adamw(
                ins[i][...], g, ins[nv + i][...], ins[2 * nv + i][...])
            c0 += n
        outs[-1][...] = tot[:, c0:]

    return pl.pallas_call(
        body, name="reduce_adamw_small",
        out_shape=[jax.ShapeDtypeStruct(a.shape, F32) for a in ws] * 4 + [jax.ShapeDtypeStruct((1, 128), F32)],
    )(sbuf, *ws, *ms, *vs)


_TRANSPOSED = ("w_in", "w_gate", "w_up")
_ROW_STACKED = MATS
_ADAM_TILE = {"w_in": 208, "w_out": 256, "w_gate": 176, "w_up": 176, "w_down": 176, "decay_w2": 64, "iclr_a2": 64,
              "gate_g2": 128}
_SUM_TILE = {"w_in": 208, "w_out": 128, "w_gate": 176, "w_up": 176, "w_down": 176, "decay_w2": 32, "iclr_a2": 32,
             "gate_g2": 64}


def _full(n, stacked):
    p, r, c = stacked.shape
    if n in _ROW_STACKED:
        return stacked.reshape(p * r, c)
    return jnp.transpose(stacked, (1, 0, 2)).reshape(r, p * c)


def _by_chip(n, full):
    if n in _ROW_STACKED:
        return full.reshape(N_CHIP, full.shape[0] // N_CHIP, full.shape[1])
    r, c = full.shape
    return jnp.transpose(full.reshape(r, N_CHIP, c // N_CHIP), (1, 0, 2))


def _with_own(land_shape, dtype, own, slot):
    return lax.dynamic_update_slice(lax.empty(land_shape, dtype), own[None], (slot,) + (0,) * own.ndim)


def kernel(x, mix_norm_g, w_in, mu_shift, decay_w0, decay_w2, iclr_a0, iclr_a2, gate_g2, k_k, k_a, r_k, ln_x_w, ln_x_b, attn_out_g, w_out, ffn_norm_g, w_gate, w_up, w_down, final_norm_g, loss_target, m_mix_norm_g, m_w_in, m_mu_shift, m_decay_w0, m_decay_w2, m_iclr_a0, m_iclr_a2, m_gate_g2, m_k_k, m_k_a, m_r_k, m_ln_x_w, m_ln_x_b, m_attn_out_g, m_w_out, m_ffn_norm_g, m_w_gate, m_w_up, m_w_down, m_final_norm_g, v_mix_norm_g, v_w_in, v_mu_shift, v_decay_w0, v_decay_w2, v_iclr_a0, v_iclr_a2, v_gate_g2, v_k_k, v_k_a, v_r_k, v_ln_x_w, v_ln_x_b, v_attn_out_g, v_w_out, v_ffn_norm_g, v_w_gate, v_w_up, v_w_down, v_final_norm_g):
    names = ("mix_norm_g", "w_in", "mu_shift", "decay_w0", "decay_w2", "iclr_a0", "iclr_a2", "gate_g2", "k_k", "k_a",
             "r_k", "ln_x_w", "ln_x_b", "attn_out_g", "w_out", "ffn_norm_g", "w_gate", "w_up", "w_down", "final_norm_g")
    w = dict(zip(names, (mix_norm_g, w_in, mu_shift, decay_w0, decay_w2, iclr_a0, iclr_a2, gate_g2, k_k, k_a, r_k,
                         ln_x_w, ln_x_b, attn_out_g, w_out, ffn_norm_g, w_gate, w_up, w_down, final_norm_g)))
    m = dict(zip(names, (m_mix_norm_g, m_w_in, m_mu_shift, m_decay_w0, m_decay_w2, m_iclr_a0, m_iclr_a2, m_gate_g2,
                         m_k_k, m_k_a, m_r_k, m_ln_x_w, m_ln_x_b, m_attn_out_g, m_w_out, m_ffn_norm_g, m_w_gate,
                         m_w_up, m_w_down, m_final_norm_g)))
    v = dict(zip(names, (v_mix_norm_g, v_w_in, v_mu_shift, v_decay_w0, v_decay_w2, v_iclr_a0, v_iclr_a2, v_gate_g2,
                         v_k_k, v_k_a, v_r_k, v_ln_x_w, v_ln_x_b, v_attn_out_g, v_w_out, v_ffn_norm_g, v_w_gate,
                         v_w_up, v_w_down, v_final_norm_g)))
    first = ("w_in",) + LORAS
    rest = ("w_out", "w_gate", "w_up", "w_down")
    xi, yi, ci = lax.axis_index("x"), lax.axis_index("y"), lax.axis_index("c")
    my_chip, my_dev = 2 * xi + yi, 4 * xi + 2 * yi + ci
    gather, scatter = ("gather",) * 4, ("scatter",) * 4

    sh = lambda z, n: jnp.transpose(z[0]) if n in _TRANSPOSED else z[0]
    mine = [sh(w["w_in"], "w_in").astype(BF16)] + [w[n][0] for n in LORAS]
    early = _swap_start(mine, [_with_own((N_CHIP,) + a.shape, a.dtype, a, my_chip) for a in mine], gather, "gather_first_start")
    wb = {n: (sh(w[n], n) + early[4][0, 0]).astype(BF16) for n in rest}
    lands = [_with_own((N_CHIP,) + wb[n].shape, BF16, wb[n], my_chip) for n in rest]
    ssem, rsem, srcs_thru, lands_thru, tok = _swap_start([wb[n] for n in rest], lands, gather, "gather_rest_start")
    got = _swap_wait(early[0], early[1], early[2], early[3], tok, gather, "gather_first_wait")
    win, w2, a2, g2m = (_full(n, z) for n, z in zip(first, _swap_gathered(got, "gather_first_halves")))

    vecs = {n: w[n].reshape(1, sz) for n, sz in VECS}
    vecs["mix_norm_g"] = vecs["mix_norm_g"] + tok[0, 0]

    def get_rest(after):
        halves = _swap_wait(ssem, rsem, srcs_thru, lands_thru, after, gather, "gather_rest_wait")
        return [_full(n, z) for n, z in zip(rest, _swap_gathered(halves, "gather_rest_halves"))]

    flight = []

    def my_half(g):
        h = g.shape[1] // 2
        return lax.dynamic_slice(g, (my_chip, ci * h, 0), (1, h, g.shape[2]))[0]

    def send_rest(gw):
        gs = [_by_chip(n, gw[n]).astype(BF16) for n in rest]
        into = [_with_own((N_DEV,) + my_half(g).shape, BF16, my_half(g), my_dev) for g in gs]
        flight.extend(_swap_start(gs, into, scatter, "exchange_rest_start"))
        return flight[4]

    loss8, dx, gw, gv = _local_step(x[0], loss_target[0], win, vecs, w2, a2, g2m, get_rest, send_rest)

    small = _rowsum_small([gv[n] for n, _ in VECS], loss8)
    gs = [_by_chip(n, gw[n]).astype(BF16) for n in first]
    into = [_with_own((N_DEV,) + my_half(g).shape, BF16, my_half(g), my_dev) for g in gs]
    into.append(_with_own((N_DEV,) + small.shape, F32, small, my_dev))
    last = _swap_start(gs + [small], into, scatter + ("all",), "exchange_first_start")

    core = jnp.reshape(ci, (1,)).astype(jnp.int32)

    def update(group, rbufs, tag):
        sums = [_reduce8(rb, core, _SUM_TILE[n], "reduce_" + n) for n, rb in zip(group, rbufs)]
        gsum = _join_halves(sums, "join_halves_" + tag)
        out = {}
        for n, g in zip(group, gsum):
            r = _adamw_call(g, sh(w[n], n)[None], sh(m[n], n)[None], sh(v[n], n)[None], _ADAM_TILE[n], "adamw_" + n)
            out[n] = [jnp.transpose(z[0])[None] for z in r] if n in _TRANSPOSED else r
        return out

    res = update(rest, _swap_wait(flight[0], flight[1], flight[2], flight[3], last[4], scatter, "exchange_rest_wait"), "rest")
    got = _swap_wait(last[0], last[1], last[2], last[3], res["w_down"][1], scatter + ("all",), "exchange_first_wait")
    res.update(update(first, got[:4], "first"))
    rows = lambda d: [d[n].reshape(1, sz) for n, sz in VECS]
    small_res = _reduce_adamw_small(got[4], rows(w), rows(m), rows(v))

    outs = []
    for k in range(4):
        piece = {n: r[k] for n, r in res.items()}
        for i, (n, _) in enumerate(VECS):
            piece[n] = small_res[k * len(VECS) + i].reshape(w[n].shape)
        outs.extend(piece[n] for n in names)
    return (small_res[-1][0, 0], dx[None], *outs)
```

```python
import jax
import jax.numpy as jnp
from jax import lax
from jax.experimental import pallas as pl
from jax.experimental.pallas import tpu as pltpu

F32 = jnp.float32
BF16 = jnp.bfloat16

D_MODEL = 1024
HEAD_DIM = 64
RW = 512
N_PAIR = RW // 128
SHIFT_COLS = 1792
IN_COLS = 3328
D_FF = 2816
NORM_EPS = 1e-6
GN_EPS = 64e-5
CHUNK = 64
SUB = 16
WKV_PASSES = 1
ATTN_PASSES = 1
ATTN_BLOCK = 128
DILATIONS = (1, 4, 16)
NEG = -1e30
ADAM_LR, ADAM_B1, ADAM_B2, ADAM_EPS, ADAM_WD, ADAM_STEP = 0.001, 0.9, 0.999, 1e-08, 0.01, 10
VMEM_LIMIT = 56 * 1024 * 1024
MESH = pl.DeviceIdType.MESH


def _params(sem=None, **kw):
    return pltpu.CompilerParams(dimension_semantics=sem, vmem_limit_bytes=VMEM_LIMIT, **kw)


def _dot(a, b, prec=None):
    return lax.dot_general(a, b, (((1,), (0,)), ((), ())), preferred_element_type=F32, precision=prec)


def _dot_nt(a, b, prec=None):
    return lax.dot_general(a, b, (((1,), (1,)), ((), ())), preferred_element_type=F32, precision=prec)


def _dot_tn(a, b, prec=None):
    return lax.dot_general(a, b, (((0,), (0,)), ((), ())), preferred_element_type=F32, precision=prec)


_FORMS = {"nn": ((1,), (0,)), "nt": ((1,), (1,)), "tn": ((0,), (0,))}


def _dg(a, b, form):
    if a.ndim == 3 or b.ndim == 3:
        nb = a.shape[0] if a.ndim == 3 else b.shape[0]
        return jnp.stack([_dg(a[i] if a.ndim == 3 else a, b[i] if b.ndim == 3 else b, form) for i in range(nb)], axis=0)
    return lax.dot_general(a, b, (_FORMS[form], ((), ())), preferred_element_type=F32)


def _split2(x):
    hi = x.astype(BF16)
    return hi, (x - hi.astype(F32)).astype(BF16)


def _split3(x):
    hi = x.astype(BF16)
    rest = x - hi.astype(F32)
    mid = rest.astype(BF16)
    return hi, mid, (rest - mid.astype(F32)).astype(BF16)


def _mm_raw(a, b, form, mode):
    if mode == 1:
        return _dg(a.astype(BF16), b.astype(BF16), form)
    if mode == 3:
        ah, al = _split2(a)
        bh, bl = _split2(b)
        return _dg(ah, bh, form) + (_dg(ah, bl, form) + _dg(al, bh, form))
    if mode == "L3":
        ab = a.astype(BF16)
        b1, b2, b3 = _split3(b)
        return _dg(ab, b1, form) + (_dg(ab, b2, form) + _dg(ab, b3, form))
    assert mode == "R3", mode
    bb = b.astype(BF16)
    a1, a2, a3 = _split3(a)
    return _dg(a1, bb, form) + (_dg(a2, bb, form) + _dg(a3, bb, form))


def _mm(a, b, form, mode):
    @jax.custom_vjp
    def f(a, b):
        return _mm_raw(a, b, form, mode)

    def fwd(a, b):
        return _mm_raw(a, b, form, mode), (a, b)

    def bwd(res, ct):
        a, b = res
        la = {1: 1, 3: 3, "L3": None, "R3": "R3"}[mode]
        lb = {1: 1, 3: 3, "L3": "L3", "R3": None}[mode]
        if form == "nn":
            da = None if la is None else _mm_raw(ct, b, "nt", la)
            db = None if lb is None else _mm_raw(a, ct, "tn", lb)
        elif form == "nt":
            da = None if la is None else _mm_raw(ct, b, "nn", la)
            db = None if lb is None else _mm_raw(ct, a, "tn", "R3" if lb == "L3" else lb)
        else:
            da = None if la is None else _mm_raw(b, ct, "nt", "L3" if la == "R3" else la)
            db = None if lb is None else _mm_raw(a, ct, "nn", lb)
        return (jnp.zeros_like(a) if da is None else da, jnp.zeros_like(b) if db is None else db)

    f.defvjp(fwd, bwd)
    return f(a, b)


def _seg_ones(n):
    r = lax.broadcasted_iota(jnp.int32, (n, n), 0) // HEAD_DIM
    c = lax.broadcasted_iota(jnp.int32, (n, n), 1) // HEAD_DIM
    return (r == c).astype(F32)


def _segsum(x, seg):
    return _mm(x, seg, "nn", "R3")


def _rms_fwd(x, g):
    rstd = lax.rsqrt(jnp.mean(x * x, axis=-1, keepdims=True) + NORM_EPS)
    return x * rstd * g


def _rms_bwd(dy, x, g):
    rstd = lax.rsqrt(jnp.mean(x * x, axis=-1, keepdims=True) + NORM_EPS)
    xn = x * rstd
    dxn = dy * g
    dx = rstd * (dxn - xn * jnp.mean(dxn * xn, axis=-1, keepdims=True))
    return dx, dy * xn


def _sigmoid(x):
    return 1.0 / (1.0 + jnp.exp(-x))


def _softplus(x):
    return jnp.maximum(x, 0.0) + jnp.log(1.0 + jnp.exp(-jnp.abs(x)))


def _acc(ref, val, first):
    @pl.when(first)
    def _():
        ref[...] = val

    @pl.when(jnp.logical_not(first))
    def _():
        ref[...] += val


def _colsum8(v):
    rows, n = v.shape
    return jnp.sum(v.reshape(rows // 8, 8, n), axis=0)


def _prep_fn(p, pprev, mu, w0, w2p, a0, a2p, g2, k_k, k_a):
    seg = _seg_ones(RW)
    ps = p + (pprev - p) * mu
    r = ps[:, 0:RW]
    k = ps[:, RW:2 * RW]
    v = ps[:, 2 * RW:3 * RW]
    xwa = ps[:, 3 * RW:3 * RW + 128]
    xg = ps[:, 3 * RW + 128:3 * RW + 256]
    wraw = -_softplus(-(w0 + _mm(jnp.tanh(xwa), w2p, "nn", 3))) - 0.5
    lw = -jnp.exp(wraw)
    a = _sigmoid(a0 + _mm(xwa, a2p, "nn", 3))
    g = _mm(_sigmoid(xg), g2, "nn", 3)
    kk = k * k_k
    kk = kk / jnp.maximum(jnp.sqrt(_segsum(kk * kk, seg)), 1e-12)
    k2 = k * (1.0 + (a - 1.0) * k_a)
    return r, lw, k2, v, kk, a, g


def _transposed(z):
    return jnp.stack([z[i].T for i in range(z.shape[0])], axis=0) if z.ndim == 3 else z.T


def _solve_unit_lower(lmat, rhs):
    c = lmat.shape[-1]
    row = lax.broadcasted_iota(jnp.int32, (c, c), 0)
    col = lax.broadcasted_iota(jnp.int32, (c, c), 1)
    eye = (row == col).astype(F32)
    ld = jnp.where(row // SUB == col // SUB, lmat, 0.0)
    lo = lmat - ld
    x = eye + ld
    m = ld
    mm = lambda p, q: _mm(p, q, "nn", WKV_PASSES)
    cat = jnp.concatenate
    m = mm(m, m)
    for _ in range(2):
        mx = mm(m, cat([m, x], axis=-1))
        m, x = mx[..., :c], x + mx[..., c:]
    x = x + mm(m, x)
    gw = mm(x, cat([lo, rhs], axis=-1))
    g, w = gw[..., :c], gw[..., c:]
    gg = mm(g, cat([g, w], axis=-1))
    w = w + gg[..., c:]
    return w + mm(gg[..., :c], w)


def _wkv_chunk_fn(s0, r, lw, k, v, kk, a):
    c = r.shape[-2]
    n = 2 * c
    row = lax.broadcasted_iota(jnp.int32, (n, n), 0)
    col = lax.broadcasted_iota(jnp.int32, (n, n), 1)
    same = (row // c) == (col // c)
    incl = jnp.logical_and(row >= col, same)
    strict = jnp.logical_and(row > col, same)
    sel = (lax.broadcasted_iota(jnp.int32, (n, 128), 0) // c) == (lax.broadcasted_iota(jnp.int32, (n, 128), 1) // HEAD_DIM)
    two = lambda z: jnp.concatenate([z, z], axis=-2)
    lw2 = two(lw)
    mm = lambda p_, q_, form: _mm(p_, q_, form, WKV_PASSES)
    cl = _mm(incl.astype(F32), lw2, "nn", "L3")
    p = jnp.exp(cl)
    pinv = jnp.exp(-cl)
    pprev = jnp.exp(cl - lw2)
    kk2 = two(kk)
    at = jnp.where(sel, -kk2 * pprev, 0.0)
    bt = jnp.where(sel, kk2 * two(a) * pinv, 0.0)
    kt = jnp.where(sel, two(k) * pinv, 0.0)
    rt = jnp.where(sel, two(r) * p, 0.0)
    vt = jnp.where(sel, two(v), 0.0)
    cat = jnp.concatenate
    bk = cat([bt, kt], axis=-2)
    abk = mm(at, bk, "nt")
    rbk = mm(rt, bk, "nt")
    ab, ak = jnp.where(strict, abk[..., :n], 0.0), jnp.where(strict, abk[..., n:], 0.0)
    rb, rk = jnp.where(incl, rbk[..., :n], 0.0), jnp.where(incl, rbk[..., n:], 0.0)
    s0t = _transposed(s0)
    u = _solve_unit_lower(ab, mm(cat([at, ak], axis=-1), cat([s0t, vt], axis=-2), "nn"))
    y2 = mm(cat([rt, rb, rk], axis=-1), cat([s0t, u, vt], axis=-2), "nn")
    plast = jnp.exp(jnp.sum(lw, axis=-2, keepdims=True))
    s1 = (s0 + mm(cat([u, vt], axis=-2), bk, "tn")) * plast
    r2 = lax.broadcasted_iota(jnp.int32, (128, 128), 0) // HEAD_DIM
    c2 = lax.broadcasted_iota(jnp.int32, (128, 128), 1) // HEAD_DIM
    return y2[..., :c, :] + y2[..., c:, :], jnp.where(r2 == c2, s1, 0.0)


def _post_fn(y, r, k2, v, g, lnw, lnb, rk):
    seg = _seg_ones(RW)
    mean = _segsum(y, seg) * (1.0 / HEAD_DIM)
    yc = y - mean
    var = _segsum(yc * yc, seg) * (1.0 / HEAD_DIM)
    yn = yc * lax.rsqrt(var + GN_EPS)
    out = yn * lnw + lnb + _segsum(r * k2 * rk, seg) * v
    return out * g


def _attn_block_fn(q, kc, vc, kp=None, vp=None):
    n = ATTN_BLOCK
    qi = lax.broadcasted_iota(jnp.int32, (n, n), 0)
    kj = lax.broadcasted_iota(jnp.int32, (n, n), 1)
    lane = lax.broadcasted_iota(jnp.int32, (1, 128), 1)
    scale = HEAD_DIM ** -0.5
    valid = kj <= qi
    keys, vals = kc, vc
    if kp is not None:
        valid = jnp.concatenate([valid, kj >= qi], axis=-1)
        keys, vals = jnp.concatenate([kc, kp], axis=-2), jnp.concatenate([vc, vp], axis=-2)
    os_, ls_ = [], []
    for h in range(2):
        mh = (lane // HEAD_DIM) == h
        s = jnp.where(valid, _mm(jnp.where(mh, q, 0.0), keys, "nt", ATTN_PASSES) * scale, NEG)
        m = jnp.max(s, axis=-1, keepdims=True)
        p = jnp.exp(s - m)
        den = jnp.sum(p, axis=-1, keepdims=True)
        os_.append(_mm(p, vals, "nn", ATTN_PASSES) / den)
        ls_.append(m + jnp.log(den))
    m0 = (lane // HEAD_DIM) == 0
    return jnp.where(m0, os_[0], os_[1]), jnp.where(m0, ls_[0], ls_[1])


def _combine_fn(o1, o2, o3, l1, l2, l3, og):
    seg = _seg_ones(o1.shape[-1])
    m = jnp.maximum(jnp.maximum(l1, l2), l3)
    e1, e2, e3 = jnp.exp(l1 - m), jnp.exp(l2 - m), jnp.exp(l3 - m)
    o = (e1 * o1 + e2 * o2 + e3 * o3) / (e1 + e2 + e3)
    o = o * lax.rsqrt(_segsum(o * o, seg) * (1.0 / HEAD_DIM) + NORM_EPS)
    return o * og


def _in_proj(x, g1, win):
    t = x.shape[0]
    tm = 512

    def body(x_ref, g_ref, w_ref, h_ref, pa_ref, qkv_ref):
        h = _rms_fwd(x_ref[...], g_ref[...]).astype(BF16)
        h_ref[...] = h
        proj = _dot_nt(h, w_ref[...])
        pa_ref[...] = proj[:, :SHIFT_COLS]
        for j in range(3):
            for p in range(N_PAIR):
                c0 = SHIFT_COLS + j * RW + p * 128
                qkv_ref[j, p] = proj[:, c0:c0 + 128]

    return pl.pallas_call(
        body, name="in_proj", grid=(t // tm,),
        in_specs=[pl.BlockSpec((tm, D_MODEL), lambda i: (i, 0)), pl.BlockSpec((1, D_MODEL), lambda i: (0, 0)),
                  pl.BlockSpec((IN_COLS, D_MODEL), lambda i: (0, 0))],
        out_specs=[pl.BlockSpec((tm, D_MODEL), lambda i: (i, 0)), pl.BlockSpec((tm, SHIFT_COLS), lambda i: (i, 0)),
                   pl.BlockSpec((3, N_PAIR, tm, 128), lambda i: (0, 0, i, 0))],
        out_shape=[jax.ShapeDtypeStruct((t, D_MODEL), BF16), jax.ShapeDtypeStruct((t, SHIFT_COLS), F32),
                   jax.ShapeDtypeStruct((3, N_PAIR, t, 128), F32)],
        compiler_params=_params(("parallel",)),
    )(x, g1, win)


def _shifted(p, last8, first):
    prow = jnp.where(first, 0.0, last8[7:8, :])
    rolled = pltpu.roll(p, 1, axis=0)
    rid = lax.broadcasted_iota(jnp.int32, p.shape, 0)
    return jnp.where(rid == 0, prow, rolled)


_PREP_TM = 256


def _prep_specs(tm):
    vec = lambda n: pl.BlockSpec((1, n), lambda i: (0, 0))
    mat = lambda r, n: pl.BlockSpec((r, n), lambda i: (0, 0))
    return [vec(SHIFT_COLS), vec(RW), mat(128, RW), vec(RW), mat(128, RW), mat(128, RW), vec(RW), vec(RW)]


def _prep_fwd(proj, pw):
    t = proj.shape[0]
    tm = _PREP_TM

    def body(p_ref, l8_ref, mu, w0, w2p, a0, a2p, g2, k_k, k_a, *outs):
        p = p_ref[...]
        pprev = _shifted(p, l8_ref[...], pl.program_id(0) == 0)
        res = _prep_fn(p, pprev, mu[...], w0[...], w2p[...], a0[...], a2p[...], g2[...], k_k[...], k_a[...])
        for o_ref, val in zip(outs, res):
            o_ref[...] = val

    row = pl.BlockSpec((tm, RW), lambda i: (i, 0))
    return pl.pallas_call(
        body, name="rwkv_prep", grid=(t // tm,),
        in_specs=[pl.BlockSpec((tm, SHIFT_COLS), lambda i: (i, 0)),
                  pl.BlockSpec((8, SHIFT_COLS), lambda i: (jnp.maximum(i * (tm // 8) - 1, 0), 0))] + _prep_specs(tm),
        out_specs=[row] * 7,
        out_shape=[jax.ShapeDtypeStruct((t, RW), F32)] * 7,
        compiler_params=_params(("parallel",)),
    )(proj, proj, *pw)


def _pairs(ref):
    return jnp.stack([ref[:, 128 * p:128 * (p + 1)] for p in range(N_PAIR)], axis=0)


def _wkv_fwd(r, lw, k2, v, kk, a):
    t = r.shape[0]
    nc = t // CHUNK

    def body(r_ref, lw_ref, k_ref, v_ref, kk_ref, a_ref, y_ref, s_ref, st):
        @pl.when(pl.program_id(0) == 0)
        def _():
            st[...] = jnp.zeros_like(st)

        s0 = st[...]
        s_ref[0] = s0
        y, s1 = _wkv_chunk_fn(s0, *[_pairs(ref) for ref in (r_ref, lw_ref, k_ref, v_ref, kk_ref, a_ref)])
        for p in range(N_PAIR):
            y_ref[:, 128 * p:128 * (p + 1)] = y[p]
        st[...] = s1

    blk = pl.BlockSpec((CHUNK, RW), lambda c: (c, 0))
    return pl.pallas_call(
        body, name="wkv_fwd", grid=(nc,),
        in_specs=[blk] * 6,
        out_specs=[blk, pl.BlockSpec((1, N_PAIR, 128, 128), lambda c: (c, 0, 0, 0))],
        out_shape=[jax.ShapeDtypeStruct((t, RW), F32), jax.ShapeDtypeStruct((nc, N_PAIR, 128, 128), F32)],
        scratch_shapes=[pltpu.VMEM((N_PAIR, 128, 128), F32)],
        compiler_params=_params(("arbitrary",)),
    )(r, lw, k2, v, kk, a)


_POST_TM = 256


def _post_fwd(y, r, k2, v, g, lnw, lnb, rk):
    t = y.shape[0]
    tm = _POST_TM

    def body(y_ref, r_ref, k_ref, v_ref, g_ref, lnw_ref, lnb_ref, rk_ref, o_ref):
        o_ref[...] = _post_fn(y_ref[...], r_ref[...], k_ref[...], v_ref[...], g_ref[...],
                              lnw_ref[...], lnb_ref[...], rk_ref[...]).astype(BF16)

    row = pl.BlockSpec((tm, RW), lambda i: (i, 0))
    vec = pl.BlockSpec((1, RW), lambda i: (0, 0))
    return pl.pallas_call(
        body, name="rwkv_post", grid=(t // tm,),
        in_specs=[row] * 5 + [vec] * 3, out_specs=row,
        out_shape=jax.ShapeDtypeStruct((t, RW), BF16),
        compiler_params=_params(("parallel",)),
    )(y, r, k2, v, g, lnw, lnb, rk)


ATTN_GROUP = 2


def _dilated_rows(d, r, n):
    if d == 1:
        return pl.ds(pl.multiple_of(n * ATTN_BLOCK, ATTN_BLOCK), ATTN_BLOCK)
    return pl.ds(r + n * (ATTN_BLOCK * d), ATTN_BLOCK, stride=d)


def _for_each_sequence(t, unit):
    for di, d in enumerate(DILATIONS):

        @pl.when(pl.program_id(1) == di)
        def _(di=di, d=d):
            nb = t // (ATTN_BLOCK * d)
            if d == 1:
                unit(di, [(d, 0, 0)], False)
                unit(di, [(d, 0, 1)], True)
                lax.fori_loop(1, nb // 2, lambda k, c: (unit(di, [(d, 0, 2 * k), (d, 0, 2 * k + 1)], True), c)[1], 0)
            else:

                def residues(r, carry):
                    unit(di, [(d, r, 0), (d, r + d // 2, 0)], False)
                    if nb > 1:
                        lax.fori_loop(1, nb, lambda n, c: (unit(di, [(d, r, n), (d, r + d // 2, n)], True), c)[1], 0)
                    return carry

                lax.fori_loop(0, d // 2, residues, 0)


def _take(ref, lead, rows_list):
    return jnp.stack([ref.at[(*lead, g)][rows, :] for rows in rows_list for g in range(ATTN_GROUP)], axis=0)


def _put(ref, lead, rows_list, val, add=False):
    k = 0
    for rows in rows_list:
        for g in range(ATTN_GROUP):
            if add:
                ref.at[(*lead, g)][rows, :] += val[k]
            else:
                ref.at[(*lead, g)][rows, :] = val[k]
            k += 1


def _attn_fwd(qkv):
    t = qkv.shape[2]

    def body(q_ref, k_ref, v_ref, o_ref, l_ref):
        def unit(di, places, has_prev):
            cur = [_dilated_rows(d, r, n) for d, r, n in places]
            args = [_take(ref, (0,), cur) for ref in (q_ref, k_ref, v_ref)]
            if has_prev:
                prv = [_dilated_rows(d, r, n - 1) for d, r, n in places]
                args += [_take(ref, (0,), prv) for ref in (k_ref, v_ref)]
            o, lse = _attn_block_fn(*args)
            _put(o_ref, (0,), cur, o)
            _put(l_ref, (0,), cur, lse)

        _for_each_sequence(t, unit)

    spec = lambda j: pl.BlockSpec((1, ATTN_GROUP, t, 128), lambda i, b: (j, i, 0, 0))
    out = pl.BlockSpec((1, ATTN_GROUP, t, 128), lambda i, b: (b, i, 0, 0))
    return pl.pallas_call(
        body, name="attn_fwd", grid=(N_PAIR // ATTN_GROUP, len(DILATIONS)),
        in_specs=[spec(0), spec(1), spec(2)], out_specs=[out, out],
        out_shape=[jax.ShapeDtypeStruct((3, N_PAIR, t, 128), F32)] * 2,
        compiler_params=_params(("parallel", "arbitrary")),
    )(qkv, qkv, qkv)


_COMB_TM = 256


def _combine_fwd(o, l, og):
    t = o.shape[2]
    tm = _COMB_TM

    def body(o_ref, l_ref, og_ref, y_ref):
        for p in range(N_PAIR):
            cols = slice(128 * p, 128 * (p + 1))
            y_ref[:, cols] = _combine_fn(o_ref[0, p], o_ref[1, p], o_ref[2, p], l_ref[0, p], l_ref[1, p], l_ref[2, p],
                                         og_ref[:, cols]).astype(BF16)

    blk = pl.BlockSpec((3, N_PAIR, tm, 128), lambda i: (0, 0, i, 0))
    return pl.pallas_call(
        body, name="attn_combine", grid=(t // tm,),
        in_specs=[blk, blk, pl.BlockSpec((1, RW), lambda i: (0, 0))], out_specs=pl.BlockSpec((tm, RW), lambda i: (i, 0)),
        out_shape=jax.ShapeDtypeStruct((t, RW), BF16),
        compiler_params=_params(("parallel",)),
    )(o, l, og)


def _out_proj(x, ycat, wout, g2):
    t = x.shape[0]
    tm = 256

    def body(x_ref, y_ref, w_ref, g_ref, x1_ref, h_ref):
        x1 = x_ref[...] + _dot(y_ref[...], w_ref[...])
        x1_ref[...] = x1
        h_ref[...] = _rms_fwd(x1, g_ref[...]).astype(BF16)

    row = pl.BlockSpec((tm, D_MODEL), lambda i: (i, 0))
    return pl.pallas_call(
        body, name="out_proj", grid=(t // tm,),
        in_specs=[row, row, pl.BlockSpec((D_MODEL, D_MODEL), lambda i: (0, 0)), pl.BlockSpec((1, D_MODEL), lambda i: (0, 0))],
        out_specs=[row, row],
        out_shape=[jax.ShapeDtypeStruct((t, D_MODEL), F32), jax.ShapeDtypeStruct((t, D_MODEL), BF16)],
        compiler_params=_params(("parallel",)),
    )(x, ycat, wout, g2)


def _ffn_up(h2, wg, wu):
    t = h2.shape[0]
    tm = 512

    def body(h_ref, wg_ref, wu_ref, gt_ref, up_ref, act_ref):
        h = h_ref[...]
        gt = _dot_nt(h, wg_ref[...])
        up = _dot_nt(h, wu_ref[...])
        gt_ref[...] = gt.astype(BF16)
        up_ref[...] = up.astype(BF16)
        act_ref[...] = (gt * _sigmoid(gt) * up).astype(BF16)

    wide = pl.BlockSpec((tm, D_FF), lambda i: (i, 0))
    wsp = pl.BlockSpec((D_FF, D_MODEL), lambda i: (0, 0))
    return pl.pallas_call(
        body, name="ffn_up", grid=(t // tm,),
        in_specs=[pl.BlockSpec((tm, D_MODEL), lambda i: (i, 0)), wsp, wsp],
        out_specs=[wide, wide, wide],
        out_shape=[jax.ShapeDtypeStruct((t, D_FF), BF16)] * 3,
        compiler_params=_params(("parallel",)),
    )(h2, wg, wu)


def _ffn_down_loss(x1, act, wd, gf, tgt):
    t = x1.shape[0]
    tm = 256

    def body(x1_ref, a_ref, w_ref, g_ref, t_ref, dx_ref, dxb_ref, loss_ref, dg_ref):
        first = pl.program_id(0) == 0
        x2 = x1_ref[...] + _dot(a_ref[...], w_ref[...])
        g = g_ref[...]
        diff = _rms_fwd(x2, g) - t_ref[...]
        lrow = 0.5 * jnp.sum(_colsum8(diff * diff), axis=1, keepdims=True) * (1.0 / D_MODEL)
        _acc(loss_ref, jnp.broadcast_to(lrow, (8, 128)), first)
        dx2, dgr = _rms_bwd(diff * (1.0 / D_MODEL), x2, g)
        dx_ref[...] = dx2
        dxb_ref[...] = dx2.astype(BF16)
        _acc(dg_ref, _colsum8(dgr), first)

    row = pl.BlockSpec((tm, D_MODEL), lambda i: (i, 0))
    return pl.pallas_call(
        body, name="ffn_down_loss", grid=(t // tm,),
        in_specs=[row, pl.BlockSpec((tm, D_FF), lambda i: (i, 0)), pl.BlockSpec((D_FF, D_MODEL), lambda i: (0, 0)),
                  pl.BlockSpec((1, D_MODEL), lambda i: (0, 0)), row],
        out_specs=[row, row, pl.BlockSpec((8, 128), lambda i: (0, 0)), pl.BlockSpec((8, D_MODEL), lambda i: (0, 0))],
        out_shape=[jax.ShapeDtypeStruct((t, D_MODEL), F32), jax.ShapeDtypeStruct((t, D_MODEL), BF16),
                   jax.ShapeDtypeStruct((8, 128), F32), jax.ShapeDtypeStruct((8, D_MODEL), F32)],
        compiler_params=_params(("arbitrary",)),
    )(x1, act, wd, gf, tgt)


def _ffn_bwd_act(dx2b, wd, gt, up):
    t = dx2b.shape[0]
    tm = 512

    def body(dx_ref, w_ref, gt_ref, up_ref, dgt_ref, dup_ref):
        dact = _dot_nt(dx_ref[...], w_ref[...])
        gt = gt_ref[...].astype(F32)
        sg = _sigmoid(gt)
        dgt_ref[...] = (dact * up_ref[...].astype(F32) * sg * (1.0 + gt * (1.0 - sg))).astype(BF16)
        dup_ref[...] = (dact * gt * sg).astype(BF16)

    wide = pl.BlockSpec((tm, D_FF), lambda i: (i, 0))
    return pl.pallas_call(
        body, name="ffn_bwd_act", grid=(t // tm,),
        in_specs=[pl.BlockSpec((tm, D_MODEL), lambda i: (i, 0)), pl.BlockSpec((D_FF, D_MODEL), lambda i: (0, 0)), wide, wide],
        out_specs=[wide, wide],
        out_shape=[jax.ShapeDtypeStruct((t, D_FF), BF16)] * 2,
        compiler_params=_params(("parallel",)),
    )(dx2b, wd, gt, up)


def _ffn_bwd_h(dgt, dup, wg, wu, dx2, x1, g2, wout):
    t = dgt.shape[0]
    tm = 256

    def body(dgt_ref, dup_ref, wg_ref, wu_ref, dx2_ref, x1_ref, g_ref, wo_ref, dx1_ref, dx1b_ref, dya_ref, dyb_ref, dg_ref):
        dh = _dot(dgt_ref[...], wg_ref[...]) + _dot(dup_ref[...], wu_ref[...])
        dxn, dgr = _rms_bwd(dh, x1_ref[...], g_ref[...])
        dx1 = dx2_ref[...] + dxn
        dx1_ref[...] = dx1
        dx1b = dx1.astype(BF16)
        dx1b_ref[...] = dx1b
        dy = _dot_nt(dx1b, wo_ref[...])
        dya_ref[...] = dy[:, :RW]
        dyb_ref[...] = dy[:, RW:]
        _acc(dg_ref, _colsum8(dgr), pl.program_id(0) == 0)

    wide = pl.BlockSpec((tm, D_FF), lambda i: (i, 0))
    row = pl.BlockSpec((tm, D_MODEL), lambda i: (i, 0))
    half = pl.BlockSpec((tm, RW), lambda i: (i, 0))
    wsp = pl.BlockSpec((D_FF, D_MODEL), lambda i: (0, 0))
    return pl.pallas_call(
        body, name="ffn_bwd_h", grid=(t // tm,),
        in_specs=[wide, wide, wsp, wsp, row, row, pl.BlockSpec((1, D_MODEL), lambda i: (0, 0)),
                  pl.BlockSpec((D_MODEL, D_MODEL), lambda i: (0, 0))],
        out_specs=[row, row, half, half, pl.BlockSpec((8, D_MODEL), lambda i: (0, 0))],
        out_shape=[jax.ShapeDtypeStruct((t, D_MODEL), F32), jax.ShapeDtypeStruct((t, D_MODEL), BF16),
                   jax.ShapeDtypeStruct((t, RW), F32), jax.ShapeDtypeStruct((t, RW), F32),
                   jax.ShapeDtypeStruct((8, D_MODEL), F32)],
        compiler_params=_params(("arbitrary",)),
    )(dgt, dup, wg, wu, dx2, x1, g2, wout)


def _wgrad(a, b, tk, tn, name):
    t, kdim = a.shape
    ndim = b.shape[1]

    def body(a_ref, b_ref, o_ref):
        o_ref[...] = _dot_tn(a_ref[...], b_ref[...])

    return pl.pallas_call(
        body, name=name, grid=(kdim // tk, ndim // tn),
        in_specs=[pl.BlockSpec((t, tk), lambda i, j: (0, i)), pl.BlockSpec((t, tn), lambda i, j: (0, j))],
        out_specs=pl.BlockSpec((tk, tn), lambda i, j: (i, j)),
        out_shape=jax.ShapeDtypeStruct((kdim, ndim), F32),
        compiler_params=_params(("parallel", "parallel")),
    )(a, b)


def _post_bwd(dya, y, r, k2, v, g, lnw, lnb, rk):
    t = y.shape[0]
    tm = _POST_TM

    def body(d_ref, y_ref, r_ref, k_ref, v_ref, g_ref, lnw_ref, lnb_ref, rk_ref,
             dy_ref, dr_ref, dk_ref, dv_ref, dg_ref, dlnw_ref, dlnb_ref, drk_ref):
        first = pl.program_id(0) == 0
        ones = jnp.ones((tm, 1), F32)
        prim = (y_ref[...], r_ref[...], k_ref[...], v_ref[...], g_ref[...],
                ones * lnw_ref[...], ones * lnb_ref[...], ones * rk_ref[...])
        _, vjp = jax.vjp(_post_fn, *prim)
        dy, dr, dk, dv, dg, dlnw, dlnb, drk = vjp(d_ref[...])
        dy_ref[...] = dy
        dr_ref[...] = dr
        dk_ref[...] = dk
        dv_ref[...] = dv
        dg_ref[...] = dg
        _acc(dlnw_ref, _colsum8(dlnw), first)
        _acc(dlnb_ref, _colsum8(dlnb), first)
        _acc(drk_ref, _colsum8(drk), first)

    row = pl.BlockSpec((tm, RW), lambda i: (i, 0))
    vec = pl.BlockSpec((1, RW), lambda i: (0, 0))
    part = pl.BlockSpec((8, RW), lambda i: (0, 0))
    return pl.pallas_call(
        body, name="rwkv_post_bwd", grid=(t // tm,),
        in_specs=[row] * 6 + [vec] * 3, out_specs=[row] * 5 + [part] * 3,
        out_shape=[jax.ShapeDtypeStruct((t, RW), F32)] * 5 + [jax.ShapeDtypeStruct((8, RW), F32)] * 3,
        compiler_params=_params(("arbitrary",)),
    )(dya, y, r, k2, v, g, lnw, lnb, rk)


def _wkv_bwd(dy, s0s, r, lw, k2, v, kk, a):
    t = r.shape[0]
    nc = t // CHUNK

    def body(dy_ref, s_ref, r_ref, lw_ref, k_ref, v_ref, kk_ref, a_ref,
             dr_ref, dlw_ref, dk_ref, dv_ref, dkk_ref, da_ref, ds):
        @pl.when(pl.program_id(0) == 0)
        def _():
            ds[...] = jnp.zeros_like(ds)

        _, vjp = jax.vjp(_wkv_chunk_fn, s_ref[0],
                         *[_pairs(ref) for ref in (r_ref, lw_ref, k_ref, v_ref, kk_ref, a_ref)])
        res = vjp((_pairs(dy_ref), ds[...]))
        ds[...] = res[0]
        for ref, val in zip((dr_ref, dlw_ref, dk_ref, dv_ref, dkk_ref, da_ref), res[1:]):
            for p in range(N_PAIR):
                ref[:, 128 * p:128 * (p + 1)] = val[p]

    blk = pl.BlockSpec((CHUNK, RW), lambda c: (nc - 1 - c, 0))
    return pl.pallas_call(
        body, name="wkv_bwd", grid=(nc,),
        in_specs=[blk, pl.BlockSpec((1, N_PAIR, 128, 128), lambda c: (nc - 1 - c, 0, 0, 0))] + [blk] * 6,
        out_specs=[blk] * 6,
        out_shape=[jax.ShapeDtypeStruct((t, RW), F32)] * 6,
        scratch_shapes=[pltpu.VMEM((N_PAIR, 128, 128), F32)],
        compiler_params=_params(("arbitrary",)),
    )(dy, s0s, r, lw, k2, v, kk, a)


def _prep_bwd(proj, pw, douts):
    t = proj.shape[0]
    tm = _PREP_TM
    nt = t // tm

    def body(p_ref, l8_ref, mu, w0, w2p, a0, a2p, g2, k_k, k_a, dr, dr2, dlw, dk2, dk22, dv, dv2, dkk, da, dg,
             dp_ref, dmu_ref, dw0_ref, dw2_ref, da0_ref, da2_ref, dg2_ref, dkk_ref, dka_ref, carry):
        i = pl.program_id(0)
        first = i == 0

        @pl.when(first)
        def _():
            carry[...] = jnp.zeros_like(carry)

        p = p_ref[...]
        pprev = _shifted(p, l8_ref[...], i == nt - 1)
        ones = jnp.ones((tm, 1), F32)
        prim = (p, pprev, ones * mu[...], ones * w0[...], w2p[...], ones * a0[...], a2p[...], g2[...],
                ones * k_k[...], ones * k_a[...])
        _, vjp = jax.vjp(_prep_fn, *prim)
        dp, dpp, dmu, dw0, dw2, da0, da2, dg2, dkk_, dka = vjp(
            (dr[...] + dr2[...], dlw[...], dk2[...] + dk22[...], dv[...] + dv2[...], dkk[...], da[...], dg[...]))
        up = pltpu.roll(dpp, tm - 1, axis=0)
        rid = lax.broadcasted_iota(jnp.int32, dpp.shape, 0)
        dp_ref[...] = dp + jnp.where(rid == tm - 1, carry[0:1, :], up)
        carry[...] = jnp.broadcast_to(dpp[0:1, :], carry.shape)
        _acc(dmu_ref, _colsum8(dmu), first)
        _acc(dw0_ref, _colsum8(dw0), first)
        _acc(dw2_ref, dw2, first)
        _acc(da0_ref, _colsum8(da0), first)
        _acc(da2_ref, da2, first)
        _acc(dg2_ref, dg2, first)
        _acc(dkk_ref, _colsum8(dkk_), first)
        _acc(dka_ref, _colsum8(dka), first)

    rev = lambda i: (nt - 1 - i, 0)
    row = pl.BlockSpec((tm, RW), rev)
    part = lambda n: pl.BlockSpec((8, n), lambda i: (0, 0))
    mat = pl.BlockSpec((128, RW), lambda i: (0, 0))
    return pl.pallas_call(
        body, name="rwkv_prep_bwd", grid=(nt,),
        in_specs=[pl.BlockSpec((tm, SHIFT_COLS), rev),
                  pl.BlockSpec((8, SHIFT_COLS), lambda i: (jnp.maximum((nt - 1 - i) * (tm // 8) - 1, 0), 0))]
                 + _prep_specs(tm) + [row] * 10,
        out_specs=[pl.BlockSpec((tm, SHIFT_COLS), rev), part(SHIFT_COLS), part(RW), mat, part(RW), mat, mat,
                   part(RW), part(RW)],
        out_shape=[jax.ShapeDtypeStruct((t, SHIFT_COLS), F32), jax.ShapeDtypeStruct((8, SHIFT_COLS), F32),
                   jax.ShapeDtypeStruct((8, RW), F32), jax.ShapeDtypeStruct((128, RW), F32),
                   jax.ShapeDtypeStruct((8, RW), F32), jax.ShapeDtypeStruct((128, RW), F32),
                   jax.ShapeDtypeStruct((128, RW), F32), jax.ShapeDtypeStruct((8, RW), F32),
                   jax.ShapeDtypeStruct((8, RW), F32)],
        scratch_shapes=[pltpu.VMEM((8, SHIFT_COLS), F32)],
        compiler_params=_params(("arbitrary",)),
    )(proj, proj, *pw, *douts)


def _combine_bwd(dyb, o, l, og):
    t = dyb.shape[0]
    tm = _COMB_TM

    def body(d_ref, o_ref, l_ref, og_ref, do_ref, dl_ref, dog_ref):
        ones = jnp.ones((tm, 1), F32)
        dog = []
        for p in range(N_PAIR):
            cols = slice(128 * p, 128 * (p + 1))
            _, vjp = jax.vjp(_combine_fn, o_ref[0, p], o_ref[1, p], o_ref[2, p], l_ref[0, p], l_ref[1, p], l_ref[2, p],
                             ones * og_ref[:, cols])
            res = vjp(d_ref[:, cols])
            for b in range(3):
                do_ref[b, p] = res[b]
                dl_ref[b, p] = res[3 + b]
            dog.append(_colsum8(res[6]))
        _acc(dog_ref, jnp.concatenate(dog, axis=1), pl.program_id(0) == 0)

    blk = pl.BlockSpec((3, N_PAIR, tm, 128), lambda i: (0, 0, i, 0))
    return pl.pallas_call(
        body, name="attn_combine_bwd", grid=(t // tm,),
        in_specs=[pl.BlockSpec((tm, RW), lambda i: (i, 0)), blk, blk, pl.BlockSpec((1, RW), lambda i: (0, 0))],
        out_specs=[blk, blk, pl.BlockSpec((8, RW), lambda i: (0, 0))],
        out_shape=[jax.ShapeDtypeStruct((3, N_PAIR, t, 128), F32)] * 2 + [jax.ShapeDtypeStruct((8, RW), F32)],
        compiler_params=_params(("arbitrary",)),
    )(dyb, o, l, og)


def _attn_bwd(do, dl, qkv):
    t = qkv.shape[2]

    def body(do_ref, dl_ref, q_ref, k_ref, v_ref, dq_ref, dk_ref, dv_ref):
        @pl.when(pl.program_id(1) == 0)
        def _():
            for ref in (dq_ref, dk_ref, dv_ref):
                ref[...] = jnp.zeros_like(ref)

        def unit(di, places, has_prev):
            cur = [_dilated_rows(d, r, n) for d, r, n in places]
            args = [_take(ref, (0,), cur) for ref in (q_ref, k_ref, v_ref)]
            if has_prev:
                prv = [_dilated_rows(d, r, n - 1) for d, r, n in places]
                args += [_take(ref, (0,), prv) for ref in (k_ref, v_ref)]
            _, vjp = jax.vjp(_attn_block_fn, *args)
            res = vjp((_take(do_ref, (0,), cur), _take(dl_ref, (0,), cur)))
            _put(dq_ref, (), cur, res[0], add=True)
            _put(dk_ref, (), cur, res[1], add=True)
            _put(dv_ref, (), cur, res[2], add=True)
            if has_prev:
                _put(dk_ref, (), prv, res[3], add=True)
                _put(dv_ref, (), prv, res[4], add=True)

        _for_each_sequence(t, unit)

    spec = lambda j: pl.BlockSpec((1, ATTN_GROUP, t, 128), lambda i, b: (j, i, 0, 0))
    branch = pl.BlockSpec((1, ATTN_GROUP, t, 128), lambda i, b: (b, i, 0, 0))
    out = pl.BlockSpec((ATTN_GROUP, t, 128), lambda i, b: (i, 0, 0))
    return pl.pallas_call(
        body, name="attn_bwd", grid=(N_PAIR // ATTN_GROUP, len(DILATIONS)),
        in_specs=[branch, branch, spec(0), spec(1), spec(2)], out_specs=[out] * 3,
        out_shape=[jax.ShapeDtypeStruct((N_PAIR, t, 128), F32)] * 3,
        compiler_params=_params(("parallel", "arbitrary")),
    )(do, dl, qkv, qkv, qkv)


def _in_proj_bwd(dpa, dq, dk, dv, win, x, g1, dx1):
    t = x.shape[0]
    tm = 256

    def body(dpa_ref, dq_ref, dk_ref, dv_ref, w_ref, x_ref, g_ref, dx1_ref, dproj_ref, dx_ref, dg_ref):
        parts = [dpa_ref[...]] + [ref[p] for ref in (dq_ref, dk_ref, dv_ref) for p in range(N_PAIR)]
        dproj = jnp.concatenate([z.astype(BF16) for z in parts], axis=1)
        dproj_ref[...] = dproj
        dh = _dot(dproj, w_ref[...])
        dxn, dgr = _rms_bwd(dh, x_ref[...], g_ref[...])
        dx_ref[...] = dx1_ref[...] + dxn
        _acc(dg_ref, _colsum8(dgr), pl.program_id(0) == 0)

    row = pl.BlockSpec((tm, D_MODEL), lambda i: (i, 0))
    pair = pl.BlockSpec((N_PAIR, tm, 128), lambda i: (0, i, 0))
    return pl.pallas_call(
        body, name="in_proj_bwd", grid=(t // tm,),
        in_specs=[pl.BlockSpec((tm, SHIFT_COLS), lambda i: (i, 0))] + [pair] * 3
                 + [pl.BlockSpec((IN_COLS, D_MODEL), lambda i: (0, 0)), row, pl.BlockSpec((1, D_MODEL), lambda i: (0, 0)), row],
        out_specs=[pl.BlockSpec((tm, IN_COLS), lambda i: (i, 0)), row, pl.BlockSpec((8, D_MODEL), lambda i: (0, 0))],
        out_shape=[jax.ShapeDtypeStruct((t, IN_COLS), BF16), jax.ShapeDtypeStruct((t, D_MODEL), F32),
                   jax.ShapeDtypeStruct((8, D_MODEL), F32)],
        compiler_params=_params(("arbitrary",)),
    )(dpa, dq, dk, dv, win, x, g1, dx1)


def _pad_lora(w, lo):
    z = jnp.zeros((64, RW), F32)
    return jnp.concatenate([w, z], axis=0) if lo == 0 else jnp.concatenate([z, w], axis=0)


def _local_step(x, tgt, win, vecs, w2, a2, g2m, get_rest, send_rest):
    pw = (vecs["mu_shift"], vecs["decay_w0"], _pad_lora(w2, 0), vecs["iclr_a0"], _pad_lora(a2, 64), g2m,
          vecs["k_k"], vecs["k_a"])
    h, proj, qkv = _in_proj(x, vecs["mix_norm_g"], win)
    r, lw, k2, v, kk, a, g = _prep_fwd(proj, pw)
    y, s0s = _wkv_fwd(r, lw, k2, v, kk, a)
    ya = _post_fwd(y, r, k2, v, g, vecs["ln_x_w"], vecs["ln_x_b"], vecs["r_k"])
    o_att, l_att = _attn_fwd(qkv)
    yb = _combine_fwd(o_att, l_att, vecs["attn_out_g"])

    wout, wg, wu, wd = get_rest(yb)
    ycat = jnp.concatenate([ya, yb], axis=1)
    x1, h2 = _out_proj(x, ycat, wout, vecs["ffn_norm_g"])
    gt, up, act = _ffn_up(h2, wg, wu)
    dx2, dx2b, loss8, dgf = _ffn_down_loss(x1, act, wd, vecs["final_norm_g"], tgt)

    dgt, dup = _ffn_bwd_act(dx2b, wd, gt, up)
    dx1, dx1b, dya, dyb, dg2n = _ffn_bwd_h(dgt, dup, wg, wu, dx2, x1, vecs["ffn_norm_g"], wout)
    gw = {
        "w_down": _wgrad(act, dx2b, 1408, 1024, "wgrad_down"),
        "w_gate": _wgrad(dgt, h2, 1408, 1024, "wgrad_gate"),
        "w_up": _wgrad(dup, h2, 1408, 1024, "wgrad_up"),
        "w_out": _wgrad(ycat, dx1b, 1024, 1024, "wgrad_out"),
    }

    lnw = vecs["ln_x_w"] + send_rest(gw)[0, 0]
    dy, dr_p, dk2_p, dv_p, dg, dlnw, dlnb, drk = _post_bwd(dya, y, r, k2, v, g, lnw, vecs["ln_x_b"], vecs["r_k"])
    dr_s, dlw, dk2_s, dv_s, dkk, da = _wkv_bwd(dy, s0s, r, lw, k2, v, kk, a)
    dpa, dmu, dw0, dw2p, da0, da2p, dg2m, dk_k, dk_a = _prep_bwd(
        proj, pw, (dr_p, dr_s, dlw, dk2_p, dk2_s, dv_p, dv_s, dkk, da, dg))

    do_att, dl_att, dog = _combine_bwd(dyb, o_att, l_att, vecs["attn_out_g"])
    dq, dk, dv = _attn_bwd(do_att, dl_att, qkv)
    dproj, dx, dg1 = _in_proj_bwd(dpa, dq, dk, dv, win, x, vecs["mix_norm_g"], dx1)
    gw["w_in"] = _wgrad(dproj, h, 1664, 1024, "wgrad_in")
    gw["decay_w2"] = dw2p[:64]
    gw["iclr_a2"] = da2p[64:]
    gw["gate_g2"] = dg2m
    gv = {"mix_norm_g": dg1, "mu_shift": dmu, "decay_w0": dw0, "iclr_a0": da0, "k_k": dk_k, "k_a": dk_a, "r_k": drk,
          "ln_x_w": dlnw, "ln_x_b": dlnb, "attn_out_g": dog, "ffn_norm_g": dg2n, "final_norm_g": dgf}
    return loss8, dx, gw, gv


N_CHIP = 4
N_DEV = 8
MATS = ("w_in", "w_out", "w_gate", "w_up", "w_down")
LORAS = ("decay_w2", "iclr_a2", "gate_g2")
VECS = (("mix_norm_g", 1024), ("mu_shift", 1792), ("decay_w0", 512), ("iclr_a0", 512), ("k_k", 512), ("k_a", 512),
        ("r_k", 512), ("ln_x_w", 512), ("ln_x_b", 512), ("attn_out_g", 512), ("ffn_norm_g", 1024),
        ("final_norm_g", 1024))
N_VEC = sum(n for _, n in VECS)
N_SMALL = N_VEC + 128
ANY = pl.BlockSpec(memory_space=pl.ANY)


def _flip(v, f):
    return 1 - v if f else v


class _Me:
    def __init__(self, mode):
        x, y, c = lax.axis_index("x"), lax.axis_index("y"), lax.axis_index("c")
        self.core, self.chip, self.dev = c, 2 * x + y, 4 * x + 2 * y + c
        self.sibling = (x, y, 1 - c)
        if mode == "chips":
            self.peers = [(px, py, c) for px, py in ((1 - x, y), (x, 1 - y), (1 - x, 1 - y))]
        else:
            self.peers = [(_flip(x, k & 4), _flip(y, k & 2), _flip(c, k & 1)) for k in range(1, N_DEV)]


def _half(core, rows):
    h = rows // 2
    return pl.ds(pl.multiple_of(core * h, h), h)


def _peer_copy(srcs, dsts, kinds, send_sems, recv_sems, me, j, i, incoming):
    px, py, pc = me.peers[j]
    pchip, pdev = 2 * px + py, 4 * px + 2 * py + pc
    src, dst, kind = srcs[i], dsts[i], kinds[i]
    if kind == "gather":
        rows = _half(me.core, src.shape[0])
        src, dst = src.at[rows], dst.at[pchip if incoming else me.chip, rows]
    elif kind == "scatter":
        src, dst = src.at[pchip, _half(pc, src.shape[1])], dst.at[pdev if incoming else me.dev]
    else:
        dst = dst.at[pdev if incoming else me.dev]
    n = len(srcs)
    return pltpu.make_async_remote_copy(src_ref=src, dst_ref=dst, send_sem=send_sems.at[n * j + i],
                                        recv_sem=recv_sems.at[n * j + i], device_id=(px, py, pc), device_id_type=MESH)


def _mode(kinds):
    return "chips" if kinds[0] == "gather" else "devs"


def _npeer(kinds):
    return N_CHIP - 1 if kinds[0] == "gather" else N_DEV - 1


def _swap_gathered(lands, name):
    n = len(lands)

    def body(*refs):
        dsts, send_sems, recv_sems = refs[n:2 * n], refs[2 * n], refs[2 * n + 1]
        me = _Me("chips")

        def copy(j, i, incoming):
            px, py, _ = me.peers[j]
            rows_out, rows_in = _half(me.core, dsts[i].shape[1]), _half(1 - me.core, dsts[i].shape[1])
            return pltpu.make_async_remote_copy(
                src_ref=dsts[i].at[2 * px + py, rows_out], dst_ref=dsts[i].at[2 * px + py, rows_in if incoming else rows_out],
                send_sem=send_sems.at[n * j + i], recv_sem=recv_sems.at[n * j + i], device_id=me.sibling, device_id_type=MESH)

        sends = [copy(j, i, False) for j in range(3) for i in range(n)]
        for cp in sends:
            cp.start()
        for j in range(3):
            for i in range(n):
                copy(j, i, True).wait_recv()
        for cp in sends:
            cp.wait_send()

    return pl.pallas_call(
        body, name=name, in_specs=[ANY] * n, out_specs=[ANY] * n,
        out_shape=[jax.ShapeDtypeStruct(l.shape, l.dtype) for l in lands],
        input_output_aliases={i: i for i in range(n)},
        scratch_shapes=[pltpu.SemaphoreType.DMA((3 * n,)), pltpu.SemaphoreType.DMA((3 * n,))],
    )(*lands)


def _join_halves(sums, name):
    n = len(sums)

    def body(*refs):
        dsts, send_sems, recv_sems = refs[n:2 * n], refs[2 * n], refs[2 * n + 1]
        me = _Me("chips")

        def copy(i, incoming):
            mine, other = _half(me.core, dsts[i].shape[0]), _half(1 - me.core, dsts[i].shape[0])
            return pltpu.make_async_remote_copy(src_ref=dsts[i].at[mine], dst_ref=dsts[i].at[other if incoming else mine],
                                                send_sem=send_sems.at[i], recv_sem=recv_sems.at[i],
                                                device_id=me.sibling, device_id_type=MESH)

        sends = [copy(i, False) for i in range(n)]
        for cp in sends:
            cp.start()
        for i in range(n):
            copy(i, True).wait_recv()
        for cp in sends:
            cp.wait_send()

    return pl.pallas_call(
        body, name=name, in_specs=[ANY] * n, out_specs=[ANY] * n,
        out_shape=[jax.ShapeDtypeStruct(s.shape, s.dtype) for s in sums],
        input_output_aliases={i: i for i in range(n)},
        scratch_shapes=[pltpu.SemaphoreType.DMA((n,)), pltpu.SemaphoreType.DMA((n,))],
    )(*sums)


HBM = pl.BlockSpec(memory_space=pltpu.HBM)
SEM = pl.BlockSpec(memory_space=pltpu.SEMAPHORE)
EFFECT = pltpu.SideEffectType.DATAFLOW_SIDE_EFFECTING


def _swap_start(arrs, lands, kinds, name):
    n = len(arrs)

    def body(*refs):
        srcs, dsts, send_sems, recv_sems, token = refs[:n], refs[n:2 * n], refs[2 * n], refs[2 * n + 1], refs[-1]
        me = _Me(_mode(kinds))
        for j in range(len(me.peers)):
            for i in range(n):
                _peer_copy(srcs, dsts, kinds, send_sems, recv_sems, me, j, i, False).start()
        token[...] = jnp.zeros_like(token)

    ns = _npeer(kinds) * n
    outs = pl.pallas_call(
        body, name=name,
        out_shape=(pltpu.SemaphoreType.DMA((ns,)), pltpu.SemaphoreType.DMA((ns,)),
                   *[pltpu.HBM(a.shape, a.dtype) for a in arrs], *[pltpu.HBM(l.shape, l.dtype) for l in lands],
                   jax.ShapeDtypeStruct((8, 128), F32)),
        in_specs=[HBM] * (2 * n), out_specs=(SEM, SEM, *[HBM] * (2 * n), pl.BlockSpec(memory_space=pltpu.VMEM)),
        input_output_aliases={k: 2 + k for k in range(2 * n)},
        compiler_params=pltpu.CompilerParams(has_side_effects=EFFECT),
    )(*[pltpu.with_memory_space_constraint(a, pltpu.HBM) for a in arrs],
      *[pltpu.with_memory_space_constraint(l, pltpu.HBM) for l in lands])
    return outs[0], outs[1], outs[2:2 + n], outs[2 + n:2 + 2 * n], outs[-1]


def _swap_wait(send_sems, recv_sems, srcs_thru, lands_thru, after, kinds, name):
    n = len(srcs_thru)

    def body(*refs):
        srcs, dsts, s_sems, r_sems = refs[:n], refs[n:2 * n], refs[2 * n], refs[2 * n + 1]
        me = _Me(_mode(kinds))
        for j in range(len(me.peers)):
            for i in range(n):
                cp = _peer_copy(srcs, dsts, kinds, s_sems, r_sems, me, j, i, True)
                cp.wait_send()
                cp.wait_recv()

    outs = pl.pallas_call(
        body, name=name,
        out_shape=tuple(pltpu.HBM(a.shape, a.dtype) for a in (*srcs_thru, *lands_thru)),
        in_specs=[HBM] * (2 * n) + [SEM, SEM, ANY], out_specs=tuple([HBM] * (2 * n)),
        input_output_aliases={k: k for k in range(2 * n)},
        compiler_params=pltpu.CompilerParams(has_side_effects=EFFECT),
    )(*srcs_thru, *lands_thru, send_sems, recv_sems, after)
    return outs[n:]


def _adamw(w, g, m, v):
    m = ADAM_B1 * m + (1.0 - ADAM_B1) * g
    v = ADAM_B2 * v + (1.0 - ADAM_B2) * (g * g)
    m_hat = m / (1.0 - ADAM_B1 ** ADAM_STEP)
    v_hat = v / (1.0 - ADAM_B2 ** ADAM_STEP)
    delta = -ADAM_LR * (m_hat / (jnp.sqrt(v_hat) + ADAM_EPS) + ADAM_WD * w)
    return delta, m, v


def _reduce8(rbuf, core, tr, name):
    _, h, cols = rbuf.shape

    def body(core_ref, r_ref, g_ref):
        g = r_ref[0].astype(F32)
        for s in range(1, N_DEV):
            g = g + r_ref[s].astype(F32)
        g_ref[...] = g

    return pl.pallas_call(
        body, name=name,
        grid_spec=pltpu.PrefetchScalarGridSpec(
            num_scalar_prefetch=1, grid=(h // tr,),
            in_specs=[pl.BlockSpec((N_DEV, tr, cols), lambda i, core_ref: (0, i, 0))],
            out_specs=pl.BlockSpec((tr, cols), lambda i, core_ref: (core_ref[0] * (h // tr) + i, 0))),
        out_shape=jax.ShapeDtypeStruct((2 * h, cols), F32),
        compiler_params=_params(("parallel",)),
    )(core, rbuf)


def _adamw_call(g, w, m, v, tr, name):
    _, rows, cols = w.shape

    def body(g_in, w_ref, m_ref, v_ref, g_ref, d_ref, nm_ref, nv_ref):
        g = g_in[...]
        g_ref[0] = g
        d_ref[0], nm_ref[0], nv_ref[0] = _adamw(w_ref[0], g, m_ref[0], v_ref[0])

    row = pl.BlockSpec((1, tr, cols), lambda i: (0, i, 0))
    return pl.pallas_call(
        body, name=name, grid=(rows // tr,),
        in_specs=[pl.BlockSpec((tr, cols), lambda i: (i, 0)), row, row, row], out_specs=[row] * 4,
        out_shape=[jax.ShapeDtypeStruct(w.shape, F32)] * 4,
        compiler_params=_params(("parallel",)),
    )(g, w, m, v)


def _rowsum_small(parts, loss8):
    def body(*refs):
        out = refs[-1]
        c0 = 0
        for ref in refs[:-1]:
            n = ref.shape[1]
            out[:, c0:c0 + n] = jnp.sum(ref[...], axis=0, keepdims=True)
            c0 += n

    return pl.pallas_call(body, name="rowsum_small", out_shape=jax.ShapeDtypeStruct((1, N_SMALL), F32))(*parts, loss8)


def _reduce_adamw_small(sbuf, ws, ms, vs):
    nv = len(ws)

    def body(*refs):
        s_ref, ins, outs = refs[0], refs[1:1 + 3 * nv], refs[1 + 3 * nv:]
        tot = s_ref[0]
        for s in range(1, N_DEV):
            tot = tot + s_ref[s]
        c0 = 0
        for i in range(nv):
            n = ins[i].shape[1]
            g = tot[:, c0:c0 + n]
            outs[i][...] = g
            outs[nv + i][...], outs[2 * nv + i][...], outs[3 * nv + i][...] = _adamw(
                ins[i][...], g, ins[nv + i][...], ins[2 * nv + i][...])
            c0 += n
        outs[-1][...] = tot[:, c0:]

    return pl.pallas_call(
        body, name="reduce_adamw_small",
        out_shape=[jax.ShapeDtypeStruct(a.shape, F32) for a in ws] * 4 + [jax.ShapeDtypeStruct((1, 128), F32)],
    )(sbuf, *ws, *ms, *vs)


_TRANSPOSED = ("w_in", "w_gate", "w_up")
_ROW_STACKED = MATS
_ADAM_TILE = {"w_in": 208, "w_out": 256, "w_gate": 176, "w_up": 176, "w_down": 176, "decay_w2": 64, "iclr_a2": 64,
              "gate_g2": 128}
_SUM_TILE = {"w_in": 208, "w_out": 128, "w_gate": 176, "w_up": 176, "w_down": 176, "decay_w2": 32, "iclr_a2": 32,
             "gate_g2": 64}


def _full(n, stacked):
    p, r, c = stacked.shape
    if n in _ROW_STACKED:
        return stacked.reshape(p * r, c)
    return jnp.transpose(stacked, (1, 0, 2)).reshape(r, p * c)


def _by_chip(n, full):
    if n in _ROW_STACKED:
        return full.reshape(N_CHIP, full.shape[0] // N_CHIP, full.shape[1])
    r, c = full.shape
    return jnp.transpose(full.reshape(r, N_CHIP, c // N_CHIP), (1, 0, 2))


def _with_own(land_shape, dtype, own, slot):
    return lax.dynamic_update_slice(lax.empty(land_shape, dtype), own[None], (slot,) + (0,) * own.ndim)


def kernel(x, mix_norm_g, w_in, mu_shift, decay_w0, decay_w2, iclr_a0, iclr_a2, gate_g2, k_k, k_a, r_k, ln_x_w, ln_x_b, attn_out_g, w_out, ffn_norm_g, w_gate, w_up, w_down, final_norm_g, loss_target, m_mix_norm_g, m_w_in, m_mu_shift, m_decay_w0, m_decay_w2, m_iclr_a0, m_iclr_a2, m_gate_g2, m_k_k, m_k_a, m_r_k, m_ln_x_w, m_ln_x_b, m_attn_out_g, m_w_out, m_ffn_norm_g, m_w_gate, m_w_up, m_w_down, m_final_norm_g, v_mix_norm_g, v_w_in, v_mu_shift, v_decay_w0, v_decay_w2, v_iclr_a0, v_iclr_a2, v_gate_g2, v_k_k, v_k_a, v_r_k, v_ln_x_w, v_ln_x_b, v_attn_out_g, v_w_out, v_ffn_norm_g, v_w_gate, v_w_up, v_w_down, v_final_norm_g):
    names = ("mix_norm_g", "w_in", "mu_shift", "decay_w0", "decay_w2", "iclr_a0", "iclr_a2", "gate_g2", "k_k", "k_a",
             "r_k", "ln_x_w", "ln_x_b", "attn_out_g", "w_out", "ffn_norm_g", "w_gate", "w_up", "w_down", "final_norm_g")
    w = dict(zip(names, (mix_norm_g, w_in, mu_shift, decay_w0, decay_w2, iclr_a0, iclr_a2, gate_g2, k_k, k_a, r_k,
                         ln_x_w, ln_x_b, attn_out_g, w_out, ffn_norm_g, w_gate, w_up, w_down, final_norm_g)))
    m = dict(zip(names, (m_mix_norm_g, m_w_in, m_mu_shift, m_decay_w0, m_decay_w2, m_iclr_a0, m_iclr_a2, m_gate_g2,
                         m_k_k, m_k_a, m_r_k, m_ln_x_w, m_ln_x_b, m_attn_out_g, m_w_out, m_ffn_norm_g, m_w_gate,
                         m_w_up, m_w_down, m_final_norm_g)))
    v = dict(zip(names, (v_mix_norm_g, v_w_in, v_mu_shift, v_decay_w0, v_decay_w2, v_iclr_a0, v_iclr_a2, v_gate_g2,
                         v_k_k, v_k_a, v_r_k, v_ln_x_w, v_ln_x_b, v_attn_out_g, v_w_out, v_ffn_norm_g, v_w_gate,
                         v_w_up, v_w_down, v_final_norm_g)))
    first = ("w_in",) + LORAS
    rest = ("w_out", "w_gate", "w_up", "w_down")
    xi, yi, ci = lax.axis_index("x"), lax.axis_index("y"), lax.axis_index("c")
    my_chip, my_dev = 2 * xi + yi, 4 * xi + 2 * yi + ci
    gather, scatter = ("gather",) * 4, ("scatter",) * 4

    sh = lambda z, n: jnp.transpose(z[0]) if n in _TRANSPOSED else z[0]
    mine = [sh(w["w_in"], "w_in").astype(BF16)] + [w[n][0] for n in LORAS]
    early = _swap_start(mine, [_with_own((N_CHIP,) + a.shape, a.dtype, a, my_chip) for a in mine], gather, "gather_first_start")
    wb = {n: (sh(w[n], n) + early[4][0, 0]).astype(BF16) for n in rest}
    lands = [_with_own((N_CHIP,) + wb[n].shape, BF16, wb[n], my_chip) for n in rest]
    ssem, rsem, srcs_thru, lands_thru, tok = _swap_start([wb[n] for n in rest], lands, gather, "gather_rest_start")
    got = _swap_wait(early[0], early[1], early[2], early[3], tok, gather, "gather_first_wait")
    win, w2, a2, g2m = (_full(n, z) for n, z in zip(first, _swap_gathered(got, "gather_first_halves")))

    vecs = {n: w[n].reshape(1, sz) for n, sz in VECS}
    vecs["mix_norm_g"] = vecs["mix_norm_g"] + tok[0, 0]

    def get_rest(after):
        halves = _swap_wait(ssem, rsem, srcs_thru, lands_thru, after, gather, "gather_rest_wait")
        return [_full(n, z) for n, z in zip(rest, _swap_gathered(halves, "gather_rest_halves"))]

    flight = []

    def my_half(g):
        h = g.shape[1] // 2
        return lax.dynamic_slice(g, (my_chip, ci * h, 0), (1, h, g.shape[2]))[0]

    def send_rest(gw):
        gs = [_by_chip(n, gw[n]).astype(BF16) for n in rest]
        into = [_with_own((N_DEV,) + my_half(g).shape, BF16, my_half(g), my_dev) for g in gs]
        flight.extend(_swap_start(gs, into, scatter, "exchange_rest_start"))
        return flight[4]

    loss8, dx, gw, gv = _local_step(x[0], loss_target[0], win, vecs, w2, a2, g2m, get_rest, send_rest)

    small = _rowsum_small([gv[n] for n, _ in VECS], loss8)
    gs = [_by_chip(n, gw[n]).astype(BF16) for n in first]
    into = [_with_own((N_DEV,) + my_half(g).shape, BF16, my_half(g), my_dev) for g in gs]
    into.append(_with_own((N_DEV,) + small.shape, F32, small, my_dev))
    last = _swap_start(gs + [small], into, scatter + ("all",), "exchange_first_start")

    core = jnp.reshape(ci, (1,)).astype(jnp.int32)

    def update(group, rbufs, tag):
        sums = [_reduce8(rb, core, _SUM_TILE[n], "reduce_" + n) for n, rb in zip(group, rbufs)]
        gsum = _join_halves(sums, "join_halves_" + tag)
        out = {}
        for n, g in zip(group, gsum):
            r = _adamw_call(g, sh(w[n], n)[None], sh(m[n], n)[None], sh(v[n], n)[None], _ADAM_TILE[n], "adamw_" + n)
            out[n] = [jnp.transpose(z[0])[None] for z in r] if n in _TRANSPOSED else r
        return out

    res = update(rest, _swap_wait(flight[0], flight[1], flight[2], flight[3], last[4], scatter, "exchange_rest_wait"), "rest")
    got = _swap_wait(last[0], last[1], last[2], last[3], res["w_down"][1], scatter + ("all",), "exchange_first_wait")
    res.update(update(first, got[:4], "first"))
    rows = lambda d: [d[n].reshape(1, sz) for n, sz in VECS]
    small_res = _reduce_adamw_small(got[4], rows(w), rows(m), rows(v))

    outs = []
    for k in range(4):
        piece = {n: r[k] for n, r in res.items()}
        for i, (n, _) in enumerate(VECS):
            piece[n] = small_res[k * len(VECS) + i].reshape(w[n].shape)
        outs.extend(piece[n] for n in names)
    return (small_res[-1][0, 0], dx[None], *outs)
```

```python
import jax
import jax.numpy as jnp
from jax import lax
from jax.experimental import pallas as pl
from jax.experimental.pallas import tpu as pltpu

F32 = jnp.float32
BF16 = jnp.bfloat16

D_MODEL = 1024
HEAD_DIM = 64
RW = 512
N_PAIR = RW // 128
SHIFT_COLS = 1792
IN_COLS = 3328
D_FF = 2816
NORM_EPS = 1e-6
GN_EPS = 64e-5
CHUNK = 64
SUB = 16
WKV_PASSES = 1
ATTN_PASSES = 1
ATTN_BLOCK = 128
DILATIONS = (1, 4, 16)
NEG = -1e30
ADAM_LR, ADAM_B1, ADAM_B2, ADAM_EPS, ADAM_WD, ADAM_STEP = 0.001, 0.9, 0.999, 1e-08, 0.01, 10
VMEM_LIMIT = 56 * 1024 * 1024
MESH = pl.DeviceIdType.MESH


def _params(sem=None, **kw):
    return pltpu.CompilerParams(dimension_semantics=sem, vmem_limit_bytes=VMEM_LIMIT, **kw)


def _dot(a, b, prec=None):
    return lax.dot_general(a, b, (((1,), (0,)), ((), ())), preferred_element_type=F32, precision=prec)


def _dot_nt(a, b, prec=None):
    return lax.dot_general(a, b, (((1,), (1,)), ((), ())), preferred_element_type=F32, precision=prec)


def _dot_tn(a, b, prec=None):
    return lax.dot_general(a, b, (((0,), (0,)), ((), ())), preferred_element_type=F32, precision=prec)


_FORMS = {"nn": ((1,), (0,)), "nt": ((1,), (1,)), "tn": ((0,), (0,))}


def _dg(a, b, form):
    if a.ndim == 3 or b.ndim == 3:
        nb = a.shape[0] if a.ndim == 3 else b.shape[0]
        return jnp.stack([_dg(a[i] if a.ndim == 3 else a, b[i] if b.ndim == 3 else b, form) for i in range(nb)], axis=0)
    return lax.dot_general(a, b, (_FORMS[form], ((), ())), preferred_element_type=F32)


def _split2(x):
    hi = x.astype(BF16)
    return hi, (x - hi.astype(F32)).astype(BF16)


def _split3(x):
    hi = x.astype(BF16)
    rest = x - hi.astype(F32)
    mid = rest.astype(BF16)
    return hi, mid, (rest - mid.astype(F32)).astype(BF16)


def _mm_raw(a, b, form, mode):
    if mode == 1:
        return _dg(a.astype(BF16), b.astype(BF16), form)
    if mode == 3:
        ah, al = _split2(a)
        bh, bl = _split2(b)
        return _dg(ah, bh, form) + (_dg(ah, bl, form) + _dg(al, bh, form))
    if mode == "L3":
        ab = a.astype(BF16)
        b1, b2, b3 = _split3(b)
        if form == "nn":
            n = b.shape[-1]
            wide = _dg(ab, jnp.concatenate([b1, b2, b3], axis=-1), form)
            return wide[..., :n] + (wide[..., n:2 * n] + wide[..., 2 * n:])
        return _dg(ab, b1, form) + (_dg(ab, b2, form) + _dg(ab, b3, form))
    assert mode == "R3", mode
    bb = b.astype(BF16)
    a1, a2, a3 = _split3(a)
    if form in ("nn", "nt"):
        m = a.shape[-2]
        tall = _dg(jnp.concatenate([a1, a2, a3], axis=-2), bb, form)
        return tall[..., :m, :] + (tall[..., m:2 * m, :] + tall[..., 2 * m:, :])
    return _dg(a1, bb, form) + (_dg(a2, bb, form) + _dg(a3, bb, form))


def _mm(a, b, form, mode):
    @jax.custom_vjp
    def f(a, b):
        return _mm_raw(a, b, form, mode)

    def fwd(a, b):
        return _mm_raw(a, b, form, mode), (a, b)

    def bwd(res, ct):
        a, b = res
        la = {1: 1, 3: 3, "L3": None, "R3": "R3"}[mode]
        lb = {1: 1, 3: 3, "L3": "L3", "R3": None}[mode]
        if form == "nn":
            da = None if la is None else _mm_raw(ct, b, "nt", la)
            db = None if lb is None else _mm_raw(a, ct, "tn", lb)
        elif form == "nt":
            da = None if la is None else _mm_raw(ct, b, "nn", la)
            db = None if lb is None else _mm_raw(ct, a, "tn", "R3" if lb == "L3" else lb)
        else:
            da = None if la is None else _mm_raw(b, ct, "nt", "L3" if la == "R3" else la)
            db = None if lb is None else _mm_raw(a, ct, "nn", lb)
        return (jnp.zeros_like(a) if da is None else da, jnp.zeros_like(b) if db is None else db)

    f.defvjp(fwd, bwd)
    return f(a, b)


def _seg_ones(n):
    r = lax.broadcasted_iota(jnp.int32, (n, n), 0) // HEAD_DIM
    c = lax.broadcasted_iota(jnp.int32, (n, n), 1) // HEAD_DIM
    return (r == c).astype(F32)


def _segsum(x, seg):
    return _mm(x, seg, "nn", "R3")


def _rms_fwd(x, g):
    rstd = lax.rsqrt(jnp.mean(x * x, axis=-1, keepdims=True) + NORM_EPS)
    return x * rstd * g


def _rms_bwd(dy, x, g):
    rstd = lax.rsqrt(jnp.mean(x * x, axis=-1, keepdims=True) + NORM_EPS)
    xn = x * rstd
    dxn = dy * g
    dx = rstd * (dxn - xn * jnp.mean(dxn * xn, axis=-1, keepdims=True))
    return dx, dy * xn


def _sigmoid(x):
    return 1.0 / (1.0 + jnp.exp(-x))


def _softplus(x):
    return jnp.maximum(x, 0.0) + jnp.log(1.0 + jnp.exp(-jnp.abs(x)))


def _acc(ref, val, first):
    @pl.when(first)
    def _():
        ref[...] = val

    @pl.when(jnp.logical_not(first))
    def _():
        ref[...] += val


def _colsum8(v):
    rows, n = v.shape
    return jnp.sum(v.reshape(rows // 8, 8, n), axis=0)


def _prep_fn(p, pprev, mu, w0, w2p, a0, a2p, g2, k_k, k_a):
    seg = _seg_ones(RW)
    ps = p + (pprev - p) * mu
    r = ps[:, 0:RW]
    k = ps[:, RW:2 * RW]
    v = ps[:, 2 * RW:3 * RW]
    xwa = ps[:, 3 * RW:3 * RW + 128]
    xg = ps[:, 3 * RW + 128:3 * RW + 256]
    wraw = -_softplus(-(w0 + _mm(jnp.tanh(xwa), w2p, "nn", 3))) - 0.5
    lw = -jnp.exp(wraw)
    a = _sigmoid(a0 + _mm(xwa, a2p, "nn", 3))
    g = _mm(_sigmoid(xg), g2, "nn", 3)
    kk = k * k_k
    kk = kk / jnp.maximum(jnp.sqrt(_segsum(kk * kk, seg)), 1e-12)
    k2 = k * (1.0 + (a - 1.0) * k_a)
    return r, lw, k2, v, kk, a, g


def _transposed(z):
    return jnp.stack([z[i].T for i in range(z.shape[0])], axis=0) if z.ndim == 3 else z.T


def _solve_unit_lower(lmat, rhs):
    c = lmat.shape[-1]
    row = lax.broadcasted_iota(jnp.int32, (c, c), 0)
    col = lax.broadcasted_iota(jnp.int32, (c, c), 1)
    eye = (row == col).astype(F32)
    ld = jnp.where(row // SUB == col // SUB, lmat, 0.0)
    lo = lmat - ld
    x = eye + ld
    m = ld
    mm = lambda p, q: _mm(p, q, "nn", WKV_PASSES)
    cat = jnp.concatenate
    m = mm(m, m)
    for _ in range(2):
        mx = mm(m, cat([m, x], axis=-1))
        m, x = mx[..., :c], x + mx[..., c:]
    x = x + mm(m, x)
    gw = mm(x, cat([lo, rhs], axis=-1))
    g, w = gw[..., :c], gw[..., c:]
    gg = mm(g, cat([g, w], axis=-1))
    w = w + gg[..., c:]
    return w + mm(gg[..., :c], w)


def _wkv_chunk_fn(s0, r, lw, k, v, kk, a):
    c = r.shape[-2]
    n = 2 * c
    row = lax.broadcasted_iota(jnp.int32, (n, n), 0)
    col = lax.broadcasted_iota(jnp.int32, (n, n), 1)
    same = (row // c) == (col // c)
    incl = jnp.logical_and(row >= col, same)
    strict = jnp.logical_and(row > col, same)
    sel = (lax.broadcasted_iota(jnp.int32, (n, 128), 0) // c) == (lax.broadcasted_iota(jnp.int32, (n, 128), 1) // HEAD_DIM)
    two = lambda z: jnp.concatenate([z, z], axis=-2)
    lw2 = two(lw)
    mm = lambda p_, q_, form: _mm(p_, q_, form, WKV_PASSES)
    cl = _mm(incl.astype(F32), lw2, "nn", "L3")
    p = jnp.exp(cl)
    pinv = jnp.exp(-cl)
    pprev = jnp.exp(cl - lw2)
    kk2 = two(kk)
    at = jnp.where(sel, -kk2 * pprev, 0.0)
    bt = jnp.where(sel, kk2 * two(a) * pinv, 0.0)
    kt = jnp.where(sel, two(k) * pinv, 0.0)
    rt = jnp.where(sel, two(r) * p, 0.0)
    vt = jnp.where(sel, two(v), 0.0)
    cat = jnp.concatenate
    bk = cat([bt, kt], axis=-2)
    arbk = mm(cat([at, rt], axis=-2), bk, "nt")
    ab, ak = jnp.where(strict, arbk[..., :n, :n], 0.0), jnp.where(strict, arbk[..., :n, n:], 0.0)
    rb, rk = jnp.where(incl, arbk[..., n:, :n], 0.0), jnp.where(incl, arbk[..., n:, n:], 0.0)
    s0t = _transposed(s0)
    u = _solve_unit_lower(ab, mm(cat([at, ak], axis=-1), cat([s0t, vt], axis=-2), "nn"))
    y2 = mm(cat([rt, rb, rk], axis=-1), cat([s0t, u, vt], axis=-2), "nn")
    plast = jnp.exp(jnp.sum(lw, axis=-2, keepdims=True))
    s1 = (s0 + mm(cat([u, vt], axis=-2), bk, "tn")) * plast
    r2 = lax.broadcasted_iota(jnp.int32, (128, 128), 0) // HEAD_DIM
    c2 = lax.broadcasted_iota(jnp.int32, (128, 128), 1) // HEAD_DIM
    return y2[..., :c, :] + y2[..., c:, :], jnp.where(r2 == c2, s1, 0.0)


def _post_fn(y, r, k2, v, g, lnw, lnb, rk):
    seg = _seg_ones(RW)
    mean = _segsum(y, seg) * (1.0 / HEAD_DIM)
    yc = y - mean
    var = _segsum(yc * yc, seg) * (1.0 / HEAD_DIM)
    yn = yc * lax.rsqrt(var + GN_EPS)
    out = yn * lnw + lnb + _segsum(r * k2 * rk, seg) * v
    return out * g


def _attn_block_fn(q, kc, vc, kp=None, vp=None):
    n = ATTN_BLOCK
    qi = lax.broadcasted_iota(jnp.int32, (n, n), 0)
    kj = lax.broadcasted_iota(jnp.int32, (n, n), 1)
    lane = lax.broadcasted_iota(jnp.int32, (1, 128), 1)
    scale = HEAD_DIM ** -0.5
    valid = kj <= qi
    keys, vals = kc, vc
    if kp is not None:
        valid = jnp.concatenate([valid, kj >= qi], axis=-1)
        keys, vals = jnp.concatenate([kc, kp], axis=-2), jnp.concatenate([vc, vp], axis=-2)
    m0 = (lane // HEAD_DIM) == 0
    q2 = jnp.concatenate([jnp.where(m0, q, 0.0), jnp.where(m0, 0.0, q)], axis=-2)
    valid2 = jnp.concatenate([valid, valid], axis=-2)
    s = jnp.where(valid2, _mm(q2, keys, "nt", ATTN_PASSES) * scale, NEG)
    m = jnp.max(s, axis=-1, keepdims=True)
    p = jnp.exp(s - m)
    den = jnp.sum(p, axis=-1, keepdims=True)
    o2 = _mm(p, vals, "nn", ATTN_PASSES) / den
    l2 = m + jnp.log(den)
    return jnp.where(m0, o2[..., :n, :], o2[..., n:, :]), jnp.where(m0, l2[..., :n, :], l2[..., n:, :])


def _combine_fn(o1, o2, o3, l1, l2, l3, og):
    seg = _seg_ones(o1.shape[-1])
    m = jnp.maximum(jnp.maximum(l1, l2), l3)
    e1, e2, e3 = jnp.exp(l1 - m), jnp.exp(l2 - m), jnp.exp(l3 - m)
    o = (e1 * o1 + e2 * o2 + e3 * o3) / (e1 + e2 + e3)
    o = o * lax.rsqrt(_segsum(o * o, seg) * (1.0 / HEAD_DIM) + NORM_EPS)
    return o * og


def _in_proj(x, g1, win):
    t = x.shape[0]
    tm = 512

    def body(x_ref, g_ref, w_ref, h_ref, pa_ref, qkv_ref):
        h = _rms_fwd(x_ref[...], g_ref[...]).astype(BF16)
        h_ref[...] = h
        proj = _dot_nt(h, w_ref[...])
        pa_ref[...] = proj[:, :SHIFT_COLS]
        for j in range(3):
            for p in range(N_PAIR):
                c0 = SHIFT_COLS + j * RW + p * 128
                qkv_ref[j, p] = proj[:, c0:c0 + 128]

    return pl.pallas_call(
        body, name="in_proj", grid=(t // tm,),
        in_specs=[pl.BlockSpec((tm, D_MODEL), lambda i: (i, 0)), pl.BlockSpec((1, D_MODEL), lambda i: (0, 0)),
                  pl.BlockSpec((IN_COLS, D_MODEL), lambda i: (0, 0))],
        out_specs=[pl.BlockSpec((tm, D_MODEL), lambda i: (i, 0)), pl.BlockSpec((tm, SHIFT_COLS), lambda i: (i, 0)),
                   pl.BlockSpec((3, N_PAIR, tm, 128), lambda i: (0, 0, i, 0))],
        out_shape=[jax.ShapeDtypeStruct((t, D_MODEL), BF16), jax.ShapeDtypeStruct((t, SHIFT_COLS), F32),
                   jax.ShapeDtypeStruct((3, N_PAIR, t, 128), F32)],
        compiler_params=_params(("parallel",)),
    )(x, g1, win)


def _shifted(p, last8, first):
    prow = jnp.where(first, 0.0, last8[7:8, :])
    rolled = pltpu.roll(p, 1, axis=0)
    rid = lax.broadcasted_iota(jnp.int32, p.shape, 0)
    return jnp.where(rid == 0, prow, rolled)


_PREP_TM = 256


def _prep_specs(tm):
    vec = lambda n: pl.BlockSpec((1, n), lambda i: (0, 0))
    mat = lambda r, n: pl.BlockSpec((r, n), lambda i: (0, 0))
    return [vec(SHIFT_COLS), vec(RW), mat(128, RW), vec(RW), mat(128, RW), mat(128, RW), vec(RW), vec(RW)]


def _prep_fwd(proj, pw):
    t = proj.shape[0]
    tm = _PREP_TM

    def body(p_ref, l8_ref, mu, w0, w2p, a0, a2p, g2, k_k, k_a, *outs):
        p = p_ref[...]
        pprev = _shifted(p, l8_ref[...], pl.program_id(0) == 0)
        res = _prep_fn(p, pprev, mu[...], w0[...], w2p[...], a0[...], a2p[...], g2[...], k_k[...], k_a[...])
        for o_ref, val in zip(outs, res):
            o_ref[...] = val

    row = pl.BlockSpec((tm, RW), lambda i: (i, 0))
    return pl.pallas_call(
        body, name="rwkv_prep", grid=(t // tm,),
        in_specs=[pl.BlockSpec((tm, SHIFT_COLS), lambda i: (i, 0)),
                  pl.BlockSpec((8, SHIFT_COLS), lambda i: (jnp.maximum(i * (tm // 8) - 1, 0), 0))] + _prep_specs(tm),
        out_specs=[row] * 7,
        out_shape=[jax.ShapeDtypeStruct((t, RW), F32)] * 7,
        compiler_params=_params(("parallel",)),
    )(proj, proj, *pw)


def _pairs(ref):
    return jnp.stack([ref[:, 128 * p:128 * (p + 1)] for p in range(N_PAIR)], axis=0)


def _wkv_fwd(r, lw, k2, v, kk, a):
    t = r.shape[0]
    nc = t // CHUNK

    def body(r_ref, lw_ref, k_ref, v_ref, kk_ref, a_ref, y_ref, s_ref, st):
        @pl.when(pl.program_id(0) == 0)
        def _():
            st[...] = jnp.zeros_like(st)

        s0 = st[...]
        s_ref[0] = s0
        y, s1 = _wkv_chunk_fn(s0, *[_pairs(ref) for ref in (r_ref, lw_ref, k_ref, v_ref, kk_ref, a_ref)])
        for p in range(N_PAIR):
            y_ref[:, 128 * p:128 * (p + 1)] = y[p]
        st[...] = s1

    blk = pl.BlockSpec((CHUNK, RW), lambda c: (c, 0))
    return pl.pallas_call(
        body, name="wkv_fwd", grid=(nc,),
        in_specs=[blk] * 6,
        out_specs=[blk, pl.BlockSpec((1, N_PAIR, 128, 128), lambda c: (c, 0, 0, 0))],
        out_shape=[jax.ShapeDtypeStruct((t, RW), F32), jax.ShapeDtypeStruct((nc, N_PAIR, 128, 128), F32)],
        scratch_shapes=[pltpu.VMEM((N_PAIR, 128, 128), F32)],
        compiler_params=_params(("arbitrary",)),
    )(r, lw, k2, v, kk, a)


_POST_TM = 256


def _post_fwd(y, r, k2, v, g, lnw, lnb, rk):
    t = y.shape[0]
    tm = _POST_TM

    def body(y_ref, r_ref, k_ref, v_ref, g_ref, lnw_ref, lnb_ref, rk_ref, o_ref):
        o_ref[...] = _post_fn(y_ref[...], r_ref[...], k_ref[...], v_ref[...], g_ref[...],
                              lnw_ref[...], lnb_ref[...], rk_ref[...]).astype(BF16)

    row = pl.BlockSpec((tm, RW), lambda i: (i, 0))
    vec = pl.BlockSpec((1, RW), lambda i: (0, 0))
    return pl.pallas_call(
        body, name="rwkv_post", grid=(t // tm,),
        in_specs=[row] * 5 + [vec] * 3, out_specs=row,
        out_shape=jax.ShapeDtypeStruct((t, RW), BF16),
        compiler_params=_params(("parallel",)),
    )(y, r, k2, v, g, lnw, lnb, rk)


ATTN_GROUP = 2


def _dilated_rows(d, r, n):
    if d == 1:
        return pl.ds(pl.multiple_of(n * ATTN_BLOCK, ATTN_BLOCK), ATTN_BLOCK)
    return pl.ds(r + n * (ATTN_BLOCK * d), ATTN_BLOCK, stride=d)


def _for_each_sequence(t, unit):
    for di, d in enumerate(DILATIONS):

        @pl.when(pl.program_id(1) == di)
        def _(di=di, d=d):
            nb = t // (ATTN_BLOCK * d)
            if d == 1:
                unit(di, [(d, 0, 0)], False)
                unit(di, [(d, 0, 1)], True)
                lax.fori_loop(1, nb // 2, lambda k, c: (unit(di, [(d, 0, 2 * k), (d, 0, 2 * k + 1)], True), c)[1], 0)
            else:

                def residues(r, carry):
                    unit(di, [(d, r, 0), (d, r + d // 2, 0)], False)
                    if nb > 1:
                        lax.fori_loop(1, nb, lambda n, c: (unit(di, [(d, r, n), (d, r + d // 2, n)], True), c)[1], 0)
                    return carry

                lax.fori_loop(0, d // 2, residues, 0)


def _take(ref, lead, rows_list):
    return jnp.stack([ref.at[(*lead, g)][rows, :] for rows in rows_list for g in range(ATTN_GROUP)], axis=0)


def _put(ref, lead, rows_list, val, add=False):
    k = 0
    for rows in rows_list:
        for g in range(ATTN_GROUP):
            if add:
                ref.at[(*lead, g)][rows, :] += val[k]
            else:
                ref.at[(*lead, g)][rows, :] = val[k]
            k += 1


def _attn_fwd(qkv):
    t = qkv.shape[2]

    def body(q_ref, k_ref, v_ref, o_ref, l_ref):
        def unit(di, places, has_prev):
            cur = [_dilated_rows(d, r, n) for d, r, n in places]
            args = [_take(ref, (0,), cur) for ref in (q_ref, k_ref, v_ref)]
            if has_prev:
                prv = [_dilated_rows(d, r, n - 1) for d, r, n in places]
                args += [_take(ref, (0,), prv) for ref in (k_ref, v_ref)]
            o, lse = _attn_block_fn(*args)
            _put(o_ref, (0,), cur, o)
            _put(l_ref, (0,), cur, lse)

        _for_each_sequence(t, unit)

    spec = lambda j: pl.BlockSpec((1, ATTN_GROUP, t, 128), lambda i, b: (j, i, 0, 0))
    out = pl.BlockSpec((1, ATTN_GROUP, t, 128), lambda i, b: (b, i, 0, 0))
    return pl.pallas_call(
        body, name="attn_fwd", grid=(N_PAIR // ATTN_GROUP, len(DILATIONS)),
        in_specs=[spec(0), spec(1), spec(2)], out_specs=[out, out],
        out_shape=[jax.ShapeDtypeStruct((3, N_PAIR, t, 128), F32)] * 2,
        compiler_params=_params(("parallel", "arbitrary")),
    )(qkv, qkv, qkv)


_COMB_TM = 256


def _combine_fwd(o, l, og):
    t = o.shape[2]
    tm = _COMB_TM

    def body(o_ref, l_ref, og_ref, y_ref):
        for p in range(N_PAIR):
            cols = slice(128 * p, 128 * (p + 1))
            y_ref[:, cols] = _combine_fn(o_ref[0, p], o_ref[1, p], o_ref[2, p], l_ref[0, p], l_ref[1, p], l_ref[2, p],
                                         og_ref[:, cols]).astype(BF16)

    blk = pl.BlockSpec((3, N_PAIR, tm, 128), lambda i: (0, 0, i, 0))
    return pl.pallas_call(
        body, name="attn_combine", grid=(t // tm,),
        in_specs=[blk, blk, pl.BlockSpec((1, RW), lambda i: (0, 0))], out_specs=pl.BlockSpec((tm, RW), lambda i: (i, 0)),
        out_shape=jax.ShapeDtypeStruct((t, RW), BF16),
        compiler_params=_params(("parallel",)),
    )(o, l, og)


def _out_proj(x, ycat, wout, g2):
    t = x.shape[0]
    tm = 256

    def body(x_ref, y_ref, w_ref, g_ref, x1_ref, h_ref):
        x1 = x_ref[...] + _dot(y_ref[...], w_ref[...])
        x1_ref[...] = x1
        h_ref[...] = _rms_fwd(x1, g_ref[...]).astype(BF16)

    row = pl.BlockSpec((tm, D_MODEL), lambda i: (i, 0))
    return pl.pallas_call(
        body, name="out_proj", grid=(t // tm,),
        in_specs=[row, row, pl.BlockSpec((D_MODEL, D_MODEL), lambda i: (0, 0)), pl.BlockSpec((1, D_MODEL), lambda i: (0, 0))],
        out_specs=[row, row],
        out_shape=[jax.ShapeDtypeStruct((t, D_MODEL), F32), jax.ShapeDtypeStruct((t, D_MODEL), BF16)],
        compiler_params=_params(("parallel",)),
    )(x, ycat, wout, g2)


def _ffn_up(h2, wg, wu):
    t = h2.shape[0]
    tm = 512

    def body(h_ref, wg_ref, wu_ref, gt_ref, up_ref, act_ref):
        h = h_ref[...]
        gt = _dot_nt(h, wg_ref[...])
        up = _dot_nt(h, wu_ref[...])
        gt_ref[...] = gt.astype(BF16)
        up_ref[...] = up.astype(BF16)
        act_ref[...] = (gt * _sigmoid(gt) * up).astype(BF16)

    wide = pl.BlockSpec((tm, D_FF), lambda i: (i, 0))
    wsp = pl.BlockSpec((D_FF, D_MODEL), lambda i: (0, 0))
    return pl.pallas_call(
        body, name="ffn_up", grid=(t // tm,),
        in_specs=[pl.BlockSpec((tm, D_MODEL), lambda i: (i, 0)), wsp, wsp],
        out_specs=[wide, wide, wide],
        out_shape=[jax.ShapeDtypeStruct((t, D_FF), BF16)] * 3,
        compiler_params=_params(("parallel",)),
    )(h2, wg, wu)


def _ffn_down_loss(x1, act, wd, gf, tgt):
    t = x1.shape[0]
    tm = 256

    def body(x1_ref, a_ref, w_ref, g_ref, t_ref, dx_ref, dxb_ref, loss_ref, dg_ref):
        first = pl.program_id(0) == 0
        x2 = x1_ref[...] + _dot(a_ref[...], w_ref[...])
        g = g_ref[...]
        diff = _rms_fwd(x2, g) - t_ref[...]
        lrow = 0.5 * jnp.sum(_colsum8(diff * diff), axis=1, keepdims=True) * (1.0 / D_MODEL)
        _acc(loss_ref, jnp.broadcast_to(lrow, (8, 128)), first)
        dx2, dgr = _rms_bwd(diff * (1.0 / D_MODEL), x2, g)
        dx_ref[...] = dx2
        dxb_ref[...] = dx2.astype(BF16)
        _acc(dg_ref, _colsum8(dgr), first)

    row = pl.BlockSpec((tm, D_MODEL), lambda i: (i, 0))
    return pl.pallas_call(
        body, name="ffn_down_loss", grid=(t // tm,),
        in_specs=[row, pl.BlockSpec((tm, D_FF), lambda i: (i, 0)), pl.BlockSpec((D_FF, D_MODEL), lambda i: (0, 0)),
                  pl.BlockSpec((1, D_MODEL), lambda i: (0, 0)), row],
        out_specs=[row, row, pl.BlockSpec((8, 128), lambda i: (0, 0)), pl.BlockSpec((8, D_MODEL), lambda i: (0, 0))],
        out_shape=[jax.ShapeDtypeStruct((t, D_MODEL), F32), jax.ShapeDtypeStruct((t, D_MODEL), BF16),
                   jax.ShapeDtypeStruct((8, 128), F32), jax.ShapeDtypeStruct((8, D_MODEL), F32)],
        compiler_params=_params(("arbitrary",)),
    )(x1, act, wd, gf, tgt)


def _ffn_bwd_act(dx2b, wd, gt, up):
    t = dx2b.shape[0]
    tm = 512

    def body(dx_ref, w_ref, gt_ref, up_ref, dgt_ref, dup_ref):
        dact = _dot_nt(dx_ref[...], w_ref[...])
        gt = gt_ref[...].astype(F32)
        sg = _sigmoid(gt)
        dgt_ref[...] = (dact * up_ref[...].astype(F32) * sg * (1.0 + gt * (1.0 - sg))).astype(BF16)
        dup_ref[...] = (dact * gt * sg).astype(BF16)

    wide = pl.BlockSpec((tm, D_FF), lambda i: (i, 0))
    return pl.pallas_call(
        body, name="ffn_bwd_act", grid=(t // tm,),
        in_specs=[pl.BlockSpec((tm, D_MODEL), lambda i: (i, 0)), pl.BlockSpec((D_FF, D_MODEL), lambda i: (0, 0)), wide, wide],
        out_specs=[wide, wide],
        out_shape=[jax.ShapeDtypeStruct((t, D_FF), BF16)] * 2,
        compiler_params=_params(("parallel",)),
    )(dx2b, wd, gt, up)


def _ffn_bwd_h(dgt, dup, wg, wu, dx2, x1, g2, wout):
    t = dgt.shape[0]
    tm = 256

    def body(dgt_ref, dup_ref, wg_ref, wu_ref, dx2_ref, x1_ref, g_ref, wo_ref, dx1_ref, dx1b_ref, dya_ref, dyb_ref, dg_ref):
        dh = _dot(dgt_ref[...], wg_ref[...]) + _dot(dup_ref[...], wu_ref[...])
        dxn, dgr = _rms_bwd(dh, x1_ref[...], g_ref[...])
        dx1 = dx2_ref[...] + dxn
        dx1_ref[...] = dx1
        dx1b = dx1.astype(BF16)
        dx1b_ref[...] = dx1b
        dy = _dot_nt(dx1b, wo_ref[...])
        dya_ref[...] = dy[:, :RW]
        dyb_ref[...] = dy[:, RW:]
        _acc(dg_ref, _colsum8(dgr), pl.program_id(0) == 0)

    wide = pl.BlockSpec((tm, D_FF), lambda i: (i, 0))
    row = pl.BlockSpec((tm, D_MODEL), lambda i: (i, 0))
    half = pl.BlockSpec((tm, RW), lambda i: (i, 0))
    wsp = pl.BlockSpec((D_FF, D_MODEL), lambda i: (0, 0))
    return pl.pallas_call(
        body, name="ffn_bwd_h", grid=(t // tm,),
        in_specs=[wide, wide, wsp, wsp, row, row, pl.BlockSpec((1, D_MODEL), lambda i: (0, 0)),
                  pl.BlockSpec((D_MODEL, D_MODEL), lambda i: (0, 0))],
        out_specs=[row, row, half, half, pl.BlockSpec((8, D_MODEL), lambda i: (0, 0))],
        out_shape=[jax.ShapeDtypeStruct((t, D_MODEL), F32), jax.ShapeDtypeStruct((t, D_MODEL), BF16),
                   jax.ShapeDtypeStruct((t, RW), F32), jax.ShapeDtypeStruct((t, RW), F32),
                   jax.ShapeDtypeStruct((8, D_MODEL), F32)],
        compiler_params=_params(("arbitrary",)),
    )(dgt, dup, wg, wu, dx2, x1, g2, wout)


def _wgrad(a, b, tk, tn, name):
    t, kdim = a.shape
    ndim = b.shape[1]

    def body(a_ref, b_ref, o_ref):
        o_ref[...] = _dot_tn(a_ref[...], b_ref[...])

    return pl.pallas_call(
        body, name=name, grid=(kdim // tk, ndim // tn),
        in_specs=[pl.BlockSpec((t, tk), lambda i, j: (0, i)), pl.BlockSpec((t, tn), lambda i, j: (0, j))],
        out_specs=pl.BlockSpec((tk, tn), lambda i, j: (i, j)),
        out_shape=jax.ShapeDtypeStruct((kdim, ndim), F32),
        compiler_params=_params(("parallel", "parallel")),
    )(a, b)


def _post_bwd(dya, y, r, k2, v, g, lnw, lnb, rk):
    t = y.shape[0]
    tm = _POST_TM

    def body(d_ref, y_ref, r_ref, k_ref, v_ref, g_ref, lnw_ref, lnb_ref, rk_ref,
             dy_ref, dr_ref, dk_ref, dv_ref, dg_ref, dlnw_ref, dlnb_ref, drk_ref):
        first = pl.program_id(0) == 0
        ones = jnp.ones((tm, 1), F32)
        prim = (y_ref[...], r_ref[...], k_ref[...], v_ref[...], g_ref[...],
                ones * lnw_ref[...], ones * lnb_ref[...], ones * rk_ref[...])
        _, vjp = jax.vjp(_post_fn, *prim)
        dy, dr, dk, dv, dg, dlnw, dlnb, drk = vjp(d_ref[...])
        dy_ref[...] = dy
        dr_ref[...] = dr
        dk_ref[...] = dk
        dv_ref[...] = dv
        dg_ref[...] = dg
        _acc(dlnw_ref, _colsum8(dlnw), first)
        _acc(dlnb_ref, _colsum8(dlnb), first)
        _acc(drk_ref, _colsum8(drk), first)

    row = pl.BlockSpec((tm, RW), lambda i: (i, 0))
    vec = pl.BlockSpec((1, RW), lambda i: (0, 0))
    part = pl.BlockSpec((8, RW), lambda i: (0, 0))
    return pl.pallas_call(
        body, name="rwkv_post_bwd", grid=(t // tm,),
        in_specs=[row] * 6 + [vec] * 3, out_specs=[row] * 5 + [part] * 3,
        out_shape=[jax.ShapeDtypeStruct((t, RW), F32)] * 5 + [jax.ShapeDtypeStruct((8, RW), F32)] * 3,
        compiler_params=_params(("arbitrary",)),
    )(dya, y, r, k2, v, g, lnw, lnb, rk)


def _wkv_bwd(dy, s0s, r, lw, k2, v, kk, a):
    t = r.shape[0]
    nc = t // CHUNK

    def body(dy_ref, s_ref, r_ref, lw_ref, k_ref, v_ref, kk_ref, a_ref,
             dr_ref, dlw_ref, dk_ref, dv_ref, dkk_ref, da_ref, ds):
        @pl.when(pl.program_id(0) == 0)
        def _():
            ds[...] = jnp.zeros_like(ds)

        _, vjp = jax.vjp(_wkv_chunk_fn, s_ref[0],
                         *[_pairs(ref) for ref in (r_ref, lw_ref, k_ref, v_ref, kk_ref, a_ref)])
        res = vjp((_pairs(dy_ref), ds[...]))
        ds[...] = res[0]
        for ref, val in zip((dr_ref, dlw_ref, dk_ref, dv_ref, dkk_ref, da_ref), res[1:]):
            for p in range(N_PAIR):
                ref[:, 128 * p:128 * (p + 1)] = val[p]

    blk = pl.BlockSpec((CHUNK, RW), lambda c: (nc - 1 - c, 0))
    return pl.pallas_call(
        body, name="wkv_bwd", grid=(nc,),
        in_specs=[blk, pl.BlockSpec((1, N_PAIR, 128, 128), lambda c: (nc - 1 - c, 0, 0, 0))] + [blk] * 6,
        out_specs=[blk] * 6,
        out_shape=[jax.ShapeDtypeStruct((t, RW), F32)] * 6,
        scratch_shapes=[pltpu.VMEM((N_PAIR, 128, 128), F32)],
        compiler_params=_params(("arbitrary",)),
    )(dy, s0s, r, lw, k2, v, kk, a)


def _prep_bwd(proj, pw, douts):
    t = proj.shape[0]
    tm = _PREP_TM
    nt = t // tm

    def body(p_ref, l8_ref, mu, w0, w2p, a0, a2p, g2, k_k, k_a, dr, dr2, dlw, dk2, dk22, dv, dv2, dkk, da, dg,
             dp_ref, dmu_ref, dw0_ref, dw2_ref, da0_ref, da2_ref, dg2_ref, dkk_ref, dka_ref, carry):
        i = pl.program_id(0)
        first = i == 0

        @pl.when(first)
        def _():
            carry[...] = jnp.zeros_like(carry)

        p = p_ref[...]
        pprev = _shifted(p, l8_ref[...], i == nt - 1)
        ones = jnp.ones((tm, 1), F32)
        prim = (p, pprev, ones * mu[...], ones * w0[...], w2p[...], ones * a0[...], a2p[...], g2[...],
                ones * k_k[...], ones * k_a[...])
        _, vjp = jax.vjp(_prep_fn, *prim)
        dp, dpp, dmu, dw0, dw2, da0, da2, dg2, dkk_, dka = vjp(
            (dr[...] + dr2[...], dlw[...], dk2[...] + dk22[...], dv[...] + dv2[...], dkk[...], da[...], dg[...]))
        up = pltpu.roll(dpp, tm - 1, axis=0)
        rid = lax.broadcasted_iota(jnp.int32, dpp.shape, 0)
        dp_ref[...] = dp + jnp.where(rid == tm - 1, carry[0:1, :], up)
        carry[...] = jnp.broadcast_to(dpp[0:1, :], carry.shape)
        _acc(dmu_ref, _colsum8(dmu), first)
        _acc(dw0_ref, _colsum8(dw0), first)
        _acc(dw2_ref, dw2, first)
        _acc(da0_ref, _colsum8(da0), first)
        _acc(da2_ref, da2, first)
        _acc(dg2_ref, dg2, first)
        _acc(dkk_ref, _colsum8(dkk_), first)
        _acc(dka_ref, _colsum8(dka), first)

    rev = lambda i: (nt - 1 - i, 0)
    row = pl.BlockSpec((tm, RW), rev)
    part = lambda n: pl.BlockSpec((8, n), lambda i: (0, 0))
    mat = pl.BlockSpec((128, RW), lambda i: (0, 0))
    return pl.pallas_call(
        body, name="rwkv_prep_bwd", grid=(nt,),
        in_specs=[pl.BlockSpec((tm, SHIFT_COLS), rev),
                  pl.BlockSpec((8, SHIFT_COLS), lambda i: (jnp.maximum((nt - 1 - i) * (tm // 8) - 1, 0), 0))]
                 + _prep_specs(tm) + [row] * 10,
        out_specs=[pl.BlockSpec((tm, SHIFT_COLS), rev), part(SHIFT_COLS), part(RW), mat, part(RW), mat, mat,
                   part(RW), part(RW)],
        out_shape=[jax.ShapeDtypeStruct((t, SHIFT_COLS), F32), jax.ShapeDtypeStruct((8, SHIFT_COLS), F32),
                   jax.ShapeDtypeStruct((8, RW), F32), jax.ShapeDtypeStruct((128, RW), F32),
                   jax.ShapeDtypeStruct((8, RW), F32), jax.ShapeDtypeStruct((128, RW), F32),
                   jax.ShapeDtypeStruct((128, RW), F32), jax.ShapeDtypeStruct((8, RW), F32),
                   jax.ShapeDtypeStruct((8, RW), F32)],
        scratch_shapes=[pltpu.VMEM((8, SHIFT_COLS), F32)],
        compiler_params=_params(("arbitrary",)),
    )(proj, proj, *pw, *douts)


def _combine_bwd(dyb, o, l, og):
    t = dyb.shape[0]
    tm = _COMB_TM

    def body(d_ref, o_ref, l_ref, og_ref, do_ref, dl_ref, dog_ref):
        ones = jnp.ones((tm, 1), F32)
        dog = []
        for p in range(N_PAIR):
            cols = slice(128 * p, 128 * (p + 1))
            _, vjp = jax.vjp(_combine_fn, o_ref[0, p], o_ref[1, p], o_ref[2, p], l_ref[0, p], l_ref[1, p], l_ref[2, p],
                             ones * og_ref[:, cols])
            res = vjp(d_ref[:, cols])
            for b in range(3):
                do_ref[b, p] = res[b]
                dl_ref[b, p] = res[3 + b]
            dog.append(_colsum8(res[6]))
        _acc(dog_ref, jnp.concatenate(dog, axis=1), pl.program_id(0) == 0)

    blk = pl.BlockSpec((3, N_PAIR, tm, 128), lambda i: (0, 0, i, 0))
    return pl.pallas_call(
        body, name="attn_combine_bwd", grid=(t // tm,),
        in_specs=[pl.BlockSpec((tm, RW), lambda i: (i, 0)), blk, blk, pl.BlockSpec((1, RW), lambda i: (0, 0))],
        out_specs=[blk, blk, pl.BlockSpec((8, RW), lambda i: (0, 0))],
        out_shape=[jax.ShapeDtypeStruct((3, N_PAIR, t, 128), F32)] * 2 + [jax.ShapeDtypeStruct((8, RW), F32)],
        compiler_params=_params(("arbitrary",)),
    )(dyb, o, l, og)


def _attn_bwd(do, dl, qkv):
    t = qkv.shape[2]

    def body(do_ref, dl_ref, q_ref, k_ref, v_ref, dq_ref, dk_ref, dv_ref):
        @pl.when(pl.program_id(1) == 0)
        def _():
            for ref in (dq_ref, dk_ref, dv_ref):
                ref[...] = jnp.zeros_like(ref)

        def unit(di, places, has_prev):
            cur = [_dilated_rows(d, r, n) for d, r, n in places]
            args = [_take(ref, (0,), cur) for ref in (q_ref, k_ref, v_ref)]
            if has_prev:
                prv = [_dilated_rows(d, r, n - 1) for d, r, n in places]
                args += [_take(ref, (0,), prv) for ref in (k_ref, v_ref)]
            _, vjp = jax.vjp(_attn_block_fn, *args)
            res = vjp((_take(do_ref, (0,), cur), _take(dl_ref, (0,), cur)))
            _put(dq_ref, (), cur, res[0], add=True)
            _put(dk_ref, (), cur, res[1], add=True)
            _put(dv_ref, (), cur, res[2], add=True)
            if has_prev:
                _put(dk_ref, (), prv, res[3], add=True)
                _put(dv_ref, (), prv, res[4], add=True)

        _for_each_sequence(t, unit)

    spec = lambda j: pl.BlockSpec((1, ATTN_GROUP, t, 128), lambda i, b: (j, i, 0, 0))
    branch = pl.BlockSpec((1, ATTN_GROUP, t, 128), lambda i, b: (b, i, 0, 0))
    out = pl.BlockSpec((ATTN_GROUP, t, 128), lambda i, b: (i, 0, 0))
    return pl.pallas_call(
        body, name="attn_bwd", grid=(N_PAIR // ATTN_GROUP, len(DILATIONS)),
        in_specs=[branch, branch, spec(0), spec(1), spec(2)], out_specs=[out] * 3,
        out_shape=[jax.ShapeDtypeStruct((N_PAIR, t, 128), F32)] * 3,
        compiler_params=_params(("parallel", "arbitrary")),
    )(do, dl, qkv, qkv, qkv)


def _in_proj_bwd(dpa, dq, dk, dv, win, x, g1, dx1):
    t = x.shape[0]
    tm = 256

    def body(dpa_ref, dq_ref, dk_ref, dv_ref, w_ref, x_ref, g_ref, dx1_ref, dproj_ref, dx_ref, dg_ref):
        parts = [dpa_ref[...]] + [ref[p] for ref in (dq_ref, dk_ref, dv_ref) for p in range(N_PAIR)]
        dproj = jnp.concatenate([z.astype(BF16) for z in parts], axis=1)
        dproj_ref[...] = dproj
        dh = _dot(dproj, w_ref[...])
        dxn, dgr = _rms_bwd(dh, x_ref[...], g_ref[...])
        dx_ref[...] = dx1_ref[...] + dxn
        _acc(dg_ref, _colsum8(dgr), pl.program_id(0) == 0)

    row = pl.BlockSpec((tm, D_MODEL), lambda i: (i, 0))
    pair = pl.BlockSpec((N_PAIR, tm, 128), lambda i: (0, i, 0))
    return pl.pallas_call(
        body, name="in_proj_bwd", grid=(t // tm,),
        in_specs=[pl.BlockSpec((tm, SHIFT_COLS), lambda i: (i, 0))] + [pair] * 3
                 + [pl.BlockSpec((IN_COLS, D_MODEL), lambda i: (0, 0)), row, pl.BlockSpec((1, D_MODEL), lambda i: (0, 0)), row],
        out_specs=[pl.BlockSpec((tm, IN_COLS), lambda i: (i, 0)), row, pl.BlockSpec((8, D_MODEL), lambda i: (0, 0))],
        out_shape=[jax.ShapeDtypeStruct((t, IN_COLS), BF16), jax.ShapeDtypeStruct((t, D_MODEL), F32),
                   jax.ShapeDtypeStruct((8, D_MODEL), F32)],
        compiler_params=_params(("arbitrary",)),
    )(dpa, dq, dk, dv, win, x, g1, dx1)


def _pad_lora(w, lo):
    z = jnp.zeros((64, RW), F32)
    return jnp.concatenate([w, z], axis=0) if lo == 0 else jnp.concatenate([z, w], axis=0)


def _local_step(x, tgt, win, vecs, w2, a2, g2m, get_rest, send_rest):
    pw = (vecs["mu_shift"], vecs["decay_w0"], _pad_lora(w2, 0), vecs["iclr_a0"], _pad_lora(a2, 64), g2m,
          vecs["k_k"], vecs["k_a"])
    h, proj, qkv = _in_proj(x, vecs["mix_norm_g"], win)
    r, lw, k2, v, kk, a, g = _prep_fwd(proj, pw)
    y, s0s = _wkv_fwd(r, lw, k2, v, kk, a)
    ya = _post_fwd(y, r, k2, v, g, vecs["ln_x_w"], vecs["ln_x_b"], vecs["r_k"])
    o_att, l_att = _attn_fwd(qkv)
    yb = _combine_fwd(o_att, l_att, vecs["attn_out_g"])

    wout, wg, wu, wd = get_rest(yb)
    ycat = jnp.concatenate([ya, yb], axis=1)
    x1, h2 = _out_proj(x, ycat, wout, vecs["ffn_norm_g"])
    gt, up, act = _ffn_up(h2, wg, wu)
    dx2, dx2b, loss8, dgf = _ffn_down_loss(x1, act, wd, vecs["final_norm_g"], tgt)

    dgt, dup = _ffn_bwd_act(dx2b, wd, gt, up)
    dx1, dx1b, dya, dyb, dg2n = _ffn_bwd_h(dgt, dup, wg, wu, dx2, x1, vecs["ffn_norm_g"], wout)
    gw = {
        "w_down": _wgrad(act, dx2b, 1408, 1024, "wgrad_down"),
        "w_gate": _wgrad(dgt, h2, 1408, 1024, "wgrad_gate"),
        "w_up": _wgrad(dup, h2, 1408, 1024, "wgrad_up"),
        "w_out": _wgrad(ycat, dx1b, 1024, 1024, "wgrad_out"),
    }

    lnw = vecs["ln_x_w"] + send_rest(gw)[0, 0]
    dy, dr_p, dk2_p, dv_p, dg, dlnw, dlnb, drk = _post_bwd(dya, y, r, k2, v, g, lnw, vecs["ln_x_b"], vecs["r_k"])
    dr_s, dlw, dk2_s, dv_s, dkk, da = _wkv_bwd(dy, s0s, r, lw, k2, v, kk, a)
    dpa, dmu, dw0, dw2p, da0, da2p, dg2m, dk_k, dk_a = _prep_bwd(
        proj, pw, (dr_p, dr_s, dlw, dk2_p, dk2_s, dv_p, dv_s, dkk, da, dg))

    do_att, dl_att, dog = _combine_bwd(dyb, o_att, l_att, vecs["attn_out_g"])
    dq, dk, dv = _attn_bwd(do_att, dl_att, qkv)
    dproj, dx, dg1 = _in_proj_bwd(dpa, dq, dk, dv, win, x, vecs["mix_norm_g"], dx1)
    gw["w_in"] = _wgrad(dproj, h, 1664, 1024, "wgrad_in")
    gw["decay_w2"] = dw2p[:64]
    gw["iclr_a2"] = da2p[64:]
    gw["gate_g2"] = dg2m
    gv = {"mix_norm_g": dg1, "mu_shift": dmu, "decay_w0": dw0, "iclr_a0": da0, "k_k": dk_k, "k_a": dk_a, "r_k": drk,
          "ln_x_w": dlnw, "ln_x_b": dlnb, "attn_out_g": dog, "ffn_norm_g": dg2n, "final_norm_g": dgf}
    return loss8, dx, gw, gv


N_CHIP = 4
N_DEV = 8
MATS = ("w_in", "w_out", "w_gate", "w_up", "w_down")
LORAS = ("decay_w2", "iclr_a2", "gate_g2")
VECS = (("mix_norm_g", 1024), ("mu_shift", 1792), ("decay_w0", 512), ("iclr_a0", 512), ("k_k", 512), ("k_a", 512),
        ("r_k", 512), ("ln_x_w", 512), ("ln_x_b", 512), ("attn_out_g", 512), ("ffn_norm_g", 1024),
        ("final_norm_g", 1024))
N_VEC = sum(n for _, n in VECS)
N_SMALL = N_VEC + 128
ANY = pl.BlockSpec(memory_space=pl.ANY)


def _flip(v, f):
    return 1 - v if f else v


class _Me:
    def __init__(self, mode):
        x, y, c = lax.axis_index("x"), lax.axis_index("y"), lax.axis_index("c")
        self.core, self.chip, self.dev = c, 2 * x + y, 4 * x + 2 * y + c
        self.sibling = (x, y, 1 - c)
        if mode == "chips":
            self.peers = [(px, py, c) for px, py in ((1 - x, y), (x, 1 - y), (1 - x, 1 - y))]
        else:
            self.peers = [(_flip(x, k & 4), _flip(y, k & 2), _flip(c, k & 1)) for k in range(1, N_DEV)]


def _half(core, rows):
    h = rows // 2
    return pl.ds(pl.multiple_of(core * h, h), h)


def _peer_copy(srcs, dsts, kinds, send_sems, recv_sems, me, j, i, incoming):
    px, py, pc = me.peers[j]
    pchip, pdev = 2 * px + py, 4 * px + 2 * py + pc
    src, dst, kind = srcs[i], dsts[i], kinds[i]
    if kind == "gather":
        rows = _half(me.core, src.shape[0])
        src, dst = src.at[rows], dst.at[pchip if incoming else me.chip, rows]
    elif kind == "scatter":
        src, dst = src.at[pchip, _half(pc, src.shape[1])], dst.at[pdev if incoming else me.dev]
    else:
        dst = dst.at[pdev if incoming else me.dev]
    n = len(srcs)
    return pltpu.make_async_remote_copy(src_ref=src, dst_ref=dst, send_sem=send_sems.at[n * j + i],
                                        recv_sem=recv_sems.at[n * j + i], device_id=(px, py, pc), device_id_type=MESH)


def _mode(kinds):
    return "chips" if kinds[0] == "gather" else "devs"


def _npeer(kinds):
    return N_CHIP - 1 if kinds[0] == "gather" else N_DEV - 1


def _swap_gathered(lands, name):
    n = len(lands)

    def body(*refs):
        dsts, send_sems, recv_sems = refs[n:2 * n], refs[2 * n], refs[2 * n + 1]
        me = _Me("chips")

        def copy(j, i, incoming):
            px, py, _ = me.peers[j]
            rows_out, rows_in = _half(me.core, dsts[i].shape[1]), _half(1 - me.core, dsts[i].shape[1])
            return pltpu.make_async_remote_copy(
                src_ref=dsts[i].at[2 * px + py, rows_out], dst_ref=dsts[i].at[2 * px + py, rows_in if incoming else rows_out],
                send_sem=send_sems.at[n * j + i], recv_sem=recv_sems.at[n * j + i], device_id=me.sibling, device_id_type=MESH)

        sends = [copy(j, i, False) for j in range(3) for i in range(n)]
        for cp in sends:
            cp.start()
        for j in range(3):
            for i in range(n):
                copy(j, i, True).wait_recv()
        for cp in sends:
            cp.wait_send()

    return pl.pallas_call(
        body, name=name, in_specs=[ANY] * n, out_specs=[ANY] * n,
        out_shape=[jax.ShapeDtypeStruct(l.shape, l.dtype) for l in lands],
        input_output_aliases={i: i for i in range(n)},
        scratch_shapes=[pltpu.SemaphoreType.DMA((3 * n,)), pltpu.SemaphoreType.DMA((3 * n,))],
    )(*lands)


def _join_halves(sums, name):
    n = len(sums)

    def body(*refs):
        dsts, send_sems, recv_sems = refs[n:2 * n], refs[2 * n], refs[2 * n + 1]
        me = _Me("chips")

        def copy(i, incoming):
            mine, other = _half(me.core, dsts[i].shape[0]), _half(1 - me.core, dsts[i].shape[0])
            return pltpu.make_async_remote_copy(src_ref=dsts[i].at[mine], dst_ref=dsts[i].at[other if incoming else mine],
                                                send_sem=send_sems.at[i], recv_sem=recv_sems.at[i],
                                                device_id=me.sibling, device_id_type=MESH)

        sends = [copy(i, False) for i in range(n)]
        for cp in sends:
            cp.start()
        for i in range(n):
            copy(i, True).wait_recv()
        for cp in sends:
            cp.wait_send()

    return pl.pallas_call(
        body, name=name, in_specs=[ANY] * n, out_specs=[ANY] * n,
        out_shape=[jax.ShapeDtypeStruct(s.shape, s.dtype) for s in sums],
        input_output_aliases={i: i for i in range(n)},
        scratch_shapes=[pltpu.SemaphoreType.DMA((n,)), pltpu.SemaphoreType.DMA((n,))],
    )(*sums)


HBM = pl.BlockSpec(memory_space=pltpu.HBM)
SEM = pl.BlockSpec(memory_space=pltpu.SEMAPHORE)
EFFECT = pltpu.SideEffectType.DATAFLOW_SIDE_EFFECTING


def _swap_start(arrs, lands, kinds, name):
    n = len(arrs)

    def body(*refs):
        srcs, dsts, send_sems, recv_sems, token = refs[:n], refs[n:2 * n], refs[2 * n], refs[2 * n + 1], refs[-1]
        me = _Me(_mode(kinds))
        for j in range(len(me.peers)):
            for i in range(n):
                _peer_copy(srcs, dsts, kinds, send_sems, recv_sems, me, j, i, False).start()
        token[...] = jnp.zeros_like(token)

    ns = _npeer(kinds) * n
    outs = pl.pallas_call(
        body, name=name,
        out_shape=(pltpu.SemaphoreType.DMA((ns,)), pltpu.SemaphoreType.DMA((ns,)),
                   *[pltpu.HBM(a.shape, a.dtype) for a in arrs], *[pltpu.HBM(l.shape, l.dtype) for l in lands],
                   jax.ShapeDtypeStruct((8, 128), F32)),
        in_specs=[HBM] * (2 * n), out_specs=(SEM, SEM, *[HBM] * (2 * n), pl.BlockSpec(memory_space=pltpu.VMEM)),
        input_output_aliases={k: 2 + k for k in range(2 * n)},
        compiler_params=pltpu.CompilerParams(has_side_effects=EFFECT),
    )(*[pltpu.with_memory_space_constraint(a, pltpu.HBM) for a in arrs],
      *[pltpu.with_memory_space_constraint(l, pltpu.HBM) for l in lands])
    return outs[0], outs[1], outs[2:2 + n], outs[2 + n:2 + 2 * n], outs[-1]


def _swap_wait(send_sems, recv_sems, srcs_thru, lands_thru, after, kinds, name):
    n = len(srcs_thru)

    def body(*refs):
        srcs, dsts, s_sems, r_sems = refs[:n], refs[n:2 * n], refs[2 * n], refs[2 * n + 1]
        me = _Me(_mode(kinds))
        for j in range(len(me.peers)):
            for i in range(n):
                cp = _peer_copy(srcs, dsts, kinds, s_sems, r_sems, me, j, i, True)
                cp.wait_send()
                cp.wait_recv()

    outs = pl.pallas_call(
        body, name=name,
        out_shape=tuple(pltpu.HBM(a.shape, a.dtype) for a in (*srcs_thru, *lands_thru)),
        in_specs=[HBM] * (2 * n) + [SEM, SEM, ANY], out_specs=tuple([HBM] * (2 * n)),
        input_output_aliases={k: k for k in range(2 * n)},
        compiler_params=pltpu.CompilerParams(has_side_effects=EFFECT),
    )(*srcs_thru, *lands_thru, send_sems, recv_sems, after)
    return outs[n:]


def _adamw(w, g, m, v):
    m = ADAM_B1 * m + (1.0 - ADAM_B1) * g
    v = ADAM_B2 * v + (1.0 - ADAM_B2) * (g * g)
    m_hat = m / (1.0 - ADAM_B1 ** ADAM_STEP)
    v_hat = v / (1.0 - ADAM_B2 ** ADAM_STEP)
    delta = -ADAM_LR * (m_hat / (jnp.sqrt(v_hat) + ADAM_EPS) + ADAM_WD * w)
    return delta, m, v


def _reduce8(rbuf, core, tr, name):
    _, h, cols = rbuf.shape

    def body(core_ref, r_ref, g_ref):
        g = r_ref[0].astype(F32)
        for s in range(1, N_DEV):
            g = g + r_ref[s].astype(F32)
        g_ref[...] = g

    return pl.pallas_call(
        body, name=name,
        grid_spec=pltpu.PrefetchScalarGridSpec(
            num_scalar_prefetch=1, grid=(h // tr,),
            in_specs=[pl.BlockSpec((N_DEV, tr, cols), lambda i, core_ref: (0, i, 0))],
            out_specs=pl.BlockSpec((tr, cols), lambda i, core_ref: (core_ref[0] * (h // tr) + i, 0))),
        out_shape=jax.ShapeDtypeStruct((2 * h, cols), F32),
        compiler_params=_params(("parallel",)),
    )(core, rbuf)


def _adamw_call(g, w, m, v, tr, name):
    _, rows, cols = w.shape

    def body(g_in, w_ref, m_ref, v_ref, g_ref, d_ref, nm_ref, nv_ref):
        g = g_in[...]
        g_ref[0] = g
        d_ref[0], nm_ref[0], nv_ref[0] = _adamw(w_ref[0], g, m_ref[0], v_ref[0])

    row = pl.BlockSpec((1, tr, cols), lambda i: (0, i, 0))
    return pl.pallas_call(
        body, name=name, grid=(rows // tr,),
        in_specs=[pl.BlockSpec((tr, cols), lambda i: (i, 0)), row, row, row], out_specs=[row] * 4,
        out_shape=[jax.ShapeDtypeStruct(w.shape, F32)] * 4,
        compiler_params=_params(("parallel",)),
    )(g, w, m, v)


def _rowsum_small(parts, loss8):
    def body(*refs):
        out = refs[-1]
        c0 = 0
        for ref in refs[:-1]:
            n = ref.shape[1]
            out[:, c0:c0 + n] = jnp.sum(ref[...], axis=0, keepdims=True)
            c0 += n

    return pl.pallas_call(body, name="rowsum_small", out_shape=jax.ShapeDtypeStruct((1, N_SMALL), F32))(*parts, loss8)


def _reduce_adamw_small(sbuf, ws, ms, vs):
    nv = len(ws)

    def body(*refs):
        s_ref, ins, outs = refs[0], refs[1:1 + 3 * nv], refs[1 + 3 * nv:]
        tot = s_ref[0]
        for s in range(1, N_DEV):
            tot = tot + s_ref[s]
        c0 = 0
        for i in range(nv):
            n = ins[i].shape[1]
            g = tot[:, c0:c0 + n]
            outs[i][...] = g
            outs[nv + i][...], outs[2 * nv + i][...], outs[3 * nv + i][...] = _adamw(
                ins[i][...], g, ins[nv + i][...], ins[2 * nv + i][...])
            c0 += n
        outs[-1][...] = tot[:, c0:]

    return pl.pallas_call(
        body, name="reduce_adamw_small",
        out_shape=[jax.ShapeDtypeStruct(a.shape, F32) for a in ws] * 4 + [jax.ShapeDtypeStruct((1, 128), F32)],
    )(sbuf, *ws, *ms, *vs)


_TRANSPOSED = ("w_in", "w_gate", "w_up")
_ROW_STACKED = MATS
_ADAM_TILE = {"w_in": 208, "w_out": 256, "w_gate": 176, "w_up": 176, "w_down": 176, "decay_w2": 64, "iclr_a2": 64,
              "gate_g2": 128}
_SUM_TILE = {"w_in": 208, "w_out": 128, "w_gate": 176, "w_up": 176, "w_down": 176, "decay_w2": 32, "iclr_a2": 32,
             "gate_g2": 64}


def _full(n, stacked):
    p, r, c = stacked.shape
    if n in _ROW_STACKED:
        return stacked.reshape(p * r, c)
    return jnp.transpose(stacked, (1, 0, 2)).reshape(r, p * c)


def _by_chip(n, full):
    if n in _ROW_STACKED:
        return full.reshape(N_CHIP, full.shape[0] // N_CHIP, full.shape[1])
    r, c = full.shape
    return jnp.transpose(full.reshape(r, N_CHIP, c // N_CHIP), (1, 0, 2))


def _with_own(land_shape, dtype, own, slot):
    return lax.dynamic_update_slice(lax.empty(land_shape, dtype), own[None], (slot,) + (0,) * own.ndim)


def kernel(x, mix_norm_g, w_in, mu_shift, decay_w0, decay_w2, iclr_a0, iclr_a2, gate_g2, k_k, k_a, r_k, ln_x_w, ln_x_b, attn_out_g, w_out, ffn_norm_g, w_gate, w_up, w_down, final_norm_g, loss_target, m_mix_norm_g, m_w_in, m_mu_shift, m_decay_w0, m_decay_w2, m_iclr_a0, m_iclr_a2, m_gate_g2, m_k_k, m_k_a, m_r_k, m_ln_x_w, m_ln_x_b, m_attn_out_g, m_w_out, m_ffn_norm_g, m_w_gate, m_w_up, m_w_down, m_final_norm_g, v_mix_norm_g, v_w_in, v_mu_shift, v_decay_w0, v_decay_w2, v_iclr_a0, v_iclr_a2, v_gate_g2, v_k_k, v_k_a, v_r_k, v_ln_x_w, v_ln_x_b, v_attn_out_g, v_w_out, v_ffn_norm_g, v_w_gate, v_w_up, v_w_down, v_final_norm_g):
    names = ("mix_norm_g", "w_in", "mu_shift", "decay_w0", "decay_w2", "iclr_a0", "iclr_a2", "gate_g2", "k_k", "k_a",
             "r_k", "ln_x_w", "ln_x_b", "attn_out_g", "w_out", "ffn_norm_g", "w_gate", "w_up", "w_down", "final_norm_g")
    w = dict(zip(names, (mix_norm_g, w_in, mu_shift, decay_w0, decay_w2, iclr_a0, iclr_a2, gate_g2, k_k, k_a, r_k,
                         ln_x_w, ln_x_b, attn_out_g, w_out, ffn_norm_g, w_gate, w_up, w_down, final_norm_g)))
    m = dict(zip(names, (m_mix_norm_g, m_w_in, m_mu_shift, m_decay_w0, m_decay_w2, m_iclr_a0, m_iclr_a2, m_gate_g2,
                         m_k_k, m_k_a, m_r_k, m_ln_x_w, m_ln_x_b, m_attn_out_g, m_w_out, m_ffn_norm_g, m_w_gate,
                         m_w_up, m_w_down, m_final_norm_g)))
    v = dict(zip(names, (v_mix_norm_g, v_w_in, v_mu_shift, v_decay_w0, v_decay_w2, v_iclr_a0, v_iclr_a2, v_gate_g2,
                         v_k_k, v_k_a, v_r_k, v_ln_x_w, v_ln_x_b, v_attn_out_g, v_w_out, v_ffn_norm_g, v_w_gate,
                         v_w_up, v_w_down, v_final_norm_g)))
    first = ("w_in",) + LORAS
    rest = ("w_out", "w_gate", "w_up", "w_down")
    xi, yi, ci = lax.axis_index("x"), lax.axis_index("y"), lax.axis_index("c")
    my_chip, my_dev = 2 * xi + yi, 4 * xi + 2 * yi + ci
    gather, scatter = ("gather",) * 4, ("scatter",) * 4

    sh = lambda z, n: jnp.transpose(z[0]) if n in _TRANSPOSED else z[0]
    mine = [sh(w["w_in"], "w_in").astype(BF16)] + [w[n][0] for n in LORAS]
    early = _swap_start(mine, [_with_own((N_CHIP,) + a.shape, a.dtype, a, my_chip) for a in mine], gather, "gather_first_start")
    wb = {n: (sh(w[n], n) + early[4][0, 0]).astype(BF16) for n in rest}
    lands = [_with_own((N_CHIP,) + wb[n].shape, BF16, wb[n], my_chip) for n in rest]
    ssem, rsem, srcs_thru, lands_thru, tok = _swap_start([wb[n] for n in rest], lands, gather, "gather_rest_start")
    got = _swap_wait(early[0], early[1], early[2], early[3], tok, gather, "gather_first_wait")
    win, w2, a2, g2m = (_full(n, z) for n, z in zip(first, _swap_gathered(got, "gather_first_halves")))

    vecs = {n: w[n].reshape(1, sz) for n, sz in VECS}
    vecs["mix_norm_g"] = vecs["mix_norm_g"] + tok[0, 0]

    def get_rest(after):
        halves = _swap_wait(ssem, rsem, srcs_thru, lands_thru, after, gather, "gather_rest_wait")
        return [_full(n, z) for n, z in zip(rest, _swap_gathered(halves, "gather_rest_halves"))]

    flight = []

    def my_half(g):
        h = g.shape[1] // 2
        return lax.dynamic_slice(g, (my_chip, ci * h, 0), (1, h, g.shape[2]))[0]

    def send_rest(gw):
        gs = [_by_chip(n, gw[n]).astype(BF16) for n in rest]
        into = [_with_own((N_DEV,) + my_half(g).shape, BF16, my_half(g), my_dev) for g in gs]
        flight.extend(_swap_start(gs, into, scatter, "exchange_rest_start"))
        return flight[4]

    loss8, dx, gw, gv = _local_step(x[0], loss_target[0], win, vecs, w2, a2, g2m, get_rest, send_rest)

    small = _rowsum_small([gv[n] for n, _ in VECS], loss8)
    gs = [_by_chip(n, gw[n]).astype(BF16) for n in first]
    into = [_with_own((N_DEV,) + my_half(g).shape, BF16, my_half(g), my_dev) for g in gs]
    into.append(_with_own((N_DEV,) + small.shape, F32, small, my_dev))
    last = _swap_start(gs + [small], into, scatter + ("all",), "exchange_first_start")

    core = jnp.reshape(ci, (1,)).astype(jnp.int32)

    def update(group, rbufs, tag):
        sums = [_reduce8(rb, core, _SUM_TILE[n], "reduce_" + n) for n, rb in zip(group, rbufs)]
        gsum = _join_halves(sums, "join_halves_" + tag)
        out = {}
        for n, g in zip(group, gsum):
            r = _adamw_call(g, sh(w[n], n)[None], sh(m[n], n)[None], sh(v[n], n)[None], _ADAM_TILE[n], "adamw_" + n)
            out[n] = [jnp.transpose(z[0])[None] for z in r] if n in _TRANSPOSED else r
        return out

    res = update(rest, _swap_wait(flight[0], flight[1], flight[2], flight[3], last[4], scatter, "exchange_rest_wait"), "rest")
    got = _swap_wait(last[0], last[1], last[2], last[3], res["w_down"][1], scatter + ("all",), "exchange_first_wait")
    res.update(update(first, got[:4], "first"))
    rows = lambda d: [d[n].reshape(1, sz) for n, sz in VECS]
    small_res = _reduce_adamw_small(got[4], rows(w), rows(m), rows(v))

    outs = []
    for k in range(4):
        piece = {n: r[k] for n, r in res.items()}
        for i, (n, _) in enumerate(VECS):
            piece[n] = small_res[k * len(VECS) + i].reshape(w[n].shape)
        outs.extend(piece[n] for n in names)
    return (small_res[-1][0, 0], dx[None], *outs)
```

```python
import jax
import jax.numpy as jnp
from jax import lax
from jax.experimental import pallas as pl
from jax.experimental.pallas import tpu as pltpu

F32 = jnp.float32
BF16 = jnp.bfloat16

D_MODEL = 1024
HEAD_DIM = 64
RW = 512
N_PAIR = RW // 128
SHIFT_COLS = 1792
IN_COLS = 3328
D_FF = 2816
NORM_EPS = 1e-6
GN_EPS = 64e-5
CHUNK = 64
SUB = 16
WKV_PASSES = 1
ATTN_PASSES = 1
ATTN_BLOCK = 128
DILATIONS = (1, 4, 16)
NEG = -1e30
ADAM_LR, ADAM_B1, ADAM_B2, ADAM_EPS, ADAM_WD, ADAM_STEP = 0.001, 0.9, 0.999, 1e-08, 0.01, 10
VMEM_LIMIT = 56 * 1024 * 1024
MESH = pl.DeviceIdType.MESH


def _params(sem=None, **kw):
    return pltpu.CompilerParams(dimension_semantics=sem, vmem_limit_bytes=VMEM_LIMIT, **kw)


def _dot(a, b, prec=None):
    return lax.dot_general(a, b, (((1,), (0,)), ((), ())), preferred_element_type=F32, precision=prec)


def _dot_nt(a, b, prec=None):
    return lax.dot_general(a, b, (((1,), (1,)), ((), ())), preferred_element_type=F32, precision=prec)


def _dot_tn(a, b, prec=None):
    return lax.dot_general(a, b, (((0,), (0,)), ((), ())), preferred_element_type=F32, precision=prec)


_FORMS = {"nn": ((1,), (0,)), "nt": ((1,), (1,)), "tn": ((0,), (0,))}


def _dg(a, b, form):
    if a.ndim == 3 or b.ndim == 3:
        nb = a.shape[0] if a.ndim == 3 else b.shape[0]
        return jnp.stack([_dg(a[i] if a.ndim == 3 else a, b[i] if b.ndim == 3 else b, form) for i in range(nb)], axis=0)
    return lax.dot_general(a, b, (_FORMS[form], ((), ())), preferred_element_type=F32)


def _split2(x):
    hi = x.astype(BF16)
    return hi, (x - hi.astype(F32)).astype(BF16)


def _split3(x):
    hi = x.astype(BF16)
    rest = x - hi.astype(F32)
    mid = rest.astype(BF16)
    return hi, mid, (rest - mid.astype(F32)).astype(BF16)


def _mm_raw(a, b, form, mode):
    if mode == 1:
        return _dg(a.astype(BF16), b.astype(BF16), form)
    if mode == 3:
        ah, al = _split2(a)
        bh, bl = _split2(b)
        return _dg(ah, bh, form) + (_dg(ah, bl, form) + _dg(al, bh, form))
    if mode == "L3":
        ab = a.astype(BF16)
        b1, b2, b3 = _split3(b)
        if form == "nn":
            n = b.shape[-1]
            wide = _dg(ab, jnp.concatenate([b1, b2, b3], axis=-1), form)
            return wide[..., :n] + (wide[..., n:2 * n] + wide[..., 2 * n:])
        return _dg(ab, b1, form) + (_dg(ab, b2, form) + _dg(ab, b3, form))
    assert mode == "R3", mode
    bb = b.astype(BF16)
    a1, a2, a3 = _split3(a)
    if form in ("nn", "nt"):
        m = a.shape[-2]
        tall = _dg(jnp.concatenate([a1, a2, a3], axis=-2), bb, form)
        return tall[..., :m, :] + (tall[..., m:2 * m, :] + tall[..., 2 * m:, :])
    return _dg(a1, bb, form) + (_dg(a2, bb, form) + _dg(a3, bb, form))


def _mm(a, b, form, mode):
    @jax.custom_vjp
    def f(a, b):
        return _mm_raw(a, b, form, mode)

    def fwd(a, b):
        return _mm_raw(a, b, form, mode), (a, b)

    def bwd(res, ct):
        a, b = res
        la = {1: 1, 3: 3, "L3": None, "R3": "R3"}[mode]
        lb = {1: 1, 3: 3, "L3": "L3", "R3": None}[mode]
        if form == "nn":
            da = None if la is None else _mm_raw(ct, b, "nt", la)
            db = None if lb is None else _mm_raw(a, ct, "tn", lb)
        elif form == "nt":
            da = None if la is None else _mm_raw(ct, b, "nn", la)
            db = None if lb is None else _mm_raw(ct, a, "tn", "R3" if lb == "L3" else lb)
        else:
            da = None if la is None else _mm_raw(b, ct, "nt", "L3" if la == "R3" else la)
            db = None if lb is None else _mm_raw(a, ct, "nn", lb)
        return (jnp.zeros_like(a) if da is None else da, jnp.zeros_like(b) if db is None else db)

    f.defvjp(fwd, bwd)
    return f(a, b)


def _seg_ones(n):
    r = lax.broadcasted_iota(jnp.int32, (n, n), 0) // HEAD_DIM
    c = lax.broadcasted_iota(jnp.int32, (n, n), 1) // HEAD_DIM
    return (r == c).astype(F32)


def _segsum(x, seg):
    return _mm(x, seg, "nn", "R3")


def _rms_fwd(x, g):
    rstd = lax.rsqrt(jnp.mean(x * x, axis=-1, keepdims=True) + NORM_EPS)
    return x * rstd * g


def _rms_bwd(dy, x, g):
    rstd = lax.rsqrt(jnp.mean(x * x, axis=-1, keepdims=True) + NORM_EPS)
    xn = x * rstd
    dxn = dy * g
    dx = rstd * (dxn - xn * jnp.mean(dxn * xn, axis=-1, keepdims=True))
    return dx, dy * xn


def _sigmoid(x):
    return 1.0 / (1.0 + jnp.exp(-x))


def _softplus(x):
    return jnp.maximum(x, 0.0) + jnp.log(1.0 + jnp.exp(-jnp.abs(x)))


def _acc(ref, val, first):
    @pl.when(first)
    def _():
        ref[...] = val

    @pl.when(jnp.logical_not(first))
    def _():
        ref[...] += val


def _colsum8(v):
    rows, n = v.shape
    return jnp.sum(v.reshape(rows // 8, 8, n), axis=0)


def _prep_fn(p, pprev, mu, w0, w2p, a0, a2p, g2, k_k, k_a):
    seg = _seg_ones(RW)
    ps = p + (pprev - p) * mu
    r = ps[:, 0:RW]
    k = ps[:, RW:2 * RW]
    v = ps[:, 2 * RW:3 * RW]
    xwa = ps[:, 3 * RW:3 * RW + 128]
    xg = ps[:, 3 * RW + 128:3 * RW + 256]
    wraw = -_softplus(-(w0 + _mm(jnp.tanh(xwa), w2p, "nn", 3))) - 0.5
    lw = -jnp.exp(wraw)
    a = _sigmoid(a0 + _mm(xwa, a2p, "nn", 3))
    g = _mm(_sigmoid(xg), g2, "nn", 3)
    kk = k * k_k
    kk = kk / jnp.maximum(jnp.sqrt(_segsum(kk * kk, seg)), 1e-12)
    k2 = k * (1.0 + (a - 1.0) * k_a)
    return r, lw, k2, v, kk, a, g


def _transposed(z):
    return jnp.stack([z[i].T for i in range(z.shape[0])], axis=0) if z.ndim == 3 else z.T


def _solve_unit_lower(lmat, rhs):
    c = lmat.shape[-1]
    row = lax.broadcasted_iota(jnp.int32, (c, c), 0)
    col = lax.broadcasted_iota(jnp.int32, (c, c), 1)
    eye = (row == col).astype(F32)
    ld = jnp.where(row // SUB == col // SUB, lmat, 0.0)
    lo = lmat - ld
    x = eye + ld
    m = ld
    mm = lambda p, q: _mm(p, q, "nn", WKV_PASSES)
    cat = jnp.concatenate
    m = mm(m, m)
    for _ in range(2):
        mx = mm(m, cat([m, x], axis=-1))
        m, x = mx[..., :c], x + mx[..., c:]
    x = x + mm(m, x)
    gw = mm(x, cat([lo, rhs], axis=-1))
    g, w = gw[..., :c], gw[..., c:]
    gg = mm(g, cat([g, w], axis=-1))
    w = w + gg[..., c:]
    return w + mm(gg[..., :c], w)


def _wkv_chunk_fn(s0, r, lw, k, v, kk, a):
    c = r.shape[-2]
    n = 2 * c
    row = lax.broadcasted_iota(jnp.int32, (n, n), 0)
    col = lax.broadcasted_iota(jnp.int32, (n, n), 1)
    same = (row // c) == (col // c)
    incl = jnp.logical_and(row >= col, same)
    strict = jnp.logical_and(row > col, same)
    sel = (lax.broadcasted_iota(jnp.int32, (n, 128), 0) // c) == (lax.broadcasted_iota(jnp.int32, (n, 128), 1) // HEAD_DIM)
    two = lambda z: jnp.concatenate([z, z], axis=-2)
    lw2 = two(lw)
    mm = lambda p_, q_, form: _mm(p_, q_, form, WKV_PASSES)
    cl = _mm(incl.astype(F32), lw2, "nn", "L3")
    p = jnp.exp(cl)
    pinv = jnp.exp(-cl)
    pprev = jnp.exp(cl - lw2)
    kk2 = two(kk)
    at = jnp.where(sel, -kk2 * pprev, 0.0)
    bt = jnp.where(sel, kk2 * two(a) * pinv, 0.0)
    kt = jnp.where(sel, two(k) * pinv, 0.0)
    rt = jnp.where(sel, two(r) * p, 0.0)
    vt = jnp.where(sel, two(v), 0.0)
    cat = jnp.concatenate
    bk = cat([bt, kt], axis=-2)
    arbk = mm(cat([at, rt], axis=-2), bk, "nt")
    ab, ak = jnp.where(strict, arbk[..., :n, :n], 0.0), jnp.where(strict, arbk[..., :n, n:], 0.0)
    rb, rk = jnp.where(incl, arbk[..., n:, :n], 0.0), jnp.where(incl, arbk[..., n:, n:], 0.0)
    s0t = _transposed(s0)
    u = _solve_unit_lower(ab, mm(cat([at, ak], axis=-1), cat([s0t, vt], axis=-2), "nn"))
    y2 = mm(cat([rt, rb, rk], axis=-1), cat([s0t, u, vt], axis=-2), "nn")
    plast = jnp.exp(jnp.sum(lw, axis=-2, keepdims=True))
    s1 = (s0 + mm(cat([u, vt], axis=-2), bk, "tn")) * plast
    r2 = lax.broadcasted_iota(jnp.int32, (128, 128), 0) // HEAD_DIM
    c2 = lax.broadcasted_iota(jnp.int32, (128, 128), 1) // HEAD_DIM
    return y2[..., :c, :] + y2[..., c:, :], jnp.where(r2 == c2, s1, 0.0)


def _post_fn(y, r, k2, v, g, lnw, lnb, rk):
    seg = _seg_ones(RW)
    mean = _segsum(y, seg) * (1.0 / HEAD_DIM)
    yc = y - mean
    var = _segsum(yc * yc, seg) * (1.0 / HEAD_DIM)
    yn = yc * lax.rsqrt(var + GN_EPS)
    out = yn * lnw + lnb + _segsum(r * k2 * rk, seg) * v
    return out * g


def _attn_block_fn(q, kc, vc, kp=None, vp=None):
    n = ATTN_BLOCK
    qi = lax.broadcasted_iota(jnp.int32, (n, n), 0)
    kj = lax.broadcasted_iota(jnp.int32, (n, n), 1)
    lane = lax.broadcasted_iota(jnp.int32, (1, 128), 1)
    scale = HEAD_DIM ** -0.5
    valid = kj <= qi
    keys, vals = kc, vc
    if kp is not None:
        valid = jnp.concatenate([valid, kj >= qi], axis=-1)
        keys, vals = jnp.concatenate([kc, kp], axis=-2), jnp.concatenate([vc, vp], axis=-2)
    m0 = (lane // HEAD_DIM) == 0
    q2 = jnp.concatenate([jnp.where(m0, q, 0.0), jnp.where(m0, 0.0, q)], axis=-2)
    valid2 = jnp.concatenate([valid, valid], axis=-2)
    s = jnp.where(valid2, _mm(q2, keys, "nt", ATTN_PASSES) * scale, NEG)
    m = jnp.max(s, axis=-1, keepdims=True)
    p = jnp.exp(s - m)
    den = jnp.sum(p, axis=-1, keepdims=True)
    o2 = _mm(p, vals, "nn", ATTN_PASSES) / den
    l2 = m + jnp.log(den)
    return jnp.where(m0, o2[..., :n, :], o2[..., n:, :]), jnp.where(m0, l2[..., :n, :], l2[..., n:, :])


def _attn_block_bwd(q, kc, vc, kp, vp, o, lse, do, dl):
    n = ATTN_BLOCK
    cat = jnp.concatenate
    qi = lax.broadcasted_iota(jnp.int32, (n, n), 0)
    kj = lax.broadcasted_iota(jnp.int32, (n, n), 1)
    m0 = (lax.broadcasted_iota(jnp.int32, (1, 128), 1) // HEAD_DIM) == 0
    scale = HEAD_DIM ** -0.5
    valid = kj <= qi
    keys, vals = kc, vc
    if kp is not None:
        valid = cat([valid, kj >= qi], axis=-1)
        keys, vals = cat([kc, kp], axis=-2), cat([vc, vp], axis=-2)
    stack = lambda z: cat([jnp.where(m0, z, 0.0), jnp.where(m0, 0.0, z)], axis=-2)
    q2, do2 = stack(q), stack(do)
    lse2 = cat([jnp.max(jnp.where(m0, lse, NEG), axis=-1, keepdims=True),
                jnp.max(jnp.where(m0, NEG, lse), axis=-1, keepdims=True)], axis=-2)
    delta = jnp.sum(do2 * cat([o, o], axis=-2), axis=-1, keepdims=True)
    dlse = jnp.sum(stack(dl), axis=-1, keepdims=True)
    mm = lambda a, b, form: _mm_raw(a, b, form, ATTN_PASSES)
    s = jnp.where(cat([valid, valid], axis=-2), mm(q2, keys, "nt") * scale, NEG)
    p = jnp.exp(s - lse2)
    ds = p * (mm(do2, vals, "nt") - delta + dlse)
    dq2 = mm(ds, keys, "nn") * scale
    dq = jnp.where(m0, dq2[..., :n, :], dq2[..., n:, :])
    dkeys = mm(ds, q2, "tn") * scale
    dvals = mm(p, do2, "tn")
    if kp is None:
        return dq, dkeys, dvals
    return dq, dkeys[..., :n, :], dvals[..., :n, :], dkeys[..., n:, :], dvals[..., n:, :]


def _combine_fn(o1, o2, o3, l1, l2, l3, og):
    seg = _seg_ones(o1.shape[-1])
    m = jnp.maximum(jnp.maximum(l1, l2), l3)
    e1, e2, e3 = jnp.exp(l1 - m), jnp.exp(l2 - m), jnp.exp(l3 - m)
    o = (e1 * o1 + e2 * o2 + e3 * o3) / (e1 + e2 + e3)
    o = o * lax.rsqrt(_segsum(o * o, seg) * (1.0 / HEAD_DIM) + NORM_EPS)
    return o * og


def _in_proj(x, g1, win):
    t = x.shape[0]
    tm = 512

    def body(x_ref, g_ref, w_ref, h_ref, pa_ref, qkv_ref):
        h = _rms_fwd(x_ref[...], g_ref[...]).astype(BF16)
        h_ref[...] = h
        proj = _dot_nt(h, w_ref[...])
        pa_ref[...] = proj[:, :SHIFT_COLS]
        for j in range(3):
            for p in range(N_PAIR):
                c0 = SHIFT_COLS + j * RW + p * 128
                qkv_ref[j, p] = proj[:, c0:c0 + 128]

    return pl.pallas_call(
        body, name="in_proj", grid=(t // tm,),
        in_specs=[pl.BlockSpec((tm, D_MODEL), lambda i: (i, 0)), pl.BlockSpec((1, D_MODEL), lambda i: (0, 0)),
                  pl.BlockSpec((IN_COLS, D_MODEL), lambda i: (0, 0))],
        out_specs=[pl.BlockSpec((tm, D_MODEL), lambda i: (i, 0)), pl.BlockSpec((tm, SHIFT_COLS), lambda i: (i, 0)),
                   pl.BlockSpec((3, N_PAIR, tm, 128), lambda i: (0, 0, i, 0))],
        out_shape=[jax.ShapeDtypeStruct((t, D_MODEL), BF16), jax.ShapeDtypeStruct((t, SHIFT_COLS), F32),
                   jax.ShapeDtypeStruct((3, N_PAIR, t, 128), F32)],
        compiler_params=_params(("parallel",)),
    )(x, g1, win)


def _shifted(p, last8, first):
    prow = jnp.where(first, 0.0, last8[7:8, :])
    rolled = pltpu.roll(p, 1, axis=0)
    rid = lax.broadcasted_iota(jnp.int32, p.shape, 0)
    return jnp.where(rid == 0, prow, rolled)


_PREP_TM = 256


def _prep_specs(tm):
    vec = lambda n: pl.BlockSpec((1, n), lambda i: (0, 0))
    mat = lambda r, n: pl.BlockSpec((r, n), lambda i: (0, 0))
    return [vec(SHIFT_COLS), vec(RW), mat(128, RW), vec(RW), mat(128, RW), mat(128, RW), vec(RW), vec(RW)]


def _prep_fwd(proj, pw):
    t = proj.shape[0]
    tm = _PREP_TM

    def body(p_ref, l8_ref, mu, w0, w2p, a0, a2p, g2, k_k, k_a, *outs):
        p = p_ref[...]
        pprev = _shifted(p, l8_ref[...], pl.program_id(0) == 0)
        res = _prep_fn(p, pprev, mu[...], w0[...], w2p[...], a0[...], a2p[...], g2[...], k_k[...], k_a[...])
        for o_ref, val in zip(outs, res):
            o_ref[...] = val

    row = pl.BlockSpec((tm, RW), lambda i: (i, 0))
    return pl.pallas_call(
        body, name="rwkv_prep", grid=(t // tm,),
        in_specs=[pl.BlockSpec((tm, SHIFT_COLS), lambda i: (i, 0)),
                  pl.BlockSpec((8, SHIFT_COLS), lambda i: (jnp.maximum(i * (tm // 8) - 1, 0), 0))] + _prep_specs(tm),
        out_specs=[row] * 7,
        out_shape=[jax.ShapeDtypeStruct((t, RW), F32)] * 7,
        compiler_params=_params(("parallel",)),
    )(proj, proj, *pw)


def _pairs(ref):
    return jnp.stack([ref[:, 128 * p:128 * (p + 1)] for p in range(N_PAIR)], axis=0)


def _wkv_fwd(r, lw, k2, v, kk, a):
    t = r.shape[0]
    nc = t // CHUNK

    def body(r_ref, lw_ref, k_ref, v_ref, kk_ref, a_ref, y_ref, s_ref, st):
        @pl.when(pl.program_id(0) == 0)
        def _():
            st[...] = jnp.zeros_like(st)

        s0 = st[...]
        s_ref[0] = s0
        y, s1 = _wkv_chunk_fn(s0, *[_pairs(ref) for ref in (r_ref, lw_ref, k_ref, v_ref, kk_ref, a_ref)])
        for p in range(N_PAIR):
            y_ref[:, 128 * p:128 * (p + 1)] = y[p]
        st[...] = s1

    blk = pl.BlockSpec((CHUNK, RW), lambda c: (c, 0))
    return pl.pallas_call(
        body, name="wkv_fwd", grid=(nc,),
        in_specs=[blk] * 6,
        out_specs=[blk, pl.BlockSpec((1, N_PAIR, 128, 128), lambda c: (c, 0, 0, 0))],
        out_shape=[jax.ShapeDtypeStruct((t, RW), F32), jax.ShapeDtypeStruct((nc, N_PAIR, 128, 128), F32)],
        scratch_shapes=[pltpu.VMEM((N_PAIR, 128, 128), F32)],
        compiler_params=_params(("arbitrary",)),
    )(r, lw, k2, v, kk, a)


_POST_TM = 256


def _post_fwd(y, r, k2, v, g, lnw, lnb, rk):
    t = y.shape[0]
    tm = _POST_TM

    def body(y_ref, r_ref, k_ref, v_ref, g_ref, lnw_ref, lnb_ref, rk_ref, o_ref):
        o_ref[...] = _post_fn(y_ref[...], r_ref[...], k_ref[...], v_ref[...], g_ref[...],
                              lnw_ref[...], lnb_ref[...], rk_ref[...]).astype(BF16)

    row = pl.BlockSpec((tm, RW), lambda i: (i, 0))
    vec = pl.BlockSpec((1, RW), lambda i: (0, 0))
    return pl.pallas_call(
        body, name="rwkv_post", grid=(t // tm,),
        in_specs=[row] * 5 + [vec] * 3, out_specs=row,
        out_shape=jax.ShapeDtypeStruct((t, RW), BF16),
        compiler_params=_params(("parallel",)),
    )(y, r, k2, v, g, lnw, lnb, rk)


ATTN_GROUP = 2


def _dilated_rows(d, r, n):
    if d == 1:
        return pl.ds(pl.multiple_of(n * ATTN_BLOCK, ATTN_BLOCK), ATTN_BLOCK)
    return pl.ds(r + n * (ATTN_BLOCK * d), ATTN_BLOCK, stride=d)


def _for_each_sequence(t, unit):
    for di, d in enumerate(DILATIONS):

        @pl.when(pl.program_id(1) == di)
        def _(di=di, d=d):
            nb = t // (ATTN_BLOCK * d)
            if d == 1:
                unit(di, [(d, 0, 0)], False)
                unit(di, [(d, 0, 1)], True)
                lax.fori_loop(1, nb // 2, lambda k, c: (unit(di, [(d, 0, 2 * k), (d, 0, 2 * k + 1)], True), c)[1], 0)
            else:

                def residues(r, carry):
                    unit(di, [(d, r, 0), (d, r + d // 2, 0)], False)
                    if nb > 1:
                        lax.fori_loop(1, nb, lambda n, c: (unit(di, [(d, r, n), (d, r + d // 2, n)], True), c)[1], 0)
                    return carry

                lax.fori_loop(0, d // 2, residues, 0)


def _take(ref, lead, rows_list):
    return jnp.stack([ref.at[(*lead, g)][rows, :] for rows in rows_list for g in range(ATTN_GROUP)], axis=0)


def _put(ref, lead, rows_list, val, add=False):
    k = 0
    for rows in rows_list:
        for g in range(ATTN_GROUP):
            if add:
                ref.at[(*lead, g)][rows, :] += val[k]
            else:
                ref.at[(*lead, g)][rows, :] = val[k]
            k += 1


def _attn_fwd(qkv):
    t = qkv.shape[2]

    def body(q_ref, k_ref, v_ref, o_ref, l_ref):
        def unit(di, places, has_prev):
            cur = [_dilated_rows(d, r, n) for d, r, n in places]
            args = [_take(ref, (0,), cur) for ref in (q_ref, k_ref, v_ref)]
            if has_prev:
                prv = [_dilated_rows(d, r, n - 1) for d, r, n in places]
                args += [_take(ref, (0,), prv) for ref in (k_ref, v_ref)]
            o, lse = _attn_block_fn(*args)
            _put(o_ref, (0,), cur, o)
            _put(l_ref, (0,), cur, lse)

        _for_each_sequence(t, unit)

    spec = lambda j: pl.BlockSpec((1, ATTN_GROUP, t, 128), lambda i, b: (j, i, 0, 0))
    out = pl.BlockSpec((1, ATTN_GROUP, t, 128), lambda i, b: (b, i, 0, 0))
    return pl.pallas_call(
        body, name="attn_fwd", grid=(N_PAIR // ATTN_GROUP, len(DILATIONS)),
        in_specs=[spec(0), spec(1), spec(2)], out_specs=[out, out],
        out_shape=[jax.ShapeDtypeStruct((3, N_PAIR, t, 128), F32)] * 2,
        compiler_params=_params(("parallel", "arbitrary")),
    )(qkv, qkv, qkv)


_COMB_TM = 256


def _combine_fwd(o, l, og):
    t = o.shape[2]
    tm = _COMB_TM

    def body(o_ref, l_ref, og_ref, y_ref):
        for p in range(N_PAIR):
            cols = slice(128 * p, 128 * (p + 1))
            y_ref[:, cols] = _combine_fn(o_ref[0, p], o_ref[1, p], o_ref[2, p], l_ref[0, p], l_ref[1, p], l_ref[2, p],
                                         og_ref[:, cols]).astype(BF16)

    blk = pl.BlockSpec((3, N_PAIR, tm, 128), lambda i: (0, 0, i, 0))
    return pl.pallas_call(
        body, name="attn_combine", grid=(t // tm,),
        in_specs=[blk, blk, pl.BlockSpec((1, RW), lambda i: (0, 0))], out_specs=pl.BlockSpec((tm, RW), lambda i: (i, 0)),
        out_shape=jax.ShapeDtypeStruct((t, RW), BF16),
        compiler_params=_params(("parallel",)),
    )(o, l, og)


def _out_proj(x, ycat, wout, g2):
    t = x.shape[0]
    tm = 256

    def body(x_ref, y_ref, w_ref, g_ref, x1_ref, h_ref):
        x1 = x_ref[...] + _dot(y_ref[...], w_ref[...])
        x1_ref[...] = x1
        h_ref[...] = _rms_fwd(x1, g_ref[...]).astype(BF16)

    row = pl.BlockSpec((tm, D_MODEL), lambda i: (i, 0))
    return pl.pallas_call(
        body, name="out_proj", grid=(t // tm,),
        in_specs=[row, row, pl.BlockSpec((D_MODEL, D_MODEL), lambda i: (0, 0)), pl.BlockSpec((1, D_MODEL), lambda i: (0, 0))],
        out_specs=[row, row],
        out_shape=[jax.ShapeDtypeStruct((t, D_MODEL), F32), jax.ShapeDtypeStruct((t, D_MODEL), BF16)],
        compiler_params=_params(("parallel",)),
    )(x, ycat, wout, g2)


def _ffn_up(h2, wg, wu):
    t = h2.shape[0]
    tm = 512

    def body(h_ref, wg_ref, wu_ref, gt_ref, up_ref, act_ref):
        h = h_ref[...]
        gt = _dot_nt(h, wg_ref[...])
        up = _dot_nt(h, wu_ref[...])
        gt_ref[...] = gt.astype(BF16)
        up_ref[...] = up.astype(BF16)
        act_ref[...] = (gt * _sigmoid(gt) * up).astype(BF16)

    wide = pl.BlockSpec((tm, D_FF), lambda i: (i, 0))
    wsp = pl.BlockSpec((D_FF, D_MODEL), lambda i: (0, 0))
    return pl.pallas_call(
        body, name="ffn_up", grid=(t // tm,),
        in_specs=[pl.BlockSpec((tm, D_MODEL), lambda i: (i, 0)), wsp, wsp],
        out_specs=[wide, wide, wide],
        out_shape=[jax.ShapeDtypeStruct((t, D_FF), BF16)] * 3,
        compiler_params=_params(("parallel",)),
    )(h2, wg, wu)


def _ffn_down_loss(x1, act, wd, gf, tgt):
    t = x1.shape[0]
    tm = 256

    def body(x1_ref, a_ref, w_ref, g_ref, t_ref, dx_ref, dxb_ref, loss_ref, dg_ref):
        first = pl.program_id(0) == 0
        x2 = x1_ref[...] + _dot(a_ref[...], w_ref[...])
        g = g_ref[...]
        diff = _rms_fwd(x2, g) - t_ref[...]
        lrow = 0.5 * jnp.sum(_colsum8(diff * diff), axis=1, keepdims=True) * (1.0 / D_MODEL)
        _acc(loss_ref, jnp.broadcast_to(lrow, (8, 128)), first)
        dx2, dgr = _rms_bwd(diff * (1.0 / D_MODEL), x2, g)
        dx_ref[...] = dx2
        dxb_ref[...] = dx2.astype(BF16)
        _acc(dg_ref, _colsum8(dgr), first)

    row = pl.BlockSpec((tm, D_MODEL), lambda i: (i, 0))
    return pl.pallas_call(
        body, name="ffn_down_loss", grid=(t // tm,),
        in_specs=[row, pl.BlockSpec((tm, D_FF), lambda i: (i, 0)), pl.BlockSpec((D_FF, D_MODEL), lambda i: (0, 0)),
                  pl.BlockSpec((1, D_MODEL), lambda i: (0, 0)), row],
        out_specs=[row, row, pl.BlockSpec((8, 128), lambda i: (0, 0)), pl.BlockSpec((8, D_MODEL), lambda i: (0, 0))],
        out_shape=[jax.ShapeDtypeStruct((t, D_MODEL), F32), jax.ShapeDtypeStruct((t, D_MODEL), BF16),
                   jax.ShapeDtypeStruct((8, 128), F32), jax.ShapeDtypeStruct((8, D_MODEL), F32)],
        compiler_params=_params(("arbitrary",)),
    )(x1, act, wd, gf, tgt)


def _ffn_bwd_act(dx2b, wd, gt, up):
    t = dx2b.shape[0]
    tm = 512

    def body(dx_ref, w_ref, gt_ref, up_ref, dgt_ref, dup_ref):
        dact = _dot_nt(dx_ref[...], w_ref[...])
        gt = gt_ref[...].astype(F32)
        sg = _sigmoid(gt)
        dgt_ref[...] = (dact * up_ref[...].astype(F32) * sg * (1.0 + gt * (1.0 - sg))).astype(BF16)
        dup_ref[...] = (dact * gt * sg).astype(BF16)

    wide = pl.BlockSpec((tm, D_FF), lambda i: (i, 0))
    return pl.pallas_call(
        body, name="ffn_bwd_act", grid=(t // tm,),
        in_specs=[pl.BlockSpec((tm, D_MODEL), lambda i: (i, 0)), pl.BlockSpec((D_FF, D_MODEL), lambda i: (0, 0)), wide, wide],
        out_specs=[wide, wide],
        out_shape=[jax.ShapeDtypeStruct((t, D_FF), BF16)] * 2,
        compiler_params=_params(("parallel",)),
    )(dx2b, wd, gt, up)


def _ffn_bwd_h(dgt, dup, wg, wu, dx2, x1, g2, wout):
    t = dgt.shape[0]
    tm = 256

    def body(dgt_ref, dup_ref, wg_ref, wu_ref, dx2_ref, x1_ref, g_ref, wo_ref, dx1_ref, dx1b_ref, dya_ref, dyb_ref, dg_ref):
        dh = _dot(dgt_ref[...], wg_ref[...]) + _dot(dup_ref[...], wu_ref[...])
        dxn, dgr = _rms_bwd(dh, x1_ref[...], g_ref[...])
        dx1 = dx2_ref[...] + dxn
        dx1_ref[...] = dx1
        dx1b = dx1.astype(BF16)
        dx1b_ref[...] = dx1b
        dy = _dot_nt(dx1b, wo_ref[...])
        dya_ref[...] = dy[:, :RW]
        dyb_ref[...] = dy[:, RW:]
        _acc(dg_ref, _colsum8(dgr), pl.program_id(0) == 0)

    wide = pl.BlockSpec((tm, D_FF), lambda i: (i, 0))
    row = pl.BlockSpec((tm, D_MODEL), lambda i: (i, 0))
    half = pl.BlockSpec((tm, RW), lambda i: (i, 0))
    wsp = pl.BlockSpec((D_FF, D_MODEL), lambda i: (0, 0))
    return pl.pallas_call(
        body, name="ffn_bwd_h", grid=(t // tm,),
        in_specs=[wide, wide, wsp, wsp, row, row, pl.BlockSpec((1, D_MODEL), lambda i: (0, 0)),
                  pl.BlockSpec((D_MODEL, D_MODEL), lambda i: (0, 0))],
        out_specs=[row, row, half, half, pl.BlockSpec((8, D_MODEL), lambda i: (0, 0))],
        out_shape=[jax.ShapeDtypeStruct((t, D_MODEL), F32), jax.ShapeDtypeStruct((t, D_MODEL), BF16),
                   jax.ShapeDtypeStruct((t, RW), F32), jax.ShapeDtypeStruct((t, RW), F32),
                   jax.ShapeDtypeStruct((8, D_MODEL), F32)],
        compiler_params=_params(("arbitrary",)),
    )(dgt, dup, wg, wu, dx2, x1, g2, wout)


def _wgrad(a, b, tk, tn, name):
    t, kdim = a.shape
    ndim = b.shape[1]

    def body(a_ref, b_ref, o_ref):
        o_ref[...] = _dot_tn(a_ref[...], b_ref[...])

    return pl.pallas_call(
        body, name=name, grid=(kdim // tk, ndim // tn),
        in_specs=[pl.BlockSpec((t, tk), lambda i, j: (0, i)), pl.BlockSpec((t, tn), lambda i, j: (0, j))],
        out_specs=pl.BlockSpec((tk, tn), lambda i, j: (i, j)),
        out_shape=jax.ShapeDtypeStruct((kdim, ndim), F32),
        compiler_params=_params(("parallel", "parallel")),
    )(a, b)


def _post_bwd(dya, y, r, k2, v, g, lnw, lnb, rk):
    t = y.shape[0]
    tm = _POST_TM

    def body(d_ref, y_ref, r_ref, k_ref, v_ref, g_ref, lnw_ref, lnb_ref, rk_ref,
             dy_ref, dr_ref, dk_ref, dv_ref, dg_ref, dlnw_ref, dlnb_ref, drk_ref):
        first = pl.program_id(0) == 0
        ones = jnp.ones((tm, 1), F32)
        prim = (y_ref[...], r_ref[...], k_ref[...], v_ref[...], g_ref[...],
                ones * lnw_ref[...], ones * lnb_ref[...], ones * rk_ref[...])
        _, vjp = jax.vjp(_post_fn, *prim)
        dy, dr, dk, dv, dg, dlnw, dlnb, drk = vjp(d_ref[...])
        dy_ref[...] = dy
        dr_ref[...] = dr
        dk_ref[...] = dk
        dv_ref[...] = dv
        dg_ref[...] = dg
        _acc(dlnw_ref, _colsum8(dlnw), first)
        _acc(dlnb_ref, _colsum8(dlnb), first)
        _acc(drk_ref, _colsum8(drk), first)

    row = pl.BlockSpec((tm, RW), lambda i: (i, 0))
    vec = pl.BlockSpec((1, RW), lambda i: (0, 0))
    part = pl.BlockSpec((8, RW), lambda i: (0, 0))
    return pl.pallas_call(
        body, name="rwkv_post_bwd", grid=(t // tm,),
        in_specs=[row] * 6 + [vec] * 3, out_specs=[row] * 5 + [part] * 3,
        out_shape=[jax.ShapeDtypeStruct((t, RW), F32)] * 5 + [jax.ShapeDtypeStruct((8, RW), F32)] * 3,
        compiler_params=_params(("arbitrary",)),
    )(dya, y, r, k2, v, g, lnw, lnb, rk)


def _wkv_bwd(dy, s0s, r, lw, k2, v, kk, a):
    t = r.shape[0]
    nc = t // CHUNK

    def body(dy_ref, s_ref, r_ref, lw_ref, k_ref, v_ref, kk_ref, a_ref,
             dr_ref, dlw_ref, dk_ref, dv_ref, dkk_ref, da_ref, ds):
        @pl.when(pl.program_id(0) == 0)
        def _():
            ds[...] = jnp.zeros_like(ds)

        _, vjp = jax.vjp(_wkv_chunk_fn, s_ref[0],
                         *[_pairs(ref) for ref in (r_ref, lw_ref, k_ref, v_ref, kk_ref, a_ref)])
        res = vjp((_pairs(dy_ref), ds[...]))
        ds[...] = res[0]
        for ref, val in zip((dr_ref, dlw_ref, dk_ref, dv_ref, dkk_ref, da_ref), res[1:]):
            for p in range(N_PAIR):
                ref[:, 128 * p:128 * (p + 1)] = val[p]

    blk = pl.BlockSpec((CHUNK, RW), lambda c: (nc - 1 - c, 0))
    return pl.pallas_call(
        body, name="wkv_bwd", grid=(nc,),
        in_specs=[blk, pl.BlockSpec((1, N_PAIR, 128, 128), lambda c: (nc - 1 - c, 0, 0, 0))] + [blk] * 6,
        out_specs=[blk] * 6,
        out_shape=[jax.ShapeDtypeStruct((t, RW), F32)] * 6,
        scratch_shapes=[pltpu.VMEM((N_PAIR, 128, 128), F32)],
        compiler_params=_params(("arbitrary",)),
    )(dy, s0s, r, lw, k2, v, kk, a)


def _prep_bwd(proj, pw, douts):
    t = proj.shape[0]
    tm = _PREP_TM
    nt = t // tm

    def body(p_ref, l8_ref, mu, w0, w2p, a0, a2p, g2, k_k, k_a, dr, dr2, dlw, dk2, dk22, dv, dv2, dkk, da, dg,
             dp_ref, dmu_ref, dw0_ref, dw2_ref, da0_ref, da2_ref, dg2_ref, dkk_ref, dka_ref, carry):
        i = pl.program_id(0)
        first = i == 0

        @pl.when(first)
        def _():
            carry[...] = jnp.zeros_like(carry)

        p = p_ref[...]
        pprev = _shifted(p, l8_ref[...], i == nt - 1)
        ones = jnp.ones((tm, 1), F32)
        prim = (p, pprev, ones * mu[...], ones * w0[...], w2p[...], ones * a0[...], a2p[...], g2[...],
                ones * k_k[...], ones * k_a[...])
        _, vjp = jax.vjp(_prep_fn, *prim)
        dp, dpp, dmu, dw0, dw2, da0, da2, dg2, dkk_, dka = vjp(
            (dr[...] + dr2[...], dlw[...], dk2[...] + dk22[...], dv[...] + dv2[...], dkk[...], da[...], dg[...]))
        up = pltpu.roll(dpp, tm - 1, axis=0)
        rid = lax.broadcasted_iota(jnp.int32, dpp.shape, 0)
        dp_ref[...] = dp + jnp.where(rid == tm - 1, carry[0:1, :], up)
        carry[...] = jnp.broadcast_to(dpp[0:1, :], carry.shape)
        _acc(dmu_ref, _colsum8(dmu), first)
        _acc(dw0_ref, _colsum8(dw0), first)
        _acc(dw2_ref, dw2, first)
        _acc(da0_ref, _colsum8(da0), first)
        _acc(da2_ref, da2, first)
        _acc(dg2_ref, dg2, first)
        _acc(dkk_ref, _colsum8(dkk_), first)
        _acc(dka_ref, _colsum8(dka), first)

    rev = lambda i: (nt - 1 - i, 0)
    row = pl.BlockSpec((tm, RW), rev)
    part = lambda n: pl.BlockSpec((8, n), lambda i: (0, 0))
    mat = pl.BlockSpec((128, RW), lambda i: (0, 0))
    return pl.pallas_call(
        body, name="rwkv_prep_bwd", grid=(nt,),
        in_specs=[pl.BlockSpec((tm, SHIFT_COLS), rev),
                  pl.BlockSpec((8, SHIFT_COLS), lambda i: (jnp.maximum((nt - 1 - i) * (tm // 8) - 1, 0), 0))]
                 + _prep_specs(tm) + [row] * 10,
        out_specs=[pl.BlockSpec((tm, SHIFT_COLS), rev), part(SHIFT_COLS), part(RW), mat, part(RW), mat, mat,
                   part(RW), part(RW)],
        out_shape=[jax.ShapeDtypeStruct((t, SHIFT_COLS), F32), jax.ShapeDtypeStruct((8, SHIFT_COLS), F32),
                   jax.ShapeDtypeStruct((8, RW), F32), jax.ShapeDtypeStruct((128, RW), F32),
                   jax.ShapeDtypeStruct((8, RW), F32), jax.ShapeDtypeStruct((128, RW), F32),
                   jax.ShapeDtypeStruct((128, RW), F32), jax.ShapeDtypeStruct((8, RW), F32),
                   jax.ShapeDtypeStruct((8, RW), F32)],
        scratch_shapes=[pltpu.VMEM((8, SHIFT_COLS), F32)],
        compiler_params=_params(("arbitrary",)),
    )(proj, proj, *pw, *douts)


def _combine_bwd(dyb, o, l, og):
    t = dyb.shape[0]
    tm = _COMB_TM

    def body(d_ref, o_ref, l_ref, og_ref, do_ref, dl_ref, dog_ref):
        ones = jnp.ones((tm, 1), F32)
        dog = []
        for p in range(N_PAIR):
            cols = slice(128 * p, 128 * (p + 1))
            _, vjp = jax.vjp(_combine_fn, o_ref[0, p], o_ref[1, p], o_ref[2, p], l_ref[0, p], l_ref[1, p], l_ref[2, p],
                             ones * og_ref[:, cols])
            res = vjp(d_ref[:, cols])
            for b in range(3):
                do_ref[b, p] = res[b]
                dl_ref[b, p] = res[3 + b]
            dog.append(_colsum8(res[6]))
        _acc(dog_ref, jnp.concatenate(dog, axis=1), pl.program_id(0) == 0)

    blk = pl.BlockSpec((3, N_PAIR, tm, 128), lambda i: (0, 0, i, 0))
    return pl.pallas_call(
        body, name="attn_combine_bwd", grid=(t // tm,),
        in_specs=[pl.BlockSpec((tm, RW), lambda i: (i, 0)), blk, blk, pl.BlockSpec((1, RW), lambda i: (0, 0))],
        out_specs=[blk, blk, pl.BlockSpec((8, RW), lambda i: (0, 0))],
        out_shape=[jax.ShapeDtypeStruct((3, N_PAIR, t, 128), F32)] * 2 + [jax.ShapeDtypeStruct((8, RW), F32)],
        compiler_params=_params(("arbitrary",)),
    )(dyb, o, l, og)


def _attn_bwd(do, dl, o, lse, qkv):
    t = qkv.shape[2]

    def body(do_ref, dl_ref, o_ref, l_ref, q_ref, k_ref, v_ref, dq_ref, dk_ref, dv_ref):
        @pl.when(pl.program_id(1) == 0)
        def _():
            for ref in (dq_ref, dk_ref, dv_ref):
                ref[...] = jnp.zeros_like(ref)

        def unit(di, places, has_prev):
            cur = [_dilated_rows(d, r, n) for d, r, n in places]
            q, kc, vc = [_take(ref, (0,), cur) for ref in (q_ref, k_ref, v_ref)]
            kp = vp = None
            if has_prev:
                prv = [_dilated_rows(d, r, n - 1) for d, r, n in places]
                kp, vp = [_take(ref, (0,), prv) for ref in (k_ref, v_ref)]
            res = _attn_block_bwd(q, kc, vc, kp, vp, *[_take(ref, (0,), cur) for ref in (o_ref, l_ref, do_ref, dl_ref)])
            _put(dq_ref, (), cur, res[0], add=True)
            _put(dk_ref, (), cur, res[1], add=True)
            _put(dv_ref, (), cur, res[2], add=True)
            if has_prev:
                _put(dk_ref, (), prv, res[3], add=True)
                _put(dv_ref, (), prv, res[4], add=True)

        _for_each_sequence(t, unit)

    spec = lambda j: pl.BlockSpec((1, ATTN_GROUP, t, 128), lambda i, b: (j, i, 0, 0))
    branch = pl.BlockSpec((1, ATTN_GROUP, t, 128), lambda i, b: (b, i, 0, 0))
    out = pl.BlockSpec((ATTN_GROUP, t, 128), lambda i, b: (i, 0, 0))
    return pl.pallas_call(
        body, name="attn_bwd", grid=(N_PAIR // ATTN_GROUP, len(DILATIONS)),
        in_specs=[branch] * 4 + [spec(0), spec(1), spec(2)], out_specs=[out] * 3,
        out_shape=[jax.ShapeDtypeStruct((N_PAIR, t, 128), F32)] * 3,
        compiler_params=_params(("parallel", "arbitrary")),
    )(do, dl, o, lse, qkv, qkv, qkv)


def _in_proj_bwd(dpa, dq, dk, dv, win, x, g1, dx1):
    t = x.shape[0]
    tm = 256

    def body(dpa_ref, dq_ref, dk_ref, dv_ref, w_ref, x_ref, g_ref, dx1_ref, dproj_ref, dx_ref, dg_ref):
        parts = [dpa_ref[...]] + [ref[p] for ref in (dq_ref, dk_ref, dv_ref) for p in range(N_PAIR)]
        dproj = jnp.concatenate([z.astype(BF16) for z in parts], axis=1)
        dproj_ref[...] = dproj
        dh = _dot(dproj, w_ref[...])
        dxn, dgr = _rms_bwd(dh, x_ref[...], g_ref[...])
        dx_ref[...] = dx1_ref[...] + dxn
        _acc(dg_ref, _colsum8(dgr), pl.program_id(0) == 0)

    row = pl.BlockSpec((tm, D_MODEL), lambda i: (i, 0))
    pair = pl.BlockSpec((N_PAIR, tm, 128), lambda i: (0, i, 0))
    return pl.pallas_call(
        body, name="in_proj_bwd", grid=(t // tm,),
        in_specs=[pl.BlockSpec((tm, SHIFT_COLS), lambda i: (i, 0))] + [pair] * 3
                 + [pl.BlockSpec((IN_COLS, D_MODEL), lambda i: (0, 0)), row, pl.BlockSpec((1, D_MODEL), lambda i: (0, 0)), row],
        out_specs=[pl.BlockSpec((tm, IN_COLS), lambda i: (i, 0)), row, pl.BlockSpec((8, D_MODEL), lambda i: (0, 0))],
        out_shape=[jax.ShapeDtypeStruct((t, IN_COLS), BF16), jax.ShapeDtypeStruct((t, D_MODEL), F32),
                   jax.ShapeDtypeStruct((8, D_MODEL), F32)],
        compiler_params=_params(("arbitrary",)),
    )(dpa, dq, dk, dv, win, x, g1, dx1)


def _pad_lora(w, lo):
    z = jnp.zeros((64, RW), F32)
    return jnp.concatenate([w, z], axis=0) if lo == 0 else jnp.concatenate([z, w], axis=0)


def _local_step(x, tgt, win, vecs, w2, a2, g2m, get_rest, send_rest):
    pw = (vecs["mu_shift"], vecs["decay_w0"], _pad_lora(w2, 0), vecs["iclr_a0"], _pad_lora(a2, 64), g2m,
          vecs["k_k"], vecs["k_a"])
    h, proj, qkv = _in_proj(x, vecs["mix_norm_g"], win)
    r, lw, k2, v, kk, a, g = _prep_fwd(proj, pw)
    y, s0s = _wkv_fwd(r, lw, k2, v, kk, a)
    ya = _post_fwd(y, r, k2, v, g, vecs["ln_x_w"], vecs["ln_x_b"], vecs["r_k"])
    o_att, l_att = _attn_fwd(qkv)
    yb = _combine_fwd(o_att, l_att, vecs["attn_out_g"])

    wout, wg, wu, wd = get_rest(yb)
    ycat = jnp.concatenate([ya, yb], axis=1)
    x1, h2 = _out_proj(x, ycat, wout, vecs["ffn_norm_g"])
    gt, up, act = _ffn_up(h2, wg, wu)
    dx2, dx2b, loss8, dgf = _ffn_down_loss(x1, act, wd, vecs["final_norm_g"], tgt)

    dgt, dup = _ffn_bwd_act(dx2b, wd, gt, up)
    dx1, dx1b, dya, dyb, dg2n = _ffn_bwd_h(dgt, dup, wg, wu, dx2, x1, vecs["ffn_norm_g"], wout)
    gw = {
        "w_down": _wgrad(act, dx2b, 1408, 1024, "wgrad_down"),
        "w_gate": _wgrad(dgt, h2, 1408, 1024, "wgrad_gate"),
        "w_up": _wgrad(dup, h2, 1408, 1024, "wgrad_up"),
        "w_out": _wgrad(ycat, dx1b, 1024, 1024, "wgrad_out"),
    }

    lnw = vecs["ln_x_w"] + send_rest(gw)[0, 0]
    dy, dr_p, dk2_p, dv_p, dg, dlnw, dlnb, drk = _post_bwd(dya, y, r, k2, v, g, lnw, vecs["ln_x_b"], vecs["r_k"])
    dr_s, dlw, dk2_s, dv_s, dkk, da = _wkv_bwd(dy, s0s, r, lw, k2, v, kk, a)
    dpa, dmu, dw0, dw2p, da0, da2p, dg2m, dk_k, dk_a = _prep_bwd(
        proj, pw, (dr_p, dr_s, dlw, dk2_p, dk2_s, dv_p, dv_s, dkk, da, dg))

    do_att, dl_att, dog = _combine_bwd(dyb, o_att, l_att, vecs["attn_out_g"])
    dq, dk, dv = _attn_bwd(do_att, dl_att, o_att, l_att, qkv)
    dproj, dx, dg1 = _in_proj_bwd(dpa, dq, dk, dv, win, x, vecs["mix_norm_g"], dx1)
    gw["w_in"] = _wgrad(dproj, h, 1664, 1024, "wgrad_in")
    gw["decay_w2"] = dw2p[:64]
    gw["iclr_a2"] = da2p[64:]
    gw["gate_g2"] = dg2m
    gv = {"mix_norm_g": dg1, "mu_shift": dmu, "decay_w0": dw0, "iclr_a0": da0, "k_k": dk_k, "k_a": dk_a, "r_k": drk,
          "ln_x_w": dlnw, "ln_x_b": dlnb, "attn_out_g": dog, "ffn_norm_g": dg2n, "final_norm_g": dgf}
    return loss8, dx, gw, gv


N_CHIP = 4
N_DEV = 8
MATS = ("w_in", "w_out", "w_gate", "w_up", "w_down")
LORAS = ("decay_w2", "iclr_a2", "gate_g2")
VECS = (("mix_norm_g", 1024), ("mu_shift", 1792), ("decay_w0", 512), ("iclr_a0", 512), ("k_k", 512), ("k_a", 512),
        ("r_k", 512), ("ln_x_w", 512), ("ln_x_b", 512), ("attn_out_g", 512), ("ffn_norm_g", 1024),
        ("final_norm_g", 1024))
N_VEC = sum(n for _, n in VECS)
N_SMALL = N_VEC + 128
ANY = pl.BlockSpec(memory_space=pl.ANY)


def _flip(v, f):
    return 1 - v if f else v


class _Me:
    def __init__(self, mode):
        x, y, c = lax.axis_index("x"), lax.axis_index("y"), lax.axis_index("c")
        self.core, self.chip, self.dev = c, 2 * x + y, 4 * x + 2 * y + c
        self.sibling = (x, y, 1 - c)
        if mode == "chips":
            self.peers = [(px, py, c) for px, py in ((1 - x, y), (x, 1 - y), (1 - x, 1 - y))]
        else:
            self.peers = [(_flip(x, k & 4), _flip(y, k & 2), _flip(c, k & 1)) for k in range(1, N_DEV)]


def _half(core, rows):
    h = rows // 2
    return pl.ds(pl.multiple_of(core * h, h), h)


def _peer_copy(srcs, dsts, kinds, send_sems, recv_sems, me, j, i, incoming):
    px, py, pc = me.peers[j]
    pchip, pdev = 2 * px + py, 4 * px + 2 * py + pc
    src, dst, kind = srcs[i], dsts[i], kinds[i]
    if kind == "gather":
        rows = _half(me.core, src.shape[0])
        src, dst = src.at[rows], dst.at[pchip if incoming else me.chip, rows]
    elif kind == "scatter":
        src, dst = src.at[pchip, _half(pc, src.shape[1])], dst.at[pdev if incoming else me.dev]
    else:
        dst = dst.at[pdev if incoming else me.dev]
    n = len(srcs)
    return pltpu.make_async_remote_copy(src_ref=src, dst_ref=dst, send_sem=send_sems.at[n * j + i],
                                        recv_sem=recv_sems.at[n * j + i], device_id=(px, py, pc), device_id_type=MESH)


def _mode(kinds):
    return "chips" if kinds[0] == "gather" else "devs"


def _npeer(kinds):
    return N_CHIP - 1 if kinds[0] == "gather" else N_DEV - 1


def _swap_gathered(lands, name):
    n = len(lands)

    def body(*refs):
        dsts, send_sems, recv_sems = refs[n:2 * n], refs[2 * n], refs[2 * n + 1]
        me = _Me("chips")

        def copy(j, i, incoming):
            px, py, _ = me.peers[j]
            rows_out, rows_in = _half(me.core, dsts[i].shape[1]), _half(1 - me.core, dsts[i].shape[1])
            return pltpu.make_async_remote_copy(
                src_ref=dsts[i].at[2 * px + py, rows_out], dst_ref=dsts[i].at[2 * px + py, rows_in if incoming else rows_out],
                send_sem=send_sems.at[n * j + i], recv_sem=recv_sems.at[n * j + i], device_id=me.sibling, device_id_type=MESH)

        sends = [copy(j, i, False) for j in range(3) for i in range(n)]
        for cp in sends:
            cp.start()
        for j in range(3):
            for i in range(n):
                copy(j, i, True).wait_recv()
        for cp in sends:
            cp.wait_send()

    return pl.pallas_call(
        body, name=name, in_specs=[ANY] * n, out_specs=[ANY] * n,
        out_shape=[jax.ShapeDtypeStruct(l.shape, l.dtype) for l in lands],
        input_output_aliases={i: i for i in range(n)},
        scratch_shapes=[pltpu.SemaphoreType.DMA((3 * n,)), pltpu.SemaphoreType.DMA((3 * n,))],
    )(*lands)


def _join_halves(sums, name):
    n = len(sums)

    def body(*refs):
        dsts, send_sems, recv_sems = refs[n:2 * n], refs[2 * n], refs[2 * n + 1]
        me = _Me("chips")

        def copy(i, incoming):
            mine, other = _half(me.core, dsts[i].shape[0]), _half(1 - me.core, dsts[i].shape[0])
            return pltpu.make_async_remote_copy(src_ref=dsts[i].at[mine], dst_ref=dsts[i].at[other if incoming else mine],
                                                send_sem=send_sems.at[i], recv_sem=recv_sems.at[i],
                                                device_id=me.sibling, device_id_type=MESH)

        sends = [copy(i, False) for i in range(n)]
        for cp in sends:
            cp.start()
        for i in range(n):
            copy(i, True).wait_recv()
        for cp in sends:
            cp.wait_send()

    return pl.pallas_call(
        body, name=name, in_specs=[ANY] * n, out_specs=[ANY] * n,
        out_shape=[jax.ShapeDtypeStruct(s.shape, s.dtype) for s in sums],
        input_output_aliases={i: i for i in range(n)},
        scratch_shapes=[pltpu.SemaphoreType.DMA((n,)), pltpu.SemaphoreType.DMA((n,))],
    )(*sums)


HBM = pl.BlockSpec(memory_space=pltpu.HBM)
SEM = pl.BlockSpec(memory_space=pltpu.SEMAPHORE)
EFFECT = pltpu.SideEffectType.DATAFLOW_SIDE_EFFECTING


def _swap_start(arrs, lands, kinds, name):
    n = len(arrs)

    def body(*refs):
        srcs, dsts, send_sems, recv_sems, token = refs[:n], refs[n:2 * n], refs[2 * n], refs[2 * n + 1], refs[-1]
        me = _Me(_mode(kinds))
        for j in range(len(me.peers)):
            for i in range(n):
                _peer_copy(srcs, dsts, kinds, send_sems, recv_sems, me, j, i, False).start()
        token[...] = jnp.zeros_like(token)

    ns = _npeer(kinds) * n
    outs = pl.pallas_call(
        body, name=name,
        out_shape=(pltpu.SemaphoreType.DMA((ns,)), pltpu.SemaphoreType.DMA((ns,)),
                   *[pltpu.HBM(a.shape, a.dtype) for a in arrs], *[pltpu.HBM(l.shape, l.dtype) for l in lands],
                   jax.ShapeDtypeStruct((8, 128), F32)),
        in_specs=[HBM] * (2 * n), out_specs=(SEM, SEM, *[HBM] * (2 * n), pl.BlockSpec(memory_space=pltpu.VMEM)),
        input_output_aliases={k: 2 + k for k in range(2 * n)},
        compiler_params=pltpu.CompilerParams(has_side_effects=EFFECT),
    )(*[pltpu.with_memory_space_constraint(a, pltpu.HBM) for a in arrs],
      *[pltpu.with_memory_space_constraint(l, pltpu.HBM) for l in lands])
    return outs[0], outs[1], outs[2:2 + n], outs[2 + n:2 + 2 * n], outs[-1]


def _swap_wait(send_sems, recv_sems, srcs_thru, lands_thru, after, kinds, name):
    n = len(srcs_thru)

    def body(*refs):
        srcs, dsts, s_sems, r_sems = refs[:n], refs[n:2 * n], refs[2 * n], refs[2 * n + 1]
        me = _Me(_mode(kinds))
        for j in range(len(me.peers)):
            for i in range(n):
                cp = _peer_copy(srcs, dsts, kinds, s_sems, r_sems, me, j, i, True)
                cp.wait_send()
                cp.wait_recv()

    outs = pl.pallas_call(
        body, name=name,
        out_shape=tuple(pltpu.HBM(a.shape, a.dtype) for a in (*srcs_thru, *lands_thru)),
        in_specs=[HBM] * (2 * n) + [SEM, SEM, ANY], out_specs=tuple([HBM] * (2 * n)),
        input_output_aliases={k: k for k in range(2 * n)},
        compiler_params=pltpu.CompilerParams(has_side_effects=EFFECT),
    )(*srcs_thru, *lands_thru, send_sems, recv_sems, after)
    return outs[n:]


def _adamw(w, g, m, v):
    m = ADAM_B1 * m + (1.0 - ADAM_B1) * g
    v = ADAM_B2 * v + (1.0 - ADAM_B2) * (g * g)
    m_hat = m / (1.0 - ADAM_B1 ** ADAM_STEP)
    v_hat = v / (1.0 - ADAM_B2 ** ADAM_STEP)
    delta = -ADAM_LR * (m_hat / (jnp.sqrt(v_hat) + ADAM_EPS) + ADAM_WD * w)
    return delta, m, v


def _reduce8(rbuf, core, tr, name):
    _, h, cols = rbuf.shape

    def body(core_ref, r_ref, g_ref):
        g = r_ref[0].astype(F32)
        for s in range(1, N_DEV):
            g = g + r_ref[s].astype(F32)
        g_ref[...] = g

    return pl.pallas_call(
        body, name=name,
        grid_spec=pltpu.PrefetchScalarGridSpec(
            num_scalar_prefetch=1, grid=(h // tr,),
            in_specs=[pl.BlockSpec((N_DEV, tr, cols), lambda i, core_ref: (0, i, 0))],
            out_specs=pl.BlockSpec((tr, cols), lambda i, core_ref: (core_ref[0] * (h // tr) + i, 0))),
        out_shape=jax.ShapeDtypeStruct((2 * h, cols), F32),
        compiler_params=_params(("parallel",)),
    )(core, rbuf)


def _adamw_call(g, w, m, v, tr, name):
    _, rows, cols = w.shape

    def body(g_in, w_ref, m_ref, v_ref, g_ref, d_ref, nm_ref, nv_ref):
        g = g_in[...]
        g_ref[0] = g
        d_ref[0], nm_ref[0], nv_ref[0] = _adamw(w_ref[0], g, m_ref[0], v_ref[0])

    row = pl.BlockSpec((1, tr, cols), lambda i: (0, i, 0))
    return pl.pallas_call(
        body, name=name, grid=(rows // tr,),
        in_specs=[pl.BlockSpec((tr, cols), lambda i: (i, 0)), row, row, row], out_specs=[row] * 4,
        out_shape=[jax.ShapeDtypeStruct(w.shape, F32)] * 4,
        compiler_params=_params(("parallel",)),
    )(g, w, m, v)


def _rowsum_small(parts, loss8):
    def body(*refs):
        out = refs[-1]
        c0 = 0
        for ref in refs[:-1]:
            n = ref.shape[1]
            out[:, c0:c0 + n] = jnp.sum(ref[...], axis=0, keepdims=True)
            c0 += n

    return pl.pallas_call(body, name="rowsum_small", out_shape=jax.ShapeDtypeStruct((1, N_SMALL), F32))(*parts, loss8)


def _reduce_adamw_small(sbuf, ws, ms, vs):
    nv = len(ws)

    def body(*refs):
        s_ref, ins, outs = refs[0], refs[1:1 + 3 * nv], refs[1 + 3 * nv:]
        tot = s_ref[0]
        for s in range(1, N_DEV):
            tot = tot + s_ref[s]
        c0 = 0
        for i in range(nv):
            n = ins[i].shape[1]
            g = tot[:, c0:c0 + n]
            outs[i][...] = g
            outs[nv + i][...], outs[2 * nv + i][...], outs[3 * nv + i][...] = _adamw(
                ins[i][...], g, ins[nv + i][...], ins[2 * nv + i][...])
            c0 += n
        outs[-1][...] = tot[:, c0:]

    return pl.pallas_call(
        body, name="reduce_adamw_small",
        out_shape=[jax.ShapeDtypeStruct(a.shape, F32) for a in ws] * 4 + [jax.ShapeDtypeStruct((1, 128), F32)],
    )(sbuf, *ws, *ms, *vs)


_TRANSPOSED = ("w_in", "w_gate", "w_up")
_ROW_STACKED = MATS
_ADAM_TILE = {"w_in": 208, "w_out": 256, "w_gate": 176, "w_up": 176, "w_down": 176, "decay_w2": 64, "iclr_a2": 64,
              "gate_g2": 128}
_SUM_TILE = {"w_in": 208, "w_out": 128, "w_gate": 176, "w_up": 176, "w_down": 176, "decay_w2": 32, "iclr_a2": 32,
             "gate_g2": 64}


def _full(n, stacked):
    p, r, c = stacked.shape
    if n in _ROW_STACKED:
        return stacked.reshape(p * r, c)
    return jnp.transpose(stacked, (1, 0, 2)).reshape(r, p * c)


def _by_chip(n, full):
    if n in _ROW_STACKED:
        return full.reshape(N_CHIP, full.shape[0] // N_CHIP, full.shape[1])
    r, c = full.shape
    return jnp.transpose(full.reshape(r, N_CHIP, c // N_CHIP), (1, 0, 2))


def _with_own(land_shape, dtype, own, slot):
    return lax.dynamic_update_slice(lax.empty(land_shape, dtype), own[None], (slot,) + (0,) * own.ndim)


def kernel(x, mix_norm_g, w_in, mu_shift, decay_w0, decay_w2, iclr_a0, iclr_a2, gate_g2, k_k, k_a, r_k, ln_x_w, ln_x_b, attn_out_g, w_out, ffn_norm_g, w_gate, w_up, w_down, final_norm_g, loss_target, m_mix_norm_g, m_w_in, m_mu_shift, m_decay_w0, m_decay_w2, m_iclr_a0, m_iclr_a2, m_gate_g2, m_k_k, m_k_a, m_r_k, m_ln_x_w, m_ln_x_b, m_attn_out_g, m_w_out, m_ffn_norm_g, m_w_gate, m_w_up, m_w_down, m_final_norm_g, v_mix_norm_g, v_w_in, v_mu_shift, v_decay_w0, v_decay_w2, v_iclr_a0, v_iclr_a2, v_gate_g2, v_k_k, v_k_a, v_r_k, v_ln_x_w, v_ln_x_b, v_attn_out_g, v_w_out, v_ffn_norm_g, v_w_gate, v_w_up, v_w_down, v_final_norm_g):
    names = ("mix_norm_g", "w_in", "mu_shift", "decay_w0", "decay_w2", "iclr_a0", "iclr_a2", "gate_g2", "k_k", "k_a",
             "r_k", "ln_x_w", "ln_x_b", "attn_out_g", "w_out", "ffn_norm_g", "w_gate", "w_up", "w_down", "final_norm_g")
    w = dict(zip(names, (mix_norm_g, w_in, mu_shift, decay_w0, decay_w2, iclr_a0, iclr_a2, gate_g2, k_k, k_a, r_k,
                         ln_x_w, ln_x_b, attn_out_g, w_out, ffn_norm_g, w_gate, w_up, w_down, final_norm_g)))
    m = dict(zip(names, (m_mix_norm_g, m_w_in, m_mu_shift, m_decay_w0, m_decay_w2, m_iclr_a0, m_iclr_a2, m_gate_g2,
                         m_k_k, m_k_a, m_r_k, m_ln_x_w, m_ln_x_b, m_attn_out_g, m_w_out, m_ffn_norm_g, m_w_gate,
                         m_w_up, m_w_down, m_final_norm_g)))
    v = dict(zip(names, (v_mix_norm_g, v_w_in, v_mu_shift, v_decay_w0, v_decay_w2, v_iclr_a0, v_iclr_a2, v_gate_g2,
                         v_k_k, v_k_a, v_r_k, v_ln_x_w, v_ln_x_b, v_attn_out_g, v_w_out, v_ffn_norm_g, v_w_gate,
                         v_w_up, v_w_down, v_final_norm_g)))
    first = ("w_in",) + LORAS
    rest = ("w_out", "w_gate", "w_up", "w_down")
    xi, yi, ci = lax.axis_index("x"), lax.axis_index("y"), lax.axis_index("c")
    my_chip, my_dev = 2 * xi + yi, 4 * xi + 2 * yi + ci
    gather, scatter = ("gather",) * 4, ("scatter",) * 4

    sh = lambda z, n: jnp.transpose(z[0]) if n in _TRANSPOSED else z[0]
    mine = [sh(w["w_in"], "w_in").astype(BF16)] + [w[n][0] for n in LORAS]
    early = _swap_start(mine, [_with_own((N_CHIP,) + a.shape, a.dtype, a, my_chip) for a in mine], gather, "gather_first_start")
    wb = {n: (sh(w[n], n) + early[4][0, 0]).astype(BF16) for n in rest}
    lands = [_with_own((N_CHIP,) + wb[n].shape, BF16, wb[n], my_chip) for n in rest]
    ssem, rsem, srcs_thru, lands_thru, tok = _swap_start([wb[n] for n in rest], lands, gather, "gather_rest_start")
    got = _swap_wait(early[0], early[1], early[2], early[3], tok, gather, "gather_first_wait")
    win, w2, a2, g2m = (_full(n, z) for n, z in zip(first, _swap_gathered(got, "gather_first_halves")))

    vecs = {n: w[n].reshape(1, sz) for n, sz in VECS}
    vecs["mix_norm_g"] = vecs["mix_norm_g"] + tok[0, 0]

    def get_rest(after):
        halves = _swap_wait(ssem, rsem, srcs_thru, lands_thru, after, gather, "gather_rest_wait")
        return [_full(n, z) for n, z in zip(rest, _swap_gathered(halves, "gather_rest_halves"))]

    flight = []

    def my_half(g):
        h = g.shape[1] // 2
        return lax.dynamic_slice(g, (my_chip, ci * h, 0), (1, h, g.shape[2]))[0]

    def send_rest(gw):
        gs = [_by_chip(n, gw[n]).astype(BF16) for n in rest]
        into = [_with_own((N_DEV,) + my_half(g).shape, BF16, my_half(g), my_dev) for g in gs]
        flight.extend(_swap_start(gs, into, scatter, "exchange_rest_start"))
        return flight[4]

    loss8, dx, gw, gv = _local_step(x[0], loss_target[0], win, vecs, w2, a2, g2m, get_rest, send_rest)

    small = _rowsum_small([gv[n] for n, _ in VECS], loss8)
    gs = [_by_chip(n, gw[n]).astype(BF16) for n in first]
    into = [_with_own((N_DEV,) + my_half(g).shape, BF16, my_half(g), my_dev) for g in gs]
    into.append(_with_own((N_DEV,) + small.shape, F32, small, my_dev))
    last = _swap_start(gs + [small], into, scatter + ("all",), "exchange_first_start")

    core = jnp.reshape(ci, (1,)).astype(jnp.int32)

    def update(group, rbufs, tag):
        sums = [_reduce8(rb, core, _SUM_TILE[n], "reduce_" + n) for n, rb in zip(group, rbufs)]
        gsum = _join_halves(sums, "join_halves_" + tag)
        out = {}
        for n, g in zip(group, gsum):
            r = _adamw_call(g, sh(w[n], n)[None], sh(m[n], n)[None], sh(v[n], n)[None], _ADAM_TILE[n], "adamw_" + n)
            out[n] = [jnp.transpose(z[0])[None] for z in r] if n in _TRANSPOSED else r
        return out

    res = update(rest, _swap_wait(flight[0], flight[1], flight[2], flight[3], last[4], scatter, "exchange_rest_wait"), "rest")
    got = _swap_wait(last[0], last[1], last[2], last[3], res["w_down"][1], scatter + ("all",), "exchange_first_wait")
    res.update(update(first, got[:4], "first"))
    rows = lambda d: [d[n].reshape(1, sz) for n, sz in VECS]
    small_res = _reduce_adamw_small(got[4], rows(w), rows(m), rows(v))

    outs = []
    for k in range(4):
        piece = {n: r[k] for n, r in res.items()}
        for i, (n, _) in enumerate(VECS):
            piece[n] = small_res[k * len(VECS) + i].reshape(w[n].shape)
        outs.extend(piece[n] for n in names)
    return (small_res[-1][0, 0], dx[None], *outs)
```

```python
import jax
import jax.numpy as jnp
from jax import lax
from jax.experimental import pallas as pl
from jax.experimental.pallas import tpu as pltpu

F32 = jnp.float32
BF16 = jnp.bfloat16

D_MODEL = 1024
HEAD_DIM = 64
RW = 512
N_PAIR = RW // 128
SHIFT_COLS = 1792
IN_COLS = 3328
D_FF = 2816
FF_CHUNK = 256
NORM_EPS = 1e-6
GN_EPS = 64e-5
CHUNK = 64
SUB = 16
WKV_PASSES = 1
ATTN_PASSES = 1
ATTN_BLOCK = 128
DILATIONS = (1, 4, 16)
NEG = -1e30
ADAM_LR, ADAM_B1, ADAM_B2, ADAM_EPS, ADAM_WD, ADAM_STEP = 0.001, 0.9, 0.999, 1e-08, 0.01, 10
VMEM_LIMIT = 56 * 1024 * 1024
MESH = pl.DeviceIdType.MESH


def _params(sem=None, **kw):
    return pltpu.CompilerParams(dimension_semantics=sem, vmem_limit_bytes=VMEM_LIMIT, **kw)


def _dot(a, b, prec=None):
    return lax.dot_general(a, b, (((1,), (0,)), ((), ())), preferred_element_type=F32, precision=prec)


def _dot_nt(a, b, prec=None):
    return lax.dot_general(a, b, (((1,), (1,)), ((), ())), preferred_element_type=F32, precision=prec)


def _dot_tn(a, b, prec=None):
    return lax.dot_general(a, b, (((0,), (0,)), ((), ())), preferred_element_type=F32, precision=prec)


_FORMS = {"nn": ((1,), (0,)), "nt": ((1,), (1,)), "tn": ((0,), (0,))}


def _dg(a, b, form):
    if a.ndim == 3 or b.ndim == 3:
        nb = a.shape[0] if a.ndim == 3 else b.shape[0]
        return jnp.stack([_dg(a[i] if a.ndim == 3 else a, b[i] if b.ndim == 3 else b, form) for i in range(nb)], axis=0)
    return lax.dot_general(a, b, (_FORMS[form], ((), ())), preferred_element_type=F32)


def _split2(x):
    hi = x.astype(BF16)
    return hi, (x - hi.astype(F32)).astype(BF16)


def _split3(x):
    hi = x.astype(BF16)
    rest = x - hi.astype(F32)
    mid = rest.astype(BF16)
    return hi, mid, (rest - mid.astype(F32)).astype(BF16)


def _mm_raw(a, b, form, mode):
    if mode == 1:
        return _dg(a.astype(BF16), b.astype(BF16), form)
    if mode == 3:
        ah, al = _split2(a)
        bh, bl = _split2(b)
        return _dg(ah, bh, form) + (_dg(ah, bl, form) + _dg(al, bh, form))
    if mode == "L3":
        ab = a.astype(BF16)
        b1, b2, b3 = _split3(b)
        if form == "nn":
            n = b.shape[-1]
            wide = _dg(ab, jnp.concatenate([b1, b2, b3], axis=-1), form)
            return wide[..., :n] + (wide[..., n:2 * n] + wide[..., 2 * n:])
        return _dg(ab, b1, form) + (_dg(ab, b2, form) + _dg(ab, b3, form))
    assert mode == "R3", mode
    bb = b.astype(BF16)
    a1, a2, a3 = _split3(a)
    if form in ("nn", "nt"):
        m = a.shape[-2]
        tall = _dg(jnp.concatenate([a1, a2, a3], axis=-2), bb, form)
        return tall[..., :m, :] + (tall[..., m:2 * m, :] + tall[..., 2 * m:, :])
    return _dg(a1, bb, form) + (_dg(a2, bb, form) + _dg(a3, bb, form))


def _mm(a, b, form, mode):
    @jax.custom_vjp
    def f(a, b):
        return _mm_raw(a, b, form, mode)

    def fwd(a, b):
        return _mm_raw(a, b, form, mode), (a, b)

    def bwd(res, ct):
        a, b = res
        la = {1: 1, 3: 3, "L3": None, "R3": "R3"}[mode]
        lb = {1: 1, 3: 3, "L3": "L3", "R3": None}[mode]
        if form == "nn":
            da = None if la is None else _mm_raw(ct, b, "nt", la)
            db = None if lb is None else _mm_raw(a, ct, "tn", lb)
        elif form == "nt":
            da = None if la is None else _mm_raw(ct, b, "nn", la)
            db = None if lb is None else _mm_raw(ct, a, "tn", "R3" if lb == "L3" else lb)
        else:
            da = None if la is None else _mm_raw(b, ct, "nt", "L3" if la == "R3" else la)
            db = None if lb is None else _mm_raw(a, ct, "nn", lb)
        return (jnp.zeros_like(a) if da is None else da, jnp.zeros_like(b) if db is None else db)

    f.defvjp(fwd, bwd)
    return f(a, b)


def _seg_ones(n):
    r = lax.broadcasted_iota(jnp.int32, (n, n), 0) // HEAD_DIM
    c = lax.broadcasted_iota(jnp.int32, (n, n), 1) // HEAD_DIM
    return (r == c).astype(F32)


def _segsum(x, seg):
    return _mm(x, seg, "nn", "R3")


def _rms_fwd(x, g):
    rstd = lax.rsqrt(jnp.mean(x * x, axis=-1, keepdims=True) + NORM_EPS)
    return x * rstd * g


def _rms_bwd(dy, x, g):
    rstd = lax.rsqrt(jnp.mean(x * x, axis=-1, keepdims=True) + NORM_EPS)
    xn = x * rstd
    dxn = dy * g
    dx = rstd * (dxn - xn * jnp.mean(dxn * xn, axis=-1, keepdims=True))
    return dx, dy * xn


def _sigmoid(x):
    return 1.0 / (1.0 + jnp.exp(-x))


def _softplus(x):
    return jnp.maximum(x, 0.0) + jnp.log(1.0 + jnp.exp(-jnp.abs(x)))


def _acc(ref, val, first):
    @pl.when(first)
    def _():
        ref[...] = val

    @pl.when(jnp.logical_not(first))
    def _():
        ref[...] += val


def _colsum8(v):
    rows, n = v.shape
    return jnp.sum(v.reshape(rows // 8, 8, n), axis=0)


def _prep_fn(p, pprev, mu, w0, w2p, a0, a2p, g2, k_k, k_a):
    seg = _seg_ones(RW)
    ps = p + (pprev - p) * mu
    r = ps[:, 0:RW]
    k = ps[:, RW:2 * RW]
    v = ps[:, 2 * RW:3 * RW]
    xwa = ps[:, 3 * RW:3 * RW + 128]
    xg = ps[:, 3 * RW + 128:3 * RW + 256]
    wraw = -_softplus(-(w0 + _mm(jnp.tanh(xwa), w2p, "nn", 3))) - 0.5
    lw = -jnp.exp(wraw)
    a = _sigmoid(a0 + _mm(xwa, a2p, "nn", 3))
    g = _mm(_sigmoid(xg), g2, "nn", 3)
    kk = k * k_k
    kk = kk / jnp.maximum(jnp.sqrt(_segsum(kk * kk, seg)), 1e-12)
    k2 = k * (1.0 + (a - 1.0) * k_a)
    return r, lw, k2, v, kk, a, g


def _transposed(z):
    return jnp.stack([z[i].T for i in range(z.shape[0])], axis=0) if z.ndim == 3 else z.T


def _solve_unit_lower(lmat, rhs):
    c = lmat.shape[-1]
    row = lax.broadcasted_iota(jnp.int32, (c, c), 0)
    col = lax.broadcasted_iota(jnp.int32, (c, c), 1)
    eye = (row == col).astype(F32)
    ld = jnp.where(row // SUB == col // SUB, lmat, 0.0)
    lo = lmat - ld
    x = eye + ld
    m = ld
    mm = lambda p, q: _mm(p, q, "nn", WKV_PASSES)
    cat = jnp.concatenate
    m = mm(m, m)
    for _ in range(2):
        mx = mm(m, cat([m, x], axis=-1))
        m, x = mx[..., :c], x + mx[..., c:]
    x = x + mm(m, x)
    gw = mm(x, cat([lo, rhs], axis=-1))
    g, w = gw[..., :c], gw[..., c:]
    gg = mm(g, cat([g, w], axis=-1))
    w = w + gg[..., c:]
    return w + mm(gg[..., :c], w)


def _wkv_chunk_fn(s0, r, lw, k, v, kk, a):
    c = r.shape[-2]
    n = 2 * c
    row = lax.broadcasted_iota(jnp.int32, (n, n), 0)
    col = lax.broadcasted_iota(jnp.int32, (n, n), 1)
    same = (row // c) == (col // c)
    incl = jnp.logical_and(row >= col, same)
    strict = jnp.logical_and(row > col, same)
    sel = (lax.broadcasted_iota(jnp.int32, (n, 128), 0) // c) == (lax.broadcasted_iota(jnp.int32, (n, 128), 1) // HEAD_DIM)
    two = lambda z: jnp.concatenate([z, z], axis=-2)
    lw2 = two(lw)
    mm = lambda p_, q_, form: _mm(p_, q_, form, WKV_PASSES)
    cl = _mm(incl.astype(F32), lw2, "nn", "L3")
    p = jnp.exp(cl)
    pinv = jnp.exp(-cl)
    pprev = jnp.exp(cl - lw2)
    kk2 = two(kk)
    at = jnp.where(sel, -kk2 * pprev, 0.0)
    bt = jnp.where(sel, kk2 * two(a) * pinv, 0.0)
    kt = jnp.where(sel, two(k) * pinv, 0.0)
    rt = jnp.where(sel, two(r) * p, 0.0)
    vt = jnp.where(sel, two(v), 0.0)
    cat = jnp.concatenate
    bk = cat([bt, kt], axis=-2)
    arbk = mm(cat([at, rt], axis=-2), bk, "nt")
    ab, ak = jnp.where(strict, arbk[..., :n, :n], 0.0), jnp.where(strict, arbk[..., :n, n:], 0.0)
    rb, rk = jnp.where(incl, arbk[..., n:, :n], 0.0), jnp.where(incl, arbk[..., n:, n:], 0.0)
    s0t = _transposed(s0)
    u = _solve_unit_lower(ab, mm(cat([at, ak], axis=-1), cat([s0t, vt], axis=-2), "nn"))
    y2 = mm(cat([rt, rb, rk], axis=-1), cat([s0t, u, vt], axis=-2), "nn")
    plast = jnp.exp(jnp.sum(lw, axis=-2, keepdims=True))
    s1 = (s0 + mm(cat([u, vt], axis=-2), bk, "tn")) * plast
    r2 = lax.broadcasted_iota(jnp.int32, (128, 128), 0) // HEAD_DIM
    c2 = lax.broadcasted_iota(jnp.int32, (128, 128), 1) // HEAD_DIM
    return y2[..., :c, :] + y2[..., c:, :], jnp.where(r2 == c2, s1, 0.0)


def _post_fn(y, r, k2, v, g, lnw, lnb, rk):
    seg = _seg_ones(RW)
    mean = _segsum(y, seg) * (1.0 / HEAD_DIM)
    yc = y - mean
    var = _segsum(yc * yc, seg) * (1.0 / HEAD_DIM)
    yn = yc * lax.rsqrt(var + GN_EPS)
    out = yn * lnw + lnb + _segsum(r * k2 * rk, seg) * v
    return out * g


def _attn_block_fn(q, kc, vc, kp=None, vp=None):
    n = ATTN_BLOCK
    qi = lax.broadcasted_iota(jnp.int32, (n, n), 0)
    kj = lax.broadcasted_iota(jnp.int32, (n, n), 1)
    lane = lax.broadcasted_iota(jnp.int32, (1, 128), 1)
    scale = HEAD_DIM ** -0.5
    valid = kj <= qi
    keys, vals = kc, vc
    if kp is not None:
        valid = jnp.concatenate([valid, kj >= qi], axis=-1)
        keys, vals = jnp.concatenate([kc, kp], axis=-2), jnp.concatenate([vc, vp], axis=-2)
    m0 = (lane // HEAD_DIM) == 0
    q2 = jnp.concatenate([jnp.where(m0, q, 0.0), jnp.where(m0, 0.0, q)], axis=-2)
    valid2 = jnp.concatenate([valid, valid], axis=-2)
    s = jnp.where(valid2, _mm(q2, keys, "nt", ATTN_PASSES) * scale, NEG)
    m = jnp.max(s, axis=-1, keepdims=True)
    p = jnp.exp(s - m)
    den = jnp.sum(p, axis=-1, keepdims=True)
    o2 = _mm(p, vals, "nn", ATTN_PASSES) / den
    l2 = m + jnp.log(den)
    return jnp.where(m0, o2[..., :n, :], o2[..., n:, :]), jnp.where(m0, l2[..., :n, :], l2[..., n:, :])


def _attn_block_bwd(q, kc, vc, kp, vp, o, lse, do, dl):
    n = ATTN_BLOCK
    cat = jnp.concatenate
    qi = lax.broadcasted_iota(jnp.int32, (n, n), 0)
    kj = lax.broadcasted_iota(jnp.int32, (n, n), 1)
    m0 = (lax.broadcasted_iota(jnp.int32, (1, 128), 1) // HEAD_DIM) == 0
    scale = HEAD_DIM ** -0.5
    valid = kj <= qi
    keys, vals = kc, vc
    if kp is not None:
        valid = cat([valid, kj >= qi], axis=-1)
        keys, vals = cat([kc, kp], axis=-2), cat([vc, vp], axis=-2)
    stack = lambda z: cat([jnp.where(m0, z, 0.0), jnp.where(m0, 0.0, z)], axis=-2)
    q2, do2 = stack(q), stack(do)
    lse2 = cat([jnp.max(jnp.where(m0, lse, NEG), axis=-1, keepdims=True),
                jnp.max(jnp.where(m0, NEG, lse), axis=-1, keepdims=True)], axis=-2)
    delta = jnp.sum(do2 * cat([o, o], axis=-2), axis=-1, keepdims=True)
    dlse = jnp.sum(stack(dl), axis=-1, keepdims=True)
    mm = lambda a, b, form: _mm_raw(a, b, form, ATTN_PASSES)
    s = jnp.where(cat([valid, valid], axis=-2), mm(q2, keys, "nt") * scale, NEG)
    p = jnp.exp(s - lse2)
    ds = p * (mm(do2, vals, "nt") - delta + dlse)
    dq2 = mm(ds, keys, "nn") * scale
    dq = jnp.where(m0, dq2[..., :n, :], dq2[..., n:, :])
    dkeys = mm(ds, q2, "tn") * scale
    dvals = mm(p, do2, "tn")
    if kp is None:
        return dq, dkeys, dvals
    return dq, dkeys[..., :n, :], dvals[..., :n, :], dkeys[..., n:, :], dvals[..., n:, :]


def _combine_fn(o1, o2, o3, l1, l2, l3, og):
    seg = _seg_ones(o1.shape[-1])
    m = jnp.maximum(jnp.maximum(l1, l2), l3)
    e1, e2, e3 = jnp.exp(l1 - m), jnp.exp(l2 - m), jnp.exp(l3 - m)
    o = (e1 * o1 + e2 * o2 + e3 * o3) / (e1 + e2 + e3)
    o = o * lax.rsqrt(_segsum(o * o, seg) * (1.0 / HEAD_DIM) + NORM_EPS)
    return o * og


def _in_proj(x, g1, win):
    t = x.shape[0]
    tm = 512

    def body(x_ref, g_ref, w_ref, h_ref, pa_ref, qkv_ref):
        h = _rms_fwd(x_ref[...], g_ref[...]).astype(BF16)
        h_ref[...] = h
        proj = _dot_nt(h, w_ref[...])
        pa_ref[...] = proj[:, :SHIFT_COLS]
        for j in range(3):
            for p in range(N_PAIR):
                c0 = SHIFT_COLS + j * RW + p * 128
                qkv_ref[j, p] = proj[:, c0:c0 + 128]

    return pl.pallas_call(
        body, name="in_proj", grid=(t // tm,),
        in_specs=[pl.BlockSpec((tm, D_MODEL), lambda i: (i, 0)), pl.BlockSpec((1, D_MODEL), lambda i: (0, 0)),
                  pl.BlockSpec((IN_COLS, D_MODEL), lambda i: (0, 0))],
        out_specs=[pl.BlockSpec((tm, D_MODEL), lambda i: (i, 0)), pl.BlockSpec((tm, SHIFT_COLS), lambda i: (i, 0)),
                   pl.BlockSpec((3, N_PAIR, tm, 128), lambda i: (0, 0, i, 0))],
        out_shape=[jax.ShapeDtypeStruct((t, D_MODEL), BF16), jax.ShapeDtypeStruct((t, SHIFT_COLS), F32),
                   jax.ShapeDtypeStruct((3, N_PAIR, t, 128), F32)],
        compiler_params=_params(("parallel",)),
    )(x, g1, win)


def _shifted(p, last8, first):
    prow = jnp.where(first, 0.0, last8[7:8, :])
    rolled = pltpu.roll(p, 1, axis=0)
    rid = lax.broadcasted_iota(jnp.int32, p.shape, 0)
    return jnp.where(rid == 0, prow, rolled)


_PREP_TM = 256


def _prep_specs(tm):
    vec = lambda n: pl.BlockSpec((1, n), lambda i: (0, 0))
    mat = lambda r, n: pl.BlockSpec((r, n), lambda i: (0, 0))
    return [vec(SHIFT_COLS), vec(RW), mat(128, RW), vec(RW), mat(128, RW), mat(128, RW), vec(RW), vec(RW)]


def _prep_fwd(proj, pw):
    t = proj.shape[0]
    tm = _PREP_TM

    def body(p_ref, l8_ref, mu, w0, w2p, a0, a2p, g2, k_k, k_a, *outs):
        p = p_ref[...]
        pprev = _shifted(p, l8_ref[...], pl.program_id(0) == 0)
        res = _prep_fn(p, pprev, mu[...], w0[...], w2p[...], a0[...], a2p[...], g2[...], k_k[...], k_a[...])
        for o_ref, val in zip(outs, res):
            o_ref[...] = val

    row = pl.BlockSpec((tm, RW), lambda i: (i, 0))
    return pl.pallas_call(
        body, name="rwkv_prep", grid=(t // tm,),
        in_specs=[pl.BlockSpec((tm, SHIFT_COLS), lambda i: (i, 0)),
                  pl.BlockSpec((8, SHIFT_COLS), lambda i: (jnp.maximum(i * (tm // 8) - 1, 0), 0))] + _prep_specs(tm),
        out_specs=[row] * 7,
        out_shape=[jax.ShapeDtypeStruct((t, RW), F32)] * 7,
        compiler_params=_params(("parallel",)),
    )(proj, proj, *pw)


def _pairs(ref):
    return jnp.stack([ref[:, 128 * p:128 * (p + 1)] for p in range(N_PAIR)], axis=0)


def _wkv_fwd(r, lw, k2, v, kk, a):
    t = r.shape[0]
    nc = t // CHUNK

    def body(r_ref, lw_ref, k_ref, v_ref, kk_ref, a_ref, y_ref, s_ref, st):
        @pl.when(pl.program_id(0) == 0)
        def _():
            st[...] = jnp.zeros_like(st)

        s0 = st[...]
        s_ref[0] = s0
        y, s1 = _wkv_chunk_fn(s0, *[_pairs(ref) for ref in (r_ref, lw_ref, k_ref, v_ref, kk_ref, a_ref)])
        for p in range(N_PAIR):
            y_ref[:, 128 * p:128 * (p + 1)] = y[p]
        st[...] = s1

    blk = pl.BlockSpec((CHUNK, RW), lambda c: (c, 0))
    return pl.pallas_call(
        body, name="wkv_fwd", grid=(nc,),
        in_specs=[blk] * 6,
        out_specs=[blk, pl.BlockSpec((1, N_PAIR, 128, 128), lambda c: (c, 0, 0, 0))],
        out_shape=[jax.ShapeDtypeStruct((t, RW), F32), jax.ShapeDtypeStruct((nc, N_PAIR, 128, 128), F32)],
        scratch_shapes=[pltpu.VMEM((N_PAIR, 128, 128), F32)],
        compiler_params=_params(("arbitrary",)),
    )(r, lw, k2, v, kk, a)


_POST_TM = 256


def _post_fwd(y, r, k2, v, g, lnw, lnb, rk):
    t = y.shape[0]
    tm = _POST_TM

    def body(y_ref, r_ref, k_ref, v_ref, g_ref, lnw_ref, lnb_ref, rk_ref, o_ref):
        o_ref[...] = _post_fn(y_ref[...], r_ref[...], k_ref[...], v_ref[...], g_ref[...],
                              lnw_ref[...], lnb_ref[...], rk_ref[...]).astype(BF16)

    row = pl.BlockSpec((tm, RW), lambda i: (i, 0))
    vec = pl.BlockSpec((1, RW), lambda i: (0, 0))
    return pl.pallas_call(
        body, name="rwkv_post", grid=(t // tm,),
        in_specs=[row] * 5 + [vec] * 3, out_specs=row,
        out_shape=jax.ShapeDtypeStruct((t, RW), BF16),
        compiler_params=_params(("parallel",)),
    )(y, r, k2, v, g, lnw, lnb, rk)


ATTN_GROUP = 2


def _dilated_rows(d, r, n):
    if d == 1:
        return pl.ds(pl.multiple_of(n * ATTN_BLOCK, ATTN_BLOCK), ATTN_BLOCK)
    return pl.ds(r + n * (ATTN_BLOCK * d), ATTN_BLOCK, stride=d)


def _for_each_sequence(t, unit):
    for di, d in enumerate(DILATIONS):

        @pl.when(pl.program_id(1) == di)
        def _(di=di, d=d):
            nb = t // (ATTN_BLOCK * d)
            if d == 1:
                unit(di, [(d, 0, 0)], False)
                unit(di, [(d, 0, 1)], True)
                lax.fori_loop(1, nb // 2, lambda k, c: (unit(di, [(d, 0, 2 * k), (d, 0, 2 * k + 1)], True), c)[1], 0)
            else:

                def residues(r, carry):
                    unit(di, [(d, r, 0), (d, r + d // 2, 0)], False)
                    if nb > 1:
                        lax.fori_loop(1, nb, lambda n, c: (unit(di, [(d, r, n), (d, r + d // 2, n)], True), c)[1], 0)
                    return carry

                lax.fori_loop(0, d // 2, residues, 0)


def _take(ref, lead, rows_list):
    return jnp.stack([ref.at[(*lead, g)][rows, :] for rows in rows_list for g in range(ATTN_GROUP)], axis=0)


def _put(ref, lead, rows_list, val, add=False):
    k = 0
    for rows in rows_list:
        for g in range(ATTN_GROUP):
            if add:
                ref.at[(*lead, g)][rows, :] += val[k]
            else:
                ref.at[(*lead, g)][rows, :] = val[k]
            k += 1


def _attn_fwd(qkv):
    t = qkv.shape[2]

    def body(q_ref, k_ref, v_ref, o_ref, l_ref):
        def unit(di, places, has_prev):
            cur = [_dilated_rows(d, r, n) for d, r, n in places]
            args = [_take(ref, (0,), cur) for ref in (q_ref, k_ref, v_ref)]
            if has_prev:
                prv = [_dilated_rows(d, r, n - 1) for d, r, n in places]
                args += [_take(ref, (0,), prv) for ref in (k_ref, v_ref)]
            o, lse = _attn_block_fn(*args)
            _put(o_ref, (0,), cur, o)
            _put(l_ref, (0,), cur, lse)

        _for_each_sequence(t, unit)

    spec = lambda j: pl.BlockSpec((1, ATTN_GROUP, t, 128), lambda i, b: (j, i, 0, 0))
    out = pl.BlockSpec((1, ATTN_GROUP, t, 128), lambda i, b: (b, i, 0, 0))
    return pl.pallas_call(
        body, name="attn_fwd", grid=(N_PAIR // ATTN_GROUP, len(DILATIONS)),
        in_specs=[spec(0), spec(1), spec(2)], out_specs=[out, out],
        out_shape=[jax.ShapeDtypeStruct((3, N_PAIR, t, 128), F32)] * 2,
        compiler_params=_params(("parallel", "arbitrary")),
    )(qkv, qkv, qkv)


_COMB_TM = 256


def _combine_fwd(o, l, og):
    t = o.shape[2]
    tm = _COMB_TM

    def body(o_ref, l_ref, og_ref, y_ref):
        for p in range(N_PAIR):
            cols = slice(128 * p, 128 * (p + 1))
            y_ref[:, cols] = _combine_fn(o_ref[0, p], o_ref[1, p], o_ref[2, p], l_ref[0, p], l_ref[1, p], l_ref[2, p],
                                         og_ref[:, cols]).astype(BF16)

    blk = pl.BlockSpec((3, N_PAIR, tm, 128), lambda i: (0, 0, i, 0))
    return pl.pallas_call(
        body, name="attn_combine", grid=(t // tm,),
        in_specs=[blk, blk, pl.BlockSpec((1, RW), lambda i: (0, 0))], out_specs=pl.BlockSpec((tm, RW), lambda i: (i, 0)),
        out_shape=jax.ShapeDtypeStruct((t, RW), BF16),
        compiler_params=_params(("parallel",)),
    )(o, l, og)


def _out_proj(x, ycat, wout, g2):
    t = x.shape[0]
    tm = 256

    def body(x_ref, y_ref, w_ref, g_ref, x1_ref, h_ref):
        x1 = x_ref[...] + _dot(y_ref[...], w_ref[...])
        x1_ref[...] = x1
        h_ref[...] = _rms_fwd(x1, g_ref[...]).astype(BF16)

    row = pl.BlockSpec((tm, D_MODEL), lambda i: (i, 0))
    return pl.pallas_call(
        body, name="out_proj", grid=(t // tm,),
        in_specs=[row, row, pl.BlockSpec((D_MODEL, D_MODEL), lambda i: (0, 0)), pl.BlockSpec((1, D_MODEL), lambda i: (0, 0))],
        out_specs=[row, row],
        out_shape=[jax.ShapeDtypeStruct((t, D_MODEL), F32), jax.ShapeDtypeStruct((t, D_MODEL), BF16)],
        compiler_params=_params(("parallel",)),
    )(x, ycat, wout, g2)


def _ffn_up(h2, wg, wu):
    t = h2.shape[0]
    tm = 512

    def body(h_ref, wg_ref, wu_ref, gt_ref, up_ref, act_ref):
        h = h_ref[...]
        for c0 in range(0, D_FF, FF_CHUNK):
            cols = slice(c0, c0 + FF_CHUNK)
            gt = _dot_nt(h, wg_ref[cols, :])
            up = _dot_nt(h, wu_ref[cols, :])
            gt_ref[:, cols] = gt.astype(BF16)
            up_ref[:, cols] = up.astype(BF16)
            act_ref[:, cols] = (gt * _sigmoid(gt) * up).astype(BF16)

    wide = pl.BlockSpec((tm, D_FF), lambda i: (i, 0))
    wsp = pl.BlockSpec((D_FF, D_MODEL), lambda i: (0, 0))
    return pl.pallas_call(
        body, name="ffn_up", grid=(t // tm,),
        in_specs=[pl.BlockSpec((tm, D_MODEL), lambda i: (i, 0)), wsp, wsp],
        out_specs=[wide, wide, wide],
        out_shape=[jax.ShapeDtypeStruct((t, D_FF), BF16)] * 3,
        compiler_params=_params(("parallel",)),
    )(h2, wg, wu)


def _ffn_down_loss(x1, act, wd, gf, tgt):
    t = x1.shape[0]
    tm = 256

    def body(x1_ref, a_ref, w_ref, g_ref, t_ref, dx_ref, dxb_ref, loss_ref, dg_ref):
        first = pl.program_id(0) == 0
        x2 = x1_ref[...] + _dot(a_ref[...], w_ref[...])
        g = g_ref[...]
        diff = _rms_fwd(x2, g) - t_ref[...]
        lrow = 0.5 * jnp.sum(_colsum8(diff * diff), axis=1, keepdims=True) * (1.0 / D_MODEL)
        _acc(loss_ref, jnp.broadcast_to(lrow, (8, 128)), first)
        dx2, dgr = _rms_bwd(diff * (1.0 / D_MODEL), x2, g)
        dx_ref[...] = dx2
        dxb_ref[...] = dx2.astype(BF16)
        _acc(dg_ref, _colsum8(dgr), first)

    row = pl.BlockSpec((tm, D_MODEL), lambda i: (i, 0))
    return pl.pallas_call(
        body, name="ffn_down_loss", grid=(t // tm,),
        in_specs=[row, pl.BlockSpec((tm, D_FF), lambda i: (i, 0)), pl.BlockSpec((D_FF, D_MODEL), lambda i: (0, 0)),
                  pl.BlockSpec((1, D_MODEL), lambda i: (0, 0)), row],
        out_specs=[row, row, pl.BlockSpec((8, 128), lambda i: (0, 0)), pl.BlockSpec((8, D_MODEL), lambda i: (0, 0))],
        out_shape=[jax.ShapeDtypeStruct((t, D_MODEL), F32), jax.ShapeDtypeStruct((t, D_MODEL), BF16),
                   jax.ShapeDtypeStruct((8, 128), F32), jax.ShapeDtypeStruct((8, D_MODEL), F32)],
        compiler_params=_params(("arbitrary",)),
    )(x1, act, wd, gf, tgt)


def _ffn_bwd_act(dx2b, wd, gt, up):
    t = dx2b.shape[0]
    tm = 512

    def body(dx_ref, w_ref, gt_ref, up_ref, dgt_ref, dup_ref):
        dx = dx_ref[...]
        for c0 in range(0, D_FF, FF_CHUNK):
            cols = slice(c0, c0 + FF_CHUNK)
            dact = _dot_nt(dx, w_ref[cols, :])
            gt = gt_ref[:, cols].astype(F32)
            sg = _sigmoid(gt)
            dgt_ref[:, cols] = (dact * up_ref[:, cols].astype(F32) * sg * (1.0 + gt * (1.0 - sg))).astype(BF16)
            dup_ref[:, cols] = (dact * gt * sg).astype(BF16)

    wide = pl.BlockSpec((tm, D_FF), lambda i: (i, 0))
    return pl.pallas_call(
        body, name="ffn_bwd_act", grid=(t // tm,),
        in_specs=[pl.BlockSpec((tm, D_MODEL), lambda i: (i, 0)), pl.BlockSpec((D_FF, D_MODEL), lambda i: (0, 0)), wide, wide],
        out_specs=[wide, wide],
        out_shape=[jax.ShapeDtypeStruct((t, D_FF), BF16)] * 2,
        compiler_params=_params(("parallel",)),
    )(dx2b, wd, gt, up)


def _ffn_bwd_h(dgt, dup, wg, wu, dx2, x1, g2, wout):
    t = dgt.shape[0]
    tm = 256

    def body(dgt_ref, dup_ref, wg_ref, wu_ref, dx2_ref, x1_ref, g_ref, wo_ref, dx1_ref, dx1b_ref, dya_ref, dyb_ref, dg_ref):
        dh = _dot(dgt_ref[...], wg_ref[...]) + _dot(dup_ref[...], wu_ref[...])
        dxn, dgr = _rms_bwd(dh, x1_ref[...], g_ref[...])
        dx1 = dx2_ref[...] + dxn
        dx1_ref[...] = dx1
        dx1b = dx1.astype(BF16)
        dx1b_ref[...] = dx1b
        dy = _dot_nt(dx1b, wo_ref[...])
        dya_ref[...] = dy[:, :RW]
        dyb_ref[...] = dy[:, RW:]
        _acc(dg_ref, _colsum8(dgr), pl.program_id(0) == 0)

    wide = pl.BlockSpec((tm, D_FF), lambda i: (i, 0))
    row = pl.BlockSpec((tm, D_MODEL), lambda i: (i, 0))
    half = pl.BlockSpec((tm, RW), lambda i: (i, 0))
    wsp = pl.BlockSpec((D_FF, D_MODEL), lambda i: (0, 0))
    return pl.pallas_call(
        body, name="ffn_bwd_h", grid=(t // tm,),
        in_specs=[wide, wide, wsp, wsp, row, row, pl.BlockSpec((1, D_MODEL), lambda i: (0, 0)),
                  pl.BlockSpec((D_MODEL, D_MODEL), lambda i: (0, 0))],
        out_specs=[row, row, half, half, pl.BlockSpec((8, D_MODEL), lambda i: (0, 0))],
        out_shape=[jax.ShapeDtypeStruct((t, D_MODEL), F32), jax.ShapeDtypeStruct((t, D_MODEL), BF16),
                   jax.ShapeDtypeStruct((t, RW), F32), jax.ShapeDtypeStruct((t, RW), F32),
                   jax.ShapeDtypeStruct((8, D_MODEL), F32)],
        compiler_params=_params(("arbitrary",)),
    )(dgt, dup, wg, wu, dx2, x1, g2, wout)


def _wgrad(a, b, tk, tn, name):
    t, kdim = a.shape
    ndim = b.shape[1]

    def body(a_ref, b_ref, o_ref):
        o_ref[...] = _dot_tn(a_ref[...], b_ref[...])

    return pl.pallas_call(
        body, name=name, grid=(kdim // tk, ndim // tn),
        in_specs=[pl.BlockSpec((t, tk), lambda i, j: (0, i)), pl.BlockSpec((t, tn), lambda i, j: (0, j))],
        out_specs=pl.BlockSpec((tk, tn), lambda i, j: (i, j)),
        out_shape=jax.ShapeDtypeStruct((kdim, ndim), F32),
        compiler_params=_params(("parallel", "parallel")),
    )(a, b)


def _post_bwd(dya, y, r, k2, v, g, lnw, lnb, rk):
    t = y.shape[0]
    tm = _POST_TM

    def body(d_ref, y_ref, r_ref, k_ref, v_ref, g_ref, lnw_ref, lnb_ref, rk_ref,
             dy_ref, dr_ref, dk_ref, dv_ref, dg_ref, dlnw_ref, dlnb_ref, drk_ref):
        first = pl.program_id(0) == 0
        ones = jnp.ones((tm, 1), F32)
        prim = (y_ref[...], r_ref[...], k_ref[...], v_ref[...], g_ref[...],
                ones * lnw_ref[...], ones * lnb_ref[...], ones * rk_ref[...])
        _, vjp = jax.vjp(_post_fn, *prim)
        dy, dr, dk, dv, dg, dlnw, dlnb, drk = vjp(d_ref[...])
        dy_ref[...] = dy
        dr_ref[...] = dr
        dk_ref[...] = dk
        dv_ref[...] = dv
        dg_ref[...] = dg
        _acc(dlnw_ref, _colsum8(dlnw), first)
        _acc(dlnb_ref, _colsum8(dlnb), first)
        _acc(drk_ref, _colsum8(drk), first)

    row = pl.BlockSpec((tm, RW), lambda i: (i, 0))
    vec = pl.BlockSpec((1, RW), lambda i: (0, 0))
    part = pl.BlockSpec((8, RW), lambda i: (0, 0))
    return pl.pallas_call(
        body, name="rwkv_post_bwd", grid=(t // tm,),
        in_specs=[row] * 6 + [vec] * 3, out_specs=[row] * 5 + [part] * 3,
        out_shape=[jax.ShapeDtypeStruct((t, RW), F32)] * 5 + [jax.ShapeDtypeStruct((8, RW), F32)] * 3,
        compiler_params=_params(("arbitrary",)),
    )(dya, y, r, k2, v, g, lnw, lnb, rk)


def _wkv_bwd(dy, s0s, r, lw, k2, v, kk, a):
    t = r.shape[0]
    nc = t // CHUNK

    def body(dy_ref, s_ref, r_ref, lw_ref, k_ref, v_ref, kk_ref, a_ref,
             dr_ref, dlw_ref, dk_ref, dv_ref, dkk_ref, da_ref, ds):
        @pl.when(pl.program_id(0) == 0)
        def _():
            ds[...] = jnp.zeros_like(ds)

        _, vjp = jax.vjp(_wkv_chunk_fn, s_ref[0],
                         *[_pairs(ref) for ref in (r_ref, lw_ref, k_ref, v_ref, kk_ref, a_ref)])
        res = vjp((_pairs(dy_ref), ds[...]))
        ds[...] = res[0]
        for ref, val in zip((dr_ref, dlw_ref, dk_ref, dv_ref, dkk_ref, da_ref), res[1:]):
            for p in range(N_PAIR):
                ref[:, 128 * p:128 * (p + 1)] = val[p]

    blk = pl.BlockSpec((CHUNK, RW), lambda c: (nc - 1 - c, 0))
    return pl.pallas_call(
        body, name="wkv_bwd", grid=(nc,),
        in_specs=[blk, pl.BlockSpec((1, N_PAIR, 128, 128), lambda c: (nc - 1 - c, 0, 0, 0))] + [blk] * 6,
        out_specs=[blk] * 6,
        out_shape=[jax.ShapeDtypeStruct((t, RW), F32)] * 6,
        scratch_shapes=[pltpu.VMEM((N_PAIR, 128, 128), F32)],
        compiler_params=_params(("arbitrary",)),
    )(dy, s0s, r, lw, k2, v, kk, a)


def _prep_bwd(proj, pw, douts):
    t = proj.shape[0]
    tm = _PREP_TM
    nt = t // tm

    def body(p_ref, l8_ref, mu, w0, w2p, a0, a2p, g2, k_k, k_a, dr, dr2, dlw, dk2, dk22, dv, dv2, dkk, da, dg,
             dp_ref, dmu_ref, dw0_ref, dw2_ref, da0_ref, da2_ref, dg2_ref, dkk_ref, dka_ref, carry):
        i = pl.program_id(0)
        first = i == 0

        @pl.when(first)
        def _():
            carry[...] = jnp.zeros_like(carry)

        p = p_ref[...]
        pprev = _shifted(p, l8_ref[...], i == nt - 1)
        ones = jnp.ones((tm, 1), F32)
        prim = (p, pprev, ones * mu[...], ones * w0[...], w2p[...], ones * a0[...], a2p[...], g2[...],
                ones * k_k[...], ones * k_a[...])
        _, vjp = jax.vjp(_prep_fn, *prim)
        dp, dpp, dmu, dw0, dw2, da0, da2, dg2, dkk_, dka = vjp(
            (dr[...] + dr2[...], dlw[...], dk2[...] + dk22[...], dv[...] + dv2[...], dkk[...], da[...], dg[...]))
        up = pltpu.roll(dpp, tm - 1, axis=0)
        rid = lax.broadcasted_iota(jnp.int32, dpp.shape, 0)
        dp_ref[...] = dp + jnp.where(rid == tm - 1, carry[0:1, :], up)
        carry[...] = jnp.broadcast_to(dpp[0:1, :], carry.shape)
        _acc(dmu_ref, _colsum8(dmu), first)
        _acc(dw0_ref, _colsum8(dw0), first)
        _acc(dw2_ref, dw2, first)
        _acc(da0_ref, _colsum8(da0), first)
        _acc(da2_ref, da2, first)
        _acc(dg2_ref, dg2, first)
        _acc(dkk_ref, _colsum8(dkk_), first)
        _acc(dka_ref, _colsum8(dka), first)

    rev = lambda i: (nt - 1 - i, 0)
    row = pl.BlockSpec((tm, RW), rev)
    part = lambda n: pl.BlockSpec((8, n), lambda i: (0, 0))
    mat = pl.BlockSpec((128, RW), lambda i: (0, 0))
    return pl.pallas_call(
        body, name="rwkv_prep_bwd", grid=(nt,),
        in_specs=[pl.BlockSpec((tm, SHIFT_COLS), rev),
                  pl.BlockSpec((8, SHIFT_COLS), lambda i: (jnp.maximum((nt - 1 - i) * (tm // 8) - 1, 0), 0))]
                 + _prep_specs(tm) + [row] * 10,
        out_specs=[pl.BlockSpec((tm, SHIFT_COLS), rev), part(SHIFT_COLS), part(RW), mat, part(RW), mat, mat,
                   part(RW), part(RW)],
        out_shape=[jax.ShapeDtypeStruct((t, SHIFT_COLS), F32), jax.ShapeDtypeStruct((8, SHIFT_COLS), F32),
                   jax.ShapeDtypeStruct((8, RW), F32), jax.ShapeDtypeStruct((128, RW), F32),
                   jax.ShapeDtypeStruct((8, RW), F32), jax.ShapeDtypeStruct((128, RW), F32),
                   jax.ShapeDtypeStruct((128, RW), F32), jax.ShapeDtypeStruct((8, RW), F32),
                   jax.ShapeDtypeStruct((8, RW), F32)],
        scratch_shapes=[pltpu.VMEM((8, SHIFT_COLS), F32)],
        compiler_params=_params(("arbitrary",)),
    )(proj, proj, *pw, *douts)


def _combine_bwd(dyb, o, l, og):
    t = dyb.shape[0]
    tm = _COMB_TM

    def body(d_ref, o_ref, l_ref, og_ref, do_ref, dl_ref, dog_ref):
        ones = jnp.ones((tm, 1), F32)
        dog = []
        for p in range(N_PAIR):
            cols = slice(128 * p, 128 * (p + 1))
            _, vjp = jax.vjp(_combine_fn, o_ref[0, p], o_ref[1, p], o_ref[2, p], l_ref[0, p], l_ref[1, p], l_ref[2, p],
                             ones * og_ref[:, cols])
            res = vjp(d_ref[:, cols])
            for b in range(3):
                do_ref[b, p] = res[b]
                dl_ref[b, p] = res[3 + b]
            dog.append(_colsum8(res[6]))
        _acc(dog_ref, jnp.concatenate(dog, axis=1), pl.program_id(0) == 0)

    blk = pl.BlockSpec((3, N_PAIR, tm, 128), lambda i: (0, 0, i, 0))
    return pl.pallas_call(
        body, name="attn_combine_bwd", grid=(t // tm,),
        in_specs=[pl.BlockSpec((tm, RW), lambda i: (i, 0)), blk, blk, pl.BlockSpec((1, RW), lambda i: (0, 0))],
        out_specs=[blk, blk, pl.BlockSpec((8, RW), lambda i: (0, 0))],
        out_shape=[jax.ShapeDtypeStruct((3, N_PAIR, t, 128), F32)] * 2 + [jax.ShapeDtypeStruct((8, RW), F32)],
        compiler_params=_params(("arbitrary",)),
    )(dyb, o, l, og)


def _attn_bwd(do, dl, o, lse, qkv):
    t = qkv.shape[2]

    def body(do_ref, dl_ref, o_ref, l_ref, q_ref, k_ref, v_ref, dq_ref, dk_ref, dv_ref):
        @pl.when(pl.program_id(1) == 0)
        def _():
            for ref in (dq_ref, dk_ref, dv_ref):
                ref[...] = jnp.zeros_like(ref)

        def unit(di, places, has_prev):
            cur = [_dilated_rows(d, r, n) for d, r, n in places]
            q, kc, vc = [_take(ref, (0,), cur) for ref in (q_ref, k_ref, v_ref)]
            kp = vp = None
            if has_prev:
                prv = [_dilated_rows(d, r, n - 1) for d, r, n in places]
                kp, vp = [_take(ref, (0,), prv) for ref in (k_ref, v_ref)]
            res = _attn_block_bwd(q, kc, vc, kp, vp, *[_take(ref, (0,), cur) for ref in (o_ref, l_ref, do_ref, dl_ref)])
            _put(dq_ref, (), cur, res[0], add=True)
            _put(dk_ref, (), cur, res[1], add=True)
            _put(dv_ref, (), cur, res[2], add=True)
            if has_prev:
                _put(dk_ref, (), prv, res[3], add=True)
                _put(dv_ref, (), prv, res[4], add=True)

        _for_each_sequence(t, unit)

    spec = lambda j: pl.BlockSpec((1, ATTN_GROUP, t, 128), lambda i, b: (j, i, 0, 0))
    branch = pl.BlockSpec((1, ATTN_GROUP, t, 128), lambda i, b: (b, i, 0, 0))
    out = pl.BlockSpec((ATTN_GROUP, t, 128), lambda i, b: (i, 0, 0))
    return pl.pallas_call(
        body, name="attn_bwd", grid=(N_PAIR // ATTN_GROUP, len(DILATIONS)),
        in_specs=[branch] * 4 + [spec(0), spec(1), spec(2)], out_specs=[out] * 3,
        out_shape=[jax.ShapeDtypeStruct((N_PAIR, t, 128), F32)] * 3,
        compiler_params=_params(("parallel", "arbitrary")),
    )(do, dl, o, lse, qkv, qkv, qkv)


def _in_proj_bwd(dpa, dq, dk, dv, win, x, g1, dx1):
    t = x.shape[0]
    tm = 256

    def body(dpa_ref, dq_ref, dk_ref, dv_ref, w_ref, x_ref, g_ref, dx1_ref, dproj_ref, dx_ref, dg_ref):
        parts = [dpa_ref[...]] + [ref[p] for ref in (dq_ref, dk_ref, dv_ref) for p in range(N_PAIR)]
        dproj = jnp.concatenate([z.astype(BF16) for z in parts], axis=1)
        dproj_ref[...] = dproj
        dh = _dot(dproj, w_ref[...])
        dxn, dgr = _rms_bwd(dh, x_ref[...], g_ref[...])
        dx_ref[...] = dx1_ref[...] + dxn
        _acc(dg_ref, _colsum8(dgr), pl.program_id(0) == 0)

    row = pl.BlockSpec((tm, D_MODEL), lambda i: (i, 0))
    pair = pl.BlockSpec((N_PAIR, tm, 128), lambda i: (0, i, 0))
    return pl.pallas_call(
        body, name="in_proj_bwd", grid=(t // tm,),
        in_specs=[pl.BlockSpec((tm, SHIFT_COLS), lambda i: (i, 0))] + [pair] * 3
                 + [pl.BlockSpec((IN_COLS, D_MODEL), lambda i: (0, 0)), row, pl.BlockSpec((1, D_MODEL), lambda i: (0, 0)), row],
        out_specs=[pl.BlockSpec((tm, IN_COLS), lambda i: (i, 0)), row, pl.BlockSpec((8, D_MODEL), lambda i: (0, 0))],
        out_shape=[jax.ShapeDtypeStruct((t, IN_COLS), BF16), jax.ShapeDtypeStruct((t, D_MODEL), F32),
                   jax.ShapeDtypeStruct((8, D_MODEL), F32)],
        compiler_params=_params(("arbitrary",)),
    )(dpa, dq, dk, dv, win, x, g1, dx1)


def _pad_lora(w, lo):
    z = jnp.zeros((64, RW), F32)
    return jnp.concatenate([w, z], axis=0) if lo == 0 else jnp.concatenate([z, w], axis=0)


def _local_step(x, tgt, win, vecs, w2, a2, g2m, get_rest, send_rest):
    pw = (vecs["mu_shift"], vecs["decay_w0"], _pad_lora(w2, 0), vecs["iclr_a0"], _pad_lora(a2, 64), g2m,
          vecs["k_k"], vecs["k_a"])
    h, proj, qkv = _in_proj(x, vecs["mix_norm_g"], win)
    r, lw, k2, v, kk, a, g = _prep_fwd(proj, pw)
    y, s0s = _wkv_fwd(r, lw, k2, v, kk, a)
    ya = _post_fwd(y, r, k2, v, g, vecs["ln_x_w"], vecs["ln_x_b"], vecs["r_k"])
    o_att, l_att = _attn_fwd(qkv)
    yb = _combine_fwd(o_att, l_att, vecs["attn_out_g"])

    wout, wg, wu, wd = get_rest(yb)
    ycat = jnp.concatenate([ya, yb], axis=1)
    x1, h2 = _out_proj(x, ycat, wout, vecs["ffn_norm_g"])
    gt, up, act = _ffn_up(h2, wg, wu)
    dx2, dx2b, loss8, dgf = _ffn_down_loss(x1, act, wd, vecs["final_norm_g"], tgt)

    dgt, dup = _ffn_bwd_act(dx2b, wd, gt, up)
    dx1, dx1b, dya, dyb, dg2n = _ffn_bwd_h(dgt, dup, wg, wu, dx2, x1, vecs["ffn_norm_g"], wout)
    gw = {
        "w_down": _wgrad(act, dx2b, 1408, 1024, "wgrad_down"),
        "w_gate": _wgrad(dgt, h2, 1408, 1024, "wgrad_gate"),
        "w_up": _wgrad(dup, h2, 1408, 1024, "wgrad_up"),
        "w_out": _wgrad(ycat, dx1b, 1024, 1024, "wgrad_out"),
    }

    lnw = vecs["ln_x_w"] + send_rest(gw)[0, 0]
    dy, dr_p, dk2_p, dv_p, dg, dlnw, dlnb, drk = _post_bwd(dya, y, r, k2, v, g, lnw, vecs["ln_x_b"], vecs["r_k"])
    dr_s, dlw, dk2_s, dv_s, dkk, da = _wkv_bwd(dy, s0s, r, lw, k2, v, kk, a)
    dpa, dmu, dw0, dw2p, da0, da2p, dg2m, dk_k, dk_a = _prep_bwd(
        proj, pw, (dr_p, dr_s, dlw, dk2_p, dk2_s, dv_p, dv_s, dkk, da, dg))

    do_att, dl_att, dog = _combine_bwd(dyb, o_att, l_att, vecs["attn_out_g"])
    dq, dk, dv = _attn_bwd(do_att, dl_att, o_att, l_att, qkv)
    dproj, dx, dg1 = _in_proj_bwd(dpa, dq, dk, dv, win, x, vecs["mix_norm_g"], dx1)
    gw["w_in"] = _wgrad(dproj, h, 1664, 1024, "wgrad_in")
    gw["decay_w2"] = dw2p[:64]
    gw["iclr_a2"] = da2p[64:]
    gw["gate_g2"] = dg2m
    gv = {"mix_norm_g": dg1, "mu_shift": dmu, "decay_w0": dw0, "iclr_a0": da0, "k_k": dk_k, "k_a": dk_a, "r_k": drk,
          "ln_x_w": dlnw, "ln_x_b": dlnb, "attn_out_g": dog, "ffn_norm_g": dg2n, "final_norm_g": dgf}
    return loss8, dx, gw, gv


N_CHIP = 4
N_DEV = 8
MATS = ("w_in", "w_out", "w_gate", "w_up", "w_down")
LORAS = ("decay_w2", "iclr_a2", "gate_g2")
VECS = (("mix_norm_g", 1024), ("mu_shift", 1792), ("decay_w0", 512), ("iclr_a0", 512), ("k_k", 512), ("k_a", 512),
        ("r_k", 512), ("ln_x_w", 512), ("ln_x_b", 512), ("attn_out_g", 512), ("ffn_norm_g", 1024),
        ("final_norm_g", 1024))
N_VEC = sum(n for _, n in VECS)
N_SMALL = N_VEC + 128
ANY = pl.BlockSpec(memory_space=pl.ANY)


def _flip(v, f):
    return 1 - v if f else v


class _Me:
    def __init__(self, mode):
        x, y, c = lax.axis_index("x"), lax.axis_index("y"), lax.axis_index("c")
        self.core, self.chip, self.dev = c, 2 * x + y, 4 * x + 2 * y + c
        self.sibling = (x, y, 1 - c)
        if mode == "chips":
            self.peers = [(px, py, c) for px, py in ((1 - x, y), (x, 1 - y), (1 - x, 1 - y))]
        else:
            self.peers = [(_flip(x, k & 4), _flip(y, k & 2), _flip(c, k & 1)) for k in range(1, N_DEV)]


def _half(core, rows):
    h = rows // 2
    return pl.ds(pl.multiple_of(core * h, h), h)


def _peer_copy(srcs, dsts, kinds, send_sems, recv_sems, me, j, i, incoming):
    px, py, pc = me.peers[j]
    pchip, pdev = 2 * px + py, 4 * px + 2 * py + pc
    src, dst, kind = srcs[i], dsts[i], kinds[i]
    if kind == "gather":
        rows = _half(me.core, src.shape[0])
        src, dst = src.at[rows], dst.at[pchip if incoming else me.chip, rows]
    elif kind == "scatter":
        src, dst = src.at[pchip, _half(pc, src.shape[1])], dst.at[pdev if incoming else me.dev]
    else:
        dst = dst.at[pdev if incoming else me.dev]
    n = len(srcs)
    return pltpu.make_async_remote_copy(src_ref=src, dst_ref=dst, send_sem=send_sems.at[n * j + i],
                                        recv_sem=recv_sems.at[n * j + i], device_id=(px, py, pc), device_id_type=MESH)


def _mode(kinds):
    return "chips" if kinds[0] == "gather" else "devs"


def _npeer(kinds):
    return N_CHIP - 1 if kinds[0] == "gather" else N_DEV - 1


def _swap_gathered(lands, name):
    n = len(lands)

    def body(*refs):
        dsts, send_sems, recv_sems = refs[n:2 * n], refs[2 * n], refs[2 * n + 1]
        me = _Me("chips")

        def copy(j, i, incoming):
            px, py, _ = me.peers[j]
            rows_out, rows_in = _half(me.core, dsts[i].shape[1]), _half(1 - me.core, dsts[i].shape[1])
            return pltpu.make_async_remote_copy(
                src_ref=dsts[i].at[2 * px + py, rows_out], dst_ref=dsts[i].at[2 * px + py, rows_in if incoming else rows_out],
                send_sem=send_sems.at[n * j + i], recv_sem=recv_sems.at[n * j + i], device_id=me.sibling, device_id_type=MESH)

        sends = [copy(j, i, False) for j in range(3) for i in range(n)]
        for cp in sends:
            cp.start()
        for j in range(3):
            for i in range(n):
                copy(j, i, True).wait_recv()
        for cp in sends:
            cp.wait_send()

    return pl.pallas_call(
        body, name=name, in_specs=[ANY] * n, out_specs=[ANY] * n,
        out_shape=[jax.ShapeDtypeStruct(l.shape, l.dtype) for l in lands],
        input_output_aliases={i: i for i in range(n)},
        scratch_shapes=[pltpu.SemaphoreType.DMA((3 * n,)), pltpu.SemaphoreType.DMA((3 * n,))],
    )(*lands)


def _join_halves(sums, name):
    n = len(sums)

    def body(*refs):
        dsts, send_sems, recv_sems = refs[n:2 * n], refs[2 * n], refs[2 * n + 1]
        me = _Me("chips")

        def copy(i, incoming):
            mine, other = _half(me.core, dsts[i].shape[0]), _half(1 - me.core, dsts[i].shape[0])
            return pltpu.make_async_remote_copy(src_ref=dsts[i].at[mine], dst_ref=dsts[i].at[other if incoming else mine],
                                                send_sem=send_sems.at[i], recv_sem=recv_sems.at[i],
                                                device_id=me.sibling, device_id_type=MESH)

        sends = [copy(i, False) for i in range(n)]
        for cp in sends:
            cp.start()
        for i in range(n):
            copy(i, True).wait_recv()
        for cp in sends:
            cp.wait_send()

    return pl.pallas_call(
        body, name=name, in_specs=[ANY] * n, out_specs=[ANY] * n,
        out_shape=[jax.ShapeDtypeStruct(s.shape, s.dtype) for s in sums],
        input_output_aliases={i: i for i in range(n)},
        scratch_shapes=[pltpu.SemaphoreType.DMA((n,)), pltpu.SemaphoreType.DMA((n,))],
    )(*sums)


HBM = pl.BlockSpec(memory_space=pltpu.HBM)
SEM = pl.BlockSpec(memory_space=pltpu.SEMAPHORE)
EFFECT = pltpu.SideEffectType.DATAFLOW_SIDE_EFFECTING


def _swap_start(arrs, lands, kinds, name):
    n = len(arrs)

    def body(*refs):
        srcs, dsts, send_sems, recv_sems, token = refs[:n], refs[n:2 * n], refs[2 * n], refs[2 * n + 1], refs[-1]
        me = _Me(_mode(kinds))
        for j in range(len(me.peers)):
            for i in range(n):
                _peer_copy(srcs, dsts, kinds, send_sems, recv_sems, me, j, i, False).start()
        token[...] = jnp.zeros_like(token)

    ns = _npeer(kinds) * n
    outs = pl.pallas_call(
        body, name=name,
        out_shape=(pltpu.SemaphoreType.DMA((ns,)), pltpu.SemaphoreType.DMA((ns,)),
                   *[pltpu.HBM(a.shape, a.dtype) for a in arrs], *[pltpu.HBM(l.shape, l.dtype) for l in lands],
                   jax.ShapeDtypeStruct((8, 128), F32)),
        in_specs=[HBM] * (2 * n), out_specs=(SEM, SEM, *[HBM] * (2 * n), pl.BlockSpec(memory_space=pltpu.VMEM)),
        input_output_aliases={k: 2 + k for k in range(2 * n)},
        compiler_params=pltpu.CompilerParams(has_side_effects=EFFECT),
    )(*[pltpu.with_memory_space_constraint(a, pltpu.HBM) for a in arrs],
      *[pltpu.with_memory_space_constraint(l, pltpu.HBM) for l in lands])
    return outs[0], outs[1], outs[2:2 + n], outs[2 + n:2 + 2 * n], outs[-1]


def _swap_wait(send_sems, recv_sems, srcs_thru, lands_thru, after, kinds, name):
    n = len(srcs_thru)

    def body(*refs):
        srcs, dsts, s_sems, r_sems = refs[:n], refs[n:2 * n], refs[2 * n], refs[2 * n + 1]
        me = _Me(_mode(kinds))
        for j in range(len(me.peers)):
            for i in range(n):
                cp = _peer_copy(srcs, dsts, kinds, s_sems, r_sems, me, j, i, True)
                cp.wait_send()
                cp.wait_recv()

    outs = pl.pallas_call(
        body, name=name,
        out_shape=tuple(pltpu.HBM(a.shape, a.dtype) for a in (*srcs_thru, *lands_thru)),
        in_specs=[HBM] * (2 * n) + [SEM, SEM, ANY], out_specs=tuple([HBM] * (2 * n)),
        input_output_aliases={k: k for k in range(2 * n)},
        compiler_params=pltpu.CompilerParams(has_side_effects=EFFECT),
    )(*srcs_thru, *lands_thru, send_sems, recv_sems, after)
    return outs[n:]


def _adamw(w, g, m, v):
    m = ADAM_B1 * m + (1.0 - ADAM_B1) * g
    v = ADAM_B2 * v + (1.0 - ADAM_B2) * (g * g)
    m_hat = m / (1.0 - ADAM_B1 ** ADAM_STEP)
    v_hat = v / (1.0 - ADAM_B2 ** ADAM_STEP)
    delta = -ADAM_LR * (m_hat / (jnp.sqrt(v_hat) + ADAM_EPS) + ADAM_WD * w)
    return delta, m, v


def _reduce8(rbuf, core, tr, name):
    _, h, cols = rbuf.shape

    def body(core_ref, r_ref, g_ref):
        g = r_ref[0].astype(F32)
        for s in range(1, N_DEV):
            g = g + r_ref[s].astype(F32)
        g_ref[...] = g

    return pl.pallas_call(
        body, name=name,
        grid_spec=pltpu.PrefetchScalarGridSpec(
            num_scalar_prefetch=1, grid=(h // tr,),
            in_specs=[pl.BlockSpec((N_DEV, tr, cols), lambda i, core_ref: (0, i, 0))],
            out_specs=pl.BlockSpec((tr, cols), lambda i, core_ref: (core_ref[0] * (h // tr) + i, 0))),
        out_shape=jax.ShapeDtypeStruct((2 * h, cols), F32),
        compiler_params=_params(("parallel",)),
    )(core, rbuf)


def _adamw_call(g, w, m, v, tr, name):
    _, rows, cols = w.shape

    def body(g_in, w_ref, m_ref, v_ref, g_ref, d_ref, nm_ref, nv_ref):
        g = g_in[...]
        g_ref[0] = g
        d_ref[0], nm_ref[0], nv_ref[0] = _adamw(w_ref[0], g, m_ref[0], v_ref[0])

    row = pl.BlockSpec((1, tr, cols), lambda i: (0, i, 0))
    return pl.pallas_call(
        body, name=name, grid=(rows // tr,),
        in_specs=[pl.BlockSpec((tr, cols), lambda i: (i, 0)), row, row, row], out_specs=[row] * 4,
        out_shape=[jax.ShapeDtypeStruct(w.shape, F32)] * 4,
        compiler_params=_params(("parallel",)),
    )(g, w, m, v)


def _rowsum_small(parts, loss8):
    def body(*refs):
        out = refs[-1]
        c0 = 0
        for ref in refs[:-1]:
            n = ref.shape[1]
            out[:, c0:c0 + n] = jnp.sum(ref[...], axis=0, keepdims=True)
            c0 += n

    return pl.pallas_call(body, name="rowsum_small", out_shape=jax.ShapeDtypeStruct((1, N_SMALL), F32))(*parts, loss8)


def _reduce_adamw_small(sbuf, ws, ms, vs):
    nv = len(ws)

    def body(*refs):
        s_ref, ins, outs = refs[0], refs[1:1 + 3 * nv], refs[1 + 3 * nv:]
        tot = s_ref[0]
        for s in range(1, N_DEV):
            tot = tot + s_ref[s]
        c0 = 0
        for i in range(nv):
            n = ins[i].shape[1]
            g = tot[:, c0:c0 + n]
            outs[i][...] = g
            outs[nv + i][...], outs[2 * nv + i][...], outs[3 * nv + i][...] = _adamw(
                ins[i][...], g, ins[nv + i][...], ins[2 * nv + i][...])
            c0 += n
        outs[-1][...] = tot[:, c0:]

    return pl.pallas_call(
        body, name="reduce_adamw_small",
        out_shape=[jax.ShapeDtypeStruct(a.shape, F32) for a in ws] * 4 + [jax.ShapeDtypeStruct((1, 128), F32)],
    )(sbuf, *ws, *ms, *vs)


_TRANSPOSED = ("w_in", "w_gate", "w_up")
_ROW_STACKED = MATS
_ADAM_TILE = {"w_in": 208, "w_out": 256, "w_gate": 176, "w_up": 176, "w_down": 176, "decay_w2": 64, "iclr_a2": 64,
              "gate_g2": 128}
_SUM_TILE = {"w_in": 208, "w_out": 128, "w_gate": 176, "w_up": 176, "w_down": 176, "decay_w2": 32, "iclr_a2": 32,
             "gate_g2": 64}


def _full(n, stacked):
    p, r, c = stacked.shape
    if n in _ROW_STACKED:
        return stacked.reshape(p * r, c)
    return jnp.transpose(stacked, (1, 0, 2)).reshape(r, p * c)


def _by_chip(n, full):
    if n in _ROW_STACKED:
        return full.reshape(N_CHIP, full.shape[0] // N_CHIP, full.shape[1])
    r, c = full.shape
    return jnp.transpose(full.reshape(r, N_CHIP, c // N_CHIP), (1, 0, 2))


def _with_own(land_shape, dtype, own, slot):
    return lax.dynamic_update_slice(lax.empty(land_shape, dtype), own[None], (slot,) + (0,) * own.ndim)


def kernel(x, mix_norm_g, w_in, mu_shift, decay_w0, decay_w2, iclr_a0, iclr_a2, gate_g2, k_k, k_a, r_k, ln_x_w, ln_x_b, attn_out_g, w_out, ffn_norm_g, w_gate, w_up, w_down, final_norm_g, loss_target, m_mix_norm_g, m_w_in, m_mu_shift, m_decay_w0, m_decay_w2, m_iclr_a0, m_iclr_a2, m_gate_g2, m_k_k, m_k_a, m_r_k, m_ln_x_w, m_ln_x_b, m_attn_out_g, m_w_out, m_ffn_norm_g, m_w_gate, m_w_up, m_w_down, m_final_norm_g, v_mix_norm_g, v_w_in, v_mu_shift, v_decay_w0, v_decay_w2, v_iclr_a0, v_iclr_a2, v_gate_g2, v_k_k, v_k_a, v_r_k, v_ln_x_w, v_ln_x_b, v_attn_out_g, v_w_out, v_ffn_norm_g, v_w_gate, v_w_up, v_w_down, v_final_norm_g):
    names = ("mix_norm_g", "w_in", "mu_shift", "decay_w0", "decay_w2", "iclr_a0", "iclr_a2", "gate_g2", "k_k", "k_a",
             "r_k", "ln_x_w", "ln_x_b", "attn_out_g", "w_out", "ffn_norm_g", "w_gate", "w_up", "w_down", "final_norm_g")
    w = dict(zip(names, (mix_norm_g, w_in, mu_shift, decay_w0, decay_w2, iclr_a0, iclr_a2, gate_g2, k_k, k_a, r_k,
                         ln_x_w, ln_x_b, attn_out_g, w_out, ffn_norm_g, w_gate, w_up, w_down, final_norm_g)))
    m = dict(zip(names, (m_mix_norm_g, m_w_in, m_mu_shift, m_decay_w0, m_decay_w2, m_iclr_a0, m_iclr_a2, m_gate_g2,
                         m_k_k, m_k_a, m_r_k, m_ln_x_w, m_ln_x_b, m_attn_out_g, m_w_out, m_ffn_norm_g, m_w_gate,
                         m_w_up, m_w_down, m_final_norm_g)))
    v = dict(zip(names, (v_mix_norm_g, v_w_in, v_mu_shift, v_decay_w0, v_decay_w2, v_iclr_a0, v_iclr_a2, v_gate_g2,
                         v_k_k, v_k_a, v_r_k, v_ln_x_w, v_ln_x_b, v_attn_out_g, v_w_out, v_ffn_norm_g, v_w_gate,
                         v_w_up, v_w_down, v_final_norm_g)))
    first = ("w_in",) + LORAS
    rest = ("w_out", "w_gate", "w_up", "w_down")
    xi, yi, ci = lax.axis_index("x"), lax.axis_index("y"), lax.axis_index("c")
    my_chip, my_dev = 2 * xi + yi, 4 * xi + 2 * yi + ci
    gather, scatter = ("gather",) * 4, ("scatter",) * 4

    sh = lambda z, n: jnp.transpose(z[0]) if n in _TRANSPOSED else z[0]
    mine = [sh(w["w_in"], "w_in").astype(BF16)] + [w[n][0] for n in LORAS]
    early = _swap_start(mine, [_with_own((N_CHIP,) + a.shape, a.dtype, a, my_chip) for a in mine], gather, "gather_first_start")
    wb = {n: (sh(w[n], n) + early[4][0, 0]).astype(BF16) for n in rest}
    lands = [_with_own((N_CHIP,) + wb[n].shape, BF16, wb[n], my_chip) for n in rest]
    ssem, rsem, srcs_thru, lands_thru, tok = _swap_start([wb[n] for n in rest], lands, gather, "gather_rest_start")
    got = _swap_wait(early[0], early[1], early[2], early[3], tok, gather, "gather_first_wait")
    win, w2, a2, g2m = (_full(n, z) for n, z in zip(first, _swap_gathered(got, "gather_first_halves")))

    vecs = {n: w[n].reshape(1, sz) for n, sz in VECS}
    vecs["mix_norm_g"] = vecs["mix_norm_g"] + tok[0, 0]

    def get_rest(after):
        halves = _swap_wait(ssem, rsem, srcs_thru, lands_thru, after, gather, "gather_rest_wait")
        return [_full(n, z) for n, z in zip(rest, _swap_gathered(halves, "gather_rest_halves"))]

    flight = []

    def my_half(g):
        h = g.shape[1] // 2
        return lax.dynamic_slice(g, (my_chip, ci * h, 0), (1, h, g.shape[2]))[0]

    def send_rest(gw):
        gs = [_by_chip(n, gw[n]).astype(BF16) for n in rest]
        into = [_with_own((N_DEV,) + my_half(g).shape, BF16, my_half(g), my_dev) for g in gs]
        flight.extend(_swap_start(gs, into, scatter, "exchange_rest_start"))
        return flight[4]

    loss8, dx, gw, gv = _local_step(x[0], loss_target[0], win, vecs, w2, a2, g2m, get_rest, send_rest)

    small = _rowsum_small([gv[n] for n, _ in VECS], loss8)
    gs = [_by_chip(n, gw[n]).astype(BF16) for n in first]
    into = [_with_own((N_DEV,) + my_half(g).shape, BF16, my_half(g), my_dev) for g in gs]
    into.append(_with_own((N_DEV,) + small.shape, F32, small, my_dev))
    last = _swap_start(gs + [small], into, scatter + ("all",), "exchange_first_start")

    core = jnp.reshape(ci, (1,)).astype(jnp.int32)

    def update(group, rbufs, tag):
        sums = [_reduce8(rb, core, _SUM_TILE[n], "reduce_" + n) for n, rb in zip(group, rbufs)]
        gsum = _join_halves(sums, "join_halves_" + tag)
        out = {}
        for n, g in zip(group, gsum):
            r = _adamw_call(g, sh(w[n], n)[None], sh(m[n], n)[None], sh(v[n], n)[None], _ADAM_TILE[n], "adamw_" + n)
            out[n] = [jnp.transpose(z[0])[None] for z in r] if n in _TRANSPOSED else r
        return out

    res = update(rest, _swap_wait(flight[0], flight[1], flight[2], flight[3], last[4], scatter, "exchange_rest_wait"), "rest")
    got = _swap_wait(last[0], last[1], last[2], last[3], res["w_down"][1], scatter + ("all",), "exchange_first_wait")
    res.update(update(first, got[:4], "first"))
    rows = lambda d: [d[n].reshape(1, sz) for n, sz in VECS]
    small_res = _reduce_adamw_small(got[4], rows(w), rows(m), rows(v))

    outs = []
    for k in range(4):
        piece = {n: r[k] for n, r in res.items()}
        for i, (n, _) in enumerate(VECS):
            piece[n] = small_res[k * len(VECS) + i].reshape(w[n].shape)
        outs.extend(piece[n] for n in names)
    return (small_res[-1][0, 0], dx[None], *outs)
```

```python
import jax
import jax.numpy as jnp
from jax import lax
from jax.experimental import pallas as pl
from jax.experimental.pallas import tpu as pltpu

F32 = jnp.float32
BF16 = jnp.bfloat16

D_MODEL = 1024
HEAD_DIM = 64
RW = 512
N_PAIR = RW // 128
SHIFT_COLS = 1792
IN_COLS = 3328
D_FF = 2816
FF_CHUNK = 256
NORM_EPS = 1e-6
GN_EPS = 64e-5
CHUNK = 64
SUB = 16
WKV_PASSES = 1
ATTN_PASSES = 1
ATTN_BLOCK = 128
DILATIONS = (1, 4, 16)
NEG = -1e30
ADAM_LR, ADAM_B1, ADAM_B2, ADAM_EPS, ADAM_WD, ADAM_STEP = 0.001, 0.9, 0.999, 1e-08, 0.01, 10
VMEM_LIMIT = 56 * 1024 * 1024
MESH = pl.DeviceIdType.MESH


def _params(sem=None, **kw):
    return pltpu.CompilerParams(dimension_semantics=sem, vmem_limit_bytes=VMEM_LIMIT, **kw)


def _dot(a, b, prec=None):
    return lax.dot_general(a, b, (((1,), (0,)), ((), ())), preferred_element_type=F32, precision=prec)


def _dot_nt(a, b, prec=None):
    return lax.dot_general(a, b, (((1,), (1,)), ((), ())), preferred_element_type=F32, precision=prec)


def _dot_tn(a, b, prec=None):
    return lax.dot_general(a, b, (((0,), (0,)), ((), ())), preferred_element_type=F32, precision=prec)


_FORMS = {"nn": ((1,), (0,)), "nt": ((1,), (1,)), "tn": ((0,), (0,))}


def _dg(a, b, form):
    if a.ndim == 3 or b.ndim == 3:
        nb = a.shape[0] if a.ndim == 3 else b.shape[0]
        return jnp.stack([_dg(a[i] if a.ndim == 3 else a, b[i] if b.ndim == 3 else b, form) for i in range(nb)], axis=0)
    return lax.dot_general(a, b, (_FORMS[form], ((), ())), preferred_element_type=F32)


def _split2(x):
    hi = x.astype(BF16)
    return hi, (x - hi.astype(F32)).astype(BF16)


def _split3(x):
    hi = x.astype(BF16)
    rest = x - hi.astype(F32)
    mid = rest.astype(BF16)
    return hi, mid, (rest - mid.astype(F32)).astype(BF16)


def _mm_raw(a, b, form, mode):
    if mode == 1:
        return _dg(a.astype(BF16), b.astype(BF16), form)
    if mode == 3:
        ah, al = _split2(a)
        bh, bl = _split2(b)
        return _dg(ah, bh, form) + (_dg(ah, bl, form) + _dg(al, bh, form))
    if mode == "L3":
        ab = a.astype(BF16)
        b1, b2, b3 = _split3(b)
        if form == "nn":
            n = b.shape[-1]
            wide = _dg(ab, jnp.concatenate([b1, b2, b3], axis=-1), form)
            return wide[..., :n] + (wide[..., n:2 * n] + wide[..., 2 * n:])
        return _dg(ab, b1, form) + (_dg(ab, b2, form) + _dg(ab, b3, form))
    assert mode == "R3", mode
    bb = b.astype(BF16)
    a1, a2, a3 = _split3(a)
    if form in ("nn", "nt"):
        m = a.shape[-2]
        tall = _dg(jnp.concatenate([a1, a2, a3], axis=-2), bb, form)
        return tall[..., :m, :] + (tall[..., m:2 * m, :] + tall[..., 2 * m:, :])
    return _dg(a1, bb, form) + (_dg(a2, bb, form) + _dg(a3, bb, form))


def _mm(a, b, form, mode):
    @jax.custom_vjp
    def f(a, b):
        return _mm_raw(a, b, form, mode)

    def fwd(a, b):
        return _mm_raw(a, b, form, mode), (a, b)

    def bwd(res, ct):
        a, b = res
        la = {1: 1, 3: 3, "L3": None, "R3": "R3"}[mode]
        lb = {1: 1, 3: 3, "L3": "L3", "R3": None}[mode]
        if form == "nn":
            da = None if la is None else _mm_raw(ct, b, "nt", la)
            db = None if lb is None else _mm_raw(a, ct, "tn", lb)
        elif form == "nt":
            da = None if la is None else _mm_raw(ct, b, "nn", la)
            db = None if lb is None else _mm_raw(ct, a, "tn", "R3" if lb == "L3" else lb)
        else:
            da = None if la is None else _mm_raw(b, ct, "nt", "L3" if la == "R3" else la)
            db = None if lb is None else _mm_raw(a, ct, "nn", lb)
        return (jnp.zeros_like(a) if da is None else da, jnp.zeros_like(b) if db is None else db)

    f.defvjp(fwd, bwd)
    return f(a, b)


def _seg_ones(n):
    r = lax.broadcasted_iota(jnp.int32, (n, n), 0) // HEAD_DIM
    c = lax.broadcasted_iota(jnp.int32, (n, n), 1) // HEAD_DIM
    return (r == c).astype(F32)


def _segsum(x, seg):
    return _mm(x, seg, "nn", "R3")


def _rms_fwd(x, g):
    rstd = lax.rsqrt(jnp.mean(x * x, axis=-1, keepdims=True) + NORM_EPS)
    return x * rstd * g


def _rms_bwd(dy, x, g):
    rstd = lax.rsqrt(jnp.mean(x * x, axis=-1, keepdims=True) + NORM_EPS)
    xn = x * rstd
    dxn = dy * g
    dx = rstd * (dxn - xn * jnp.mean(dxn * xn, axis=-1, keepdims=True))
    return dx, dy * xn


def _sigmoid(x):
    return 1.0 / (1.0 + jnp.exp(-x))


def _softplus(x):
    return jnp.maximum(x, 0.0) + jnp.log(1.0 + jnp.exp(-jnp.abs(x)))


def _acc(ref, val, first):
    @pl.when(first)
    def _():
        ref[...] = val

    @pl.when(jnp.logical_not(first))
    def _():
        ref[...] += val


def _colsum8(v):
    rows, n = v.shape
    return jnp.sum(v.reshape(rows // 8, 8, n), axis=0)


def _prep_fn(p, pprev, mu, w0, w2p, a0, a2p, g2, k_k, k_a):
    seg = _seg_ones(RW)
    ps = p + (pprev - p) * mu
    r = ps[:, 0:RW]
    k = ps[:, RW:2 * RW]
    v = ps[:, 2 * RW:3 * RW]
    xwa = ps[:, 3 * RW:3 * RW + 128]
    xg = ps[:, 3 * RW + 128:3 * RW + 256]
    wraw = -_softplus(-(w0 + _mm(jnp.tanh(xwa), w2p, "nn", 3))) - 0.5
    lw = -jnp.exp(wraw)
    a = _sigmoid(a0 + _mm(xwa, a2p, "nn", 3))
    g = _mm(_sigmoid(xg), g2, "nn", 3)
    kk = k * k_k
    kk = kk / jnp.maximum(jnp.sqrt(_segsum(kk * kk, seg)), 1e-12)
    k2 = k * (1.0 + (a - 1.0) * k_a)
    return r, lw, k2, v, kk, a, g


def _transposed(z):
    return jnp.stack([z[i].T for i in range(z.shape[0])], axis=0) if z.ndim == 3 else z.T


def _solve_unit_lower(lmat, rhs):
    c = lmat.shape[-1]
    row = lax.broadcasted_iota(jnp.int32, (c, c), 0)
    col = lax.broadcasted_iota(jnp.int32, (c, c), 1)
    eye = (row == col).astype(F32)
    ld = jnp.where(row // SUB == col // SUB, lmat, 0.0)
    lo = lmat - ld
    x = eye + ld
    m = ld
    mm = lambda p, q: _mm(p, q, "nn", WKV_PASSES)
    cat = jnp.concatenate
    m = mm(m, m)
    for _ in range(2):
        mx = mm(m, cat([m, x], axis=-1))
        m, x = mx[..., :c], x + mx[..., c:]
    x = x + mm(m, x)
    gw = mm(x, cat([lo, rhs], axis=-1))
    g, w = gw[..., :c], gw[..., c:]
    gg = mm(g, cat([g, w], axis=-1))
    w = w + gg[..., c:]
    return w + mm(gg[..., :c], w)


def _wkv_chunk_fn(s0, r, lw, k, v, kk, a):
    c = r.shape[-2]
    n = 2 * c
    row = lax.broadcasted_iota(jnp.int32, (n, n), 0)
    col = lax.broadcasted_iota(jnp.int32, (n, n), 1)
    same = (row // c) == (col // c)
    incl = jnp.logical_and(row >= col, same)
    strict = jnp.logical_and(row > col, same)
    sel = (lax.broadcasted_iota(jnp.int32, (n, 128), 0) // c) == (lax.broadcasted_iota(jnp.int32, (n, 128), 1) // HEAD_DIM)
    two = lambda z: jnp.concatenate([z, z], axis=-2)
    lw2 = two(lw)
    mm = lambda p_, q_, form: _mm(p_, q_, form, WKV_PASSES)
    cl = _mm(incl.astype(F32), lw2, "nn", "L3")
    p = jnp.exp(cl)
    pinv = jnp.exp(-cl)
    pprev = jnp.exp(cl - lw2)
    kk2 = two(kk)
    at = jnp.where(sel, -kk2 * pprev, 0.0)
    bt = jnp.where(sel, kk2 * two(a) * pinv, 0.0)
    kt = jnp.where(sel, two(k) * pinv, 0.0)
    rt = jnp.where(sel, two(r) * p, 0.0)
    vt = jnp.where(sel, two(v), 0.0)
    cat = jnp.concatenate
    bk = cat([bt, kt], axis=-2)
    arbk = mm(cat([at, rt], axis=-2), bk, "nt")
    ab, ak = jnp.where(strict, arbk[..., :n, :n], 0.0), jnp.where(strict, arbk[..., :n, n:], 0.0)
    rb, rk = jnp.where(incl, arbk[..., n:, :n], 0.0), jnp.where(incl, arbk[..., n:, n:], 0.0)
    s0t = _transposed(s0)
    u = _solve_unit_lower(ab, mm(cat([at, ak], axis=-1), cat([s0t, vt], axis=-2), "nn"))
    y2 = mm(cat([rt, rb, rk], axis=-1), cat([s0t, u, vt], axis=-2), "nn")
    plast = jnp.exp(jnp.sum(lw, axis=-2, keepdims=True))
    s1 = (s0 + mm(cat([u, vt], axis=-2), bk, "tn")) * plast
    r2 = lax.broadcasted_iota(jnp.int32, (128, 128), 0) // HEAD_DIM
    c2 = lax.broadcasted_iota(jnp.int32, (128, 128), 1) // HEAD_DIM
    return y2[..., :c, :] + y2[..., c:, :], jnp.where(r2 == c2, s1, 0.0)


def _post_fn(y, r, k2, v, g, lnw, lnb, rk):
    seg = _seg_ones(RW)
    mean = _segsum(y, seg) * (1.0 / HEAD_DIM)
    yc = y - mean
    var = _segsum(yc * yc, seg) * (1.0 / HEAD_DIM)
    yn = yc * lax.rsqrt(var + GN_EPS)
    out = yn * lnw + lnb + _segsum(r * k2 * rk, seg) * v
    return out * g


def _attn_block_fn(q, kc, vc, kp=None, vp=None):
    n = ATTN_BLOCK
    qi = lax.broadcasted_iota(jnp.int32, (n, n), 0)
    kj = lax.broadcasted_iota(jnp.int32, (n, n), 1)
    lane = lax.broadcasted_iota(jnp.int32, (1, 128), 1)
    scale = HEAD_DIM ** -0.5
    valid = kj <= qi
    keys, vals = kc, vc
    if kp is not None:
        valid = jnp.concatenate([valid, kj >= qi], axis=-1)
        keys, vals = jnp.concatenate([kc, kp], axis=-2), jnp.concatenate([vc, vp], axis=-2)
    m0 = (lane // HEAD_DIM) == 0
    q2 = jnp.concatenate([jnp.where(m0, q, 0.0), jnp.where(m0, 0.0, q)], axis=-2)
    valid2 = jnp.concatenate([valid, valid], axis=-2)
    s = jnp.where(valid2, _mm(q2, keys, "nt", ATTN_PASSES) * scale, NEG)
    m = jnp.max(s, axis=-1, keepdims=True)
    p = jnp.exp(s - m)
    den = jnp.sum(p, axis=-1, keepdims=True)
    o2 = _mm(p, vals, "nn", ATTN_PASSES) / den
    l2 = m + jnp.log(den)
    return jnp.where(m0, o2[..., :n, :], o2[..., n:, :]), jnp.where(m0, l2[..., :n, :], l2[..., n:, :])


def _attn_block_bwd(q, kc, vc, kp, vp, o, lse, do, dl):
    n = ATTN_BLOCK
    cat = jnp.concatenate
    qi = lax.broadcasted_iota(jnp.int32, (n, n), 0)
    kj = lax.broadcasted_iota(jnp.int32, (n, n), 1)
    m0 = (lax.broadcasted_iota(jnp.int32, (1, 128), 1) // HEAD_DIM) == 0
    scale = HEAD_DIM ** -0.5
    valid = kj <= qi
    keys, vals = kc, vc
    if kp is not None:
        valid = cat([valid, kj >= qi], axis=-1)
        keys, vals = cat([kc, kp], axis=-2), cat([vc, vp], axis=-2)
    stack = lambda z: cat([jnp.where(m0, z, 0.0), jnp.where(m0, 0.0, z)], axis=-2)
    q2, do2 = stack(q), stack(do)
    lse2 = cat([jnp.max(jnp.where(m0, lse, NEG), axis=-1, keepdims=True),
                jnp.max(jnp.where(m0, NEG, lse), axis=-1, keepdims=True)], axis=-2)
    delta = jnp.sum(do2 * cat([o, o], axis=-2), axis=-1, keepdims=True)
    dlse = jnp.sum(stack(dl), axis=-1, keepdims=True)
    mm = lambda a, b, form: _mm_raw(a, b, form, ATTN_PASSES)
    s = jnp.where(cat([valid, valid], axis=-2), mm(q2, keys, "nt") * scale, NEG)
    p = jnp.exp(s - lse2)
    ds = p * (mm(do2, vals, "nt") - delta + dlse)
    dq2 = mm(ds, keys, "nn") * scale
    dq = jnp.where(m0, dq2[..., :n, :], dq2[..., n:, :])
    dkeys = mm(ds, q2, "tn") * scale
    dvals = mm(p, do2, "tn")
    if kp is None:
        return dq, dkeys, dvals
    return dq, dkeys[..., :n, :], dvals[..., :n, :], dkeys[..., n:, :], dvals[..., n:, :]


def _combine_fn(o1, o2, o3, l1, l2, l3, og):
    seg = _seg_ones(o1.shape[-1])
    m = jnp.maximum(jnp.maximum(l1, l2), l3)
    e1, e2, e3 = jnp.exp(l1 - m), jnp.exp(l2 - m), jnp.exp(l3 - m)
    o = (e1 * o1 + e2 * o2 + e3 * o3) / (e1 + e2 + e3)
    o = o * lax.rsqrt(_segsum(o * o, seg) * (1.0 / HEAD_DIM) + NORM_EPS)
    return o * og


def _in_proj(x, g1, win):
    t = x.shape[0]
    tm = 512

    def body(x_ref, g_ref, w_ref, h_ref, pa_ref, qkv_ref):
        h = _rms_fwd(x_ref[...], g_ref[...]).astype(BF16)
        h_ref[...] = h
        proj = _dot_nt(h, w_ref[...])
        pa_ref[...] = proj[:, :SHIFT_COLS]
        for j in range(3):
            for p in range(N_PAIR):
                c0 = SHIFT_COLS + j * RW + p * 128
                qkv_ref[j, p] = proj[:, c0:c0 + 128]

    return pl.pallas_call(
        body, name="in_proj", grid=(t // tm,),
        in_specs=[pl.BlockSpec((tm, D_MODEL), lambda i: (i, 0)), pl.BlockSpec((1, D_MODEL), lambda i: (0, 0)),
                  pl.BlockSpec((IN_COLS, D_MODEL), lambda i: (0, 0))],
        out_specs=[pl.BlockSpec((tm, D_MODEL), lambda i: (i, 0)), pl.BlockSpec((tm, SHIFT_COLS), lambda i: (i, 0)),
                   pl.BlockSpec((3, N_PAIR, tm, 128), lambda i: (0, 0, i, 0))],
        out_shape=[jax.ShapeDtypeStruct((t, D_MODEL), BF16), jax.ShapeDtypeStruct((t, SHIFT_COLS), F32),
                   jax.ShapeDtypeStruct((3, N_PAIR, t, 128), F32)],
        compiler_params=_params(("parallel",)),
    )(x, g1, win)


def _shifted(p, last8, first):
    prow = jnp.where(first, 0.0, last8[7:8, :])
    rolled = pltpu.roll(p, 1, axis=0)
    rid = lax.broadcasted_iota(jnp.int32, p.shape, 0)
    return jnp.where(rid == 0, prow, rolled)


_PREP_TM = 256


def _prep_specs(tm):
    vec = lambda n: pl.BlockSpec((1, n), lambda i: (0, 0))
    mat = lambda r, n: pl.BlockSpec((r, n), lambda i: (0, 0))
    return [vec(SHIFT_COLS), vec(RW), mat(128, RW), vec(RW), mat(128, RW), mat(128, RW), vec(RW), vec(RW)]


def _prep_fwd(proj, pw):
    t = proj.shape[0]
    tm = _PREP_TM

    def body(p_ref, l8_ref, mu, w0, w2p, a0, a2p, g2, k_k, k_a, *outs):
        p = p_ref[...]
        pprev = _shifted(p, l8_ref[...], pl.program_id(0) == 0)
        res = _prep_fn(p, pprev, mu[...], w0[...], w2p[...], a0[...], a2p[...], g2[...], k_k[...], k_a[...])
        for o_ref, val in zip(outs, res):
            o_ref[...] = val

    row = pl.BlockSpec((tm, RW), lambda i: (i, 0))
    return pl.pallas_call(
        body, name="rwkv_prep", grid=(t // tm,),
        in_specs=[pl.BlockSpec((tm, SHIFT_COLS), lambda i: (i, 0)),
                  pl.BlockSpec((8, SHIFT_COLS), lambda i: (jnp.maximum(i * (tm // 8) - 1, 0), 0))] + _prep_specs(tm),
        out_specs=[row] * 7,
        out_shape=[jax.ShapeDtypeStruct((t, RW), F32)] * 7,
        compiler_params=_params(("parallel",)),
    )(proj, proj, *pw)


def _pairs(ref):
    return jnp.stack([ref[:, 128 * p:128 * (p + 1)] for p in range(N_PAIR)], axis=0)


def _wkv_fwd(r, lw, k2, v, kk, a):
    t = r.shape[0]
    nc = t // CHUNK

    def body(r_ref, lw_ref, k_ref, v_ref, kk_ref, a_ref, y_ref, s_ref, st):
        @pl.when(pl.program_id(0) == 0)
        def _():
            st[...] = jnp.zeros_like(st)

        s0 = st[...]
        s_ref[0] = s0
        y, s1 = _wkv_chunk_fn(s0, *[_pairs(ref) for ref in (r_ref, lw_ref, k_ref, v_ref, kk_ref, a_ref)])
        for p in range(N_PAIR):
            y_ref[:, 128 * p:128 * (p + 1)] = y[p]
        st[...] = s1

    blk = pl.BlockSpec((CHUNK, RW), lambda c: (c, 0))
    return pl.pallas_call(
        body, name="wkv_fwd", grid=(nc,),
        in_specs=[blk] * 6,
        out_specs=[blk, pl.BlockSpec((1, N_PAIR, 128, 128), lambda c: (c, 0, 0, 0))],
        out_shape=[jax.ShapeDtypeStruct((t, RW), F32), jax.ShapeDtypeStruct((nc, N_PAIR, 128, 128), F32)],
        scratch_shapes=[pltpu.VMEM((N_PAIR, 128, 128), F32)],
        compiler_params=_params(("arbitrary",)),
    )(r, lw, k2, v, kk, a)


_POST_TM = 256


def _post_fwd(y, r, k2, v, g, lnw, lnb, rk):
    t = y.shape[0]
    tm = _POST_TM

    def body(y_ref, r_ref, k_ref, v_ref, g_ref, lnw_ref, lnb_ref, rk_ref, o_ref):
        o_ref[...] = _post_fn(y_ref[...], r_ref[...], k_ref[...], v_ref[...], g_ref[...],
                              lnw_ref[...], lnb_ref[...], rk_ref[...]).astype(BF16)

    row = pl.BlockSpec((tm, RW), lambda i: (i, 0))
    vec = pl.BlockSpec((1, RW), lambda i: (0, 0))
    return pl.pallas_call(
        body, name="rwkv_post", grid=(t // tm,),
        in_specs=[row] * 5 + [vec] * 3, out_specs=row,
        out_shape=jax.ShapeDtypeStruct((t, RW), BF16),
        compiler_params=_params(("parallel",)),
    )(y, r, k2, v, g, lnw, lnb, rk)


ATTN_GROUP = 2


def _dilated_rows(d, r, n):
    if d == 1:
        return pl.ds(pl.multiple_of(n * ATTN_BLOCK, ATTN_BLOCK), ATTN_BLOCK)
    return pl.ds(r + n * (ATTN_BLOCK * d), ATTN_BLOCK, stride=d)


def _for_each_sequence(t, unit):
    for di, d in enumerate(DILATIONS):

        @pl.when(pl.program_id(1) == di)
        def _(di=di, d=d):
            nb = t // (ATTN_BLOCK * d)
            if d == 1:
                unit(di, [(d, 0, 0)], False)
                unit(di, [(d, 0, 1)], True)
                lax.fori_loop(1, nb // 2, lambda k, c: (unit(di, [(d, 0, 2 * k), (d, 0, 2 * k + 1)], True), c)[1], 0)
            else:

                def residues(r, carry):
                    unit(di, [(d, r, 0), (d, r + d // 2, 0)], False)
                    if nb > 1:
                        lax.fori_loop(1, nb, lambda n, c: (unit(di, [(d, r, n), (d, r + d // 2, n)], True), c)[1], 0)
                    return carry

                lax.fori_loop(0, d // 2, residues, 0)


def _take(ref, lead, rows_list):
    return jnp.stack([ref.at[(*lead, g)][rows, :] for rows in rows_list for g in range(ATTN_GROUP)], axis=0)


def _put(ref, lead, rows_list, val, add=False):
    k = 0
    for rows in rows_list:
        for g in range(ATTN_GROUP):
            if add:
                ref.at[(*lead, g)][rows, :] += val[k]
            else:
                ref.at[(*lead, g)][rows, :] = val[k]
            k += 1


def _attn_fwd(qkv):
    t = qkv.shape[2]

    def body(q_ref, k_ref, v_ref, o_ref, l_ref):
        def unit(di, places, has_prev):
            cur = [_dilated_rows(d, r, n) for d, r, n in places]
            args = [_take(ref, (0,), cur) for ref in (q_ref, k_ref, v_ref)]
            if has_prev:
                prv = [_dilated_rows(d, r, n - 1) for d, r, n in places]
                args += [_take(ref, (0,), prv) for ref in (k_ref, v_ref)]
            o, lse = _attn_block_fn(*args)
            _put(o_ref, (0,), cur, o)
            _put(l_ref, (0,), cur, lse)

        _for_each_sequence(t, unit)

    spec = lambda j: pl.BlockSpec((1, ATTN_GROUP, t, 128), lambda i, b: (j, i, 0, 0))
    out = pl.BlockSpec((1, ATTN_GROUP, t, 128), lambda i, b: (b, i, 0, 0))
    return pl.pallas_call(
        body, name="attn_fwd", grid=(N_PAIR // ATTN_GROUP, len(DILATIONS)),
        in_specs=[spec(0), spec(1), spec(2)], out_specs=[out, out],
        out_shape=[jax.ShapeDtypeStruct((3, N_PAIR, t, 128), F32)] * 2,
        compiler_params=_params(("parallel", "arbitrary")),
    )(qkv, qkv, qkv)


_COMB_TM = 256


def _combine_fwd(o, l, og):
    t = o.shape[2]
    tm = _COMB_TM

    def body(o_ref, l_ref, og_ref, y_ref):
        for p in range(N_PAIR):
            cols = slice(128 * p, 128 * (p + 1))
            y_ref[:, cols] = _combine_fn(o_ref[0, p], o_ref[1, p], o_ref[2, p], l_ref[0, p], l_ref[1, p], l_ref[2, p],
                                         og_ref[:, cols]).astype(BF16)

    blk = pl.BlockSpec((3, N_PAIR, tm, 128), lambda i: (0, 0, i, 0))
    return pl.pallas_call(
        body, name="attn_combine", grid=(t // tm,),
        in_specs=[blk, blk, pl.BlockSpec((1, RW), lambda i: (0, 0))], out_specs=pl.BlockSpec((tm, RW), lambda i: (i, 0)),
        out_shape=jax.ShapeDtypeStruct((t, RW), BF16),
        compiler_params=_params(("parallel",)),
    )(o, l, og)


def _out_proj(x, ycat, wout, g2):
    t = x.shape[0]
    tm = 256

    def body(x_ref, y_ref, w_ref, g_ref, x1_ref, h_ref):
        x1 = x_ref[...] + _dot(y_ref[...], w_ref[...])
        x1_ref[...] = x1
        h_ref[...] = _rms_fwd(x1, g_ref[...]).astype(BF16)

    row = pl.BlockSpec((tm, D_MODEL), lambda i: (i, 0))
    return pl.pallas_call(
        body, name="out_proj", grid=(t // tm,),
        in_specs=[row, row, pl.BlockSpec((D_MODEL, D_MODEL), lambda i: (0, 0)), pl.BlockSpec((1, D_MODEL), lambda i: (0, 0))],
        out_specs=[row, row],
        out_shape=[jax.ShapeDtypeStruct((t, D_MODEL), F32), jax.ShapeDtypeStruct((t, D_MODEL), BF16)],
        compiler_params=_params(("parallel",)),
    )(x, ycat, wout, g2)


def _ffn_all(x1, h2, wg, wu, wd, wout, g2, gf, tgt):
    t = x1.shape[0]
    tm = 256

    def body(x1_ref, h_ref, wg_ref, wu_ref, wd_ref, wo_ref, g2_ref, gf_ref, t_ref,
             act_ref, dx2b_ref, dgt_ref, dup_ref, dx1b_ref, dx1_ref, dya_ref, dyb_ref, loss_ref, dgf_ref, dg2_ref,
             gt_s, up_s):
        first = pl.program_id(0) == 0
        h = h_ref[...]
        for c0 in range(0, D_FF, FF_CHUNK):
            cols = slice(c0, c0 + FF_CHUNK)
            gt = _dot_nt(h, wg_ref[cols, :])
            up = _dot_nt(h, wu_ref[cols, :])
            gt_s[:, cols] = gt.astype(BF16)
            up_s[:, cols] = up.astype(BF16)
            act_ref[:, cols] = (gt * _sigmoid(gt) * up).astype(BF16)
        x1 = x1_ref[...]
        x2 = x1 + _dot(act_ref[...], wd_ref[...])
        gf_ = gf_ref[...]
        diff = _rms_fwd(x2, gf_) - t_ref[...]
        lrow = 0.5 * jnp.sum(_colsum8(diff * diff), axis=1, keepdims=True) * (1.0 / D_MODEL)
        _acc(loss_ref, jnp.broadcast_to(lrow, (8, 128)), first)
        dx2, dgr = _rms_bwd(diff * (1.0 / D_MODEL), x2, gf_)
        _acc(dgf_ref, _colsum8(dgr), first)
        dx2b = dx2.astype(BF16)
        dx2b_ref[...] = dx2b
        for c0 in range(0, D_FF, FF_CHUNK):
            cols = slice(c0, c0 + FF_CHUNK)
            dact = _dot_nt(dx2b, wd_ref[cols, :])
            gt = gt_s[:, cols].astype(F32)
            sg = _sigmoid(gt)
            dgt_ref[:, cols] = (dact * up_s[:, cols].astype(F32) * sg * (1.0 + gt * (1.0 - sg))).astype(BF16)
            dup_ref[:, cols] = (dact * gt * sg).astype(BF16)
        dh = _dot(dgt_ref[...], wg_ref[...]) + _dot(dup_ref[...], wu_ref[...])
        dxn, dgr2 = _rms_bwd(dh, x1, g2_ref[...])
        _acc(dg2_ref, _colsum8(dgr2), first)
        dx1 = dx2 + dxn
        dx1_ref[...] = dx1
        dx1b = dx1.astype(BF16)
        dx1b_ref[...] = dx1b
        dy = _dot_nt(dx1b, wo_ref[...])
        dya_ref[...] = dy[:, :RW]
        dyb_ref[...] = dy[:, RW:]

    row = pl.BlockSpec((tm, D_MODEL), lambda i: (i, 0))
    wide = pl.BlockSpec((tm, D_FF), lambda i: (i, 0))
    half = pl.BlockSpec((tm, RW), lambda i: (i, 0))
    wsp = pl.BlockSpec((D_FF, D_MODEL), lambda i: (0, 0))
    vec = pl.BlockSpec((1, D_MODEL), lambda i: (0, 0))
    part = pl.BlockSpec((8, D_MODEL), lambda i: (0, 0))
    bf = lambda n: jax.ShapeDtypeStruct((t, n), BF16)
    return pl.pallas_call(
        body, name="ffn_all", grid=(t // tm,),
        in_specs=[row, row, wsp, wsp, wsp, pl.BlockSpec((D_MODEL, D_MODEL), lambda i: (0, 0)), vec, vec, row],
        out_specs=[wide, row, wide, wide, row, row, half, half, pl.BlockSpec((8, 128), lambda i: (0, 0)), part, part],
        out_shape=[bf(D_FF), bf(D_MODEL), bf(D_FF), bf(D_FF), bf(D_MODEL), jax.ShapeDtypeStruct((t, D_MODEL), F32),
                   jax.ShapeDtypeStruct((t, RW), F32), jax.ShapeDtypeStruct((t, RW), F32),
                   jax.ShapeDtypeStruct((8, 128), F32), jax.ShapeDtypeStruct((8, D_MODEL), F32),
                   jax.ShapeDtypeStruct((8, D_MODEL), F32)],
        scratch_shapes=[pltpu.VMEM((tm, D_FF), BF16), pltpu.VMEM((tm, D_FF), BF16)],
        compiler_params=_params(("arbitrary",)),
    )(x1, h2, wg, wu, wd, wout, g2, gf, tgt)


def _wgrad(a, b, tk, tn, name):
    t, kdim = a.shape
    ndim = b.shape[1]

    def body(a_ref, b_ref, o_ref):
        o_ref[...] = _dot_tn(a_ref[...], b_ref[...])

    return pl.pallas_call(
        body, name=name, grid=(kdim // tk, ndim // tn),
        in_specs=[pl.BlockSpec((t, tk), lambda i, j: (0, i)), pl.BlockSpec((t, tn), lambda i, j: (0, j))],
        out_specs=pl.BlockSpec((tk, tn), lambda i, j: (i, j)),
        out_shape=jax.ShapeDtypeStruct((kdim, ndim), F32),
        compiler_params=_params(("parallel", "parallel")),
    )(a, b)


def _post_bwd(dya, y, r, k2, v, g, lnw, lnb, rk):
    t = y.shape[0]
    tm = _POST_TM

    def body(d_ref, y_ref, r_ref, k_ref, v_ref, g_ref, lnw_ref, lnb_ref, rk_ref,
             dy_ref, dr_ref, dk_ref, dv_ref, dg_ref, dlnw_ref, dlnb_ref, drk_ref):
        first = pl.program_id(0) == 0
        ones = jnp.ones((tm, 1), F32)
        prim = (y_ref[...], r_ref[...], k_ref[...], v_ref[...], g_ref[...],
                ones * lnw_ref[...], ones * lnb_ref[...], ones * rk_ref[...])
        _, vjp = jax.vjp(_post_fn, *prim)
        dy, dr, dk, dv, dg, dlnw, dlnb, drk = vjp(d_ref[...])
        dy_ref[...] = dy
        dr_ref[...] = dr
        dk_ref[...] = dk
        dv_ref[...] = dv
        dg_ref[...] = dg
        _acc(dlnw_ref, _colsum8(dlnw), first)
        _acc(dlnb_ref, _colsum8(dlnb), first)
        _acc(drk_ref, _colsum8(drk), first)

    row = pl.BlockSpec((tm, RW), lambda i: (i, 0))
    vec = pl.BlockSpec((1, RW), lambda i: (0, 0))
    part = pl.BlockSpec((8, RW), lambda i: (0, 0))
    return pl.pallas_call(
        body, name="rwkv_post_bwd", grid=(t // tm,),
        in_specs=[row] * 6 + [vec] * 3, out_specs=[row] * 5 + [part] * 3,
        out_shape=[jax.ShapeDtypeStruct((t, RW), F32)] * 5 + [jax.ShapeDtypeStruct((8, RW), F32)] * 3,
        compiler_params=_params(("arbitrary",)),
    )(dya, y, r, k2, v, g, lnw, lnb, rk)


def _wkv_bwd(dy, s0s, r, lw, k2, v, kk, a):
    t = r.shape[0]
    nc = t // CHUNK

    def body(dy_ref, s_ref, r_ref, lw_ref, k_ref, v_ref, kk_ref, a_ref,
             dr_ref, dlw_ref, dk_ref, dv_ref, dkk_ref, da_ref, ds):
        @pl.when(pl.program_id(0) == 0)
        def _():
            ds[...] = jnp.zeros_like(ds)

        _, vjp = jax.vjp(_wkv_chunk_fn, s_ref[0],
                         *[_pairs(ref) for ref in (r_ref, lw_ref, k_ref, v_ref, kk_ref, a_ref)])
        res = vjp((_pairs(dy_ref), ds[...]))
        ds[...] = res[0]
        for ref, val in zip((dr_ref, dlw_ref, dk_ref, dv_ref, dkk_ref, da_ref), res[1:]):
            for p in range(N_PAIR):
                ref[:, 128 * p:128 * (p + 1)] = val[p]

    blk = pl.BlockSpec((CHUNK, RW), lambda c: (nc - 1 - c, 0))
    return pl.pallas_call(
        body, name="wkv_bwd", grid=(nc,),
        in_specs=[blk, pl.BlockSpec((1, N_PAIR, 128, 128), lambda c: (nc - 1 - c, 0, 0, 0))] + [blk] * 6,
        out_specs=[blk] * 6,
        out_shape=[jax.ShapeDtypeStruct((t, RW), F32)] * 6,
        scratch_shapes=[pltpu.VMEM((N_PAIR, 128, 128), F32)],
        compiler_params=_params(("arbitrary",)),
    )(dy, s0s, r, lw, k2, v, kk, a)


def _prep_bwd(proj, pw, douts):
    t = proj.shape[0]
    tm = _PREP_TM
    nt = t // tm

    def body(p_ref, l8_ref, mu, w0, w2p, a0, a2p, g2, k_k, k_a, dr, dr2, dlw, dk2, dk22, dv, dv2, dkk, da, dg,
             dp_ref, dmu_ref, dw0_ref, dw2_ref, da0_ref, da2_ref, dg2_ref, dkk_ref, dka_ref, carry):
        i = pl.program_id(0)
        first = i == 0

        @pl.when(first)
        def _():
            carry[...] = jnp.zeros_like(carry)

        p = p_ref[...]
        pprev = _shifted(p, l8_ref[...], i == nt - 1)
        ones = jnp.ones((tm, 1), F32)
        prim = (p, pprev, ones * mu[...], ones * w0[...], w2p[...], ones * a0[...], a2p[...], g2[...],
                ones * k_k[...], ones * k_a[...])
        _, vjp = jax.vjp(_prep_fn, *prim)
        dp, dpp, dmu, dw0, dw2, da0, da2, dg2, dkk_, dka = vjp(
            (dr[...] + dr2[...], dlw[...], dk2[...] + dk22[...], dv[...] + dv2[...], dkk[...], da[...], dg[...]))
        up = pltpu.roll(dpp, tm - 1, axis=0)
        rid = lax.broadcasted_iota(jnp.int32, dpp.shape, 0)
        dp_ref[...] = dp + jnp.where(rid == tm - 1, carry[0:1, :], up)
        carry[...] = jnp.broadcast_to(dpp[0:1, :], carry.shape)
        _acc(dmu_ref, _colsum8(dmu), first)
        _acc(dw0_ref, _colsum8(dw0), first)
        _acc(dw2_ref, dw2, first)
        _acc(da0_ref, _colsum8(da0), first)
        _acc(da2_ref, da2, first)
        _acc(dg2_ref, dg2, first)
        _acc(dkk_ref, _colsum8(dkk_), first)
        _acc(dka_ref, _colsum8(dka), first)

    rev = lambda i: (nt - 1 - i, 0)
    row = pl.BlockSpec((tm, RW), rev)
    part = lambda n: pl.BlockSpec((8, n), lambda i: (0, 0))
    mat = pl.BlockSpec((128, RW), lambda i: (0, 0))
    return pl.pallas_call(
        body, name="rwkv_prep_bwd", grid=(nt,),
        in_specs=[pl.BlockSpec((tm, SHIFT_COLS), rev),
                  pl.BlockSpec((8, SHIFT_COLS), lambda i: (jnp.maximum((nt - 1 - i) * (tm // 8) - 1, 0), 0))]
                 + _prep_specs(tm) + [row] * 10,
        out_specs=[pl.BlockSpec((tm, SHIFT_COLS), rev), part(SHIFT_COLS), part(RW), mat, part(RW), mat, mat,
                   part(RW), part(RW)],
        out_shape=[jax.ShapeDtypeStruct((t, SHIFT_COLS), F32), jax.ShapeDtypeStruct((8, SHIFT_COLS), F32),
                   jax.ShapeDtypeStruct((8, RW), F32), jax.ShapeDtypeStruct((128, RW), F32),
                   jax.ShapeDtypeStruct((8, RW), F32), jax.ShapeDtypeStruct((128, RW), F32),
                   jax.ShapeDtypeStruct((128, RW), F32), jax.ShapeDtypeStruct((8, RW), F32),
                   jax.ShapeDtypeStruct((8, RW), F32)],
        scratch_shapes=[pltpu.VMEM((8, SHIFT_COLS), F32)],
        compiler_params=_params(("arbitrary",)),
    )(proj, proj, *pw, *douts)


def _combine_bwd(dyb, o, l, og):
    t = dyb.shape[0]
    tm = _COMB_TM

    def body(d_ref, o_ref, l_ref, og_ref, do_ref, dl_ref, dog_ref):
        ones = jnp.ones((tm, 1), F32)
        dog = []
        for p in range(N_PAIR):
            cols = slice(128 * p, 128 * (p + 1))
            _, vjp = jax.vjp(_combine_fn, o_ref[0, p], o_ref[1, p], o_ref[2, p], l_ref[0, p], l_ref[1, p], l_ref[2, p],
                             ones * og_ref[:, cols])
            res = vjp(d_ref[:, cols])
            for b in range(3):
                do_ref[b, p] = res[b]
                dl_ref[b, p] = res[3 + b]
            dog.append(_colsum8(res[6]))
        _acc(dog_ref, jnp.concatenate(dog, axis=1), pl.program_id(0) == 0)

    blk = pl.BlockSpec((3, N_PAIR, tm, 128), lambda i: (0, 0, i, 0))
    return pl.pallas_call(
        body, name="attn_combine_bwd", grid=(t // tm,),
        in_specs=[pl.BlockSpec((tm, RW), lambda i: (i, 0)), blk, blk, pl.BlockSpec((1, RW), lambda i: (0, 0))],
        out_specs=[blk, blk, pl.BlockSpec((8, RW), lambda i: (0, 0))],
        out_shape=[jax.ShapeDtypeStruct((3, N_PAIR, t, 128), F32)] * 2 + [jax.ShapeDtypeStruct((8, RW), F32)],
        compiler_params=_params(("arbitrary",)),
    )(dyb, o, l, og)


def _attn_bwd(do, dl, o, lse, qkv):
    t = qkv.shape[2]

    def body(do_ref, dl_ref, o_ref, l_ref, q_ref, k_ref, v_ref, dq_ref, dk_ref, dv_ref):
        @pl.when(pl.program_id(1) == 0)
        def _():
            for ref in (dq_ref, dk_ref, dv_ref):
                ref[...] = jnp.zeros_like(ref)

        def unit(di, places, has_prev):
            cur = [_dilated_rows(d, r, n) for d, r, n in places]
            q, kc, vc = [_take(ref, (0,), cur) for ref in (q_ref, k_ref, v_ref)]
            kp = vp = None
            if has_prev:
                prv = [_dilated_rows(d, r, n - 1) for d, r, n in places]
                kp, vp = [_take(ref, (0,), prv) for ref in (k_ref, v_ref)]
            res = _attn_block_bwd(q, kc, vc, kp, vp, *[_take(ref, (0,), cur) for ref in (o_ref, l_ref, do_ref, dl_ref)])
            _put(dq_ref, (), cur, res[0], add=True)
            _put(dk_ref, (), cur, res[1], add=True)
            _put(dv_ref, (), cur, res[2], add=True)
            if has_prev:
                _put(dk_ref, (), prv, res[3], add=True)
                _put(dv_ref, (), prv, res[4], add=True)

        _for_each_sequence(t, unit)

    spec = lambda j: pl.BlockSpec((1, ATTN_GROUP, t, 128), lambda i, b: (j, i, 0, 0))
    branch = pl.BlockSpec((1, ATTN_GROUP, t, 128), lambda i, b: (b, i, 0, 0))
    out = pl.BlockSpec((ATTN_GROUP, t, 128), lambda i, b: (i, 0, 0))
    return pl.pallas_call(
        body, name="attn_bwd", grid=(N_PAIR // ATTN_GROUP, len(DILATIONS)),
        in_specs=[branch] * 4 + [spec(0), spec(1), spec(2)], out_specs=[out] * 3,
        out_shape=[jax.ShapeDtypeStruct((N_PAIR, t, 128), F32)] * 3,
        compiler_params=_params(("parallel", "arbitrary")),
    )(do, dl, o, lse, qkv, qkv, qkv)


def _in_proj_bwd(dpa, dq, dk, dv, win, x, g1, dx1):
    t = x.shape[0]
    tm = 256

    def body(dpa_ref, dq_ref, dk_ref, dv_ref, w_ref, x_ref, g_ref, dx1_ref, dproj_ref, dx_ref, dg_ref):
        parts = [dpa_ref[...]] + [ref[p] for ref in (dq_ref, dk_ref, dv_ref) for p in range(N_PAIR)]
        dproj = jnp.concatenate([z.astype(BF16) for z in parts], axis=1)
        dproj_ref[...] = dproj
        dh = _dot(dproj, w_ref[...])
        dxn, dgr = _rms_bwd(dh, x_ref[...], g_ref[...])
        dx_ref[...] = dx1_ref[...] + dxn
        _acc(dg_ref, _colsum8(dgr), pl.program_id(0) == 0)

    row = pl.BlockSpec((tm, D_MODEL), lambda i: (i, 0))
    pair = pl.BlockSpec((N_PAIR, tm, 128), lambda i: (0, i, 0))
    return pl.pallas_call(
        body, name="in_proj_bwd", grid=(t // tm,),
        in_specs=[pl.BlockSpec((tm, SHIFT_COLS), lambda i: (i, 0))] + [pair] * 3
                 + [pl.BlockSpec((IN_COLS, D_MODEL), lambda i: (0, 0)), row, pl.BlockSpec((1, D_MODEL), lambda i: (0, 0)), row],
        out_specs=[pl.BlockSpec((tm, IN_COLS), lambda i: (i, 0)), row, pl.BlockSpec((8, D_MODEL), lambda i: (0, 0))],
        out_shape=[jax.ShapeDtypeStruct((t, IN_COLS), BF16), jax.ShapeDtypeStruct((t, D_MODEL), F32),
                   jax.ShapeDtypeStruct((8, D_MODEL), F32)],
        compiler_params=_params(("arbitrary",)),
    )(dpa, dq, dk, dv, win, x, g1, dx1)


def _pad_lora(w, lo):
    z = jnp.zeros((64, RW), F32)
    return jnp.concatenate([w, z], axis=0) if lo == 0 else jnp.concatenate([z, w], axis=0)


def _local_step(x, tgt, win, vecs, w2, a2, g2m, get_rest, send_rest):
    pw = (vecs["mu_shift"], vecs["decay_w0"], _pad_lora(w2, 0), vecs["iclr_a0"], _pad_lora(a2, 64), g2m,
          vecs["k_k"], vecs["k_a"])
    h, proj, qkv = _in_proj(x, vecs["mix_norm_g"], win)
    r, lw, k2, v, kk, a, g = _prep_fwd(proj, pw)
    y, s0s = _wkv_fwd(r, lw, k2, v, kk, a)
    ya = _post_fwd(y, r, k2, v, g, vecs["ln_x_w"], vecs["ln_x_b"], vecs["r_k"])
    o_att, l_att = _attn_fwd(qkv)
    yb = _combine_fwd(o_att, l_att, vecs["attn_out_g"])

    wout, wg, wu, wd = get_rest(yb)
    ycat = jnp.concatenate([ya, yb], axis=1)
    x1, h2 = _out_proj(x, ycat, wout, vecs["ffn_norm_g"])
    act, dx2b, dgt, dup, dx1b, dx1, dya, dyb, loss8, dgf, dg2n = _ffn_all(
        x1, h2, wg, wu, wd, wout, vecs["ffn_norm_g"], vecs["final_norm_g"], tgt)
    gw = {
        "w_down": _wgrad(act, dx2b, 1408, 1024, "wgrad_down"),
        "w_gate": _wgrad(dgt, h2, 1408, 1024, "wgrad_gate"),
        "w_up": _wgrad(dup, h2, 1408, 1024, "wgrad_up"),
        "w_out": _wgrad(ycat, dx1b, 1024, 1024, "wgrad_out"),
    }

    lnw = vecs["ln_x_w"] + send_rest(gw)[0, 0]
    dy, dr_p, dk2_p, dv_p, dg, dlnw, dlnb, drk = _post_bwd(dya, y, r, k2, v, g, lnw, vecs["ln_x_b"], vecs["r_k"])
    dr_s, dlw, dk2_s, dv_s, dkk, da = _wkv_bwd(dy, s0s, r, lw, k2, v, kk, a)
    dpa, dmu, dw0, dw2p, da0, da2p, dg2m, dk_k, dk_a = _prep_bwd(
        proj, pw, (dr_p, dr_s, dlw, dk2_p, dk2_s, dv_p, dv_s, dkk, da, dg))

    do_att, dl_att, dog = _combine_bwd(dyb, o_att, l_att, vecs["attn_out_g"])
    dq, dk, dv = _attn_bwd(do_att, dl_att, o_att, l_att, qkv)
    dproj, dx, dg1 = _in_proj_bwd(dpa, dq, dk, dv, win, x, vecs["mix_norm_g"], dx1)
    gw["w_in"] = _wgrad(dproj, h, 1664, 1024, "wgrad_in")
    gw["decay_w2"] = dw2p[:64]
    gw["iclr_a2"] = da2p[64:]
    gw["gate_g2"] = dg2m
    gv = {"mix_norm_g": dg1, "mu_shift": dmu, "decay_w0": dw0, "iclr_a0": da0, "k_k": dk_k, "k_a": dk_a, "r_k": drk,
          "ln_x_w": dlnw, "ln_x_b": dlnb, "attn_out_g": dog, "ffn_norm_g": dg2n, "final_norm_g": dgf}
    return loss8, dx, gw, gv


N_CHIP = 4
N_DEV = 8
MATS = ("w_in", "w_out", "w_gate", "w_up", "w_down")
LORAS = ("decay_w2", "iclr_a2", "gate_g2")
VECS = (("mix_norm_g", 1024), ("mu_shift", 1792), ("decay_w0", 512), ("iclr_a0", 512), ("k_k", 512), ("k_a", 512),
        ("r_k", 512), ("ln_x_w", 512), ("ln_x_b", 512), ("attn_out_g", 512), ("ffn_norm_g", 1024),
        ("final_norm_g", 1024))
N_VEC = sum(n for _, n in VECS)
N_SMALL = N_VEC + 128
ANY = pl.BlockSpec(memory_space=pl.ANY)


def _flip(v, f):
    return 1 - v if f else v


class _Me:
    def __init__(self, mode):
        x, y, c = lax.axis_index("x"), lax.axis_index("y"), lax.axis_index("c")
        self.core, self.chip, self.dev = c, 2 * x + y, 4 * x + 2 * y + c
        self.sibling = (x, y, 1 - c)
        if mode == "chips":
            self.peers = [(px, py, c) for px, py in ((1 - x, y), (x, 1 - y), (1 - x, 1 - y))]
        else:
            self.peers = [(_flip(x, k & 4), _flip(y, k & 2), _flip(c, k & 1)) for k in range(1, N_DEV)]


def _half(core, rows):
    h = rows // 2
    return pl.ds(pl.multiple_of(core * h, h), h)


def _peer_copy(srcs, dsts, kinds, send_sems, recv_sems, me, j, i, incoming):
    px, py, pc = me.peers[j]
    pchip, pdev = 2 * px + py, 4 * px + 2 * py + pc
    src, dst, kind = srcs[i], dsts[i], kinds[i]
    if kind == "gather":
        rows = _half(me.core, src.shape[0])
        src, dst = src.at[rows], dst.at[pchip if incoming else me.chip, rows]
    elif kind == "scatter":
        src, dst = src.at[pchip, _half(pc, src.shape[1])], dst.at[pdev if incoming else me.dev]
    else:
        dst = dst.at[pdev if incoming else me.dev]
    n = len(srcs)
    return pltpu.make_async_remote_copy(src_ref=src, dst_ref=dst, send_sem=send_sems.at[n * j + i],
                                        recv_sem=recv_sems.at[n * j + i], device_id=(px, py, pc), device_id_type=MESH)


def _mode(kinds):
    return "chips" if kinds[0] == "gather" else "devs"


def _npeer(kinds):
    return N_CHIP - 1 if kinds[0] == "gather" else N_DEV - 1


def _swap_gathered(lands, name):
    n = len(lands)

    def body(*refs):
        dsts, send_sems, recv_sems = refs[n:2 * n], refs[2 * n], refs[2 * n + 1]
        me = _Me("chips")

        def copy(j, i, incoming):
            px, py, _ = me.peers[j]
            rows_out, rows_in = _half(me.core, dsts[i].shape[1]), _half(1 - me.core, dsts[i].shape[1])
            return pltpu.make_async_remote_copy(
                src_ref=dsts[i].at[2 * px + py, rows_out], dst_ref=dsts[i].at[2 * px + py, rows_in if incoming else rows_out],
                send_sem=send_sems.at[n * j + i], recv_sem=recv_sems.at[n * j + i], device_id=me.sibling, device_id_type=MESH)

        sends = [copy(j, i, False) for j in range(3) for i in range(n)]
        for cp in sends:
            cp.start()
        for j in range(3):
            for i in range(n):
                copy(j, i, True).wait_recv()
        for cp in sends:
            cp.wait_send()

    return pl.pallas_call(
        body, name=name, in_specs=[ANY] * n, out_specs=[ANY] * n,
        out_shape=[jax.ShapeDtypeStruct(l.shape, l.dtype) for l in lands],
        input_output_aliases={i: i for i in range(n)},
        scratch_shapes=[pltpu.SemaphoreType.DMA((3 * n,)), pltpu.SemaphoreType.DMA((3 * n,))],
    )(*lands)


def _join_halves(sums, name):
    n = len(sums)

    def body(*refs):
        dsts, send_sems, recv_sems = refs[n:2 * n], refs[2 * n], refs[2 * n + 1]
        me = _Me("chips")

        def copy(i, incoming):
            mine, other = _half(me.core, dsts[i].shape[0]), _half(1 - me.core, dsts[i].shape[0])
            return pltpu.make_async_remote_copy(src_ref=dsts[i].at[mine], dst_ref=dsts[i].at[other if incoming else mine],
                                                send_sem=send_sems.at[i], recv_sem=recv_sems.at[i],
                                                device_id=me.sibling, device_id_type=MESH)

        sends = [copy(i, False) for i in range(n)]
        for cp in sends:
            cp.start()
        for i in range(n):
            copy(i, True).wait_recv()
        for cp in sends:
            cp.wait_send()

    return pl.pallas_call(
        body, name=name, in_specs=[ANY] * n, out_specs=[ANY] * n,
        out_shape=[jax.ShapeDtypeStruct(s.shape, s.dtype) for s in sums],
        input_output_aliases={i: i for i in range(n)},
        scratch_shapes=[pltpu.SemaphoreType.DMA((n,)), pltpu.SemaphoreType.DMA((n,))],
    )(*sums)


HBM = pl.BlockSpec(memory_space=pltpu.HBM)
SEM = pl.BlockSpec(memory_space=pltpu.SEMAPHORE)
EFFECT = pltpu.SideEffectType.DATAFLOW_SIDE_EFFECTING


def _swap_start(arrs, lands, kinds, name):
    n = len(arrs)

    def body(*refs):
        srcs, dsts, send_sems, recv_sems, token = refs[:n], refs[n:2 * n], refs[2 * n], refs[2 * n + 1], refs[-1]
        me = _Me(_mode(kinds))
        for j in range(len(me.peers)):
            for i in range(n):
                _peer_copy(srcs, dsts, kinds, send_sems, recv_sems, me, j, i, False).start()
        token[...] = jnp.zeros_like(token)

    ns = _npeer(kinds) * n
    outs = pl.pallas_call(
        body, name=name,
        out_shape=(pltpu.SemaphoreType.DMA((ns,)), pltpu.SemaphoreType.DMA((ns,)),
                   *[pltpu.HBM(a.shape, a.dtype) for a in arrs], *[pltpu.HBM(l.shape, l.dtype) for l in lands],
                   jax.ShapeDtypeStruct((8, 128), F32)),
        in_specs=[HBM] * (2 * n), out_specs=(SEM, SEM, *[HBM] * (2 * n), pl.BlockSpec(memory_space=pltpu.VMEM)),
        input_output_aliases={k: 2 + k for k in range(2 * n)},
        compiler_params=pltpu.CompilerParams(has_side_effects=EFFECT),
    )(*[pltpu.with_memory_space_constraint(a, pltpu.HBM) for a in arrs],
      *[pltpu.with_memory_space_constraint(l, pltpu.HBM) for l in lands])
    return outs[0], outs[1], outs[2:2 + n], outs[2 + n:2 + 2 * n], outs[-1]


def _swap_wait(send_sems, recv_sems, srcs_thru, lands_thru, after, kinds, name):
    n = len(srcs_thru)

    def body(*refs):
        srcs, dsts, s_sems, r_sems = refs[:n], refs[n:2 * n], refs[2 * n], refs[2 * n + 1]
        me = _Me(_mode(kinds))
        for j in range(len(me.peers)):
            for i in range(n):
                cp = _peer_copy(srcs, dsts, kinds, s_sems, r_sems, me, j, i, True)
                cp.wait_send()
                cp.wait_recv()

    outs = pl.pallas_call(
        body, name=name,
        out_shape=tuple(pltpu.HBM(a.shape, a.dtype) for a in (*srcs_thru, *lands_thru)),
        in_specs=[HBM] * (2 * n) + [SEM, SEM, ANY], out_specs=tuple([HBM] * (2 * n)),
        input_output_aliases={k: k for k in range(2 * n)},
        compiler_params=pltpu.CompilerParams(has_side_effects=EFFECT),
    )(*srcs_thru, *lands_thru, send_sems, recv_sems, after)
    return outs[n:]


def _adamw(w, g, m, v):
    m = ADAM_B1 * m + (1.0 - ADAM_B1) * g
    v = ADAM_B2 * v + (1.0 - ADAM_B2) * (g * g)
    m_hat = m / (1.0 - ADAM_B1 ** ADAM_STEP)
    v_hat = v / (1.0 - ADAM_B2 ** ADAM_STEP)
    delta = -ADAM_LR * (m_hat / (jnp.sqrt(v_hat) + ADAM_EPS) + ADAM_WD * w)
    return delta, m, v


def _reduce8(rbuf, core, tr, name):
    _, h, cols = rbuf.shape

    def body(core_ref, r_ref, g_ref):
        g = r_ref[0].astype(F32)
        for s in range(1, N_DEV):
            g = g + r_ref[s].astype(F32)
        g_ref[...] = g

    return pl.pallas_call(
        body, name=name,
        grid_spec=pltpu.PrefetchScalarGridSpec(
            num_scalar_prefetch=1, grid=(h // tr,),
            in_specs=[pl.BlockSpec((N_DEV, tr, cols), lambda i, core_ref: (0, i, 0))],
            out_specs=pl.BlockSpec((tr, cols), lambda i, core_ref: (core_ref[0] * (h // tr) + i, 0))),
        out_shape=jax.ShapeDtypeStruct((2 * h, cols), F32),
        compiler_params=_params(("parallel",)),
    )(core, rbuf)


def _adamw_call(g, w, m, v, tr, name):
    _, rows, cols = w.shape

    def body(g_in, w_ref, m_ref, v_ref, g_ref, d_ref, nm_ref, nv_ref):
        g = g_in[...]
        g_ref[0] = g
        d_ref[0], nm_ref[0], nv_ref[0] = _adamw(w_ref[0], g, m_ref[0], v_ref[0])

    row = pl.BlockSpec((1, tr, cols), lambda i: (0, i, 0))
    return pl.pallas_call(
        body, name=name, grid=(rows // tr,),
        in_specs=[pl.BlockSpec((tr, cols), lambda i: (i, 0)), row, row, row], out_specs=[row] * 4,
        out_shape=[jax.ShapeDtypeStruct(w.shape, F32)] * 4,
        compiler_params=_params(("parallel",)),
    )(g, w, m, v)


def _rowsum_small(parts, loss8):
    def body(*refs):
        out = refs[-1]
        c0 = 0
        for ref in refs[:-1]:
            n = ref.shape[1]
            out[:, c0:c0 + n] = jnp.sum(ref[...], axis=0, keepdims=True)
            c0 += n

    return pl.pallas_call(body, name="rowsum_small", out_shape=jax.ShapeDtypeStruct((1, N_SMALL), F32))(*parts, loss8)


def _reduce_adamw_small(sbuf, ws, ms, vs):
    nv = len(ws)

    def body(*refs):
        s_ref, ins, outs = refs[0], refs[1:1 + 3 * nv], refs[1 + 3 * nv:]
        tot = s_ref[0]
        for s in range(1, N_DEV):
            tot = tot + s_ref[s]
        c0 = 0
        for i in range(nv):
            n = ins[i].shape[1]
            g = tot[:, c0:c0 + n]
            outs[i][...] = g
            outs[nv + i][...], outs[2 * nv + i][...], outs[3 * nv + i][...] = _adamw(
                ins[i][...], g, ins[nv + i][...], ins[2 * nv + i][...])
            c0 += n
        outs[-1][...] = tot[:, c0:]

    return pl.pallas_call(
        body, name="reduce_adamw_small",
        out_shape=[jax.ShapeDtypeStruct(a.shape, F32) for a in ws] * 4 + [jax.ShapeDtypeStruct((1, 128), F32)],
    )(sbuf, *ws, *ms, *vs)


_TRANSPOSED = ("w_in", "w_gate", "w_up")
_ROW_STACKED = MATS
_ADAM_TILE = {"w_in": 208, "w_out": 256, "w_gate": 176, "w_up": 176, "w_down": 176, "decay_w2": 64, "iclr_a2": 64,
              "gate_g2": 128}
_SUM_TILE = {"w_in": 208, "w_out": 128, "w_gate": 176, "w_up": 176, "w_down": 176, "decay_w2": 32, "iclr_a2": 32,
             "gate_g2": 64}


def _full(n, stacked):
    p, r, c = stacked.shape
    if n in _ROW_STACKED:
        return stacked.reshape(p * r, c)
    return jnp.transpose(stacked, (1, 0, 2)).reshape(r, p * c)


def _by_chip(n, full):
    if n in _ROW_STACKED:
        return full.reshape(N_CHIP, full.shape[0] // N_CHIP, full.shape[1])
    r, c = full.shape
    return jnp.transpose(full.reshape(r, N_CHIP, c // N_CHIP), (1, 0, 2))


def _with_own(land_shape, dtype, own, slot):
    return lax.dynamic_update_slice(lax.empty(land_shape, dtype), own[None], (slot,) + (0,) * own.ndim)


def kernel(x, mix_norm_g, w_in, mu_shift, decay_w0, decay_w2, iclr_a0, iclr_a2, gate_g2, k_k, k_a, r_k, ln_x_w, ln_x_b, attn_out_g, w_out, ffn_norm_g, w_gate, w_up, w_down, final_norm_g, loss_target, m_mix_norm_g, m_w_in, m_mu_shift, m_decay_w0, m_decay_w2, m_iclr_a0, m_iclr_a2, m_gate_g2, m_k_k, m_k_a, m_r_k, m_ln_x_w, m_ln_x_b, m_attn_out_g, m_w_out, m_ffn_norm_g, m_w_gate, m_w_up, m_w_down, m_final_norm_g, v_mix_norm_g, v_w_in, v_mu_shift, v_decay_w0, v_decay_w2, v_iclr_a0, v_iclr_a2, v_gate_g2, v_k_k, v_k_a, v_r_k, v_ln_x_w, v_ln_x_b, v_attn_out_g, v_w_out, v_ffn_norm_g, v_w_gate, v_w_up, v_w_down, v_final_norm_g):
    names = ("mix_norm_g", "w_in", "mu_shift", "decay_w0", "decay_w2", "iclr_a0", "iclr_a2", "gate_g2", "k_k", "k_a",
             "r_k", "ln_x_w", "ln_x_b", "attn_out_g", "w_out", "ffn_norm_g", "w_gate", "w_up", "w_down", "final_norm_g")
    w = dict(zip(names, (mix_norm_g, w_in, mu_shift, decay_w0, decay_w2, iclr_a0, iclr_a2, gate_g2, k_k, k_a, r_k,
                         ln_x_w, ln_x_b, attn_out_g, w_out, ffn_norm_g, w_gate, w_up, w_down, final_norm_g)))
    m = dict(zip(names, (m_mix_norm_g, m_w_in, m_mu_shift, m_decay_w0, m_decay_w2, m_iclr_a0, m_iclr_a2, m_gate_g2,
                         m_k_k, m_k_a, m_r_k, m_ln_x_w, m_ln_x_b, m_attn_out_g, m_w_out, m_ffn_norm_g, m_w_gate,
                         m_w_up, m_w_down, m_final_norm_g)))
    v = dict(zip(names, (v_mix_norm_g, v_w_in, v_mu_shift, v_decay_w0, v_decay_w2, v_iclr_a0, v_iclr_a2, v_gate_g2,
                         v_k_k, v_k_a, v_r_k, v_ln_x_w, v_ln_x_b, v_attn_out_g, v_w_out, v_ffn_norm_g, v_w_gate,
                         v_w_up, v_w_down, v_final_norm_g)))
    first = ("w_in",) + LORAS
    rest = ("w_out", "w_gate", "w_up", "w_down")
    xi, yi, ci = lax.axis_index("x"), lax.axis_index("y"), lax.axis_index("c")
    my_chip, my_dev = 2 * xi + yi, 4 * xi + 2 * yi + ci
    gather, scatter = ("gather",) * 4, ("scatter",) * 4

    sh = lambda z, n: jnp.transpose(z[0]) if n in _TRANSPOSED else z[0]
    mine = [sh(w["w_in"], "w_in").astype(BF16)] + [w[n][0] for n in LORAS]
    early = _swap_start(mine, [_with_own((N_CHIP,) + a.shape, a.dtype, a, my_chip) for a in mine], gather, "gather_first_start")
    wb = {n: (sh(w[n], n) + early[4][0, 0]).astype(BF16) for n in rest}
    lands = [_with_own((N_CHIP,) + wb[n].shape, BF16, wb[n], my_chip) for n in rest]
    ssem, rsem, srcs_thru, lands_thru, tok = _swap_start([wb[n] for n in rest], lands, gather, "gather_rest_start")
    got = _swap_wait(early[0], early[1], early[2], early[3], tok, gather, "gather_first_wait")
    win, w2, a2, g2m = (_full(n, z) for n, z in zip(first, _swap_gathered(got, "gather_first_halves")))

    vecs = {n: w[n].reshape(1, sz) for n, sz in VECS}
    vecs["mix_norm_g"] = vecs["mix_norm_g"] + tok[0, 0]

    def get_rest(after):
        halves = _swap_wait(ssem, rsem, srcs_thru, lands_thru, after, gather, "gather_rest_wait")
        return [_full(n, z) for n, z in zip(rest, _swap_gathered(halves, "gather_rest_halves"))]

    flight = []

    def my_half(g):
        h = g.shape[1] // 2
        return lax.dynamic_slice(g, (my_chip, ci * h, 0), (1, h, g.shape[2]))[0]

    def send_rest(gw):
        gs = [_by_chip(n, gw[n]).astype(BF16) for n in rest]
        into = [_with_own((N_DEV,) + my_half(g).shape, BF16, my_half(g), my_dev) for g in gs]
        flight.extend(_swap_start(gs, into, scatter, "exchange_rest_start"))
        return flight[4]

    loss8, dx, gw, gv = _local_step(x[0], loss_target[0], win, vecs, w2, a2, g2m, get_rest, send_rest)

    small = _rowsum_small([gv[n] for n, _ in VECS], loss8)
    gs = [_by_chip(n, gw[n]).astype(BF16) for n in first]
    into = [_with_own((N_DEV,) + my_half(g).shape, BF16, my_half(g), my_dev) for g in gs]
    into.append(_with_own((N_DEV,) + small.shape, F32, small, my_dev))
    last = _swap_start(gs + [small], into, scatter + ("all",), "exchange_first_start")

    core = jnp.reshape(ci, (1,)).astype(jnp.int32)

    def update(group, rbufs, tag):
        sums = [_reduce8(rb, core, _SUM_TILE[n], "reduce_" + n) for n, rb in zip(group, rbufs)]
        gsum = _join_halves(sums, "join_halves_" + tag)
        out = {}
        for n, g in zip(group, gsum):
            r = _adamw_call(g, sh(w[n], n)[None], sh(m[n], n)[None], sh(v[n], n)[None], _ADAM_TILE[n], "adamw_" + n)
            out[n] = [jnp.transpose(z[0])[None] for z in r] if n in _TRANSPOSED else r
        return out

    res = update(rest, _swap_wait(flight[0], flight[1], flight[2], flight[3], last[4], scatter, "exchange_rest_wait"), "rest")
    got = _swap_wait(last[0], last[1], last[2], last[3], res["w_down"][1], scatter + ("all",), "exchange_first_wait")
    res.update(update(first, got[:4], "first"))
    rows = lambda d: [d[n].reshape(1, sz) for n, sz in VECS]
    small_res = _reduce_adamw_small(got[4], rows(w), rows(m), rows(v))

    outs = []
    for k in range(4):
        piece = {n: r[k] for n, r in res.items()}
        for i, (n, _) in enumerate(VECS):
            piece[n] = small_res[k * len(VECS) + i].reshape(w[n].shape)
        outs.extend(piece[n] for n in names)
    return (small_res[-1][0, 0], dx[None], *outs)
```

```python
import jax
import jax.numpy as jnp
from jax import lax
from jax.experimental import pallas as pl
from jax.experimental.pallas import tpu as pltpu

F32 = jnp.float32
BF16 = jnp.bfloat16

D_MODEL = 1024
HEAD_DIM = 64
RW = 512
N_PAIR = RW // 128
SHIFT_COLS = 1792
IN_COLS = 3328
D_FF = 2816
FF_CHUNK = 256
NORM_EPS = 1e-6
GN_EPS = 64e-5
CHUNK = 64
SUB = 16
WKV_PASSES = 1
ATTN_PASSES = 1
ATTN_BLOCK = 128
DILATIONS = (1, 4, 16)
NEG = -1e30
ADAM_LR, ADAM_B1, ADAM_B2, ADAM_EPS, ADAM_WD, ADAM_STEP = 0.001, 0.9, 0.999, 1e-08, 0.01, 10
VMEM_LIMIT = 56 * 1024 * 1024
MESH = pl.DeviceIdType.MESH


def _params(sem=None, **kw):
    return pltpu.CompilerParams(dimension_semantics=sem, vmem_limit_bytes=VMEM_LIMIT, **kw)


def _dot(a, b, prec=None):
    return lax.dot_general(a, b, (((1,), (0,)), ((), ())), preferred_element_type=F32, precision=prec)


def _dot_nt(a, b, prec=None):
    return lax.dot_general(a, b, (((1,), (1,)), ((), ())), preferred_element_type=F32, precision=prec)


def _dot_tn(a, b, prec=None):
    return lax.dot_general(a, b, (((0,), (0,)), ((), ())), preferred_element_type=F32, precision=prec)


_FORMS = {"nn": ((1,), (0,)), "nt": ((1,), (1,)), "tn": ((0,), (0,))}


def _dg(a, b, form):
    if a.ndim == 3 or b.ndim == 3:
        nb = a.shape[0] if a.ndim == 3 else b.shape[0]
        return jnp.stack([_dg(a[i] if a.ndim == 3 else a, b[i] if b.ndim == 3 else b, form) for i in range(nb)], axis=0)
    return lax.dot_general(a, b, (_FORMS[form], ((), ())), preferred_element_type=F32)


def _split2(x):
    hi = x.astype(BF16)
    return hi, (x - hi.astype(F32)).astype(BF16)


def _split3(x):
    hi = x.astype(BF16)
    rest = x - hi.astype(F32)
    mid = rest.astype(BF16)
    return hi, mid, (rest - mid.astype(F32)).astype(BF16)


def _mm_raw(a, b, form, mode):
    if mode == 1:
        return _dg(a.astype(BF16), b.astype(BF16), form)
    if mode == 3:
        ah, al = _split2(a)
        bh, bl = _split2(b)
        return _dg(ah, bh, form) + (_dg(ah, bl, form) + _dg(al, bh, form))
    if mode == "L3":
        ab = a.astype(BF16)
        b1, b2, b3 = _split3(b)
        if form == "nn":
            n = b.shape[-1]
            wide = _dg(ab, jnp.concatenate([b1, b2, b3], axis=-1), form)
            return wide[..., :n] + (wide[..., n:2 * n] + wide[..., 2 * n:])
        return _dg(ab, b1, form) + (_dg(ab, b2, form) + _dg(ab, b3, form))
    assert mode == "R3", mode
    bb = b.astype(BF16)
    a1, a2, a3 = _split3(a)
    if form in ("nn", "nt"):
        m = a.shape[-2]
        tall = _dg(jnp.concatenate([a1, a2, a3], axis=-2), bb, form)
        return tall[..., :m, :] + (tall[..., m:2 * m, :] + tall[..., 2 * m:, :])
    return _dg(a1, bb, form) + (_dg(a2, bb, form) + _dg(a3, bb, form))


def _mm(a, b, form, mode):
    @jax.custom_vjp
    def f(a, b):
        return _mm_raw(a, b, form, mode)

    def fwd(a, b):
        return _mm_raw(a, b, form, mode), (a, b)

    def bwd(res, ct):
        a, b = res
        la = {1: 1, 3: 3, "L3": None, "R3": "R3"}[mode]
        lb = {1: 1, 3: 3, "L3": "L3", "R3": None}[mode]
        if form == "nn":
            da = None if la is None else _mm_raw(ct, b, "nt", la)
            db = None if lb is None else _mm_raw(a, ct, "tn", lb)
        elif form == "nt":
            da = None if la is None else _mm_raw(ct, b, "nn", la)
            db = None if lb is None else _mm_raw(ct, a, "tn", "R3" if lb == "L3" else lb)
        else:
            da = None if la is None else _mm_raw(b, ct, "nt", "L3" if la == "R3" else la)
            db = None if lb is None else _mm_raw(a, ct, "nn", lb)
        return (jnp.zeros_like(a) if da is None else da, jnp.zeros_like(b) if db is None else db)

    f.defvjp(fwd, bwd)
    return f(a, b)


def _seg_ones(n):
    r = lax.broadcasted_iota(jnp.int32, (n, n), 0) // HEAD_DIM
    c = lax.broadcasted_iota(jnp.int32, (n, n), 1) // HEAD_DIM
    return (r == c).astype(F32)


def _segsum(x, seg):
    return _mm(x, seg, "nn", "R3")


def _rms_fwd(x, g):
    rstd = lax.rsqrt(jnp.mean(x * x, axis=-1, keepdims=True) + NORM_EPS)
    return x * rstd * g


def _rms_bwd(dy, x, g):
    rstd = lax.rsqrt(jnp.mean(x * x, axis=-1, keepdims=True) + NORM_EPS)
    xn = x * rstd
    dxn = dy * g
    dx = rstd * (dxn - xn * jnp.mean(dxn * xn, axis=-1, keepdims=True))
    return dx, dy * xn


def _sigmoid(x):
    return 1.0 / (1.0 + jnp.exp(-x))


def _softplus(x):
    return jnp.maximum(x, 0.0) + jnp.log(1.0 + jnp.exp(-jnp.abs(x)))


def _acc(ref, val, first):
    @pl.when(first)
    def _():
        ref[...] = val

    @pl.when(jnp.logical_not(first))
    def _():
        ref[...] += val


def _colsum8(v):
    rows, n = v.shape
    return jnp.sum(v.reshape(rows // 8, 8, n), axis=0)


def _prep_fn(p, pprev, mu, w0, w2p, a0, a2p, g2, k_k, k_a):
    seg = _seg_ones(RW)
    ps = p + (pprev - p) * mu
    r = ps[:, 0:RW]
    k = ps[:, RW:2 * RW]
    v = ps[:, 2 * RW:3 * RW]
    xwa = ps[:, 3 * RW:3 * RW + 128]
    xg = ps[:, 3 * RW + 128:3 * RW + 256]
    wraw = -_softplus(-(w0 + _mm(jnp.tanh(xwa), w2p, "nn", 3))) - 0.5
    lw = -jnp.exp(wraw)
    a = _sigmoid(a0 + _mm(xwa, a2p, "nn", 3))
    g = _mm(_sigmoid(xg), g2, "nn", 3)
    kk = k * k_k
    kk = kk / jnp.maximum(jnp.sqrt(_segsum(kk * kk, seg)), 1e-12)
    k2 = k * (1.0 + (a - 1.0) * k_a)
    return r, lw, k2, v, kk, a, g


def _transposed(z):
    return jnp.stack([z[i].T for i in range(z.shape[0])], axis=0) if z.ndim == 3 else z.T


def _solve_unit_lower(lmat, rhs):
    c = lmat.shape[-1]
    row = lax.broadcasted_iota(jnp.int32, (c, c), 0)
    col = lax.broadcasted_iota(jnp.int32, (c, c), 1)
    eye = (row == col).astype(F32)
    ld = jnp.where(row // SUB == col // SUB, lmat, 0.0)
    lo = lmat - ld
    x = eye + ld
    m = ld
    mm = lambda p, q: _mm(p, q, "nn", WKV_PASSES)
    cat = jnp.concatenate
    m = mm(m, m)
    for _ in range(2):
        mx = mm(m, cat([m, x], axis=-1))
        m, x = mx[..., :c], x + mx[..., c:]
    x = x + mm(m, x)
    gw = mm(x, cat([lo, rhs], axis=-1))
    g, w = gw[..., :c], gw[..., c:]
    gg = mm(g, cat([g, w], axis=-1))
    w = w + gg[..., c:]
    return w + mm(gg[..., :c], w)


def _wkv_chunk_fn(s0, r, lw, k, v, kk, a):
    c = r.shape[-2]
    n = 2 * c
    row = lax.broadcasted_iota(jnp.int32, (n, n), 0)
    col = lax.broadcasted_iota(jnp.int32, (n, n), 1)
    same = (row // c) == (col // c)
    incl = jnp.logical_and(row >= col, same)
    strict = jnp.logical_and(row > col, same)
    sel = (lax.broadcasted_iota(jnp.int32, (n, 128), 0) // c) == (lax.broadcasted_iota(jnp.int32, (n, 128), 1) // HEAD_DIM)
    two = lambda z: jnp.concatenate([z, z], axis=-2)
    lw2 = two(lw)
    mm = lambda p_, q_, form: _mm(p_, q_, form, WKV_PASSES)
    cl = _mm(incl.astype(F32), lw2, "nn", "L3")
    p = jnp.exp(cl)
    pinv = jnp.exp(-cl)
    pprev = jnp.exp(cl - lw2)
    kk2 = two(kk)
    at = jnp.where(sel, -kk2 * pprev, 0.0)
    bt = jnp.where(sel, kk2 * two(a) * pinv, 0.0)
    kt = jnp.where(sel, two(k) * pinv, 0.0)
    rt = jnp.where(sel, two(r) * p, 0.0)
    vt = jnp.where(sel, two(v), 0.0)
    cat = jnp.concatenate
    bk = cat([bt, kt], axis=-2)
    arbk = mm(cat([at, rt], axis=-2), bk, "nt")
    ab, ak = jnp.where(strict, arbk[..., :n, :n], 0.0), jnp.where(strict, arbk[..., :n, n:], 0.0)
    rb, rk = jnp.where(incl, arbk[..., n:, :n], 0.0), jnp.where(incl, arbk[..., n:, n:], 0.0)
    s0t = _transposed(s0)
    u = _solve_unit_lower(ab, mm(cat([at, ak], axis=-1), cat([s0t, vt], axis=-2), "nn"))
    y2 = mm(cat([rt, rb, rk], axis=-1), cat([s0t, u, vt], axis=-2), "nn")
    plast = jnp.exp(jnp.sum(lw, axis=-2, keepdims=True))
    s1 = (s0 + mm(cat([u, vt], axis=-2), bk, "tn")) * plast
    r2 = lax.broadcasted_iota(jnp.int32, (128, 128), 0) // HEAD_DIM
    c2 = lax.broadcasted_iota(jnp.int32, (128, 128), 1) // HEAD_DIM
    return y2[..., :c, :] + y2[..., c:, :], jnp.where(r2 == c2, s1, 0.0)


def _post_fn(y, r, k2, v, g, lnw, lnb, rk):
    seg = _seg_ones(RW)
    mean = _segsum(y, seg) * (1.0 / HEAD_DIM)
    yc = y - mean
    var = _segsum(yc * yc, seg) * (1.0 / HEAD_DIM)
    yn = yc * lax.rsqrt(var + GN_EPS)
    out = yn * lnw + lnb + _segsum(r * k2 * rk, seg) * v
    return out * g


def _attn_block_fn(q, kc, vc, kp=None, vp=None):
    n = ATTN_BLOCK
    qi = lax.broadcasted_iota(jnp.int32, (n, n), 0)
    kj = lax.broadcasted_iota(jnp.int32, (n, n), 1)
    lane = lax.broadcasted_iota(jnp.int32, (1, 128), 1)
    scale = HEAD_DIM ** -0.5
    valid = kj <= qi
    keys, vals = kc, vc
    if kp is not None:
        valid = jnp.concatenate([valid, kj >= qi], axis=-1)
        keys, vals = jnp.concatenate([kc, kp], axis=-2), jnp.concatenate([vc, vp], axis=-2)
    m0 = (lane // HEAD_DIM) == 0
    q2 = jnp.concatenate([jnp.where(m0, q, 0.0), jnp.where(m0, 0.0, q)], axis=-2)
    valid2 = jnp.concatenate([valid, valid], axis=-2)
    s = jnp.where(valid2, _mm(q2, keys, "nt", ATTN_PASSES) * scale, NEG)
    m = jnp.max(s, axis=-1, keepdims=True)
    p = jnp.exp(s - m)
    den = jnp.sum(p, axis=-1, keepdims=True)
    o2 = _mm(p, vals, "nn", ATTN_PASSES) / den
    l2 = m + jnp.log(den)
    return jnp.where(m0, o2[..., :n, :], o2[..., n:, :]), jnp.where(m0, l2[..., :n, :], l2[..., n:, :])


def _attn_block_bwd(q, kc, vc, kp, vp, o, lse, do, dl):
    n = ATTN_BLOCK
    cat = jnp.concatenate
    qi = lax.broadcasted_iota(jnp.int32, (n, n), 0)
    kj = lax.broadcasted_iota(jnp.int32, (n, n), 1)
    m0 = (lax.broadcasted_iota(jnp.int32, (1, 128), 1) // HEAD_DIM) == 0
    scale = HEAD_DIM ** -0.5
    valid = kj <= qi
    keys, vals = kc, vc
    if kp is not None:
        valid = cat([valid, kj >= qi], axis=-1)
        keys, vals = cat([kc, kp], axis=-2), cat([vc, vp], axis=-2)
    stack = lambda z: cat([jnp.where(m0, z, 0.0), jnp.where(m0, 0.0, z)], axis=-2)
    q2, do2 = stack(q), stack(do)
    lse2 = cat([jnp.max(jnp.where(m0, lse, NEG), axis=-1, keepdims=True),
                jnp.max(jnp.where(m0, NEG, lse), axis=-1, keepdims=True)], axis=-2)
    delta = jnp.sum(do2 * cat([o, o], axis=-2), axis=-1, keepdims=True)
    dlse = jnp.sum(stack(dl), axis=-1, keepdims=True)
    mm = lambda a, b, form: _mm_raw(a, b, form, ATTN_PASSES)
    s = jnp.where(cat([valid, valid], axis=-2), mm(q2, keys, "nt") * scale, NEG)
    p = jnp.exp(s - lse2)
    ds = p * (mm(do2, vals, "nt") - delta + dlse)
    dq2 = mm(ds, keys, "nn") * scale
    dq = jnp.where(m0, dq2[..., :n, :], dq2[..., n:, :])
    dkeys = mm(ds, q2, "tn") * scale
    dvals = mm(p, do2, "tn")
    if kp is None:
        return dq, dkeys, dvals
    return dq, dkeys[..., :n, :], dvals[..., :n, :], dkeys[..., n:, :], dvals[..., n:, :]


def _combine_fn(o1, o2, o3, l1, l2, l3, og):
    seg = _seg_ones(o1.shape[-1])
    m = jnp.maximum(jnp.maximum(l1, l2), l3)
    e1, e2, e3 = jnp.exp(l1 - m), jnp.exp(l2 - m), jnp.exp(l3 - m)
    o = (e1 * o1 + e2 * o2 + e3 * o3) / (e1 + e2 + e3)
    o = o * lax.rsqrt(_segsum(o * o, seg) * (1.0 / HEAD_DIM) + NORM_EPS)
    return o * og


def _in_proj(x, g1, win):
    t = x.shape[0]
    tm = 512

    def body(x_ref, g_ref, w_ref, h_ref, pa_ref, qkv_ref):
        h = _rms_fwd(x_ref[...], g_ref[...]).astype(BF16)
        h_ref[...] = h
        proj = _dot_nt(h, w_ref[...])
        pa_ref[...] = proj[:, :SHIFT_COLS]
        for j in range(3):
            for p in range(N_PAIR):
                c0 = SHIFT_COLS + j * RW + p * 128
                qkv_ref[j, p] = proj[:, c0:c0 + 128]

    return pl.pallas_call(
        body, name="in_proj", grid=(t // tm,),
        in_specs=[pl.BlockSpec((tm, D_MODEL), lambda i: (i, 0)), pl.BlockSpec((1, D_MODEL), lambda i: (0, 0)),
                  pl.BlockSpec((IN_COLS, D_MODEL), lambda i: (0, 0))],
        out_specs=[pl.BlockSpec((tm, D_MODEL), lambda i: (i, 0)), pl.BlockSpec((tm, SHIFT_COLS), lambda i: (i, 0)),
                   pl.BlockSpec((3, N_PAIR, tm, 128), lambda i: (0, 0, i, 0))],
        out_shape=[jax.ShapeDtypeStruct((t, D_MODEL), BF16), jax.ShapeDtypeStruct((t, SHIFT_COLS), F32),
                   jax.ShapeDtypeStruct((3, N_PAIR, t, 128), F32)],
        compiler_params=_params(("parallel",)),
    )(x, g1, win)


def _shifted(p, last8, first):
    prow = jnp.where(first, 0.0, last8[7:8, :])
    rolled = pltpu.roll(p, 1, axis=0)
    rid = lax.broadcasted_iota(jnp.int32, p.shape, 0)
    return jnp.where(rid == 0, prow, rolled)


_PREP_TM = 256


def _prep_specs(tm):
    vec = lambda n: pl.BlockSpec((1, n), lambda i: (0, 0))
    mat = lambda r, n: pl.BlockSpec((r, n), lambda i: (0, 0))
    return [vec(SHIFT_COLS), vec(RW), mat(128, RW), vec(RW), mat(128, RW), mat(128, RW), vec(RW), vec(RW)]


def _prep_fwd(proj, pw):
    t = proj.shape[0]
    tm = _PREP_TM

    def body(p_ref, l8_ref, mu, w0, w2p, a0, a2p, g2, k_k, k_a, *outs):
        p = p_ref[...]
        pprev = _shifted(p, l8_ref[...], pl.program_id(0) == 0)
        res = _prep_fn(p, pprev, mu[...], w0[...], w2p[...], a0[...], a2p[...], g2[...], k_k[...], k_a[...])
        for o_ref, val in zip(outs, res):
            o_ref[...] = val

    row = pl.BlockSpec((tm, RW), lambda i: (i, 0))
    return pl.pallas_call(
        body, name="rwkv_prep", grid=(t // tm,),
        in_specs=[pl.BlockSpec((tm, SHIFT_COLS), lambda i: (i, 0)),
                  pl.BlockSpec((8, SHIFT_COLS), lambda i: (jnp.maximum(i * (tm // 8) - 1, 0), 0))] + _prep_specs(tm),
        out_specs=[row] * 7,
        out_shape=[jax.ShapeDtypeStruct((t, RW), F32)] * 7,
        compiler_params=_params(("parallel",)),
    )(proj, proj, *pw)


def _pairs(ref):
    return jnp.stack([ref[:, 128 * p:128 * (p + 1)] for p in range(N_PAIR)], axis=0)


def _wkv_fwd(r, lw, k2, v, kk, a):
    t = r.shape[0]
    nc = t // CHUNK

    def body(r_ref, lw_ref, k_ref, v_ref, kk_ref, a_ref, y_ref, s_ref, st):
        @pl.when(pl.program_id(0) == 0)
        def _():
            st[...] = jnp.zeros_like(st)

        s0 = st[...]
        s_ref[0] = s0
        y, s1 = _wkv_chunk_fn(s0, *[_pairs(ref) for ref in (r_ref, lw_ref, k_ref, v_ref, kk_ref, a_ref)])
        for p in range(N_PAIR):
            y_ref[:, 128 * p:128 * (p + 1)] = y[p]
        st[...] = s1

    blk = pl.BlockSpec((CHUNK, RW), lambda c: (c, 0))
    return pl.pallas_call(
        body, name="wkv_fwd", grid=(nc,),
        in_specs=[blk] * 6,
        out_specs=[blk, pl.BlockSpec((1, N_PAIR, 128, 128), lambda c: (c, 0, 0, 0))],
        out_shape=[jax.ShapeDtypeStruct((t, RW), F32), jax.ShapeDtypeStruct((nc, N_PAIR, 128, 128), F32)],
        scratch_shapes=[pltpu.VMEM((N_PAIR, 128, 128), F32)],
        compiler_params=_params(("arbitrary",)),
    )(r, lw, k2, v, kk, a)


_POST_TM = 256


def _post_fwd(y, r, k2, v, g, lnw, lnb, rk):
    t = y.shape[0]
    tm = _POST_TM

    def body(y_ref, r_ref, k_ref, v_ref, g_ref, lnw_ref, lnb_ref, rk_ref, o_ref):
        o_ref[...] = _post_fn(y_ref[...], r_ref[...], k_ref[...], v_ref[...], g_ref[...],
                              lnw_ref[...], lnb_ref[...], rk_ref[...]).astype(BF16)

    row = pl.BlockSpec((tm, RW), lambda i: (i, 0))
    vec = pl.BlockSpec((1, RW), lambda i: (0, 0))
    return pl.pallas_call(
        body, name="rwkv_post", grid=(t // tm,),
        in_specs=[row] * 5 + [vec] * 3, out_specs=row,
        out_shape=jax.ShapeDtypeStruct((t, RW), BF16),
        compiler_params=_params(("parallel",)),
    )(y, r, k2, v, g, lnw, lnb, rk)


ATTN_GROUP = 2


def _dilated_rows(d, r, n):
    if d == 1:
        return pl.ds(pl.multiple_of(n * ATTN_BLOCK, ATTN_BLOCK), ATTN_BLOCK)
    return pl.ds(r + n * (ATTN_BLOCK * d), ATTN_BLOCK, stride=d)


def _for_each_sequence(t, unit):
    for di, d in enumerate(DILATIONS):

        @pl.when(pl.program_id(1) == di)
        def _(di=di, d=d):
            nb = t // (ATTN_BLOCK * d)
            if d == 1:
                unit(di, [(d, 0, 0)], False)
                unit(di, [(d, 0, 1)], True)
                lax.fori_loop(1, nb // 2, lambda k, c: (unit(di, [(d, 0, 2 * k), (d, 0, 2 * k + 1)], True), c)[1], 0)
            else:

                def residues(r, carry):
                    unit(di, [(d, r, 0), (d, r + d // 2, 0)], False)
                    if nb > 1:
                        lax.fori_loop(1, nb, lambda n, c: (unit(di, [(d, r, n), (d, r + d // 2, n)], True), c)[1], 0)
                    return carry

                lax.fori_loop(0, d // 2, residues, 0)


def _take(ref, lead, rows_list):
    return jnp.stack([ref.at[(*lead, g)][rows, :] for rows in rows_list for g in range(ATTN_GROUP)], axis=0)


def _put(ref, lead, rows_list, val, add=False):
    k = 0
    for rows in rows_list:
        for g in range(ATTN_GROUP):
            if add:
                ref.at[(*lead, g)][rows, :] += val[k]
            else:
                ref.at[(*lead, g)][rows, :] = val[k]
            k += 1


def _attn_fwd(qkv):
    t = qkv.shape[2]

    def body(q_ref, k_ref, v_ref, o_ref, l_ref):
        def unit(di, places, has_prev):
            cur = [_dilated_rows(d, r, n) for d, r, n in places]
            args = [_take(ref, (0,), cur) for ref in (q_ref, k_ref, v_ref)]
            if has_prev:
                prv = [_dilated_rows(d, r, n - 1) for d, r, n in places]
                args += [_take(ref, (0,), prv) for ref in (k_ref, v_ref)]
            o, lse = _attn_block_fn(*args)
            _put(o_ref, (0,), cur, o)
            _put(l_ref, (0,), cur, lse)

        _for_each_sequence(t, unit)

    spec = lambda j: pl.BlockSpec((1, ATTN_GROUP, t, 128), lambda i, b: (j, i, 0, 0))
    out = pl.BlockSpec((1, ATTN_GROUP, t, 128), lambda i, b: (b, i, 0, 0))
    return pl.pallas_call(
        body, name="attn_fwd", grid=(N_PAIR // ATTN_GROUP, len(DILATIONS)),
        in_specs=[spec(0), spec(1), spec(2)], out_specs=[out, out],
        out_shape=[jax.ShapeDtypeStruct((3, N_PAIR, t, 128), F32)] * 2,
        compiler_params=_params(("parallel", "arbitrary")),
    )(qkv, qkv, qkv)


_COMB_TM = 256


def _combine_fwd(o, l, og):
    t = o.shape[2]
    tm = _COMB_TM

    def body(o_ref, l_ref, og_ref, y_ref):
        for p in range(N_PAIR):
            cols = slice(128 * p, 128 * (p + 1))
            y_ref[:, cols] = _combine_fn(o_ref[0, p], o_ref[1, p], o_ref[2, p], l_ref[0, p], l_ref[1, p], l_ref[2, p],
                                         og_ref[:, cols]).astype(BF16)

    blk = pl.BlockSpec((3, N_PAIR, tm, 128), lambda i: (0, 0, i, 0))
    return pl.pallas_call(
        body, name="attn_combine", grid=(t // tm,),
        in_specs=[blk, blk, pl.BlockSpec((1, RW), lambda i: (0, 0))], out_specs=pl.BlockSpec((tm, RW), lambda i: (i, 0)),
        out_shape=jax.ShapeDtypeStruct((t, RW), BF16),
        compiler_params=_params(("parallel",)),
    )(o, l, og)


def _ffn_all(x, ycat, wg, wu, wd, wout, g2, gf, tgt):
    t = x.shape[0]
    tm = 256

    def body(x_ref, y_ref, wg_ref, wu_ref, wd_ref, wo_ref, g2_ref, gf_ref, t_ref,
             h_ref, act_ref, dx2b_ref, dgt_ref, dup_ref, dx1b_ref, dx1_ref, dya_ref, dyb_ref, loss_ref, dgf_ref, dg2_ref,
             gt_s, up_s):
        first = pl.program_id(0) == 0
        x1 = x_ref[...] + _dot(y_ref[...], wo_ref[...])
        h = _rms_fwd(x1, g2_ref[...]).astype(BF16)
        h_ref[...] = h
        for c0 in range(0, D_FF, FF_CHUNK):
            cols = slice(c0, c0 + FF_CHUNK)
            gt = _dot_nt(h, wg_ref[cols, :])
            up = _dot_nt(h, wu_ref[cols, :])
            gt_s[:, cols] = gt.astype(BF16)
            up_s[:, cols] = up.astype(BF16)
            act_ref[:, cols] = (gt * _sigmoid(gt) * up).astype(BF16)
        x2 = x1 + _dot(act_ref[...], wd_ref[...])
        gf_ = gf_ref[...]
        diff = _rms_fwd(x2, gf_) - t_ref[...]
        lrow = 0.5 * jnp.sum(_colsum8(diff * diff), axis=1, keepdims=True) * (1.0 / D_MODEL)
        _acc(loss_ref, jnp.broadcast_to(lrow, (8, 128)), first)
        dx2, dgr = _rms_bwd(diff * (1.0 / D_MODEL), x2, gf_)
        _acc(dgf_ref, _colsum8(dgr), first)
        dx2b = dx2.astype(BF16)
        dx2b_ref[...] = dx2b
        for c0 in range(0, D_FF, FF_CHUNK):
            cols = slice(c0, c0 + FF_CHUNK)
            dact = _dot_nt(dx2b, wd_ref[cols, :])
            gt = gt_s[:, cols].astype(F32)
            sg = _sigmoid(gt)
            dgt_ref[:, cols] = (dact * up_s[:, cols].astype(F32) * sg * (1.0 + gt * (1.0 - sg))).astype(BF16)
            dup_ref[:, cols] = (dact * gt * sg).astype(BF16)
        dh = _dot(dgt_ref[...], wg_ref[...]) + _dot(dup_ref[...], wu_ref[...])
        dxn, dgr2 = _rms_bwd(dh, x1, g2_ref[...])
        _acc(dg2_ref, _colsum8(dgr2), first)
        dx1 = dx2 + dxn
        dx1_ref[...] = dx1
        dx1b = dx1.astype(BF16)
        dx1b_ref[...] = dx1b
        dy = _dot_nt(dx1b, wo_ref[...])
        dya_ref[...] = dy[:, :RW]
        dyb_ref[...] = dy[:, RW:]

    row = pl.BlockSpec((tm, D_MODEL), lambda i: (i, 0))
    wide = pl.BlockSpec((tm, D_FF), lambda i: (i, 0))
    half = pl.BlockSpec((tm, RW), lambda i: (i, 0))
    wsp = pl.BlockSpec((D_FF, D_MODEL), lambda i: (0, 0))
    vec = pl.BlockSpec((1, D_MODEL), lambda i: (0, 0))
    part = pl.BlockSpec((8, D_MODEL), lambda i: (0, 0))
    bf = lambda n: jax.ShapeDtypeStruct((t, n), BF16)
    return pl.pallas_call(
        body, name="ffn_all", grid=(t // tm,),
        in_specs=[row, row, wsp, wsp, wsp, pl.BlockSpec((D_MODEL, D_MODEL), lambda i: (0, 0)), vec, vec, row],
        out_specs=[row, wide, row, wide, wide, row, row, half, half, pl.BlockSpec((8, 128), lambda i: (0, 0)), part, part],
        out_shape=[bf(D_MODEL), bf(D_FF), bf(D_MODEL), bf(D_FF), bf(D_FF), bf(D_MODEL),
                   jax.ShapeDtypeStruct((t, D_MODEL), F32), jax.ShapeDtypeStruct((t, RW), F32),
                   jax.ShapeDtypeStruct((t, RW), F32), jax.ShapeDtypeStruct((8, 128), F32),
                   jax.ShapeDtypeStruct((8, D_MODEL), F32), jax.ShapeDtypeStruct((8, D_MODEL), F32)],
        scratch_shapes=[pltpu.VMEM((tm, D_FF), BF16), pltpu.VMEM((tm, D_FF), BF16)],
        compiler_params=_params(("arbitrary",)),
    )(x, ycat, wg, wu, wd, wout, g2, gf, tgt)


def _wgrad(a, b, tk, tn, name):
    t, kdim = a.shape
    ndim = b.shape[1]

    def body(a_ref, b_ref, o_ref):
        o_ref[...] = _dot_tn(a_ref[...], b_ref[...])

    return pl.pallas_call(
        body, name=name, grid=(kdim // tk, ndim // tn),
        in_specs=[pl.BlockSpec((t, tk), lambda i, j: (0, i)), pl.BlockSpec((t, tn), lambda i, j: (0, j))],
        out_specs=pl.BlockSpec((tk, tn), lambda i, j: (i, j)),
        out_shape=jax.ShapeDtypeStruct((kdim, ndim), F32),
        compiler_params=_params(("parallel", "parallel")),
    )(a, b)


def _post_bwd(dya, y, r, k2, v, g, lnw, lnb, rk):
    t = y.shape[0]
    tm = _POST_TM

    def body(d_ref, y_ref, r_ref, k_ref, v_ref, g_ref, lnw_ref, lnb_ref, rk_ref,
             dy_ref, dr_ref, dk_ref, dv_ref, dg_ref, dlnw_ref, dlnb_ref, drk_ref):
        first = pl.program_id(0) == 0
        ones = jnp.ones((tm, 1), F32)
        prim = (y_ref[...], r_ref[...], k_ref[...], v_ref[...], g_ref[...],
                ones * lnw_ref[...], ones * lnb_ref[...], ones * rk_ref[...])
        _, vjp = jax.vjp(_post_fn, *prim)
        dy, dr, dk, dv, dg, dlnw, dlnb, drk = vjp(d_ref[...])
        dy_ref[...] = dy
        dr_ref[...] = dr
        dk_ref[...] = dk
        dv_ref[...] = dv
        dg_ref[...] = dg
        _acc(dlnw_ref, _colsum8(dlnw), first)
        _acc(dlnb_ref, _colsum8(dlnb), first)
        _acc(drk_ref, _colsum8(drk), first)

    row = pl.BlockSpec((tm, RW), lambda i: (i, 0))
    vec = pl.BlockSpec((1, RW), lambda i: (0, 0))
    part = pl.BlockSpec((8, RW), lambda i: (0, 0))
    return pl.pallas_call(
        body, name="rwkv_post_bwd", grid=(t // tm,),
        in_specs=[row] * 6 + [vec] * 3, out_specs=[row] * 5 + [part] * 3,
        out_shape=[jax.ShapeDtypeStruct((t, RW), F32)] * 5 + [jax.ShapeDtypeStruct((8, RW), F32)] * 3,
        compiler_params=_params(("arbitrary",)),
    )(dya, y, r, k2, v, g, lnw, lnb, rk)


def _wkv_bwd(dy, s0s, r, lw, k2, v, kk, a):
    t = r.shape[0]
    nc = t // CHUNK

    def body(dy_ref, s_ref, r_ref, lw_ref, k_ref, v_ref, kk_ref, a_ref,
             dr_ref, dlw_ref, dk_ref, dv_ref, dkk_ref, da_ref, ds):
        @pl.when(pl.program_id(0) == 0)
        def _():
            ds[...] = jnp.zeros_like(ds)

        _, vjp = jax.vjp(_wkv_chunk_fn, s_ref[0],
                         *[_pairs(ref) for ref in (r_ref, lw_ref, k_ref, v_ref, kk_ref, a_ref)])
        res = vjp((_pairs(dy_ref), ds[...]))
        ds[...] = res[0]
        for ref, val in zip((dr_ref, dlw_ref, dk_ref, dv_ref, dkk_ref, da_ref), res[1:]):
            for p in range(N_PAIR):
                ref[:, 128 * p:128 * (p + 1)] = val[p]

    blk = pl.BlockSpec((CHUNK, RW), lambda c: (nc - 1 - c, 0))
    return pl.pallas_call(
        body, name="wkv_bwd", grid=(nc,),
        in_specs=[blk, pl.BlockSpec((1, N_PAIR, 128, 128), lambda c: (nc - 1 - c, 0, 0, 0))] + [blk] * 6,
        out_specs=[blk] * 6,
        out_shape=[jax.ShapeDtypeStruct((t, RW), F32)] * 6,
        scratch_shapes=[pltpu.VMEM((N_PAIR, 128, 128), F32)],
        compiler_params=_params(("arbitrary",)),
    )(dy, s0s, r, lw, k2, v, kk, a)


def _prep_bwd(proj, pw, douts):
    t = proj.shape[0]
    tm = _PREP_TM
    nt = t // tm

    def body(p_ref, l8_ref, mu, w0, w2p, a0, a2p, g2, k_k, k_a, dr, dr2, dlw, dk2, dk22, dv, dv2, dkk, da, dg,
             dp_ref, dmu_ref, dw0_ref, dw2_ref, da0_ref, da2_ref, dg2_ref, dkk_ref, dka_ref, carry):
        i = pl.program_id(0)
        first = i == 0

        @pl.when(first)
        def _():
            carry[...] = jnp.zeros_like(carry)

        p = p_ref[...]
        pprev = _shifted(p, l8_ref[...], i == nt - 1)
        ones = jnp.ones((tm, 1), F32)
        prim = (p, pprev, ones * mu[...], ones * w0[...], w2p[...], ones * a0[...], a2p[...], g2[...],
                ones * k_k[...], ones * k_a[...])
        _, vjp = jax.vjp(_prep_fn, *prim)
        dp, dpp, dmu, dw0, dw2, da0, da2, dg2, dkk_, dka = vjp(
            (dr[...] + dr2[...], dlw[...], dk2[...] + dk22[...], dv[...] + dv2[...], dkk[...], da[...], dg[...]))
        up = pltpu.roll(dpp, tm - 1, axis=0)
        rid = lax.broadcasted_iota(jnp.int32, dpp.shape, 0)
        dp_ref[...] = dp + jnp.where(rid == tm - 1, carry[0:1, :], up)
        carry[...] = jnp.broadcast_to(dpp[0:1, :], carry.shape)
        _acc(dmu_ref, _colsum8(dmu), first)
        _acc(dw0_ref, _colsum8(dw0), first)
        _acc(dw2_ref, dw2, first)
        _acc(da0_ref, _colsum8(da0), first)
        _acc(da2_ref, da2, first)
        _acc(dg2_ref, dg2, first)
        _acc(dkk_ref, _colsum8(dkk_), first)
        _acc(dka_ref, _colsum8(dka), first)

    rev = lambda i: (nt - 1 - i, 0)
    row = pl.BlockSpec((tm, RW), rev)
    part = lambda n: pl.BlockSpec((8, n), lambda i: (0, 0))
    mat = pl.BlockSpec((128, RW), lambda i: (0, 0))
    return pl.pallas_call(
        body, name="rwkv_prep_bwd", grid=(nt,),
        in_specs=[pl.BlockSpec((tm, SHIFT_COLS), rev),
                  pl.BlockSpec((8, SHIFT_COLS), lambda i: (jnp.maximum((nt - 1 - i) * (tm // 8) - 1, 0), 0))]
                 + _prep_specs(tm) + [row] * 10,
        out_specs=[pl.BlockSpec((tm, SHIFT_COLS), rev), part(SHIFT_COLS), part(RW), mat, part(RW), mat, mat,
                   part(RW), part(RW)],
        out_shape=[jax.ShapeDtypeStruct((t, SHIFT_COLS), F32), jax.ShapeDtypeStruct((8, SHIFT_COLS), F32),
                   jax.ShapeDtypeStruct((8, RW), F32), jax.ShapeDtypeStruct((128, RW), F32),
                   jax.ShapeDtypeStruct((8, RW), F32), jax.ShapeDtypeStruct((128, RW), F32),
                   jax.ShapeDtypeStruct((128, RW), F32), jax.ShapeDtypeStruct((8, RW), F32),
                   jax.ShapeDtypeStruct((8, RW), F32)],
        scratch_shapes=[pltpu.VMEM((8, SHIFT_COLS), F32)],
        compiler_params=_params(("arbitrary",)),
    )(proj, proj, *pw, *douts)


def _combine_bwd(dyb, o, l, og):
    t = dyb.shape[0]
    tm = _COMB_TM

    def body(d_ref, o_ref, l_ref, og_ref, do_ref, dl_ref, dog_ref):
        ones = jnp.ones((tm, 1), F32)
        dog = []
        for p in range(N_PAIR):
            cols = slice(128 * p, 128 * (p + 1))
            _, vjp = jax.vjp(_combine_fn, o_ref[0, p], o_ref[1, p], o_ref[2, p], l_ref[0, p], l_ref[1, p], l_ref[2, p],
                             ones * og_ref[:, cols])
            res = vjp(d_ref[:, cols])
            for b in range(3):
                do_ref[b, p] = res[b]
                dl_ref[b, p] = res[3 + b]
            dog.append(_colsum8(res[6]))
        _acc(dog_ref, jnp.concatenate(dog, axis=1), pl.program_id(0) == 0)

    blk = pl.BlockSpec((3, N_PAIR, tm, 128), lambda i: (0, 0, i, 0))
    return pl.pallas_call(
        body, name="attn_combine_bwd", grid=(t // tm,),
        in_specs=[pl.BlockSpec((tm, RW), lambda i: (i, 0)), blk, blk, pl.BlockSpec((1, RW), lambda i: (0, 0))],
        out_specs=[blk, blk, pl.BlockSpec((8, RW), lambda i: (0, 0))],
        out_shape=[jax.ShapeDtypeStruct((3, N_PAIR, t, 128), F32)] * 2 + [jax.ShapeDtypeStruct((8, RW), F32)],
        compiler_params=_params(("arbitrary",)),
    )(dyb, o, l, og)


def _attn_bwd(do, dl, o, lse, qkv):
    t = qkv.shape[2]

    def body(do_ref, dl_ref, o_ref, l_ref, q_ref, k_ref, v_ref, dq_ref, dk_ref, dv_ref):
        @pl.when(pl.program_id(1) == 0)
        def _():
            for ref in (dq_ref, dk_ref, dv_ref):
                ref[...] = jnp.zeros_like(ref)

        def unit(di, places, has_prev):
            cur = [_dilated_rows(d, r, n) for d, r, n in places]
            q, kc, vc = [_take(ref, (0,), cur) for ref in (q_ref, k_ref, v_ref)]
            kp = vp = None
            if has_prev:
                prv = [_dilated_rows(d, r, n - 1) for d, r, n in places]
                kp, vp = [_take(ref, (0,), prv) for ref in (k_ref, v_ref)]
            res = _attn_block_bwd(q, kc, vc, kp, vp, *[_take(ref, (0,), cur) for ref in (o_ref, l_ref, do_ref, dl_ref)])
            _put(dq_ref, (), cur, res[0], add=True)
            _put(dk_ref, (), cur, res[1], add=True)
            _put(dv_ref, (), cur, res[2], add=True)
            if has_prev:
                _put(dk_ref, (), prv, res[3], add=True)
                _put(dv_ref, (), prv, res[4], add=True)

        _for_each_sequence(t, unit)

    spec = lambda j: pl.BlockSpec((1, ATTN_GROUP, t, 128), lambda i, b: (j, i, 0, 0))
    branch = pl.BlockSpec((1, ATTN_GROUP, t, 128), lambda i, b: (b, i, 0, 0))
    out = pl.BlockSpec((ATTN_GROUP, t, 128), lambda i, b: (i, 0, 0))
    return pl.pallas_call(
        body, name="attn_bwd", grid=(N_PAIR // ATTN_GROUP, len(DILATIONS)),
        in_specs=[branch] * 4 + [spec(0), spec(1), spec(2)], out_specs=[out] * 3,
        out_shape=[jax.ShapeDtypeStruct((N_PAIR, t, 128), F32)] * 3,
        compiler_params=_params(("parallel", "arbitrary")),
    )(do, dl, o, lse, qkv, qkv, qkv)


def _in_proj_bwd(dpa, dq, dk, dv, win, x, g1, dx1):
    t = x.shape[0]
    tm = 256

    def body(dpa_ref, dq_ref, dk_ref, dv_ref, w_ref, x_ref, g_ref, dx1_ref, dproj_ref, dx_ref, dg_ref):
        parts = [dpa_ref[...]] + [ref[p] for ref in (dq_ref, dk_ref, dv_ref) for p in range(N_PAIR)]
        dproj = jnp.concatenate([z.astype(BF16) for z in parts], axis=1)
        dproj_ref[...] = dproj
        dh = _dot(dproj, w_ref[...])
        dxn, dgr = _rms_bwd(dh, x_ref[...], g_ref[...])
        dx_ref[...] = dx1_ref[...] + dxn
        _acc(dg_ref, _colsum8(dgr), pl.program_id(0) == 0)

    row = pl.BlockSpec((tm, D_MODEL), lambda i: (i, 0))
    pair = pl.BlockSpec((N_PAIR, tm, 128), lambda i: (0, i, 0))
    return pl.pallas_call(
        body, name="in_proj_bwd", grid=(t // tm,),
        in_specs=[pl.BlockSpec((tm, SHIFT_COLS), lambda i: (i, 0))] + [pair] * 3
                 + [pl.BlockSpec((IN_COLS, D_MODEL), lambda i: (0, 0)), row, pl.BlockSpec((1, D_MODEL), lambda i: (0, 0)), row],
        out_specs=[pl.BlockSpec((tm, IN_COLS), lambda i: (i, 0)), row, pl.BlockSpec((8, D_MODEL), lambda i: (0, 0))],
        out_shape=[jax.ShapeDtypeStruct((t, IN_COLS), BF16), jax.ShapeDtypeStruct((t, D_MODEL), F32),
                   jax.ShapeDtypeStruct((8, D_MODEL), F32)],
        compiler_params=_params(("arbitrary",)),
    )(dpa, dq, dk, dv, win, x, g1, dx1)


def _pad_lora(w, lo):
    z = jnp.zeros((64, RW), F32)
    return jnp.concatenate([w, z], axis=0) if lo == 0 else jnp.concatenate([z, w], axis=0)


def _local_step(x, tgt, win, vecs, w2, a2, g2m, get_rest, send_rest):
    pw = (vecs["mu_shift"], vecs["decay_w0"], _pad_lora(w2, 0), vecs["iclr_a0"], _pad_lora(a2, 64), g2m,
          vecs["k_k"], vecs["k_a"])
    h, proj, qkv = _in_proj(x, vecs["mix_norm_g"], win)
    r, lw, k2, v, kk, a, g = _prep_fwd(proj, pw)
    y, s0s = _wkv_fwd(r, lw, k2, v, kk, a)
    ya = _post_fwd(y, r, k2, v, g, vecs["ln_x_w"], vecs["ln_x_b"], vecs["r_k"])
    o_att, l_att = _attn_fwd(qkv)
    yb = _combine_fwd(o_att, l_att, vecs["attn_out_g"])

    wout, wg, wu, wd = get_rest(yb)
    ycat = jnp.concatenate([ya, yb], axis=1)
    h2, act, dx2b, dgt, dup, dx1b, dx1, dya, dyb, loss8, dgf, dg2n = _ffn_all(
        x, ycat, wg, wu, wd, wout, vecs["ffn_norm_g"], vecs["final_norm_g"], tgt)
    gw = {
        "w_down": _wgrad(act, dx2b, 256, 1024, "wgrad_down"),
        "w_gate": _wgrad(dgt, h2, 256, 1024, "wgrad_gate"),
        "w_up": _wgrad(dup, h2, 256, 1024, "wgrad_up"),
        "w_out": _wgrad(ycat, dx1b, 256, 1024, "wgrad_out"),
    }

    lnw = vecs["ln_x_w"] + send_rest(gw)[0, 0]
    dy, dr_p, dk2_p, dv_p, dg, dlnw, dlnb, drk = _post_bwd(dya, y, r, k2, v, g, lnw, vecs["ln_x_b"], vecs["r_k"])
    dr_s, dlw, dk2_s, dv_s, dkk, da = _wkv_bwd(dy, s0s, r, lw, k2, v, kk, a)
    dpa, dmu, dw0, dw2p, da0, da2p, dg2m, dk_k, dk_a = _prep_bwd(
        proj, pw, (dr_p, dr_s, dlw, dk2_p, dk2_s, dv_p, dv_s, dkk, da, dg))

    do_att, dl_att, dog = _combine_bwd(dyb, o_att, l_att, vecs["attn_out_g"])
    dq, dk, dv = _attn_bwd(do_att, dl_att, o_att, l_att, qkv)
    dproj, dx, dg1 = _in_proj_bwd(dpa, dq, dk, dv, win, x, vecs["mix_norm_g"], dx1)
    gw["w_in"] = _wgrad(dproj, h, 256, 1024, "wgrad_in")
    gw["decay_w2"] = dw2p[:64]
    gw["iclr_a2"] = da2p[64:]
    gw["gate_g2"] = dg2m
    gv = {"mix_norm_g": dg1, "mu_shift": dmu, "decay_w0": dw0, "iclr_a0": da0, "k_k": dk_k, "k_a": dk_a, "r_k": drk,
          "ln_x_w": dlnw, "ln_x_b": dlnb, "attn_out_g": dog, "ffn_norm_g": dg2n, "final_norm_g": dgf}
    return loss8, dx, gw, gv


N_CHIP = 4
N_DEV = 8
MATS = ("w_in", "w_out", "w_gate", "w_up", "w_down")
LORAS = ("decay_w2", "iclr_a2", "gate_g2")
VECS = (("mix_norm_g", 1024), ("mu_shift", 1792), ("decay_w0", 512), ("iclr_a0", 512), ("k_k", 512), ("k_a", 512),
        ("r_k", 512), ("ln_x_w", 512), ("ln_x_b", 512), ("attn_out_g", 512), ("ffn_norm_g", 1024),
        ("final_norm_g", 1024))
N_VEC = sum(n for _, n in VECS)
N_SMALL = N_VEC + 128
ANY = pl.BlockSpec(memory_space=pl.ANY)


def _flip(v, f):
    return 1 - v if f else v


class _Me:
    def __init__(self, mode):
        x, y, c = lax.axis_index("x"), lax.axis_index("y"), lax.axis_index("c")
        self.core, self.chip, self.dev = c, 2 * x + y, 4 * x + 2 * y + c
        self.sibling = (x, y, 1 - c)
        if mode == "chips":
            self.peers = [(px, py, c) for px, py in ((1 - x, y), (x, 1 - y), (1 - x, 1 - y))]
        else:
            self.peers = [(_flip(x, k & 4), _flip(y, k & 2), _flip(c, k & 1)) for k in range(1, N_DEV)]


def _half(core, rows):
    h = rows // 2
    return pl.ds(pl.multiple_of(core * h, h), h)


def _peer_copy(srcs, dsts, kinds, send_sems, recv_sems, me, j, i, incoming):
    px, py, pc = me.peers[j]
    pchip, pdev = 2 * px + py, 4 * px + 2 * py + pc
    src, dst, kind = srcs[i], dsts[i], kinds[i]
    if kind == "gather":
        rows = _half(me.core, src.shape[0])
        src, dst = src.at[rows], dst.at[pchip if incoming else me.chip, rows]
    elif kind == "scatter":
        src, dst = src.at[pchip, _half(pc, src.shape[1])], dst.at[pdev if incoming else me.dev]
    else:
        dst = dst.at[pdev if incoming else me.dev]
    n = len(srcs)
    return pltpu.make_async_remote_copy(src_ref=src, dst_ref=dst, send_sem=send_sems.at[n * j + i],
                                        recv_sem=recv_sems.at[n * j + i], device_id=(px, py, pc), device_id_type=MESH)


def _mode(kinds):
    return "chips" if kinds[0] == "gather" else "devs"


def _npeer(kinds):
    return N_CHIP - 1 if kinds[0] == "gather" else N_DEV - 1


def _swap_gathered(lands, name):
    n = len(lands)

    def body(*refs):
        dsts, send_sems, recv_sems = refs[n:2 * n], refs[2 * n], refs[2 * n + 1]
        me = _Me("chips")

        def copy(j, i, incoming):
            px, py, _ = me.peers[j]
            rows_out, rows_in = _half(me.core, dsts[i].shape[1]), _half(1 - me.core, dsts[i].shape[1])
            return pltpu.make_async_remote_copy(
                src_ref=dsts[i].at[2 * px + py, rows_out], dst_ref=dsts[i].at[2 * px + py, rows_in if incoming else rows_out],
                send_sem=send_sems.at[n * j + i], recv_sem=recv_sems.at[n * j + i], device_id=me.sibling, device_id_type=MESH)

        sends = [copy(j, i, False) for j in range(3) for i in range(n)]
        for cp in sends:
            cp.start()
        for j in range(3):
            for i in range(n):
                copy(j, i, True).wait_recv()
        for cp in sends:
            cp.wait_send()

    return pl.pallas_call(
        body, name=name, in_specs=[ANY] * n, out_specs=[ANY] * n,
        out_shape=[jax.ShapeDtypeStruct(l.shape, l.dtype) for l in lands],
        input_output_aliases={i: i for i in range(n)},
        scratch_shapes=[pltpu.SemaphoreType.DMA((3 * n,)), pltpu.SemaphoreType.DMA((3 * n,))],
    )(*lands)


def _join_halves(sums, name):
    n = len(sums)

    def body(*refs):
        dsts, send_sems, recv_sems = refs[n:2 * n], refs[2 * n], refs[2 * n + 1]
        me = _Me("chips")

        def copy(i, incoming):
            mine, other = _half(me.core, dsts[i].shape[0]), _half(1 - me.core, dsts[i].shape[0])
            return pltpu.make_async_remote_copy(src_ref=dsts[i].at[mine], dst_ref=dsts[i].at[other if incoming else mine],
                                                send_sem=send_sems.at[i], recv_sem=recv_sems.at[i],
                                                device_id=me.sibling, device_id_type=MESH)

        sends = [copy(i, False) for i in range(n)]
        for cp in sends:
            cp.start()
        for i in range(n):
            copy(i, True).wait_recv()
        for cp in sends:
            cp.wait_send()

    return pl.pallas_call(
        body, name=name, in_specs=[ANY] * n, out_specs=[ANY] * n,
        out_shape=[jax.ShapeDtypeStruct(s.shape, s.dtype) for s in sums],
        input_output_aliases={i: i for i in range(n)},
        scratch_shapes=[pltpu.SemaphoreType.DMA((n,)), pltpu.SemaphoreType.DMA((n,))],
    )(*sums)


HBM = pl.BlockSpec(memory_space=pltpu.HBM)
SEM = pl.BlockSpec(memory_space=pltpu.SEMAPHORE)
EFFECT = pltpu.SideEffectType.DATAFLOW_SIDE_EFFECTING


def _swap_start(arrs, lands, kinds, name):
    n = len(arrs)

    def body(*refs):
        srcs, dsts, send_sems, recv_sems, token = refs[:n], refs[n:2 * n], refs[2 * n], refs[2 * n + 1], refs[-1]
        me = _Me(_mode(kinds))
        for j in range(len(me.peers)):
            for i in range(n):
                _peer_copy(srcs, dsts, kinds, send_sems, recv_sems, me, j, i, False).start()
        token[...] = jnp.zeros_like(token)

    ns = _npeer(kinds) * n
    outs = pl.pallas_call(
        body, name=name,
        out_shape=(pltpu.SemaphoreType.DMA((ns,)), pltpu.SemaphoreType.DMA((ns,)),
                   *[pltpu.HBM(a.shape, a.dtype) for a in arrs], *[pltpu.HBM(l.shape, l.dtype) for l in lands],
                   jax.ShapeDtypeStruct((8, 128), F32)),
        in_specs=[HBM] * (2 * n), out_specs=(SEM, SEM, *[HBM] * (2 * n), pl.BlockSpec(memory_space=pltpu.VMEM)),
        input_output_aliases={k: 2 + k for k in range(2 * n)},
        compiler_params=pltpu.CompilerParams(has_side_effects=EFFECT),
    )(*[pltpu.with_memory_space_constraint(a, pltpu.HBM) for a in arrs],
      *[pltpu.with_memory_space_constraint(l, pltpu.HBM) for l in lands])
    return outs[0], outs[1], outs[2:2 + n], outs[2 + n:2 + 2 * n], outs[-1]


def _swap_wait(send_sems, recv_sems, srcs_thru, lands_thru, after, kinds, name):
    n = len(srcs_thru)

    def body(*refs):
        srcs, dsts, s_sems, r_sems = refs[:n], refs[n:2 * n], refs[2 * n], refs[2 * n + 1]
        me = _Me(_mode(kinds))
        for j in range(len(me.peers)):
            for i in range(n):
                cp = _peer_copy(srcs, dsts, kinds, s_sems, r_sems, me, j, i, True)
                cp.wait_send()
                cp.wait_recv()

    outs = pl.pallas_call(
        body, name=name,
        out_shape=tuple(pltpu.HBM(a.shape, a.dtype) for a in (*srcs_thru, *lands_thru)),
        in_specs=[HBM] * (2 * n) + [SEM, SEM, ANY], out_specs=tuple([HBM] * (2 * n)),
        input_output_aliases={k: k for k in range(2 * n)},
        compiler_params=pltpu.CompilerParams(has_side_effects=EFFECT),
    )(*srcs_thru, *lands_thru, send_sems, recv_sems, after)
    return outs[n:]


def _adamw(w, g, m, v):
    m = ADAM_B1 * m + (1.0 - ADAM_B1) * g
    v = ADAM_B2 * v + (1.0 - ADAM_B2) * (g * g)
    m_hat = m / (1.0 - ADAM_B1 ** ADAM_STEP)
    v_hat = v / (1.0 - ADAM_B2 ** ADAM_STEP)
    delta = -ADAM_LR * (m_hat / (jnp.sqrt(v_hat) + ADAM_EPS) + ADAM_WD * w)
    return delta, m, v


def _reduce8(rbuf, core, tr, name):
    _, h, cols = rbuf.shape

    def body(core_ref, r_ref, g_ref):
        g = r_ref[0].astype(F32)
        for s in range(1, N_DEV):
            g = g + r_ref[s].astype(F32)
        g_ref[...] = g

    return pl.pallas_call(
        body, name=name,
        grid_spec=pltpu.PrefetchScalarGridSpec(
            num_scalar_prefetch=1, grid=(h // tr,),
            in_specs=[pl.BlockSpec((N_DEV, tr, cols), lambda i, core_ref: (0, i, 0))],
            out_specs=pl.BlockSpec((tr, cols), lambda i, core_ref: (core_ref[0] * (h // tr) + i, 0))),
        out_shape=jax.ShapeDtypeStruct((2 * h, cols), F32),
        compiler_params=_params(("parallel",)),
    )(core, rbuf)


def _adamw_call(g, w, m, v, tr, name):
    _, rows, cols = w.shape

    def body(g_in, w_ref, m_ref, v_ref, g_ref, d_ref, nm_ref, nv_ref):
        g = g_in[...]
        g_ref[0] = g
        d_ref[0], nm_ref[0], nv_ref[0] = _adamw(w_ref[0], g, m_ref[0], v_ref[0])

    row = pl.BlockSpec((1, tr, cols), lambda i: (0, i, 0))
    return pl.pallas_call(
        body, name=name, grid=(rows // tr,),
        in_specs=[pl.BlockSpec((tr, cols), lambda i: (i, 0)), row, row, row], out_specs=[row] * 4,
        out_shape=[jax.ShapeDtypeStruct(w.shape, F32)] * 4,
        compiler_params=_params(("parallel",)),
    )(g, w, m, v)


def _rowsum_small(parts, loss8):
    def body(*refs):
        out = refs[-1]
        c0 = 0
        for ref in refs[:-1]:
            n = ref.shape[1]
            out[:, c0:c0 + n] = jnp.sum(ref[...], axis=0, keepdims=True)
            c0 += n

    return pl.pallas_call(body, name="rowsum_small", out_shape=jax.ShapeDtypeStruct((1, N_SMALL), F32))(*parts, loss8)


def _reduce_adamw_small(sbuf, ws, ms, vs):
    nv = len(ws)

    def body(*refs):
        s_ref, ins, outs = refs[0], refs[1:1 + 3 * nv], refs[1 + 3 * nv:]
        tot = s_ref[0]
        for s in range(1, N_DEV):
            tot = tot + s_ref[s]
        c0 = 0
        for i in range(nv):
            n = ins[i].shape[1]
            g = tot[:, c0:c0 + n]
            outs[i][...] = g
            outs[nv + i][...], outs[2 * nv + i][...], outs[3 * nv + i][...] = _adamw(
                ins[i][...], g, ins[nv + i][...], ins[2 * nv + i][...])
            c0 += n
        outs[-1][...] = tot[:, c0:]

    return pl.pallas_call(
        body, name="reduce_adamw_small",
        out_shape=[jax.ShapeDtypeStruct(a.shape, F32) for a in ws] * 4 + [jax.ShapeDtypeStruct((1, 128), F32)],
    )(sbuf, *ws, *ms, *vs)


_TRANSPOSED = ("w_in", "w_gate", "w_up")
_ROW_STACKED = MATS
_ADAM_TILE = {"w_in": 208, "w_out": 256, "w_gate": 176, "w_up": 176, "w_down": 176, "decay_w2": 64, "iclr_a2": 64,
              "gate_g2": 128}
_SUM_TILE = {"w_in": 208, "w_out": 128, "w_gate": 176, "w_up": 176, "w_down": 176, "decay_w2": 32, "iclr_a2": 32,
             "gate_g2": 64}


def _full(n, stacked):
    p, r, c = stacked.shape
    if n in _ROW_STACKED:
        return stacked.reshape(p * r, c)
    return jnp.transpose(stacked, (1, 0, 2)).reshape(r, p * c)


def _by_chip(n, full):
    if n in _ROW_STACKED:
        return full.reshape(N_CHIP, full.shape[0] // N_CHIP, full.shape[1])
    r, c = full.shape
    return jnp.transpose(full.reshape(r, N_CHIP, c // N_CHIP), (1, 0, 2))


def _with_own(land_shape, dtype, own, slot):
    return lax.dynamic_update_slice(lax.empty(land_shape, dtype), own[None], (slot,) + (0,) * own.ndim)


def kernel(x, mix_norm_g, w_in, mu_shift, decay_w0, decay_w2, iclr_a0, iclr_a2, gate_g2, k_k, k_a, r_k, ln_x_w, ln_x_b, attn_out_g, w_out, ffn_norm_g, w_gate, w_up, w_down, final_norm_g, loss_target, m_mix_norm_g, m_w_in, m_mu_shift, m_decay_w0, m_decay_w2, m_iclr_a0, m_iclr_a2, m_gate_g2, m_k_k, m_k_a, m_r_k, m_ln_x_w, m_ln_x_b, m_attn_out_g, m_w_out, m_ffn_norm_g, m_w_gate, m_w_up, m_w_down, m_final_norm_g, v_mix_norm_g, v_w_in, v_mu_shift, v_decay_w0, v_decay_w2, v_iclr_a0, v_iclr_a2, v_gate_g2, v_k_k, v_k_a, v_r_k, v_ln_x_w, v_ln_x_b, v_attn_out_g, v_w_out, v_ffn_norm_g, v_w_gate, v_w_up, v_w_down, v_final_norm_g):
    names = ("mix_norm_g", "w_in", "mu_shift", "decay_w0", "decay_w2", "iclr_a0", "iclr_a2", "gate_g2", "k_k", "k_a",
             "r_k", "ln_x_w", "ln_x_b", "attn_out_g", "w_out", "ffn_norm_g", "w_gate", "w_up", "w_down", "final_norm_g")
    w = dict(zip(names, (mix_norm_g, w_in, mu_shift, decay_w0, decay_w2, iclr_a0, iclr_a2, gate_g2, k_k, k_a, r_k,
                         ln_x_w, ln_x_b, attn_out_g, w_out, ffn_norm_g, w_gate, w_up, w_down, final_norm_g)))
    m = dict(zip(names, (m_mix_norm_g, m_w_in, m_mu_shift, m_decay_w0, m_decay_w2, m_iclr_a0, m_iclr_a2, m_gate_g2,
                         m_k_k, m_k_a, m_r_k, m_ln_x_w, m_ln_x_b, m_attn_out_g, m_w_out, m_ffn_norm_g, m_w_gate,
                         m_w_up, m_w_down, m_final_norm_g)))
    v = dict(zip(names, (v_mix_norm_g, v_w_in, v_mu_shift, v_decay_w0, v_decay_w2, v_iclr_a0, v_iclr_a2, v_gate_g2,
                         v_k_k, v_k_a, v_r_k, v_ln_x_w, v_ln_x_b, v_attn_out_g, v_w_out, v_ffn_norm_g, v_w_gate,
                         v_w_up, v_w_down, v_final_norm_g)))
    first = ("w_in",) + LORAS
    rest = ("w_out", "w_gate", "w_up", "w_down")
    xi, yi, ci = lax.axis_index("x"), lax.axis_index("y"), lax.axis_index("c")
    my_chip, my_dev = 2 * xi + yi, 4 * xi + 2 * yi + ci
    gather, scatter = ("gather",) * 4, ("scatter",) * 4

    sh = lambda z, n: jnp.transpose(z[0]) if n in _TRANSPOSED else z[0]
    mine = [sh(w["w_in"], "w_in").astype(BF16)] + [w[n][0] for n in LORAS]
    early = _swap_start(mine, [_with_own((N_CHIP,) + a.shape, a.dtype, a, my_chip) for a in mine], gather, "gather_first_start")
    wb = {n: (sh(w[n], n) + early[4][0, 0]).astype(BF16) for n in rest}
    lands = [_with_own((N_CHIP,) + wb[n].shape, BF16, wb[n], my_chip) for n in rest]
    ssem, rsem, srcs_thru, lands_thru, tok = _swap_start([wb[n] for n in rest], lands, gather, "gather_rest_start")
    got = _swap_wait(early[0], early[1], early[2], early[3], tok, gather, "gather_first_wait")
    win, w2, a2, g2m = (_full(n, z) for n, z in zip(first, _swap_gathered(got, "gather_first_halves")))

    vecs = {n: w[n].reshape(1, sz) for n, sz in VECS}
    vecs["mix_norm_g"] = vecs["mix_norm_g"] + tok[0, 0]

    def get_rest(after):
        halves = _swap_wait(ssem, rsem, srcs_thru, lands_thru, after, gather, "gather_rest_wait")
        return [_full(n, z) for n, z in zip(rest, _swap_gathered(halves, "gather_rest_halves"))]

    flight = []

    def my_half(g):
        h = g.shape[1] // 2
        return lax.dynamic_slice(g, (my_chip, ci * h, 0), (1, h, g.shape[2]))[0]

    def send_rest(gw):
        gs = [_by_chip(n, gw[n]).astype(BF16) for n in rest]
        into = [_with_own((N_DEV,) + my_half(g).shape, BF16, my_half(g), my_dev) for g in gs]
        flight.extend(_swap_start(gs, into, scatter, "exchange_rest_start"))
        return flight[4]

    loss8, dx, gw, gv = _local_step(x[0], loss_target[0], win, vecs, w2, a2, g2m, get_rest, send_rest)

    small = _rowsum_small([gv[n] for n, _ in VECS], loss8)
    gs = [_by_chip(n, gw[n]).astype(BF16) for n in first]
    into = [_with_own((N_DEV,) + my_half(g).shape, BF16, my_half(g), my_dev) for g in gs]
    into.append(_with_own((N_DEV,) + small.shape, F32, small, my_dev))
    last = _swap_start(gs + [small], into, scatter + ("all",), "exchange_first_start")

    core = jnp.reshape(ci, (1,)).astype(jnp.int32)

    def update(group, rbufs, tag):
        sums = [_reduce8(rb, core, _SUM_TILE[n], "reduce_" + n) for n, rb in zip(group, rbufs)]
        gsum = _join_halves(sums, "join_halves_" + tag)
        out = {}
        for n, g in zip(group, gsum):
            r = _adamw_call(g, sh(w[n], n)[None], sh(m[n], n)[None], sh(v[n], n)[None], _ADAM_TILE[n], "adamw_" + n)
            out[n] = [jnp.transpose(z[0])[None] for z in r] if n in _TRANSPOSED else r
        return out

    res = update(rest, _swap_wait(flight[0], flight[1], flight[2], flight[3], last[4], scatter, "exchange_rest_wait"), "rest")
    got = _swap_wait(last[0], last[1], last[2], last[3], res["w_down"][1], scatter + ("all",), "exchange_first_wait")
    res.update(update(first, got[:4], "first"))
    rows = lambda d: [d[n].reshape(1, sz) for n, sz in VECS]
    small_res = _reduce_adamw_small(got[4], rows(w), rows(m), rows(v))

    outs = []
    for k in range(4):
        piece = {n: r[k] for n, r in res.items()}
        for i, (n, _) in enumerate(VECS):
            piece[n] = small_res[k * len(VECS) + i].reshape(w[n].shape)
        outs.extend(piece[n] for n in names)
    return (small_res[-1][0, 0], dx[None], *outs)
```

```python
import jax
import jax.numpy as jnp
from jax import lax
from jax.experimental import pallas as pl
from jax.experimental.pallas import tpu as pltpu

F32 = jnp.float32
BF16 = jnp.bfloat16

D_MODEL = 1024
HEAD_DIM = 64
RW = 512
N_PAIR = RW // 128
SHIFT_COLS = 1792
IN_COLS = 3328
D_FF = 2816
FF_CHUNK = 256
NORM_EPS = 1e-6
GN_EPS = 64e-5
CHUNK = 64
SUB = 16
WKV_PASSES = 1
ATTN_PASSES = 1
ATTN_BLOCK = 128
DILATIONS = (1, 4, 16)
NEG = -1e30
ADAM_LR, ADAM_B1, ADAM_B2, ADAM_EPS, ADAM_WD, ADAM_STEP = 0.001, 0.9, 0.999, 1e-08, 0.01, 10
VMEM_LIMIT = 56 * 1024 * 1024
MESH = pl.DeviceIdType.MESH


def _params(sem=None, **kw):
    return pltpu.CompilerParams(dimension_semantics=sem, vmem_limit_bytes=VMEM_LIMIT, **kw)


def _dot(a, b, prec=None):
    return lax.dot_general(a, b, (((1,), (0,)), ((), ())), preferred_element_type=F32, precision=prec)


def _dot_nt(a, b, prec=None):
    return lax.dot_general(a, b, (((1,), (1,)), ((), ())), preferred_element_type=F32, precision=prec)


def _dot_tn(a, b, prec=None):
    return lax.dot_general(a, b, (((0,), (0,)), ((), ())), preferred_element_type=F32, precision=prec)


_FORMS = {"nn": ((1,), (0,)), "nt": ((1,), (1,)), "tn": ((0,), (0,))}


def _dg(a, b, form):
    if a.ndim == 3 or b.ndim == 3:
        nb = a.shape[0] if a.ndim == 3 else b.shape[0]
        return jnp.stack([_dg(a[i] if a.ndim == 3 else a, b[i] if b.ndim == 3 else b, form) for i in range(nb)], axis=0)
    return lax.dot_general(a, b, (_FORMS[form], ((), ())), preferred_element_type=F32)


def _split2(x):
    hi = x.astype(BF16)
    return hi, (x - hi.astype(F32)).astype(BF16)


def _split3(x):
    hi = x.astype(BF16)
    rest = x - hi.astype(F32)
    mid = rest.astype(BF16)
    return hi, mid, (rest - mid.astype(F32)).astype(BF16)


def _mm_raw(a, b, form, mode):
    if mode == 1:
        return _dg(a.astype(BF16), b.astype(BF16), form)
    if mode == 3:
        ah, al = _split2(a)
        bh, bl = _split2(b)
        return _dg(ah, bh, form) + (_dg(ah, bl, form) + _dg(al, bh, form))
    if mode == "L3":
        ab = a.astype(BF16)
        b1, b2, b3 = _split3(b)
        if form == "nn":
            n = b.shape[-1]
            wide = _dg(ab, jnp.concatenate([b1, b2, b3], axis=-1), form)
            return wide[..., :n] + (wide[..., n:2 * n] + wide[..., 2 * n:])
        return _dg(ab, b1, form) + (_dg(ab, b2, form) + _dg(ab, b3, form))
    assert mode == "R3", mode
    bb = b.astype(BF16)
    a1, a2, a3 = _split3(a)
    if form in ("nn", "nt"):
        m = a.shape[-2]
        tall = _dg(jnp.concatenate([a1, a2, a3], axis=-2), bb, form)
        return tall[..., :m, :] + (tall[..., m:2 * m, :] + tall[..., 2 * m:, :])
    return _dg(a1, bb, form) + (_dg(a2, bb, form) + _dg(a3, bb, form))


def _mm(a, b, form, mode):
    @jax.custom_vjp
    def f(a, b):
        return _mm_raw(a, b, form, mode)

    def fwd(a, b):
        return _mm_raw(a, b, form, mode), (a, b)

    def bwd(res, ct):
        a, b = res
        la = {1: 1, 3: 3, "L3": None, "R3": "R3"}[mode]
        lb = {1: 1, 3: 3, "L3": "L3", "R3": None}[mode]
        if form == "nn":
            da = None if la is None else _mm_raw(ct, b, "nt", la)
            db = None if lb is None else _mm_raw(a, ct, "tn", lb)
        elif form == "nt":
            da = None if la is None else _mm_raw(ct, b, "nn", la)
            db = None if lb is None else _mm_raw(ct, a, "tn", "R3" if lb == "L3" else lb)
        else:
            da = None if la is None else _mm_raw(b, ct, "nt", "L3" if la == "R3" else la)
            db = None if lb is None else _mm_raw(a, ct, "nn", lb)
        return (jnp.zeros_like(a) if da is None else da, jnp.zeros_like(b) if db is None else db)

    f.defvjp(fwd, bwd)
    return f(a, b)


def _seg_ones(n):
    r = lax.broadcasted_iota(jnp.int32, (n, n), 0) // HEAD_DIM
    c = lax.broadcasted_iota(jnp.int32, (n, n), 1) // HEAD_DIM
    return (r == c).astype(F32)


def _segsum(x, seg):
    return _mm(x, seg, "nn", "R3")


def _rms_fwd(x, g):
    rstd = lax.rsqrt(jnp.mean(x * x, axis=-1, keepdims=True) + NORM_EPS)
    return x * rstd * g


def _rms_bwd(dy, x, g):
    rstd = lax.rsqrt(jnp.mean(x * x, axis=-1, keepdims=True) + NORM_EPS)
    xn = x * rstd
    dxn = dy * g
    dx = rstd * (dxn - xn * jnp.mean(dxn * xn, axis=-1, keepdims=True))
    return dx, dy * xn


def _sigmoid(x):
    return 1.0 / (1.0 + jnp.exp(-x))


def _softplus(x):
    return jnp.maximum(x, 0.0) + jnp.log(1.0 + jnp.exp(-jnp.abs(x)))


def _acc(ref, val, first):
    @pl.when(first)
    def _():
        ref[...] = val

    @pl.when(jnp.logical_not(first))
    def _():
        ref[...] += val


def _colsum8(v):
    rows, n = v.shape
    return jnp.sum(v.reshape(rows // 8, 8, n), axis=0)


def _prep_fn(p, pprev, mu, w0, w2p, a0, a2p, g2, k_k, k_a):
    seg = _seg_ones(RW)
    ps = p + (pprev - p) * mu
    r = ps[:, 0:RW]
    k = ps[:, RW:2 * RW]
    v = ps[:, 2 * RW:3 * RW]
    xwa = ps[:, 3 * RW:3 * RW + 128]
    xg = ps[:, 3 * RW + 128:3 * RW + 256]
    wraw = -_softplus(-(w0 + _mm(jnp.tanh(xwa), w2p, "nn", 3))) - 0.5
    lw = -jnp.exp(wraw)
    a = _sigmoid(a0 + _mm(xwa, a2p, "nn", 3))
    g = _mm(_sigmoid(xg), g2, "nn", 3)
    kk = k * k_k
    kk = kk / jnp.maximum(jnp.sqrt(_segsum(kk * kk, seg)), 1e-12)
    k2 = k * (1.0 + (a - 1.0) * k_a)
    return r, lw, k2, v, kk, a, g


def _transposed(z):
    return jnp.stack([z[i].T for i in range(z.shape[0])], axis=0) if z.ndim == 3 else z.T


def _solve_unit_lower(lmat, rhs):
    c = lmat.shape[-1]
    row = lax.broadcasted_iota(jnp.int32, (c, c), 0)
    col = lax.broadcasted_iota(jnp.int32, (c, c), 1)
    eye = (row == col).astype(F32)
    ld = jnp.where(row // SUB == col // SUB, lmat, 0.0)
    lo = lmat - ld
    x = eye + ld
    m = ld
    mm = lambda p, q: _mm(p, q, "nn", WKV_PASSES)
    cat = jnp.concatenate
    m = mm(m, m)
    for _ in range(2):
        mx = mm(m, cat([m, x], axis=-1))
        m, x = mx[..., :c], x + mx[..., c:]
    x = x + mm(m, x)
    gw = mm(x, cat([lo, rhs], axis=-1))
    g, w = gw[..., :c], gw[..., c:]
    gg = mm(g, cat([g, w], axis=-1))
    w = w + gg[..., c:]
    return w + mm(gg[..., :c], w)


def _wkv_chunk_fn(s0, r, lw, k, v, kk, a):
    c = r.shape[-2]
    n = 2 * c
    row = lax.broadcasted_iota(jnp.int32, (n, n), 0)
    col = lax.broadcasted_iota(jnp.int32, (n, n), 1)
    same = (row // c) == (col // c)
    incl = jnp.logical_and(row >= col, same)
    strict = jnp.logical_and(row > col, same)
    sel = (lax.broadcasted_iota(jnp.int32, (n, 128), 0) // c) == (lax.broadcasted_iota(jnp.int32, (n, 128), 1) // HEAD_DIM)
    two = lambda z: jnp.concatenate([z, z], axis=-2)
    lw2 = two(lw)
    mm = lambda p_, q_, form: _mm(p_, q_, form, WKV_PASSES)
    cl = _mm(incl.astype(F32), lw2, "nn", "L3")
    p = jnp.exp(cl)
    pinv = jnp.exp(-cl)
    pprev = jnp.exp(cl - lw2)
    kk2 = two(kk)
    at = jnp.where(sel, -kk2 * pprev, 0.0)
    bt = jnp.where(sel, kk2 * two(a) * pinv, 0.0)
    kt = jnp.where(sel, two(k) * pinv, 0.0)
    rt = jnp.where(sel, two(r) * p, 0.0)
    vt = jnp.where(sel, two(v), 0.0)
    cat = jnp.concatenate
    bk = cat([bt, kt], axis=-2)
    arbk = mm(cat([at, rt], axis=-2), bk, "nt")
    ab, ak = jnp.where(strict, arbk[..., :n, :n], 0.0), jnp.where(strict, arbk[..., :n, n:], 0.0)
    rb, rk = jnp.where(incl, arbk[..., n:, :n], 0.0), jnp.where(incl, arbk[..., n:, n:], 0.0)
    s0t = _transposed(s0)
    u = _solve_unit_lower(ab, mm(cat([at, ak], axis=-1), cat([s0t, vt], axis=-2), "nn"))
    y2 = mm(cat([rt, rb, rk], axis=-1), cat([s0t, u, vt], axis=-2), "nn")
    plast = jnp.exp(jnp.sum(lw, axis=-2, keepdims=True))
    s1 = (s0 + mm(cat([u, vt], axis=-2), bk, "tn")) * plast
    r2 = lax.broadcasted_iota(jnp.int32, (128, 128), 0) // HEAD_DIM
    c2 = lax.broadcasted_iota(jnp.int32, (128, 128), 1) // HEAD_DIM
    return y2[..., :c, :] + y2[..., c:, :], jnp.where(r2 == c2, s1, 0.0)


def _post_fn(y, r, k2, v, g, lnw, lnb, rk):
    seg = _seg_ones(RW)
    mean = _segsum(y, seg) * (1.0 / HEAD_DIM)
    yc = y - mean
    var = _segsum(yc * yc, seg) * (1.0 / HEAD_DIM)
    yn = yc * lax.rsqrt(var + GN_EPS)
    out = yn * lnw + lnb + _segsum(r * k2 * rk, seg) * v
    return out * g


def _attn_block_fn(q, kc, vc, kp=None, vp=None):
    n = ATTN_BLOCK
    qi = lax.broadcasted_iota(jnp.int32, (n, n), 0)
    kj = lax.broadcasted_iota(jnp.int32, (n, n), 1)
    lane = lax.broadcasted_iota(jnp.int32, (1, 128), 1)
    scale = HEAD_DIM ** -0.5
    valid = kj <= qi
    keys, vals = kc, vc
    if kp is not None:
        valid = jnp.concatenate([valid, kj >= qi], axis=-1)
        keys, vals = jnp.concatenate([kc, kp], axis=-2), jnp.concatenate([vc, vp], axis=-2)
    m0 = (lane // HEAD_DIM) == 0
    q2 = jnp.concatenate([jnp.where(m0, q, 0.0), jnp.where(m0, 0.0, q)], axis=-2)
    valid2 = jnp.concatenate([valid, valid], axis=-2)
    s = jnp.where(valid2, _mm(q2, keys, "nt", ATTN_PASSES) * scale, NEG)
    m = jnp.max(s, axis=-1, keepdims=True)
    p = jnp.exp(s - m)
    den = jnp.sum(p, axis=-1, keepdims=True)
    o2 = _mm(p, vals, "nn", ATTN_PASSES) / den
    l2 = m + jnp.log(den)
    return jnp.where(m0, o2[..., :n, :], o2[..., n:, :]), jnp.where(m0, l2[..., :n, :], l2[..., n:, :])


def _attn_block_bwd(q, kc, vc, kp, vp, o, lse, do, dl):
    n = ATTN_BLOCK
    cat = jnp.concatenate
    qi = lax.broadcasted_iota(jnp.int32, (n, n), 0)
    kj = lax.broadcasted_iota(jnp.int32, (n, n), 1)
    m0 = (lax.broadcasted_iota(jnp.int32, (1, 128), 1) // HEAD_DIM) == 0
    scale = HEAD_DIM ** -0.5
    valid = kj <= qi
    keys, vals = kc, vc
    if kp is not None:
        valid = cat([valid, kj >= qi], axis=-1)
        keys, vals = cat([kc, kp], axis=-2), cat([vc, vp], axis=-2)
    stack = lambda z: cat([jnp.where(m0, z, 0.0), jnp.where(m0, 0.0, z)], axis=-2)
    q2, do2 = stack(q), stack(do)
    lse2 = cat([jnp.max(jnp.where(m0, lse, NEG), axis=-1, keepdims=True),
                jnp.max(jnp.where(m0, NEG, lse), axis=-1, keepdims=True)], axis=-2)
    delta = jnp.sum(do2 * cat([o, o], axis=-2), axis=-1, keepdims=True)
    dlse = jnp.sum(stack(dl), axis=-1, keepdims=True)
    mm = lambda a, b, form: _mm_raw(a, b, form, ATTN_PASSES)
    s = jnp.where(cat([valid, valid], axis=-2), mm(q2, keys, "nt") * scale, NEG)
    p = jnp.exp(s - lse2)
    ds = p * (mm(do2, vals, "nt") - delta + dlse)
    dq2 = mm(ds, keys, "nn") * scale
    dq = jnp.where(m0, dq2[..., :n, :], dq2[..., n:, :])
    dkeys = mm(ds, q2, "tn") * scale
    dvals = mm(p, do2, "tn")
    if kp is None:
        return dq, dkeys, dvals
    return dq, dkeys[..., :n, :], dvals[..., :n, :], dkeys[..., n:, :], dvals[..., n:, :]


def _combine_fn(o1, o2, o3, l1, l2, l3, og):
    seg = _seg_ones(o1.shape[-1])
    m = jnp.maximum(jnp.maximum(l1, l2), l3)
    e1, e2, e3 = jnp.exp(l1 - m), jnp.exp(l2 - m), jnp.exp(l3 - m)
    o = (e1 * o1 + e2 * o2 + e3 * o3) / (e1 + e2 + e3)
    o = o * lax.rsqrt(_segsum(o * o, seg) * (1.0 / HEAD_DIM) + NORM_EPS)
    return o * og


def _in_proj(x, g1, win):
    t = x.shape[0]
    tm = 512

    def body(x_ref, g_ref, w_ref, h_ref, pa_ref, qkv_ref):
        h = _rms_fwd(x_ref[...], g_ref[...]).astype(BF16)
        h_ref[...] = h
        proj = _dot_nt(h, w_ref[...])
        pa_ref[...] = proj[:, :SHIFT_COLS]
        for j in range(3):
            for p in range(N_PAIR):
                c0 = SHIFT_COLS + j * RW + p * 128
                qkv_ref[j, p] = proj[:, c0:c0 + 128]

    return pl.pallas_call(
        body, name="in_proj", grid=(t // tm,),
        in_specs=[pl.BlockSpec((tm, D_MODEL), lambda i: (i, 0)), pl.BlockSpec((1, D_MODEL), lambda i: (0, 0)),
                  pl.BlockSpec((IN_COLS, D_MODEL), lambda i: (0, 0))],
        out_specs=[pl.BlockSpec((tm, D_MODEL), lambda i: (i, 0)), pl.BlockSpec((tm, SHIFT_COLS), lambda i: (i, 0)),
                   pl.BlockSpec((3, N_PAIR, tm, 128), lambda i: (0, 0, i, 0))],
        out_shape=[jax.ShapeDtypeStruct((t, D_MODEL), BF16), jax.ShapeDtypeStruct((t, SHIFT_COLS), F32),
                   jax.ShapeDtypeStruct((3, N_PAIR, t, 128), F32)],
        compiler_params=_params(("parallel",)),
    )(x, g1, win)


def _shifted(p, last8, first):
    prow = jnp.where(first, 0.0, last8[7:8, :])
    rolled = pltpu.roll(p, 1, axis=0)
    rid = lax.broadcasted_iota(jnp.int32, p.shape, 0)
    return jnp.where(rid == 0, prow, rolled)


_PREP_TM = 256


def _prep_specs(tm):
    vec = lambda n: pl.BlockSpec((1, n), lambda i: (0, 0))
    mat = lambda r, n: pl.BlockSpec((r, n), lambda i: (0, 0))
    return [vec(SHIFT_COLS), vec(RW), mat(128, RW), vec(RW), mat(128, RW), mat(128, RW), vec(RW), vec(RW)]


def _prep_fwd(proj, pw):
    t = proj.shape[0]
    tm = _PREP_TM

    def body(p_ref, l8_ref, mu, w0, w2p, a0, a2p, g2, k_k, k_a, *outs):
        p = p_ref[...]
        pprev = _shifted(p, l8_ref[...], pl.program_id(0) == 0)
        res = _prep_fn(p, pprev, mu[...], w0[...], w2p[...], a0[...], a2p[...], g2[...], k_k[...], k_a[...])
        for o_ref, val in zip(outs, res):
            o_ref[...] = val

    row = pl.BlockSpec((tm, RW), lambda i: (i, 0))
    return pl.pallas_call(
        body, name="rwkv_prep", grid=(t // tm,),
        in_specs=[pl.BlockSpec((tm, SHIFT_COLS), lambda i: (i, 0)),
                  pl.BlockSpec((8, SHIFT_COLS), lambda i: (jnp.maximum(i * (tm // 8) - 1, 0), 0))] + _prep_specs(tm),
        out_specs=[row] * 7,
        out_shape=[jax.ShapeDtypeStruct((t, RW), F32)] * 7,
        compiler_params=_params(("parallel",)),
    )(proj, proj, *pw)


def _pairs(ref):
    return jnp.stack([ref[:, 128 * p:128 * (p + 1)] for p in range(N_PAIR)], axis=0)


def _wkv_fwd(r, lw, k2, v, kk, a):
    t = r.shape[0]
    nc = t // CHUNK

    def body(r_ref, lw_ref, k_ref, v_ref, kk_ref, a_ref, y_ref, s_ref, st):
        @pl.when(pl.program_id(0) == 0)
        def _():
            st[...] = jnp.zeros_like(st)

        s0 = st[...]
        s_ref[0] = s0
        y, s1 = _wkv_chunk_fn(s0, *[_pairs(ref) for ref in (r_ref, lw_ref, k_ref, v_ref, kk_ref, a_ref)])
        for p in range(N_PAIR):
            y_ref[:, 128 * p:128 * (p + 1)] = y[p]
        st[...] = s1

    blk = pl.BlockSpec((CHUNK, RW), lambda c: (c, 0))
    return pl.pallas_call(
        body, name="wkv_fwd", grid=(nc,),
        in_specs=[blk] * 6,
        out_specs=[blk, pl.BlockSpec((1, N_PAIR, 128, 128), lambda c: (c, 0, 0, 0))],
        out_shape=[jax.ShapeDtypeStruct((t, RW), F32), jax.ShapeDtypeStruct((nc, N_PAIR, 128, 128), F32)],
        scratch_shapes=[pltpu.VMEM((N_PAIR, 128, 128), F32)],
        compiler_params=_params(("arbitrary",)),
    )(r, lw, k2, v, kk, a)


_POST_TM = 256


ATTN_GROUP = 2


def _dilated_rows(d, r, n):
    if d == 1:
        return pl.ds(pl.multiple_of(n * ATTN_BLOCK, ATTN_BLOCK), ATTN_BLOCK)
    return pl.ds(r + n * (ATTN_BLOCK * d), ATTN_BLOCK, stride=d)


def _for_each_sequence(t, unit):
    for di, d in enumerate(DILATIONS):

        @pl.when(pl.program_id(1) == di)
        def _(di=di, d=d):
            nb = t // (ATTN_BLOCK * d)
            if d == 1:
                unit(di, [(d, 0, 0)], False)
                unit(di, [(d, 0, 1)], True)
                lax.fori_loop(1, nb // 2, lambda k, c: (unit(di, [(d, 0, 2 * k), (d, 0, 2 * k + 1)], True), c)[1], 0)
            else:

                def residues(r, carry):
                    unit(di, [(d, r, 0), (d, r + d // 2, 0)], False)
                    if nb > 1:
                        lax.fori_loop(1, nb, lambda n, c: (unit(di, [(d, r, n), (d, r + d // 2, n)], True), c)[1], 0)
                    return carry

                lax.fori_loop(0, d // 2, residues, 0)


def _take(ref, lead, rows_list):
    return jnp.stack([ref.at[(*lead, g)][rows, :] for rows in rows_list for g in range(ATTN_GROUP)], axis=0)


def _put(ref, lead, rows_list, val, add=False):
    k = 0
    for rows in rows_list:
        for g in range(ATTN_GROUP):
            if add:
                ref.at[(*lead, g)][rows, :] += val[k]
            else:
                ref.at[(*lead, g)][rows, :] = val[k]
            k += 1


def _attn_fwd(qkv):
    t = qkv.shape[2]

    def body(q_ref, k_ref, v_ref, o_ref, l_ref):
        def unit(di, places, has_prev):
            cur = [_dilated_rows(d, r, n) for d, r, n in places]
            args = [_take(ref, (0,), cur) for ref in (q_ref, k_ref, v_ref)]
            if has_prev:
                prv = [_dilated_rows(d, r, n - 1) for d, r, n in places]
                args += [_take(ref, (0,), prv) for ref in (k_ref, v_ref)]
            o, lse = _attn_block_fn(*args)
            _put(o_ref, (0,), cur, o)
            _put(l_ref, (0,), cur, lse)

        _for_each_sequence(t, unit)

    spec = lambda j: pl.BlockSpec((1, ATTN_GROUP, t, 128), lambda i, b: (j, i, 0, 0))
    out = pl.BlockSpec((1, ATTN_GROUP, t, 128), lambda i, b: (b, i, 0, 0))
    return pl.pallas_call(
        body, name="attn_fwd", grid=(N_PAIR // ATTN_GROUP, len(DILATIONS)),
        in_specs=[spec(0), spec(1), spec(2)], out_specs=[out, out],
        out_shape=[jax.ShapeDtypeStruct((3, N_PAIR, t, 128), F32)] * 2,
        compiler_params=_params(("parallel", "arbitrary")),
    )(qkv, qkv, qkv)


_COMB_TM = 256


def _mixers_out(y, r, k2, v, g, lnw, lnb, rk, o, l, og):
    t = y.shape[0]
    tm = _COMB_TM

    def body(y_ref, r_ref, k_ref, v_ref, g_ref, lnw_ref, lnb_ref, rk_ref, o_ref, l_ref, og_ref, out_ref):
        out_ref[:, :RW] = _post_fn(y_ref[...], r_ref[...], k_ref[...], v_ref[...], g_ref[...],
                                   lnw_ref[...], lnb_ref[...], rk_ref[...]).astype(BF16)
        for p in range(N_PAIR):
            cols = slice(128 * p, 128 * (p + 1))
            out_ref[:, RW + 128 * p:RW + 128 * (p + 1)] = _combine_fn(
                o_ref[0, p], o_ref[1, p], o_ref[2, p], l_ref[0, p], l_ref[1, p], l_ref[2, p], og_ref[:, cols]).astype(BF16)

    row = pl.BlockSpec((tm, RW), lambda i: (i, 0))
    vec = pl.BlockSpec((1, RW), lambda i: (0, 0))
    blk = pl.BlockSpec((3, N_PAIR, tm, 128), lambda i: (0, 0, i, 0))
    return pl.pallas_call(
        body, name="mixers_out", grid=(t // tm,),
        in_specs=[row] * 5 + [vec] * 3 + [blk, blk, vec], out_specs=pl.BlockSpec((tm, D_MODEL), lambda i: (i, 0)),
        out_shape=jax.ShapeDtypeStruct((t, D_MODEL), BF16),
        compiler_params=_params(("parallel",)),
    )(y, r, k2, v, g, lnw, lnb, rk, o, l, og)


def _ffn_all(x, ycat, wg, wu, wd, wout, g2, gf, tgt):
    t = x.shape[0]
    tm = 256

    def body(x_ref, y_ref, wg_ref, wu_ref, wd_ref, wo_ref, g2_ref, gf_ref, t_ref,
             h_ref, act_ref, dx2b_ref, dgt_ref, dup_ref, dx1b_ref, dx1_ref, dya_ref, dyb_ref, loss_ref, dgf_ref, dg2_ref,
             gt_s, up_s):
        first = pl.program_id(0) == 0
        x1 = x_ref[...] + _dot(y_ref[...], wo_ref[...])
        h = _rms_fwd(x1, g2_ref[...]).astype(BF16)
        h_ref[...] = h
        for c0 in range(0, D_FF, FF_CHUNK):
            cols = slice(c0, c0 + FF_CHUNK)
            gt = _dot_nt(h, wg_ref[cols, :])
            up = _dot_nt(h, wu_ref[cols, :])
            gt_s[:, cols] = gt.astype(BF16)
            up_s[:, cols] = up.astype(BF16)
            act_ref[:, cols] = (gt * _sigmoid(gt) * up).astype(BF16)
        x2 = x1 + _dot(act_ref[...], wd_ref[...])
        gf_ = gf_ref[...]
        diff = _rms_fwd(x2, gf_) - t_ref[...]
        lrow = 0.5 * jnp.sum(_colsum8(diff * diff), axis=1, keepdims=True) * (1.0 / D_MODEL)
        _acc(loss_ref, jnp.broadcast_to(lrow, (8, 128)), first)
        dx2, dgr = _rms_bwd(diff * (1.0 / D_MODEL), x2, gf_)
        _acc(dgf_ref, _colsum8(dgr), first)
        dx2b = dx2.astype(BF16)
        dx2b_ref[...] = dx2b
        for c0 in range(0, D_FF, FF_CHUNK):
            cols = slice(c0, c0 + FF_CHUNK)
            dact = _dot_nt(dx2b, wd_ref[cols, :])
            gt = gt_s[:, cols].astype(F32)
            sg = _sigmoid(gt)
            dgt_ref[:, cols] = (dact * up_s[:, cols].astype(F32) * sg * (1.0 + gt * (1.0 - sg))).astype(BF16)
            dup_ref[:, cols] = (dact * gt * sg).astype(BF16)
        dh = _dot(dgt_ref[...], wg_ref[...]) + _dot(dup_ref[...], wu_ref[...])
        dxn, dgr2 = _rms_bwd(dh, x1, g2_ref[...])
        _acc(dg2_ref, _colsum8(dgr2), first)
        dx1 = dx2 + dxn
        dx1_ref[...] = dx1
        dx1b = dx1.astype(BF16)
        dx1b_ref[...] = dx1b
        dy = _dot_nt(dx1b, wo_ref[...])
        dya_ref[...] = dy[:, :RW]
        dyb_ref[...] = dy[:, RW:]

    row = pl.BlockSpec((tm, D_MODEL), lambda i: (i, 0))
    wide = pl.BlockSpec((tm, D_FF), lambda i: (i, 0))
    half = pl.BlockSpec((tm, RW), lambda i: (i, 0))
    wsp = pl.BlockSpec((D_FF, D_MODEL), lambda i: (0, 0))
    vec = pl.BlockSpec((1, D_MODEL), lambda i: (0, 0))
    part = pl.BlockSpec((8, D_MODEL), lambda i: (0, 0))
    bf = lambda n: jax.ShapeDtypeStruct((t, n), BF16)
    return pl.pallas_call(
        body, name="ffn_all", grid=(t // tm,),
        in_specs=[row, row, wsp, wsp, wsp, pl.BlockSpec((D_MODEL, D_MODEL), lambda i: (0, 0)), vec, vec, row],
        out_specs=[row, wide, row, wide, wide, row, row, half, half, pl.BlockSpec((8, 128), lambda i: (0, 0)), part, part],
        out_shape=[bf(D_MODEL), bf(D_FF), bf(D_MODEL), bf(D_FF), bf(D_FF), bf(D_MODEL),
                   jax.ShapeDtypeStruct((t, D_MODEL), F32), jax.ShapeDtypeStruct((t, RW), F32),
                   jax.ShapeDtypeStruct((t, RW), F32), jax.ShapeDtypeStruct((8, 128), F32),
                   jax.ShapeDtypeStruct((8, D_MODEL), F32), jax.ShapeDtypeStruct((8, D_MODEL), F32)],
        scratch_shapes=[pltpu.VMEM((tm, D_FF), BF16), pltpu.VMEM((tm, D_FF), BF16)],
        compiler_params=_params(("arbitrary",)),
    )(x, ycat, wg, wu, wd, wout, g2, gf, tgt)


def _wgrad(a, b, tk, tn, name):
    t, kdim = a.shape
    ndim = b.shape[1]

    def body(a_ref, b_ref, o_ref):
        o_ref[...] = _dot_tn(a_ref[...], b_ref[...])

    return pl.pallas_call(
        body, name=name, grid=(kdim // tk, ndim // tn),
        in_specs=[pl.BlockSpec((t, tk), lambda i, j: (0, i)), pl.BlockSpec((t, tn), lambda i, j: (0, j))],
        out_specs=pl.BlockSpec((tk, tn), lambda i, j: (i, j)),
        out_shape=jax.ShapeDtypeStruct((kdim, ndim), F32),
        compiler_params=_params(("parallel", "parallel")),
    )(a, b)


def _post_bwd(dya, y, r, k2, v, g, lnw, lnb, rk):
    t = y.shape[0]
    tm = _POST_TM

    def body(d_ref, y_ref, r_ref, k_ref, v_ref, g_ref, lnw_ref, lnb_ref, rk_ref,
             dy_ref, dr_ref, dk_ref, dv_ref, dg_ref, dlnw_ref, dlnb_ref, drk_ref):
        first = pl.program_id(0) == 0
        ones = jnp.ones((tm, 1), F32)
        prim = (y_ref[...], r_ref[...], k_ref[...], v_ref[...], g_ref[...],
                ones * lnw_ref[...], ones * lnb_ref[...], ones * rk_ref[...])
        _, vjp = jax.vjp(_post_fn, *prim)
        dy, dr, dk, dv, dg, dlnw, dlnb, drk = vjp(d_ref[...])
        dy_ref[...] = dy
        dr_ref[...] = dr
        dk_ref[...] = dk
        dv_ref[...] = dv
        dg_ref[...] = dg
        _acc(dlnw_ref, _colsum8(dlnw), first)
        _acc(dlnb_ref, _colsum8(dlnb), first)
        _acc(drk_ref, _colsum8(drk), first)

    row = pl.BlockSpec((tm, RW), lambda i: (i, 0))
    vec = pl.BlockSpec((1, RW), lambda i: (0, 0))
    part = pl.BlockSpec((8, RW), lambda i: (0, 0))
    return pl.pallas_call(
        body, name="rwkv_post_bwd", grid=(t // tm,),
        in_specs=[row] * 6 + [vec] * 3, out_specs=[row] * 5 + [part] * 3,
        out_shape=[jax.ShapeDtypeStruct((t, RW), F32)] * 5 + [jax.ShapeDtypeStruct((8, RW), F32)] * 3,
        compiler_params=_params(("arbitrary",)),
    )(dya, y, r, k2, v, g, lnw, lnb, rk)


def _wkv_bwd(dy, s0s, r, lw, k2, v, kk, a):
    t = r.shape[0]
    nc = t // CHUNK

    def body(dy_ref, s_ref, r_ref, lw_ref, k_ref, v_ref, kk_ref, a_ref,
             dr_ref, dlw_ref, dk_ref, dv_ref, dkk_ref, da_ref, ds):
        @pl.when(pl.program_id(0) == 0)
        def _():
            ds[...] = jnp.zeros_like(ds)

        _, vjp = jax.vjp(_wkv_chunk_fn, s_ref[0],
                         *[_pairs(ref) for ref in (r_ref, lw_ref, k_ref, v_ref, kk_ref, a_ref)])
        res = vjp((_pairs(dy_ref), ds[...]))
        ds[...] = res[0]
        for ref, val in zip((dr_ref, dlw_ref, dk_ref, dv_ref, dkk_ref, da_ref), res[1:]):
            for p in range(N_PAIR):
                ref[:, 128 * p:128 * (p + 1)] = val[p]

    blk = pl.BlockSpec((CHUNK, RW), lambda c: (nc - 1 - c, 0))
    return pl.pallas_call(
        body, name="wkv_bwd", grid=(nc,),
        in_specs=[blk, pl.BlockSpec((1, N_PAIR, 128, 128), lambda c: (nc - 1 - c, 0, 0, 0))] + [blk] * 6,
        out_specs=[blk] * 6,
        out_shape=[jax.ShapeDtypeStruct((t, RW), F32)] * 6,
        scratch_shapes=[pltpu.VMEM((N_PAIR, 128, 128), F32)],
        compiler_params=_params(("arbitrary",)),
    )(dy, s0s, r, lw, k2, v, kk, a)


def _prep_bwd(proj, pw, douts):
    t = proj.shape[0]
    tm = _PREP_TM
    nt = t // tm

    def body(p_ref, l8_ref, mu, w0, w2p, a0, a2p, g2, k_k, k_a, dr, dr2, dlw, dk2, dk22, dv, dv2, dkk, da, dg,
             dp_ref, dmu_ref, dw0_ref, dw2_ref, da0_ref, da2_ref, dg2_ref, dkk_ref, dka_ref, carry):
        i = pl.program_id(0)
        first = i == 0

        @pl.when(first)
        def _():
            carry[...] = jnp.zeros_like(carry)

        p = p_ref[...]
        pprev = _shifted(p, l8_ref[...], i == nt - 1)
        ones = jnp.ones((tm, 1), F32)
        prim = (p, pprev, ones * mu[...], ones * w0[...], w2p[...], ones * a0[...], a2p[...], g2[...],
                ones * k_k[...], ones * k_a[...])
        _, vjp = jax.vjp(_prep_fn, *prim)
        dp, dpp, dmu, dw0, dw2, da0, da2, dg2, dkk_, dka = vjp(
            (dr[...] + dr2[...], dlw[...], dk2[...] + dk22[...], dv[...] + dv2[...], dkk[...], da[...], dg[...]))
        up = pltpu.roll(dpp, tm - 1, axis=0)
        rid = lax.broadcasted_iota(jnp.int32, dpp.shape, 0)
        dp_ref[...] = dp + jnp.where(rid == tm - 1, carry[0:1, :], up)
        carry[...] = jnp.broadcast_to(dpp[0:1, :], carry.shape)
        _acc(dmu_ref, _colsum8(dmu), first)
        _acc(dw0_ref, _colsum8(dw0), first)
        _acc(dw2_ref, dw2, first)
        _acc(da0_ref, _colsum8(da0), first)
        _acc(da2_ref, da2, first)
        _acc(dg2_ref, dg2, first)
        _acc(dkk_ref, _colsum8(dkk_), first)
        _acc(dka_ref, _colsum8(dka), first)

    rev = lambda i: (nt - 1 - i, 0)
    row = pl.BlockSpec((tm, RW), rev)
    part = lambda n: pl.BlockSpec((8, n), lambda i: (0, 0))
    mat = pl.BlockSpec((128, RW), lambda i: (0, 0))
    return pl.pallas_call(
        body, name="rwkv_prep_bwd", grid=(nt,),
        in_specs=[pl.BlockSpec((tm, SHIFT_COLS), rev),
                  pl.BlockSpec((8, SHIFT_COLS), lambda i: (jnp.maximum((nt - 1 - i) * (tm // 8) - 1, 0), 0))]
                 + _prep_specs(tm) + [row] * 10,
        out_specs=[pl.BlockSpec((tm, SHIFT_COLS), rev), part(SHIFT_COLS), part(RW), mat, part(RW), mat, mat,
                   part(RW), part(RW)],
        out_shape=[jax.ShapeDtypeStruct((t, SHIFT_COLS), F32), jax.ShapeDtypeStruct((8, SHIFT_COLS), F32),
                   jax.ShapeDtypeStruct((8, RW), F32), jax.ShapeDtypeStruct((128, RW), F32),
                   jax.ShapeDtypeStruct((8, RW), F32), jax.ShapeDtypeStruct((128, RW), F32),
                   jax.ShapeDtypeStruct((128, RW), F32), jax.ShapeDtypeStruct((8, RW), F32),
                   jax.ShapeDtypeStruct((8, RW), F32)],
        scratch_shapes=[pltpu.VMEM((8, SHIFT_COLS), F32)],
        compiler_params=_params(("arbitrary",)),
    )(proj, proj, *pw, *douts)


def _combine_bwd(dyb, o, l, og):
    t = dyb.shape[0]
    tm = _COMB_TM

    def body(d_ref, o_ref, l_ref, og_ref, do_ref, dl_ref, dog_ref):
        ones = jnp.ones((tm, 1), F32)
        dog = []
        for p in range(N_PAIR):
            cols = slice(128 * p, 128 * (p + 1))
            _, vjp = jax.vjp(_combine_fn, o_ref[0, p], o_ref[1, p], o_ref[2, p], l_ref[0, p], l_ref[1, p], l_ref[2, p],
                             ones * og_ref[:, cols])
            res = vjp(d_ref[:, cols])
            for b in range(3):
                do_ref[b, p] = res[b]
                dl_ref[b, p] = res[3 + b]
            dog.append(_colsum8(res[6]))
        _acc(dog_ref, jnp.concatenate(dog, axis=1), pl.program_id(0) == 0)

    blk = pl.BlockSpec((3, N_PAIR, tm, 128), lambda i: (0, 0, i, 0))
    return pl.pallas_call(
        body, name="attn_combine_bwd", grid=(t // tm,),
        in_specs=[pl.BlockSpec((tm, RW), lambda i: (i, 0)), blk, blk, pl.BlockSpec((1, RW), lambda i: (0, 0))],
        out_specs=[blk, blk, pl.BlockSpec((8, RW), lambda i: (0, 0))],
        out_shape=[jax.ShapeDtypeStruct((3, N_PAIR, t, 128), F32)] * 2 + [jax.ShapeDtypeStruct((8, RW), F32)],
        compiler_params=_params(("arbitrary",)),
    )(dyb, o, l, og)


def _attn_bwd(do, dl, o, lse, qkv):
    t = qkv.shape[2]

    def body(do_ref, dl_ref, o_ref, l_ref, q_ref, k_ref, v_ref, dq_ref, dk_ref, dv_ref):
        @pl.when(pl.program_id(1) == 0)
        def _():
            for ref in (dq_ref, dk_ref, dv_ref):
                ref[...] = jnp.zeros_like(ref)

        def unit(di, places, has_prev):
            cur = [_dilated_rows(d, r, n) for d, r, n in places]
            q, kc, vc = [_take(ref, (0,), cur) for ref in (q_ref, k_ref, v_ref)]
            kp = vp = None
            if has_prev:
                prv = [_dilated_rows(d, r, n - 1) for d, r, n in places]
                kp, vp = [_take(ref, (0,), prv) for ref in (k_ref, v_ref)]
            res = _attn_block_bwd(q, kc, vc, kp, vp, *[_take(ref, (0,), cur) for ref in (o_ref, l_ref, do_ref, dl_ref)])
            _put(dq_ref, (), cur, res[0], add=True)
            _put(dk_ref, (), cur, res[1], add=True)
            _put(dv_ref, (), cur, res[2], add=True)
            if has_prev:
                _put(dk_ref, (), prv, res[3], add=True)
                _put(dv_ref, (), prv, res[4], add=True)

        _for_each_sequence(t, unit)

    spec = lambda j: pl.BlockSpec((1, ATTN_GROUP, t, 128), lambda i, b: (j, i, 0, 0))
    branch = pl.BlockSpec((1, ATTN_GROUP, t, 128), lambda i, b: (b, i, 0, 0))
    out = pl.BlockSpec((ATTN_GROUP, t, 128), lambda i, b: (i, 0, 0))
    return pl.pallas_call(
        body, name="attn_bwd", grid=(N_PAIR // ATTN_GROUP, len(DILATIONS)),
        in_specs=[branch] * 4 + [spec(0), spec(1), spec(2)], out_specs=[out] * 3,
        out_shape=[jax.ShapeDtypeStruct((N_PAIR, t, 128), F32)] * 3,
        compiler_params=_params(("parallel", "arbitrary")),
    )(do, dl, o, lse, qkv, qkv, qkv)


def _in_proj_bwd(dpa, dq, dk, dv, win, x, g1, dx1):
    t = x.shape[0]
    tm = 256

    def body(dpa_ref, dq_ref, dk_ref, dv_ref, w_ref, x_ref, g_ref, dx1_ref, dproj_ref, dx_ref, dg_ref):
        parts = [dpa_ref[...]] + [ref[p] for ref in (dq_ref, dk_ref, dv_ref) for p in range(N_PAIR)]
        dproj = jnp.concatenate([z.astype(BF16) for z in parts], axis=1)
        dproj_ref[...] = dproj
        dh = _dot(dproj, w_ref[...])
        dxn, dgr = _rms_bwd(dh, x_ref[...], g_ref[...])
        dx_ref[...] = dx1_ref[...] + dxn
        _acc(dg_ref, _colsum8(dgr), pl.program_id(0) == 0)

    row = pl.BlockSpec((tm, D_MODEL), lambda i: (i, 0))
    pair = pl.BlockSpec((N_PAIR, tm, 128), lambda i: (0, i, 0))
    return pl.pallas_call(
        body, name="in_proj_bwd", grid=(t // tm,),
        in_specs=[pl.BlockSpec((tm, SHIFT_COLS), lambda i: (i, 0))] + [pair] * 3
                 + [pl.BlockSpec((IN_COLS, D_MODEL), lambda i: (0, 0)), row, pl.BlockSpec((1, D_MODEL), lambda i: (0, 0)), row],
        out_specs=[pl.BlockSpec((tm, IN_COLS), lambda i: (i, 0)), row, pl.BlockSpec((8, D_MODEL), lambda i: (0, 0))],
        out_shape=[jax.ShapeDtypeStruct((t, IN_COLS), BF16), jax.ShapeDtypeStruct((t, D_MODEL), F32),
                   jax.ShapeDtypeStruct((8, D_MODEL), F32)],
        compiler_params=_params(("arbitrary",)),
    )(dpa, dq, dk, dv, win, x, g1, dx1)


def _pad_lora(w, lo):
    z = jnp.zeros((64, RW), F32)
    return jnp.concatenate([w, z], axis=0) if lo == 0 else jnp.concatenate([z, w], axis=0)


def _local_step(x, tgt, win, vecs, w2, a2, g2m, get_rest, send_rest):
    pw = (vecs["mu_shift"], vecs["decay_w0"], _pad_lora(w2, 0), vecs["iclr_a0"], _pad_lora(a2, 64), g2m,
          vecs["k_k"], vecs["k_a"])
    h, proj, qkv = _in_proj(x, vecs["mix_norm_g"], win)
    r, lw, k2, v, kk, a, g = _prep_fwd(proj, pw)
    y, s0s = _wkv_fwd(r, lw, k2, v, kk, a)
    o_att, l_att = _attn_fwd(qkv)
    ycat = _mixers_out(y, r, k2, v, g, vecs["ln_x_w"], vecs["ln_x_b"], vecs["r_k"], o_att, l_att, vecs["attn_out_g"])
    wout, wg, wu, wd = get_rest(ycat)
    h2, act, dx2b, dgt, dup, dx1b, dx1, dya, dyb, loss8, dgf, dg2n = _ffn_all(
        x, ycat, wg, wu, wd, wout, vecs["ffn_norm_g"], vecs["final_norm_g"], tgt)
    gw = {
        "w_down": _wgrad(act, dx2b, 1408, 1024, "wgrad_down"),
        "w_gate": _wgrad(dgt, h2, 1408, 1024, "wgrad_gate"),
        "w_up": _wgrad(dup, h2, 1408, 1024, "wgrad_up"),
        "w_out": _wgrad(ycat, dx1b, 1024, 1024, "wgrad_out"),
    }

    lnw = vecs["ln_x_w"] + send_rest(gw)[0, 0]
    dy, dr_p, dk2_p, dv_p, dg, dlnw, dlnb, drk = _post_bwd(dya, y, r, k2, v, g, lnw, vecs["ln_x_b"], vecs["r_k"])
    dr_s, dlw, dk2_s, dv_s, dkk, da = _wkv_bwd(dy, s0s, r, lw, k2, v, kk, a)
    dpa, dmu, dw0, dw2p, da0, da2p, dg2m, dk_k, dk_a = _prep_bwd(
        proj, pw, (dr_p, dr_s, dlw, dk2_p, dk2_s, dv_p, dv_s, dkk, da, dg))

    do_att, dl_att, dog = _combine_bwd(dyb, o_att, l_att, vecs["attn_out_g"])
    dq, dk, dv = _attn_bwd(do_att, dl_att, o_att, l_att, qkv)
    dproj, dx, dg1 = _in_proj_bwd(dpa, dq, dk, dv, win, x, vecs["mix_norm_g"], dx1)
    gw["w_in"] = _wgrad(dproj, h, 1664, 1024, "wgrad_in")
    gw["decay_w2"] = dw2p[:64]
    gw["iclr_a2"] = da2p[64:]
    gw["gate_g2"] = dg2m
    gv = {"mix_norm_g": dg1, "mu_shift": dmu, "decay_w0": dw0, "iclr_a0": da0, "k_k": dk_k, "k_a": dk_a, "r_k": drk,
          "ln_x_w": dlnw, "ln_x_b": dlnb, "attn_out_g": dog, "ffn_norm_g": dg2n, "final_norm_g": dgf}
    return loss8, dx, gw, gv


N_CHIP = 4
N_DEV = 8
MATS = ("w_in", "w_out", "w_gate", "w_up", "w_down")
LORAS = ("decay_w2", "iclr_a2", "gate_g2")
VECS = (("mix_norm_g", 1024), ("mu_shift", 1792), ("decay_w0", 512), ("iclr_a0", 512), ("k_k", 512), ("k_a", 512),
        ("r_k", 512), ("ln_x_w", 512), ("ln_x_b", 512), ("attn_out_g", 512), ("ffn_norm_g", 1024),
        ("final_norm_g", 1024))
N_VEC = sum(n for _, n in VECS)
N_SMALL = N_VEC + 128
ANY = pl.BlockSpec(memory_space=pl.ANY)


def _flip(v, f):
    return 1 - v if f else v


class _Me:
    def __init__(self, mode):
        x, y, c = lax.axis_index("x"), lax.axis_index("y"), lax.axis_index("c")
        self.core, self.chip, self.dev = c, 2 * x + y, 4 * x + 2 * y + c
        self.sibling = (x, y, 1 - c)
        if mode == "chips":
            self.peers = [(px, py, c) for px, py in ((1 - x, y), (x, 1 - y), (1 - x, 1 - y))]
        else:
            self.peers = [(_flip(x, k & 4), _flip(y, k & 2), _flip(c, k & 1)) for k in range(1, N_DEV)]


def _half(core, rows):
    h = rows // 2
    return pl.ds(pl.multiple_of(core * h, h), h)


_BY_CHIP = ("gather", "chipsum")


def _peer_copy(srcs, dsts, kinds, send_sems, recv_sems, me, j, i, incoming):
    px, py, pc = me.peers[j]
    pchip, pdev = 2 * px + py, 4 * px + 2 * py + pc
    src, dst, kind = srcs[i], dsts[i], kinds[i]
    if kind == "gather":
        rows = _half(me.core, src.shape[0])
        src, dst = src.at[rows], dst.at[pchip if incoming else me.chip, rows]
    elif kind == "scatter":
        src, dst = src.at[pchip, _half(pc, src.shape[1])], dst.at[pdev if incoming else me.dev]
    elif kind == "chipsum":
        src, dst = src.at[pchip], dst.at[pchip if incoming else me.chip]
    else:
        dst = dst.at[pdev if incoming else me.dev]
    n = len(srcs)
    return pltpu.make_async_remote_copy(src_ref=src, dst_ref=dst, send_sem=send_sems.at[n * j + i],
                                        recv_sem=recv_sems.at[n * j + i], device_id=(px, py, pc), device_id_type=MESH)


def _mode(kinds):
    return "chips" if kinds[0] in _BY_CHIP else "devs"


def _npeer(kinds):
    return N_CHIP - 1 if kinds[0] in _BY_CHIP else N_DEV - 1


def _sibling_halves(gs, name):
    n = len(gs)

    def body(*refs):
        srcs, dsts, send_sems, recv_sems = refs[:n], refs[n:2 * n], refs[2 * n], refs[2 * n + 1]
        me = _Me("chips")

        def copy(i, p):
            return pltpu.make_async_remote_copy(
                src_ref=srcs[i].at[p, _half(1 - me.core, srcs[i].shape[1])], dst_ref=dsts[i].at[p],
                send_sem=send_sems.at[N_CHIP * i + p], recv_sem=recv_sems.at[N_CHIP * i + p],
                device_id=me.sibling, device_id_type=MESH)

        copies = [copy(i, p) for i in range(n) for p in range(N_CHIP)]
        for cp in copies:
            cp.start()
        for cp in copies:
            cp.wait()

    return pl.pallas_call(
        body, name=name, in_specs=[ANY] * n, out_specs=[ANY] * n,
        out_shape=[jax.ShapeDtypeStruct((N_CHIP, g.shape[1] // 2, g.shape[2]), g.dtype) for g in gs],
        scratch_shapes=[pltpu.SemaphoreType.DMA((N_CHIP * n,)), pltpu.SemaphoreType.DMA((N_CHIP * n,))],
    )(*gs)


def _add_halves(g, other, core, tr, name):
    _, h, cols = other.shape

    def body(core_ref, g_ref, o_ref, out_ref):
        out_ref[...] = (g_ref[...].astype(F32) + o_ref[...].astype(F32)).astype(BF16)

    blk = lambda off: pl.BlockSpec((1, tr, cols), lambda p, i, core_ref: (p, core_ref[0] * (h // tr) * off + i, 0))
    return pl.pallas_call(
        body, name=name,
        grid_spec=pltpu.PrefetchScalarGridSpec(num_scalar_prefetch=1, grid=(N_CHIP, h // tr),
                                               in_specs=[blk(1), blk(0)], out_specs=blk(0)),
        out_shape=jax.ShapeDtypeStruct(other.shape, BF16),
        compiler_params=_params(("parallel", "parallel")),
    )(core, g, other)


def _swap_gathered(lands, name):
    n = len(lands)

    def body(*refs):
        dsts, send_sems, recv_sems = refs[n:2 * n], refs[2 * n], refs[2 * n + 1]
        me = _Me("chips")

        def copy(j, i, incoming):
            px, py, _ = me.peers[j]
            rows_out, rows_in = _half(me.core, dsts[i].shape[1]), _half(1 - me.core, dsts[i].shape[1])
            return pltpu.make_async_remote_copy(
                src_ref=dsts[i].at[2 * px + py, rows_out], dst_ref=dsts[i].at[2 * px + py, rows_in if incoming else rows_out],
                send_sem=send_sems.at[n * j + i], recv_sem=recv_sems.at[n * j + i], device_id=me.sibling, device_id_type=MESH)

        sends = [copy(j, i, False) for j in range(3) for i in range(n)]
        for cp in sends:
            cp.start()
        for j in range(3):
            for i in range(n):
                copy(j, i, True).wait_recv()
        for cp in sends:
            cp.wait_send()

    return pl.pallas_call(
        body, name=name, in_specs=[ANY] * n, out_specs=[ANY] * n,
        out_shape=[jax.ShapeDtypeStruct(l.shape, l.dtype) for l in lands],
        input_output_aliases={i: i for i in range(n)},
        scratch_shapes=[pltpu.SemaphoreType.DMA((3 * n,)), pltpu.SemaphoreType.DMA((3 * n,))],
    )(*lands)


def _join_halves(sums, name):
    n = len(sums)

    def body(*refs):
        dsts, send_sems, recv_sems = refs[n:2 * n], refs[2 * n], refs[2 * n + 1]
        me = _Me("chips")

        def copy(i, incoming):
            mine, other = _half(me.core, dsts[i].shape[0]), _half(1 - me.core, dsts[i].shape[0])
            return pltpu.make_async_remote_copy(src_ref=dsts[i].at[mine], dst_ref=dsts[i].at[other if incoming else mine],
                                                send_sem=send_sems.at[i], recv_sem=recv_sems.at[i],
                                                device_id=me.sibling, device_id_type=MESH)

        sends = [copy(i, False) for i in range(n)]
        for cp in sends:
            cp.start()
        for i in range(n):
            copy(i, True).wait_recv()
        for cp in sends:
            cp.wait_send()

    return pl.pallas_call(
        body, name=name, in_specs=[ANY] * n, out_specs=[ANY] * n,
        out_shape=[jax.ShapeDtypeStruct(s.shape, s.dtype) for s in sums],
        input_output_aliases={i: i for i in range(n)},
        scratch_shapes=[pltpu.SemaphoreType.DMA((n,)), pltpu.SemaphoreType.DMA((n,))],
    )(*sums)


HBM = pl.BlockSpec(memory_space=pltpu.HBM)
SEM = pl.BlockSpec(memory_space=pltpu.SEMAPHORE)
EFFECT = pltpu.SideEffectType.DATAFLOW_SIDE_EFFECTING


def _swap_start(arrs, lands, kinds, name):
    n = len(arrs)

    def body(*refs):
        srcs, dsts, send_sems, recv_sems, token = refs[:n], refs[n:2 * n], refs[2 * n], refs[2 * n + 1], refs[-1]
        me = _Me(_mode(kinds))
        for j in range(len(me.peers)):
            for i in range(n):
                _peer_copy(srcs, dsts, kinds, send_sems, recv_sems, me, j, i, False).start()
        token[...] = jnp.zeros_like(token)

    ns = _npeer(kinds) * n
    outs = pl.pallas_call(
        body, name=name,
        out_shape=(pltpu.SemaphoreType.DMA((ns,)), pltpu.SemaphoreType.DMA((ns,)),
                   *[pltpu.HBM(a.shape, a.dtype) for a in arrs], *[pltpu.HBM(l.shape, l.dtype) for l in lands],
                   jax.ShapeDtypeStruct((8, 128), F32)),
        in_specs=[HBM] * (2 * n), out_specs=(SEM, SEM, *[HBM] * (2 * n), pl.BlockSpec(memory_space=pltpu.VMEM)),
        input_output_aliases={k: 2 + k for k in range(2 * n)},
        compiler_params=pltpu.CompilerParams(has_side_effects=EFFECT),
    )(*[pltpu.with_memory_space_constraint(a, pltpu.HBM) for a in arrs],
      *[pltpu.with_memory_space_constraint(l, pltpu.HBM) for l in lands])
    return outs[0], outs[1], outs[2:2 + n], outs[2 + n:2 + 2 * n], outs[-1]


def _swap_wait(send_sems, recv_sems, srcs_thru, lands_thru, after, kinds, name):
    n = len(srcs_thru)

    def body(*refs):
        srcs, dsts, s_sems, r_sems = refs[:n], refs[n:2 * n], refs[2 * n], refs[2 * n + 1]
        me = _Me(_mode(kinds))
        for j in range(len(me.peers)):
            for i in range(n):
                cp = _peer_copy(srcs, dsts, kinds, s_sems, r_sems, me, j, i, True)
                cp.wait_send()
                cp.wait_recv()

    outs = pl.pallas_call(
        body, name=name,
        out_shape=tuple(pltpu.HBM(a.shape, a.dtype) for a in (*srcs_thru, *lands_thru)),
        in_specs=[HBM] * (2 * n) + [SEM, SEM, ANY], out_specs=tuple([HBM] * (2 * n)),
        input_output_aliases={k: k for k in range(2 * n)},
        compiler_params=pltpu.CompilerParams(has_side_effects=EFFECT),
    )(*srcs_thru, *lands_thru, send_sems, recv_sems, after)
    return outs[n:]


def _adamw(w, g, m, v):
    m = ADAM_B1 * m + (1.0 - ADAM_B1) * g
    v = ADAM_B2 * v + (1.0 - ADAM_B2) * (g * g)
    m_hat = m / (1.0 - ADAM_B1 ** ADAM_STEP)
    v_hat = v / (1.0 - ADAM_B2 ** ADAM_STEP)
    delta = -ADAM_LR * (m_hat / (jnp.sqrt(v_hat) + ADAM_EPS) + ADAM_WD * w)
    return delta, m, v


def _reduce8(rbuf, core, tr, name):
    slots, h, cols = rbuf.shape

    def body(core_ref, r_ref, g_ref):
        g = r_ref[0].astype(F32)
        for s in range(1, slots):
            g = g + r_ref[s].astype(F32)
        g_ref[...] = g

    return pl.pallas_call(
        body, name=name,
        grid_spec=pltpu.PrefetchScalarGridSpec(
            num_scalar_prefetch=1, grid=(h // tr,),
            in_specs=[pl.BlockSpec((slots, tr, cols), lambda i, core_ref: (0, i, 0))],
            out_specs=pl.BlockSpec((tr, cols), lambda i, core_ref: (core_ref[0] * (h // tr) + i, 0))),
        out_shape=jax.ShapeDtypeStruct((2 * h, cols), F32),
        compiler_params=_params(("parallel",)),
    )(core, rbuf)


def _adamw_call(g, w, m, v, tr, name):
    _, rows, cols = w.shape

    def body(g_in, w_ref, m_ref, v_ref, g_ref, d_ref, nm_ref, nv_ref):
        g = g_in[...]
        g_ref[0] = g
        d_ref[0], nm_ref[0], nv_ref[0] = _adamw(w_ref[0], g, m_ref[0], v_ref[0])

    row = pl.BlockSpec((1, tr, cols), lambda i: (0, i, 0))
    return pl.pallas_call(
        body, name=name, grid=(rows // tr,),
        in_specs=[pl.BlockSpec((tr, cols), lambda i: (i, 0)), row, row, row], out_specs=[row] * 4,
        out_shape=[jax.ShapeDtypeStruct(w.shape, F32)] * 4,
        compiler_params=_params(("parallel",)),
    )(g, w, m, v)


def _rowsum_small(parts, loss8):
    def body(*refs):
        out = refs[-1]
        c0 = 0
        for ref in refs[:-1]:
            n = ref.shape[1]
            out[:, c0:c0 + n] = jnp.sum(ref[...], axis=0, keepdims=True)
            c0 += n

    return pl.pallas_call(body, name="rowsum_small", out_shape=jax.ShapeDtypeStruct((1, N_SMALL), F32))(*parts, loss8)


def _reduce_adamw_small(sbuf, ws, ms, vs):
    nv = len(ws)

    def body(*refs):
        s_ref, ins, outs = refs[0], refs[1:1 + 3 * nv], refs[1 + 3 * nv:]
        tot = s_ref[0]
        for s in range(1, N_DEV):
            tot = tot + s_ref[s]
        c0 = 0
        for i in range(nv):
            n = ins[i].shape[1]
            g = tot[:, c0:c0 + n]
            outs[i][...] = g
            outs[nv + i][...], outs[2 * nv + i][...], outs[3 * nv + i][...] = _adamw(
                ins[i][...], g, ins[nv + i][...], ins[2 * nv + i][...])
            c0 += n
        outs[-1][...] = tot[:, c0:]

    return pl.pallas_call(
        body, name="reduce_adamw_small",
        out_shape=[jax.ShapeDtypeStruct(a.shape, F32) for a in ws] * 4 + [jax.ShapeDtypeStruct((1, 128), F32)],
    )(sbuf, *ws, *ms, *vs)


_TRANSPOSED = ("w_in", "w_gate", "w_up")
_ROW_STACKED = MATS
_ADAM_TILE = {"w_in": 208, "w_out": 256, "w_gate": 176, "w_up": 176, "w_down": 176, "decay_w2": 64, "iclr_a2": 64,
              "gate_g2": 128}
_SUM_TILE = {"w_in": 208, "w_out": 128, "w_gate": 176, "w_up": 176, "w_down": 176, "decay_w2": 32, "iclr_a2": 32,
             "gate_g2": 64}


def _full(n, stacked):
    p, r, c = stacked.shape
    if n in _ROW_STACKED:
        return stacked.reshape(p * r, c)
    return jnp.transpose(stacked, (1, 0, 2)).reshape(r, p * c)


def _by_chip(n, full):
    if n in _ROW_STACKED:
        return full.reshape(N_CHIP, full.shape[0] // N_CHIP, full.shape[1])
    r, c = full.shape
    return jnp.transpose(full.reshape(r, N_CHIP, c // N_CHIP), (1, 0, 2))


def _with_own(land_shape, dtype, own, slot):
    return lax.dynamic_update_slice(lax.empty(land_shape, dtype), own[None], (slot,) + (0,) * own.ndim)


def kernel(x, mix_norm_g, w_in, mu_shift, decay_w0, decay_w2, iclr_a0, iclr_a2, gate_g2, k_k, k_a, r_k, ln_x_w, ln_x_b, attn_out_g, w_out, ffn_norm_g, w_gate, w_up, w_down, final_norm_g, loss_target, m_mix_norm_g, m_w_in, m_mu_shift, m_decay_w0, m_decay_w2, m_iclr_a0, m_iclr_a2, m_gate_g2, m_k_k, m_k_a, m_r_k, m_ln_x_w, m_ln_x_b, m_attn_out_g, m_w_out, m_ffn_norm_g, m_w_gate, m_w_up, m_w_down, m_final_norm_g, v_mix_norm_g, v_w_in, v_mu_shift, v_decay_w0, v_decay_w2, v_iclr_a0, v_iclr_a2, v_gate_g2, v_k_k, v_k_a, v_r_k, v_ln_x_w, v_ln_x_b, v_attn_out_g, v_w_out, v_ffn_norm_g, v_w_gate, v_w_up, v_w_down, v_final_norm_g):
    names = ("mix_norm_g", "w_in", "mu_shift", "decay_w0", "decay_w2", "iclr_a0", "iclr_a2", "gate_g2", "k_k", "k_a",
             "r_k", "ln_x_w", "ln_x_b", "attn_out_g", "w_out", "ffn_norm_g", "w_gate", "w_up", "w_down", "final_norm_g")
    w = dict(zip(names, (mix_norm_g, w_in, mu_shift, decay_w0, decay_w2, iclr_a0, iclr_a2, gate_g2, k_k, k_a, r_k,
                         ln_x_w, ln_x_b, attn_out_g, w_out, ffn_norm_g, w_gate, w_up, w_down, final_norm_g)))
    m = dict(zip(names, (m_mix_norm_g, m_w_in, m_mu_shift, m_decay_w0, m_decay_w2, m_iclr_a0, m_iclr_a2, m_gate_g2,
                         m_k_k, m_k_a, m_r_k, m_ln_x_w, m_ln_x_b, m_attn_out_g, m_w_out, m_ffn_norm_g, m_w_gate,
                         m_w_up, m_w_down, m_final_norm_g)))
    v = dict(zip(names, (v_mix_norm_g, v_w_in, v_mu_shift, v_decay_w0, v_decay_w2, v_iclr_a0, v_iclr_a2, v_gate_g2,
                         v_k_k, v_k_a, v_r_k, v_ln_x_w, v_ln_x_b, v_attn_out_g, v_w_out, v_ffn_norm_g, v_w_gate,
                         v_w_up, v_w_down, v_final_norm_g)))
    first = ("w_in",) + LORAS
    rest = ("w_out", "w_gate", "w_up", "w_down")
    xi, yi, ci = lax.axis_index("x"), lax.axis_index("y"), lax.axis_index("c")
    my_chip, my_dev = 2 * xi + yi, 4 * xi + 2 * yi + ci
    gather, scatter = ("gather",) * 4, ("scatter",) * 4

    sh = lambda z, n: jnp.transpose(z[0]) if n in _TRANSPOSED else z[0]
    mine = [sh(w["w_in"], "w_in").astype(BF16)] + [w[n][0] for n in LORAS]
    early = _swap_start(mine, [_with_own((N_CHIP,) + a.shape, a.dtype, a, my_chip) for a in mine], gather, "gather_first_start")
    wb = {n: (sh(w[n], n) + early[4][0, 0]).astype(BF16) for n in rest}
    lands = [_with_own((N_CHIP,) + wb[n].shape, BF16, wb[n], my_chip) for n in rest]
    ssem, rsem, srcs_thru, lands_thru, tok = _swap_start([wb[n] for n in rest], lands, gather, "gather_rest_start")
    got = _swap_wait(early[0], early[1], early[2], early[3], tok, gather, "gather_first_wait")
    win, w2, a2, g2m = (_full(n, z) for n, z in zip(first, _swap_gathered(got, "gather_first_halves")))

    vecs = {n: w[n].reshape(1, sz) for n, sz in VECS}
    vecs["mix_norm_g"] = vecs["mix_norm_g"] + tok[0, 0]

    def get_rest(after):
        halves = _swap_wait(ssem, rsem, srcs_thru, lands_thru, after, gather, "gather_rest_wait")
        return [_full(n, z) for n, z in zip(rest, _swap_gathered(halves, "gather_rest_halves"))]

    flight = []

    def my_half(g):
        h = g.shape[1] // 2
        return lax.dynamic_slice(g, (my_chip, ci * h, 0), (1, h, g.shape[2]))[0]

    def send_rest(gw):
        gs = [_by_chip(n, gw[n]).astype(BF16) for n in rest]
        into = [_with_own((N_DEV,) + my_half(g).shape, BF16, my_half(g), my_dev) for g in gs]
        flight.extend(_swap_start(gs, into, scatter, "exchange_rest_start"))
        return flight[4]

    loss8, dx, gw, gv = _local_step(x[0], loss_target[0], win, vecs, w2, a2, g2m, get_rest, send_rest)

    core = jnp.reshape(ci, (1,)).astype(jnp.int32)
    gs = [_by_chip(n, gw[n]).astype(BF16) for n in first]
    theirs = _sibling_halves(gs, "presum_halves")
    sums = [_add_halves(g, o, core, _SUM_TILE[n], "chipsum_" + n) for n, g, o in zip(first, gs, theirs)]
    own = [lax.dynamic_index_in_dim(s, my_chip, 0, keepdims=False) for s in sums]
    last = _swap_start(sums, [_with_own(s.shape, BF16, o, my_chip) for s, o in zip(sums, own)], ("chipsum",) * 4,
                       "exchange_first_start")
    small = _rowsum_small([gv[n] for n, _ in VECS], loss8 + last[4])
    vecs_out = _swap_start([small], [_with_own((N_DEV,) + small.shape, F32, small, my_dev)], ("all",),
                           "exchange_vectors_start")


    def update(group, rbufs, tag):
        sums = [_reduce8(rb, core, _SUM_TILE[n], "reduce_" + n) for n, rb in zip(group, rbufs)]
        gsum = _join_halves(sums, "join_halves_" + tag)
        out = {}
        for n, g in zip(group, gsum):
            r = _adamw_call(g, sh(w[n], n)[None], sh(m[n], n)[None], sh(v[n], n)[None], _ADAM_TILE[n], "adamw_" + n)
            out[n] = [jnp.transpose(z[0])[None] for z in r] if n in _TRANSPOSED else r
        return out

    res = update(rest, _swap_wait(flight[0], flight[1], flight[2], flight[3], vecs_out[4], scatter, "exchange_rest_wait"),
                 "rest")
    got = _swap_wait(last[0], last[1], last[2], last[3], res["w_down"][1], ("chipsum",) * 4, "exchange_first_wait")
    res.update(update(first, got, "first"))
    sbuf = _swap_wait(vecs_out[0], vecs_out[1], vecs_out[2], vecs_out[3], res["w_in"][1], ("all",),
                      "exchange_vectors_wait")[0]
    rows = lambda d: [d[n].reshape(1, sz) for n, sz in VECS]
    small_res = _reduce_adamw_small(sbuf, rows(w), rows(m), rows(v))

    outs = []
    for k in range(4):
        piece = {n: r[k] for n, r in res.items()}
        for i, (n, _) in enumerate(VECS):
            piece[n] = small_res[k * len(VECS) + i].reshape(w[n].shape)
        outs.extend(piece[n] for n in names)
    return (small_res[-1][0, 0], dx[None], *outs)
```

```python
import jax
import jax.numpy as jnp
from jax import lax
from jax.experimental import pallas as pl
from jax.experimental.pallas import tpu as pltpu

F32 = jnp.float32
BF16 = jnp.bfloat16

D_MODEL = 1024
HEAD_DIM = 64
RW = 512
N_PAIR = RW // 128
SHIFT_COLS = 1792
IN_COLS = 3328
D_FF = 2816
FF_CHUNK = 256
NORM_EPS = 1e-6
GN_EPS = 64e-5
CHUNK = 64
SUB = 16
WKV_PASSES = 1
ATTN_PASSES = 1
ATTN_BLOCK = 128
DILATIONS = (1, 4, 16)
NEG = -1e30
ADAM_LR, ADAM_B1, ADAM_B2, ADAM_EPS, ADAM_WD, ADAM_STEP = 0.001, 0.9, 0.999, 1e-08, 0.01, 10
VMEM_LIMIT = 56 * 1024 * 1024
MESH = pl.DeviceIdType.MESH


def _params(sem=None, **kw):
    return pltpu.CompilerParams(dimension_semantics=sem, vmem_limit_bytes=VMEM_LIMIT, **kw)


def _dot(a, b, prec=None):
    return lax.dot_general(a, b, (((1,), (0,)), ((), ())), preferred_element_type=F32, precision=prec)


def _dot_nt(a, b, prec=None):
    return lax.dot_general(a, b, (((1,), (1,)), ((), ())), preferred_element_type=F32, precision=prec)


def _dot_tn(a, b, prec=None):
    return lax.dot_general(a, b, (((0,), (0,)), ((), ())), preferred_element_type=F32, precision=prec)


_FORMS = {"nn": ((1,), (0,)), "nt": ((1,), (1,)), "tn": ((0,), (0,))}


def _dg(a, b, form):
    if a.ndim == 3 or b.ndim == 3:
        nb = a.shape[0] if a.ndim == 3 else b.shape[0]
        return jnp.stack([_dg(a[i] if a.ndim == 3 else a, b[i] if b.ndim == 3 else b, form) for i in range(nb)], axis=0)
    return lax.dot_general(a, b, (_FORMS[form], ((), ())), preferred_element_type=F32)


def _split2(x):
    hi = x.astype(BF16)
    return hi, (x - hi.astype(F32)).astype(BF16)


def _split3(x):
    hi = x.astype(BF16)
    rest = x - hi.astype(F32)
    mid = rest.astype(BF16)
    return hi, mid, (rest - mid.astype(F32)).astype(BF16)


def _mm_raw(a, b, form, mode):
    if mode == 1:
        return _dg(a.astype(BF16), b.astype(BF16), form)
    if mode == 3:
        ah, al = _split2(a)
        bh, bl = _split2(b)
        return _dg(ah, bh, form) + (_dg(ah, bl, form) + _dg(al, bh, form))
    if mode == "L3":
        ab = a.astype(BF16)
        b1, b2, b3 = _split3(b)
        if form == "nn":
            n = b.shape[-1]
            wide = _dg(ab, jnp.concatenate([b1, b2, b3], axis=-1), form)
            return wide[..., :n] + (wide[..., n:2 * n] + wide[..., 2 * n:])
        return _dg(ab, b1, form) + (_dg(ab, b2, form) + _dg(ab, b3, form))
    assert mode == "R3", mode
    bb = b.astype(BF16)
    a1, a2, a3 = _split3(a)
    if form in ("nn", "nt"):
        m = a.shape[-2]
        tall = _dg(jnp.concatenate([a1, a2, a3], axis=-2), bb, form)
        return tall[..., :m, :] + (tall[..., m:2 * m, :] + tall[..., 2 * m:, :])
    return _dg(a1, bb, form) + (_dg(a2, bb, form) + _dg(a3, bb, form))


def _mm(a, b, form, mode):
    @jax.custom_vjp
    def f(a, b):
        return _mm_raw(a, b, form, mode)

    def fwd(a, b):
        return _mm_raw(a, b, form, mode), (a, b)

    def bwd(res, ct):
        a, b = res
        la = {1: 1, 3: 3, "L3": None, "R3": "R3"}[mode]
        lb = {1: 1, 3: 3, "L3": "L3", "R3": None}[mode]
        if form == "nn":
            da = None if la is None else _mm_raw(ct, b, "nt", la)
            db = None if lb is None else _mm_raw(a, ct, "tn", lb)
        elif form == "nt":
            da = None if la is None else _mm_raw(ct, b, "nn", la)
            db = None if lb is None else _mm_raw(ct, a, "tn", "R3" if lb == "L3" else lb)
        else:
            da = None if la is None else _mm_raw(b, ct, "nt", "L3" if la == "R3" else la)
            db = None if lb is None else _mm_raw(a, ct, "nn", lb)
        return (jnp.zeros_like(a) if da is None else da, jnp.zeros_like(b) if db is None else db)

    f.defvjp(fwd, bwd)
    return f(a, b)


def _seg_ones(n):
    r = lax.broadcasted_iota(jnp.int32, (n, n), 0) // HEAD_DIM
    c = lax.broadcasted_iota(jnp.int32, (n, n), 1) // HEAD_DIM
    return (r == c).astype(F32)


def _segsum(x, seg):
    return _mm(x, seg, "nn", "R3")


def _rms_fwd(x, g):
    rstd = lax.rsqrt(jnp.mean(x * x, axis=-1, keepdims=True) + NORM_EPS)
    return x * rstd * g


def _rms_bwd(dy, x, g):
    rstd = lax.rsqrt(jnp.mean(x * x, axis=-1, keepdims=True) + NORM_EPS)
    xn = x * rstd
    dxn = dy * g
    dx = rstd * (dxn - xn * jnp.mean(dxn * xn, axis=-1, keepdims=True))
    return dx, dy * xn


def _sigmoid(x):
    return 1.0 / (1.0 + jnp.exp(-x))


def _softplus(x):
    return jnp.maximum(x, 0.0) + jnp.log(1.0 + jnp.exp(-jnp.abs(x)))


def _acc(ref, val, first):
    @pl.when(first)
    def _():
        ref[...] = val

    @pl.when(jnp.logical_not(first))
    def _():
        ref[...] += val


def _colsum8(v):
    rows, n = v.shape
    return jnp.sum(v.reshape(rows // 8, 8, n), axis=0)


def _prep_fn(p, pprev, mu, w0, w2p, a0, a2p, g2, k_k, k_a):
    seg = _seg_ones(RW)
    ps = p + (pprev - p) * mu
    r = ps[:, 0:RW]
    k = ps[:, RW:2 * RW]
    v = ps[:, 2 * RW:3 * RW]
    xwa = ps[:, 3 * RW:3 * RW + 128]
    xg = ps[:, 3 * RW + 128:3 * RW + 256]
    wraw = -_softplus(-(w0 + _mm(jnp.tanh(xwa), w2p, "nn", 3))) - 0.5
    lw = -jnp.exp(wraw)
    a = _sigmoid(a0 + _mm(xwa, a2p, "nn", 3))
    g = _mm(_sigmoid(xg), g2, "nn", 3)
    kk = k * k_k
    kk = kk / jnp.maximum(jnp.sqrt(_segsum(kk * kk, seg)), 1e-12)
    k2 = k * (1.0 + (a - 1.0) * k_a)
    return r, lw, k2, v, kk, a, g


def _transposed(z):
    return jnp.stack([z[i].T for i in range(z.shape[0])], axis=0) if z.ndim == 3 else z.T


def _solve_unit_lower(lmat, rhs):
    c = lmat.shape[-1]
    row = lax.broadcasted_iota(jnp.int32, (c, c), 0)
    col = lax.broadcasted_iota(jnp.int32, (c, c), 1)
    eye = (row == col).astype(F32)
    ld = jnp.where(row // SUB == col // SUB, lmat, 0.0)
    lo = lmat - ld
    x = eye + ld
    m = ld
    mm = lambda p, q: _mm(p, q, "nn", WKV_PASSES)
    cat = jnp.concatenate
    m = mm(m, m)
    for _ in range(2):
        mx = mm(m, cat([m, x], axis=-1))
        m, x = mx[..., :c], x + mx[..., c:]
    x = x + mm(m, x)
    gw = mm(x, cat([lo, rhs], axis=-1))
    g, w = gw[..., :c], gw[..., c:]
    gg = mm(g, cat([g, w], axis=-1))
    w = w + gg[..., c:]
    return w + mm(gg[..., :c], w)


def _wkv_chunk_fn(s0, r, lw, k, v, kk, a):
    c = r.shape[-2]
    n = 2 * c
    row = lax.broadcasted_iota(jnp.int32, (n, n), 0)
    col = lax.broadcasted_iota(jnp.int32, (n, n), 1)
    same = (row // c) == (col // c)
    incl = jnp.logical_and(row >= col, same)
    strict = jnp.logical_and(row > col, same)
    sel = (lax.broadcasted_iota(jnp.int32, (n, 128), 0) // c) == (lax.broadcasted_iota(jnp.int32, (n, 128), 1) // HEAD_DIM)
    two = lambda z: jnp.concatenate([z, z], axis=-2)
    lw2 = two(lw)
    mm = lambda p_, q_, form: _mm(p_, q_, form, WKV_PASSES)
    cl = _mm(incl.astype(F32), lw2, "nn", "L3")
    p = jnp.exp(cl)
    pinv = jnp.exp(-cl)
    pprev = jnp.exp(cl - lw2)
    kk2 = two(kk)
    at = jnp.where(sel, -kk2 * pprev, 0.0)
    bt = jnp.where(sel, kk2 * two(a) * pinv, 0.0)
    kt = jnp.where(sel, two(k) * pinv, 0.0)
    rt = jnp.where(sel, two(r) * p, 0.0)
    vt = jnp.where(sel, two(v), 0.0)
    cat = jnp.concatenate
    bk = cat([bt, kt], axis=-2)
    arbk = mm(cat([at, rt], axis=-2), bk, "nt")
    ab, ak = jnp.where(strict, arbk[..., :n, :n], 0.0), jnp.where(strict, arbk[..., :n, n:], 0.0)
    rb, rk = jnp.where(incl, arbk[..., n:, :n], 0.0), jnp.where(incl, arbk[..., n:, n:], 0.0)
    s0t = _transposed(s0)
    u = _solve_unit_lower(ab, mm(cat([at, ak], axis=-1), cat([s0t, vt], axis=-2), "nn"))
    y2 = mm(cat([rt, rb, rk], axis=-1), cat([s0t, u, vt], axis=-2), "nn")
    plast = jnp.exp(jnp.sum(lw, axis=-2, keepdims=True))
    s1 = (s0 + mm(cat([u, vt], axis=-2), bk, "tn")) * plast
    r2 = lax.broadcasted_iota(jnp.int32, (128, 128), 0) // HEAD_DIM
    c2 = lax.broadcasted_iota(jnp.int32, (128, 128), 1) // HEAD_DIM
    return y2[..., :c, :] + y2[..., c:, :], jnp.where(r2 == c2, s1, 0.0)


def _post_fn(y, r, k2, v, g, lnw, lnb, rk):
    seg = _seg_ones(RW)
    mean = _segsum(y, seg) * (1.0 / HEAD_DIM)
    yc = y - mean
    var = _segsum(yc * yc, seg) * (1.0 / HEAD_DIM)
    yn = yc * lax.rsqrt(var + GN_EPS)
    out = yn * lnw + lnb + _segsum(r * k2 * rk, seg) * v
    return out * g


def _attn_block_fn(q, kc, vc, kp=None, vp=None):
    n = ATTN_BLOCK
    qi = lax.broadcasted_iota(jnp.int32, (n, n), 0)
    kj = lax.broadcasted_iota(jnp.int32, (n, n), 1)
    lane = lax.broadcasted_iota(jnp.int32, (1, 128), 1)
    scale = HEAD_DIM ** -0.5
    valid = kj <= qi
    keys, vals = kc, vc
    if kp is not None:
        valid = jnp.concatenate([valid, kj >= qi], axis=-1)
        keys, vals = jnp.concatenate([kc, kp], axis=-2), jnp.concatenate([vc, vp], axis=-2)
    m0 = (lane // HEAD_DIM) == 0
    q2 = jnp.concatenate([jnp.where(m0, q, 0.0), jnp.where(m0, 0.0, q)], axis=-2)
    valid2 = jnp.concatenate([valid, valid], axis=-2)
    s = jnp.where(valid2, _mm(q2, keys, "nt", ATTN_PASSES) * scale, NEG)
    m = jnp.max(s, axis=-1, keepdims=True)
    p = jnp.exp(s - m)
    den = jnp.sum(p, axis=-1, keepdims=True)
    o2 = _mm(p, vals, "nn", ATTN_PASSES) / den
    l2 = m + jnp.log(den)
    return jnp.where(m0, o2[..., :n, :], o2[..., n:, :]), jnp.where(m0, l2[..., :n, :], l2[..., n:, :])


def _attn_block_bwd(q, kc, vc, kp, vp, o, lse, do, dl):
    n = ATTN_BLOCK
    cat = jnp.concatenate
    qi = lax.broadcasted_iota(jnp.int32, (n, n), 0)
    kj = lax.broadcasted_iota(jnp.int32, (n, n), 1)
    m0 = (lax.broadcasted_iota(jnp.int32, (1, 128), 1) // HEAD_DIM) == 0
    scale = HEAD_DIM ** -0.5
    valid = kj <= qi
    keys, vals = kc, vc
    if kp is not None:
        valid = cat([valid, kj >= qi], axis=-1)
        keys, vals = cat([kc, kp], axis=-2), cat([vc, vp], axis=-2)
    stack = lambda z: cat([jnp.where(m0, z, 0.0), jnp.where(m0, 0.0, z)], axis=-2)
    q2, do2 = stack(q), stack(do)
    lse2 = cat([jnp.max(jnp.where(m0, lse, NEG), axis=-1, keepdims=True),
                jnp.max(jnp.where(m0, NEG, lse), axis=-1, keepdims=True)], axis=-2)
    delta = jnp.sum(do2 * cat([o, o], axis=-2), axis=-1, keepdims=True)
    dlse = jnp.sum(stack(dl), axis=-1, keepdims=True)
    mm = lambda a, b, form: _mm_raw(a, b, form, ATTN_PASSES)
    s = jnp.where(cat([valid, valid], axis=-2), mm(q2, keys, "nt") * scale, NEG)
    p = jnp.exp(s - lse2)
    ds = p * (mm(do2, vals, "nt") - delta + dlse)
    dq2 = mm(ds, keys, "nn") * scale
    dq = jnp.where(m0, dq2[..., :n, :], dq2[..., n:, :])
    dkeys = mm(ds, q2, "tn") * scale
    dvals = mm(p, do2, "tn")
    if kp is None:
        return dq, dkeys, dvals
    return dq, dkeys[..., :n, :], dvals[..., :n, :], dkeys[..., n:, :], dvals[..., n:, :]


def _combine_fn(o1, o2, o3, l1, l2, l3, og):
    seg = _seg_ones(o1.shape[-1])
    m = jnp.maximum(jnp.maximum(l1, l2), l3)
    e1, e2, e3 = jnp.exp(l1 - m), jnp.exp(l2 - m), jnp.exp(l3 - m)
    o = (e1 * o1 + e2 * o2 + e3 * o3) / (e1 + e2 + e3)
    o = o * lax.rsqrt(_segsum(o * o, seg) * (1.0 / HEAD_DIM) + NORM_EPS)
    return o * og


def _in_proj(x, g1, win):
    t = x.shape[0]
    tm = 512

    def body(x_ref, g_ref, w_ref, h_ref, pa_ref, qkv_ref):
        h = _rms_fwd(x_ref[...], g_ref[...]).astype(BF16)
        h_ref[...] = h
        proj = _dot_nt(h, w_ref[...])
        pa_ref[...] = proj[:, :SHIFT_COLS]
        for j in range(3):
            for p in range(N_PAIR):
                c0 = SHIFT_COLS + j * RW + p * 128
                qkv_ref[j, p] = proj[:, c0:c0 + 128]

    return pl.pallas_call(
        body, name="in_proj", grid=(t // tm,),
        in_specs=[pl.BlockSpec((tm, D_MODEL), lambda i: (i, 0)), pl.BlockSpec((1, D_MODEL), lambda i: (0, 0)),
                  pl.BlockSpec((IN_COLS, D_MODEL), lambda i: (0, 0))],
        out_specs=[pl.BlockSpec((tm, D_MODEL), lambda i: (i, 0)), pl.BlockSpec((tm, SHIFT_COLS), lambda i: (i, 0)),
                   pl.BlockSpec((3, N_PAIR, tm, 128), lambda i: (0, 0, i, 0))],
        out_shape=[jax.ShapeDtypeStruct((t, D_MODEL), BF16), jax.ShapeDtypeStruct((t, SHIFT_COLS), F32),
                   jax.ShapeDtypeStruct((3, N_PAIR, t, 128), F32)],
        compiler_params=_params(("parallel",)),
    )(x, g1, win)


def _shifted(p, last8, first):
    prow = jnp.where(first, 0.0, last8[7:8, :])
    rolled = pltpu.roll(p, 1, axis=0)
    rid = lax.broadcasted_iota(jnp.int32, p.shape, 0)
    return jnp.where(rid == 0, prow, rolled)


_PREP_TM = 256


def _prep_specs(tm):
    vec = lambda n: pl.BlockSpec((1, n), lambda i: (0, 0))
    mat = lambda r, n: pl.BlockSpec((r, n), lambda i: (0, 0))
    return [vec(SHIFT_COLS), vec(RW), mat(128, RW), vec(RW), mat(128, RW), mat(128, RW), vec(RW), vec(RW)]


def _prep_fwd(proj, pw):
    t = proj.shape[0]
    tm = _PREP_TM

    def body(p_ref, l8_ref, mu, w0, w2p, a0, a2p, g2, k_k, k_a, *outs):
        p = p_ref[...]
        pprev = _shifted(p, l8_ref[...], pl.program_id(0) == 0)
        res = _prep_fn(p, pprev, mu[...], w0[...], w2p[...], a0[...], a2p[...], g2[...], k_k[...], k_a[...])
        for o_ref, val in zip(outs, res):
            o_ref[...] = val

    row = pl.BlockSpec((tm, RW), lambda i: (i, 0))
    return pl.pallas_call(
        body, name="rwkv_prep", grid=(t // tm,),
        in_specs=[pl.BlockSpec((tm, SHIFT_COLS), lambda i: (i, 0)),
                  pl.BlockSpec((8, SHIFT_COLS), lambda i: (jnp.maximum(i * (tm // 8) - 1, 0), 0))] + _prep_specs(tm),
        out_specs=[row] * 7,
        out_shape=[jax.ShapeDtypeStruct((t, RW), F32)] * 7,
        compiler_params=_params(("parallel",)),
    )(proj, proj, *pw)


def _pairs(ref):
    return jnp.stack([ref[:, 128 * p:128 * (p + 1)] for p in range(N_PAIR)], axis=0)


def _wkv_fwd(r, lw, k2, v, kk, a):
    t = r.shape[0]
    nc = t // CHUNK

    def body(r_ref, lw_ref, k_ref, v_ref, kk_ref, a_ref, y_ref, s_ref, st):
        @pl.when(pl.program_id(0) == 0)
        def _():
            st[...] = jnp.zeros_like(st)

        s0 = st[...]
        s_ref[0] = s0
        y, s1 = _wkv_chunk_fn(s0, *[_pairs(ref) for ref in (r_ref, lw_ref, k_ref, v_ref, kk_ref, a_ref)])
        for p in range(N_PAIR):
            y_ref[:, 128 * p:128 * (p + 1)] = y[p]
        st[...] = s1

    blk = pl.BlockSpec((CHUNK, RW), lambda c: (c, 0))
    return pl.pallas_call(
        body, name="wkv_fwd", grid=(nc,),
        in_specs=[blk] * 6,
        out_specs=[blk, pl.BlockSpec((1, N_PAIR, 128, 128), lambda c: (c, 0, 0, 0))],
        out_shape=[jax.ShapeDtypeStruct((t, RW), F32), jax.ShapeDtypeStruct((nc, N_PAIR, 128, 128), F32)],
        scratch_shapes=[pltpu.VMEM((N_PAIR, 128, 128), F32)],
        compiler_params=_params(("arbitrary",)),
    )(r, lw, k2, v, kk, a)


_POST_TM = 256


ATTN_GROUP = 2


def _dilated_rows(d, r, n):
    if d == 1:
        return pl.ds(pl.multiple_of(n * ATTN_BLOCK, ATTN_BLOCK), ATTN_BLOCK)
    return pl.ds(r + n * (ATTN_BLOCK * d), ATTN_BLOCK, stride=d)


def _for_each_sequence(t, unit):
    for di, d in enumerate(DILATIONS):

        @pl.when(pl.program_id(1) == di)
        def _(di=di, d=d):
            nb = t // (ATTN_BLOCK * d)
            if d == 1:
                unit(di, [(d, 0, 0)], False)
                unit(di, [(d, 0, 1)], True)
                lax.fori_loop(1, nb // 2, lambda k, c: (unit(di, [(d, 0, 2 * k), (d, 0, 2 * k + 1)], True), c)[1], 0)
            else:

                def residues(r, carry):
                    unit(di, [(d, r, 0), (d, r + d // 2, 0)], False)
                    if nb > 1:
                        lax.fori_loop(1, nb, lambda n, c: (unit(di, [(d, r, n), (d, r + d // 2, n)], True), c)[1], 0)
                    return carry

                lax.fori_loop(0, d // 2, residues, 0)


def _take(ref, lead, rows_list):
    return jnp.stack([ref.at[(*lead, g)][rows, :] for rows in rows_list for g in range(ATTN_GROUP)], axis=0)


def _put(ref, lead, rows_list, val, add=False):
    k = 0
    for rows in rows_list:
        for g in range(ATTN_GROUP):
            if add:
                ref.at[(*lead, g)][rows, :] += val[k]
            else:
                ref.at[(*lead, g)][rows, :] = val[k]
            k += 1


def _attn_fwd(qkv):
    t = qkv.shape[2]

    def body(q_ref, k_ref, v_ref, o_ref, l_ref):
        def unit(di, places, has_prev):
            cur = [_dilated_rows(d, r, n) for d, r, n in places]
            args = [_take(ref, (0,), cur) for ref in (q_ref, k_ref, v_ref)]
            if has_prev:
                prv = [_dilated_rows(d, r, n - 1) for d, r, n in places]
                args += [_take(ref, (0,), prv) for ref in (k_ref, v_ref)]
            o, lse = _attn_block_fn(*args)
            _put(o_ref, (0,), cur, o)
            _put(l_ref, (0,), cur, lse)

        _for_each_sequence(t, unit)

    spec = lambda j: pl.BlockSpec((1, ATTN_GROUP, t, 128), lambda i, b: (j, i, 0, 0))
    out = pl.BlockSpec((1, ATTN_GROUP, t, 128), lambda i, b: (b, i, 0, 0))
    return pl.pallas_call(
        body, name="attn_fwd", grid=(N_PAIR // ATTN_GROUP, len(DILATIONS)),
        in_specs=[spec(0), spec(1), spec(2)], out_specs=[out, out],
        out_shape=[jax.ShapeDtypeStruct((3, N_PAIR, t, 128), F32)] * 2,
        compiler_params=_params(("parallel", "arbitrary")),
    )(qkv, qkv, qkv)


_COMB_TM = 256


def _mixers_out(y, r, k2, v, g, lnw, lnb, rk, o, l, og):
    t = y.shape[0]
    tm = _COMB_TM

    def body(y_ref, r_ref, k_ref, v_ref, g_ref, lnw_ref, lnb_ref, rk_ref, o_ref, l_ref, og_ref, out_ref):
        out_ref[:, :RW] = _post_fn(y_ref[...], r_ref[...], k_ref[...], v_ref[...], g_ref[...],
                                   lnw_ref[...], lnb_ref[...], rk_ref[...]).astype(BF16)
        for p in range(N_PAIR):
            cols = slice(128 * p, 128 * (p + 1))
            out_ref[:, RW + 128 * p:RW + 128 * (p + 1)] = _combine_fn(
                o_ref[0, p], o_ref[1, p], o_ref[2, p], l_ref[0, p], l_ref[1, p], l_ref[2, p], og_ref[:, cols]).astype(BF16)

    row = pl.BlockSpec((tm, RW), lambda i: (i, 0))
    vec = pl.BlockSpec((1, RW), lambda i: (0, 0))
    blk = pl.BlockSpec((3, N_PAIR, tm, 128), lambda i: (0, 0, i, 0))
    return pl.pallas_call(
        body, name="mixers_out", grid=(t // tm,),
        in_specs=[row] * 5 + [vec] * 3 + [blk, blk, vec], out_specs=pl.BlockSpec((tm, D_MODEL), lambda i: (i, 0)),
        out_shape=jax.ShapeDtypeStruct((t, D_MODEL), BF16),
        compiler_params=_params(("parallel",)),
    )(y, r, k2, v, g, lnw, lnb, rk, o, l, og)


def _ffn_all(x, ycat, wg, wu, wd, wout, g2, gf, tgt):
    t = x.shape[0]
    tm = 256

    def body(x_ref, y_ref, wg_ref, wu_ref, wd_ref, wo_ref, g2_ref, gf_ref, t_ref,
             h_ref, act_ref, dx2b_ref, dgt_ref, dup_ref, dx1b_ref, dx1_ref, dya_ref, dyb_ref, loss_ref, dgf_ref, dg2_ref,
             gt_s, up_s):
        first = pl.program_id(0) == 0
        x1 = x_ref[...] + _dot(y_ref[...], wo_ref[...])
        h = _rms_fwd(x1, g2_ref[...]).astype(BF16)
        h_ref[...] = h
        for c0 in range(0, D_FF, FF_CHUNK):
            cols = slice(c0, c0 + FF_CHUNK)
            gt = _dot_nt(h, wg_ref[cols, :])
            up = _dot_nt(h, wu_ref[cols, :])
            gt_s[:, cols] = gt.astype(BF16)
            up_s[:, cols] = up.astype(BF16)
            act_ref[:, cols] = (gt * _sigmoid(gt) * up).astype(BF16)
        x2 = x1 + _dot(act_ref[...], wd_ref[...])
        gf_ = gf_ref[...]
        diff = _rms_fwd(x2, gf_) - t_ref[...]
        lrow = 0.5 * jnp.sum(_colsum8(diff * diff), axis=1, keepdims=True) * (1.0 / D_MODEL)
        _acc(loss_ref, jnp.broadcast_to(lrow, (8, 128)), first)
        dx2, dgr = _rms_bwd(diff * (1.0 / D_MODEL), x2, gf_)
        _acc(dgf_ref, _colsum8(dgr), first)
        dx2b = dx2.astype(BF16)
        dx2b_ref[...] = dx2b
        for c0 in range(0, D_FF, FF_CHUNK):
            cols = slice(c0, c0 + FF_CHUNK)
            dact = _dot_nt(dx2b, wd_ref[cols, :])
            gt = gt_s[:, cols].astype(F32)
            sg = _sigmoid(gt)
            dgt_ref[:, cols] = (dact * up_s[:, cols].astype(F32) * sg * (1.0 + gt * (1.0 - sg))).astype(BF16)
            dup_ref[:, cols] = (dact * gt * sg).astype(BF16)
        dh = _dot(dgt_ref[...], wg_ref[...]) + _dot(dup_ref[...], wu_ref[...])
        dxn, dgr2 = _rms_bwd(dh, x1, g2_ref[...])
        _acc(dg2_ref, _colsum8(dgr2), first)
        dx1 = dx2 + dxn
        dx1_ref[...] = dx1
        dx1b = dx1.astype(BF16)
        dx1b_ref[...] = dx1b
        dy = _dot_nt(dx1b, wo_ref[...])
        dya_ref[...] = dy[:, :RW]
        dyb_ref[...] = dy[:, RW:]

    row = pl.BlockSpec((tm, D_MODEL), lambda i: (i, 0))
    wide = pl.BlockSpec((tm, D_FF), lambda i: (i, 0))
    half = pl.BlockSpec((tm, RW), lambda i: (i, 0))
    wsp = pl.BlockSpec((D_FF, D_MODEL), lambda i: (0, 0))
    vec = pl.BlockSpec((1, D_MODEL), lambda i: (0, 0))
    part = pl.BlockSpec((8, D_MODEL), lambda i: (0, 0))
    bf = lambda n: jax.ShapeDtypeStruct((t, n), BF16)
    return pl.pallas_call(
        body, name="ffn_all", grid=(t // tm,),
        in_specs=[row, row, wsp, wsp, wsp, pl.BlockSpec((D_MODEL, D_MODEL), lambda i: (0, 0)), vec, vec, row],
        out_specs=[row, wide, row, wide, wide, row, row, half, half, pl.BlockSpec((8, 128), lambda i: (0, 0)), part, part],
        out_shape=[bf(D_MODEL), bf(D_FF), bf(D_MODEL), bf(D_FF), bf(D_FF), bf(D_MODEL),
                   jax.ShapeDtypeStruct((t, D_MODEL), F32), jax.ShapeDtypeStruct((t, RW), F32),
                   jax.ShapeDtypeStruct((t, RW), F32), jax.ShapeDtypeStruct((8, 128), F32),
                   jax.ShapeDtypeStruct((8, D_MODEL), F32), jax.ShapeDtypeStruct((8, D_MODEL), F32)],
        scratch_shapes=[pltpu.VMEM((tm, D_FF), BF16), pltpu.VMEM((tm, D_FF), BF16)],
        compiler_params=_params(("arbitrary",)),
    )(x, ycat, wg, wu, wd, wout, g2, gf, tgt)


def _wgrad(a, b, tk, tn, name):
    t, kdim = a.shape
    ndim = b.shape[1]

    def body(a_ref, b_ref, o_ref):
        o_ref[...] = _dot_tn(a_ref[...], b_ref[...])

    return pl.pallas_call(
        body, name=name, grid=(kdim // tk, ndim // tn),
        in_specs=[pl.BlockSpec((t, tk), lambda i, j: (0, i)), pl.BlockSpec((t, tn), lambda i, j: (0, j))],
        out_specs=pl.BlockSpec((tk, tn), lambda i, j: (i, j)),
        out_shape=jax.ShapeDtypeStruct((kdim, ndim), F32),
        compiler_params=_params(("parallel", "parallel")),
    )(a, b)


def _post_bwd(dya, y, r, k2, v, g, lnw, lnb, rk):
    t = y.shape[0]
    tm = _POST_TM

    def body(d_ref, y_ref, r_ref, k_ref, v_ref, g_ref, lnw_ref, lnb_ref, rk_ref,
             dy_ref, dr_ref, dk_ref, dv_ref, dg_ref, dlnw_ref, dlnb_ref, drk_ref):
        first = pl.program_id(0) == 0
        ones = jnp.ones((tm, 1), F32)
        prim = (y_ref[...], r_ref[...], k_ref[...], v_ref[...], g_ref[...],
                ones * lnw_ref[...], ones * lnb_ref[...], ones * rk_ref[...])
        _, vjp = jax.vjp(_post_fn, *prim)
        dy, dr, dk, dv, dg, dlnw, dlnb, drk = vjp(d_ref[...])
        dy_ref[...] = dy
        dr_ref[...] = dr
        dk_ref[...] = dk
        dv_ref[...] = dv
        dg_ref[...] = dg
        _acc(dlnw_ref, _colsum8(dlnw), first)
        _acc(dlnb_ref, _colsum8(dlnb), first)
        _acc(drk_ref, _colsum8(drk), first)

    row = pl.BlockSpec((tm, RW), lambda i: (i, 0))
    vec = pl.BlockSpec((1, RW), lambda i: (0, 0))
    part = pl.BlockSpec((8, RW), lambda i: (0, 0))
    return pl.pallas_call(
        body, name="rwkv_post_bwd", grid=(t // tm,),
        in_specs=[row] * 6 + [vec] * 3, out_specs=[row] * 5 + [part] * 3,
        out_shape=[jax.ShapeDtypeStruct((t, RW), F32)] * 5 + [jax.ShapeDtypeStruct((8, RW), F32)] * 3,
        compiler_params=_params(("arbitrary",)),
    )(dya, y, r, k2, v, g, lnw, lnb, rk)


def _wkv_bwd(dy, s0s, r, lw, k2, v, kk, a):
    t = r.shape[0]
    nc = t // CHUNK

    def body(dy_ref, s_ref, r_ref, lw_ref, k_ref, v_ref, kk_ref, a_ref,
             dr_ref, dlw_ref, dk_ref, dv_ref, dkk_ref, da_ref, ds):
        @pl.when(pl.program_id(0) == 0)
        def _():
            ds[...] = jnp.zeros_like(ds)

        _, vjp = jax.vjp(_wkv_chunk_fn, s_ref[0],
                         *[_pairs(ref) for ref in (r_ref, lw_ref, k_ref, v_ref, kk_ref, a_ref)])
        res = vjp((_pairs(dy_ref), ds[...]))
        ds[...] = res[0]
        for ref, val in zip((dr_ref, dlw_ref, dk_ref, dv_ref, dkk_ref, da_ref), res[1:]):
            for p in range(N_PAIR):
                ref[:, 128 * p:128 * (p + 1)] = val[p]

    blk = pl.BlockSpec((CHUNK, RW), lambda c: (nc - 1 - c, 0))
    return pl.pallas_call(
        body, name="wkv_bwd", grid=(nc,),
        in_specs=[blk, pl.BlockSpec((1, N_PAIR, 128, 128), lambda c: (nc - 1 - c, 0, 0, 0))] + [blk] * 6,
        out_specs=[blk] * 6,
        out_shape=[jax.ShapeDtypeStruct((t, RW), F32)] * 6,
        scratch_shapes=[pltpu.VMEM((N_PAIR, 128, 128), F32)],
        compiler_params=_params(("arbitrary",)),
    )(dy, s0s, r, lw, k2, v, kk, a)


def _prep_bwd(proj, pw, douts):
    t = proj.shape[0]
    tm = _PREP_TM
    nt = t // tm

    def body(p_ref, l8_ref, mu, w0, w2p, a0, a2p, g2, k_k, k_a, dr, dr2, dlw, dk2, dk22, dv, dv2, dkk, da, dg,
             dp_ref, dmu_ref, dw0_ref, dw2_ref, da0_ref, da2_ref, dg2_ref, dkk_ref, dka_ref, carry):
        i = pl.program_id(0)
        first = i == 0

        @pl.when(first)
        def _():
            carry[...] = jnp.zeros_like(carry)

        p = p_ref[...]
        pprev = _shifted(p, l8_ref[...], i == nt - 1)
        ones = jnp.ones((tm, 1), F32)
        prim = (p, pprev, ones * mu[...], ones * w0[...], w2p[...], ones * a0[...], a2p[...], g2[...],
                ones * k_k[...], ones * k_a[...])
        _, vjp = jax.vjp(_prep_fn, *prim)
        dp, dpp, dmu, dw0, dw2, da0, da2, dg2, dkk_, dka = vjp(
            (dr[...] + dr2[...], dlw[...], dk2[...] + dk22[...], dv[...] + dv2[...], dkk[...], da[...], dg[...]))
        up = pltpu.roll(dpp, tm - 1, axis=0)
        rid = lax.broadcasted_iota(jnp.int32, dpp.shape, 0)
        dp_ref[...] = dp + jnp.where(rid == tm - 1, carry[0:1, :], up)
        carry[...] = jnp.broadcast_to(dpp[0:1, :], carry.shape)
        _acc(dmu_ref, _colsum8(dmu), first)
        _acc(dw0_ref, _colsum8(dw0), first)
        _acc(dw2_ref, dw2, first)
        _acc(da0_ref, _colsum8(da0), first)
        _acc(da2_ref, da2, first)
        _acc(dg2_ref, dg2, first)
        _acc(dkk_ref, _colsum8(dkk_), first)
        _acc(dka_ref, _colsum8(dka), first)

    rev = lambda i: (nt - 1 - i, 0)
    row = pl.BlockSpec((tm, RW), rev)
    part = lambda n: pl.BlockSpec((8, n), lambda i: (0, 0))
    mat = pl.BlockSpec((128, RW), lambda i: (0, 0))
    return pl.pallas_call(
        body, name="rwkv_prep_bwd", grid=(nt,),
        in_specs=[pl.BlockSpec((tm, SHIFT_COLS), rev),
                  pl.BlockSpec((8, SHIFT_COLS), lambda i: (jnp.maximum((nt - 1 - i) * (tm // 8) - 1, 0), 0))]
                 + _prep_specs(tm) + [row] * 10,
        out_specs=[pl.BlockSpec((tm, SHIFT_COLS), rev), part(SHIFT_COLS), part(RW), mat, part(RW), mat, mat,
                   part(RW), part(RW)],
        out_shape=[jax.ShapeDtypeStruct((t, SHIFT_COLS), F32), jax.ShapeDtypeStruct((8, SHIFT_COLS), F32),
                   jax.ShapeDtypeStruct((8, RW), F32), jax.ShapeDtypeStruct((128, RW), F32),
                   jax.ShapeDtypeStruct((8, RW), F32), jax.ShapeDtypeStruct((128, RW), F32),
                   jax.ShapeDtypeStruct((128, RW), F32), jax.ShapeDtypeStruct((8, RW), F32),
                   jax.ShapeDtypeStruct((8, RW), F32)],
        scratch_shapes=[pltpu.VMEM((8, SHIFT_COLS), F32)],
        compiler_params=_params(("arbitrary",)),
    )(proj, proj, *pw, *douts)


def _combine_bwd(dyb, o, l, og):
    t = dyb.shape[0]
    tm = _COMB_TM

    def body(d_ref, o_ref, l_ref, og_ref, do_ref, dl_ref, dog_ref):
        ones = jnp.ones((tm, 1), F32)
        dog = []
        for p in range(N_PAIR):
            cols = slice(128 * p, 128 * (p + 1))
            _, vjp = jax.vjp(_combine_fn, o_ref[0, p], o_ref[1, p], o_ref[2, p], l_ref[0, p], l_ref[1, p], l_ref[2, p],
                             ones * og_ref[:, cols])
            res = vjp(d_ref[:, cols])
            for b in range(3):
                do_ref[b, p] = res[b]
                dl_ref[b, p] = res[3 + b]
            dog.append(_colsum8(res[6]))
        _acc(dog_ref, jnp.concatenate(dog, axis=1), pl.program_id(0) == 0)

    blk = pl.BlockSpec((3, N_PAIR, tm, 128), lambda i: (0, 0, i, 0))
    return pl.pallas_call(
        body, name="attn_combine_bwd", grid=(t // tm,),
        in_specs=[pl.BlockSpec((tm, RW), lambda i: (i, 0)), blk, blk, pl.BlockSpec((1, RW), lambda i: (0, 0))],
        out_specs=[blk, blk, pl.BlockSpec((8, RW), lambda i: (0, 0))],
        out_shape=[jax.ShapeDtypeStruct((3, N_PAIR, t, 128), F32)] * 2 + [jax.ShapeDtypeStruct((8, RW), F32)],
        compiler_params=_params(("arbitrary",)),
    )(dyb, o, l, og)


def _attn_bwd(do, dl, o, lse, qkv):
    t = qkv.shape[2]

    def body(do_ref, dl_ref, o_ref, l_ref, q_ref, k_ref, v_ref, dq_ref, dk_ref, dv_ref):
        @pl.when(pl.program_id(1) == 0)
        def _():
            for ref in (dq_ref, dk_ref, dv_ref):
                ref[...] = jnp.zeros_like(ref)

        def unit(di, places, has_prev):
            cur = [_dilated_rows(d, r, n) for d, r, n in places]
            q, kc, vc = [_take(ref, (0,), cur) for ref in (q_ref, k_ref, v_ref)]
            kp = vp = None
            if has_prev:
                prv = [_dilated_rows(d, r, n - 1) for d, r, n in places]
                kp, vp = [_take(ref, (0,), prv) for ref in (k_ref, v_ref)]
            res = _attn_block_bwd(q, kc, vc, kp, vp, *[_take(ref, (0,), cur) for ref in (o_ref, l_ref, do_ref, dl_ref)])
            _put(dq_ref, (), cur, res[0], add=True)
            _put(dk_ref, (), cur, res[1], add=True)
            _put(dv_ref, (), cur, res[2], add=True)
            if has_prev:
                _put(dk_ref, (), prv, res[3], add=True)
                _put(dv_ref, (), prv, res[4], add=True)

        _for_each_sequence(t, unit)

    spec = lambda j: pl.BlockSpec((1, ATTN_GROUP, t, 128), lambda i, b: (j, i, 0, 0))
    branch = pl.BlockSpec((1, ATTN_GROUP, t, 128), lambda i, b: (b, i, 0, 0))
    out = pl.BlockSpec((ATTN_GROUP, t, 128), lambda i, b: (i, 0, 0))
    return pl.pallas_call(
        body, name="attn_bwd", grid=(N_PAIR // ATTN_GROUP, len(DILATIONS)),
        in_specs=[branch] * 4 + [spec(0), spec(1), spec(2)], out_specs=[out] * 3,
        out_shape=[jax.ShapeDtypeStruct((N_PAIR, t, 128), F32)] * 3,
        compiler_params=_params(("parallel", "arbitrary")),
    )(do, dl, o, lse, qkv, qkv, qkv)


def _in_proj_bwd(dpa, dq, dk, dv, win, x, g1, dx1):
    t = x.shape[0]
    tm = 256

    def body(dpa_ref, dq_ref, dk_ref, dv_ref, w_ref, x_ref, g_ref, dx1_ref, dproj_ref, dx_ref, dg_ref):
        parts = [dpa_ref[...]] + [ref[p] for ref in (dq_ref, dk_ref, dv_ref) for p in range(N_PAIR)]
        dproj = jnp.concatenate([z.astype(BF16) for z in parts], axis=1)
        dproj_ref[...] = dproj
        dh = _dot(dproj, w_ref[...])
        dxn, dgr = _rms_bwd(dh, x_ref[...], g_ref[...])
        dx_ref[...] = dx1_ref[...] + dxn
        _acc(dg_ref, _colsum8(dgr), pl.program_id(0) == 0)

    row = pl.BlockSpec((tm, D_MODEL), lambda i: (i, 0))
    pair = pl.BlockSpec((N_PAIR, tm, 128), lambda i: (0, i, 0))
    return pl.pallas_call(
        body, name="in_proj_bwd", grid=(t // tm,),
        in_specs=[pl.BlockSpec((tm, SHIFT_COLS), lambda i: (i, 0))] + [pair] * 3
                 + [pl.BlockSpec((IN_COLS, D_MODEL), lambda i: (0, 0)), row, pl.BlockSpec((1, D_MODEL), lambda i: (0, 0)), row],
        out_specs=[pl.BlockSpec((tm, IN_COLS), lambda i: (i, 0)), row, pl.BlockSpec((8, D_MODEL), lambda i: (0, 0))],
        out_shape=[jax.ShapeDtypeStruct((t, IN_COLS), BF16), jax.ShapeDtypeStruct((t, D_MODEL), F32),
                   jax.ShapeDtypeStruct((8, D_MODEL), F32)],
        compiler_params=_params(("arbitrary",)),
    )(dpa, dq, dk, dv, win, x, g1, dx1)


def _pad_lora(w, lo):
    z = jnp.zeros((64, RW), F32)
    return jnp.concatenate([w, z], axis=0) if lo == 0 else jnp.concatenate([z, w], axis=0)


def _local_step(x, tgt, win, vecs, w2, a2, g2m, get_rest, send_rest):
    pw = (vecs["mu_shift"], vecs["decay_w0"], _pad_lora(w2, 0), vecs["iclr_a0"], _pad_lora(a2, 64), g2m,
          vecs["k_k"], vecs["k_a"])
    h, proj, qkv = _in_proj(x, vecs["mix_norm_g"], win)
    r, lw, k2, v, kk, a, g = _prep_fwd(proj, pw)
    y, s0s = _wkv_fwd(r, lw, k2, v, kk, a)
    o_att, l_att = _attn_fwd(qkv)
    ycat = _mixers_out(y, r, k2, v, g, vecs["ln_x_w"], vecs["ln_x_b"], vecs["r_k"], o_att, l_att, vecs["attn_out_g"])
    wout, wg, wu, wd = get_rest(ycat)
    h2, act, dx2b, dgt, dup, dx1b, dx1, dya, dyb, loss8, dgf, dg2n = _ffn_all(
        x, ycat, wg, wu, wd, wout, vecs["ffn_norm_g"], vecs["final_norm_g"], tgt)
    gw = {
        "w_down": _wgrad(act, dx2b, 1408, 1024, "wgrad_down"),
        "w_gate": _wgrad(dgt, h2, 1408, 1024, "wgrad_gate"),
        "w_up": _wgrad(dup, h2, 1408, 1024, "wgrad_up"),
        "w_out": _wgrad(ycat, dx1b, 1024, 1024, "wgrad_out"),
    }

    lnw = vecs["ln_x_w"] + send_rest(gw)[0, 0]
    dy, dr_p, dk2_p, dv_p, dg, dlnw, dlnb, drk = _post_bwd(dya, y, r, k2, v, g, lnw, vecs["ln_x_b"], vecs["r_k"])
    dr_s, dlw, dk2_s, dv_s, dkk, da = _wkv_bwd(dy, s0s, r, lw, k2, v, kk, a)
    dpa, dmu, dw0, dw2p, da0, da2p, dg2m, dk_k, dk_a = _prep_bwd(
        proj, pw, (dr_p, dr_s, dlw, dk2_p, dk2_s, dv_p, dv_s, dkk, da, dg))

    do_att, dl_att, dog = _combine_bwd(dyb, o_att, l_att, vecs["attn_out_g"])
    dq, dk, dv = _attn_bwd(do_att, dl_att, o_att, l_att, qkv)
    dproj, dx, dg1 = _in_proj_bwd(dpa, dq, dk, dv, win, x, vecs["mix_norm_g"], dx1)
    gw["w_in"] = _wgrad(dproj, h, 1664, 1024, "wgrad_in")
    gw["decay_w2"] = dw2p[:64]
    gw["iclr_a2"] = da2p[64:]
    gw["gate_g2"] = dg2m
    gv = {"mix_norm_g": dg1, "mu_shift": dmu, "decay_w0": dw0, "iclr_a0": da0, "k_k": dk_k, "k_a": dk_a, "r_k": drk,
          "ln_x_w": dlnw, "ln_x_b": dlnb, "attn_out_g": dog, "ffn_norm_g": dg2n, "final_norm_g": dgf}
    return loss8, dx, gw, gv


N_CHIP = 4
N_DEV = 8
MATS = ("w_in", "w_out", "w_gate", "w_up", "w_down")
LORAS = ("decay_w2", "iclr_a2", "gate_g2")
VECS = (("mix_norm_g", 1024), ("mu_shift", 1792), ("decay_w0", 512), ("iclr_a0", 512), ("k_k", 512), ("k_a", 512),
        ("r_k", 512), ("ln_x_w", 512), ("ln_x_b", 512), ("attn_out_g", 512), ("ffn_norm_g", 1024),
        ("final_norm_g", 1024))
N_VEC = sum(n for _, n in VECS)
N_SMALL = N_VEC + 128
ANY = pl.BlockSpec(memory_space=pl.ANY)


def _flip(v, f):
    return 1 - v if f else v


class _Me:
    def __init__(self, mode):
        x, y, c = lax.axis_index("x"), lax.axis_index("y"), lax.axis_index("c")
        self.core, self.chip, self.dev = c, 2 * x + y, 4 * x + 2 * y + c
        self.sibling = (x, y, 1 - c)
        if mode == "chips":
            self.peers = [(px, py, c) for px, py in ((1 - x, y), (x, 1 - y), (1 - x, 1 - y))]
        else:
            self.peers = [(_flip(x, k & 4), _flip(y, k & 2), _flip(c, k & 1)) for k in range(1, N_DEV)]


def _half(core, rows):
    h = rows // 2
    return pl.ds(pl.multiple_of(core * h, h), h)


_BY_CHIP = ("gather", "chipsum")


def _peer_copy(srcs, dsts, kinds, send_sems, recv_sems, me, j, i, incoming):
    px, py, pc = me.peers[j]
    pchip, pdev = 2 * px + py, 4 * px + 2 * py + pc
    src, dst, kind = srcs[i], dsts[i], kinds[i]
    if kind == "gather":
        rows = _half(me.core, src.shape[0])
        src, dst = src.at[rows], dst.at[pchip if incoming else me.chip, rows]
    elif kind == "scatter":
        src, dst = src.at[pchip, _half(pc, src.shape[1])], dst.at[pdev if incoming else me.dev]
    elif kind == "chipsum":
        src, dst = src.at[pchip], dst.at[pchip if incoming else me.chip]
    else:
        dst = dst.at[pdev if incoming else me.dev]
    n = len(srcs)
    return pltpu.make_async_remote_copy(src_ref=src, dst_ref=dst, send_sem=send_sems.at[n * j + i],
                                        recv_sem=recv_sems.at[n * j + i], device_id=(px, py, pc), device_id_type=MESH)


def _mode(kinds):
    return "chips" if kinds[0] in _BY_CHIP else "devs"


def _npeer(kinds):
    return N_CHIP - 1 if kinds[0] in _BY_CHIP else N_DEV - 1


def _sibling_halves(gs, name):
    n = len(gs)

    def body(*refs):
        srcs, dsts, send_sems, recv_sems = refs[:n], refs[n:2 * n], refs[2 * n], refs[2 * n + 1]
        me = _Me("chips")

        def copy(i, p):
            return pltpu.make_async_remote_copy(
                src_ref=srcs[i].at[p, _half(1 - me.core, srcs[i].shape[1])], dst_ref=dsts[i].at[p],
                send_sem=send_sems.at[N_CHIP * i + p], recv_sem=recv_sems.at[N_CHIP * i + p],
                device_id=me.sibling, device_id_type=MESH)

        copies = [copy(i, p) for i in range(n) for p in range(N_CHIP)]
        for cp in copies:
            cp.start()
        for cp in copies:
            cp.wait()

    return pl.pallas_call(
        body, name=name, in_specs=[ANY] * n, out_specs=[ANY] * n,
        out_shape=[jax.ShapeDtypeStruct((N_CHIP, g.shape[1] // 2, g.shape[2]), g.dtype) for g in gs],
        scratch_shapes=[pltpu.SemaphoreType.DMA((N_CHIP * n,)), pltpu.SemaphoreType.DMA((N_CHIP * n,))],
    )(*gs)


def _add_halves(g, other, core, tr, name):
    _, h, cols = other.shape

    def body(core_ref, g_ref, o_ref, out_ref):
        out_ref[...] = (g_ref[...].astype(F32) + o_ref[...].astype(F32)).astype(BF16)

    blk = lambda off: pl.BlockSpec((1, tr, cols), lambda p, i, core_ref: (p, core_ref[0] * (h // tr) * off + i, 0))
    return pl.pallas_call(
        body, name=name,
        grid_spec=pltpu.PrefetchScalarGridSpec(num_scalar_prefetch=1, grid=(N_CHIP, h // tr),
                                               in_specs=[blk(1), blk(0)], out_specs=blk(0)),
        out_shape=jax.ShapeDtypeStruct(other.shape, BF16),
        compiler_params=_params(("parallel", "parallel")),
    )(core, g, other)


def _swap_gathered(lands, name):
    n = len(lands)

    def body(*refs):
        dsts, send_sems, recv_sems = refs[n:2 * n], refs[2 * n], refs[2 * n + 1]
        me = _Me("chips")

        def copy(j, i, incoming):
            px, py, _ = me.peers[j]
            rows_out, rows_in = _half(me.core, dsts[i].shape[1]), _half(1 - me.core, dsts[i].shape[1])
            return pltpu.make_async_remote_copy(
                src_ref=dsts[i].at[2 * px + py, rows_out], dst_ref=dsts[i].at[2 * px + py, rows_in if incoming else rows_out],
                send_sem=send_sems.at[n * j + i], recv_sem=recv_sems.at[n * j + i], device_id=me.sibling, device_id_type=MESH)

        sends = [copy(j, i, False) for j in range(3) for i in range(n)]
        for cp in sends:
            cp.start()
        for j in range(3):
            for i in range(n):
                copy(j, i, True).wait_recv()
        for cp in sends:
            cp.wait_send()

    return pl.pallas_call(
        body, name=name, in_specs=[ANY] * n, out_specs=[ANY] * n,
        out_shape=[jax.ShapeDtypeStruct(l.shape, l.dtype) for l in lands],
        input_output_aliases={i: i for i in range(n)},
        scratch_shapes=[pltpu.SemaphoreType.DMA((3 * n,)), pltpu.SemaphoreType.DMA((3 * n,))],
    )(*lands)


def _join_halves(sums, name):
    n = len(sums)

    def body(*refs):
        dsts, send_sems, recv_sems = refs[n:2 * n], refs[2 * n], refs[2 * n + 1]
        me = _Me("chips")

        def copy(i, incoming):
            mine, other = _half(me.core, dsts[i].shape[0]), _half(1 - me.core, dsts[i].shape[0])
            return pltpu.make_async_remote_copy(src_ref=dsts[i].at[mine], dst_ref=dsts[i].at[other if incoming else mine],
                                                send_sem=send_sems.at[i], recv_sem=recv_sems.at[i],
                                                device_id=me.sibling, device_id_type=MESH)

        sends = [copy(i, False) for i in range(n)]
        for cp in sends:
            cp.start()
        for i in range(n):
            copy(i, True).wait_recv()
        for cp in sends:
            cp.wait_send()

    return pl.pallas_call(
        body, name=name, in_specs=[ANY] * n, out_specs=[ANY] * n,
        out_shape=[jax.ShapeDtypeStruct(s.shape, s.dtype) for s in sums],
        input_output_aliases={i: i for i in range(n)},
        scratch_shapes=[pltpu.SemaphoreType.DMA((n,)), pltpu.SemaphoreType.DMA((n,))],
    )(*sums)


HBM = pl.BlockSpec(memory_space=pltpu.HBM)
SEM = pl.BlockSpec(memory_space=pltpu.SEMAPHORE)
EFFECT = pltpu.SideEffectType.DATAFLOW_SIDE_EFFECTING


def _swap_start(arrs, lands, kinds, name):
    n = len(arrs)

    def body(*refs):
        srcs, dsts, send_sems, recv_sems, token = refs[:n], refs[n:2 * n], refs[2 * n], refs[2 * n + 1], refs[-1]
        me = _Me(_mode(kinds))
        for j in range(len(me.peers)):
            for i in range(n):
                _peer_copy(srcs, dsts, kinds, send_sems, recv_sems, me, j, i, False).start()
        token[...] = jnp.zeros_like(token)

    ns = _npeer(kinds) * n
    outs = pl.pallas_call(
        body, name=name,
        out_shape=(pltpu.SemaphoreType.DMA((ns,)), pltpu.SemaphoreType.DMA((ns,)),
                   *[pltpu.HBM(a.shape, a.dtype) for a in arrs], *[pltpu.HBM(l.shape, l.dtype) for l in lands],
                   jax.ShapeDtypeStruct((8, 128), F32)),
        in_specs=[HBM] * (2 * n), out_specs=(SEM, SEM, *[HBM] * (2 * n), pl.BlockSpec(memory_space=pltpu.VMEM)),
        input_output_aliases={k: 2 + k for k in range(2 * n)},
        compiler_params=pltpu.CompilerParams(has_side_effects=EFFECT),
    )(*[pltpu.with_memory_space_constraint(a, pltpu.HBM) for a in arrs],
      *[pltpu.with_memory_space_constraint(l, pltpu.HBM) for l in lands])
    return outs[0], outs[1], outs[2:2 + n], outs[2 + n:2 + 2 * n], outs[-1]


def _swap_wait(send_sems, recv_sems, srcs_thru, lands_thru, after, kinds, name):
    n = len(srcs_thru)

    def body(*refs):
        srcs, dsts, s_sems, r_sems = refs[:n], refs[n:2 * n], refs[2 * n], refs[2 * n + 1]
        me = _Me(_mode(kinds))
        for j in range(len(me.peers)):
            for i in range(n):
                cp = _peer_copy(srcs, dsts, kinds, s_sems, r_sems, me, j, i, True)
                cp.wait_send()
                cp.wait_recv()

    outs = pl.pallas_call(
        body, name=name,
        out_shape=tuple(pltpu.HBM(a.shape, a.dtype) for a in (*srcs_thru, *lands_thru)),
        in_specs=[HBM] * (2 * n) + [SEM, SEM, ANY], out_specs=tuple([HBM] * (2 * n)),
        input_output_aliases={k: k for k in range(2 * n)},
        compiler_params=pltpu.CompilerParams(has_side_effects=EFFECT),
    )(*srcs_thru, *lands_thru, send_sems, recv_sems, after)
    return outs[n:]


def _adamw(w, g, m, v):
    m = ADAM_B1 * m + (1.0 - ADAM_B1) * g
    v = ADAM_B2 * v + (1.0 - ADAM_B2) * (g * g)
    m_hat = m / (1.0 - ADAM_B1 ** ADAM_STEP)
    v_hat = v / (1.0 - ADAM_B2 ** ADAM_STEP)
    delta = -ADAM_LR * (m_hat / (jnp.sqrt(v_hat) + ADAM_EPS) + ADAM_WD * w)
    return delta, m, v


def _reduce8(rbuf, core, tr, name):
    slots, h, cols = rbuf.shape

    def body(core_ref, r_ref, g_ref):
        g = r_ref[0].astype(F32)
        for s in range(1, slots):
            g = g + r_ref[s].astype(F32)
        g_ref[...] = g

    return pl.pallas_call(
        body, name=name,
        grid_spec=pltpu.PrefetchScalarGridSpec(
            num_scalar_prefetch=1, grid=(h // tr,),
            in_specs=[pl.BlockSpec((slots, tr, cols), lambda i, core_ref: (0, i, 0))],
            out_specs=pl.BlockSpec((tr, cols), lambda i, core_ref: (core_ref[0] * (h // tr) + i, 0))),
        out_shape=jax.ShapeDtypeStruct((2 * h, cols), F32),
        compiler_params=_params(("parallel",)),
    )(core, rbuf)


def _adamw_call(g, w, m, v, tr, name):
    _, rows, cols = w.shape

    def body(g_in, w_ref, m_ref, v_ref, g_ref, d_ref, nm_ref, nv_ref):
        g = g_in[...]
        g_ref[0] = g
        d_ref[0], nm_ref[0], nv_ref[0] = _adamw(w_ref[0], g, m_ref[0], v_ref[0])

    row = pl.BlockSpec((1, tr, cols), lambda i: (0, i, 0))
    return pl.pallas_call(
        body, name=name, grid=(rows // tr,),
        in_specs=[pl.BlockSpec((tr, cols), lambda i: (i, 0)), row, row, row], out_specs=[row] * 4,
        out_shape=[jax.ShapeDtypeStruct(w.shape, F32)] * 4,
        compiler_params=_params(("parallel",)),
    )(g, w, m, v)


def _rowsum_small(parts, loss8):
    def body(*refs):
        out = refs[-1]
        c0 = 0
        for ref in refs[:-1]:
            n = ref.shape[1]
            out[:, c0:c0 + n] = jnp.sum(ref[...], axis=0, keepdims=True)
            c0 += n

    return pl.pallas_call(body, name="rowsum_small", out_shape=jax.ShapeDtypeStruct((1, N_SMALL), F32))(*parts, loss8)


def _reduce_adamw_small(sbuf, ws, ms, vs):
    nv = len(ws)

    def body(*refs):
        s_ref, ins, outs = refs[0], refs[1:1 + 3 * nv], refs[1 + 3 * nv:]
        tot = s_ref[0]
        for s in range(1, N_DEV):
            tot = tot + s_ref[s]
        c0 = 0
        for i in range(nv):
            n = ins[i].shape[1]
            g = tot[:, c0:c0 + n]
            outs[i][...] = g
            outs[nv + i][...], outs[2 * nv + i][...], outs[3 * nv + i][...] = _adamw(
                ins[i][...], g, ins[nv + i][...], ins[2 * nv + i][...])
            c0 += n
        outs[-1][...] = tot[:, c0:]

    return pl.pallas_call(
        body, name="reduce_adamw_small",
        out_shape=[jax.ShapeDtypeStruct(a.shape, F32) for a in ws] * 4 + [jax.ShapeDtypeStruct((1, 128), F32)],
    )(sbuf, *ws, *ms, *vs)


_TRANSPOSED = ("w_in", "w_gate", "w_up")
_ROW_STACKED = MATS
_ADAM_TILE = {"w_in": 208, "w_out": 256, "w_gate": 176, "w_up": 176, "w_down": 176, "lora": 256}
_SUM_TILE = {"w_in": 208, "w_out": 128, "w_gate": 176, "w_up": 176, "w_down": 176, "lora": 128}


def _full(n, stacked):
    p, r, c = stacked.shape
    if n in _ROW_STACKED:
        return stacked.reshape(p * r, c)
    return jnp.transpose(stacked, (1, 0, 2)).reshape(r, p * c)


def _by_chip(n, full):
    if n in _ROW_STACKED:
        return full.reshape(N_CHIP, full.shape[0] // N_CHIP, full.shape[1])
    r, c = full.shape
    return jnp.transpose(full.reshape(r, N_CHIP, c // N_CHIP), (1, 0, 2))


def _with_own(land_shape, dtype, own, slot):
    return lax.dynamic_update_slice(lax.empty(land_shape, dtype), own[None], (slot,) + (0,) * own.ndim)


def kernel(x, mix_norm_g, w_in, mu_shift, decay_w0, decay_w2, iclr_a0, iclr_a2, gate_g2, k_k, k_a, r_k, ln_x_w, ln_x_b, attn_out_g, w_out, ffn_norm_g, w_gate, w_up, w_down, final_norm_g, loss_target, m_mix_norm_g, m_w_in, m_mu_shift, m_decay_w0, m_decay_w2, m_iclr_a0, m_iclr_a2, m_gate_g2, m_k_k, m_k_a, m_r_k, m_ln_x_w, m_ln_x_b, m_attn_out_g, m_w_out, m_ffn_norm_g, m_w_gate, m_w_up, m_w_down, m_final_norm_g, v_mix_norm_g, v_w_in, v_mu_shift, v_decay_w0, v_decay_w2, v_iclr_a0, v_iclr_a2, v_gate_g2, v_k_k, v_k_a, v_r_k, v_ln_x_w, v_ln_x_b, v_attn_out_g, v_w_out, v_ffn_norm_g, v_w_gate, v_w_up, v_w_down, v_final_norm_g):
    names = ("mix_norm_g", "w_in", "mu_shift", "decay_w0", "decay_w2", "iclr_a0", "iclr_a2", "gate_g2", "k_k", "k_a",
             "r_k", "ln_x_w", "ln_x_b", "attn_out_g", "w_out", "ffn_norm_g", "w_gate", "w_up", "w_down", "final_norm_g")
    w = dict(zip(names, (mix_norm_g, w_in, mu_shift, decay_w0, decay_w2, iclr_a0, iclr_a2, gate_g2, k_k, k_a, r_k,
                         ln_x_w, ln_x_b, attn_out_g, w_out, ffn_norm_g, w_gate, w_up, w_down, final_norm_g)))
    m = dict(zip(names, (m_mix_norm_g, m_w_in, m_mu_shift, m_decay_w0, m_decay_w2, m_iclr_a0, m_iclr_a2, m_gate_g2,
                         m_k_k, m_k_a, m_r_k, m_ln_x_w, m_ln_x_b, m_attn_out_g, m_w_out, m_ffn_norm_g, m_w_gate,
                         m_w_up, m_w_down, m_final_norm_g)))
    v = dict(zip(names, (v_mix_norm_g, v_w_in, v_mu_shift, v_decay_w0, v_decay_w2, v_iclr_a0, v_iclr_a2, v_gate_g2,
                         v_k_k, v_k_a, v_r_k, v_ln_x_w, v_ln_x_b, v_attn_out_g, v_w_out, v_ffn_norm_g, v_w_gate,
                         v_w_up, v_w_down, v_final_norm_g)))
    first = ("w_in", "lora")
    rest = ("w_out", "w_gate", "w_up", "w_down")
    xi, yi, ci = lax.axis_index("x"), lax.axis_index("y"), lax.axis_index("c")
    my_chip, my_dev = 2 * xi + yi, 4 * xi + 2 * yi + ci
    gather, scatter = ("gather",) * 4, ("scatter",) * 4

    def stored(d):
        out = {n: jnp.transpose(d[n][0]) if n in _TRANSPOSED else d[n][0] for n in MATS}
        out["lora"] = jnp.concatenate([d[n][0] for n in LORAS], axis=0)
        return out

    ws, ms, vs = stored(w), stored(m), stored(v)
    lora_rows = [(0, 64), (64, 128), (128, 256)]
    mine = [ws["w_in"].astype(BF16), ws["lora"]]
    early = _swap_start(mine, [_with_own((N_CHIP,) + a.shape, a.dtype, a, my_chip) for a in mine], gather[:2],
                        "gather_first_start")
    wb = {n: (ws[n] + early[4][0, 0]).astype(BF16) for n in rest}
    lands = [_with_own((N_CHIP,) + wb[n].shape, BF16, wb[n], my_chip) for n in rest]
    ssem, rsem, srcs_thru, lands_thru, tok = _swap_start([wb[n] for n in rest], lands, gather, "gather_rest_start")
    got = _swap_wait(early[0], early[1], early[2], early[3], tok, gather[:2], "gather_first_wait")
    win_all, lora_all = _swap_gathered(got, "gather_first_halves")
    win = _full("w_in", win_all)
    w2, a2, g2m = (_full(n, lora_all[:, a:b]) for n, (a, b) in zip(LORAS, lora_rows))

    vecs = {n: w[n].reshape(1, sz) for n, sz in VECS}
    vecs["mix_norm_g"] = vecs["mix_norm_g"] + tok[0, 0]

    def get_rest(after):
        halves = _swap_wait(ssem, rsem, srcs_thru, lands_thru, after, gather, "gather_rest_wait")
        return [_full(n, z) for n, z in zip(rest, _swap_gathered(halves, "gather_rest_halves"))]

    flight = []

    def my_half(g):
        h = g.shape[1] // 2
        return lax.dynamic_slice(g, (my_chip, ci * h, 0), (1, h, g.shape[2]))[0]

    def send_rest(gw):
        gs = [_by_chip(n, gw[n]).astype(BF16) for n in rest]
        into = [_with_own((N_DEV,) + my_half(g).shape, BF16, my_half(g), my_dev) for g in gs]
        flight.extend(_swap_start(gs, into, scatter, "exchange_rest_start"))
        return flight[4]

    loss8, dx, gw, gv = _local_step(x[0], loss_target[0], win, vecs, w2, a2, g2m, get_rest, send_rest)

    core = jnp.reshape(ci, (1,)).astype(jnp.int32)
    gs = [_by_chip("w_in", gw["w_in"]).astype(BF16),
          jnp.concatenate([_by_chip(n, gw[n]) for n in LORAS], axis=1).astype(BF16)]
    theirs = _sibling_halves(gs, "presum_halves")
    sums = [_add_halves(g, o, core, _SUM_TILE[n], "chipsum_" + n) for n, g, o in zip(first, gs, theirs)]
    own = [lax.dynamic_index_in_dim(s, my_chip, 0, keepdims=False) for s in sums]
    last = _swap_start(sums, [_with_own(s.shape, BF16, o, my_chip) for s, o in zip(sums, own)], ("chipsum",) * 2,
                       "exchange_first_start")
    small = _rowsum_small([gv[n] for n, _ in VECS], loss8 + last[4])
    vecs_out = _swap_start([small], [_with_own((N_DEV,) + small.shape, F32, small, my_dev)], ("all",),
                           "exchange_vectors_start")


    def update(group, rbufs, tag):
        sums = [_reduce8(rb, core, _SUM_TILE[n], "reduce_" + n) for n, rb in zip(group, rbufs)]
        gsum = _join_halves(sums, "join_halves_" + tag)
        out = {}
        for n, g in zip(group, gsum):
            r = _adamw_call(g, ws[n][None], ms[n][None], vs[n][None], _ADAM_TILE[n], "adamw_" + n)
            if n == "lora":
                for name, (a, b) in zip(LORAS, lora_rows):
                    out[name] = [z[:, a:b] for z in r]
            else:
                out[n] = [jnp.transpose(z[0])[None] for z in r] if n in _TRANSPOSED else r
        return out, r[1]

    res, done = update(rest, _swap_wait(flight[0], flight[1], flight[2], flight[3], vecs_out[4], scatter,
                                        "exchange_rest_wait"), "rest")
    got = _swap_wait(last[0], last[1], last[2], last[3], done, ("chipsum",) * 2, "exchange_first_wait")
    res_first, done = update(first, got, "first")
    res.update(res_first)
    sbuf = _swap_wait(vecs_out[0], vecs_out[1], vecs_out[2], vecs_out[3], done, ("all",), "exchange_vectors_wait")[0]
    rows = lambda d: [d[n].reshape(1, sz) for n, sz in VECS]
    small_res = _reduce_adamw_small(sbuf, rows(w), rows(m), rows(v))

    outs = []
    for k in range(4):
        piece = {n: r[k] for n, r in res.items()}
        for i, (n, _) in enumerate(VECS):
            piece[n] = small_res[k * len(VECS) + i].reshape(w[n].shape)
        outs.extend(piece[n] for n in names)
    return (small_res[-1][0, 0], dx[None], *outs)
```

```python
import jax
import jax.numpy as jnp
from jax import lax
from jax.experimental import pallas as pl
from jax.experimental.pallas import tpu as pltpu

F32 = jnp.float32
BF16 = jnp.bfloat16

D_MODEL = 1024
HEAD_DIM = 64
RW = 512
N_PAIR = RW // 128
SHIFT_COLS = 1792
IN_COLS = 3328
D_FF = 2816
FF_CHUNK = 256
NORM_EPS = 1e-6
GN_EPS = 64e-5
CHUNK = 64
SUB = 16
WKV_PASSES = 1
ATTN_PASSES = 1
ATTN_BLOCK = 128
DILATIONS = (1, 4, 16)
NEG = -1e30
ADAM_LR, ADAM_B1, ADAM_B2, ADAM_EPS, ADAM_WD, ADAM_STEP = 0.001, 0.9, 0.999, 1e-08, 0.01, 10
VMEM_LIMIT = 56 * 1024 * 1024
MESH = pl.DeviceIdType.MESH


def _params(sem=None, **kw):
    return pltpu.CompilerParams(dimension_semantics=sem, vmem_limit_bytes=VMEM_LIMIT, **kw)


def _dot(a, b):
    return lax.dot_general(a, b, (((1,), (0,)), ((), ())), preferred_element_type=F32)


def _dot_nt(a, b):
    return lax.dot_general(a, b, (((1,), (1,)), ((), ())), preferred_element_type=F32)


def _dot_tn(a, b):
    return lax.dot_general(a, b, (((0,), (0,)), ((), ())), preferred_element_type=F32)


_FORMS = {"nn": ((1,), (0,)), "nt": ((1,), (1,)), "tn": ((0,), (0,))}


def _dg(a, b, form):
    if a.ndim == 3 or b.ndim == 3:
        nb = a.shape[0] if a.ndim == 3 else b.shape[0]
        return jnp.stack([_dg(a[i] if a.ndim == 3 else a, b[i] if b.ndim == 3 else b, form) for i in range(nb)], axis=0)
    return lax.dot_general(a, b, (_FORMS[form], ((), ())), preferred_element_type=F32)


def _split2(x):
    hi = x.astype(BF16)
    return hi, (x - hi.astype(F32)).astype(BF16)


def _split3(x):
    hi = x.astype(BF16)
    rest = x - hi.astype(F32)
    mid = rest.astype(BF16)
    return hi, mid, (rest - mid.astype(F32)).astype(BF16)


def _mm_raw(a, b, form, mode):
    if mode == 1:
        return _dg(a.astype(BF16), b.astype(BF16), form)
    if mode == 3:
        ah, al = _split2(a)
        bh, bl = _split2(b)
        return _dg(ah, bh, form) + (_dg(ah, bl, form) + _dg(al, bh, form))
    if mode == "L3":
        ab = a.astype(BF16)
        b1, b2, b3 = _split3(b)
        if form == "nn":
            n = b.shape[-1]
            wide = _dg(ab, jnp.concatenate([b1, b2, b3], axis=-1), form)
            return wide[..., :n] + (wide[..., n:2 * n] + wide[..., 2 * n:])
        return _dg(ab, b1, form) + (_dg(ab, b2, form) + _dg(ab, b3, form))
    assert mode == "R3", mode
    bb = b.astype(BF16)
    a1, a2, a3 = _split3(a)
    if form in ("nn", "nt"):
        m = a.shape[-2]
        tall = _dg(jnp.concatenate([a1, a2, a3], axis=-2), bb, form)
        return tall[..., :m, :] + (tall[..., m:2 * m, :] + tall[..., 2 * m:, :])
    return _dg(a1, bb, form) + (_dg(a2, bb, form) + _dg(a3, bb, form))


def _mm(a, b, form, mode):
    @jax.custom_vjp
    def f(a, b):
        return _mm_raw(a, b, form, mode)

    def fwd(a, b):
        return _mm_raw(a, b, form, mode), (a, b)

    def bwd(res, ct):
        a, b = res
        la = {1: 1, 3: 3, "L3": None, "R3": "R3"}[mode]
        lb = {1: 1, 3: 3, "L3": "L3", "R3": None}[mode]
        if form == "nn":
            da = None if la is None else _mm_raw(ct, b, "nt", la)
            db = None if lb is None else _mm_raw(a, ct, "tn", lb)
        elif form == "nt":
            da = None if la is None else _mm_raw(ct, b, "nn", la)
            db = None if lb is None else _mm_raw(ct, a, "tn", "R3" if lb == "L3" else lb)
        else:
            da = None if la is None else _mm_raw(b, ct, "nt", "L3" if la == "R3" else la)
            db = None if lb is None else _mm_raw(a, ct, "nn", lb)
        return (jnp.zeros_like(a) if da is None else da, jnp.zeros_like(b) if db is None else db)

    f.defvjp(fwd, bwd)
    return f(a, b)


def _seg_ones(n):
    r = lax.broadcasted_iota(jnp.int32, (n, n), 0) // HEAD_DIM
    c = lax.broadcasted_iota(jnp.int32, (n, n), 1) // HEAD_DIM
    return (r == c).astype(F32)


def _segsum(x, seg):
    return _mm(x, seg, "nn", "R3")


def _rms_fwd(x, g):
    rstd = lax.rsqrt(jnp.mean(x * x, axis=-1, keepdims=True) + NORM_EPS)
    return x * rstd * g


def _rms_bwd(dy, x, g):
    rstd = lax.rsqrt(jnp.mean(x * x, axis=-1, keepdims=True) + NORM_EPS)
    xn = x * rstd
    dxn = dy * g
    dx = rstd * (dxn - xn * jnp.mean(dxn * xn, axis=-1, keepdims=True))
    return dx, dy * xn


def _sigmoid(x):
    return 1.0 / (1.0 + jnp.exp(-x))


def _softplus(x):
    return jnp.maximum(x, 0.0) + jnp.log(1.0 + jnp.exp(-jnp.abs(x)))


def _acc(ref, val, first):
    @pl.when(first)
    def _():
        ref[...] = val

    @pl.when(jnp.logical_not(first))
    def _():
        ref[...] += val


def _colsum8(v):
    rows, n = v.shape
    return jnp.sum(v.reshape(rows // 8, 8, n), axis=0)


def _prep_fn(p, pprev, mu, w0, w2p, a0, a2p, g2, k_k, k_a):
    seg = _seg_ones(RW)
    ps = p + (pprev - p) * mu
    r = ps[:, 0:RW]
    k = ps[:, RW:2 * RW]
    v = ps[:, 2 * RW:3 * RW]
    xwa = ps[:, 3 * RW:3 * RW + 128]
    xg = ps[:, 3 * RW + 128:3 * RW + 256]
    wraw = -_softplus(-(w0 + _mm(jnp.tanh(xwa), w2p, "nn", 3))) - 0.5
    lw = -jnp.exp(wraw)
    a = _sigmoid(a0 + _mm(xwa, a2p, "nn", 3))
    g = _mm(_sigmoid(xg), g2, "nn", 3)
    kk = k * k_k
    kk = kk / jnp.maximum(jnp.sqrt(_segsum(kk * kk, seg)), 1e-12)
    k2 = k * (1.0 + (a - 1.0) * k_a)
    return r, lw, k2, v, kk, a, g


def _transposed(z):
    return jnp.stack([z[i].T for i in range(z.shape[0])], axis=0) if z.ndim == 3 else z.T


def _solve_unit_lower(lmat, rhs):
    c = lmat.shape[-1]
    row = lax.broadcasted_iota(jnp.int32, (c, c), 0)
    col = lax.broadcasted_iota(jnp.int32, (c, c), 1)
    eye = (row == col).astype(F32)
    ld = jnp.where(row // SUB == col // SUB, lmat, 0.0)
    lo = lmat - ld
    x = eye + ld
    m = ld
    mm = lambda p, q: _mm(p, q, "nn", WKV_PASSES)
    cat = jnp.concatenate
    m = mm(m, m)
    for _ in range(2):
        mx = mm(m, cat([m, x], axis=-1))
        m, x = mx[..., :c], x + mx[..., c:]
    x = x + mm(m, x)
    gw = mm(x, cat([lo, rhs], axis=-1))
    g, w = gw[..., :c], gw[..., c:]
    gg = mm(g, cat([g, w], axis=-1))
    w = w + gg[..., c:]
    return w + mm(gg[..., :c], w)


def _wkv_chunk_fn(s0, r, lw, k, v, kk, a):
    c = r.shape[-2]
    n = 2 * c
    row = lax.broadcasted_iota(jnp.int32, (n, n), 0)
    col = lax.broadcasted_iota(jnp.int32, (n, n), 1)
    same = (row // c) == (col // c)
    incl = jnp.logical_and(row >= col, same)
    strict = jnp.logical_and(row > col, same)
    sel = (lax.broadcasted_iota(jnp.int32, (n, 128), 0) // c) == (lax.broadcasted_iota(jnp.int32, (n, 128), 1) // HEAD_DIM)
    two = lambda z: jnp.concatenate([z, z], axis=-2)
    lw2 = two(lw)
    mm = lambda p_, q_, form: _mm(p_, q_, form, WKV_PASSES)
    cl = _mm(incl.astype(F32), lw2, "nn", "L3")
    p = jnp.exp(cl)
    pinv = jnp.exp(-cl)
    pprev = jnp.exp(cl - lw2)
    kk2 = two(kk)
    at = jnp.where(sel, -kk2 * pprev, 0.0)
    bt = jnp.where(sel, kk2 * two(a) * pinv, 0.0)
    kt = jnp.where(sel, two(k) * pinv, 0.0)
    rt = jnp.where(sel, two(r) * p, 0.0)
    vt = jnp.where(sel, two(v), 0.0)
    cat = jnp.concatenate
    bk = cat([bt, kt], axis=-2)
    arbk = mm(cat([at, rt], axis=-2), bk, "nt")
    ab, ak = jnp.where(strict, arbk[..., :n, :n], 0.0), jnp.where(strict, arbk[..., :n, n:], 0.0)
    rb, rk = jnp.where(incl, arbk[..., n:, :n], 0.0), jnp.where(incl, arbk[..., n:, n:], 0.0)
    s0t = _transposed(s0)
    u = _solve_unit_lower(ab, mm(cat([at, ak], axis=-1), cat([s0t, vt], axis=-2), "nn"))
    y2 = mm(cat([rt, rb, rk], axis=-1), cat([s0t, u, vt], axis=-2), "nn")
    plast = jnp.exp(jnp.sum(lw, axis=-2, keepdims=True))
    s1 = (s0 + mm(cat([u, vt], axis=-2), bk, "tn")) * plast
    r2 = lax.broadcasted_iota(jnp.int32, (128, 128), 0) // HEAD_DIM
    c2 = lax.broadcasted_iota(jnp.int32, (128, 128), 1) // HEAD_DIM
    return y2[..., :c, :] + y2[..., c:, :], jnp.where(r2 == c2, s1, 0.0)


def _post_fn(y, r, k2, v, g, lnw, lnb, rk):
    seg = _seg_ones(RW)
    mean = _segsum(y, seg) * (1.0 / HEAD_DIM)
    yc = y - mean
    var = _segsum(yc * yc, seg) * (1.0 / HEAD_DIM)
    yn = yc * lax.rsqrt(var + GN_EPS)
    out = yn * lnw + lnb + _segsum(r * k2 * rk, seg) * v
    return out * g


def _attn_block_fn(q, kc, vc, kp=None, vp=None):
    n = ATTN_BLOCK
    qi = lax.broadcasted_iota(jnp.int32, (n, n), 0)
    kj = lax.broadcasted_iota(jnp.int32, (n, n), 1)
    lane = lax.broadcasted_iota(jnp.int32, (1, 128), 1)
    scale = HEAD_DIM ** -0.5
    valid = kj <= qi
    keys, vals = kc, vc
    if kp is not None:
        valid = jnp.concatenate([valid, kj >= qi], axis=-1)
        keys, vals = jnp.concatenate([kc, kp], axis=-2), jnp.concatenate([vc, vp], axis=-2)
    m0 = (lane // HEAD_DIM) == 0
    q2 = jnp.concatenate([jnp.where(m0, q, 0.0), jnp.where(m0, 0.0, q)], axis=-2)
    valid2 = jnp.concatenate([valid, valid], axis=-2)
    s = jnp.where(valid2, _mm(q2, keys, "nt", ATTN_PASSES) * scale, NEG)
    m = jnp.max(s, axis=-1, keepdims=True)
    p = jnp.exp(s - m)
    den = jnp.sum(p, axis=-1, keepdims=True)
    o2 = _mm(p, vals, "nn", ATTN_PASSES) / den
    l2 = m + jnp.log(den)
    return jnp.where(m0, o2[..., :n, :], o2[..., n:, :]), jnp.where(m0, l2[..., :n, :], l2[..., n:, :])


def _attn_block_bwd(q, kc, vc, kp, vp, o, lse, do, dl):
    n = ATTN_BLOCK
    cat = jnp.concatenate
    qi = lax.broadcasted_iota(jnp.int32, (n, n), 0)
    kj = lax.broadcasted_iota(jnp.int32, (n, n), 1)
    m0 = (lax.broadcasted_iota(jnp.int32, (1, 128), 1) // HEAD_DIM) == 0
    scale = HEAD_DIM ** -0.5
    valid = kj <= qi
    keys, vals = kc, vc
    if kp is not None:
        valid = cat([valid, kj >= qi], axis=-1)
        keys, vals = cat([kc, kp], axis=-2), cat([vc, vp], axis=-2)
    stack = lambda z: cat([jnp.where(m0, z, 0.0), jnp.where(m0, 0.0, z)], axis=-2)
    q2, do2 = stack(q), stack(do)
    lse2 = cat([jnp.max(jnp.where(m0, lse, NEG), axis=-1, keepdims=True),
                jnp.max(jnp.where(m0, NEG, lse), axis=-1, keepdims=True)], axis=-2)
    delta = jnp.sum(do2 * cat([o, o], axis=-2), axis=-1, keepdims=True)
    dlse = jnp.sum(stack(dl), axis=-1, keepdims=True)
    mm = lambda a, b, form: _mm_raw(a, b, form, ATTN_PASSES)
    s = jnp.where(cat([valid, valid], axis=-2), mm(q2, keys, "nt") * scale, NEG)
    p = jnp.exp(s - lse2)
    ds = p * (mm(do2, vals, "nt") - delta + dlse)
    dq2 = mm(ds, keys, "nn") * scale
    dq = jnp.where(m0, dq2[..., :n, :], dq2[..., n:, :])
    dkeys = mm(ds, q2, "tn") * scale
    dvals = mm(p, do2, "tn")
    if kp is None:
        return dq, dkeys, dvals
    return dq, dkeys[..., :n, :], dvals[..., :n, :], dkeys[..., n:, :], dvals[..., n:, :]


def _combine_fn(o1, o2, o3, l1, l2, l3, og):
    seg = _seg_ones(o1.shape[-1])
    m = jnp.maximum(jnp.maximum(l1, l2), l3)
    e1, e2, e3 = jnp.exp(l1 - m), jnp.exp(l2 - m), jnp.exp(l3 - m)
    o = (e1 * o1 + e2 * o2 + e3 * o3) / (e1 + e2 + e3)
    o = o * lax.rsqrt(_segsum(o * o, seg) * (1.0 / HEAD_DIM) + NORM_EPS)
    return o * og


def _in_proj(x, g1, win):
    t = x.shape[0]
    tm = 512

    def body(x_ref, g_ref, w_ref, h_ref, pa_ref, qkv_ref):
        h = _rms_fwd(x_ref[...], g_ref[...]).astype(BF16)
        h_ref[...] = h
        proj = _dot_nt(h, w_ref[...])
        pa_ref[...] = proj[:, :SHIFT_COLS]
        for j in range(3):
            for p in range(N_PAIR):
                c0 = SHIFT_COLS + j * RW + p * 128
                qkv_ref[j, p] = proj[:, c0:c0 + 128]

    return pl.pallas_call(
        body, name="in_proj", grid=(t // tm,),
        in_specs=[pl.BlockSpec((tm, D_MODEL), lambda i: (i, 0)), pl.BlockSpec((1, D_MODEL), lambda i: (0, 0)),
                  pl.BlockSpec((IN_COLS, D_MODEL), lambda i: (0, 0))],
        out_specs=[pl.BlockSpec((tm, D_MODEL), lambda i: (i, 0)), pl.BlockSpec((tm, SHIFT_COLS), lambda i: (i, 0)),
                   pl.BlockSpec((3, N_PAIR, tm, 128), lambda i: (0, 0, i, 0))],
        out_shape=[jax.ShapeDtypeStruct((t, D_MODEL), BF16), jax.ShapeDtypeStruct((t, SHIFT_COLS), F32),
                   jax.ShapeDtypeStruct((3, N_PAIR, t, 128), F32)],
        compiler_params=_params(("parallel",)),
    )(x, g1, win)


def _shifted(p, last8, first):
    prow = jnp.where(first, 0.0, last8[7:8, :])
    rolled = pltpu.roll(p, 1, axis=0)
    rid = lax.broadcasted_iota(jnp.int32, p.shape, 0)
    return jnp.where(rid == 0, prow, rolled)


_PREP_TM = 256


def _prep_specs(tm):
    vec = lambda n: pl.BlockSpec((1, n), lambda i: (0, 0))
    mat = lambda r, n: pl.BlockSpec((r, n), lambda i: (0, 0))
    return [vec(SHIFT_COLS), vec(RW), mat(128, RW), vec(RW), mat(128, RW), mat(128, RW), vec(RW), vec(RW)]


def _prep_fwd(proj, pw):
    t = proj.shape[0]
    tm = _PREP_TM

    def body(p_ref, l8_ref, mu, w0, w2p, a0, a2p, g2, k_k, k_a, *outs):
        p = p_ref[...]
        pprev = _shifted(p, l8_ref[...], pl.program_id(0) == 0)
        res = _prep_fn(p, pprev, mu[...], w0[...], w2p[...], a0[...], a2p[...], g2[...], k_k[...], k_a[...])
        for o_ref, val in zip(outs, res):
            o_ref[...] = val

    row = pl.BlockSpec((tm, RW), lambda i: (i, 0))
    return pl.pallas_call(
        body, name="rwkv_prep", grid=(t // tm,),
        in_specs=[pl.BlockSpec((tm, SHIFT_COLS), lambda i: (i, 0)),
                  pl.BlockSpec((8, SHIFT_COLS), lambda i: (jnp.maximum(i * (tm // 8) - 1, 0), 0))] + _prep_specs(tm),
        out_specs=[row] * 7,
        out_shape=[jax.ShapeDtypeStruct((t, RW), F32)] * 7,
        compiler_params=_params(("parallel",)),
    )(proj, proj, *pw)


def _pairs(ref):
    return jnp.stack([ref[:, 128 * p:128 * (p + 1)] for p in range(N_PAIR)], axis=0)


def _wkv_fwd(r, lw, k2, v, kk, a):
    t = r.shape[0]
    nc = t // CHUNK

    def body(r_ref, lw_ref, k_ref, v_ref, kk_ref, a_ref, y_ref, s_ref, st):
        @pl.when(pl.program_id(0) == 0)
        def _():
            st[...] = jnp.zeros_like(st)

        s0 = st[...]
        s_ref[0] = s0
        y, s1 = _wkv_chunk_fn(s0, *[_pairs(ref) for ref in (r_ref, lw_ref, k_ref, v_ref, kk_ref, a_ref)])
        for p in range(N_PAIR):
            y_ref[:, 128 * p:128 * (p + 1)] = y[p]
        st[...] = s1

    blk = pl.BlockSpec((CHUNK, RW), lambda c: (c, 0))
    return pl.pallas_call(
        body, name="wkv_fwd", grid=(nc,),
        in_specs=[blk] * 6,
        out_specs=[blk, pl.BlockSpec((1, N_PAIR, 128, 128), lambda c: (c, 0, 0, 0))],
        out_shape=[jax.ShapeDtypeStruct((t, RW), F32), jax.ShapeDtypeStruct((nc, N_PAIR, 128, 128), F32)],
        scratch_shapes=[pltpu.VMEM((N_PAIR, 128, 128), F32)],
        compiler_params=_params(("arbitrary",)),
    )(r, lw, k2, v, kk, a)


_POST_TM = 256


ATTN_GROUP = 2


def _dilated_rows(d, r, n):
    if d == 1:
        return pl.ds(pl.multiple_of(n * ATTN_BLOCK, ATTN_BLOCK), ATTN_BLOCK)
    return pl.ds(r + n * (ATTN_BLOCK * d), ATTN_BLOCK, stride=d)


def _for_each_sequence(t, unit):
    for di, d in enumerate(DILATIONS):

        @pl.when(pl.program_id(1) == di)
        def _(di=di, d=d):
            nb = t // (ATTN_BLOCK * d)
            if d == 1:
                unit(di, [(d, 0, 0)], False)
                unit(di, [(d, 0, 1)], True)
                lax.fori_loop(1, nb // 2, lambda k, c: (unit(di, [(d, 0, 2 * k), (d, 0, 2 * k + 1)], True), c)[1], 0)
            else:

                def residues(r, carry):
                    unit(di, [(d, r, 0), (d, r + d // 2, 0)], False)
                    if nb > 1:
                        lax.fori_loop(1, nb, lambda n, c: (unit(di, [(d, r, n), (d, r + d // 2, n)], True), c)[1], 0)
                    return carry

                lax.fori_loop(0, d // 2, residues, 0)


def _take(ref, lead, rows_list):
    return jnp.stack([ref.at[(*lead, g)][rows, :] for rows in rows_list for g in range(ATTN_GROUP)], axis=0)


def _put(ref, lead, rows_list, val, add=False):
    k = 0
    for rows in rows_list:
        for g in range(ATTN_GROUP):
            if add:
                ref.at[(*lead, g)][rows, :] += val[k]
            else:
                ref.at[(*lead, g)][rows, :] = val[k]
            k += 1


def _attn_fwd(qkv):
    t = qkv.shape[2]

    def body(q_ref, k_ref, v_ref, o_ref, l_ref):
        def unit(di, places, has_prev):
            cur = [_dilated_rows(d, r, n) for d, r, n in places]
            args = [_take(ref, (0,), cur) for ref in (q_ref, k_ref, v_ref)]
            if has_prev:
                prv = [_dilated_rows(d, r, n - 1) for d, r, n in places]
                args += [_take(ref, (0,), prv) for ref in (k_ref, v_ref)]
            o, lse = _attn_block_fn(*args)
            _put(o_ref, (0,), cur, o)
            _put(l_ref, (0,), cur, lse)

        _for_each_sequence(t, unit)

    spec = lambda j: pl.BlockSpec((1, ATTN_GROUP, t, 128), lambda i, b: (j, i, 0, 0))
    out = pl.BlockSpec((1, ATTN_GROUP, t, 128), lambda i, b: (b, i, 0, 0))
    return pl.pallas_call(
        body, name="attn_fwd", grid=(N_PAIR // ATTN_GROUP, len(DILATIONS)),
        in_specs=[spec(0), spec(1), spec(2)], out_specs=[out, out],
        out_shape=[jax.ShapeDtypeStruct((3, N_PAIR, t, 128), F32)] * 2,
        compiler_params=_params(("parallel", "arbitrary")),
    )(qkv, qkv, qkv)


_COMB_TM = 256


def _mixers_out(y, r, k2, v, g, lnw, lnb, rk, o, l, og):
    t = y.shape[0]
    tm = _COMB_TM

    def body(y_ref, r_ref, k_ref, v_ref, g_ref, lnw_ref, lnb_ref, rk_ref, o_ref, l_ref, og_ref, out_ref):
        out_ref[:, :RW] = _post_fn(y_ref[...], r_ref[...], k_ref[...], v_ref[...], g_ref[...],
                                   lnw_ref[...], lnb_ref[...], rk_ref[...]).astype(BF16)
        for p in range(N_PAIR):
            cols = slice(128 * p, 128 * (p + 1))
            out_ref[:, RW + 128 * p:RW + 128 * (p + 1)] = _combine_fn(
                o_ref[0, p], o_ref[1, p], o_ref[2, p], l_ref[0, p], l_ref[1, p], l_ref[2, p], og_ref[:, cols]).astype(BF16)

    row = pl.BlockSpec((tm, RW), lambda i: (i, 0))
    vec = pl.BlockSpec((1, RW), lambda i: (0, 0))
    blk = pl.BlockSpec((3, N_PAIR, tm, 128), lambda i: (0, 0, i, 0))
    return pl.pallas_call(
        body, name="mixers_out", grid=(t // tm,),
        in_specs=[row] * 5 + [vec] * 3 + [blk, blk, vec], out_specs=pl.BlockSpec((tm, D_MODEL), lambda i: (i, 0)),
        out_shape=jax.ShapeDtypeStruct((t, D_MODEL), BF16),
        compiler_params=_params(("parallel",)),
    )(y, r, k2, v, g, lnw, lnb, rk, o, l, og)


def _ffn_all(x, ycat, wg, wu, wd, wout, g2, gf, tgt):
    t = x.shape[0]
    tm = 256

    def body(x_ref, y_ref, wg_ref, wu_ref, wd_ref, wo_ref, g2_ref, gf_ref, t_ref,
             h_ref, act_ref, dx2b_ref, dgt_ref, dup_ref, dx1b_ref, dx1_ref, dya_ref, dyb_ref, loss_ref, dgf_ref, dg2_ref,
             gt_s, up_s):
        first = pl.program_id(0) == 0
        x1 = x_ref[...] + _dot(y_ref[...], wo_ref[...])
        h = _rms_fwd(x1, g2_ref[...]).astype(BF16)
        h_ref[...] = h
        for c0 in range(0, D_FF, FF_CHUNK):
            cols = slice(c0, c0 + FF_CHUNK)
            gt = _dot_nt(h, wg_ref[cols, :])
            up = _dot_nt(h, wu_ref[cols, :])
            gt_s[:, cols] = gt.astype(BF16)
            up_s[:, cols] = up.astype(BF16)
            act_ref[:, cols] = (gt * _sigmoid(gt) * up).astype(BF16)
        x2 = x1 + _dot(act_ref[...], wd_ref[...])
        gf_ = gf_ref[...]
        diff = _rms_fwd(x2, gf_) - t_ref[...]
        lrow = 0.5 * jnp.sum(_colsum8(diff * diff), axis=1, keepdims=True) * (1.0 / D_MODEL)
        _acc(loss_ref, jnp.broadcast_to(lrow, (8, 128)), first)
        dx2, dgr = _rms_bwd(diff * (1.0 / D_MODEL), x2, gf_)
        _acc(dgf_ref, _colsum8(dgr), first)
        dx2b = dx2.astype(BF16)
        dx2b_ref[...] = dx2b
        for c0 in range(0, D_FF, FF_CHUNK):
            cols = slice(c0, c0 + FF_CHUNK)
            dact = _dot_nt(dx2b, wd_ref[cols, :])
            gt = gt_s[:, cols].astype(F32)
            sg = _sigmoid(gt)
            dgt_ref[:, cols] = (dact * up_s[:, cols].astype(F32) * sg * (1.0 + gt * (1.0 - sg))).astype(BF16)
            dup_ref[:, cols] = (dact * gt * sg).astype(BF16)
        dh = _dot(dgt_ref[...], wg_ref[...]) + _dot(dup_ref[...], wu_ref[...])
        dxn, dgr2 = _rms_bwd(dh, x1, g2_ref[...])
        _acc(dg2_ref, _colsum8(dgr2), first)
        dx1 = dx2 + dxn
        dx1_ref[...] = dx1
        dx1b = dx1.astype(BF16)
        dx1b_ref[...] = dx1b
        dy = _dot_nt(dx1b, wo_ref[...])
        dya_ref[...] = dy[:, :RW]
        dyb_ref[...] = dy[:, RW:]

    row = pl.BlockSpec((tm, D_MODEL), lambda i: (i, 0))
    wide = pl.BlockSpec((tm, D_FF), lambda i: (i, 0))
    half = pl.BlockSpec((tm, RW), lambda i: (i, 0))
    wsp = pl.BlockSpec((D_FF, D_MODEL), lambda i: (0, 0))
    vec = pl.BlockSpec((1, D_MODEL), lambda i: (0, 0))
    part = pl.BlockSpec((8, D_MODEL), lambda i: (0, 0))
    bf = lambda n: jax.ShapeDtypeStruct((t, n), BF16)
    return pl.pallas_call(
        body, name="ffn_all", grid=(t // tm,),
        in_specs=[row, row, wsp, wsp, wsp, pl.BlockSpec((D_MODEL, D_MODEL), lambda i: (0, 0)), vec, vec, row],
        out_specs=[row, wide, row, wide, wide, row, row, half, half, pl.BlockSpec((8, 128), lambda i: (0, 0)), part, part],
        out_shape=[bf(D_MODEL), bf(D_FF), bf(D_MODEL), bf(D_FF), bf(D_FF), bf(D_MODEL),
                   jax.ShapeDtypeStruct((t, D_MODEL), F32), jax.ShapeDtypeStruct((t, RW), F32),
                   jax.ShapeDtypeStruct((t, RW), F32), jax.ShapeDtypeStruct((8, 128), F32),
                   jax.ShapeDtypeStruct((8, D_MODEL), F32), jax.ShapeDtypeStruct((8, D_MODEL), F32)],
        scratch_shapes=[pltpu.VMEM((tm, D_FF), BF16), pltpu.VMEM((tm, D_FF), BF16)],
        compiler_params=_params(("arbitrary",)),
    )(x, ycat, wg, wu, wd, wout, g2, gf, tgt)


def _wgrad(a, b, tk, tn, name):
    t, kdim = a.shape
    ndim = b.shape[1]

    def body(a_ref, b_ref, o_ref):
        o_ref[...] = _dot_tn(a_ref[...], b_ref[...])

    return pl.pallas_call(
        body, name=name, grid=(kdim // tk, ndim // tn),
        in_specs=[pl.BlockSpec((t, tk), lambda i, j: (0, i)), pl.BlockSpec((t, tn), lambda i, j: (0, j))],
        out_specs=pl.BlockSpec((tk, tn), lambda i, j: (i, j)),
        out_shape=jax.ShapeDtypeStruct((kdim, ndim), F32),
        compiler_params=_params(("parallel", "parallel")),
    )(a, b)


def _post_bwd(dya, y, r, k2, v, g, lnw, lnb, rk):
    t = y.shape[0]
    tm = _POST_TM

    def body(d_ref, y_ref, r_ref, k_ref, v_ref, g_ref, lnw_ref, lnb_ref, rk_ref,
             dy_ref, dr_ref, dk_ref, dv_ref, dg_ref, dlnw_ref, dlnb_ref, drk_ref):
        first = pl.program_id(0) == 0
        ones = jnp.ones((tm, 1), F32)
        prim = (y_ref[...], r_ref[...], k_ref[...], v_ref[...], g_ref[...],
                ones * lnw_ref[...], ones * lnb_ref[...], ones * rk_ref[...])
        _, vjp = jax.vjp(_post_fn, *prim)
        dy, dr, dk, dv, dg, dlnw, dlnb, drk = vjp(d_ref[...])
        dy_ref[...] = dy
        dr_ref[...] = dr
        dk_ref[...] = dk
        dv_ref[...] = dv
        dg_ref[...] = dg
        _acc(dlnw_ref, _colsum8(dlnw), first)
        _acc(dlnb_ref, _colsum8(dlnb), first)
        _acc(drk_ref, _colsum8(drk), first)

    row = pl.BlockSpec((tm, RW), lambda i: (i, 0))
    vec = pl.BlockSpec((1, RW), lambda i: (0, 0))
    part = pl.BlockSpec((8, RW), lambda i: (0, 0))
    return pl.pallas_call(
        body, name="rwkv_post_bwd", grid=(t // tm,),
        in_specs=[row] * 6 + [vec] * 3, out_specs=[row] * 5 + [part] * 3,
        out_shape=[jax.ShapeDtypeStruct((t, RW), F32)] * 5 + [jax.ShapeDtypeStruct((8, RW), F32)] * 3,
        compiler_params=_params(("arbitrary",)),
    )(dya, y, r, k2, v, g, lnw, lnb, rk)


def _wkv_bwd(dy, s0s, r, lw, k2, v, kk, a):
    t = r.shape[0]
    nc = t // CHUNK

    def body(dy_ref, s_ref, r_ref, lw_ref, k_ref, v_ref, kk_ref, a_ref,
             dr_ref, dlw_ref, dk_ref, dv_ref, dkk_ref, da_ref, ds):
        @pl.when(pl.program_id(0) == 0)
        def _():
            ds[...] = jnp.zeros_like(ds)

        _, vjp = jax.vjp(_wkv_chunk_fn, s_ref[0],
                         *[_pairs(ref) for ref in (r_ref, lw_ref, k_ref, v_ref, kk_ref, a_ref)])
        res = vjp((_pairs(dy_ref), ds[...]))
        ds[...] = res[0]
        for ref, val in zip((dr_ref, dlw_ref, dk_ref, dv_ref, dkk_ref, da_ref), res[1:]):
            for p in range(N_PAIR):
                ref[:, 128 * p:128 * (p + 1)] = val[p]

    blk = pl.BlockSpec((CHUNK, RW), lambda c: (nc - 1 - c, 0))
    return pl.pallas_call(
        body, name="wkv_bwd", grid=(nc,),
        in_specs=[blk, pl.BlockSpec((1, N_PAIR, 128, 128), lambda c: (nc - 1 - c, 0, 0, 0))] + [blk] * 6,
        out_specs=[blk] * 6,
        out_shape=[jax.ShapeDtypeStruct((t, RW), F32)] * 6,
        scratch_shapes=[pltpu.VMEM((N_PAIR, 128, 128), F32)],
        compiler_params=_params(("arbitrary",)),
    )(dy, s0s, r, lw, k2, v, kk, a)


def _prep_bwd(proj, pw, douts):
    t = proj.shape[0]
    tm = _PREP_TM
    nt = t // tm

    def body(p_ref, l8_ref, mu, w0, w2p, a0, a2p, g2, k_k, k_a, dr, dr2, dlw, dk2, dk22, dv, dv2, dkk, da, dg,
             dp_ref, dmu_ref, dw0_ref, dw2_ref, da0_ref, da2_ref, dg2_ref, dkk_ref, dka_ref, carry):
        i = pl.program_id(0)
        first = i == 0

        @pl.when(first)
        def _():
            carry[...] = jnp.zeros_like(carry)

        p = p_ref[...]
        pprev = _shifted(p, l8_ref[...], i == nt - 1)
        ones = jnp.ones((tm, 1), F32)
        prim = (p, pprev, ones * mu[...], ones * w0[...], w2p[...], ones * a0[...], a2p[...], g2[...],
                ones * k_k[...], ones * k_a[...])
        _, vjp = jax.vjp(_prep_fn, *prim)
        dp, dpp, dmu, dw0, dw2, da0, da2, dg2, dkk_, dka = vjp(
            (dr[...] + dr2[...], dlw[...], dk2[...] + dk22[...], dv[...] + dv2[...], dkk[...], da[...], dg[...]))
        up = pltpu.roll(dpp, tm - 1, axis=0)
        rid = lax.broadcasted_iota(jnp.int32, dpp.shape, 0)
        dp_ref[...] = dp + jnp.where(rid == tm - 1, carry[0:1, :], up)
        carry[...] = jnp.broadcast_to(dpp[0:1, :], carry.shape)
        _acc(dmu_ref, _colsum8(dmu), first)
        _acc(dw0_ref, _colsum8(dw0), first)
        _acc(dw2_ref, dw2, first)
        _acc(da0_ref, _colsum8(da0), first)
        _acc(da2_ref, da2, first)
        _acc(dg2_ref, dg2, first)
        _acc(dkk_ref, _colsum8(dkk_), first)
        _acc(dka_ref, _colsum8(dka), first)

    rev = lambda i: (nt - 1 - i, 0)
    row = pl.BlockSpec((tm, RW), rev)
    part = lambda n: pl.BlockSpec((8, n), lambda i: (0, 0))
    mat = pl.BlockSpec((128, RW), lambda i: (0, 0))
    return pl.pallas_call(
        body, name="rwkv_prep_bwd", grid=(nt,),
        in_specs=[pl.BlockSpec((tm, SHIFT_COLS), rev),
                  pl.BlockSpec((8, SHIFT_COLS), lambda i: (jnp.maximum((nt - 1 - i) * (tm // 8) - 1, 0), 0))]
                 + _prep_specs(tm) + [row] * 10,
        out_specs=[pl.BlockSpec((tm, SHIFT_COLS), rev), part(SHIFT_COLS), part(RW), mat, part(RW), mat, mat,
                   part(RW), part(RW)],
        out_shape=[jax.ShapeDtypeStruct((t, SHIFT_COLS), F32), jax.ShapeDtypeStruct((8, SHIFT_COLS), F32),
                   jax.ShapeDtypeStruct((8, RW), F32), jax.ShapeDtypeStruct((128, RW), F32),
                   jax.ShapeDtypeStruct((8, RW), F32), jax.ShapeDtypeStruct((128, RW), F32),
                   jax.ShapeDtypeStruct((128, RW), F32), jax.ShapeDtypeStruct((8, RW), F32),
                   jax.ShapeDtypeStruct((8, RW), F32)],
        scratch_shapes=[pltpu.VMEM((8, SHIFT_COLS), F32)],
        compiler_params=_params(("arbitrary",)),
    )(proj, proj, *pw, *douts)


def _combine_bwd(dyb, o, l, og):
    t = dyb.shape[0]
    tm = _COMB_TM

    def body(d_ref, o_ref, l_ref, og_ref, do_ref, dl_ref, dog_ref):
        ones = jnp.ones((tm, 1), F32)
        dog = []
        for p in range(N_PAIR):
            cols = slice(128 * p, 128 * (p + 1))
            _, vjp = jax.vjp(_combine_fn, o_ref[0, p], o_ref[1, p], o_ref[2, p], l_ref[0, p], l_ref[1, p], l_ref[2, p],
                             ones * og_ref[:, cols])
            res = vjp(d_ref[:, cols])
            for b in range(3):
                do_ref[b, p] = res[b]
                dl_ref[b, p] = res[3 + b]
            dog.append(_colsum8(res[6]))
        _acc(dog_ref, jnp.concatenate(dog, axis=1), pl.program_id(0) == 0)

    blk = pl.BlockSpec((3, N_PAIR, tm, 128), lambda i: (0, 0, i, 0))
    return pl.pallas_call(
        body, name="attn_combine_bwd", grid=(t // tm,),
        in_specs=[pl.BlockSpec((tm, RW), lambda i: (i, 0)), blk, blk, pl.BlockSpec((1, RW), lambda i: (0, 0))],
        out_specs=[blk, blk, pl.BlockSpec((8, RW), lambda i: (0, 0))],
        out_shape=[jax.ShapeDtypeStruct((3, N_PAIR, t, 128), F32)] * 2 + [jax.ShapeDtypeStruct((8, RW), F32)],
        compiler_params=_params(("arbitrary",)),
    )(dyb, o, l, og)


def _attn_bwd(do, dl, o, lse, qkv):
    t = qkv.shape[2]

    def body(do_ref, dl_ref, o_ref, l_ref, q_ref, k_ref, v_ref, dq_ref, dk_ref, dv_ref):
        @pl.when(pl.program_id(1) == 0)
        def _():
            for ref in (dq_ref, dk_ref, dv_ref):
                ref[...] = jnp.zeros_like(ref)

        def unit(di, places, has_prev):
            cur = [_dilated_rows(d, r, n) for d, r, n in places]
            q, kc, vc = [_take(ref, (0,), cur) for ref in (q_ref, k_ref, v_ref)]
            kp = vp = None
            if has_prev:
                prv = [_dilated_rows(d, r, n - 1) for d, r, n in places]
                kp, vp = [_take(ref, (0,), prv) for ref in (k_ref, v_ref)]
            res = _attn_block_bwd(q, kc, vc, kp, vp, *[_take(ref, (0,), cur) for ref in (o_ref, l_ref, do_ref, dl_ref)])
            _put(dq_ref, (), cur, res[0], add=True)
            _put(dk_ref, (), cur, res[1], add=True)
            _put(dv_ref, (), cur, res[2], add=True)
            if has_prev:
                _put(dk_ref, (), prv, res[3], add=True)
                _put(dv_ref, (), prv, res[4], add=True)

        _for_each_sequence(t, unit)

    spec = lambda j: pl.BlockSpec((1, ATTN_GROUP, t, 128), lambda i, b: (j, i, 0, 0))
    branch = pl.BlockSpec((1, ATTN_GROUP, t, 128), lambda i, b: (b, i, 0, 0))
    out = pl.BlockSpec((ATTN_GROUP, t, 128), lambda i, b: (i, 0, 0))
    return pl.pallas_call(
        body, name="attn_bwd", grid=(N_PAIR // ATTN_GROUP, len(DILATIONS)),
        in_specs=[branch] * 4 + [spec(0), spec(1), spec(2)], out_specs=[out] * 3,
        out_shape=[jax.ShapeDtypeStruct((N_PAIR, t, 128), F32)] * 3,
        compiler_params=_params(("parallel", "arbitrary")),
    )(do, dl, o, lse, qkv, qkv, qkv)


def _in_proj_bwd(dpa, dq, dk, dv, win, x, g1, dx1):
    t = x.shape[0]
    tm = 256

    def body(dpa_ref, dq_ref, dk_ref, dv_ref, w_ref, x_ref, g_ref, dx1_ref, dproj_ref, dx_ref, dg_ref):
        parts = [dpa_ref[...]] + [ref[p] for ref in (dq_ref, dk_ref, dv_ref) for p in range(N_PAIR)]
        dproj = jnp.concatenate([z.astype(BF16) for z in parts], axis=1)
        dproj_ref[...] = dproj
        dh = _dot(dproj, w_ref[...])
        dxn, dgr = _rms_bwd(dh, x_ref[...], g_ref[...])
        dx_ref[...] = dx1_ref[...] + dxn
        _acc(dg_ref, _colsum8(dgr), pl.program_id(0) == 0)

    row = pl.BlockSpec((tm, D_MODEL), lambda i: (i, 0))
    pair = pl.BlockSpec((N_PAIR, tm, 128), lambda i: (0, i, 0))
    return pl.pallas_call(
        body, name="in_proj_bwd", grid=(t // tm,),
        in_specs=[pl.BlockSpec((tm, SHIFT_COLS), lambda i: (i, 0))] + [pair] * 3
                 + [pl.BlockSpec((IN_COLS, D_MODEL), lambda i: (0, 0)), row, pl.BlockSpec((1, D_MODEL), lambda i: (0, 0)), row],
        out_specs=[pl.BlockSpec((tm, IN_COLS), lambda i: (i, 0)), row, pl.BlockSpec((8, D_MODEL), lambda i: (0, 0))],
        out_shape=[jax.ShapeDtypeStruct((t, IN_COLS), BF16), jax.ShapeDtypeStruct((t, D_MODEL), F32),
                   jax.ShapeDtypeStruct((8, D_MODEL), F32)],
        compiler_params=_params(("arbitrary",)),
    )(dpa, dq, dk, dv, win, x, g1, dx1)


def _pad_lora(w, lo):
    z = jnp.zeros((64, RW), F32)
    return jnp.concatenate([w, z], axis=0) if lo == 0 else jnp.concatenate([z, w], axis=0)


def _local_step(x, tgt, win, vecs, w2, a2, g2m, get_rest, send_rest):
    pw = (vecs["mu_shift"], vecs["decay_w0"], _pad_lora(w2, 0), vecs["iclr_a0"], _pad_lora(a2, 64), g2m,
          vecs["k_k"], vecs["k_a"])
    h, proj, qkv = _in_proj(x, vecs["mix_norm_g"], win)
    r, lw, k2, v, kk, a, g = _prep_fwd(proj, pw)
    y, s0s = _wkv_fwd(r, lw, k2, v, kk, a)
    o_att, l_att = _attn_fwd(qkv)
    ycat = _mixers_out(y, r, k2, v, g, vecs["ln_x_w"], vecs["ln_x_b"], vecs["r_k"], o_att, l_att, vecs["attn_out_g"])
    wout, wg, wu, wd = get_rest(ycat)
    h2, act, dx2b, dgt, dup, dx1b, dx1, dya, dyb, loss8, dgf, dg2n = _ffn_all(
        x, ycat, wg, wu, wd, wout, vecs["ffn_norm_g"], vecs["final_norm_g"], tgt)
    gw = {
        "w_down": _wgrad(act, dx2b, 1408, 1024, "wgrad_down"),
        "w_gate": _wgrad(dgt, h2, 1408, 1024, "wgrad_gate"),
        "w_up": _wgrad(dup, h2, 1408, 1024, "wgrad_up"),
        "w_out": _wgrad(ycat, dx1b, 1024, 1024, "wgrad_out"),
    }

    lnw = vecs["ln_x_w"] + send_rest(gw)[0, 0]
    dy, dr_p, dk2_p, dv_p, dg, dlnw, dlnb, drk = _post_bwd(dya, y, r, k2, v, g, lnw, vecs["ln_x_b"], vecs["r_k"])
    dr_s, dlw, dk2_s, dv_s, dkk, da = _wkv_bwd(dy, s0s, r, lw, k2, v, kk, a)
    dpa, dmu, dw0, dw2p, da0, da2p, dg2m, dk_k, dk_a = _prep_bwd(
        proj, pw, (dr_p, dr_s, dlw, dk2_p, dk2_s, dv_p, dv_s, dkk, da, dg))

    do_att, dl_att, dog = _combine_bwd(dyb, o_att, l_att, vecs["attn_out_g"])
    dq, dk, dv = _attn_bwd(do_att, dl_att, o_att, l_att, qkv)
    dproj, dx, dg1 = _in_proj_bwd(dpa, dq, dk, dv, win, x, vecs["mix_norm_g"], dx1)
    gw["w_in"] = _wgrad(dproj, h, 1664, 1024, "wgrad_in")
    gw["decay_w2"] = dw2p[:64]
    gw["iclr_a2"] = da2p[64:]
    gw["gate_g2"] = dg2m
    gv = {"mix_norm_g": dg1, "mu_shift": dmu, "decay_w0": dw0, "iclr_a0": da0, "k_k": dk_k, "k_a": dk_a, "r_k": drk,
          "ln_x_w": dlnw, "ln_x_b": dlnb, "attn_out_g": dog, "ffn_norm_g": dg2n, "final_norm_g": dgf}
    return loss8, dx, gw, gv


N_CHIP = 4
N_DEV = 8
MATS = ("w_in", "w_out", "w_gate", "w_up", "w_down")
LORAS = ("decay_w2", "iclr_a2", "gate_g2")
VECS = (("mix_norm_g", 1024), ("mu_shift", 1792), ("decay_w0", 512), ("iclr_a0", 512), ("k_k", 512), ("k_a", 512),
        ("r_k", 512), ("ln_x_w", 512), ("ln_x_b", 512), ("attn_out_g", 512), ("ffn_norm_g", 1024),
        ("final_norm_g", 1024))
N_VEC = sum(n for _, n in VECS)
N_SMALL = N_VEC + 128
ANY = pl.BlockSpec(memory_space=pl.ANY)


def _flip(v, f):
    return 1 - v if f else v


class _Me:
    def __init__(self, mode):
        x, y, c = lax.axis_index("x"), lax.axis_index("y"), lax.axis_index("c")
        self.core, self.chip, self.dev = c, 2 * x + y, 4 * x + 2 * y + c
        self.sibling = (x, y, 1 - c)
        if mode == "chips":
            self.peers = [(px, py, c) for px, py in ((1 - x, y), (x, 1 - y), (1 - x, 1 - y))]
        else:
            self.peers = [(_flip(x, k & 4), _flip(y, k & 2), _flip(c, k & 1)) for k in range(1, N_DEV)]


def _half(core, rows):
    h = rows // 2
    return pl.ds(pl.multiple_of(core * h, h), h)


_BY_CHIP = ("gather", "chipsum")


def _peer_copy(srcs, dsts, kinds, send_sems, recv_sems, me, j, i, incoming):
    px, py, pc = me.peers[j]
    pchip, pdev = 2 * px + py, 4 * px + 2 * py + pc
    src, dst, kind = srcs[i], dsts[i], kinds[i]
    if kind == "gather":
        rows = _half(me.core, src.shape[1])
        src, dst = src.at[me.chip, rows], dst.at[pchip if incoming else me.chip, rows]
    elif kind == "scatter":
        src, dst = src.at[pchip, _half(pc, src.shape[1])], dst.at[pdev if incoming else me.dev]
    elif kind == "chipsum":
        src, dst = src.at[pchip], dst.at[pchip if incoming else me.chip]
    else:
        dst = dst.at[pdev if incoming else me.dev]
    n = len(srcs)
    return pltpu.make_async_remote_copy(src_ref=src, dst_ref=dst, send_sem=send_sems.at[n * j + i],
                                        recv_sem=recv_sems.at[n * j + i], device_id=(px, py, pc), device_id_type=MESH)


def _mode(kinds):
    return "chips" if kinds[0] in _BY_CHIP else "devs"


def _npeer(kinds):
    return N_CHIP - 1 if kinds[0] in _BY_CHIP else N_DEV - 1


def _sibling_halves(gs, name):
    n = len(gs)

    def body(*refs):
        srcs, dsts, send_sems, recv_sems = refs[:n], refs[n:2 * n], refs[2 * n], refs[2 * n + 1]
        me = _Me("chips")

        def copy(i, p):
            return pltpu.make_async_remote_copy(
                src_ref=srcs[i].at[p, _half(1 - me.core, srcs[i].shape[1])], dst_ref=dsts[i].at[p],
                send_sem=send_sems.at[N_CHIP * i + p], recv_sem=recv_sems.at[N_CHIP * i + p],
                device_id=me.sibling, device_id_type=MESH)

        copies = [copy(i, p) for i in range(n) for p in range(N_CHIP)]
        for cp in copies:
            cp.start()
        for cp in copies:
            cp.wait()

    return pl.pallas_call(
        body, name=name, in_specs=[ANY] * n, out_specs=[ANY] * n,
        out_shape=[jax.ShapeDtypeStruct((N_CHIP, g.shape[1] // 2, g.shape[2]), g.dtype) for g in gs],
        scratch_shapes=[pltpu.SemaphoreType.DMA((N_CHIP * n,)), pltpu.SemaphoreType.DMA((N_CHIP * n,))],
    )(*gs)


def _add_halves(g, other, core, tr, name):
    _, h, cols = other.shape

    def body(core_ref, g_ref, o_ref, out_ref):
        out_ref[...] = (g_ref[...].astype(F32) + o_ref[...].astype(F32)).astype(BF16)

    blk = lambda off: pl.BlockSpec((1, tr, cols), lambda p, i, core_ref: (p, core_ref[0] * (h // tr) * off + i, 0))
    return pl.pallas_call(
        body, name=name,
        grid_spec=pltpu.PrefetchScalarGridSpec(num_scalar_prefetch=1, grid=(N_CHIP, h // tr),
                                               in_specs=[blk(1), blk(0)], out_specs=blk(0)),
        out_shape=jax.ShapeDtypeStruct(other.shape, BF16),
        compiler_params=_params(("parallel", "parallel")),
    )(core, g, other)


def _swap_gathered(lands, name):
    n = len(lands)

    def body(*refs):
        dsts, send_sems, recv_sems = refs[n:2 * n], refs[2 * n], refs[2 * n + 1]
        me = _Me("chips")

        def copy(j, i, incoming):
            px, py, _ = me.peers[j]
            rows_out, rows_in = _half(me.core, dsts[i].shape[1]), _half(1 - me.core, dsts[i].shape[1])
            return pltpu.make_async_remote_copy(
                src_ref=dsts[i].at[2 * px + py, rows_out], dst_ref=dsts[i].at[2 * px + py, rows_in if incoming else rows_out],
                send_sem=send_sems.at[n * j + i], recv_sem=recv_sems.at[n * j + i], device_id=me.sibling, device_id_type=MESH)

        sends = [copy(j, i, False) for j in range(3) for i in range(n)]
        for cp in sends:
            cp.start()
        for j in range(3):
            for i in range(n):
                copy(j, i, True).wait_recv()
        for cp in sends:
            cp.wait_send()

    return pl.pallas_call(
        body, name=name, in_specs=[ANY] * n, out_specs=[ANY] * n,
        out_shape=[jax.ShapeDtypeStruct(l.shape, l.dtype) for l in lands],
        input_output_aliases={i: i for i in range(n)},
        scratch_shapes=[pltpu.SemaphoreType.DMA((3 * n,)), pltpu.SemaphoreType.DMA((3 * n,))],
    )(*lands)


def _join_halves(sums, name):
    n = len(sums)

    def body(*refs):
        dsts, send_sems, recv_sems = refs[n:2 * n], refs[2 * n], refs[2 * n + 1]
        me = _Me("chips")

        def copy(i, incoming):
            mine, other = _half(me.core, dsts[i].shape[0]), _half(1 - me.core, dsts[i].shape[0])
            return pltpu.make_async_remote_copy(src_ref=dsts[i].at[mine], dst_ref=dsts[i].at[other if incoming else mine],
                                                send_sem=send_sems.at[i], recv_sem=recv_sems.at[i],
                                                device_id=me.sibling, device_id_type=MESH)

        sends = [copy(i, False) for i in range(n)]
        for cp in sends:
            cp.start()
        for i in range(n):
            copy(i, True).wait_recv()
        for cp in sends:
            cp.wait_send()

    return pl.pallas_call(
        body, name=name, in_specs=[ANY] * n, out_specs=[ANY] * n,
        out_shape=[jax.ShapeDtypeStruct(s.shape, s.dtype) for s in sums],
        input_output_aliases={i: i for i in range(n)},
        scratch_shapes=[pltpu.SemaphoreType.DMA((n,)), pltpu.SemaphoreType.DMA((n,))],
    )(*sums)


HBM = pl.BlockSpec(memory_space=pltpu.HBM)
SEM = pl.BlockSpec(memory_space=pltpu.SEMAPHORE)
EFFECT = pltpu.SideEffectType.DATAFLOW_SIDE_EFFECTING


def _swap_start(arrs, lands, kinds, name):
    n = len(lands)
    ops = list(lands) if arrs is None else [*arrs, *lands]
    k = len(ops)

    def body(*refs):
        srcs, dsts, send_sems, recv_sems, token = refs[:n], refs[k - n:k], refs[k], refs[k + 1], refs[-1]
        me = _Me(_mode(kinds))
        for j in range(len(me.peers)):
            for i in range(n):
                _peer_copy(srcs, dsts, kinds, send_sems, recv_sems, me, j, i, False).start()
        token[...] = jnp.zeros_like(token)

    ns = _npeer(kinds) * n
    outs = pl.pallas_call(
        body, name=name,
        out_shape=(pltpu.SemaphoreType.DMA((ns,)), pltpu.SemaphoreType.DMA((ns,)),
                   *[pltpu.HBM(a.shape, a.dtype) for a in ops], jax.ShapeDtypeStruct((8, 128), F32)),
        in_specs=[HBM] * k, out_specs=(SEM, SEM, *[HBM] * k, pl.BlockSpec(memory_space=pltpu.VMEM)),
        input_output_aliases={i: 2 + i for i in range(k)},
        compiler_params=pltpu.CompilerParams(has_side_effects=EFFECT),
    )(*[pltpu.with_memory_space_constraint(a, pltpu.HBM) for a in ops])
    return outs[0], outs[1], outs[2:2 + k - n], outs[2 + k - n:2 + k], outs[-1]


def _swap_wait(send_sems, recv_sems, srcs_thru, lands_thru, after, kinds, name):
    n = len(lands_thru)
    ops = [*srcs_thru, *lands_thru]
    k = len(ops)

    def body(*refs):
        srcs, dsts, s_sems, r_sems = refs[:n], refs[k - n:k], refs[k], refs[k + 1]
        me = _Me(_mode(kinds))
        for j in range(len(me.peers)):
            for i in range(n):
                cp = _peer_copy(srcs, dsts, kinds, s_sems, r_sems, me, j, i, True)
                cp.wait_send()
                cp.wait_recv()

    outs = pl.pallas_call(
        body, name=name,
        out_shape=tuple(pltpu.HBM(a.shape, a.dtype) for a in ops),
        in_specs=[HBM] * k + [SEM, SEM, ANY], out_specs=tuple([HBM] * k),
        input_output_aliases={i: i for i in range(k)},
        compiler_params=pltpu.CompilerParams(has_side_effects=EFFECT),
    )(*ops, send_sems, recv_sems, after)
    return outs[k - n:]


def _adamw(w, g, m, v):
    m = ADAM_B1 * m + (1.0 - ADAM_B1) * g
    v = ADAM_B2 * v + (1.0 - ADAM_B2) * (g * g)
    m_hat = m / (1.0 - ADAM_B1 ** ADAM_STEP)
    v_hat = v / (1.0 - ADAM_B2 ** ADAM_STEP)
    delta = -ADAM_LR * (m_hat / (jnp.sqrt(v_hat) + ADAM_EPS) + ADAM_WD * w)
    return delta, m, v


def _reduce8(rbuf, core, tr, name):
    slots, h, cols = rbuf.shape

    def body(core_ref, r_ref, g_ref):
        g = r_ref[0].astype(F32)
        for s in range(1, slots):
            g = g + r_ref[s].astype(F32)
        g_ref[...] = g

    return pl.pallas_call(
        body, name=name,
        grid_spec=pltpu.PrefetchScalarGridSpec(
            num_scalar_prefetch=1, grid=(h // tr,),
            in_specs=[pl.BlockSpec((slots, tr, cols), lambda i, core_ref: (0, i, 0))],
            out_specs=pl.BlockSpec((tr, cols), lambda i, core_ref: (core_ref[0] * (h // tr) + i, 0))),
        out_shape=jax.ShapeDtypeStruct((2 * h, cols), F32),
        compiler_params=_params(("parallel",)),
    )(core, rbuf)


def _adamw_call(g, w, m, v, tr, name):
    _, rows, cols = w.shape

    def body(g_in, w_ref, m_ref, v_ref, g_ref, d_ref, nm_ref, nv_ref):
        g = g_in[...]
        g_ref[0] = g
        d_ref[0], nm_ref[0], nv_ref[0] = _adamw(w_ref[0], g, m_ref[0], v_ref[0])

    row = pl.BlockSpec((1, tr, cols), lambda i: (0, i, 0))
    return pl.pallas_call(
        body, name=name, grid=(rows // tr,),
        in_specs=[pl.BlockSpec((tr, cols), lambda i: (i, 0)), row, row, row], out_specs=[row] * 4,
        out_shape=[jax.ShapeDtypeStruct(w.shape, F32)] * 4,
        compiler_params=_params(("parallel",)),
    )(g, w, m, v)


def _rowsum_small(parts, loss8):
    def body(*refs):
        out = refs[-1]
        c0 = 0
        for ref in refs[:-1]:
            n = ref.shape[1]
            out[:, c0:c0 + n] = jnp.sum(ref[...], axis=0, keepdims=True)
            c0 += n

    return pl.pallas_call(body, name="rowsum_small", out_shape=jax.ShapeDtypeStruct((1, N_SMALL), F32))(*parts, loss8)


def _reduce_adamw_small(sbuf, ws, ms, vs):
    nv = len(ws)

    def body(*refs):
        s_ref, ins, outs = refs[0], refs[1:1 + 3 * nv], refs[1 + 3 * nv:]
        tot = s_ref[0]
        for s in range(1, N_DEV):
            tot = tot + s_ref[s]
        c0 = 0
        for i in range(nv):
            n = ins[i].shape[1]
            g = tot[:, c0:c0 + n]
            outs[i][...] = g
            outs[nv + i][...], outs[2 * nv + i][...], outs[3 * nv + i][...] = _adamw(
                ins[i][...], g, ins[nv + i][...], ins[2 * nv + i][...])
            c0 += n
        outs[-1][...] = tot[:, c0:]

    return pl.pallas_call(
        body, name="reduce_adamw_small",
        out_shape=[jax.ShapeDtypeStruct(a.shape, F32) for a in ws] * 4 + [jax.ShapeDtypeStruct((1, 128), F32)],
    )(sbuf, *ws, *ms, *vs)


_TRANSPOSED = ("w_in", "w_gate", "w_up")
_ROW_STACKED = MATS
_ADAM_TILE = {"w_in": 208, "w_out": 256, "w_gate": 176, "w_up": 176, "w_down": 176, "lora": 256}
_SUM_TILE = {"w_in": 208, "w_out": 128, "w_gate": 176, "w_up": 176, "w_down": 176, "lora": 128}


def _full(n, stacked):
    p, r, c = stacked.shape
    if n in _ROW_STACKED:
        return stacked.reshape(p * r, c)
    return jnp.transpose(stacked, (1, 0, 2)).reshape(r, p * c)


def _by_chip(n, full):
    if n in _ROW_STACKED:
        return full.reshape(N_CHIP, full.shape[0] // N_CHIP, full.shape[1])
    r, c = full.shape
    return jnp.transpose(full.reshape(r, N_CHIP, c // N_CHIP), (1, 0, 2))


def _with_own(land_shape, dtype, own, slot):
    return lax.dynamic_update_slice(lax.empty(land_shape, dtype), own[None], (slot,) + (0,) * own.ndim)


def kernel(x, mix_norm_g, w_in, mu_shift, decay_w0, decay_w2, iclr_a0, iclr_a2, gate_g2, k_k, k_a, r_k, ln_x_w, ln_x_b, attn_out_g, w_out, ffn_norm_g, w_gate, w_up, w_down, final_norm_g, loss_target, m_mix_norm_g, m_w_in, m_mu_shift, m_decay_w0, m_decay_w2, m_iclr_a0, m_iclr_a2, m_gate_g2, m_k_k, m_k_a, m_r_k, m_ln_x_w, m_ln_x_b, m_attn_out_g, m_w_out, m_ffn_norm_g, m_w_gate, m_w_up, m_w_down, m_final_norm_g, v_mix_norm_g, v_w_in, v_mu_shift, v_decay_w0, v_decay_w2, v_iclr_a0, v_iclr_a2, v_gate_g2, v_k_k, v_k_a, v_r_k, v_ln_x_w, v_ln_x_b, v_attn_out_g, v_w_out, v_ffn_norm_g, v_w_gate, v_w_up, v_w_down, v_final_norm_g):
    names = ("mix_norm_g", "w_in", "mu_shift", "decay_w0", "decay_w2", "iclr_a0", "iclr_a2", "gate_g2", "k_k", "k_a",
             "r_k", "ln_x_w", "ln_x_b", "attn_out_g", "w_out", "ffn_norm_g", "w_gate", "w_up", "w_down", "final_norm_g")
    w = dict(zip(names, (mix_norm_g, w_in, mu_shift, decay_w0, decay_w2, iclr_a0, iclr_a2, gate_g2, k_k, k_a, r_k,
                         ln_x_w, ln_x_b, attn_out_g, w_out, ffn_norm_g, w_gate, w_up, w_down, final_norm_g)))
    m = dict(zip(names, (m_mix_norm_g, m_w_in, m_mu_shift, m_decay_w0, m_decay_w2, m_iclr_a0, m_iclr_a2, m_gate_g2,
                         m_k_k, m_k_a, m_r_k, m_ln_x_w, m_ln_x_b, m_attn_out_g, m_w_out, m_ffn_norm_g, m_w_gate,
                         m_w_up, m_w_down, m_final_norm_g)))
    v = dict(zip(names, (v_mix_norm_g, v_w_in, v_mu_shift, v_decay_w0, v_decay_w2, v_iclr_a0, v_iclr_a2, v_gate_g2,
                         v_k_k, v_k_a, v_r_k, v_ln_x_w, v_ln_x_b, v_attn_out_g, v_w_out, v_ffn_norm_g, v_w_gate,
                         v_w_up, v_w_down, v_final_norm_g)))
    first = ("w_in", "lora")
    rest = ("w_out", "w_gate", "w_up", "w_down")
    xi, yi, ci = lax.axis_index("x"), lax.axis_index("y"), lax.axis_index("c")
    my_chip, my_dev = 2 * xi + yi, 4 * xi + 2 * yi + ci
    gather, scatter = ("gather",) * 4, ("scatter",) * 4

    def stored(d):
        out = {n: jnp.transpose(d[n][0]) if n in _TRANSPOSED else d[n][0] for n in MATS}
        out["lora"] = jnp.concatenate([d[n][0] for n in LORAS], axis=0)
        return out

    ws, ms, vs = stored(w), stored(m), stored(v)
    lora_rows = [(0, 64), (64, 128), (128, 256)]
    mine = [ws["w_in"].astype(BF16), ws["lora"]]
    early = _swap_start(None, [_with_own((N_CHIP,) + a.shape, a.dtype, a, my_chip) for a in mine], gather[:2],
                        "gather_first_start")
    wb = {n: (ws[n] + early[4][0, 0]).astype(BF16) for n in rest}
    lands = [_with_own((N_CHIP,) + wb[n].shape, BF16, wb[n], my_chip) for n in rest]
    ssem, rsem, srcs_thru, lands_thru, tok = _swap_start(None, lands, gather, "gather_rest_start")
    got = _swap_wait(early[0], early[1], early[2], early[3], tok, gather[:2], "gather_first_wait")
    win_all, lora_all = _swap_gathered(got, "gather_first_halves")
    win = _full("w_in", win_all)
    w2, a2, g2m = (_full(n, lora_all[:, a:b]) for n, (a, b) in zip(LORAS, lora_rows))

    vecs = {n: w[n].reshape(1, sz) for n, sz in VECS}
    vecs["mix_norm_g"] = vecs["mix_norm_g"] + tok[0, 0]

    def get_rest(after):
        halves = _swap_wait(ssem, rsem, srcs_thru, lands_thru, after, gather, "gather_rest_wait")
        return [_full(n, z) for n, z in zip(rest, _swap_gathered(halves, "gather_rest_halves"))]

    flight = []

    def my_half(g):
        h = g.shape[1] // 2
        return lax.dynamic_slice(g, (my_chip, ci * h, 0), (1, h, g.shape[2]))[0]

    def send_rest(gw):
        gs = [_by_chip(n, gw[n]).astype(BF16) for n in rest]
        into = [_with_own((N_DEV,) + my_half(g).shape, BF16, my_half(g), my_dev) for g in gs]
        flight.extend(_swap_start(gs, into, scatter, "exchange_rest_start"))
        return flight[4]

    loss8, dx, gw, gv = _local_step(x[0], loss_target[0], win, vecs, w2, a2, g2m, get_rest, send_rest)

    core = jnp.reshape(ci, (1,)).astype(jnp.int32)
    gs = [_by_chip("w_in", gw["w_in"]).astype(BF16),
          jnp.concatenate([_by_chip(n, gw[n]) for n in LORAS], axis=1).astype(BF16)]
    theirs = _sibling_halves(gs, "presum_halves")
    sums = [_add_halves(g, o, core, _SUM_TILE[n], "chipsum_" + n) for n, g, o in zip(first, gs, theirs)]
    own = [lax.dynamic_index_in_dim(s, my_chip, 0, keepdims=False) for s in sums]
    last = _swap_start(sums, [_with_own(s.shape, BF16, o, my_chip) for s, o in zip(sums, own)], ("chipsum",) * 2,
                       "exchange_first_start")
    small = _rowsum_small([gv[n] for n, _ in VECS], loss8 + last[4])
    vecs_out = _swap_start([small], [_with_own((N_DEV,) + small.shape, F32, small, my_dev)], ("all",),
                           "exchange_vectors_start")


    def update(group, rbufs, tag):
        sums = [_reduce8(rb, core, _SUM_TILE[n], "reduce_" + n) for n, rb in zip(group, rbufs)]
        gsum = _join_halves(sums, "join_halves_" + tag)
        out = {}
        for n, g in zip(group, gsum):
            r = _adamw_call(g, ws[n][None], ms[n][None], vs[n][None], _ADAM_TILE[n], "adamw_" + n)
            if n == "lora":
                for name, (a, b) in zip(LORAS, lora_rows):
                    out[name] = [z[:, a:b] for z in r]
            else:
                out[n] = [jnp.transpose(z[0])[None] for z in r] if n in _TRANSPOSED else r
        return out, r[1]

    res, done = update(rest, _swap_wait(flight[0], flight[1], flight[2], flight[3], vecs_out[4], scatter,
                                        "exchange_rest_wait"), "rest")
    got = _swap_wait(last[0], last[1], last[2], last[3], done, ("chipsum",) * 2, "exchange_first_wait")
    res_first, done = update(first, got, "first")
    res.update(res_first)
    sbuf = _swap_wait(vecs_out[0], vecs_out[1], vecs_out[2], vecs_out[3], done, ("all",), "exchange_vectors_wait")[0]
    rows = lambda d: [d[n].reshape(1, sz) for n, sz in VECS]
    small_res = _reduce_adamw_small(sbuf, rows(w), rows(m), rows(v))

    outs = []
    for k in range(4):
        piece = {n: r[k] for n, r in res.items()}
        for i, (n, _) in enumerate(VECS):
            piece[n] = small_res[k * len(VECS) + i].reshape(w[n].shape)
        outs.extend(piece[n] for n in names)
    return (small_res[-1][0, 0], dx[None], *outs)
```

```python
import jax
import jax.numpy as jnp
from jax import lax
from jax.experimental import pallas as pl
from jax.experimental.pallas import tpu as pltpu

F32 = jnp.float32
BF16 = jnp.bfloat16

D_MODEL = 1024
HEAD_DIM = 64
RW = 512
N_PAIR = RW // 128
SHIFT_COLS = 1792
IN_COLS = 3328
D_FF = 2816
FF_CHUNK = 256
NORM_EPS = 1e-6
GN_EPS = 64e-5
CHUNK = 64
SUB = 16
WKV_PASSES = 1
ATTN_PASSES = 1
ATTN_BLOCK = 128
DILATIONS = (1, 4, 16)
NEG = -1e30
ADAM_LR, ADAM_B1, ADAM_B2, ADAM_EPS, ADAM_WD, ADAM_STEP = 0.001, 0.9, 0.999, 1e-08, 0.01, 10
VMEM_LIMIT = 56 * 1024 * 1024
MESH = pl.DeviceIdType.MESH


def _params(sem=None, **kw):
    return pltpu.CompilerParams(dimension_semantics=sem, vmem_limit_bytes=VMEM_LIMIT, **kw)


def _dot(a, b):
    return lax.dot_general(a, b, (((1,), (0,)), ((), ())), preferred_element_type=F32)


def _dot_nt(a, b):
    return lax.dot_general(a, b, (((1,), (1,)), ((), ())), preferred_element_type=F32)


def _dot_tn(a, b):
    return lax.dot_general(a, b, (((0,), (0,)), ((), ())), preferred_element_type=F32)


_FORMS = {"nn": ((1,), (0,)), "nt": ((1,), (1,)), "tn": ((0,), (0,))}


def _dg(a, b, form):
    if a.ndim == 3 or b.ndim == 3:
        nb = a.shape[0] if a.ndim == 3 else b.shape[0]
        return jnp.stack([_dg(a[i] if a.ndim == 3 else a, b[i] if b.ndim == 3 else b, form) for i in range(nb)], axis=0)
    return lax.dot_general(a, b, (_FORMS[form], ((), ())), preferred_element_type=F32)


def _split2(x):
    hi = x.astype(BF16)
    return hi, (x - hi.astype(F32)).astype(BF16)


def _split3(x):
    hi = x.astype(BF16)
    rest = x - hi.astype(F32)
    mid = rest.astype(BF16)
    return hi, mid, (rest - mid.astype(F32)).astype(BF16)


def _mm_raw(a, b, form, mode):
    if mode == 1:
        return _dg(a.astype(BF16), b.astype(BF16), form)
    if mode == 3:
        ah, al = _split2(a)
        bh, bl = _split2(b)
        return _dg(ah, bh, form) + (_dg(ah, bl, form) + _dg(al, bh, form))
    if mode == "L3":
        ab = a.astype(BF16)
        b1, b2, b3 = _split3(b)
        if form == "nn":
            n = b.shape[-1]
            wide = _dg(ab, jnp.concatenate([b1, b2, b3], axis=-1), form)
            return wide[..., :n] + (wide[..., n:2 * n] + wide[..., 2 * n:])
        return _dg(ab, b1, form) + (_dg(ab, b2, form) + _dg(ab, b3, form))
    assert mode == "R3", mode
    bb = b.astype(BF16)
    a1, a2, a3 = _split3(a)
    if form in ("nn", "nt"):
        m = a.shape[-2]
        tall = _dg(jnp.concatenate([a1, a2, a3], axis=-2), bb, form)
        return tall[..., :m, :] + (tall[..., m:2 * m, :] + tall[..., 2 * m:, :])
    return _dg(a1, bb, form) + (_dg(a2, bb, form) + _dg(a3, bb, form))


def _mm(a, b, form, mode):
    @jax.custom_vjp
    def f(a, b):
        return _mm_raw(a, b, form, mode)

    def fwd(a, b):
        return _mm_raw(a, b, form, mode), (a, b)

    def bwd(res, ct):
        a, b = res
        la = {1: 1, 3: 3, "L3": None, "R3": "R3"}[mode]
        lb = {1: 1, 3: 3, "L3": "L3", "R3": None}[mode]
        if form == "nn":
            da = None if la is None else _mm_raw(ct, b, "nt", la)
            db = None if lb is None else _mm_raw(a, ct, "tn", lb)
        elif form == "nt":
            da = None if la is None else _mm_raw(ct, b, "nn", la)
            db = None if lb is None else _mm_raw(ct, a, "tn", "R3" if lb == "L3" else lb)
        else:
            da = None if la is None else _mm_raw(b, ct, "nt", "L3" if la == "R3" else la)
            db = None if lb is None else _mm_raw(a, ct, "nn", lb)
        return (jnp.zeros_like(a) if da is None else da, jnp.zeros_like(b) if db is None else db)

    f.defvjp(fwd, bwd)
    return f(a, b)


def _seg_ones(n):
    r = lax.broadcasted_iota(jnp.int32, (n, n), 0) // HEAD_DIM
    c = lax.broadcasted_iota(jnp.int32, (n, n), 1) // HEAD_DIM
    return (r == c).astype(F32)


def _segsum(x, seg):
    return _mm(x, seg, "nn", "R3")


def _rms_fwd(x, g):
    rstd = lax.rsqrt(jnp.mean(x * x, axis=-1, keepdims=True) + NORM_EPS)
    return x * rstd * g


def _rms_bwd(dy, x, g):
    rstd = lax.rsqrt(jnp.mean(x * x, axis=-1, keepdims=True) + NORM_EPS)
    xn = x * rstd
    dxn = dy * g
    dx = rstd * (dxn - xn * jnp.mean(dxn * xn, axis=-1, keepdims=True))
    return dx, dy * xn


def _sigmoid(x):
    return 1.0 / (1.0 + jnp.exp(-x))


def _softplus(x):
    return jnp.maximum(x, 0.0) + jnp.log(1.0 + jnp.exp(-jnp.abs(x)))


def _acc(ref, val, first):
    @pl.when(first)
    def _():
        ref[...] = val

    @pl.when(jnp.logical_not(first))
    def _():
        ref[...] += val


def _colsum8(v):
    rows, n = v.shape
    return jnp.sum(v.reshape(rows // 8, 8, n), axis=0)


def _prep_fn(p, pprev, mu, w0, w2p, a0, a2p, g2, k_k, k_a):
    seg = _seg_ones(RW)
    ps = p + (pprev - p) * mu
    r = ps[:, 0:RW]
    k = ps[:, RW:2 * RW]
    v = ps[:, 2 * RW:3 * RW]
    xwa = ps[:, 3 * RW:3 * RW + 128]
    xg = ps[:, 3 * RW + 128:3 * RW + 256]
    wraw = -_softplus(-(w0 + _mm(jnp.tanh(xwa), w2p, "nn", 3))) - 0.5
    lw = -jnp.exp(wraw)
    a = _sigmoid(a0 + _mm(xwa, a2p, "nn", 3))
    g = _mm(_sigmoid(xg), g2, "nn", 3)
    kk = k * k_k
    kk = kk / jnp.maximum(jnp.sqrt(_segsum(kk * kk, seg)), 1e-12)
    k2 = k * (1.0 + (a - 1.0) * k_a)
    return r, lw, k2, v, kk, a, g


def _transposed(z):
    return jnp.stack([z[i].T for i in range(z.shape[0])], axis=0) if z.ndim == 3 else z.T


def _solve_unit_lower(lmat, rhs):
    c = lmat.shape[-1]
    row = lax.broadcasted_iota(jnp.int32, (c, c), 0)
    col = lax.broadcasted_iota(jnp.int32, (c, c), 1)
    eye = (row == col).astype(F32)
    ld = jnp.where(row // SUB == col // SUB, lmat, 0.0)
    lo = lmat - ld
    x = eye + ld
    m = ld
    mm = lambda p, q: _mm(p, q, "nn", WKV_PASSES)
    cat = jnp.concatenate
    m = mm(m, m)
    for _ in range(2):
        mx = mm(m, cat([m, x], axis=-1))
        m, x = mx[..., :c], x + mx[..., c:]
    x = x + mm(m, x)
    gw = mm(x, cat([lo, rhs], axis=-1))
    g, w = gw[..., :c], gw[..., c:]
    gg = mm(g, cat([g, w], axis=-1))
    w = w + gg[..., c:]
    return w + mm(gg[..., :c], w)


def _wkv_chunk_fn(s0, r, lw, k, v, kk, a):
    c = r.shape[-2]
    n = 2 * c
    row = lax.broadcasted_iota(jnp.int32, (n, n), 0)
    col = lax.broadcasted_iota(jnp.int32, (n, n), 1)
    same = (row // c) == (col // c)
    incl = jnp.logical_and(row >= col, same)
    strict = jnp.logical_and(row > col, same)
    sel = (lax.broadcasted_iota(jnp.int32, (n, 128), 0) // c) == (lax.broadcasted_iota(jnp.int32, (n, 128), 1) // HEAD_DIM)
    two = lambda z: jnp.concatenate([z, z], axis=-2)
    lw2 = two(lw)
    mm = lambda p_, q_, form: _mm(p_, q_, form, WKV_PASSES)
    cl = _mm(incl.astype(F32), lw2, "nn", "L3")
    p = jnp.exp(cl)
    pinv = jnp.exp(-cl)
    pprev = jnp.exp(cl - lw2)
    kk2 = two(kk)
    at = jnp.where(sel, -kk2 * pprev, 0.0)
    bt = jnp.where(sel, kk2 * two(a) * pinv, 0.0)
    kt = jnp.where(sel, two(k) * pinv, 0.0)
    rt = jnp.where(sel, two(r) * p, 0.0)
    vt = jnp.where(sel, two(v), 0.0)
    cat = jnp.concatenate
    bk = cat([bt, kt], axis=-2)
    arbk = mm(cat([at, rt], axis=-2), bk, "nt")
    ab, ak = jnp.where(strict, arbk[..., :n, :n], 0.0), jnp.where(strict, arbk[..., :n, n:], 0.0)
    rb, rk = jnp.where(incl, arbk[..., n:, :n], 0.0), jnp.where(incl, arbk[..., n:, n:], 0.0)
    s0t = _transposed(s0)
    u = _solve_unit_lower(ab, mm(cat([at, ak], axis=-1), cat([s0t, vt], axis=-2), "nn"))
    y2 = mm(cat([rt, rb, rk], axis=-1), cat([s0t, u, vt], axis=-2), "nn")
    plast = jnp.exp(jnp.sum(lw, axis=-2, keepdims=True))
    s1 = (s0 + mm(cat([u, vt], axis=-2), bk, "tn")) * plast
    r2 = lax.broadcasted_iota(jnp.int32, (128, 128), 0) // HEAD_DIM
    c2 = lax.broadcasted_iota(jnp.int32, (128, 128), 1) // HEAD_DIM
    return y2[..., :c, :] + y2[..., c:, :], jnp.where(r2 == c2, s1, 0.0)


def _post_fn(y, r, k2, v, g, lnw, lnb, rk):
    seg = _seg_ones(RW)
    mean = _segsum(y, seg) * (1.0 / HEAD_DIM)
    yc = y - mean
    var = _segsum(yc * yc, seg) * (1.0 / HEAD_DIM)
    yn = yc * lax.rsqrt(var + GN_EPS)
    out = yn * lnw + lnb + _segsum(r * k2 * rk, seg) * v
    return out * g


def _attn_block_fn(q, kc, vc, kp=None, vp=None):
    n = ATTN_BLOCK
    qi = lax.broadcasted_iota(jnp.int32, (n, n), 0)
    kj = lax.broadcasted_iota(jnp.int32, (n, n), 1)
    lane = lax.broadcasted_iota(jnp.int32, (1, 128), 1)
    scale = HEAD_DIM ** -0.5
    valid = kj <= qi
    keys, vals = kc, vc
    if kp is not None:
        valid = jnp.concatenate([valid, kj >= qi], axis=-1)
        keys, vals = jnp.concatenate([kc, kp], axis=-2), jnp.concatenate([vc, vp], axis=-2)
    m0 = (lane // HEAD_DIM) == 0
    q2 = jnp.concatenate([jnp.where(m0, q, 0.0), jnp.where(m0, 0.0, q)], axis=-2)
    valid2 = jnp.concatenate([valid, valid], axis=-2)
    s = jnp.where(valid2, _mm(q2, keys, "nt", ATTN_PASSES) * scale, NEG)
    m = jnp.max(s, axis=-1, keepdims=True)
    p = jnp.exp(s - m)
    den = jnp.sum(p, axis=-1, keepdims=True)
    o2 = _mm(p, vals, "nn", ATTN_PASSES) / den
    l2 = m + jnp.log(den)
    return jnp.where(m0, o2[..., :n, :], o2[..., n:, :]), jnp.where(m0, l2[..., :n, :], l2[..., n:, :])


def _attn_block_bwd(q, kc, vc, kp, vp, o, lse, do, dl):
    n = ATTN_BLOCK
    cat = jnp.concatenate
    qi = lax.broadcasted_iota(jnp.int32, (n, n), 0)
    kj = lax.broadcasted_iota(jnp.int32, (n, n), 1)
    m0 = (lax.broadcasted_iota(jnp.int32, (1, 128), 1) // HEAD_DIM) == 0
    scale = HEAD_DIM ** -0.5
    valid = kj <= qi
    keys, vals = kc, vc
    if kp is not None:
        valid = cat([valid, kj >= qi], axis=-1)
        keys, vals = cat([kc, kp], axis=-2), cat([vc, vp], axis=-2)
    stack = lambda z: cat([jnp.where(m0, z, 0.0), jnp.where(m0, 0.0, z)], axis=-2)
    q2, do2 = stack(q), stack(do)
    lse2 = cat([jnp.max(jnp.where(m0, lse, NEG), axis=-1, keepdims=True),
                jnp.max(jnp.where(m0, NEG, lse), axis=-1, keepdims=True)], axis=-2)
    delta = jnp.sum(do2 * cat([o, o], axis=-2), axis=-1, keepdims=True)
    dlse = jnp.sum(stack(dl), axis=-1, keepdims=True)
    mm = lambda a, b, form: _mm_raw(a, b, form, ATTN_PASSES)
    s = jnp.where(cat([valid, valid], axis=-2), mm(q2, keys, "nt") * scale, NEG)
    p = jnp.exp(s - lse2)
    ds = p * (mm(do2, vals, "nt") - delta + dlse)
    dq2 = mm(ds, keys, "nn") * scale
    dq = jnp.where(m0, dq2[..., :n, :], dq2[..., n:, :])
    dkeys = mm(ds, q2, "tn") * scale
    dvals = mm(p, do2, "tn")
    if kp is None:
        return dq, dkeys, dvals
    return dq, dkeys[..., :n, :], dvals[..., :n, :], dkeys[..., n:, :], dvals[..., n:, :]


def _combine_fn(o1, o2, o3, l1, l2, l3, og):
    seg = _seg_ones(o1.shape[-1])
    m = jnp.maximum(jnp.maximum(l1, l2), l3)
    e1, e2, e3 = jnp.exp(l1 - m), jnp.exp(l2 - m), jnp.exp(l3 - m)
    o = (e1 * o1 + e2 * o2 + e3 * o3) / (e1 + e2 + e3)
    o = o * lax.rsqrt(_segsum(o * o, seg) * (1.0 / HEAD_DIM) + NORM_EPS)
    return o * og


def _in_proj(x, g1, win):
    t = x.shape[0]
    tm = 512

    def body(x_ref, g_ref, w_ref, h_ref, pa_ref, qkv_ref):
        h = _rms_fwd(x_ref[...], g_ref[...]).astype(BF16)
        h_ref[...] = h
        proj = _dot_nt(h, w_ref[...])
        pa_ref[...] = proj[:, :SHIFT_COLS]
        for j in range(3):
            for p in range(N_PAIR):
                c0 = SHIFT_COLS + j * RW + p * 128
                qkv_ref[j, p] = proj[:, c0:c0 + 128]

    return pl.pallas_call(
        body, name="in_proj", grid=(t // tm,),
        in_specs=[pl.BlockSpec((tm, D_MODEL), lambda i: (i, 0)), pl.BlockSpec((1, D_MODEL), lambda i: (0, 0)),
                  pl.BlockSpec((IN_COLS, D_MODEL), lambda i: (0, 0))],
        out_specs=[pl.BlockSpec((tm, D_MODEL), lambda i: (i, 0)), pl.BlockSpec((tm, SHIFT_COLS), lambda i: (i, 0)),
                   pl.BlockSpec((3, N_PAIR, tm, 128), lambda i: (0, 0, i, 0))],
        out_shape=[jax.ShapeDtypeStruct((t, D_MODEL), BF16), jax.ShapeDtypeStruct((t, SHIFT_COLS), F32),
                   jax.ShapeDtypeStruct((3, N_PAIR, t, 128), F32)],
        compiler_params=_params(("parallel",)),
    )(x, g1, win)


def _shifted(p, last8, first):
    prow = jnp.where(first, 0.0, last8[7:8, :])
    rolled = pltpu.roll(p, 1, axis=0)
    rid = lax.broadcasted_iota(jnp.int32, p.shape, 0)
    return jnp.where(rid == 0, prow, rolled)


_PREP_TM = 512


def _prep_specs(tm):
    vec = lambda n: pl.BlockSpec((1, n), lambda i: (0, 0))
    mat = lambda r, n: pl.BlockSpec((r, n), lambda i: (0, 0))
    return [vec(SHIFT_COLS), vec(RW), mat(128, RW), vec(RW), mat(128, RW), mat(128, RW), vec(RW), vec(RW)]


def _prep_fwd(proj, pw):
    t = proj.shape[0]
    tm = _PREP_TM

    def body(p_ref, l8_ref, mu, w0, w2p, a0, a2p, g2, k_k, k_a, *outs):
        p = p_ref[...]
        pprev = _shifted(p, l8_ref[...], pl.program_id(0) == 0)
        res = _prep_fn(p, pprev, mu[...], w0[...], w2p[...], a0[...], a2p[...], g2[...], k_k[...], k_a[...])
        for o_ref, val in zip(outs, res):
            o_ref[...] = val

    row = pl.BlockSpec((tm, RW), lambda i: (i, 0))
    return pl.pallas_call(
        body, name="rwkv_prep", grid=(t // tm,),
        in_specs=[pl.BlockSpec((tm, SHIFT_COLS), lambda i: (i, 0)),
                  pl.BlockSpec((8, SHIFT_COLS), lambda i: (jnp.maximum(i * (tm // 8) - 1, 0), 0))] + _prep_specs(tm),
        out_specs=[row] * 7,
        out_shape=[jax.ShapeDtypeStruct((t, RW), F32)] * 7,
        compiler_params=_params(("parallel",)),
    )(proj, proj, *pw)


def _pairs(ref):
    return jnp.stack([ref[:, 128 * p:128 * (p + 1)] for p in range(N_PAIR)], axis=0)


def _wkv_fwd(r, lw, k2, v, kk, a):
    t = r.shape[0]
    nc = t // CHUNK

    def body(r_ref, lw_ref, k_ref, v_ref, kk_ref, a_ref, y_ref, s_ref, st):
        @pl.when(pl.program_id(0) == 0)
        def _():
            st[...] = jnp.zeros_like(st)

        s0 = st[...]
        s_ref[0] = s0
        y, s1 = _wkv_chunk_fn(s0, *[_pairs(ref) for ref in (r_ref, lw_ref, k_ref, v_ref, kk_ref, a_ref)])
        for p in range(N_PAIR):
            y_ref[:, 128 * p:128 * (p + 1)] = y[p]
        st[...] = s1

    blk = pl.BlockSpec((CHUNK, RW), lambda c: (c, 0))
    return pl.pallas_call(
        body, name="wkv_fwd", grid=(nc,),
        in_specs=[blk] * 6,
        out_specs=[blk, pl.BlockSpec((1, N_PAIR, 128, 128), lambda c: (c, 0, 0, 0))],
        out_shape=[jax.ShapeDtypeStruct((t, RW), F32), jax.ShapeDtypeStruct((nc, N_PAIR, 128, 128), F32)],
        scratch_shapes=[pltpu.VMEM((N_PAIR, 128, 128), F32)],
        compiler_params=_params(("arbitrary",)),
    )(r, lw, k2, v, kk, a)


_POST_TM = 512


ATTN_GROUP = 2


def _dilated_rows(d, r, n):
    if d == 1:
        return pl.ds(pl.multiple_of(n * ATTN_BLOCK, ATTN_BLOCK), ATTN_BLOCK)
    return pl.ds(r + n * (ATTN_BLOCK * d), ATTN_BLOCK, stride=d)


def _for_each_sequence(t, unit):
    for di, d in enumerate(DILATIONS):

        @pl.when(pl.program_id(1) == di)
        def _(di=di, d=d):
            nb = t // (ATTN_BLOCK * d)
            if d == 1:
                unit(di, [(d, 0, 0)], False)
                unit(di, [(d, 0, 1)], True)
                lax.fori_loop(1, nb // 2, lambda k, c: (unit(di, [(d, 0, 2 * k), (d, 0, 2 * k + 1)], True), c)[1], 0)
            else:

                def residues(r, carry):
                    unit(di, [(d, r, 0), (d, r + d // 2, 0)], False)
                    if nb > 1:
                        lax.fori_loop(1, nb, lambda n, c: (unit(di, [(d, r, n), (d, r + d // 2, n)], True), c)[1], 0)
                    return carry

                lax.fori_loop(0, d // 2, residues, 0)


def _take(ref, lead, rows_list):
    return jnp.stack([ref.at[(*lead, g)][rows, :] for rows in rows_list for g in range(ref.shape[len(lead)])], axis=0)


def _put(ref, lead, rows_list, val, add=False):
    k = 0
    for rows in rows_list:
        for g in range(ref.shape[len(lead)]):
            if add:
                ref.at[(*lead, g)][rows, :] += val[k]
            else:
                ref.at[(*lead, g)][rows, :] = val[k]
            k += 1


def _attn_fwd(qkv):
    t = qkv.shape[2]

    def body(q_ref, k_ref, v_ref, o_ref, l_ref):
        def unit(di, places, has_prev):
            cur = [_dilated_rows(d, r, n) for d, r, n in places]
            args = [_take(ref, (0,), cur) for ref in (q_ref, k_ref, v_ref)]
            if has_prev:
                prv = [_dilated_rows(d, r, n - 1) for d, r, n in places]
                args += [_take(ref, (0,), prv) for ref in (k_ref, v_ref)]
            o, lse = _attn_block_fn(*args)
            _put(o_ref, (0,), cur, o)
            _put(l_ref, (0,), cur, lse)

        _for_each_sequence(t, unit)

    spec = lambda j: pl.BlockSpec((1, N_PAIR, t, 128), lambda i, b: (j, i, 0, 0))
    out = pl.BlockSpec((1, N_PAIR, t, 128), lambda i, b: (b, i, 0, 0))
    return pl.pallas_call(
        body, name="attn_fwd", grid=(1, len(DILATIONS)),
        in_specs=[spec(0), spec(1), spec(2)], out_specs=[out, out],
        out_shape=[jax.ShapeDtypeStruct((3, N_PAIR, t, 128), F32)] * 2,
        compiler_params=_params(("parallel", "arbitrary")),
    )(qkv, qkv, qkv)


_COMB_TM = 512


def _mixers_out(y, r, k2, v, g, lnw, lnb, rk, o, l, og):
    t = y.shape[0]
    tm = _COMB_TM

    def body(y_ref, r_ref, k_ref, v_ref, g_ref, lnw_ref, lnb_ref, rk_ref, o_ref, l_ref, og_ref, out_ref):
        out_ref[:, :RW] = _post_fn(y_ref[...], r_ref[...], k_ref[...], v_ref[...], g_ref[...],
                                   lnw_ref[...], lnb_ref[...], rk_ref[...]).astype(BF16)
        for p in range(N_PAIR):
            cols = slice(128 * p, 128 * (p + 1))
            out_ref[:, RW + 128 * p:RW + 128 * (p + 1)] = _combine_fn(
                o_ref[0, p], o_ref[1, p], o_ref[2, p], l_ref[0, p], l_ref[1, p], l_ref[2, p], og_ref[:, cols]).astype(BF16)

    row = pl.BlockSpec((tm, RW), lambda i: (i, 0))
    vec = pl.BlockSpec((1, RW), lambda i: (0, 0))
    blk = pl.BlockSpec((3, N_PAIR, tm, 128), lambda i: (0, 0, i, 0))
    return pl.pallas_call(
        body, name="mixers_out", grid=(t // tm,),
        in_specs=[row] * 5 + [vec] * 3 + [blk, blk, vec], out_specs=pl.BlockSpec((tm, D_MODEL), lambda i: (i, 0)),
        out_shape=jax.ShapeDtypeStruct((t, D_MODEL), BF16),
        compiler_params=_params(("parallel",)),
    )(y, r, k2, v, g, lnw, lnb, rk, o, l, og)


def _ffn_all(x, ycat, wg, wu, wd, wout, g2, gf, tgt):
    t = x.shape[0]
    tm = 256

    def body(x_ref, y_ref, wg_ref, wu_ref, wd_ref, wo_ref, g2_ref, gf_ref, t_ref,
             h_ref, act_ref, dx2b_ref, dgt_ref, dup_ref, dx1b_ref, dx1_ref, dya_ref, dyb_ref, loss_ref, dgf_ref, dg2_ref,
             gt_s, up_s):
        first = pl.program_id(0) == 0
        x1 = x_ref[...] + _dot(y_ref[...], wo_ref[...])
        h = _rms_fwd(x1, g2_ref[...]).astype(BF16)
        h_ref[...] = h
        for c0 in range(0, D_FF, FF_CHUNK):
            cols = slice(c0, c0 + FF_CHUNK)
            gt = _dot_nt(h, wg_ref[cols, :])
            up = _dot_nt(h, wu_ref[cols, :])
            gt_s[:, cols] = gt.astype(BF16)
            up_s[:, cols] = up.astype(BF16)
            act_ref[:, cols] = (gt * _sigmoid(gt) * up).astype(BF16)
        x2 = x1 + _dot(act_ref[...], wd_ref[...])
        gf_ = gf_ref[...]
        diff = _rms_fwd(x2, gf_) - t_ref[...]
        lrow = 0.5 * jnp.sum(_colsum8(diff * diff), axis=1, keepdims=True) * (1.0 / D_MODEL)
        _acc(loss_ref, jnp.broadcast_to(lrow, (8, 128)), first)
        dx2, dgr = _rms_bwd(diff * (1.0 / D_MODEL), x2, gf_)
        _acc(dgf_ref, _colsum8(dgr), first)
        dx2b = dx2.astype(BF16)
        dx2b_ref[...] = dx2b
        for c0 in range(0, D_FF, FF_CHUNK):
            cols = slice(c0, c0 + FF_CHUNK)
            dact = _dot_nt(dx2b, wd_ref[cols, :])
            gt = gt_s[:, cols].astype(F32)
            sg = _sigmoid(gt)
            dgt_ref[:, cols] = (dact * up_s[:, cols].astype(F32) * sg * (1.0 + gt * (1.0 - sg))).astype(BF16)
            dup_ref[:, cols] = (dact * gt * sg).astype(BF16)
        dh = _dot(dgt_ref[...], wg_ref[...]) + _dot(dup_ref[...], wu_ref[...])
        dxn, dgr2 = _rms_bwd(dh, x1, g2_ref[...])
        _acc(dg2_ref, _colsum8(dgr2), first)
        dx1 = dx2 + dxn
        dx1_ref[...] = dx1
        dx1b = dx1.astype(BF16)
        dx1b_ref[...] = dx1b
        dy = _dot_nt(dx1b, wo_ref[...])
        dya_ref[...] = dy[:, :RW]
        dyb_ref[...] = dy[:, RW:]

    row = pl.BlockSpec((tm, D_MODEL), lambda i: (i, 0))
    wide = pl.BlockSpec((tm, D_FF), lambda i: (i, 0))
    half = pl.BlockSpec((tm, RW), lambda i: (i, 0))
    wsp = pl.BlockSpec((D_FF, D_MODEL), lambda i: (0, 0))
    vec = pl.BlockSpec((1, D_MODEL), lambda i: (0, 0))
    part = pl.BlockSpec((8, D_MODEL), lambda i: (0, 0))
    bf = lambda n: jax.ShapeDtypeStruct((t, n), BF16)
    return pl.pallas_call(
        body, name="ffn_all", grid=(t // tm,),
        in_specs=[row, row, wsp, wsp, wsp, pl.BlockSpec((D_MODEL, D_MODEL), lambda i: (0, 0)), vec, vec, row],
        out_specs=[row, wide, row, wide, wide, row, row, half, half, pl.BlockSpec((8, 128), lambda i: (0, 0)), part, part],
        out_shape=[bf(D_MODEL), bf(D_FF), bf(D_MODEL), bf(D_FF), bf(D_FF), bf(D_MODEL),
                   jax.ShapeDtypeStruct((t, D_MODEL), F32), jax.ShapeDtypeStruct((t, RW), F32),
                   jax.ShapeDtypeStruct((t, RW), F32), jax.ShapeDtypeStruct((8, 128), F32),
                   jax.ShapeDtypeStruct((8, D_MODEL), F32), jax.ShapeDtypeStruct((8, D_MODEL), F32)],
        scratch_shapes=[pltpu.VMEM((tm, D_FF), BF16), pltpu.VMEM((tm, D_FF), BF16)],
        compiler_params=_params(("arbitrary",)),
    )(x, ycat, wg, wu, wd, wout, g2, gf, tgt)


def _wgrad(a, b, tk, tn, name):
    t, kdim = a.shape
    ndim = b.shape[1]

    def body(a_ref, b_ref, o_ref):
        o_ref[...] = _dot_tn(a_ref[...], b_ref[...])

    return pl.pallas_call(
        body, name=name, grid=(kdim // tk, ndim // tn),
        in_specs=[pl.BlockSpec((t, tk), lambda i, j: (0, i)), pl.BlockSpec((t, tn), lambda i, j: (0, j))],
        out_specs=pl.BlockSpec((tk, tn), lambda i, j: (i, j)),
        out_shape=jax.ShapeDtypeStruct((kdim, ndim), F32),
        compiler_params=_params(("parallel", "parallel")),
    )(a, b)


def _post_bwd(dya, y, r, k2, v, g, lnw, lnb, rk):
    t = y.shape[0]
    tm = _POST_TM

    def body(d_ref, y_ref, r_ref, k_ref, v_ref, g_ref, lnw_ref, lnb_ref, rk_ref,
             dy_ref, dr_ref, dk_ref, dv_ref, dg_ref, dlnw_ref, dlnb_ref, drk_ref):
        first = pl.program_id(0) == 0
        ones = jnp.ones((tm, 1), F32)
        prim = (y_ref[...], r_ref[...], k_ref[...], v_ref[...], g_ref[...],
                ones * lnw_ref[...], ones * lnb_ref[...], ones * rk_ref[...])
        _, vjp = jax.vjp(_post_fn, *prim)
        dy, dr, dk, dv, dg, dlnw, dlnb, drk = vjp(d_ref[...])
        dy_ref[...] = dy
        dr_ref[...] = dr
        dk_ref[...] = dk
        dv_ref[...] = dv
        dg_ref[...] = dg
        _acc(dlnw_ref, _colsum8(dlnw), first)
        _acc(dlnb_ref, _colsum8(dlnb), first)
        _acc(drk_ref, _colsum8(drk), first)

    row = pl.BlockSpec((tm, RW), lambda i: (i, 0))
    vec = pl.BlockSpec((1, RW), lambda i: (0, 0))
    part = pl.BlockSpec((8, RW), lambda i: (0, 0))
    return pl.pallas_call(
        body, name="rwkv_post_bwd", grid=(t // tm,),
        in_specs=[row] * 6 + [vec] * 3, out_specs=[row] * 5 + [part] * 3,
        out_shape=[jax.ShapeDtypeStruct((t, RW), F32)] * 5 + [jax.ShapeDtypeStruct((8, RW), F32)] * 3,
        compiler_params=_params(("arbitrary",)),
    )(dya, y, r, k2, v, g, lnw, lnb, rk)


def _wkv_bwd(dy, s0s, r, lw, k2, v, kk, a):
    t = r.shape[0]
    nc = t // CHUNK

    def body(dy_ref, s_ref, r_ref, lw_ref, k_ref, v_ref, kk_ref, a_ref,
             dr_ref, dlw_ref, dk_ref, dv_ref, dkk_ref, da_ref, ds):
        @pl.when(pl.program_id(0) == 0)
        def _():
            ds[...] = jnp.zeros_like(ds)

        _, vjp = jax.vjp(_wkv_chunk_fn, s_ref[0],
                         *[_pairs(ref) for ref in (r_ref, lw_ref, k_ref, v_ref, kk_ref, a_ref)])
        res = vjp((_pairs(dy_ref), ds[...]))
        ds[...] = res[0]
        for ref, val in zip((dr_ref, dlw_ref, dk_ref, dv_ref, dkk_ref, da_ref), res[1:]):
            for p in range(N_PAIR):
                ref[:, 128 * p:128 * (p + 1)] = val[p]

    blk = pl.BlockSpec((CHUNK, RW), lambda c: (nc - 1 - c, 0))
    return pl.pallas_call(
        body, name="wkv_bwd", grid=(nc,),
        in_specs=[blk, pl.BlockSpec((1, N_PAIR, 128, 128), lambda c: (nc - 1 - c, 0, 0, 0))] + [blk] * 6,
        out_specs=[blk] * 6,
        out_shape=[jax.ShapeDtypeStruct((t, RW), F32)] * 6,
        scratch_shapes=[pltpu.VMEM((N_PAIR, 128, 128), F32)],
        compiler_params=_params(("arbitrary",)),
    )(dy, s0s, r, lw, k2, v, kk, a)


def _prep_bwd(proj, pw, douts):
    t = proj.shape[0]
    tm = _PREP_TM
    nt = t // tm

    def body(p_ref, l8_ref, mu, w0, w2p, a0, a2p, g2, k_k, k_a, dr, dr2, dlw, dk2, dk22, dv, dv2, dkk, da, dg,
             dp_ref, dmu_ref, dw0_ref, dw2_ref, da0_ref, da2_ref, dg2_ref, dkk_ref, dka_ref, carry):
        i = pl.program_id(0)
        first = i == 0

        @pl.when(first)
        def _():
            carry[...] = jnp.zeros_like(carry)

        p = p_ref[...]
        pprev = _shifted(p, l8_ref[...], i == nt - 1)
        ones = jnp.ones((tm, 1), F32)
        prim = (p, pprev, ones * mu[...], ones * w0[...], w2p[...], ones * a0[...], a2p[...], g2[...],
                ones * k_k[...], ones * k_a[...])
        _, vjp = jax.vjp(_prep_fn, *prim)
        dp, dpp, dmu, dw0, dw2, da0, da2, dg2, dkk_, dka = vjp(
            (dr[...] + dr2[...], dlw[...], dk2[...] + dk22[...], dv[...] + dv2[...], dkk[...], da[...], dg[...]))
        up = pltpu.roll(dpp, tm - 1, axis=0)
        rid = lax.broadcasted_iota(jnp.int32, dpp.shape, 0)
        dp_ref[...] = dp + jnp.where(rid == tm - 1, carry[0:1, :], up)
        carry[...] = jnp.broadcast_to(dpp[0:1, :], carry.shape)
        _acc(dmu_ref, _colsum8(dmu), first)
        _acc(dw0_ref, _colsum8(dw0), first)
        _acc(dw2_ref, dw2, first)
        _acc(da0_ref, _colsum8(da0), first)
        _acc(da2_ref, da2, first)
        _acc(dg2_ref, dg2, first)
        _acc(dkk_ref, _colsum8(dkk_), first)
        _acc(dka_ref, _colsum8(dka), first)

    rev = lambda i: (nt - 1 - i, 0)
    row = pl.BlockSpec((tm, RW), rev)
    part = lambda n: pl.BlockSpec((8, n), lambda i: (0, 0))
    mat = pl.BlockSpec((128, RW), lambda i: (0, 0))
    return pl.pallas_call(
        body, name="rwkv_prep_bwd", grid=(nt,),
        in_specs=[pl.BlockSpec((tm, SHIFT_COLS), rev),
                  pl.BlockSpec((8, SHIFT_COLS), lambda i: (jnp.maximum((nt - 1 - i) * (tm // 8) - 1, 0), 0))]
                 + _prep_specs(tm) + [row] * 10,
        out_specs=[pl.BlockSpec((tm, SHIFT_COLS), rev), part(SHIFT_COLS), part(RW), mat, part(RW), mat, mat,
                   part(RW), part(RW)],
        out_shape=[jax.ShapeDtypeStruct((t, SHIFT_COLS), F32), jax.ShapeDtypeStruct((8, SHIFT_COLS), F32),
                   jax.ShapeDtypeStruct((8, RW), F32), jax.ShapeDtypeStruct((128, RW), F32),
                   jax.ShapeDtypeStruct((8, RW), F32), jax.ShapeDtypeStruct((128, RW), F32),
                   jax.ShapeDtypeStruct((128, RW), F32), jax.ShapeDtypeStruct((8, RW), F32),
                   jax.ShapeDtypeStruct((8, RW), F32)],
        scratch_shapes=[pltpu.VMEM((8, SHIFT_COLS), F32)],
        compiler_params=_params(("arbitrary",)),
    )(proj, proj, *pw, *douts)


def _combine_bwd(dyb, o, l, og):
    t = dyb.shape[0]
    tm = _COMB_TM

    def body(d_ref, o_ref, l_ref, og_ref, do_ref, dl_ref, dog_ref):
        ones = jnp.ones((tm, 1), F32)
        dog = []
        for p in range(N_PAIR):
            cols = slice(128 * p, 128 * (p + 1))
            _, vjp = jax.vjp(_combine_fn, o_ref[0, p], o_ref[1, p], o_ref[2, p], l_ref[0, p], l_ref[1, p], l_ref[2, p],
                             ones * og_ref[:, cols])
            res = vjp(d_ref[:, cols])
            for b in range(3):
                do_ref[b, p] = res[b]
                dl_ref[b, p] = res[3 + b]
            dog.append(_colsum8(res[6]))
        _acc(dog_ref, jnp.concatenate(dog, axis=1), pl.program_id(0) == 0)

    blk = pl.BlockSpec((3, N_PAIR, tm, 128), lambda i: (0, 0, i, 0))
    return pl.pallas_call(
        body, name="attn_combine_bwd", grid=(t // tm,),
        in_specs=[pl.BlockSpec((tm, RW), lambda i: (i, 0)), blk, blk, pl.BlockSpec((1, RW), lambda i: (0, 0))],
        out_specs=[blk, blk, pl.BlockSpec((8, RW), lambda i: (0, 0))],
        out_shape=[jax.ShapeDtypeStruct((3, N_PAIR, t, 128), F32)] * 2 + [jax.ShapeDtypeStruct((8, RW), F32)],
        compiler_params=_params(("arbitrary",)),
    )(dyb, o, l, og)


def _attn_bwd(do, dl, o, lse, qkv):
    t = qkv.shape[2]

    def body(do_ref, dl_ref, o_ref, l_ref, q_ref, k_ref, v_ref, dq_ref, dk_ref, dv_ref):
        @pl.when(pl.program_id(1) == 0)
        def _():
            for ref in (dq_ref, dk_ref, dv_ref):
                ref[...] = jnp.zeros_like(ref)

        def unit(di, places, has_prev):
            cur = [_dilated_rows(d, r, n) for d, r, n in places]
            q, kc, vc = [_take(ref, (0,), cur) for ref in (q_ref, k_ref, v_ref)]
            kp = vp = None
            if has_prev:
                prv = [_dilated_rows(d, r, n - 1) for d, r, n in places]
                kp, vp = [_take(ref, (0,), prv) for ref in (k_ref, v_ref)]
            res = _attn_block_bwd(q, kc, vc, kp, vp, *[_take(ref, (0,), cur) for ref in (o_ref, l_ref, do_ref, dl_ref)])
            _put(dq_ref, (), cur, res[0], add=True)
            _put(dk_ref, (), cur, res[1], add=True)
            _put(dv_ref, (), cur, res[2], add=True)
            if has_prev:
                _put(dk_ref, (), prv, res[3], add=True)
                _put(dv_ref, (), prv, res[4], add=True)

        _for_each_sequence(t, unit)

    spec = lambda j: pl.BlockSpec((1, ATTN_GROUP, t, 128), lambda i, b: (j, i, 0, 0))
    branch = pl.BlockSpec((1, ATTN_GROUP, t, 128), lambda i, b: (b, i, 0, 0))
    out = pl.BlockSpec((ATTN_GROUP, t, 128), lambda i, b: (i, 0, 0))
    return pl.pallas_call(
        body, name="attn_bwd", grid=(N_PAIR // ATTN_GROUP, len(DILATIONS)),
        in_specs=[branch] * 4 + [spec(0), spec(1), spec(2)], out_specs=[out] * 3,
        out_shape=[jax.ShapeDtypeStruct((N_PAIR, t, 128), F32)] * 3,
        compiler_params=_params(("parallel", "arbitrary")),
    )(do, dl, o, lse, qkv, qkv, qkv)


def _in_proj_bwd(dpa, dq, dk, dv, win, x, g1, dx1):
    t = x.shape[0]
    tm = 256

    def body(dpa_ref, dq_ref, dk_ref, dv_ref, w_ref, x_ref, g_ref, dx1_ref, dproj_ref, dx_ref, dg_ref):
        parts = [dpa_ref[...]] + [ref[p] for ref in (dq_ref, dk_ref, dv_ref) for p in range(N_PAIR)]
        dproj = jnp.concatenate([z.astype(BF16) for z in parts], axis=1)
        dproj_ref[...] = dproj
        dh = _dot(dproj, w_ref[...])
        dxn, dgr = _rms_bwd(dh, x_ref[...], g_ref[...])
        dx_ref[...] = dx1_ref[...] + dxn
        _acc(dg_ref, _colsum8(dgr), pl.program_id(0) == 0)

    row = pl.BlockSpec((tm, D_MODEL), lambda i: (i, 0))
    pair = pl.BlockSpec((N_PAIR, tm, 128), lambda i: (0, i, 0))
    return pl.pallas_call(
        body, name="in_proj_bwd", grid=(t // tm,),
        in_specs=[pl.BlockSpec((tm, SHIFT_COLS), lambda i: (i, 0))] + [pair] * 3
                 + [pl.BlockSpec((IN_COLS, D_MODEL), lambda i: (0, 0)), row, pl.BlockSpec((1, D_MODEL), lambda i: (0, 0)), row],
        out_specs=[pl.BlockSpec((tm, IN_COLS), lambda i: (i, 0)), row, pl.BlockSpec((8, D_MODEL), lambda i: (0, 0))],
        out_shape=[jax.ShapeDtypeStruct((t, IN_COLS), BF16), jax.ShapeDtypeStruct((t, D_MODEL), F32),
                   jax.ShapeDtypeStruct((8, D_MODEL), F32)],
        compiler_params=_params(("arbitrary",)),
    )(dpa, dq, dk, dv, win, x, g1, dx1)


def _pad_lora(w, lo):
    z = jnp.zeros((64, RW), F32)
    return jnp.concatenate([w, z], axis=0) if lo == 0 else jnp.concatenate([z, w], axis=0)


def _local_step(x, tgt, win, vecs, w2, a2, g2m, get_rest, send_rest):
    pw = (vecs["mu_shift"], vecs["decay_w0"], _pad_lora(w2, 0), vecs["iclr_a0"], _pad_lora(a2, 64), g2m,
          vecs["k_k"], vecs["k_a"])
    h, proj, qkv = _in_proj(x, vecs["mix_norm_g"], win)
    r, lw, k2, v, kk, a, g = _prep_fwd(proj, pw)
    y, s0s = _wkv_fwd(r, lw, k2, v, kk, a)
    o_att, l_att = _attn_fwd(qkv)
    ycat = _mixers_out(y, r, k2, v, g, vecs["ln_x_w"], vecs["ln_x_b"], vecs["r_k"], o_att, l_att, vecs["attn_out_g"])
    wout, wg, wu, wd = get_rest(ycat)
    h2, act, dx2b, dgt, dup, dx1b, dx1, dya, dyb, loss8, dgf, dg2n = _ffn_all(
        x, ycat, wg, wu, wd, wout, vecs["ffn_norm_g"], vecs["final_norm_g"], tgt)
    gw = {
        "w_down": _wgrad(act, dx2b, 1408, 1024, "wgrad_down"),
        "w_gate": _wgrad(dgt, h2, 1408, 1024, "wgrad_gate"),
        "w_up": _wgrad(dup, h2, 1408, 1024, "wgrad_up"),
        "w_out": _wgrad(ycat, dx1b, 1024, 1024, "wgrad_out"),
    }

    lnw = vecs["ln_x_w"] + send_rest(gw)[0, 0]
    dy, dr_p, dk2_p, dv_p, dg, dlnw, dlnb, drk = _post_bwd(dya, y, r, k2, v, g, lnw, vecs["ln_x_b"], vecs["r_k"])
    dr_s, dlw, dk2_s, dv_s, dkk, da = _wkv_bwd(dy, s0s, r, lw, k2, v, kk, a)
    dpa, dmu, dw0, dw2p, da0, da2p, dg2m, dk_k, dk_a = _prep_bwd(
        proj, pw, (dr_p, dr_s, dlw, dk2_p, dk2_s, dv_p, dv_s, dkk, da, dg))

    do_att, dl_att, dog = _combine_bwd(dyb, o_att, l_att, vecs["attn_out_g"])
    dq, dk, dv = _attn_bwd(do_att, dl_att, o_att, l_att, qkv)
    dproj, dx, dg1 = _in_proj_bwd(dpa, dq, dk, dv, win, x, vecs["mix_norm_g"], dx1)
    gw["w_in"] = _wgrad(dproj, h, 1664, 1024, "wgrad_in")
    gw["decay_w2"] = dw2p[:64]
    gw["iclr_a2"] = da2p[64:]
    gw["gate_g2"] = dg2m
    gv = {"mix_norm_g": dg1, "mu_shift": dmu, "decay_w0": dw0, "iclr_a0": da0, "k_k": dk_k, "k_a": dk_a, "r_k": drk,
          "ln_x_w": dlnw, "ln_x_b": dlnb, "attn_out_g": dog, "ffn_norm_g": dg2n, "final_norm_g": dgf}
    return loss8, dx, gw, gv


N_CHIP = 4
N_DEV = 8
MATS = ("w_in", "w_out", "w_gate", "w_up", "w_down")
LORAS = ("decay_w2", "iclr_a2", "gate_g2")
VECS = (("mix_norm_g", 1024), ("mu_shift", 1792), ("decay_w0", 512), ("iclr_a0", 512), ("k_k", 512), ("k_a", 512),
        ("r_k", 512), ("ln_x_w", 512), ("ln_x_b", 512), ("attn_out_g", 512), ("ffn_norm_g", 1024),
        ("final_norm_g", 1024))
N_VEC = sum(n for _, n in VECS)
N_SMALL = N_VEC + 128
ANY = pl.BlockSpec(memory_space=pl.ANY)


def _flip(v, f):
    return 1 - v if f else v


class _Me:
    def __init__(self, mode):
        x, y, c = lax.axis_index("x"), lax.axis_index("y"), lax.axis_index("c")
        self.core, self.chip, self.dev = c, 2 * x + y, 4 * x + 2 * y + c
        self.sibling = (x, y, 1 - c)
        if mode == "chips":
            self.peers = [(px, py, c) for px, py in ((1 - x, y), (x, 1 - y), (1 - x, 1 - y))]
        else:
            self.peers = [(_flip(x, k & 4), _flip(y, k & 2), _flip(c, k & 1)) for k in range(1, N_DEV)]


def _half(core, rows):
    h = rows // 2
    return pl.ds(pl.multiple_of(core * h, h), h)


_BY_CHIP = ("gather", "chipsum")


def _peer_copy(srcs, dsts, kinds, send_sems, recv_sems, me, j, i, incoming):
    px, py, pc = me.peers[j]
    pchip, pdev = 2 * px + py, 4 * px + 2 * py + pc
    src, dst, kind = srcs[i], dsts[i], kinds[i]
    if kind == "gather":
        rows = _half(me.core, src.shape[1])
        src, dst = src.at[me.chip, rows], dst.at[pchip if incoming else me.chip, rows]
    elif kind == "scatter":
        src, dst = src.at[pchip, _half(pc, src.shape[1])], dst.at[pdev if incoming else me.dev]
    elif kind == "chipsum":
        src, dst = src.at[pchip], dst.at[pchip if incoming else me.chip]
    else:
        dst = dst.at[pdev if incoming else me.dev]
    n = len(srcs)
    return pltpu.make_async_remote_copy(src_ref=src, dst_ref=dst, send_sem=send_sems.at[n * j + i],
                                        recv_sem=recv_sems.at[n * j + i], device_id=(px, py, pc), device_id_type=MESH)


def _mode(kinds):
    return "chips" if kinds[0] in _BY_CHIP else "devs"


def _npeer(kinds):
    return N_CHIP - 1 if kinds[0] in _BY_CHIP else N_DEV - 1


def _sibling_halves(gs, name):
    n = len(gs)

    def body(*refs):
        srcs, dsts, send_sems, recv_sems = refs[:n], refs[n:2 * n], refs[2 * n], refs[2 * n + 1]
        me = _Me("chips")

        def copy(i, p):
            return pltpu.make_async_remote_copy(
                src_ref=srcs[i].at[p, _half(1 - me.core, srcs[i].shape[1])], dst_ref=dsts[i].at[p],
                send_sem=send_sems.at[N_CHIP * i + p], recv_sem=recv_sems.at[N_CHIP * i + p],
                device_id=me.sibling, device_id_type=MESH)

        copies = [copy(i, p) for i in range(n) for p in range(N_CHIP)]
        for cp in copies:
            cp.start()
        for cp in copies:
            cp.wait()

    return pl.pallas_call(
        body, name=name, in_specs=[ANY] * n, out_specs=[ANY] * n,
        out_shape=[jax.ShapeDtypeStruct((N_CHIP, g.shape[1] // 2, g.shape[2]), g.dtype) for g in gs],
        scratch_shapes=[pltpu.SemaphoreType.DMA((N_CHIP * n,)), pltpu.SemaphoreType.DMA((N_CHIP * n,))],
    )(*gs)


def _add_halves(g, other, core, tr, name):
    _, h, cols = other.shape

    def body(core_ref, g_ref, o_ref, out_ref):
        out_ref[...] = (g_ref[...].astype(F32) + o_ref[...].astype(F32)).astype(BF16)

    blk = lambda off: pl.BlockSpec((1, tr, cols), lambda p, i, core_ref: (p, core_ref[0] * (h // tr) * off + i, 0))
    return pl.pallas_call(
        body, name=name,
        grid_spec=pltpu.PrefetchScalarGridSpec(num_scalar_prefetch=1, grid=(N_CHIP, h // tr),
                                               in_specs=[blk(1), blk(0)], out_specs=blk(0)),
        out_shape=jax.ShapeDtypeStruct(other.shape, BF16),
        compiler_params=_params(("parallel", "parallel")),
    )(core, g, other)


def _swap_gathered(lands, name):
    n = len(lands)

    def body(*refs):
        dsts, send_sems, recv_sems = refs[n:2 * n], refs[2 * n], refs[2 * n + 1]
        me = _Me("chips")

        def copy(j, i, incoming):
            px, py, _ = me.peers[j]
            rows_out, rows_in = _half(me.core, dsts[i].shape[1]), _half(1 - me.core, dsts[i].shape[1])
            return pltpu.make_async_remote_copy(
                src_ref=dsts[i].at[2 * px + py, rows_out], dst_ref=dsts[i].at[2 * px + py, rows_in if incoming else rows_out],
                send_sem=send_sems.at[n * j + i], recv_sem=recv_sems.at[n * j + i], device_id=me.sibling, device_id_type=MESH)

        sends = [copy(j, i, False) for j in range(3) for i in range(n)]
        for cp in sends:
            cp.start()
        for j in range(3):
            for i in range(n):
                copy(j, i, True).wait_recv()
        for cp in sends:
            cp.wait_send()

    return pl.pallas_call(
        body, name=name, in_specs=[ANY] * n, out_specs=[ANY] * n,
        out_shape=[jax.ShapeDtypeStruct(l.shape, l.dtype) for l in lands],
        input_output_aliases={i: i for i in range(n)},
        scratch_shapes=[pltpu.SemaphoreType.DMA((3 * n,)), pltpu.SemaphoreType.DMA((3 * n,))],
    )(*lands)


def _join_halves(sums, name):
    n = len(sums)

    def body(*refs):
        dsts, send_sems, recv_sems = refs[n:2 * n], refs[2 * n], refs[2 * n + 1]
        me = _Me("chips")

        def copy(i, incoming):
            mine, other = _half(me.core, dsts[i].shape[0]), _half(1 - me.core, dsts[i].shape[0])
            return pltpu.make_async_remote_copy(src_ref=dsts[i].at[mine], dst_ref=dsts[i].at[other if incoming else mine],
                                                send_sem=send_sems.at[i], recv_sem=recv_sems.at[i],
                                                device_id=me.sibling, device_id_type=MESH)

        sends = [copy(i, False) for i in range(n)]
        for cp in sends:
            cp.start()
        for i in range(n):
            copy(i, True).wait_recv()
        for cp in sends:
            cp.wait_send()

    return pl.pallas_call(
        body, name=name, in_specs=[ANY] * n, out_specs=[ANY] * n,
        out_shape=[jax.ShapeDtypeStruct(s.shape, s.dtype) for s in sums],
        input_output_aliases={i: i for i in range(n)},
        scratch_shapes=[pltpu.SemaphoreType.DMA((n,)), pltpu.SemaphoreType.DMA((n,))],
    )(*sums)


HBM = pl.BlockSpec(memory_space=pltpu.HBM)
SEM = pl.BlockSpec(memory_space=pltpu.SEMAPHORE)
EFFECT = pltpu.SideEffectType.DATAFLOW_SIDE_EFFECTING


def _swap_start(arrs, lands, kinds, name):
    n = len(lands)
    ops = list(lands) if arrs is None else [*arrs, *lands]
    k = len(ops)

    def body(*refs):
        srcs, dsts, send_sems, recv_sems, token = refs[:n], refs[k - n:k], refs[k], refs[k + 1], refs[-1]
        me = _Me(_mode(kinds))
        for j in range(len(me.peers)):
            for i in range(n):
                _peer_copy(srcs, dsts, kinds, send_sems, recv_sems, me, j, i, False).start()
        token[...] = jnp.zeros_like(token)

    ns = _npeer(kinds) * n
    outs = pl.pallas_call(
        body, name=name,
        out_shape=(pltpu.SemaphoreType.DMA((ns,)), pltpu.SemaphoreType.DMA((ns,)),
                   *[pltpu.HBM(a.shape, a.dtype) for a in ops], jax.ShapeDtypeStruct((8, 128), F32)),
        in_specs=[HBM] * k, out_specs=(SEM, SEM, *[HBM] * k, pl.BlockSpec(memory_space=pltpu.VMEM)),
        input_output_aliases={i: 2 + i for i in range(k)},
        compiler_params=pltpu.CompilerParams(has_side_effects=EFFECT),
    )(*[pltpu.with_memory_space_constraint(a, pltpu.HBM) for a in ops])
    return outs[0], outs[1], outs[2:2 + k - n], outs[2 + k - n:2 + k], outs[-1]


def _swap_wait(send_sems, recv_sems, srcs_thru, lands_thru, after, kinds, name):
    n = len(lands_thru)
    ops = [*srcs_thru, *lands_thru]
    k = len(ops)

    def body(*refs):
        srcs, dsts, s_sems, r_sems = refs[:n], refs[k - n:k], refs[k], refs[k + 1]
        me = _Me(_mode(kinds))
        for j in range(len(me.peers)):
            for i in range(n):
                cp = _peer_copy(srcs, dsts, kinds, s_sems, r_sems, me, j, i, True)
                cp.wait_send()
                cp.wait_recv()

    outs = pl.pallas_call(
        body, name=name,
        out_shape=tuple(pltpu.HBM(a.shape, a.dtype) for a in ops),
        in_specs=[HBM] * k + [SEM, SEM, ANY], out_specs=tuple([HBM] * k),
        input_output_aliases={i: i for i in range(k)},
        compiler_params=pltpu.CompilerParams(has_side_effects=EFFECT),
    )(*ops, send_sems, recv_sems, after)
    return outs[k - n:]


def _adamw(w, g, m, v):
    m = ADAM_B1 * m + (1.0 - ADAM_B1) * g
    v = ADAM_B2 * v + (1.0 - ADAM_B2) * (g * g)
    m_hat = m / (1.0 - ADAM_B1 ** ADAM_STEP)
    v_hat = v / (1.0 - ADAM_B2 ** ADAM_STEP)
    delta = -ADAM_LR * (m_hat / (jnp.sqrt(v_hat) + ADAM_EPS) + ADAM_WD * w)
    return delta, m, v


def _reduce8(rbuf, core, tr, name):
    slots, h, cols = rbuf.shape

    def body(core_ref, r_ref, g_ref):
        g = r_ref[0].astype(F32)
        for s in range(1, slots):
            g = g + r_ref[s].astype(F32)
        g_ref[...] = g

    return pl.pallas_call(
        body, name=name,
        grid_spec=pltpu.PrefetchScalarGridSpec(
            num_scalar_prefetch=1, grid=(h // tr,),
            in_specs=[pl.BlockSpec((slots, tr, cols), lambda i, core_ref: (0, i, 0))],
            out_specs=pl.BlockSpec((tr, cols), lambda i, core_ref: (core_ref[0] * (h // tr) + i, 0))),
        out_shape=jax.ShapeDtypeStruct((2 * h, cols), F32),
        compiler_params=_params(("parallel",)),
    )(core, rbuf)


def _adamw_call(g, w, m, v, tr, name):
    _, rows, cols = w.shape

    def body(g_in, w_ref, m_ref, v_ref, g_ref, d_ref, nm_ref, nv_ref):
        g = g_in[...]
        g_ref[0] = g
        d_ref[0], nm_ref[0], nv_ref[0] = _adamw(w_ref[0], g, m_ref[0], v_ref[0])

    row = pl.BlockSpec((1, tr, cols), lambda i: (0, i, 0))
    return pl.pallas_call(
        body, name=name, grid=(rows // tr,),
        in_specs=[pl.BlockSpec((tr, cols), lambda i: (i, 0)), row, row, row], out_specs=[row] * 4,
        out_shape=[jax.ShapeDtypeStruct(w.shape, F32)] * 4,
        compiler_params=_params(("parallel",)),
    )(g, w, m, v)


def _rowsum_small(parts, loss8):
    def body(*refs):
        out = refs[-1]
        c0 = 0
        for ref in refs[:-1]:
            n = ref.shape[1]
            out[:, c0:c0 + n] = jnp.sum(ref[...], axis=0, keepdims=True)
            c0 += n

    return pl.pallas_call(body, name="rowsum_small", out_shape=jax.ShapeDtypeStruct((1, N_SMALL), F32))(*parts, loss8)


def _reduce_adamw_small(sbuf, ws, ms, vs):
    nv = len(ws)

    def body(*refs):
        s_ref, ins, outs = refs[0], refs[1:1 + 3 * nv], refs[1 + 3 * nv:]
        tot = s_ref[0]
        for s in range(1, N_DEV):
            tot = tot + s_ref[s]
        c0 = 0
        for i in range(nv):
            n = ins[i].shape[1]
            g = tot[:, c0:c0 + n]
            outs[i][...] = g
            outs[nv + i][...], outs[2 * nv + i][...], outs[3 * nv + i][...] = _adamw(
                ins[i][...], g, ins[nv + i][...], ins[2 * nv + i][...])
            c0 += n
        outs[-1][...] = tot[:, c0:]

    return pl.pallas_call(
        body, name="reduce_adamw_small",
        out_shape=[jax.ShapeDtypeStruct(a.shape, F32) for a in ws] * 4 + [jax.ShapeDtypeStruct((1, 128), F32)],
    )(sbuf, *ws, *ms, *vs)


_TRANSPOSED = ("w_in", "w_gate", "w_up")
_ROW_STACKED = MATS
_ADAM_TILE = {"w_in": 208, "w_out": 256, "w_gate": 176, "w_up": 176, "w_down": 176, "lora": 256}
_SUM_TILE = {"w_in": 208, "w_out": 128, "w_gate": 176, "w_up": 176, "w_down": 176, "lora": 128}


def _full(n, stacked):
    p, r, c = stacked.shape
    if n in _ROW_STACKED:
        return stacked.reshape(p * r, c)
    return jnp.transpose(stacked, (1, 0, 2)).reshape(r, p * c)


def _by_chip(n, full):
    if n in _ROW_STACKED:
        return full.reshape(N_CHIP, full.shape[0] // N_CHIP, full.shape[1])
    r, c = full.shape
    return jnp.transpose(full.reshape(r, N_CHIP, c // N_CHIP), (1, 0, 2))


def _with_own(land_shape, dtype, own, slot):
    return lax.dynamic_update_slice(lax.empty(land_shape, dtype), own[None], (slot,) + (0,) * own.ndim)


def kernel(x, mix_norm_g, w_in, mu_shift, decay_w0, decay_w2, iclr_a0, iclr_a2, gate_g2, k_k, k_a, r_k, ln_x_w, ln_x_b, attn_out_g, w_out, ffn_norm_g, w_gate, w_up, w_down, final_norm_g, loss_target, m_mix_norm_g, m_w_in, m_mu_shift, m_decay_w0, m_decay_w2, m_iclr_a0, m_iclr_a2, m_gate_g2, m_k_k, m_k_a, m_r_k, m_ln_x_w, m_ln_x_b, m_attn_out_g, m_w_out, m_ffn_norm_g, m_w_gate, m_w_up, m_w_down, m_final_norm_g, v_mix_norm_g, v_w_in, v_mu_shift, v_decay_w0, v_decay_w2, v_iclr_a0, v_iclr_a2, v_gate_g2, v_k_k, v_k_a, v_r_k, v_ln_x_w, v_ln_x_b, v_attn_out_g, v_w_out, v_ffn_norm_g, v_w_gate, v_w_up, v_w_down, v_final_norm_g):
    names = ("mix_norm_g", "w_in", "mu_shift", "decay_w0", "decay_w2", "iclr_a0", "iclr_a2", "gate_g2", "k_k", "k_a",
             "r_k", "ln_x_w", "ln_x_b", "attn_out_g", "w_out", "ffn_norm_g", "w_gate", "w_up", "w_down", "final_norm_g")
    w = dict(zip(names, (mix_norm_g, w_in, mu_shift, decay_w0, decay_w2, iclr_a0, iclr_a2, gate_g2, k_k, k_a, r_k,
                         ln_x_w, ln_x_b, attn_out_g, w_out, ffn_norm_g, w_gate, w_up, w_down, final_norm_g)))
    m = dict(zip(names, (m_mix_norm_g, m_w_in, m_mu_shift, m_decay_w0, m_decay_w2, m_iclr_a0, m_iclr_a2, m_gate_g2,
                         m_k_k, m_k_a, m_r_k, m_ln_x_w, m_ln_x_b, m_attn_out_g, m_w_out, m_ffn_norm_g, m_w_gate,
                         m_w_up, m_w_down, m_final_norm_g)))
    v = dict(zip(names, (v_mix_norm_g, v_w_in, v_mu_shift, v_decay_w0, v_decay_w2, v_iclr_a0, v_iclr_a2, v_gate_g2,
                         v_k_k, v_k_a, v_r_k, v_ln_x_w, v_ln_x_b, v_attn_out_g, v_w_out, v_ffn_norm_g, v_w_gate,
                         v_w_up, v_w_down, v_final_norm_g)))
    first = ("w_in", "lora")
    rest = ("w_out", "w_gate", "w_up", "w_down")
    xi, yi, ci = lax.axis_index("x"), lax.axis_index("y"), lax.axis_index("c")
    my_chip, my_dev = 2 * xi + yi, 4 * xi + 2 * yi + ci
    gather, scatter = ("gather",) * 4, ("scatter",) * 4

    def stored(d):
        out = {n: jnp.transpose(d[n][0]) if n in _TRANSPOSED else d[n][0] for n in MATS}
        out["lora"] = jnp.concatenate([d[n][0] for n in LORAS], axis=0)
        return out

    ws, ms, vs = stored(w), stored(m), stored(v)
    lora_rows = [(0, 64), (64, 128), (128, 256)]
    mine = [ws["w_in"].astype(BF16), ws["lora"]]
    early = _swap_start(None, [_with_own((N_CHIP,) + a.shape, a.dtype, a, my_chip) for a in mine], gather[:2],
                        "gather_first_start")
    wb = {n: (ws[n] + early[4][0, 0]).astype(BF16) for n in rest}
    lands = [_with_own((N_CHIP,) + wb[n].shape, BF16, wb[n], my_chip) for n in rest]
    ssem, rsem, srcs_thru, lands_thru, tok = _swap_start(None, lands, gather, "gather_rest_start")
    got = _swap_wait(early[0], early[1], early[2], early[3], tok, gather[:2], "gather_first_wait")
    win_all, lora_all = _swap_gathered(got, "gather_first_halves")
    win = _full("w_in", win_all)
    w2, a2, g2m = (_full(n, lora_all[:, a:b]) for n, (a, b) in zip(LORAS, lora_rows))

    vecs = {n: w[n].reshape(1, sz) for n, sz in VECS}
    vecs["mix_norm_g"] = vecs["mix_norm_g"] + tok[0, 0]

    def get_rest(after):
        halves = _swap_wait(ssem, rsem, srcs_thru, lands_thru, after, gather, "gather_rest_wait")
        return [_full(n, z) for n, z in zip(rest, _swap_gathered(halves, "gather_rest_halves"))]

    flight = []

    def my_half(g):
        h = g.shape[1] // 2
        return lax.dynamic_slice(g, (my_chip, ci * h, 0), (1, h, g.shape[2]))[0]

    def send_rest(gw):
        gs = [_by_chip(n, gw[n]).astype(BF16) for n in rest]
        into = [_with_own((N_DEV,) + my_half(g).shape, BF16, my_half(g), my_dev) for g in gs]
        flight.extend(_swap_start(gs, into, scatter, "exchange_rest_start"))
        return flight[4]

    loss8, dx, gw, gv = _local_step(x[0], loss_target[0], win, vecs, w2, a2, g2m, get_rest, send_rest)

    core = jnp.reshape(ci, (1,)).astype(jnp.int32)
    gs = [_by_chip("w_in", gw["w_in"]).astype(BF16),
          jnp.concatenate([_by_chip(n, gw[n]) for n in LORAS], axis=1).astype(BF16)]
    theirs = _sibling_halves(gs, "presum_halves")
    sums = [_add_halves(g, o, core, _SUM_TILE[n], "chipsum_" + n) for n, g, o in zip(first, gs, theirs)]
    own = [lax.dynamic_index_in_dim(s, my_chip, 0, keepdims=False) for s in sums]
    last = _swap_start(sums, [_with_own(s.shape, BF16, o, my_chip) for s, o in zip(sums, own)], ("chipsum",) * 2,
                       "exchange_first_start")
    small = _rowsum_small([gv[n] for n, _ in VECS], loss8 + last[4])
    vecs_out = _swap_start([small], [_with_own((N_DEV,) + small.shape, F32, small, my_dev)], ("all",),
                           "exchange_vectors_start")


    def update(group, rbufs, tag):
        sums = [_reduce8(rb, core, _SUM_TILE[n], "reduce_" + n) for n, rb in zip(group, rbufs)]
        gsum = _join_halves(sums, "join_halves_" + tag)
        out = {}
        for n, g in zip(group, gsum):
            r = _adamw_call(g, ws[n][None], ms[n][None], vs[n][None], _ADAM_TILE[n], "adamw_" + n)
            if n == "lora":
                for name, (a, b) in zip(LORAS, lora_rows):
                    out[name] = [z[:, a:b] for z in r]
            else:
                out[n] = [jnp.transpose(z[0])[None] for z in r] if n in _TRANSPOSED else r
        return out, r[1]

    res, done = update(rest, _swap_wait(flight[0], flight[1], flight[2], flight[3], vecs_out[4], scatter,
                                        "exchange_rest_wait"), "rest")
    got = _swap_wait(last[0], last[1], last[2], last[3], done, ("chipsum",) * 2, "exchange_first_wait")
    res_first, done = update(first, got, "first")
    res.update(res_first)
    sbuf = _swap_wait(vecs_out[0], vecs_out[1], vecs_out[2], vecs_out[3], done, ("all",), "exchange_vectors_wait")[0]
    rows = lambda d: [d[n].reshape(1, sz) for n, sz in VECS]
    small_res = _reduce_adamw_small(sbuf, rows(w), rows(m), rows(v))

    outs = []
    for k in range(4):
        piece = {n: r[k] for n, r in res.items()}
        for i, (n, _) in enumerate(VECS):
            piece[n] = small_res[k * len(VECS) + i].reshape(w[n].shape)
        outs.extend(piece[n] for n in names)
    return (small_res[-1][0, 0], dx[None], *outs)
```

```python
import jax
import jax.numpy as jnp
from jax import lax
from jax.experimental import pallas as pl
from jax.experimental.pallas import tpu as pltpu

F32 = jnp.float32
BF16 = jnp.bfloat16

D_MODEL = 1024
HEAD_DIM = 64
RW = 512
N_PAIR = RW // 128
SHIFT_COLS = 1792
IN_COLS = 3328
D_FF = 2816
FF_CHUNK = 256
NORM_EPS = 1e-6
GN_EPS = 64e-5
CHUNK = 64
SUB = 16
WKV_PASSES = 1
ATTN_PASSES = 1
ATTN_BLOCK = 128
DILATIONS = (1, 4, 16)
NEG = -1e30
ADAM_LR, ADAM_B1, ADAM_B2, ADAM_EPS, ADAM_WD, ADAM_STEP = 0.001, 0.9, 0.999, 1e-08, 0.01, 10
VMEM_LIMIT = 56 * 1024 * 1024
MESH = pl.DeviceIdType.MESH


def _params(sem=None, **kw):
    return pltpu.CompilerParams(dimension_semantics=sem, vmem_limit_bytes=VMEM_LIMIT, **kw)


def _dot(a, b):
    return lax.dot_general(a, b, (((1,), (0,)), ((), ())), preferred_element_type=F32)


def _dot_nt(a, b):
    return lax.dot_general(a, b, (((1,), (1,)), ((), ())), preferred_element_type=F32)


def _dot_tn(a, b):
    return lax.dot_general(a, b, (((0,), (0,)), ((), ())), preferred_element_type=F32)


_FORMS = {"nn": ((1,), (0,)), "nt": ((1,), (1,)), "tn": ((0,), (0,))}


def _dg(a, b, form):
    if a.ndim == 3 or b.ndim == 3:
        nb = a.shape[0] if a.ndim == 3 else b.shape[0]
        return jnp.stack([_dg(a[i] if a.ndim == 3 else a, b[i] if b.ndim == 3 else b, form) for i in range(nb)], axis=0)
    return lax.dot_general(a, b, (_FORMS[form], ((), ())), preferred_element_type=F32)


def _split2(x):
    hi = x.astype(BF16)
    return hi, (x - hi.astype(F32)).astype(BF16)


def _split3(x):
    hi = x.astype(BF16)
    rest = x - hi.astype(F32)
    mid = rest.astype(BF16)
    return hi, mid, (rest - mid.astype(F32)).astype(BF16)


def _mm_raw(a, b, form, mode):
    if mode == 1:
        return _dg(a.astype(BF16), b.astype(BF16), form)
    if mode == 3:
        ah, al = _split2(a)
        bh, bl = _split2(b)
        return _dg(ah, bh, form) + (_dg(ah, bl, form) + _dg(al, bh, form))
    if mode == "L3":
        ab = a.astype(BF16)
        b1, b2, b3 = _split3(b)
        if form == "nn":
            n = b.shape[-1]
            wide = _dg(ab, jnp.concatenate([b1, b2, b3], axis=-1), form)
            return wide[..., :n] + (wide[..., n:2 * n] + wide[..., 2 * n:])
        return _dg(ab, b1, form) + (_dg(ab, b2, form) + _dg(ab, b3, form))
    assert mode == "R3", mode
    bb = b.astype(BF16)
    a1, a2, a3 = _split3(a)
    if form in ("nn", "nt"):
        m = a.shape[-2]
        tall = _dg(jnp.concatenate([a1, a2, a3], axis=-2), bb, form)
        return tall[..., :m, :] + (tall[..., m:2 * m, :] + tall[..., 2 * m:, :])
    return _dg(a1, bb, form) + (_dg(a2, bb, form) + _dg(a3, bb, form))


def _mm(a, b, form, mode):
    @jax.custom_vjp
    def f(a, b):
        return _mm_raw(a, b, form, mode)

    def fwd(a, b):
        return _mm_raw(a, b, form, mode), (a, b)

    def bwd(res, ct):
        a, b = res
        la = {1: 1, 3: 3, "L3": None, "R3": "R3"}[mode]
        lb = {1: 1, 3: 3, "L3": "L3", "R3": None}[mode]
        if form == "nn":
            da = None if la is None else _mm_raw(ct, b, "nt", la)
            db = None if lb is None else _mm_raw(a, ct, "tn", lb)
        elif form == "nt":
            da = None if la is None else _mm_raw(ct, b, "nn", la)
            db = None if lb is None else _mm_raw(ct, a, "tn", "R3" if lb == "L3" else lb)
        else:
            da = None if la is None else _mm_raw(b, ct, "nt", "L3" if la == "R3" else la)
            db = None if lb is None else _mm_raw(a, ct, "nn", lb)
        return (jnp.zeros_like(a) if da is None else da, jnp.zeros_like(b) if db is None else db)

    f.defvjp(fwd, bwd)
    return f(a, b)


def _seg_ones(n):
    r = lax.broadcasted_iota(jnp.int32, (n, n), 0) // HEAD_DIM
    c = lax.broadcasted_iota(jnp.int32, (n, n), 1) // HEAD_DIM
    return (r == c).astype(F32)


def _segsum(x, seg):
    return _mm(x, seg, "nn", "R3")


def _rms_fwd(x, g):
    rstd = lax.rsqrt(jnp.mean(x * x, axis=-1, keepdims=True) + NORM_EPS)
    return x * rstd * g


def _rms_bwd(dy, x, g):
    rstd = lax.rsqrt(jnp.mean(x * x, axis=-1, keepdims=True) + NORM_EPS)
    xn = x * rstd
    dxn = dy * g
    dx = rstd * (dxn - xn * jnp.mean(dxn * xn, axis=-1, keepdims=True))
    return dx, dy * xn


def _sigmoid(x):
    return 1.0 / (1.0 + jnp.exp(-x))


def _softplus(x):
    return jnp.maximum(x, 0.0) + jnp.log(1.0 + jnp.exp(-jnp.abs(x)))


def _acc(ref, val, first):
    @pl.when(first)
    def _():
        ref[...] = val

    @pl.when(jnp.logical_not(first))
    def _():
        ref[...] += val


def _colsum8(v):
    rows, n = v.shape
    return jnp.sum(v.reshape(rows // 8, 8, n), axis=0)


def _prep_fn(p, pprev, mu, w0, w2p, a0, a2p, g2, k_k, k_a):
    seg = _seg_ones(RW)
    ps = p + (pprev - p) * mu
    r = ps[:, 0:RW]
    k = ps[:, RW:2 * RW]
    v = ps[:, 2 * RW:3 * RW]
    xwa = ps[:, 3 * RW:3 * RW + 128]
    xg = ps[:, 3 * RW + 128:3 * RW + 256]
    wraw = -_softplus(-(w0 + _mm(jnp.tanh(xwa), w2p, "nn", 3))) - 0.5
    lw = -jnp.exp(wraw)
    a = _sigmoid(a0 + _mm(xwa, a2p, "nn", 3))
    g = _mm(_sigmoid(xg), g2, "nn", 3)
    kk = k * k_k
    kk = kk / jnp.maximum(jnp.sqrt(_segsum(kk * kk, seg)), 1e-12)
    k2 = k * (1.0 + (a - 1.0) * k_a)
    return r, lw, k2, v, kk, a, g


def _transposed(z):
    return jnp.stack([z[i].T for i in range(z.shape[0])], axis=0) if z.ndim == 3 else z.T


def _solve_unit_lower(lmat, rhs):
    c = lmat.shape[-1]
    row = lax.broadcasted_iota(jnp.int32, (c, c), 0)
    col = lax.broadcasted_iota(jnp.int32, (c, c), 1)
    eye = (row == col).astype(F32)
    ld = jnp.where(row // SUB == col // SUB, lmat, 0.0)
    lo = lmat - ld
    x = eye + ld
    m = ld
    mm = lambda p, q: _mm(p, q, "nn", WKV_PASSES)
    cat = jnp.concatenate
    m = mm(m, m)
    for _ in range(2):
        mx = mm(m, cat([m, x], axis=-1))
        m, x = mx[..., :c], x + mx[..., c:]
    x = x + mm(m, x)
    gw = mm(x, cat([lo, rhs], axis=-1))
    g, w = gw[..., :c], gw[..., c:]
    gg = mm(g, cat([g, w], axis=-1))
    w = w + gg[..., c:]
    return w + mm(gg[..., :c], w)


def _wkv_chunk_fn(s0, r, lw, k, v, kk, a):
    c = r.shape[-2]
    n = 2 * c
    row = lax.broadcasted_iota(jnp.int32, (n, n), 0)
    col = lax.broadcasted_iota(jnp.int32, (n, n), 1)
    same = (row // c) == (col // c)
    incl = jnp.logical_and(row >= col, same)
    strict = jnp.logical_and(row > col, same)
    sel = (lax.broadcasted_iota(jnp.int32, (n, 128), 0) // c) == (lax.broadcasted_iota(jnp.int32, (n, 128), 1) // HEAD_DIM)
    two = lambda z: jnp.concatenate([z, z], axis=-2)
    lw2 = two(lw)
    mm = lambda p_, q_, form: _mm(p_, q_, form, WKV_PASSES)
    cl = _mm(incl.astype(F32), lw2, "nn", "L3")
    p = jnp.exp(cl)
    pinv = jnp.exp(-cl)
    pprev = jnp.exp(cl - lw2)
    kk2 = two(kk)
    at = jnp.where(sel, -kk2 * pprev, 0.0)
    bt = jnp.where(sel, kk2 * two(a) * pinv, 0.0)
    kt = jnp.where(sel, two(k) * pinv, 0.0)
    rt = jnp.where(sel, two(r) * p, 0.0)
    vt = jnp.where(sel, two(v), 0.0)
    cat = jnp.concatenate
    bk = cat([bt, kt], axis=-2)
    arbk = mm(cat([at, rt], axis=-2), bk, "nt")
    ab, ak = jnp.where(strict, arbk[..., :n, :n], 0.0), jnp.where(strict, arbk[..., :n, n:], 0.0)
    rb, rk = jnp.where(incl, arbk[..., n:, :n], 0.0), jnp.where(incl, arbk[..., n:, n:], 0.0)
    s0t = _transposed(s0)
    u = _solve_unit_lower(ab, mm(cat([at, ak], axis=-1), cat([s0t, vt], axis=-2), "nn"))
    y2 = mm(cat([rt, rb, rk], axis=-1), cat([s0t, u, vt], axis=-2), "nn")
    plast = jnp.exp(jnp.sum(lw, axis=-2, keepdims=True))
    s1 = (s0 + mm(cat([u, vt], axis=-2), bk, "tn")) * plast
    r2 = lax.broadcasted_iota(jnp.int32, (128, 128), 0) // HEAD_DIM
    c2 = lax.broadcasted_iota(jnp.int32, (128, 128), 1) // HEAD_DIM
    return y2[..., :c, :] + y2[..., c:, :], jnp.where(r2 == c2, s1, 0.0)


def _post_fn(y, r, k2, v, g, lnw, lnb, rk):
    seg = _seg_ones(RW)
    mean = _segsum(y, seg) * (1.0 / HEAD_DIM)
    yc = y - mean
    var = _segsum(yc * yc, seg) * (1.0 / HEAD_DIM)
    yn = yc * lax.rsqrt(var + GN_EPS)
    out = yn * lnw + lnb + _segsum(r * k2 * rk, seg) * v
    return out * g


def _attn_block_fn(q, kc, vc, kp=None, vp=None):
    n = ATTN_BLOCK
    qi = lax.broadcasted_iota(jnp.int32, (n, n), 0)
    kj = lax.broadcasted_iota(jnp.int32, (n, n), 1)
    lane = lax.broadcasted_iota(jnp.int32, (1, 128), 1)
    scale = HEAD_DIM ** -0.5
    valid = kj <= qi
    keys, vals = kc, vc
    if kp is not None:
        valid = jnp.concatenate([valid, kj >= qi], axis=-1)
        keys, vals = jnp.concatenate([kc, kp], axis=-2), jnp.concatenate([vc, vp], axis=-2)
    m0 = (lane // HEAD_DIM) == 0
    q2 = jnp.concatenate([jnp.where(m0, q, 0.0), jnp.where(m0, 0.0, q)], axis=-2)
    valid2 = jnp.concatenate([valid, valid], axis=-2)
    s = jnp.where(valid2, _mm(q2, keys, "nt", ATTN_PASSES) * scale, NEG)
    m = jnp.max(s, axis=-1, keepdims=True)
    p = jnp.exp(s - m)
    den = jnp.sum(p, axis=-1, keepdims=True)
    o2 = _mm(p, vals, "nn", ATTN_PASSES) / den
    l2 = m + jnp.log(den)
    return jnp.where(m0, o2[..., :n, :], o2[..., n:, :]), jnp.where(m0, l2[..., :n, :], l2[..., n:, :])


def _attn_block_bwd(q, kc, vc, kp, vp, o, lse, do, dl):
    n = ATTN_BLOCK
    cat = jnp.concatenate
    qi = lax.broadcasted_iota(jnp.int32, (n, n), 0)
    kj = lax.broadcasted_iota(jnp.int32, (n, n), 1)
    m0 = (lax.broadcasted_iota(jnp.int32, (1, 128), 1) // HEAD_DIM) == 0
    scale = HEAD_DIM ** -0.5
    valid = kj <= qi
    keys, vals = kc, vc
    if kp is not None:
        valid = cat([valid, kj >= qi], axis=-1)
        keys, vals = cat([kc, kp], axis=-2), cat([vc, vp], axis=-2)
    stack = lambda z: cat([jnp.where(m0, z, 0.0), jnp.where(m0, 0.0, z)], axis=-2)
    q2, do2 = stack(q), stack(do)
    lse2 = cat([jnp.max(jnp.where(m0, lse, NEG), axis=-1, keepdims=True),
                jnp.max(jnp.where(m0, NEG, lse), axis=-1, keepdims=True)], axis=-2)
    delta = jnp.sum(do2 * cat([o, o], axis=-2), axis=-1, keepdims=True)
    dlse = jnp.sum(stack(dl), axis=-1, keepdims=True)
    mm = lambda a, b, form: _mm_raw(a, b, form, ATTN_PASSES)
    s = jnp.where(cat([valid, valid], axis=-2), mm(q2, keys, "nt") * scale, NEG)
    p = jnp.exp(s - lse2)
    ds = p * (mm(do2, vals, "nt") - delta + dlse)
    dq2 = mm(ds, keys, "nn") * scale
    dq = jnp.where(m0, dq2[..., :n, :], dq2[..., n:, :])
    dkeys = mm(ds, q2, "tn") * scale
    dvals = mm(p, do2, "tn")
    if kp is None:
        return dq, dkeys, dvals
    return dq, dkeys[..., :n, :], dvals[..., :n, :], dkeys[..., n:, :], dvals[..., n:, :]


def _combine_fn(o1, o2, o3, l1, l2, l3, og):
    seg = _seg_ones(o1.shape[-1])
    m = jnp.maximum(jnp.maximum(l1, l2), l3)
    e1, e2, e3 = jnp.exp(l1 - m), jnp.exp(l2 - m), jnp.exp(l3 - m)
    o = (e1 * o1 + e2 * o2 + e3 * o3) / (e1 + e2 + e3)
    o = o * lax.rsqrt(_segsum(o * o, seg) * (1.0 / HEAD_DIM) + NORM_EPS)
    return o * og


def _shifted(p, last8, first):
    prow = jnp.where(first, 0.0, last8[7:8, :])
    rolled = pltpu.roll(p, 1, axis=0)
    rid = lax.broadcasted_iota(jnp.int32, p.shape, 0)
    return jnp.where(rid == 0, prow, rolled)


_PREP_TM = 256


def _prep_specs(tm):
    vec = lambda n: pl.BlockSpec((1, n), lambda i: (0, 0))
    mat = lambda r, n: pl.BlockSpec((r, n), lambda i: (0, 0))
    return [vec(SHIFT_COLS), vec(RW), mat(128, RW), vec(RW), mat(128, RW), mat(128, RW), vec(RW), vec(RW)]


def _in_proj_prep(x, g1, win, pw):
    t = x.shape[0]
    tm = _PREP_TM

    def body(x_ref, g_ref, w_ref, mu, w0, w2p, a0, a2p, g2, k_k, k_a, h_ref, pa_ref, qkv_ref, *rest):
        outs, carry = rest[:7], rest[7]

        @pl.when(pl.program_id(0) == 0)
        def _():
            carry[...] = jnp.zeros_like(carry)

        h = _rms_fwd(x_ref[...], g_ref[...]).astype(BF16)
        h_ref[...] = h
        proj = _dot_nt(h, w_ref[...])
        p = proj[:, :SHIFT_COLS]
        pa_ref[...] = p
        for j in range(3):
            for pr in range(N_PAIR):
                c0 = SHIFT_COLS + j * RW + pr * 128
                qkv_ref[j, pr] = proj[:, c0:c0 + 128]
        pprev = _shifted(p, carry[...], pl.program_id(0) == 0)
        carry[...] = p[tm - 8:, :]
        res = _prep_fn(p, pprev, mu[...], w0[...], w2p[...], a0[...], a2p[...], g2[...], k_k[...], k_a[...])
        for o_ref, val in zip(outs, res):
            o_ref[...] = val

    row = pl.BlockSpec((tm, RW), lambda i: (i, 0))
    return pl.pallas_call(
        body, name="in_proj_prep", grid=(t // tm,),
        in_specs=[pl.BlockSpec((tm, D_MODEL), lambda i: (i, 0)), pl.BlockSpec((1, D_MODEL), lambda i: (0, 0)),
                  pl.BlockSpec((IN_COLS, D_MODEL), lambda i: (0, 0))] + _prep_specs(tm),
        out_specs=[pl.BlockSpec((tm, D_MODEL), lambda i: (i, 0)), pl.BlockSpec((tm, SHIFT_COLS), lambda i: (i, 0)),
                   pl.BlockSpec((3, N_PAIR, tm, 128), lambda i: (0, 0, i, 0))] + [row] * 7,
        out_shape=[jax.ShapeDtypeStruct((t, D_MODEL), BF16), jax.ShapeDtypeStruct((t, SHIFT_COLS), F32),
                   jax.ShapeDtypeStruct((3, N_PAIR, t, 128), F32)] + [jax.ShapeDtypeStruct((t, RW), F32)] * 7,
        scratch_shapes=[pltpu.VMEM((8, SHIFT_COLS), F32)],
        compiler_params=_params(("arbitrary",)),
    )(x, g1, win, *pw)


def _pairs(ref):
    return jnp.stack([ref[:, 128 * p:128 * (p + 1)] for p in range(N_PAIR)], axis=0)


def _wkv_fwd(r, lw, k2, v, kk, a):
    t = r.shape[0]
    nc = t // CHUNK

    def body(r_ref, lw_ref, k_ref, v_ref, kk_ref, a_ref, y_ref, s_ref, st):
        @pl.when(pl.program_id(0) == 0)
        def _():
            st[...] = jnp.zeros_like(st)

        s0 = st[...]
        s_ref[0] = s0
        y, s1 = _wkv_chunk_fn(s0, *[_pairs(ref) for ref in (r_ref, lw_ref, k_ref, v_ref, kk_ref, a_ref)])
        for p in range(N_PAIR):
            y_ref[:, 128 * p:128 * (p + 1)] = y[p]
        st[...] = s1

    blk = pl.BlockSpec((CHUNK, RW), lambda c: (c, 0))
    return pl.pallas_call(
        body, name="wkv_fwd", grid=(nc,),
        in_specs=[blk] * 6,
        out_specs=[blk, pl.BlockSpec((1, N_PAIR, 128, 128), lambda c: (c, 0, 0, 0))],
        out_shape=[jax.ShapeDtypeStruct((t, RW), F32), jax.ShapeDtypeStruct((nc, N_PAIR, 128, 128), F32)],
        scratch_shapes=[pltpu.VMEM((N_PAIR, 128, 128), F32)],
        compiler_params=_params(("arbitrary",)),
    )(r, lw, k2, v, kk, a)


_POST_TM = 512


ATTN_GROUP = 2


def _dilated_rows(d, r, n):
    if d == 1:
        return pl.ds(pl.multiple_of(n * ATTN_BLOCK, ATTN_BLOCK), ATTN_BLOCK)
    return pl.ds(r + n * (ATTN_BLOCK * d), ATTN_BLOCK, stride=d)


def _for_each_sequence(t, unit):
    for di, d in enumerate(DILATIONS):

        @pl.when(pl.program_id(1) == di)
        def _(di=di, d=d):
            nb = t // (ATTN_BLOCK * d)
            if d == 1:
                unit(di, [(d, 0, 0)], False)
                unit(di, [(d, 0, 1)], True)
                lax.fori_loop(1, nb // 2, lambda k, c: (unit(di, [(d, 0, 2 * k), (d, 0, 2 * k + 1)], True), c)[1], 0)
            else:

                def residues(r, carry):
                    unit(di, [(d, r, 0), (d, r + d // 2, 0)], False)
                    if nb > 1:
                        lax.fori_loop(1, nb, lambda n, c: (unit(di, [(d, r, n), (d, r + d // 2, n)], True), c)[1], 0)
                    return carry

                lax.fori_loop(0, d // 2, residues, 0)


def _take(ref, lead, rows_list):
    return jnp.stack([ref.at[(*lead, g)][rows, :] for rows in rows_list for g in range(ref.shape[len(lead)])], axis=0)


def _put(ref, lead, rows_list, val, add=False):
    k = 0
    for rows in rows_list:
        for g in range(ref.shape[len(lead)]):
            if add:
                ref.at[(*lead, g)][rows, :] += val[k]
            else:
                ref.at[(*lead, g)][rows, :] = val[k]
            k += 1


def _attn_fwd(qkv):
    t = qkv.shape[2]

    def body(q_ref, k_ref, v_ref, o_ref, l_ref):
        def unit(di, places, has_prev):
            cur = [_dilated_rows(d, r, n) for d, r, n in places]
            args = [_take(ref, (0,), cur) for ref in (q_ref, k_ref, v_ref)]
            if has_prev:
                prv = [_dilated_rows(d, r, n - 1) for d, r, n in places]
                args += [_take(ref, (0,), prv) for ref in (k_ref, v_ref)]
            o, lse = _attn_block_fn(*args)
            _put(o_ref, (0,), cur, o)
            _put(l_ref, (0,), cur, lse)

        _for_each_sequence(t, unit)

    spec = lambda j: pl.BlockSpec((1, ATTN_GROUP, t, 128), lambda i, b: (j, i, 0, 0))
    out = pl.BlockSpec((1, ATTN_GROUP, t, 128), lambda i, b: (b, i, 0, 0))
    return pl.pallas_call(
        body, name="attn_fwd", grid=(N_PAIR // ATTN_GROUP, len(DILATIONS)),
        in_specs=[spec(0), spec(1), spec(2)], out_specs=[out, out],
        out_shape=[jax.ShapeDtypeStruct((3, N_PAIR, t, 128), F32)] * 2,
        compiler_params=_params(("parallel", "arbitrary")),
    )(qkv, qkv, qkv)


_COMB_TM = 512


def _mixers_out(y, r, k2, v, g, lnw, lnb, rk, o, l, og):
    t = y.shape[0]
    tm = _COMB_TM

    def body(y_ref, r_ref, k_ref, v_ref, g_ref, lnw_ref, lnb_ref, rk_ref, o_ref, l_ref, og_ref, out_ref):
        out_ref[:, :RW] = _post_fn(y_ref[...], r_ref[...], k_ref[...], v_ref[...], g_ref[...],
                                   lnw_ref[...], lnb_ref[...], rk_ref[...]).astype(BF16)
        for p in range(N_PAIR):
            cols = slice(128 * p, 128 * (p + 1))
            out_ref[:, RW + 128 * p:RW + 128 * (p + 1)] = _combine_fn(
                o_ref[0, p], o_ref[1, p], o_ref[2, p], l_ref[0, p], l_ref[1, p], l_ref[2, p], og_ref[:, cols]).astype(BF16)

    row = pl.BlockSpec((tm, RW), lambda i: (i, 0))
    vec = pl.BlockSpec((1, RW), lambda i: (0, 0))
    blk = pl.BlockSpec((3, N_PAIR, tm, 128), lambda i: (0, 0, i, 0))
    return pl.pallas_call(
        body, name="mixers_out", grid=(t // tm,),
        in_specs=[row] * 5 + [vec] * 3 + [blk, blk, vec], out_specs=pl.BlockSpec((tm, D_MODEL), lambda i: (i, 0)),
        out_shape=jax.ShapeDtypeStruct((t, D_MODEL), BF16),
        compiler_params=_params(("parallel",)),
    )(y, r, k2, v, g, lnw, lnb, rk, o, l, og)


def _ffn_all(x, ycat, wg, wu, wd, wout, g2, gf, tgt):
    t = x.shape[0]
    tm = 256

    def body(x_ref, y_ref, wg_ref, wu_ref, wd_ref, wo_ref, g2_ref, gf_ref, t_ref,
             h_ref, act_ref, dx2b_ref, dgt_ref, dup_ref, dx1b_ref, dx1_ref, dya_ref, dyb_ref, loss_ref, dgf_ref, dg2_ref,
             gt_s, up_s):
        first = pl.program_id(0) == 0
        x1 = x_ref[...] + _dot(y_ref[...], wo_ref[...])
        h = _rms_fwd(x1, g2_ref[...]).astype(BF16)
        h_ref[...] = h
        for c0 in range(0, D_FF, FF_CHUNK):
            cols = slice(c0, c0 + FF_CHUNK)
            gt = _dot_nt(h, wg_ref[cols, :])
            up = _dot_nt(h, wu_ref[cols, :])
            gt_s[:, cols] = gt.astype(BF16)
            up_s[:, cols] = up.astype(BF16)
            act_ref[:, cols] = (gt * _sigmoid(gt) * up).astype(BF16)
        x2 = x1 + _dot(act_ref[...], wd_ref[...])
        gf_ = gf_ref[...]
        diff = _rms_fwd(x2, gf_) - t_ref[...]
        lrow = 0.5 * jnp.sum(_colsum8(diff * diff), axis=1, keepdims=True) * (1.0 / D_MODEL)
        _acc(loss_ref, jnp.broadcast_to(lrow, (8, 128)), first)
        dx2, dgr = _rms_bwd(diff * (1.0 / D_MODEL), x2, gf_)
        _acc(dgf_ref, _colsum8(dgr), first)
        dx2b = dx2.astype(BF16)
        dx2b_ref[...] = dx2b
        for c0 in range(0, D_FF, FF_CHUNK):
            cols = slice(c0, c0 + FF_CHUNK)
            dact = _dot_nt(dx2b, wd_ref[cols, :])
            gt = gt_s[:, cols].astype(F32)
            sg = _sigmoid(gt)
            dgt_ref[:, cols] = (dact * up_s[:, cols].astype(F32) * sg * (1.0 + gt * (1.0 - sg))).astype(BF16)
            dup_ref[:, cols] = (dact * gt * sg).astype(BF16)
        dh = _dot(dgt_ref[...], wg_ref[...]) + _dot(dup_ref[...], wu_ref[...])
        dxn, dgr2 = _rms_bwd(dh, x1, g2_ref[...])
        _acc(dg2_ref, _colsum8(dgr2), first)
        dx1 = dx2 + dxn
        dx1_ref[...] = dx1
        dx1b = dx1.astype(BF16)
        dx1b_ref[...] = dx1b
        dy = _dot_nt(dx1b, wo_ref[...])
        dya_ref[...] = dy[:, :RW]
        dyb_ref[...] = dy[:, RW:]

    row = pl.BlockSpec((tm, D_MODEL), lambda i: (i, 0))
    wide = pl.BlockSpec((tm, D_FF), lambda i: (i, 0))
    half = pl.BlockSpec((tm, RW), lambda i: (i, 0))
    wsp = pl.BlockSpec((D_FF, D_MODEL), lambda i: (0, 0))
    vec = pl.BlockSpec((1, D_MODEL), lambda i: (0, 0))
    part = pl.BlockSpec((8, D_MODEL), lambda i: (0, 0))
    bf = lambda n: jax.ShapeDtypeStruct((t, n), BF16)
    return pl.pallas_call(
        body, name="ffn_all", grid=(t // tm,),
        in_specs=[row, row, wsp, wsp, wsp, pl.BlockSpec((D_MODEL, D_MODEL), lambda i: (0, 0)), vec, vec, row],
        out_specs=[row, wide, row, wide, wide, row, row, half, half, pl.BlockSpec((8, 128), lambda i: (0, 0)), part, part],
        out_shape=[bf(D_MODEL), bf(D_FF), bf(D_MODEL), bf(D_FF), bf(D_FF), bf(D_MODEL),
                   jax.ShapeDtypeStruct((t, D_MODEL), F32), jax.ShapeDtypeStruct((t, RW), F32),
                   jax.ShapeDtypeStruct((t, RW), F32), jax.ShapeDtypeStruct((8, 128), F32),
                   jax.ShapeDtypeStruct((8, D_MODEL), F32), jax.ShapeDtypeStruct((8, D_MODEL), F32)],
        scratch_shapes=[pltpu.VMEM((tm, D_FF), BF16), pltpu.VMEM((tm, D_FF), BF16)],
        compiler_params=_params(("arbitrary",)),
    )(x, ycat, wg, wu, wd, wout, g2, gf, tgt)


def _wgrad(a, b, tk, tn, name):
    t, kdim = a.shape
    ndim = b.shape[1]

    def body(a_ref, b_ref, o_ref):
        o_ref[...] = _dot_tn(a_ref[...], b_ref[...])

    return pl.pallas_call(
        body, name=name, grid=(kdim // tk, ndim // tn),
        in_specs=[pl.BlockSpec((t, tk), lambda i, j: (0, i)), pl.BlockSpec((t, tn), lambda i, j: (0, j))],
        out_specs=pl.BlockSpec((tk, tn), lambda i, j: (i, j)),
        out_shape=jax.ShapeDtypeStruct((kdim, ndim), F32),
        compiler_params=_params(("parallel", "parallel")),
    )(a, b)


def _post_bwd(dya, y, r, k2, v, g, lnw, lnb, rk):
    t = y.shape[0]
    tm = _POST_TM

    def body(d_ref, y_ref, r_ref, k_ref, v_ref, g_ref, lnw_ref, lnb_ref, rk_ref,
             dy_ref, dr_ref, dk_ref, dv_ref, dg_ref, dlnw_ref, dlnb_ref, drk_ref):
        first = pl.program_id(0) == 0
        ones = jnp.ones((tm, 1), F32)
        prim = (y_ref[...], r_ref[...], k_ref[...], v_ref[...], g_ref[...],
                ones * lnw_ref[...], ones * lnb_ref[...], ones * rk_ref[...])
        _, vjp = jax.vjp(_post_fn, *prim)
        dy, dr, dk, dv, dg, dlnw, dlnb, drk = vjp(d_ref[...])
        dy_ref[...] = dy
        dr_ref[...] = dr
        dk_ref[...] = dk
        dv_ref[...] = dv
        dg_ref[...] = dg
        _acc(dlnw_ref, _colsum8(dlnw), first)
        _acc(dlnb_ref, _colsum8(dlnb), first)
        _acc(drk_ref, _colsum8(drk), first)

    row = pl.BlockSpec((tm, RW), lambda i: (i, 0))
    vec = pl.BlockSpec((1, RW), lambda i: (0, 0))
    part = pl.BlockSpec((8, RW), lambda i: (0, 0))
    return pl.pallas_call(
        body, name="rwkv_post_bwd", grid=(t // tm,),
        in_specs=[row] * 6 + [vec] * 3, out_specs=[row] * 5 + [part] * 3,
        out_shape=[jax.ShapeDtypeStruct((t, RW), F32)] * 5 + [jax.ShapeDtypeStruct((8, RW), F32)] * 3,
        compiler_params=_params(("arbitrary",)),
    )(dya, y, r, k2, v, g, lnw, lnb, rk)


def _wkv_bwd(dy, s0s, r, lw, k2, v, kk, a):
    t = r.shape[0]
    nc = t // CHUNK

    def body(dy_ref, s_ref, r_ref, lw_ref, k_ref, v_ref, kk_ref, a_ref,
             dr_ref, dlw_ref, dk_ref, dv_ref, dkk_ref, da_ref, ds):
        @pl.when(pl.program_id(0) == 0)
        def _():
            ds[...] = jnp.zeros_like(ds)

        _, vjp = jax.vjp(_wkv_chunk_fn, s_ref[0],
                         *[_pairs(ref) for ref in (r_ref, lw_ref, k_ref, v_ref, kk_ref, a_ref)])
        res = vjp((_pairs(dy_ref), ds[...]))
        ds[...] = res[0]
        for ref, val in zip((dr_ref, dlw_ref, dk_ref, dv_ref, dkk_ref, da_ref), res[1:]):
            for p in range(N_PAIR):
                ref[:, 128 * p:128 * (p + 1)] = val[p]

    blk = pl.BlockSpec((CHUNK, RW), lambda c: (nc - 1 - c, 0))
    return pl.pallas_call(
        body, name="wkv_bwd", grid=(nc,),
        in_specs=[blk, pl.BlockSpec((1, N_PAIR, 128, 128), lambda c: (nc - 1 - c, 0, 0, 0))] + [blk] * 6,
        out_specs=[blk] * 6,
        out_shape=[jax.ShapeDtypeStruct((t, RW), F32)] * 6,
        scratch_shapes=[pltpu.VMEM((N_PAIR, 128, 128), F32)],
        compiler_params=_params(("arbitrary",)),
    )(dy, s0s, r, lw, k2, v, kk, a)


def _prep_in_proj_bwd(proj, pw, douts, dq, dk, dv, win, x, g1, dx1):
    t = proj.shape[0]
    tm = _PREP_TM
    nt = t // tm

    def body(p_ref, l8_ref, mu, w0, w2p, a0, a2p, g2, k_k, k_a, dr, dr2, dlw, dk2, dk22, dv, dv2, dkk, da, dg,
             dq_ref, dkq_ref, dvq_ref, w_ref, x_ref, g1_ref, dx1_ref,
             dproj_ref, dx_ref, dg1_ref, dmu_ref, dw0_ref, dw2_ref, da0_ref, da2_ref, dg2_ref, dkk_ref, dka_ref, carry):
        i = pl.program_id(0)
        first = i == 0

        @pl.when(first)
        def _():
            carry[...] = jnp.zeros_like(carry)

        p = p_ref[...]
        pprev = _shifted(p, l8_ref[...], i == nt - 1)
        ones = jnp.ones((tm, 1), F32)
        prim = (p, pprev, ones * mu[...], ones * w0[...], w2p[...], ones * a0[...], a2p[...], g2[...],
                ones * k_k[...], ones * k_a[...])
        _, vjp = jax.vjp(_prep_fn, *prim)
        dp, dpp, dmu, dw0, dw2, da0, da2, dg2, dkk_, dka = vjp(
            (dr[...] + dr2[...], dlw[...], dk2[...] + dk22[...], dv[...] + dv2[...], dkk[...], da[...], dg[...]))
        up = pltpu.roll(dpp, tm - 1, axis=0)
        rid = lax.broadcasted_iota(jnp.int32, dpp.shape, 0)
        dpa = dp + jnp.where(rid == tm - 1, carry[0:1, :], up)
        carry[...] = jnp.broadcast_to(dpp[0:1, :], carry.shape)
        _acc(dmu_ref, _colsum8(dmu), first)
        _acc(dw0_ref, _colsum8(dw0), first)
        _acc(dw2_ref, dw2, first)
        _acc(da0_ref, _colsum8(da0), first)
        _acc(da2_ref, da2, first)
        _acc(dg2_ref, dg2, first)
        _acc(dkk_ref, _colsum8(dkk_), first)
        _acc(dka_ref, _colsum8(dka), first)
        parts = [dpa] + [ref[pr] for ref in (dq_ref, dkq_ref, dvq_ref) for pr in range(N_PAIR)]
        dproj = jnp.concatenate([z.astype(BF16) for z in parts], axis=1)
        dproj_ref[...] = dproj
        dxn, dgr = _rms_bwd(_dot(dproj, w_ref[...]), x_ref[...], g1_ref[...])
        dx_ref[...] = dx1_ref[...] + dxn
        _acc(dg1_ref, _colsum8(dgr), first)

    rev = lambda i: (nt - 1 - i, 0)
    row = pl.BlockSpec((tm, RW), rev)
    wide = pl.BlockSpec((tm, D_MODEL), rev)
    pair = pl.BlockSpec((N_PAIR, tm, 128), lambda i: (0, nt - 1 - i, 0))
    part = lambda n: pl.BlockSpec((8, n), lambda i: (0, 0))
    mat = pl.BlockSpec((128, RW), lambda i: (0, 0))
    return pl.pallas_call(
        body, name="prep_in_proj_bwd", grid=(nt,),
        in_specs=[pl.BlockSpec((tm, SHIFT_COLS), rev),
                  pl.BlockSpec((8, SHIFT_COLS), lambda i: (jnp.maximum((nt - 1 - i) * (tm // 8) - 1, 0), 0))]
                 + _prep_specs(tm) + [row] * 10
                 + [pair] * 3 + [pl.BlockSpec((IN_COLS, D_MODEL), lambda i: (0, 0)), wide,
                                 pl.BlockSpec((1, D_MODEL), lambda i: (0, 0)), wide],
        out_specs=[pl.BlockSpec((tm, IN_COLS), rev), wide, part(D_MODEL), part(SHIFT_COLS), part(RW), mat, part(RW), mat,
                   mat, part(RW), part(RW)],
        out_shape=[jax.ShapeDtypeStruct((t, IN_COLS), BF16), jax.ShapeDtypeStruct((t, D_MODEL), F32),
                   jax.ShapeDtypeStruct((8, D_MODEL), F32), jax.ShapeDtypeStruct((8, SHIFT_COLS), F32),
                   jax.ShapeDtypeStruct((8, RW), F32), jax.ShapeDtypeStruct((128, RW), F32),
                   jax.ShapeDtypeStruct((8, RW), F32), jax.ShapeDtypeStruct((128, RW), F32),
                   jax.ShapeDtypeStruct((128, RW), F32), jax.ShapeDtypeStruct((8, RW), F32),
                   jax.ShapeDtypeStruct((8, RW), F32)],
        scratch_shapes=[pltpu.VMEM((8, SHIFT_COLS), F32)],
        compiler_params=_params(("arbitrary",)),
    )(proj, proj, *pw, *douts, dq, dk, dv, win, x, g1, dx1)


def _combine_bwd(dyb, o, l, og):
    t = dyb.shape[0]
    tm = _COMB_TM

    def body(d_ref, o_ref, l_ref, og_ref, do_ref, dl_ref, dog_ref):
        ones = jnp.ones((tm, 1), F32)
        dog = []
        for p in range(N_PAIR):
            cols = slice(128 * p, 128 * (p + 1))
            _, vjp = jax.vjp(_combine_fn, o_ref[0, p], o_ref[1, p], o_ref[2, p], l_ref[0, p], l_ref[1, p], l_ref[2, p],
                             ones * og_ref[:, cols])
            res = vjp(d_ref[:, cols])
            for b in range(3):
                do_ref[b, p] = res[b]
                dl_ref[b, p] = res[3 + b]
            dog.append(_colsum8(res[6]))
        _acc(dog_ref, jnp.concatenate(dog, axis=1), pl.program_id(0) == 0)

    blk = pl.BlockSpec((3, N_PAIR, tm, 128), lambda i: (0, 0, i, 0))
    return pl.pallas_call(
        body, name="attn_combine_bwd", grid=(t // tm,),
        in_specs=[pl.BlockSpec((tm, RW), lambda i: (i, 0)), blk, blk, pl.BlockSpec((1, RW), lambda i: (0, 0))],
        out_specs=[blk, blk, pl.BlockSpec((8, RW), lambda i: (0, 0))],
        out_shape=[jax.ShapeDtypeStruct((3, N_PAIR, t, 128), F32)] * 2 + [jax.ShapeDtypeStruct((8, RW), F32)],
        compiler_params=_params(("arbitrary",)),
    )(dyb, o, l, og)


def _attn_bwd(do, dl, o, lse, qkv):
    t = qkv.shape[2]

    def body(do_ref, dl_ref, o_ref, l_ref, q_ref, k_ref, v_ref, dq_ref, dk_ref, dv_ref):
        @pl.when(pl.program_id(1) == 0)
        def _():
            for ref in (dq_ref, dk_ref, dv_ref):
                ref[...] = jnp.zeros_like(ref)

        def unit(di, places, has_prev):
            cur = [_dilated_rows(d, r, n) for d, r, n in places]
            q, kc, vc = [_take(ref, (0,), cur) for ref in (q_ref, k_ref, v_ref)]
            kp = vp = None
            if has_prev:
                prv = [_dilated_rows(d, r, n - 1) for d, r, n in places]
                kp, vp = [_take(ref, (0,), prv) for ref in (k_ref, v_ref)]
            res = _attn_block_bwd(q, kc, vc, kp, vp, *[_take(ref, (0,), cur) for ref in (o_ref, l_ref, do_ref, dl_ref)])
            _put(dq_ref, (), cur, res[0], add=True)
            _put(dk_ref, (), cur, res[1], add=True)
            _put(dv_ref, (), cur, res[2], add=True)
            if has_prev:
                _put(dk_ref, (), prv, res[3], add=True)
                _put(dv_ref, (), prv, res[4], add=True)

        _for_each_sequence(t, unit)

    spec = lambda j: pl.BlockSpec((1, ATTN_GROUP, t, 128), lambda i, b: (j, i, 0, 0))
    branch = pl.BlockSpec((1, ATTN_GROUP, t, 128), lambda i, b: (b, i, 0, 0))
    out = pl.BlockSpec((ATTN_GROUP, t, 128), lambda i, b: (i, 0, 0))
    return pl.pallas_call(
        body, name="attn_bwd", grid=(N_PAIR // ATTN_GROUP, len(DILATIONS)),
        in_specs=[branch] * 4 + [spec(0), spec(1), spec(2)], out_specs=[out] * 3,
        out_shape=[jax.ShapeDtypeStruct((N_PAIR, t, 128), F32)] * 3,
        compiler_params=_params(("parallel", "arbitrary")),
    )(do, dl, o, lse, qkv, qkv, qkv)


def _pad_lora(w, lo):
    z = jnp.zeros((64, RW), F32)
    return jnp.concatenate([w, z], axis=0) if lo == 0 else jnp.concatenate([z, w], axis=0)


def _local_step(x, tgt, win, vecs, w2, a2, g2m, get_rest, send_rest):
    pw = (vecs["mu_shift"], vecs["decay_w0"], _pad_lora(w2, 0), vecs["iclr_a0"], _pad_lora(a2, 64), g2m,
          vecs["k_k"], vecs["k_a"])
    h, proj, qkv, r, lw, k2, v, kk, a, g = _in_proj_prep(x, vecs["mix_norm_g"], win, pw)
    y, s0s = _wkv_fwd(r, lw, k2, v, kk, a)
    o_att, l_att = _attn_fwd(qkv)
    ycat = _mixers_out(y, r, k2, v, g, vecs["ln_x_w"], vecs["ln_x_b"], vecs["r_k"], o_att, l_att, vecs["attn_out_g"])
    wout, wg, wu, wd = get_rest(ycat)
    h2, act, dx2b, dgt, dup, dx1b, dx1, dya, dyb, loss8, dgf, dg2n = _ffn_all(
        x, ycat, wg, wu, wd, wout, vecs["ffn_norm_g"], vecs["final_norm_g"], tgt)
    gw = {
        "w_down": _wgrad(act, dx2b, 1408, 1024, "wgrad_down"),
        "w_gate": _wgrad(dgt, h2, 1408, 1024, "wgrad_gate"),
        "w_up": _wgrad(dup, h2, 1408, 1024, "wgrad_up"),
        "w_out": _wgrad(ycat, dx1b, 1024, 1024, "wgrad_out"),
    }

    lnw = vecs["ln_x_w"] + send_rest(gw)[0, 0]
    dy, dr_p, dk2_p, dv_p, dg, dlnw, dlnb, drk = _post_bwd(dya, y, r, k2, v, g, lnw, vecs["ln_x_b"], vecs["r_k"])
    dr_s, dlw, dk2_s, dv_s, dkk, da = _wkv_bwd(dy, s0s, r, lw, k2, v, kk, a)
    do_att, dl_att, dog = _combine_bwd(dyb, o_att, l_att, vecs["attn_out_g"])
    dq, dk, dv = _attn_bwd(do_att, dl_att, o_att, l_att, qkv)
    dproj, dx, dg1, dmu, dw0, dw2p, da0, da2p, dg2m, dk_k, dk_a = _prep_in_proj_bwd(
        proj, pw, (dr_p, dr_s, dlw, dk2_p, dk2_s, dv_p, dv_s, dkk, da, dg), dq, dk, dv, win, x, vecs["mix_norm_g"], dx1)
    gw["w_in"] = _wgrad(dproj, h, 1664, 1024, "wgrad_in")
    gw["decay_w2"] = dw2p[:64]
    gw["iclr_a2"] = da2p[64:]
    gw["gate_g2"] = dg2m
    gv = {"mix_norm_g": dg1, "mu_shift": dmu, "decay_w0": dw0, "iclr_a0": da0, "k_k": dk_k, "k_a": dk_a, "r_k": drk,
          "ln_x_w": dlnw, "ln_x_b": dlnb, "attn_out_g": dog, "ffn_norm_g": dg2n, "final_norm_g": dgf}
    return loss8, dx, gw, gv


N_CHIP = 4
N_DEV = 8
MATS = ("w_in", "w_out", "w_gate", "w_up", "w_down")
LORAS = ("decay_w2", "iclr_a2", "gate_g2")
VECS = (("mix_norm_g", 1024), ("mu_shift", 1792), ("decay_w0", 512), ("iclr_a0", 512), ("k_k", 512), ("k_a", 512),
        ("r_k", 512), ("ln_x_w", 512), ("ln_x_b", 512), ("attn_out_g", 512), ("ffn_norm_g", 1024),
        ("final_norm_g", 1024))
N_VEC = sum(n for _, n in VECS)
N_SMALL = N_VEC + 128
ANY = pl.BlockSpec(memory_space=pl.ANY)


def _flip(v, f):
    return 1 - v if f else v


class _Me:
    def __init__(self, mode):
        x, y, c = lax.axis_index("x"), lax.axis_index("y"), lax.axis_index("c")
        self.core, self.chip, self.dev = c, 2 * x + y, 4 * x + 2 * y + c
        self.sibling = (x, y, 1 - c)
        if mode == "chips":
            self.peers = [(px, py, c) for px, py in ((1 - x, y), (x, 1 - y), (1 - x, 1 - y))]
        else:
            self.peers = [(_flip(x, k & 4), _flip(y, k & 2), _flip(c, k & 1)) for k in range(1, N_DEV)]


def _half(core, rows):
    h = rows // 2
    return pl.ds(pl.multiple_of(core * h, h), h)


_BY_CHIP = ("gather", "chipsum")


def _peer_copy(srcs, dsts, kinds, send_sems, recv_sems, me, j, i, incoming):
    px, py, pc = me.peers[j]
    pchip, pdev = 2 * px + py, 4 * px + 2 * py + pc
    src, dst, kind = srcs[i], dsts[i], kinds[i]
    if kind == "gather":
        rows = _half(me.core, src.shape[1])
        src, dst = src.at[me.chip, rows], dst.at[pchip if incoming else me.chip, rows]
    elif kind == "scatter":
        src, dst = src.at[pchip, _half(pc, src.shape[1])], dst.at[pdev if incoming else me.dev]
    elif kind == "chipsum":
        src, dst = src.at[pchip], dst.at[pchip if incoming else me.chip]
    else:
        dst = dst.at[pdev if incoming else me.dev]
    n = len(srcs)
    return pltpu.make_async_remote_copy(src_ref=src, dst_ref=dst, send_sem=send_sems.at[n * j + i],
                                        recv_sem=recv_sems.at[n * j + i], device_id=(px, py, pc), device_id_type=MESH)


def _mode(kinds):
    return "chips" if kinds[0] in _BY_CHIP else "devs"


def _npeer(kinds):
    return N_CHIP - 1 if kinds[0] in _BY_CHIP else N_DEV - 1


def _sibling_halves(gs, name):
    n = len(gs)

    def body(*refs):
        srcs, dsts, send_sems, recv_sems = refs[:n], refs[n:2 * n], refs[2 * n], refs[2 * n + 1]
        me = _Me("chips")

        def copy(i, p):
            return pltpu.make_async_remote_copy(
                src_ref=srcs[i].at[p, _half(1 - me.core, srcs[i].shape[1])], dst_ref=dsts[i].at[p],
                send_sem=send_sems.at[N_CHIP * i + p], recv_sem=recv_sems.at[N_CHIP * i + p],
                device_id=me.sibling, device_id_type=MESH)

        copies = [copy(i, p) for i in range(n) for p in range(N_CHIP)]
        for cp in copies:
            cp.start()
        for cp in copies:
            cp.wait()

    return pl.pallas_call(
        body, name=name, in_specs=[ANY] * n, out_specs=[ANY] * n,
        out_shape=[jax.ShapeDtypeStruct((N_CHIP, g.shape[1] // 2, g.shape[2]), g.dtype) for g in gs],
        scratch_shapes=[pltpu.SemaphoreType.DMA((N_CHIP * n,)), pltpu.SemaphoreType.DMA((N_CHIP * n,))],
    )(*gs)


def _add_halves(g, other, core, tr, name):
    _, h, cols = other.shape

    def body(core_ref, g_ref, o_ref, out_ref):
        out_ref[...] = (g_ref[...].astype(F32) + o_ref[...].astype(F32)).astype(BF16)

    blk = lambda off: pl.BlockSpec((1, tr, cols), lambda p, i, core_ref: (p, core_ref[0] * (h // tr) * off + i, 0))
    return pl.pallas_call(
        body, name=name,
        grid_spec=pltpu.PrefetchScalarGridSpec(num_scalar_prefetch=1, grid=(N_CHIP, h // tr),
                                               in_specs=[blk(1), blk(0)], out_specs=blk(0)),
        out_shape=jax.ShapeDtypeStruct(other.shape, BF16),
        compiler_params=_params(("parallel", "parallel")),
    )(core, g, other)


def _swap_gathered(lands, name):
    n = len(lands)

    def body(*refs):
        dsts, send_sems, recv_sems = refs[n:2 * n], refs[2 * n], refs[2 * n + 1]
        me = _Me("chips")

        def copy(j, i, incoming):
            px, py, _ = me.peers[j]
            rows_out, rows_in = _half(me.core, dsts[i].shape[1]), _half(1 - me.core, dsts[i].shape[1])
            return pltpu.make_async_remote_copy(
                src_ref=dsts[i].at[2 * px + py, rows_out], dst_ref=dsts[i].at[2 * px + py, rows_in if incoming else rows_out],
                send_sem=send_sems.at[n * j + i], recv_sem=recv_sems.at[n * j + i], device_id=me.sibling, device_id_type=MESH)

        sends = [copy(j, i, False) for j in range(3) for i in range(n)]
        for cp in sends:
            cp.start()
        for j in range(3):
            for i in range(n):
                copy(j, i, True).wait_recv()
        for cp in sends:
            cp.wait_send()

    return pl.pallas_call(
        body, name=name, in_specs=[ANY] * n, out_specs=[ANY] * n,
        out_shape=[jax.ShapeDtypeStruct(l.shape, l.dtype) for l in lands],
        input_output_aliases={i: i for i in range(n)},
        scratch_shapes=[pltpu.SemaphoreType.DMA((3 * n,)), pltpu.SemaphoreType.DMA((3 * n,))],
    )(*lands)


def _join_halves(sums, name):
    n = len(sums)

    def body(*refs):
        dsts, send_sems, recv_sems = refs[n:2 * n], refs[2 * n], refs[2 * n + 1]
        me = _Me("chips")

        def copy(i, incoming):
            mine, other = _half(me.core, dsts[i].shape[0]), _half(1 - me.core, dsts[i].shape[0])
            return pltpu.make_async_remote_copy(src_ref=dsts[i].at[mine], dst_ref=dsts[i].at[other if incoming else mine],
                                                send_sem=send_sems.at[i], recv_sem=recv_sems.at[i],
                                                device_id=me.sibling, device_id_type=MESH)

        sends = [copy(i, False) for i in range(n)]
        for cp in sends:
            cp.start()
        for i in range(n):
            copy(i, True).wait_recv()
        for cp in sends:
            cp.wait_send()

    return pl.pallas_call(
        body, name=name, in_specs=[ANY] * n, out_specs=[ANY] * n,
        out_shape=[jax.ShapeDtypeStruct(s.shape, s.dtype) for s in sums],
        input_output_aliases={i: i for i in range(n)},
        scratch_shapes=[pltpu.SemaphoreType.DMA((n,)), pltpu.SemaphoreType.DMA((n,))],
    )(*sums)


HBM = pl.BlockSpec(memory_space=pltpu.HBM)
SEM = pl.BlockSpec(memory_space=pltpu.SEMAPHORE)
EFFECT = pltpu.SideEffectType.DATAFLOW_SIDE_EFFECTING


def _swap_start(arrs, lands, kinds, name):
    n = len(lands)
    ops = list(lands) if arrs is None else [*arrs, *lands]
    k = len(ops)

    def body(*refs):
        srcs, dsts, send_sems, recv_sems, token = refs[:n], refs[k - n:k], refs[k], refs[k + 1], refs[-1]
        me = _Me(_mode(kinds))
        for j in range(len(me.peers)):
            for i in range(n):
                _peer_copy(srcs, dsts, kinds, send_sems, recv_sems, me, j, i, False).start()
        token[...] = jnp.zeros_like(token)

    ns = _npeer(kinds) * n
    outs = pl.pallas_call(
        body, name=name,
        out_shape=(pltpu.SemaphoreType.DMA((ns,)), pltpu.SemaphoreType.DMA((ns,)),
                   *[pltpu.HBM(a.shape, a.dtype) for a in ops], jax.ShapeDtypeStruct((8, 128), F32)),
        in_specs=[HBM] * k, out_specs=(SEM, SEM, *[HBM] * k, pl.BlockSpec(memory_space=pltpu.VMEM)),
        input_output_aliases={i: 2 + i for i in range(k)},
        compiler_params=pltpu.CompilerParams(has_side_effects=EFFECT),
    )(*[pltpu.with_memory_space_constraint(a, pltpu.HBM) for a in ops])
    return outs[0], outs[1], outs[2:2 + k - n], outs[2 + k - n:2 + k], outs[-1]


def _swap_wait(send_sems, recv_sems, srcs_thru, lands_thru, after, kinds, name):
    n = len(lands_thru)
    ops = [*srcs_thru, *lands_thru]
    k = len(ops)

    def body(*refs):
        srcs, dsts, s_sems, r_sems = refs[:n], refs[k - n:k], refs[k], refs[k + 1]
        me = _Me(_mode(kinds))
        for j in range(len(me.peers)):
            for i in range(n):
                cp = _peer_copy(srcs, dsts, kinds, s_sems, r_sems, me, j, i, True)
                cp.wait_send()
                cp.wait_recv()

    outs = pl.pallas_call(
        body, name=name,
        out_shape=tuple(pltpu.HBM(a.shape, a.dtype) for a in ops),
        in_specs=[HBM] * k + [SEM, SEM, ANY], out_specs=tuple([HBM] * k),
        input_output_aliases={i: i for i in range(k)},
        compiler_params=pltpu.CompilerParams(has_side_effects=EFFECT),
    )(*ops, send_sems, recv_sems, after)
    return outs[k - n:]


def _adamw(w, g, m, v):
    m = ADAM_B1 * m + (1.0 - ADAM_B1) * g
    v = ADAM_B2 * v + (1.0 - ADAM_B2) * (g * g)
    m_hat = m / (1.0 - ADAM_B1 ** ADAM_STEP)
    v_hat = v / (1.0 - ADAM_B2 ** ADAM_STEP)
    delta = -ADAM_LR * (m_hat / (jnp.sqrt(v_hat) + ADAM_EPS) + ADAM_WD * w)
    return delta, m, v


def _reduce8(rbuf, core, tr, name):
    slots, h, cols = rbuf.shape

    def body(core_ref, r_ref, g_ref):
        g = r_ref[0].astype(F32)
        for s in range(1, slots):
            g = g + r_ref[s].astype(F32)
        g_ref[...] = g

    return pl.pallas_call(
        body, name=name,
        grid_spec=pltpu.PrefetchScalarGridSpec(
            num_scalar_prefetch=1, grid=(h // tr,),
            in_specs=[pl.BlockSpec((slots, tr, cols), lambda i, core_ref: (0, i, 0))],
            out_specs=pl.BlockSpec((tr, cols), lambda i, core_ref: (core_ref[0] * (h // tr) + i, 0))),
        out_shape=jax.ShapeDtypeStruct((2 * h, cols), F32),
        compiler_params=_params(("parallel",)),
    )(core, rbuf)


def _adamw_call(g, w, m, v, tr, name):
    _, rows, cols = w.shape

    def body(g_in, w_ref, m_ref, v_ref, g_ref, d_ref, nm_ref, nv_ref):
        g = g_in[...]
        g_ref[0] = g
        d_ref[0], nm_ref[0], nv_ref[0] = _adamw(w_ref[0], g, m_ref[0], v_ref[0])

    row = pl.BlockSpec((1, tr, cols), lambda i: (0, i, 0))
    return pl.pallas_call(
        body, name=name, grid=(rows // tr,),
        in_specs=[pl.BlockSpec((tr, cols), lambda i: (i, 0)), row, row, row], out_specs=[row] * 4,
        out_shape=[jax.ShapeDtypeStruct(w.shape, F32)] * 4,
        compiler_params=_params(("parallel",)),
    )(g, w, m, v)


def _rowsum_small(parts, loss8):
    def body(*refs):
        out = refs[-1]
        c0 = 0
        for ref in refs[:-1]:
            n = ref.shape[1]
            out[:, c0:c0 + n] = jnp.sum(ref[...], axis=0, keepdims=True)
            c0 += n

    return pl.pallas_call(body, name="rowsum_small", out_shape=jax.ShapeDtypeStruct((1, N_SMALL), F32))(*parts, loss8)


def _reduce_adamw_small(sbuf, ws, ms, vs):
    nv = len(ws)

    def body(*refs):
        s_ref, ins, outs = refs[0], refs[1:1 + 3 * nv], refs[1 + 3 * nv:]
        tot = s_ref[0]
        for s in range(1, N_DEV):
            tot = tot + s_ref[s]
        c0 = 0
        for i in range(nv):
            n = ins[i].shape[1]
            g = tot[:, c0:c0 + n]
            outs[i][...] = g
            outs[nv + i][...], outs[2 * nv + i][...], outs[3 * nv + i][...] = _adamw(
                ins[i][...], g, ins[nv + i][...], ins[2 * nv + i][...])
            c0 += n
        outs[-1][...] = tot[:, c0:]

    return pl.pallas_call(
        body, name="reduce_adamw_small",
        out_shape=[jax.ShapeDtypeStruct(a.shape, F32) for a in ws] * 4 + [jax.ShapeDtypeStruct((1, 128), F32)],
    )(sbuf, *ws, *ms, *vs)


_TRANSPOSED = ("w_in", "w_gate", "w_up")
_ROW_STACKED = MATS
_ADAM_TILE = {"w_in": 208, "w_out": 256, "w_gate": 176, "w_up": 176, "w_down": 176, "lora": 256}
_SUM_TILE = {"w_in": 208, "w_out": 128, "w_gate": 176, "w_up": 176, "w_down": 176, "lora": 128}


def _full(n, stacked):
    p, r, c = stacked.shape
    if n in _ROW_STACKED:
        return stacked.reshape(p * r, c)
    return jnp.transpose(stacked, (1, 0, 2)).reshape(r, p * c)


def _by_chip(n, full):
    if n in _ROW_STACKED:
        return full.reshape(N_CHIP, full.shape[0] // N_CHIP, full.shape[1])
    r, c = full.shape
    return jnp.transpose(full.reshape(r, N_CHIP, c // N_CHIP), (1, 0, 2))


def _with_own(land_shape, dtype, own, slot):
    return lax.dynamic_update_slice(lax.empty(land_shape, dtype), own[None], (slot,) + (0,) * own.ndim)


def kernel(x, mix_norm_g, w_in, mu_shift, decay_w0, decay_w2, iclr_a0, iclr_a2, gate_g2, k_k, k_a, r_k, ln_x_w, ln_x_b, attn_out_g, w_out, ffn_norm_g, w_gate, w_up, w_down, final_norm_g, loss_target, m_mix_norm_g, m_w_in, m_mu_shift, m_decay_w0, m_decay_w2, m_iclr_a0, m_iclr_a2, m_gate_g2, m_k_k, m_k_a, m_r_k, m_ln_x_w, m_ln_x_b, m_attn_out_g, m_w_out, m_ffn_norm_g, m_w_gate, m_w_up, m_w_down, m_final_norm_g, v_mix_norm_g, v_w_in, v_mu_shift, v_decay_w0, v_decay_w2, v_iclr_a0, v_iclr_a2, v_gate_g2, v_k_k, v_k_a, v_r_k, v_ln_x_w, v_ln_x_b, v_attn_out_g, v_w_out, v_ffn_norm_g, v_w_gate, v_w_up, v_w_down, v_final_norm_g):
    names = ("mix_norm_g", "w_in", "mu_shift", "decay_w0", "decay_w2", "iclr_a0", "iclr_a2", "gate_g2", "k_k", "k_a",
             "r_k", "ln_x_w", "ln_x_b", "attn_out_g", "w_out", "ffn_norm_g", "w_gate", "w_up", "w_down", "final_norm_g")
    w = dict(zip(names, (mix_norm_g, w_in, mu_shift, decay_w0, decay_w2, iclr_a0, iclr_a2, gate_g2, k_k, k_a, r_k,
                         ln_x_w, ln_x_b, attn_out_g, w_out, ffn_norm_g, w_gate, w_up, w_down, final_norm_g)))
    m = dict(zip(names, (m_mix_norm_g, m_w_in, m_mu_shift, m_decay_w0, m_decay_w2, m_iclr_a0, m_iclr_a2, m_gate_g2,
                         m_k_k, m_k_a, m_r_k, m_ln_x_w, m_ln_x_b, m_attn_out_g, m_w_out, m_ffn_norm_g, m_w_gate,
                         m_w_up, m_w_down, m_final_norm_g)))
    v = dict(zip(names, (v_mix_norm_g, v_w_in, v_mu_shift, v_decay_w0, v_decay_w2, v_iclr_a0, v_iclr_a2, v_gate_g2,
                         v_k_k, v_k_a, v_r_k, v_ln_x_w, v_ln_x_b, v_attn_out_g, v_w_out, v_ffn_norm_g, v_w_gate,
                         v_w_up, v_w_down, v_final_norm_g)))
    first = ("w_in", "lora")
    rest = ("w_out", "w_gate", "w_up", "w_down")
    xi, yi, ci = lax.axis_index("x"), lax.axis_index("y"), lax.axis_index("c")
    my_chip, my_dev = 2 * xi + yi, 4 * xi + 2 * yi + ci
    gather, scatter = ("gather",) * 4, ("scatter",) * 4

    def stored(d):
        out = {n: jnp.transpose(d[n][0]) if n in _TRANSPOSED else d[n][0] for n in MATS}
        out["lora"] = jnp.concatenate([d[n][0] for n in LORAS], axis=0)
        return out

    ws, ms, vs = stored(w), stored(m), stored(v)
    lora_rows = [(0, 64), (64, 128), (128, 256)]
    mine = [ws["w_in"].astype(BF16), ws["lora"]]
    early = _swap_start(None, [_with_own((N_CHIP,) + a.shape, a.dtype, a, my_chip) for a in mine], gather[:2],
                        "gather_first_start")
    wb = {n: (ws[n] + early[4][0, 0]).astype(BF16) for n in rest}
    lands = [_with_own((N_CHIP,) + wb[n].shape, BF16, wb[n], my_chip) for n in rest]
    ssem, rsem, srcs_thru, lands_thru, tok = _swap_start(None, lands, gather, "gather_rest_start")
    got = _swap_wait(early[0], early[1], early[2], early[3], tok, gather[:2], "gather_first_wait")
    win_all, lora_all = _swap_gathered(got, "gather_first_halves")
    win = _full("w_in", win_all)
    w2, a2, g2m = (_full(n, lora_all[:, a:b]) for n, (a, b) in zip(LORAS, lora_rows))

    vecs = {n: w[n].reshape(1, sz) for n, sz in VECS}
    vecs["mix_norm_g"] = vecs["mix_norm_g"] + tok[0, 0]

    def get_rest(after):
        halves = _swap_wait(ssem, rsem, srcs_thru, lands_thru, after, gather, "gather_rest_wait")
        return [_full(n, z) for n, z in zip(rest, _swap_gathered(halves, "gather_rest_halves"))]

    flight = []

    def my_half(g):
        h = g.shape[1] // 2
        return lax.dynamic_slice(g, (my_chip, ci * h, 0), (1, h, g.shape[2]))[0]

    def send_rest(gw):
        gs = [_by_chip(n, gw[n]).astype(BF16) for n in rest]
        into = [_with_own((N_DEV,) + my_half(g).shape, BF16, my_half(g), my_dev) for g in gs]
        flight.extend(_swap_start(gs, into, scatter, "exchange_rest_start"))
        return flight[4]

    loss8, dx, gw, gv = _local_step(x[0], loss_target[0], win, vecs, w2, a2, g2m, get_rest, send_rest)

    core = jnp.reshape(ci, (1,)).astype(jnp.int32)
    gs = [_by_chip("w_in", gw["w_in"]).astype(BF16),
          jnp.concatenate([_by_chip(n, gw[n]) for n in LORAS], axis=1).astype(BF16)]
    theirs = _sibling_halves(gs, "presum_halves")
    sums = [_add_halves(g, o, core, _SUM_TILE[n], "chipsum_" + n) for n, g, o in zip(first, gs, theirs)]
    own = [lax.dynamic_index_in_dim(s, my_chip, 0, keepdims=False) for s in sums]
    last = _swap_start(sums, [_with_own(s.shape, BF16, o, my_chip) for s, o in zip(sums, own)], ("chipsum",) * 2,
                       "exchange_first_start")
    small = _rowsum_small([gv[n] for n, _ in VECS], loss8 + last[4])
    vecs_out = _swap_start([small], [_with_own((N_DEV,) + small.shape, F32, small, my_dev)], ("all",),
                           "exchange_vectors_start")


    def update(group, rbufs, tag):
        sums = [_reduce8(rb, core, _SUM_TILE[n], "reduce_" + n) for n, rb in zip(group, rbufs)]
        gsum = _join_halves(sums, "join_halves_" + tag)
        out = {}
        for n, g in zip(group, gsum):
            r = _adamw_call(g, ws[n][None], ms[n][None], vs[n][None], _ADAM_TILE[n], "adamw_" + n)
            if n == "lora":
                for name, (a, b) in zip(LORAS, lora_rows):
                    out[name] = [z[:, a:b] for z in r]
            else:
                out[n] = [jnp.transpose(z[0])[None] for z in r] if n in _TRANSPOSED else r
        return out, r[1]

    res, done = update(rest, _swap_wait(flight[0], flight[1], flight[2], flight[3], vecs_out[4], scatter,
                                        "exchange_rest_wait"), "rest")
    got = _swap_wait(last[0], last[1], last[2], last[3], done, ("chipsum",) * 2, "exchange_first_wait")
    res_first, done = update(first, got, "first")
    res.update(res_first)
    sbuf = _swap_wait(vecs_out[0], vecs_out[1], vecs_out[2], vecs_out[3], done, ("all",), "exchange_vectors_wait")[0]
    rows = lambda d: [d[n].reshape(1, sz) for n, sz in VECS]
    small_res = _reduce_adamw_small(sbuf, rows(w), rows(m), rows(v))

    outs = []
    for k in range(4):
        piece = {n: r[k] for n, r in res.items()}
        for i, (n, _) in enumerate(VECS):
            piece[n] = small_res[k * len(VECS) + i].reshape(w[n].shape)
        outs.extend(piece[n] for n in names)
    return (small_res[-1][0, 0], dx[None], *outs)
```

```python
import jax
import jax.numpy as jnp
from jax import lax
from jax.experimental import pallas as pl
from jax.experimental.pallas import tpu as pltpu

F32 = jnp.float32
BF16 = jnp.bfloat16

D_MODEL = 1024
HEAD_DIM = 64
RW = 512
N_PAIR = RW // 128
SHIFT_COLS = 1792
IN_COLS = 3328
D_FF = 2816
FF_CHUNK = 256
NORM_EPS = 1e-6
GN_EPS = 64e-5
CHUNK = 64
SUB = 16
WKV_PASSES = 1
ATTN_PASSES = 1
ATTN_BLOCK = 128
DILATIONS = (1, 4, 16)
NEG = -1e30
ADAM_LR, ADAM_B1, ADAM_B2, ADAM_EPS, ADAM_WD, ADAM_STEP = 0.001, 0.9, 0.999, 1e-08, 0.01, 10
VMEM_LIMIT = 56 * 1024 * 1024
MESH = pl.DeviceIdType.MESH


def _params(sem=None, **kw):
    return pltpu.CompilerParams(dimension_semantics=sem, vmem_limit_bytes=VMEM_LIMIT, **kw)


def _dot(a, b):
    return lax.dot_general(a, b, (((1,), (0,)), ((), ())), preferred_element_type=F32)


def _dot_nt(a, b):
    return lax.dot_general(a, b, (((1,), (1,)), ((), ())), preferred_element_type=F32)


def _dot_tn(a, b):
    return lax.dot_general(a, b, (((0,), (0,)), ((), ())), preferred_element_type=F32)


_FORMS = {"nn": ((1,), (0,)), "nt": ((1,), (1,)), "tn": ((0,), (0,))}


def _dg(a, b, form):
    if a.ndim == 3 or b.ndim == 3:
        nb = a.shape[0] if a.ndim == 3 else b.shape[0]
        return jnp.stack([_dg(a[i] if a.ndim == 3 else a, b[i] if b.ndim == 3 else b, form) for i in range(nb)], axis=0)
    return lax.dot_general(a, b, (_FORMS[form], ((), ())), preferred_element_type=F32)


def _split2(x):
    hi = x.astype(BF16)
    return hi, (x - hi.astype(F32)).astype(BF16)


def _split3(x):
    hi = x.astype(BF16)
    rest = x - hi.astype(F32)
    mid = rest.astype(BF16)
    return hi, mid, (rest - mid.astype(F32)).astype(BF16)


def _mm_raw(a, b, form, mode):
    if mode == 1:
        return _dg(a.astype(BF16), b.astype(BF16), form)
    if mode == 3:
        ah, al = _split2(a)
        bh, bl = _split2(b)
        return _dg(ah, bh, form) + (_dg(ah, bl, form) + _dg(al, bh, form))
    if mode == "L3":
        ab = a.astype(BF16)
        b1, b2, b3 = _split3(b)
        if form == "nn":
            n = b.shape[-1]
            wide = _dg(ab, jnp.concatenate([b1, b2, b3], axis=-1), form)
            return wide[..., :n] + (wide[..., n:2 * n] + wide[..., 2 * n:])
        return _dg(ab, b1, form) + (_dg(ab, b2, form) + _dg(ab, b3, form))
    assert mode == "R3", mode
    bb = b.astype(BF16)
    a1, a2, a3 = _split3(a)
    if form in ("nn", "nt"):
        m = a.shape[-2]
        tall = _dg(jnp.concatenate([a1, a2, a3], axis=-2), bb, form)
        return tall[..., :m, :] + (tall[..., m:2 * m, :] + tall[..., 2 * m:, :])
    return _dg(a1, bb, form) + (_dg(a2, bb, form) + _dg(a3, bb, form))


def _mm(a, b, form, mode):
    @jax.custom_vjp
    def f(a, b):
        return _mm_raw(a, b, form, mode)

    def fwd(a, b):
        return _mm_raw(a, b, form, mode), (a, b)

    def bwd(res, ct):
        a, b = res
        la = {1: 1, 3: 3, "L3": None, "R3": "R3"}[mode]
        lb = {1: 1, 3: 3, "L3": "L3", "R3": None}[mode]
        if form == "nn":
            da = None if la is None else _mm_raw(ct, b, "nt", la)
            db = None if lb is None else _mm_raw(a, ct, "tn", lb)
        elif form == "nt":
            da = None if la is None else _mm_raw(ct, b, "nn", la)
            db = None if lb is None else _mm_raw(ct, a, "tn", "R3" if lb == "L3" else lb)
        else:
            da = None if la is None else _mm_raw(b, ct, "nt", "L3" if la == "R3" else la)
            db = None if lb is None else _mm_raw(a, ct, "nn", lb)
        return (jnp.zeros_like(a) if da is None else da, jnp.zeros_like(b) if db is None else db)

    f.defvjp(fwd, bwd)
    return f(a, b)


def _seg_ones(n):
    r = lax.broadcasted_iota(jnp.int32, (n, n), 0) // HEAD_DIM
    c = lax.broadcasted_iota(jnp.int32, (n, n), 1) // HEAD_DIM
    return (r == c).astype(F32)


def _segsum(x, seg):
    return _mm(x, seg, "nn", "R3")


def _rms_fwd(x, g):
    rstd = lax.rsqrt(jnp.mean(x * x, axis=-1, keepdims=True) + NORM_EPS)
    return x * rstd * g


def _rms_bwd(dy, x, g):
    rstd = lax.rsqrt(jnp.mean(x * x, axis=-1, keepdims=True) + NORM_EPS)
    xn = x * rstd
    dxn = dy * g
    dx = rstd * (dxn - xn * jnp.mean(dxn * xn, axis=-1, keepdims=True))
    return dx, dy * xn


def _sigmoid(x):
    return 1.0 / (1.0 + jnp.exp(-x))


def _softplus(x):
    return jnp.maximum(x, 0.0) + jnp.log(1.0 + jnp.exp(-jnp.abs(x)))


def _acc(ref, val, first):
    @pl.when(first)
    def _():
        ref[...] = val

    @pl.when(jnp.logical_not(first))
    def _():
        ref[...] += val


def _colsum8(v):
    rows, n = v.shape
    return jnp.sum(v.reshape(rows // 8, 8, n), axis=0)


def _prep_fn(p, pprev, mu, w0, w2p, a0, a2p, g2, k_k, k_a):
    seg = _seg_ones(RW)
    ps = p + (pprev - p) * mu
    r = ps[:, 0:RW]
    k = ps[:, RW:2 * RW]
    v = ps[:, 2 * RW:3 * RW]
    xwa = ps[:, 3 * RW:3 * RW + 128]
    xg = ps[:, 3 * RW + 128:3 * RW + 256]
    wraw = -_softplus(-(w0 + _mm(jnp.tanh(xwa), w2p, "nn", 3))) - 0.5
    lw = -jnp.exp(wraw)
    a = _sigmoid(a0 + _mm(xwa, a2p, "nn", 3))
    g = _mm(_sigmoid(xg), g2, "nn", 3)
    kk = k * k_k
    kk = kk / jnp.maximum(jnp.sqrt(_segsum(kk * kk, seg)), 1e-12)
    k2 = k * (1.0 + (a - 1.0) * k_a)
    return r, lw, k2, v, kk, a, g


def _transposed(z):
    return jnp.stack([z[i].T for i in range(z.shape[0])], axis=0) if z.ndim == 3 else z.T


def _solve_unit_lower(lmat, rhs):
    c = lmat.shape[-1]
    row = lax.broadcasted_iota(jnp.int32, (c, c), 0)
    col = lax.broadcasted_iota(jnp.int32, (c, c), 1)
    eye = (row == col).astype(F32)
    ld = jnp.where(row // SUB == col // SUB, lmat, 0.0)
    lo = lmat - ld
    x = eye + ld
    m = ld
    mm = lambda p, q: _mm(p, q, "nn", WKV_PASSES)
    cat = jnp.concatenate
    m = mm(m, m)
    for _ in range(2):
        mx = mm(m, cat([m, x], axis=-1))
        m, x = mx[..., :c], x + mx[..., c:]
    x = x + mm(m, x)
    gw = mm(x, cat([lo, rhs], axis=-1))
    g, w = gw[..., :c], gw[..., c:]
    gg = mm(g, cat([g, w], axis=-1))
    w = w + gg[..., c:]
    return w + mm(gg[..., :c], w)


def _wkv_chunk_fn(s0, r, lw, k, v, kk, a):
    c = r.shape[-2]
    n = 2 * c
    row = lax.broadcasted_iota(jnp.int32, (n, n), 0)
    col = lax.broadcasted_iota(jnp.int32, (n, n), 1)
    same = (row // c) == (col // c)
    incl = jnp.logical_and(row >= col, same)
    strict = jnp.logical_and(row > col, same)
    sel = (lax.broadcasted_iota(jnp.int32, (n, 128), 0) // c) == (lax.broadcasted_iota(jnp.int32, (n, 128), 1) // HEAD_DIM)
    two = lambda z: jnp.concatenate([z, z], axis=-2)
    lw2 = two(lw)
    mm = lambda p_, q_, form: _mm(p_, q_, form, WKV_PASSES)
    cl = _mm(incl.astype(F32), lw2, "nn", "L3")
    p = jnp.exp(cl)
    pinv = jnp.exp(-cl)
    pprev = jnp.exp(cl - lw2)
    kk2 = two(kk)
    at = jnp.where(sel, -kk2 * pprev, 0.0)
    bt = jnp.where(sel, kk2 * two(a) * pinv, 0.0)
    kt = jnp.where(sel, two(k) * pinv, 0.0)
    rt = jnp.where(sel, two(r) * p, 0.0)
    vt = jnp.where(sel, two(v), 0.0)
    cat = jnp.concatenate
    bk = cat([bt, kt], axis=-2)
    arbk = mm(cat([at, rt], axis=-2), bk, "nt")
    ab, ak = jnp.where(strict, arbk[..., :n, :n], 0.0), jnp.where(strict, arbk[..., :n, n:], 0.0)
    rb, rk = jnp.where(incl, arbk[..., n:, :n], 0.0), jnp.where(incl, arbk[..., n:, n:], 0.0)
    s0t = _transposed(s0)
    u = _solve_unit_lower(ab, mm(cat([at, ak], axis=-1), cat([s0t, vt], axis=-2), "nn"))
    y2 = mm(cat([rt, rb, rk], axis=-1), cat([s0t, u, vt], axis=-2), "nn")
    plast = jnp.exp(jnp.sum(lw, axis=-2, keepdims=True))
    s1 = (s0 + mm(cat([u, vt], axis=-2), bk, "tn")) * plast
    r2 = lax.broadcasted_iota(jnp.int32, (128, 128), 0) // HEAD_DIM
    c2 = lax.broadcasted_iota(jnp.int32, (128, 128), 1) // HEAD_DIM
    return y2[..., :c, :] + y2[..., c:, :], jnp.where(r2 == c2, s1, 0.0)


def _post_fn(y, r, k2, v, g, lnw, lnb, rk):
    seg = _seg_ones(RW)
    mean = _segsum(y, seg) * (1.0 / HEAD_DIM)
    yc = y - mean
    var = _segsum(yc * yc, seg) * (1.0 / HEAD_DIM)
    yn = yc * lax.rsqrt(var + GN_EPS)
    out = yn * lnw + lnb + _segsum(r * k2 * rk, seg) * v
    return out * g


def _attn_block_fn(q, kc, vc, kp=None, vp=None):
    n = ATTN_BLOCK
    qi = lax.broadcasted_iota(jnp.int32, (n, n), 0)
    kj = lax.broadcasted_iota(jnp.int32, (n, n), 1)
    lane = lax.broadcasted_iota(jnp.int32, (1, 128), 1)
    scale = HEAD_DIM ** -0.5
    valid = kj <= qi
    keys, vals = kc, vc
    if kp is not None:
        valid = jnp.concatenate([valid, kj >= qi], axis=-1)
        keys, vals = jnp.concatenate([kc, kp], axis=-2), jnp.concatenate([vc, vp], axis=-2)
    m0 = (lane // HEAD_DIM) == 0
    q2 = jnp.concatenate([jnp.where(m0, q, 0.0), jnp.where(m0, 0.0, q)], axis=-2)
    valid2 = jnp.concatenate([valid, valid], axis=-2)
    s = jnp.where(valid2, _mm(q2, keys, "nt", ATTN_PASSES) * scale, NEG)
    m = jnp.max(s, axis=-1, keepdims=True)
    p = jnp.exp(s - m)
    den = jnp.sum(p, axis=-1, keepdims=True)
    o2 = _mm(p, vals, "nn", ATTN_PASSES) / den
    l2 = m + jnp.log(den)
    return jnp.where(m0, o2[..., :n, :], o2[..., n:, :]), jnp.where(m0, l2[..., :n, :], l2[..., n:, :])


def _attn_block_bwd(q, kc, vc, kp, vp, o, lse, do, dl):
    n = ATTN_BLOCK
    cat = jnp.concatenate
    qi = lax.broadcasted_iota(jnp.int32, (n, n), 0)
    kj = lax.broadcasted_iota(jnp.int32, (n, n), 1)
    m0 = (lax.broadcasted_iota(jnp.int32, (1, 128), 1) // HEAD_DIM) == 0
    scale = HEAD_DIM ** -0.5
    valid = kj <= qi
    keys, vals = kc, vc
    if kp is not None:
        valid = cat([valid, kj >= qi], axis=-1)
        keys, vals = cat([kc, kp], axis=-2), cat([vc, vp], axis=-2)
    stack = lambda z: cat([jnp.where(m0, z, 0.0), jnp.where(m0, 0.0, z)], axis=-2)
    q2, do2 = stack(q), stack(do)
    lse2 = cat([jnp.max(jnp.where(m0, lse, NEG), axis=-1, keepdims=True),
                jnp.max(jnp.where(m0, NEG, lse), axis=-1, keepdims=True)], axis=-2)
    delta = jnp.sum(do2 * cat([o, o], axis=-2), axis=-1, keepdims=True)
    dlse = jnp.sum(stack(dl), axis=-1, keepdims=True)
    mm = lambda a, b, form: _mm_raw(a, b, form, ATTN_PASSES)
    s = jnp.where(cat([valid, valid], axis=-2), mm(q2, keys, "nt") * scale, NEG)
    p = jnp.exp(s - lse2)
    ds = p * (mm(do2, vals, "nt") - delta + dlse)
    dq2 = mm(ds, keys, "nn") * scale
    dq = jnp.where(m0, dq2[..., :n, :], dq2[..., n:, :])
    dkeys = mm(ds, q2, "tn") * scale
    dvals = mm(p, do2, "tn")
    if kp is None:
        return dq, dkeys, dvals
    return dq, dkeys[..., :n, :], dvals[..., :n, :], dkeys[..., n:, :], dvals[..., n:, :]


def _combine_fn(o1, o2, o3, l1, l2, l3, og):
    seg = _seg_ones(o1.shape[-1])
    m = jnp.maximum(jnp.maximum(l1, l2), l3)
    e1, e2, e3 = jnp.exp(l1 - m), jnp.exp(l2 - m), jnp.exp(l3 - m)
    o = (e1 * o1 + e2 * o2 + e3 * o3) / (e1 + e2 + e3)
    o = o * lax.rsqrt(_segsum(o * o, seg) * (1.0 / HEAD_DIM) + NORM_EPS)
    return o * og


def _shifted(p, last8, first):
    prow = jnp.where(first, 0.0, last8[7:8, :])
    rolled = pltpu.roll(p, 1, axis=0)
    rid = lax.broadcasted_iota(jnp.int32, p.shape, 0)
    return jnp.where(rid == 0, prow, rolled)


_PREP_TM = 256


def _prep_specs(tm):
    vec = lambda n: pl.BlockSpec((1, n), lambda i: (0, 0))
    mat = lambda r, n: pl.BlockSpec((r, n), lambda i: (0, 0))
    return [vec(SHIFT_COLS), vec(RW), mat(128, RW), vec(RW), mat(128, RW), mat(128, RW), vec(RW), vec(RW)]


def _in_proj_prep(x, g1, win, pw):
    t = x.shape[0]
    tm = _PREP_TM

    def body(x_ref, g_ref, w_ref, mu, w0, w2p, a0, a2p, g2, k_k, k_a, h_ref, pa_ref, qkv_ref, *rest):
        outs, carry = rest[:7], rest[7]

        @pl.when(pl.program_id(0) == 0)
        def _():
            carry[...] = jnp.zeros_like(carry)

        h = _rms_fwd(x_ref[...], g_ref[...]).astype(BF16)
        h_ref[...] = h
        proj = _dot_nt(h, w_ref[...])
        p = proj[:, :SHIFT_COLS]
        pa_ref[...] = p
        for j in range(3):
            for pr in range(N_PAIR):
                c0 = SHIFT_COLS + j * RW + pr * 128
                qkv_ref[j, pr] = proj[:, c0:c0 + 128]
        pprev = _shifted(p, carry[...], pl.program_id(0) == 0)
        carry[...] = p[tm - 8:, :]
        res = _prep_fn(p, pprev, mu[...], w0[...], w2p[...], a0[...], a2p[...], g2[...], k_k[...], k_a[...])
        for o_ref, val in zip(outs, res):
            o_ref[...] = val

    row = pl.BlockSpec((tm, RW), lambda i: (i, 0))
    return pl.pallas_call(
        body, name="in_proj_prep", grid=(t // tm,),
        in_specs=[pl.BlockSpec((tm, D_MODEL), lambda i: (i, 0)), pl.BlockSpec((1, D_MODEL), lambda i: (0, 0)),
                  pl.BlockSpec((IN_COLS, D_MODEL), lambda i: (0, 0))] + _prep_specs(tm),
        out_specs=[pl.BlockSpec((tm, D_MODEL), lambda i: (i, 0)), pl.BlockSpec((tm, SHIFT_COLS), lambda i: (i, 0)),
                   pl.BlockSpec((3, N_PAIR, tm, 128), lambda i: (0, 0, i, 0))] + [row] * 7,
        out_shape=[jax.ShapeDtypeStruct((t, D_MODEL), BF16), jax.ShapeDtypeStruct((t, SHIFT_COLS), F32),
                   jax.ShapeDtypeStruct((3, N_PAIR, t, 128), F32)] + [jax.ShapeDtypeStruct((t, RW), F32)] * 7,
        scratch_shapes=[pltpu.VMEM((8, SHIFT_COLS), F32)],
        compiler_params=_params(("arbitrary",)),
    )(x, g1, win, *pw)


def _pairs(ref):
    return jnp.stack([ref[:, 128 * p:128 * (p + 1)] for p in range(N_PAIR)], axis=0)


def _wkv_fwd(r, lw, k2, v, kk, a):
    t = r.shape[0]
    nc = t // CHUNK

    def body(r_ref, lw_ref, k_ref, v_ref, kk_ref, a_ref, y_ref, s_ref, st):
        @pl.when(pl.program_id(0) == 0)
        def _():
            st[...] = jnp.zeros_like(st)

        s0 = st[...]
        s_ref[0] = s0
        y, s1 = _wkv_chunk_fn(s0, *[_pairs(ref) for ref in (r_ref, lw_ref, k_ref, v_ref, kk_ref, a_ref)])
        for p in range(N_PAIR):
            y_ref[:, 128 * p:128 * (p + 1)] = y[p]
        st[...] = s1

    blk = pl.BlockSpec((CHUNK, RW), lambda c: (c, 0))
    return pl.pallas_call(
        body, name="wkv_fwd", grid=(nc,),
        in_specs=[blk] * 6,
        out_specs=[blk, pl.BlockSpec((1, N_PAIR, 128, 128), lambda c: (c, 0, 0, 0))],
        out_shape=[jax.ShapeDtypeStruct((t, RW), F32), jax.ShapeDtypeStruct((nc, N_PAIR, 128, 128), F32)],
        scratch_shapes=[pltpu.VMEM((N_PAIR, 128, 128), F32)],
        compiler_params=_params(("arbitrary",)),
    )(r, lw, k2, v, kk, a)


_POST_TM = 512


ATTN_GROUP = 2


def _dilated_rows(d, r, n):
    if d == 1:
        return pl.ds(pl.multiple_of(n * ATTN_BLOCK, ATTN_BLOCK), ATTN_BLOCK)
    return pl.ds(r + n * (ATTN_BLOCK * d), ATTN_BLOCK, stride=d)


def _for_each_sequence(t, unit):
    for di, d in enumerate(DILATIONS):

        @pl.when(pl.program_id(1) == di)
        def _(di=di, d=d):
            nb = t // (ATTN_BLOCK * d)
            if d == 1:
                unit(di, [(d, 0, 0)], False)
                unit(di, [(d, 0, 1)], True)
                lax.fori_loop(1, nb // 2, lambda k, c: (unit(di, [(d, 0, 2 * k), (d, 0, 2 * k + 1)], True), c)[1], 0)
            else:

                def residues(r, carry):
                    unit(di, [(d, r, 0), (d, r + d // 2, 0)], False)
                    if nb > 1:
                        lax.fori_loop(1, nb, lambda n, c: (unit(di, [(d, r, n), (d, r + d // 2, n)], True), c)[1], 0)
                    return carry

                lax.fori_loop(0, d // 2, residues, 0)


def _take(ref, lead, rows_list):
    return jnp.stack([ref.at[(*lead, g)][rows, :] for rows in rows_list for g in range(ref.shape[len(lead)])], axis=0)


def _put(ref, lead, rows_list, val, add=False):
    k = 0
    for rows in rows_list:
        for g in range(ref.shape[len(lead)]):
            if add:
                ref.at[(*lead, g)][rows, :] += val[k]
            else:
                ref.at[(*lead, g)][rows, :] = val[k]
            k += 1


def _attn_fwd(qkv):
    t = qkv.shape[2]

    def body(q_ref, k_ref, v_ref, o_ref, l_ref):
        def unit(di, places, has_prev):
            cur = [_dilated_rows(d, r, n) for d, r, n in places]
            args = [_take(ref, (0,), cur) for ref in (q_ref, k_ref, v_ref)]
            if has_prev:
                prv = [_dilated_rows(d, r, n - 1) for d, r, n in places]
                args += [_take(ref, (0,), prv) for ref in (k_ref, v_ref)]
            o, lse = _attn_block_fn(*args)
            _put(o_ref, (0,), cur, o)
            _put(l_ref, (0,), cur, lse)

        _for_each_sequence(t, unit)

    spec = lambda j: pl.BlockSpec((1, ATTN_GROUP, t, 128), lambda i, b: (j, i, 0, 0))
    out = pl.BlockSpec((1, ATTN_GROUP, t, 128), lambda i, b: (b, i, 0, 0))
    return pl.pallas_call(
        body, name="attn_fwd", grid=(N_PAIR // ATTN_GROUP, len(DILATIONS)),
        in_specs=[spec(0), spec(1), spec(2)], out_specs=[out, out],
        out_shape=[jax.ShapeDtypeStruct((3, N_PAIR, t, 128), F32)] * 2,
        compiler_params=_params(("parallel", "arbitrary")),
    )(qkv, qkv, qkv)


_COMB_TM = 512


def _mixers_out(y, r, k2, v, g, lnw, lnb, rk, o, l, og):
    t = y.shape[0]
    tm = _COMB_TM

    def body(y_ref, r_ref, k_ref, v_ref, g_ref, lnw_ref, lnb_ref, rk_ref, o_ref, l_ref, og_ref, out_ref):
        out_ref[:, :RW] = _post_fn(y_ref[...], r_ref[...], k_ref[...], v_ref[...], g_ref[...],
                                   lnw_ref[...], lnb_ref[...], rk_ref[...]).astype(BF16)
        for p in range(N_PAIR):
            cols = slice(128 * p, 128 * (p + 1))
            out_ref[:, RW + 128 * p:RW + 128 * (p + 1)] = _combine_fn(
                o_ref[0, p], o_ref[1, p], o_ref[2, p], l_ref[0, p], l_ref[1, p], l_ref[2, p], og_ref[:, cols]).astype(BF16)

    row = pl.BlockSpec((tm, RW), lambda i: (i, 0))
    vec = pl.BlockSpec((1, RW), lambda i: (0, 0))
    blk = pl.BlockSpec((3, N_PAIR, tm, 128), lambda i: (0, 0, i, 0))
    return pl.pallas_call(
        body, name="mixers_out", grid=(t // tm,),
        in_specs=[row] * 5 + [vec] * 3 + [blk, blk, vec], out_specs=pl.BlockSpec((tm, D_MODEL), lambda i: (i, 0)),
        out_shape=jax.ShapeDtypeStruct((t, D_MODEL), BF16),
        compiler_params=_params(("parallel",)),
    )(y, r, k2, v, g, lnw, lnb, rk, o, l, og)


def _ffn_all(x, ycat, wg, wu, wd, wout, g2, gf, tgt):
    t = x.shape[0]
    tm = 256

    def body(x_ref, y_ref, wg_ref, wu_ref, wd_ref, wo_ref, g2_ref, gf_ref, t_ref,
             h_ref, act_ref, dx2b_ref, dgt_ref, dup_ref, dx1b_ref, dx1_ref, dya_ref, dyb_ref, loss_ref, dgf_ref, dg2_ref,
             gt_s, up_s):
        first = pl.program_id(0) == 0
        x1 = x_ref[...] + _dot(y_ref[...], wo_ref[...])
        h = _rms_fwd(x1, g2_ref[...]).astype(BF16)
        h_ref[...] = h
        for c0 in range(0, D_FF, FF_CHUNK):
            cols = slice(c0, c0 + FF_CHUNK)
            gt = _dot_nt(h, wg_ref[cols, :])
            up = _dot_nt(h, wu_ref[cols, :])
            gt_s[:, cols] = gt.astype(BF16)
            up_s[:, cols] = up.astype(BF16)
            act_ref[:, cols] = (gt * _sigmoid(gt) * up).astype(BF16)
        x2 = x1 + _dot(act_ref[...], wd_ref[...])
        gf_ = gf_ref[...]
        diff = _rms_fwd(x2, gf_) - t_ref[...]
        lrow = 0.5 * jnp.sum(_colsum8(diff * diff), axis=1, keepdims=True) * (1.0 / D_MODEL)
        _acc(loss_ref, jnp.broadcast_to(lrow, (8, 128)), first)
        dx2, dgr = _rms_bwd(diff * (1.0 / D_MODEL), x2, gf_)
        _acc(dgf_ref, _colsum8(dgr), first)
        dx2b = dx2.astype(BF16)
        dx2b_ref[...] = dx2b
        for c0 in range(0, D_FF, FF_CHUNK):
            cols = slice(c0, c0 + FF_CHUNK)
            dact = _dot_nt(dx2b, wd_ref[cols, :])
            gt = gt_s[:, cols].astype(F32)
            sg = _sigmoid(gt)
            dgt_ref[:, cols] = (dact * up_s[:, cols].astype(F32) * sg * (1.0 + gt * (1.0 - sg))).astype(BF16)
            dup_ref[:, cols] = (dact * gt * sg).astype(BF16)
        dh = _dot(dgt_ref[...], wg_ref[...]) + _dot(dup_ref[...], wu_ref[...])
        dxn, dgr2 = _rms_bwd(dh, x1, g2_ref[...])
        _acc(dg2_ref, _colsum8(dgr2), first)
        dx1 = dx2 + dxn
        dx1_ref[...] = dx1
        dx1b = dx1.astype(BF16)
        dx1b_ref[...] = dx1b
        dy = _dot_nt(dx1b, wo_ref[...])
        dya_ref[...] = dy[:, :RW]
        dyb_ref[...] = dy[:, RW:]

    row = pl.BlockSpec((tm, D_MODEL), lambda i: (i, 0))
    wide = pl.BlockSpec((tm, D_FF), lambda i: (i, 0))
    half = pl.BlockSpec((tm, RW), lambda i: (i, 0))
    wsp = pl.BlockSpec((D_FF, D_MODEL), lambda i: (0, 0))
    vec = pl.BlockSpec((1, D_MODEL), lambda i: (0, 0))
    part = pl.BlockSpec((8, D_MODEL), lambda i: (0, 0))
    bf = lambda n: jax.ShapeDtypeStruct((t, n), BF16)
    return pl.pallas_call(
        body, name="ffn_all", grid=(t // tm,),
        in_specs=[row, row, wsp, wsp, wsp, pl.BlockSpec((D_MODEL, D_MODEL), lambda i: (0, 0)), vec, vec, row],
        out_specs=[row, wide, row, wide, wide, row, row, half, half, pl.BlockSpec((8, 128), lambda i: (0, 0)), part, part],
        out_shape=[bf(D_MODEL), bf(D_FF), bf(D_MODEL), bf(D_FF), bf(D_FF), bf(D_MODEL),
                   jax.ShapeDtypeStruct((t, D_MODEL), F32), jax.ShapeDtypeStruct((t, RW), F32),
                   jax.ShapeDtypeStruct((t, RW), F32), jax.ShapeDtypeStruct((8, 128), F32),
                   jax.ShapeDtypeStruct((8, D_MODEL), F32), jax.ShapeDtypeStruct((8, D_MODEL), F32)],
        scratch_shapes=[pltpu.VMEM((tm, D_FF), BF16), pltpu.VMEM((tm, D_FF), BF16)],
        compiler_params=_params(("arbitrary",)),
    )(x, ycat, wg, wu, wd, wout, g2, gf, tgt)


def _wgrad(a, b, tk, tn, name):
    t, kdim = a.shape
    ndim = b.shape[1]

    def body(a_ref, b_ref, o_ref):
        o_ref[...] = _dot_tn(a_ref[...], b_ref[...])

    return pl.pallas_call(
        body, name=name, grid=(kdim // tk, ndim // tn),
        in_specs=[pl.BlockSpec((t, tk), lambda i, j: (0, i)), pl.BlockSpec((t, tn), lambda i, j: (0, j))],
        out_specs=pl.BlockSpec((tk, tn), lambda i, j: (i, j)),
        out_shape=jax.ShapeDtypeStruct((kdim, ndim), F32),
        compiler_params=_params(("parallel", "parallel")),
    )(a, b)


def _post_bwd(dya, y, r, k2, v, g, lnw, lnb, rk):
    t = y.shape[0]
    tm = _POST_TM

    def body(d_ref, y_ref, r_ref, k_ref, v_ref, g_ref, lnw_ref, lnb_ref, rk_ref,
             dy_ref, dr_ref, dk_ref, dv_ref, dg_ref, dlnw_ref, dlnb_ref, drk_ref):
        first = pl.program_id(0) == 0
        ones = jnp.ones((tm, 1), F32)
        prim = (y_ref[...], r_ref[...], k_ref[...], v_ref[...], g_ref[...],
                ones * lnw_ref[...], ones * lnb_ref[...], ones * rk_ref[...])
        _, vjp = jax.vjp(_post_fn, *prim)
        dy, dr, dk, dv, dg, dlnw, dlnb, drk = vjp(d_ref[...])
        dy_ref[...] = dy
        dr_ref[...] = dr
        dk_ref[...] = dk
        dv_ref[...] = dv
        dg_ref[...] = dg
        _acc(dlnw_ref, _colsum8(dlnw), first)
        _acc(dlnb_ref, _colsum8(dlnb), first)
        _acc(drk_ref, _colsum8(drk), first)

    row = pl.BlockSpec((tm, RW), lambda i: (i, 0))
    vec = pl.BlockSpec((1, RW), lambda i: (0, 0))
    part = pl.BlockSpec((8, RW), lambda i: (0, 0))
    return pl.pallas_call(
        body, name="rwkv_post_bwd", grid=(t // tm,),
        in_specs=[row] * 6 + [vec] * 3, out_specs=[row] * 5 + [part] * 3,
        out_shape=[jax.ShapeDtypeStruct((t, RW), F32)] * 5 + [jax.ShapeDtypeStruct((8, RW), F32)] * 3,
        compiler_params=_params(("arbitrary",)),
    )(dya, y, r, k2, v, g, lnw, lnb, rk)


def _wkv_bwd(dy, s0s, r, lw, k2, v, kk, a):
    t = r.shape[0]
    nc = t // CHUNK

    def body(dy_ref, s_ref, r_ref, lw_ref, k_ref, v_ref, kk_ref, a_ref,
             dr_ref, dlw_ref, dk_ref, dv_ref, dkk_ref, da_ref, ds):
        @pl.when(pl.program_id(0) == 0)
        def _():
            ds[...] = jnp.zeros_like(ds)

        _, vjp = jax.vjp(_wkv_chunk_fn, s_ref[0],
                         *[_pairs(ref) for ref in (r_ref, lw_ref, k_ref, v_ref, kk_ref, a_ref)])
        res = vjp((_pairs(dy_ref), ds[...]))
        ds[...] = res[0]
        for ref, val in zip((dr_ref, dlw_ref, dk_ref, dv_ref, dkk_ref, da_ref), res[1:]):
            for p in range(N_PAIR):
                ref[:, 128 * p:128 * (p + 1)] = val[p]

    blk = pl.BlockSpec((CHUNK, RW), lambda c: (nc - 1 - c, 0))
    return pl.pallas_call(
        body, name="wkv_bwd", grid=(nc,),
        in_specs=[blk, pl.BlockSpec((1, N_PAIR, 128, 128), lambda c: (nc - 1 - c, 0, 0, 0))] + [blk] * 6,
        out_specs=[blk] * 6,
        out_shape=[jax.ShapeDtypeStruct((t, RW), F32)] * 6,
        scratch_shapes=[pltpu.VMEM((N_PAIR, 128, 128), F32)],
        compiler_params=_params(("arbitrary",)),
    )(dy, s0s, r, lw, k2, v, kk, a)


def _prep_in_proj_bwd(proj, pw, douts, dq, dk, dv, win, x, g1, dx1):
    t = proj.shape[0]
    tm = _PREP_TM
    nt = t // tm

    def body(p_ref, l8_ref, mu, w0, w2p, a0, a2p, g2, k_k, k_a, dr, dr2, dlw, dk2, dk22, dv, dv2, dkk, da, dg,
             dq_ref, dkq_ref, dvq_ref, w_ref, x_ref, g1_ref, dx1_ref,
             dproj_ref, dx_ref, dg1_ref, dmu_ref, dw0_ref, dw2_ref, da0_ref, da2_ref, dg2_ref, dkk_ref, dka_ref, carry):
        i = pl.program_id(0)
        first = i == 0

        @pl.when(first)
        def _():
            carry[...] = jnp.zeros_like(carry)

        p = p_ref[...]
        pprev = _shifted(p, l8_ref[...], i == nt - 1)
        ones = jnp.ones((tm, 1), F32)
        prim = (p, pprev, ones * mu[...], ones * w0[...], w2p[...], ones * a0[...], a2p[...], g2[...],
                ones * k_k[...], ones * k_a[...])
        _, vjp = jax.vjp(_prep_fn, *prim)
        dp, dpp, dmu, dw0, dw2, da0, da2, dg2, dkk_, dka = vjp(
            (dr[...] + dr2[...], dlw[...], dk2[...] + dk22[...], dv[...] + dv2[...], dkk[...], da[...], dg[...]))
        up = pltpu.roll(dpp, tm - 1, axis=0)
        rid = lax.broadcasted_iota(jnp.int32, dpp.shape, 0)
        dpa = dp + jnp.where(rid == tm - 1, carry[0:1, :], up)
        carry[...] = jnp.broadcast_to(dpp[0:1, :], carry.shape)
        _acc(dmu_ref, _colsum8(dmu), first)
        _acc(dw0_ref, _colsum8(dw0), first)
        _acc(dw2_ref, dw2, first)
        _acc(da0_ref, _colsum8(da0), first)
        _acc(da2_ref, da2, first)
        _acc(dg2_ref, dg2, first)
        _acc(dkk_ref, _colsum8(dkk_), first)
        _acc(dka_ref, _colsum8(dka), first)
        parts = [dpa] + [ref[pr] for ref in (dq_ref, dkq_ref, dvq_ref) for pr in range(N_PAIR)]
        dproj = jnp.concatenate([z.astype(BF16) for z in parts], axis=1)
        dproj_ref[...] = dproj
        dxn, dgr = _rms_bwd(_dot(dproj, w_ref[...]), x_ref[...], g1_ref[...])
        dx_ref[...] = dx1_ref[...] + dxn
        _acc(dg1_ref, _colsum8(dgr), first)

    rev = lambda i: (nt - 1 - i, 0)
    row = pl.BlockSpec((tm, RW), rev)
    wide = pl.BlockSpec((tm, D_MODEL), rev)
    pair = pl.BlockSpec((N_PAIR, tm, 128), lambda i: (0, nt - 1 - i, 0))
    part = lambda n: pl.BlockSpec((8, n), lambda i: (0, 0))
    mat = pl.BlockSpec((128, RW), lambda i: (0, 0))
    return pl.pallas_call(
        body, name="prep_in_proj_bwd", grid=(nt,),
        in_specs=[pl.BlockSpec((tm, SHIFT_COLS), rev),
                  pl.BlockSpec((8, SHIFT_COLS), lambda i: (jnp.maximum((nt - 1 - i) * (tm // 8) - 1, 0), 0))]
                 + _prep_specs(tm) + [row] * 10
                 + [pair] * 3 + [pl.BlockSpec((IN_COLS, D_MODEL), lambda i: (0, 0)), wide,
                                 pl.BlockSpec((1, D_MODEL), lambda i: (0, 0)), wide],
        out_specs=[pl.BlockSpec((tm, IN_COLS), rev), wide, part(D_MODEL), part(SHIFT_COLS), part(RW), mat, part(RW), mat,
                   mat, part(RW), part(RW)],
        out_shape=[jax.ShapeDtypeStruct((t, IN_COLS), BF16), jax.ShapeDtypeStruct((t, D_MODEL), F32),
                   jax.ShapeDtypeStruct((8, D_MODEL), F32), jax.ShapeDtypeStruct((8, SHIFT_COLS), F32),
                   jax.ShapeDtypeStruct((8, RW), F32), jax.ShapeDtypeStruct((128, RW), F32),
                   jax.ShapeDtypeStruct((8, RW), F32), jax.ShapeDtypeStruct((128, RW), F32),
                   jax.ShapeDtypeStruct((128, RW), F32), jax.ShapeDtypeStruct((8, RW), F32),
                   jax.ShapeDtypeStruct((8, RW), F32)],
        scratch_shapes=[pltpu.VMEM((8, SHIFT_COLS), F32)],
        compiler_params=_params(("arbitrary",)),
    )(proj, proj, *pw, *douts, dq, dk, dv, win, x, g1, dx1)


def _combine_bwd(dyb, o, l, og):
    t = dyb.shape[0]
    tm = _COMB_TM

    def body(d_ref, o_ref, l_ref, og_ref, do_ref, dl_ref, dog_ref):
        ones = jnp.ones((tm, 1), F32)
        dog = []
        for p in range(N_PAIR):
            cols = slice(128 * p, 128 * (p + 1))
            _, vjp = jax.vjp(_combine_fn, o_ref[0, p], o_ref[1, p], o_ref[2, p], l_ref[0, p], l_ref[1, p], l_ref[2, p],
                             ones * og_ref[:, cols])
            res = vjp(d_ref[:, cols])
            for b in range(3):
                do_ref[b, p] = res[b]
                dl_ref[b, p] = res[3 + b]
            dog.append(_colsum8(res[6]))
        _acc(dog_ref, jnp.concatenate(dog, axis=1), pl.program_id(0) == 0)

    blk = pl.BlockSpec((3, N_PAIR, tm, 128), lambda i: (0, 0, i, 0))
    return pl.pallas_call(
        body, name="attn_combine_bwd", grid=(t // tm,),
        in_specs=[pl.BlockSpec((tm, RW), lambda i: (i, 0)), blk, blk, pl.BlockSpec((1, RW), lambda i: (0, 0))],
        out_specs=[blk, blk, pl.BlockSpec((8, RW), lambda i: (0, 0))],
        out_shape=[jax.ShapeDtypeStruct((3, N_PAIR, t, 128), F32)] * 2 + [jax.ShapeDtypeStruct((8, RW), F32)],
        compiler_params=_params(("arbitrary",)),
    )(dyb, o, l, og)


def _attn_bwd(do, dl, o, lse, qkv):
    t = qkv.shape[2]

    def body(do_ref, dl_ref, o_ref, l_ref, q_ref, k_ref, v_ref, dq_ref, dk_ref, dv_ref):
        @pl.when(pl.program_id(1) == 0)
        def _():
            for ref in (dq_ref, dk_ref, dv_ref):
                ref[...] = jnp.zeros_like(ref)

        def unit(di, places, has_prev):
            cur = [_dilated_rows(d, r, n) for d, r, n in places]
            q, kc, vc = [_take(ref, (0,), cur) for ref in (q_ref, k_ref, v_ref)]
            kp = vp = None
            if has_prev:
                prv = [_dilated_rows(d, r, n - 1) for d, r, n in places]
                kp, vp = [_take(ref, (0,), prv) for ref in (k_ref, v_ref)]
            res = _attn_block_bwd(q, kc, vc, kp, vp, *[_take(ref, (0,), cur) for ref in (o_ref, l_ref, do_ref, dl_ref)])
            _put(dq_ref, (), cur, res[0], add=True)
            _put(dk_ref, (), cur, res[1], add=True)
            _put(dv_ref, (), cur, res[2], add=True)
            if has_prev:
                _put(dk_ref, (), prv, res[3], add=True)
                _put(dv_ref, (), prv, res[4], add=True)

        _for_each_sequence(t, unit)

    spec = lambda j: pl.BlockSpec((1, ATTN_GROUP, t, 128), lambda i, b: (j, i, 0, 0))
    branch = pl.BlockSpec((1, ATTN_GROUP, t, 128), lambda i, b: (b, i, 0, 0))
    out = pl.BlockSpec((ATTN_GROUP, t, 128), lambda i, b: (i, 0, 0))
    return pl.pallas_call(
        body, name="attn_bwd", grid=(N_PAIR // ATTN_GROUP, len(DILATIONS)),
        in_specs=[branch] * 4 + [spec(0), spec(1), spec(2)], out_specs=[out] * 3,
        out_shape=[jax.ShapeDtypeStruct((N_PAIR, t, 128), F32)] * 3,
        compiler_params=_params(("parallel", "arbitrary")),
    )(do, dl, o, lse, qkv, qkv, qkv)


def _pad_lora(w, lo):
    z = jnp.zeros((64, RW), F32)
    return jnp.concatenate([w, z], axis=0) if lo == 0 else jnp.concatenate([z, w], axis=0)


def _local_step(x, tgt, win, vecs, w2, a2, g2m, get_rest, send_rest):
    pw = (vecs["mu_shift"], vecs["decay_w0"], _pad_lora(w2, 0), vecs["iclr_a0"], _pad_lora(a2, 64), g2m,
          vecs["k_k"], vecs["k_a"])
    h, proj, qkv, r, lw, k2, v, kk, a, g = _in_proj_prep(x, vecs["mix_norm_g"], win, pw)
    y, s0s = _wkv_fwd(r, lw, k2, v, kk, a)
    o_att, l_att = _attn_fwd(qkv)
    ycat = _mixers_out(y, r, k2, v, g, vecs["ln_x_w"], vecs["ln_x_b"], vecs["r_k"], o_att, l_att, vecs["attn_out_g"])
    wout, wg, wu, wd = get_rest(ycat)
    h2, act, dx2b, dgt, dup, dx1b, dx1, dya, dyb, loss8, dgf, dg2n = _ffn_all(
        x, ycat, wg, wu, wd, wout, vecs["ffn_norm_g"], vecs["final_norm_g"], tgt)
    gw = {
        "w_down": _wgrad(act, dx2b, 1408, 1024, "wgrad_down"),
        "w_gate": _wgrad(dgt, h2, 1408, 1024, "wgrad_gate"),
        "w_up": _wgrad(dup, h2, 1408, 1024, "wgrad_up"),
        "w_out": _wgrad(ycat, dx1b, 1024, 1024, "wgrad_out"),
    }

    lnw = vecs["ln_x_w"] + send_rest(gw)[0, 0]
    dy, dr_p, dk2_p, dv_p, dg, dlnw, dlnb, drk = _post_bwd(dya, y, r, k2, v, g, lnw, vecs["ln_x_b"], vecs["r_k"])
    dr_s, dlw, dk2_s, dv_s, dkk, da = _wkv_bwd(dy, s0s, r, lw, k2, v, kk, a)
    do_att, dl_att, dog = _combine_bwd(dyb, o_att, l_att, vecs["attn_out_g"])
    dq, dk, dv = _attn_bwd(do_att, dl_att, o_att, l_att, qkv)
    dproj, dx, dg1, dmu, dw0, dw2p, da0, da2p, dg2m, dk_k, dk_a = _prep_in_proj_bwd(
        proj, pw, (dr_p, dr_s, dlw, dk2_p, dk2_s, dv_p, dv_s, dkk, da, dg), dq, dk, dv, win, x, vecs["mix_norm_g"], dx1)
    gw["w_in"] = _wgrad(dproj, h, 1664, 1024, "wgrad_in")
    gw["decay_w2"] = dw2p[:64]
    gw["iclr_a2"] = da2p[64:]
    gw["gate_g2"] = dg2m
    gv = {"mix_norm_g": dg1, "mu_shift": dmu, "decay_w0": dw0, "iclr_a0": da0, "k_k": dk_k, "k_a": dk_a, "r_k": drk,
          "ln_x_w": dlnw, "ln_x_b": dlnb, "attn_out_g": dog, "ffn_norm_g": dg2n, "final_norm_g": dgf}
    return loss8, dx, gw, gv


N_CHIP = 4
N_DEV = 8
MATS = ("w_in", "w_out", "w_gate", "w_up", "w_down")
LORAS = ("decay_w2", "iclr_a2", "gate_g2")
VECS = (("mix_norm_g", 1024), ("mu_shift", 1792), ("decay_w0", 512), ("iclr_a0", 512), ("k_k", 512), ("k_a", 512),
        ("r_k", 512), ("ln_x_w", 512), ("ln_x_b", 512), ("attn_out_g", 512), ("ffn_norm_g", 1024),
        ("final_norm_g", 1024))
N_VEC = sum(n for _, n in VECS)
N_SMALL = N_VEC + 128
ANY = pl.BlockSpec(memory_space=pl.ANY)


def _flip(v, f):
    return 1 - v if f else v


class _Me:
    def __init__(self, mode):
        x, y, c = lax.axis_index("x"), lax.axis_index("y"), lax.axis_index("c")
        self.core, self.chip, self.dev = c, 2 * x + y, 4 * x + 2 * y + c
        self.sibling = (x, y, 1 - c)
        if mode == "chips":
            self.peers = [(px, py, c) for px, py in ((1 - x, y), (x, 1 - y), (1 - x, 1 - y))]
        else:
            self.peers = [(_flip(x, k & 4), _flip(y, k & 2), _flip(c, k & 1)) for k in range(1, N_DEV)]


def _half(core, rows):
    h = rows // 2
    return pl.ds(pl.multiple_of(core * h, h), h)


_BY_CHIP = ("gather", "chipsum")


def _peer_copy(srcs, dsts, kinds, send_sems, recv_sems, me, j, i, incoming):
    px, py, pc = me.peers[j]
    pchip, pdev = 2 * px + py, 4 * px + 2 * py + pc
    src, dst, kind = srcs[i], dsts[i], kinds[i]
    if kind == "gather":
        rows = _half(me.core, src.shape[1])
        src, dst = src.at[me.chip, rows], dst.at[pchip if incoming else me.chip, rows]
    elif kind == "scatter":
        src, dst = src.at[pchip, _half(pc, src.shape[1])], dst.at[pdev if incoming else me.dev]
    elif kind == "chipsum":
        src, dst = src.at[pchip], dst.at[pchip if incoming else me.chip]
    else:
        dst = dst.at[pdev if incoming else me.dev]
    n = len(srcs)
    return pltpu.make_async_remote_copy(src_ref=src, dst_ref=dst, send_sem=send_sems.at[n * j + i],
                                        recv_sem=recv_sems.at[n * j + i], device_id=(px, py, pc), device_id_type=MESH)


def _mode(kinds):
    return "chips" if kinds[0] in _BY_CHIP else "devs"


def _npeer(kinds):
    return N_CHIP - 1 if kinds[0] in _BY_CHIP else N_DEV - 1


def _sibling_halves(gs, name):
    n = len(gs)

    def body(*refs):
        srcs, dsts, send_sems, recv_sems = refs[:n], refs[n:2 * n], refs[2 * n], refs[2 * n + 1]
        me = _Me("chips")

        def copy(i, p):
            return pltpu.make_async_remote_copy(
                src_ref=srcs[i].at[p, _half(1 - me.core, srcs[i].shape[1])], dst_ref=dsts[i].at[p],
                send_sem=send_sems.at[N_CHIP * i + p], recv_sem=recv_sems.at[N_CHIP * i + p],
                device_id=me.sibling, device_id_type=MESH)

        copies = [copy(i, p) for i in range(n) for p in range(N_CHIP)]
        for cp in copies:
            cp.start()
        for cp in copies:
            cp.wait()

    return pl.pallas_call(
        body, name=name, in_specs=[ANY] * n, out_specs=[ANY] * n,
        out_shape=[jax.ShapeDtypeStruct((N_CHIP, g.shape[1] // 2, g.shape[2]), g.dtype) for g in gs],
        scratch_shapes=[pltpu.SemaphoreType.DMA((N_CHIP * n,)), pltpu.SemaphoreType.DMA((N_CHIP * n,))],
    )(*gs)


def _add_halves(g, other, core, tr, name):
    _, h, cols = other.shape

    def body(core_ref, g_ref, o_ref, out_ref):
        out_ref[...] = (g_ref[...].astype(F32) + o_ref[...].astype(F32)).astype(BF16)

    blk = lambda off: pl.BlockSpec((1, tr, cols), lambda p, i, core_ref: (p, core_ref[0] * (h // tr) * off + i, 0))
    return pl.pallas_call(
        body, name=name,
        grid_spec=pltpu.PrefetchScalarGridSpec(num_scalar_prefetch=1, grid=(N_CHIP, h // tr),
                                               in_specs=[blk(1), blk(0)], out_specs=blk(0)),
        out_shape=jax.ShapeDtypeStruct(other.shape, BF16),
        compiler_params=_params(("parallel", "parallel")),
    )(core, g, other)


def _swap_gathered(lands, name):
    n = len(lands)

    def body(*refs):
        dsts, send_sems, recv_sems = refs[n:2 * n], refs[2 * n], refs[2 * n + 1]
        me = _Me("chips")

        def copy(j, i, incoming):
            px, py, _ = me.peers[j]
            rows_out, rows_in = _half(me.core, dsts[i].shape[1]), _half(1 - me.core, dsts[i].shape[1])
            return pltpu.make_async_remote_copy(
                src_ref=dsts[i].at[2 * px + py, rows_out], dst_ref=dsts[i].at[2 * px + py, rows_in if incoming else rows_out],
                send_sem=send_sems.at[n * j + i], recv_sem=recv_sems.at[n * j + i], device_id=me.sibling, device_id_type=MESH)

        sends = [copy(j, i, False) for j in range(3) for i in range(n)]
        for cp in sends:
            cp.start()
        for j in range(3):
            for i in range(n):
                copy(j, i, True).wait_recv()
        for cp in sends:
            cp.wait_send()

    return pl.pallas_call(
        body, name=name, in_specs=[ANY] * n, out_specs=[ANY] * n,
        out_shape=[jax.ShapeDtypeStruct(l.shape, l.dtype) for l in lands],
        input_output_aliases={i: i for i in range(n)},
        scratch_shapes=[pltpu.SemaphoreType.DMA((3 * n,)), pltpu.SemaphoreType.DMA((3 * n,))],
    )(*lands)


def _join_halves(sums, name):
    n = len(sums)

    def body(*refs):
        dsts, send_sems, recv_sems = refs[n:2 * n], refs[2 * n], refs[2 * n + 1]
        me = _Me("chips")

        def copy(i, incoming):
            mine, other = _half(me.core, dsts[i].shape[0]), _half(1 - me.core, dsts[i].shape[0])
            return pltpu.make_async_remote_copy(src_ref=dsts[i].at[mine], dst_ref=dsts[i].at[other if incoming else mine],
                                                send_sem=send_sems.at[i], recv_sem=recv_sems.at[i],
                                                device_id=me.sibling, device_id_type=MESH)

        sends = [copy(i, False) for i in range(n)]
        for cp in sends:
            cp.start()
        for i in range(n):
            copy(i, True).wait_recv()
        for cp in sends:
            cp.wait_send()

    return pl.pallas_call(
        body, name=name, in_specs=[ANY] * n, out_specs=[ANY] * n,
        out_shape=[jax.ShapeDtypeStruct(s.shape, s.dtype) for s in sums],
        input_output_aliases={i: i for i in range(n)},
        scratch_shapes=[pltpu.SemaphoreType.DMA((n,)), pltpu.SemaphoreType.DMA((n,))],
    )(*sums)


HBM = pl.BlockSpec(memory_space=pltpu.HBM)
SEM = pl.BlockSpec(memory_space=pltpu.SEMAPHORE)
EFFECT = pltpu.SideEffectType.DATAFLOW_SIDE_EFFECTING


def _swap_start(arrs, lands, kinds, name):
    n = len(lands)
    ops = list(lands) if arrs is None else [*arrs, *lands]
    k = len(ops)

    def body(*refs):
        srcs, dsts, send_sems, recv_sems, token = refs[:n], refs[k - n:k], refs[k], refs[k + 1], refs[-1]
        me = _Me(_mode(kinds))
        for j in range(len(me.peers)):
            for i in range(n):
                _peer_copy(srcs, dsts, kinds, send_sems, recv_sems, me, j, i, False).start()
        token[...] = jnp.zeros_like(token)

    ns = _npeer(kinds) * n
    outs = pl.pallas_call(
        body, name=name,
        out_shape=(pltpu.SemaphoreType.DMA((ns,)), pltpu.SemaphoreType.DMA((ns,)),
                   *[pltpu.HBM(a.shape, a.dtype) for a in ops], jax.ShapeDtypeStruct((8, 128), F32)),
        in_specs=[HBM] * k, out_specs=(SEM, SEM, *[HBM] * k, pl.BlockSpec(memory_space=pltpu.VMEM)),
        input_output_aliases={i: 2 + i for i in range(k)},
        compiler_params=pltpu.CompilerParams(has_side_effects=EFFECT),
    )(*[pltpu.with_memory_space_constraint(a, pltpu.HBM) for a in ops])
    return outs[0], outs[1], outs[2:2 + k - n], outs[2 + k - n:2 + k], outs[-1]


def _swap_wait(send_sems, recv_sems, srcs_thru, lands_thru, after, kinds, name):
    n = len(lands_thru)
    ops = [*srcs_thru, *lands_thru]
    k = len(ops)

    def body(*refs):
        srcs, dsts, s_sems, r_sems = refs[:n], refs[k - n:k], refs[k], refs[k + 1]
        me = _Me(_mode(kinds))
        for j in range(len(me.peers)):
            for i in range(n):
                cp = _peer_copy(srcs, dsts, kinds, s_sems, r_sems, me, j, i, True)
                cp.wait_send()
                cp.wait_recv()

    outs = pl.pallas_call(
        body, name=name,
        out_shape=tuple(pltpu.HBM(a.shape, a.dtype) for a in ops),
        in_specs=[HBM] * k + [SEM, SEM, ANY], out_specs=tuple([HBM] * k),
        input_output_aliases={i: i for i in range(k)},
        compiler_params=pltpu.CompilerParams(has_side_effects=EFFECT),
    )(*ops, send_sems, recv_sems, after)
    return outs[k - n:]


def _adamw(w, g, m, v):
    m = ADAM_B1 * m + (1.0 - ADAM_B1) * g
    v = ADAM_B2 * v + (1.0 - ADAM_B2) * (g * g)
    m_hat = m / (1.0 - ADAM_B1 ** ADAM_STEP)
    v_hat = v / (1.0 - ADAM_B2 ** ADAM_STEP)
    delta = -ADAM_LR * (m_hat / (jnp.sqrt(v_hat) + ADAM_EPS) + ADAM_WD * w)
    return delta, m, v


def _reduce8(rbuf, core, tr, name):
    slots, h, cols = rbuf.shape

    def body(core_ref, r_ref, g_ref):
        g = r_ref[0].astype(F32)
        for s in range(1, slots):
            g = g + r_ref[s].astype(F32)
        g_ref[...] = g

    return pl.pallas_call(
        body, name=name,
        grid_spec=pltpu.PrefetchScalarGridSpec(
            num_scalar_prefetch=1, grid=(h // tr,),
            in_specs=[pl.BlockSpec((slots, tr, cols), lambda i, core_ref: (0, i, 0))],
            out_specs=pl.BlockSpec((tr, cols), lambda i, core_ref: (core_ref[0] * (h // tr) + i, 0))),
        out_shape=jax.ShapeDtypeStruct((2 * h, cols), F32),
        compiler_params=_params(("parallel",)),
    )(core, rbuf)


def _adamw_call(g, w, m, v, tr, name):
    _, rows, cols = w.shape

    def body(g_in, w_ref, m_ref, v_ref, g_ref, d_ref, nm_ref, nv_ref):
        g = g_in[...]
        g_ref[0] = g
        d_ref[0], nm_ref[0], nv_ref[0] = _adamw(w_ref[0], g, m_ref[0], v_ref[0])

    row = pl.BlockSpec((1, tr, cols), lambda i: (0, i, 0))
    return pl.pallas_call(
        body, name=name, grid=(rows // tr,),
        in_specs=[pl.BlockSpec((tr, cols), lambda i: (i, 0)), row, row, row], out_specs=[row] * 4,
        out_shape=[jax.ShapeDtypeStruct(w.shape, F32)] * 4,
        compiler_params=_params(("parallel",)),
    )(g, w, m, v)


def _rowsum_small(parts, loss8):
    def body(*refs):
        out = refs[-1]
        c0 = 0
        for ref in refs[:-1]:
            n = ref.shape[1]
            out[:, c0:c0 + n] = jnp.sum(ref[...], axis=0, keepdims=True)
            c0 += n

    return pl.pallas_call(body, name="rowsum_small", out_shape=jax.ShapeDtypeStruct((1, N_SMALL), F32))(*parts, loss8)


def _reduce_adamw_small(sbuf, ws, ms, vs):
    nv = len(ws)

    def body(*refs):
        s_ref, ins, outs = refs[0], refs[1:1 + 3 * nv], refs[1 + 3 * nv:]
        tot = s_ref[0]
        for s in range(1, N_DEV):
            tot = tot + s_ref[s]
        c0 = 0
        for i in range(nv):
            n = ins[i].shape[1]
            g = tot[:, c0:c0 + n]
            outs[i][...] = g
            outs[nv + i][...], outs[2 * nv + i][...], outs[3 * nv + i][...] = _adamw(
                ins[i][...], g, ins[nv + i][...], ins[2 * nv + i][...])
            c0 += n
        outs[-1][...] = tot[:, c0:]

    return pl.pallas_call(
        body, name="reduce_adamw_small",
        out_shape=[jax.ShapeDtypeStruct(a.shape, F32) for a in ws] * 4 + [jax.ShapeDtypeStruct((1, 128), F32)],
    )(sbuf, *ws, *ms, *vs)


_TRANSPOSED = ("w_in", "w_gate", "w_up")
_ROW_STACKED = MATS
_ADAM_TILE = {"w_in": 208, "w_out": 256, "w_gate": 176, "w_up": 176, "w_down": 176, "lora": 256}
_SUM_TILE = {"w_in": 208, "w_out": 128, "w_gate": 176, "w_up": 176, "w_down": 176, "lora": 128}


def _full(n, stacked):
    p, r, c = stacked.shape
    if n in _ROW_STACKED:
        return stacked.reshape(p * r, c)
    return jnp.transpose(stacked, (1, 0, 2)).reshape(r, p * c)


def _by_chip(n, full):
    if n in _ROW_STACKED:
        return full.reshape(N_CHIP, full.shape[0] // N_CHIP, full.shape[1])
    r, c = full.shape
    return jnp.transpose(full.reshape(r, N_CHIP, c // N_CHIP), (1, 0, 2))


def _with_own(land_shape, dtype, own, slot):
    return lax.dynamic_update_slice(lax.empty(land_shape, dtype), own[None], (slot,) + (0,) * own.ndim)


def _cast_into_slot(a, chip, tr, name, after=None):
    rows, cols = a.shape

    def body(chip_ref, a_ref, *rest):
        rest[-1][0] = a_ref[...].astype(BF16)

    extra = [] if after is None else [after]
    return pl.pallas_call(
        body, name=name,
        grid_spec=pltpu.PrefetchScalarGridSpec(
            num_scalar_prefetch=1, grid=(rows // tr,),
            in_specs=[pl.BlockSpec((tr, cols), lambda i, chip_ref: (i, 0))] + [ANY] * len(extra),
            out_specs=pl.BlockSpec((1, tr, cols), lambda i, chip_ref: (chip_ref[0], i, 0))),
        out_shape=jax.ShapeDtypeStruct((N_CHIP, rows, cols), BF16),
        compiler_params=_params(("parallel",)),
    )(chip, a, *extra)


def kernel(x, mix_norm_g, w_in, mu_shift, decay_w0, decay_w2, iclr_a0, iclr_a2, gate_g2, k_k, k_a, r_k, ln_x_w, ln_x_b, attn_out_g, w_out, ffn_norm_g, w_gate, w_up, w_down, final_norm_g, loss_target, m_mix_norm_g, m_w_in, m_mu_shift, m_decay_w0, m_decay_w2, m_iclr_a0, m_iclr_a2, m_gate_g2, m_k_k, m_k_a, m_r_k, m_ln_x_w, m_ln_x_b, m_attn_out_g, m_w_out, m_ffn_norm_g, m_w_gate, m_w_up, m_w_down, m_final_norm_g, v_mix_norm_g, v_w_in, v_mu_shift, v_decay_w0, v_decay_w2, v_iclr_a0, v_iclr_a2, v_gate_g2, v_k_k, v_k_a, v_r_k, v_ln_x_w, v_ln_x_b, v_attn_out_g, v_w_out, v_ffn_norm_g, v_w_gate, v_w_up, v_w_down, v_final_norm_g):
    names = ("mix_norm_g", "w_in", "mu_shift", "decay_w0", "decay_w2", "iclr_a0", "iclr_a2", "gate_g2", "k_k", "k_a",
             "r_k", "ln_x_w", "ln_x_b", "attn_out_g", "w_out", "ffn_norm_g", "w_gate", "w_up", "w_down", "final_norm_g")
    w = dict(zip(names, (mix_norm_g, w_in, mu_shift, decay_w0, decay_w2, iclr_a0, iclr_a2, gate_g2, k_k, k_a, r_k,
                         ln_x_w, ln_x_b, attn_out_g, w_out, ffn_norm_g, w_gate, w_up, w_down, final_norm_g)))
    m = dict(zip(names, (m_mix_norm_g, m_w_in, m_mu_shift, m_decay_w0, m_decay_w2, m_iclr_a0, m_iclr_a2, m_gate_g2,
                         m_k_k, m_k_a, m_r_k, m_ln_x_w, m_ln_x_b, m_attn_out_g, m_w_out, m_ffn_norm_g, m_w_gate,
                         m_w_up, m_w_down, m_final_norm_g)))
    v = dict(zip(names, (v_mix_norm_g, v_w_in, v_mu_shift, v_decay_w0, v_decay_w2, v_iclr_a0, v_iclr_a2, v_gate_g2,
                         v_k_k, v_k_a, v_r_k, v_ln_x_w, v_ln_x_b, v_attn_out_g, v_w_out, v_ffn_norm_g, v_w_gate,
                         v_w_up, v_w_down, v_final_norm_g)))
    first = ("w_in", "lora")
    rest = ("w_out", "w_gate", "w_up", "w_down")
    xi, yi, ci = lax.axis_index("x"), lax.axis_index("y"), lax.axis_index("c")
    my_chip, my_dev = 2 * xi + yi, 4 * xi + 2 * yi + ci
    gather, scatter = ("gather",) * 4, ("scatter",) * 4

    def stored(d):
        out = {n: jnp.transpose(d[n][0]) if n in _TRANSPOSED else d[n][0] for n in MATS}
        out["lora"] = jnp.concatenate([d[n][0] for n in LORAS], axis=0)
        return out

    ws, ms, vs = stored(w), stored(m), stored(v)
    lora_rows = [(0, 64), (64, 128), (128, 256)]
    chip = jnp.reshape(my_chip, (1,)).astype(jnp.int32)
    early = _swap_start(None, [_cast_into_slot(ws["w_in"], chip, _ADAM_TILE["w_in"], "cast_w_in"),
                               _with_own((N_CHIP,) + ws["lora"].shape, F32, ws["lora"], my_chip)], gather[:2],
                        "gather_first_start")
    lands = [_cast_into_slot(ws[n], chip, _ADAM_TILE[n], "cast_" + n, after=early[4]) for n in rest]
    ssem, rsem, srcs_thru, lands_thru, tok = _swap_start(None, lands, gather, "gather_rest_start")
    got = _swap_wait(early[0], early[1], early[2], early[3], tok, gather[:2], "gather_first_wait")
    win_all, lora_all = _swap_gathered(got, "gather_first_halves")
    win = _full("w_in", win_all)
    w2, a2, g2m = (_full(n, lora_all[:, a:b]) for n, (a, b) in zip(LORAS, lora_rows))

    vecs = {n: w[n].reshape(1, sz) for n, sz in VECS}
    vecs["mix_norm_g"] = vecs["mix_norm_g"] + tok[0, 0]

    def get_rest(after):
        halves = _swap_wait(ssem, rsem, srcs_thru, lands_thru, after, gather, "gather_rest_wait")
        return [_full(n, z) for n, z in zip(rest, _swap_gathered(halves, "gather_rest_halves"))]

    flight = []

    def my_half(g):
        h = g.shape[1] // 2
        return lax.dynamic_slice(g, (my_chip, ci * h, 0), (1, h, g.shape[2]))[0]

    def send_rest(gw):
        gs = [_by_chip(n, gw[n]).astype(BF16) for n in rest]
        into = [_with_own((N_DEV,) + my_half(g).shape, BF16, my_half(g), my_dev) for g in gs]
        flight.extend(_swap_start(gs, into, scatter, "exchange_rest_start"))
        return flight[4]

    loss8, dx, gw, gv = _local_step(x[0], loss_target[0], win, vecs, w2, a2, g2m, get_rest, send_rest)

    core = jnp.reshape(ci, (1,)).astype(jnp.int32)
    gs = [_by_chip("w_in", gw["w_in"]).astype(BF16),
          jnp.concatenate([_by_chip(n, gw[n]) for n in LORAS], axis=1).astype(BF16)]
    theirs = _sibling_halves(gs, "presum_halves")
    sums = [_add_halves(g, o, core, _SUM_TILE[n], "chipsum_" + n) for n, g, o in zip(first, gs, theirs)]
    own = [lax.dynamic_index_in_dim(s, my_chip, 0, keepdims=False) for s in sums]
    last = _swap_start(sums, [_with_own(s.shape, BF16, o, my_chip) for s, o in zip(sums, own)], ("chipsum",) * 2,
                       "exchange_first_start")
    small = _rowsum_small([gv[n] for n, _ in VECS], loss8 + last[4])
    vecs_out = _swap_start([small], [_with_own((N_DEV,) + small.shape, F32, small, my_dev)], ("all",),
                           "exchange_vectors_start")


    def update(group, rbufs, tag):
        sums = [_reduce8(rb, core, _SUM_TILE[n], "reduce_" + n) for n, rb in zip(group, rbufs)]
        gsum = _join_halves(sums, "join_halves_" + tag)
        out = {}
        for n, g in zip(group, gsum):
            r = _adamw_call(g, ws[n][None], ms[n][None], vs[n][None], _ADAM_TILE[n], "adamw_" + n)
            if n == "lora":
                for name, (a, b) in zip(LORAS, lora_rows):
                    out[name] = [z[:, a:b] for z in r]
            else:
                out[n] = [jnp.transpose(z[0])[None] for z in r] if n in _TRANSPOSED else r
        return out, r[1]

    res, done = update(rest, _swap_wait(flight[0], flight[1], flight[2], flight[3], vecs_out[4], scatter,
                                        "exchange_rest_wait"), "rest")
    got = _swap_wait(last[0], last[1], last[2], last[3], done, ("chipsum",) * 2, "exchange_first_wait")
    res_first, done = update(first, got, "first")
    res.update(res_first)
    sbuf = _swap_wait(vecs_out[0], vecs_out[1], vecs_out[2], vecs_out[3], done, ("all",), "exchange_vectors_wait")[0]
    rows = lambda d: [d[n].reshape(1, sz) for n, sz in VECS]
    small_res = _reduce_adamw_small(sbuf, rows(w), rows(m), rows(v))

    outs = []
    for k in range(4):
        piece = {n: r[k] for n, r in res.items()}
        for i, (n, _) in enumerate(VECS):
            piece[n] = small_res[k * len(VECS) + i].reshape(w[n].shape)
        outs.extend(piece[n] for n in names)
    return (small_res[-1][0, 0], dx[None], *outs)
```

```python
import jax
import jax.numpy as jnp
from jax import lax
from jax.experimental import pallas as pl
from jax.experimental.pallas import tpu as pltpu

F32 = jnp.float32
BF16 = jnp.bfloat16

D_MODEL = 1024
HEAD_DIM = 64
RW = 512
N_PAIR = RW // 128
SHIFT_COLS = 1792
IN_COLS = 3328
D_FF = 2816
FF_CHUNK = 256
NORM_EPS = 1e-6
GN_EPS = 64e-5
CHUNK = 64
SUB = 16
WKV_PASSES = 1
ATTN_PASSES = 1
ATTN_BLOCK = 128
DILATIONS = (1, 4, 16)
NEG = -1e30
ADAM_LR, ADAM_B1, ADAM_B2, ADAM_EPS, ADAM_WD, ADAM_STEP = 0.001, 0.9, 0.999, 1e-08, 0.01, 10
VMEM_LIMIT = 56 * 1024 * 1024
MESH = pl.DeviceIdType.MESH


def _params(sem=None, **kw):
    return pltpu.CompilerParams(dimension_semantics=sem, vmem_limit_bytes=VMEM_LIMIT, **kw)


def _dot(a, b):
    return lax.dot_general(a, b, (((1,), (0,)), ((), ())), preferred_element_type=F32)


def _dot_nt(a, b):
    return lax.dot_general(a, b, (((1,), (1,)), ((), ())), preferred_element_type=F32)


def _dot_tn(a, b):
    return lax.dot_general(a, b, (((0,), (0,)), ((), ())), preferred_element_type=F32)


_FORMS = {"nn": ((1,), (0,)), "nt": ((1,), (1,)), "tn": ((0,), (0,))}


def _dg(a, b, form):
    if a.ndim == 3 or b.ndim == 3:
        nb = a.shape[0] if a.ndim == 3 else b.shape[0]
        return jnp.stack([_dg(a[i] if a.ndim == 3 else a, b[i] if b.ndim == 3 else b, form) for i in range(nb)], axis=0)
    return lax.dot_general(a, b, (_FORMS[form], ((), ())), preferred_element_type=F32)


def _split2(x):
    hi = x.astype(BF16)
    return hi, (x - hi.astype(F32)).astype(BF16)


def _split3(x):
    hi = x.astype(BF16)
    rest = x - hi.astype(F32)
    mid = rest.astype(BF16)
    return hi, mid, (rest - mid.astype(F32)).astype(BF16)


def _mm_raw(a, b, form, mode):
    if mode == 1:
        return _dg(a.astype(BF16), b.astype(BF16), form)
    if mode == 3:
        ah, al = _split2(a)
        bh, bl = _split2(b)
        return _dg(ah, bh, form) + (_dg(ah, bl, form) + _dg(al, bh, form))
    if mode == "L3":
        ab = a.astype(BF16)
        b1, b2, b3 = _split3(b)
        if form == "nn":
            n = b.shape[-1]
            wide = _dg(ab, jnp.concatenate([b1, b2, b3], axis=-1), form)
            return wide[..., :n] + (wide[..., n:2 * n] + wide[..., 2 * n:])
        return _dg(ab, b1, form) + (_dg(ab, b2, form) + _dg(ab, b3, form))
    assert mode == "R3", mode
    bb = b.astype(BF16)
    a1, a2, a3 = _split3(a)
    if form in ("nn", "nt"):
        m = a.shape[-2]
        tall = _dg(jnp.concatenate([a1, a2, a3], axis=-2), bb, form)
        return tall[..., :m, :] + (tall[..., m:2 * m, :] + tall[..., 2 * m:, :])
    return _dg(a1, bb, form) + (_dg(a2, bb, form) + _dg(a3, bb, form))


def _mm(a, b, form, mode):
    @jax.custom_vjp
    def f(a, b):
        return _mm_raw(a, b, form, mode)

    def fwd(a, b):
        return _mm_raw(a, b, form, mode), (a, b)

    def bwd(res, ct):
        a, b = res
        la = {1: 1, 3: 3, "L3": None, "R3": "R3"}[mode]
        lb = {1: 1, 3: 3, "L3": "L3", "R3": None}[mode]
        if form == "nn":
            da = None if la is None else _mm_raw(ct, b, "nt", la)
            db = None if lb is None else _mm_raw(a, ct, "tn", lb)
        elif form == "nt":
            da = None if la is None else _mm_raw(ct, b, "nn", la)
            db = None if lb is None else _mm_raw(ct, a, "tn", "R3" if lb == "L3" else lb)
        else:
            da = None if la is None else _mm_raw(b, ct, "nt", "L3" if la == "R3" else la)
            db = None if lb is None else _mm_raw(a, ct, "nn", lb)
        return (jnp.zeros_like(a) if da is None else da, jnp.zeros_like(b) if db is None else db)

    f.defvjp(fwd, bwd)
    return f(a, b)


def _seg_ones(n):
    r = lax.broadcasted_iota(jnp.int32, (n, n), 0) // HEAD_DIM
    c = lax.broadcasted_iota(jnp.int32, (n, n), 1) // HEAD_DIM
    return (r == c).astype(F32)


def _segsum(x, seg):
    return _mm(x, seg, "nn", "R3")


def _rms_fwd(x, g):
    rstd = lax.rsqrt(jnp.mean(x * x, axis=-1, keepdims=True) + NORM_EPS)
    return x * rstd * g


def _rms_bwd(dy, x, g):
    rstd = lax.rsqrt(jnp.mean(x * x, axis=-1, keepdims=True) + NORM_EPS)
    xn = x * rstd
    dxn = dy * g
    dx = rstd * (dxn - xn * jnp.mean(dxn * xn, axis=-1, keepdims=True))
    return dx, dy * xn


def _sigmoid(x):
    return 1.0 / (1.0 + jnp.exp(-x))


def _softplus(x):
    return jnp.maximum(x, 0.0) + jnp.log(1.0 + jnp.exp(-jnp.abs(x)))


def _acc(ref, val, first):
    @pl.when(first)
    def _():
        ref[...] = val

    @pl.when(jnp.logical_not(first))
    def _():
        ref[...] += val


def _colsum8(v):
    rows, n = v.shape
    return jnp.sum(v.reshape(rows // 8, 8, n), axis=0)


def _prep_fn(p, pprev, mu, w0, w2p, a0, a2p, g2, k_k, k_a):
    seg = _seg_ones(RW)
    ps = p + (pprev - p) * mu
    r = ps[:, 0:RW]
    k = ps[:, RW:2 * RW]
    v = ps[:, 2 * RW:3 * RW]
    xwa = ps[:, 3 * RW:3 * RW + 128]
    xg = ps[:, 3 * RW + 128:3 * RW + 256]
    wraw = -_softplus(-(w0 + _mm(jnp.tanh(xwa), w2p, "nn", 3))) - 0.5
    lw = -jnp.exp(wraw)
    a = _sigmoid(a0 + _mm(xwa, a2p, "nn", 3))
    g = _mm(_sigmoid(xg), g2, "nn", 3)
    kk = k * k_k
    kk = kk / jnp.maximum(jnp.sqrt(_segsum(kk * kk, seg)), 1e-12)
    k2 = k * (1.0 + (a - 1.0) * k_a)
    return r, lw, k2, v, kk, a, g


def _transposed(z):
    return jnp.stack([z[i].T for i in range(z.shape[0])], axis=0) if z.ndim == 3 else z.T


def _solve_unit_lower(lmat, rhs):
    c = lmat.shape[-1]
    row = lax.broadcasted_iota(jnp.int32, (c, c), 0)
    col = lax.broadcasted_iota(jnp.int32, (c, c), 1)
    eye = (row == col).astype(F32)
    ld = jnp.where(row // SUB == col // SUB, lmat, 0.0)
    lo = lmat - ld
    x = eye + ld
    m = ld
    mm = lambda p, q: _mm(p, q, "nn", WKV_PASSES)
    cat = jnp.concatenate
    m = mm(m, m)
    for _ in range(2):
        mx = mm(m, cat([m, x], axis=-1))
        m, x = mx[..., :c], x + mx[..., c:]
    x = x + mm(m, x)
    gw = mm(x, cat([lo, rhs], axis=-1))
    g, w = gw[..., :c], gw[..., c:]
    gg = mm(g, cat([g, w], axis=-1))
    w = w + gg[..., c:]
    return w + mm(gg[..., :c], w)


def _wkv_chunk_fn(s0, r, lw, k, v, kk, a):
    c = r.shape[-2]
    n = 2 * c
    row = lax.broadcasted_iota(jnp.int32, (n, n), 0)
    col = lax.broadcasted_iota(jnp.int32, (n, n), 1)
    same = (row // c) == (col // c)
    incl = jnp.logical_and(row >= col, same)
    strict = jnp.logical_and(row > col, same)
    sel = (lax.broadcasted_iota(jnp.int32, (n, 128), 0) // c) == (lax.broadcasted_iota(jnp.int32, (n, 128), 1) // HEAD_DIM)
    two = lambda z: jnp.concatenate([z, z], axis=-2)
    lw2 = two(lw)
    mm = lambda p_, q_, form: _mm(p_, q_, form, WKV_PASSES)
    cl = _mm(incl.astype(F32), lw2, "nn", "L3")
    p = jnp.exp(cl)
    pinv = jnp.exp(-cl)
    pprev = jnp.exp(cl - lw2)
    kk2 = two(kk)
    at = jnp.where(sel, -kk2 * pprev, 0.0)
    bt = jnp.where(sel, kk2 * two(a) * pinv, 0.0)
    kt = jnp.where(sel, two(k) * pinv, 0.0)
    rt = jnp.where(sel, two(r) * p, 0.0)
    vt = jnp.where(sel, two(v), 0.0)
    cat = jnp.concatenate
    bk = cat([bt, kt], axis=-2)
    arbk = mm(cat([at, rt], axis=-2), bk, "nt")
    ab, ak = jnp.where(strict, arbk[..., :n, :n], 0.0), jnp.where(strict, arbk[..., :n, n:], 0.0)
    rb, rk = jnp.where(incl, arbk[..., n:, :n], 0.0), jnp.where(incl, arbk[..., n:, n:], 0.0)
    s0t = _transposed(s0)
    u = _solve_unit_lower(ab, mm(cat([at, ak], axis=-1), cat([s0t, vt], axis=-2), "nn"))
    y2 = mm(cat([rt, rb, rk], axis=-1), cat([s0t, u, vt], axis=-2), "nn")
    plast = jnp.exp(jnp.sum(lw, axis=-2, keepdims=True))
    s1 = (s0 + mm(cat([u, vt], axis=-2), bk, "tn")) * plast
    r2 = lax.broadcasted_iota(jnp.int32, (128, 128), 0) // HEAD_DIM
    c2 = lax.broadcasted_iota(jnp.int32, (128, 128), 1) // HEAD_DIM
    return y2[..., :c, :] + y2[..., c:, :], jnp.where(r2 == c2, s1, 0.0)


def _post_fn(y, r, k2, v, g, lnw, lnb, rk):
    seg = _seg_ones(RW)
    mean = _segsum(y, seg) * (1.0 / HEAD_DIM)
    yc = y - mean
    var = _segsum(yc * yc, seg) * (1.0 / HEAD_DIM)
    yn = yc * lax.rsqrt(var + GN_EPS)
    out = yn * lnw + lnb + _segsum(r * k2 * rk, seg) * v
    return out * g


def _attn_block_fn(q, kc, vc, kp=None, vp=None):
    n = ATTN_BLOCK
    qi = lax.broadcasted_iota(jnp.int32, (n, n), 0)
    kj = lax.broadcasted_iota(jnp.int32, (n, n), 1)
    lane = lax.broadcasted_iota(jnp.int32, (1, 128), 1)
    scale = HEAD_DIM ** -0.5
    valid = kj <= qi
    keys, vals = kc, vc
    if kp is not None:
        valid = jnp.concatenate([valid, kj >= qi], axis=-1)
        keys, vals = jnp.concatenate([kc, kp], axis=-2), jnp.concatenate([vc, vp], axis=-2)
    m0 = (lane // HEAD_DIM) == 0
    q2 = jnp.concatenate([jnp.where(m0, q, 0.0), jnp.where(m0, 0.0, q)], axis=-2)
    valid2 = jnp.concatenate([valid, valid], axis=-2)
    s = jnp.where(valid2, _mm(q2, keys, "nt", ATTN_PASSES) * scale, NEG)
    m = jnp.max(s, axis=-1, keepdims=True)
    p = jnp.exp(s - m)
    den = jnp.sum(p, axis=-1, keepdims=True)
    o2 = _mm(p, vals, "nn", ATTN_PASSES) / den
    l2 = m + jnp.log(den)
    return jnp.where(m0, o2[..., :n, :], o2[..., n:, :]), jnp.where(m0, l2[..., :n, :], l2[..., n:, :])


def _attn_block_bwd(q, kc, vc, kp, vp, o, lse, do, dl):
    n = ATTN_BLOCK
    cat = jnp.concatenate
    qi = lax.broadcasted_iota(jnp.int32, (n, n), 0)
    kj = lax.broadcasted_iota(jnp.int32, (n, n), 1)
    m0 = (lax.broadcasted_iota(jnp.int32, (1, 128), 1) // HEAD_DIM) == 0
    scale = HEAD_DIM ** -0.5
    valid = kj <= qi
    keys, vals = kc, vc
    if kp is not None:
        valid = cat([valid, kj >= qi], axis=-1)
        keys, vals = cat([kc, kp], axis=-2), cat([vc, vp], axis=-2)
    stack = lambda z: cat([jnp.where(m0, z, 0.0), jnp.where(m0, 0.0, z)], axis=-2)
    q2, do2 = stack(q), stack(do)
    lse2 = cat([jnp.max(jnp.where(m0, lse, NEG), axis=-1, keepdims=True),
                jnp.max(jnp.where(m0, NEG, lse), axis=-1, keepdims=True)], axis=-2)
    delta = jnp.sum(do2 * cat([o, o], axis=-2), axis=-1, keepdims=True)
    dlse = jnp.sum(stack(dl), axis=-1, keepdims=True)
    mm = lambda a, b, form: _mm_raw(a, b, form, ATTN_PASSES)
    s = jnp.where(cat([valid, valid], axis=-2), mm(q2, keys, "nt") * scale, NEG)
    p = jnp.exp(s - lse2)
    ds = p * (mm(do2, vals, "nt") - delta + dlse)
    dq2 = mm(ds, keys, "nn") * scale
    dq = jnp.where(m0, dq2[..., :n, :], dq2[..., n:, :])
    dkeys = mm(ds, q2, "tn") * scale
    dvals = mm(p, do2, "tn")
    if kp is None:
        return dq, dkeys, dvals
    return dq, dkeys[..., :n, :], dvals[..., :n, :], dkeys[..., n:, :], dvals[..., n:, :]


def _combine_fn(o1, o2, o3, l1, l2, l3, og):
    seg = _seg_ones(o1.shape[-1])
    m = jnp.maximum(jnp.maximum(l1, l2), l3)
    e1, e2, e3 = jnp.exp(l1 - m), jnp.exp(l2 - m), jnp.exp(l3 - m)
    o = (e1 * o1 + e2 * o2 + e3 * o3) / (e1 + e2 + e3)
    o = o * lax.rsqrt(_segsum(o * o, seg) * (1.0 / HEAD_DIM) + NORM_EPS)
    return o * og


def _shifted(p, last8, first):
    prow = jnp.where(first, 0.0, last8[7:8, :])
    rolled = pltpu.roll(p, 1, axis=0)
    rid = lax.broadcasted_iota(jnp.int32, p.shape, 0)
    return jnp.where(rid == 0, prow, rolled)


_PREP_TM = 256


def _prep_specs(tm):
    vec = lambda n: pl.BlockSpec((1, n), lambda i: (0, 0))
    mat = lambda r, n: pl.BlockSpec((r, n), lambda i: (0, 0))
    return [vec(SHIFT_COLS), vec(RW), mat(128, RW), vec(RW), mat(128, RW), mat(128, RW), vec(RW), vec(RW)]


def _in_proj_prep(x, g1, win, pw):
    t = x.shape[0]
    tm = _PREP_TM

    def body(x_ref, g_ref, w_ref, mu, w0, w2p, a0, a2p, g2, k_k, k_a, h_ref, pa_ref, qkv_ref, *rest):
        outs, carry = rest[:7], rest[7]

        @pl.when(pl.program_id(0) == 0)
        def _():
            carry[...] = jnp.zeros_like(carry)

        h = _rms_fwd(x_ref[...], g_ref[...]).astype(BF16)
        h_ref[...] = h
        proj = _dot_nt(h, w_ref[...])
        p = proj[:, :SHIFT_COLS]
        pa_ref[...] = p
        for j in range(3):
            for pr in range(N_PAIR):
                c0 = SHIFT_COLS + j * RW + pr * 128
                qkv_ref[j, pr] = proj[:, c0:c0 + 128]
        pprev = _shifted(p, carry[...], pl.program_id(0) == 0)
        carry[...] = p[tm - 8:, :]
        res = _prep_fn(p, pprev, mu[...], w0[...], w2p[...], a0[...], a2p[...], g2[...], k_k[...], k_a[...])
        for o_ref, val in zip(outs, res):
            o_ref[...] = val

    row = pl.BlockSpec((tm, RW), lambda i: (i, 0))
    return pl.pallas_call(
        body, name="in_proj_prep", grid=(t // tm,),
        in_specs=[pl.BlockSpec((tm, D_MODEL), lambda i: (i, 0)), pl.BlockSpec((1, D_MODEL), lambda i: (0, 0)),
                  pl.BlockSpec((IN_COLS, D_MODEL), lambda i: (0, 0))] + _prep_specs(tm),
        out_specs=[pl.BlockSpec((tm, D_MODEL), lambda i: (i, 0)), pl.BlockSpec((tm, SHIFT_COLS), lambda i: (i, 0)),
                   pl.BlockSpec((3, N_PAIR, tm, 128), lambda i: (0, 0, i, 0))] + [row] * 7,
        out_shape=[jax.ShapeDtypeStruct((t, D_MODEL), BF16), jax.ShapeDtypeStruct((t, SHIFT_COLS), F32),
                   jax.ShapeDtypeStruct((3, N_PAIR, t, 128), F32)] + [jax.ShapeDtypeStruct((t, RW), F32)] * 7,
        scratch_shapes=[pltpu.VMEM((8, SHIFT_COLS), F32)],
        compiler_params=_params(("arbitrary",)),
    )(x, g1, win, *pw)


def _pairs(ref):
    return jnp.stack([ref[:, 128 * p:128 * (p + 1)] for p in range(N_PAIR)], axis=0)


def _wkv_fwd(r, lw, k2, v, kk, a):
    t = r.shape[0]
    nc = t // CHUNK

    def body(r_ref, lw_ref, k_ref, v_ref, kk_ref, a_ref, y_ref, s_ref, st):
        @pl.when(pl.program_id(0) == 0)
        def _():
            st[...] = jnp.zeros_like(st)

        s0 = st[...]
        s_ref[0] = s0
        y, s1 = _wkv_chunk_fn(s0, *[_pairs(ref) for ref in (r_ref, lw_ref, k_ref, v_ref, kk_ref, a_ref)])
        for p in range(N_PAIR):
            y_ref[:, 128 * p:128 * (p + 1)] = y[p]
        st[...] = s1

    blk = pl.BlockSpec((CHUNK, RW), lambda c: (c, 0))
    return pl.pallas_call(
        body, name="wkv_fwd", grid=(nc,),
        in_specs=[blk] * 6,
        out_specs=[blk, pl.BlockSpec((1, N_PAIR, 128, 128), lambda c: (c, 0, 0, 0))],
        out_shape=[jax.ShapeDtypeStruct((t, RW), F32), jax.ShapeDtypeStruct((nc, N_PAIR, 128, 128), F32)],
        scratch_shapes=[pltpu.VMEM((N_PAIR, 128, 128), F32)],
        compiler_params=_params(("arbitrary",)),
    )(r, lw, k2, v, kk, a)


_POST_TM = 512


ATTN_GROUP = 2


def _dilated_rows(d, r, n):
    if d == 1:
        return pl.ds(pl.multiple_of(n * ATTN_BLOCK, ATTN_BLOCK), ATTN_BLOCK)
    return pl.ds(r + n * (ATTN_BLOCK * d), ATTN_BLOCK, stride=d)


def _for_each_sequence(t, unit):
    for di, d in enumerate(DILATIONS):

        @pl.when(pl.program_id(1) == di)
        def _(di=di, d=d):
            nb = t // (ATTN_BLOCK * d)
            if d == 1:
                unit(di, [(d, 0, 0)], False)
                unit(di, [(d, 0, 1)], True)
                lax.fori_loop(1, nb // 2, lambda k, c: (unit(di, [(d, 0, 2 * k), (d, 0, 2 * k + 1)], True), c)[1], 0)
            else:

                def residues(r, carry):
                    unit(di, [(d, r, 0), (d, r + d // 2, 0)], False)
                    if nb > 1:
                        lax.fori_loop(1, nb, lambda n, c: (unit(di, [(d, r, n), (d, r + d // 2, n)], True), c)[1], 0)
                    return carry

                lax.fori_loop(0, d // 2, residues, 0)


def _take(ref, lead, rows_list):
    return jnp.stack([ref.at[(*lead, g)][rows, :] for rows in rows_list for g in range(ref.shape[len(lead)])], axis=0)


def _put(ref, lead, rows_list, val, add=False):
    k = 0
    for rows in rows_list:
        for g in range(ref.shape[len(lead)]):
            if add:
                ref.at[(*lead, g)][rows, :] += val[k]
            else:
                ref.at[(*lead, g)][rows, :] = val[k]
            k += 1


def _attn_fwd(qkv):
    t = qkv.shape[2]

    def body(q_ref, k_ref, v_ref, o_ref, l_ref):
        def unit(di, places, has_prev):
            cur = [_dilated_rows(d, r, n) for d, r, n in places]
            args = [_take(ref, (0,), cur) for ref in (q_ref, k_ref, v_ref)]
            if has_prev:
                prv = [_dilated_rows(d, r, n - 1) for d, r, n in places]
                args += [_take(ref, (0,), prv) for ref in (k_ref, v_ref)]
            o, lse = _attn_block_fn(*args)
            _put(o_ref, (0,), cur, o)
            _put(l_ref, (0,), cur, lse)

        _for_each_sequence(t, unit)

    spec = lambda j: pl.BlockSpec((1, ATTN_GROUP, t, 128), lambda i, b: (j, i, 0, 0))
    out = pl.BlockSpec((1, ATTN_GROUP, t, 128), lambda i, b: (b, i, 0, 0))
    return pl.pallas_call(
        body, name="attn_fwd", grid=(N_PAIR // ATTN_GROUP, len(DILATIONS)),
        in_specs=[spec(0), spec(1), spec(2)], out_specs=[out, out],
        out_shape=[jax.ShapeDtypeStruct((3, N_PAIR, t, 128), F32)] * 2,
        compiler_params=_params(("parallel", "arbitrary")),
    )(qkv, qkv, qkv)


_COMB_TM = 512


def _mixers_out(y, r, k2, v, g, lnw, lnb, rk, o, l, og):
    t = y.shape[0]
    tm = _COMB_TM

    def body(y_ref, r_ref, k_ref, v_ref, g_ref, lnw_ref, lnb_ref, rk_ref, o_ref, l_ref, og_ref, out_ref):
        out_ref[:, :RW] = _post_fn(y_ref[...], r_ref[...], k_ref[...], v_ref[...], g_ref[...],
                                   lnw_ref[...], lnb_ref[...], rk_ref[...]).astype(BF16)
        for p in range(N_PAIR):
            cols = slice(128 * p, 128 * (p + 1))
            out_ref[:, RW + 128 * p:RW + 128 * (p + 1)] = _combine_fn(
                o_ref[0, p], o_ref[1, p], o_ref[2, p], l_ref[0, p], l_ref[1, p], l_ref[2, p], og_ref[:, cols]).astype(BF16)

    row = pl.BlockSpec((tm, RW), lambda i: (i, 0))
    vec = pl.BlockSpec((1, RW), lambda i: (0, 0))
    blk = pl.BlockSpec((3, N_PAIR, tm, 128), lambda i: (0, 0, i, 0))
    return pl.pallas_call(
        body, name="mixers_out", grid=(t // tm,),
        in_specs=[row] * 5 + [vec] * 3 + [blk, blk, vec], out_specs=pl.BlockSpec((tm, D_MODEL), lambda i: (i, 0)),
        out_shape=jax.ShapeDtypeStruct((t, D_MODEL), BF16),
        compiler_params=_params(("parallel",)),
    )(y, r, k2, v, g, lnw, lnb, rk, o, l, og)


def _ffn_all(x, ycat, wg, wu, wd, wout, g2, gf, tgt):
    t = x.shape[0]
    tm = 256

    def body(x_ref, y_ref, wg_ref, wu_ref, wd_ref, wo_ref, g2_ref, gf_ref, t_ref,
             h_ref, act_ref, dx2b_ref, dgt_ref, dup_ref, dx1b_ref, dx1_ref, dya_ref, dyb_ref, loss_ref, dgf_ref, dg2_ref,
             gt_s, up_s):
        first = pl.program_id(0) == 0
        x1 = x_ref[...] + _dot(y_ref[...], wo_ref[...])
        h = _rms_fwd(x1, g2_ref[...]).astype(BF16)
        h_ref[...] = h
        for c0 in range(0, D_FF, FF_CHUNK):
            cols = slice(c0, c0 + FF_CHUNK)
            gt = _dot_nt(h, wg_ref[cols, :])
            up = _dot_nt(h, wu_ref[cols, :])
            gt_s[:, cols] = gt.astype(BF16)
            up_s[:, cols] = up.astype(BF16)
            act_ref[:, cols] = (gt * _sigmoid(gt) * up).astype(BF16)
        x2 = x1 + _dot(act_ref[...], wd_ref[...])
        gf_ = gf_ref[...]
        diff = _rms_fwd(x2, gf_) - t_ref[...]
        lrow = 0.5 * jnp.sum(_colsum8(diff * diff), axis=1, keepdims=True) * (1.0 / D_MODEL)
        _acc(loss_ref, jnp.broadcast_to(lrow, (8, 128)), first)
        dx2, dgr = _rms_bwd(diff * (1.0 / D_MODEL), x2, gf_)
        _acc(dgf_ref, _colsum8(dgr), first)
        dx2b = dx2.astype(BF16)
        dx2b_ref[...] = dx2b
        for c0 in range(0, D_FF, FF_CHUNK):
            cols = slice(c0, c0 + FF_CHUNK)
            dact = _dot_nt(dx2b, wd_ref[cols, :])
            gt = gt_s[:, cols].astype(F32)
            sg = _sigmoid(gt)
            dgt_ref[:, cols] = (dact * up_s[:, cols].astype(F32) * sg * (1.0 + gt * (1.0 - sg))).astype(BF16)
            dup_ref[:, cols] = (dact * gt * sg).astype(BF16)
        dh = _dot(dgt_ref[...], wg_ref[...]) + _dot(dup_ref[...], wu_ref[...])
        dxn, dgr2 = _rms_bwd(dh, x1, g2_ref[...])
        _acc(dg2_ref, _colsum8(dgr2), first)
        dx1 = dx2 + dxn
        dx1_ref[...] = dx1
        dx1b = dx1.astype(BF16)
        dx1b_ref[...] = dx1b
        dy = _dot_nt(dx1b, wo_ref[...])
        dya_ref[...] = dy[:, :RW]
        dyb_ref[...] = dy[:, RW:]

    row = pl.BlockSpec((tm, D_MODEL), lambda i: (i, 0))
    wide = pl.BlockSpec((tm, D_FF), lambda i: (i, 0))
    half = pl.BlockSpec((tm, RW), lambda i: (i, 0))
    wsp = pl.BlockSpec((D_FF, D_MODEL), lambda i: (0, 0))
    vec = pl.BlockSpec((1, D_MODEL), lambda i: (0, 0))
    part = pl.BlockSpec((8, D_MODEL), lambda i: (0, 0))
    bf = lambda n: jax.ShapeDtypeStruct((t, n), BF16)
    return pl.pallas_call(
        body, name="ffn_all", grid=(t // tm,),
        in_specs=[row, row, wsp, wsp, wsp, pl.BlockSpec((D_MODEL, D_MODEL), lambda i: (0, 0)), vec, vec, row],
        out_specs=[row, wide, row, wide, wide, row, row, half, half, pl.BlockSpec((8, 128), lambda i: (0, 0)), part, part],
        out_shape=[bf(D_MODEL), bf(D_FF), bf(D_MODEL), bf(D_FF), bf(D_FF), bf(D_MODEL),
                   jax.ShapeDtypeStruct((t, D_MODEL), F32), jax.ShapeDtypeStruct((t, RW), F32),
                   jax.ShapeDtypeStruct((t, RW), F32), jax.ShapeDtypeStruct((8, 128), F32),
                   jax.ShapeDtypeStruct((8, D_MODEL), F32), jax.ShapeDtypeStruct((8, D_MODEL), F32)],
        scratch_shapes=[pltpu.VMEM((tm, D_FF), BF16), pltpu.VMEM((tm, D_FF), BF16)],
        compiler_params=_params(("arbitrary",)),
    )(x, ycat, wg, wu, wd, wout, g2, gf, tgt)


def _wgrad(a, b, tk, tn, name):
    t, kdim = a.shape
    ndim = b.shape[1]

    def body(a_ref, b_ref, o_ref):
        o_ref[...] = _dot_tn(a_ref[...], b_ref[...]).astype(BF16)

    return pl.pallas_call(
        body, name=name, grid=(kdim // tk, ndim // tn),
        in_specs=[pl.BlockSpec((t, tk), lambda i, j: (0, i)), pl.BlockSpec((t, tn), lambda i, j: (0, j))],
        out_specs=pl.BlockSpec((tk, tn), lambda i, j: (i, j)),
        out_shape=jax.ShapeDtypeStruct((kdim, ndim), BF16),
        compiler_params=_params(("parallel", "parallel")),
    )(a, b)


def _post_bwd(dya, y, r, k2, v, g, lnw, lnb, rk):
    t = y.shape[0]
    tm = _POST_TM

    def body(d_ref, y_ref, r_ref, k_ref, v_ref, g_ref, lnw_ref, lnb_ref, rk_ref,
             dy_ref, dr_ref, dk_ref, dv_ref, dg_ref, dlnw_ref, dlnb_ref, drk_ref):
        first = pl.program_id(0) == 0
        ones = jnp.ones((tm, 1), F32)
        prim = (y_ref[...], r_ref[...], k_ref[...], v_ref[...], g_ref[...],
                ones * lnw_ref[...], ones * lnb_ref[...], ones * rk_ref[...])
        _, vjp = jax.vjp(_post_fn, *prim)
        dy, dr, dk, dv, dg, dlnw, dlnb, drk = vjp(d_ref[...])
        dy_ref[...] = dy
        dr_ref[...] = dr
        dk_ref[...] = dk
        dv_ref[...] = dv
        dg_ref[...] = dg
        _acc(dlnw_ref, _colsum8(dlnw), first)
        _acc(dlnb_ref, _colsum8(dlnb), first)
        _acc(drk_ref, _colsum8(drk), first)

    row = pl.BlockSpec((tm, RW), lambda i: (i, 0))
    vec = pl.BlockSpec((1, RW), lambda i: (0, 0))
    part = pl.BlockSpec((8, RW), lambda i: (0, 0))
    return pl.pallas_call(
        body, name="rwkv_post_bwd", grid=(t // tm,),
        in_specs=[row] * 6 + [vec] * 3, out_specs=[row] * 5 + [part] * 3,
        out_shape=[jax.ShapeDtypeStruct((t, RW), F32)] * 5 + [jax.ShapeDtypeStruct((8, RW), F32)] * 3,
        compiler_params=_params(("arbitrary",)),
    )(dya, y, r, k2, v, g, lnw, lnb, rk)


def _wkv_bwd(dy, s0s, r, lw, k2, v, kk, a):
    t = r.shape[0]
    nc = t // CHUNK

    def body(dy_ref, s_ref, r_ref, lw_ref, k_ref, v_ref, kk_ref, a_ref,
             dr_ref, dlw_ref, dk_ref, dv_ref, dkk_ref, da_ref, ds):
        @pl.when(pl.program_id(0) == 0)
        def _():
            ds[...] = jnp.zeros_like(ds)

        _, vjp = jax.vjp(_wkv_chunk_fn, s_ref[0],
                         *[_pairs(ref) for ref in (r_ref, lw_ref, k_ref, v_ref, kk_ref, a_ref)])
        res = vjp((_pairs(dy_ref), ds[...]))
        ds[...] = res[0]
        for ref, val in zip((dr_ref, dlw_ref, dk_ref, dv_ref, dkk_ref, da_ref), res[1:]):
            for p in range(N_PAIR):
                ref[:, 128 * p:128 * (p + 1)] = val[p]

    blk = pl.BlockSpec((CHUNK, RW), lambda c: (nc - 1 - c, 0))
    return pl.pallas_call(
        body, name="wkv_bwd", grid=(nc,),
        in_specs=[blk, pl.BlockSpec((1, N_PAIR, 128, 128), lambda c: (nc - 1 - c, 0, 0, 0))] + [blk] * 6,
        out_specs=[blk] * 6,
        out_shape=[jax.ShapeDtypeStruct((t, RW), F32)] * 6,
        scratch_shapes=[pltpu.VMEM((N_PAIR, 128, 128), F32)],
        compiler_params=_params(("arbitrary",)),
    )(dy, s0s, r, lw, k2, v, kk, a)


def _prep_in_proj_bwd(proj, pw, douts, dq, dk, dv, win, x, g1, dx1):
    t = proj.shape[0]
    tm = _PREP_TM
    nt = t // tm

    def body(p_ref, l8_ref, mu, w0, w2p, a0, a2p, g2, k_k, k_a, dr, dr2, dlw, dk2, dk22, dv, dv2, dkk, da, dg,
             dq_ref, dkq_ref, dvq_ref, w_ref, x_ref, g1_ref, dx1_ref,
             dproj_ref, dx_ref, dg1_ref, dmu_ref, dw0_ref, dw2_ref, da0_ref, da2_ref, dg2_ref, dkk_ref, dka_ref, carry):
        i = pl.program_id(0)
        first = i == 0

        @pl.when(first)
        def _():
            carry[...] = jnp.zeros_like(carry)

        p = p_ref[...]
        pprev = _shifted(p, l8_ref[...], i == nt - 1)
        ones = jnp.ones((tm, 1), F32)
        prim = (p, pprev, ones * mu[...], ones * w0[...], w2p[...], ones * a0[...], a2p[...], g2[...],
                ones * k_k[...], ones * k_a[...])
        _, vjp = jax.vjp(_prep_fn, *prim)
        dp, dpp, dmu, dw0, dw2, da0, da2, dg2, dkk_, dka = vjp(
            (dr[...] + dr2[...], dlw[...], dk2[...] + dk22[...], dv[...] + dv2[...], dkk[...], da[...], dg[...]))
        up = pltpu.roll(dpp, tm - 1, axis=0)
        rid = lax.broadcasted_iota(jnp.int32, dpp.shape, 0)
        dpa = dp + jnp.where(rid == tm - 1, carry[0:1, :], up)
        carry[...] = jnp.broadcast_to(dpp[0:1, :], carry.shape)
        _acc(dmu_ref, _colsum8(dmu), first)
        _acc(dw0_ref, _colsum8(dw0), first)
        _acc(dw2_ref, dw2, first)
        _acc(da0_ref, _colsum8(da0), first)
        _acc(da2_ref, da2, first)
        _acc(dg2_ref, dg2, first)
        _acc(dkk_ref, _colsum8(dkk_), first)
        _acc(dka_ref, _colsum8(dka), first)
        parts = [dpa] + [ref[pr] for ref in (dq_ref, dkq_ref, dvq_ref) for pr in range(N_PAIR)]
        dproj = jnp.concatenate([z.astype(BF16) for z in parts], axis=1)
        dproj_ref[...] = dproj
        dxn, dgr = _rms_bwd(_dot(dproj, w_ref[...]), x_ref[...], g1_ref[...])
        dx_ref[...] = dx1_ref[...] + dxn
        _acc(dg1_ref, _colsum8(dgr), first)

    rev = lambda i: (nt - 1 - i, 0)
    row = pl.BlockSpec((tm, RW), rev)
    wide = pl.BlockSpec((tm, D_MODEL), rev)
    pair = pl.BlockSpec((N_PAIR, tm, 128), lambda i: (0, nt - 1 - i, 0))
    part = lambda n: pl.BlockSpec((8, n), lambda i: (0, 0))
    mat = pl.BlockSpec((128, RW), lambda i: (0, 0))
    return pl.pallas_call(
        body, name="prep_in_proj_bwd", grid=(nt,),
        in_specs=[pl.BlockSpec((tm, SHIFT_COLS), rev),
                  pl.BlockSpec((8, SHIFT_COLS), lambda i: (jnp.maximum((nt - 1 - i) * (tm // 8) - 1, 0), 0))]
                 + _prep_specs(tm) + [row] * 10
                 + [pair] * 3 + [pl.BlockSpec((IN_COLS, D_MODEL), lambda i: (0, 0)), wide,
                                 pl.BlockSpec((1, D_MODEL), lambda i: (0, 0)), wide],
        out_specs=[pl.BlockSpec((tm, IN_COLS), rev), wide, part(D_MODEL), part(SHIFT_COLS), part(RW), mat, part(RW), mat,
                   mat, part(RW), part(RW)],
        out_shape=[jax.ShapeDtypeStruct((t, IN_COLS), BF16), jax.ShapeDtypeStruct((t, D_MODEL), F32),
                   jax.ShapeDtypeStruct((8, D_MODEL), F32), jax.ShapeDtypeStruct((8, SHIFT_COLS), F32),
                   jax.ShapeDtypeStruct((8, RW), F32), jax.ShapeDtypeStruct((128, RW), F32),
                   jax.ShapeDtypeStruct((8, RW), F32), jax.ShapeDtypeStruct((128, RW), F32),
                   jax.ShapeDtypeStruct((128, RW), F32), jax.ShapeDtypeStruct((8, RW), F32),
                   jax.ShapeDtypeStruct((8, RW), F32)],
        scratch_shapes=[pltpu.VMEM((8, SHIFT_COLS), F32)],
        compiler_params=_params(("arbitrary",)),
    )(proj, proj, *pw, *douts, dq, dk, dv, win, x, g1, dx1)


def _combine_bwd(dyb, o, l, og):
    t = dyb.shape[0]
    tm = _COMB_TM

    def body(d_ref, o_ref, l_ref, og_ref, do_ref, dl_ref, dog_ref):
        ones = jnp.ones((tm, 1), F32)
        dog = []
        for p in range(N_PAIR):
            cols = slice(128 * p, 128 * (p + 1))
            _, vjp = jax.vjp(_combine_fn, o_ref[0, p], o_ref[1, p], o_ref[2, p], l_ref[0, p], l_ref[1, p], l_ref[2, p],
                             ones * og_ref[:, cols])
            res = vjp(d_ref[:, cols])
            for b in range(3):
                do_ref[b, p] = res[b]
                dl_ref[b, p] = res[3 + b]
            dog.append(_colsum8(res[6]))
        _acc(dog_ref, jnp.concatenate(dog, axis=1), pl.program_id(0) == 0)

    blk = pl.BlockSpec((3, N_PAIR, tm, 128), lambda i: (0, 0, i, 0))
    return pl.pallas_call(
        body, name="attn_combine_bwd", grid=(t // tm,),
        in_specs=[pl.BlockSpec((tm, RW), lambda i: (i, 0)), blk, blk, pl.BlockSpec((1, RW), lambda i: (0, 0))],
        out_specs=[blk, blk, pl.BlockSpec((8, RW), lambda i: (0, 0))],
        out_shape=[jax.ShapeDtypeStruct((3, N_PAIR, t, 128), F32)] * 2 + [jax.ShapeDtypeStruct((8, RW), F32)],
        compiler_params=_params(("arbitrary",)),
    )(dyb, o, l, og)


def _attn_bwd(do, dl, o, lse, qkv):
    t = qkv.shape[2]

    def body(do_ref, dl_ref, o_ref, l_ref, q_ref, k_ref, v_ref, dq_ref, dk_ref, dv_ref):
        @pl.when(pl.program_id(1) == 0)
        def _():
            for ref in (dq_ref, dk_ref, dv_ref):
                ref[...] = jnp.zeros_like(ref)

        def unit(di, places, has_prev):
            cur = [_dilated_rows(d, r, n) for d, r, n in places]
            q, kc, vc = [_take(ref, (0,), cur) for ref in (q_ref, k_ref, v_ref)]
            kp = vp = None
            if has_prev:
                prv = [_dilated_rows(d, r, n - 1) for d, r, n in places]
                kp, vp = [_take(ref, (0,), prv) for ref in (k_ref, v_ref)]
            res = _attn_block_bwd(q, kc, vc, kp, vp, *[_take(ref, (0,), cur) for ref in (o_ref, l_ref, do_ref, dl_ref)])
            _put(dq_ref, (), cur, res[0], add=True)
            _put(dk_ref, (), cur, res[1], add=True)
            _put(dv_ref, (), cur, res[2], add=True)
            if has_prev:
                _put(dk_ref, (), prv, res[3], add=True)
                _put(dv_ref, (), prv, res[4], add=True)

        _for_each_sequence(t, unit)

    spec = lambda j: pl.BlockSpec((1, ATTN_GROUP, t, 128), lambda i, b: (j, i, 0, 0))
    branch = pl.BlockSpec((1, ATTN_GROUP, t, 128), lambda i, b: (b, i, 0, 0))
    out = pl.BlockSpec((ATTN_GROUP, t, 128), lambda i, b: (i, 0, 0))
    return pl.pallas_call(
        body, name="attn_bwd", grid=(N_PAIR // ATTN_GROUP, len(DILATIONS)),
        in_specs=[branch] * 4 + [spec(0), spec(1), spec(2)], out_specs=[out] * 3,
        out_shape=[jax.ShapeDtypeStruct((N_PAIR, t, 128), F32)] * 3,
        compiler_params=_params(("parallel", "arbitrary")),
    )(do, dl, o, lse, qkv, qkv, qkv)


def _pad_lora(w, lo):
    z = jnp.zeros((64, RW), F32)
    return jnp.concatenate([w, z], axis=0) if lo == 0 else jnp.concatenate([z, w], axis=0)


def _local_step(x, tgt, win, vecs, w2, a2, g2m, get_rest, send_rest):
    pw = (vecs["mu_shift"], vecs["decay_w0"], _pad_lora(w2, 0), vecs["iclr_a0"], _pad_lora(a2, 64), g2m,
          vecs["k_k"], vecs["k_a"])
    h, proj, qkv, r, lw, k2, v, kk, a, g = _in_proj_prep(x, vecs["mix_norm_g"], win, pw)
    y, s0s = _wkv_fwd(r, lw, k2, v, kk, a)
    o_att, l_att = _attn_fwd(qkv)
    ycat = _mixers_out(y, r, k2, v, g, vecs["ln_x_w"], vecs["ln_x_b"], vecs["r_k"], o_att, l_att, vecs["attn_out_g"])
    wout, wg, wu, wd = get_rest(ycat)
    h2, act, dx2b, dgt, dup, dx1b, dx1, dya, dyb, loss8, dgf, dg2n = _ffn_all(
        x, ycat, wg, wu, wd, wout, vecs["ffn_norm_g"], vecs["final_norm_g"], tgt)
    gw = {
        "w_down": _wgrad(act, dx2b, 1408, 1024, "wgrad_down"),
        "w_gate": _wgrad(dgt, h2, 1408, 1024, "wgrad_gate"),
        "w_up": _wgrad(dup, h2, 1408, 1024, "wgrad_up"),
        "w_out": _wgrad(ycat, dx1b, 1024, 1024, "wgrad_out"),
    }

    lnw = vecs["ln_x_w"] + send_rest(gw)[0, 0]
    dy, dr_p, dk2_p, dv_p, dg, dlnw, dlnb, drk = _post_bwd(dya, y, r, k2, v, g, lnw, vecs["ln_x_b"], vecs["r_k"])
    dr_s, dlw, dk2_s, dv_s, dkk, da = _wkv_bwd(dy, s0s, r, lw, k2, v, kk, a)
    do_att, dl_att, dog = _combine_bwd(dyb, o_att, l_att, vecs["attn_out_g"])
    dq, dk, dv = _attn_bwd(do_att, dl_att, o_att, l_att, qkv)
    dproj, dx, dg1, dmu, dw0, dw2p, da0, da2p, dg2m, dk_k, dk_a = _prep_in_proj_bwd(
        proj, pw, (dr_p, dr_s, dlw, dk2_p, dk2_s, dv_p, dv_s, dkk, da, dg), dq, dk, dv, win, x, vecs["mix_norm_g"], dx1)
    gw["w_in"] = _wgrad(dproj, h, 1664, 1024, "wgrad_in")
    gw["decay_w2"] = dw2p[:64]
    gw["iclr_a2"] = da2p[64:]
    gw["gate_g2"] = dg2m
    gv = {"mix_norm_g": dg1, "mu_shift": dmu, "decay_w0": dw0, "iclr_a0": da0, "k_k": dk_k, "k_a": dk_a, "r_k": drk,
          "ln_x_w": dlnw, "ln_x_b": dlnb, "attn_out_g": dog, "ffn_norm_g": dg2n, "final_norm_g": dgf}
    return loss8, dx, gw, gv


N_CHIP = 4
N_DEV = 8
MATS = ("w_in", "w_out", "w_gate", "w_up", "w_down")
LORAS = ("decay_w2", "iclr_a2", "gate_g2")
VECS = (("mix_norm_g", 1024), ("mu_shift", 1792), ("decay_w0", 512), ("iclr_a0", 512), ("k_k", 512), ("k_a", 512),
        ("r_k", 512), ("ln_x_w", 512), ("ln_x_b", 512), ("attn_out_g", 512), ("ffn_norm_g", 1024),
        ("final_norm_g", 1024))
N_VEC = sum(n for _, n in VECS)
N_SMALL = N_VEC + 128
ANY = pl.BlockSpec(memory_space=pl.ANY)


def _flip(v, f):
    return 1 - v if f else v


class _Me:
    def __init__(self, mode):
        x, y, c = lax.axis_index("x"), lax.axis_index("y"), lax.axis_index("c")
        self.core, self.chip, self.dev = c, 2 * x + y, 4 * x + 2 * y + c
        self.sibling = (x, y, 1 - c)
        if mode == "chips":
            self.peers = [(px, py, c) for px, py in ((1 - x, y), (x, 1 - y), (1 - x, 1 - y))]
        else:
            self.peers = [(_flip(x, k & 4), _flip(y, k & 2), _flip(c, k & 1)) for k in range(1, N_DEV)]


def _half(core, rows):
    h = rows // 2
    return pl.ds(pl.multiple_of(core * h, h), h)


_BY_CHIP = ("gather", "chipsum")


def _peer_copy(srcs, dsts, kinds, send_sems, recv_sems, me, j, i, incoming):
    px, py, pc = me.peers[j]
    pchip, pdev = 2 * px + py, 4 * px + 2 * py + pc
    src, dst, kind = srcs[i], dsts[i], kinds[i]
    if kind == "gather":
        rows = _half(me.core, src.shape[1])
        src, dst = src.at[me.chip, rows], dst.at[pchip if incoming else me.chip, rows]
    elif kind == "scatter":
        src, dst = src.at[pchip, _half(pc, src.shape[1])], dst.at[pdev if incoming else me.dev]
    elif kind == "chipsum":
        src, dst = src.at[pchip], dst.at[pchip if incoming else me.chip]
    else:
        dst = dst.at[pdev if incoming else me.dev]
    n = len(srcs)
    return pltpu.make_async_remote_copy(src_ref=src, dst_ref=dst, send_sem=send_sems.at[n * j + i],
                                        recv_sem=recv_sems.at[n * j + i], device_id=(px, py, pc), device_id_type=MESH)


def _mode(kinds):
    return "chips" if kinds[0] in _BY_CHIP else "devs"


def _npeer(kinds):
    return N_CHIP - 1 if kinds[0] in _BY_CHIP else N_DEV - 1


def _sibling_halves(gs, name):
    n = len(gs)

    def body(*refs):
        srcs, dsts, send_sems, recv_sems = refs[:n], refs[n:2 * n], refs[2 * n], refs[2 * n + 1]
        me = _Me("chips")

        def copy(i, p):
            return pltpu.make_async_remote_copy(
                src_ref=srcs[i].at[p, _half(1 - me.core, srcs[i].shape[1])], dst_ref=dsts[i].at[p],
                send_sem=send_sems.at[N_CHIP * i + p], recv_sem=recv_sems.at[N_CHIP * i + p],
                device_id=me.sibling, device_id_type=MESH)

        copies = [copy(i, p) for i in range(n) for p in range(N_CHIP)]
        for cp in copies:
            cp.start()
        for cp in copies:
            cp.wait()

    return pl.pallas_call(
        body, name=name, in_specs=[ANY] * n, out_specs=[ANY] * n,
        out_shape=[jax.ShapeDtypeStruct((N_CHIP, g.shape[1] // 2, g.shape[2]), g.dtype) for g in gs],
        scratch_shapes=[pltpu.SemaphoreType.DMA((N_CHIP * n,)), pltpu.SemaphoreType.DMA((N_CHIP * n,))],
    )(*gs)


def _add_halves(g, other, core, tr, name):
    _, h, cols = other.shape

    def body(core_ref, g_ref, o_ref, out_ref):
        out_ref[...] = (g_ref[...].astype(F32) + o_ref[...].astype(F32)).astype(BF16)

    blk = lambda off: pl.BlockSpec((1, tr, cols), lambda p, i, core_ref: (p, core_ref[0] * (h // tr) * off + i, 0))
    return pl.pallas_call(
        body, name=name,
        grid_spec=pltpu.PrefetchScalarGridSpec(num_scalar_prefetch=1, grid=(N_CHIP, h // tr),
                                               in_specs=[blk(1), blk(0)], out_specs=blk(0)),
        out_shape=jax.ShapeDtypeStruct(other.shape, BF16),
        compiler_params=_params(("parallel", "parallel")),
    )(core, g, other)


def _swap_gathered(lands, name):
    n = len(lands)

    def body(*refs):
        dsts, send_sems, recv_sems = refs[n:2 * n], refs[2 * n], refs[2 * n + 1]
        me = _Me("chips")

        def copy(j, i, incoming):
            px, py, _ = me.peers[j]
            rows_out, rows_in = _half(me.core, dsts[i].shape[1]), _half(1 - me.core, dsts[i].shape[1])
            return pltpu.make_async_remote_copy(
                src_ref=dsts[i].at[2 * px + py, rows_out], dst_ref=dsts[i].at[2 * px + py, rows_in if incoming else rows_out],
                send_sem=send_sems.at[n * j + i], recv_sem=recv_sems.at[n * j + i], device_id=me.sibling, device_id_type=MESH)

        sends = [copy(j, i, False) for j in range(3) for i in range(n)]
        for cp in sends:
            cp.start()
        for j in range(3):
            for i in range(n):
                copy(j, i, True).wait_recv()
        for cp in sends:
            cp.wait_send()

    return pl.pallas_call(
        body, name=name, in_specs=[ANY] * n, out_specs=[ANY] * n,
        out_shape=[jax.ShapeDtypeStruct(l.shape, l.dtype) for l in lands],
        input_output_aliases={i: i for i in range(n)},
        scratch_shapes=[pltpu.SemaphoreType.DMA((3 * n,)), pltpu.SemaphoreType.DMA((3 * n,))],
    )(*lands)


def _join_halves(sums, name):
    n = len(sums)

    def body(*refs):
        dsts, send_sems, recv_sems = refs[n:2 * n], refs[2 * n], refs[2 * n + 1]
        me = _Me("chips")

        def copy(i, incoming):
            mine, other = _half(me.core, dsts[i].shape[0]), _half(1 - me.core, dsts[i].shape[0])
            return pltpu.make_async_remote_copy(src_ref=dsts[i].at[mine], dst_ref=dsts[i].at[other if incoming else mine],
                                                send_sem=send_sems.at[i], recv_sem=recv_sems.at[i],
                                                device_id=me.sibling, device_id_type=MESH)

        sends = [copy(i, False) for i in range(n)]
        for cp in sends:
            cp.start()
        for i in range(n):
            copy(i, True).wait_recv()
        for cp in sends:
            cp.wait_send()

    return pl.pallas_call(
        body, name=name, in_specs=[ANY] * n, out_specs=[ANY] * n,
        out_shape=[jax.ShapeDtypeStruct(s.shape, s.dtype) for s in sums],
        input_output_aliases={i: i for i in range(n)},
        scratch_shapes=[pltpu.SemaphoreType.DMA((n,)), pltpu.SemaphoreType.DMA((n,))],
    )(*sums)


HBM = pl.BlockSpec(memory_space=pltpu.HBM)
SEM = pl.BlockSpec(memory_space=pltpu.SEMAPHORE)
EFFECT = pltpu.SideEffectType.DATAFLOW_SIDE_EFFECTING


def _swap_start(arrs, lands, kinds, name):
    n = len(lands)
    ops = list(lands) if arrs is None else [*arrs, *lands]
    k = len(ops)

    def body(*refs):
        srcs, dsts, send_sems, recv_sems, token = refs[:n], refs[k - n:k], refs[k], refs[k + 1], refs[-1]
        me = _Me(_mode(kinds))
        for j in range(len(me.peers)):
            for i in range(n):
                _peer_copy(srcs, dsts, kinds, send_sems, recv_sems, me, j, i, False).start()
        token[...] = jnp.zeros_like(token)

    ns = _npeer(kinds) * n
    outs = pl.pallas_call(
        body, name=name,
        out_shape=(pltpu.SemaphoreType.DMA((ns,)), pltpu.SemaphoreType.DMA((ns,)),
                   *[pltpu.HBM(a.shape, a.dtype) for a in ops], jax.ShapeDtypeStruct((8, 128), F32)),
        in_specs=[HBM] * k, out_specs=(SEM, SEM, *[HBM] * k, pl.BlockSpec(memory_space=pltpu.VMEM)),
        input_output_aliases={i: 2 + i for i in range(k)},
        compiler_params=pltpu.CompilerParams(has_side_effects=EFFECT),
    )(*[pltpu.with_memory_space_constraint(a, pltpu.HBM) for a in ops])
    return outs[0], outs[1], outs[2:2 + k - n], outs[2 + k - n:2 + k], outs[-1]


def _swap_wait(send_sems, recv_sems, srcs_thru, lands_thru, after, kinds, name):
    n = len(lands_thru)
    ops = [*srcs_thru, *lands_thru]
    k = len(ops)

    def body(*refs):
        srcs, dsts, s_sems, r_sems = refs[:n], refs[k - n:k], refs[k], refs[k + 1]
        me = _Me(_mode(kinds))
        for j in range(len(me.peers)):
            for i in range(n):
                cp = _peer_copy(srcs, dsts, kinds, s_sems, r_sems, me, j, i, True)
                cp.wait_send()
                cp.wait_recv()

    outs = pl.pallas_call(
        body, name=name,
        out_shape=tuple(pltpu.HBM(a.shape, a.dtype) for a in ops),
        in_specs=[HBM] * k + [SEM, SEM, ANY], out_specs=tuple([HBM] * k),
        input_output_aliases={i: i for i in range(k)},
        compiler_params=pltpu.CompilerParams(has_side_effects=EFFECT),
    )(*ops, send_sems, recv_sems, after)
    return outs[k - n:]


def _adamw(w, g, m, v):
    m = ADAM_B1 * m + (1.0 - ADAM_B1) * g
    v = ADAM_B2 * v + (1.0 - ADAM_B2) * (g * g)
    m_hat = m / (1.0 - ADAM_B1 ** ADAM_STEP)
    v_hat = v / (1.0 - ADAM_B2 ** ADAM_STEP)
    delta = -ADAM_LR * (m_hat / (jnp.sqrt(v_hat) + ADAM_EPS) + ADAM_WD * w)
    return delta, m, v


def _reduce8(rbuf, core, tr, name):
    slots, h, cols = rbuf.shape

    def body(core_ref, r_ref, g_ref):
        g = r_ref[0].astype(F32)
        for s in range(1, slots):
            g = g + r_ref[s].astype(F32)
        g_ref[...] = g

    return pl.pallas_call(
        body, name=name,
        grid_spec=pltpu.PrefetchScalarGridSpec(
            num_scalar_prefetch=1, grid=(h // tr,),
            in_specs=[pl.BlockSpec((slots, tr, cols), lambda i, core_ref: (0, i, 0))],
            out_specs=pl.BlockSpec((tr, cols), lambda i, core_ref: (core_ref[0] * (h // tr) + i, 0))),
        out_shape=jax.ShapeDtypeStruct((2 * h, cols), F32),
        compiler_params=_params(("parallel",)),
    )(core, rbuf)


def _adamw_call(g, w, m, v, tr, name):
    _, rows, cols = w.shape

    def body(g_in, w_ref, m_ref, v_ref, g_ref, d_ref, nm_ref, nv_ref):
        g = g_in[...]
        g_ref[0] = g
        d_ref[0], nm_ref[0], nv_ref[0] = _adamw(w_ref[0], g, m_ref[0], v_ref[0])

    row = pl.BlockSpec((1, tr, cols), lambda i: (0, i, 0))
    return pl.pallas_call(
        body, name=name, grid=(rows // tr,),
        in_specs=[pl.BlockSpec((tr, cols), lambda i: (i, 0)), row, row, row], out_specs=[row] * 4,
        out_shape=[jax.ShapeDtypeStruct(w.shape, F32)] * 4,
        compiler_params=_params(("parallel",)),
    )(g, w, m, v)


def _rowsum_small(parts, loss8):
    def body(*refs):
        out = refs[-1]
        c0 = 0
        for ref in refs[:-1]:
            n = ref.shape[1]
            out[:, c0:c0 + n] = jnp.sum(ref[...], axis=0, keepdims=True)
            c0 += n

    return pl.pallas_call(body, name="rowsum_small", out_shape=jax.ShapeDtypeStruct((1, N_SMALL), F32))(*parts, loss8)


def _reduce_adamw_small(sbuf, ws, ms, vs):
    nv = len(ws)

    def body(*refs):
        s_ref, ins, outs = refs[0], refs[1:1 + 3 * nv], refs[1 + 3 * nv:]
        tot = s_ref[0]
        for s in range(1, N_DEV):
            tot = tot + s_ref[s]
        c0 = 0
        for i in range(nv):
            n = ins[i].shape[1]
            g = tot[:, c0:c0 + n]
            outs[i][...] = g
            outs[nv + i][...], outs[2 * nv + i][...], outs[3 * nv + i][...] = _adamw(
                ins[i][...], g, ins[nv + i][...], ins[2 * nv + i][...])
            c0 += n
        outs[-1][...] = tot[:, c0:]

    return pl.pallas_call(
        body, name="reduce_adamw_small",
        out_shape=[jax.ShapeDtypeStruct(a.shape, F32) for a in ws] * 4 + [jax.ShapeDtypeStruct((1, 128), F32)],
    )(sbuf, *ws, *ms, *vs)


_TRANSPOSED = ("w_in", "w_gate", "w_up")
_ROW_STACKED = MATS
_ADAM_TILE = {"w_in": 208, "w_out": 256, "w_gate": 176, "w_up": 176, "w_down": 176, "lora": 256}
_SUM_TILE = {"w_in": 208, "w_out": 128, "w_gate": 176, "w_up": 176, "w_down": 176, "lora": 128}


def _full(n, stacked):
    p, r, c = stacked.shape
    if n in _ROW_STACKED:
        return stacked.reshape(p * r, c)
    return jnp.transpose(stacked, (1, 0, 2)).reshape(r, p * c)


def _by_chip(n, full):
    if n in _ROW_STACKED:
        return full.reshape(N_CHIP, full.shape[0] // N_CHIP, full.shape[1])
    r, c = full.shape
    return jnp.transpose(full.reshape(r, N_CHIP, c // N_CHIP), (1, 0, 2))


def _with_own(land_shape, dtype, own, slot):
    return lax.dynamic_update_slice(lax.empty(land_shape, dtype), own[None], (slot,) + (0,) * own.ndim)


def _cast_into_slot(a, chip, tr, name, after=None):
    rows, cols = a.shape

    def body(chip_ref, a_ref, *rest):
        rest[-1][0] = a_ref[...].astype(BF16)

    extra = [] if after is None else [after]
    return pl.pallas_call(
        body, name=name,
        grid_spec=pltpu.PrefetchScalarGridSpec(
            num_scalar_prefetch=1, grid=(rows // tr,),
            in_specs=[pl.BlockSpec((tr, cols), lambda i, chip_ref: (i, 0))] + [ANY] * len(extra),
            out_specs=pl.BlockSpec((1, tr, cols), lambda i, chip_ref: (chip_ref[0], i, 0))),
        out_shape=jax.ShapeDtypeStruct((N_CHIP, rows, cols), BF16),
        compiler_params=_params(("parallel",)),
    )(chip, a, *extra)


def kernel(x, mix_norm_g, w_in, mu_shift, decay_w0, decay_w2, iclr_a0, iclr_a2, gate_g2, k_k, k_a, r_k, ln_x_w, ln_x_b, attn_out_g, w_out, ffn_norm_g, w_gate, w_up, w_down, final_norm_g, loss_target, m_mix_norm_g, m_w_in, m_mu_shift, m_decay_w0, m_decay_w2, m_iclr_a0, m_iclr_a2, m_gate_g2, m_k_k, m_k_a, m_r_k, m_ln_x_w, m_ln_x_b, m_attn_out_g, m_w_out, m_ffn_norm_g, m_w_gate, m_w_up, m_w_down, m_final_norm_g, v_mix_norm_g, v_w_in, v_mu_shift, v_decay_w0, v_decay_w2, v_iclr_a0, v_iclr_a2, v_gate_g2, v_k_k, v_k_a, v_r_k, v_ln_x_w, v_ln_x_b, v_attn_out_g, v_w_out, v_ffn_norm_g, v_w_gate, v_w_up, v_w_down, v_final_norm_g):
    names = ("mix_norm_g", "w_in", "mu_shift", "decay_w0", "decay_w2", "iclr_a0", "iclr_a2", "gate_g2", "k_k", "k_a",
             "r_k", "ln_x_w", "ln_x_b", "attn_out_g", "w_out", "ffn_norm_g", "w_gate", "w_up", "w_down", "final_norm_g")
    w = dict(zip(names, (mix_norm_g, w_in, mu_shift, decay_w0, decay_w2, iclr_a0, iclr_a2, gate_g2, k_k, k_a, r_k,
                         ln_x_w, ln_x_b, attn_out_g, w_out, ffn_norm_g, w_gate, w_up, w_down, final_norm_g)))
    m = dict(zip(names, (m_mix_norm_g, m_w_in, m_mu_shift, m_decay_w0, m_decay_w2, m_iclr_a0, m_iclr_a2, m_gate_g2,
                         m_k_k, m_k_a, m_r_k, m_ln_x_w, m_ln_x_b, m_attn_out_g, m_w_out, m_ffn_norm_g, m_w_gate,
                         m_w_up, m_w_down, m_final_norm_g)))
    v = dict(zip(names, (v_mix_norm_g, v_w_in, v_mu_shift, v_decay_w0, v_decay_w2, v_iclr_a0, v_iclr_a2, v_gate_g2,
                         v_k_k, v_k_a, v_r_k, v_ln_x_w, v_ln_x_b, v_attn_out_g, v_w_out, v_ffn_norm_g, v_w_gate,
                         v_w_up, v_w_down, v_final_norm_g)))
    first = ("w_in", "lora")
    rest = ("w_out", "w_gate", "w_up", "w_down")
    xi, yi, ci = lax.axis_index("x"), lax.axis_index("y"), lax.axis_index("c")
    my_chip, my_dev = 2 * xi + yi, 4 * xi + 2 * yi + ci
    gather, scatter = ("gather",) * 4, ("scatter",) * 4

    def stored(d):
        out = {n: jnp.transpose(d[n][0]) if n in _TRANSPOSED else d[n][0] for n in MATS}
        out["lora"] = jnp.concatenate([d[n][0] for n in LORAS], axis=0)
        return out

    ws, ms, vs = stored(w), stored(m), stored(v)
    lora_rows = [(0, 64), (64, 128), (128, 256)]
    chip = jnp.reshape(my_chip, (1,)).astype(jnp.int32)
    early = _swap_start(None, [_cast_into_slot(ws["w_in"], chip, _ADAM_TILE["w_in"], "cast_w_in"),
                               _with_own((N_CHIP,) + ws["lora"].shape, F32, ws["lora"], my_chip)], gather[:2],
                        "gather_first_start")
    lands = [_cast_into_slot(ws[n], chip, _ADAM_TILE[n], "cast_" + n, after=early[4]) for n in rest]
    ssem, rsem, srcs_thru, lands_thru, tok = _swap_start(None, lands, gather, "gather_rest_start")
    got = _swap_wait(early[0], early[1], early[2], early[3], tok, gather[:2], "gather_first_wait")
    win_all, lora_all = _swap_gathered(got, "gather_first_halves")
    win = _full("w_in", win_all)
    w2, a2, g2m = (_full(n, lora_all[:, a:b]) for n, (a, b) in zip(LORAS, lora_rows))

    vecs = {n: w[n].reshape(1, sz) for n, sz in VECS}
    vecs["mix_norm_g"] = vecs["mix_norm_g"] + tok[0, 0]

    def get_rest(after):
        halves = _swap_wait(ssem, rsem, srcs_thru, lands_thru, after, gather, "gather_rest_wait")
        return [_full(n, z) for n, z in zip(rest, _swap_gathered(halves, "gather_rest_halves"))]

    flight = []

    def my_half(g):
        h = g.shape[1] // 2
        return lax.dynamic_slice(g, (my_chip, ci * h, 0), (1, h, g.shape[2]))[0]

    def send_rest(gw):
        gs = [_by_chip(n, gw[n]) for n in rest]
        into = [_with_own((N_DEV,) + my_half(g).shape, BF16, my_half(g), my_dev) for g in gs]
        flight.extend(_swap_start(gs, into, scatter, "exchange_rest_start"))
        return flight[4]

    loss8, dx, gw, gv = _local_step(x[0], loss_target[0], win, vecs, w2, a2, g2m, get_rest, send_rest)

    core = jnp.reshape(ci, (1,)).astype(jnp.int32)
    gs = [_by_chip("w_in", gw["w_in"]),
          jnp.concatenate([_by_chip(n, gw[n]) for n in LORAS], axis=1).astype(BF16)]
    theirs = _sibling_halves(gs, "presum_halves")
    sums = [_add_halves(g, o, core, _SUM_TILE[n], "chipsum_" + n) for n, g, o in zip(first, gs, theirs)]
    own = [lax.dynamic_index_in_dim(s, my_chip, 0, keepdims=False) for s in sums]
    last = _swap_start(sums, [_with_own(s.shape, BF16, o, my_chip) for s, o in zip(sums, own)], ("chipsum",) * 2,
                       "exchange_first_start")
    small = _rowsum_small([gv[n] for n, _ in VECS], loss8 + last[4])
    vecs_out = _swap_start([small], [_with_own((N_DEV,) + small.shape, F32, small, my_dev)], ("all",),
                           "exchange_vectors_start")


    def update(group, rbufs, tag):
        sums = [_reduce8(rb, core, _SUM_TILE[n], "reduce_" + n) for n, rb in zip(group, rbufs)]
        gsum = _join_halves(sums, "join_halves_" + tag)
        out = {}
        for n, g in zip(group, gsum):
            r = _adamw_call(g, ws[n][None], ms[n][None], vs[n][None], _ADAM_TILE[n], "adamw_" + n)
            if n == "lora":
                for name, (a, b) in zip(LORAS, lora_rows):
                    out[name] = [z[:, a:b] for z in r]
            else:
                out[n] = [jnp.transpose(z[0])[None] for z in r] if n in _TRANSPOSED else r
        return out, r[1]

    res, done = update(rest, _swap_wait(flight[0], flight[1], flight[2], flight[3], vecs_out[4], scatter,
                                        "exchange_rest_wait"), "rest")
    got = _swap_wait(last[0], last[1], last[2], last[3], done, ("chipsum",) * 2, "exchange_first_wait")
    res_first, done = update(first, got, "first")
    res.update(res_first)
    sbuf = _swap_wait(vecs_out[0], vecs_out[1], vecs_out[2], vecs_out[3], done, ("all",), "exchange_vectors_wait")[0]
    rows = lambda d: [d[n].reshape(1, sz) for n, sz in VECS]
    small_res = _reduce_adamw_small(sbuf, rows(w), rows(m), rows(v))

    outs = []
    for k in range(4):
        piece = {n: r[k] for n, r in res.items()}
        for i, (n, _) in enumerate(VECS):
            piece[n] = small_res[k * len(VECS) + i].reshape(w[n].shape)
        outs.extend(piece[n] for n in names)
    return (small_res[-1][0, 0], dx[None], *outs)
```

```python
import jax
import jax.numpy as jnp
from jax import lax
from jax.experimental import pallas as pl
from jax.experimental.pallas import tpu as pltpu

F32 = jnp.float32
BF16 = jnp.bfloat16

D_MODEL = 1024
HEAD_DIM = 64
RW = 512
N_PAIR = RW // 128
SHIFT_COLS = 1792
IN_COLS = 3328
D_FF = 2816
FF_CHUNK = 256
NORM_EPS = 1e-6
GN_EPS = 64e-5
CHUNK = 64
SUB = 16
WKV_PASSES = 1
ATTN_PASSES = 1
ATTN_BLOCK = 128
DILATIONS = (1, 4, 16)
NEG = -1e30
ADAM_LR, ADAM_B1, ADAM_B2, ADAM_EPS, ADAM_WD, ADAM_STEP = 0.001, 0.9, 0.999, 1e-08, 0.01, 10
VMEM_LIMIT = 56 * 1024 * 1024
MESH = pl.DeviceIdType.MESH


def _params(sem=None, **kw):
    return pltpu.CompilerParams(dimension_semantics=sem, vmem_limit_bytes=VMEM_LIMIT, **kw)


def _dot(a, b):
    return lax.dot_general(a, b, (((1,), (0,)), ((), ())), preferred_element_type=F32)


def _dot_nt(a, b):
    return lax.dot_general(a, b, (((1,), (1,)), ((), ())), preferred_element_type=F32)


def _dot_tn(a, b):
    return lax.dot_general(a, b, (((0,), (0,)), ((), ())), preferred_element_type=F32)


_FORMS = {"nn": ((1,), (0,)), "nt": ((1,), (1,)), "tn": ((0,), (0,))}


def _dg(a, b, form):
    if a.ndim == 3 or b.ndim == 3:
        nb = a.shape[0] if a.ndim == 3 else b.shape[0]
        return jnp.stack([_dg(a[i] if a.ndim == 3 else a, b[i] if b.ndim == 3 else b, form) for i in range(nb)], axis=0)
    return lax.dot_general(a, b, (_FORMS[form], ((), ())), preferred_element_type=F32)


def _split2(x):
    hi = x.astype(BF16)
    return hi, (x - hi.astype(F32)).astype(BF16)


def _split3(x):
    hi = x.astype(BF16)
    rest = x - hi.astype(F32)
    mid = rest.astype(BF16)
    return hi, mid, (rest - mid.astype(F32)).astype(BF16)


def _mm_raw(a, b, form, mode):
    if mode == 1:
        return _dg(a.astype(BF16), b.astype(BF16), form)
    if mode == 3:
        ah, al = _split2(a)
        bh, bl = _split2(b)
        return _dg(ah, bh, form) + (_dg(ah, bl, form) + _dg(al, bh, form))
    if mode == "L3":
        ab = a.astype(BF16)
        b1, b2, b3 = _split3(b)
        if form == "nn":
            n = b.shape[-1]
            wide = _dg(ab, jnp.concatenate([b1, b2, b3], axis=-1), form)
            return wide[..., :n] + (wide[..., n:2 * n] + wide[..., 2 * n:])
        return _dg(ab, b1, form) + (_dg(ab, b2, form) + _dg(ab, b3, form))
    assert mode == "R3", mode
    bb = b.astype(BF16)
    a1, a2, a3 = _split3(a)
    if form in ("nn", "nt"):
        m = a.shape[-2]
        tall = _dg(jnp.concatenate([a1, a2, a3], axis=-2), bb, form)
        return tall[..., :m, :] + (tall[..., m:2 * m, :] + tall[..., 2 * m:, :])
    return _dg(a1, bb, form) + (_dg(a2, bb, form) + _dg(a3, bb, form))


def _mm(a, b, form, mode):
    @jax.custom_vjp
    def f(a, b):
        return _mm_raw(a, b, form, mode)

    def fwd(a, b):
        return _mm_raw(a, b, form, mode), (a, b)

    def bwd(res, ct):
        a, b = res
        la = {1: 1, 3: 3, "L3": None, "R3": "R3"}[mode]
        lb = {1: 1, 3: 3, "L3": "L3", "R3": None}[mode]
        if form == "nn":
            da = None if la is None else _mm_raw(ct, b, "nt", la)
            db = None if lb is None else _mm_raw(a, ct, "tn", lb)
        elif form == "nt":
            da = None if la is None else _mm_raw(ct, b, "nn", la)
            db = None if lb is None else _mm_raw(ct, a, "tn", "R3" if lb == "L3" else lb)
        else:
            da = None if la is None else _mm_raw(b, ct, "nt", "L3" if la == "R3" else la)
            db = None if lb is None else _mm_raw(a, ct, "nn", lb)
        return (jnp.zeros_like(a) if da is None else da, jnp.zeros_like(b) if db is None else db)

    f.defvjp(fwd, bwd)
    return f(a, b)


def _seg_ones(n):
    r = lax.broadcasted_iota(jnp.int32, (n, n), 0) // HEAD_DIM
    c = lax.broadcasted_iota(jnp.int32, (n, n), 1) // HEAD_DIM
    return (r == c).astype(F32)


def _segsum(x, seg):
    return _mm(x, seg, "nn", "R3")


def _rms_fwd(x, g):
    rstd = lax.rsqrt(jnp.mean(x * x, axis=-1, keepdims=True) + NORM_EPS)
    return x * rstd * g


def _rms_bwd(dy, x, g):
    rstd = lax.rsqrt(jnp.mean(x * x, axis=-1, keepdims=True) + NORM_EPS)
    xn = x * rstd
    dxn = dy * g
    dx = rstd * (dxn - xn * jnp.mean(dxn * xn, axis=-1, keepdims=True))
    return dx, dy * xn


def _sigmoid(x):
    return 1.0 / (1.0 + jnp.exp(-x))


def _softplus(x):
    return jnp.maximum(x, 0.0) + jnp.log(1.0 + jnp.exp(-jnp.abs(x)))


def _acc(ref, val, first):
    @pl.when(first)
    def _():
        ref[...] = val

    @pl.when(jnp.logical_not(first))
    def _():
        ref[...] += val


def _colsum8(v):
    rows, n = v.shape
    return jnp.sum(v.reshape(rows // 8, 8, n), axis=0)


def _prep_fn(p, pprev, mu, w0, w2p, a0, a2p, g2, k_k, k_a):
    seg = _seg_ones(RW)
    ps = p + (pprev - p) * mu
    r = ps[:, 0:RW]
    k = ps[:, RW:2 * RW]
    v = ps[:, 2 * RW:3 * RW]
    xwa = ps[:, 3 * RW:3 * RW + 128]
    xg = ps[:, 3 * RW + 128:3 * RW + 256]
    wraw = -_softplus(-(w0 + _mm(jnp.tanh(xwa), w2p, "nn", 3))) - 0.5
    lw = -jnp.exp(wraw)
    a = _sigmoid(a0 + _mm(xwa, a2p, "nn", 3))
    g = _mm(_sigmoid(xg), g2, "nn", 3)
    kk = k * k_k
    kk = kk / jnp.maximum(jnp.sqrt(_segsum(kk * kk, seg)), 1e-12)
    k2 = k * (1.0 + (a - 1.0) * k_a)
    return r, lw, k2, v, kk, a, g


def _transposed(z):
    return jnp.stack([z[i].T for i in range(z.shape[0])], axis=0) if z.ndim == 3 else z.T


def _solve_unit_lower(lmat, rhs):
    c = lmat.shape[-1]
    row = lax.broadcasted_iota(jnp.int32, (c, c), 0)
    col = lax.broadcasted_iota(jnp.int32, (c, c), 1)
    eye = (row == col).astype(F32)
    ld = jnp.where(row // SUB == col // SUB, lmat, 0.0)
    lo = lmat - ld
    x = eye + ld
    m = ld
    mm = lambda p, q: _mm(p, q, "nn", WKV_PASSES)
    cat = jnp.concatenate
    m = mm(m, m)
    for _ in range(2):
        mx = mm(m, cat([m, x], axis=-1))
        m, x = mx[..., :c], x + mx[..., c:]
    x = x + mm(m, x)
    gw = mm(x, cat([lo, rhs], axis=-1))
    g, w = gw[..., :c], gw[..., c:]
    gg = mm(g, cat([g, w], axis=-1))
    w = w + gg[..., c:]
    return w + mm(gg[..., :c], w)


def _wkv_chunk_fn(s0, r, lw, k, v, kk, a):
    c = r.shape[-2]
    n = 2 * c
    row = lax.broadcasted_iota(jnp.int32, (n, n), 0)
    col = lax.broadcasted_iota(jnp.int32, (n, n), 1)
    same = (row // c) == (col // c)
    incl = jnp.logical_and(row >= col, same)
    strict = jnp.logical_and(row > col, same)
    sel = (lax.broadcasted_iota(jnp.int32, (n, 128), 0) // c) == (lax.broadcasted_iota(jnp.int32, (n, 128), 1) // HEAD_DIM)
    two = lambda z: jnp.concatenate([z, z], axis=-2)
    lw2 = two(lw)
    mm = lambda p_, q_, form: _mm(p_, q_, form, WKV_PASSES)
    cl = _mm(incl.astype(F32), lw2, "nn", "L3")
    p = jnp.exp(cl)
    pinv = jnp.exp(-cl)
    pprev = jnp.exp(cl - lw2)
    kk2 = two(kk)
    at = jnp.where(sel, -kk2 * pprev, 0.0)
    bt = jnp.where(sel, kk2 * two(a) * pinv, 0.0)
    kt = jnp.where(sel, two(k) * pinv, 0.0)
    rt = jnp.where(sel, two(r) * p, 0.0)
    vt = jnp.where(sel, two(v), 0.0)
    cat = jnp.concatenate
    bk = cat([bt, kt], axis=-2)
    arbk = mm(cat([at, rt], axis=-2), bk, "nt")
    ab, ak = jnp.where(strict, arbk[..., :n, :n], 0.0), jnp.where(strict, arbk[..., :n, n:], 0.0)
    rb, rk = jnp.where(incl, arbk[..., n:, :n], 0.0), jnp.where(incl, arbk[..., n:, n:], 0.0)
    s0t = _transposed(s0)
    u = _solve_unit_lower(ab, mm(cat([at, ak], axis=-1), cat([s0t, vt], axis=-2), "nn"))
    y2 = mm(cat([rt, rb, rk], axis=-1), cat([s0t, u, vt], axis=-2), "nn")
    plast = jnp.exp(jnp.sum(lw, axis=-2, keepdims=True))
    s1 = (s0 + mm(cat([u, vt], axis=-2), bk, "tn")) * plast
    r2 = lax.broadcasted_iota(jnp.int32, (128, 128), 0) // HEAD_DIM
    c2 = lax.broadcasted_iota(jnp.int32, (128, 128), 1) // HEAD_DIM
    return y2[..., :c, :] + y2[..., c:, :], jnp.where(r2 == c2, s1, 0.0)


def _post_fn(y, r, k2, v, g, lnw, lnb, rk):
    seg = _seg_ones(RW)
    mean = _segsum(y, seg) * (1.0 / HEAD_DIM)
    yc = y - mean
    var = _segsum(yc * yc, seg) * (1.0 / HEAD_DIM)
    yn = yc * lax.rsqrt(var + GN_EPS)
    out = yn * lnw + lnb + _segsum(r * k2 * rk, seg) * v
    return out * g


def _attn_block_fn(q, kc, vc, kp=None, vp=None):
    n = ATTN_BLOCK
    qi = lax.broadcasted_iota(jnp.int32, (n, n), 0)
    kj = lax.broadcasted_iota(jnp.int32, (n, n), 1)
    lane = lax.broadcasted_iota(jnp.int32, (1, 128), 1)
    scale = HEAD_DIM ** -0.5
    valid = kj <= qi
    keys, vals = kc, vc
    if kp is not None:
        valid = jnp.concatenate([valid, kj >= qi], axis=-1)
        keys, vals = jnp.concatenate([kc, kp], axis=-2), jnp.concatenate([vc, vp], axis=-2)
    m0 = (lane // HEAD_DIM) == 0
    q2 = jnp.concatenate([jnp.where(m0, q, 0.0), jnp.where(m0, 0.0, q)], axis=-2)
    valid2 = jnp.concatenate([valid, valid], axis=-2)
    s = jnp.where(valid2, _mm(q2, keys, "nt", ATTN_PASSES) * scale, NEG)
    m = jnp.max(s, axis=-1, keepdims=True)
    p = jnp.exp(s - m)
    den = jnp.sum(p, axis=-1, keepdims=True)
    o2 = _mm(p, vals, "nn", ATTN_PASSES) / den
    l2 = m + jnp.log(den)
    return jnp.where(m0, o2[..., :n, :], o2[..., n:, :]), jnp.where(m0, l2[..., :n, :], l2[..., n:, :])


def _attn_block_bwd(q, kc, vc, kp, vp, o, lse, do, dl):
    n = ATTN_BLOCK
    cat = jnp.concatenate
    qi = lax.broadcasted_iota(jnp.int32, (n, n), 0)
    kj = lax.broadcasted_iota(jnp.int32, (n, n), 1)
    m0 = (lax.broadcasted_iota(jnp.int32, (1, 128), 1) // HEAD_DIM) == 0
    scale = HEAD_DIM ** -0.5
    valid = kj <= qi
    keys, vals = kc, vc
    if kp is not None:
        valid = cat([valid, kj >= qi], axis=-1)
        keys, vals = cat([kc, kp], axis=-2), cat([vc, vp], axis=-2)
    stack = lambda z: cat([jnp.where(m0, z, 0.0), jnp.where(m0, 0.0, z)], axis=-2)
    q2, do2 = stack(q), stack(do)
    lse2 = cat([jnp.max(jnp.where(m0, lse, NEG), axis=-1, keepdims=True),
                jnp.max(jnp.where(m0, NEG, lse), axis=-1, keepdims=True)], axis=-2)
    delta = jnp.sum(do2 * cat([o, o], axis=-2), axis=-1, keepdims=True)
    dlse = jnp.sum(stack(dl), axis=-1, keepdims=True)
    mm = lambda a, b, form: _mm_raw(a, b, form, ATTN_PASSES)
    s = jnp.where(cat([valid, valid], axis=-2), mm(q2, keys, "nt") * scale, NEG)
    p = jnp.exp(s - lse2)
    ds = p * (mm(do2, vals, "nt") - delta + dlse)
    dq2 = mm(ds, keys, "nn") * scale
    dq = jnp.where(m0, dq2[..., :n, :], dq2[..., n:, :])
    dkeys = mm(ds, q2, "tn") * scale
    dvals = mm(p, do2, "tn")
    if kp is None:
        return dq, dkeys, dvals
    return dq, dkeys[..., :n, :], dvals[..., :n, :], dkeys[..., n:, :], dvals[..., n:, :]


def _combine_fn(o1, o2, o3, l1, l2, l3, og):
    seg = _seg_ones(o1.shape[-1])
    m = jnp.maximum(jnp.maximum(l1, l2), l3)
    e1, e2, e3 = jnp.exp(l1 - m), jnp.exp(l2 - m), jnp.exp(l3 - m)
    o = (e1 * o1 + e2 * o2 + e3 * o3) / (e1 + e2 + e3)
    o = o * lax.rsqrt(_segsum(o * o, seg) * (1.0 / HEAD_DIM) + NORM_EPS)
    return o * og


def _shifted(p, last8, first):
    prow = jnp.where(first, 0.0, last8[7:8, :])
    rolled = pltpu.roll(p, 1, axis=0)
    rid = lax.broadcasted_iota(jnp.int32, p.shape, 0)
    return jnp.where(rid == 0, prow, rolled)


_PREP_TM = 256


def _prep_specs(tm):
    vec = lambda n: pl.BlockSpec((1, n), lambda i: (0, 0))
    mat = lambda r, n: pl.BlockSpec((r, n), lambda i: (0, 0))
    return [vec(SHIFT_COLS), vec(RW), mat(128, RW), vec(RW), mat(128, RW), mat(128, RW), vec(RW), vec(RW)]


def _in_proj_prep(x, g1, win, pw):
    t = x.shape[0]
    tm = _PREP_TM

    def body(x_ref, g_ref, w_ref, mu, w0, w2p, a0, a2p, g2, k_k, k_a, h_ref, pa_ref, qkv_ref, *rest):
        outs, carry = rest[:7], rest[7]

        @pl.when(pl.program_id(0) == 0)
        def _():
            carry[...] = jnp.zeros_like(carry)

        h = _rms_fwd(x_ref[...], g_ref[...]).astype(BF16)
        h_ref[...] = h
        proj = _dot_nt(h, w_ref[...])
        p = proj[:, :SHIFT_COLS]
        pa_ref[...] = p
        for j in range(3):
            for pr in range(N_PAIR):
                c0 = SHIFT_COLS + j * RW + pr * 128
                qkv_ref[j, pr] = proj[:, c0:c0 + 128]
        pprev = _shifted(p, carry[...], pl.program_id(0) == 0)
        carry[...] = p[tm - 8:, :]
        res = _prep_fn(p, pprev, mu[...], w0[...], w2p[...], a0[...], a2p[...], g2[...], k_k[...], k_a[...])
        for o_ref, val in zip(outs, res):
            o_ref[...] = val

    row = pl.BlockSpec((tm, RW), lambda i: (i, 0))
    return pl.pallas_call(
        body, name="in_proj_prep", grid=(t // tm,),
        in_specs=[pl.BlockSpec((tm, D_MODEL), lambda i: (i, 0)), pl.BlockSpec((1, D_MODEL), lambda i: (0, 0)),
                  pl.BlockSpec((IN_COLS, D_MODEL), lambda i: (0, 0))] + _prep_specs(tm),
        out_specs=[pl.BlockSpec((tm, D_MODEL), lambda i: (i, 0)), pl.BlockSpec((tm, SHIFT_COLS), lambda i: (i, 0)),
                   pl.BlockSpec((3, N_PAIR, tm, 128), lambda i: (0, 0, i, 0))] + [row] * 7,
        out_shape=[jax.ShapeDtypeStruct((t, D_MODEL), BF16), jax.ShapeDtypeStruct((t, SHIFT_COLS), F32),
                   jax.ShapeDtypeStruct((3, N_PAIR, t, 128), F32)] + [jax.ShapeDtypeStruct((t, RW), F32)] * 7,
        scratch_shapes=[pltpu.VMEM((8, SHIFT_COLS), F32)],
        compiler_params=_params(("arbitrary",)),
    )(x, g1, win, *pw)


def _pairs(ref):
    return jnp.stack([ref[:, 128 * p:128 * (p + 1)] for p in range(N_PAIR)], axis=0)


def _wkv_fwd(r, lw, k2, v, kk, a):
    t = r.shape[0]
    nc = t // CHUNK

    def body(r_ref, lw_ref, k_ref, v_ref, kk_ref, a_ref, y_ref, s_ref, st):
        @pl.when(pl.program_id(0) == 0)
        def _():
            st[...] = jnp.zeros_like(st)

        s0 = st[...]
        s_ref[0] = s0
        y, s1 = _wkv_chunk_fn(s0, *[_pairs(ref) for ref in (r_ref, lw_ref, k_ref, v_ref, kk_ref, a_ref)])
        for p in range(N_PAIR):
            y_ref[:, 128 * p:128 * (p + 1)] = y[p]
        st[...] = s1

    blk = pl.BlockSpec((CHUNK, RW), lambda c: (c, 0))
    return pl.pallas_call(
        body, name="wkv_fwd", grid=(nc,),
        in_specs=[blk] * 6,
        out_specs=[blk, pl.BlockSpec((1, N_PAIR, 128, 128), lambda c: (c, 0, 0, 0))],
        out_shape=[jax.ShapeDtypeStruct((t, RW), F32), jax.ShapeDtypeStruct((nc, N_PAIR, 128, 128), F32)],
        scratch_shapes=[pltpu.VMEM((N_PAIR, 128, 128), F32)],
        compiler_params=_params(("arbitrary",)),
    )(r, lw, k2, v, kk, a)


ATTN_GROUP = 2


def _dilated_rows(d, r, n):
    if d == 1:
        return pl.ds(pl.multiple_of(n * ATTN_BLOCK, ATTN_BLOCK), ATTN_BLOCK)
    return pl.ds(r + n * (ATTN_BLOCK * d), ATTN_BLOCK, stride=d)


def _for_each_sequence(t, unit):
    for di, d in enumerate(DILATIONS):

        @pl.when(pl.program_id(1) == di)
        def _(di=di, d=d):
            nb = t // (ATTN_BLOCK * d)
            if d == 1:
                unit(di, [(d, 0, 0)], False)
                unit(di, [(d, 0, 1)], True)
                lax.fori_loop(1, nb // 2, lambda k, c: (unit(di, [(d, 0, 2 * k), (d, 0, 2 * k + 1)], True), c)[1], 0)
            else:

                def residues(r, carry):
                    unit(di, [(d, r, 0), (d, r + d // 2, 0)], False)
                    if nb > 1:
                        lax.fori_loop(1, nb, lambda n, c: (unit(di, [(d, r, n), (d, r + d // 2, n)], True), c)[1], 0)
                    return carry

                lax.fori_loop(0, d // 2, residues, 0)


def _take(ref, lead, rows_list):
    return jnp.stack([ref.at[(*lead, g)][rows, :] for rows in rows_list for g in range(ref.shape[len(lead)])], axis=0)


def _put(ref, lead, rows_list, val, add=False):
    k = 0
    for rows in rows_list:
        for g in range(ref.shape[len(lead)]):
            if add:
                ref.at[(*lead, g)][rows, :] += val[k]
            else:
                ref.at[(*lead, g)][rows, :] = val[k]
            k += 1


def _attn_fwd(qkv):
    t = qkv.shape[2]

    def body(q_ref, k_ref, v_ref, o_ref, l_ref):
        def unit(di, places, has_prev):
            cur = [_dilated_rows(d, r, n) for d, r, n in places]
            args = [_take(ref, (0,), cur) for ref in (q_ref, k_ref, v_ref)]
            if has_prev:
                prv = [_dilated_rows(d, r, n - 1) for d, r, n in places]
                args += [_take(ref, (0,), prv) for ref in (k_ref, v_ref)]
            o, lse = _attn_block_fn(*args)
            _put(o_ref, (0,), cur, o)
            _put(l_ref, (0,), cur, lse)

        _for_each_sequence(t, unit)

    spec = lambda j: pl.BlockSpec((1, ATTN_GROUP, t, 128), lambda i, b: (j, i, 0, 0))
    out = pl.BlockSpec((1, ATTN_GROUP, t, 128), lambda i, b: (b, i, 0, 0))
    return pl.pallas_call(
        body, name="attn_fwd", grid=(N_PAIR // ATTN_GROUP, len(DILATIONS)),
        in_specs=[spec(0), spec(1), spec(2)], out_specs=[out, out],
        out_shape=[jax.ShapeDtypeStruct((3, N_PAIR, t, 128), F32)] * 2,
        compiler_params=_params(("parallel", "arbitrary")),
    )(qkv, qkv, qkv)


_COMB_TM = 512


def _mixers_out(y, r, k2, v, g, lnw, lnb, rk, o, l, og):
    t = y.shape[0]
    tm = _COMB_TM

    def body(y_ref, r_ref, k_ref, v_ref, g_ref, lnw_ref, lnb_ref, rk_ref, o_ref, l_ref, og_ref, out_ref):
        out_ref[:, :RW] = _post_fn(y_ref[...], r_ref[...], k_ref[...], v_ref[...], g_ref[...],
                                   lnw_ref[...], lnb_ref[...], rk_ref[...]).astype(BF16)
        for p in range(N_PAIR):
            cols = slice(128 * p, 128 * (p + 1))
            out_ref[:, RW + 128 * p:RW + 128 * (p + 1)] = _combine_fn(
                o_ref[0, p], o_ref[1, p], o_ref[2, p], l_ref[0, p], l_ref[1, p], l_ref[2, p], og_ref[:, cols]).astype(BF16)

    row = pl.BlockSpec((tm, RW), lambda i: (i, 0))
    vec = pl.BlockSpec((1, RW), lambda i: (0, 0))
    blk = pl.BlockSpec((3, N_PAIR, tm, 128), lambda i: (0, 0, i, 0))
    return pl.pallas_call(
        body, name="mixers_out", grid=(t // tm,),
        in_specs=[row] * 5 + [vec] * 3 + [blk, blk, vec], out_specs=pl.BlockSpec((tm, D_MODEL), lambda i: (i, 0)),
        out_shape=jax.ShapeDtypeStruct((t, D_MODEL), BF16),
        compiler_params=_params(("parallel",)),
    )(y, r, k2, v, g, lnw, lnb, rk, o, l, og)


def _ffn_all(x, ycat, wg, wu, wd, wout, g2, gf, tgt):
    t = x.shape[0]
    tm = 256

    def body(x_ref, y_ref, wg_ref, wu_ref, wd_ref, wo_ref, g2_ref, gf_ref, t_ref,
             h_ref, act_ref, dx2b_ref, dgt_ref, dup_ref, dx1b_ref, dx1_ref, dya_ref, dyb_ref, loss_ref, dgf_ref, dg2_ref,
             gt_s, up_s):
        first = pl.program_id(0) == 0
        x1 = x_ref[...] + _dot(y_ref[...], wo_ref[...])
        h = _rms_fwd(x1, g2_ref[...]).astype(BF16)
        h_ref[...] = h
        for c0 in range(0, D_FF, FF_CHUNK):
            cols = slice(c0, c0 + FF_CHUNK)
            gt = _dot_nt(h, wg_ref[cols, :])
            up = _dot_nt(h, wu_ref[cols, :])
            gt_s[:, cols] = gt.astype(BF16)
            up_s[:, cols] = up.astype(BF16)
            act_ref[:, cols] = (gt * _sigmoid(gt) * up).astype(BF16)
        x2 = x1 + _dot(act_ref[...], wd_ref[...])
        gf_ = gf_ref[...]
        diff = _rms_fwd(x2, gf_) - t_ref[...]
        lrow = 0.5 * jnp.sum(_colsum8(diff * diff), axis=1, keepdims=True) * (1.0 / D_MODEL)
        _acc(loss_ref, jnp.broadcast_to(lrow, (8, 128)), first)
        dx2, dgr = _rms_bwd(diff * (1.0 / D_MODEL), x2, gf_)
        _acc(dgf_ref, _colsum8(dgr), first)
        dx2b = dx2.astype(BF16)
        dx2b_ref[...] = dx2b
        for c0 in range(0, D_FF, FF_CHUNK):
            cols = slice(c0, c0 + FF_CHUNK)
            dact = _dot_nt(dx2b, wd_ref[cols, :])
            gt = gt_s[:, cols].astype(F32)
            sg = _sigmoid(gt)
            dgt_ref[:, cols] = (dact * up_s[:, cols].astype(F32) * sg * (1.0 + gt * (1.0 - sg))).astype(BF16)
            dup_ref[:, cols] = (dact * gt * sg).astype(BF16)
        dh = _dot(dgt_ref[...], wg_ref[...]) + _dot(dup_ref[...], wu_ref[...])
        dxn, dgr2 = _rms_bwd(dh, x1, g2_ref[...])
        _acc(dg2_ref, _colsum8(dgr2), first)
        dx1 = dx2 + dxn
        dx1_ref[...] = dx1
        dx1b = dx1.astype(BF16)
        dx1b_ref[...] = dx1b
        dy = _dot_nt(dx1b, wo_ref[...])
        dya_ref[...] = dy[:, :RW]
        dyb_ref[...] = dy[:, RW:]

    row = pl.BlockSpec((tm, D_MODEL), lambda i: (i, 0))
    wide = pl.BlockSpec((tm, D_FF), lambda i: (i, 0))
    half = pl.BlockSpec((tm, RW), lambda i: (i, 0))
    wsp = pl.BlockSpec((D_FF, D_MODEL), lambda i: (0, 0))
    vec = pl.BlockSpec((1, D_MODEL), lambda i: (0, 0))
    part = pl.BlockSpec((8, D_MODEL), lambda i: (0, 0))
    bf = lambda n: jax.ShapeDtypeStruct((t, n), BF16)
    return pl.pallas_call(
        body, name="ffn_all", grid=(t // tm,),
        in_specs=[row, row, wsp, wsp, wsp, pl.BlockSpec((D_MODEL, D_MODEL), lambda i: (0, 0)), vec, vec, row],
        out_specs=[row, wide, row, wide, wide, row, row, half, half, pl.BlockSpec((8, 128), lambda i: (0, 0)), part, part],
        out_shape=[bf(D_MODEL), bf(D_FF), bf(D_MODEL), bf(D_FF), bf(D_FF), bf(D_MODEL),
                   jax.ShapeDtypeStruct((t, D_MODEL), F32), jax.ShapeDtypeStruct((t, RW), F32),
                   jax.ShapeDtypeStruct((t, RW), F32), jax.ShapeDtypeStruct((8, 128), F32),
                   jax.ShapeDtypeStruct((8, D_MODEL), F32), jax.ShapeDtypeStruct((8, D_MODEL), F32)],
        scratch_shapes=[pltpu.VMEM((tm, D_FF), BF16), pltpu.VMEM((tm, D_FF), BF16)],
        compiler_params=_params(("arbitrary",)),
    )(x, ycat, wg, wu, wd, wout, g2, gf, tgt)


def _wgrad(a, b, tk, tn, name):
    t, kdim = a.shape
    ndim = b.shape[1]

    def body(a_ref, b_ref, o_ref):
        o_ref[...] = _dot_tn(a_ref[...], b_ref[...]).astype(BF16)

    return pl.pallas_call(
        body, name=name, grid=(kdim // tk, ndim // tn),
        in_specs=[pl.BlockSpec((t, tk), lambda i, j: (0, i)), pl.BlockSpec((t, tn), lambda i, j: (0, j))],
        out_specs=pl.BlockSpec((tk, tn), lambda i, j: (i, j)),
        out_shape=jax.ShapeDtypeStruct((kdim, ndim), BF16),
        compiler_params=_params(("parallel", "parallel")),
    )(a, b)


def _wkv_post_bwd(dya, y, g, lnw, lnb, rk, s0s, r, lw, k2, v, kk, a):
    t = r.shape[0]
    nc = t // CHUNK

    def body(d_ref, y_ref, g_ref, lnw_ref, lnb_ref, rk_ref, s_ref, r_ref, lw_ref, k_ref, v_ref, kk_ref, a_ref,
             dr_ref, dlw_ref, dk_ref, dv_ref, dkk_ref, da_ref, dg_ref, dlnw_ref, dlnb_ref, drk_ref, ds):
        first = pl.program_id(0) == 0

        @pl.when(first)
        def _():
            ds[...] = jnp.zeros_like(ds)

        ones = jnp.ones((CHUNK, 1), F32)
        prim = (y_ref[...], r_ref[...], k_ref[...], v_ref[...], g_ref[...],
                ones * lnw_ref[...], ones * lnb_ref[...], ones * rk_ref[...])
        _, vjp_post = jax.vjp(_post_fn, *prim)
        dy, dr_p, dk_p, dv_p, dg, dlnw, dlnb, drk = vjp_post(d_ref[...])
        dg_ref[...] = dg
        _acc(dlnw_ref, _colsum8(dlnw), first)
        _acc(dlnb_ref, _colsum8(dlnb), first)
        _acc(drk_ref, _colsum8(drk), first)

        split = lambda z: jnp.stack([z[:, 128 * p:128 * (p + 1)] for p in range(N_PAIR)], axis=0)
        _, vjp = jax.vjp(_wkv_chunk_fn, s_ref[0],
                         *[_pairs(ref) for ref in (r_ref, lw_ref, k_ref, v_ref, kk_ref, a_ref)])
        res = vjp((split(dy), ds[...]))
        ds[...] = res[0]
        extra = (split(dr_p), None, split(dk_p), split(dv_p), None, None)
        for ref, val, more in zip((dr_ref, dlw_ref, dk_ref, dv_ref, dkk_ref, da_ref), res[1:], extra):
            for p in range(N_PAIR):
                ref[:, 128 * p:128 * (p + 1)] = val[p] if more is None else val[p] + more[p]

    blk = pl.BlockSpec((CHUNK, RW), lambda c: (nc - 1 - c, 0))
    vec = pl.BlockSpec((1, RW), lambda c: (0, 0))
    part = pl.BlockSpec((8, RW), lambda c: (0, 0))
    return pl.pallas_call(
        body, name="wkv_post_bwd", grid=(nc,),
        in_specs=[blk, blk, blk, vec, vec, vec, pl.BlockSpec((1, N_PAIR, 128, 128), lambda c: (nc - 1 - c, 0, 0, 0))]
                 + [blk] * 6,
        out_specs=[blk] * 7 + [part] * 3,
        out_shape=[jax.ShapeDtypeStruct((t, RW), F32)] * 7 + [jax.ShapeDtypeStruct((8, RW), F32)] * 3,
        scratch_shapes=[pltpu.VMEM((N_PAIR, 128, 128), F32)],
        compiler_params=_params(("arbitrary",)),
    )(dya, y, g, lnw, lnb, rk, s0s, r, lw, k2, v, kk, a)


def _prep_in_proj_bwd(proj, pw, douts, dq, dk, dv, win, x, g1, dx1):
    t = proj.shape[0]
    tm = _PREP_TM
    nt = t // tm

    def body(p_ref, l8_ref, mu, w0, w2p, a0, a2p, g2, k_k, k_a, dr, dlw, dk2, dv, dkk, da, dg,
             dq_ref, dkq_ref, dvq_ref, w_ref, x_ref, g1_ref, dx1_ref,
             dproj_ref, dx_ref, dg1_ref, dmu_ref, dw0_ref, dw2_ref, da0_ref, da2_ref, dg2_ref, dkk_ref, dka_ref, carry):
        i = pl.program_id(0)
        first = i == 0

        @pl.when(first)
        def _():
            carry[...] = jnp.zeros_like(carry)

        p = p_ref[...]
        pprev = _shifted(p, l8_ref[...], i == nt - 1)
        ones = jnp.ones((tm, 1), F32)
        prim = (p, pprev, ones * mu[...], ones * w0[...], w2p[...], ones * a0[...], a2p[...], g2[...],
                ones * k_k[...], ones * k_a[...])
        _, vjp = jax.vjp(_prep_fn, *prim)
        dp, dpp, dmu, dw0, dw2, da0, da2, dg2, dkk_, dka = vjp(
            (dr[...], dlw[...], dk2[...], dv[...], dkk[...], da[...], dg[...]))
        up = pltpu.roll(dpp, tm - 1, axis=0)
        rid = lax.broadcasted_iota(jnp.int32, dpp.shape, 0)
        dpa = dp + jnp.where(rid == tm - 1, carry[0:1, :], up)
        carry[...] = jnp.broadcast_to(dpp[0:1, :], carry.shape)
        _acc(dmu_ref, _colsum8(dmu), first)
        _acc(dw0_ref, _colsum8(dw0), first)
        _acc(dw2_ref, dw2, first)
        _acc(da0_ref, _colsum8(da0), first)
        _acc(da2_ref, da2, first)
        _acc(dg2_ref, dg2, first)
        _acc(dkk_ref, _colsum8(dkk_), first)
        _acc(dka_ref, _colsum8(dka), first)
        parts = [dpa] + [ref[pr] for ref in (dq_ref, dkq_ref, dvq_ref) for pr in range(N_PAIR)]
        dproj = jnp.concatenate([z.astype(BF16) for z in parts], axis=1)
        dproj_ref[...] = dproj
        dxn, dgr = _rms_bwd(_dot(dproj, w_ref[...]), x_ref[...], g1_ref[...])
        dx_ref[...] = dx1_ref[...] + dxn
        _acc(dg1_ref, _colsum8(dgr), first)

    rev = lambda i: (nt - 1 - i, 0)
    row = pl.BlockSpec((tm, RW), rev)
    wide = pl.BlockSpec((tm, D_MODEL), rev)
    pair = pl.BlockSpec((N_PAIR, tm, 128), lambda i: (0, nt - 1 - i, 0))
    part = lambda n: pl.BlockSpec((8, n), lambda i: (0, 0))
    mat = pl.BlockSpec((128, RW), lambda i: (0, 0))
    return pl.pallas_call(
        body, name="prep_in_proj_bwd", grid=(nt,),
        in_specs=[pl.BlockSpec((tm, SHIFT_COLS), rev),
                  pl.BlockSpec((8, SHIFT_COLS), lambda i: (jnp.maximum((nt - 1 - i) * (tm // 8) - 1, 0), 0))]
                 + _prep_specs(tm) + [row] * 7
                 + [pair] * 3 + [pl.BlockSpec((IN_COLS, D_MODEL), lambda i: (0, 0)), wide,
                                 pl.BlockSpec((1, D_MODEL), lambda i: (0, 0)), wide],
        out_specs=[pl.BlockSpec((tm, IN_COLS), rev), wide, part(D_MODEL), part(SHIFT_COLS), part(RW), mat, part(RW), mat,
                   mat, part(RW), part(RW)],
        out_shape=[jax.ShapeDtypeStruct((t, IN_COLS), BF16), jax.ShapeDtypeStruct((t, D_MODEL), F32),
                   jax.ShapeDtypeStruct((8, D_MODEL), F32), jax.ShapeDtypeStruct((8, SHIFT_COLS), F32),
                   jax.ShapeDtypeStruct((8, RW), F32), jax.ShapeDtypeStruct((128, RW), F32),
                   jax.ShapeDtypeStruct((8, RW), F32), jax.ShapeDtypeStruct((128, RW), F32),
                   jax.ShapeDtypeStruct((128, RW), F32), jax.ShapeDtypeStruct((8, RW), F32),
                   jax.ShapeDtypeStruct((8, RW), F32)],
        scratch_shapes=[pltpu.VMEM((8, SHIFT_COLS), F32)],
        compiler_params=_params(("arbitrary",)),
    )(proj, proj, *pw, *douts, dq, dk, dv, win, x, g1, dx1)


def _combine_bwd(dyb, o, l, og):
    t = dyb.shape[0]
    tm = _COMB_TM

    def body(d_ref, o_ref, l_ref, og_ref, do_ref, dl_ref, dog_ref):
        ones = jnp.ones((tm, 1), F32)
        dog = []
        for p in range(N_PAIR):
            cols = slice(128 * p, 128 * (p + 1))
            _, vjp = jax.vjp(_combine_fn, o_ref[0, p], o_ref[1, p], o_ref[2, p], l_ref[0, p], l_ref[1, p], l_ref[2, p],
                             ones * og_ref[:, cols])
            res = vjp(d_ref[:, cols])
            for b in range(3):
                do_ref[b, p] = res[b]
                dl_ref[b, p] = res[3 + b]
            dog.append(_colsum8(res[6]))
        _acc(dog_ref, jnp.concatenate(dog, axis=1), pl.program_id(0) == 0)

    blk = pl.BlockSpec((3, N_PAIR, tm, 128), lambda i: (0, 0, i, 0))
    return pl.pallas_call(
        body, name="attn_combine_bwd", grid=(t // tm,),
        in_specs=[pl.BlockSpec((tm, RW), lambda i: (i, 0)), blk, blk, pl.BlockSpec((1, RW), lambda i: (0, 0))],
        out_specs=[blk, blk, pl.BlockSpec((8, RW), lambda i: (0, 0))],
        out_shape=[jax.ShapeDtypeStruct((3, N_PAIR, t, 128), F32)] * 2 + [jax.ShapeDtypeStruct((8, RW), F32)],
        compiler_params=_params(("arbitrary",)),
    )(dyb, o, l, og)


def _attn_bwd(do, dl, o, lse, qkv):
    t = qkv.shape[2]

    def body(do_ref, dl_ref, o_ref, l_ref, q_ref, k_ref, v_ref, dq_ref, dk_ref, dv_ref):
        @pl.when(pl.program_id(1) == 0)
        def _():
            for ref in (dq_ref, dk_ref, dv_ref):
                ref[...] = jnp.zeros_like(ref)

        def unit(di, places, has_prev):
            cur = [_dilated_rows(d, r, n) for d, r, n in places]
            q, kc, vc = [_take(ref, (0,), cur) for ref in (q_ref, k_ref, v_ref)]
            kp = vp = None
            if has_prev:
                prv = [_dilated_rows(d, r, n - 1) for d, r, n in places]
                kp, vp = [_take(ref, (0,), prv) for ref in (k_ref, v_ref)]
            res = _attn_block_bwd(q, kc, vc, kp, vp, *[_take(ref, (0,), cur) for ref in (o_ref, l_ref, do_ref, dl_ref)])
            _put(dq_ref, (), cur, res[0], add=True)
            _put(dk_ref, (), cur, res[1], add=True)
            _put(dv_ref, (), cur, res[2], add=True)
            if has_prev:
                _put(dk_ref, (), prv, res[3], add=True)
                _put(dv_ref, (), prv, res[4], add=True)

        _for_each_sequence(t, unit)

    spec = lambda j: pl.BlockSpec((1, ATTN_GROUP, t, 128), lambda i, b: (j, i, 0, 0))
    branch = pl.BlockSpec((1, ATTN_GROUP, t, 128), lambda i, b: (b, i, 0, 0))
    out = pl.BlockSpec((ATTN_GROUP, t, 128), lambda i, b: (i, 0, 0))
    return pl.pallas_call(
        body, name="attn_bwd", grid=(N_PAIR // ATTN_GROUP, len(DILATIONS)),
        in_specs=[branch] * 4 + [spec(0), spec(1), spec(2)], out_specs=[out] * 3,
        out_shape=[jax.ShapeDtypeStruct((N_PAIR, t, 128), F32)] * 3,
        compiler_params=_params(("parallel", "arbitrary")),
    )(do, dl, o, lse, qkv, qkv, qkv)


def _pad_lora(w, lo):
    z = jnp.zeros((64, RW), F32)
    return jnp.concatenate([w, z], axis=0) if lo == 0 else jnp.concatenate([z, w], axis=0)


def _local_step(x, tgt, win, vecs, w2, a2, g2m, get_rest, send_rest):
    pw = (vecs["mu_shift"], vecs["decay_w0"], _pad_lora(w2, 0), vecs["iclr_a0"], _pad_lora(a2, 64), g2m,
          vecs["k_k"], vecs["k_a"])
    h, proj, qkv, r, lw, k2, v, kk, a, g = _in_proj_prep(x, vecs["mix_norm_g"], win, pw)
    y, s0s = _wkv_fwd(r, lw, k2, v, kk, a)
    o_att, l_att = _attn_fwd(qkv)
    ycat = _mixers_out(y, r, k2, v, g, vecs["ln_x_w"], vecs["ln_x_b"], vecs["r_k"], o_att, l_att, vecs["attn_out_g"])
    wout, wg, wu, wd = get_rest(ycat)
    h2, act, dx2b, dgt, dup, dx1b, dx1, dya, dyb, loss8, dgf, dg2n = _ffn_all(
        x, ycat, wg, wu, wd, wout, vecs["ffn_norm_g"], vecs["final_norm_g"], tgt)
    gw = {
        "w_down": _wgrad(act, dx2b, 1408, 1024, "wgrad_down"),
        "w_gate": _wgrad(dgt, h2, 1408, 1024, "wgrad_gate"),
        "w_up": _wgrad(dup, h2, 1408, 1024, "wgrad_up"),
        "w_out": _wgrad(ycat, dx1b, 1024, 1024, "wgrad_out"),
    }

    lnw = vecs["ln_x_w"] + send_rest(gw)[0, 0]
    dr, dlw, dk2, dv_, dkk, da, dg, dlnw, dlnb, drk = _wkv_post_bwd(
        dya, y, g, lnw, vecs["ln_x_b"], vecs["r_k"], s0s, r, lw, k2, v, kk, a)
    do_att, dl_att, dog = _combine_bwd(dyb, o_att, l_att, vecs["attn_out_g"])
    dq, dk, dv = _attn_bwd(do_att, dl_att, o_att, l_att, qkv)
    dproj, dx, dg1, dmu, dw0, dw2p, da0, da2p, dg2m, dk_k, dk_a = _prep_in_proj_bwd(
        proj, pw, (dr, dlw, dk2, dv_, dkk, da, dg), dq, dk, dv, win, x, vecs["mix_norm_g"], dx1)
    gw["w_in"] = _wgrad(dproj, h, 1664, 1024, "wgrad_in")
    gw["decay_w2"] = dw2p[:64]
    gw["iclr_a2"] = da2p[64:]
    gw["gate_g2"] = dg2m
    gv = {"mix_norm_g": dg1, "mu_shift": dmu, "decay_w0": dw0, "iclr_a0": da0, "k_k": dk_k, "k_a": dk_a, "r_k": drk,
          "ln_x_w": dlnw, "ln_x_b": dlnb, "attn_out_g": dog, "ffn_norm_g": dg2n, "final_norm_g": dgf}
    return loss8, dx, gw, gv


N_CHIP = 4
N_DEV = 8
MATS = ("w_in", "w_out", "w_gate", "w_up", "w_down")
LORAS = ("decay_w2", "iclr_a2", "gate_g2")
VECS = (("mix_norm_g", 1024), ("mu_shift", 1792), ("decay_w0", 512), ("iclr_a0", 512), ("k_k", 512), ("k_a", 512),
        ("r_k", 512), ("ln_x_w", 512), ("ln_x_b", 512), ("attn_out_g", 512), ("ffn_norm_g", 1024),
        ("final_norm_g", 1024))
N_VEC = sum(n for _, n in VECS)
N_SMALL = N_VEC + 128
ANY = pl.BlockSpec(memory_space=pl.ANY)


def _flip(v, f):
    return 1 - v if f else v


class _Me:
    def __init__(self, mode):
        x, y, c = lax.axis_index("x"), lax.axis_index("y"), lax.axis_index("c")
        self.core, self.chip, self.dev = c, 2 * x + y, 4 * x + 2 * y + c
        self.sibling = (x, y, 1 - c)
        if mode == "chips":
            self.peers = [(px, py, c) for px, py in ((1 - x, y), (x, 1 - y), (1 - x, 1 - y))]
        else:
            self.peers = [(_flip(x, k & 4), _flip(y, k & 2), _flip(c, k & 1)) for k in range(1, N_DEV)]


def _half(core, rows):
    h = rows // 2
    return pl.ds(pl.multiple_of(core * h, h), h)


_BY_CHIP = ("gather", "chipsum")


def _peer_copy(srcs, dsts, kinds, send_sems, recv_sems, me, j, i, incoming):
    px, py, pc = me.peers[j]
    pchip, pdev = 2 * px + py, 4 * px + 2 * py + pc
    src, dst, kind = srcs[i], dsts[i], kinds[i]
    if kind == "gather":
        rows = _half(me.core, src.shape[1])
        src, dst = src.at[me.chip, rows], dst.at[pchip if incoming else me.chip, rows]
    elif kind == "scatter":
        src, dst = src.at[pchip, _half(pc, src.shape[1])], dst.at[pdev if incoming else me.dev]
    elif kind == "chipsum":
        src, dst = src.at[pchip], dst.at[pchip if incoming else me.chip]
    else:
        dst = dst.at[pdev if incoming else me.dev]
    n = len(srcs)
    return pltpu.make_async_remote_copy(src_ref=src, dst_ref=dst, send_sem=send_sems.at[n * j + i],
                                        recv_sem=recv_sems.at[n * j + i], device_id=(px, py, pc), device_id_type=MESH)


def _mode(kinds):
    return "chips" if kinds[0] in _BY_CHIP else "devs"


def _npeer(kinds):
    return N_CHIP - 1 if kinds[0] in _BY_CHIP else N_DEV - 1


def _sibling_halves(gs, name):
    n = len(gs)

    def body(*refs):
        srcs, dsts, send_sems, recv_sems = refs[:n], refs[n:2 * n], refs[2 * n], refs[2 * n + 1]
        me = _Me("chips")

        def copy(i, p):
            return pltpu.make_async_remote_copy(
                src_ref=srcs[i].at[p, _half(1 - me.core, srcs[i].shape[1])], dst_ref=dsts[i].at[p],
                send_sem=send_sems.at[N_CHIP * i + p], recv_sem=recv_sems.at[N_CHIP * i + p],
                device_id=me.sibling, device_id_type=MESH)

        copies = [copy(i, p) for i in range(n) for p in range(N_CHIP)]
        for cp in copies:
            cp.start()
        for cp in copies:
            cp.wait()

    return pl.pallas_call(
        body, name=name, in_specs=[ANY] * n, out_specs=[ANY] * n,
        out_shape=[jax.ShapeDtypeStruct((N_CHIP, g.shape[1] // 2, g.shape[2]), g.dtype) for g in gs],
        scratch_shapes=[pltpu.SemaphoreType.DMA((N_CHIP * n,)), pltpu.SemaphoreType.DMA((N_CHIP * n,))],
    )(*gs)


def _add_halves(g, other, core, tr, name):
    _, h, cols = other.shape

    def body(core_ref, g_ref, o_ref, out_ref):
        out_ref[...] = (g_ref[...].astype(F32) + o_ref[...].astype(F32)).astype(BF16)

    blk = lambda off: pl.BlockSpec((1, tr, cols), lambda p, i, core_ref: (p, core_ref[0] * (h // tr) * off + i, 0))
    return pl.pallas_call(
        body, name=name,
        grid_spec=pltpu.PrefetchScalarGridSpec(num_scalar_prefetch=1, grid=(N_CHIP, h // tr),
                                               in_specs=[blk(1), blk(0)], out_specs=blk(0)),
        out_shape=jax.ShapeDtypeStruct(other.shape, BF16),
        compiler_params=_params(("parallel", "parallel")),
    )(core, g, other)


def _swap_gathered(lands, name):
    n = len(lands)

    def body(*refs):
        dsts, send_sems, recv_sems = refs[n:2 * n], refs[2 * n], refs[2 * n + 1]
        me = _Me("chips")

        def copy(j, i, incoming):
            px, py, _ = me.peers[j]
            rows_out, rows_in = _half(me.core, dsts[i].shape[1]), _half(1 - me.core, dsts[i].shape[1])
            return pltpu.make_async_remote_copy(
                src_ref=dsts[i].at[2 * px + py, rows_out], dst_ref=dsts[i].at[2 * px + py, rows_in if incoming else rows_out],
                send_sem=send_sems.at[n * j + i], recv_sem=recv_sems.at[n * j + i], device_id=me.sibling, device_id_type=MESH)

        sends = [copy(j, i, False) for j in range(3) for i in range(n)]
        for cp in sends:
            cp.start()
        for j in range(3):
            for i in range(n):
                copy(j, i, True).wait_recv()
        for cp in sends:
            cp.wait_send()

    return pl.pallas_call(
        body, name=name, in_specs=[ANY] * n, out_specs=[ANY] * n,
        out_shape=[jax.ShapeDtypeStruct(l.shape, l.dtype) for l in lands],
        input_output_aliases={i: i for i in range(n)},
        scratch_shapes=[pltpu.SemaphoreType.DMA((3 * n,)), pltpu.SemaphoreType.DMA((3 * n,))],
    )(*lands)


def _join_halves(sums, name):
    n = len(sums)

    def body(*refs):
        dsts, send_sems, recv_sems = refs[n:2 * n], refs[2 * n], refs[2 * n + 1]
        me = _Me("chips")

        def copy(i, incoming):
            mine, other = _half(me.core, dsts[i].shape[0]), _half(1 - me.core, dsts[i].shape[0])
            return pltpu.make_async_remote_copy(src_ref=dsts[i].at[mine], dst_ref=dsts[i].at[other if incoming else mine],
                                                send_sem=send_sems.at[i], recv_sem=recv_sems.at[i],
                                                device_id=me.sibling, device_id_type=MESH)

        sends = [copy(i, False) for i in range(n)]
        for cp in sends:
            cp.start()
        for i in range(n):
            copy(i, True).wait_recv()
        for cp in sends:
            cp.wait_send()

    return pl.pallas_call(
        body, name=name, in_specs=[ANY] * n, out_specs=[ANY] * n,
        out_shape=[jax.ShapeDtypeStruct(s.shape, s.dtype) for s in sums],
        input_output_aliases={i: i for i in range(n)},
        scratch_shapes=[pltpu.SemaphoreType.DMA((n,)), pltpu.SemaphoreType.DMA((n,))],
    )(*sums)


HBM = pl.BlockSpec(memory_space=pltpu.HBM)
SEM = pl.BlockSpec(memory_space=pltpu.SEMAPHORE)
EFFECT = pltpu.SideEffectType.DATAFLOW_SIDE_EFFECTING


def _swap_start(arrs, lands, kinds, name):
    n = len(lands)
    ops = list(lands) if arrs is None else [*arrs, *lands]
    k = len(ops)

    def body(*refs):
        srcs, dsts, send_sems, recv_sems, token = refs[:n], refs[k - n:k], refs[k], refs[k + 1], refs[-1]
        me = _Me(_mode(kinds))
        for j in range(len(me.peers)):
            for i in range(n):
                _peer_copy(srcs, dsts, kinds, send_sems, recv_sems, me, j, i, False).start()
        token[...] = jnp.zeros_like(token)

    ns = _npeer(kinds) * n
    outs = pl.pallas_call(
        body, name=name,
        out_shape=(pltpu.SemaphoreType.DMA((ns,)), pltpu.SemaphoreType.DMA((ns,)),
                   *[pltpu.HBM(a.shape, a.dtype) for a in ops], jax.ShapeDtypeStruct((8, 128), F32)),
        in_specs=[HBM] * k, out_specs=(SEM, SEM, *[HBM] * k, pl.BlockSpec(memory_space=pltpu.VMEM)),
        input_output_aliases={i: 2 + i for i in range(k)},
        compiler_params=pltpu.CompilerParams(has_side_effects=EFFECT),
    )(*[pltpu.with_memory_space_constraint(a, pltpu.HBM) for a in ops])
    return outs[0], outs[1], outs[2:2 + k - n], outs[2 + k - n:2 + k], outs[-1]


def _swap_wait(send_sems, recv_sems, srcs_thru, lands_thru, after, kinds, name):
    n = len(lands_thru)
    ops = [*srcs_thru, *lands_thru]
    k = len(ops)

    def body(*refs):
        srcs, dsts, s_sems, r_sems = refs[:n], refs[k - n:k], refs[k], refs[k + 1]
        me = _Me(_mode(kinds))
        for j in range(len(me.peers)):
            for i in range(n):
                cp = _peer_copy(srcs, dsts, kinds, s_sems, r_sems, me, j, i, True)
                cp.wait_send()
                cp.wait_recv()

    outs = pl.pallas_call(
        body, name=name,
        out_shape=tuple(pltpu.HBM(a.shape, a.dtype) for a in ops),
        in_specs=[HBM] * k + [SEM, SEM, ANY], out_specs=tuple([HBM] * k),
        input_output_aliases={i: i for i in range(k)},
        compiler_params=pltpu.CompilerParams(has_side_effects=EFFECT),
    )(*ops, send_sems, recv_sems, after)
    return outs[k - n:]


def _adamw(w, g, m, v):
    m = ADAM_B1 * m + (1.0 - ADAM_B1) * g
    v = ADAM_B2 * v + (1.0 - ADAM_B2) * (g * g)
    m_hat = m / (1.0 - ADAM_B1 ** ADAM_STEP)
    v_hat = v / (1.0 - ADAM_B2 ** ADAM_STEP)
    delta = -ADAM_LR * (m_hat / (jnp.sqrt(v_hat) + ADAM_EPS) + ADAM_WD * w)
    return delta, m, v


def _reduce8(rbuf, core, tr, name):
    slots, h, cols = rbuf.shape

    def body(core_ref, r_ref, g_ref):
        g = r_ref[0].astype(F32)
        for s in range(1, slots):
            g = g + r_ref[s].astype(F32)
        g_ref[...] = g

    return pl.pallas_call(
        body, name=name,
        grid_spec=pltpu.PrefetchScalarGridSpec(
            num_scalar_prefetch=1, grid=(h // tr,),
            in_specs=[pl.BlockSpec((slots, tr, cols), lambda i, core_ref: (0, i, 0))],
            out_specs=pl.BlockSpec((tr, cols), lambda i, core_ref: (core_ref[0] * (h // tr) + i, 0))),
        out_shape=jax.ShapeDtypeStruct((2 * h, cols), F32),
        compiler_params=_params(("parallel",)),
    )(core, rbuf)


def _adamw_call(g, w, m, v, tr, name):
    _, rows, cols = w.shape

    def body(g_in, w_ref, m_ref, v_ref, g_ref, d_ref, nm_ref, nv_ref):
        g = g_in[...]
        g_ref[0] = g
        d_ref[0], nm_ref[0], nv_ref[0] = _adamw(w_ref[0], g, m_ref[0], v_ref[0])

    row = pl.BlockSpec((1, tr, cols), lambda i: (0, i, 0))
    return pl.pallas_call(
        body, name=name, grid=(rows // tr,),
        in_specs=[pl.BlockSpec((tr, cols), lambda i: (i, 0)), row, row, row], out_specs=[row] * 4,
        out_shape=[jax.ShapeDtypeStruct(w.shape, F32)] * 4,
        compiler_params=_params(("parallel",)),
    )(g, w, m, v)


def _rowsum_small(parts, loss8):
    def body(*refs):
        out = refs[-1]
        c0 = 0
        for ref in refs[:-1]:
            n = ref.shape[1]
            out[:, c0:c0 + n] = jnp.sum(ref[...], axis=0, keepdims=True)
            c0 += n

    return pl.pallas_call(body, name="rowsum_small", out_shape=jax.ShapeDtypeStruct((1, N_SMALL), F32))(*parts, loss8)


def _reduce_adamw_small(sbuf, ws, ms, vs):
    nv = len(ws)

    def body(*refs):
        s_ref, ins, outs = refs[0], refs[1:1 + 3 * nv], refs[1 + 3 * nv:]
        tot = s_ref[0]
        for s in range(1, N_DEV):
            tot = tot + s_ref[s]
        c0 = 0
        for i in range(nv):
            n = ins[i].shape[1]
            g = tot[:, c0:c0 + n]
            outs[i][...] = g
            outs[nv + i][...], outs[2 * nv + i][...], outs[3 * nv + i][...] = _adamw(
                ins[i][...], g, ins[nv + i][...], ins[2 * nv + i][...])
            c0 += n
        outs[-1][...] = tot[:, c0:]

    return pl.pallas_call(
        body, name="reduce_adamw_small",
        out_shape=[jax.ShapeDtypeStruct(a.shape, F32) for a in ws] * 4 + [jax.ShapeDtypeStruct((1, 128), F32)],
    )(sbuf, *ws, *ms, *vs)


_TRANSPOSED = ("w_in", "w_gate", "w_up")
_ROW_STACKED = MATS
_ADAM_TILE = {"w_in": 208, "w_out": 256, "w_gate": 176, "w_up": 176, "w_down": 176, "lora": 256}
_SUM_TILE = {"w_in": 208, "w_out": 128, "w_gate": 176, "w_up": 176, "w_down": 176, "lora": 128}


def _full(n, stacked):
    p, r, c = stacked.shape
    if n in _ROW_STACKED:
        return stacked.reshape(p * r, c)
    return jnp.transpose(stacked, (1, 0, 2)).reshape(r, p * c)


def _by_chip(n, full):
    if n in _ROW_STACKED:
        return full.reshape(N_CHIP, full.shape[0] // N_CHIP, full.shape[1])
    r, c = full.shape
    return jnp.transpose(full.reshape(r, N_CHIP, c // N_CHIP), (1, 0, 2))


def _with_own(land_shape, dtype, own, slot):
    return lax.dynamic_update_slice(lax.empty(land_shape, dtype), own[None], (slot,) + (0,) * own.ndim)


def _cast_into_slot(a, chip, tr, name, after=None):
    rows, cols = a.shape

    def body(chip_ref, a_ref, *rest):
        rest[-1][0] = a_ref[...].astype(BF16)

    extra = [] if after is None else [after]
    return pl.pallas_call(
        body, name=name,
        grid_spec=pltpu.PrefetchScalarGridSpec(
            num_scalar_prefetch=1, grid=(rows // tr,),
            in_specs=[pl.BlockSpec((tr, cols), lambda i, chip_ref: (i, 0))] + [ANY] * len(extra),
            out_specs=pl.BlockSpec((1, tr, cols), lambda i, chip_ref: (chip_ref[0], i, 0))),
        out_shape=jax.ShapeDtypeStruct((N_CHIP, rows, cols), BF16),
        compiler_params=_params(("parallel",)),
    )(chip, a, *extra)


def kernel(x, mix_norm_g, w_in, mu_shift, decay_w0, decay_w2, iclr_a0, iclr_a2, gate_g2, k_k, k_a, r_k, ln_x_w, ln_x_b, attn_out_g, w_out, ffn_norm_g, w_gate, w_up, w_down, final_norm_g, loss_target, m_mix_norm_g, m_w_in, m_mu_shift, m_decay_w0, m_decay_w2, m_iclr_a0, m_iclr_a2, m_gate_g2, m_k_k, m_k_a, m_r_k, m_ln_x_w, m_ln_x_b, m_attn_out_g, m_w_out, m_ffn_norm_g, m_w_gate, m_w_up, m_w_down, m_final_norm_g, v_mix_norm_g, v_w_in, v_mu_shift, v_decay_w0, v_decay_w2, v_iclr_a0, v_iclr_a2, v_gate_g2, v_k_k, v_k_a, v_r_k, v_ln_x_w, v_ln_x_b, v_attn_out_g, v_w_out, v_ffn_norm_g, v_w_gate, v_w_up, v_w_down, v_final_norm_g):
    names = ("mix_norm_g", "w_in", "mu_shift", "decay_w0", "decay_w2", "iclr_a0", "iclr_a2", "gate_g2", "k_k", "k_a",
             "r_k", "ln_x_w", "ln_x_b", "attn_out_g", "w_out", "ffn_norm_g", "w_gate", "w_up", "w_down", "final_norm_g")
    w = dict(zip(names, (mix_norm_g, w_in, mu_shift, decay_w0, decay_w2, iclr_a0, iclr_a2, gate_g2, k_k, k_a, r_k,
                         ln_x_w, ln_x_b, attn_out_g, w_out, ffn_norm_g, w_gate, w_up, w_down, final_norm_g)))
    m = dict(zip(names, (m_mix_norm_g, m_w_in, m_mu_shift, m_decay_w0, m_decay_w2, m_iclr_a0, m_iclr_a2, m_gate_g2,
                         m_k_k, m_k_a, m_r_k, m_ln_x_w, m_ln_x_b, m_attn_out_g, m_w_out, m_ffn_norm_g, m_w_gate,
                         m_w_up, m_w_down, m_final_norm_g)))
    v = dict(zip(names, (v_mix_norm_g, v_w_in, v_mu_shift, v_decay_w0, v_decay_w2, v_iclr_a0, v_iclr_a2, v_gate_g2,
                         v_k_k, v_k_a, v_r_k, v_ln_x_w, v_ln_x_b, v_attn_out_g, v_w_out, v_ffn_norm_g, v_w_gate,
                         v_w_up, v_w_down, v_final_norm_g)))
    first = ("w_in", "lora")
    rest = ("w_out", "w_gate", "w_up", "w_down")
    xi, yi, ci = lax.axis_index("x"), lax.axis_index("y"), lax.axis_index("c")
    my_chip, my_dev = 2 * xi + yi, 4 * xi + 2 * yi + ci
    gather, scatter = ("gather",) * 4, ("scatter",) * 4

    def stored(d):
        out = {n: jnp.transpose(d[n][0]) if n in _TRANSPOSED else d[n][0] for n in MATS}
        out["lora"] = jnp.concatenate([d[n][0] for n in LORAS], axis=0)
        return out

    ws, ms, vs = stored(w), stored(m), stored(v)
    lora_rows = [(0, 64), (64, 128), (128, 256)]
    chip = jnp.reshape(my_chip, (1,)).astype(jnp.int32)
    early = _swap_start(None, [_cast_into_slot(ws["w_in"], chip, _ADAM_TILE["w_in"], "cast_w_in"),
                               _with_own((N_CHIP,) + ws["lora"].shape, F32, ws["lora"], my_chip)], gather[:2],
                        "gather_first_start")
    lands = [_cast_into_slot(ws[n], chip, _ADAM_TILE[n], "cast_" + n, after=early[4]) for n in rest]
    ssem, rsem, srcs_thru, lands_thru, tok = _swap_start(None, lands, gather, "gather_rest_start")
    got = _swap_wait(early[0], early[1], early[2], early[3], tok, gather[:2], "gather_first_wait")
    win_all, lora_all = _swap_gathered(got, "gather_first_halves")
    win = _full("w_in", win_all)
    w2, a2, g2m = (_full(n, lora_all[:, a:b]) for n, (a, b) in zip(LORAS, lora_rows))

    vecs = {n: w[n].reshape(1, sz) for n, sz in VECS}
    vecs["mix_norm_g"] = vecs["mix_norm_g"] + tok[0, 0]

    def get_rest(after):
        halves = _swap_wait(ssem, rsem, srcs_thru, lands_thru, after, gather, "gather_rest_wait")
        return [_full(n, z) for n, z in zip(rest, _swap_gathered(halves, "gather_rest_halves"))]

    flight = []

    def my_half(g):
        h = g.shape[1] // 2
        return lax.dynamic_slice(g, (my_chip, ci * h, 0), (1, h, g.shape[2]))[0]

    def send_rest(gw):
        gs = [_by_chip(n, gw[n]) for n in rest]
        into = [_with_own((N_DEV,) + my_half(g).shape, BF16, my_half(g), my_dev) for g in gs]
        flight.extend(_swap_start(gs, into, scatter, "exchange_rest_start"))
        return flight[4]

    loss8, dx, gw, gv = _local_step(x[0], loss_target[0], win, vecs, w2, a2, g2m, get_rest, send_rest)

    core = jnp.reshape(ci, (1,)).astype(jnp.int32)
    gs = [_by_chip("w_in", gw["w_in"]),
          jnp.concatenate([_by_chip(n, gw[n]) for n in LORAS], axis=1).astype(BF16)]
    theirs = _sibling_halves(gs, "presum_halves")
    sums = [_add_halves(g, o, core, _SUM_TILE[n], "chipsum_" + n) for n, g, o in zip(first, gs, theirs)]
    own = [lax.dynamic_index_in_dim(s, my_chip, 0, keepdims=False) for s in sums]
    last = _swap_start(sums, [_with_own(s.shape, BF16, o, my_chip) for s, o in zip(sums, own)], ("chipsum",) * 2,
                       "exchange_first_start")
    small = _rowsum_small([gv[n] for n, _ in VECS], loss8 + last[4])
    vecs_out = _swap_start([small], [_with_own((N_DEV,) + small.shape, F32, small, my_dev)], ("all",),
                           "exchange_vectors_start")


    def update(group, rbufs, tag):
        sums = [_reduce8(rb, core, _SUM_TILE[n], "reduce_" + n) for n, rb in zip(group, rbufs)]
        gsum = _join_halves(sums, "join_halves_" + tag)
        out = {}
        for n, g in zip(group, gsum):
            r = _adamw_call(g, ws[n][None], ms[n][None], vs[n][None], _ADAM_TILE[n], "adamw_" + n)
            if n == "lora":
                for name, (a, b) in zip(LORAS, lora_rows):
                    out[name] = [z[:, a:b] for z in r]
            else:
                out[n] = [jnp.transpose(z[0])[None] for z in r] if n in _TRANSPOSED else r
        return out, r[1]

    res, done = update(rest, _swap_wait(flight[0], flight[1], flight[2], flight[3], vecs_out[4], scatter,
                                        "exchange_rest_wait"), "rest")
    got = _swap_wait(last[0], last[1], last[2], last[3], done, ("chipsum",) * 2, "exchange_first_wait")
    res_first, done = update(first, got, "first")
    res.update(res_first)
    sbuf = _swap_wait(vecs_out[0], vecs_out[1], vecs_out[2], vecs_out[3], done, ("all",), "exchange_vectors_wait")[0]
    rows = lambda d: [d[n].reshape(1, sz) for n, sz in VECS]
    small_res = _reduce_adamw_small(sbuf, rows(w), rows(m), rows(v))

    outs = []
    for k in range(4):
        piece = {n: r[k] for n, r in res.items()}
        for i, (n, _) in enumerate(VECS):
            piece[n] = small_res[k * len(VECS) + i].reshape(w[n].shape)
        outs.extend(piece[n] for n in names)
    return (small_res[-1][0, 0], dx[None], *outs)
```

```python
import jax
import jax.numpy as jnp
from jax import lax
from jax.experimental import pallas as pl
from jax.experimental.pallas import tpu as pltpu

F32 = jnp.float32
BF16 = jnp.bfloat16

D_MODEL = 1024
HEAD_DIM = 64
RW = 512
N_PAIR = RW // 128
SHIFT_COLS = 1792
IN_COLS = 3328
D_FF = 2816
FF_CHUNK = 256
NORM_EPS = 1e-6
GN_EPS = 64e-5
CHUNK = 64
SUB = 16
WKV_PASSES = 1
ATTN_PASSES = 1
ATTN_BLOCK = 128
DILATIONS = (1, 4, 16)
NEG = -1e30
ADAM_LR, ADAM_B1, ADAM_B2, ADAM_EPS, ADAM_WD, ADAM_STEP = 0.001, 0.9, 0.999, 1e-08, 0.01, 10
VMEM_LIMIT = 56 * 1024 * 1024
MESH = pl.DeviceIdType.MESH


def _params(sem=None, **kw):
    return pltpu.CompilerParams(dimension_semantics=sem, vmem_limit_bytes=VMEM_LIMIT, **kw)


def _dot(a, b):
    return lax.dot_general(a, b, (((1,), (0,)), ((), ())), preferred_element_type=F32)


def _dot_nt(a, b):
    return lax.dot_general(a, b, (((1,), (1,)), ((), ())), preferred_element_type=F32)


def _dot_tn(a, b):
    return lax.dot_general(a, b, (((0,), (0,)), ((), ())), preferred_element_type=F32)


_FORMS = {"nn": ((1,), (0,)), "nt": ((1,), (1,)), "tn": ((0,), (0,))}


def _dg(a, b, form):
    if a.ndim == 3 or b.ndim == 3:
        nb = a.shape[0] if a.ndim == 3 else b.shape[0]
        return jnp.stack([_dg(a[i] if a.ndim == 3 else a, b[i] if b.ndim == 3 else b, form) for i in range(nb)], axis=0)
    return lax.dot_general(a, b, (_FORMS[form], ((), ())), preferred_element_type=F32)


def _split2(x):
    hi = x.astype(BF16)
    return hi, (x - hi.astype(F32)).astype(BF16)


def _split3(x):
    hi = x.astype(BF16)
    rest = x - hi.astype(F32)
    mid = rest.astype(BF16)
    return hi, mid, (rest - mid.astype(F32)).astype(BF16)


def _mm_raw(a, b, form, mode):
    if mode == 1:
        return _dg(a.astype(BF16), b.astype(BF16), form)
    if mode == 3:
        ah, al = _split2(a)
        bh, bl = _split2(b)
        return _dg(ah, bh, form) + (_dg(ah, bl, form) + _dg(al, bh, form))
    if mode == "L3":
        ab = a.astype(BF16)
        b1, b2, b3 = _split3(b)
        if form == "nn":
            n = b.shape[-1]
            wide = _dg(ab, jnp.concatenate([b1, b2, b3], axis=-1), form)
            return wide[..., :n] + (wide[..., n:2 * n] + wide[..., 2 * n:])
        return _dg(ab, b1, form) + (_dg(ab, b2, form) + _dg(ab, b3, form))
    assert mode == "R3", mode
    bb = b.astype(BF16)
    a1, a2, a3 = _split3(a)
    if form in ("nn", "nt"):
        m = a.shape[-2]
        tall = _dg(jnp.concatenate([a1, a2, a3], axis=-2), bb, form)
        return tall[..., :m, :] + (tall[..., m:2 * m, :] + tall[..., 2 * m:, :])
    return _dg(a1, bb, form) + (_dg(a2, bb, form) + _dg(a3, bb, form))


def _mm(a, b, form, mode):
    @jax.custom_vjp
    def f(a, b):
        return _mm_raw(a, b, form, mode)

    def fwd(a, b):
        return _mm_raw(a, b, form, mode), (a, b)

    def bwd(res, ct):
        a, b = res
        la = {1: 1, 3: 3, "L3": None, "R3": "R3"}[mode]
        lb = {1: 1, 3: 3, "L3": "L3", "R3": None}[mode]
        if form == "nn":
            da = None if la is None else _mm_raw(ct, b, "nt", la)
            db = None if lb is None else _mm_raw(a, ct, "tn", lb)
        elif form == "nt":
            da = None if la is None else _mm_raw(ct, b, "nn", la)
            db = None if lb is None else _mm_raw(ct, a, "tn", "R3" if lb == "L3" else lb)
        else:
            da = None if la is None else _mm_raw(b, ct, "nt", "L3" if la == "R3" else la)
            db = None if lb is None else _mm_raw(a, ct, "nn", lb)
        return (jnp.zeros_like(a) if da is None else da, jnp.zeros_like(b) if db is None else db)

    f.defvjp(fwd, bwd)
    return f(a, b)


def _seg_ones(n):
    r = lax.broadcasted_iota(jnp.int32, (n, n), 0) // HEAD_DIM
    c = lax.broadcasted_iota(jnp.int32, (n, n), 1) // HEAD_DIM
    return (r == c).astype(F32)


def _segsum(x, seg):
    return _mm(x, seg, "nn", "R3")


def _rms_fwd(x, g):
    rstd = lax.rsqrt(jnp.mean(x * x, axis=-1, keepdims=True) + NORM_EPS)
    return x * rstd * g


def _rms_bwd(dy, x, g):
    rstd = lax.rsqrt(jnp.mean(x * x, axis=-1, keepdims=True) + NORM_EPS)
    xn = x * rstd
    dxn = dy * g
    dx = rstd * (dxn - xn * jnp.mean(dxn * xn, axis=-1, keepdims=True))
    return dx, dy * xn


def _sigmoid(x):
    return 1.0 / (1.0 + jnp.exp(-x))


def _softplus(x):
    return jnp.maximum(x, 0.0) + jnp.log(1.0 + jnp.exp(-jnp.abs(x)))


def _acc(ref, val, first):
    @pl.when(first)
    def _():
        ref[...] = val

    @pl.when(jnp.logical_not(first))
    def _():
        ref[...] += val


def _colsum8(v):
    rows, n = v.shape
    return jnp.sum(v.reshape(rows // 8, 8, n), axis=0)


def _prep_fn(p, pprev, mu, w0, w2p, a0, a2p, g2, k_k, k_a):
    seg = _seg_ones(RW)
    ps = p + (pprev - p) * mu
    r = ps[:, 0:RW]
    k = ps[:, RW:2 * RW]
    v = ps[:, 2 * RW:3 * RW]
    xwa = ps[:, 3 * RW:3 * RW + 128]
    xg = ps[:, 3 * RW + 128:3 * RW + 256]
    wraw = -_softplus(-(w0 + _mm(jnp.tanh(xwa), w2p, "nn", 3))) - 0.5
    lw = -jnp.exp(wraw)
    a = _sigmoid(a0 + _mm(xwa, a2p, "nn", 3))
    g = _mm(_sigmoid(xg), g2, "nn", 3)
    kk = k * k_k
    kk = kk / jnp.maximum(jnp.sqrt(_segsum(kk * kk, seg)), 1e-12)
    k2 = k * (1.0 + (a - 1.0) * k_a)
    return r, lw, k2, v, kk, a, g


def _transposed(z):
    return jnp.stack([z[i].T for i in range(z.shape[0])], axis=0) if z.ndim == 3 else z.T


def _solve_unit_lower(lmat, rhs):
    c = lmat.shape[-1]
    row = lax.broadcasted_iota(jnp.int32, (c, c), 0)
    col = lax.broadcasted_iota(jnp.int32, (c, c), 1)
    eye = (row == col).astype(F32)
    ld = jnp.where(row // SUB == col // SUB, lmat, 0.0)
    lo = lmat - ld
    x = eye + ld
    m = ld
    mm = lambda p, q: _mm(p, q, "nn", WKV_PASSES)
    cat = jnp.concatenate
    m = mm(m, m)
    for _ in range(2):
        mx = mm(m, cat([m, x], axis=-1))
        m, x = mx[..., :c], x + mx[..., c:]
    x = x + mm(m, x)
    gw = mm(x, cat([lo, rhs], axis=-1))
    g, w = gw[..., :c], gw[..., c:]
    gg = mm(g, cat([g, w], axis=-1))
    w = w + gg[..., c:]
    return w + mm(gg[..., :c], w)


def _wkv_chunk_fn(s0, r, lw, k, v, kk, a):
    c = r.shape[-2]
    n = 2 * c
    row = lax.broadcasted_iota(jnp.int32, (n, n), 0)
    col = lax.broadcasted_iota(jnp.int32, (n, n), 1)
    same = (row // c) == (col // c)
    incl = jnp.logical_and(row >= col, same)
    strict = jnp.logical_and(row > col, same)
    sel = (lax.broadcasted_iota(jnp.int32, (n, 128), 0) // c) == (lax.broadcasted_iota(jnp.int32, (n, 128), 1) // HEAD_DIM)
    two = lambda z: jnp.concatenate([z, z], axis=-2)
    lw2 = two(lw)
    mm = lambda p_, q_, form: _mm(p_, q_, form, WKV_PASSES)
    cl = _mm(incl.astype(F32), lw2, "nn", "L3")
    p = jnp.exp(cl)
    pinv = jnp.exp(-cl)
    pprev = jnp.exp(cl - lw2)
    kk2 = two(kk)
    at = jnp.where(sel, -kk2 * pprev, 0.0)
    bt = jnp.where(sel, kk2 * two(a) * pinv, 0.0)
    kt = jnp.where(sel, two(k) * pinv, 0.0)
    rt = jnp.where(sel, two(r) * p, 0.0)
    vt = jnp.where(sel, two(v), 0.0)
    cat = jnp.concatenate
    bk = cat([bt, kt], axis=-2)
    arbk = mm(cat([at, rt], axis=-2), bk, "nt")
    ab, ak = jnp.where(strict, arbk[..., :n, :n], 0.0), jnp.where(strict, arbk[..., :n, n:], 0.0)
    rb, rk = jnp.where(incl, arbk[..., n:, :n], 0.0), jnp.where(incl, arbk[..., n:, n:], 0.0)
    s0t = _transposed(s0)
    u = _solve_unit_lower(ab, mm(cat([at, ak], axis=-1), cat([s0t, vt], axis=-2), "nn"))
    y2 = mm(cat([rt, rb, rk], axis=-1), cat([s0t, u, vt], axis=-2), "nn")
    plast = jnp.exp(jnp.sum(lw, axis=-2, keepdims=True))
    s1 = (s0 + mm(cat([u, vt], axis=-2), bk, "tn")) * plast
    r2 = lax.broadcasted_iota(jnp.int32, (128, 128), 0) // HEAD_DIM
    c2 = lax.broadcasted_iota(jnp.int32, (128, 128), 1) // HEAD_DIM
    return y2[..., :c, :] + y2[..., c:, :], jnp.where(r2 == c2, s1, 0.0)


def _post_fn(y, r, k2, v, g, lnw, lnb, rk):
    seg = _seg_ones(RW)
    mean = _segsum(y, seg) * (1.0 / HEAD_DIM)
    yc = y - mean
    var = _segsum(yc * yc, seg) * (1.0 / HEAD_DIM)
    yn = yc * lax.rsqrt(var + GN_EPS)
    out = yn * lnw + lnb + _segsum(r * k2 * rk, seg) * v
    return out * g


def _attn_block_fn(q, kc, vc, kp=None, vp=None):
    n = ATTN_BLOCK
    qi = lax.broadcasted_iota(jnp.int32, (n, n), 0)
    kj = lax.broadcasted_iota(jnp.int32, (n, n), 1)
    lane = lax.broadcasted_iota(jnp.int32, (1, 128), 1)
    scale = HEAD_DIM ** -0.5
    valid = kj <= qi
    keys, vals = kc, vc
    if kp is not None:
        valid = jnp.concatenate([valid, kj >= qi], axis=-1)
        keys, vals = jnp.concatenate([kc, kp], axis=-2), jnp.concatenate([vc, vp], axis=-2)
    m0 = (lane // HEAD_DIM) == 0
    q2 = jnp.concatenate([jnp.where(m0, q, 0.0), jnp.where(m0, 0.0, q)], axis=-2)
    valid2 = jnp.concatenate([valid, valid], axis=-2)
    s = jnp.where(valid2, _mm(q2, keys, "nt", ATTN_PASSES) * scale, NEG)
    m = jnp.max(s, axis=-1, keepdims=True)
    p = jnp.exp(s - m)
    den = jnp.sum(p, axis=-1, keepdims=True)
    o2 = _mm(p, vals, "nn", ATTN_PASSES) / den
    l2 = m + jnp.log(den)
    return jnp.where(m0, o2[..., :n, :], o2[..., n:, :]), jnp.where(m0, l2[..., :n, :], l2[..., n:, :])


def _attn_block_bwd(q, kc, vc, kp, vp, o, lse, do, dl):
    n = ATTN_BLOCK
    cat = jnp.concatenate
    qi = lax.broadcasted_iota(jnp.int32, (n, n), 0)
    kj = lax.broadcasted_iota(jnp.int32, (n, n), 1)
    m0 = (lax.broadcasted_iota(jnp.int32, (1, 128), 1) // HEAD_DIM) == 0
    scale = HEAD_DIM ** -0.5
    valid = kj <= qi
    keys, vals = kc, vc
    if kp is not None:
        valid = cat([valid, kj >= qi], axis=-1)
        keys, vals = cat([kc, kp], axis=-2), cat([vc, vp], axis=-2)
    stack = lambda z: cat([jnp.where(m0, z, 0.0), jnp.where(m0, 0.0, z)], axis=-2)
    q2, do2 = stack(q), stack(do)
    lse2 = cat([jnp.max(jnp.where(m0, lse, NEG), axis=-1, keepdims=True),
                jnp.max(jnp.where(m0, NEG, lse), axis=-1, keepdims=True)], axis=-2)
    delta = jnp.sum(do2 * cat([o, o], axis=-2), axis=-1, keepdims=True)
    dlse = jnp.sum(stack(dl), axis=-1, keepdims=True)
    mm = lambda a, b, form: _mm_raw(a, b, form, ATTN_PASSES)
    s = jnp.where(cat([valid, valid], axis=-2), mm(q2, keys, "nt") * scale, NEG)
    p = jnp.exp(s - lse2)
    ds = p * (mm(do2, vals, "nt") - delta + dlse)
    dq2 = mm(ds, keys, "nn") * scale
    dq = jnp.where(m0, dq2[..., :n, :], dq2[..., n:, :])
    dkeys = mm(ds, q2, "tn") * scale
    dvals = mm(p, do2, "tn")
    if kp is None:
        return dq, dkeys, dvals
    return dq, dkeys[..., :n, :], dvals[..., :n, :], dkeys[..., n:, :], dvals[..., n:, :]


def _combine_fn(o1, o2, o3, l1, l2, l3, og):
    seg = _seg_ones(o1.shape[-1])
    m = jnp.maximum(jnp.maximum(l1, l2), l3)
    e1, e2, e3 = jnp.exp(l1 - m), jnp.exp(l2 - m), jnp.exp(l3 - m)
    o = (e1 * o1 + e2 * o2 + e3 * o3) / (e1 + e2 + e3)
    o = o * lax.rsqrt(_segsum(o * o, seg) * (1.0 / HEAD_DIM) + NORM_EPS)
    return o * og


def _shifted(p, last8, first):
    prow = jnp.where(first, 0.0, last8[7:8, :])
    rolled = pltpu.roll(p, 1, axis=0)
    rid = lax.broadcasted_iota(jnp.int32, p.shape, 0)
    return jnp.where(rid == 0, prow, rolled)


_PREP_TM = 256


def _prep_specs(tm):
    vec = lambda n: pl.BlockSpec((1, n), lambda i: (0, 0))
    mat = lambda r, n: pl.BlockSpec((r, n), lambda i: (0, 0))
    return [vec(SHIFT_COLS), vec(RW), mat(128, RW), vec(RW), mat(128, RW), mat(128, RW), vec(RW), vec(RW)]


def _in_proj_prep(x, g1, win, pw):
    t = x.shape[0]
    tm = _PREP_TM

    def body(x_ref, g_ref, w_ref, mu, w0, w2p, a0, a2p, g2, k_k, k_a, h_ref, pa_ref, qkv_ref, *rest):
        outs, carry = rest[:7], rest[7]

        @pl.when(pl.program_id(0) == 0)
        def _():
            carry[...] = jnp.zeros_like(carry)

        h = _rms_fwd(x_ref[...], g_ref[...]).astype(BF16)
        h_ref[...] = h
        proj = _dot_nt(h, w_ref[...])
        p = proj[:, :SHIFT_COLS]
        pa_ref[...] = p
        for j in range(3):
            for pr in range(N_PAIR):
                c0 = SHIFT_COLS + j * RW + pr * 128
                qkv_ref[j, pr] = proj[:, c0:c0 + 128]
        pprev = _shifted(p, carry[...], pl.program_id(0) == 0)
        carry[...] = p[tm - 8:, :]
        res = _prep_fn(p, pprev, mu[...], w0[...], w2p[...], a0[...], a2p[...], g2[...], k_k[...], k_a[...])
        for o_ref, val in zip(outs, res):
            o_ref[...] = val

    row = pl.BlockSpec((tm, RW), lambda i: (i, 0))
    return pl.pallas_call(
        body, name="in_proj_prep", grid=(t // tm,),
        in_specs=[pl.BlockSpec((tm, D_MODEL), lambda i: (i, 0)), pl.BlockSpec((1, D_MODEL), lambda i: (0, 0)),
                  pl.BlockSpec((IN_COLS, D_MODEL), lambda i: (0, 0))] + _prep_specs(tm),
        out_specs=[pl.BlockSpec((tm, D_MODEL), lambda i: (i, 0)), pl.BlockSpec((tm, SHIFT_COLS), lambda i: (i, 0)),
                   pl.BlockSpec((3, N_PAIR, tm, 128), lambda i: (0, 0, i, 0))] + [row] * 7,
        out_shape=[jax.ShapeDtypeStruct((t, D_MODEL), BF16), jax.ShapeDtypeStruct((t, SHIFT_COLS), F32),
                   jax.ShapeDtypeStruct((3, N_PAIR, t, 128), F32)] + [jax.ShapeDtypeStruct((t, RW), F32)] * 7,
        scratch_shapes=[pltpu.VMEM((8, SHIFT_COLS), F32)],
        compiler_params=_params(("arbitrary",)),
    )(x, g1, win, *pw)


def _pairs(ref):
    return jnp.stack([ref[:, 128 * p:128 * (p + 1)] for p in range(N_PAIR)], axis=0)


def _wkv_fwd(r, lw, k2, v, kk, a):
    t = r.shape[0]
    nc = t // CHUNK

    def body(r_ref, lw_ref, k_ref, v_ref, kk_ref, a_ref, y_ref, s_ref, st):
        @pl.when(pl.program_id(0) == 0)
        def _():
            st[...] = jnp.zeros_like(st)

        s0 = st[...]
        s_ref[0] = s0
        y, s1 = _wkv_chunk_fn(s0, *[_pairs(ref) for ref in (r_ref, lw_ref, k_ref, v_ref, kk_ref, a_ref)])
        for p in range(N_PAIR):
            y_ref[:, 128 * p:128 * (p + 1)] = y[p]
        st[...] = s1

    blk = pl.BlockSpec((CHUNK, RW), lambda c: (c, 0))
    return pl.pallas_call(
        body, name="wkv_fwd", grid=(nc,),
        in_specs=[blk] * 6,
        out_specs=[blk, pl.BlockSpec((1, N_PAIR, 128, 128), lambda c: (c, 0, 0, 0))],
        out_shape=[jax.ShapeDtypeStruct((t, RW), F32), jax.ShapeDtypeStruct((nc, N_PAIR, 128, 128), F32)],
        scratch_shapes=[pltpu.VMEM((N_PAIR, 128, 128), F32)],
        compiler_params=_params(("arbitrary",)),
    )(r, lw, k2, v, kk, a)


_POST_TM = 512


ATTN_GROUP = 2


def _dilated_rows(d, r, n):
    if d == 1:
        return pl.ds(pl.multiple_of(n * ATTN_BLOCK, ATTN_BLOCK), ATTN_BLOCK)
    return pl.ds(r + n * (ATTN_BLOCK * d), ATTN_BLOCK, stride=d)


def _for_each_sequence(t, unit):
    for di, d in enumerate(DILATIONS):

        @pl.when(pl.program_id(1) == di)
        def _(di=di, d=d):
            nb = t // (ATTN_BLOCK * d)
            if d == 1:
                unit(di, [(d, 0, 0)], False)
                unit(di, [(d, 0, 1)], True)
                lax.fori_loop(1, nb // 2, lambda k, c: (unit(di, [(d, 0, 2 * k), (d, 0, 2 * k + 1)], True), c)[1], 0)
            else:

                def residues(r, carry):
                    unit(di, [(d, r, 0), (d, r + d // 2, 0)], False)
                    if nb > 1:
                        lax.fori_loop(1, nb, lambda n, c: (unit(di, [(d, r, n), (d, r + d // 2, n)], True), c)[1], 0)
                    return carry

                lax.fori_loop(0, d // 2, residues, 0)


def _take(ref, lead, rows_list):
    return jnp.stack([ref.at[(*lead, g)][rows, :] for rows in rows_list for g in range(ref.shape[len(lead)])], axis=0)


def _put(ref, lead, rows_list, val, add=False):
    k = 0
    for rows in rows_list:
        for g in range(ref.shape[len(lead)]):
            if add:
                ref.at[(*lead, g)][rows, :] += val[k]
            else:
                ref.at[(*lead, g)][rows, :] = val[k]
            k += 1


def _attn_fwd(qkv):
    t = qkv.shape[2]

    def body(q_ref, k_ref, v_ref, o_ref, l_ref):
        def unit(di, places, has_prev):
            cur = [_dilated_rows(d, r, n) for d, r, n in places]
            args = [_take(ref, (0,), cur) for ref in (q_ref, k_ref, v_ref)]
            if has_prev:
                prv = [_dilated_rows(d, r, n - 1) for d, r, n in places]
                args += [_take(ref, (0,), prv) for ref in (k_ref, v_ref)]
            o, lse = _attn_block_fn(*args)
            _put(o_ref, (0,), cur, o)
            _put(l_ref, (0,), cur, lse)

        _for_each_sequence(t, unit)

    spec = lambda j: pl.BlockSpec((1, ATTN_GROUP, t, 128), lambda i, b: (j, i, 0, 0))
    out = pl.BlockSpec((1, ATTN_GROUP, t, 128), lambda i, b: (b, i, 0, 0))
    return pl.pallas_call(
        body, name="attn_fwd", grid=(N_PAIR // ATTN_GROUP, len(DILATIONS)),
        in_specs=[spec(0), spec(1), spec(2)], out_specs=[out, out],
        out_shape=[jax.ShapeDtypeStruct((3, N_PAIR, t, 128), F32)] * 2,
        compiler_params=_params(("parallel", "arbitrary")),
    )(qkv, qkv, qkv)


_COMB_TM = 512


def _mixers_out(y, r, k2, v, g, lnw, lnb, rk, o, l, og):
    t = y.shape[0]
    tm = _COMB_TM

    def body(y_ref, r_ref, k_ref, v_ref, g_ref, lnw_ref, lnb_ref, rk_ref, o_ref, l_ref, og_ref, out_ref):
        out_ref[:, :RW] = _post_fn(y_ref[...], r_ref[...], k_ref[...], v_ref[...], g_ref[...],
                                   lnw_ref[...], lnb_ref[...], rk_ref[...]).astype(BF16)
        for p in range(N_PAIR):
            cols = slice(128 * p, 128 * (p + 1))
            out_ref[:, RW + 128 * p:RW + 128 * (p + 1)] = _combine_fn(
                o_ref[0, p], o_ref[1, p], o_ref[2, p], l_ref[0, p], l_ref[1, p], l_ref[2, p], og_ref[:, cols]).astype(BF16)

    row = pl.BlockSpec((tm, RW), lambda i: (i, 0))
    vec = pl.BlockSpec((1, RW), lambda i: (0, 0))
    blk = pl.BlockSpec((3, N_PAIR, tm, 128), lambda i: (0, 0, i, 0))
    return pl.pallas_call(
        body, name="mixers_out", grid=(t // tm,),
        in_specs=[row] * 5 + [vec] * 3 + [blk, blk, vec], out_specs=pl.BlockSpec((tm, D_MODEL), lambda i: (i, 0)),
        out_shape=jax.ShapeDtypeStruct((t, D_MODEL), BF16),
        compiler_params=_params(("parallel",)),
    )(y, r, k2, v, g, lnw, lnb, rk, o, l, og)


def _ffn_all(x, ycat, wg, wu, wd, wout, g2, gf, tgt):
    t = x.shape[0]
    tm = 256

    def body(x_ref, y_ref, wg_ref, wu_ref, wd_ref, wo_ref, g2_ref, gf_ref, t_ref,
             h_ref, act_ref, dx2b_ref, dgt_ref, dup_ref, dx1b_ref, dx1_ref, dya_ref, dyb_ref, loss_ref, dgf_ref, dg2_ref,
             gt_s, up_s):
        first = pl.program_id(0) == 0
        x1 = x_ref[...] + _dot(y_ref[...], wo_ref[...])
        h = _rms_fwd(x1, g2_ref[...]).astype(BF16)
        h_ref[...] = h
        for c0 in range(0, D_FF, FF_CHUNK):
            cols = slice(c0, c0 + FF_CHUNK)
            gt = _dot_nt(h, wg_ref[cols, :])
            up = _dot_nt(h, wu_ref[cols, :])
            gt_s[:, cols] = gt.astype(BF16)
            up_s[:, cols] = up.astype(BF16)
            act_ref[:, cols] = (gt * _sigmoid(gt) * up).astype(BF16)
        x2 = x1 + _dot(act_ref[...], wd_ref[...])
        gf_ = gf_ref[...]
        diff = _rms_fwd(x2, gf_) - t_ref[...]
        lrow = 0.5 * jnp.sum(_colsum8(diff * diff), axis=1, keepdims=True) * (1.0 / D_MODEL)
        _acc(loss_ref, jnp.broadcast_to(lrow, (8, 128)), first)
        dx2, dgr = _rms_bwd(diff * (1.0 / D_MODEL), x2, gf_)
        _acc(dgf_ref, _colsum8(dgr), first)
        dx2b = dx2.astype(BF16)
        dx2b_ref[...] = dx2b
        for c0 in range(0, D_FF, FF_CHUNK):
            cols = slice(c0, c0 + FF_CHUNK)
            dact = _dot_nt(dx2b, wd_ref[cols, :])
            gt = gt_s[:, cols].astype(F32)
            sg = _sigmoid(gt)
            dgt_ref[:, cols] = (dact * up_s[:, cols].astype(F32) * sg * (1.0 + gt * (1.0 - sg))).astype(BF16)
            dup_ref[:, cols] = (dact * gt * sg).astype(BF16)
        dh = _dot(dgt_ref[...], wg_ref[...]) + _dot(dup_ref[...], wu_ref[...])
        dxn, dgr2 = _rms_bwd(dh, x1, g2_ref[...])
        _acc(dg2_ref, _colsum8(dgr2), first)
        dx1 = dx2 + dxn
        dx1_ref[...] = dx1
        dx1b = dx1.astype(BF16)
        dx1b_ref[...] = dx1b
        dy = _dot_nt(dx1b, wo_ref[...])
        dya_ref[...] = dy[:, :RW]
        dyb_ref[...] = dy[:, RW:]

    row = pl.BlockSpec((tm, D_MODEL), lambda i: (i, 0))
    wide = pl.BlockSpec((tm, D_FF), lambda i: (i, 0))
    half = pl.BlockSpec((tm, RW), lambda i: (i, 0))
    wsp = pl.BlockSpec((D_FF, D_MODEL), lambda i: (0, 0))
    vec = pl.BlockSpec((1, D_MODEL), lambda i: (0, 0))
    part = pl.BlockSpec((8, D_MODEL), lambda i: (0, 0))
    bf = lambda n: jax.ShapeDtypeStruct((t, n), BF16)
    return pl.pallas_call(
        body, name="ffn_all", grid=(t // tm,),
        in_specs=[row, row, wsp, wsp, wsp, pl.BlockSpec((D_MODEL, D_MODEL), lambda i: (0, 0)), vec, vec, row],
        out_specs=[row, wide, row, wide, wide, row, row, half, half, pl.BlockSpec((8, 128), lambda i: (0, 0)), part, part],
        out_shape=[bf(D_MODEL), bf(D_FF), bf(D_MODEL), bf(D_FF), bf(D_FF), bf(D_MODEL),
                   jax.ShapeDtypeStruct((t, D_MODEL), F32), jax.ShapeDtypeStruct((t, RW), F32),
                   jax.ShapeDtypeStruct((t, RW), F32), jax.ShapeDtypeStruct((8, 128), F32),
                   jax.ShapeDtypeStruct((8, D_MODEL), F32), jax.ShapeDtypeStruct((8, D_MODEL), F32)],
        scratch_shapes=[pltpu.VMEM((tm, D_FF), BF16), pltpu.VMEM((tm, D_FF), BF16)],
        compiler_params=_params(("arbitrary",)),
    )(x, ycat, wg, wu, wd, wout, g2, gf, tgt)


def _wgrad(a, b, tk, tn, name):
    t, kdim = a.shape
    ndim = b.shape[1]

    def body(a_ref, b_ref, o_ref):
        o_ref[...] = _dot_tn(a_ref[...], b_ref[...]).astype(BF16)

    return pl.pallas_call(
        body, name=name, grid=(kdim // tk, ndim // tn),
        in_specs=[pl.BlockSpec((t, tk), lambda i, j: (0, i)), pl.BlockSpec((t, tn), lambda i, j: (0, j))],
        out_specs=pl.BlockSpec((tk, tn), lambda i, j: (i, j)),
        out_shape=jax.ShapeDtypeStruct((kdim, ndim), BF16),
        compiler_params=_params(("parallel", "parallel")),
    )(a, b)


def _post_bwd(dya, y, r, k2, v, g, lnw, lnb, rk):
    t = y.shape[0]
    tm = _POST_TM

    def body(d_ref, y_ref, r_ref, k_ref, v_ref, g_ref, lnw_ref, lnb_ref, rk_ref,
             dy_ref, dr_ref, dk_ref, dv_ref, dg_ref, dlnw_ref, dlnb_ref, drk_ref):
        first = pl.program_id(0) == 0
        ones = jnp.ones((tm, 1), F32)
        prim = (y_ref[...], r_ref[...], k_ref[...], v_ref[...], g_ref[...],
                ones * lnw_ref[...], ones * lnb_ref[...], ones * rk_ref[...])
        _, vjp = jax.vjp(_post_fn, *prim)
        dy, dr, dk, dv, dg, dlnw, dlnb, drk = vjp(d_ref[...])
        dy_ref[...] = dy
        dr_ref[...] = dr
        dk_ref[...] = dk
        dv_ref[...] = dv
        dg_ref[...] = dg
        _acc(dlnw_ref, _colsum8(dlnw), first)
        _acc(dlnb_ref, _colsum8(dlnb), first)
        _acc(drk_ref, _colsum8(drk), first)

    row = pl.BlockSpec((tm, RW), lambda i: (i, 0))
    vec = pl.BlockSpec((1, RW), lambda i: (0, 0))
    part = pl.BlockSpec((8, RW), lambda i: (0, 0))
    return pl.pallas_call(
        body, name="rwkv_post_bwd", grid=(t // tm,),
        in_specs=[row] * 6 + [vec] * 3, out_specs=[row] * 5 + [part] * 3,
        out_shape=[jax.ShapeDtypeStruct((t, RW), F32)] * 5 + [jax.ShapeDtypeStruct((8, RW), F32)] * 3,
        compiler_params=_params(("arbitrary",)),
    )(dya, y, r, k2, v, g, lnw, lnb, rk)


def _wkv_bwd(dy, s0s, r, lw, k2, v, kk, a):
    t = r.shape[0]
    nc = t // CHUNK

    def body(dy_ref, s_ref, r_ref, lw_ref, k_ref, v_ref, kk_ref, a_ref,
             dr_ref, dlw_ref, dk_ref, dv_ref, dkk_ref, da_ref, ds):
        @pl.when(pl.program_id(0) == 0)
        def _():
            ds[...] = jnp.zeros_like(ds)

        _, vjp = jax.vjp(_wkv_chunk_fn, s_ref[0],
                         *[_pairs(ref) for ref in (r_ref, lw_ref, k_ref, v_ref, kk_ref, a_ref)])
        res = vjp((_pairs(dy_ref), ds[...]))
        ds[...] = res[0]
        for ref, val in zip((dr_ref, dlw_ref, dk_ref, dv_ref, dkk_ref, da_ref), res[1:]):
            for p in range(N_PAIR):
                ref[:, 128 * p:128 * (p + 1)] = val[p]

    blk = pl.BlockSpec((CHUNK, RW), lambda c: (nc - 1 - c, 0))
    return pl.pallas_call(
        body, name="wkv_bwd", grid=(nc,),
        in_specs=[blk, pl.BlockSpec((1, N_PAIR, 128, 128), lambda c: (nc - 1 - c, 0, 0, 0))] + [blk] * 6,
        out_specs=[blk] * 6,
        out_shape=[jax.ShapeDtypeStruct((t, RW), F32)] * 6,
        scratch_shapes=[pltpu.VMEM((N_PAIR, 128, 128), F32)],
        compiler_params=_params(("arbitrary",)),
    )(dy, s0s, r, lw, k2, v, kk, a)


def _prep_in_proj_bwd(proj, pw, douts, dq, dk, dv, win, x, g1, dx1):
    t = proj.shape[0]
    tm = _PREP_TM
    nt = t // tm

    def body(p_ref, l8_ref, mu, w0, w2p, a0, a2p, g2, k_k, k_a, dr, dr2, dlw, dk2, dk22, dv, dv2, dkk, da, dg,
             dq_ref, dkq_ref, dvq_ref, w_ref, x_ref, g1_ref, dx1_ref,
             dproj_ref, dx_ref, dg1_ref, dmu_ref, dw0_ref, dw2_ref, da0_ref, da2_ref, dg2_ref, dkk_ref, dka_ref, carry):
        i = pl.program_id(0)
        first = i == 0

        @pl.when(first)
        def _():
            carry[...] = jnp.zeros_like(carry)

        p = p_ref[...]
        pprev = _shifted(p, l8_ref[...], i == nt - 1)
        ones = jnp.ones((tm, 1), F32)
        prim = (p, pprev, ones * mu[...], ones * w0[...], w2p[...], ones * a0[...], a2p[...], g2[...],
                ones * k_k[...], ones * k_a[...])
        _, vjp = jax.vjp(_prep_fn, *prim)
        dp, dpp, dmu, dw0, dw2, da0, da2, dg2, dkk_, dka = vjp(
            (dr[...] + dr2[...], dlw[...], dk2[...] + dk22[...], dv[...] + dv2[...], dkk[...], da[...], dg[...]))
        up = pltpu.roll(dpp, tm - 1, axis=0)
        rid = lax.broadcasted_iota(jnp.int32, dpp.shape, 0)
        dpa = dp + jnp.where(rid == tm - 1, carry[0:1, :], up)
        carry[...] = jnp.broadcast_to(dpp[0:1, :], carry.shape)
        _acc(dmu_ref, _colsum8(dmu), first)
        _acc(dw0_ref, _colsum8(dw0), first)
        _acc(dw2_ref, dw2, first)
        _acc(da0_ref, _colsum8(da0), first)
        _acc(da2_ref, da2, first)
        _acc(dg2_ref, dg2, first)
        _acc(dkk_ref, _colsum8(dkk_), first)
        _acc(dka_ref, _colsum8(dka), first)
        parts = [dpa] + [ref[pr] for ref in (dq_ref, dkq_ref, dvq_ref) for pr in range(N_PAIR)]
        dproj = jnp.concatenate([z.astype(BF16) for z in parts], axis=1)
        dproj_ref[...] = dproj
        dxn, dgr = _rms_bwd(_dot(dproj, w_ref[...]), x_ref[...], g1_ref[...])
        dx_ref[...] = dx1_ref[...] + dxn
        _acc(dg1_ref, _colsum8(dgr), first)

    rev = lambda i: (nt - 1 - i, 0)
    row = pl.BlockSpec((tm, RW), rev)
    wide = pl.BlockSpec((tm, D_MODEL), rev)
    pair = pl.BlockSpec((N_PAIR, tm, 128), lambda i: (0, nt - 1 - i, 0))
    part = lambda n: pl.BlockSpec((8, n), lambda i: (0, 0))
    mat = pl.BlockSpec((128, RW), lambda i: (0, 0))
    return pl.pallas_call(
        body, name="prep_in_proj_bwd", grid=(nt,),
        in_specs=[pl.BlockSpec((tm, SHIFT_COLS), rev),
                  pl.BlockSpec((8, SHIFT_COLS), lambda i: (jnp.maximum((nt - 1 - i) * (tm // 8) - 1, 0), 0))]
                 + _prep_specs(tm) + [row] * 10
                 + [pair] * 3 + [pl.BlockSpec((IN_COLS, D_MODEL), lambda i: (0, 0)), wide,
                                 pl.BlockSpec((1, D_MODEL), lambda i: (0, 0)), wide],
        out_specs=[pl.BlockSpec((tm, IN_COLS), rev), wide, part(D_MODEL), part(SHIFT_COLS), part(RW), mat, part(RW), mat,
                   mat, part(RW), part(RW)],
        out_shape=[jax.ShapeDtypeStruct((t, IN_COLS), BF16), jax.ShapeDtypeStruct((t, D_MODEL), F32),
                   jax.ShapeDtypeStruct((8, D_MODEL), F32), jax.ShapeDtypeStruct((8, SHIFT_COLS), F32),
                   jax.ShapeDtypeStruct((8, RW), F32), jax.ShapeDtypeStruct((128, RW), F32),
                   jax.ShapeDtypeStruct((8, RW), F32), jax.ShapeDtypeStruct((128, RW), F32),
                   jax.ShapeDtypeStruct((128, RW), F32), jax.ShapeDtypeStruct((8, RW), F32),
                   jax.ShapeDtypeStruct((8, RW), F32)],
        scratch_shapes=[pltpu.VMEM((8, SHIFT_COLS), F32)],
        compiler_params=_params(("arbitrary",)),
    )(proj, proj, *pw, *douts, dq, dk, dv, win, x, g1, dx1)


def _combine_bwd(dyb, o, l, og):
    t = dyb.shape[0]
    tm = _COMB_TM

    def body(d_ref, o_ref, l_ref, og_ref, do_ref, dl_ref, dog_ref):
        ones = jnp.ones((tm, 1), F32)
        dog = []
        for p in range(N_PAIR):
            cols = slice(128 * p, 128 * (p + 1))
            _, vjp = jax.vjp(_combine_fn, o_ref[0, p], o_ref[1, p], o_ref[2, p], l_ref[0, p], l_ref[1, p], l_ref[2, p],
                             ones * og_ref[:, cols])
            res = vjp(d_ref[:, cols])
            for b in range(3):
                do_ref[b, p] = res[b]
                dl_ref[b, p] = res[3 + b]
            dog.append(_colsum8(res[6]))
        _acc(dog_ref, jnp.concatenate(dog, axis=1), pl.program_id(0) == 0)

    blk = pl.BlockSpec((3, N_PAIR, tm, 128), lambda i: (0, 0, i, 0))
    return pl.pallas_call(
        body, name="attn_combine_bwd", grid=(t // tm,),
        in_specs=[pl.BlockSpec((tm, RW), lambda i: (i, 0)), blk, blk, pl.BlockSpec((1, RW), lambda i: (0, 0))],
        out_specs=[blk, blk, pl.BlockSpec((8, RW), lambda i: (0, 0))],
        out_shape=[jax.ShapeDtypeStruct((3, N_PAIR, t, 128), F32)] * 2 + [jax.ShapeDtypeStruct((8, RW), F32)],
        compiler_params=_params(("arbitrary",)),
    )(dyb, o, l, og)


def _attn_bwd(do, dl, o, lse, qkv):
    t = qkv.shape[2]

    def body(do_ref, dl_ref, o_ref, l_ref, q_ref, k_ref, v_ref, dq_ref, dk_ref, dv_ref):
        @pl.when(pl.program_id(1) == 0)
        def _():
            for ref in (dq_ref, dk_ref, dv_ref):
                ref[...] = jnp.zeros_like(ref)

        def unit(di, places, has_prev):
            cur = [_dilated_rows(d, r, n) for d, r, n in places]
            q, kc, vc = [_take(ref, (0,), cur) for ref in (q_ref, k_ref, v_ref)]
            kp = vp = None
            if has_prev:
                prv = [_dilated_rows(d, r, n - 1) for d, r, n in places]
                kp, vp = [_take(ref, (0,), prv) for ref in (k_ref, v_ref)]
            res = _attn_block_bwd(q, kc, vc, kp, vp, *[_take(ref, (0,), cur) for ref in (o_ref, l_ref, do_ref, dl_ref)])
            _put(dq_ref, (), cur, res[0], add=True)
            _put(dk_ref, (), cur, res[1], add=True)
            _put(dv_ref, (), cur, res[2], add=True)
            if has_prev:
                _put(dk_ref, (), prv, res[3], add=True)
                _put(dv_ref, (), prv, res[4], add=True)

        _for_each_sequence(t, unit)

    spec = lambda j: pl.BlockSpec((1, ATTN_GROUP, t, 128), lambda i, b: (j, i, 0, 0))
    branch = pl.BlockSpec((1, ATTN_GROUP, t, 128), lambda i, b: (b, i, 0, 0))
    out = pl.BlockSpec((ATTN_GROUP, t, 128), lambda i, b: (i, 0, 0))
    return pl.pallas_call(
        body, name="attn_bwd", grid=(N_PAIR // ATTN_GROUP, len(DILATIONS)),
        in_specs=[branch] * 4 + [spec(0), spec(1), spec(2)], out_specs=[out] * 3,
        out_shape=[jax.ShapeDtypeStruct((N_PAIR, t, 128), F32)] * 3,
        compiler_params=_params(("parallel", "arbitrary")),
    )(do, dl, o, lse, qkv, qkv, qkv)


def _pad_lora(w, lo):
    z = jnp.zeros((64, RW), F32)
    return jnp.concatenate([w, z], axis=0) if lo == 0 else jnp.concatenate([z, w], axis=0)


def _local_step(x, tgt, win, vecs, w2, a2, g2m, rest_arriving, get_rest, send_rest):
    pw = (vecs["mu_shift"], vecs["decay_w0"], _pad_lora(w2, 0), vecs["iclr_a0"], _pad_lora(a2, 64), g2m,
          vecs["k_k"], vecs["k_a"])
    h, proj, qkv, r, lw, k2, v, kk, a, g = _in_proj_prep(x, vecs["mix_norm_g"], win, pw)
    y, s0s = _wkv_fwd(r, lw, k2, v, kk, a)
    o_att, l_att = _attn_fwd(qkv)
    og = vecs["attn_out_g"] + rest_arriving(o_att)[0, 0]
    ycat = _mixers_out(y, r, k2, v, g, vecs["ln_x_w"], vecs["ln_x_b"], vecs["r_k"], o_att, l_att, og)
    wout, wg, wu, wd = get_rest(ycat)
    h2, act, dx2b, dgt, dup, dx1b, dx1, dya, dyb, loss8, dgf, dg2n = _ffn_all(
        x, ycat, wg, wu, wd, wout, vecs["ffn_norm_g"], vecs["final_norm_g"], tgt)
    gw = {
        "w_down": _wgrad(act, dx2b, 1408, 1024, "wgrad_down"),
        "w_gate": _wgrad(dgt, h2, 1408, 1024, "wgrad_gate"),
        "w_up": _wgrad(dup, h2, 1408, 1024, "wgrad_up"),
        "w_out": _wgrad(ycat, dx1b, 1024, 1024, "wgrad_out"),
    }

    lnw = vecs["ln_x_w"] + send_rest(gw)[0, 0]
    dy, dr_p, dk2_p, dv_p, dg, dlnw, dlnb, drk = _post_bwd(dya, y, r, k2, v, g, lnw, vecs["ln_x_b"], vecs["r_k"])
    dr_s, dlw, dk2_s, dv_s, dkk, da = _wkv_bwd(dy, s0s, r, lw, k2, v, kk, a)
    do_att, dl_att, dog = _combine_bwd(dyb, o_att, l_att, vecs["attn_out_g"])
    dq, dk, dv = _attn_bwd(do_att, dl_att, o_att, l_att, qkv)
    dproj, dx, dg1, dmu, dw0, dw2p, da0, da2p, dg2m, dk_k, dk_a = _prep_in_proj_bwd(
        proj, pw, (dr_p, dr_s, dlw, dk2_p, dk2_s, dv_p, dv_s, dkk, da, dg), dq, dk, dv, win, x, vecs["mix_norm_g"], dx1)
    gw["w_in"] = _wgrad(dproj, h, 1664, 1024, "wgrad_in")
    gw["decay_w2"] = dw2p[:64]
    gw["iclr_a2"] = da2p[64:]
    gw["gate_g2"] = dg2m
    gv = {"mix_norm_g": dg1, "mu_shift": dmu, "decay_w0": dw0, "iclr_a0": da0, "k_k": dk_k, "k_a": dk_a, "r_k": drk,
          "ln_x_w": dlnw, "ln_x_b": dlnb, "attn_out_g": dog, "ffn_norm_g": dg2n, "final_norm_g": dgf}
    return loss8, dx, gw, gv


N_CHIP = 4
N_DEV = 8
MATS = ("w_in", "w_out", "w_gate", "w_up", "w_down")
LORAS = ("decay_w2", "iclr_a2", "gate_g2")
VECS = (("mix_norm_g", 1024), ("mu_shift", 1792), ("decay_w0", 512), ("iclr_a0", 512), ("k_k", 512), ("k_a", 512),
        ("r_k", 512), ("ln_x_w", 512), ("ln_x_b", 512), ("attn_out_g", 512), ("ffn_norm_g", 1024),
        ("final_norm_g", 1024))
N_VEC = sum(n for _, n in VECS)
N_SMALL = N_VEC + 128
ANY = pl.BlockSpec(memory_space=pl.ANY)


def _flip(v, f):
    return 1 - v if f else v


class _Me:
    def __init__(self, mode):
        x, y, c = lax.axis_index("x"), lax.axis_index("y"), lax.axis_index("c")
        self.core, self.chip, self.dev = c, 2 * x + y, 4 * x + 2 * y + c
        self.sibling = (x, y, 1 - c)
        if mode == "chips":
            self.peers = [(px, py, c) for px, py in ((1 - x, y), (x, 1 - y), (1 - x, 1 - y))]
        else:
            self.peers = [(_flip(x, k & 4), _flip(y, k & 2), _flip(c, k & 1)) for k in range(1, N_DEV)]


def _half(core, rows):
    h = rows // 2
    return pl.ds(pl.multiple_of(core * h, h), h)


_BY_CHIP = ("gather", "chipsum", "halves")


def _peer_copy(srcs, dsts, kinds, send_sems, recv_sems, me, j, i, incoming):
    px, py, pc = me.peers[j]
    pchip, pdev = 2 * px + py, 4 * px + 2 * py + pc
    src, dst, kind = srcs[i], dsts[i], kinds[i]
    dev = (px, py, pc)
    if kind == "halves":
        mine, other = _half(me.core, src.shape[1]), _half(1 - me.core, src.shape[1])
        src, dst, dev = src.at[pchip, mine], dst.at[pchip, other if incoming else mine], me.sibling
    elif kind == "gather":
        rows = _half(me.core, src.shape[1])
        src, dst = src.at[me.chip, rows], dst.at[pchip if incoming else me.chip, rows]
    elif kind == "scatter":
        src, dst = src.at[pchip, _half(pc, src.shape[1])], dst.at[pdev if incoming else me.dev]
    elif kind == "chipsum":
        src, dst = src.at[pchip], dst.at[pchip if incoming else me.chip]
    else:
        dst = dst.at[pdev if incoming else me.dev]
    n = len(srcs)
    return pltpu.make_async_remote_copy(src_ref=src, dst_ref=dst, send_sem=send_sems.at[n * j + i],
                                        recv_sem=recv_sems.at[n * j + i], device_id=dev, device_id_type=MESH)


def _mode(kinds):
    return "chips" if kinds[0] in _BY_CHIP else "devs"


def _npeer(kinds):
    return N_CHIP - 1 if kinds[0] in _BY_CHIP else N_DEV - 1


def _sibling_halves(gs, name):
    n = len(gs)

    def body(*refs):
        srcs, dsts, send_sems, recv_sems = refs[:n], refs[n:2 * n], refs[2 * n], refs[2 * n + 1]
        me = _Me("chips")

        def copy(i, p):
            return pltpu.make_async_remote_copy(
                src_ref=srcs[i].at[p, _half(1 - me.core, srcs[i].shape[1])], dst_ref=dsts[i].at[p],
                send_sem=send_sems.at[N_CHIP * i + p], recv_sem=recv_sems.at[N_CHIP * i + p],
                device_id=me.sibling, device_id_type=MESH)

        copies = [copy(i, p) for i in range(n) for p in range(N_CHIP)]
        for cp in copies:
            cp.start()
        for cp in copies:
            cp.wait()

    return pl.pallas_call(
        body, name=name, in_specs=[ANY] * n, out_specs=[ANY] * n,
        out_shape=[jax.ShapeDtypeStruct((N_CHIP, g.shape[1] // 2, g.shape[2]), g.dtype) for g in gs],
        scratch_shapes=[pltpu.SemaphoreType.DMA((N_CHIP * n,)), pltpu.SemaphoreType.DMA((N_CHIP * n,))],
    )(*gs)


def _add_halves(g, other, core, tr, name):
    _, h, cols = other.shape

    def body(core_ref, g_ref, o_ref, out_ref):
        out_ref[...] = (g_ref[...].astype(F32) + o_ref[...].astype(F32)).astype(BF16)

    blk = lambda off: pl.BlockSpec((1, tr, cols), lambda p, i, core_ref: (p, core_ref[0] * (h // tr) * off + i, 0))
    return pl.pallas_call(
        body, name=name,
        grid_spec=pltpu.PrefetchScalarGridSpec(num_scalar_prefetch=1, grid=(N_CHIP, h // tr),
                                               in_specs=[blk(1), blk(0)], out_specs=blk(0)),
        out_shape=jax.ShapeDtypeStruct(other.shape, BF16),
        compiler_params=_params(("parallel", "parallel")),
    )(core, g, other)


def _swap_gathered(lands, name):
    n = len(lands)

    def body(*refs):
        dsts, send_sems, recv_sems = refs[n:2 * n], refs[2 * n], refs[2 * n + 1]
        me = _Me("chips")

        def copy(j, i, incoming):
            px, py, _ = me.peers[j]
            rows_out, rows_in = _half(me.core, dsts[i].shape[1]), _half(1 - me.core, dsts[i].shape[1])
            return pltpu.make_async_remote_copy(
                src_ref=dsts[i].at[2 * px + py, rows_out], dst_ref=dsts[i].at[2 * px + py, rows_in if incoming else rows_out],
                send_sem=send_sems.at[n * j + i], recv_sem=recv_sems.at[n * j + i], device_id=me.sibling, device_id_type=MESH)

        sends = [copy(j, i, False) for j in range(3) for i in range(n)]
        for cp in sends:
            cp.start()
        for j in range(3):
            for i in range(n):
                copy(j, i, True).wait_recv()
        for cp in sends:
            cp.wait_send()

    return pl.pallas_call(
        body, name=name, in_specs=[ANY] * n, out_specs=[ANY] * n,
        out_shape=[jax.ShapeDtypeStruct(l.shape, l.dtype) for l in lands],
        input_output_aliases={i: i for i in range(n)},
        scratch_shapes=[pltpu.SemaphoreType.DMA((3 * n,)), pltpu.SemaphoreType.DMA((3 * n,))],
    )(*lands)


def _join_halves(sums, name):
    n = len(sums)

    def body(*refs):
        dsts, send_sems, recv_sems = refs[n:2 * n], refs[2 * n], refs[2 * n + 1]
        me = _Me("chips")

        def copy(i, incoming):
            mine, other = _half(me.core, dsts[i].shape[0]), _half(1 - me.core, dsts[i].shape[0])
            return pltpu.make_async_remote_copy(src_ref=dsts[i].at[mine], dst_ref=dsts[i].at[other if incoming else mine],
                                                send_sem=send_sems.at[i], recv_sem=recv_sems.at[i],
                                                device_id=me.sibling, device_id_type=MESH)

        sends = [copy(i, False) for i in range(n)]
        for cp in sends:
            cp.start()
        for i in range(n):
            copy(i, True).wait_recv()
        for cp in sends:
            cp.wait_send()

    return pl.pallas_call(
        body, name=name, in_specs=[ANY] * n, out_specs=[ANY] * n,
        out_shape=[jax.ShapeDtypeStruct(s.shape, s.dtype) for s in sums],
        input_output_aliases={i: i for i in range(n)},
        scratch_shapes=[pltpu.SemaphoreType.DMA((n,)), pltpu.SemaphoreType.DMA((n,))],
    )(*sums)


HBM = pl.BlockSpec(memory_space=pltpu.HBM)
SEM = pl.BlockSpec(memory_space=pltpu.SEMAPHORE)
EFFECT = pltpu.SideEffectType.DATAFLOW_SIDE_EFFECTING


def _swap_start(arrs, lands, kinds, name):
    n = len(lands)
    ops = list(lands) if arrs is None else [*arrs, *lands]
    k = len(ops)

    def body(*refs):
        srcs, dsts, send_sems, recv_sems, token = refs[:n], refs[k - n:k], refs[k], refs[k + 1], refs[-1]
        me = _Me(_mode(kinds))
        for j in range(len(me.peers)):
            for i in range(n):
                _peer_copy(srcs, dsts, kinds, send_sems, recv_sems, me, j, i, False).start()
        token[...] = jnp.zeros_like(token)

    ns = _npeer(kinds) * n
    outs = pl.pallas_call(
        body, name=name,
        out_shape=(pltpu.SemaphoreType.DMA((ns,)), pltpu.SemaphoreType.DMA((ns,)),
                   *[pltpu.HBM(a.shape, a.dtype) for a in ops], jax.ShapeDtypeStruct((8, 128), F32)),
        in_specs=[HBM] * k, out_specs=(SEM, SEM, *[HBM] * k, pl.BlockSpec(memory_space=pltpu.VMEM)),
        input_output_aliases={i: 2 + i for i in range(k)},
        compiler_params=pltpu.CompilerParams(has_side_effects=EFFECT),
    )(*[pltpu.with_memory_space_constraint(a, pltpu.HBM) for a in ops])
    return outs[0], outs[1], outs[2:2 + k - n], outs[2 + k - n:2 + k], outs[-1]


def _swap_wait(send_sems, recv_sems, srcs_thru, lands_thru, after, kinds, name):
    n = len(lands_thru)
    ops = [*srcs_thru, *lands_thru]
    k = len(ops)

    def body(*refs):
        srcs, dsts, s_sems, r_sems = refs[:n], refs[k - n:k], refs[k], refs[k + 1]
        me = _Me(_mode(kinds))
        for j in range(len(me.peers)):
            for i in range(n):
                cp = _peer_copy(srcs, dsts, kinds, s_sems, r_sems, me, j, i, True)
                cp.wait_send()
                cp.wait_recv()

    outs = pl.pallas_call(
        body, name=name,
        out_shape=tuple(pltpu.HBM(a.shape, a.dtype) for a in ops),
        in_specs=[HBM] * k + [SEM, SEM, ANY], out_specs=tuple([HBM] * k),
        input_output_aliases={i: i for i in range(k)},
        compiler_params=pltpu.CompilerParams(has_side_effects=EFFECT),
    )(*ops, send_sems, recv_sems, after)
    return outs[k - n:]


def _adamw(w, g, m, v):
    m = ADAM_B1 * m + (1.0 - ADAM_B1) * g
    v = ADAM_B2 * v + (1.0 - ADAM_B2) * (g * g)
    m_hat = m / (1.0 - ADAM_B1 ** ADAM_STEP)
    v_hat = v / (1.0 - ADAM_B2 ** ADAM_STEP)
    delta = -ADAM_LR * (m_hat / (jnp.sqrt(v_hat) + ADAM_EPS) + ADAM_WD * w)
    return delta, m, v


def _reduce8(rbuf, core, tr, name):
    slots, h, cols = rbuf.shape

    def body(core_ref, r_ref, g_ref):
        g = r_ref[0].astype(F32)
        for s in range(1, slots):
            g = g + r_ref[s].astype(F32)
        g_ref[...] = g

    return pl.pallas_call(
        body, name=name,
        grid_spec=pltpu.PrefetchScalarGridSpec(
            num_scalar_prefetch=1, grid=(h // tr,),
            in_specs=[pl.BlockSpec((slots, tr, cols), lambda i, core_ref: (0, i, 0))],
            out_specs=pl.BlockSpec((tr, cols), lambda i, core_ref: (core_ref[0] * (h // tr) + i, 0))),
        out_shape=jax.ShapeDtypeStruct((2 * h, cols), F32),
        compiler_params=_params(("parallel",)),
    )(core, rbuf)


def _adamw_call(g, w, m, v, tr, name):
    _, rows, cols = w.shape

    def body(g_in, w_ref, m_ref, v_ref, g_ref, d_ref, nm_ref, nv_ref):
        g = g_in[...]
        g_ref[0] = g
        d_ref[0], nm_ref[0], nv_ref[0] = _adamw(w_ref[0], g, m_ref[0], v_ref[0])

    row = pl.BlockSpec((1, tr, cols), lambda i: (0, i, 0))
    return pl.pallas_call(
        body, name=name, grid=(rows // tr,),
        in_specs=[pl.BlockSpec((tr, cols), lambda i: (i, 0)), row, row, row], out_specs=[row] * 4,
        out_shape=[jax.ShapeDtypeStruct(w.shape, F32)] * 4,
        compiler_params=_params(("parallel",)),
    )(g, w, m, v)


def _rowsum_small(parts, loss8):
    def body(*refs):
        out = refs[-1]
        c0 = 0
        for ref in refs[:-1]:
            n = ref.shape[1]
            out[:, c0:c0 + n] = jnp.sum(ref[...], axis=0, keepdims=True)
            c0 += n

    return pl.pallas_call(body, name="rowsum_small", out_shape=jax.ShapeDtypeStruct((1, N_SMALL), F32))(*parts, loss8)


def _reduce_adamw_small(sbuf, ws, ms, vs):
    nv = len(ws)

    def body(*refs):
        s_ref, ins, outs = refs[0], refs[1:1 + 3 * nv], refs[1 + 3 * nv:]
        tot = s_ref[0]
        for s in range(1, N_DEV):
            tot = tot + s_ref[s]
        c0 = 0
        for i in range(nv):
            n = ins[i].shape[1]
            g = tot[:, c0:c0 + n]
            outs[i][...] = g
            outs[nv + i][...], outs[2 * nv + i][...], outs[3 * nv + i][...] = _adamw(
                ins[i][...], g, ins[nv + i][...], ins[2 * nv + i][...])
            c0 += n
        outs[-1][...] = tot[:, c0:]

    return pl.pallas_call(
        body, name="reduce_adamw_small",
        out_shape=[jax.ShapeDtypeStruct(a.shape, F32) for a in ws] * 4 + [jax.ShapeDtypeStruct((1, 128), F32)],
    )(sbuf, *ws, *ms, *vs)


_TRANSPOSED = ("w_in", "w_gate", "w_up")
_ROW_STACKED = MATS
_ADAM_TILE = {"w_in": 208, "w_out": 256, "w_gate": 176, "w_up": 176, "w_down": 176, "lora": 256}
_SUM_TILE = {"w_in": 208, "w_out": 128, "w_gate": 176, "w_up": 176, "w_down": 176, "lora": 128}


def _full(n, stacked):
    p, r, c = stacked.shape
    if n in _ROW_STACKED:
        return stacked.reshape(p * r, c)
    return jnp.transpose(stacked, (1, 0, 2)).reshape(r, p * c)


def _by_chip(n, full):
    if n in _ROW_STACKED:
        return full.reshape(N_CHIP, full.shape[0] // N_CHIP, full.shape[1])
    r, c = full.shape
    return jnp.transpose(full.reshape(r, N_CHIP, c // N_CHIP), (1, 0, 2))


def _with_own(land_shape, dtype, own, slot):
    return lax.dynamic_update_slice(lax.empty(land_shape, dtype), own[None], (slot,) + (0,) * own.ndim)


def _cast_into_slot(a, chip, tr, name, after=None):
    rows, cols = a.shape

    def body(chip_ref, a_ref, *rest):
        rest[-1][0] = a_ref[...].astype(BF16)

    extra = [] if after is None else [after]
    return pl.pallas_call(
        body, name=name,
        grid_spec=pltpu.PrefetchScalarGridSpec(
            num_scalar_prefetch=1, grid=(rows // tr,),
            in_specs=[pl.BlockSpec((tr, cols), lambda i, chip_ref: (i, 0))] + [ANY] * len(extra),
            out_specs=pl.BlockSpec((1, tr, cols), lambda i, chip_ref: (chip_ref[0], i, 0))),
        out_shape=jax.ShapeDtypeStruct((N_CHIP, rows, cols), BF16),
        compiler_params=_params(("parallel",)),
    )(chip, a, *extra)


def kernel(x, mix_norm_g, w_in, mu_shift, decay_w0, decay_w2, iclr_a0, iclr_a2, gate_g2, k_k, k_a, r_k, ln_x_w, ln_x_b, attn_out_g, w_out, ffn_norm_g, w_gate, w_up, w_down, final_norm_g, loss_target, m_mix_norm_g, m_w_in, m_mu_shift, m_decay_w0, m_decay_w2, m_iclr_a0, m_iclr_a2, m_gate_g2, m_k_k, m_k_a, m_r_k, m_ln_x_w, m_ln_x_b, m_attn_out_g, m_w_out, m_ffn_norm_g, m_w_gate, m_w_up, m_w_down, m_final_norm_g, v_mix_norm_g, v_w_in, v_mu_shift, v_decay_w0, v_decay_w2, v_iclr_a0, v_iclr_a2, v_gate_g2, v_k_k, v_k_a, v_r_k, v_ln_x_w, v_ln_x_b, v_attn_out_g, v_w_out, v_ffn_norm_g, v_w_gate, v_w_up, v_w_down, v_final_norm_g):
    names = ("mix_norm_g", "w_in", "mu_shift", "decay_w0", "decay_w2", "iclr_a0", "iclr_a2", "gate_g2", "k_k", "k_a",
             "r_k", "ln_x_w", "ln_x_b", "attn_out_g", "w_out", "ffn_norm_g", "w_gate", "w_up", "w_down", "final_norm_g")
    w = dict(zip(names, (mix_norm_g, w_in, mu_shift, decay_w0, decay_w2, iclr_a0, iclr_a2, gate_g2, k_k, k_a, r_k,
                         ln_x_w, ln_x_b, attn_out_g, w_out, ffn_norm_g, w_gate, w_up, w_down, final_norm_g)))
    m = dict(zip(names, (m_mix_norm_g, m_w_in, m_mu_shift, m_decay_w0, m_decay_w2, m_iclr_a0, m_iclr_a2, m_gate_g2,
                         m_k_k, m_k_a, m_r_k, m_ln_x_w, m_ln_x_b, m_attn_out_g, m_w_out, m_ffn_norm_g, m_w_gate,
                         m_w_up, m_w_down, m_final_norm_g)))
    v = dict(zip(names, (v_mix_norm_g, v_w_in, v_mu_shift, v_decay_w0, v_decay_w2, v_iclr_a0, v_iclr_a2, v_gate_g2,
                         v_k_k, v_k_a, v_r_k, v_ln_x_w, v_ln_x_b, v_attn_out_g, v_w_out, v_ffn_norm_g, v_w_gate,
                         v_w_up, v_w_down, v_final_norm_g)))
    first = ("w_in", "lora")
    rest = ("w_out", "w_gate", "w_up", "w_down")
    xi, yi, ci = lax.axis_index("x"), lax.axis_index("y"), lax.axis_index("c")
    my_chip, my_dev = 2 * xi + yi, 4 * xi + 2 * yi + ci
    gather, scatter = ("gather",) * 4, ("scatter",) * 4

    def stored(d):
        out = {n: jnp.transpose(d[n][0]) if n in _TRANSPOSED else d[n][0] for n in MATS}
        out["lora"] = jnp.concatenate([d[n][0] for n in LORAS], axis=0)
        return out

    ws, ms, vs = stored(w), stored(m), stored(v)
    lora_rows = [(0, 64), (64, 128), (128, 256)]
    chip = jnp.reshape(my_chip, (1,)).astype(jnp.int32)
    early = _swap_start(None, [_cast_into_slot(ws["w_in"], chip, _ADAM_TILE["w_in"], "cast_w_in"),
                               _with_own((N_CHIP,) + ws["lora"].shape, F32, ws["lora"], my_chip)], gather[:2],
                        "gather_first_start")
    lands = [_cast_into_slot(ws[n], chip, _ADAM_TILE[n], "cast_" + n, after=early[4]) for n in rest]
    ssem, rsem, srcs_thru, lands_thru, tok = _swap_start(None, lands, gather, "gather_rest_start")
    got = _swap_wait(early[0], early[1], early[2], early[3], tok, gather[:2], "gather_first_wait")
    win_all, lora_all = _swap_gathered(got, "gather_first_halves")
    win = _full("w_in", win_all)
    w2, a2, g2m = (_full(n, lora_all[:, a:b]) for n, (a, b) in zip(LORAS, lora_rows))

    vecs = {n: w[n].reshape(1, sz) for n, sz in VECS}
    vecs["mix_norm_g"] = vecs["mix_norm_g"] + tok[0, 0]

    swapping = []

    def rest_arriving(after):
        halves = _swap_wait(ssem, rsem, srcs_thru, lands_thru, after, gather, "gather_rest_wait")
        swapping.extend(_swap_start(None, halves, ("halves",) * 4, "gather_rest_halves_start"))
        return swapping[4]

    def get_rest(after):
        full = _swap_wait(swapping[0], swapping[1], swapping[2], swapping[3], after, ("halves",) * 4,
                          "gather_rest_halves_wait")
        return [_full(n, z) for n, z in zip(rest, full)]

    flight = []

    def my_half(g):
        h = g.shape[1] // 2
        return lax.dynamic_slice(g, (my_chip, ci * h, 0), (1, h, g.shape[2]))[0]

    def send_rest(gw):
        gs = [_by_chip(n, gw[n]) for n in rest]
        into = [_with_own((N_DEV,) + my_half(g).shape, BF16, my_half(g), my_dev) for g in gs]
        flight.extend(_swap_start(gs, into, scatter, "exchange_rest_start"))
        return flight[4]

    loss8, dx, gw, gv = _local_step(x[0], loss_target[0], win, vecs, w2, a2, g2m, rest_arriving, get_rest, send_rest)

    core = jnp.reshape(ci, (1,)).astype(jnp.int32)
    gs = [_by_chip("w_in", gw["w_in"]),
          jnp.concatenate([_by_chip(n, gw[n]) for n in LORAS], axis=1).astype(BF16)]
    theirs = _sibling_halves(gs, "presum_halves")
    sums = [_add_halves(g, o, core, _SUM_TILE[n], "chipsum_" + n) for n, g, o in zip(first, gs, theirs)]
    own = [lax.dynamic_index_in_dim(s, my_chip, 0, keepdims=False) for s in sums]
    last = _swap_start(sums, [_with_own(s.shape, BF16, o, my_chip) for s, o in zip(sums, own)], ("chipsum",) * 2,
                       "exchange_first_start")
    small = _rowsum_small([gv[n] for n, _ in VECS], loss8 + last[4])
    vecs_out = _swap_start([small], [_with_own((N_DEV,) + small.shape, F32, small, my_dev)], ("all",),
                           "exchange_vectors_start")


    def update(group, rbufs, tag):
        sums = [_reduce8(rb, core, _SUM_TILE[n], "reduce_" + n) for n, rb in zip(group, rbufs)]
        gsum = _join_halves(sums, "join_halves_" + tag)
        out = {}
        for n, g in zip(group, gsum):
            r = _adamw_call(g, ws[n][None], ms[n][None], vs[n][None], _ADAM_TILE[n], "adamw_" + n)
            if n == "lora":
                for name, (a, b) in zip(LORAS, lora_rows):
                    out[name] = [z[:, a:b] for z in r]
            else:
                out[n] = [jnp.transpose(z[0])[None] for z in r] if n in _TRANSPOSED else r
        return out, r[1]

    res, done = update(rest, _swap_wait(flight[0], flight[1], flight[2], flight[3], vecs_out[4], scatter,
                                        "exchange_rest_wait"), "rest")
    got = _swap_wait(last[0], last[1], last[2], last[3], done, ("chipsum",) * 2, "exchange_first_wait")
    res_first, done = update(first, got, "first")
    res.update(res_first)
    sbuf = _swap_wait(vecs_out[0], vecs_out[1], vecs_out[2], vecs_out[3], done, ("all",), "exchange_vectors_wait")[0]
    rows = lambda d: [d[n].reshape(1, sz) for n, sz in VECS]
    small_res = _reduce_adamw_small(sbuf, rows(w), rows(m), rows(v))

    outs = []
    for k in range(4):
        piece = {n: r[k] for n, r in res.items()}
        for i, (n, _) in enumerate(VECS):
            piece[n] = small_res[k * len(VECS) + i].reshape(w[n].shape)
        outs.extend(piece[n] for n in names)
    return (small_res[-1][0, 0], dx[None], *outs)
```

```python
import jax
import jax.numpy as jnp
from jax import lax
from jax.experimental import pallas as pl
from jax.experimental.pallas import tpu as pltpu

F32 = jnp.float32
BF16 = jnp.bfloat16

D_MODEL = 1024
HEAD_DIM = 64
RW = 512
N_PAIR = RW // 128
SHIFT_COLS = 1792
IN_COLS = 3328
D_FF = 2816
FF_CHUNK = 256
NORM_EPS = 1e-6
GN_EPS = 64e-5
CHUNK = 64
SUB = 16
WKV_PASSES = 1
ATTN_PASSES = 1
ATTN_BLOCK = 128
DILATIONS = (1, 4, 16)
NEG = -1e30
ADAM_LR, ADAM_B1, ADAM_B2, ADAM_EPS, ADAM_WD, ADAM_STEP = 0.001, 0.9, 0.999, 1e-08, 0.01, 10
VMEM_LIMIT = 56 * 1024 * 1024
MESH = pl.DeviceIdType.MESH


def _params(sem=None, **kw):
    return pltpu.CompilerParams(dimension_semantics=sem, vmem_limit_bytes=VMEM_LIMIT, **kw)


def _dot(a, b):
    return lax.dot_general(a, b, (((1,), (0,)), ((), ())), preferred_element_type=F32)


def _dot_nt(a, b):
    return lax.dot_general(a, b, (((1,), (1,)), ((), ())), preferred_element_type=F32)


def _dot_tn(a, b):
    return lax.dot_general(a, b, (((0,), (0,)), ((), ())), preferred_element_type=F32)


_FORMS = {"nn": ((1,), (0,)), "nt": ((1,), (1,)), "tn": ((0,), (0,))}


def _dg(a, b, form):
    if a.ndim == 3 or b.ndim == 3:
        nb = a.shape[0] if a.ndim == 3 else b.shape[0]
        return jnp.stack([_dg(a[i] if a.ndim == 3 else a, b[i] if b.ndim == 3 else b, form) for i in range(nb)], axis=0)
    return lax.dot_general(a, b, (_FORMS[form], ((), ())), preferred_element_type=F32)


def _split2(x):
    hi = x.astype(BF16)
    return hi, (x - hi.astype(F32)).astype(BF16)


def _split3(x):
    hi = x.astype(BF16)
    rest = x - hi.astype(F32)
    mid = rest.astype(BF16)
    return hi, mid, (rest - mid.astype(F32)).astype(BF16)


def _mm_raw(a, b, form, mode):
    if mode == 1:
        return _dg(a.astype(BF16), b.astype(BF16), form)
    if mode == 3:
        ah, al = _split2(a)
        bh, bl = _split2(b)
        return _dg(ah, bh, form) + (_dg(ah, bl, form) + _dg(al, bh, form))
    if mode == "L3":
        ab = a.astype(BF16)
        b1, b2, b3 = _split3(b)
        if form == "nn":
            n = b.shape[-1]
            wide = _dg(ab, jnp.concatenate([b1, b2, b3], axis=-1), form)
            return wide[..., :n] + (wide[..., n:2 * n] + wide[..., 2 * n:])
        return _dg(ab, b1, form) + (_dg(ab, b2, form) + _dg(ab, b3, form))
    assert mode == "R3", mode
    bb = b.astype(BF16)
    a1, a2, a3 = _split3(a)
    if form in ("nn", "nt"):
        m = a.shape[-2]
        tall = _dg(jnp.concatenate([a1, a2, a3], axis=-2), bb, form)
        return tall[..., :m, :] + (tall[..., m:2 * m, :] + tall[..., 2 * m:, :])
    return _dg(a1, bb, form) + (_dg(a2, bb, form) + _dg(a3, bb, form))


def _mm(a, b, form, mode):
    @jax.custom_vjp
    def f(a, b):
        return _mm_raw(a, b, form, mode)

    def fwd(a, b):
        return _mm_raw(a, b, form, mode), (a, b)

    def bwd(res, ct):
        a, b = res
        la = {1: 1, 3: 3, "L3": None, "R3": "R3"}[mode]
        lb = {1: 1, 3: 3, "L3": "L3", "R3": None}[mode]
        if form == "nn":
            da = None if la is None else _mm_raw(ct, b, "nt", la)
            db = None if lb is None else _mm_raw(a, ct, "tn", lb)
        elif form == "nt":
            da = None if la is None else _mm_raw(ct, b, "nn", la)
            db = None if lb is None else _mm_raw(ct, a, "tn", "R3" if lb == "L3" else lb)
        else:
            da = None if la is None else _mm_raw(b, ct, "nt", "L3" if la == "R3" else la)
            db = None if lb is None else _mm_raw(a, ct, "nn", lb)
        return (jnp.zeros_like(a) if da is None else da, jnp.zeros_like(b) if db is None else db)

    f.defvjp(fwd, bwd)
    return f(a, b)


def _seg_ones(n):
    r = lax.broadcasted_iota(jnp.int32, (n, n), 0) // HEAD_DIM
    c = lax.broadcasted_iota(jnp.int32, (n, n), 1) // HEAD_DIM
    return (r == c).astype(F32)


def _segsum(x, seg):
    return _mm(x, seg, "nn", "R3")


def _rms_fwd(x, g):
    rstd = lax.rsqrt(jnp.mean(x * x, axis=-1, keepdims=True) + NORM_EPS)
    return x * rstd * g


def _rms_bwd(dy, x, g):
    rstd = lax.rsqrt(jnp.mean(x * x, axis=-1, keepdims=True) + NORM_EPS)
    xn = x * rstd
    dxn = dy * g
    dx = rstd * (dxn - xn * jnp.mean(dxn * xn, axis=-1, keepdims=True))
    return dx, dy * xn


def _sigmoid(x):
    return 1.0 / (1.0 + jnp.exp(-x))


def _softplus(x):
    return jnp.maximum(x, 0.0) + jnp.log(1.0 + jnp.exp(-jnp.abs(x)))


def _acc(ref, val, first):
    @pl.when(first)
    def _():
        ref[...] = val

    @pl.when(jnp.logical_not(first))
    def _():
        ref[...] += val


def _colsum8(v):
    rows, n = v.shape
    return jnp.sum(v.reshape(rows // 8, 8, n), axis=0)


def _prep_fn(p, pprev, mu, w0, w2p, a0, a2p, g2, k_k, k_a):
    seg = _seg_ones(RW)
    ps = p + (pprev - p) * mu
    r = ps[:, 0:RW]
    k = ps[:, RW:2 * RW]
    v = ps[:, 2 * RW:3 * RW]
    xwa = ps[:, 3 * RW:3 * RW + 128]
    xg = ps[:, 3 * RW + 128:3 * RW + 256]
    wraw = -_softplus(-(w0 + _mm(jnp.tanh(xwa), w2p, "nn", 3))) - 0.5
    lw = -jnp.exp(wraw)
    a = _sigmoid(a0 + _mm(xwa, a2p, "nn", 3))
    g = _mm(_sigmoid(xg), g2, "nn", 3)
    kk = k * k_k
    kk = kk / jnp.maximum(jnp.sqrt(_segsum(kk * kk, seg)), 1e-12)
    k2 = k * (1.0 + (a - 1.0) * k_a)
    return r, lw, k2, v, kk, a, g


def _transposed(z):
    return jnp.stack([z[i].T for i in range(z.shape[0])], axis=0) if z.ndim == 3 else z.T


def _solve_unit_lower(lmat, rhs):
    c = lmat.shape[-1]
    row = lax.broadcasted_iota(jnp.int32, (c, c), 0)
    col = lax.broadcasted_iota(jnp.int32, (c, c), 1)
    eye = (row == col).astype(F32)
    ld = jnp.where(row // SUB == col // SUB, lmat, 0.0)
    lo = lmat - ld
    x = eye + ld
    m = ld
    mm = lambda p, q: _mm(p, q, "nn", WKV_PASSES)
    cat = jnp.concatenate
    m = mm(m, m)
    for _ in range(2):
        mx = mm(m, cat([m, x], axis=-1))
        m, x = mx[..., :c], x + mx[..., c:]
    x = x + mm(m, x)
    gw = mm(x, cat([lo, rhs], axis=-1))
    g, w = gw[..., :c], gw[..., c:]
    gg = mm(g, cat([g, w], axis=-1))
    w = w + gg[..., c:]
    return w + mm(gg[..., :c], w)


def _wkv_chunk_fn(s0, r, lw, k, v, kk, a):
    c = r.shape[-2]
    n = 2 * c
    row = lax.broadcasted_iota(jnp.int32, (n, n), 0)
    col = lax.broadcasted_iota(jnp.int32, (n, n), 1)
    same = (row // c) == (col // c)
    incl = jnp.logical_and(row >= col, same)
    strict = jnp.logical_and(row > col, same)
    sel = (lax.broadcasted_iota(jnp.int32, (n, 128), 0) // c) == (lax.broadcasted_iota(jnp.int32, (n, 128), 1) // HEAD_DIM)
    two = lambda z: jnp.concatenate([z, z], axis=-2)
    lw2 = two(lw)
    mm = lambda p_, q_, form: _mm(p_, q_, form, WKV_PASSES)
    cl = _mm(incl.astype(F32), lw2, "nn", "L3")
    p = jnp.exp(cl)
    pinv = jnp.exp(-cl)
    pprev = jnp.exp(cl - lw2)
    kk2 = two(kk)
    at = jnp.where(sel, -kk2 * pprev, 0.0)
    bt = jnp.where(sel, kk2 * two(a) * pinv, 0.0)
    kt = jnp.where(sel, two(k) * pinv, 0.0)
    rt = jnp.where(sel, two(r) * p, 0.0)
    vt = jnp.where(sel, two(v), 0.0)
    cat = jnp.concatenate
    bk = cat([bt, kt], axis=-2)
    arbk = mm(cat([at, rt], axis=-2), bk, "nt")
    ab, ak = jnp.where(strict, arbk[..., :n, :n], 0.0), jnp.where(strict, arbk[..., :n, n:], 0.0)
    rb, rk = jnp.where(incl, arbk[..., n:, :n], 0.0), jnp.where(incl, arbk[..., n:, n:], 0.0)
    s0t = _transposed(s0)
    u = _solve_unit_lower(ab, mm(cat([at, ak], axis=-1), cat([s0t, vt], axis=-2), "nn"))
    y2 = mm(cat([rt, rb, rk], axis=-1), cat([s0t, u, vt], axis=-2), "nn")
    plast = jnp.exp(jnp.sum(lw, axis=-2, keepdims=True))
    s1 = (s0 + mm(cat([u, vt], axis=-2), bk, "tn")) * plast
    r2 = lax.broadcasted_iota(jnp.int32, (128, 128), 0) // HEAD_DIM
    c2 = lax.broadcasted_iota(jnp.int32, (128, 128), 1) // HEAD_DIM
    return y2[..., :c, :] + y2[..., c:, :], jnp.where(r2 == c2, s1, 0.0)


def _post_fn(y, r, k2, v, g, lnw, lnb, rk):
    seg = _seg_ones(RW)
    mean = _segsum(y, seg) * (1.0 / HEAD_DIM)
    yc = y - mean
    var = _segsum(yc * yc, seg) * (1.0 / HEAD_DIM)
    yn = yc * lax.rsqrt(var + GN_EPS)
    out = yn * lnw + lnb + _segsum(r * k2 * rk, seg) * v
    return out * g


def _attn_block_fn(q, kc, vc, kp=None, vp=None):
    n = ATTN_BLOCK
    qi = lax.broadcasted_iota(jnp.int32, (n, n), 0)
    kj = lax.broadcasted_iota(jnp.int32, (n, n), 1)
    lane = lax.broadcasted_iota(jnp.int32, (1, 128), 1)
    scale = HEAD_DIM ** -0.5
    valid = kj <= qi
    keys, vals = kc, vc
    if kp is not None:
        valid = jnp.concatenate([valid, kj >= qi], axis=-1)
        keys, vals = jnp.concatenate([kc, kp], axis=-2), jnp.concatenate([vc, vp], axis=-2)
    m0 = (lane // HEAD_DIM) == 0
    q2 = jnp.concatenate([jnp.where(m0, q, 0.0), jnp.where(m0, 0.0, q)], axis=-2)
    valid2 = jnp.concatenate([valid, valid], axis=-2)
    s = jnp.where(valid2, _mm(q2, keys, "nt", ATTN_PASSES) * scale, NEG)
    m = jnp.max(s, axis=-1, keepdims=True)
    p = jnp.exp(s - m)
    den = jnp.sum(p, axis=-1, keepdims=True)
    o2 = _mm(p, vals, "nn", ATTN_PASSES) / den
    l2 = m + jnp.log(den)
    return jnp.where(m0, o2[..., :n, :], o2[..., n:, :]), jnp.where(m0, l2[..., :n, :], l2[..., n:, :])


def _attn_block_bwd(q, kc, vc, kp, vp, o, lse, do, dl):
    n = ATTN_BLOCK
    cat = jnp.concatenate
    qi = lax.broadcasted_iota(jnp.int32, (n, n), 0)
    kj = lax.broadcasted_iota(jnp.int32, (n, n), 1)
    m0 = (lax.broadcasted_iota(jnp.int32, (1, 128), 1) // HEAD_DIM) == 0
    scale = HEAD_DIM ** -0.5
    valid = kj <= qi
    keys, vals = kc, vc
    if kp is not None:
        valid = cat([valid, kj >= qi], axis=-1)
        keys, vals = cat([kc, kp], axis=-2), cat([vc, vp], axis=-2)
    stack = lambda z: cat([jnp.where(m0, z, 0.0), jnp.where(m0, 0.0, z)], axis=-2)
    q2, do2 = stack(q), stack(do)
    lse2 = cat([jnp.max(jnp.where(m0, lse, NEG), axis=-1, keepdims=True),
                jnp.max(jnp.where(m0, NEG, lse), axis=-1, keepdims=True)], axis=-2)
    delta = jnp.sum(do2 * cat([o, o], axis=-2), axis=-1, keepdims=True)
    dlse = jnp.sum(stack(dl), axis=-1, keepdims=True)
    mm = lambda a, b, form: _mm_raw(a, b, form, ATTN_PASSES)
    s = jnp.where(cat([valid, valid], axis=-2), mm(q2, keys, "nt") * scale, NEG)
    p = jnp.exp(s - lse2)
    ds = p * (mm(do2, vals, "nt") - delta + dlse)
    dq2 = mm(ds, keys, "nn") * scale
    dq = jnp.where(m0, dq2[..., :n, :], dq2[..., n:, :])
    dkeys = mm(ds, q2, "tn") * scale
    dvals = mm(p, do2, "tn")
    if kp is None:
        return dq, dkeys, dvals
    return dq, dkeys[..., :n, :], dvals[..., :n, :], dkeys[..., n:, :], dvals[..., n:, :]


def _combine_fn(o1, o2, o3, l1, l2, l3, og):
    seg = _seg_ones(o1.shape[-1])
    m = jnp.maximum(jnp.maximum(l1, l2), l3)
    e1, e2, e3 = jnp.exp(l1 - m), jnp.exp(l2 - m), jnp.exp(l3 - m)
    o = (e1 * o1 + e2 * o2 + e3 * o3) / (e1 + e2 + e3)
    o = o * lax.rsqrt(_segsum(o * o, seg) * (1.0 / HEAD_DIM) + NORM_EPS)
    return o * og


def _shifted(p, last8, first):
    prow = jnp.where(first, 0.0, last8[7:8, :])
    rolled = pltpu.roll(p, 1, axis=0)
    rid = lax.broadcasted_iota(jnp.int32, p.shape, 0)
    return jnp.where(rid == 0, prow, rolled)


_PREP_TM = 256


def _prep_specs(tm):
    vec = lambda n: pl.BlockSpec((1, n), lambda i: (0, 0))
    mat = lambda r, n: pl.BlockSpec((r, n), lambda i: (0, 0))
    return [vec(SHIFT_COLS), vec(RW), mat(128, RW), vec(RW), mat(128, RW), mat(128, RW), vec(RW), vec(RW)]


def _in_proj_prep(x, g1, win, pw):
    t = x.shape[0]
    tm = _PREP_TM

    def body(x_ref, g_ref, w_ref, mu, w0, w2p, a0, a2p, g2, k_k, k_a, h_ref, pa_ref, qkv_ref, *rest):
        outs, carry = rest[:7], rest[7]

        @pl.when(pl.program_id(0) == 0)
        def _():
            carry[...] = jnp.zeros_like(carry)

        h = _rms_fwd(x_ref[...], g_ref[...]).astype(BF16)
        h_ref[...] = h
        proj = _dot_nt(h, w_ref[...])
        p = proj[:, :SHIFT_COLS]
        pa_ref[...] = p
        for j in range(3):
            for pr in range(N_PAIR):
                c0 = SHIFT_COLS + j * RW + pr * 128
                qkv_ref[j, pr] = proj[:, c0:c0 + 128]
        pprev = _shifted(p, carry[...], pl.program_id(0) == 0)
        carry[...] = p[tm - 8:, :]
        res = _prep_fn(p, pprev, mu[...], w0[...], w2p[...], a0[...], a2p[...], g2[...], k_k[...], k_a[...])
        for o_ref, val in zip(outs, res):
            o_ref[...] = val

    row = pl.BlockSpec((tm, RW), lambda i: (i, 0))
    return pl.pallas_call(
        body, name="in_proj_prep", grid=(t // tm,),
        in_specs=[pl.BlockSpec((tm, D_MODEL), lambda i: (i, 0)), pl.BlockSpec((1, D_MODEL), lambda i: (0, 0)),
                  pl.BlockSpec((IN_COLS, D_MODEL), lambda i: (0, 0))] + _prep_specs(tm),
        out_specs=[pl.BlockSpec((tm, D_MODEL), lambda i: (i, 0)), pl.BlockSpec((tm, SHIFT_COLS), lambda i: (i, 0)),
                   pl.BlockSpec((3, N_PAIR, tm, 128), lambda i: (0, 0, i, 0))] + [row] * 7,
        out_shape=[jax.ShapeDtypeStruct((t, D_MODEL), BF16), jax.ShapeDtypeStruct((t, SHIFT_COLS), F32),
                   jax.ShapeDtypeStruct((3, N_PAIR, t, 128), F32)] + [jax.ShapeDtypeStruct((t, RW), F32)] * 7,
        scratch_shapes=[pltpu.VMEM((8, SHIFT_COLS), F32)],
        compiler_params=_params(("arbitrary",)),
    )(x, g1, win, *pw)


def _pairs(ref):
    return jnp.stack([ref[:, 128 * p:128 * (p + 1)] for p in range(N_PAIR)], axis=0)


def _wkv_fwd(r, lw, k2, v, kk, a):
    t = r.shape[0]
    nc = t // CHUNK

    def body(r_ref, lw_ref, k_ref, v_ref, kk_ref, a_ref, y_ref, s_ref, st):
        @pl.when(pl.program_id(0) == 0)
        def _():
            st[...] = jnp.zeros_like(st)

        s0 = st[...]
        s_ref[0] = s0
        y, s1 = _wkv_chunk_fn(s0, *[_pairs(ref) for ref in (r_ref, lw_ref, k_ref, v_ref, kk_ref, a_ref)])
        for p in range(N_PAIR):
            y_ref[:, 128 * p:128 * (p + 1)] = y[p]
        st[...] = s1

    blk = pl.BlockSpec((CHUNK, RW), lambda c: (c, 0))
    return pl.pallas_call(
        body, name="wkv_fwd", grid=(nc,),
        in_specs=[blk] * 6,
        out_specs=[blk, pl.BlockSpec((1, N_PAIR, 128, 128), lambda c: (c, 0, 0, 0))],
        out_shape=[jax.ShapeDtypeStruct((t, RW), F32), jax.ShapeDtypeStruct((nc, N_PAIR, 128, 128), F32)],
        scratch_shapes=[pltpu.VMEM((N_PAIR, 128, 128), F32)],
        compiler_params=_params(("arbitrary",)),
    )(r, lw, k2, v, kk, a)


_POST_TM = 512


ATTN_GROUP = 2


def _dilated_rows(d, r, n):
    if d == 1:
        return pl.ds(pl.multiple_of(n * ATTN_BLOCK, ATTN_BLOCK), ATTN_BLOCK)
    return pl.ds(r + n * (ATTN_BLOCK * d), ATTN_BLOCK, stride=d)


def _for_each_sequence(t, unit):
    for di, d in enumerate(DILATIONS):

        @pl.when(pl.program_id(1) == di)
        def _(di=di, d=d):
            nb = t // (ATTN_BLOCK * d)
            if d == 1:
                unit(di, [(d, 0, 0)], False)
                unit(di, [(d, 0, 1)], True)
                lax.fori_loop(1, nb // 2, lambda k, c: (unit(di, [(d, 0, 2 * k), (d, 0, 2 * k + 1)], True), c)[1], 0)
            else:

                def residues(r, carry):
                    unit(di, [(d, r, 0), (d, r + d // 2, 0)], False)
                    if nb > 1:
                        lax.fori_loop(1, nb, lambda n, c: (unit(di, [(d, r, n), (d, r + d // 2, n)], True), c)[1], 0)
                    return carry

                lax.fori_loop(0, d // 2, residues, 0)


def _take(ref, lead, rows_list):
    return jnp.stack([ref.at[(*lead, g)][rows, :] for rows in rows_list for g in range(ref.shape[len(lead)])], axis=0)


def _put(ref, lead, rows_list, val, add=False):
    k = 0
    for rows in rows_list:
        for g in range(ref.shape[len(lead)]):
            if add:
                ref.at[(*lead, g)][rows, :] += val[k]
            else:
                ref.at[(*lead, g)][rows, :] = val[k]
            k += 1


def _attn_fwd(qkv):
    t = qkv.shape[2]

    def body(q_ref, k_ref, v_ref, o_ref, l_ref):
        def unit(di, places, has_prev):
            cur = [_dilated_rows(d, r, n) for d, r, n in places]
            args = [_take(ref, (0,), cur) for ref in (q_ref, k_ref, v_ref)]
            if has_prev:
                prv = [_dilated_rows(d, r, n - 1) for d, r, n in places]
                args += [_take(ref, (0,), prv) for ref in (k_ref, v_ref)]
            o, lse = _attn_block_fn(*args)
            _put(o_ref, (0,), cur, o)
            _put(l_ref, (0,), cur, lse)

        _for_each_sequence(t, unit)

    spec = lambda j: pl.BlockSpec((1, ATTN_GROUP, t, 128), lambda i, b: (j, i, 0, 0))
    out = pl.BlockSpec((1, ATTN_GROUP, t, 128), lambda i, b: (b, i, 0, 0))
    return pl.pallas_call(
        body, name="attn_fwd", grid=(N_PAIR // ATTN_GROUP, len(DILATIONS)),
        in_specs=[spec(0), spec(1), spec(2)], out_specs=[out, out],
        out_shape=[jax.ShapeDtypeStruct((3, N_PAIR, t, 128), F32)] * 2,
        compiler_params=_params(("parallel", "arbitrary")),
    )(qkv, qkv, qkv)


_COMB_TM = 512


def _mixers_out(y, r, k2, v, g, lnw, lnb, rk, o, l, og):
    t = y.shape[0]
    tm = _COMB_TM

    def body(y_ref, r_ref, k_ref, v_ref, g_ref, lnw_ref, lnb_ref, rk_ref, o_ref, l_ref, og_ref, out_ref):
        out_ref[:, :RW] = _post_fn(y_ref[...], r_ref[...], k_ref[...], v_ref[...], g_ref[...],
                                   lnw_ref[...], lnb_ref[...], rk_ref[...]).astype(BF16)
        for p in range(N_PAIR):
            cols = slice(128 * p, 128 * (p + 1))
            out_ref[:, RW + 128 * p:RW + 128 * (p + 1)] = _combine_fn(
                o_ref[0, p], o_ref[1, p], o_ref[2, p], l_ref[0, p], l_ref[1, p], l_ref[2, p], og_ref[:, cols]).astype(BF16)

    row = pl.BlockSpec((tm, RW), lambda i: (i, 0))
    vec = pl.BlockSpec((1, RW), lambda i: (0, 0))
    blk = pl.BlockSpec((3, N_PAIR, tm, 128), lambda i: (0, 0, i, 0))
    return pl.pallas_call(
        body, name="mixers_out", grid=(t // tm,),
        in_specs=[row] * 5 + [vec] * 3 + [blk, blk, vec], out_specs=pl.BlockSpec((tm, D_MODEL), lambda i: (i, 0)),
        out_shape=jax.ShapeDtypeStruct((t, D_MODEL), BF16),
        compiler_params=_params(("parallel",)),
    )(y, r, k2, v, g, lnw, lnb, rk, o, l, og)


def _ffn_all(x, ycat, wg, wu, wd, wout, g2, gf, tgt):
    t = x.shape[0]
    tm = 256

    def body(x_ref, y_ref, wg_ref, wu_ref, wd_ref, wo_ref, g2_ref, gf_ref, t_ref,
             h_ref, act_ref, dx2b_ref, dgt_ref, dup_ref, dx1b_ref, dx1_ref, dya_ref, dyb_ref, loss_ref, dgf_ref, dg2_ref,
             gt_s, up_s):
        first = pl.program_id(0) == 0
        x1 = x_ref[...] + _dot(y_ref[...], wo_ref[...])
        h = _rms_fwd(x1, g2_ref[...]).astype(BF16)
        h_ref[...] = h
        for c0 in range(0, D_FF, FF_CHUNK):
            cols = slice(c0, c0 + FF_CHUNK)
            gt = _dot_nt(h, wg_ref[cols, :])
            up = _dot_nt(h, wu_ref[cols, :])
            gt_s[:, cols] = gt.astype(BF16)
            up_s[:, cols] = up.astype(BF16)
            act_ref[:, cols] = (gt * _sigmoid(gt) * up).astype(BF16)
        x2 = x1 + _dot(act_ref[...], wd_ref[...])
        gf_ = gf_ref[...]
        diff = _rms_fwd(x2, gf_) - t_ref[...]
        lrow = 0.5 * jnp.sum(_colsum8(diff * diff), axis=1, keepdims=True) * (1.0 / D_MODEL)
        _acc(loss_ref, jnp.broadcast_to(lrow, (8, 128)), first)
        dx2, dgr = _rms_bwd(diff * (1.0 / D_MODEL), x2, gf_)
        _acc(dgf_ref, _colsum8(dgr), first)
        dx2b = dx2.astype(BF16)
        dx2b_ref[...] = dx2b
        for c0 in range(0, D_FF, FF_CHUNK):
            cols = slice(c0, c0 + FF_CHUNK)
            dact = _dot_nt(dx2b, wd_ref[cols, :])
            gt = gt_s[:, cols].astype(F32)
            sg = _sigmoid(gt)
            dgt_ref[:, cols] = (dact * up_s[:, cols].astype(F32) * sg * (1.0 + gt * (1.0 - sg))).astype(BF16)
            dup_ref[:, cols] = (dact * gt * sg).astype(BF16)
        dh = _dot(dgt_ref[...], wg_ref[...]) + _dot(dup_ref[...], wu_ref[...])
        dxn, dgr2 = _rms_bwd(dh, x1, g2_ref[...])
        _acc(dg2_ref, _colsum8(dgr2), first)
        dx1 = dx2 + dxn
        dx1_ref[...] = dx1
        dx1b = dx1.astype(BF16)
        dx1b_ref[...] = dx1b
        dy = _dot_nt(dx1b, wo_ref[...])
        dya_ref[...] = dy[:, :RW]
        dyb_ref[...] = dy[:, RW:]

    row = pl.BlockSpec((tm, D_MODEL), lambda i: (i, 0))
    wide = pl.BlockSpec((tm, D_FF), lambda i: (i, 0))
    half = pl.BlockSpec((tm, RW), lambda i: (i, 0))
    wsp = pl.BlockSpec((D_FF, D_MODEL), lambda i: (0, 0))
    vec = pl.BlockSpec((1, D_MODEL), lambda i: (0, 0))
    part = pl.BlockSpec((8, D_MODEL), lambda i: (0, 0))
    bf = lambda n: jax.ShapeDtypeStruct((t, n), BF16)
    return pl.pallas_call(
        body, name="ffn_all", grid=(t // tm,),
        in_specs=[row, row, wsp, wsp, wsp, pl.BlockSpec((D_MODEL, D_MODEL), lambda i: (0, 0)), vec, vec, row],
        out_specs=[row, wide, row, wide, wide, row, row, half, half, pl.BlockSpec((8, 128), lambda i: (0, 0)), part, part],
        out_shape=[bf(D_MODEL), bf(D_FF), bf(D_MODEL), bf(D_FF), bf(D_FF), bf(D_MODEL),
                   jax.ShapeDtypeStruct((t, D_MODEL), F32), jax.ShapeDtypeStruct((t, RW), F32),
                   jax.ShapeDtypeStruct((t, RW), F32), jax.ShapeDtypeStruct((8, 128), F32),
                   jax.ShapeDtypeStruct((8, D_MODEL), F32), jax.ShapeDtypeStruct((8, D_MODEL), F32)],
        scratch_shapes=[pltpu.VMEM((tm, D_FF), BF16), pltpu.VMEM((tm, D_FF), BF16)],
        compiler_params=_params(("arbitrary",)),
    )(x, ycat, wg, wu, wd, wout, g2, gf, tgt)


def _wgrad(a, b, tk, tn, name):
    t, kdim = a.shape
    ndim = b.shape[1]

    def body(a_ref, b_ref, o_ref):
        o_ref[...] = _dot_tn(a_ref[...], b_ref[...]).astype(BF16)

    return pl.pallas_call(
        body, name=name, grid=(kdim // tk, ndim // tn),
        in_specs=[pl.BlockSpec((t, tk), lambda i, j: (0, i)), pl.BlockSpec((t, tn), lambda i, j: (0, j))],
        out_specs=pl.BlockSpec((tk, tn), lambda i, j: (i, j)),
        out_shape=jax.ShapeDtypeStruct((kdim, ndim), BF16),
        compiler_params=_params(("parallel", "parallel")),
    )(a, b)


def _post_bwd(dya, y, r, k2, v, g, lnw, lnb, rk):
    t = y.shape[0]
    tm = _POST_TM

    def body(d_ref, y_ref, r_ref, k_ref, v_ref, g_ref, lnw_ref, lnb_ref, rk_ref,
             dy_ref, dr_ref, dk_ref, dv_ref, dg_ref, dlnw_ref, dlnb_ref, drk_ref):
        first = pl.program_id(0) == 0
        ones = jnp.ones((tm, 1), F32)
        prim = (y_ref[...], r_ref[...], k_ref[...], v_ref[...], g_ref[...],
                ones * lnw_ref[...], ones * lnb_ref[...], ones * rk_ref[...])
        _, vjp = jax.vjp(_post_fn, *prim)
        dy, dr, dk, dv, dg, dlnw, dlnb, drk = vjp(d_ref[...])
        dy_ref[...] = dy
        dr_ref[...] = dr
        dk_ref[...] = dk
        dv_ref[...] = dv
        dg_ref[...] = dg
        _acc(dlnw_ref, _colsum8(dlnw), first)
        _acc(dlnb_ref, _colsum8(dlnb), first)
        _acc(drk_ref, _colsum8(drk), first)

    row = pl.BlockSpec((tm, RW), lambda i: (i, 0))
    vec = pl.BlockSpec((1, RW), lambda i: (0, 0))
    part = pl.BlockSpec((8, RW), lambda i: (0, 0))
    return pl.pallas_call(
        body, name="rwkv_post_bwd", grid=(t // tm,),
        in_specs=[row] * 6 + [vec] * 3, out_specs=[row] * 5 + [part] * 3,
        out_shape=[jax.ShapeDtypeStruct((t, RW), F32)] * 5 + [jax.ShapeDtypeStruct((8, RW), F32)] * 3,
        compiler_params=_params(("arbitrary",)),
    )(dya, y, r, k2, v, g, lnw, lnb, rk)


def _wkv_bwd(dy, s0s, r, lw, k2, v, kk, a):
    t = r.shape[0]
    nc = t // CHUNK

    def body(dy_ref, s_ref, r_ref, lw_ref, k_ref, v_ref, kk_ref, a_ref,
             dr_ref, dlw_ref, dk_ref, dv_ref, dkk_ref, da_ref, ds):
        @pl.when(pl.program_id(0) == 0)
        def _():
            ds[...] = jnp.zeros_like(ds)

        _, vjp = jax.vjp(_wkv_chunk_fn, s_ref[0],
                         *[_pairs(ref) for ref in (r_ref, lw_ref, k_ref, v_ref, kk_ref, a_ref)])
        res = vjp((_pairs(dy_ref), ds[...]))
        ds[...] = res[0]
        for ref, val in zip((dr_ref, dlw_ref, dk_ref, dv_ref, dkk_ref, da_ref), res[1:]):
            for p in range(N_PAIR):
                ref[:, 128 * p:128 * (p + 1)] = val[p]

    blk = pl.BlockSpec((CHUNK, RW), lambda c: (nc - 1 - c, 0))
    return pl.pallas_call(
        body, name="wkv_bwd", grid=(nc,),
        in_specs=[blk, pl.BlockSpec((1, N_PAIR, 128, 128), lambda c: (nc - 1 - c, 0, 0, 0))] + [blk] * 6,
        out_specs=[blk] * 6,
        out_shape=[jax.ShapeDtypeStruct((t, RW), F32)] * 6,
        scratch_shapes=[pltpu.VMEM((N_PAIR, 128, 128), F32)],
        compiler_params=_params(("arbitrary",)),
    )(dy, s0s, r, lw, k2, v, kk, a)


def _prep_in_proj_bwd(proj, pw, douts, dq, dk, dv, win, x, g1, dx1):
    t = proj.shape[0]
    tm = _PREP_TM
    nt = t // tm

    def body(p_ref, l8_ref, mu, w0, w2p, a0, a2p, g2, k_k, k_a, dr, dr2, dlw, dk2, dk22, dv, dv2, dkk, da, dg,
             dq_ref, dkq_ref, dvq_ref, w_ref, x_ref, g1_ref, dx1_ref,
             dproj_ref, dx_ref, dg1_ref, dmu_ref, dw0_ref, dw2_ref, da0_ref, da2_ref, dg2_ref, dkk_ref, dka_ref, carry):
        i = pl.program_id(0)
        first = i == 0

        @pl.when(first)
        def _():
            carry[...] = jnp.zeros_like(carry)

        p = p_ref[...]
        pprev = _shifted(p, l8_ref[...], i == nt - 1)
        ones = jnp.ones((tm, 1), F32)
        prim = (p, pprev, ones * mu[...], ones * w0[...], w2p[...], ones * a0[...], a2p[...], g2[...],
                ones * k_k[...], ones * k_a[...])
        _, vjp = jax.vjp(_prep_fn, *prim)
        dp, dpp, dmu, dw0, dw2, da0, da2, dg2, dkk_, dka = vjp(
            (dr[...] + dr2[...], dlw[...], dk2[...] + dk22[...], dv[...] + dv2[...], dkk[...], da[...], dg[...]))
        up = pltpu.roll(dpp, tm - 1, axis=0)
        rid = lax.broadcasted_iota(jnp.int32, dpp.shape, 0)
        dpa = dp + jnp.where(rid == tm - 1, carry[0:1, :], up)
        carry[...] = jnp.broadcast_to(dpp[0:1, :], carry.shape)
        _acc(dmu_ref, _colsum8(dmu), first)
        _acc(dw0_ref, _colsum8(dw0), first)
        _acc(dw2_ref, dw2, first)
        _acc(da0_ref, _colsum8(da0), first)
        _acc(da2_ref, da2, first)
        _acc(dg2_ref, dg2, first)
        _acc(dkk_ref, _colsum8(dkk_), first)
        _acc(dka_ref, _colsum8(dka), first)
        parts = [dpa] + [ref[pr] for ref in (dq_ref, dkq_ref, dvq_ref) for pr in range(N_PAIR)]
        dproj = jnp.concatenate([z.astype(BF16) for z in parts], axis=1)
        dproj_ref[...] = dproj
        dxn, dgr = _rms_bwd(_dot(dproj, w_ref[...]), x_ref[...], g1_ref[...])
        dx_ref[...] = dx1_ref[...] + dxn
        _acc(dg1_ref, _colsum8(dgr), first)

    rev = lambda i: (nt - 1 - i, 0)
    row = pl.BlockSpec((tm, RW), rev)
    wide = pl.BlockSpec((tm, D_MODEL), rev)
    pair = pl.BlockSpec((N_PAIR, tm, 128), lambda i: (0, nt - 1 - i, 0))
    part = lambda n: pl.BlockSpec((8, n), lambda i: (0, 0))
    mat = pl.BlockSpec((128, RW), lambda i: (0, 0))
    return pl.pallas_call(
        body, name="prep_in_proj_bwd", grid=(nt,),
        in_specs=[pl.BlockSpec((tm, SHIFT_COLS), rev),
                  pl.BlockSpec((8, SHIFT_COLS), lambda i: (jnp.maximum((nt - 1 - i) * (tm // 8) - 1, 0), 0))]
                 + _prep_specs(tm) + [row] * 10
                 + [pair] * 3 + [pl.BlockSpec((IN_COLS, D_MODEL), lambda i: (0, 0)), wide,
                                 pl.BlockSpec((1, D_MODEL), lambda i: (0, 0)), wide],
        out_specs=[pl.BlockSpec((tm, IN_COLS), rev), wide, part(D_MODEL), part(SHIFT_COLS), part(RW), mat, part(RW), mat,
                   mat, part(RW), part(RW)],
        out_shape=[jax.ShapeDtypeStruct((t, IN_COLS), BF16), jax.ShapeDtypeStruct((t, D_MODEL), F32),
                   jax.ShapeDtypeStruct((8, D_MODEL), F32), jax.ShapeDtypeStruct((8, SHIFT_COLS), F32),
                   jax.ShapeDtypeStruct((8, RW), F32), jax.ShapeDtypeStruct((128, RW), F32),
                   jax.ShapeDtypeStruct((8, RW), F32), jax.ShapeDtypeStruct((128, RW), F32),
                   jax.ShapeDtypeStruct((128, RW), F32), jax.ShapeDtypeStruct((8, RW), F32),
                   jax.ShapeDtypeStruct((8, RW), F32)],
        scratch_shapes=[pltpu.VMEM((8, SHIFT_COLS), F32)],
        compiler_params=_params(("arbitrary",)),
    )(proj, proj, *pw, *douts, dq, dk, dv, win, x, g1, dx1)


def _combine_bwd(dyb, o, l, og):
    t = dyb.shape[0]
    tm = _COMB_TM

    def body(d_ref, o_ref, l_ref, og_ref, do_ref, dl_ref, dog_ref):
        ones = jnp.ones((tm, 1), F32)
        dog = []
        for p in range(N_PAIR):
            cols = slice(128 * p, 128 * (p + 1))
            _, vjp = jax.vjp(_combine_fn, o_ref[0, p], o_ref[1, p], o_ref[2, p], l_ref[0, p], l_ref[1, p], l_ref[2, p],
                             ones * og_ref[:, cols])
            res = vjp(d_ref[:, cols])
            for b in range(3):
                do_ref[b, p] = res[b]
                dl_ref[b, p] = res[3 + b]
            dog.append(_colsum8(res[6]))
        _acc(dog_ref, jnp.concatenate(dog, axis=1), pl.program_id(0) == 0)

    blk = pl.BlockSpec((3, N_PAIR, tm, 128), lambda i: (0, 0, i, 0))
    return pl.pallas_call(
        body, name="attn_combine_bwd", grid=(t // tm,),
        in_specs=[pl.BlockSpec((tm, RW), lambda i: (i, 0)), blk, blk, pl.BlockSpec((1, RW), lambda i: (0, 0))],
        out_specs=[blk, blk, pl.BlockSpec((8, RW), lambda i: (0, 0))],
        out_shape=[jax.ShapeDtypeStruct((3, N_PAIR, t, 128), F32)] * 2 + [jax.ShapeDtypeStruct((8, RW), F32)],
        compiler_params=_params(("arbitrary",)),
    )(dyb, o, l, og)


def _attn_bwd(do, dl, o, lse, qkv):
    t = qkv.shape[2]

    def body(do_ref, dl_ref, o_ref, l_ref, q_ref, k_ref, v_ref, dq_ref, dk_ref, dv_ref):
        @pl.when(pl.program_id(1) == 0)
        def _():
            for ref in (dq_ref, dk_ref, dv_ref):
                ref[...] = jnp.zeros_like(ref)

        def unit(di, places, has_prev):
            cur = [_dilated_rows(d, r, n) for d, r, n in places]
            q, kc, vc = [_take(ref, (0,), cur) for ref in (q_ref, k_ref, v_ref)]
            kp = vp = None
            if has_prev:
                prv = [_dilated_rows(d, r, n - 1) for d, r, n in places]
                kp, vp = [_take(ref, (0,), prv) for ref in (k_ref, v_ref)]
            res = _attn_block_bwd(q, kc, vc, kp, vp, *[_take(ref, (0,), cur) for ref in (o_ref, l_ref, do_ref, dl_ref)])
            _put(dq_ref, (), cur, res[0], add=True)
            _put(dk_ref, (), cur, res[1], add=True)
            _put(dv_ref, (), cur, res[2], add=True)
            if has_prev:
                _put(dk_ref, (), prv, res[3], add=True)
                _put(dv_ref, (), prv, res[4], add=True)

        _for_each_sequence(t, unit)

    spec = lambda j: pl.BlockSpec((1, ATTN_GROUP, t, 128), lambda i, b: (j, i, 0, 0))
    branch = pl.BlockSpec((1, ATTN_GROUP, t, 128), lambda i, b: (b, i, 0, 0))
    out = pl.BlockSpec((ATTN_GROUP, t, 128), lambda i, b: (i, 0, 0))
    return pl.pallas_call(
        body, name="attn_bwd", grid=(N_PAIR // ATTN_GROUP, len(DILATIONS)),
        in_specs=[branch] * 4 + [spec(0), spec(1), spec(2)], out_specs=[out] * 3,
        out_shape=[jax.ShapeDtypeStruct((N_PAIR, t, 128), F32)] * 3,
        compiler_params=_params(("parallel", "arbitrary")),
    )(do, dl, o, lse, qkv, qkv, qkv)


def _pad_lora(w, lo):
    z = jnp.zeros((64, RW), F32)
    return jnp.concatenate([w, z], axis=0) if lo == 0 else jnp.concatenate([z, w], axis=0)


def _local_step(x, tgt, win, vecs, w2, a2, g2m, get_rest, send_rest):
    pw = (vecs["mu_shift"], vecs["decay_w0"], _pad_lora(w2, 0), vecs["iclr_a0"], _pad_lora(a2, 64), g2m,
          vecs["k_k"], vecs["k_a"])
    h, proj, qkv, r, lw, k2, v, kk, a, g = _in_proj_prep(x, vecs["mix_norm_g"], win, pw)
    y, s0s = _wkv_fwd(r, lw, k2, v, kk, a)
    o_att, l_att = _attn_fwd(qkv)
    ycat = _mixers_out(y, r, k2, v, g, vecs["ln_x_w"], vecs["ln_x_b"], vecs["r_k"], o_att, l_att, vecs["attn_out_g"])
    wout, wg, wu, wd = get_rest(ycat)
    h2, act, dx2b, dgt, dup, dx1b, dx1, dya, dyb, loss8, dgf, dg2n = _ffn_all(
        x, ycat, wg, wu, wd, wout, vecs["ffn_norm_g"], vecs["final_norm_g"], tgt)
    gw = {
        "w_down": _wgrad(act, dx2b, 1408, 1024, "wgrad_down"),
        "w_gate": _wgrad(dgt, h2, 1408, 1024, "wgrad_gate"),
        "w_up": _wgrad(dup, h2, 1408, 1024, "wgrad_up"),
        "w_out": _wgrad(ycat, dx1b, 1024, 1024, "wgrad_out"),
    }

    lnw = vecs["ln_x_w"] + send_rest(gw)[0, 0]
    dy, dr_p, dk2_p, dv_p, dg, dlnw, dlnb, drk = _post_bwd(dya, y, r, k2, v, g, lnw, vecs["ln_x_b"], vecs["r_k"])
    dr_s, dlw, dk2_s, dv_s, dkk, da = _wkv_bwd(dy, s0s, r, lw, k2, v, kk, a)
    do_att, dl_att, dog = _combine_bwd(dyb, o_att, l_att, vecs["attn_out_g"])
    dq, dk, dv = _attn_bwd(do_att, dl_att, o_att, l_att, qkv)
    dproj, dx, dg1, dmu, dw0, dw2p, da0, da2p, dg2m, dk_k, dk_a = _prep_in_proj_bwd(
        proj, pw, (dr_p, dr_s, dlw, dk2_p, dk2_s, dv_p, dv_s, dkk, da, dg), dq, dk, dv, win, x, vecs["mix_norm_g"], dx1)
    gw["w_in"] = _wgrad(dproj, h, 1664, 1024, "wgrad_in")
    gw["decay_w2"] = dw2p[:64]
    gw["iclr_a2"] = da2p[64:]
    gw["gate_g2"] = dg2m
    gv = {"mix_norm_g": dg1, "mu_shift": dmu, "decay_w0": dw0, "iclr_a0": da0, "k_k": dk_k, "k_a": dk_a, "r_k": drk,
          "ln_x_w": dlnw, "ln_x_b": dlnb, "attn_out_g": dog, "ffn_norm_g": dg2n, "final_norm_g": dgf}
    return loss8, dx, gw, gv


N_CHIP = 4
N_DEV = 8
MATS = ("w_in", "w_out", "w_gate", "w_up", "w_down")
LORAS = ("decay_w2", "iclr_a2", "gate_g2")
VECS = (("mix_norm_g", 1024), ("mu_shift", 1792), ("decay_w0", 512), ("iclr_a0", 512), ("k_k", 512), ("k_a", 512),
        ("r_k", 512), ("ln_x_w", 512), ("ln_x_b", 512), ("attn_out_g", 512), ("ffn_norm_g", 1024),
        ("final_norm_g", 1024))
N_VEC = sum(n for _, n in VECS)
N_SMALL = N_VEC + 128
ANY = pl.BlockSpec(memory_space=pl.ANY)


def _flip(v, f):
    return 1 - v if f else v


class _Me:
    def __init__(self, mode):
        x, y, c = lax.axis_index("x"), lax.axis_index("y"), lax.axis_index("c")
        self.core, self.chip, self.dev = c, 2 * x + y, 4 * x + 2 * y + c
        self.sibling = (x, y, 1 - c)
        if mode == "chips":
            self.peers = [(px, py, c) for px, py in ((1 - x, y), (x, 1 - y), (1 - x, 1 - y))]
        else:
            self.peers = [(_flip(x, k & 4), _flip(y, k & 2), _flip(c, k & 1)) for k in range(1, N_DEV)]


def _half(core, rows):
    h = rows // 2
    return pl.ds(pl.multiple_of(core * h, h), h)


_BY_CHIP = ("gather", "whole", "chipsum")


def _peer_copy(srcs, dsts, kinds, send_sems, recv_sems, me, j, i, incoming):
    px, py, pc = me.peers[j]
    pchip, pdev = 2 * px + py, 4 * px + 2 * py + pc
    src, dst, kind = srcs[i], dsts[i], kinds[i]
    if kind in ("gather", "whole"):
        rows = _half(me.core, src.shape[1]) if kind == "gather" else pl.ds(0, src.shape[1])
        src, dst = src.at[me.chip, rows], dst.at[pchip if incoming else me.chip, rows]
    elif kind == "scatter":
        src, dst = src.at[pchip, _half(pc, src.shape[1])], dst.at[pdev if incoming else me.dev]
    elif kind == "chipsum":
        src, dst = src.at[pchip], dst.at[pchip if incoming else me.chip]
    else:
        dst = dst.at[pdev if incoming else me.dev]
    n = len(srcs)
    return pltpu.make_async_remote_copy(src_ref=src, dst_ref=dst, send_sem=send_sems.at[n * j + i],
                                        recv_sem=recv_sems.at[n * j + i], device_id=(px, py, pc), device_id_type=MESH)


def _mode(kinds):
    return "chips" if kinds[0] in _BY_CHIP else "devs"


def _npeer(kinds):
    return N_CHIP - 1 if kinds[0] in _BY_CHIP else N_DEV - 1


def _sibling_halves(gs, name):
    n = len(gs)

    def body(*refs):
        srcs, dsts, send_sems, recv_sems = refs[:n], refs[n:2 * n], refs[2 * n], refs[2 * n + 1]
        me = _Me("chips")

        def copy(i, p):
            return pltpu.make_async_remote_copy(
                src_ref=srcs[i].at[p, _half(1 - me.core, srcs[i].shape[1])], dst_ref=dsts[i].at[p],
                send_sem=send_sems.at[N_CHIP * i + p], recv_sem=recv_sems.at[N_CHIP * i + p],
                device_id=me.sibling, device_id_type=MESH)

        copies = [copy(i, p) for i in range(n) for p in range(N_CHIP)]
        for cp in copies:
            cp.start()
        for cp in copies:
            cp.wait()

    return pl.pallas_call(
        body, name=name, in_specs=[ANY] * n, out_specs=[ANY] * n,
        out_shape=[jax.ShapeDtypeStruct((N_CHIP, g.shape[1] // 2, g.shape[2]), g.dtype) for g in gs],
        scratch_shapes=[pltpu.SemaphoreType.DMA((N_CHIP * n,)), pltpu.SemaphoreType.DMA((N_CHIP * n,))],
    )(*gs)


def _add_halves(g, other, core, tr, name):
    _, h, cols = other.shape

    def body(core_ref, g_ref, o_ref, out_ref):
        out_ref[...] = (g_ref[...].astype(F32) + o_ref[...].astype(F32)).astype(BF16)

    blk = lambda off: pl.BlockSpec((1, tr, cols), lambda p, i, core_ref: (p, core_ref[0] * (h // tr) * off + i, 0))
    return pl.pallas_call(
        body, name=name,
        grid_spec=pltpu.PrefetchScalarGridSpec(num_scalar_prefetch=1, grid=(N_CHIP, h // tr),
                                               in_specs=[blk(1), blk(0)], out_specs=blk(0)),
        out_shape=jax.ShapeDtypeStruct(other.shape, BF16),
        compiler_params=_params(("parallel", "parallel")),
    )(core, g, other)


def _swap_gathered(lands, name):
    n = len(lands)

    def body(*refs):
        dsts, send_sems, recv_sems = refs[n:2 * n], refs[2 * n], refs[2 * n + 1]
        me = _Me("chips")

        def copy(j, i, incoming):
            px, py, _ = me.peers[j]
            rows_out, rows_in = _half(me.core, dsts[i].shape[1]), _half(1 - me.core, dsts[i].shape[1])
            return pltpu.make_async_remote_copy(
                src_ref=dsts[i].at[2 * px + py, rows_out], dst_ref=dsts[i].at[2 * px + py, rows_in if incoming else rows_out],
                send_sem=send_sems.at[n * j + i], recv_sem=recv_sems.at[n * j + i], device_id=me.sibling, device_id_type=MESH)

        sends = [copy(j, i, False) for j in range(3) for i in range(n)]
        for cp in sends:
            cp.start()
        for j in range(3):
            for i in range(n):
                copy(j, i, True).wait_recv()
        for cp in sends:
            cp.wait_send()

    return pl.pallas_call(
        body, name=name, in_specs=[ANY] * n, out_specs=[ANY] * n,
        out_shape=[jax.ShapeDtypeStruct(l.shape, l.dtype) for l in lands],
        input_output_aliases={i: i for i in range(n)},
        scratch_shapes=[pltpu.SemaphoreType.DMA((3 * n,)), pltpu.SemaphoreType.DMA((3 * n,))],
    )(*lands)


def _join_halves(sums, name):
    n = len(sums)

    def body(*refs):
        dsts, send_sems, recv_sems = refs[n:2 * n], refs[2 * n], refs[2 * n + 1]
        me = _Me("chips")

        def copy(i, incoming):
            mine, other = _half(me.core, dsts[i].shape[0]), _half(1 - me.core, dsts[i].shape[0])
            return pltpu.make_async_remote_copy(src_ref=dsts[i].at[mine], dst_ref=dsts[i].at[other if incoming else mine],
                                                send_sem=send_sems.at[i], recv_sem=recv_sems.at[i],
                                                device_id=me.sibling, device_id_type=MESH)

        sends = [copy(i, False) for i in range(n)]
        for cp in sends:
            cp.start()
        for i in range(n):
            copy(i, True).wait_recv()
        for cp in sends:
            cp.wait_send()

    return pl.pallas_call(
        body, name=name, in_specs=[ANY] * n, out_specs=[ANY] * n,
        out_shape=[jax.ShapeDtypeStruct(s.shape, s.dtype) for s in sums],
        input_output_aliases={i: i for i in range(n)},
        scratch_shapes=[pltpu.SemaphoreType.DMA((n,)), pltpu.SemaphoreType.DMA((n,))],
    )(*sums)


HBM = pl.BlockSpec(memory_space=pltpu.HBM)
SEM = pl.BlockSpec(memory_space=pltpu.SEMAPHORE)
EFFECT = pltpu.SideEffectType.DATAFLOW_SIDE_EFFECTING


def _swap_start(arrs, lands, kinds, name):
    n = len(lands)
    ops = list(lands) if arrs is None else [*arrs, *lands]
    k = len(ops)

    def body(*refs):
        srcs, dsts, send_sems, recv_sems, token = refs[:n], refs[k - n:k], refs[k], refs[k + 1], refs[-1]
        me = _Me(_mode(kinds))
        for j in range(len(me.peers)):
            for i in range(n):
                _peer_copy(srcs, dsts, kinds, send_sems, recv_sems, me, j, i, False).start()
        token[...] = jnp.zeros_like(token)

    ns = _npeer(kinds) * n
    outs = pl.pallas_call(
        body, name=name,
        out_shape=(pltpu.SemaphoreType.DMA((ns,)), pltpu.SemaphoreType.DMA((ns,)),
                   *[pltpu.HBM(a.shape, a.dtype) for a in ops], jax.ShapeDtypeStruct((8, 128), F32)),
        in_specs=[HBM] * k, out_specs=(SEM, SEM, *[HBM] * k, pl.BlockSpec(memory_space=pltpu.VMEM)),
        input_output_aliases={i: 2 + i for i in range(k)},
        compiler_params=pltpu.CompilerParams(has_side_effects=EFFECT),
    )(*[pltpu.with_memory_space_constraint(a, pltpu.HBM) for a in ops])
    return outs[0], outs[1], outs[2:2 + k - n], outs[2 + k - n:2 + k], outs[-1]


def _swap_wait(send_sems, recv_sems, srcs_thru, lands_thru, after, kinds, name):
    n = len(lands_thru)
    ops = [*srcs_thru, *lands_thru]
    k = len(ops)

    def body(*refs):
        srcs, dsts, s_sems, r_sems = refs[:n], refs[k - n:k], refs[k], refs[k + 1]
        me = _Me(_mode(kinds))
        for j in range(len(me.peers)):
            for i in range(n):
                cp = _peer_copy(srcs, dsts, kinds, s_sems, r_sems, me, j, i, True)
                cp.wait_send()
                cp.wait_recv()

    outs = pl.pallas_call(
        body, name=name,
        out_shape=tuple(pltpu.HBM(a.shape, a.dtype) for a in ops),
        in_specs=[HBM] * k + [SEM, SEM, ANY], out_specs=tuple([HBM] * k),
        input_output_aliases={i: i for i in range(k)},
        compiler_params=pltpu.CompilerParams(has_side_effects=EFFECT),
    )(*ops, send_sems, recv_sems, after)
    return outs[k - n:]


def _adamw(w, g, m, v):
    m = ADAM_B1 * m + (1.0 - ADAM_B1) * g
    v = ADAM_B2 * v + (1.0 - ADAM_B2) * (g * g)
    m_hat = m / (1.0 - ADAM_B1 ** ADAM_STEP)
    v_hat = v / (1.0 - ADAM_B2 ** ADAM_STEP)
    delta = -ADAM_LR * (m_hat / (jnp.sqrt(v_hat) + ADAM_EPS) + ADAM_WD * w)
    return delta, m, v


def _reduce8(rbuf, core, tr, name):
    slots, h, cols = rbuf.shape

    def body(core_ref, r_ref, g_ref):
        g = r_ref[0].astype(F32)
        for s in range(1, slots):
            g = g + r_ref[s].astype(F32)
        g_ref[...] = g

    return pl.pallas_call(
        body, name=name,
        grid_spec=pltpu.PrefetchScalarGridSpec(
            num_scalar_prefetch=1, grid=(h // tr,),
            in_specs=[pl.BlockSpec((slots, tr, cols), lambda i, core_ref: (0, i, 0))],
            out_specs=pl.BlockSpec((tr, cols), lambda i, core_ref: (core_ref[0] * (h // tr) + i, 0))),
        out_shape=jax.ShapeDtypeStruct((2 * h, cols), F32),
        compiler_params=_params(("parallel",)),
    )(core, rbuf)


def _adamw_call(g, w, m, v, tr, name):
    _, rows, cols = w.shape

    def body(g_in, w_ref, m_ref, v_ref, g_ref, d_ref, nm_ref, nv_ref):
        g = g_in[...]
        g_ref[0] = g
        d_ref[0], nm_ref[0], nv_ref[0] = _adamw(w_ref[0], g, m_ref[0], v_ref[0])

    row = pl.BlockSpec((1, tr, cols), lambda i: (0, i, 0))
    return pl.pallas_call(
        body, name=name, grid=(rows // tr,),
        in_specs=[pl.BlockSpec((tr, cols), lambda i: (i, 0)), row, row, row], out_specs=[row] * 4,
        out_shape=[jax.ShapeDtypeStruct(w.shape, F32)] * 4,
        compiler_params=_params(("parallel",)),
    )(g, w, m, v)


def _rowsum_small(parts, loss8):
    def body(*refs):
        out = refs[-1]
        c0 = 0
        for ref in refs[:-1]:
            n = ref.shape[1]
            out[:, c0:c0 + n] = jnp.sum(ref[...], axis=0, keepdims=True)
            c0 += n

    return pl.pallas_call(body, name="rowsum_small", out_shape=jax.ShapeDtypeStruct((1, N_SMALL), F32))(*parts, loss8)


def _reduce_adamw_small(sbuf, ws, ms, vs):
    nv = len(ws)

    def body(*refs):
        s_ref, ins, outs = refs[0], refs[1:1 + 3 * nv], refs[1 + 3 * nv:]
        tot = s_ref[0]
        for s in range(1, N_DEV):
            tot = tot + s_ref[s]
        c0 = 0
        for i in range(nv):
            n = ins[i].shape[1]
            g = tot[:, c0:c0 + n]
            outs[i][...] = g
            outs[nv + i][...], outs[2 * nv + i][...], outs[3 * nv + i][...] = _adamw(
                ins[i][...], g, ins[nv + i][...], ins[2 * nv + i][...])
            c0 += n
        outs[-1][...] = tot[:, c0:]

    return pl.pallas_call(
        body, name="reduce_adamw_small",
        out_shape=[jax.ShapeDtypeStruct(a.shape, F32) for a in ws] * 4 + [jax.ShapeDtypeStruct((1, 128), F32)],
    )(sbuf, *ws, *ms, *vs)


_TRANSPOSED = ("w_in", "w_gate", "w_up")
_ROW_STACKED = MATS
_ADAM_TILE = {"w_in": 208, "w_out": 256, "w_gate": 176, "w_up": 176, "w_down": 176, "lora": 256}
_SUM_TILE = {"w_in": 208, "w_out": 128, "w_gate": 176, "w_up": 176, "w_down": 176, "lora": 128}


def _full(n, stacked):
    p, r, c = stacked.shape
    if n in _ROW_STACKED:
        return stacked.reshape(p * r, c)
    return jnp.transpose(stacked, (1, 0, 2)).reshape(r, p * c)


def _by_chip(n, full):
    if n in _ROW_STACKED:
        return full.reshape(N_CHIP, full.shape[0] // N_CHIP, full.shape[1])
    r, c = full.shape
    return jnp.transpose(full.reshape(r, N_CHIP, c // N_CHIP), (1, 0, 2))


def _with_own(land_shape, dtype, own, slot):
    return lax.dynamic_update_slice(lax.empty(land_shape, dtype), own[None], (slot,) + (0,) * own.ndim)


def _cast_into_slot(a, chip, tr, name, after=None):
    rows, cols = a.shape

    def body(chip_ref, a_ref, *rest):
        rest[-1][0] = a_ref[...].astype(BF16)

    extra = [] if after is None else [after]
    return pl.pallas_call(
        body, name=name,
        grid_spec=pltpu.PrefetchScalarGridSpec(
            num_scalar_prefetch=1, grid=(rows // tr,),
            in_specs=[pl.BlockSpec((tr, cols), lambda i, chip_ref: (i, 0))] + [ANY] * len(extra),
            out_specs=pl.BlockSpec((1, tr, cols), lambda i, chip_ref: (chip_ref[0], i, 0))),
        out_shape=jax.ShapeDtypeStruct((N_CHIP, rows, cols), BF16),
        compiler_params=_params(("parallel",)),
    )(chip, a, *extra)


def kernel(x, mix_norm_g, w_in, mu_shift, decay_w0, decay_w2, iclr_a0, iclr_a2, gate_g2, k_k, k_a, r_k, ln_x_w, ln_x_b, attn_out_g, w_out, ffn_norm_g, w_gate, w_up, w_down, final_norm_g, loss_target, m_mix_norm_g, m_w_in, m_mu_shift, m_decay_w0, m_decay_w2, m_iclr_a0, m_iclr_a2, m_gate_g2, m_k_k, m_k_a, m_r_k, m_ln_x_w, m_ln_x_b, m_attn_out_g, m_w_out, m_ffn_norm_g, m_w_gate, m_w_up, m_w_down, m_final_norm_g, v_mix_norm_g, v_w_in, v_mu_shift, v_decay_w0, v_decay_w2, v_iclr_a0, v_iclr_a2, v_gate_g2, v_k_k, v_k_a, v_r_k, v_ln_x_w, v_ln_x_b, v_attn_out_g, v_w_out, v_ffn_norm_g, v_w_gate, v_w_up, v_w_down, v_final_norm_g):
    names = ("mix_norm_g", "w_in", "mu_shift", "decay_w0", "decay_w2", "iclr_a0", "iclr_a2", "gate_g2", "k_k", "k_a",
             "r_k", "ln_x_w", "ln_x_b", "attn_out_g", "w_out", "ffn_norm_g", "w_gate", "w_up", "w_down", "final_norm_g")
    w = dict(zip(names, (mix_norm_g, w_in, mu_shift, decay_w0, decay_w2, iclr_a0, iclr_a2, gate_g2, k_k, k_a, r_k,
                         ln_x_w, ln_x_b, attn_out_g, w_out, ffn_norm_g, w_gate, w_up, w_down, final_norm_g)))
    m = dict(zip(names, (m_mix_norm_g, m_w_in, m_mu_shift, m_decay_w0, m_decay_w2, m_iclr_a0, m_iclr_a2, m_gate_g2,
                         m_k_k, m_k_a, m_r_k, m_ln_x_w, m_ln_x_b, m_attn_out_g, m_w_out, m_ffn_norm_g, m_w_gate,
                         m_w_up, m_w_down, m_final_norm_g)))
    v = dict(zip(names, (v_mix_norm_g, v_w_in, v_mu_shift, v_decay_w0, v_decay_w2, v_iclr_a0, v_iclr_a2, v_gate_g2,
                         v_k_k, v_k_a, v_r_k, v_ln_x_w, v_ln_x_b, v_attn_out_g, v_w_out, v_ffn_norm_g, v_w_gate,
                         v_w_up, v_w_down, v_final_norm_g)))
    first = ("w_in", "lora")
    rest = ("w_out", "w_gate", "w_up", "w_down")
    xi, yi, ci = lax.axis_index("x"), lax.axis_index("y"), lax.axis_index("c")
    my_chip, my_dev = 2 * xi + yi, 4 * xi + 2 * yi + ci
    gather, scatter = ("gather",) * 4, ("scatter",) * 4

    def stored(d):
        out = {n: jnp.transpose(d[n][0]) if n in _TRANSPOSED else d[n][0] for n in MATS}
        out["lora"] = jnp.concatenate([d[n][0] for n in LORAS], axis=0)
        return out

    ws, ms, vs = stored(w), stored(m), stored(v)
    lora_rows = [(0, 64), (64, 128), (128, 256)]
    chip = jnp.reshape(my_chip, (1,)).astype(jnp.int32)
    early = _swap_start(None, [_cast_into_slot(ws["w_in"], chip, _ADAM_TILE["w_in"], "cast_w_in"),
                               _with_own((N_CHIP,) + ws["lora"].shape, F32, ws["lora"], my_chip)], gather[:2],
                        "gather_first_start")
    lands = [_cast_into_slot(ws[n], chip, _ADAM_TILE[n], "cast_" + n, after=early[4]) for n in rest]
    gather_rest = ("gather", "gather", "whole", "whole")
    ssem, rsem, srcs_thru, lands_thru, tok = _swap_start(None, lands, gather_rest, "gather_rest_start")
    got = _swap_wait(early[0], early[1], early[2], early[3], tok, gather[:2], "gather_first_wait")
    win_all, lora_all = _swap_gathered(got, "gather_first_halves")
    win = _full("w_in", win_all)
    w2, a2, g2m = (_full(n, lora_all[:, a:b]) for n, (a, b) in zip(LORAS, lora_rows))

    vecs = {n: w[n].reshape(1, sz) for n, sz in VECS}
    vecs["mix_norm_g"] = vecs["mix_norm_g"] + tok[0, 0]

    def get_rest(after):
        got_rest = _swap_wait(ssem, rsem, srcs_thru, lands_thru, after, gather_rest, "gather_rest_wait")
        swapped = _swap_gathered(got_rest[:2], "gather_rest_halves")
        return [_full(n, z) for n, z in zip(rest, [*swapped, *got_rest[2:]])]

    flight = []

    def my_half(g):
        h = g.shape[1] // 2
        return lax.dynamic_slice(g, (my_chip, ci * h, 0), (1, h, g.shape[2]))[0]

    def send_rest(gw):
        gs = [_by_chip(n, gw[n]) for n in rest]
        into = [_with_own((N_DEV,) + my_half(g).shape, BF16, my_half(g), my_dev) for g in gs]
        flight.extend(_swap_start(gs, into, scatter, "exchange_rest_start"))
        return flight[4]

    loss8, dx, gw, gv = _local_step(x[0], loss_target[0], win, vecs, w2, a2, g2m, get_rest, send_rest)

    core = jnp.reshape(ci, (1,)).astype(jnp.int32)
    gs = [_by_chip("w_in", gw["w_in"]),
          jnp.concatenate([_by_chip(n, gw[n]) for n in LORAS], axis=1).astype(BF16)]
    theirs = _sibling_halves(gs, "presum_halves")
    sums = [_add_halves(g, o, core, _SUM_TILE[n], "chipsum_" + n) for n, g, o in zip(first, gs, theirs)]
    own = [lax.dynamic_index_in_dim(s, my_chip, 0, keepdims=False) for s in sums]
    last = _swap_start(sums, [_with_own(s.shape, BF16, o, my_chip) for s, o in zip(sums, own)], ("chipsum",) * 2,
                       "exchange_first_start")
    small = _rowsum_small([gv[n] for n, _ in VECS], loss8 + last[4])
    vecs_out = _swap_start([small], [_with_own((N_DEV,) + small.shape, F32, small, my_dev)], ("all",),
                           "exchange_vectors_start")


    def update(group, rbufs, tag):
        sums = [_reduce8(rb, core, _SUM_TILE[n], "reduce_" + n) for n, rb in zip(group, rbufs)]
        gsum = _join_halves(sums, "join_halves_" + tag)
        out = {}
        for n, g in zip(group, gsum):
            r = _adamw_call(g, ws[n][None], ms[n][None], vs[n][None], _ADAM_TILE[n], "adamw_" + n)
            if n == "lora":
                for name, (a, b) in zip(LORAS, lora_rows):
                    out[name] = [z[:, a:b] for z in r]
            else:
                out[n] = [jnp.transpose(z[0])[None] for z in r] if n in _TRANSPOSED else r
        return out, r[1]

    res, done = update(rest, _swap_wait(flight[0], flight[1], flight[2], flight[3], vecs_out[4], scatter,
                                        "exchange_rest_wait"), "rest")
    got = _swap_wait(last[0], last[1], last[2], last[3], done, ("chipsum",) * 2, "exchange_first_wait")
    res_first, done = update(first, got, "first")
    res.update(res_first)
    sbuf = _swap_wait(vecs_out[0], vecs_out[1], vecs_out[2], vecs_out[3], done, ("all",), "exchange_vectors_wait")[0]
    rows = lambda d: [d[n].reshape(1, sz) for n, sz in VECS]
    small_res = _reduce_adamw_small(sbuf, rows(w), rows(m), rows(v))

    outs = []
    for k in range(4):
        piece = {n: r[k] for n, r in res.items()}
        for i, (n, _) in enumerate(VECS):
            piece[n] = small_res[k * len(VECS) + i].reshape(w[n].shape)
        outs.extend(piece[n] for n in names)
    return (small_res[-1][0, 0], dx[None], *outs)
```

```python
import jax
import jax.numpy as jnp
from jax import lax
from jax.experimental import pallas as pl
from jax.experimental.pallas import tpu as pltpu

F32 = jnp.float32
BF16 = jnp.bfloat16

D_MODEL = 1024
HEAD_DIM = 64
RW = 512
N_PAIR = RW // 128
SHIFT_COLS = 1792
IN_COLS = 3328
D_FF = 2816
FF_CHUNK = 256
NORM_EPS = 1e-6
GN_EPS = 64e-5
CHUNK = 64
SUB = 16
WKV_PASSES = 1
ATTN_PASSES = 1
ATTN_BLOCK = 128
DILATIONS = (1, 4, 16)
NEG = -1e30
ADAM_LR, ADAM_B1, ADAM_B2, ADAM_EPS, ADAM_WD, ADAM_STEP = 0.001, 0.9, 0.999, 1e-08, 0.01, 10
VMEM_LIMIT = 56 * 1024 * 1024
MESH = pl.DeviceIdType.MESH


def _params(sem=None, **kw):
    return pltpu.CompilerParams(dimension_semantics=sem, vmem_limit_bytes=VMEM_LIMIT, **kw)


def _dot(a, b):
    return lax.dot_general(a, b, (((1,), (0,)), ((), ())), preferred_element_type=F32)


def _dot_nt(a, b):
    return lax.dot_general(a, b, (((1,), (1,)), ((), ())), preferred_element_type=F32)


def _dot_tn(a, b):
    return lax.dot_general(a, b, (((0,), (0,)), ((), ())), preferred_element_type=F32)


_FORMS = {"nn": ((1,), (0,)), "nt": ((1,), (1,)), "tn": ((0,), (0,))}


def _dg(a, b, form):
    if a.ndim == 3 or b.ndim == 3:
        nb = a.shape[0] if a.ndim == 3 else b.shape[0]
        return jnp.stack([_dg(a[i] if a.ndim == 3 else a, b[i] if b.ndim == 3 else b, form) for i in range(nb)], axis=0)
    return lax.dot_general(a, b, (_FORMS[form], ((), ())), preferred_element_type=F32)


def _split2(x):
    hi = x.astype(BF16)
    return hi, (x - hi.astype(F32)).astype(BF16)


def _split3(x):
    hi = x.astype(BF16)
    rest = x - hi.astype(F32)
    mid = rest.astype(BF16)
    return hi, mid, (rest - mid.astype(F32)).astype(BF16)


def _mm_raw(a, b, form, mode):
    if mode == 1:
        return _dg(a.astype(BF16), b.astype(BF16), form)
    if mode == 3:
        ah, al = _split2(a)
        bh, bl = _split2(b)
        return _dg(ah, bh, form) + (_dg(ah, bl, form) + _dg(al, bh, form))
    if mode == "L3":
        ab = a.astype(BF16)
        b1, b2, b3 = _split3(b)
        if form == "nn":
            n = b.shape[-1]
            wide = _dg(ab, jnp.concatenate([b1, b2, b3], axis=-1), form)
            return wide[..., :n] + (wide[..., n:2 * n] + wide[..., 2 * n:])
        return _dg(ab, b1, form) + (_dg(ab, b2, form) + _dg(ab, b3, form))
    assert mode == "R3", mode
    bb = b.astype(BF16)
    a1, a2, a3 = _split3(a)
    if form in ("nn", "nt"):
        m = a.shape[-2]
        tall = _dg(jnp.concatenate([a1, a2, a3], axis=-2), bb, form)
        return tall[..., :m, :] + (tall[..., m:2 * m, :] + tall[..., 2 * m:, :])
    return _dg(a1, bb, form) + (_dg(a2, bb, form) + _dg(a3, bb, form))


def _mm(a, b, form, mode):
    @jax.custom_vjp
    def f(a, b):
        return _mm_raw(a, b, form, mode)

    def fwd(a, b):
        return _mm_raw(a, b, form, mode), (a, b)

    def bwd(res, ct):
        a, b = res
        la = {1: 1, 3: 3, "L3": None, "R3": "R3"}[mode]
        lb = {1: 1, 3: 3, "L3": "L3", "R3": None}[mode]
        if form == "nn":
            da = None if la is None else _mm_raw(ct, b, "nt", la)
            db = None if lb is None else _mm_raw(a, ct, "tn", lb)
        elif form == "nt":
            da = None if la is None else _mm_raw(ct, b, "nn", la)
            db = None if lb is None else _mm_raw(ct, a, "tn", "R3" if lb == "L3" else lb)
        else:
            da = None if la is None else _mm_raw(b, ct, "nt", "L3" if la == "R3" else la)
            db = None if lb is None else _mm_raw(a, ct, "nn", lb)
        return (jnp.zeros_like(a) if da is None else da, jnp.zeros_like(b) if db is None else db)

    f.defvjp(fwd, bwd)
    return f(a, b)


def _seg_ones(n):
    r = lax.broadcasted_iota(jnp.int32, (n, n), 0) // HEAD_DIM
    c = lax.broadcasted_iota(jnp.int32, (n, n), 1) // HEAD_DIM
    return (r == c).astype(F32)


def _segsum(x, seg):
    return _mm(x, seg, "nn", "R3")


def _rms_fwd(x, g):
    rstd = lax.rsqrt(jnp.mean(x * x, axis=-1, keepdims=True) + NORM_EPS)
    return x * rstd * g


def _rms_bwd(dy, x, g):
    rstd = lax.rsqrt(jnp.mean(x * x, axis=-1, keepdims=True) + NORM_EPS)
    xn = x * rstd
    dxn = dy * g
    dx = rstd * (dxn - xn * jnp.mean(dxn * xn, axis=-1, keepdims=True))
    return dx, dy * xn


def _sigmoid(x):
    return 1.0 / (1.0 + jnp.exp(-x))


def _softplus(x):
    return jnp.maximum(x, 0.0) + jnp.log(1.0 + jnp.exp(-jnp.abs(x)))


def _acc(ref, val, first):
    @pl.when(first)
    def _():
        ref[...] = val

    @pl.when(jnp.logical_not(first))
    def _():
        ref[...] += val


def _colsum8(v):
    rows, n = v.shape
    return jnp.sum(v.reshape(rows // 8, 8, n), axis=0)


def _prep_fn(p, pprev, mu, w0, w2p, a0, a2p, g2, k_k, k_a):
    seg = _seg_ones(RW)
    ps = p + (pprev - p) * mu
    r = ps[:, 0:RW]
    k = ps[:, RW:2 * RW]
    v = ps[:, 2 * RW:3 * RW]
    xwa = ps[:, 3 * RW:3 * RW + 128]
    xg = ps[:, 3 * RW + 128:3 * RW + 256]
    wraw = -_softplus(-(w0 + _mm(jnp.tanh(xwa), w2p, "nn", 3))) - 0.5
    lw = -jnp.exp(wraw)
    a = _sigmoid(a0 + _mm(xwa, a2p, "nn", 3))
    g = _mm(_sigmoid(xg), g2, "nn", 3)
    kk = k * k_k
    kk = kk / jnp.maximum(jnp.sqrt(_segsum(kk * kk, seg)), 1e-12)
    k2 = k * (1.0 + (a - 1.0) * k_a)
    return r, lw, k2, v, kk, a, g


def _transposed(z):
    return jnp.stack([z[i].T for i in range(z.shape[0])], axis=0) if z.ndim == 3 else z.T


def _solve_unit_lower(lmat, rhs):
    c = lmat.shape[-1]
    row = lax.broadcasted_iota(jnp.int32, (c, c), 0)
    col = lax.broadcasted_iota(jnp.int32, (c, c), 1)
    eye = (row == col).astype(F32)
    ld = jnp.where(row // SUB == col // SUB, lmat, 0.0)
    lo = lmat - ld
    x = eye + ld
    m = ld
    mm = lambda p, q: _mm(p, q, "nn", WKV_PASSES)
    cat = jnp.concatenate
    m = mm(m, m)
    for _ in range(2):
        mx = mm(m, cat([m, x], axis=-1))
        m, x = mx[..., :c], x + mx[..., c:]
    x = x + mm(m, x)
    gw = mm(x, cat([lo, rhs], axis=-1))
    g, w = gw[..., :c], gw[..., c:]
    gg = mm(g, cat([g, w], axis=-1))
    w = w + gg[..., c:]
    return w + mm(gg[..., :c], w)


def _wkv_chunk_fn(s0, r, lw, k, v, kk, a):
    c = r.shape[-2]
    n = 2 * c
    row = lax.broadcasted_iota(jnp.int32, (n, n), 0)
    col = lax.broadcasted_iota(jnp.int32, (n, n), 1)
    same = (row // c) == (col // c)
    incl = jnp.logical_and(row >= col, same)
    strict = jnp.logical_and(row > col, same)
    sel = (lax.broadcasted_iota(jnp.int32, (n, 128), 0) // c) == (lax.broadcasted_iota(jnp.int32, (n, 128), 1) // HEAD_DIM)
    two = lambda z: jnp.concatenate([z, z], axis=-2)
    lw2 = two(lw)
    mm = lambda p_, q_, form: _mm(p_, q_, form, WKV_PASSES)
    cl = _mm(incl.astype(F32), lw2, "nn", "L3")
    p = jnp.exp(cl)
    pinv = jnp.exp(-cl)
    pprev = jnp.exp(cl - lw2)
    kk2 = two(kk)
    at = jnp.where(sel, -kk2 * pprev, 0.0)
    bt = jnp.where(sel, kk2 * two(a) * pinv, 0.0)
    kt = jnp.where(sel, two(k) * pinv, 0.0)
    rt = jnp.where(sel, two(r) * p, 0.0)
    vt = jnp.where(sel, two(v), 0.0)
    cat = jnp.concatenate
    bk = cat([bt, kt], axis=-2)
    arbk = mm(cat([at, rt], axis=-2), bk, "nt")
    ab, ak = jnp.where(strict, arbk[..., :n, :n], 0.0), jnp.where(strict, arbk[..., :n, n:], 0.0)
    rb, rk = jnp.where(incl, arbk[..., n:, :n], 0.0), jnp.where(incl, arbk[..., n:, n:], 0.0)
    s0t = _transposed(s0)
    u = _solve_unit_lower(ab, mm(cat([at, ak], axis=-1), cat([s0t, vt], axis=-2), "nn"))
    y2 = mm(cat([rt, rb, rk], axis=-1), cat([s0t, u, vt], axis=-2), "nn")
    plast = jnp.exp(jnp.sum(lw, axis=-2, keepdims=True))
    s1 = (s0 + mm(cat([u, vt], axis=-2), bk, "tn")) * plast
    r2 = lax.broadcasted_iota(jnp.int32, (128, 128), 0) // HEAD_DIM
    c2 = lax.broadcasted_iota(jnp.int32, (128, 128), 1) // HEAD_DIM
    return y2[..., :c, :] + y2[..., c:, :], jnp.where(r2 == c2, s1, 0.0)


def _post_fn(y, r, k2, v, g, lnw, lnb, rk):
    seg = _seg_ones(RW)
    mean = _segsum(y, seg) * (1.0 / HEAD_DIM)
    yc = y - mean
    var = _segsum(yc * yc, seg) * (1.0 / HEAD_DIM)
    yn = yc * lax.rsqrt(var + GN_EPS)
    out = yn * lnw + lnb + _segsum(r * k2 * rk, seg) * v
    return out * g


def _attn_block_fn(q, kc, vc, kp=None, vp=None):
    n = ATTN_BLOCK
    qi = lax.broadcasted_iota(jnp.int32, (n, n), 0)
    kj = lax.broadcasted_iota(jnp.int32, (n, n), 1)
    lane = lax.broadcasted_iota(jnp.int32, (1, 128), 1)
    scale = HEAD_DIM ** -0.5
    valid = kj <= qi
    keys, vals = kc, vc
    if kp is not None:
        valid = jnp.concatenate([valid, kj >= qi], axis=-1)
        keys, vals = jnp.concatenate([kc, kp], axis=-2), jnp.concatenate([vc, vp], axis=-2)
    m0 = (lane // HEAD_DIM) == 0
    q2 = jnp.concatenate([jnp.where(m0, q, 0.0), jnp.where(m0, 0.0, q)], axis=-2)
    valid2 = jnp.concatenate([valid, valid], axis=-2)
    s = jnp.where(valid2, _mm(q2, keys, "nt", ATTN_PASSES) * scale, NEG)
    m = jnp.max(s, axis=-1, keepdims=True)
    p = jnp.exp(s - m)
    den = jnp.sum(p, axis=-1, keepdims=True)
    o2 = _mm(p, vals, "nn", ATTN_PASSES) / den
    l2 = m + jnp.log(den)
    return jnp.where(m0, o2[..., :n, :], o2[..., n:, :]), jnp.where(m0, l2[..., :n, :], l2[..., n:, :])


def _attn_block_bwd(q, kc, vc, kp, vp, o, lse, do, dl):
    n = ATTN_BLOCK
    cat = jnp.concatenate
    qi = lax.broadcasted_iota(jnp.int32, (n, n), 0)
    kj = lax.broadcasted_iota(jnp.int32, (n, n), 1)
    m0 = (lax.broadcasted_iota(jnp.int32, (1, 128), 1) // HEAD_DIM) == 0
    scale = HEAD_DIM ** -0.5
    valid = kj <= qi
    keys, vals = kc, vc
    if kp is not None:
        valid = cat([valid, kj >= qi], axis=-1)
        keys, vals = cat([kc, kp], axis=-2), cat([vc, vp], axis=-2)
    stack = lambda z: cat([jnp.where(m0, z, 0.0), jnp.where(m0, 0.0, z)], axis=-2)
    q2, do2 = stack(q), stack(do)
    lse2 = cat([jnp.max(jnp.where(m0, lse, NEG), axis=-1, keepdims=True),
                jnp.max(jnp.where(m0, NEG, lse), axis=-1, keepdims=True)], axis=-2)
    delta = jnp.sum(do2 * cat([o, o], axis=-2), axis=-1, keepdims=True)
    dlse = jnp.sum(stack(dl), axis=-1, keepdims=True)
    mm = lambda a, b, form: _mm_raw(a, b, form, ATTN_PASSES)
    s = jnp.where(cat([valid, valid], axis=-2), mm(q2, keys, "nt") * scale, NEG)
    p = jnp.exp(s - lse2)
    ds = p * (mm(do2, vals, "nt") - delta + dlse)
    dq2 = mm(ds, keys, "nn") * scale
    dq = jnp.where(m0, dq2[..., :n, :], dq2[..., n:, :])
    dkeys = mm(ds, q2, "tn") * scale
    dvals = mm(p, do2, "tn")
    if kp is None:
        return dq, dkeys, dvals
    return dq, dkeys[..., :n, :], dvals[..., :n, :], dkeys[..., n:, :], dvals[..., n:, :]


def _combine_fn(o1, o2, o3, l1, l2, l3, og):
    seg = _seg_ones(o1.shape[-1])
    m = jnp.maximum(jnp.maximum(l1, l2), l3)
    e1, e2, e3 = jnp.exp(l1 - m), jnp.exp(l2 - m), jnp.exp(l3 - m)
    o = (e1 * o1 + e2 * o2 + e3 * o3) / (e1 + e2 + e3)
    o = o * lax.rsqrt(_segsum(o * o, seg) * (1.0 / HEAD_DIM) + NORM_EPS)
    return o * og


def _shifted(p, last8, first):
    prow = jnp.where(first, 0.0, last8[7:8, :])
    rolled = pltpu.roll(p, 1, axis=0)
    rid = lax.broadcasted_iota(jnp.int32, p.shape, 0)
    return jnp.where(rid == 0, prow, rolled)


_PREP_TM = 256


def _prep_specs(tm):
    vec = lambda n: pl.BlockSpec((1, n), lambda i: (0, 0))
    mat = lambda r, n: pl.BlockSpec((r, n), lambda i: (0, 0))
    return [vec(SHIFT_COLS), vec(RW), mat(128, RW), vec(RW), mat(128, RW), mat(128, RW), vec(RW), vec(RW)]


def _in_proj_prep(x, g1, win, pw):
    t = x.shape[0]
    tm = _PREP_TM

    def body(x_ref, g_ref, w_ref, mu, w0, w2p, a0, a2p, g2, k_k, k_a, h_ref, pa_ref, qkv_ref, *rest):
        outs, carry = rest[:7], rest[7]

        @pl.when(pl.program_id(0) == 0)
        def _():
            carry[...] = jnp.zeros_like(carry)

        h = _rms_fwd(x_ref[...], g_ref[...]).astype(BF16)
        h_ref[...] = h
        proj = _dot_nt(h, w_ref[...])
        p = proj[:, :SHIFT_COLS]
        pa_ref[...] = p
        for j in range(3):
            for pr in range(N_PAIR):
                c0 = SHIFT_COLS + j * RW + pr * 128
                qkv_ref[j, pr] = proj[:, c0:c0 + 128]
        pprev = _shifted(p, carry[...], pl.program_id(0) == 0)
        carry[...] = p[tm - 8:, :]
        res = _prep_fn(p, pprev, mu[...], w0[...], w2p[...], a0[...], a2p[...], g2[...], k_k[...], k_a[...])
        for o_ref, val in zip(outs, res):
            o_ref[...] = val

    row = pl.BlockSpec((tm, RW), lambda i: (i, 0))
    return pl.pallas_call(
        body, name="in_proj_prep", grid=(t // tm,),
        in_specs=[pl.BlockSpec((tm, D_MODEL), lambda i: (i, 0)), pl.BlockSpec((1, D_MODEL), lambda i: (0, 0)),
                  pl.BlockSpec((IN_COLS, D_MODEL), lambda i: (0, 0))] + _prep_specs(tm),
        out_specs=[pl.BlockSpec((tm, D_MODEL), lambda i: (i, 0)), pl.BlockSpec((tm, SHIFT_COLS), lambda i: (i, 0)),
                   pl.BlockSpec((3, N_PAIR, tm, 128), lambda i: (0, 0, i, 0))] + [row] * 7,
        out_shape=[jax.ShapeDtypeStruct((t, D_MODEL), BF16), jax.ShapeDtypeStruct((t, SHIFT_COLS), F32),
                   jax.ShapeDtypeStruct((3, N_PAIR, t, 128), F32)] + [jax.ShapeDtypeStruct((t, RW), F32)] * 7,
        scratch_shapes=[pltpu.VMEM((8, SHIFT_COLS), F32)],
        compiler_params=_params(("arbitrary",)),
    )(x, g1, win, *pw)


def _pairs(ref):
    return jnp.stack([ref[:, 128 * p:128 * (p + 1)] for p in range(N_PAIR)], axis=0)


def _wkv_fwd(r, lw, k2, v, kk, a):
    t = r.shape[0]
    nc = t // CHUNK

    def body(r_ref, lw_ref, k_ref, v_ref, kk_ref, a_ref, y_ref, s_ref, st):
        @pl.when(pl.program_id(0) == 0)
        def _():
            st[...] = jnp.zeros_like(st)

        s0 = st[...]
        s_ref[0] = s0
        y, s1 = _wkv_chunk_fn(s0, *[_pairs(ref) for ref in (r_ref, lw_ref, k_ref, v_ref, kk_ref, a_ref)])
        for p in range(N_PAIR):
            y_ref[:, 128 * p:128 * (p + 1)] = y[p]
        st[...] = s1

    blk = pl.BlockSpec((CHUNK, RW), lambda c: (c, 0))
    return pl.pallas_call(
        body, name="wkv_fwd", grid=(nc,),
        in_specs=[blk] * 6,
        out_specs=[blk, pl.BlockSpec((1, N_PAIR, 128, 128), lambda c: (c, 0, 0, 0))],
        out_shape=[jax.ShapeDtypeStruct((t, RW), F32), jax.ShapeDtypeStruct((nc, N_PAIR, 128, 128), F32)],
        scratch_shapes=[pltpu.VMEM((N_PAIR, 128, 128), F32)],
        compiler_params=_params(("arbitrary",)),
    )(r, lw, k2, v, kk, a)


_POST_TM = 512


ATTN_GROUP = 2


def _dilated_rows(d, r, n):
    if d == 1:
        return pl.ds(pl.multiple_of(n * ATTN_BLOCK, ATTN_BLOCK), ATTN_BLOCK)
    return pl.ds(r + n * (ATTN_BLOCK * d), ATTN_BLOCK, stride=d)


def _for_each_sequence(t, unit):
    for di, d in enumerate(DILATIONS):

        @pl.when(pl.program_id(1) == di)
        def _(di=di, d=d):
            nb = t // (ATTN_BLOCK * d)
            if d == 1:
                unit(di, [(d, 0, 0)], False)
                unit(di, [(d, 0, 1)], True)
                lax.fori_loop(1, nb // 2, lambda k, c: (unit(di, [(d, 0, 2 * k), (d, 0, 2 * k + 1)], True), c)[1], 0)
            else:

                def residues(r, carry):
                    unit(di, [(d, r, 0), (d, r + d // 2, 0)], False)
                    if nb > 1:
                        lax.fori_loop(1, nb, lambda n, c: (unit(di, [(d, r, n), (d, r + d // 2, n)], True), c)[1], 0)
                    return carry

                lax.fori_loop(0, d // 2, residues, 0)


def _take(ref, lead, rows_list):
    return jnp.stack([ref.at[(*lead, g)][rows, :] for rows in rows_list for g in range(ref.shape[len(lead)])], axis=0)


def _put(ref, lead, rows_list, val, add=False):
    k = 0
    for rows in rows_list:
        for g in range(ref.shape[len(lead)]):
            if add:
                ref.at[(*lead, g)][rows, :] += val[k]
            else:
                ref.at[(*lead, g)][rows, :] = val[k]
            k += 1


def _attn_fwd(qkv):
    t = qkv.shape[2]

    def body(q_ref, k_ref, v_ref, o_ref, l_ref):
        def unit(di, places, has_prev):
            cur = [_dilated_rows(d, r, n) for d, r, n in places]
            args = [_take(ref, (0,), cur) for ref in (q_ref, k_ref, v_ref)]
            if has_prev:
                prv = [_dilated_rows(d, r, n - 1) for d, r, n in places]
                args += [_take(ref, (0,), prv) for ref in (k_ref, v_ref)]
            o, lse = _attn_block_fn(*args)
            _put(o_ref, (0,), cur, o)
            _put(l_ref, (0,), cur, lse)

        _for_each_sequence(t, unit)

    spec = lambda j: pl.BlockSpec((1, ATTN_GROUP, t, 128), lambda i, b: (j, i, 0, 0))
    out = pl.BlockSpec((1, ATTN_GROUP, t, 128), lambda i, b: (b, i, 0, 0))
    return pl.pallas_call(
        body, name="attn_fwd", grid=(N_PAIR // ATTN_GROUP, len(DILATIONS)),
        in_specs=[spec(0), spec(1), spec(2)], out_specs=[out, out],
        out_shape=[jax.ShapeDtypeStruct((3, N_PAIR, t, 128), F32)] * 2,
        compiler_params=_params(("parallel", "arbitrary")),
    )(qkv, qkv, qkv)


_COMB_TM = 512


def _mixers_out(y, r, k2, v, g, lnw, lnb, rk, o, l, og):
    t = y.shape[0]
    tm = _COMB_TM

    def body(y_ref, r_ref, k_ref, v_ref, g_ref, lnw_ref, lnb_ref, rk_ref, o_ref, l_ref, og_ref, out_ref):
        out_ref[:, :RW] = _post_fn(y_ref[...], r_ref[...], k_ref[...], v_ref[...], g_ref[...],
                                   lnw_ref[...], lnb_ref[...], rk_ref[...]).astype(BF16)
        for p in range(N_PAIR):
            cols = slice(128 * p, 128 * (p + 1))
            out_ref[:, RW + 128 * p:RW + 128 * (p + 1)] = _combine_fn(
                o_ref[0, p], o_ref[1, p], o_ref[2, p], l_ref[0, p], l_ref[1, p], l_ref[2, p], og_ref[:, cols]).astype(BF16)

    row = pl.BlockSpec((tm, RW), lambda i: (i, 0))
    vec = pl.BlockSpec((1, RW), lambda i: (0, 0))
    blk = pl.BlockSpec((3, N_PAIR, tm, 128), lambda i: (0, 0, i, 0))
    return pl.pallas_call(
        body, name="mixers_out", grid=(t // tm,),
        in_specs=[row] * 5 + [vec] * 3 + [blk, blk, vec], out_specs=pl.BlockSpec((tm, D_MODEL), lambda i: (i, 0)),
        out_shape=jax.ShapeDtypeStruct((t, D_MODEL), BF16),
        compiler_params=_params(("parallel",)),
    )(y, r, k2, v, g, lnw, lnb, rk, o, l, og)


def _ffn_all(x, ycat, wg, wu, wd, wout, g2, gf, tgt):
    t = x.shape[0]
    tm = 256

    def body(x_ref, y_ref, wg_ref, wu_ref, wd_ref, wo_ref, g2_ref, gf_ref, t_ref,
             h_ref, act_ref, dx2b_ref, dgt_ref, dup_ref, dx1b_ref, dx1_ref, dya_ref, dyb_ref, loss_ref, dgf_ref, dg2_ref,
             gt_s, up_s):
        first = pl.program_id(0) == 0
        x1 = x_ref[...] + _dot(y_ref[...], wo_ref[...])
        h = _rms_fwd(x1, g2_ref[...]).astype(BF16)
        h_ref[...] = h
        for c0 in range(0, D_FF, FF_CHUNK):
            cols = slice(c0, c0 + FF_CHUNK)
            gt = _dot_nt(h, wg_ref[cols, :])
            up = _dot_nt(h, wu_ref[cols, :])
            gt_s[:, cols] = gt.astype(BF16)
            up_s[:, cols] = up.astype(BF16)
            act_ref[:, cols] = (gt * _sigmoid(gt) * up).astype(BF16)
        x2 = x1 + _dot(act_ref[...], wd_ref[...])
        gf_ = gf_ref[...]
        diff = _rms_fwd(x2, gf_) - t_ref[...]
        lrow = 0.5 * jnp.sum(_colsum8(diff * diff), axis=1, keepdims=True) * (1.0 / D_MODEL)
        _acc(loss_ref, jnp.broadcast_to(lrow, (8, 128)), first)
        dx2, dgr = _rms_bwd(diff * (1.0 / D_MODEL), x2, gf_)
        _acc(dgf_ref, _colsum8(dgr), first)
        dx2b = dx2.astype(BF16)
        dx2b_ref[...] = dx2b
        for c0 in range(0, D_FF, FF_CHUNK):
            cols = slice(c0, c0 + FF_CHUNK)
            dact = _dot_nt(dx2b, wd_ref[cols, :])
            gt = gt_s[:, cols].astype(F32)
            sg = _sigmoid(gt)
            dgt_ref[:, cols] = (dact * up_s[:, cols].astype(F32) * sg * (1.0 + gt * (1.0 - sg))).astype(BF16)
            dup_ref[:, cols] = (dact * gt * sg).astype(BF16)
        dh = _dot(dgt_ref[...], wg_ref[...]) + _dot(dup_ref[...], wu_ref[...])
        dxn, dgr2 = _rms_bwd(dh, x1, g2_ref[...])
        _acc(dg2_ref, _colsum8(dgr2), first)
        dx1 = dx2 + dxn
        dx1_ref[...] = dx1
        dx1b = dx1.astype(BF16)
        dx1b_ref[...] = dx1b
        dy = _dot_nt(dx1b, wo_ref[...])
        dya_ref[...] = dy[:, :RW]
        dyb_ref[...] = dy[:, RW:]

    row = pl.BlockSpec((tm, D_MODEL), lambda i: (i, 0))
    wide = pl.BlockSpec((tm, D_FF), lambda i: (i, 0))
    half = pl.BlockSpec((tm, RW), lambda i: (i, 0))
    wsp = pl.BlockSpec((D_FF, D_MODEL), lambda i: (0, 0))
    vec = pl.BlockSpec((1, D_MODEL), lambda i: (0, 0))
    part = pl.BlockSpec((8, D_MODEL), lambda i: (0, 0))
    bf = lambda n: jax.ShapeDtypeStruct((t, n), BF16)
    return pl.pallas_call(
        body, name="ffn_all", grid=(t // tm,),
        in_specs=[row, row, wsp, wsp, wsp, pl.BlockSpec((D_MODEL, D_MODEL), lambda i: (0, 0)), vec, vec, row],
        out_specs=[row, wide, row, wide, wide, row, row, half, half, pl.BlockSpec((8, 128), lambda i: (0, 0)), part, part],
        out_shape=[bf(D_MODEL), bf(D_FF), bf(D_MODEL), bf(D_FF), bf(D_FF), bf(D_MODEL),
                   jax.ShapeDtypeStruct((t, D_MODEL), F32), jax.ShapeDtypeStruct((t, RW), F32),
                   jax.ShapeDtypeStruct((t, RW), F32), jax.ShapeDtypeStruct((8, 128), F32),
                   jax.ShapeDtypeStruct((8, D_MODEL), F32), jax.ShapeDtypeStruct((8, D_MODEL), F32)],
        scratch_shapes=[pltpu.VMEM((tm, D_FF), BF16), pltpu.VMEM((tm, D_FF), BF16)],
        compiler_params=_params(("arbitrary",)),
    )(x, ycat, wg, wu, wd, wout, g2, gf, tgt)


def _wgrad(a, b, tk, tn, name):
    t, kdim = a.shape
    ndim = b.shape[1]

    def body(a_ref, b_ref, o_ref):
        o_ref[...] = _dot_tn(a_ref[...], b_ref[...]).astype(BF16)

    return pl.pallas_call(
        body, name=name, grid=(kdim // tk, ndim // tn),
        in_specs=[pl.BlockSpec((t, tk), lambda i, j: (0, i)), pl.BlockSpec((t, tn), lambda i, j: (0, j))],
        out_specs=pl.BlockSpec((tk, tn), lambda i, j: (i, j)),
        out_shape=jax.ShapeDtypeStruct((kdim, ndim), BF16),
        compiler_params=_params(("parallel", "parallel")),
    )(a, b)


def _post_bwd(dya, y, r, k2, v, g, lnw, lnb, rk):
    t = y.shape[0]
    tm = _POST_TM

    def body(d_ref, y_ref, r_ref, k_ref, v_ref, g_ref, lnw_ref, lnb_ref, rk_ref,
             dy_ref, dr_ref, dk_ref, dv_ref, dg_ref, dlnw_ref, dlnb_ref, drk_ref):
        first = pl.program_id(0) == 0
        ones = jnp.ones((tm, 1), F32)
        prim = (y_ref[...], r_ref[...], k_ref[...], v_ref[...], g_ref[...],
                ones * lnw_ref[...], ones * lnb_ref[...], ones * rk_ref[...])
        _, vjp = jax.vjp(_post_fn, *prim)
        dy, dr, dk, dv, dg, dlnw, dlnb, drk = vjp(d_ref[...])
        dy_ref[...] = dy
        dr_ref[...] = dr
        dk_ref[...] = dk
        dv_ref[...] = dv
        dg_ref[...] = dg
        _acc(dlnw_ref, _colsum8(dlnw), first)
        _acc(dlnb_ref, _colsum8(dlnb), first)
        _acc(drk_ref, _colsum8(drk), first)

    row = pl.BlockSpec((tm, RW), lambda i: (i, 0))
    vec = pl.BlockSpec((1, RW), lambda i: (0, 0))
    part = pl.BlockSpec((8, RW), lambda i: (0, 0))
    return pl.pallas_call(
        body, name="rwkv_post_bwd", grid=(t // tm,),
        in_specs=[row] * 6 + [vec] * 3, out_specs=[row] * 5 + [part] * 3,
        out_shape=[jax.ShapeDtypeStruct((t, RW), F32)] * 5 + [jax.ShapeDtypeStruct((8, RW), F32)] * 3,
        compiler_params=_params(("arbitrary",)),
    )(dya, y, r, k2, v, g, lnw, lnb, rk)


def _wkv_bwd(dy, s0s, r, lw, k2, v, kk, a):
    t = r.shape[0]
    nc = t // CHUNK

    def body(dy_ref, s_ref, r_ref, lw_ref, k_ref, v_ref, kk_ref, a_ref,
             dr_ref, dlw_ref, dk_ref, dv_ref, dkk_ref, da_ref, ds):
        @pl.when(pl.program_id(0) == 0)
        def _():
            ds[...] = jnp.zeros_like(ds)

        _, vjp = jax.vjp(_wkv_chunk_fn, s_ref[0],
                         *[_pairs(ref) for ref in (r_ref, lw_ref, k_ref, v_ref, kk_ref, a_ref)])
        res = vjp((_pairs(dy_ref), ds[...]))
        ds[...] = res[0]
        for ref, val in zip((dr_ref, dlw_ref, dk_ref, dv_ref, dkk_ref, da_ref), res[1:]):
            for p in range(N_PAIR):
                ref[:, 128 * p:128 * (p + 1)] = val[p]

    blk = pl.BlockSpec((CHUNK, RW), lambda c: (nc - 1 - c, 0))
    return pl.pallas_call(
        body, name="wkv_bwd", grid=(nc,),
        in_specs=[blk, pl.BlockSpec((1, N_PAIR, 128, 128), lambda c: (nc - 1 - c, 0, 0, 0))] + [blk] * 6,
        out_specs=[blk] * 6,
        out_shape=[jax.ShapeDtypeStruct((t, RW), F32)] * 6,
        scratch_shapes=[pltpu.VMEM((N_PAIR, 128, 128), F32)],
        compiler_params=_params(("arbitrary",)),
    )(dy, s0s, r, lw, k2, v, kk, a)


def _prep_in_proj_bwd(proj, pw, douts, dq, dk, dv, win, x, g1, dx1):
    t = proj.shape[0]
    tm = _PREP_TM
    nt = t // tm

    def body(p_ref, l8_ref, mu, w0, w2p, a0, a2p, g2, k_k, k_a, dr, dr2, dlw, dk2, dk22, dv, dv2, dkk, da, dg,
             dq_ref, dkq_ref, dvq_ref, w_ref, x_ref, g1_ref, dx1_ref,
             dproj_ref, dx_ref, dg1_ref, dmu_ref, dw0_ref, dw2_ref, da0_ref, da2_ref, dg2_ref, dkk_ref, dka_ref, carry):
        i = pl.program_id(0)
        first = i == 0

        @pl.when(first)
        def _():
            carry[...] = jnp.zeros_like(carry)

        p = p_ref[...]
        pprev = _shifted(p, l8_ref[...], i == nt - 1)
        ones = jnp.ones((tm, 1), F32)
        prim = (p, pprev, ones * mu[...], ones * w0[...], w2p[...], ones * a0[...], a2p[...], g2[...],
                ones * k_k[...], ones * k_a[...])
        _, vjp = jax.vjp(_prep_fn, *prim)
        dp, dpp, dmu, dw0, dw2, da0, da2, dg2, dkk_, dka = vjp(
            (dr[...] + dr2[...], dlw[...], dk2[...] + dk22[...], dv[...] + dv2[...], dkk[...], da[...], dg[...]))
        up = pltpu.roll(dpp, tm - 1, axis=0)
        rid = lax.broadcasted_iota(jnp.int32, dpp.shape, 0)
        dpa = dp + jnp.where(rid == tm - 1, carry[0:1, :], up)
        carry[...] = jnp.broadcast_to(dpp[0:1, :], carry.shape)
        _acc(dmu_ref, _colsum8(dmu), first)
        _acc(dw0_ref, _colsum8(dw0), first)
        _acc(dw2_ref, dw2, first)
        _acc(da0_ref, _colsum8(da0), first)
        _acc(da2_ref, da2, first)
        _acc(dg2_ref, dg2, first)
        _acc(dkk_ref, _colsum8(dkk_), first)
        _acc(dka_ref, _colsum8(dka), first)
        parts = [dpa] + [ref[pr] for ref in (dq_ref, dkq_ref, dvq_ref) for pr in range(N_PAIR)]
        dproj = jnp.concatenate([z.astype(BF16) for z in parts], axis=1)
        dproj_ref[...] = dproj
        dxn, dgr = _rms_bwd(_dot(dproj, w_ref[...]), x_ref[...], g1_ref[...])
        dx_ref[...] = dx1_ref[...] + dxn
        _acc(dg1_ref, _colsum8(dgr), first)

    rev = lambda i: (nt - 1 - i, 0)
    row = pl.BlockSpec((tm, RW), rev)
    wide = pl.BlockSpec((tm, D_MODEL), rev)
    pair = pl.BlockSpec((N_PAIR, tm, 128), lambda i: (0, nt - 1 - i, 0))
    part = lambda n: pl.BlockSpec((8, n), lambda i: (0, 0))
    mat = pl.BlockSpec((128, RW), lambda i: (0, 0))
    return pl.pallas_call(
        body, name="prep_in_proj_bwd", grid=(nt,),
        in_specs=[pl.BlockSpec((tm, SHIFT_COLS), rev),
                  pl.BlockSpec((8, SHIFT_COLS), lambda i: (jnp.maximum((nt - 1 - i) * (tm // 8) - 1, 0), 0))]
                 + _prep_specs(tm) + [row] * 10
                 + [pair] * 3 + [pl.BlockSpec((IN_COLS, D_MODEL), lambda i: (0, 0)), wide,
                                 pl.BlockSpec((1, D_MODEL), lambda i: (0, 0)), wide],
        out_specs=[pl.BlockSpec((tm, IN_COLS), rev), wide, part(D_MODEL), part(SHIFT_COLS), part(RW), mat, part(RW), mat,
                   mat, part(RW), part(RW)],
        out_shape=[jax.ShapeDtypeStruct((t, IN_COLS), BF16), jax.ShapeDtypeStruct((t, D_MODEL), F32),
                   jax.ShapeDtypeStruct((8, D_MODEL), F32), jax.ShapeDtypeStruct((8, SHIFT_COLS), F32),
                   jax.ShapeDtypeStruct((8, RW), F32), jax.ShapeDtypeStruct((128, RW), F32),
                   jax.ShapeDtypeStruct((8, RW), F32), jax.ShapeDtypeStruct((128, RW), F32),
                   jax.ShapeDtypeStruct((128, RW), F32), jax.ShapeDtypeStruct((8, RW), F32),
                   jax.ShapeDtypeStruct((8, RW), F32)],
        scratch_shapes=[pltpu.VMEM((8, SHIFT_COLS), F32)],
        compiler_params=_params(("arbitrary",)),
    )(proj, proj, *pw, *douts, dq, dk, dv, win, x, g1, dx1)


def _combine_bwd(dyb, o, l, og):
    t = dyb.shape[0]
    tm = _COMB_TM

    def body(d_ref, o_ref, l_ref, og_ref, do_ref, dl_ref, dog_ref):
        ones = jnp.ones((tm, 1), F32)
        dog = []
        for p in range(N_PAIR):
            cols = slice(128 * p, 128 * (p + 1))
            _, vjp = jax.vjp(_combine_fn, o_ref[0, p], o_ref[1, p], o_ref[2, p], l_ref[0, p], l_ref[1, p], l_ref[2, p],
                             ones * og_ref[:, cols])
            res = vjp(d_ref[:, cols])
            for b in range(3):
                do_ref[b, p] = res[b]
                dl_ref[b, p] = res[3 + b]
            dog.append(_colsum8(res[6]))
        _acc(dog_ref, jnp.concatenate(dog, axis=1), pl.program_id(0) == 0)

    blk = pl.BlockSpec((3, N_PAIR, tm, 128), lambda i: (0, 0, i, 0))
    return pl.pallas_call(
        body, name="attn_combine_bwd", grid=(t // tm,),
        in_specs=[pl.BlockSpec((tm, RW), lambda i: (i, 0)), blk, blk, pl.BlockSpec((1, RW), lambda i: (0, 0))],
        out_specs=[blk, blk, pl.BlockSpec((8, RW), lambda i: (0, 0))],
        out_shape=[jax.ShapeDtypeStruct((3, N_PAIR, t, 128), F32)] * 2 + [jax.ShapeDtypeStruct((8, RW), F32)],
        compiler_params=_params(("arbitrary",)),
    )(dyb, o, l, og)


def _attn_bwd(do, dl, o, lse, qkv):
    t = qkv.shape[2]

    def body(do_ref, dl_ref, o_ref, l_ref, q_ref, k_ref, v_ref, dq_ref, dk_ref, dv_ref):
        @pl.when(pl.program_id(1) == 0)
        def _():
            for ref in (dq_ref, dk_ref, dv_ref):
                ref[...] = jnp.zeros_like(ref)

        def unit(di, places, has_prev):
            cur = [_dilated_rows(d, r, n) for d, r, n in places]
            q, kc, vc = [_take(ref, (0,), cur) for ref in (q_ref, k_ref, v_ref)]
            kp = vp = None
            if has_prev:
                prv = [_dilated_rows(d, r, n - 1) for d, r, n in places]
                kp, vp = [_take(ref, (0,), prv) for ref in (k_ref, v_ref)]
            res = _attn_block_bwd(q, kc, vc, kp, vp, *[_take(ref, (0,), cur) for ref in (o_ref, l_ref, do_ref, dl_ref)])
            _put(dq_ref, (), cur, res[0], add=True)
            _put(dk_ref, (), cur, res[1], add=True)
            _put(dv_ref, (), cur, res[2], add=True)
            if has_prev:
                _put(dk_ref, (), prv, res[3], add=True)
                _put(dv_ref, (), prv, res[4], add=True)

        _for_each_sequence(t, unit)

    spec = lambda j: pl.BlockSpec((1, ATTN_GROUP, t, 128), lambda i, b: (j, i, 0, 0))
    branch = pl.BlockSpec((1, ATTN_GROUP, t, 128), lambda i, b: (b, i, 0, 0))
    out = pl.BlockSpec((ATTN_GROUP, t, 128), lambda i, b: (i, 0, 0))
    return pl.pallas_call(
        body, name="attn_bwd", grid=(N_PAIR // ATTN_GROUP, len(DILATIONS)),
        in_specs=[branch] * 4 + [spec(0), spec(1), spec(2)], out_specs=[out] * 3,
        out_shape=[jax.ShapeDtypeStruct((N_PAIR, t, 128), F32)] * 3,
        compiler_params=_params(("parallel", "arbitrary")),
    )(do, dl, o, lse, qkv, qkv, qkv)


def _pad_lora(w, lo):
    z = jnp.zeros((64, RW), F32)
    return jnp.concatenate([w, z], axis=0) if lo == 0 else jnp.concatenate([z, w], axis=0)


def _local_step(x, tgt, win, vecs, w2, a2, g2m, get_rest, send_rest):
    pw = (vecs["mu_shift"], vecs["decay_w0"], _pad_lora(w2, 0), vecs["iclr_a0"], _pad_lora(a2, 64), g2m,
          vecs["k_k"], vecs["k_a"])
    h, proj, qkv, r, lw, k2, v, kk, a, g = _in_proj_prep(x, vecs["mix_norm_g"], win, pw)
    y, s0s = _wkv_fwd(r, lw, k2, v, kk, a)
    o_att, l_att = _attn_fwd(qkv)
    ycat = _mixers_out(y, r, k2, v, g, vecs["ln_x_w"], vecs["ln_x_b"], vecs["r_k"], o_att, l_att, vecs["attn_out_g"])
    wout, wg, wu, wd = get_rest(ycat)
    h2, act, dx2b, dgt, dup, dx1b, dx1, dya, dyb, loss8, dgf, dg2n = _ffn_all(
        x, ycat, wg, wu, wd, wout, vecs["ffn_norm_g"], vecs["final_norm_g"], tgt)
    gw = {
        "w_down": _wgrad(act, dx2b, 1408, 1024, "wgrad_down"),
        "w_gate": _wgrad(dgt, h2, 1408, 1024, "wgrad_gate"),
        "w_up": _wgrad(dup, h2, 1408, 1024, "wgrad_up"),
        "w_out": _wgrad(ycat, dx1b, 1024, 1024, "wgrad_out"),
    }

    lnw = vecs["ln_x_w"] + send_rest(gw)[0, 0]
    dy, dr_p, dk2_p, dv_p, dg, dlnw, dlnb, drk = _post_bwd(dya, y, r, k2, v, g, lnw, vecs["ln_x_b"], vecs["r_k"])
    dr_s, dlw, dk2_s, dv_s, dkk, da = _wkv_bwd(dy, s0s, r, lw, k2, v, kk, a)
    do_att, dl_att, dog = _combine_bwd(dyb, o_att, l_att, vecs["attn_out_g"])
    dq, dk, dv = _attn_bwd(do_att, dl_att, o_att, l_att, qkv)
    dproj, dx, dg1, dmu, dw0, dw2p, da0, da2p, dg2m, dk_k, dk_a = _prep_in_proj_bwd(
        proj, pw, (dr_p, dr_s, dlw, dk2_p, dk2_s, dv_p, dv_s, dkk, da, dg), dq, dk, dv, win, x, vecs["mix_norm_g"], dx1)
    gw["w_in"] = _wgrad(dproj, h, 1664, 1024, "wgrad_in")
    gw["decay_w2"] = dw2p[:64]
    gw["iclr_a2"] = da2p[64:]
    gw["gate_g2"] = dg2m
    gv = {"mix_norm_g": dg1, "mu_shift": dmu, "decay_w0": dw0, "iclr_a0": da0, "k_k": dk_k, "k_a": dk_a, "r_k": drk,
          "ln_x_w": dlnw, "ln_x_b": dlnb, "attn_out_g": dog, "ffn_norm_g": dg2n, "final_norm_g": dgf}
    return loss8, dx, gw, gv


N_CHIP = 4
N_DEV = 8
MATS = ("w_in", "w_out", "w_gate", "w_up", "w_down")
LORAS = ("decay_w2", "iclr_a2", "gate_g2")
VECS = (("mix_norm_g", 1024), ("mu_shift", 1792), ("decay_w0", 512), ("iclr_a0", 512), ("k_k", 512), ("k_a", 512),
        ("r_k", 512), ("ln_x_w", 512), ("ln_x_b", 512), ("attn_out_g", 512), ("ffn_norm_g", 1024),
        ("final_norm_g", 1024))
N_VEC = sum(n for _, n in VECS)
N_SMALL = N_VEC + 128
ANY = pl.BlockSpec(memory_space=pl.ANY)


def _flip(v, f):
    return 1 - v if f else v


class _Me:
    def __init__(self, mode):
        x, y, c = lax.axis_index("x"), lax.axis_index("y"), lax.axis_index("c")
        self.core, self.chip, self.dev = c, 2 * x + y, 4 * x + 2 * y + c
        self.sibling = (x, y, 1 - c)
        if mode == "chips":
            self.peers = [(px, py, c) for px, py in ((1 - x, y), (x, 1 - y), (1 - x, 1 - y))]
        else:
            self.peers = [(_flip(x, k & 4), _flip(y, k & 2), _flip(c, k & 1)) for k in range(1, N_DEV)]


def _half(core, rows):
    h = rows // 2
    return pl.ds(pl.multiple_of(core * h, h), h)


_BY_CHIP = ("gather", "whole", "chipsum")


def _peer_copy(srcs, dsts, kinds, send_sems, recv_sems, me, j, i, incoming):
    px, py, pc = me.peers[j]
    pchip, pdev = 2 * px + py, 4 * px + 2 * py + pc
    src, dst, kind = srcs[i], dsts[i], kinds[i]
    if kind in ("gather", "whole"):
        rows = _half(me.core, src.shape[1]) if kind == "gather" else pl.ds(0, src.shape[1])
        src, dst = src.at[me.chip, rows], dst.at[pchip if incoming else me.chip, rows]
    elif kind == "scatter":
        src, dst = src.at[pchip, _half(pc, src.shape[1])], dst.at[pdev if incoming else me.dev]
    elif kind == "chipsum":
        src, dst = src.at[pchip], dst.at[pchip if incoming else me.chip]
    else:
        dst = dst.at[pdev if incoming else me.dev]
    n = len(srcs)
    return pltpu.make_async_remote_copy(src_ref=src, dst_ref=dst, send_sem=send_sems.at[n * j + i],
                                        recv_sem=recv_sems.at[n * j + i], device_id=(px, py, pc), device_id_type=MESH)


def _mode(kinds):
    return "chips" if kinds[0] in _BY_CHIP else "devs"


def _npeer(kinds):
    return N_CHIP - 1 if kinds[0] in _BY_CHIP else N_DEV - 1


def _sibling_halves(gs, name):
    n = len(gs)

    def body(*refs):
        srcs, dsts, send_sems, recv_sems = refs[:n], refs[n:2 * n], refs[2 * n], refs[2 * n + 1]
        me = _Me("chips")

        def copy(i, p):
            return pltpu.make_async_remote_copy(
                src_ref=srcs[i].at[p, _half(1 - me.core, srcs[i].shape[1])], dst_ref=dsts[i].at[p],
                send_sem=send_sems.at[N_CHIP * i + p], recv_sem=recv_sems.at[N_CHIP * i + p],
                device_id=me.sibling, device_id_type=MESH)

        copies = [copy(i, p) for i in range(n) for p in range(N_CHIP)]
        for cp in copies:
            cp.start()
        for cp in copies:
            cp.wait()

    return pl.pallas_call(
        body, name=name, in_specs=[ANY] * n, out_specs=[ANY] * n,
        out_shape=[jax.ShapeDtypeStruct((N_CHIP, g.shape[1] // 2, g.shape[2]), g.dtype) for g in gs],
        scratch_shapes=[pltpu.SemaphoreType.DMA((N_CHIP * n,)), pltpu.SemaphoreType.DMA((N_CHIP * n,))],
    )(*gs)


def _add_halves(g, other, core, tr, name):
    _, h, cols = other.shape

    def body(core_ref, g_ref, o_ref, out_ref):
        out_ref[...] = (g_ref[...].astype(F32) + o_ref[...].astype(F32)).astype(BF16)

    blk = lambda off: pl.BlockSpec((1, tr, cols), lambda p, i, core_ref: (p, core_ref[0] * (h // tr) * off + i, 0))
    return pl.pallas_call(
        body, name=name,
        grid_spec=pltpu.PrefetchScalarGridSpec(num_scalar_prefetch=1, grid=(N_CHIP, h // tr),
                                               in_specs=[blk(1), blk(0)], out_specs=blk(0)),
        out_shape=jax.ShapeDtypeStruct(other.shape, BF16),
        compiler_params=_params(("parallel", "parallel")),
    )(core, g, other)


def _swap_gathered(lands, name):
    n = len(lands)

    def body(*refs):
        dsts, send_sems, recv_sems = refs[n:2 * n], refs[2 * n], refs[2 * n + 1]
        me = _Me("chips")

        def copy(j, i, incoming):
            px, py, _ = me.peers[j]
            rows_out, rows_in = _half(me.core, dsts[i].shape[1]), _half(1 - me.core, dsts[i].shape[1])
            return pltpu.make_async_remote_copy(
                src_ref=dsts[i].at[2 * px + py, rows_out], dst_ref=dsts[i].at[2 * px + py, rows_in if incoming else rows_out],
                send_sem=send_sems.at[n * j + i], recv_sem=recv_sems.at[n * j + i], device_id=me.sibling, device_id_type=MESH)

        sends = [copy(j, i, False) for j in range(3) for i in range(n)]
        for cp in sends:
            cp.start()
        for j in range(3):
            for i in range(n):
                copy(j, i, True).wait_recv()
        for cp in sends:
            cp.wait_send()

    return pl.pallas_call(
        body, name=name, in_specs=[ANY] * n, out_specs=[ANY] * n,
        out_shape=[jax.ShapeDtypeStruct(l.shape, l.dtype) for l in lands],
        input_output_aliases={i: i for i in range(n)},
        scratch_shapes=[pltpu.SemaphoreType.DMA((3 * n,)), pltpu.SemaphoreType.DMA((3 * n,))],
    )(*lands)


def _join_halves(sums, name):
    n = len(sums)

    def body(*refs):
        dsts, send_sems, recv_sems = refs[n:2 * n], refs[2 * n], refs[2 * n + 1]
        me = _Me("chips")

        def copy(i, incoming):
            mine, other = _half(me.core, dsts[i].shape[0]), _half(1 - me.core, dsts[i].shape[0])
            return pltpu.make_async_remote_copy(src_ref=dsts[i].at[mine], dst_ref=dsts[i].at[other if incoming else mine],
                                                send_sem=send_sems.at[i], recv_sem=recv_sems.at[i],
                                                device_id=me.sibling, device_id_type=MESH)

        sends = [copy(i, False) for i in range(n)]
        for cp in sends:
            cp.start()
        for i in range(n):
            copy(i, True).wait_recv()
        for cp in sends:
            cp.wait_send()

    return pl.pallas_call(
        body, name=name, in_specs=[ANY] * n, out_specs=[ANY] * n,
        out_shape=[jax.ShapeDtypeStruct(s.shape, s.dtype) for s in sums],
        input_output_aliases={i: i for i in range(n)},
        scratch_shapes=[pltpu.SemaphoreType.DMA((n,)), pltpu.SemaphoreType.DMA((n,))],
    )(*sums)


HBM = pl.BlockSpec(memory_space=pltpu.HBM)
SEM = pl.BlockSpec(memory_space=pltpu.SEMAPHORE)
EFFECT = pltpu.SideEffectType.DATAFLOW_SIDE_EFFECTING


def _swap_start(arrs, lands, kinds, name):
    n = len(lands)
    ops = list(lands) if arrs is None else [*arrs, *lands]
    k = len(ops)

    def body(*refs):
        srcs, dsts, send_sems, recv_sems, token = refs[:n], refs[k - n:k], refs[k], refs[k + 1], refs[-1]
        me = _Me(_mode(kinds))
        for j in range(len(me.peers)):
            for i in range(n):
                _peer_copy(srcs, dsts, kinds, send_sems, recv_sems, me, j, i, False).start()
        token[...] = jnp.zeros_like(token)

    ns = _npeer(kinds) * n
    outs = pl.pallas_call(
        body, name=name,
        out_shape=(pltpu.SemaphoreType.DMA((ns,)), pltpu.SemaphoreType.DMA((ns,)),
                   *[pltpu.HBM(a.shape, a.dtype) for a in ops], jax.ShapeDtypeStruct((8, 128), F32)),
        in_specs=[HBM] * k, out_specs=(SEM, SEM, *[HBM] * k, pl.BlockSpec(memory_space=pltpu.VMEM)),
        input_output_aliases={i: 2 + i for i in range(k)},
        compiler_params=pltpu.CompilerParams(has_side_effects=EFFECT),
    )(*[pltpu.with_memory_space_constraint(a, pltpu.HBM) for a in ops])
    return outs[0], outs[1], outs[2:2 + k - n], outs[2 + k - n:2 + k], outs[-1]


def _swap_wait(send_sems, recv_sems, srcs_thru, lands_thru, after, kinds, name):
    n = len(lands_thru)
    ops = [*srcs_thru, *lands_thru]
    k = len(ops)

    def body(*refs):
        srcs, dsts, s_sems, r_sems = refs[:n], refs[k - n:k], refs[k], refs[k + 1]
        me = _Me(_mode(kinds))
        for j in range(len(me.peers)):
            for i in range(n):
                cp = _peer_copy(srcs, dsts, kinds, s_sems, r_sems, me, j, i, True)
                cp.wait_send()
                cp.wait_recv()

    outs = pl.pallas_call(
        body, name=name,
        out_shape=tuple(pltpu.HBM(a.shape, a.dtype) for a in ops),
        in_specs=[HBM] * k + [SEM, SEM, ANY], out_specs=tuple([HBM] * k),
        input_output_aliases={i: i for i in range(k)},
        compiler_params=pltpu.CompilerParams(has_side_effects=EFFECT),
    )(*ops, send_sems, recv_sems, after)
    return outs[k - n:]


def _adamw(w, g, m, v):
    m = ADAM_B1 * m + (1.0 - ADAM_B1) * g
    v = ADAM_B2 * v + (1.0 - ADAM_B2) * (g * g)
    m_hat = m / (1.0 - ADAM_B1 ** ADAM_STEP)
    v_hat = v / (1.0 - ADAM_B2 ** ADAM_STEP)
    delta = -ADAM_LR * (m_hat / (jnp.sqrt(v_hat) + ADAM_EPS) + ADAM_WD * w)
    return delta, m, v


def _reduce8(rbuf, core, tr, name):
    slots, h, cols = rbuf.shape

    def body(core_ref, r_ref, g_ref):
        g = r_ref[0].astype(F32)
        for s in range(1, slots):
            g = g + r_ref[s].astype(F32)
        g_ref[...] = g

    return pl.pallas_call(
        body, name=name,
        grid_spec=pltpu.PrefetchScalarGridSpec(
            num_scalar_prefetch=1, grid=(h // tr,),
            in_specs=[pl.BlockSpec((slots, tr, cols), lambda i, core_ref: (0, i, 0))],
            out_specs=pl.BlockSpec((tr, cols), lambda i, core_ref: (core_ref[0] * (h // tr) + i, 0))),
        out_shape=jax.ShapeDtypeStruct((2 * h, cols), F32),
        compiler_params=_params(("parallel",)),
    )(core, rbuf)


def _adamw_call(g, w, m, v, tr, name):
    _, rows, cols = w.shape

    def body(g_in, w_ref, m_ref, v_ref, g_ref, d_ref, nm_ref, nv_ref):
        g = g_in[...]
        g_ref[0] = g
        d_ref[0], nm_ref[0], nv_ref[0] = _adamw(w_ref[0], g, m_ref[0], v_ref[0])

    row = pl.BlockSpec((1, tr, cols), lambda i: (0, i, 0))
    return pl.pallas_call(
        body, name=name, grid=(rows // tr,),
        in_specs=[pl.BlockSpec((tr, cols), lambda i: (i, 0)), row, row, row], out_specs=[row] * 4,
        out_shape=[jax.ShapeDtypeStruct(w.shape, F32)] * 4,
        compiler_params=_params(("parallel",)),
    )(g, w, m, v)


def _rowsum_small(parts, loss8):
    def body(*refs):
        out = refs[-1]
        c0 = 0
        for ref in refs[:-1]:
            n = ref.shape[1]
            out[:, c0:c0 + n] = jnp.sum(ref[...], axis=0, keepdims=True)
            c0 += n

    return pl.pallas_call(body, name="rowsum_small", out_shape=jax.ShapeDtypeStruct((1, N_SMALL), F32))(*parts, loss8)


def _reduce_adamw_small(sbuf, ws, ms, vs):
    nv = len(ws)

    def body(*refs):
        s_ref, ins, outs = refs[0], refs[1:1 + 3 * nv], refs[1 + 3 * nv:]
        tot = s_ref[0]
        for s in range(1, N_DEV):
            tot = tot + s_ref[s]
        c0 = 0
        for i in range(nv):
            n = ins[i].shape[1]
            g = tot[:, c0:c0 + n]
            outs[i][...] = g
            outs[nv + i][...], outs[2 * nv + i][...], outs[3 * nv + i][...] = _adamw(
                ins[i][...], g, ins[nv + i][...], ins[2 * nv + i][...])
            c0 += n
        outs[-1][...] = tot[:, c0:]

    return pl.pallas_call(
        body, name="reduce_adamw_small",
        out_shape=[jax.ShapeDtypeStruct(a.shape, F32) for a in ws] * 4 + [jax.ShapeDtypeStruct((1, 128), F32)],
    )(sbuf, *ws, *ms, *vs)


_TRANSPOSED = ("w_in", "w_gate", "w_up")
_ROW_STACKED = MATS
_ADAM_TILE = {"w_in": 208, "w_out": 256, "w_gate": 176, "w_up": 176, "w_down": 176, "lora": 256}
_SUM_TILE = {"w_in": 208, "w_out": 128, "w_gate": 176, "w_up": 176, "w_down": 176, "lora": 128}


def _full(n, stacked):
    p, r, c = stacked.shape
    if n in _ROW_STACKED:
        return stacked.reshape(p * r, c)
    return jnp.transpose(stacked, (1, 0, 2)).reshape(r, p * c)


def _by_chip(n, full):
    if n in _ROW_STACKED:
        return full.reshape(N_CHIP, full.shape[0] // N_CHIP, full.shape[1])
    r, c = full.shape
    return jnp.transpose(full.reshape(r, N_CHIP, c // N_CHIP), (1, 0, 2))


def _with_own(land_shape, dtype, own, slot):
    return lax.dynamic_update_slice(lax.empty(land_shape, dtype), own[None], (slot,) + (0,) * own.ndim)


def _cast_into_slot(a, chip, tr, name, after=None):
    rows, cols = a.shape

    def body(chip_ref, a_ref, *rest):
        rest[-1][0] = a_ref[...].astype(BF16)

    extra = [] if after is None else [after]
    return pl.pallas_call(
        body, name=name,
        grid_spec=pltpu.PrefetchScalarGridSpec(
            num_scalar_prefetch=1, grid=(rows // tr,),
            in_specs=[pl.BlockSpec((tr, cols), lambda i, chip_ref: (i, 0))] + [ANY] * len(extra),
            out_specs=pl.BlockSpec((1, tr, cols), lambda i, chip_ref: (chip_ref[0], i, 0))),
        out_shape=jax.ShapeDtypeStruct((N_CHIP, rows, cols), BF16),
        compiler_params=_params(("parallel",)),
    )(chip, a, *extra)


def kernel(x, mix_norm_g, w_in, mu_shift, decay_w0, decay_w2, iclr_a0, iclr_a2, gate_g2, k_k, k_a, r_k, ln_x_w, ln_x_b, attn_out_g, w_out, ffn_norm_g, w_gate, w_up, w_down, final_norm_g, loss_target, m_mix_norm_g, m_w_in, m_mu_shift, m_decay_w0, m_decay_w2, m_iclr_a0, m_iclr_a2, m_gate_g2, m_k_k, m_k_a, m_r_k, m_ln_x_w, m_ln_x_b, m_attn_out_g, m_w_out, m_ffn_norm_g, m_w_gate, m_w_up, m_w_down, m_final_norm_g, v_mix_norm_g, v_w_in, v_mu_shift, v_decay_w0, v_decay_w2, v_iclr_a0, v_iclr_a2, v_gate_g2, v_k_k, v_k_a, v_r_k, v_ln_x_w, v_ln_x_b, v_attn_out_g, v_w_out, v_ffn_norm_g, v_w_gate, v_w_up, v_w_down, v_final_norm_g):
    names = ("mix_norm_g", "w_in", "mu_shift", "decay_w0", "decay_w2", "iclr_a0", "iclr_a2", "gate_g2", "k_k", "k_a",
             "r_k", "ln_x_w", "ln_x_b", "attn_out_g", "w_out", "ffn_norm_g", "w_gate", "w_up", "w_down", "final_norm_g")
    w = dict(zip(names, (mix_norm_g, w_in, mu_shift, decay_w0, decay_w2, iclr_a0, iclr_a2, gate_g2, k_k, k_a, r_k,
                         ln_x_w, ln_x_b, attn_out_g, w_out, ffn_norm_g, w_gate, w_up, w_down, final_norm_g)))
    m = dict(zip(names, (m_mix_norm_g, m_w_in, m_mu_shift, m_decay_w0, m_decay_w2, m_iclr_a0, m_iclr_a2, m_gate_g2,
                         m_k_k, m_k_a, m_r_k, m_ln_x_w, m_ln_x_b, m_attn_out_g, m_w_out, m_ffn_norm_g, m_w_gate,
                         m_w_up, m_w_down, m_final_norm_g)))
    v = dict(zip(names, (v_mix_norm_g, v_w_in, v_mu_shift, v_decay_w0, v_decay_w2, v_iclr_a0, v_iclr_a2, v_gate_g2,
                         v_k_k, v_k_a, v_r_k, v_ln_x_w, v_ln_x_b, v_attn_out_g, v_w_out, v_ffn_norm_g, v_w_gate,
                         v_w_up, v_w_down, v_final_norm_g)))
    first = ("w_in", "lora")
    rest = ("w_out", "w_gate", "w_up", "w_down")
    xi, yi, ci = lax.axis_index("x"), lax.axis_index("y"), lax.axis_index("c")
    my_chip, my_dev = 2 * xi + yi, 4 * xi + 2 * yi + ci
    gather, scatter = ("gather",) * 4, ("scatter",) * 4

    def stored(d):
        out = {n: jnp.transpose(d[n][0]) if n in _TRANSPOSED else d[n][0] for n in MATS}
        out["lora"] = jnp.concatenate([d[n][0] for n in LORAS], axis=0)
        return out

    ws, ms, vs = stored(w), stored(m), stored(v)
    lora_rows = [(0, 64), (64, 128), (128, 256)]
    chip = jnp.reshape(my_chip, (1,)).astype(jnp.int32)
    early = _swap_start(None, [_cast_into_slot(ws["w_in"], chip, _ADAM_TILE["w_in"], "cast_w_in"),
                               _with_own((N_CHIP,) + ws["lora"].shape, F32, ws["lora"], my_chip)], gather[:2],
                        "gather_first_start")
    lands = [_cast_into_slot(ws[n], chip, _ADAM_TILE[n], "cast_" + n, after=early[4]) for n in rest]
    gather_rest = ("whole",) * 4
    ssem, rsem, srcs_thru, lands_thru, tok = _swap_start(None, lands, gather_rest, "gather_rest_start")
    got = _swap_wait(early[0], early[1], early[2], early[3], tok, gather[:2], "gather_first_wait")
    win_all, lora_all = _swap_gathered(got, "gather_first_halves")
    win = _full("w_in", win_all)
    w2, a2, g2m = (_full(n, lora_all[:, a:b]) for n, (a, b) in zip(LORAS, lora_rows))

    vecs = {n: w[n].reshape(1, sz) for n, sz in VECS}
    vecs["mix_norm_g"] = vecs["mix_norm_g"] + tok[0, 0]

    def get_rest(after):
        got_rest = _swap_wait(ssem, rsem, srcs_thru, lands_thru, after, gather_rest, "gather_rest_wait")
        return [_full(n, z) for n, z in zip(rest, got_rest)]

    flight = []

    def my_half(g):
        h = g.shape[1] // 2
        return lax.dynamic_slice(g, (my_chip, ci * h, 0), (1, h, g.shape[2]))[0]

    def send_rest(gw):
        gs = [_by_chip(n, gw[n]) for n in rest]
        into = [_with_own((N_DEV,) + my_half(g).shape, BF16, my_half(g), my_dev) for g in gs]
        flight.extend(_swap_start(gs, into, scatter, "exchange_rest_start"))
        return flight[4]

    loss8, dx, gw, gv = _local_step(x[0], loss_target[0], win, vecs, w2, a2, g2m, get_rest, send_rest)

    core = jnp.reshape(ci, (1,)).astype(jnp.int32)
    gs = [_by_chip("w_in", gw["w_in"]),
          jnp.concatenate([_by_chip(n, gw[n]) for n in LORAS], axis=1).astype(BF16)]
    theirs = _sibling_halves(gs, "presum_halves")
    sums = [_add_halves(g, o, core, _SUM_TILE[n], "chipsum_" + n) for n, g, o in zip(first, gs, theirs)]
    own = [lax.dynamic_index_in_dim(s, my_chip, 0, keepdims=False) for s in sums]
    last = _swap_start(sums, [_with_own(s.shape, BF16, o, my_chip) for s, o in zip(sums, own)], ("chipsum",) * 2,
                       "exchange_first_start")
    small = _rowsum_small([gv[n] for n, _ in VECS], loss8 + last[4])
    vecs_out = _swap_start([small], [_with_own((N_DEV,) + small.shape, F32, small, my_dev)], ("all",),
                           "exchange_vectors_start")


    def update(group, rbufs, tag):
        sums = [_reduce8(rb, core, _SUM_TILE[n], "reduce_" + n) for n, rb in zip(group, rbufs)]
        gsum = _join_halves(sums, "join_halves_" + tag)
        out = {}
        for n, g in zip(group, gsum):
            r = _adamw_call(g, ws[n][None], ms[n][None], vs[n][None], _ADAM_TILE[n], "adamw_" + n)
            if n == "lora":
                for name, (a, b) in zip(LORAS, lora_rows):
                    out[name] = [z[:, a:b] for z in r]
            else:
                out[n] = [jnp.transpose(z[0])[None] for z in r] if n in _TRANSPOSED else r
        return out, r[1]

    res, done = update(rest, _swap_wait(flight[0], flight[1], flight[2], flight[3], vecs_out[4], scatter,
                                        "exchange_rest_wait"), "rest")
    got = _swap_wait(last[0], last[1], last[2], last[3], done, ("chipsum",) * 2, "exchange_first_wait")
    res_first, done = update(first, got, "first")
    res.update(res_first)
    sbuf = _swap_wait(vecs_out[0], vecs_out[1], vecs_out[2], vecs_out[3], done, ("all",), "exchange_vectors_wait")[0]
    rows = lambda d: [d[n].reshape(1, sz) for n, sz in VECS]
    small_res = _reduce_adamw_small(sbuf, rows(w), rows(m), rows(v))

    outs = []
    for k in range(4):
        piece = {n: r[k] for n, r in res.items()}
        for i, (n, _) in enumerate(VECS):
            piece[n] = small_res[k * len(VECS) + i].reshape(w[n].shape)
        outs.extend(piece[n] for n in names)
    return (small_res[-1][0, 0], dx[None], *outs)
```

```python
import jax
import jax.numpy as jnp
from jax import lax
from jax.experimental import pallas as pl
from jax.experimental.pallas import tpu as pltpu

F32 = jnp.float32
BF16 = jnp.bfloat16

D_MODEL = 1024
HEAD_DIM = 64
RW = 512
N_PAIR = RW // 128
SHIFT_COLS = 1792
IN_COLS = 3328
D_FF = 2816
FF_CHUNK = 256
NORM_EPS = 1e-6
GN_EPS = 64e-5
CHUNK = 64
SUB = 16
WKV_PASSES = 1
ATTN_PASSES = 1
ATTN_BLOCK = 128
DILATIONS = (1, 4, 16)
NEG = -1e30
ADAM_LR, ADAM_B1, ADAM_B2, ADAM_EPS, ADAM_WD, ADAM_STEP = 0.001, 0.9, 0.999, 1e-08, 0.01, 10
VMEM_LIMIT = 56 * 1024 * 1024
MESH = pl.DeviceIdType.MESH


def _params(sem=None, **kw):
    return pltpu.CompilerParams(dimension_semantics=sem, vmem_limit_bytes=VMEM_LIMIT, **kw)


def _dot(a, b):
    return lax.dot_general(a, b, (((1,), (0,)), ((), ())), preferred_element_type=F32)


def _dot_nt(a, b):
    return lax.dot_general(a, b, (((1,), (1,)), ((), ())), preferred_element_type=F32)


def _dot_tn(a, b):
    return lax.dot_general(a, b, (((0,), (0,)), ((), ())), preferred_element_type=F32)


_FORMS = {"nn": ((1,), (0,)), "nt": ((1,), (1,)), "tn": ((0,), (0,))}


def _dg(a, b, form):
    if a.ndim == 3 or b.ndim == 3:
        nb = a.shape[0] if a.ndim == 3 else b.shape[0]
        return jnp.stack([_dg(a[i] if a.ndim == 3 else a, b[i] if b.ndim == 3 else b, form) for i in range(nb)], axis=0)
    return lax.dot_general(a, b, (_FORMS[form], ((), ())), preferred_element_type=F32)


def _split2(x):
    hi = x.astype(BF16)
    return hi, (x - hi.astype(F32)).astype(BF16)


def _split3(x):
    hi = x.astype(BF16)
    rest = x - hi.astype(F32)
    mid = rest.astype(BF16)
    return hi, mid, (rest - mid.astype(F32)).astype(BF16)


def _mm_raw(a, b, form, mode):
    if mode == 1:
        return _dg(a.astype(BF16), b.astype(BF16), form)
    if mode == 3:
        ah, al = _split2(a)
        bh, bl = _split2(b)
        return _dg(ah, bh, form) + (_dg(ah, bl, form) + _dg(al, bh, form))
    if mode == "L3":
        ab = a.astype(BF16)
        b1, b2, b3 = _split3(b)
        if form == "nn":
            n = b.shape[-1]
            wide = _dg(ab, jnp.concatenate([b1, b2, b3], axis=-1), form)
            return wide[..., :n] + (wide[..., n:2 * n] + wide[..., 2 * n:])
        return _dg(ab, b1, form) + (_dg(ab, b2, form) + _dg(ab, b3, form))
    assert mode == "R3", mode
    bb = b.astype(BF16)
    a1, a2, a3 = _split3(a)
    if form in ("nn", "nt"):
        m = a.shape[-2]
        tall = _dg(jnp.concatenate([a1, a2, a3], axis=-2), bb, form)
        return tall[..., :m, :] + (tall[..., m:2 * m, :] + tall[..., 2 * m:, :])
    return _dg(a1, bb, form) + (_dg(a2, bb, form) + _dg(a3, bb, form))


def _mm(a, b, form, mode):
    @jax.custom_vjp
    def f(a, b):
        return _mm_raw(a, b, form, mode)

    def fwd(a, b):
        return _mm_raw(a, b, form, mode), (a, b)

    def bwd(res, ct):
        a, b = res
        la = {1: 1, 3: 3, "L3": None, "R3": "R3"}[mode]
        lb = {1: 1, 3: 3, "L3": "L3", "R3": None}[mode]
        if form == "nn":
            da = None if la is None else _mm_raw(ct, b, "nt", la)
            db = None if lb is None else _mm_raw(a, ct, "tn", lb)
        elif form == "nt":
            da = None if la is None else _mm_raw(ct, b, "nn", la)
            db = None if lb is None else _mm_raw(ct, a, "tn", "R3" if lb == "L3" else lb)
        else:
            da = None if la is None else _mm_raw(b, ct, "nt", "L3" if la == "R3" else la)
            db = None if lb is None else _mm_raw(a, ct, "nn", lb)
        return (jnp.zeros_like(a) if da is None else da, jnp.zeros_like(b) if db is None else db)

    f.defvjp(fwd, bwd)
    return f(a, b)


def _seg_ones(n):
    r = lax.broadcasted_iota(jnp.int32, (n, n), 0) // HEAD_DIM
    c = lax.broadcasted_iota(jnp.int32, (n, n), 1) // HEAD_DIM
    return (r == c).astype(F32)


def _segsum(x, seg):
    return _mm(x, seg, "nn", "R3")


def _rms_fwd(x, g):
    rstd = lax.rsqrt(jnp.mean(x * x, axis=-1, keepdims=True) + NORM_EPS)
    return x * rstd * g


def _rms_bwd(dy, x, g):
    rstd = lax.rsqrt(jnp.mean(x * x, axis=-1, keepdims=True) + NORM_EPS)
    xn = x * rstd
    dxn = dy * g
    dx = rstd * (dxn - xn * jnp.mean(dxn * xn, axis=-1, keepdims=True))
    return dx, dy * xn


def _sigmoid(x):
    return 1.0 / (1.0 + jnp.exp(-x))


def _softplus(x):
    return jnp.maximum(x, 0.0) + jnp.log(1.0 + jnp.exp(-jnp.abs(x)))


def _acc(ref, val, first):
    @pl.when(first)
    def _():
        ref[...] = val

    @pl.when(jnp.logical_not(first))
    def _():
        ref[...] += val


def _colsum8(v):
    rows, n = v.shape
    return jnp.sum(v.reshape(rows // 8, 8, n), axis=0)


def _prep_fn(p, pprev, mu, w0, w2p, a0, a2p, g2, k_k, k_a):
    seg = _seg_ones(RW)
    ps = p + (pprev - p) * mu
    r = ps[:, 0:RW]
    k = ps[:, RW:2 * RW]
    v = ps[:, 2 * RW:3 * RW]
    xwa = ps[:, 3 * RW:3 * RW + 128]
    xg = ps[:, 3 * RW + 128:3 * RW + 256]
    wraw = -_softplus(-(w0 + _mm(jnp.tanh(xwa), w2p, "nn", 3))) - 0.5
    lw = -jnp.exp(wraw)
    a = _sigmoid(a0 + _mm(xwa, a2p, "nn", 3))
    g = _mm(_sigmoid(xg), g2, "nn", 3)
    kk = k * k_k
    kk = kk / jnp.maximum(jnp.sqrt(_segsum(kk * kk, seg)), 1e-12)
    k2 = k * (1.0 + (a - 1.0) * k_a)
    return r, lw, k2, v, kk, a, g


def _transposed(z):
    return jnp.stack([z[i].T for i in range(z.shape[0])], axis=0) if z.ndim == 3 else z.T


def _solve_unit_lower(lmat, rhs):
    c = lmat.shape[-1]
    row = lax.broadcasted_iota(jnp.int32, (c, c), 0)
    col = lax.broadcasted_iota(jnp.int32, (c, c), 1)
    eye = (row == col).astype(F32)
    ld = jnp.where(row // SUB == col // SUB, lmat, 0.0)
    lo = lmat - ld
    x = eye + ld
    m = ld
    mm = lambda p, q: _mm(p, q, "nn", WKV_PASSES)
    cat = jnp.concatenate
    m = mm(m, m)
    for _ in range(2):
        mx = mm(m, cat([m, x], axis=-1))
        m, x = mx[..., :c], x + mx[..., c:]
    x = x + mm(m, x)
    gw = mm(x, cat([lo, rhs], axis=-1))
    g, w = gw[..., :c], gw[..., c:]
    gg = mm(g, cat([g, w], axis=-1))
    w = w + gg[..., c:]
    return w + mm(gg[..., :c], w)


def _wkv_chunk_fn(s0, r, lw, k, v, kk, a):
    c = r.shape[-2]
    n = 2 * c
    row = lax.broadcasted_iota(jnp.int32, (n, n), 0)
    col = lax.broadcasted_iota(jnp.int32, (n, n), 1)
    same = (row // c) == (col // c)
    incl = jnp.logical_and(row >= col, same)
    strict = jnp.logical_and(row > col, same)
    sel = (lax.broadcasted_iota(jnp.int32, (n, 128), 0) // c) == (lax.broadcasted_iota(jnp.int32, (n, 128), 1) // HEAD_DIM)
    two = lambda z: jnp.concatenate([z, z], axis=-2)
    lw2 = two(lw)
    mm = lambda p_, q_, form: _mm(p_, q_, form, WKV_PASSES)
    cl = _mm(incl.astype(F32), lw2, "nn", "L3")
    p = jnp.exp(cl)
    pinv = jnp.exp(-cl)
    pprev = jnp.exp(cl - lw2)
    kk2 = two(kk)
    at = jnp.where(sel, -kk2 * pprev, 0.0)
    bt = jnp.where(sel, kk2 * two(a) * pinv, 0.0)
    kt = jnp.where(sel, two(k) * pinv, 0.0)
    rt = jnp.where(sel, two(r) * p, 0.0)
    vt = jnp.where(sel, two(v), 0.0)
    cat = jnp.concatenate
    bk = cat([bt, kt], axis=-2)
    arbk = mm(cat([at, rt], axis=-2), bk, "nt")
    ab, ak = jnp.where(strict, arbk[..., :n, :n], 0.0), jnp.where(strict, arbk[..., :n, n:], 0.0)
    rb, rk = jnp.where(incl, arbk[..., n:, :n], 0.0), jnp.where(incl, arbk[..., n:, n:], 0.0)
    s0t = _transposed(s0)
    u = _solve_unit_lower(ab, mm(cat([at, ak], axis=-1), cat([s0t, vt], axis=-2), "nn"))
    y2 = mm(cat([rt, rb, rk], axis=-1), cat([s0t, u, vt], axis=-2), "nn")
    plast = jnp.exp(jnp.sum(lw, axis=-2, keepdims=True))
    s1 = (s0 + mm(cat([u, vt], axis=-2), bk, "tn")) * plast
    r2 = lax.broadcasted_iota(jnp.int32, (128, 128), 0) // HEAD_DIM
    c2 = lax.broadcasted_iota(jnp.int32, (128, 128), 1) // HEAD_DIM
    return y2[..., :c, :] + y2[..., c:, :], jnp.where(r2 == c2, s1, 0.0)


def _post_fn(y, r, k2, v, g, lnw, lnb, rk):
    seg = _seg_ones(RW)
    mean = _segsum(y, seg) * (1.0 / HEAD_DIM)
    yc = y - mean
    var = _segsum(yc * yc, seg) * (1.0 / HEAD_DIM)
    yn = yc * lax.rsqrt(var + GN_EPS)
    out = yn * lnw + lnb + _segsum(r * k2 * rk, seg) * v
    return out * g


def _attn_block_fn(q, kc, vc, kp=None, vp=None):
    n = ATTN_BLOCK
    qi = lax.broadcasted_iota(jnp.int32, (n, n), 0)
    kj = lax.broadcasted_iota(jnp.int32, (n, n), 1)
    lane = lax.broadcasted_iota(jnp.int32, (1, 128), 1)
    scale = HEAD_DIM ** -0.5
    valid = kj <= qi
    keys, vals = kc, vc
    if kp is not None:
        valid = jnp.concatenate([valid, kj >= qi], axis=-1)
        keys, vals = jnp.concatenate([kc, kp], axis=-2), jnp.concatenate([vc, vp], axis=-2)
    m0 = (lane // HEAD_DIM) == 0
    q2 = jnp.concatenate([jnp.where(m0, q, 0.0), jnp.where(m0, 0.0, q)], axis=-2)
    valid2 = jnp.concatenate([valid, valid], axis=-2)
    s = jnp.where(valid2, _mm(q2, keys, "nt", ATTN_PASSES) * scale, NEG)
    m = jnp.max(s, axis=-1, keepdims=True)
    p = jnp.exp(s - m)
    den = jnp.sum(p, axis=-1, keepdims=True)
    o2 = _mm(p, vals, "nn", ATTN_PASSES) / den
    l2 = m + jnp.log(den)
    return jnp.where(m0, o2[..., :n, :], o2[..., n:, :]), jnp.where(m0, l2[..., :n, :], l2[..., n:, :])


def _attn_block_bwd(q, kc, vc, kp, vp, o, lse, do, dl):
    n = ATTN_BLOCK
    cat = jnp.concatenate
    qi = lax.broadcasted_iota(jnp.int32, (n, n), 0)
    kj = lax.broadcasted_iota(jnp.int32, (n, n), 1)
    m0 = (lax.broadcasted_iota(jnp.int32, (1, 128), 1) // HEAD_DIM) == 0
    scale = HEAD_DIM ** -0.5
    valid = kj <= qi
    keys, vals = kc, vc
    if kp is not None:
        valid = cat([valid, kj >= qi], axis=-1)
        keys, vals = cat([kc, kp], axis=-2), cat([vc, vp], axis=-2)
    stack = lambda z: cat([jnp.where(m0, z, 0.0), jnp.where(m0, 0.0, z)], axis=-2)
    q2, do2 = stack(q), stack(do)
    lse2 = cat([jnp.max(jnp.where(m0, lse, NEG), axis=-1, keepdims=True),
                jnp.max(jnp.where(m0, NEG, lse), axis=-1, keepdims=True)], axis=-2)
    delta = jnp.sum(do2 * cat([o, o], axis=-2), axis=-1, keepdims=True)
    dlse = jnp.sum(stack(dl), axis=-1, keepdims=True)
    mm = lambda a, b, form: _mm_raw(a, b, form, ATTN_PASSES)
    s = jnp.where(cat([valid, valid], axis=-2), mm(q2, keys, "nt") * scale, NEG)
    p = jnp.exp(s - lse2)
    ds = p * (mm(do2, vals, "nt") - delta + dlse)
    dq2 = mm(ds, keys, "nn") * scale
    dq = jnp.where(m0, dq2[..., :n, :], dq2[..., n:, :])
    dkeys = mm(ds, q2, "tn") * scale
    dvals = mm(p, do2, "tn")
    if kp is None:
        return dq, dkeys, dvals
    return dq, dkeys[..., :n, :], dvals[..., :n, :], dkeys[..., n:, :], dvals[..., n:, :]


def _combine_fn(o1, o2, o3, l1, l2, l3, og):
    seg = _seg_ones(o1.shape[-1])
    m = jnp.maximum(jnp.maximum(l1, l2), l3)
    e1, e2, e3 = jnp.exp(l1 - m), jnp.exp(l2 - m), jnp.exp(l3 - m)
    o = (e1 * o1 + e2 * o2 + e3 * o3) / (e1 + e2 + e3)
    o = o * lax.rsqrt(_segsum(o * o, seg) * (1.0 / HEAD_DIM) + NORM_EPS)
    return o * og


def _shifted(p, last8, first):
    prow = jnp.where(first, 0.0, last8[7:8, :])
    rolled = pltpu.roll(p, 1, axis=0)
    rid = lax.broadcasted_iota(jnp.int32, p.shape, 0)
    return jnp.where(rid == 0, prow, rolled)


_PREP_TM = 256


def _prep_specs(tm):
    vec = lambda n: pl.BlockSpec((1, n), lambda i: (0, 0))
    mat = lambda r, n: pl.BlockSpec((r, n), lambda i: (0, 0))
    return [vec(SHIFT_COLS), vec(RW), mat(128, RW), vec(RW), mat(128, RW), mat(128, RW), vec(RW), vec(RW)]


def _in_proj_prep(x, g1, win, pw):
    t = x.shape[0]
    tm = _PREP_TM

    def body(x_ref, g_ref, w_ref, mu, w0, w2p, a0, a2p, g2, k_k, k_a, h_ref, pa_ref, qkv_ref, *rest):
        outs, carry = rest[:7], rest[7]

        @pl.when(pl.program_id(0) == 0)
        def _():
            carry[...] = jnp.zeros_like(carry)

        h = _rms_fwd(x_ref[...], g_ref[...]).astype(BF16)
        h_ref[...] = h
        proj = _dot_nt(h, w_ref[...])
        p = proj[:, :SHIFT_COLS]
        pa_ref[...] = p
        for j in range(3):
            for pr in range(N_PAIR):
                c0 = SHIFT_COLS + j * RW + pr * 128
                qkv_ref[j, pr] = proj[:, c0:c0 + 128]
        pprev = _shifted(p, carry[...], pl.program_id(0) == 0)
        carry[...] = p[tm - 8:, :]
        res = _prep_fn(p, pprev, mu[...], w0[...], w2p[...], a0[...], a2p[...], g2[...], k_k[...], k_a[...])
        for o_ref, val in zip(outs, res):
            o_ref[...] = val

    row = pl.BlockSpec((tm, RW), lambda i: (i, 0))
    return pl.pallas_call(
        body, name="in_proj_prep", grid=(t // tm,),
        in_specs=[pl.BlockSpec((tm, D_MODEL), lambda i: (i, 0)), pl.BlockSpec((1, D_MODEL), lambda i: (0, 0)),
                  pl.BlockSpec((IN_COLS, D_MODEL), lambda i: (0, 0))] + _prep_specs(tm),
        out_specs=[pl.BlockSpec((tm, D_MODEL), lambda i: (i, 0)), pl.BlockSpec((tm, SHIFT_COLS), lambda i: (i, 0)),
                   pl.BlockSpec((3, N_PAIR, tm, 128), lambda i: (0, 0, i, 0))] + [row] * 7,
        out_shape=[jax.ShapeDtypeStruct((t, D_MODEL), BF16), jax.ShapeDtypeStruct((t, SHIFT_COLS), F32),
                   jax.ShapeDtypeStruct((3, N_PAIR, t, 128), F32)] + [jax.ShapeDtypeStruct((t, RW), F32)] * 7,
        scratch_shapes=[pltpu.VMEM((8, SHIFT_COLS), F32)],
        compiler_params=_params(("arbitrary",)),
    )(x, g1, win, *pw)


def _pairs(ref):
    return jnp.stack([ref[:, 128 * p:128 * (p + 1)] for p in range(N_PAIR)], axis=0)


def _wkv_fwd(r, lw, k2, v, kk, a):
    t = r.shape[0]
    nc = t // CHUNK

    def body(r_ref, lw_ref, k_ref, v_ref, kk_ref, a_ref, y_ref, s_ref, st):
        @pl.when(pl.program_id(0) == 0)
        def _():
            st[...] = jnp.zeros_like(st)

        s0 = st[...]
        s_ref[0] = s0
        y, s1 = _wkv_chunk_fn(s0, *[_pairs(ref) for ref in (r_ref, lw_ref, k_ref, v_ref, kk_ref, a_ref)])
        for p in range(N_PAIR):
            y_ref[:, 128 * p:128 * (p + 1)] = y[p]
        st[...] = s1

    blk = pl.BlockSpec((CHUNK, RW), lambda c: (c, 0))
    return pl.pallas_call(
        body, name="wkv_fwd", grid=(nc,),
        in_specs=[blk] * 6,
        out_specs=[blk, pl.BlockSpec((1, N_PAIR, 128, 128), lambda c: (c, 0, 0, 0))],
        out_shape=[jax.ShapeDtypeStruct((t, RW), F32), jax.ShapeDtypeStruct((nc, N_PAIR, 128, 128), F32)],
        scratch_shapes=[pltpu.VMEM((N_PAIR, 128, 128), F32)],
        compiler_params=_params(("arbitrary",)),
    )(r, lw, k2, v, kk, a)


_POST_TM = 512


ATTN_GROUP = 2


def _dilated_rows(d, r, n):
    if d == 1:
        return pl.ds(pl.multiple_of(n * ATTN_BLOCK, ATTN_BLOCK), ATTN_BLOCK)
    return pl.ds(r + n * (ATTN_BLOCK * d), ATTN_BLOCK, stride=d)


def _for_each_sequence(t, unit):
    for di, d in enumerate(DILATIONS):

        @pl.when(pl.program_id(1) == di)
        def _(di=di, d=d):
            nb = t // (ATTN_BLOCK * d)
            if d == 1:
                unit(di, [(d, 0, 0)], False)
                unit(di, [(d, 0, 1)], True)
                lax.fori_loop(1, nb // 2, lambda k, c: (unit(di, [(d, 0, 2 * k), (d, 0, 2 * k + 1)], True), c)[1], 0)
            else:

                def residues(r, carry):
                    unit(di, [(d, r, 0), (d, r + d // 2, 0)], False)
                    if nb > 1:
                        lax.fori_loop(1, nb, lambda n, c: (unit(di, [(d, r, n), (d, r + d // 2, n)], True), c)[1], 0)
                    return carry

                lax.fori_loop(0, d // 2, residues, 0)


def _take(ref, lead, rows_list):
    return jnp.stack([ref.at[(*lead, g)][rows, :] for rows in rows_list for g in range(ref.shape[len(lead)])], axis=0)


def _put(ref, lead, rows_list, val, add=False):
    k = 0
    for rows in rows_list:
        for g in range(ref.shape[len(lead)]):
            if add:
                ref.at[(*lead, g)][rows, :] += val[k]
            else:
                ref.at[(*lead, g)][rows, :] = val[k]
            k += 1


def _attn_fwd(qkv):
    t = qkv.shape[2]

    def body(q_ref, k_ref, v_ref, o_ref, l_ref):
        def unit(di, places, has_prev):
            cur = [_dilated_rows(d, r, n) for d, r, n in places]
            args = [_take(ref, (0,), cur) for ref in (q_ref, k_ref, v_ref)]
            if has_prev:
                prv = [_dilated_rows(d, r, n - 1) for d, r, n in places]
                args += [_take(ref, (0,), prv) for ref in (k_ref, v_ref)]
            o, lse = _attn_block_fn(*args)
            _put(o_ref, (0,), cur, o)
            _put(l_ref, (0,), cur, lse)

        _for_each_sequence(t, unit)

    spec = lambda j: pl.BlockSpec((1, ATTN_GROUP, t, 128), lambda i, b: (j, i, 0, 0))
    out = pl.BlockSpec((1, ATTN_GROUP, t, 128), lambda i, b: (b, i, 0, 0))
    return pl.pallas_call(
        body, name="attn_fwd", grid=(N_PAIR // ATTN_GROUP, len(DILATIONS)),
        in_specs=[spec(0), spec(1), spec(2)], out_specs=[out, out],
        out_shape=[jax.ShapeDtypeStruct((3, N_PAIR, t, 128), F32)] * 2,
        compiler_params=_params(("parallel", "arbitrary")),
    )(qkv, qkv, qkv)


_COMB_TM = 512


def _mixers_out(y, r, k2, v, g, lnw, lnb, rk, o, l, og):
    t = y.shape[0]
    tm = _COMB_TM

    def body(y_ref, r_ref, k_ref, v_ref, g_ref, lnw_ref, lnb_ref, rk_ref, o_ref, l_ref, og_ref, out_ref):
        out_ref[:, :RW] = _post_fn(y_ref[...], r_ref[...], k_ref[...], v_ref[...], g_ref[...],
                                   lnw_ref[...], lnb_ref[...], rk_ref[...]).astype(BF16)
        for p in range(N_PAIR):
            cols = slice(128 * p, 128 * (p + 1))
            out_ref[:, RW + 128 * p:RW + 128 * (p + 1)] = _combine_fn(
                o_ref[0, p], o_ref[1, p], o_ref[2, p], l_ref[0, p], l_ref[1, p], l_ref[2, p], og_ref[:, cols]).astype(BF16)

    row = pl.BlockSpec((tm, RW), lambda i: (i, 0))
    vec = pl.BlockSpec((1, RW), lambda i: (0, 0))
    blk = pl.BlockSpec((3, N_PAIR, tm, 128), lambda i: (0, 0, i, 0))
    return pl.pallas_call(
        body, name="mixers_out", grid=(t // tm,),
        in_specs=[row] * 5 + [vec] * 3 + [blk, blk, vec], out_specs=pl.BlockSpec((tm, D_MODEL), lambda i: (i, 0)),
        out_shape=jax.ShapeDtypeStruct((t, D_MODEL), BF16),
        compiler_params=_params(("parallel",)),
    )(y, r, k2, v, g, lnw, lnb, rk, o, l, og)


def _ffn_all(x, ycat, wg, wu, wd, wout, g2, gf, tgt):
    t = x.shape[0]
    tm = 256

    def body(x_ref, y_ref, wg_ref, wu_ref, wd_ref, wo_ref, g2_ref, gf_ref, t_ref,
             h_ref, act_ref, dx2b_ref, dgt_ref, dup_ref, dx1b_ref, dx1_ref, dya_ref, dyb_ref, loss_ref, dgf_ref, dg2_ref,
             gt_s, up_s):
        first = pl.program_id(0) == 0
        x1 = x_ref[...] + _dot(y_ref[...], wo_ref[...])
        h = _rms_fwd(x1, g2_ref[...]).astype(BF16)
        h_ref[...] = h
        for c0 in range(0, D_FF, FF_CHUNK):
            cols = slice(c0, c0 + FF_CHUNK)
            gt = _dot_nt(h, wg_ref[cols, :])
            up = _dot_nt(h, wu_ref[cols, :])
            gt_s[:, cols] = gt.astype(BF16)
            up_s[:, cols] = up.astype(BF16)
            act_ref[:, cols] = (gt * _sigmoid(gt) * up).astype(BF16)
        x2 = x1 + _dot(act_ref[...], wd_ref[...])
        gf_ = gf_ref[...]
        diff = _rms_fwd(x2, gf_) - t_ref[...]
        lrow = 0.5 * jnp.sum(_colsum8(diff * diff), axis=1, keepdims=True) * (1.0 / D_MODEL)
        _acc(loss_ref, jnp.broadcast_to(lrow, (8, 128)), first)
        dx2, dgr = _rms_bwd(diff * (1.0 / D_MODEL), x2, gf_)
        _acc(dgf_ref, _colsum8(dgr), first)
        dx2b = dx2.astype(BF16)
        dx2b_ref[...] = dx2b
        for c0 in range(0, D_FF, FF_CHUNK):
            cols = slice(c0, c0 + FF_CHUNK)
            dact = _dot_nt(dx2b, wd_ref[cols, :])
            gt = gt_s[:, cols].astype(F32)
            sg = _sigmoid(gt)
            dgt_ref[:, cols] = (dact * up_s[:, cols].astype(F32) * sg * (1.0 + gt * (1.0 - sg))).astype(BF16)
            dup_ref[:, cols] = (dact * gt * sg).astype(BF16)
        dh = _dot(dgt_ref[...], wg_ref[...]) + _dot(dup_ref[...], wu_ref[...])
        dxn, dgr2 = _rms_bwd(dh, x1, g2_ref[...])
        _acc(dg2_ref, _colsum8(dgr2), first)
        dx1 = dx2 + dxn
        dx1_ref[...] = dx1
        dx1b = dx1.astype(BF16)
        dx1b_ref[...] = dx1b
        dy = _dot_nt(dx1b, wo_ref[...])
        dya_ref[...] = dy[:, :RW]
        dyb_ref[...] = dy[:, RW:]

    row = pl.BlockSpec((tm, D_MODEL), lambda i: (i, 0))
    wide = pl.BlockSpec((tm, D_FF), lambda i: (i, 0))
    half = pl.BlockSpec((tm, RW), lambda i: (i, 0))
    wsp = pl.BlockSpec((D_FF, D_MODEL), lambda i: (0, 0))
    vec = pl.BlockSpec((1, D_MODEL), lambda i: (0, 0))
    part = pl.BlockSpec((8, D_MODEL), lambda i: (0, 0))
    bf = lambda n: jax.ShapeDtypeStruct((t, n), BF16)
    return pl.pallas_call(
        body, name="ffn_all", grid=(t // tm,),
        in_specs=[row, row, wsp, wsp, wsp, pl.BlockSpec((D_MODEL, D_MODEL), lambda i: (0, 0)), vec, vec, row],
        out_specs=[row, wide, row, wide, wide, row, row, half, half, pl.BlockSpec((8, 128), lambda i: (0, 0)), part, part],
        out_shape=[bf(D_MODEL), bf(D_FF), bf(D_MODEL), bf(D_FF), bf(D_FF), bf(D_MODEL),
                   jax.ShapeDtypeStruct((t, D_MODEL), F32), jax.ShapeDtypeStruct((t, RW), F32),
                   jax.ShapeDtypeStruct((t, RW), F32), jax.ShapeDtypeStruct((8, 128), F32),
                   jax.ShapeDtypeStruct((8, D_MODEL), F32), jax.ShapeDtypeStruct((8, D_MODEL), F32)],
        scratch_shapes=[pltpu.VMEM((tm, D_FF), BF16), pltpu.VMEM((tm, D_FF), BF16)],
        compiler_params=_params(("arbitrary",)),
    )(x, ycat, wg, wu, wd, wout, g2, gf, tgt)


_WGRAD_ROWS = 512


def _wgrad(a, b, tk, tn, name):
    t, kdim = a.shape
    ndim = b.shape[1]

    tt = _WGRAD_ROWS

    def body(a_ref, b_ref, o_ref, acc_ref):
        s, last = pl.program_id(2), pl.num_programs(2) - 1
        part = _dot_tn(a_ref[...], b_ref[...])

        @pl.when(s == 0)
        def _():
            acc_ref[...] = part

        @pl.when(jnp.logical_and(s > 0, s < last))
        def _():
            acc_ref[...] += part

        @pl.when(s == last)
        def _():
            o_ref[...] = (acc_ref[...] + part).astype(BF16)

    return pl.pallas_call(
        body, name=name, grid=(kdim // tk, ndim // tn, t // tt),
        in_specs=[pl.BlockSpec((tt, tk), lambda i, j, s: (s, i)), pl.BlockSpec((tt, tn), lambda i, j, s: (s, j))],
        out_specs=pl.BlockSpec((tk, tn), lambda i, j, s: (i, j)),
        out_shape=jax.ShapeDtypeStruct((kdim, ndim), BF16),
        scratch_shapes=[pltpu.VMEM((tk, tn), F32)],
        compiler_params=_params(("parallel", "parallel", "arbitrary")),
    )(a, b)


def _post_bwd(dya, y, r, k2, v, g, lnw, lnb, rk):
    t = y.shape[0]
    tm = _POST_TM

    def body(d_ref, y_ref, r_ref, k_ref, v_ref, g_ref, lnw_ref, lnb_ref, rk_ref,
             dy_ref, dr_ref, dk_ref, dv_ref, dg_ref, dlnw_ref, dlnb_ref, drk_ref):
        first = pl.program_id(0) == 0
        ones = jnp.ones((tm, 1), F32)
        prim = (y_ref[...], r_ref[...], k_ref[...], v_ref[...], g_ref[...],
                ones * lnw_ref[...], ones * lnb_ref[...], ones * rk_ref[...])
        _, vjp = jax.vjp(_post_fn, *prim)
        dy, dr, dk, dv, dg, dlnw, dlnb, drk = vjp(d_ref[...])
        dy_ref[...] = dy
        dr_ref[...] = dr
        dk_ref[...] = dk
        dv_ref[...] = dv
        dg_ref[...] = dg
        _acc(dlnw_ref, _colsum8(dlnw), first)
        _acc(dlnb_ref, _colsum8(dlnb), first)
        _acc(drk_ref, _colsum8(drk), first)

    row = pl.BlockSpec((tm, RW), lambda i: (i, 0))
    vec = pl.BlockSpec((1, RW), lambda i: (0, 0))
    part = pl.BlockSpec((8, RW), lambda i: (0, 0))
    return pl.pallas_call(
        body, name="rwkv_post_bwd", grid=(t // tm,),
        in_specs=[row] * 6 + [vec] * 3, out_specs=[row] * 5 + [part] * 3,
        out_shape=[jax.ShapeDtypeStruct((t, RW), F32)] * 5 + [jax.ShapeDtypeStruct((8, RW), F32)] * 3,
        compiler_params=_params(("arbitrary",)),
    )(dya, y, r, k2, v, g, lnw, lnb, rk)


def _wkv_bwd(dy, s0s, r, lw, k2, v, kk, a):
    t = r.shape[0]
    nc = t // CHUNK

    def body(dy_ref, s_ref, r_ref, lw_ref, k_ref, v_ref, kk_ref, a_ref,
             dr_ref, dlw_ref, dk_ref, dv_ref, dkk_ref, da_ref, ds):
        @pl.when(pl.program_id(0) == 0)
        def _():
            ds[...] = jnp.zeros_like(ds)

        _, vjp = jax.vjp(_wkv_chunk_fn, s_ref[0],
                         *[_pairs(ref) for ref in (r_ref, lw_ref, k_ref, v_ref, kk_ref, a_ref)])
        res = vjp((_pairs(dy_ref), ds[...]))
        ds[...] = res[0]
        for ref, val in zip((dr_ref, dlw_ref, dk_ref, dv_ref, dkk_ref, da_ref), res[1:]):
            for p in range(N_PAIR):
                ref[:, 128 * p:128 * (p + 1)] = val[p]

    blk = pl.BlockSpec((CHUNK, RW), lambda c: (nc - 1 - c, 0))
    return pl.pallas_call(
        body, name="wkv_bwd", grid=(nc,),
        in_specs=[blk, pl.BlockSpec((1, N_PAIR, 128, 128), lambda c: (nc - 1 - c, 0, 0, 0))] + [blk] * 6,
        out_specs=[blk] * 6,
        out_shape=[jax.ShapeDtypeStruct((t, RW), F32)] * 6,
        scratch_shapes=[pltpu.VMEM((N_PAIR, 128, 128), F32)],
        compiler_params=_params(("arbitrary",)),
    )(dy, s0s, r, lw, k2, v, kk, a)


def _prep_in_proj_bwd(proj, pw, douts, dq, dk, dv, win, x, g1, dx1):
    t = proj.shape[0]
    tm = _PREP_TM
    nt = t // tm

    def body(p_ref, l8_ref, mu, w0, w2p, a0, a2p, g2, k_k, k_a, dr, dr2, dlw, dk2, dk22, dv, dv2, dkk, da, dg,
             dq_ref, dkq_ref, dvq_ref, w_ref, x_ref, g1_ref, dx1_ref,
             dproj_ref, dx_ref, dg1_ref, dmu_ref, dw0_ref, dw2_ref, da0_ref, da2_ref, dg2_ref, dkk_ref, dka_ref, carry):
        i = pl.program_id(0)
        first = i == 0

        @pl.when(first)
        def _():
            carry[...] = jnp.zeros_like(carry)

        p = p_ref[...]
        pprev = _shifted(p, l8_ref[...], i == nt - 1)
        ones = jnp.ones((tm, 1), F32)
        prim = (p, pprev, ones * mu[...], ones * w0[...], w2p[...], ones * a0[...], a2p[...], g2[...],
                ones * k_k[...], ones * k_a[...])
        _, vjp = jax.vjp(_prep_fn, *prim)
        dp, dpp, dmu, dw0, dw2, da0, da2, dg2, dkk_, dka = vjp(
            (dr[...] + dr2[...], dlw[...], dk2[...] + dk22[...], dv[...] + dv2[...], dkk[...], da[...], dg[...]))
        up = pltpu.roll(dpp, tm - 1, axis=0)
        rid = lax.broadcasted_iota(jnp.int32, dpp.shape, 0)
        dpa = dp + jnp.where(rid == tm - 1, carry[0:1, :], up)
        carry[...] = jnp.broadcast_to(dpp[0:1, :], carry.shape)
        _acc(dmu_ref, _colsum8(dmu), first)
        _acc(dw0_ref, _colsum8(dw0), first)
        _acc(dw2_ref, dw2, first)
        _acc(da0_ref, _colsum8(da0), first)
        _acc(da2_ref, da2, first)
        _acc(dg2_ref, dg2, first)
        _acc(dkk_ref, _colsum8(dkk_), first)
        _acc(dka_ref, _colsum8(dka), first)
        parts = [dpa] + [ref[pr] for ref in (dq_ref, dkq_ref, dvq_ref) for pr in range(N_PAIR)]
        dproj = jnp.concatenate([z.astype(BF16) for z in parts], axis=1)
        dproj_ref[...] = dproj
        dxn, dgr = _rms_bwd(_dot(dproj, w_ref[...]), x_ref[...], g1_ref[...])
        dx_ref[...] = dx1_ref[...] + dxn
        _acc(dg1_ref, _colsum8(dgr), first)

    rev = lambda i: (nt - 1 - i, 0)
    row = pl.BlockSpec((tm, RW), rev)
    wide = pl.BlockSpec((tm, D_MODEL), rev)
    pair = pl.BlockSpec((N_PAIR, tm, 128), lambda i: (0, nt - 1 - i, 0))
    part = lambda n: pl.BlockSpec((8, n), lambda i: (0, 0))
    mat = pl.BlockSpec((128, RW), lambda i: (0, 0))
    return pl.pallas_call(
        body, name="prep_in_proj_bwd", grid=(nt,),
        in_specs=[pl.BlockSpec((tm, SHIFT_COLS), rev),
                  pl.BlockSpec((8, SHIFT_COLS), lambda i: (jnp.maximum((nt - 1 - i) * (tm // 8) - 1, 0), 0))]
                 + _prep_specs(tm) + [row] * 10
                 + [pair] * 3 + [pl.BlockSpec((IN_COLS, D_MODEL), lambda i: (0, 0)), wide,
                                 pl.BlockSpec((1, D_MODEL), lambda i: (0, 0)), wide],
        out_specs=[pl.BlockSpec((tm, IN_COLS), rev), wide, part(D_MODEL), part(SHIFT_COLS), part(RW), mat, part(RW), mat,
                   mat, part(RW), part(RW)],
        out_shape=[jax.ShapeDtypeStruct((t, IN_COLS), BF16), jax.ShapeDtypeStruct((t, D_MODEL), F32),
                   jax.ShapeDtypeStruct((8, D_MODEL), F32), jax.ShapeDtypeStruct((8, SHIFT_COLS), F32),
                   jax.ShapeDtypeStruct((8, RW), F32), jax.ShapeDtypeStruct((128, RW), F32),
                   jax.ShapeDtypeStruct((8, RW), F32), jax.ShapeDtypeStruct((128, RW), F32),
                   jax.ShapeDtypeStruct((128, RW), F32), jax.ShapeDtypeStruct((8, RW), F32),
                   jax.ShapeDtypeStruct((8, RW), F32)],
        scratch_shapes=[pltpu.VMEM((8, SHIFT_COLS), F32)],
        compiler_params=_params(("arbitrary",)),
    )(proj, proj, *pw, *douts, dq, dk, dv, win, x, g1, dx1)


def _combine_bwd(dyb, o, l, og):
    t = dyb.shape[0]
    tm = _COMB_TM

    def body(d_ref, o_ref, l_ref, og_ref, do_ref, dl_ref, dog_ref):
        ones = jnp.ones((tm, 1), F32)
        dog = []
        for p in range(N_PAIR):
            cols = slice(128 * p, 128 * (p + 1))
            _, vjp = jax.vjp(_combine_fn, o_ref[0, p], o_ref[1, p], o_ref[2, p], l_ref[0, p], l_ref[1, p], l_ref[2, p],
                             ones * og_ref[:, cols])
            res = vjp(d_ref[:, cols])
            for b in range(3):
                do_ref[b, p] = res[b]
                dl_ref[b, p] = res[3 + b]
            dog.append(_colsum8(res[6]))
        _acc(dog_ref, jnp.concatenate(dog, axis=1), pl.program_id(0) == 0)

    blk = pl.BlockSpec((3, N_PAIR, tm, 128), lambda i: (0, 0, i, 0))
    return pl.pallas_call(
        body, name="attn_combine_bwd", grid=(t // tm,),
        in_specs=[pl.BlockSpec((tm, RW), lambda i: (i, 0)), blk, blk, pl.BlockSpec((1, RW), lambda i: (0, 0))],
        out_specs=[blk, blk, pl.BlockSpec((8, RW), lambda i: (0, 0))],
        out_shape=[jax.ShapeDtypeStruct((3, N_PAIR, t, 128), F32)] * 2 + [jax.ShapeDtypeStruct((8, RW), F32)],
        compiler_params=_params(("arbitrary",)),
    )(dyb, o, l, og)


def _attn_bwd(do, dl, o, lse, qkv):
    t = qkv.shape[2]

    def body(do_ref, dl_ref, o_ref, l_ref, q_ref, k_ref, v_ref, dq_ref, dk_ref, dv_ref):
        @pl.when(pl.program_id(1) == 0)
        def _():
            for ref in (dq_ref, dk_ref, dv_ref):
                ref[...] = jnp.zeros_like(ref)

        def unit(di, places, has_prev):
            cur = [_dilated_rows(d, r, n) for d, r, n in places]
            q, kc, vc = [_take(ref, (0,), cur) for ref in (q_ref, k_ref, v_ref)]
            kp = vp = None
            if has_prev:
                prv = [_dilated_rows(d, r, n - 1) for d, r, n in places]
                kp, vp = [_take(ref, (0,), prv) for ref in (k_ref, v_ref)]
            res = _attn_block_bwd(q, kc, vc, kp, vp, *[_take(ref, (0,), cur) for ref in (o_ref, l_ref, do_ref, dl_ref)])
            _put(dq_ref, (), cur, res[0], add=True)
            _put(dk_ref, (), cur, res[1], add=True)
            _put(dv_ref, (), cur, res[2], add=True)
            if has_prev:
                _put(dk_ref, (), prv, res[3], add=True)
                _put(dv_ref, (), prv, res[4], add=True)

        _for_each_sequence(t, unit)

    spec = lambda j: pl.BlockSpec((1, ATTN_GROUP, t, 128), lambda i, b: (j, i, 0, 0))
    branch = pl.BlockSpec((1, ATTN_GROUP, t, 128), lambda i, b: (b, i, 0, 0))
    out = pl.BlockSpec((ATTN_GROUP, t, 128), lambda i, b: (i, 0, 0))
    return pl.pallas_call(
        body, name="attn_bwd", grid=(N_PAIR // ATTN_GROUP, len(DILATIONS)),
        in_specs=[branch] * 4 + [spec(0), spec(1), spec(2)], out_specs=[out] * 3,
        out_shape=[jax.ShapeDtypeStruct((N_PAIR, t, 128), F32)] * 3,
        compiler_params=_params(("parallel", "arbitrary")),
    )(do, dl, o, lse, qkv, qkv, qkv)


def _pad_lora(w, lo):
    z = jnp.zeros((64, RW), F32)
    return jnp.concatenate([w, z], axis=0) if lo == 0 else jnp.concatenate([z, w], axis=0)


def _local_step(x, tgt, win, vecs, w2, a2, g2m, get_rest, send_rest):
    pw = (vecs["mu_shift"], vecs["decay_w0"], _pad_lora(w2, 0), vecs["iclr_a0"], _pad_lora(a2, 64), g2m,
          vecs["k_k"], vecs["k_a"])
    h, proj, qkv, r, lw, k2, v, kk, a, g = _in_proj_prep(x, vecs["mix_norm_g"], win, pw)
    y, s0s = _wkv_fwd(r, lw, k2, v, kk, a)
    o_att, l_att = _attn_fwd(qkv)
    ycat = _mixers_out(y, r, k2, v, g, vecs["ln_x_w"], vecs["ln_x_b"], vecs["r_k"], o_att, l_att, vecs["attn_out_g"])
    wout, wg, wu, wd = get_rest(ycat)
    h2, act, dx2b, dgt, dup, dx1b, dx1, dya, dyb, loss8, dgf, dg2n = _ffn_all(
        x, ycat, wg, wu, wd, wout, vecs["ffn_norm_g"], vecs["final_norm_g"], tgt)
    gw = {
        "w_down": _wgrad(act, dx2b, 1408, 1024, "wgrad_down"),
        "w_gate": _wgrad(dgt, h2, 1408, 1024, "wgrad_gate"),
        "w_up": _wgrad(dup, h2, 1408, 1024, "wgrad_up"),
        "w_out": _wgrad(ycat, dx1b, 1024, 1024, "wgrad_out"),
    }

    lnw = vecs["ln_x_w"] + send_rest(gw)[0, 0]
    dy, dr_p, dk2_p, dv_p, dg, dlnw, dlnb, drk = _post_bwd(dya, y, r, k2, v, g, lnw, vecs["ln_x_b"], vecs["r_k"])
    dr_s, dlw, dk2_s, dv_s, dkk, da = _wkv_bwd(dy, s0s, r, lw, k2, v, kk, a)
    do_att, dl_att, dog = _combine_bwd(dyb, o_att, l_att, vecs["attn_out_g"])
    dq, dk, dv = _attn_bwd(do_att, dl_att, o_att, l_att, qkv)
    dproj, dx, dg1, dmu, dw0, dw2p, da0, da2p, dg2m, dk_k, dk_a = _prep_in_proj_bwd(
        proj, pw, (dr_p, dr_s, dlw, dk2_p, dk2_s, dv_p, dv_s, dkk, da, dg), dq, dk, dv, win, x, vecs["mix_norm_g"], dx1)
    gw["w_in"] = _wgrad(dproj, h, 1664, 1024, "wgrad_in")
    gw["decay_w2"] = dw2p[:64]
    gw["iclr_a2"] = da2p[64:]
    gw["gate_g2"] = dg2m
    gv = {"mix_norm_g": dg1, "mu_shift": dmu, "decay_w0": dw0, "iclr_a0": da0, "k_k": dk_k, "k_a": dk_a, "r_k": drk,
          "ln_x_w": dlnw, "ln_x_b": dlnb, "attn_out_g": dog, "ffn_norm_g": dg2n, "final_norm_g": dgf}
    return loss8, dx, gw, gv


N_CHIP = 4
N_DEV = 8
MATS = ("w_in", "w_out", "w_gate", "w_up", "w_down")
LORAS = ("decay_w2", "iclr_a2", "gate_g2")
VECS = (("mix_norm_g", 1024), ("mu_shift", 1792), ("decay_w0", 512), ("iclr_a0", 512), ("k_k", 512), ("k_a", 512),
        ("r_k", 512), ("ln_x_w", 512), ("ln_x_b", 512), ("attn_out_g", 512), ("ffn_norm_g", 1024),
        ("final_norm_g", 1024))
N_VEC = sum(n for _, n in VECS)
N_SMALL = N_VEC + 128
ANY = pl.BlockSpec(memory_space=pl.ANY)


def _flip(v, f):
    return 1 - v if f else v


class _Me:
    def __init__(self, mode):
        x, y, c = lax.axis_index("x"), lax.axis_index("y"), lax.axis_index("c")
        self.core, self.chip, self.dev = c, 2 * x + y, 4 * x + 2 * y + c
        self.sibling = (x, y, 1 - c)
        if mode == "chips":
            self.peers = [(px, py, c) for px, py in ((1 - x, y), (x, 1 - y), (1 - x, 1 - y))]
        else:
            self.peers = [(_flip(x, k & 4), _flip(y, k & 2), _flip(c, k & 1)) for k in range(1, N_DEV)]


def _half(core, rows):
    h = rows // 2
    return pl.ds(pl.multiple_of(core * h, h), h)


_BY_CHIP = ("gather", "whole", "chipsum")


def _peer_copy(srcs, dsts, kinds, send_sems, recv_sems, me, j, i, incoming):
    px, py, pc = me.peers[j]
    pchip, pdev = 2 * px + py, 4 * px + 2 * py + pc
    src, dst, kind = srcs[i], dsts[i], kinds[i]
    if kind in ("gather", "whole"):
        rows = _half(me.core, src.shape[1]) if kind == "gather" else pl.ds(0, src.shape[1])
        src, dst = src.at[me.chip, rows], dst.at[pchip if incoming else me.chip, rows]
    elif kind == "scatter":
        src, dst = src.at[pchip, _half(pc, src.shape[1])], dst.at[pdev if incoming else me.dev]
    elif kind == "chipsum":
        src, dst = src.at[pchip], dst.at[pchip if incoming else me.chip]
    else:
        dst = dst.at[pdev if incoming else me.dev]
    n = len(srcs)
    return pltpu.make_async_remote_copy(src_ref=src, dst_ref=dst, send_sem=send_sems.at[n * j + i],
                                        recv_sem=recv_sems.at[n * j + i], device_id=(px, py, pc), device_id_type=MESH)


def _mode(kinds):
    return "chips" if kinds[0] in _BY_CHIP else "devs"


def _npeer(kinds):
    return N_CHIP - 1 if kinds[0] in _BY_CHIP else N_DEV - 1


def _sibling_halves(gs, name):
    n = len(gs)

    def body(*refs):
        srcs, dsts, send_sems, recv_sems = refs[:n], refs[n:2 * n], refs[2 * n], refs[2 * n + 1]
        me = _Me("chips")

        def copy(i, p):
            return pltpu.make_async_remote_copy(
                src_ref=srcs[i].at[p, _half(1 - me.core, srcs[i].shape[1])], dst_ref=dsts[i].at[p],
                send_sem=send_sems.at[N_CHIP * i + p], recv_sem=recv_sems.at[N_CHIP * i + p],
                device_id=me.sibling, device_id_type=MESH)

        copies = [copy(i, p) for i in range(n) for p in range(N_CHIP)]
        for cp in copies:
            cp.start()
        for cp in copies:
            cp.wait()

    return pl.pallas_call(
        body, name=name, in_specs=[ANY] * n, out_specs=[ANY] * n,
        out_shape=[jax.ShapeDtypeStruct((N_CHIP, g.shape[1] // 2, g.shape[2]), g.dtype) for g in gs],
        scratch_shapes=[pltpu.SemaphoreType.DMA((N_CHIP * n,)), pltpu.SemaphoreType.DMA((N_CHIP * n,))],
    )(*gs)


def _add_halves(g, other, core, tr, name):
    _, h, cols = other.shape

    def body(core_ref, g_ref, o_ref, out_ref):
        out_ref[...] = (g_ref[...].astype(F32) + o_ref[...].astype(F32)).astype(BF16)

    blk = lambda off: pl.BlockSpec((1, tr, cols), lambda p, i, core_ref: (p, core_ref[0] * (h // tr) * off + i, 0))
    return pl.pallas_call(
        body, name=name,
        grid_spec=pltpu.PrefetchScalarGridSpec(num_scalar_prefetch=1, grid=(N_CHIP, h // tr),
                                               in_specs=[blk(1), blk(0)], out_specs=blk(0)),
        out_shape=jax.ShapeDtypeStruct(other.shape, BF16),
        compiler_params=_params(("parallel", "parallel")),
    )(core, g, other)


def _swap_gathered(lands, name):
    n = len(lands)

    def body(*refs):
        dsts, send_sems, recv_sems = refs[n:2 * n], refs[2 * n], refs[2 * n + 1]
        me = _Me("chips")

        def copy(j, i, incoming):
            px, py, _ = me.peers[j]
            rows_out, rows_in = _half(me.core, dsts[i].shape[1]), _half(1 - me.core, dsts[i].shape[1])
            return pltpu.make_async_remote_copy(
                src_ref=dsts[i].at[2 * px + py, rows_out], dst_ref=dsts[i].at[2 * px + py, rows_in if incoming else rows_out],
                send_sem=send_sems.at[n * j + i], recv_sem=recv_sems.at[n * j + i], device_id=me.sibling, device_id_type=MESH)

        sends = [copy(j, i, False) for j in range(3) for i in range(n)]
        for cp in sends:
            cp.start()
        for j in range(3):
            for i in range(n):
                copy(j, i, True).wait_recv()
        for cp in sends:
            cp.wait_send()

    return pl.pallas_call(
        body, name=name, in_specs=[ANY] * n, out_specs=[ANY] * n,
        out_shape=[jax.ShapeDtypeStruct(l.shape, l.dtype) for l in lands],
        input_output_aliases={i: i for i in range(n)},
        scratch_shapes=[pltpu.SemaphoreType.DMA((3 * n,)), pltpu.SemaphoreType.DMA((3 * n,))],
    )(*lands)


def _join_halves(sums, name):
    n = len(sums)

    def body(*refs):
        dsts, send_sems, recv_sems = refs[n:2 * n], refs[2 * n], refs[2 * n + 1]
        me = _Me("chips")

        def copy(i, incoming):
            mine, other = _half(me.core, dsts[i].shape[0]), _half(1 - me.core, dsts[i].shape[0])
            return pltpu.make_async_remote_copy(src_ref=dsts[i].at[mine], dst_ref=dsts[i].at[other if incoming else mine],
                                                send_sem=send_sems.at[i], recv_sem=recv_sems.at[i],
                                                device_id=me.sibling, device_id_type=MESH)

        sends = [copy(i, False) for i in range(n)]
        for cp in sends:
            cp.start()
        for i in range(n):
            copy(i, True).wait_recv()
        for cp in sends:
            cp.wait_send()

    return pl.pallas_call(
        body, name=name, in_specs=[ANY] * n, out_specs=[ANY] * n,
        out_shape=[jax.ShapeDtypeStruct(s.shape, s.dtype) for s in sums],
        input_output_aliases={i: i for i in range(n)},
        scratch_shapes=[pltpu.SemaphoreType.DMA((n,)), pltpu.SemaphoreType.DMA((n,))],
    )(*sums)


HBM = pl.BlockSpec(memory_space=pltpu.HBM)
SEM = pl.BlockSpec(memory_space=pltpu.SEMAPHORE)
EFFECT = pltpu.SideEffectType.DATAFLOW_SIDE_EFFECTING


def _swap_start(arrs, lands, kinds, name):
    n = len(lands)
    ops = list(lands) if arrs is None else [*arrs, *lands]
    k = len(ops)

    def body(*refs):
        srcs, dsts, send_sems, recv_sems, token = refs[:n], refs[k - n:k], refs[k], refs[k + 1], refs[-1]
        me = _Me(_mode(kinds))
        for j in range(len(me.peers)):
            for i in range(n):
                _peer_copy(srcs, dsts, kinds, send_sems, recv_sems, me, j, i, False).start()
        token[...] = jnp.zeros_like(token)

    ns = _npeer(kinds) * n
    outs = pl.pallas_call(
        body, name=name,
        out_shape=(pltpu.SemaphoreType.DMA((ns,)), pltpu.SemaphoreType.DMA((ns,)),
                   *[pltpu.HBM(a.shape, a.dtype) for a in ops], jax.ShapeDtypeStruct((8, 128), F32)),
        in_specs=[HBM] * k, out_specs=(SEM, SEM, *[HBM] * k, pl.BlockSpec(memory_space=pltpu.VMEM)),
        input_output_aliases={i: 2 + i for i in range(k)},
        compiler_params=pltpu.CompilerParams(has_side_effects=EFFECT),
    )(*[pltpu.with_memory_space_constraint(a, pltpu.HBM) for a in ops])
    return outs[0], outs[1], outs[2:2 + k - n], outs[2 + k - n:2 + k], outs[-1]


def _swap_wait(send_sems, recv_sems, srcs_thru, lands_thru, after, kinds, name):
    n = len(lands_thru)
    ops = [*srcs_thru, *lands_thru]
    k = len(ops)

    def body(*refs):
        srcs, dsts, s_sems, r_sems = refs[:n], refs[k - n:k], refs[k], refs[k + 1]
        me = _Me(_mode(kinds))
        for j in range(len(me.peers)):
            for i in range(n):
                cp = _peer_copy(srcs, dsts, kinds, s_sems, r_sems, me, j, i, True)
                cp.wait_send()
                cp.wait_recv()

    outs = pl.pallas_call(
        body, name=name,
        out_shape=tuple(pltpu.HBM(a.shape, a.dtype) for a in ops),
        in_specs=[HBM] * k + [SEM, SEM, ANY], out_specs=tuple([HBM] * k),
        input_output_aliases={i: i for i in range(k)},
        compiler_params=pltpu.CompilerParams(has_side_effects=EFFECT),
    )(*ops, send_sems, recv_sems, after)
    return outs[k - n:]


def _adamw(w, g, m, v):
    m = ADAM_B1 * m + (1.0 - ADAM_B1) * g
    v = ADAM_B2 * v + (1.0 - ADAM_B2) * (g * g)
    m_hat = m / (1.0 - ADAM_B1 ** ADAM_STEP)
    v_hat = v / (1.0 - ADAM_B2 ** ADAM_STEP)
    delta = -ADAM_LR * (m_hat / (jnp.sqrt(v_hat) + ADAM_EPS) + ADAM_WD * w)
    return delta, m, v


def _reduce8(rbuf, core, tr, name):
    slots, h, cols = rbuf.shape

    def body(core_ref, r_ref, g_ref):
        g = r_ref[0].astype(F32)
        for s in range(1, slots):
            g = g + r_ref[s].astype(F32)
        g_ref[...] = g

    return pl.pallas_call(
        body, name=name,
        grid_spec=pltpu.PrefetchScalarGridSpec(
            num_scalar_prefetch=1, grid=(h // tr,),
            in_specs=[pl.BlockSpec((slots, tr, cols), lambda i, core_ref: (0, i, 0))],
            out_specs=pl.BlockSpec((tr, cols), lambda i, core_ref: (core_ref[0] * (h // tr) + i, 0))),
        out_shape=jax.ShapeDtypeStruct((2 * h, cols), F32),
        compiler_params=_params(("parallel",)),
    )(core, rbuf)


def _adamw_call(g, w, m, v, tr, name):
    _, rows, cols = w.shape

    def body(g_in, w_ref, m_ref, v_ref, g_ref, d_ref, nm_ref, nv_ref):
        g = g_in[...]
        g_ref[0] = g
        d_ref[0], nm_ref[0], nv_ref[0] = _adamw(w_ref[0], g, m_ref[0], v_ref[0])

    row = pl.BlockSpec((1, tr, cols), lambda i: (0, i, 0))
    return pl.pallas_call(
        body, name=name, grid=(rows // tr,),
        in_specs=[pl.BlockSpec((tr, cols), lambda i: (i, 0)), row, row, row], out_specs=[row] * 4,
        out_shape=[jax.ShapeDtypeStruct(w.shape, F32)] * 4,
        compiler_params=_params(("parallel",)),
    )(g, w, m, v)


def _rowsum_small(parts, loss8):
    def body(*refs):
        out = refs[-1]
        c0 = 0
        for ref in refs[:-1]:
            n = ref.shape[1]
            out[:, c0:c0 + n] = jnp.sum(ref[...], axis=0, keepdims=True)
            c0 += n

    return pl.pallas_call(body, name="rowsum_small", out_shape=jax.ShapeDtypeStruct((1, N_SMALL), F32))(*parts, loss8)


def _reduce_adamw_small(sbuf, ws, ms, vs):
    nv = len(ws)

    def body(*refs):
        s_ref, ins, outs = refs[0], refs[1:1 + 3 * nv], refs[1 + 3 * nv:]
        tot = s_ref[0]
        for s in range(1, N_DEV):
            tot = tot + s_ref[s]
        c0 = 0
        for i in range(nv):
            n = ins[i].shape[1]
            g = tot[:, c0:c0 + n]
            outs[i][...] = g
            outs[nv + i][...], outs[2 * nv + i][...], outs[3 * nv + i][...] = _adamw(
                ins[i][...], g, ins[nv + i][...], ins[2 * nv + i][...])
            c0 += n
        outs[-1][...] = tot[:, c0:]

    return pl.pallas_call(
        body, name="reduce_adamw_small",
        out_shape=[jax.ShapeDtypeStruct(a.shape, F32) for a in ws] * 4 + [jax.ShapeDtypeStruct((1, 128), F32)],
    )(sbuf, *ws, *ms, *vs)


_TRANSPOSED = ("w_in", "w_gate", "w_up")
_ROW_STACKED = MATS
_ADAM_TILE = {"w_in": 208, "w_out": 256, "w_gate": 176, "w_up": 176, "w_down": 176, "lora": 256}
_SUM_TILE = {"w_in": 208, "w_out": 128, "w_gate": 176, "w_up": 176, "w_down": 176, "lora": 128}


def _full(n, stacked):
    p, r, c = stacked.shape
    if n in _ROW_STACKED:
        return stacked.reshape(p * r, c)
    return jnp.transpose(stacked, (1, 0, 2)).reshape(r, p * c)


def _by_chip(n, full):
    if n in _ROW_STACKED:
        return full.reshape(N_CHIP, full.shape[0] // N_CHIP, full.shape[1])
    r, c = full.shape
    return jnp.transpose(full.reshape(r, N_CHIP, c // N_CHIP), (1, 0, 2))


def _with_own(land_shape, dtype, own, slot):
    return lax.dynamic_update_slice(lax.empty(land_shape, dtype), own[None], (slot,) + (0,) * own.ndim)


def _cast_into_slot(a, chip, tr, name, after=None):
    rows, cols = a.shape

    def body(chip_ref, a_ref, *rest):
        rest[-1][0] = a_ref[...].astype(BF16)

    extra = [] if after is None else [after]
    return pl.pallas_call(
        body, name=name,
        grid_spec=pltpu.PrefetchScalarGridSpec(
            num_scalar_prefetch=1, grid=(rows // tr,),
            in_specs=[pl.BlockSpec((tr, cols), lambda i, chip_ref: (i, 0))] + [ANY] * len(extra),
            out_specs=pl.BlockSpec((1, tr, cols), lambda i, chip_ref: (chip_ref[0], i, 0))),
        out_shape=jax.ShapeDtypeStruct((N_CHIP, rows, cols), BF16),
        compiler_params=_params(("parallel",)),
    )(chip, a, *extra)


def kernel(x, mix_norm_g, w_in, mu_shift, decay_w0, decay_w2, iclr_a0, iclr_a2, gate_g2, k_k, k_a, r_k, ln_x_w, ln_x_b, attn_out_g, w_out, ffn_norm_g, w_gate, w_up, w_down, final_norm_g, loss_target, m_mix_norm_g, m_w_in, m_mu_shift, m_decay_w0, m_decay_w2, m_iclr_a0, m_iclr_a2, m_gate_g2, m_k_k, m_k_a, m_r_k, m_ln_x_w, m_ln_x_b, m_attn_out_g, m_w_out, m_ffn_norm_g, m_w_gate, m_w_up, m_w_down, m_final_norm_g, v_mix_norm_g, v_w_in, v_mu_shift, v_decay_w0, v_decay_w2, v_iclr_a0, v_iclr_a2, v_gate_g2, v_k_k, v_k_a, v_r_k, v_ln_x_w, v_ln_x_b, v_attn_out_g, v_w_out, v_ffn_norm_g, v_w_gate, v_w_up, v_w_down, v_final_norm_g):
    names = ("mix_norm_g", "w_in", "mu_shift", "decay_w0", "decay_w2", "iclr_a0", "iclr_a2", "gate_g2", "k_k", "k_a",
             "r_k", "ln_x_w", "ln_x_b", "attn_out_g", "w_out", "ffn_norm_g", "w_gate", "w_up", "w_down", "final_norm_g")
    w = dict(zip(names, (mix_norm_g, w_in, mu_shift, decay_w0, decay_w2, iclr_a0, iclr_a2, gate_g2, k_k, k_a, r_k,
                         ln_x_w, ln_x_b, attn_out_g, w_out, ffn_norm_g, w_gate, w_up, w_down, final_norm_g)))
    m = dict(zip(names, (m_mix_norm_g, m_w_in, m_mu_shift, m_decay_w0, m_decay_w2, m_iclr_a0, m_iclr_a2, m_gate_g2,
                         m_k_k, m_k_a, m_r_k, m_ln_x_w, m_ln_x_b, m_attn_out_g, m_w_out, m_ffn_norm_g, m_w_gate,
                         m_w_up, m_w_down, m_final_norm_g)))
    v = dict(zip(names, (v_mix_norm_g, v_w_in, v_mu_shift, v_decay_w0, v_decay_w2, v_iclr_a0, v_iclr_a2, v_gate_g2,
                         v_k_k, v_k_a, v_r_k, v_ln_x_w, v_ln_x_b, v_attn_out_g, v_w_out, v_ffn_norm_g, v_w_gate,
                         v_w_up, v_w_down, v_final_norm_g)))
    first = ("w_in", "lora")
    rest = ("w_out", "w_gate", "w_up", "w_down")
    xi, yi, ci = lax.axis_index("x"), lax.axis_index("y"), lax.axis_index("c")
    my_chip, my_dev = 2 * xi + yi, 4 * xi + 2 * yi + ci
    gather, scatter = ("gather",) * 4, ("scatter",) * 4

    def stored(d):
        out = {n: jnp.transpose(d[n][0]) if n in _TRANSPOSED else d[n][0] for n in MATS}
        out["lora"] = jnp.concatenate([d[n][0] for n in LORAS], axis=0)
        return out

    ws, ms, vs = stored(w), stored(m), stored(v)
    lora_rows = [(0, 64), (64, 128), (128, 256)]
    chip = jnp.reshape(my_chip, (1,)).astype(jnp.int32)
    early = _swap_start(None, [_cast_into_slot(ws["w_in"], chip, _ADAM_TILE["w_in"], "cast_w_in"),
                               _with_own((N_CHIP,) + ws["lora"].shape, F32, ws["lora"], my_chip)], gather[:2],
                        "gather_first_start")
    lands = [_cast_into_slot(ws[n], chip, _ADAM_TILE[n], "cast_" + n, after=early[4]) for n in rest]
    gather_rest = ("gather", "gather", "whole", "whole")
    ssem, rsem, srcs_thru, lands_thru, tok = _swap_start(None, lands, gather_rest, "gather_rest_start")
    got = _swap_wait(early[0], early[1], early[2], early[3], tok, gather[:2], "gather_first_wait")
    win_all, lora_all = _swap_gathered(got, "gather_first_halves")
    win = _full("w_in", win_all)
    w2, a2, g2m = (_full(n, lora_all[:, a:b]) for n, (a, b) in zip(LORAS, lora_rows))

    vecs = {n: w[n].reshape(1, sz) for n, sz in VECS}
    vecs["mix_norm_g"] = vecs["mix_norm_g"] + tok[0, 0]

    def get_rest(after):
        got_rest = _swap_wait(ssem, rsem, srcs_thru, lands_thru, after, gather_rest, "gather_rest_wait")
        swapped = _swap_gathered(got_rest[:2], "gather_rest_halves")
        return [_full(n, z) for n, z in zip(rest, [*swapped, *got_rest[2:]])]

    flight = []

    def my_half(g):
        h = g.shape[1] // 2
        return lax.dynamic_slice(g, (my_chip, ci * h, 0), (1, h, g.shape[2]))[0]

    def send_rest(gw):
        gs = [_by_chip(n, gw[n]) for n in rest]
        into = [_with_own((N_DEV,) + my_half(g).shape, BF16, my_half(g), my_dev) for g in gs]
        flight.extend(_swap_start(gs, into, scatter, "exchange_rest_start"))
        return flight[4]

    loss8, dx, gw, gv = _local_step(x[0], loss_target[0], win, vecs, w2, a2, g2m, get_rest, send_rest)

    core = jnp.reshape(ci, (1,)).astype(jnp.int32)
    gs = [_by_chip("w_in", gw["w_in"]),
          jnp.concatenate([_by_chip(n, gw[n]) for n in LORAS], axis=1).astype(BF16)]
    theirs = _sibling_halves(gs, "presum_halves")
    sums = [_add_halves(g, o, core, _SUM_TILE[n], "chipsum_" + n) for n, g, o in zip(first, gs, theirs)]
    own = [lax.dynamic_index_in_dim(s, my_chip, 0, keepdims=False) for s in sums]
    last = _swap_start(sums, [_with_own(s.shape, BF16, o, my_chip) for s, o in zip(sums, own)], ("chipsum",) * 2,
                       "exchange_first_start")
    small = _rowsum_small([gv[n] for n, _ in VECS], loss8 + last[4])
    vecs_out = _swap_start([small], [_with_own((N_DEV,) + small.shape, F32, small, my_dev)], ("all",),
                           "exchange_vectors_start")


    def update(group, rbufs, tag):
        sums = [_reduce8(rb, core, _SUM_TILE[n], "reduce_" + n) for n, rb in zip(group, rbufs)]
        gsum = _join_halves(sums, "join_halves_" + tag)
        out = {}
        for n, g in zip(group, gsum):
            r = _adamw_call(g, ws[n][None], ms[n][None], vs[n][None], _ADAM_TILE[n], "adamw_" + n)
            if n == "lora":
                for name, (a, b) in zip(LORAS, lora_rows):
                    out[name] = [z[:, a:b] for z in r]
            else:
                out[n] = [jnp.transpose(z[0])[None] for z in r] if n in _TRANSPOSED else r
        return out, r[1]

    res, done = update(rest, _swap_wait(flight[0], flight[1], flight[2], flight[3], vecs_out[4], scatter,
                                        "exchange_rest_wait"), "rest")
    got = _swap_wait(last[0], last[1], last[2], last[3], done, ("chipsum",) * 2, "exchange_first_wait")
    res_first, done = update(first, got, "first")
    res.update(res_first)
    sbuf = _swap_wait(vecs_out[0], vecs_out[1], vecs_out[2], vecs_out[3], done, ("all",), "exchange_vectors_wait")[0]
    rows = lambda d: [d[n].reshape(1, sz) for n, sz in VECS]
    small_res = _reduce_adamw_small(sbuf, rows(w), rows(m), rows(v))

    outs = []
    for k in range(4):
        piece = {n: r[k] for n, r in res.items()}
        for i, (n, _) in enumerate(VECS):
            piece[n] = small_res[k * len(VECS) + i].reshape(w[n].shape)
        outs.extend(piece[n] for n in names)
    return (small_res[-1][0, 0], dx[None], *outs)
```

```python
import jax
import jax.numpy as jnp
from jax import lax
from jax.experimental import pallas as pl
from jax.experimental.pallas import tpu as pltpu

F32 = jnp.float32
BF16 = jnp.bfloat16

D_MODEL = 1024
HEAD_DIM = 64
RW = 512
N_PAIR = RW // 128
SHIFT_COLS = 1792
IN_COLS = 3328
D_FF = 2816
FF_CHUNK = 256
NORM_EPS = 1e-6
GN_EPS = 64e-5
CHUNK = 64
SUB = 16
WKV_PASSES = 1
ATTN_PASSES = 1
ATTN_BLOCK = 128
DILATIONS = (1, 4, 16)
NEG = -1e30
ADAM_LR, ADAM_B1, ADAM_B2, ADAM_EPS, ADAM_WD, ADAM_STEP = 0.001, 0.9, 0.999, 1e-08, 0.01, 10
VMEM_LIMIT = 56 * 1024 * 1024
MESH = pl.DeviceIdType.MESH


def _params(sem=None, **kw):
    return pltpu.CompilerParams(dimension_semantics=sem, vmem_limit_bytes=VMEM_LIMIT, **kw)


def _dot(a, b):
    return lax.dot_general(a, b, (((1,), (0,)), ((), ())), preferred_element_type=F32)


def _dot_nt(a, b):
    return lax.dot_general(a, b, (((1,), (1,)), ((), ())), preferred_element_type=F32)


def _dot_tn(a, b):
    return lax.dot_general(a, b, (((0,), (0,)), ((), ())), preferred_element_type=F32)


_FORMS = {"nn": ((1,), (0,)), "nt": ((1,), (1,)), "tn": ((0,), (0,))}


def _dg(a, b, form):
    if a.ndim == 3 or b.ndim == 3:
        nb = a.shape[0] if a.ndim == 3 else b.shape[0]
        return jnp.stack([_dg(a[i] if a.ndim == 3 else a, b[i] if b.ndim == 3 else b, form) for i in range(nb)], axis=0)
    return lax.dot_general(a, b, (_FORMS[form], ((), ())), preferred_element_type=F32)


def _split2(x):
    hi = x.astype(BF16)
    return hi, (x - hi.astype(F32)).astype(BF16)


def _split3(x):
    hi = x.astype(BF16)
    rest = x - hi.astype(F32)
    mid = rest.astype(BF16)
    return hi, mid, (rest - mid.astype(F32)).astype(BF16)


def _mm_raw(a, b, form, mode):
    if mode == 1:
        return _dg(a.astype(BF16), b.astype(BF16), form)
    if mode == 3:
        ah, al = _split2(a)
        bh, bl = _split2(b)
        return _dg(ah, bh, form) + (_dg(ah, bl, form) + _dg(al, bh, form))
    if mode == "L3":
        ab = a.astype(BF16)
        b1, b2, b3 = _split3(b)
        if form == "nn":
            n = b.shape[-1]
            wide = _dg(ab, jnp.concatenate([b1, b2, b3], axis=-1), form)
            return wide[..., :n] + (wide[..., n:2 * n] + wide[..., 2 * n:])
        return _dg(ab, b1, form) + (_dg(ab, b2, form) + _dg(ab, b3, form))
    assert mode == "R3", mode
    bb = b.astype(BF16)
    a1, a2, a3 = _split3(a)
    if form in ("nn", "nt"):
        m = a.shape[-2]
        tall = _dg(jnp.concatenate([a1, a2, a3], axis=-2), bb, form)
        return tall[..., :m, :] + (tall[..., m:2 * m, :] + tall[..., 2 * m:, :])
    return _dg(a1, bb, form) + (_dg(a2, bb, form) + _dg(a3, bb, form))


def _mm(a, b, form, mode):
    @jax.custom_vjp
    def f(a, b):
        return _mm_raw(a, b, form, mode)

    def fwd(a, b):
        return _mm_raw(a, b, form, mode), (a, b)

    def bwd(res, ct):
        a, b = res
        la = {1: 1, 3: 3, "L3": None, "R3": "R3"}[mode]
        lb = {1: 1, 3: 3, "L3": "L3", "R3": None}[mode]
        if form == "nn":
            da = None if la is None else _mm_raw(ct, b, "nt", la)
            db = None if lb is None else _mm_raw(a, ct, "tn", lb)
        elif form == "nt":
            da = None if la is None else _mm_raw(ct, b, "nn", la)
            db = None if lb is None else _mm_raw(ct, a, "tn", "R3" if lb == "L3" else lb)
        else:
            da = None if la is None else _mm_raw(b, ct, "nt", "L3" if la == "R3" else la)
            db = None if lb is None else _mm_raw(a, ct, "nn", lb)
        return (jnp.zeros_like(a) if da is None else da, jnp.zeros_like(b) if db is None else db)

    f.defvjp(fwd, bwd)
    return f(a, b)


def _seg_ones(n):
    r = lax.broadcasted_iota(jnp.int32, (n, n), 0) // HEAD_DIM
    c = lax.broadcasted_iota(jnp.int32, (n, n), 1) // HEAD_DIM
    return (r == c).astype(F32)


def _segsum(x, seg):
    return _mm(x, seg, "nn", "R3")


def _rms_fwd(x, g):
    rstd = lax.rsqrt(jnp.mean(x * x, axis=-1, keepdims=True) + NORM_EPS)
    return x * rstd * g


def _rms_bwd(dy, x, g):
    rstd = lax.rsqrt(jnp.mean(x * x, axis=-1, keepdims=True) + NORM_EPS)
    xn = x * rstd
    dxn = dy * g
    dx = rstd * (dxn - xn * jnp.mean(dxn * xn, axis=-1, keepdims=True))
    return dx, dy * xn


def _sigmoid(x):
    return 1.0 / (1.0 + jnp.exp(-x))


def _softplus(x):
    return jnp.maximum(x, 0.0) + jnp.log(1.0 + jnp.exp(-jnp.abs(x)))


def _acc(ref, val, first):
    @pl.when(first)
    def _():
        ref[...] = val

    @pl.when(jnp.logical_not(first))
    def _():
        ref[...] += val


def _colsum8(v):
    rows, n = v.shape
    return jnp.sum(v.reshape(rows // 8, 8, n), axis=0)


def _prep_fn(p, pprev, mu, w0, w2p, a0, a2p, g2, k_k, k_a):
    seg = _seg_ones(RW)
    ps = p + (pprev - p) * mu
    r = ps[:, 0:RW]
    k = ps[:, RW:2 * RW]
    v = ps[:, 2 * RW:3 * RW]
    xwa = ps[:, 3 * RW:3 * RW + 128]
    xg = ps[:, 3 * RW + 128:3 * RW + 256]
    wraw = -_softplus(-(w0 + _mm(jnp.tanh(xwa), w2p, "nn", 3))) - 0.5
    lw = -jnp.exp(wraw)
    a = _sigmoid(a0 + _mm(xwa, a2p, "nn", 3))
    g = _mm(_sigmoid(xg), g2, "nn", 3)
    kk = k * k_k
    kk = kk / jnp.maximum(jnp.sqrt(_segsum(kk * kk, seg)), 1e-12)
    k2 = k * (1.0 + (a - 1.0) * k_a)
    return r, lw, k2, v, kk, a, g


def _transposed(z):
    return jnp.stack([z[i].T for i in range(z.shape[0])], axis=0) if z.ndim == 3 else z.T


def _solve_unit_lower(lmat, rhs):
    c = lmat.shape[-1]
    row = lax.broadcasted_iota(jnp.int32, (c, c), 0)
    col = lax.broadcasted_iota(jnp.int32, (c, c), 1)
    eye = (row == col).astype(F32)
    ld = jnp.where(row // SUB == col // SUB, lmat, 0.0)
    lo = lmat - ld
    x = eye + ld
    m = ld
    mm = lambda p, q: _mm(p, q, "nn", WKV_PASSES)
    cat = jnp.concatenate
    m = mm(m, m)
    for _ in range(2):
        mx = mm(m, cat([m, x], axis=-1))
        m, x = mx[..., :c], x + mx[..., c:]
    x = x + mm(m, x)
    gw = mm(x, cat([lo, rhs], axis=-1))
    g, w = gw[..., :c], gw[..., c:]
    gg = mm(g, cat([g, w], axis=-1))
    w = w + gg[..., c:]
    return w + mm(gg[..., :c], w)


def _wkv_chunk_fn(s0, r, lw, k, v, kk, a):
    c = r.shape[-2]
    n = 2 * c
    row = lax.broadcasted_iota(jnp.int32, (n, n), 0)
    col = lax.broadcasted_iota(jnp.int32, (n, n), 1)
    same = (row // c) == (col // c)
    incl = jnp.logical_and(row >= col, same)
    strict = jnp.logical_and(row > col, same)
    sel = (lax.broadcasted_iota(jnp.int32, (n, 128), 0) // c) == (lax.broadcasted_iota(jnp.int32, (n, 128), 1) // HEAD_DIM)
    two = lambda z: jnp.concatenate([z, z], axis=-2)
    lw2 = two(lw)
    mm = lambda p_, q_, form: _mm(p_, q_, form, WKV_PASSES)
    cl = _mm(incl.astype(F32), lw2, "nn", "L3")
    p = jnp.exp(cl)
    pinv = jnp.exp(-cl)
    pprev = jnp.exp(cl - lw2)
    kk2 = two(kk)
    at = jnp.where(sel, -kk2 * pprev, 0.0)
    bt = jnp.where(sel, kk2 * two(a) * pinv, 0.0)
    kt = jnp.where(sel, two(k) * pinv, 0.0)
    rt = jnp.where(sel, two(r) * p, 0.0)
    vt = jnp.where(sel, two(v), 0.0)
    cat = jnp.concatenate
    bk = cat([bt, kt], axis=-2)
    arbk = mm(cat([at, rt], axis=-2), bk, "nt")
    ab, ak = jnp.where(strict, arbk[..., :n, :n], 0.0), jnp.where(strict, arbk[..., :n, n:], 0.0)
    rb, rk = jnp.where(incl, arbk[..., n:, :n], 0.0), jnp.where(incl, arbk[..., n:, n:], 0.0)
    s0t = _transposed(s0)
    u = _solve_unit_lower(ab, mm(cat([at, ak], axis=-1), cat([s0t, vt], axis=-2), "nn"))
    y2 = mm(cat([rt, rb, rk], axis=-1), cat([s0t, u, vt], axis=-2), "nn")
    plast = jnp.exp(jnp.sum(lw, axis=-2, keepdims=True))
    s1 = (s0 + mm(cat([u, vt], axis=-2), bk, "tn")) * plast
    r2 = lax.broadcasted_iota(jnp.int32, (128, 128), 0) // HEAD_DIM
    c2 = lax.broadcasted_iota(jnp.int32, (128, 128), 1) // HEAD_DIM
    return y2[..., :c, :] + y2[..., c:, :], jnp.where(r2 == c2, s1, 0.0)


def _post_fn(y, r, k2, v, g, lnw, lnb, rk):
    seg = _seg_ones(RW)
    mean = _segsum(y, seg) * (1.0 / HEAD_DIM)
    yc = y - mean
    var = _segsum(yc * yc, seg) * (1.0 / HEAD_DIM)
    yn = yc * lax.rsqrt(var + GN_EPS)
    out = yn * lnw + lnb + _segsum(r * k2 * rk, seg) * v
    return out * g


def _attn_block_fn(q, kc, vc, kp=None, vp=None):
    n = ATTN_BLOCK
    qi = lax.broadcasted_iota(jnp.int32, (n, n), 0)
    kj = lax.broadcasted_iota(jnp.int32, (n, n), 1)
    lane = lax.broadcasted_iota(jnp.int32, (1, 128), 1)
    scale = HEAD_DIM ** -0.5
    valid = kj <= qi
    keys, vals = kc, vc
    if kp is not None:
        valid = jnp.concatenate([valid, kj >= qi], axis=-1)
        keys, vals = jnp.concatenate([kc, kp], axis=-2), jnp.concatenate([vc, vp], axis=-2)
    m0 = (lane // HEAD_DIM) == 0
    q2 = jnp.concatenate([jnp.where(m0, q, 0.0), jnp.where(m0, 0.0, q)], axis=-2)
    valid2 = jnp.concatenate([valid, valid], axis=-2)
    s = jnp.where(valid2, _mm(q2, keys, "nt", ATTN_PASSES) * scale, NEG)
    m = jnp.max(s, axis=-1, keepdims=True)
    p = jnp.exp(s - m)
    den = jnp.sum(p, axis=-1, keepdims=True)
    o2 = _mm(p, vals, "nn", ATTN_PASSES) / den
    l2 = m + jnp.log(den)
    return jnp.where(m0, o2[..., :n, :], o2[..., n:, :]), jnp.where(m0, l2[..., :n, :], l2[..., n:, :])


def _attn_block_bwd(q, kc, vc, kp, vp, o, lse, do, dl):
    n = ATTN_BLOCK
    cat = jnp.concatenate
    qi = lax.broadcasted_iota(jnp.int32, (n, n), 0)
    kj = lax.broadcasted_iota(jnp.int32, (n, n), 1)
    m0 = (lax.broadcasted_iota(jnp.int32, (1, 128), 1) // HEAD_DIM) == 0
    scale = HEAD_DIM ** -0.5
    valid = kj <= qi
    keys, vals = kc, vc
    if kp is not None:
        valid = cat([valid, kj >= qi], axis=-1)
        keys, vals = cat([kc, kp], axis=-2), cat([vc, vp], axis=-2)
    stack = lambda z: cat([jnp.where(m0, z, 0.0), jnp.where(m0, 0.0, z)], axis=-2)
    q2, do2 = stack(q), stack(do)
    lse2 = cat([jnp.max(jnp.where(m0, lse, NEG), axis=-1, keepdims=True),
                jnp.max(jnp.where(m0, NEG, lse), axis=-1, keepdims=True)], axis=-2)
    delta = jnp.sum(do2 * cat([o, o], axis=-2), axis=-1, keepdims=True)
    dlse = jnp.sum(stack(dl), axis=-1, keepdims=True)
    mm = lambda a, b, form: _mm_raw(a, b, form, ATTN_PASSES)
    s = jnp.where(cat([valid, valid], axis=-2), mm(q2, keys, "nt") * scale, NEG)
    p = jnp.exp(s - lse2)
    ds = p * (mm(do2, vals, "nt") - delta + dlse)
    dq2 = mm(ds, keys, "nn") * scale
    dq = jnp.where(m0, dq2[..., :n, :], dq2[..., n:, :])
    dkeys = mm(ds, q2, "tn") * scale
    dvals = mm(p, do2, "tn")
    if kp is None:
        return dq, dkeys, dvals
    return dq, dkeys[..., :n, :], dvals[..., :n, :], dkeys[..., n:, :], dvals[..., n:, :]


def _combine_fn(o1, o2, o3, l1, l2, l3, og):
    seg = _seg_ones(o1.shape[-1])
    m = jnp.maximum(jnp.maximum(l1, l2), l3)
    e1, e2, e3 = jnp.exp(l1 - m), jnp.exp(l2 - m), jnp.exp(l3 - m)
    o = (e1 * o1 + e2 * o2 + e3 * o3) / (e1 + e2 + e3)
    o = o * lax.rsqrt(_segsum(o * o, seg) * (1.0 / HEAD_DIM) + NORM_EPS)
    return o * og


def _shifted(p, last8, first):
    prow = jnp.where(first, 0.0, last8[7:8, :])
    rolled = pltpu.roll(p, 1, axis=0)
    rid = lax.broadcasted_iota(jnp.int32, p.shape, 0)
    return jnp.where(rid == 0, prow, rolled)


_PREP_TM = 256


def _prep_specs(tm):
    vec = lambda n: pl.BlockSpec((1, n), lambda i: (0, 0))
    mat = lambda r, n: pl.BlockSpec((r, n), lambda i: (0, 0))
    return [vec(SHIFT_COLS), vec(RW), mat(128, RW), vec(RW), mat(128, RW), mat(128, RW), vec(RW), vec(RW)]


def _in_proj_prep(x, g1, win, pw):
    t = x.shape[0]
    tm = _PREP_TM

    def body(x_ref, g_ref, w_ref, mu, w0, w2p, a0, a2p, g2, k_k, k_a, h_ref, pa_ref, qkv_ref, *rest):
        outs, carry = rest[:7], rest[7]

        @pl.when(pl.program_id(0) == 0)
        def _():
            carry[...] = jnp.zeros_like(carry)

        h = _rms_fwd(x_ref[...], g_ref[...]).astype(BF16)
        h_ref[...] = h
        proj = _dot_nt(h, w_ref[...])
        p = proj[:, :SHIFT_COLS]
        pa_ref[...] = p
        for j in range(3):
            for pr in range(N_PAIR):
                c0 = SHIFT_COLS + j * RW + pr * 128
                qkv_ref[j, pr] = proj[:, c0:c0 + 128]
        pprev = _shifted(p, carry[...], pl.program_id(0) == 0)
        carry[...] = p[tm - 8:, :]
        res = _prep_fn(p, pprev, mu[...], w0[...], w2p[...], a0[...], a2p[...], g2[...], k_k[...], k_a[...])
        for o_ref, val in zip(outs, res):
            o_ref[...] = val

    row = pl.BlockSpec((tm, RW), lambda i: (i, 0))
    return pl.pallas_call(
        body, name="in_proj_prep", grid=(t // tm,),
        in_specs=[pl.BlockSpec((tm, D_MODEL), lambda i: (i, 0)), pl.BlockSpec((1, D_MODEL), lambda i: (0, 0)),
                  pl.BlockSpec((IN_COLS, D_MODEL), lambda i: (0, 0))] + _prep_specs(tm),
        out_specs=[pl.BlockSpec((tm, D_MODEL), lambda i: (i, 0)), pl.BlockSpec((tm, SHIFT_COLS), lambda i: (i, 0)),
                   pl.BlockSpec((3, N_PAIR, tm, 128), lambda i: (0, 0, i, 0))] + [row] * 7,
        out_shape=[jax.ShapeDtypeStruct((t, D_MODEL), BF16), jax.ShapeDtypeStruct((t, SHIFT_COLS), F32),
                   jax.ShapeDtypeStruct((3, N_PAIR, t, 128), F32)] + [jax.ShapeDtypeStruct((t, RW), F32)] * 7,
        scratch_shapes=[pltpu.VMEM((8, SHIFT_COLS), F32)],
        compiler_params=_params(("arbitrary",)),
    )(x, g1, win, *pw)


def _pairs(ref):
    return jnp.stack([ref[:, 128 * p:128 * (p + 1)] for p in range(N_PAIR)], axis=0)


def _wkv_fwd(r, lw, k2, v, kk, a):
    t = r.shape[0]
    nc = t // CHUNK

    def body(r_ref, lw_ref, k_ref, v_ref, kk_ref, a_ref, y_ref, s_ref, st):
        @pl.when(pl.program_id(0) == 0)
        def _():
            st[...] = jnp.zeros_like(st)

        s0 = st[...]
        s_ref[0] = s0
        y, s1 = _wkv_chunk_fn(s0, *[_pairs(ref) for ref in (r_ref, lw_ref, k_ref, v_ref, kk_ref, a_ref)])
        for p in range(N_PAIR):
            y_ref[:, 128 * p:128 * (p + 1)] = y[p]
        st[...] = s1

    blk = pl.BlockSpec((CHUNK, RW), lambda c: (c, 0))
    return pl.pallas_call(
        body, name="wkv_fwd", grid=(nc,),
        in_specs=[blk] * 6,
        out_specs=[blk, pl.BlockSpec((1, N_PAIR, 128, 128), lambda c: (c, 0, 0, 0))],
        out_shape=[jax.ShapeDtypeStruct((t, RW), F32), jax.ShapeDtypeStruct((nc, N_PAIR, 128, 128), F32)],
        scratch_shapes=[pltpu.VMEM((N_PAIR, 128, 128), F32)],
        compiler_params=_params(("arbitrary",)),
    )(r, lw, k2, v, kk, a)


_POST_TM = 512


ATTN_GROUP = 2


def _dilated_rows(d, r, n):
    if d == 1:
        return pl.ds(pl.multiple_of(n * ATTN_BLOCK, ATTN_BLOCK), ATTN_BLOCK)
    return pl.ds(r + n * (ATTN_BLOCK * d), ATTN_BLOCK, stride=d)


def _for_each_sequence(t, unit):
    for di, d in enumerate(DILATIONS):

        @pl.when(pl.program_id(1) == di)
        def _(di=di, d=d):
            nb = t // (ATTN_BLOCK * d)
            if d == 1:
                unit(di, [(d, 0, 0)], False)
                unit(di, [(d, 0, 1)], True)
                lax.fori_loop(1, nb // 2, lambda k, c: (unit(di, [(d, 0, 2 * k), (d, 0, 2 * k + 1)], True), c)[1], 0)
            else:

                def residues(r, carry):
                    unit(di, [(d, r, 0), (d, r + d // 2, 0)], False)
                    if nb > 1:
                        lax.fori_loop(1, nb, lambda n, c: (unit(di, [(d, r, n), (d, r + d // 2, n)], True), c)[1], 0)
                    return carry

                lax.fori_loop(0, d // 2, residues, 0)


def _take(ref, lead, rows_list):
    return jnp.stack([ref.at[(*lead, g)][rows, :] for rows in rows_list for g in range(ref.shape[len(lead)])], axis=0)


def _put(ref, lead, rows_list, val, add=False):
    k = 0
    for rows in rows_list:
        for g in range(ref.shape[len(lead)]):
            if add:
                ref.at[(*lead, g)][rows, :] += val[k]
            else:
                ref.at[(*lead, g)][rows, :] = val[k]
            k += 1


def _attn_fwd(qkv):
    t = qkv.shape[2]

    def body(q_ref, k_ref, v_ref, o_ref, l_ref):
        def unit(di, places, has_prev):
            cur = [_dilated_rows(d, r, n) for d, r, n in places]
            args = [_take(ref, (0,), cur) for ref in (q_ref, k_ref, v_ref)]
            if has_prev:
                prv = [_dilated_rows(d, r, n - 1) for d, r, n in places]
                args += [_take(ref, (0,), prv) for ref in (k_ref, v_ref)]
            o, lse = _attn_block_fn(*args)
            _put(o_ref, (0,), cur, o)
            _put(l_ref, (0,), cur, lse)

        _for_each_sequence(t, unit)

    spec = lambda j: pl.BlockSpec((1, ATTN_GROUP, t, 128), lambda i, b: (j, i, 0, 0))
    out = pl.BlockSpec((1, ATTN_GROUP, t, 128), lambda i, b: (b, i, 0, 0))
    return pl.pallas_call(
        body, name="attn_fwd", grid=(N_PAIR // ATTN_GROUP, len(DILATIONS)),
        in_specs=[spec(0), spec(1), spec(2)], out_specs=[out, out],
        out_shape=[jax.ShapeDtypeStruct((3, N_PAIR, t, 128), F32)] * 2,
        compiler_params=_params(("parallel", "arbitrary")),
    )(qkv, qkv, qkv)


_COMB_TM = 512


def _mixers_out(y, r, k2, v, g, lnw, lnb, rk, o, l, og):
    t = y.shape[0]
    tm = _COMB_TM

    def body(y_ref, r_ref, k_ref, v_ref, g_ref, lnw_ref, lnb_ref, rk_ref, o_ref, l_ref, og_ref, out_ref):
        out_ref[:, :RW] = _post_fn(y_ref[...], r_ref[...], k_ref[...], v_ref[...], g_ref[...],
                                   lnw_ref[...], lnb_ref[...], rk_ref[...]).astype(BF16)
        for p in range(N_PAIR):
            cols = slice(128 * p, 128 * (p + 1))
            out_ref[:, RW + 128 * p:RW + 128 * (p + 1)] = _combine_fn(
                o_ref[0, p], o_ref[1, p], o_ref[2, p], l_ref[0, p], l_ref[1, p], l_ref[2, p], og_ref[:, cols]).astype(BF16)

    row = pl.BlockSpec((tm, RW), lambda i: (i, 0))
    vec = pl.BlockSpec((1, RW), lambda i: (0, 0))
    blk = pl.BlockSpec((3, N_PAIR, tm, 128), lambda i: (0, 0, i, 0))
    return pl.pallas_call(
        body, name="mixers_out", grid=(t // tm,),
        in_specs=[row] * 5 + [vec] * 3 + [blk, blk, vec], out_specs=pl.BlockSpec((tm, D_MODEL), lambda i: (i, 0)),
        out_shape=jax.ShapeDtypeStruct((t, D_MODEL), BF16),
        compiler_params=_params(("parallel",)),
    )(y, r, k2, v, g, lnw, lnb, rk, o, l, og)


def _ffn_all(x, ycat, wg, wu, wd, wout, g2, gf, tgt):
    t = x.shape[0]
    tm = 256

    nch = D_FF // FF_CHUNK

    def body(x_ref, y_ref, wg_hbm, wu_hbm, wd_hbm, wo_hbm, g2_ref, gf_ref, t_ref,
             h_ref, act_ref, dx2b_ref, dgt_ref, dup_ref, dx1b_ref, dx1_ref, dya_ref, dyb_ref, loss_ref, dgf_ref, dg2_ref,
             gt_s, up_s, wg_ref, wu_ref, wd_ref, wo_ref, sem):
        first = pl.program_id(0) == 0
        whole = pltpu.make_async_copy(wo_hbm, wo_ref, sem.at[0])

        def piece(k, c):
            src, dst = ((wg_hbm, wg_ref), (wu_hbm, wu_ref), (wd_hbm, wd_ref))[k]
            rows = pl.ds(c * FF_CHUNK, FF_CHUNK)
            return pltpu.make_async_copy(src.at[rows], dst.at[rows], sem.at[1 + k * nch + c])

        def on_first(copies, what):
            @pl.when(first)
            def _():
                for cp in copies:
                    getattr(cp, what)()

        on_first([whole] + [piece(k, c) for c in range(nch) for k in (0, 1)] + [piece(2, c) for c in range(nch)], "start")
        on_first([whole], "wait")
        x1 = x_ref[...] + _dot(y_ref[...], wo_ref[...])
        h = _rms_fwd(x1, g2_ref[...]).astype(BF16)
        h_ref[...] = h
        for c0 in range(0, D_FF, FF_CHUNK):
            cols = slice(c0, c0 + FF_CHUNK)
            on_first([piece(0, c0 // FF_CHUNK), piece(1, c0 // FF_CHUNK)], "wait")
            gt = _dot_nt(h, wg_ref[cols, :])
            up = _dot_nt(h, wu_ref[cols, :])
            gt_s[:, cols] = gt.astype(BF16)
            up_s[:, cols] = up.astype(BF16)
            act_ref[:, cols] = (gt * _sigmoid(gt) * up).astype(BF16)
        on_first([piece(2, c) for c in range(nch)], "wait")
        x2 = x1 + _dot(act_ref[...], wd_ref[...])
        gf_ = gf_ref[...]
        diff = _rms_fwd(x2, gf_) - t_ref[...]
        lrow = 0.5 * jnp.sum(_colsum8(diff * diff), axis=1, keepdims=True) * (1.0 / D_MODEL)
        _acc(loss_ref, jnp.broadcast_to(lrow, (8, 128)), first)
        dx2, dgr = _rms_bwd(diff * (1.0 / D_MODEL), x2, gf_)
        _acc(dgf_ref, _colsum8(dgr), first)
        dx2b = dx2.astype(BF16)
        dx2b_ref[...] = dx2b
        for c0 in range(0, D_FF, FF_CHUNK):
            cols = slice(c0, c0 + FF_CHUNK)
            dact = _dot_nt(dx2b, wd_ref[cols, :])
            gt = gt_s[:, cols].astype(F32)
            sg = _sigmoid(gt)
            dgt_ref[:, cols] = (dact * up_s[:, cols].astype(F32) * sg * (1.0 + gt * (1.0 - sg))).astype(BF16)
            dup_ref[:, cols] = (dact * gt * sg).astype(BF16)
        dh = _dot(dgt_ref[...], wg_ref[...]) + _dot(dup_ref[...], wu_ref[...])
        dxn, dgr2 = _rms_bwd(dh, x1, g2_ref[...])
        _acc(dg2_ref, _colsum8(dgr2), first)
        dx1 = dx2 + dxn
        dx1_ref[...] = dx1
        dx1b = dx1.astype(BF16)
        dx1b_ref[...] = dx1b
        dy = _dot_nt(dx1b, wo_ref[...])
        dya_ref[...] = dy[:, :RW]
        dyb_ref[...] = dy[:, RW:]

    row = pl.BlockSpec((tm, D_MODEL), lambda i: (i, 0))
    wide = pl.BlockSpec((tm, D_FF), lambda i: (i, 0))
    half = pl.BlockSpec((tm, RW), lambda i: (i, 0))
    vec = pl.BlockSpec((1, D_MODEL), lambda i: (0, 0))
    part = pl.BlockSpec((8, D_MODEL), lambda i: (0, 0))
    bf = lambda n: jax.ShapeDtypeStruct((t, n), BF16)
    return pl.pallas_call(
        body, name="ffn_all", grid=(t // tm,),
        in_specs=[row, row, ANY, ANY, ANY, ANY, vec, vec, row],
        out_specs=[row, wide, row, wide, wide, row, row, half, half, pl.BlockSpec((8, 128), lambda i: (0, 0)), part, part],
        out_shape=[bf(D_MODEL), bf(D_FF), bf(D_MODEL), bf(D_FF), bf(D_FF), bf(D_MODEL),
                   jax.ShapeDtypeStruct((t, D_MODEL), F32), jax.ShapeDtypeStruct((t, RW), F32),
                   jax.ShapeDtypeStruct((t, RW), F32), jax.ShapeDtypeStruct((8, 128), F32),
                   jax.ShapeDtypeStruct((8, D_MODEL), F32), jax.ShapeDtypeStruct((8, D_MODEL), F32)],
        scratch_shapes=[pltpu.VMEM((tm, D_FF), BF16), pltpu.VMEM((tm, D_FF), BF16)]
                       + [pltpu.VMEM(w_.shape, BF16) for w_ in (wg, wu, wd, wout)] + [pltpu.SemaphoreType.DMA((1 + 3 * nch,))],
        compiler_params=_params(("arbitrary",)),
    )(x, ycat, wg, wu, wd, wout, g2, gf, tgt)


def _wgrad(a, b, tk, tn, name):
    t, kdim = a.shape
    ndim = b.shape[1]

    def body(a_ref, b_ref, o_ref):
        o_ref[...] = _dot_tn(a_ref[...], b_ref[...]).astype(BF16)

    return pl.pallas_call(
        body, name=name, grid=(kdim // tk, ndim // tn),
        in_specs=[pl.BlockSpec((t, tk), lambda i, j: (0, i)), pl.BlockSpec((t, tn), lambda i, j: (0, j))],
        out_specs=pl.BlockSpec((tk, tn), lambda i, j: (i, j)),
        out_shape=jax.ShapeDtypeStruct((kdim, ndim), BF16),
        compiler_params=_params(("parallel", "parallel")),
    )(a, b)


def _post_bwd(dya, y, r, k2, v, g, lnw, lnb, rk):
    t = y.shape[0]
    tm = _POST_TM

    def body(d_ref, y_ref, r_ref, k_ref, v_ref, g_ref, lnw_ref, lnb_ref, rk_ref,
             dy_ref, dr_ref, dk_ref, dv_ref, dg_ref, dlnw_ref, dlnb_ref, drk_ref):
        first = pl.program_id(0) == 0
        ones = jnp.ones((tm, 1), F32)
        prim = (y_ref[...], r_ref[...], k_ref[...], v_ref[...], g_ref[...],
                ones * lnw_ref[...], ones * lnb_ref[...], ones * rk_ref[...])
        _, vjp = jax.vjp(_post_fn, *prim)
        dy, dr, dk, dv, dg, dlnw, dlnb, drk = vjp(d_ref[...])
        dy_ref[...] = dy
        dr_ref[...] = dr
        dk_ref[...] = dk
        dv_ref[...] = dv
        dg_ref[...] = dg
        _acc(dlnw_ref, _colsum8(dlnw), first)
        _acc(dlnb_ref, _colsum8(dlnb), first)
        _acc(drk_ref, _colsum8(drk), first)

    row = pl.BlockSpec((tm, RW), lambda i: (i, 0))
    vec = pl.BlockSpec((1, RW), lambda i: (0, 0))
    part = pl.BlockSpec((8, RW), lambda i: (0, 0))
    return pl.pallas_call(
        body, name="rwkv_post_bwd", grid=(t // tm,),
        in_specs=[row] * 6 + [vec] * 3, out_specs=[row] * 5 + [part] * 3,
        out_shape=[jax.ShapeDtypeStruct((t, RW), F32)] * 5 + [jax.ShapeDtypeStruct((8, RW), F32)] * 3,
        compiler_params=_params(("arbitrary",)),
    )(dya, y, r, k2, v, g, lnw, lnb, rk)


def _wkv_bwd(dy, s0s, r, lw, k2, v, kk, a):
    t = r.shape[0]
    nc = t // CHUNK

    def body(dy_ref, s_ref, r_ref, lw_ref, k_ref, v_ref, kk_ref, a_ref,
             dr_ref, dlw_ref, dk_ref, dv_ref, dkk_ref, da_ref, ds):
        @pl.when(pl.program_id(0) == 0)
        def _():
            ds[...] = jnp.zeros_like(ds)

        _, vjp = jax.vjp(_wkv_chunk_fn, s_ref[0],
                         *[_pairs(ref) for ref in (r_ref, lw_ref, k_ref, v_ref, kk_ref, a_ref)])
        res = vjp((_pairs(dy_ref), ds[...]))
        ds[...] = res[0]
        for ref, val in zip((dr_ref, dlw_ref, dk_ref, dv_ref, dkk_ref, da_ref), res[1:]):
            for p in range(N_PAIR):
                ref[:, 128 * p:128 * (p + 1)] = val[p]

    blk = pl.BlockSpec((CHUNK, RW), lambda c: (nc - 1 - c, 0))
    return pl.pallas_call(
        body, name="wkv_bwd", grid=(nc,),
        in_specs=[blk, pl.BlockSpec((1, N_PAIR, 128, 128), lambda c: (nc - 1 - c, 0, 0, 0))] + [blk] * 6,
        out_specs=[blk] * 6,
        out_shape=[jax.ShapeDtypeStruct((t, RW), F32)] * 6,
        scratch_shapes=[pltpu.VMEM((N_PAIR, 128, 128), F32)],
        compiler_params=_params(("arbitrary",)),
    )(dy, s0s, r, lw, k2, v, kk, a)


def _prep_in_proj_bwd(proj, pw, douts, dq, dk, dv, win, x, g1, dx1):
    t = proj.shape[0]
    tm = _PREP_TM
    nt = t // tm

    def body(p_ref, l8_ref, mu, w0, w2p, a0, a2p, g2, k_k, k_a, dr, dr2, dlw, dk2, dk22, dv, dv2, dkk, da, dg,
             dq_ref, dkq_ref, dvq_ref, w_ref, x_ref, g1_ref, dx1_ref,
             dproj_ref, dx_ref, dg1_ref, dmu_ref, dw0_ref, dw2_ref, da0_ref, da2_ref, dg2_ref, dkk_ref, dka_ref, carry):
        i = pl.program_id(0)
        first = i == 0

        @pl.when(first)
        def _():
            carry[...] = jnp.zeros_like(carry)

        p = p_ref[...]
        pprev = _shifted(p, l8_ref[...], i == nt - 1)
        ones = jnp.ones((tm, 1), F32)
        prim = (p, pprev, ones * mu[...], ones * w0[...], w2p[...], ones * a0[...], a2p[...], g2[...],
                ones * k_k[...], ones * k_a[...])
        _, vjp = jax.vjp(_prep_fn, *prim)
        dp, dpp, dmu, dw0, dw2, da0, da2, dg2, dkk_, dka = vjp(
            (dr[...] + dr2[...], dlw[...], dk2[...] + dk22[...], dv[...] + dv2[...], dkk[...], da[...], dg[...]))
        up = pltpu.roll(dpp, tm - 1, axis=0)
        rid = lax.broadcasted_iota(jnp.int32, dpp.shape, 0)
        dpa = dp + jnp.where(rid == tm - 1, carry[0:1, :], up)
        carry[...] = jnp.broadcast_to(dpp[0:1, :], carry.shape)
        _acc(dmu_ref, _colsum8(dmu), first)
        _acc(dw0_ref, _colsum8(dw0), first)
        _acc(dw2_ref, dw2, first)
        _acc(da0_ref, _colsum8(da0), first)
        _acc(da2_ref, da2, first)
        _acc(dg2_ref, dg2, first)
        _acc(dkk_ref, _colsum8(dkk_), first)
        _acc(dka_ref, _colsum8(dka), first)
        parts = [dpa] + [ref[pr] for ref in (dq_ref, dkq_ref, dvq_ref) for pr in range(N_PAIR)]
        dproj = jnp.concatenate([z.astype(BF16) for z in parts], axis=1)
        dproj_ref[...] = dproj
        dxn, dgr = _rms_bwd(_dot(dproj, w_ref[...]), x_ref[...], g1_ref[...])
        dx_ref[...] = dx1_ref[...] + dxn
        _acc(dg1_ref, _colsum8(dgr), first)

    rev = lambda i: (nt - 1 - i, 0)
    row = pl.BlockSpec((tm, RW), rev)
    wide = pl.BlockSpec((tm, D_MODEL), rev)
    pair = pl.BlockSpec((N_PAIR, tm, 128), lambda i: (0, nt - 1 - i, 0))
    part = lambda n: pl.BlockSpec((8, n), lambda i: (0, 0))
    mat = pl.BlockSpec((128, RW), lambda i: (0, 0))
    return pl.pallas_call(
        body, name="prep_in_proj_bwd", grid=(nt,),
        in_specs=[pl.BlockSpec((tm, SHIFT_COLS), rev),
                  pl.BlockSpec((8, SHIFT_COLS), lambda i: (jnp.maximum((nt - 1 - i) * (tm // 8) - 1, 0), 0))]
                 + _prep_specs(tm) + [row] * 10
                 + [pair] * 3 + [pl.BlockSpec((IN_COLS, D_MODEL), lambda i: (0, 0)), wide,
                                 pl.BlockSpec((1, D_MODEL), lambda i: (0, 0)), wide],
        out_specs=[pl.BlockSpec((tm, IN_COLS), rev), wide, part(D_MODEL), part(SHIFT_COLS), part(RW), mat, part(RW), mat,
                   mat, part(RW), part(RW)],
        out_shape=[jax.ShapeDtypeStruct((t, IN_COLS), BF16), jax.ShapeDtypeStruct((t, D_MODEL), F32),
                   jax.ShapeDtypeStruct((8, D_MODEL), F32), jax.ShapeDtypeStruct((8, SHIFT_COLS), F32),
                   jax.ShapeDtypeStruct((8, RW), F32), jax.ShapeDtypeStruct((128, RW), F32),
                   jax.ShapeDtypeStruct((8, RW), F32), jax.ShapeDtypeStruct((128, RW), F32),
                   jax.ShapeDtypeStruct((128, RW), F32), jax.ShapeDtypeStruct((8, RW), F32),
                   jax.ShapeDtypeStruct((8, RW), F32)],
        scratch_shapes=[pltpu.VMEM((8, SHIFT_COLS), F32)],
        compiler_params=_params(("arbitrary",)),
    )(proj, proj, *pw, *douts, dq, dk, dv, win, x, g1, dx1)


def _combine_bwd(dyb, o, l, og):
    t = dyb.shape[0]
    tm = _COMB_TM

    def body(d_ref, o_ref, l_ref, og_ref, do_ref, dl_ref, dog_ref):
        ones = jnp.ones((tm, 1), F32)
        dog = []
        for p in range(N_PAIR):
            cols = slice(128 * p, 128 * (p + 1))
            _, vjp = jax.vjp(_combine_fn, o_ref[0, p], o_ref[1, p], o_ref[2, p], l_ref[0, p], l_ref[1, p], l_ref[2, p],
                             ones * og_ref[:, cols])
            res = vjp(d_ref[:, cols])
            for b in range(3):
                do_ref[b, p] = res[b]
                dl_ref[b, p] = res[3 + b]
            dog.append(_colsum8(res[6]))
        _acc(dog_ref, jnp.concatenate(dog, axis=1), pl.program_id(0) == 0)

    blk = pl.BlockSpec((3, N_PAIR, tm, 128), lambda i: (0, 0, i, 0))
    return pl.pallas_call(
        body, name="attn_combine_bwd", grid=(t // tm,),
        in_specs=[pl.BlockSpec((tm, RW), lambda i: (i, 0)), blk, blk, pl.BlockSpec((1, RW), lambda i: (0, 0))],
        out_specs=[blk, blk, pl.BlockSpec((8, RW), lambda i: (0, 0))],
        out_shape=[jax.ShapeDtypeStruct((3, N_PAIR, t, 128), F32)] * 2 + [jax.ShapeDtypeStruct((8, RW), F32)],
        compiler_params=_params(("arbitrary",)),
    )(dyb, o, l, og)


def _attn_bwd(do, dl, o, lse, qkv):
    t = qkv.shape[2]

    def body(do_ref, dl_ref, o_ref, l_ref, q_ref, k_ref, v_ref, dq_ref, dk_ref, dv_ref):
        @pl.when(pl.program_id(1) == 0)
        def _():
            for ref in (dq_ref, dk_ref, dv_ref):
                ref[...] = jnp.zeros_like(ref)

        def unit(di, places, has_prev):
            cur = [_dilated_rows(d, r, n) for d, r, n in places]
            q, kc, vc = [_take(ref, (0,), cur) for ref in (q_ref, k_ref, v_ref)]
            kp = vp = None
            if has_prev:
                prv = [_dilated_rows(d, r, n - 1) for d, r, n in places]
                kp, vp = [_take(ref, (0,), prv) for ref in (k_ref, v_ref)]
            res = _attn_block_bwd(q, kc, vc, kp, vp, *[_take(ref, (0,), cur) for ref in (o_ref, l_ref, do_ref, dl_ref)])
            _put(dq_ref, (), cur, res[0], add=True)
            _put(dk_ref, (), cur, res[1], add=True)
            _put(dv_ref, (), cur, res[2], add=True)
            if has_prev:
                _put(dk_ref, (), prv, res[3], add=True)
                _put(dv_ref, (), prv, res[4], add=True)

        _for_each_sequence(t, unit)

    spec = lambda j: pl.BlockSpec((1, ATTN_GROUP, t, 128), lambda i, b: (j, i, 0, 0))
    branch = pl.BlockSpec((1, ATTN_GROUP, t, 128), lambda i, b: (b, i, 0, 0))
    out = pl.BlockSpec((ATTN_GROUP, t, 128), lambda i, b: (i, 0, 0))
    return pl.pallas_call(
        body, name="attn_bwd", grid=(N_PAIR // ATTN_GROUP, len(DILATIONS)),
        in_specs=[branch] * 4 + [spec(0), spec(1), spec(2)], out_specs=[out] * 3,
        out_shape=[jax.ShapeDtypeStruct((N_PAIR, t, 128), F32)] * 3,
        compiler_params=_params(("parallel", "arbitrary")),
    )(do, dl, o, lse, qkv, qkv, qkv)


def _pad_lora(w, lo):
    z = jnp.zeros((64, RW), F32)
    return jnp.concatenate([w, z], axis=0) if lo == 0 else jnp.concatenate([z, w], axis=0)


def _local_step(x, tgt, win, vecs, w2, a2, g2m, get_rest, send_rest):
    pw = (vecs["mu_shift"], vecs["decay_w0"], _pad_lora(w2, 0), vecs["iclr_a0"], _pad_lora(a2, 64), g2m,
          vecs["k_k"], vecs["k_a"])
    h, proj, qkv, r, lw, k2, v, kk, a, g = _in_proj_prep(x, vecs["mix_norm_g"], win, pw)
    y, s0s = _wkv_fwd(r, lw, k2, v, kk, a)
    o_att, l_att = _attn_fwd(qkv)
    ycat = _mixers_out(y, r, k2, v, g, vecs["ln_x_w"], vecs["ln_x_b"], vecs["r_k"], o_att, l_att, vecs["attn_out_g"])
    wout, wg, wu, wd = get_rest(ycat)
    h2, act, dx2b, dgt, dup, dx1b, dx1, dya, dyb, loss8, dgf, dg2n = _ffn_all(
        x, ycat, wg, wu, wd, wout, vecs["ffn_norm_g"], vecs["final_norm_g"], tgt)
    gw = {
        "w_down": _wgrad(act, dx2b, 1408, 1024, "wgrad_down"),
        "w_gate": _wgrad(dgt, h2, 1408, 1024, "wgrad_gate"),
        "w_up": _wgrad(dup, h2, 1408, 1024, "wgrad_up"),
        "w_out": _wgrad(ycat, dx1b, 1024, 1024, "wgrad_out"),
    }

    lnw = vecs["ln_x_w"] + send_rest(gw)[0, 0]
    dy, dr_p, dk2_p, dv_p, dg, dlnw, dlnb, drk = _post_bwd(dya, y, r, k2, v, g, lnw, vecs["ln_x_b"], vecs["r_k"])
    dr_s, dlw, dk2_s, dv_s, dkk, da = _wkv_bwd(dy, s0s, r, lw, k2, v, kk, a)
    do_att, dl_att, dog = _combine_bwd(dyb, o_att, l_att, vecs["attn_out_g"])
    dq, dk, dv = _attn_bwd(do_att, dl_att, o_att, l_att, qkv)
    dproj, dx, dg1, dmu, dw0, dw2p, da0, da2p, dg2m, dk_k, dk_a = _prep_in_proj_bwd(
        proj, pw, (dr_p, dr_s, dlw, dk2_p, dk2_s, dv_p, dv_s, dkk, da, dg), dq, dk, dv, win, x, vecs["mix_norm_g"], dx1)
    gw["w_in"] = _wgrad(dproj, h, 1664, 1024, "wgrad_in")
    gw["decay_w2"] = dw2p[:64]
    gw["iclr_a2"] = da2p[64:]
    gw["gate_g2"] = dg2m
    gv = {"mix_norm_g": dg1, "mu_shift": dmu, "decay_w0": dw0, "iclr_a0": da0, "k_k": dk_k, "k_a": dk_a, "r_k": drk,
          "ln_x_w": dlnw, "ln_x_b": dlnb, "attn_out_g": dog, "ffn_norm_g": dg2n, "final_norm_g": dgf}
    return loss8, dx, gw, gv


N_CHIP = 4
N_DEV = 8
MATS = ("w_in", "w_out", "w_gate", "w_up", "w_down")
LORAS = ("decay_w2", "iclr_a2", "gate_g2")
VECS = (("mix_norm_g", 1024), ("mu_shift", 1792), ("decay_w0", 512), ("iclr_a0", 512), ("k_k", 512), ("k_a", 512),
        ("r_k", 512), ("ln_x_w", 512), ("ln_x_b", 512), ("attn_out_g", 512), ("ffn_norm_g", 1024),
        ("final_norm_g", 1024))
N_VEC = sum(n for _, n in VECS)
N_SMALL = N_VEC + 128
ANY = pl.BlockSpec(memory_space=pl.ANY)


def _flip(v, f):
    return 1 - v if f else v


class _Me:
    def __init__(self, mode):
        x, y, c = lax.axis_index("x"), lax.axis_index("y"), lax.axis_index("c")
        self.core, self.chip, self.dev = c, 2 * x + y, 4 * x + 2 * y + c
        self.sibling = (x, y, 1 - c)
        if mode == "chips":
            self.peers = [(px, py, c) for px, py in ((1 - x, y), (x, 1 - y), (1 - x, 1 - y))]
        else:
            self.peers = [(_flip(x, k & 4), _flip(y, k & 2), _flip(c, k & 1)) for k in range(1, N_DEV)]


def _half(core, rows):
    h = rows // 2
    return pl.ds(pl.multiple_of(core * h, h), h)


_BY_CHIP = ("gather", "whole", "chipsum")


def _peer_copy(srcs, dsts, kinds, send_sems, recv_sems, me, j, i, incoming):
    px, py, pc = me.peers[j]
    pchip, pdev = 2 * px + py, 4 * px + 2 * py + pc
    src, dst, kind = srcs[i], dsts[i], kinds[i]
    if kind in ("gather", "whole"):
        rows = _half(me.core, src.shape[1]) if kind == "gather" else pl.ds(0, src.shape[1])
        src, dst = src.at[me.chip, rows], dst.at[pchip if incoming else me.chip, rows]
    elif kind == "scatter":
        src, dst = src.at[pchip, _half(pc, src.shape[1])], dst.at[pdev if incoming else me.dev]
    elif kind == "chipsum":
        src, dst = src.at[pchip], dst.at[pchip if incoming else me.chip]
    else:
        dst = dst.at[pdev if incoming else me.dev]
    n = len(srcs)
    return pltpu.make_async_remote_copy(src_ref=src, dst_ref=dst, send_sem=send_sems.at[n * j + i],
                                        recv_sem=recv_sems.at[n * j + i], device_id=(px, py, pc), device_id_type=MESH)


def _mode(kinds):
    return "chips" if kinds[0] in _BY_CHIP else "devs"


def _npeer(kinds):
    return N_CHIP - 1 if kinds[0] in _BY_CHIP else N_DEV - 1


def _sibling_halves(gs, name):
    n = len(gs)

    def body(*refs):
        srcs, dsts, send_sems, recv_sems = refs[:n], refs[n:2 * n], refs[2 * n], refs[2 * n + 1]
        me = _Me("chips")

        def copy(i, p):
            return pltpu.make_async_remote_copy(
                src_ref=srcs[i].at[p, _half(1 - me.core, srcs[i].shape[1])], dst_ref=dsts[i].at[p],
                send_sem=send_sems.at[N_CHIP * i + p], recv_sem=recv_sems.at[N_CHIP * i + p],
                device_id=me.sibling, device_id_type=MESH)

        copies = [copy(i, p) for i in range(n) for p in range(N_CHIP)]
        for cp in copies:
            cp.start()
        for cp in copies:
            cp.wait()

    return pl.pallas_call(
        body, name=name, in_specs=[ANY] * n, out_specs=[ANY] * n,
        out_shape=[jax.ShapeDtypeStruct((N_CHIP, g.shape[1] // 2, g.shape[2]), g.dtype) for g in gs],
        scratch_shapes=[pltpu.SemaphoreType.DMA((N_CHIP * n,)), pltpu.SemaphoreType.DMA((N_CHIP * n,))],
    )(*gs)


def _add_halves(g, other, core, tr, name):
    _, h, cols = other.shape

    def body(core_ref, g_ref, o_ref, out_ref):
        out_ref[...] = (g_ref[...].astype(F32) + o_ref[...].astype(F32)).astype(BF16)

    blk = lambda off: pl.BlockSpec((1, tr, cols), lambda p, i, core_ref: (p, core_ref[0] * (h // tr) * off + i, 0))
    return pl.pallas_call(
        body, name=name,
        grid_spec=pltpu.PrefetchScalarGridSpec(num_scalar_prefetch=1, grid=(N_CHIP, h // tr),
                                               in_specs=[blk(1), blk(0)], out_specs=blk(0)),
        out_shape=jax.ShapeDtypeStruct(other.shape, BF16),
        compiler_params=_params(("parallel", "parallel")),
    )(core, g, other)


def _swap_gathered(lands, name):
    n = len(lands)

    def body(*refs):
        dsts, send_sems, recv_sems = refs[n:2 * n], refs[2 * n], refs[2 * n + 1]
        me = _Me("chips")

        def copy(j, i, incoming):
            px, py, _ = me.peers[j]
            rows_out, rows_in = _half(me.core, dsts[i].shape[1]), _half(1 - me.core, dsts[i].shape[1])
            return pltpu.make_async_remote_copy(
                src_ref=dsts[i].at[2 * px + py, rows_out], dst_ref=dsts[i].at[2 * px + py, rows_in if incoming else rows_out],
                send_sem=send_sems.at[n * j + i], recv_sem=recv_sems.at[n * j + i], device_id=me.sibling, device_id_type=MESH)

        sends = [copy(j, i, False) for j in range(3) for i in range(n)]
        for cp in sends:
            cp.start()
        for j in range(3):
            for i in range(n):
                copy(j, i, True).wait_recv()
        for cp in sends:
            cp.wait_send()

    return pl.pallas_call(
        body, name=name, in_specs=[ANY] * n, out_specs=[ANY] * n,
        out_shape=[jax.ShapeDtypeStruct(l.shape, l.dtype) for l in lands],
        input_output_aliases={i: i for i in range(n)},
        scratch_shapes=[pltpu.SemaphoreType.DMA((3 * n,)), pltpu.SemaphoreType.DMA((3 * n,))],
    )(*lands)


def _join_halves(sums, name):
    n = len(sums)

    def body(*refs):
        dsts, send_sems, recv_sems = refs[n:2 * n], refs[2 * n], refs[2 * n + 1]
        me = _Me("chips")

        def copy(i, incoming):
            mine, other = _half(me.core, dsts[i].shape[0]), _half(1 - me.core, dsts[i].shape[0])
            return pltpu.make_async_remote_copy(src_ref=dsts[i].at[mine], dst_ref=dsts[i].at[other if incoming else mine],
                                                send_sem=send_sems.at[i], recv_sem=recv_sems.at[i],
                                                device_id=me.sibling, device_id_type=MESH)

        sends = [copy(i, False) for i in range(n)]
        for cp in sends:
            cp.start()
        for i in range(n):
            copy(i, True).wait_recv()
        for cp in sends:
            cp.wait_send()

    return pl.pallas_call(
        body, name=name, in_specs=[ANY] * n, out_specs=[ANY] * n,
        out_shape=[jax.ShapeDtypeStruct(s.shape, s.dtype) for s in sums],
        input_output_aliases={i: i for i in range(n)},
        scratch_shapes=[pltpu.SemaphoreType.DMA((n,)), pltpu.SemaphoreType.DMA((n,))],
    )(*sums)


HBM = pl.BlockSpec(memory_space=pltpu.HBM)
SEM = pl.BlockSpec(memory_space=pltpu.SEMAPHORE)
EFFECT = pltpu.SideEffectType.DATAFLOW_SIDE_EFFECTING


def _swap_start(arrs, lands, kinds, name):
    n = len(lands)
    ops = list(lands) if arrs is None else [*arrs, *lands]
    k = len(ops)

    def body(*refs):
        srcs, dsts, send_sems, recv_sems, token = refs[:n], refs[k - n:k], refs[k], refs[k + 1], refs[-1]
        me = _Me(_mode(kinds))
        for j in range(len(me.peers)):
            for i in range(n):
                _peer_copy(srcs, dsts, kinds, send_sems, recv_sems, me, j, i, False).start()
        token[...] = jnp.zeros_like(token)

    ns = _npeer(kinds) * n
    outs = pl.pallas_call(
        body, name=name,
        out_shape=(pltpu.SemaphoreType.DMA((ns,)), pltpu.SemaphoreType.DMA((ns,)),
                   *[pltpu.HBM(a.shape, a.dtype) for a in ops], jax.ShapeDtypeStruct((8, 128), F32)),
        in_specs=[HBM] * k, out_specs=(SEM, SEM, *[HBM] * k, pl.BlockSpec(memory_space=pltpu.VMEM)),
        input_output_aliases={i: 2 + i for i in range(k)},
        compiler_params=pltpu.CompilerParams(has_side_effects=EFFECT),
    )(*[pltpu.with_memory_space_constraint(a, pltpu.HBM) for a in ops])
    return outs[0], outs[1], outs[2:2 + k - n], outs[2 + k - n:2 + k], outs[-1]


def _swap_wait(send_sems, recv_sems, srcs_thru, lands_thru, after, kinds, name):
    n = len(lands_thru)
    ops = [*srcs_thru, *lands_thru]
    k = len(ops)

    def body(*refs):
        srcs, dsts, s_sems, r_sems = refs[:n], refs[k - n:k], refs[k], refs[k + 1]
        me = _Me(_mode(kinds))
        for j in range(len(me.peers)):
            for i in range(n):
                cp = _peer_copy(srcs, dsts, kinds, s_sems, r_sems, me, j, i, True)
                cp.wait_send()
                cp.wait_recv()

    outs = pl.pallas_call(
        body, name=name,
        out_shape=tuple(pltpu.HBM(a.shape, a.dtype) for a in ops),
        in_specs=[HBM] * k + [SEM, SEM, ANY], out_specs=tuple([HBM] * k),
        input_output_aliases={i: i for i in range(k)},
        compiler_params=pltpu.CompilerParams(has_side_effects=EFFECT),
    )(*ops, send_sems, recv_sems, after)
    return outs[k - n:]


def _adamw(w, g, m, v):
    m = ADAM_B1 * m + (1.0 - ADAM_B1) * g
    v = ADAM_B2 * v + (1.0 - ADAM_B2) * (g * g)
    m_hat = m / (1.0 - ADAM_B1 ** ADAM_STEP)
    v_hat = v / (1.0 - ADAM_B2 ** ADAM_STEP)
    delta = -ADAM_LR * (m_hat / (jnp.sqrt(v_hat) + ADAM_EPS) + ADAM_WD * w)
    return delta, m, v


def _reduce8(rbuf, core, tr, name):
    slots, h, cols = rbuf.shape

    def body(core_ref, r_ref, g_ref):
        g = r_ref[0].astype(F32)
        for s in range(1, slots):
            g = g + r_ref[s].astype(F32)
        g_ref[...] = g

    return pl.pallas_call(
        body, name=name,
        grid_spec=pltpu.PrefetchScalarGridSpec(
            num_scalar_prefetch=1, grid=(h // tr,),
            in_specs=[pl.BlockSpec((slots, tr, cols), lambda i, core_ref: (0, i, 0))],
            out_specs=pl.BlockSpec((tr, cols), lambda i, core_ref: (core_ref[0] * (h // tr) + i, 0))),
        out_shape=jax.ShapeDtypeStruct((2 * h, cols), F32),
        compiler_params=_params(("parallel",)),
    )(core, rbuf)


def _adamw_call(g, w, m, v, tr, name):
    _, rows, cols = w.shape

    def body(g_in, w_ref, m_ref, v_ref, g_ref, d_ref, nm_ref, nv_ref):
        g = g_in[...]
        g_ref[0] = g
        d_ref[0], nm_ref[0], nv_ref[0] = _adamw(w_ref[0], g, m_ref[0], v_ref[0])

    row = pl.BlockSpec((1, tr, cols), lambda i: (0, i, 0))
    return pl.pallas_call(
        body, name=name, grid=(rows // tr,),
        in_specs=[pl.BlockSpec((tr, cols), lambda i: (i, 0)), row, row, row], out_specs=[row] * 4,
        out_shape=[jax.ShapeDtypeStruct(w.shape, F32)] * 4,
        compiler_params=_params(("parallel",)),
    )(g, w, m, v)


def _rowsum_small(parts, loss8):
    def body(*refs):
        out = refs[-1]
        c0 = 0
        for ref in refs[:-1]:
            n = ref.shape[1]
            out[:, c0:c0 + n] = jnp.sum(ref[...], axis=0, keepdims=True)
            c0 += n

    return pl.pallas_call(body, name="rowsum_small", out_shape=jax.ShapeDtypeStruct((1, N_SMALL), F32))(*parts, loss8)


def _reduce_adamw_small(sbuf, ws, ms, vs):
    nv = len(ws)

    def body(*refs):
        s_ref, ins, outs = refs[0], refs[1:1 + 3 * nv], refs[1 + 3 * nv:]
        tot = s_ref[0]
        for s in range(1, N_DEV):
            tot = tot + s_ref[s]
        c0 = 0
        for i in range(nv):
            n = ins[i].shape[1]
            g = tot[:, c0:c0 + n]
            outs[i][...] = g
            outs[nv + i][...], outs[2 * nv + i][...], outs[3 * nv + i][...] = _adamw(
                ins[i][...], g, ins[nv + i][...], ins[2 * nv + i][...])
            c0 += n
        outs[-1][...] = tot[:, c0:]

    return pl.pallas_call(
        body, name="reduce_adamw_small",
        out_shape=[jax.ShapeDtypeStruct(a.shape, F32) for a in ws] * 4 + [jax.ShapeDtypeStruct((1, 128), F32)],
    )(sbuf, *ws, *ms, *vs)


_TRANSPOSED = ("w_in", "w_gate", "w_up")
_ROW_STACKED = MATS
_ADAM_TILE = {"w_in": 208, "w_out": 256, "w_gate": 176, "w_up": 176, "w_down": 176, "lora": 256}
_SUM_TILE = {"w_in": 208, "w_out": 128, "w_gate": 176, "w_up": 176, "w_down": 176, "lora": 128}


def _full(n, stacked):
    p, r, c = stacked.shape
    if n in _ROW_STACKED:
        return stacked.reshape(p * r, c)
    return jnp.transpose(stacked, (1, 0, 2)).reshape(r, p * c)


def _by_chip(n, full):
    if n in _ROW_STACKED:
        return full.reshape(N_CHIP, full.shape[0] // N_CHIP, full.shape[1])
    r, c = full.shape
    return jnp.transpose(full.reshape(r, N_CHIP, c // N_CHIP), (1, 0, 2))


def _with_own(land_shape, dtype, own, slot):
    return lax.dynamic_update_slice(lax.empty(land_shape, dtype), own[None], (slot,) + (0,) * own.ndim)


def _cast_into_slot(a, chip, tr, name, after=None):
    rows, cols = a.shape

    def body(chip_ref, a_ref, *rest):
        rest[-1][0] = a_ref[...].astype(BF16)

    extra = [] if after is None else [after]
    return pl.pallas_call(
        body, name=name,
        grid_spec=pltpu.PrefetchScalarGridSpec(
            num_scalar_prefetch=1, grid=(rows // tr,),
            in_specs=[pl.BlockSpec((tr, cols), lambda i, chip_ref: (i, 0))] + [ANY] * len(extra),
            out_specs=pl.BlockSpec((1, tr, cols), lambda i, chip_ref: (chip_ref[0], i, 0))),
        out_shape=jax.ShapeDtypeStruct((N_CHIP, rows, cols), BF16),
        compiler_params=_params(("parallel",)),
    )(chip, a, *extra)


def kernel(x, mix_norm_g, w_in, mu_shift, decay_w0, decay_w2, iclr_a0, iclr_a2, gate_g2, k_k, k_a, r_k, ln_x_w, ln_x_b, attn_out_g, w_out, ffn_norm_g, w_gate, w_up, w_down, final_norm_g, loss_target, m_mix_norm_g, m_w_in, m_mu_shift, m_decay_w0, m_decay_w2, m_iclr_a0, m_iclr_a2, m_gate_g2, m_k_k, m_k_a, m_r_k, m_ln_x_w, m_ln_x_b, m_attn_out_g, m_w_out, m_ffn_norm_g, m_w_gate, m_w_up, m_w_down, m_final_norm_g, v_mix_norm_g, v_w_in, v_mu_shift, v_decay_w0, v_decay_w2, v_iclr_a0, v_iclr_a2, v_gate_g2, v_k_k, v_k_a, v_r_k, v_ln_x_w, v_ln_x_b, v_attn_out_g, v_w_out, v_ffn_norm_g, v_w_gate, v_w_up, v_w_down, v_final_norm_g):
    names = ("mix_norm_g", "w_in", "mu_shift", "decay_w0", "decay_w2", "iclr_a0", "iclr_a2", "gate_g2", "k_k", "k_a",
             "r_k", "ln_x_w", "ln_x_b", "attn_out_g", "w_out", "ffn_norm_g", "w_gate", "w_up", "w_down", "final_norm_g")
    w = dict(zip(names, (mix_norm_g, w_in, mu_shift, decay_w0, decay_w2, iclr_a0, iclr_a2, gate_g2, k_k, k_a, r_k,
                         ln_x_w, ln_x_b, attn_out_g, w_out, ffn_norm_g, w_gate, w_up, w_down, final_norm_g)))
    m = dict(zip(names, (m_mix_norm_g, m_w_in, m_mu_shift, m_decay_w0, m_decay_w2, m_iclr_a0, m_iclr_a2, m_gate_g2,
                         m_k_k, m_k_a, m_r_k, m_ln_x_w, m_ln_x_b, m_attn_out_g, m_w_out, m_ffn_norm_g, m_w_gate,
                         m_w_up, m_w_down, m_final_norm_g)))
    v = dict(zip(names, (v_mix_norm_g, v_w_in, v_mu_shift, v_decay_w0, v_decay_w2, v_iclr_a0, v_iclr_a2, v_gate_g2,
                         v_k_k, v_k_a, v_r_k, v_ln_x_w, v_ln_x_b, v_attn_out_g, v_w_out, v_ffn_norm_g, v_w_gate,
                         v_w_up, v_w_down, v_final_norm_g)))
    first = ("w_in", "lora")
    rest = ("w_out", "w_gate", "w_up", "w_down")
    xi, yi, ci = lax.axis_index("x"), lax.axis_index("y"), lax.axis_index("c")
    my_chip, my_dev = 2 * xi + yi, 4 * xi + 2 * yi + ci
    gather, scatter = ("gather",) * 4, ("scatter",) * 4

    def stored(d):
        out = {n: jnp.transpose(d[n][0]) if n in _TRANSPOSED else d[n][0] for n in MATS}
        out["lora"] = jnp.concatenate([d[n][0] for n in LORAS], axis=0)
        return out

    ws, ms, vs = stored(w), stored(m), stored(v)
    lora_rows = [(0, 64), (64, 128), (128, 256)]
    chip = jnp.reshape(my_chip, (1,)).astype(jnp.int32)
    early = _swap_start(None, [_cast_into_slot(ws["w_in"], chip, _ADAM_TILE["w_in"], "cast_w_in"),
                               _with_own((N_CHIP,) + ws["lora"].shape, F32, ws["lora"], my_chip)], gather[:2],
                        "gather_first_start")
    lands = [_cast_into_slot(ws[n], chip, _ADAM_TILE[n], "cast_" + n, after=early[4]) for n in rest]
    gather_rest = ("gather", "gather", "whole", "whole")
    ssem, rsem, srcs_thru, lands_thru, tok = _swap_start(None, lands, gather_rest, "gather_rest_start")
    got = _swap_wait(early[0], early[1], early[2], early[3], tok, gather[:2], "gather_first_wait")
    win_all, lora_all = _swap_gathered(got, "gather_first_halves")
    win = _full("w_in", win_all)
    w2, a2, g2m = (_full(n, lora_all[:, a:b]) for n, (a, b) in zip(LORAS, lora_rows))

    vecs = {n: w[n].reshape(1, sz) for n, sz in VECS}
    vecs["mix_norm_g"] = vecs["mix_norm_g"] + tok[0, 0]

    def get_rest(after):
        got_rest = _swap_wait(ssem, rsem, srcs_thru, lands_thru, after, gather_rest, "gather_rest_wait")
        swapped = _swap_gathered(got_rest[:2], "gather_rest_halves")
        return [_full(n, z) for n, z in zip(rest, [*swapped, *got_rest[2:]])]

    flight = []

    def my_half(g):
        h = g.shape[1] // 2
        return lax.dynamic_slice(g, (my_chip, ci * h, 0), (1, h, g.shape[2]))[0]

    def send_rest(gw):
        gs = [_by_chip(n, gw[n]) for n in rest]
        into = [_with_own((N_DEV,) + my_half(g).shape, BF16, my_half(g), my_dev) for g in gs]
        flight.extend(_swap_start(gs, into, scatter, "exchange_rest_start"))
        return flight[4]

    loss8, dx, gw, gv = _local_step(x[0], loss_target[0], win, vecs, w2, a2, g2m, get_rest, send_rest)

    core = jnp.reshape(ci, (1,)).astype(jnp.int32)
    gs = [_by_chip("w_in", gw["w_in"]),
          jnp.concatenate([_by_chip(n, gw[n]) for n in LORAS], axis=1).astype(BF16)]
    theirs = _sibling_halves(gs, "presum_halves")
    sums = [_add_halves(g, o, core, _SUM_TILE[n], "chipsum_" + n) for n, g, o in zip(first, gs, theirs)]
    own = [lax.dynamic_index_in_dim(s, my_chip, 0, keepdims=False) for s in sums]
    last = _swap_start(sums, [_with_own(s.shape, BF16, o, my_chip) for s, o in zip(sums, own)], ("chipsum",) * 2,
                       "exchange_first_start")
    small = _rowsum_small([gv[n] for n, _ in VECS], loss8 + last[4])
    vecs_out = _swap_start([small], [_with_own((N_DEV,) + small.shape, F32, small, my_dev)], ("all",),
                           "exchange_vectors_start")


    def update(group, rbufs, tag):
        sums = [_reduce8(rb, core, _SUM_TILE[n], "reduce_" + n) for n, rb in zip(group, rbufs)]
        gsum = _join_halves(sums, "join_halves_" + tag)
        out = {}
        for n, g in zip(group, gsum):
            r = _adamw_call(g, ws[n][None], ms[n][None], vs[n][None], _ADAM_TILE[n], "adamw_" + n)
            if n == "lora":
                for name, (a, b) in zip(LORAS, lora_rows):
                    out[name] = [z[:, a:b] for z in r]
            else:
                out[n] = [jnp.transpose(z[0])[None] for z in r] if n in _TRANSPOSED else r
        return out, r[1]

    res, done = update(rest, _swap_wait(flight[0], flight[1], flight[2], flight[3], vecs_out[4], scatter,
                                        "exchange_rest_wait"), "rest")
    got = _swap_wait(last[0], last[1], last[2], last[3], done, ("chipsum",) * 2, "exchange_first_wait")
    res_first, done = update(first, got, "first")
    res.update(res_first)
    sbuf = _swap_wait(vecs_out[0], vecs_out[1], vecs_out[2], vecs_out[3], done, ("all",), "exchange_vectors_wait")[0]
    rows = lambda d: [d[n].reshape(1, sz) for n, sz in VECS]
    small_res = _reduce_adamw_small(sbuf, rows(w), rows(m), rows(v))

    outs = []
    for k in range(4):
        piece = {n: r[k] for n, r in res.items()}
        for i, (n, _) in enumerate(VECS):
            piece[n] = small_res[k * len(VECS) + i].reshape(w[n].shape)
        outs.extend(piece[n] for n in names)
    return (small_res[-1][0, 0], dx[None], *outs)
```

```python
import jax
import jax.numpy as jnp
from jax import lax
from jax.experimental import pallas as pl
from jax.experimental.pallas import tpu as pltpu

F32 = jnp.float32
BF16 = jnp.bfloat16

D_MODEL = 1024
HEAD_DIM = 64
RW = 512
N_PAIR = RW // 128
SHIFT_COLS = 1792
IN_COLS = 3328
D_FF = 2816
FF_CHUNK = 256
NORM_EPS = 1e-6
GN_EPS = 64e-5
CHUNK = 64
SUB = 16
WKV_PASSES = 1
ATTN_PASSES = 1
ATTN_BLOCK = 128
DILATIONS = (1, 4, 16)
NEG = -1e30
ADAM_LR, ADAM_B1, ADAM_B2, ADAM_EPS, ADAM_WD, ADAM_STEP = 0.001, 0.9, 0.999, 1e-08, 0.01, 10
VMEM_LIMIT = 56 * 1024 * 1024
MESH = pl.DeviceIdType.MESH


def _params(sem=None, **kw):
    return pltpu.CompilerParams(dimension_semantics=sem, vmem_limit_bytes=VMEM_LIMIT, **kw)


def _dot(a, b):
    return lax.dot_general(a, b, (((1,), (0,)), ((), ())), preferred_element_type=F32)


def _dot_nt(a, b):
    return lax.dot_general(a, b, (((1,), (1,)), ((), ())), preferred_element_type=F32)


def _dot_tn(a, b):
    return lax.dot_general(a, b, (((0,), (0,)), ((), ())), preferred_element_type=F32)


_FORMS = {"nn": ((1,), (0,)), "nt": ((1,), (1,)), "tn": ((0,), (0,))}


def _dg(a, b, form):
    if a.ndim == 3 or b.ndim == 3:
        nb = a.shape[0] if a.ndim == 3 else b.shape[0]
        return jnp.stack([_dg(a[i] if a.ndim == 3 else a, b[i] if b.ndim == 3 else b, form) for i in range(nb)], axis=0)
    return lax.dot_general(a, b, (_FORMS[form], ((), ())), preferred_element_type=F32)


def _split2(x):
    hi = x.astype(BF16)
    return hi, (x - hi.astype(F32)).astype(BF16)


def _split3(x):
    hi = x.astype(BF16)
    rest = x - hi.astype(F32)
    mid = rest.astype(BF16)
    return hi, mid, (rest - mid.astype(F32)).astype(BF16)


def _mm_raw(a, b, form, mode):
    if mode == 1:
        return _dg(a.astype(BF16), b.astype(BF16), form)
    if mode == 3:
        ah, al = _split2(a)
        bh, bl = _split2(b)
        return _dg(ah, bh, form) + (_dg(ah, bl, form) + _dg(al, bh, form))
    if mode == "L3":
        ab = a.astype(BF16)
        b1, b2, b3 = _split3(b)
        if form == "nn":
            n = b.shape[-1]
            wide = _dg(ab, jnp.concatenate([b1, b2, b3], axis=-1), form)
            return wide[..., :n] + (wide[..., n:2 * n] + wide[..., 2 * n:])
        return _dg(ab, b1, form) + (_dg(ab, b2, form) + _dg(ab, b3, form))
    assert mode == "R3", mode
    bb = b.astype(BF16)
    a1, a2, a3 = _split3(a)
    if form in ("nn", "nt"):
        m = a.shape[-2]
        tall = _dg(jnp.concatenate([a1, a2, a3], axis=-2), bb, form)
        return tall[..., :m, :] + (tall[..., m:2 * m, :] + tall[..., 2 * m:, :])
    return _dg(a1, bb, form) + (_dg(a2, bb, form) + _dg(a3, bb, form))


def _mm(a, b, form, mode):
    @jax.custom_vjp
    def f(a, b):
        return _mm_raw(a, b, form, mode)

    def fwd(a, b):
        return _mm_raw(a, b, form, mode), (a, b)

    def bwd(res, ct):
        a, b = res
        la = {1: 1, 3: 3, "L3": None, "R3": "R3"}[mode]
        lb = {1: 1, 3: 3, "L3": "L3", "R3": None}[mode]
        if form == "nn":
            da = None if la is None else _mm_raw(ct, b, "nt", la)
            db = None if lb is None else _mm_raw(a, ct, "tn", lb)
        elif form == "nt":
            da = None if la is None else _mm_raw(ct, b, "nn", la)
            db = None if lb is None else _mm_raw(ct, a, "tn", "R3" if lb == "L3" else lb)
        else:
            da = None if la is None else _mm_raw(b, ct, "nt", "L3" if la == "R3" else la)
            db = None if lb is None else _mm_raw(a, ct, "nn", lb)
        return (jnp.zeros_like(a) if da is None else da, jnp.zeros_like(b) if db is None else db)

    f.defvjp(fwd, bwd)
    return f(a, b)


def _seg_ones(n):
    r = lax.broadcasted_iota(jnp.int32, (n, n), 0) // HEAD_DIM
    c = lax.broadcasted_iota(jnp.int32, (n, n), 1) // HEAD_DIM
    return (r == c).astype(F32)


def _segsum(x, seg):
    return _mm(x, seg, "nn", "R3")


def _rms_fwd(x, g):
    rstd = lax.rsqrt(jnp.mean(x * x, axis=-1, keepdims=True) + NORM_EPS)
    return x * rstd * g


def _rms_bwd(dy, x, g):
    rstd = lax.rsqrt(jnp.mean(x * x, axis=-1, keepdims=True) + NORM_EPS)
    xn = x * rstd
    dxn = dy * g
    dx = rstd * (dxn - xn * jnp.mean(dxn * xn, axis=-1, keepdims=True))
    return dx, dy * xn


def _sigmoid(x):
    return 1.0 / (1.0 + jnp.exp(-x))


def _softplus(x):
    return jnp.maximum(x, 0.0) + jnp.log(1.0 + jnp.exp(-jnp.abs(x)))


def _acc(ref, val, first):
    @pl.when(first)
    def _():
        ref[...] = val

    @pl.when(jnp.logical_not(first))
    def _():
        ref[...] += val


def _colsum8(v):
    rows, n = v.shape
    return jnp.sum(v.reshape(rows // 8, 8, n), axis=0)


def _prep_fn(p, pprev, mu, w0, w2p, a0, a2p, g2, k_k, k_a):
    seg = _seg_ones(RW)
    ps = p + (pprev - p) * mu
    r = ps[:, 0:RW]
    k = ps[:, RW:2 * RW]
    v = ps[:, 2 * RW:3 * RW]
    xwa = ps[:, 3 * RW:3 * RW + 128]
    xg = ps[:, 3 * RW + 128:3 * RW + 256]
    wraw = -_softplus(-(w0 + _mm(jnp.tanh(xwa), w2p, "nn", 3))) - 0.5
    lw = -jnp.exp(wraw)
    a = _sigmoid(a0 + _mm(xwa, a2p, "nn", 3))
    g = _mm(_sigmoid(xg), g2, "nn", 3)
    kk = k * k_k
    kk = kk / jnp.maximum(jnp.sqrt(_segsum(kk * kk, seg)), 1e-12)
    k2 = k * (1.0 + (a - 1.0) * k_a)
    return r, lw, k2, v, kk, a, g


def _transposed(z):
    return jnp.stack([z[i].T for i in range(z.shape[0])], axis=0) if z.ndim == 3 else z.T


def _solve_unit_lower(lmat, rhs):
    c = lmat.shape[-1]
    row = lax.broadcasted_iota(jnp.int32, (c, c), 0)
    col = lax.broadcasted_iota(jnp.int32, (c, c), 1)
    eye = (row == col).astype(F32)
    ld = jnp.where(row // SUB == col // SUB, lmat, 0.0)
    lo = lmat - ld
    x = eye + ld
    m = ld
    mm = lambda p, q: _mm(p, q, "nn", WKV_PASSES)
    cat = jnp.concatenate
    m = mm(m, m)
    for _ in range(2):
        mx = mm(m, cat([m, x], axis=-1))
        m, x = mx[..., :c], x + mx[..., c:]
    x = x + mm(m, x)
    gw = mm(x, cat([lo, rhs], axis=-1))
    g, w = gw[..., :c], gw[..., c:]
    gg = mm(g, cat([g, w], axis=-1))
    w = w + gg[..., c:]
    return w + mm(gg[..., :c], w)


def _wkv_chunk_fn(s0, r, lw, k, v, kk, a):
    c = r.shape[-2]
    n = 2 * c
    row = lax.broadcasted_iota(jnp.int32, (n, n), 0)
    col = lax.broadcasted_iota(jnp.int32, (n, n), 1)
    same = (row // c) == (col // c)
    incl = jnp.logical_and(row >= col, same)
    strict = jnp.logical_and(row > col, same)
    sel = (lax.broadcasted_iota(jnp.int32, (n, 128), 0) // c) == (lax.broadcasted_iota(jnp.int32, (n, 128), 1) // HEAD_DIM)
    two = lambda z: jnp.concatenate([z, z], axis=-2)
    lw2 = two(lw)
    mm = lambda p_, q_, form: _mm(p_, q_, form, WKV_PASSES)
    cl = _mm(incl.astype(F32), lw2, "nn", "L3")
    p = jnp.exp(cl)
    pinv = jnp.exp(-cl)
    pprev = jnp.exp(cl - lw2)
    kk2 = two(kk)
    at = jnp.where(sel, -kk2 * pprev, 0.0)
    bt = jnp.where(sel, kk2 * two(a) * pinv, 0.0)
    kt = jnp.where(sel, two(k) * pinv, 0.0)
    rt = jnp.where(sel, two(r) * p, 0.0)
    vt = jnp.where(sel, two(v), 0.0)
    cat = jnp.concatenate
    bk = cat([bt, kt], axis=-2)
    arbk = mm(cat([at, rt], axis=-2), bk, "nt")
    ab, ak = jnp.where(strict, arbk[..., :n, :n], 0.0), jnp.where(strict, arbk[..., :n, n:], 0.0)
    rb, rk = jnp.where(incl, arbk[..., n:, :n], 0.0), jnp.where(incl, arbk[..., n:, n:], 0.0)
    s0t = _transposed(s0)
    u = _solve_unit_lower(ab, mm(cat([at, ak], axis=-1), cat([s0t, vt], axis=-2), "nn"))
    y2 = mm(cat([rt, rb, rk], axis=-1), cat([s0t, u, vt], axis=-2), "nn")
    plast = jnp.exp(jnp.sum(lw, axis=-2, keepdims=True))
    s1 = (s0 + mm(cat([u, vt], axis=-2), bk, "tn")) * plast
    r2 = lax.broadcasted_iota(jnp.int32, (128, 128), 0) // HEAD_DIM
    c2 = lax.broadcasted_iota(jnp.int32, (128, 128), 1) // HEAD_DIM
    return y2[..., :c, :] + y2[..., c:, :], jnp.where(r2 == c2, s1, 0.0)


def _post_fn(y, r, k2, v, g, lnw, lnb, rk):
    seg = _seg_ones(RW)
    mean = _segsum(y, seg) * (1.0 / HEAD_DIM)
    yc = y - mean
    var = _segsum(yc * yc, seg) * (1.0 / HEAD_DIM)
    yn = yc * lax.rsqrt(var + GN_EPS)
    out = yn * lnw + lnb + _segsum(r * k2 * rk, seg) * v
    return out * g


def _attn_block_fn(q, kc, vc, kp=None, vp=None):
    n = ATTN_BLOCK
    qi = lax.broadcasted_iota(jnp.int32, (n, n), 0)
    kj = lax.broadcasted_iota(jnp.int32, (n, n), 1)
    lane = lax.broadcasted_iota(jnp.int32, (1, 128), 1)
    scale = HEAD_DIM ** -0.5
    valid = kj <= qi
    keys, vals = kc, vc
    if kp is not None:
        valid = jnp.concatenate([valid, kj >= qi], axis=-1)
        keys, vals = jnp.concatenate([kc, kp], axis=-2), jnp.concatenate([vc, vp], axis=-2)
    m0 = (lane // HEAD_DIM) == 0
    q2 = jnp.concatenate([jnp.where(m0, q, 0.0), jnp.where(m0, 0.0, q)], axis=-2)
    valid2 = jnp.concatenate([valid, valid], axis=-2)
    s = jnp.where(valid2, _mm(q2, keys, "nt", ATTN_PASSES) * scale, NEG)
    m = jnp.max(s, axis=-1, keepdims=True)
    p = jnp.exp(s - m)
    den = jnp.sum(p, axis=-1, keepdims=True)
    o2 = _mm(p, vals, "nn", ATTN_PASSES) / den
    l2 = m + jnp.log(den)
    return jnp.where(m0, o2[..., :n, :], o2[..., n:, :]), jnp.where(m0, l2[..., :n, :], l2[..., n:, :])


def _attn_block_bwd(q, kc, vc, kp, vp, o, lse, do, dl):
    n = ATTN_BLOCK
    cat = jnp.concatenate
    qi = lax.broadcasted_iota(jnp.int32, (n, n), 0)
    kj = lax.broadcasted_iota(jnp.int32, (n, n), 1)
    m0 = (lax.broadcasted_iota(jnp.int32, (1, 128), 1) // HEAD_DIM) == 0
    scale = HEAD_DIM ** -0.5
    valid = kj <= qi
    keys, vals = kc, vc
    if kp is not None:
        valid = cat([valid, kj >= qi], axis=-1)
        keys, vals = cat([kc, kp], axis=-2), cat([vc, vp], axis=-2)
    stack = lambda z: cat([jnp.where(m0, z, 0.0), jnp.where(m0, 0.0, z)], axis=-2)
    q2, do2 = stack(q), stack(do)
    lse2 = cat([jnp.max(jnp.where(m0, lse, NEG), axis=-1, keepdims=True),
                jnp.max(jnp.where(m0, NEG, lse), axis=-1, keepdims=True)], axis=-2)
    delta = jnp.sum(do2 * cat([o, o], axis=-2), axis=-1, keepdims=True)
    dlse = jnp.sum(stack(dl), axis=-1, keepdims=True)
    mm = lambda a, b, form: _mm_raw(a, b, form, ATTN_PASSES)
    s = jnp.where(cat([valid, valid], axis=-2), mm(q2, keys, "nt") * scale, NEG)
    p = jnp.exp(s - lse2)
    ds = p * (mm(do2, vals, "nt") - delta + dlse)
    dq2 = mm(ds, keys, "nn") * scale
    dq = jnp.where(m0, dq2[..., :n, :], dq2[..., n:, :])
    dkeys = mm(ds, q2, "tn") * scale
    dvals = mm(p, do2, "tn")
    if kp is None:
        return dq, dkeys, dvals
    return dq, dkeys[..., :n, :], dvals[..., :n, :], dkeys[..., n:, :], dvals[..., n:, :]


def _combine_fn(o1, o2, o3, l1, l2, l3, og):
    seg = _seg_ones(o1.shape[-1])
    m = jnp.maximum(jnp.maximum(l1, l2), l3)
    e1, e2, e3 = jnp.exp(l1 - m), jnp.exp(l2 - m), jnp.exp(l3 - m)
    o = (e1 * o1 + e2 * o2 + e3 * o3) / (e1 + e2 + e3)
    o = o * lax.rsqrt(_segsum(o * o, seg) * (1.0 / HEAD_DIM) + NORM_EPS)
    return o * og


def _shifted(p, last8, first):
    prow = jnp.where(first, 0.0, last8[7:8, :])
    rolled = pltpu.roll(p, 1, axis=0)
    rid = lax.broadcasted_iota(jnp.int32, p.shape, 0)
    return jnp.where(rid == 0, prow, rolled)


_PREP_TM = 256


def _prep_specs(tm):
    vec = lambda n: pl.BlockSpec((1, n), lambda i: (0, 0))
    mat = lambda r, n: pl.BlockSpec((r, n), lambda i: (0, 0))
    return [vec(SHIFT_COLS), vec(RW), mat(128, RW), vec(RW), mat(128, RW), mat(128, RW), vec(RW), vec(RW)]


def _in_proj_prep(x, g1, win, pw):
    t = x.shape[0]
    tm = _PREP_TM

    def body(x_ref, g_ref, w_ref, mu, w0, w2p, a0, a2p, g2, k_k, k_a, h_ref, pa_ref, qkv_ref, *rest):
        outs, carry = rest[:7], rest[7]

        @pl.when(pl.program_id(0) == 0)
        def _():
            carry[...] = jnp.zeros_like(carry)

        h = _rms_fwd(x_ref[...], g_ref[...]).astype(BF16)
        h_ref[...] = h
        proj = _dot_nt(h, w_ref[...])
        p = proj[:, :SHIFT_COLS]
        pa_ref[...] = p
        for j in range(3):
            for pr in range(N_PAIR):
                c0 = SHIFT_COLS + j * RW + pr * 128
                qkv_ref[j, pr] = proj[:, c0:c0 + 128]
        pprev = _shifted(p, carry[...], pl.program_id(0) == 0)
        carry[...] = p[tm - 8:, :]
        res = _prep_fn(p, pprev, mu[...], w0[...], w2p[...], a0[...], a2p[...], g2[...], k_k[...], k_a[...])
        for o_ref, val in zip(outs, res):
            o_ref[...] = val

    row = pl.BlockSpec((tm, RW), lambda i: (i, 0))
    return pl.pallas_call(
        body, name="in_proj_prep", grid=(t // tm,),
        in_specs=[pl.BlockSpec((tm, D_MODEL), lambda i: (i, 0)), pl.BlockSpec((1, D_MODEL), lambda i: (0, 0)),
                  pl.BlockSpec((IN_COLS, D_MODEL), lambda i: (0, 0))] + _prep_specs(tm),
        out_specs=[pl.BlockSpec((tm, D_MODEL), lambda i: (i, 0)), pl.BlockSpec((tm, SHIFT_COLS), lambda i: (i, 0)),
                   pl.BlockSpec((3, N_PAIR, tm, 128), lambda i: (0, 0, i, 0))] + [row] * 7,
        out_shape=[jax.ShapeDtypeStruct((t, D_MODEL), BF16), jax.ShapeDtypeStruct((t, SHIFT_COLS), F32),
                   jax.ShapeDtypeStruct((3, N_PAIR, t, 128), F32)] + [jax.ShapeDtypeStruct((t, RW), F32)] * 7,
        scratch_shapes=[pltpu.VMEM((8, SHIFT_COLS), F32)],
        compiler_params=_params(("arbitrary",)),
    )(x, g1, win, *pw)


def _pairs(ref):
    return jnp.stack([ref[:, 128 * p:128 * (p + 1)] for p in range(N_PAIR)], axis=0)


def _wkv_fwd(r, lw, k2, v, kk, a):
    t = r.shape[0]
    nc = t // CHUNK

    def body(r_ref, lw_ref, k_ref, v_ref, kk_ref, a_ref, y_ref, s_ref, st):
        @pl.when(pl.program_id(0) == 0)
        def _():
            st[...] = jnp.zeros_like(st)

        s0 = st[...]
        s_ref[0] = s0
        y, s1 = _wkv_chunk_fn(s0, *[_pairs(ref) for ref in (r_ref, lw_ref, k_ref, v_ref, kk_ref, a_ref)])
        for p in range(N_PAIR):
            y_ref[:, 128 * p:128 * (p + 1)] = y[p]
        st[...] = s1

    blk = pl.BlockSpec((CHUNK, RW), lambda c: (c, 0))
    return pl.pallas_call(
        body, name="wkv_fwd", grid=(nc,),
        in_specs=[blk] * 6,
        out_specs=[blk, pl.BlockSpec((1, N_PAIR, 128, 128), lambda c: (c, 0, 0, 0))],
        out_shape=[jax.ShapeDtypeStruct((t, RW), F32), jax.ShapeDtypeStruct((nc, N_PAIR, 128, 128), F32)],
        scratch_shapes=[pltpu.VMEM((N_PAIR, 128, 128), F32)],
        compiler_params=_params(("arbitrary",)),
    )(r, lw, k2, v, kk, a)


_POST_TM = 512


ATTN_GROUP = 2


def _dilated_rows(d, r, n):
    if d == 1:
        return pl.ds(pl.multiple_of(n * ATTN_BLOCK, ATTN_BLOCK), ATTN_BLOCK)
    return pl.ds(r + n * (ATTN_BLOCK * d), ATTN_BLOCK, stride=d)


def _for_each_sequence(t, unit):
    for di, d in enumerate(DILATIONS):

        @pl.when(pl.program_id(1) == di)
        def _(di=di, d=d):
            nb = t // (ATTN_BLOCK * d)
            if d == 1:
                unit(di, [(d, 0, 0)], False)
                unit(di, [(d, 0, 1)], True)
                lax.fori_loop(1, nb // 2, lambda k, c: (unit(di, [(d, 0, 2 * k), (d, 0, 2 * k + 1)], True), c)[1], 0)
            else:

                def residues(r, carry):
                    unit(di, [(d, r, 0), (d, r + d // 2, 0)], False)
                    if nb > 1:
                        lax.fori_loop(1, nb, lambda n, c: (unit(di, [(d, r, n), (d, r + d // 2, n)], True), c)[1], 0)
                    return carry

                lax.fori_loop(0, d // 2, residues, 0)


def _take(ref, lead, rows_list):
    return jnp.stack([ref.at[(*lead, g)][rows, :] for rows in rows_list for g in range(ref.shape[len(lead)])], axis=0)


def _put(ref, lead, rows_list, val, add=False):
    k = 0
    for rows in rows_list:
        for g in range(ref.shape[len(lead)]):
            if add:
                ref.at[(*lead, g)][rows, :] += val[k]
            else:
                ref.at[(*lead, g)][rows, :] = val[k]
            k += 1


def _attn_fwd(qkv):
    t = qkv.shape[2]

    def body(q_ref, k_ref, v_ref, o_ref, l_ref):
        def unit(di, places, has_prev):
            cur = [_dilated_rows(d, r, n) for d, r, n in places]
            args = [_take(ref, (0,), cur) for ref in (q_ref, k_ref, v_ref)]
            if has_prev:
                prv = [_dilated_rows(d, r, n - 1) for d, r, n in places]
                args += [_take(ref, (0,), prv) for ref in (k_ref, v_ref)]
            o, lse = _attn_block_fn(*args)
            _put(o_ref, (0,), cur, o)
            _put(l_ref, (0,), cur, lse)

        _for_each_sequence(t, unit)

    spec = lambda j: pl.BlockSpec((1, ATTN_GROUP, t, 128), lambda i, b: (j, i, 0, 0))
    out = pl.BlockSpec((1, ATTN_GROUP, t, 128), lambda i, b: (b, i, 0, 0))
    return pl.pallas_call(
        body, name="attn_fwd", grid=(N_PAIR // ATTN_GROUP, len(DILATIONS)),
        in_specs=[spec(0), spec(1), spec(2)], out_specs=[out, out],
        out_shape=[jax.ShapeDtypeStruct((3, N_PAIR, t, 128), F32)] * 2,
        compiler_params=_params(("parallel", "arbitrary")),
    )(qkv, qkv, qkv)


_COMB_TM = 512


def _mixers_out(y, r, k2, v, g, lnw, lnb, rk, o, l, og):
    t = y.shape[0]
    tm = _COMB_TM

    def body(y_ref, r_ref, k_ref, v_ref, g_ref, lnw_ref, lnb_ref, rk_ref, o_ref, l_ref, og_ref, out_ref):
        out_ref[:, :RW] = _post_fn(y_ref[...], r_ref[...], k_ref[...], v_ref[...], g_ref[...],
                                   lnw_ref[...], lnb_ref[...], rk_ref[...]).astype(BF16)
        for p in range(N_PAIR):
            cols = slice(128 * p, 128 * (p + 1))
            out_ref[:, RW + 128 * p:RW + 128 * (p + 1)] = _combine_fn(
                o_ref[0, p], o_ref[1, p], o_ref[2, p], l_ref[0, p], l_ref[1, p], l_ref[2, p], og_ref[:, cols]).astype(BF16)

    row = pl.BlockSpec((tm, RW), lambda i: (i, 0))
    vec = pl.BlockSpec((1, RW), lambda i: (0, 0))
    blk = pl.BlockSpec((3, N_PAIR, tm, 128), lambda i: (0, 0, i, 0))
    return pl.pallas_call(
        body, name="mixers_out", grid=(t // tm,),
        in_specs=[row] * 5 + [vec] * 3 + [blk, blk, vec], out_specs=pl.BlockSpec((tm, D_MODEL), lambda i: (i, 0)),
        out_shape=jax.ShapeDtypeStruct((t, D_MODEL), BF16),
        compiler_params=_params(("parallel",)),
    )(y, r, k2, v, g, lnw, lnb, rk, o, l, og)


def _ffn_all(x, ycat, wg, wu, wd, wout, g2, gf, tgt):
    t = x.shape[0]
    tm = 256

    def body(x_ref, y_ref, wg_ref, wu_ref, wd_ref, wo_ref, g2_ref, gf_ref, t_ref,
             h_ref, act_ref, dx2b_ref, dgt_ref, dup_ref, dx1b_ref, dx1_ref, dya_ref, dyb_ref, loss_ref, dgf_ref, dg2_ref,
             gt_s, up_s):
        first = pl.program_id(0) == 0
        x1 = x_ref[...] + _dot(y_ref[...], wo_ref[...])
        h = _rms_fwd(x1, g2_ref[...]).astype(BF16)
        h_ref[...] = h
        for c0 in range(0, D_FF, FF_CHUNK):
            cols = slice(c0, c0 + FF_CHUNK)
            gt = _dot_nt(h, wg_ref[cols, :])
            up = _dot_nt(h, wu_ref[cols, :])
            gt_s[:, cols] = gt.astype(BF16)
            up_s[:, cols] = up.astype(BF16)
            act_ref[:, cols] = (gt * _sigmoid(gt) * up).astype(BF16)
        x2 = x1 + _dot(act_ref[...], wd_ref[...])
        gf_ = gf_ref[...]
        diff = _rms_fwd(x2, gf_) - t_ref[...]
        lrow = 0.5 * jnp.sum(_colsum8(diff * diff), axis=1, keepdims=True) * (1.0 / D_MODEL)
        _acc(loss_ref, jnp.broadcast_to(lrow, (8, 128)), first)
        dx2, dgr = _rms_bwd(diff * (1.0 / D_MODEL), x2, gf_)
        _acc(dgf_ref, _colsum8(dgr), first)
        dx2b = dx2.astype(BF16)
        dx2b_ref[...] = dx2b
        for c0 in range(0, D_FF, FF_CHUNK):
            cols = slice(c0, c0 + FF_CHUNK)
            dact = _dot_nt(dx2b, wd_ref[cols, :])
            gt = gt_s[:, cols].astype(F32)
            sg = _sigmoid(gt)
            dgt_ref[:, cols] = (dact * up_s[:, cols].astype(F32) * sg * (1.0 + gt * (1.0 - sg))).astype(BF16)
            dup_ref[:, cols] = (dact * gt * sg).astype(BF16)
        dh = _dot(dgt_ref[...], wg_ref[...]) + _dot(dup_ref[...], wu_ref[...])
        dxn, dgr2 = _rms_bwd(dh, x1, g2_ref[...])
        _acc(dg2_ref, _colsum8(dgr2), first)
        dx1 = dx2 + dxn
        dx1_ref[...] = dx1
        dx1b = dx1.astype(BF16)
        dx1b_ref[...] = dx1b
        dy = _dot_nt(dx1b, wo_ref[...])
        dya_ref[...] = dy[:, :RW]
        dyb_ref[...] = dy[:, RW:]

    row = pl.BlockSpec((tm, D_MODEL), lambda i: (i, 0))
    wide = pl.BlockSpec((tm, D_FF), lambda i: (i, 0))
    half = pl.BlockSpec((tm, RW), lambda i: (i, 0))
    wsp = pl.BlockSpec((D_FF, D_MODEL), lambda i: (0, 0))
    vec = pl.BlockSpec((1, D_MODEL), lambda i: (0, 0))
    part = pl.BlockSpec((8, D_MODEL), lambda i: (0, 0))
    bf = lambda n: jax.ShapeDtypeStruct((t, n), BF16)
    return pl.pallas_call(
        body, name="ffn_all", grid=(t // tm,),
        in_specs=[row, row, wsp, wsp, wsp, pl.BlockSpec((D_MODEL, D_MODEL), lambda i: (0, 0)), vec, vec, row],
        out_specs=[row, wide, row, wide, wide, row, row, half, half, pl.BlockSpec((8, 128), lambda i: (0, 0)), part, part],
        out_shape=[bf(D_MODEL), bf(D_FF), bf(D_MODEL), bf(D_FF), bf(D_FF), bf(D_MODEL),
                   jax.ShapeDtypeStruct((t, D_MODEL), F32), jax.ShapeDtypeStruct((t, RW), F32),
                   jax.ShapeDtypeStruct((t, RW), F32), jax.ShapeDtypeStruct((8, 128), F32),
                   jax.ShapeDtypeStruct((8, D_MODEL), F32), jax.ShapeDtypeStruct((8, D_MODEL), F32)],
        scratch_shapes=[pltpu.VMEM((tm, D_FF), BF16), pltpu.VMEM((tm, D_FF), BF16)],
        compiler_params=_params(("arbitrary",)),
    )(x, ycat, wg, wu, wd, wout, g2, gf, tgt)


def _wgrad(a, b, tk, tn, name):
    t, kdim = a.shape
    ndim = b.shape[1]

    def body(a_ref, b_ref, o_ref):
        o_ref[...] = _dot_tn(a_ref[...], b_ref[...]).astype(BF16)

    return pl.pallas_call(
        body, name=name, grid=(kdim // tk, ndim // tn),
        in_specs=[pl.BlockSpec((t, tk), lambda i, j: (0, i)), pl.BlockSpec((t, tn), lambda i, j: (0, j))],
        out_specs=pl.BlockSpec((tk, tn), lambda i, j: (i, j)),
        out_shape=jax.ShapeDtypeStruct((kdim, ndim), BF16),
        compiler_params=_params(("parallel", "parallel")),
    )(a, b)


def _post_bwd(dya, y, r, k2, v, g, lnw, lnb, rk):
    t = y.shape[0]
    tm = _POST_TM

    def body(d_ref, y_ref, r_ref, k_ref, v_ref, g_ref, lnw_ref, lnb_ref, rk_ref,
             dy_ref, dr_ref, dk_ref, dv_ref, dg_ref, dlnw_ref, dlnb_ref, drk_ref):
        first = pl.program_id(0) == 0
        ones = jnp.ones((tm, 1), F32)
        prim = (y_ref[...], r_ref[...], k_ref[...], v_ref[...], g_ref[...],
                ones * lnw_ref[...], ones * lnb_ref[...], ones * rk_ref[...])
        _, vjp = jax.vjp(_post_fn, *prim)
        dy, dr, dk, dv, dg, dlnw, dlnb, drk = vjp(d_ref[...])
        dy_ref[...] = dy
        dr_ref[...] = dr
        dk_ref[...] = dk
        dv_ref[...] = dv
        dg_ref[...] = dg
        _acc(dlnw_ref, _colsum8(dlnw), first)
        _acc(dlnb_ref, _colsum8(dlnb), first)
        _acc(drk_ref, _colsum8(drk), first)

    row = pl.BlockSpec((tm, RW), lambda i: (i, 0))
    vec = pl.BlockSpec((1, RW), lambda i: (0, 0))
    part = pl.BlockSpec((8, RW), lambda i: (0, 0))
    return pl.pallas_call(
        body, name="rwkv_post_bwd", grid=(t // tm,),
        in_specs=[row] * 6 + [vec] * 3, out_specs=[row] * 5 + [part] * 3,
        out_shape=[jax.ShapeDtypeStruct((t, RW), F32)] * 5 + [jax.ShapeDtypeStruct((8, RW), F32)] * 3,
        compiler_params=_params(("arbitrary",)),
    )(dya, y, r, k2, v, g, lnw, lnb, rk)


def _wkv_bwd(dy, s0s, r, lw, k2, v, kk, a):
    t = r.shape[0]
    nc = t // CHUNK

    def body(dy_ref, s_ref, r_ref, lw_ref, k_ref, v_ref, kk_ref, a_ref,
             dr_ref, dlw_ref, dk_ref, dv_ref, dkk_ref, da_ref, ds):
        @pl.when(pl.program_id(0) == 0)
        def _():
            ds[...] = jnp.zeros_like(ds)

        _, vjp = jax.vjp(_wkv_chunk_fn, s_ref[0],
                         *[_pairs(ref) for ref in (r_ref, lw_ref, k_ref, v_ref, kk_ref, a_ref)])
        res = vjp((_pairs(dy_ref), ds[...]))
        ds[...] = res[0]
        for ref, val in zip((dr_ref, dlw_ref, dk_ref, dv_ref, dkk_ref, da_ref), res[1:]):
            for p in range(N_PAIR):
                ref[:, 128 * p:128 * (p + 1)] = val[p]

    blk = pl.BlockSpec((CHUNK, RW), lambda c: (nc - 1 - c, 0))
    return pl.pallas_call(
        body, name="wkv_bwd", grid=(nc,),
        in_specs=[blk, pl.BlockSpec((1, N_PAIR, 128, 128), lambda c: (nc - 1 - c, 0, 0, 0))] + [blk] * 6,
        out_specs=[blk] * 6,
        out_shape=[jax.ShapeDtypeStruct((t, RW), F32)] * 6,
        scratch_shapes=[pltpu.VMEM((N_PAIR, 128, 128), F32)],
        compiler_params=_params(("arbitrary",)),
    )(dy, s0s, r, lw, k2, v, kk, a)


def _prep_in_proj_bwd(proj, pw, douts, dq, dk, dv, win, x, g1, dx1):
    t = proj.shape[0]
    tm = _PREP_TM
    nt = t // tm

    def body(p_ref, l8_ref, mu, w0, w2p, a0, a2p, g2, k_k, k_a, dr, dr2, dlw, dk2, dk22, dv, dv2, dkk, da, dg,
             dq_ref, dkq_ref, dvq_ref, w_ref, x_ref, g1_ref, dx1_ref,
             dproj_ref, dx_ref, dg1_ref, dmu_ref, dw0_ref, dw2_ref, da0_ref, da2_ref, dg2_ref, dkk_ref, dka_ref, carry):
        i = pl.program_id(0)
        first = i == 0

        @pl.when(first)
        def _():
            carry[...] = jnp.zeros_like(carry)

        p = p_ref[...]
        pprev = _shifted(p, l8_ref[...], i == nt - 1)
        ones = jnp.ones((tm, 1), F32)
        prim = (p, pprev, ones * mu[...], ones * w0[...], w2p[...], ones * a0[...], a2p[...], g2[...],
                ones * k_k[...], ones * k_a[...])
        _, vjp = jax.vjp(_prep_fn, *prim)
        dp, dpp, dmu, dw0, dw2, da0, da2, dg2, dkk_, dka = vjp(
            (dr[...] + dr2[...], dlw[...], dk2[...] + dk22[...], dv[...] + dv2[...], dkk[...], da[...], dg[...]))
        up = pltpu.roll(dpp, tm - 1, axis=0)
        rid = lax.broadcasted_iota(jnp.int32, dpp.shape, 0)
        dpa = dp + jnp.where(rid == tm - 1, carry[0:1, :], up)
        carry[...] = jnp.broadcast_to(dpp[0:1, :], carry.shape)
        _acc(dmu_ref, _colsum8(dmu), first)
        _acc(dw0_ref, _colsum8(dw0), first)
        _acc(dw2_ref, dw2, first)
        _acc(da0_ref, _colsum8(da0), first)
        _acc(da2_ref, da2, first)
        _acc(dg2_ref, dg2, first)
        _acc(dkk_ref, _colsum8(dkk_), first)
        _acc(dka_ref, _colsum8(dka), first)
        parts = [dpa] + [ref[pr] for ref in (dq_ref, dkq_ref, dvq_ref) for pr in range(N_PAIR)]
        dproj = jnp.concatenate([z.astype(BF16) for z in parts], axis=1)
        dproj_ref[...] = dproj
        dxn, dgr = _rms_bwd(_dot(dproj, w_ref[...]), x_ref[...], g1_ref[...])
        dx_ref[...] = dx1_ref[...] + dxn
        _acc(dg1_ref, _colsum8(dgr), first)

    rev = lambda i: (nt - 1 - i, 0)
    row = pl.BlockSpec((tm, RW), rev)
    wide = pl.BlockSpec((tm, D_MODEL), rev)
    pair = pl.BlockSpec((N_PAIR, tm, 128), lambda i: (0, nt - 1 - i, 0))
    part = lambda n: pl.BlockSpec((8, n), lambda i: (0, 0))
    mat = pl.BlockSpec((128, RW), lambda i: (0, 0))
    return pl.pallas_call(
        body, name="prep_in_proj_bwd", grid=(nt,),
        in_specs=[pl.BlockSpec((tm, SHIFT_COLS), rev),
                  pl.BlockSpec((8, SHIFT_COLS), lambda i: (jnp.maximum((nt - 1 - i) * (tm // 8) - 1, 0), 0))]
                 + _prep_specs(tm) + [row] * 10
                 + [pair] * 3 + [pl.BlockSpec((IN_COLS, D_MODEL), lambda i: (0, 0)), wide,
                                 pl.BlockSpec((1, D_MODEL), lambda i: (0, 0)), wide],
        out_specs=[pl.BlockSpec((tm, IN_COLS), rev), wide, part(D_MODEL), part(SHIFT_COLS), part(RW), mat, part(RW), mat,
                   mat, part(RW), part(RW)],
        out_shape=[jax.ShapeDtypeStruct((t, IN_COLS), BF16), jax.ShapeDtypeStruct((t, D_MODEL), F32),
                   jax.ShapeDtypeStruct((8, D_MODEL), F32), jax.ShapeDtypeStruct((8, SHIFT_COLS), F32),
                   jax.ShapeDtypeStruct((8, RW), F32), jax.ShapeDtypeStruct((128, RW), F32),
                   jax.ShapeDtypeStruct((8, RW), F32), jax.ShapeDtypeStruct((128, RW), F32),
                   jax.ShapeDtypeStruct((128, RW), F32), jax.ShapeDtypeStruct((8, RW), F32),
                   jax.ShapeDtypeStruct((8, RW), F32)],
        scratch_shapes=[pltpu.VMEM((8, SHIFT_COLS), F32)],
        compiler_params=_params(("arbitrary",)),
    )(proj, proj, *pw, *douts, dq, dk, dv, win, x, g1, dx1)


def _combine_bwd(dyb, o, l, og):
    t = dyb.shape[0]
    tm = _COMB_TM

    def body(d_ref, o_ref, l_ref, og_ref, do_ref, dl_ref, dog_ref):
        ones = jnp.ones((tm, 1), F32)
        dog = []
        for p in range(N_PAIR):
            cols = slice(128 * p, 128 * (p + 1))
            _, vjp = jax.vjp(_combine_fn, o_ref[0, p], o_ref[1, p], o_ref[2, p], l_ref[0, p], l_ref[1, p], l_ref[2, p],
                             ones * og_ref[:, cols])
            res = vjp(d_ref[:, cols])
            for b in range(3):
                do_ref[b, p] = res[b]
                dl_ref[b, p] = res[3 + b]
            dog.append(_colsum8(res[6]))
        _acc(dog_ref, jnp.concatenate(dog, axis=1), pl.program_id(0) == 0)

    blk = pl.BlockSpec((3, N_PAIR, tm, 128), lambda i: (0, 0, i, 0))
    return pl.pallas_call(
        body, name="attn_combine_bwd", grid=(t // tm,),
        in_specs=[pl.BlockSpec((tm, RW), lambda i: (i, 0)), blk, blk, pl.BlockSpec((1, RW), lambda i: (0, 0))],
        out_specs=[blk, blk, pl.BlockSpec((8, RW), lambda i: (0, 0))],
        out_shape=[jax.ShapeDtypeStruct((3, N_PAIR, t, 128), F32)] * 2 + [jax.ShapeDtypeStruct((8, RW), F32)],
        compiler_params=_params(("arbitrary",)),
    )(dyb, o, l, og)


def _attn_bwd(do, dl, o, lse, qkv):
    t = qkv.shape[2]

    def body(do_ref, dl_ref, o_ref, l_ref, q_ref, k_ref, v_ref, dq_ref, dk_ref, dv_ref):
        @pl.when(pl.program_id(1) == 0)
        def _():
            for ref in (dq_ref, dk_ref, dv_ref):
                ref[...] = jnp.zeros_like(ref)

        def unit(di, places, has_prev):
            cur = [_dilated_rows(d, r, n) for d, r, n in places]
            q, kc, vc = [_take(ref, (0,), cur) for ref in (q_ref, k_ref, v_ref)]
            kp = vp = None
            if has_prev:
                prv = [_dilated_rows(d, r, n - 1) for d, r, n in places]
                kp, vp = [_take(ref, (0,), prv) for ref in (k_ref, v_ref)]
            res = _attn_block_bwd(q, kc, vc, kp, vp, *[_take(ref, (0,), cur) for ref in (o_ref, l_ref, do_ref, dl_ref)])
            _put(dq_ref, (), cur, res[0], add=True)
            _put(dk_ref, (), cur, res[1], add=True)
            _put(dv_ref, (), cur, res[2], add=True)
            if has_prev:
                _put(dk_ref, (), prv, res[3], add=True)
                _put(dv_ref, (), prv, res[4], add=True)

        _for_each_sequence(t, unit)

    spec = lambda j: pl.BlockSpec((1, ATTN_GROUP, t, 128), lambda i, b: (j, i, 0, 0))
    branch = pl.BlockSpec((1, ATTN_GROUP, t, 128), lambda i, b: (b, i, 0, 0))
    out = pl.BlockSpec((ATTN_GROUP, t, 128), lambda i, b: (i, 0, 0))
    return pl.pallas_call(
        body, name="attn_bwd", grid=(N_PAIR // ATTN_GROUP, len(DILATIONS)),
        in_specs=[branch] * 4 + [spec(0), spec(1), spec(2)], out_specs=[out] * 3,
        out_shape=[jax.ShapeDtypeStruct((N_PAIR, t, 128), F32)] * 3,
        compiler_params=_params(("parallel", "arbitrary")),
    )(do, dl, o, lse, qkv, qkv, qkv)


def _pad_lora(w, lo):
    z = jnp.zeros((64, RW), F32)
    return jnp.concatenate([w, z], axis=0) if lo == 0 else jnp.concatenate([z, w], axis=0)


def _local_step(x, tgt, win, vecs, w2, a2, g2m, get_rest, send_rest):
    pw = (vecs["mu_shift"], vecs["decay_w0"], _pad_lora(w2, 0), vecs["iclr_a0"], _pad_lora(a2, 64), g2m,
          vecs["k_k"], vecs["k_a"])
    h, proj, qkv, r, lw, k2, v, kk, a, g = _in_proj_prep(x, vecs["mix_norm_g"], win, pw)
    y, s0s = _wkv_fwd(r, lw, k2, v, kk, a)
    o_att, l_att = _attn_fwd(qkv)
    ycat = _mixers_out(y, r, k2, v, g, vecs["ln_x_w"], vecs["ln_x_b"], vecs["r_k"], o_att, l_att, vecs["attn_out_g"])
    wout, wg, wu, wd = get_rest(ycat)
    h2, act, dx2b, dgt, dup, dx1b, dx1, dya, dyb, loss8, dgf, dg2n = _ffn_all(
        x, ycat, wg, wu, wd, wout, vecs["ffn_norm_g"], vecs["final_norm_g"], tgt)
    gw = {
        "w_down": _wgrad(act, dx2b, 1408, 1024, "wgrad_down"),
        "w_gate": _wgrad(dgt, h2, 1408, 1024, "wgrad_gate"),
        "w_up": _wgrad(dup, h2, 1408, 1024, "wgrad_up"),
        "w_out": _wgrad(ycat, dx1b, 1024, 1024, "wgrad_out"),
    }

    lnw = vecs["ln_x_w"] + send_rest(gw)[0, 0]
    dy, dr_p, dk2_p, dv_p, dg, dlnw, dlnb, drk = _post_bwd(dya, y, r, k2, v, g, lnw, vecs["ln_x_b"], vecs["r_k"])
    dr_s, dlw, dk2_s, dv_s, dkk, da = _wkv_bwd(dy, s0s, r, lw, k2, v, kk, a)
    do_att, dl_att, dog = _combine_bwd(dyb, o_att, l_att, vecs["attn_out_g"])
    dq, dk, dv = _attn_bwd(do_att, dl_att, o_att, l_att, qkv)
    dproj, dx, dg1, dmu, dw0, dw2p, da0, da2p, dg2m, dk_k, dk_a = _prep_in_proj_bwd(
        proj, pw, (dr_p, dr_s, dlw, dk2_p, dk2_s, dv_p, dv_s, dkk, da, dg), dq, dk, dv, win, x, vecs["mix_norm_g"], dx1)
    gw["w_in"] = _wgrad(dproj, h, 1664, 1024, "wgrad_in")
    gw["decay_w2"] = dw2p[:64]
    gw["iclr_a2"] = da2p[64:]
    gw["gate_g2"] = dg2m
    gv = {"mix_norm_g": dg1, "mu_shift": dmu, "decay_w0": dw0, "iclr_a0": da0, "k_k": dk_k, "k_a": dk_a, "r_k": drk,
          "ln_x_w": dlnw, "ln_x_b": dlnb, "attn_out_g": dog, "ffn_norm_g": dg2n, "final_norm_g": dgf}
    return loss8, dx, gw, gv


N_CHIP = 4
N_DEV = 8
MATS = ("w_in", "w_out", "w_gate", "w_up", "w_down")
LORAS = ("decay_w2", "iclr_a2", "gate_g2")
VECS = (("mix_norm_g", 1024), ("mu_shift", 1792), ("decay_w0", 512), ("iclr_a0", 512), ("k_k", 512), ("k_a", 512),
        ("r_k", 512), ("ln_x_w", 512), ("ln_x_b", 512), ("attn_out_g", 512), ("ffn_norm_g", 1024),
        ("final_norm_g", 1024))
N_VEC = sum(n for _, n in VECS)
N_SMALL = N_VEC + 128
ANY = pl.BlockSpec(memory_space=pl.ANY)


def _flip(v, f):
    return 1 - v if f else v


class _Me:
    def __init__(self, mode):
        x, y, c = lax.axis_index("x"), lax.axis_index("y"), lax.axis_index("c")
        self.core, self.chip, self.dev = c, 2 * x + y, 4 * x + 2 * y + c
        self.sibling = (x, y, 1 - c)
        if mode == "chips":
            self.peers = [(px, py, c) for px, py in ((1 - x, y), (x, 1 - y), (1 - x, 1 - y))]
        else:
            self.peers = [(_flip(x, k & 4), _flip(y, k & 2), _flip(c, k & 1)) for k in range(1, N_DEV)]


def _half(core, rows):
    h = rows // 2
    return pl.ds(pl.multiple_of(core * h, h), h)


_BY_CHIP = ("gather", "whole", "chipsum")


def _peer_copy(srcs, dsts, kinds, send_sems, recv_sems, me, j, i, incoming):
    px, py, pc = me.peers[j]
    pchip, pdev = 2 * px + py, 4 * px + 2 * py + pc
    src, dst, kind = srcs[i], dsts[i], kinds[i]
    if kind in ("gather", "whole"):
        rows = _half(me.core, src.shape[1]) if kind == "gather" else pl.ds(0, src.shape[1])
        src, dst = src.at[me.chip, rows], dst.at[pchip if incoming else me.chip, rows]
    elif kind == "scatter":
        src, dst = src.at[pchip, _half(pc, src.shape[1])], dst.at[pdev if incoming else me.dev]
    elif kind == "chipsum":
        src, dst = src.at[pchip], dst.at[pchip if incoming else me.chip]
    else:
        dst = dst.at[pdev if incoming else me.dev]
    n = len(srcs)
    return pltpu.make_async_remote_copy(src_ref=src, dst_ref=dst, send_sem=send_sems.at[n * j + i],
                                        recv_sem=recv_sems.at[n * j + i], device_id=(px, py, pc), device_id_type=MESH)


def _mode(kinds):
    return "chips" if kinds[0] in _BY_CHIP else "devs"


def _npeer(kinds):
    return N_CHIP - 1 if kinds[0] in _BY_CHIP else N_DEV - 1


def _sibling_halves(gs, name):
    n = len(gs)

    def body(*refs):
        srcs, dsts, send_sems, recv_sems = refs[:n], refs[n:2 * n], refs[2 * n], refs[2 * n + 1]
        me = _Me("chips")

        def copy(i, p):
            return pltpu.make_async_remote_copy(
                src_ref=srcs[i].at[p, _half(1 - me.core, srcs[i].shape[1])], dst_ref=dsts[i].at[p],
                send_sem=send_sems.at[N_CHIP * i + p], recv_sem=recv_sems.at[N_CHIP * i + p],
                device_id=me.sibling, device_id_type=MESH)

        copies = [copy(i, p) for i in range(n) for p in range(N_CHIP)]
        for cp in copies:
            cp.start()
        for cp in copies:
            cp.wait()

    return pl.pallas_call(
        body, name=name, in_specs=[ANY] * n, out_specs=[ANY] * n,
        out_shape=[jax.ShapeDtypeStruct((N_CHIP, g.shape[1] // 2, g.shape[2]), g.dtype) for g in gs],
        scratch_shapes=[pltpu.SemaphoreType.DMA((N_CHIP * n,)), pltpu.SemaphoreType.DMA((N_CHIP * n,))],
    )(*gs)


def _add_halves(g, other, core, tr, name):
    _, h, cols = other.shape

    def body(core_ref, g_ref, o_ref, out_ref):
        out_ref[...] = (g_ref[...].astype(F32) + o_ref[...].astype(F32)).astype(BF16)

    blk = lambda off: pl.BlockSpec((1, tr, cols), lambda p, i, core_ref: (p, core_ref[0] * (h // tr) * off + i, 0))
    return pl.pallas_call(
        body, name=name,
        grid_spec=pltpu.PrefetchScalarGridSpec(num_scalar_prefetch=1, grid=(N_CHIP, h // tr),
                                               in_specs=[blk(1), blk(0)], out_specs=blk(0)),
        out_shape=jax.ShapeDtypeStruct(other.shape, BF16),
        compiler_params=_params(("parallel", "parallel")),
    )(core, g, other)


def _swap_gathered(lands, name):
    n = len(lands)

    def body(*refs):
        dsts, send_sems, recv_sems = refs[n:2 * n], refs[2 * n], refs[2 * n + 1]
        me = _Me("chips")

        def copy(j, i, incoming):
            px, py, _ = me.peers[j]
            rows_out, rows_in = _half(me.core, dsts[i].shape[1]), _half(1 - me.core, dsts[i].shape[1])
            return pltpu.make_async_remote_copy(
                src_ref=dsts[i].at[2 * px + py, rows_out], dst_ref=dsts[i].at[2 * px + py, rows_in if incoming else rows_out],
                send_sem=send_sems.at[n * j + i], recv_sem=recv_sems.at[n * j + i], device_id=me.sibling, device_id_type=MESH)

        sends = [copy(j, i, False) for j in range(3) for i in range(n)]
        for cp in sends:
            cp.start()
        for j in range(3):
            for i in range(n):
                copy(j, i, True).wait_recv()
        for cp in sends:
            cp.wait_send()

    return pl.pallas_call(
        body, name=name, in_specs=[ANY] * n, out_specs=[ANY] * n,
        out_shape=[jax.ShapeDtypeStruct(l.shape, l.dtype) for l in lands],
        input_output_aliases={i: i for i in range(n)},
        scratch_shapes=[pltpu.SemaphoreType.DMA((3 * n,)), pltpu.SemaphoreType.DMA((3 * n,))],
    )(*lands)


def _join_halves(sums, name):
    n = len(sums)

    def body(*refs):
        dsts, send_sems, recv_sems = refs[n:2 * n], refs[2 * n], refs[2 * n + 1]
        me = _Me("chips")

        def copy(i, incoming):
            mine, other = _half(me.core, dsts[i].shape[0]), _half(1 - me.core, dsts[i].shape[0])
            return pltpu.make_async_remote_copy(src_ref=dsts[i].at[mine], dst_ref=dsts[i].at[other if incoming else mine],
                                                send_sem=send_sems.at[i], recv_sem=recv_sems.at[i],
                                                device_id=me.sibling, device_id_type=MESH)

        sends = [copy(i, False) for i in range(n)]
        for cp in sends:
            cp.start()
        for i in range(n):
            copy(i, True).wait_recv()
        for cp in sends:
            cp.wait_send()

    return pl.pallas_call(
        body, name=name, in_specs=[ANY] * n, out_specs=[ANY] * n,
        out_shape=[jax.ShapeDtypeStruct(s.shape, s.dtype) for s in sums],
        input_output_aliases={i: i for i in range(n)},
        scratch_shapes=[pltpu.SemaphoreType.DMA((n,)), pltpu.SemaphoreType.DMA((n,))],
    )(*sums)


HBM = pl.BlockSpec(memory_space=pltpu.HBM)
SEM = pl.BlockSpec(memory_space=pltpu.SEMAPHORE)
EFFECT = pltpu.SideEffectType.DATAFLOW_SIDE_EFFECTING


def _swap_start(arrs, lands, kinds, name):
    n = len(lands)
    ops = list(lands) if arrs is None else [*arrs, *lands]
    k = len(ops)

    def body(*refs):
        srcs, dsts, send_sems, recv_sems, token = refs[:n], refs[k - n:k], refs[k], refs[k + 1], refs[-1]
        me = _Me(_mode(kinds))
        for j in range(len(me.peers)):
            for i in range(n):
                _peer_copy(srcs, dsts, kinds, send_sems, recv_sems, me, j, i, False).start()
        token[...] = jnp.zeros_like(token)

    ns = _npeer(kinds) * n
    outs = pl.pallas_call(
        body, name=name,
        out_shape=(pltpu.SemaphoreType.DMA((ns,)), pltpu.SemaphoreType.DMA((ns,)),
                   *[pltpu.HBM(a.shape, a.dtype) for a in ops], jax.ShapeDtypeStruct((8, 128), F32)),
        in_specs=[HBM] * k, out_specs=(SEM, SEM, *[HBM] * k, pl.BlockSpec(memory_space=pltpu.VMEM)),
        input_output_aliases={i: 2 + i for i in range(k)},
        compiler_params=pltpu.CompilerParams(has_side_effects=EFFECT),
    )(*[pltpu.with_memory_space_constraint(a, pltpu.HBM) for a in ops])
    return outs[0], outs[1], outs[2:2 + k - n], outs[2 + k - n:2 + k], outs[-1]


def _swap_wait(send_sems, recv_sems, srcs_thru, lands_thru, after, kinds, name):
    n = len(lands_thru)
    ops = [*srcs_thru, *lands_thru]
    k = len(ops)

    def body(*refs):
        srcs, dsts, s_sems, r_sems = refs[:n], refs[k - n:k], refs[k], refs[k + 1]
        me = _Me(_mode(kinds))
        for j in range(len(me.peers)):
            for i in range(n):
                cp = _peer_copy(srcs, dsts, kinds, s_sems, r_sems, me, j, i, True)
                cp.wait_send()
                cp.wait_recv()

    outs = pl.pallas_call(
        body, name=name,
        out_shape=tuple(pltpu.HBM(a.shape, a.dtype) for a in ops),
        in_specs=[HBM] * k + [SEM, SEM, ANY], out_specs=tuple([HBM] * k),
        input_output_aliases={i: i for i in range(k)},
        compiler_params=pltpu.CompilerParams(has_side_effects=EFFECT),
    )(*ops, send_sems, recv_sems, after)
    return outs[k - n:]


def _adamw(w, g, m, v):
    m = ADAM_B1 * m + (1.0 - ADAM_B1) * g
    v = ADAM_B2 * v + (1.0 - ADAM_B2) * (g * g)
    m_hat = m / (1.0 - ADAM_B1 ** ADAM_STEP)
    v_hat = v / (1.0 - ADAM_B2 ** ADAM_STEP)
    delta = -ADAM_LR * (m_hat / (jnp.sqrt(v_hat) + ADAM_EPS) + ADAM_WD * w)
    return delta, m, v


def _reduce8(rbuf, core, tr, name):
    slots, h, cols = rbuf.shape

    def body(core_ref, r_ref, g_ref):
        g = r_ref[0].astype(F32)
        for s in range(1, slots):
            g = g + r_ref[s].astype(F32)
        g_ref[...] = g

    return pl.pallas_call(
        body, name=name,
        grid_spec=pltpu.PrefetchScalarGridSpec(
            num_scalar_prefetch=1, grid=(h // tr,),
            in_specs=[pl.BlockSpec((slots, tr, cols), lambda i, core_ref: (0, i, 0))],
            out_specs=pl.BlockSpec((tr, cols), lambda i, core_ref: (core_ref[0] * (h // tr) + i, 0))),
        out_shape=jax.ShapeDtypeStruct((2 * h, cols), F32),
        compiler_params=_params(("parallel",)),
    )(core, rbuf)


def _adamw_call(g, w, m, v, tr, name):
    _, rows, cols = w.shape

    def body(g_in, w_ref, m_ref, v_ref, g_ref, d_ref, nm_ref, nv_ref):
        g = g_in[...]
        g_ref[0] = g
        d_ref[0], nm_ref[0], nv_ref[0] = _adamw(w_ref[0], g, m_ref[0], v_ref[0])

    row = pl.BlockSpec((1, tr, cols), lambda i: (0, i, 0))
    return pl.pallas_call(
        body, name=name, grid=(rows // tr,),
        in_specs=[pl.BlockSpec((tr, cols), lambda i: (i, 0)), row, row, row], out_specs=[row] * 4,
        out_shape=[jax.ShapeDtypeStruct(w.shape, F32)] * 4,
        compiler_params=_params(("parallel",)),
    )(g, w, m, v)


def _adamw_lora(g, ws3, ms3, vs3):
    def body(g_in, *refs):
        ins, outs = refs[:9], refs[9:]
        r0 = 0
        for i in range(3):
            rows = ins[i].shape[1]
            g = g_in[r0:r0 + rows, :]
            outs[i][0] = g
            outs[3 + i][0], outs[6 + i][0], outs[9 + i][0] = _adamw(ins[i][0], g, ins[3 + i][0], ins[6 + i][0])
            r0 += rows

    return pl.pallas_call(body, name="adamw_lora",
                          out_shape=[jax.ShapeDtypeStruct(a.shape, F32) for a in ws3] * 4)(g, *ws3, *ms3, *vs3)


def _rowsum_small(parts, loss8):
    def body(*refs):
        out = refs[-1]
        c0 = 0
        for ref in refs[:-1]:
            n = ref.shape[1]
            out[:, c0:c0 + n] = jnp.sum(ref[...], axis=0, keepdims=True)
            c0 += n

    return pl.pallas_call(body, name="rowsum_small", out_shape=jax.ShapeDtypeStruct((1, N_SMALL), F32))(*parts, loss8)


def _reduce_adamw_small(sbuf, ws, ms, vs):
    nv = len(ws)

    def body(*refs):
        s_ref, ins, outs = refs[0], refs[1:1 + 3 * nv], refs[1 + 3 * nv:]
        tot = s_ref[0]
        for s in range(1, N_DEV):
            tot = tot + s_ref[s]
        c0 = 0
        for i in range(nv):
            rows, cols = ins[i].shape
            n = rows * cols
            for r in range(rows):
                outs[i][r:r + 1, :] = tot[:, c0 + cols * r:c0 + cols * (r + 1)]
            g = outs[i][...]
            outs[nv + i][...], outs[2 * nv + i][...], outs[3 * nv + i][...] = _adamw(
                ins[i][...], g, ins[nv + i][...], ins[2 * nv + i][...])
            c0 += n
        outs[-1][...] = tot[:, c0:]

    return pl.pallas_call(
        body, name="reduce_adamw_small",
        out_shape=[jax.ShapeDtypeStruct(a.shape, F32) for a in ws] * 4 + [jax.ShapeDtypeStruct((1, 128), F32)],
    )(sbuf, *ws, *ms, *vs)


_TRANSPOSED = ("w_in", "w_gate", "w_up")
_ROW_STACKED = MATS
_ADAM_TILE = {"w_in": 208, "w_out": 256, "w_gate": 176, "w_up": 176, "w_down": 176, "lora": 256}
_SUM_TILE = {"w_in": 208, "w_out": 128, "w_gate": 176, "w_up": 176, "w_down": 176, "lora": 128}


def _full(n, stacked):
    p, r, c = stacked.shape
    if n in _ROW_STACKED:
        return stacked.reshape(p * r, c)
    return jnp.transpose(stacked, (1, 0, 2)).reshape(r, p * c)


def _by_chip(n, full):
    if n in _ROW_STACKED:
        return full.reshape(N_CHIP, full.shape[0] // N_CHIP, full.shape[1])
    r, c = full.shape
    return jnp.transpose(full.reshape(r, N_CHIP, c // N_CHIP), (1, 0, 2))


def _with_own(land_shape, dtype, own, slot):
    return lax.dynamic_update_slice(lax.empty(land_shape, dtype), own[None], (slot,) + (0,) * own.ndim)


def _cast_into_slot(a, chip, tr, name, after=None):
    rows, cols = a.shape

    def body(chip_ref, a_ref, *rest):
        rest[-1][0] = a_ref[...].astype(BF16)

    extra = [] if after is None else [after]
    return pl.pallas_call(
        body, name=name,
        grid_spec=pltpu.PrefetchScalarGridSpec(
            num_scalar_prefetch=1, grid=(rows // tr,),
            in_specs=[pl.BlockSpec((tr, cols), lambda i, chip_ref: (i, 0))] + [ANY] * len(extra),
            out_specs=pl.BlockSpec((1, tr, cols), lambda i, chip_ref: (chip_ref[0], i, 0))),
        out_shape=jax.ShapeDtypeStruct((N_CHIP, rows, cols), BF16),
        compiler_params=_params(("parallel",)),
    )(chip, a, *extra)


def kernel(x, mix_norm_g, w_in, mu_shift, decay_w0, decay_w2, iclr_a0, iclr_a2, gate_g2, k_k, k_a, r_k, ln_x_w, ln_x_b, attn_out_g, w_out, ffn_norm_g, w_gate, w_up, w_down, final_norm_g, loss_target, m_mix_norm_g, m_w_in, m_mu_shift, m_decay_w0, m_decay_w2, m_iclr_a0, m_iclr_a2, m_gate_g2, m_k_k, m_k_a, m_r_k, m_ln_x_w, m_ln_x_b, m_attn_out_g, m_w_out, m_ffn_norm_g, m_w_gate, m_w_up, m_w_down, m_final_norm_g, v_mix_norm_g, v_w_in, v_mu_shift, v_decay_w0, v_decay_w2, v_iclr_a0, v_iclr_a2, v_gate_g2, v_k_k, v_k_a, v_r_k, v_ln_x_w, v_ln_x_b, v_attn_out_g, v_w_out, v_ffn_norm_g, v_w_gate, v_w_up, v_w_down, v_final_norm_g):
    names = ("mix_norm_g", "w_in", "mu_shift", "decay_w0", "decay_w2", "iclr_a0", "iclr_a2", "gate_g2", "k_k", "k_a",
             "r_k", "ln_x_w", "ln_x_b", "attn_out_g", "w_out", "ffn_norm_g", "w_gate", "w_up", "w_down", "final_norm_g")
    w = dict(zip(names, (mix_norm_g, w_in, mu_shift, decay_w0, decay_w2, iclr_a0, iclr_a2, gate_g2, k_k, k_a, r_k,
                         ln_x_w, ln_x_b, attn_out_g, w_out, ffn_norm_g, w_gate, w_up, w_down, final_norm_g)))
    m = dict(zip(names, (m_mix_norm_g, m_w_in, m_mu_shift, m_decay_w0, m_decay_w2, m_iclr_a0, m_iclr_a2, m_gate_g2,
                         m_k_k, m_k_a, m_r_k, m_ln_x_w, m_ln_x_b, m_attn_out_g, m_w_out, m_ffn_norm_g, m_w_gate,
                         m_w_up, m_w_down, m_final_norm_g)))
    v = dict(zip(names, (v_mix_norm_g, v_w_in, v_mu_shift, v_decay_w0, v_decay_w2, v_iclr_a0, v_iclr_a2, v_gate_g2,
                         v_k_k, v_k_a, v_r_k, v_ln_x_w, v_ln_x_b, v_attn_out_g, v_w_out, v_ffn_norm_g, v_w_gate,
                         v_w_up, v_w_down, v_final_norm_g)))
    first = ("w_in", "lora")
    rest = ("w_out", "w_gate", "w_up", "w_down")
    xi, yi, ci = lax.axis_index("x"), lax.axis_index("y"), lax.axis_index("c")
    my_chip, my_dev = 2 * xi + yi, 4 * xi + 2 * yi + ci
    gather, scatter = ("gather",) * 4, ("scatter",) * 4

    def stored(d):
        out = {n: jnp.transpose(d[n][0]) if n in _TRANSPOSED else d[n][0] for n in MATS}
        out["lora"] = jnp.concatenate([d[n][0] for n in LORAS], axis=0)
        return out

    ws, ms, vs = stored(w), stored(m), stored(v)
    lora_rows = [(0, 64), (64, 128), (128, 256)]
    chip = jnp.reshape(my_chip, (1,)).astype(jnp.int32)
    early = _swap_start(None, [_cast_into_slot(ws["w_in"], chip, _ADAM_TILE["w_in"], "cast_w_in"),
                               _with_own((N_CHIP,) + ws["lora"].shape, F32, ws["lora"], my_chip)], gather[:2],
                        "gather_first_start")
    lands = [_cast_into_slot(ws[n], chip, _ADAM_TILE[n], "cast_" + n, after=early[4]) for n in rest]
    gather_rest = ("gather", "gather", "whole", "whole")
    ssem, rsem, srcs_thru, lands_thru, tok = _swap_start(None, lands, gather_rest, "gather_rest_start")
    got = _swap_wait(early[0], early[1], early[2], early[3], tok, gather[:2], "gather_first_wait")
    win_all, lora_all = _swap_gathered(got, "gather_first_halves")
    win = _full("w_in", win_all)
    w2, a2, g2m = (_full(n, lora_all[:, a:b]) for n, (a, b) in zip(LORAS, lora_rows))

    vecs = {n: w[n].reshape(1, sz) for n, sz in VECS}
    vecs["mix_norm_g"] = vecs["mix_norm_g"] + tok[0, 0]

    def get_rest(after):
        got_rest = _swap_wait(ssem, rsem, srcs_thru, lands_thru, after, gather_rest, "gather_rest_wait")
        swapped = _swap_gathered(got_rest[:2], "gather_rest_halves")
        return [_full(n, z) for n, z in zip(rest, [*swapped, *got_rest[2:]])]

    flight = []

    def my_half(g):
        h = g.shape[1] // 2
        return lax.dynamic_slice(g, (my_chip, ci * h, 0), (1, h, g.shape[2]))[0]

    def send_rest(gw):
        gs = [_by_chip(n, gw[n]) for n in rest]
        into = [_with_own((N_DEV,) + my_half(g).shape, BF16, my_half(g), my_dev) for g in gs]
        flight.extend(_swap_start(gs, into, scatter, "exchange_rest_start"))
        return flight[4]

    loss8, dx, gw, gv = _local_step(x[0], loss_target[0], win, vecs, w2, a2, g2m, get_rest, send_rest)

    core = jnp.reshape(ci, (1,)).astype(jnp.int32)
    gs = [_by_chip("w_in", gw["w_in"]),
          jnp.concatenate([_by_chip(n, gw[n]) for n in LORAS], axis=1).astype(BF16)]
    theirs = _sibling_halves(gs, "presum_halves")
    sums = [_add_halves(g, o, core, _SUM_TILE[n], "chipsum_" + n) for n, g, o in zip(first, gs, theirs)]
    own = [lax.dynamic_index_in_dim(s, my_chip, 0, keepdims=False) for s in sums]
    last = _swap_start(sums, [_with_own(s.shape, BF16, o, my_chip) for s, o in zip(sums, own)], ("chipsum",) * 2,
                       "exchange_first_start")
    small = _rowsum_small([gv[n] for n, _ in VECS], loss8 + last[4])
    vecs_out = _swap_start([small], [_with_own((N_DEV,) + small.shape, F32, small, my_dev)], ("all",),
                           "exchange_vectors_start")


    def update(group, rbufs, tag):
        sums = [_reduce8(rb, core, _SUM_TILE[n], "reduce_" + n) for n, rb in zip(group, rbufs)]
        gsum = _join_halves(sums, "join_halves_" + tag)
        out = {}
        for n, g in zip(group, gsum):
            if n == "lora":
                r = _adamw_lora(g, *[[d[k] for k in LORAS] for d in (w, m, v)])
                for i, name in enumerate(LORAS):
                    out[name] = r[i::3]
            else:
                r = _adamw_call(g, ws[n][None], ms[n][None], vs[n][None], _ADAM_TILE[n], "adamw_" + n)
                out[n] = [jnp.transpose(z[0])[None] for z in r] if n in _TRANSPOSED else r
        return out, r[1]

    res, done = update(rest, _swap_wait(flight[0], flight[1], flight[2], flight[3], vecs_out[4], scatter,
                                        "exchange_rest_wait"), "rest")
    got = _swap_wait(last[0], last[1], last[2], last[3], done, ("chipsum",) * 2, "exchange_first_wait")
    res_first, done = update(first, got, "first")
    res.update(res_first)
    sbuf = _swap_wait(vecs_out[0], vecs_out[1], vecs_out[2], vecs_out[3], done, ("all",), "exchange_vectors_wait")[0]
    rows = lambda d: [d[n].reshape(-1, d[n].shape[-1]) for n, _ in VECS]
    small_res = _reduce_adamw_small(sbuf, rows(w), rows(m), rows(v))

    outs = []
    for k in range(4):
        piece = {n: r[k] for n, r in res.items()}
        for i, (n, _) in enumerate(VECS):
            piece[n] = small_res[k * len(VECS) + i].reshape(w[n].shape)
        outs.extend(piece[n] for n in names)
    return (small_res[-1][0, 0], dx[None], *outs)
```

```python
import jax
import jax.numpy as jnp
from jax import lax
from jax.experimental import pallas as pl
from jax.experimental.pallas import tpu as pltpu

F32 = jnp.float32
BF16 = jnp.bfloat16

D_MODEL = 1024
HEAD_DIM = 64
RW = 512
N_PAIR = RW // 128
SHIFT_COLS = 1792
IN_COLS = 3328
D_FF = 2816
FF_CHUNK = 256
NORM_EPS = 1e-6
GN_EPS = 64e-5
CHUNK = 64
SUB = 16
WKV_PASSES = 1
ATTN_PASSES = 1
ATTN_BLOCK = 128
DILATIONS = (1, 4, 16)
NEG = -1e30
ADAM_LR, ADAM_B1, ADAM_B2, ADAM_EPS, ADAM_WD, ADAM_STEP = 0.001, 0.9, 0.999, 1e-08, 0.01, 10
VMEM_LIMIT = 56 * 1024 * 1024
MESH = pl.DeviceIdType.MESH


def _params(sem=None, **kw):
    return pltpu.CompilerParams(dimension_semantics=sem, vmem_limit_bytes=VMEM_LIMIT, **kw)


def _dot(a, b):
    return lax.dot_general(a, b, (((1,), (0,)), ((), ())), preferred_element_type=F32)


def _dot_nt(a, b):
    return lax.dot_general(a, b, (((1,), (1,)), ((), ())), preferred_element_type=F32)


def _dot_tn(a, b):
    return lax.dot_general(a, b, (((0,), (0,)), ((), ())), preferred_element_type=F32)


_FORMS = {"nn": ((1,), (0,)), "nt": ((1,), (1,)), "tn": ((0,), (0,))}


def _dg(a, b, form):
    if a.ndim == 3 or b.ndim == 3:
        nb = a.shape[0] if a.ndim == 3 else b.shape[0]
        return jnp.stack([_dg(a[i] if a.ndim == 3 else a, b[i] if b.ndim == 3 else b, form) for i in range(nb)], axis=0)
    return lax.dot_general(a, b, (_FORMS[form], ((), ())), preferred_element_type=F32)


def _split2(x):
    hi = x.astype(BF16)
    return hi, (x - hi.astype(F32)).astype(BF16)


def _split3(x):
    hi = x.astype(BF16)
    rest = x - hi.astype(F32)
    mid = rest.astype(BF16)
    return hi, mid, (rest - mid.astype(F32)).astype(BF16)


def _mm_raw(a, b, form, mode):
    if mode == 1:
        return _dg(a.astype(BF16), b.astype(BF16), form)
    if mode == 3:
        ah, al = _split2(a)
        bh, bl = _split2(b)
        return _dg(ah, bh, form) + (_dg(ah, bl, form) + _dg(al, bh, form))
    if mode == "L3":
        ab = a.astype(BF16)
        b1, b2, b3 = _split3(b)
        if form == "nn":
            n = b.shape[-1]
            wide = _dg(ab, jnp.concatenate([b1, b2, b3], axis=-1), form)
            return wide[..., :n] + (wide[..., n:2 * n] + wide[..., 2 * n:])
        return _dg(ab, b1, form) + (_dg(ab, b2, form) + _dg(ab, b3, form))
    assert mode == "R3", mode
    bb = b.astype(BF16)
    a1, a2, a3 = _split3(a)
    if form in ("nn", "nt"):
        m = a.shape[-2]
        tall = _dg(jnp.concatenate([a1, a2, a3], axis=-2), bb, form)
        return tall[..., :m, :] + (tall[..., m:2 * m, :] + tall[..., 2 * m:, :])
    return _dg(a1, bb, form) + (_dg(a2, bb, form) + _dg(a3, bb, form))


def _mm(a, b, form, mode):
    @jax.custom_vjp
    def f(a, b):
        return _mm_raw(a, b, form, mode)

    def fwd(a, b):
        return _mm_raw(a, b, form, mode), (a, b)

    def bwd(res, ct):
        a, b = res
        la = {1: 1, 3: 3, "L3": None, "R3": "R3"}[mode]
        lb = {1: 1, 3: 3, "L3": "L3", "R3": None}[mode]
        if form == "nn":
            da = None if la is None else _mm_raw(ct, b, "nt", la)
            db = None if lb is None else _mm_raw(a, ct, "tn", lb)
        elif form == "nt":
            da = None if la is None else _mm_raw(ct, b, "nn", la)
            db = None if lb is None else _mm_raw(ct, a, "tn", "R3" if lb == "L3" else lb)
        else:
            da = None if la is None else _mm_raw(b, ct, "nt", "L3" if la == "R3" else la)
            db = None if lb is None else _mm_raw(a, ct, "nn", lb)
        return (jnp.zeros_like(a) if da is None else da, jnp.zeros_like(b) if db is None else db)

    f.defvjp(fwd, bwd)
    return f(a, b)


def _seg_ones(n):
    r = lax.broadcasted_iota(jnp.int32, (n, n), 0) // HEAD_DIM
    c = lax.broadcasted_iota(jnp.int32, (n, n), 1) // HEAD_DIM
    return (r == c).astype(F32)


def _segsum(x, seg):
    return _mm(x, seg, "nn", "R3")


def _rms_fwd(x, g):
    rstd = lax.rsqrt(jnp.mean(x * x, axis=-1, keepdims=True) + NORM_EPS)
    return x * rstd * g


def _rms_bwd(dy, x, g):
    rstd = lax.rsqrt(jnp.mean(x * x, axis=-1, keepdims=True) + NORM_EPS)
    xn = x * rstd
    dxn = dy * g
    dx = rstd * (dxn - xn * jnp.mean(dxn * xn, axis=-1, keepdims=True))
    return dx, dy * xn


def _sigmoid(x):
    return 1.0 / (1.0 + jnp.exp(-x))


def _softplus(x):
    return jnp.maximum(x, 0.0) + jnp.log(1.0 + jnp.exp(-jnp.abs(x)))


def _acc(ref, val, first):
    @pl.when(first)
    def _():
        ref[...] = val

    @pl.when(jnp.logical_not(first))
    def _():
        ref[...] += val


def _colsum8(v):
    rows, n = v.shape
    return jnp.sum(v.reshape(rows // 8, 8, n), axis=0)


def _prep_fn(p, pprev, mu, w0, w2p, a0, a2p, g2, k_k, k_a):
    seg = _seg_ones(RW)
    ps = p + (pprev - p) * mu
    r = ps[:, 0:RW]
    k = ps[:, RW:2 * RW]
    v = ps[:, 2 * RW:3 * RW]
    xwa = ps[:, 3 * RW:3 * RW + 128]
    xg = ps[:, 3 * RW + 128:3 * RW + 256]
    wraw = -_softplus(-(w0 + _mm(jnp.tanh(xwa), w2p, "nn", 3))) - 0.5
    lw = -jnp.exp(wraw)
    a = _sigmoid(a0 + _mm(xwa, a2p, "nn", 3))
    g = _mm(_sigmoid(xg), g2, "nn", 3)
    kk = k * k_k
    kk = kk / jnp.maximum(jnp.sqrt(_segsum(kk * kk, seg)), 1e-12)
    k2 = k * (1.0 + (a - 1.0) * k_a)
    return r, lw, k2, v, kk, a, g


def _transposed(z):
    return jnp.stack([z[i].T for i in range(z.shape[0])], axis=0) if z.ndim == 3 else z.T


def _solve_unit_lower(lmat, rhs):
    c = lmat.shape[-1]
    row = lax.broadcasted_iota(jnp.int32, (c, c), 0)
    col = lax.broadcasted_iota(jnp.int32, (c, c), 1)
    eye = (row == col).astype(F32)
    ld = jnp.where(row // SUB == col // SUB, lmat, 0.0)
    lo = lmat - ld
    x = eye + ld
    m = ld
    mm = lambda p, q: _mm(p, q, "nn", WKV_PASSES)
    cat = jnp.concatenate
    m = mm(m, m)
    for _ in range(2):
        mx = mm(m, cat([m, x], axis=-1))
        m, x = mx[..., :c], x + mx[..., c:]
    x = x + mm(m, x)
    gw = mm(x, cat([lo, rhs], axis=-1))
    g, w = gw[..., :c], gw[..., c:]
    gg = mm(g, cat([g, w], axis=-1))
    w = w + gg[..., c:]
    return w + mm(gg[..., :c], w)


def _wkv_chunk_fn(s0, r, lw, k, v, kk, a):
    c = r.shape[-2]
    n = 2 * c
    row = lax.broadcasted_iota(jnp.int32, (n, n), 0)
    col = lax.broadcasted_iota(jnp.int32, (n, n), 1)
    same = (row // c) == (col // c)
    incl = jnp.logical_and(row >= col, same)
    strict = jnp.logical_and(row > col, same)
    sel = (lax.broadcasted_iota(jnp.int32, (n, 128), 0) // c) == (lax.broadcasted_iota(jnp.int32, (n, 128), 1) // HEAD_DIM)
    two = lambda z: jnp.concatenate([z, z], axis=-2)
    lw2 = two(lw)
    mm = lambda p_, q_, form: _mm(p_, q_, form, WKV_PASSES)
    cl = _mm(incl.astype(F32), lw2, "nn", "L3")
    p = jnp.exp(cl)
    pinv = jnp.exp(-cl)
    pprev = jnp.exp(cl - lw2)
    kk2 = two(kk)
    at = jnp.where(sel, -kk2 * pprev, 0.0)
    bt = jnp.where(sel, kk2 * two(a) * pinv, 0.0)
    kt = jnp.where(sel, two(k) * pinv, 0.0)
    rt = jnp.where(sel, two(r) * p, 0.0)
    vt = jnp.where(sel, two(v), 0.0)
    cat = jnp.concatenate
    bk = cat([bt, kt], axis=-2)
    arbk = mm(cat([at, rt], axis=-2), bk, "nt")
    ab, ak = jnp.where(strict, arbk[..., :n, :n], 0.0), jnp.where(strict, arbk[..., :n, n:], 0.0)
    rb, rk = jnp.where(incl, arbk[..., n:, :n], 0.0), jnp.where(incl, arbk[..., n:, n:], 0.0)
    s0t = _transposed(s0)
    u = _solve_unit_lower(ab, mm(cat([at, ak], axis=-1), cat([s0t, vt], axis=-2), "nn"))
    y2 = mm(cat([rt, rb, rk], axis=-1), cat([s0t, u, vt], axis=-2), "nn")
    plast = jnp.exp(jnp.sum(lw, axis=-2, keepdims=True))
    s1 = (s0 + mm(cat([u, vt], axis=-2), bk, "tn")) * plast
    r2 = lax.broadcasted_iota(jnp.int32, (128, 128), 0) // HEAD_DIM
    c2 = lax.broadcasted_iota(jnp.int32, (128, 128), 1) // HEAD_DIM
    return y2[..., :c, :] + y2[..., c:, :], jnp.where(r2 == c2, s1, 0.0)


def _post_fn(y, r, k2, v, g, lnw, lnb, rk):
    seg = _seg_ones(RW)
    mean = _segsum(y, seg) * (1.0 / HEAD_DIM)
    yc = y - mean
    var = _segsum(yc * yc, seg) * (1.0 / HEAD_DIM)
    yn = yc * lax.rsqrt(var + GN_EPS)
    out = yn * lnw + lnb + _segsum(r * k2 * rk, seg) * v
    return out * g


def _attn_block_fn(q, kc, vc, kp=None, vp=None):
    n = ATTN_BLOCK
    qi = lax.broadcasted_iota(jnp.int32, (n, n), 0)
    kj = lax.broadcasted_iota(jnp.int32, (n, n), 1)
    lane = lax.broadcasted_iota(jnp.int32, (1, 128), 1)
    scale = HEAD_DIM ** -0.5
    valid = kj <= qi
    keys, vals = kc, vc
    if kp is not None:
        valid = jnp.concatenate([valid, kj >= qi], axis=-1)
        keys, vals = jnp.concatenate([kc, kp], axis=-2), jnp.concatenate([vc, vp], axis=-2)
    m0 = (lane // HEAD_DIM) == 0
    q2 = jnp.concatenate([jnp.where(m0, q, 0.0), jnp.where(m0, 0.0, q)], axis=-2)
    valid2 = jnp.concatenate([valid, valid], axis=-2)
    s = jnp.where(valid2, _mm(q2, keys, "nt", ATTN_PASSES) * scale, NEG)
    m = jnp.max(s, axis=-1, keepdims=True)
    p = jnp.exp(s - m)
    den = jnp.sum(p, axis=-1, keepdims=True)
    o2 = _mm(p, vals, "nn", ATTN_PASSES) / den
    l2 = m + jnp.log(den)
    return jnp.where(m0, o2[..., :n, :], o2[..., n:, :]), jnp.where(m0, l2[..., :n, :], l2[..., n:, :])


def _attn_block_bwd(q, kc, vc, kp, vp, o, lse, do, dl):
    n = ATTN_BLOCK
    cat = jnp.concatenate
    qi = lax.broadcasted_iota(jnp.int32, (n, n), 0)
    kj = lax.broadcasted_iota(jnp.int32, (n, n), 1)
    m0 = (lax.broadcasted_iota(jnp.int32, (1, 128), 1) // HEAD_DIM) == 0
    scale = HEAD_DIM ** -0.5
    valid = kj <= qi
    keys, vals = kc, vc
    if kp is not None:
        valid = cat([valid, kj >= qi], axis=-1)
        keys, vals = cat([kc, kp], axis=-2), cat([vc, vp], axis=-2)
    stack = lambda z: cat([jnp.where(m0, z, 0.0), jnp.where(m0, 0.0, z)], axis=-2)
    q2, do2 = stack(q), stack(do)
    lse2 = cat([jnp.max(jnp.where(m0, lse, NEG), axis=-1, keepdims=True),
                jnp.max(jnp.where(m0, NEG, lse), axis=-1, keepdims=True)], axis=-2)
    delta = jnp.sum(do2 * cat([o, o], axis=-2), axis=-1, keepdims=True)
    dlse = jnp.sum(stack(dl), axis=-1, keepdims=True)
    mm = lambda a, b, form: _mm_raw(a, b, form, ATTN_PASSES)
    s = jnp.where(cat([valid, valid], axis=-2), mm(q2, keys, "nt") * scale, NEG)
    p = jnp.exp(s - lse2)
    ds = p * (mm(do2, vals, "nt") - delta + dlse)
    dq2 = mm(ds, keys, "nn") * scale
    dq = jnp.where(m0, dq2[..., :n, :], dq2[..., n:, :])
    dkeys = mm(ds, q2, "tn") * scale
    dvals = mm(p, do2, "tn")
    if kp is None:
        return dq, dkeys, dvals
    return dq, dkeys[..., :n, :], dvals[..., :n, :], dkeys[..., n:, :], dvals[..., n:, :]


def _combine_fn(o1, o2, o3, l1, l2, l3, og):
    seg = _seg_ones(o1.shape[-1])
    m = jnp.maximum(jnp.maximum(l1, l2), l3)
    e1, e2, e3 = jnp.exp(l1 - m), jnp.exp(l2 - m), jnp.exp(l3 - m)
    o = (e1 * o1 + e2 * o2 + e3 * o3) / (e1 + e2 + e3)
    o = o * lax.rsqrt(_segsum(o * o, seg) * (1.0 / HEAD_DIM) + NORM_EPS)
    return o * og


def _shifted(p, last8, first):
    prow = jnp.where(first, 0.0, last8[7:8, :])
    rolled = pltpu.roll(p, 1, axis=0)
    rid = lax.broadcasted_iota(jnp.int32, p.shape, 0)
    return jnp.where(rid == 0, prow, rolled)


_PREP_TM = 256


def _prep_specs(tm):
    vec = lambda n: pl.BlockSpec((1, n), lambda i: (0, 0))
    mat = lambda r, n: pl.BlockSpec((r, n), lambda i: (0, 0))
    return [vec(SHIFT_COLS), vec(RW), mat(128, RW), vec(RW), mat(128, RW), mat(128, RW), vec(RW), vec(RW)]


def _in_proj_prep(x, g1, win, pw):
    t = x.shape[0]
    tm = _PREP_TM

    def body(x_ref, g_ref, w_ref, mu, w0, w2p, a0, a2p, g2, k_k, k_a, h_ref, pa_ref, qkv_ref, *rest):
        outs, carry = rest[:7], rest[7]

        @pl.when(pl.program_id(0) == 0)
        def _():
            carry[...] = jnp.zeros_like(carry)

        h = _rms_fwd(x_ref[...], g_ref[...]).astype(BF16)
        h_ref[...] = h
        proj = _dot_nt(h, w_ref[...])
        p = proj[:, :SHIFT_COLS]
        pa_ref[...] = p
        for j in range(3):
            for pr in range(N_PAIR):
                c0 = SHIFT_COLS + j * RW + pr * 128
                qkv_ref[j, pr] = proj[:, c0:c0 + 128]
        pprev = _shifted(p, carry[...], pl.program_id(0) == 0)
        carry[...] = p[tm - 8:, :]
        res = _prep_fn(p, pprev, mu[...], w0[...], w2p[...], a0[...], a2p[...], g2[...], k_k[...], k_a[...])
        for o_ref, val in zip(outs, res):
            o_ref[...] = val

    row = pl.BlockSpec((tm, RW), lambda i: (i, 0))
    return pl.pallas_call(
        body, name="in_proj_prep", grid=(t // tm,),
        in_specs=[pl.BlockSpec((tm, D_MODEL), lambda i: (i, 0)), pl.BlockSpec((1, D_MODEL), lambda i: (0, 0)),
                  pl.BlockSpec((IN_COLS, D_MODEL), lambda i: (0, 0))] + _prep_specs(tm),
        out_specs=[pl.BlockSpec((tm, D_MODEL), lambda i: (i, 0)), pl.BlockSpec((tm, SHIFT_COLS), lambda i: (i, 0)),
                   pl.BlockSpec((3, N_PAIR, tm, 128), lambda i: (0, 0, i, 0))] + [row] * 7,
        out_shape=[jax.ShapeDtypeStruct((t, D_MODEL), BF16), jax.ShapeDtypeStruct((t, SHIFT_COLS), F32),
                   jax.ShapeDtypeStruct((3, N_PAIR, t, 128), F32)] + [jax.ShapeDtypeStruct((t, RW), F32)] * 7,
        scratch_shapes=[pltpu.VMEM((8, SHIFT_COLS), F32)],
        compiler_params=_params(("arbitrary",)),
    )(x, g1, win, *pw)


def _pairs(ref):
    return jnp.stack([ref[:, 128 * p:128 * (p + 1)] for p in range(N_PAIR)], axis=0)


def _wkv_fwd(r, lw, k2, v, kk, a):
    t = r.shape[0]
    nc = t // CHUNK

    def body(r_ref, lw_ref, k_ref, v_ref, kk_ref, a_ref, y_ref, s_ref, st):
        @pl.when(pl.program_id(0) == 0)
        def _():
            st[...] = jnp.zeros_like(st)

        s0 = st[...]
        s_ref[0] = s0
        y, s1 = _wkv_chunk_fn(s0, *[_pairs(ref) for ref in (r_ref, lw_ref, k_ref, v_ref, kk_ref, a_ref)])
        for p in range(N_PAIR):
            y_ref[:, 128 * p:128 * (p + 1)] = y[p]
        st[...] = s1

    blk = pl.BlockSpec((CHUNK, RW), lambda c: (c, 0))
    return pl.pallas_call(
        body, name="wkv_fwd", grid=(nc,),
        in_specs=[blk] * 6,
        out_specs=[blk, pl.BlockSpec((1, N_PAIR, 128, 128), lambda c: (c, 0, 0, 0))],
        out_shape=[jax.ShapeDtypeStruct((t, RW), F32), jax.ShapeDtypeStruct((nc, N_PAIR, 128, 128), F32)],
        scratch_shapes=[pltpu.VMEM((N_PAIR, 128, 128), F32)],
        compiler_params=_params(("arbitrary",)),
    )(r, lw, k2, v, kk, a)


_POST_TM = 512


ATTN_GROUP = 2


def _dilated_rows(d, r, n):
    if d == 1:
        return pl.ds(pl.multiple_of(n * ATTN_BLOCK, ATTN_BLOCK), ATTN_BLOCK)
    return pl.ds(r + n * (ATTN_BLOCK * d), ATTN_BLOCK, stride=d)


def _for_each_sequence(t, unit):
    for di, d in enumerate(DILATIONS):

        @pl.when(pl.program_id(1) == di)
        def _(di=di, d=d):
            nb = t // (ATTN_BLOCK * d)
            if d == 1:
                unit(di, [(d, 0, 0)], False)
                unit(di, [(d, 0, 1)], True)
                lax.fori_loop(1, nb // 2, lambda k, c: (unit(di, [(d, 0, 2 * k), (d, 0, 2 * k + 1)], True), c)[1], 0)
            else:

                def residues(r, carry):
                    unit(di, [(d, r, 0), (d, r + d // 2, 0)], False)
                    if nb > 1:
                        lax.fori_loop(1, nb, lambda n, c: (unit(di, [(d, r, n), (d, r + d // 2, n)], True), c)[1], 0)
                    return carry

                lax.fori_loop(0, d // 2, residues, 0)


def _take(ref, lead, rows_list):
    return jnp.stack([ref.at[(*lead, g)][rows, :] for rows in rows_list for g in range(ref.shape[len(lead)])], axis=0)


def _put(ref, lead, rows_list, val, add=False):
    k = 0
    for rows in rows_list:
        for g in range(ref.shape[len(lead)]):
            if add:
                ref.at[(*lead, g)][rows, :] += val[k]
            else:
                ref.at[(*lead, g)][rows, :] = val[k]
            k += 1


def _attn_fwd(qkv):
    t = qkv.shape[2]

    def body(q_ref, k_ref, v_ref, o_ref, l_ref):
        def unit(di, places, has_prev):
            cur = [_dilated_rows(d, r, n) for d, r, n in places]
            args = [_take(ref, (0,), cur) for ref in (q_ref, k_ref, v_ref)]
            if has_prev:
                prv = [_dilated_rows(d, r, n - 1) for d, r, n in places]
                args += [_take(ref, (0,), prv) for ref in (k_ref, v_ref)]
            o, lse = _attn_block_fn(*args)
            _put(o_ref, (0,), cur, o)
            _put(l_ref, (0,), cur, lse)

        _for_each_sequence(t, unit)

    spec = lambda j: pl.BlockSpec((1, ATTN_GROUP, t, 128), lambda i, b: (j, i, 0, 0))
    out = pl.BlockSpec((1, ATTN_GROUP, t, 128), lambda i, b: (b, i, 0, 0))
    return pl.pallas_call(
        body, name="attn_fwd", grid=(N_PAIR // ATTN_GROUP, len(DILATIONS)),
        in_specs=[spec(0), spec(1), spec(2)], out_specs=[out, out],
        out_shape=[jax.ShapeDtypeStruct((3, N_PAIR, t, 128), F32)] * 2,
        compiler_params=_params(("parallel", "arbitrary")),
    )(qkv, qkv, qkv)


_COMB_TM = 512


def _mixers_out(y, r, k2, v, g, lnw, lnb, rk, o, l, og):
    t = y.shape[0]
    tm = _COMB_TM

    def body(y_ref, r_ref, k_ref, v_ref, g_ref, lnw_ref, lnb_ref, rk_ref, o_ref, l_ref, og_ref, out_ref):
        out_ref[:, :RW] = _post_fn(y_ref[...], r_ref[...], k_ref[...], v_ref[...], g_ref[...],
                                   lnw_ref[...], lnb_ref[...], rk_ref[...]).astype(BF16)
        for p in range(N_PAIR):
            cols = slice(128 * p, 128 * (p + 1))
            out_ref[:, RW + 128 * p:RW + 128 * (p + 1)] = _combine_fn(
                o_ref[0, p], o_ref[1, p], o_ref[2, p], l_ref[0, p], l_ref[1, p], l_ref[2, p], og_ref[:, cols]).astype(BF16)

    row = pl.BlockSpec((tm, RW), lambda i: (i, 0))
    vec = pl.BlockSpec((1, RW), lambda i: (0, 0))
    blk = pl.BlockSpec((3, N_PAIR, tm, 128), lambda i: (0, 0, i, 0))
    return pl.pallas_call(
        body, name="mixers_out", grid=(t // tm,),
        in_specs=[row] * 5 + [vec] * 3 + [blk, blk, vec], out_specs=pl.BlockSpec((tm, D_MODEL), lambda i: (i, 0)),
        out_shape=jax.ShapeDtypeStruct((t, D_MODEL), BF16),
        compiler_params=_params(("parallel",)),
    )(y, r, k2, v, g, lnw, lnb, rk, o, l, og)


def _ffn_all(x, ycat, wg, wu, wd, wout, g2, gf, tgt):
    t = x.shape[0]
    tm = 256

    def body(x_ref, y_ref, wg_ref, wu_ref, wd_ref, wo_ref, g2_ref, gf_ref, t_ref,
             h_ref, act_ref, dx2b_ref, dgt_ref, dup_ref, dx1b_ref, dx1_ref, dya_ref, dyb_ref, loss_ref, dgf_ref, dg2_ref,
             gt_s, up_s):
        first = pl.program_id(0) == 0
        x1 = x_ref[...] + _dot(y_ref[...], wo_ref[...])
        h = _rms_fwd(x1, g2_ref[...]).astype(BF16)
        h_ref[...] = h
        for c0 in range(0, D_FF, FF_CHUNK):
            cols = slice(c0, c0 + FF_CHUNK)
            gt = _dot_nt(h, wg_ref[cols, :])
            up = _dot_nt(h, wu_ref[cols, :])
            gt_s[:, cols] = gt.astype(BF16)
            up_s[:, cols] = up.astype(BF16)
            act_ref[:, cols] = (gt * _sigmoid(gt) * up).astype(BF16)
        x2 = x1 + _dot(act_ref[...], wd_ref[...])
        gf_ = gf_ref[...]
        diff = _rms_fwd(x2, gf_) - t_ref[...]
        lrow = 0.5 * jnp.sum(_colsum8(diff * diff), axis=1, keepdims=True) * (1.0 / D_MODEL)
        _acc(loss_ref, jnp.broadcast_to(lrow, (8, 128)), first)
        dx2, dgr = _rms_bwd(diff * (1.0 / D_MODEL), x2, gf_)
        _acc(dgf_ref, _colsum8(dgr), first)
        dx2b = dx2.astype(BF16)
        dx2b_ref[...] = dx2b
        for c0 in range(0, D_FF, FF_CHUNK):
            cols = slice(c0, c0 + FF_CHUNK)
            dact = _dot_nt(dx2b, wd_ref[cols, :])
            gt = gt_s[:, cols].astype(F32)
            sg = _sigmoid(gt)
            dgt_ref[:, cols] = (dact * up_s[:, cols].astype(F32) * sg * (1.0 + gt * (1.0 - sg))).astype(BF16)
            dup_ref[:, cols] = (dact * gt * sg).astype(BF16)
        dh = _dot(dgt_ref[...], wg_ref[...]) + _dot(dup_ref[...], wu_ref[...])
        dxn, dgr2 = _rms_bwd(dh, x1, g2_ref[...])
        _acc(dg2_ref, _colsum8(dgr2), first)
        dx1 = dx2 + dxn
        dx1_ref[...] = dx1
        dx1b = dx1.astype(BF16)
        dx1b_ref[...] = dx1b
        dy = _dot_nt(dx1b, wo_ref[...])
        dya_ref[...] = dy[:, :RW]
        dyb_ref[...] = dy[:, RW:]

    row = pl.BlockSpec((tm, D_MODEL), lambda i: (i, 0))
    wide = pl.BlockSpec((tm, D_FF), lambda i: (i, 0))
    half = pl.BlockSpec((tm, RW), lambda i: (i, 0))
    wsp = pl.BlockSpec((D_FF, D_MODEL), lambda i: (0, 0))
    vec = pl.BlockSpec((1, D_MODEL), lambda i: (0, 0))
    part = pl.BlockSpec((8, D_MODEL), lambda i: (0, 0))
    bf = lambda n: jax.ShapeDtypeStruct((t, n), BF16)
    return pl.pallas_call(
        body, name="ffn_all", grid=(t // tm,),
        in_specs=[row, row, wsp, wsp, wsp, pl.BlockSpec((D_MODEL, D_MODEL), lambda i: (0, 0)), vec, vec, row],
        out_specs=[row, wide, row, wide, wide, row, row, half, half, pl.BlockSpec((8, 128), lambda i: (0, 0)), part, part],
        out_shape=[bf(D_MODEL), bf(D_FF), bf(D_MODEL), bf(D_FF), bf(D_FF), bf(D_MODEL),
                   jax.ShapeDtypeStruct((t, D_MODEL), F32), jax.ShapeDtypeStruct((t, RW), F32),
                   jax.ShapeDtypeStruct((t, RW), F32), jax.ShapeDtypeStruct((8, 128), F32),
                   jax.ShapeDtypeStruct((8, D_MODEL), F32), jax.ShapeDtypeStruct((8, D_MODEL), F32)],
        scratch_shapes=[pltpu.VMEM((tm, D_FF), BF16), pltpu.VMEM((tm, D_FF), BF16)],
        compiler_params=_params(("arbitrary",)),
    )(x, ycat, wg, wu, wd, wout, g2, gf, tgt)


def _wgrad(a, b, tk, tn, name):
    t, kdim = a.shape
    ndim = b.shape[1]

    def body(a_ref, b_ref, o_ref):
        o_ref[...] = _dot_tn(a_ref[...], b_ref[...]).astype(BF16)

    return pl.pallas_call(
        body, name=name, grid=(kdim // tk, ndim // tn),
        in_specs=[pl.BlockSpec((t, tk), lambda i, j: (0, i)), pl.BlockSpec((t, tn), lambda i, j: (0, j))],
        out_specs=pl.BlockSpec((tk, tn), lambda i, j: (i, j)),
        out_shape=jax.ShapeDtypeStruct((kdim, ndim), BF16),
        compiler_params=_params(("parallel", "parallel")),
    )(a, b)


def _post_bwd(dya, y, r, k2, v, g, lnw, lnb, rk, after):
    t = y.shape[0]
    tm = _POST_TM

    def body(d_ref, y_ref, r_ref, k_ref, v_ref, g_ref, lnw_ref, lnb_ref, rk_ref, _,
             dy_ref, dr_ref, dk_ref, dv_ref, dg_ref, dlnw_ref, dlnb_ref, drk_ref):
        first = pl.program_id(0) == 0
        ones = jnp.ones((tm, 1), F32)
        prim = (y_ref[...], r_ref[...], k_ref[...], v_ref[...], g_ref[...],
                ones * lnw_ref[...], ones * lnb_ref[...], ones * rk_ref[...])
        _, vjp = jax.vjp(_post_fn, *prim)
        dy, dr, dk, dv, dg, dlnw, dlnb, drk = vjp(d_ref[...])
        dy_ref[...] = dy
        dr_ref[...] = dr
        dk_ref[...] = dk
        dv_ref[...] = dv
        dg_ref[...] = dg
        _acc(dlnw_ref, _colsum8(dlnw), first)
        _acc(dlnb_ref, _colsum8(dlnb), first)
        _acc(drk_ref, _colsum8(drk), first)

    row = pl.BlockSpec((tm, RW), lambda i: (i, 0))
    vec = pl.BlockSpec((1, RW), lambda i: (0, 0))
    part = pl.BlockSpec((8, RW), lambda i: (0, 0))
    return pl.pallas_call(
        body, name="rwkv_post_bwd", grid=(t // tm,),
        in_specs=[row] * 6 + [vec] * 3 + [ANY], out_specs=[row] * 5 + [part] * 3,
        out_shape=[jax.ShapeDtypeStruct((t, RW), F32)] * 5 + [jax.ShapeDtypeStruct((8, RW), F32)] * 3,
        compiler_params=_params(("arbitrary",)),
    )(dya, y, r, k2, v, g, lnw, lnb, rk, after)


def _wkv_bwd(dy, s0s, r, lw, k2, v, kk, a):
    t = r.shape[0]
    nc = t // CHUNK

    def body(dy_ref, s_ref, r_ref, lw_ref, k_ref, v_ref, kk_ref, a_ref,
             dr_ref, dlw_ref, dk_ref, dv_ref, dkk_ref, da_ref, ds):
        @pl.when(pl.program_id(0) == 0)
        def _():
            ds[...] = jnp.zeros_like(ds)

        _, vjp = jax.vjp(_wkv_chunk_fn, s_ref[0],
                         *[_pairs(ref) for ref in (r_ref, lw_ref, k_ref, v_ref, kk_ref, a_ref)])
        res = vjp((_pairs(dy_ref), ds[...]))
        ds[...] = res[0]
        for ref, val in zip((dr_ref, dlw_ref, dk_ref, dv_ref, dkk_ref, da_ref), res[1:]):
            for p in range(N_PAIR):
                ref[:, 128 * p:128 * (p + 1)] = val[p]

    blk = pl.BlockSpec((CHUNK, RW), lambda c: (nc - 1 - c, 0))
    return pl.pallas_call(
        body, name="wkv_bwd", grid=(nc,),
        in_specs=[blk, pl.BlockSpec((1, N_PAIR, 128, 128), lambda c: (nc - 1 - c, 0, 0, 0))] + [blk] * 6,
        out_specs=[blk] * 6,
        out_shape=[jax.ShapeDtypeStruct((t, RW), F32)] * 6,
        scratch_shapes=[pltpu.VMEM((N_PAIR, 128, 128), F32)],
        compiler_params=_params(("arbitrary",)),
    )(dy, s0s, r, lw, k2, v, kk, a)


def _prep_in_proj_bwd(proj, pw, douts, dq, dk, dv, win, x, g1, dx1):
    t = proj.shape[0]
    tm = _PREP_TM
    nt = t // tm

    def body(p_ref, l8_ref, mu, w0, w2p, a0, a2p, g2, k_k, k_a, dr, dr2, dlw, dk2, dk22, dv, dv2, dkk, da, dg,
             dq_ref, dkq_ref, dvq_ref, w_ref, x_ref, g1_ref, dx1_ref,
             dproj_ref, dx_ref, dg1_ref, dmu_ref, dw0_ref, dw2_ref, da0_ref, da2_ref, dg2_ref, dkk_ref, dka_ref, carry):
        i = pl.program_id(0)
        first = i == 0

        @pl.when(first)
        def _():
            carry[...] = jnp.zeros_like(carry)

        p = p_ref[...]
        pprev = _shifted(p, l8_ref[...], i == nt - 1)
        ones = jnp.ones((tm, 1), F32)
        prim = (p, pprev, ones * mu[...], ones * w0[...], w2p[...], ones * a0[...], a2p[...], g2[...],
                ones * k_k[...], ones * k_a[...])
        _, vjp = jax.vjp(_prep_fn, *prim)
        dp, dpp, dmu, dw0, dw2, da0, da2, dg2, dkk_, dka = vjp(
            (dr[...] + dr2[...], dlw[...], dk2[...] + dk22[...], dv[...] + dv2[...], dkk[...], da[...], dg[...]))
        up = pltpu.roll(dpp, tm - 1, axis=0)
        rid = lax.broadcasted_iota(jnp.int32, dpp.shape, 0)
        dpa = dp + jnp.where(rid == tm - 1, carry[0:1, :], up)
        carry[...] = jnp.broadcast_to(dpp[0:1, :], carry.shape)
        _acc(dmu_ref, _colsum8(dmu), first)
        _acc(dw0_ref, _colsum8(dw0), first)
        _acc(dw2_ref, dw2, first)
        _acc(da0_ref, _colsum8(da0), first)
        _acc(da2_ref, da2, first)
        _acc(dg2_ref, dg2, first)
        _acc(dkk_ref, _colsum8(dkk_), first)
        _acc(dka_ref, _colsum8(dka), first)
        parts = [dpa] + [ref[pr] for ref in (dq_ref, dkq_ref, dvq_ref) for pr in range(N_PAIR)]
        dproj = jnp.concatenate([z.astype(BF16) for z in parts], axis=1)
        dproj_ref[...] = dproj
        dxn, dgr = _rms_bwd(_dot(dproj, w_ref[...]), x_ref[...], g1_ref[...])
        dx_ref[...] = dx1_ref[...] + dxn
        _acc(dg1_ref, _colsum8(dgr), first)

    rev = lambda i: (nt - 1 - i, 0)
    row = pl.BlockSpec((tm, RW), rev)
    wide = pl.BlockSpec((tm, D_MODEL), rev)
    pair = pl.BlockSpec((N_PAIR, tm, 128), lambda i: (0, nt - 1 - i, 0))
    part = lambda n: pl.BlockSpec((8, n), lambda i: (0, 0))
    mat = pl.BlockSpec((128, RW), lambda i: (0, 0))
    return pl.pallas_call(
        body, name="prep_in_proj_bwd", grid=(nt,),
        in_specs=[pl.BlockSpec((tm, SHIFT_COLS), rev),
                  pl.BlockSpec((8, SHIFT_COLS), lambda i: (jnp.maximum((nt - 1 - i) * (tm // 8) - 1, 0), 0))]
                 + _prep_specs(tm) + [row] * 10
                 + [pair] * 3 + [pl.BlockSpec((IN_COLS, D_MODEL), lambda i: (0, 0)), wide,
                                 pl.BlockSpec((1, D_MODEL), lambda i: (0, 0)), wide],
        out_specs=[pl.BlockSpec((tm, IN_COLS), rev), wide, part(D_MODEL), part(SHIFT_COLS), part(RW), mat, part(RW), mat,
                   mat, part(RW), part(RW)],
        out_shape=[jax.ShapeDtypeStruct((t, IN_COLS), BF16), jax.ShapeDtypeStruct((t, D_MODEL), F32),
                   jax.ShapeDtypeStruct((8, D_MODEL), F32), jax.ShapeDtypeStruct((8, SHIFT_COLS), F32),
                   jax.ShapeDtypeStruct((8, RW), F32), jax.ShapeDtypeStruct((128, RW), F32),
                   jax.ShapeDtypeStruct((8, RW), F32), jax.ShapeDtypeStruct((128, RW), F32),
                   jax.ShapeDtypeStruct((128, RW), F32), jax.ShapeDtypeStruct((8, RW), F32),
                   jax.ShapeDtypeStruct((8, RW), F32)],
        scratch_shapes=[pltpu.VMEM((8, SHIFT_COLS), F32)],
        compiler_params=_params(("arbitrary",)),
    )(proj, proj, *pw, *douts, dq, dk, dv, win, x, g1, dx1)


def _combine_bwd(dyb, o, l, og):
    t = dyb.shape[0]
    tm = _COMB_TM

    def body(d_ref, o_ref, l_ref, og_ref, do_ref, dl_ref, dog_ref):
        ones = jnp.ones((tm, 1), F32)
        dog = []
        for p in range(N_PAIR):
            cols = slice(128 * p, 128 * (p + 1))
            _, vjp = jax.vjp(_combine_fn, o_ref[0, p], o_ref[1, p], o_ref[2, p], l_ref[0, p], l_ref[1, p], l_ref[2, p],
                             ones * og_ref[:, cols])
            res = vjp(d_ref[:, cols])
            for b in range(3):
                do_ref[b, p] = res[b]
                dl_ref[b, p] = res[3 + b]
            dog.append(_colsum8(res[6]))
        _acc(dog_ref, jnp.concatenate(dog, axis=1), pl.program_id(0) == 0)

    blk = pl.BlockSpec((3, N_PAIR, tm, 128), lambda i: (0, 0, i, 0))
    return pl.pallas_call(
        body, name="attn_combine_bwd", grid=(t // tm,),
        in_specs=[pl.BlockSpec((tm, RW), lambda i: (i, 0)), blk, blk, pl.BlockSpec((1, RW), lambda i: (0, 0))],
        out_specs=[blk, blk, pl.BlockSpec((8, RW), lambda i: (0, 0))],
        out_shape=[jax.ShapeDtypeStruct((3, N_PAIR, t, 128), F32)] * 2 + [jax.ShapeDtypeStruct((8, RW), F32)],
        compiler_params=_params(("arbitrary",)),
    )(dyb, o, l, og)


def _attn_bwd(do, dl, o, lse, qkv):
    t = qkv.shape[2]

    def body(do_ref, dl_ref, o_ref, l_ref, q_ref, k_ref, v_ref, dq_ref, dk_ref, dv_ref):
        @pl.when(pl.program_id(1) == 0)
        def _():
            for ref in (dq_ref, dk_ref, dv_ref):
                ref[...] = jnp.zeros_like(ref)

        def unit(di, places, has_prev):
            cur = [_dilated_rows(d, r, n) for d, r, n in places]
            q, kc, vc = [_take(ref, (0,), cur) for ref in (q_ref, k_ref, v_ref)]
            kp = vp = None
            if has_prev:
                prv = [_dilated_rows(d, r, n - 1) for d, r, n in places]
                kp, vp = [_take(ref, (0,), prv) for ref in (k_ref, v_ref)]
            res = _attn_block_bwd(q, kc, vc, kp, vp, *[_take(ref, (0,), cur) for ref in (o_ref, l_ref, do_ref, dl_ref)])
            _put(dq_ref, (), cur, res[0], add=True)
            _put(dk_ref, (), cur, res[1], add=True)
            _put(dv_ref, (), cur, res[2], add=True)
            if has_prev:
                _put(dk_ref, (), prv, res[3], add=True)
                _put(dv_ref, (), prv, res[4], add=True)

        _for_each_sequence(t, unit)

    spec = lambda j: pl.BlockSpec((1, ATTN_GROUP, t, 128), lambda i, b: (j, i, 0, 0))
    branch = pl.BlockSpec((1, ATTN_GROUP, t, 128), lambda i, b: (b, i, 0, 0))
    out = pl.BlockSpec((ATTN_GROUP, t, 128), lambda i, b: (i, 0, 0))
    return pl.pallas_call(
        body, name="attn_bwd", grid=(N_PAIR // ATTN_GROUP, len(DILATIONS)),
        in_specs=[branch] * 4 + [spec(0), spec(1), spec(2)], out_specs=[out] * 3,
        out_shape=[jax.ShapeDtypeStruct((N_PAIR, t, 128), F32)] * 3,
        compiler_params=_params(("parallel", "arbitrary")),
    )(do, dl, o, lse, qkv, qkv, qkv)


def _pad_lora(w, lo):
    z = jnp.zeros((64, RW), F32)
    return jnp.concatenate([w, z], axis=0) if lo == 0 else jnp.concatenate([z, w], axis=0)


def _local_step(x, tgt, win, vecs, w2, a2, g2m, get_rest, send_rest):
    pw = (vecs["mu_shift"], vecs["decay_w0"], _pad_lora(w2, 0), vecs["iclr_a0"], _pad_lora(a2, 64), g2m,
          vecs["k_k"], vecs["k_a"])
    h, proj, qkv, r, lw, k2, v, kk, a, g = _in_proj_prep(x, vecs["mix_norm_g"], win, pw)
    y, s0s = _wkv_fwd(r, lw, k2, v, kk, a)
    o_att, l_att = _attn_fwd(qkv)
    ycat = _mixers_out(y, r, k2, v, g, vecs["ln_x_w"], vecs["ln_x_b"], vecs["r_k"], o_att, l_att, vecs["attn_out_g"])
    wout, wg, wu, wd = get_rest(ycat)
    h2, act, dx2b, dgt, dup, dx1b, dx1, dya, dyb, loss8, dgf, dg2n = _ffn_all(
        x, ycat, wg, wu, wd, wout, vecs["ffn_norm_g"], vecs["final_norm_g"], tgt)
    gw = {
        "w_down": _wgrad(act, dx2b, 1408, 1024, "wgrad_down"),
        "w_gate": _wgrad(dgt, h2, 1408, 1024, "wgrad_gate"),
        "w_up": _wgrad(dup, h2, 1408, 1024, "wgrad_up"),
        "w_out": _wgrad(ycat, dx1b, 1024, 1024, "wgrad_out"),
    }

    dy, dr_p, dk2_p, dv_p, dg, dlnw, dlnb, drk = _post_bwd(dya, y, r, k2, v, g, vecs["ln_x_w"], vecs["ln_x_b"], vecs["r_k"],
                                                           after=send_rest(gw))
    dr_s, dlw, dk2_s, dv_s, dkk, da = _wkv_bwd(dy, s0s, r, lw, k2, v, kk, a)
    do_att, dl_att, dog = _combine_bwd(dyb, o_att, l_att, vecs["attn_out_g"])
    dq, dk, dv = _attn_bwd(do_att, dl_att, o_att, l_att, qkv)
    dproj, dx, dg1, dmu, dw0, dw2p, da0, da2p, dg2m, dk_k, dk_a = _prep_in_proj_bwd(
        proj, pw, (dr_p, dr_s, dlw, dk2_p, dk2_s, dv_p, dv_s, dkk, da, dg), dq, dk, dv, win, x, vecs["mix_norm_g"], dx1)
    gw["w_in"] = _wgrad(dproj, h, 1664, 1024, "wgrad_in")
    gw["decay_w2"] = dw2p[:64]
    gw["iclr_a2"] = da2p[64:]
    gw["gate_g2"] = dg2m
    gv = {"mix_norm_g": dg1, "mu_shift": dmu, "decay_w0": dw0, "iclr_a0": da0, "k_k": dk_k, "k_a": dk_a, "r_k": drk,
          "ln_x_w": dlnw, "ln_x_b": dlnb, "attn_out_g": dog, "ffn_norm_g": dg2n, "final_norm_g": dgf}
    return loss8, dx, gw, gv


N_CHIP = 4
N_DEV = 8
MATS = ("w_in", "w_out", "w_gate", "w_up", "w_down")
LORAS = ("decay_w2", "iclr_a2", "gate_g2")
VECS = (("mix_norm_g", 1024), ("mu_shift", 1792), ("decay_w0", 512), ("iclr_a0", 512), ("k_k", 512), ("k_a", 512),
        ("r_k", 512), ("ln_x_w", 512), ("ln_x_b", 512), ("attn_out_g", 512), ("ffn_norm_g", 1024),
        ("final_norm_g", 1024))
N_VEC = sum(n for _, n in VECS)
N_SMALL = N_VEC + 128
ANY = pl.BlockSpec(memory_space=pl.ANY)


def _flip(v, f):
    return 1 - v if f else v


class _Me:
    def __init__(self, mode):
        x, y, c = lax.axis_index("x"), lax.axis_index("y"), lax.axis_index("c")
        self.core, self.chip, self.dev = c, 2 * x + y, 4 * x + 2 * y + c
        self.sibling = (x, y, 1 - c)
        if mode == "chips":
            self.peers = [(px, py, c) for px, py in ((1 - x, y), (x, 1 - y), (1 - x, 1 - y))]
        else:
            self.peers = [(_flip(x, k & 4), _flip(y, k & 2), _flip(c, k & 1)) for k in range(1, N_DEV)]


def _half(core, rows):
    h = rows // 2
    return pl.ds(pl.multiple_of(core * h, h), h)


_BY_CHIP = ("gather", "whole", "chipsum")


def _peer_copy(srcs, dsts, kinds, send_sems, recv_sems, me, j, i, incoming):
    px, py, pc = me.peers[j]
    pchip, pdev = 2 * px + py, 4 * px + 2 * py + pc
    src, dst, kind = srcs[i], dsts[i], kinds[i]
    if kind in ("gather", "whole"):
        rows = _half(me.core, src.shape[1]) if kind == "gather" else pl.ds(0, src.shape[1])
        src, dst = src.at[me.chip, rows], dst.at[pchip if incoming else me.chip, rows]
    elif kind == "scatter":
        src, dst = src.at[pchip, _half(pc, src.shape[1])], dst.at[pdev if incoming else me.dev]
    elif kind == "chipsum":
        src, dst = src.at[pchip], dst.at[pchip if incoming else me.chip]
    else:
        dst = dst.at[pdev if incoming else me.dev]
    n = len(srcs)
    return pltpu.make_async_remote_copy(src_ref=src, dst_ref=dst, send_sem=send_sems.at[n * j + i],
                                        recv_sem=recv_sems.at[n * j + i], device_id=(px, py, pc), device_id_type=MESH)


def _mode(kinds):
    return "chips" if kinds[0] in _BY_CHIP else "devs"


def _npeer(kinds):
    return N_CHIP - 1 if kinds[0] in _BY_CHIP else N_DEV - 1


def _sibling_halves(gs, name):
    n = len(gs)

    def body(*refs):
        srcs, dsts, send_sems, recv_sems = refs[:n], refs[n:2 * n], refs[2 * n], refs[2 * n + 1]
        me = _Me("chips")

        def copy(i, p):
            return pltpu.make_async_remote_copy(
                src_ref=srcs[i].at[p, _half(1 - me.core, srcs[i].shape[1])], dst_ref=dsts[i].at[p],
                send_sem=send_sems.at[N_CHIP * i + p], recv_sem=recv_sems.at[N_CHIP * i + p],
                device_id=me.sibling, device_id_type=MESH)

        copies = [copy(i, p) for i in range(n) for p in range(N_CHIP)]
        for cp in copies:
            cp.start()
        for cp in copies:
            cp.wait()

    return pl.pallas_call(
        body, name=name, in_specs=[ANY] * n, out_specs=[ANY] * n,
        out_shape=[jax.ShapeDtypeStruct((N_CHIP, g.shape[1] // 2, g.shape[2]), g.dtype) for g in gs],
        scratch_shapes=[pltpu.SemaphoreType.DMA((N_CHIP * n,)), pltpu.SemaphoreType.DMA((N_CHIP * n,))],
    )(*gs)


def _add_halves(g, other, core, tr, name):
    _, h, cols = other.shape

    def body(core_ref, g_ref, o_ref, out_ref):
        out_ref[...] = (g_ref[...].astype(F32) + o_ref[...].astype(F32)).astype(BF16)

    blk = lambda off: pl.BlockSpec((1, tr, cols), lambda p, i, core_ref: (p, core_ref[0] * (h // tr) * off + i, 0))
    return pl.pallas_call(
        body, name=name,
        grid_spec=pltpu.PrefetchScalarGridSpec(num_scalar_prefetch=1, grid=(N_CHIP, h // tr),
                                               in_specs=[blk(1), blk(0)], out_specs=blk(0)),
        out_shape=jax.ShapeDtypeStruct(other.shape, BF16),
        compiler_params=_params(("parallel", "parallel")),
    )(core, g, other)


def _swap_gathered(lands, name):
    n = len(lands)

    def body(*refs):
        dsts, send_sems, recv_sems = refs[n:2 * n], refs[2 * n], refs[2 * n + 1]
        me = _Me("chips")

        def copy(j, i, incoming):
            px, py, _ = me.peers[j]
            rows_out, rows_in = _half(me.core, dsts[i].shape[1]), _half(1 - me.core, dsts[i].shape[1])
            return pltpu.make_async_remote_copy(
                src_ref=dsts[i].at[2 * px + py, rows_out], dst_ref=dsts[i].at[2 * px + py, rows_in if incoming else rows_out],
                send_sem=send_sems.at[n * j + i], recv_sem=recv_sems.at[n * j + i], device_id=me.sibling, device_id_type=MESH)

        sends = [copy(j, i, False) for j in range(3) for i in range(n)]
        for cp in sends:
            cp.start()
        for j in range(3):
            for i in range(n):
                copy(j, i, True).wait_recv()
        for cp in sends:
            cp.wait_send()

    return pl.pallas_call(
        body, name=name, in_specs=[ANY] * n, out_specs=[ANY] * n,
        out_shape=[jax.ShapeDtypeStruct(l.shape, l.dtype) for l in lands],
        input_output_aliases={i: i for i in range(n)},
        scratch_shapes=[pltpu.SemaphoreType.DMA((3 * n,)), pltpu.SemaphoreType.DMA((3 * n,))],
    )(*lands)


def _join_halves(sums, name):
    n = len(sums)

    def body(*refs):
        dsts, send_sems, recv_sems = refs[n:2 * n], refs[2 * n], refs[2 * n + 1]
        me = _Me("chips")

        def copy(i, incoming):
            mine, other = _half(me.core, dsts[i].shape[0]), _half(1 - me.core, dsts[i].shape[0])
            return pltpu.make_async_remote_copy(src_ref=dsts[i].at[mine], dst_ref=dsts[i].at[other if incoming else mine],
                                                send_sem=send_sems.at[i], recv_sem=recv_sems.at[i],
                                                device_id=me.sibling, device_id_type=MESH)

        sends = [copy(i, False) for i in range(n)]
        for cp in sends:
            cp.start()
        for i in range(n):
            copy(i, True).wait_recv()
        for cp in sends:
            cp.wait_send()

    return pl.pallas_call(
        body, name=name, in_specs=[ANY] * n, out_specs=[ANY] * n,
        out_shape=[jax.ShapeDtypeStruct(s.shape, s.dtype) for s in sums],
        input_output_aliases={i: i for i in range(n)},
        scratch_shapes=[pltpu.SemaphoreType.DMA((n,)), pltpu.SemaphoreType.DMA((n,))],
    )(*sums)


HBM = pl.BlockSpec(memory_space=pltpu.HBM)
SEM = pl.BlockSpec(memory_space=pltpu.SEMAPHORE)
EFFECT = pltpu.SideEffectType.DATAFLOW_SIDE_EFFECTING


def _swap_start(arrs, lands, kinds, name):
    n = len(lands)
    ops = list(lands) if arrs is None else [*arrs, *lands]
    k = len(ops)

    def body(*refs):
        srcs, dsts, send_sems, recv_sems, token = refs[:n], refs[k - n:k], refs[k], refs[k + 1], refs[-1]
        me = _Me(_mode(kinds))
        for j in range(len(me.peers)):
            for i in range(n):
                _peer_copy(srcs, dsts, kinds, send_sems, recv_sems, me, j, i, False).start()
        token[...] = jnp.zeros_like(token)

    ns = _npeer(kinds) * n
    outs = pl.pallas_call(
        body, name=name,
        out_shape=(pltpu.SemaphoreType.DMA((ns,)), pltpu.SemaphoreType.DMA((ns,)),
                   *[pltpu.HBM(a.shape, a.dtype) for a in ops], jax.ShapeDtypeStruct((8, 128), F32)),
        in_specs=[HBM] * k, out_specs=(SEM, SEM, *[HBM] * k, pl.BlockSpec(memory_space=pltpu.VMEM)),
        input_output_aliases={i: 2 + i for i in range(k)},
        compiler_params=pltpu.CompilerParams(has_side_effects=EFFECT),
    )(*[pltpu.with_memory_space_constraint(a, pltpu.HBM) for a in ops])
    return outs[0], outs[1], outs[2:2 + k - n], outs[2 + k - n:2 + k], outs[-1]


def _swap_wait(send_sems, recv_sems, srcs_thru, lands_thru, after, kinds, name):
    n = len(lands_thru)
    ops = [*srcs_thru, *lands_thru]
    k = len(ops)

    def body(*refs):
        srcs, dsts, s_sems, r_sems = refs[:n], refs[k - n:k], refs[k], refs[k + 1]
        me = _Me(_mode(kinds))
        for j in range(len(me.peers)):
            for i in range(n):
                cp = _peer_copy(srcs, dsts, kinds, s_sems, r_sems, me, j, i, True)
                cp.wait_send()
                cp.wait_recv()

    outs = pl.pallas_call(
        body, name=name,
        out_shape=tuple(pltpu.HBM(a.shape, a.dtype) for a in ops),
        in_specs=[HBM] * k + [SEM, SEM, ANY], out_specs=tuple([HBM] * k),
        input_output_aliases={i: i for i in range(k)},
        compiler_params=pltpu.CompilerParams(has_side_effects=EFFECT),
    )(*ops, send_sems, recv_sems, after)
    return outs[k - n:]


def _adamw(w, g, m, v):
    m = ADAM_B1 * m + (1.0 - ADAM_B1) * g
    v = ADAM_B2 * v + (1.0 - ADAM_B2) * (g * g)
    m_hat = m / (1.0 - ADAM_B1 ** ADAM_STEP)
    v_hat = v / (1.0 - ADAM_B2 ** ADAM_STEP)
    delta = -ADAM_LR * (m_hat / (jnp.sqrt(v_hat) + ADAM_EPS) + ADAM_WD * w)
    return delta, m, v


def _reduce8(rbuf, core, tr, name):
    slots, h, cols = rbuf.shape

    def body(core_ref, r_ref, g_ref):
        g = r_ref[0].astype(F32)
        for s in range(1, slots):
            g = g + r_ref[s].astype(F32)
        g_ref[...] = g

    return pl.pallas_call(
        body, name=name,
        grid_spec=pltpu.PrefetchScalarGridSpec(
            num_scalar_prefetch=1, grid=(h // tr,),
            in_specs=[pl.BlockSpec((slots, tr, cols), lambda i, core_ref: (0, i, 0))],
            out_specs=pl.BlockSpec((tr, cols), lambda i, core_ref: (core_ref[0] * (h // tr) + i, 0))),
        out_shape=jax.ShapeDtypeStruct((2 * h, cols), F32),
        compiler_params=_params(("parallel",)),
    )(core, rbuf)


def _adamw_call(g, w, m, v, tr, name):
    _, rows, cols = w.shape

    def body(g_in, w_ref, m_ref, v_ref, g_ref, d_ref, nm_ref, nv_ref):
        g = g_in[...]
        g_ref[0] = g
        d_ref[0], nm_ref[0], nv_ref[0] = _adamw(w_ref[0], g, m_ref[0], v_ref[0])

    row = pl.BlockSpec((1, tr, cols), lambda i: (0, i, 0))
    return pl.pallas_call(
        body, name=name, grid=(rows // tr,),
        in_specs=[pl.BlockSpec((tr, cols), lambda i: (i, 0)), row, row, row], out_specs=[row] * 4,
        out_shape=[jax.ShapeDtypeStruct(w.shape, F32)] * 4,
        compiler_params=_params(("parallel",)),
    )(g, w, m, v)


def _adamw_lora(g, ws3, ms3, vs3):
    def body(g_in, *refs):
        ins, outs = refs[:9], refs[9:]
        r0 = 0
        for i in range(3):
            rows = ins[i].shape[1]
            g = g_in[r0:r0 + rows, :]
            outs[i][0] = g
            outs[3 + i][0], outs[6 + i][0], outs[9 + i][0] = _adamw(ins[i][0], g, ins[3 + i][0], ins[6 + i][0])
            r0 += rows

    return pl.pallas_call(body, name="adamw_lora",
                          out_shape=[jax.ShapeDtypeStruct(a.shape, F32) for a in ws3] * 4)(g, *ws3, *ms3, *vs3)


def _rowsum_small(parts, loss8, after):
    def body(*refs):
        out = refs[-1]
        c0 = 0
        for ref in refs[:-2]:
            n = ref.shape[1]
            out[:, c0:c0 + n] = jnp.sum(ref[...], axis=0, keepdims=True)
            c0 += n

    k = len(parts) + 1
    return pl.pallas_call(body, name="rowsum_small", in_specs=[pl.BlockSpec(memory_space=pltpu.VMEM)] * k + [ANY],
                          out_shape=jax.ShapeDtypeStruct((1, N_SMALL), F32))(*parts, loss8, after)


def _reduce_adamw_small(sbuf, ws, ms, vs):
    nv = len(ws)

    def body(*refs):
        s_ref, ins, outs = refs[0], refs[1:1 + 3 * nv], refs[1 + 3 * nv:]
        tot = s_ref[0]
        for s in range(1, N_DEV):
            tot = tot + s_ref[s]
        c0 = 0
        for i in range(nv):
            rows, cols = ins[i].shape
            n = rows * cols
            for r in range(rows):
                outs[i][r:r + 1, :] = tot[:, c0 + cols * r:c0 + cols * (r + 1)]
            g = outs[i][...]
            outs[nv + i][...], outs[2 * nv + i][...], outs[3 * nv + i][...] = _adamw(
                ins[i][...], g, ins[nv + i][...], ins[2 * nv + i][...])
            c0 += n
        outs[-1][...] = tot[:, c0:]

    return pl.pallas_call(
        body, name="reduce_adamw_small",
        out_shape=[jax.ShapeDtypeStruct(a.shape, F32) for a in ws] * 4 + [jax.ShapeDtypeStruct((1, 128), F32)],
    )(sbuf, *ws, *ms, *vs)


_TRANSPOSED = ("w_in", "w_gate", "w_up")
_ROW_STACKED = MATS
_ADAM_TILE = {"w_in": 208, "w_out": 256, "w_gate": 176, "w_up": 176, "w_down": 176, "lora": 256}
_SUM_TILE = {"w_in": 208, "w_out": 128, "w_gate": 176, "w_up": 176, "w_down": 176, "lora": 128}


def _full(n, stacked):
    p, r, c = stacked.shape
    if n in _ROW_STACKED:
        return stacked.reshape(p * r, c)
    return jnp.transpose(stacked, (1, 0, 2)).reshape(r, p * c)


def _by_chip(n, full):
    if n in _ROW_STACKED:
        return full.reshape(N_CHIP, full.shape[0] // N_CHIP, full.shape[1])
    r, c = full.shape
    return jnp.transpose(full.reshape(r, N_CHIP, c // N_CHIP), (1, 0, 2))


def _with_own(land_shape, dtype, own, slot):
    return lax.dynamic_update_slice(lax.empty(land_shape, dtype), own[None], (slot,) + (0,) * own.ndim)


def _cast_into_slot(a, chip, tr, name, after=None):
    rows, cols = a.shape

    def body(chip_ref, a_ref, *rest):
        rest[-1][0] = a_ref[...].astype(BF16)

    extra = [] if after is None else [after]
    return pl.pallas_call(
        body, name=name,
        grid_spec=pltpu.PrefetchScalarGridSpec(
            num_scalar_prefetch=1, grid=(rows // tr,),
            in_specs=[pl.BlockSpec((tr, cols), lambda i, chip_ref: (i, 0))] + [ANY] * len(extra),
            out_specs=pl.BlockSpec((1, tr, cols), lambda i, chip_ref: (chip_ref[0], i, 0))),
        out_shape=jax.ShapeDtypeStruct((N_CHIP, rows, cols), BF16),
        compiler_params=_params(("parallel",)),
    )(chip, a, *extra)


def kernel(x, mix_norm_g, w_in, mu_shift, decay_w0, decay_w2, iclr_a0, iclr_a2, gate_g2, k_k, k_a, r_k, ln_x_w, ln_x_b, attn_out_g, w_out, ffn_norm_g, w_gate, w_up, w_down, final_norm_g, loss_target, m_mix_norm_g, m_w_in, m_mu_shift, m_decay_w0, m_decay_w2, m_iclr_a0, m_iclr_a2, m_gate_g2, m_k_k, m_k_a, m_r_k, m_ln_x_w, m_ln_x_b, m_attn_out_g, m_w_out, m_ffn_norm_g, m_w_gate, m_w_up, m_w_down, m_final_norm_g, v_mix_norm_g, v_w_in, v_mu_shift, v_decay_w0, v_decay_w2, v_iclr_a0, v_iclr_a2, v_gate_g2, v_k_k, v_k_a, v_r_k, v_ln_x_w, v_ln_x_b, v_attn_out_g, v_w_out, v_ffn_norm_g, v_w_gate, v_w_up, v_w_down, v_final_norm_g):
    names = ("mix_norm_g", "w_in", "mu_shift", "decay_w0", "decay_w2", "iclr_a0", "iclr_a2", "gate_g2", "k_k", "k_a",
             "r_k", "ln_x_w", "ln_x_b", "attn_out_g", "w_out", "ffn_norm_g", "w_gate", "w_up", "w_down", "final_norm_g")
    w = dict(zip(names, (mix_norm_g, w_in, mu_shift, decay_w0, decay_w2, iclr_a0, iclr_a2, gate_g2, k_k, k_a, r_k,
                         ln_x_w, ln_x_b, attn_out_g, w_out, ffn_norm_g, w_gate, w_up, w_down, final_norm_g)))
    m = dict(zip(names, (m_mix_norm_g, m_w_in, m_mu_shift, m_decay_w0, m_decay_w2, m_iclr_a0, m_iclr_a2, m_gate_g2,
                         m_k_k, m_k_a, m_r_k, m_ln_x_w, m_ln_x_b, m_attn_out_g, m_w_out, m_ffn_norm_g, m_w_gate,
                         m_w_up, m_w_down, m_final_norm_g)))
    v = dict(zip(names, (v_mix_norm_g, v_w_in, v_mu_shift, v_decay_w0, v_decay_w2, v_iclr_a0, v_iclr_a2, v_gate_g2,
                         v_k_k, v_k_a, v_r_k, v_ln_x_w, v_ln_x_b, v_attn_out_g, v_w_out, v_ffn_norm_g, v_w_gate,
                         v_w_up, v_w_down, v_final_norm_g)))
    first = ("w_in", "lora")
    rest = ("w_out", "w_gate", "w_up", "w_down")
    xi, yi, ci = lax.axis_index("x"), lax.axis_index("y"), lax.axis_index("c")
    my_chip, my_dev = 2 * xi + yi, 4 * xi + 2 * yi + ci
    gather, scatter = ("gather",) * 4, ("scatter",) * 4

    def stored(d):
        out = {n: jnp.transpose(d[n][0]) if n in _TRANSPOSED else d[n][0] for n in MATS}
        out["lora"] = jnp.concatenate([d[n][0] for n in LORAS], axis=0)
        return out

    ws, ms, vs = stored(w), stored(m), stored(v)
    lora_rows = [(0, 64), (64, 128), (128, 256)]
    chip = jnp.reshape(my_chip, (1,)).astype(jnp.int32)
    early = _swap_start(None, [_cast_into_slot(ws["w_in"], chip, _ADAM_TILE["w_in"], "cast_w_in"),
                               _with_own((N_CHIP,) + ws["lora"].shape, F32, ws["lora"], my_chip)], gather[:2],
                        "gather_first_start")
    lands = [_cast_into_slot(ws[n], chip, _ADAM_TILE[n], "cast_" + n, after=early[4]) for n in rest]
    gather_rest = ("gather", "gather", "whole", "whole")
    ssem, rsem, srcs_thru, lands_thru, tok = _swap_start(None, lands, gather_rest, "gather_rest_start")
    got = _swap_wait(early[0], early[1], early[2], early[3], tok, gather[:2], "gather_first_wait")
    win_all, lora_all = _swap_gathered(got, "gather_first_halves")
    win = _full("w_in", win_all)
    w2, a2, g2m = (_full(n, lora_all[:, a:b]) for n, (a, b) in zip(LORAS, lora_rows))

    vecs = {n: w[n].reshape(1, sz) for n, sz in VECS}

    def get_rest(after):
        got_rest = _swap_wait(ssem, rsem, srcs_thru, lands_thru, after, gather_rest, "gather_rest_wait")
        swapped = _swap_gathered(got_rest[:2], "gather_rest_halves")
        return [_full(n, z) for n, z in zip(rest, [*swapped, *got_rest[2:]])]

    flight = []

    def my_half(g):
        h = g.shape[1] // 2
        return lax.dynamic_slice(g, (my_chip, ci * h, 0), (1, h, g.shape[2]))[0]

    def send_rest(gw):
        gs = [_by_chip(n, gw[n]) for n in rest]
        into = [_with_own((N_DEV,) + my_half(g).shape, BF16, my_half(g), my_dev) for g in gs]
        flight.extend(_swap_start(gs, into, scatter, "exchange_rest_start"))
        return flight[4]

    loss8, dx, gw, gv = _local_step(x[0], loss_target[0], win, vecs, w2, a2, g2m, get_rest, send_rest)

    core = jnp.reshape(ci, (1,)).astype(jnp.int32)
    gs = [_by_chip("w_in", gw["w_in"]),
          jnp.concatenate([_by_chip(n, gw[n]) for n in LORAS], axis=1).astype(BF16)]
    theirs = _sibling_halves(gs, "presum_halves")
    sums = [_add_halves(g, o, core, _SUM_TILE[n], "chipsum_" + n) for n, g, o in zip(first, gs, theirs)]
    own = [lax.dynamic_index_in_dim(s, my_chip, 0, keepdims=False) for s in sums]
    last = _swap_start(sums, [_with_own(s.shape, BF16, o, my_chip) for s, o in zip(sums, own)], ("chipsum",) * 2,
                       "exchange_first_start")
    small = _rowsum_small([gv[n] for n, _ in VECS], loss8, after=last[4])
    vecs_out = _swap_start([small], [_with_own((N_DEV,) + small.shape, F32, small, my_dev)], ("all",),
                           "exchange_vectors_start")


    def update(group, rbufs, tag):
        sums = [_reduce8(rb, core, _SUM_TILE[n], "reduce_" + n) for n, rb in zip(group, rbufs)]
        gsum = _join_halves(sums, "join_halves_" + tag)
        out = {}
        for n, g in zip(group, gsum):
            if n == "lora":
                r = _adamw_lora(g, *[[d[k] for k in LORAS] for d in (w, m, v)])
                for i, name in enumerate(LORAS):
                    out[name] = r[i::3]
            else:
                r = _adamw_call(g, ws[n][None], ms[n][None], vs[n][None], _ADAM_TILE[n], "adamw_" + n)
                out[n] = [jnp.transpose(z[0])[None] for z in r] if n in _TRANSPOSED else r
        return out, r[1]

    res, done = update(rest, _swap_wait(flight[0], flight[1], flight[2], flight[3], vecs_out[4], scatter,
                                        "exchange_rest_wait"), "rest")
    got = _swap_wait(last[0], last[1], last[2], last[3], done, ("chipsum",) * 2, "exchange_first_wait")
    res_first, done = update(first, got, "first")
    res.update(res_first)
    sbuf = _swap_wait(vecs_out[0], vecs_out[1], vecs_out[2], vecs_out[3], done, ("all",), "exchange_vectors_wait")[0]
    rows = lambda d: [d[n].reshape(-1, d[n].shape[-1]) for n, _ in VECS]
    small_res = _reduce_adamw_small(sbuf, rows(w), rows(m), rows(v))

    outs = []
    for k in range(4):
        piece = {n: r[k] for n, r in res.items()}
        for i, (n, _) in enumerate(VECS):
            piece[n] = small_res[k * len(VECS) + i].reshape(w[n].shape)
        outs.extend(piece[n] for n in names)
    return (small_res[-1][0, 0], dx[None], *outs)
```

```python
import jax
import jax.numpy as jnp
from jax import lax
from jax.experimental import pallas as pl
from jax.experimental.pallas import tpu as pltpu

F32 = jnp.float32
BF16 = jnp.bfloat16

D_MODEL = 1024
HEAD_DIM = 64
RW = 512
N_PAIR = RW // 128
SHIFT_COLS = 1792
IN_COLS = 3328
D_FF = 2816
FF_CHUNK = 256
NORM_EPS = 1e-6
GN_EPS = 64e-5
CHUNK = 64
SUB = 16
WKV_PASSES = 1
ATTN_PASSES = 1
ATTN_BLOCK = 128
DILATIONS = (1, 4, 16)
NEG = -1e30
ADAM_LR, ADAM_B1, ADAM_B2, ADAM_EPS, ADAM_WD, ADAM_STEP = 0.001, 0.9, 0.999, 1e-08, 0.01, 10
VMEM_LIMIT = 56 * 1024 * 1024
MESH = pl.DeviceIdType.MESH


def _params(sem=None, **kw):
    return pltpu.CompilerParams(dimension_semantics=sem, vmem_limit_bytes=VMEM_LIMIT, **kw)


def _dot(a, b):
    return lax.dot_general(a, b, (((1,), (0,)), ((), ())), preferred_element_type=F32)


def _dot_nt(a, b):
    return lax.dot_general(a, b, (((1,), (1,)), ((), ())), preferred_element_type=F32)


def _dot_tn(a, b):
    return lax.dot_general(a, b, (((0,), (0,)), ((), ())), preferred_element_type=F32)


_FORMS = {"nn": ((1,), (0,)), "nt": ((1,), (1,)), "tn": ((0,), (0,))}


def _dg(a, b, form):
    if a.ndim == 3 or b.ndim == 3:
        nb = a.shape[0] if a.ndim == 3 else b.shape[0]
        return jnp.stack([_dg(a[i] if a.ndim == 3 else a, b[i] if b.ndim == 3 else b, form) for i in range(nb)], axis=0)
    return lax.dot_general(a, b, (_FORMS[form], ((), ())), preferred_element_type=F32)


def _split2(x):
    hi = x.astype(BF16)
    return hi, (x - hi.astype(F32)).astype(BF16)


def _split3(x):
    hi = x.astype(BF16)
    rest = x - hi.astype(F32)
    mid = rest.astype(BF16)
    return hi, mid, (rest - mid.astype(F32)).astype(BF16)


def _mm_raw(a, b, form, mode):
    if mode == 1:
        return _dg(a.astype(BF16), b.astype(BF16), form)
    if mode == 3:
        ah, al = _split2(a)
        bh, bl = _split2(b)
        return _dg(ah, bh, form) + (_dg(ah, bl, form) + _dg(al, bh, form))
    if mode == "L3":
        ab = a.astype(BF16)
        b1, b2, b3 = _split3(b)
        if form == "nn":
            n = b.shape[-1]
            wide = _dg(ab, jnp.concatenate([b1, b2, b3], axis=-1), form)
            return wide[..., :n] + (wide[..., n:2 * n] + wide[..., 2 * n:])
        return _dg(ab, b1, form) + (_dg(ab, b2, form) + _dg(ab, b3, form))
    assert mode == "R3", mode
    bb = b.astype(BF16)
    a1, a2, a3 = _split3(a)
    if form in ("nn", "nt"):
        m = a.shape[-2]
        tall = _dg(jnp.concatenate([a1, a2, a3], axis=-2), bb, form)
        return tall[..., :m, :] + (tall[..., m:2 * m, :] + tall[..., 2 * m:, :])
    return _dg(a1, bb, form) + (_dg(a2, bb, form) + _dg(a3, bb, form))


def _mm(a, b, form, mode):
    @jax.custom_vjp
    def f(a, b):
        return _mm_raw(a, b, form, mode)

    def fwd(a, b):
        return _mm_raw(a, b, form, mode), (a, b)

    def bwd(res, ct):
        a, b = res
        la = {1: 1, 3: 3, "L3": None, "R3": "R3"}[mode]
        lb = {1: 1, 3: 3, "L3": "L3", "R3": None}[mode]
        if form == "nn":
            da = None if la is None else _mm_raw(ct, b, "nt", la)
            db = None if lb is None else _mm_raw(a, ct, "tn", lb)
        elif form == "nt":
            da = None if la is None else _mm_raw(ct, b, "nn", la)
            db = None if lb is None else _mm_raw(ct, a, "tn", "R3" if lb == "L3" else lb)
        else:
            da = None if la is None else _mm_raw(b, ct, "nt", "L3" if la == "R3" else la)
            db = None if lb is None else _mm_raw(a, ct, "nn", lb)
        return (jnp.zeros_like(a) if da is None else da, jnp.zeros_like(b) if db is None else db)

    f.defvjp(fwd, bwd)
    return f(a, b)


def _seg_ones(n):
    r = lax.broadcasted_iota(jnp.int32, (n, n), 0) // HEAD_DIM
    c = lax.broadcasted_iota(jnp.int32, (n, n), 1) // HEAD_DIM
    return (r == c).astype(F32)


def _segsum(x, seg):
    return _mm(x, seg, "nn", "R3")


def _rms_fwd(x, g):
    rstd = lax.rsqrt(jnp.mean(x * x, axis=-1, keepdims=True) + NORM_EPS)
    return x * rstd * g


def _rms_bwd(dy, x, g):
    rstd = lax.rsqrt(jnp.mean(x * x, axis=-1, keepdims=True) + NORM_EPS)
    xn = x * rstd
    dxn = dy * g
    dx = rstd * (dxn - xn * jnp.mean(dxn * xn, axis=-1, keepdims=True))
    return dx, dy * xn


def _sigmoid(x):
    return 1.0 / (1.0 + jnp.exp(-x))


def _softplus(x):
    return jnp.maximum(x, 0.0) + jnp.log(1.0 + jnp.exp(-jnp.abs(x)))


def _acc(ref, val, first):
    @pl.when(first)
    def _():
        ref[...] = val

    @pl.when(jnp.logical_not(first))
    def _():
        ref[...] += val


def _colsum8(v):
    rows, n = v.shape
    return jnp.sum(v.reshape(rows // 8, 8, n), axis=0)


def _prep_fn(p, pprev, mu, w0, w2p, a0, a2p, g2, k_k, k_a):
    seg = _seg_ones(RW)
    ps = p + (pprev - p) * mu
    r = ps[:, 0:RW]
    k = ps[:, RW:2 * RW]
    v = ps[:, 2 * RW:3 * RW]
    xwa = ps[:, 3 * RW:3 * RW + 128]
    xg = ps[:, 3 * RW + 128:3 * RW + 256]
    wraw = -_softplus(-(w0 + _mm(jnp.tanh(xwa), w2p, "nn", 3))) - 0.5
    lw = -jnp.exp(wraw)
    a = _sigmoid(a0 + _mm(xwa, a2p, "nn", 3))
    g = _mm(_sigmoid(xg), g2, "nn", 3)
    kk = k * k_k
    kk = kk / jnp.maximum(jnp.sqrt(_segsum(kk * kk, seg)), 1e-12)
    k2 = k * (1.0 + (a - 1.0) * k_a)
    return r, lw, k2, v, kk, a, g


def _transposed(z):
    return jnp.stack([z[i].T for i in range(z.shape[0])], axis=0) if z.ndim == 3 else z.T


def _solve_unit_lower(lmat, rhs):
    c = lmat.shape[-1]
    row = lax.broadcasted_iota(jnp.int32, (c, c), 0)
    col = lax.broadcasted_iota(jnp.int32, (c, c), 1)
    eye = (row == col).astype(F32)
    ld = jnp.where(row // SUB == col // SUB, lmat, 0.0)
    lo = lmat - ld
    x = eye + ld
    m = ld
    mm = lambda p, q: _mm(p, q, "nn", WKV_PASSES)
    cat = jnp.concatenate
    m = mm(m, m)
    for _ in range(2):
        mx = mm(m, cat([m, x], axis=-1))
        m, x = mx[..., :c], x + mx[..., c:]
    x = x + mm(m, x)
    gw = mm(x, cat([lo, rhs], axis=-1))
    g, w = gw[..., :c], gw[..., c:]
    gg = mm(g, cat([g, w], axis=-1))
    w = w + gg[..., c:]
    return w + mm(gg[..., :c], w)


def _wkv_chunk_fn(s0, r, lw, k, v, kk, a):
    c = r.shape[-2]
    n = 2 * c
    row = lax.broadcasted_iota(jnp.int32, (n, n), 0)
    col = lax.broadcasted_iota(jnp.int32, (n, n), 1)
    same = (row // c) == (col // c)
    incl = jnp.logical_and(row >= col, same)
    strict = jnp.logical_and(row > col, same)
    sel = (lax.broadcasted_iota(jnp.int32, (n, 128), 0) // c) == (lax.broadcasted_iota(jnp.int32, (n, 128), 1) // HEAD_DIM)
    two = lambda z: jnp.concatenate([z, z], axis=-2)
    lw2 = two(lw)
    mm = lambda p_, q_, form: _mm(p_, q_, form, WKV_PASSES)
    cl = _mm(incl.astype(F32), lw2, "nn", "L3")
    p = jnp.exp(cl)
    pinv = jnp.exp(-cl)
    pprev = jnp.exp(cl - lw2)
    kk2 = two(kk)
    at = jnp.where(sel, -kk2 * pprev, 0.0)
    bt = jnp.where(sel, kk2 * two(a) * pinv, 0.0)
    kt = jnp.where(sel, two(k) * pinv, 0.0)
    rt = jnp.where(sel, two(r) * p, 0.0)
    vt = jnp.where(sel, two(v), 0.0)
    cat = jnp.concatenate
    bk = cat([bt, kt], axis=-2)
    arbk = mm(cat([at, rt], axis=-2), bk, "nt")
    ab, ak = jnp.where(strict, arbk[..., :n, :n], 0.0), jnp.where(strict, arbk[..., :n, n:], 0.0)
    rb, rk = jnp.where(incl, arbk[..., n:, :n], 0.0), jnp.where(incl, arbk[..., n:, n:], 0.0)
    s0t = _transposed(s0)
    u = _solve_unit_lower(ab, mm(cat([at, ak], axis=-1), cat([s0t, vt], axis=-2), "nn"))
    y2 = mm(cat([rt, rb, rk], axis=-1), cat([s0t, u, vt], axis=-2), "nn")
    plast = jnp.exp(jnp.sum(lw, axis=-2, keepdims=True))
    s1 = (s0 + mm(cat([u, vt], axis=-2), bk, "tn")) * plast
    r2 = lax.broadcasted_iota(jnp.int32, (128, 128), 0) // HEAD_DIM
    c2 = lax.broadcasted_iota(jnp.int32, (128, 128), 1) // HEAD_DIM
    return y2[..., :c, :] + y2[..., c:, :], jnp.where(r2 == c2, s1, 0.0)


def _post_fn(y, r, k2, v, g, lnw, lnb, rk):
    seg = _seg_ones(RW)
    mean = _segsum(y, seg) * (1.0 / HEAD_DIM)
    yc = y - mean
    var = _segsum(yc * yc, seg) * (1.0 / HEAD_DIM)
    yn = yc * lax.rsqrt(var + GN_EPS)
    out = yn * lnw + lnb + _segsum(r * k2 * rk, seg) * v
    return out * g


def _attn_block_fn(q, kc, vc, kp=None, vp=None):
    n = ATTN_BLOCK
    qi = lax.broadcasted_iota(jnp.int32, (n, n), 0)
    kj = lax.broadcasted_iota(jnp.int32, (n, n), 1)
    lane = lax.broadcasted_iota(jnp.int32, (1, 128), 1)
    scale = HEAD_DIM ** -0.5
    valid = kj <= qi
    keys, vals = kc, vc
    if kp is not None:
        valid = jnp.concatenate([valid, kj >= qi], axis=-1)
        keys, vals = jnp.concatenate([kc, kp], axis=-2), jnp.concatenate([vc, vp], axis=-2)
    m0 = (lane // HEAD_DIM) == 0
    q2 = jnp.concatenate([jnp.where(m0, q, 0.0), jnp.where(m0, 0.0, q)], axis=-2)
    valid2 = jnp.concatenate([valid, valid], axis=-2)
    s = jnp.where(valid2, _mm(q2, keys, "nt", ATTN_PASSES) * scale, NEG)
    m = jnp.max(s, axis=-1, keepdims=True)
    p = jnp.exp(s - m)
    den = jnp.sum(p, axis=-1, keepdims=True)
    o2 = _mm(p, vals, "nn", ATTN_PASSES) / den
    l2 = m + jnp.log(den)
    return jnp.where(m0, o2[..., :n, :], o2[..., n:, :]), jnp.where(m0, l2[..., :n, :], l2[..., n:, :])


def _attn_block_bwd(q, kc, vc, kp, vp, o, lse, do, dl):
    n = ATTN_BLOCK
    cat = jnp.concatenate
    qi = lax.broadcasted_iota(jnp.int32, (n, n), 0)
    kj = lax.broadcasted_iota(jnp.int32, (n, n), 1)
    m0 = (lax.broadcasted_iota(jnp.int32, (1, 128), 1) // HEAD_DIM) == 0
    scale = HEAD_DIM ** -0.5
    valid = kj <= qi
    keys, vals = kc, vc
    if kp is not None:
        valid = cat([valid, kj >= qi], axis=-1)
        keys, vals = cat([kc, kp], axis=-2), cat([vc, vp], axis=-2)
    stack = lambda z: cat([jnp.where(m0, z, 0.0), jnp.where(m0, 0.0, z)], axis=-2)
    q2, do2 = stack(q), stack(do)
    lse2 = cat([jnp.max(jnp.where(m0, lse, NEG), axis=-1, keepdims=True),
                jnp.max(jnp.where(m0, NEG, lse), axis=-1, keepdims=True)], axis=-2)
    delta = jnp.sum(do2 * cat([o, o], axis=-2), axis=-1, keepdims=True)
    dlse = jnp.sum(stack(dl), axis=-1, keepdims=True)
    mm = lambda a, b, form: _mm_raw(a, b, form, ATTN_PASSES)
    s = jnp.where(cat([valid, valid], axis=-2), mm(q2, keys, "nt") * scale, NEG)
    p = jnp.exp(s - lse2)
    ds = p * (mm(do2, vals, "nt") - delta + dlse)
    dq2 = mm(ds, keys, "nn") * scale
    dq = jnp.where(m0, dq2[..., :n, :], dq2[..., n:, :])
    dkeys = mm(ds, q2, "tn") * scale
    dvals = mm(p, do2, "tn")
    if kp is None:
        return dq, dkeys, dvals
    return dq, dkeys[..., :n, :], dvals[..., :n, :], dkeys[..., n:, :], dvals[..., n:, :]


def _combine_fn(o1, o2, o3, l1, l2, l3, og):
    seg = _seg_ones(o1.shape[-1])
    m = jnp.maximum(jnp.maximum(l1, l2), l3)
    e1, e2, e3 = jnp.exp(l1 - m), jnp.exp(l2 - m), jnp.exp(l3 - m)
    o = (e1 * o1 + e2 * o2 + e3 * o3) / (e1 + e2 + e3)
    o = o * lax.rsqrt(_segsum(o * o, seg) * (1.0 / HEAD_DIM) + NORM_EPS)
    return o * og


def _shifted(p, last8, first):
    prow = jnp.where(first, 0.0, last8[7:8, :])
    rolled = pltpu.roll(p, 1, axis=0)
    rid = lax.broadcasted_iota(jnp.int32, p.shape, 0)
    return jnp.where(rid == 0, prow, rolled)


_PREP_TM = 256


def _prep_specs(tm):
    vec = lambda n: pl.BlockSpec((1, n), lambda i: (0, 0))
    mat = lambda r, n: pl.BlockSpec((r, n), lambda i: (0, 0))
    return [vec(SHIFT_COLS), vec(RW), mat(128, RW), vec(RW), mat(128, RW), mat(128, RW), vec(RW), vec(RW)]


def _in_proj_prep(x, g1, win, pw):
    t = x.shape[0]
    tm = _PREP_TM

    def body(x_ref, g_ref, w_ref, mu, w0, w2p, a0, a2p, g2, k_k, k_a, h_ref, pa_ref, qkv_ref, *rest):
        outs, carry = rest[:7], rest[7]

        @pl.when(pl.program_id(0) == 0)
        def _():
            carry[...] = jnp.zeros_like(carry)

        h = _rms_fwd(x_ref[...], g_ref[...]).astype(BF16)
        h_ref[...] = h
        proj = _dot_nt(h, w_ref[...])
        p = proj[:, :SHIFT_COLS]
        pa_ref[...] = p
        for j in range(3):
            for pr in range(N_PAIR):
                c0 = SHIFT_COLS + j * RW + pr * 128
                qkv_ref[j, pr] = proj[:, c0:c0 + 128]
        pprev = _shifted(p, carry[...], pl.program_id(0) == 0)
        carry[...] = p[tm - 8:, :]
        res = _prep_fn(p, pprev, mu[...], w0[...], w2p[...], a0[...], a2p[...], g2[...], k_k[...], k_a[...])
        for o_ref, val in zip(outs, res):
            o_ref[...] = val

    row = pl.BlockSpec((tm, RW), lambda i: (i, 0))
    return pl.pallas_call(
        body, name="in_proj_prep", grid=(t // tm,),
        in_specs=[pl.BlockSpec((tm, D_MODEL), lambda i: (i, 0)), pl.BlockSpec((1, D_MODEL), lambda i: (0, 0)),
                  pl.BlockSpec((IN_COLS, D_MODEL), lambda i: (0, 0))] + _prep_specs(tm),
        out_specs=[pl.BlockSpec((tm, D_MODEL), lambda i: (i, 0)), pl.BlockSpec((tm, SHIFT_COLS), lambda i: (i, 0)),
                   pl.BlockSpec((3, N_PAIR, tm, 128), lambda i: (0, 0, i, 0))] + [row] * 7,
        out_shape=[jax.ShapeDtypeStruct((t, D_MODEL), BF16), jax.ShapeDtypeStruct((t, SHIFT_COLS), F32),
                   jax.ShapeDtypeStruct((3, N_PAIR, t, 128), F32)] + [jax.ShapeDtypeStruct((t, RW), F32)] * 7,
        scratch_shapes=[pltpu.VMEM((8, SHIFT_COLS), F32)],
        compiler_params=_params(("arbitrary",)),
    )(x, g1, win, *pw)


def _pairs(ref):
    return jnp.stack([ref[:, 128 * p:128 * (p + 1)] for p in range(N_PAIR)], axis=0)


def _wkv_fwd(r, lw, k2, v, kk, a):
    t = r.shape[0]
    nc = t // CHUNK

    def body(r_ref, lw_ref, k_ref, v_ref, kk_ref, a_ref, y_ref, s_ref, st):
        @pl.when(pl.program_id(0) == 0)
        def _():
            st[...] = jnp.zeros_like(st)

        s0 = st[...]
        s_ref[0] = s0
        y, s1 = _wkv_chunk_fn(s0, *[_pairs(ref) for ref in (r_ref, lw_ref, k_ref, v_ref, kk_ref, a_ref)])
        for p in range(N_PAIR):
            y_ref[:, 128 * p:128 * (p + 1)] = y[p]
        st[...] = s1

    blk = pl.BlockSpec((CHUNK, RW), lambda c: (c, 0))
    return pl.pallas_call(
        body, name="wkv_fwd", grid=(nc,),
        in_specs=[blk] * 6,
        out_specs=[blk, pl.BlockSpec((1, N_PAIR, 128, 128), lambda c: (c, 0, 0, 0))],
        out_shape=[jax.ShapeDtypeStruct((t, RW), F32), jax.ShapeDtypeStruct((nc, N_PAIR, 128, 128), F32)],
        scratch_shapes=[pltpu.VMEM((N_PAIR, 128, 128), F32)],
        compiler_params=_params(("arbitrary",)),
    )(r, lw, k2, v, kk, a)


_POST_TM = 512


ATTN_GROUP = 2


def _dilated_rows(d, r, n):
    if d == 1:
        return pl.ds(pl.multiple_of(n * ATTN_BLOCK, ATTN_BLOCK), ATTN_BLOCK)
    return pl.ds(r + n * (ATTN_BLOCK * d), ATTN_BLOCK, stride=d)


def _for_each_sequence(t, unit):
    for di, d in enumerate(DILATIONS):

        @pl.when(pl.program_id(1) == di)
        def _(di=di, d=d):
            nb = t // (ATTN_BLOCK * d)
            if d == 1:
                unit(di, [(d, 0, 0)], False)
                unit(di, [(d, 0, 1)], True)
                lax.fori_loop(1, nb // 2, lambda k, c: (unit(di, [(d, 0, 2 * k), (d, 0, 2 * k + 1)], True), c)[1], 0)
            else:

                def residues(r, carry):
                    unit(di, [(d, r, 0), (d, r + d // 2, 0)], False)
                    if nb > 1:
                        lax.fori_loop(1, nb, lambda n, c: (unit(di, [(d, r, n), (d, r + d // 2, n)], True), c)[1], 0)
                    return carry

                lax.fori_loop(0, d // 2, residues, 0)


def _take(ref, lead, rows_list):
    return jnp.stack([ref.at[(*lead, g)][rows, :] for rows in rows_list for g in range(ref.shape[len(lead)])], axis=0)


def _put(ref, lead, rows_list, val, add=False):
    k = 0
    for rows in rows_list:
        for g in range(ref.shape[len(lead)]):
            if add:
                ref.at[(*lead, g)][rows, :] += val[k]
            else:
                ref.at[(*lead, g)][rows, :] = val[k]
            k += 1


def _attn_fwd(qkv):
    t = qkv.shape[2]

    def body(q_ref, k_ref, v_ref, o_ref, l_ref):
        def unit(di, places, has_prev):
            cur = [_dilated_rows(d, r, n) for d, r, n in places]
            args = [_take(ref, (0,), cur) for ref in (q_ref, k_ref, v_ref)]
            if has_prev:
                prv = [_dilated_rows(d, r, n - 1) for d, r, n in places]
                args += [_take(ref, (0,), prv) for ref in (k_ref, v_ref)]
            o, lse = _attn_block_fn(*args)
            _put(o_ref, (0,), cur, o)
            _put(l_ref, (0,), cur, lse)

        _for_each_sequence(t, unit)

    spec = lambda j: pl.BlockSpec((1, ATTN_GROUP, t, 128), lambda i, b: (j, i, 0, 0))
    out = pl.BlockSpec((1, ATTN_GROUP, t, 128), lambda i, b: (b, i, 0, 0))
    return pl.pallas_call(
        body, name="attn_fwd", grid=(N_PAIR // ATTN_GROUP, len(DILATIONS)),
        in_specs=[spec(0), spec(1), spec(2)], out_specs=[out, out],
        out_shape=[jax.ShapeDtypeStruct((3, N_PAIR, t, 128), F32)] * 2,
        compiler_params=_params(("parallel", "arbitrary")),
    )(qkv, qkv, qkv)


_COMB_TM = 512


def _mixers_out(y, r, k2, v, g, lnw, lnb, rk, o, l, og):
    t = y.shape[0]
    tm = _COMB_TM

    def body(y_ref, r_ref, k_ref, v_ref, g_ref, lnw_ref, lnb_ref, rk_ref, o_ref, l_ref, og_ref, out_ref):
        out_ref[:, :RW] = _post_fn(y_ref[...], r_ref[...], k_ref[...], v_ref[...], g_ref[...],
                                   lnw_ref[...], lnb_ref[...], rk_ref[...]).astype(BF16)
        for p in range(N_PAIR):
            cols = slice(128 * p, 128 * (p + 1))
            out_ref[:, RW + 128 * p:RW + 128 * (p + 1)] = _combine_fn(
                o_ref[0, p], o_ref[1, p], o_ref[2, p], l_ref[0, p], l_ref[1, p], l_ref[2, p], og_ref[:, cols]).astype(BF16)

    row = pl.BlockSpec((tm, RW), lambda i: (i, 0))
    vec = pl.BlockSpec((1, RW), lambda i: (0, 0))
    blk = pl.BlockSpec((3, N_PAIR, tm, 128), lambda i: (0, 0, i, 0))
    return pl.pallas_call(
        body, name="mixers_out", grid=(t // tm,),
        in_specs=[row] * 5 + [vec] * 3 + [blk, blk, vec], out_specs=pl.BlockSpec((tm, D_MODEL), lambda i: (i, 0)),
        out_shape=jax.ShapeDtypeStruct((t, D_MODEL), BF16),
        compiler_params=_params(("parallel",)),
    )(y, r, k2, v, g, lnw, lnb, rk, o, l, og)


def _ffn_all(x, ycat, wg, wu, wd, wout, g2, gf, tgt):
    t = x.shape[0]
    tm = 256

    def body(x_ref, y_ref, wg_ref, wu_ref, wd_ref, wo_ref, g2_ref, gf_ref, t_ref,
             h_ref, act_ref, dx2b_ref, dgt_ref, dup_ref, dx1b_ref, dx1_ref, dya_ref, dyb_ref, loss_ref, dgf_ref, dg2_ref,
             gt_s, up_s):
        first = pl.program_id(0) == 0
        x1 = x_ref[...] + _dot(y_ref[...], wo_ref[...])
        h = _rms_fwd(x1, g2_ref[...]).astype(BF16)
        h_ref[...] = h
        for c0 in range(0, D_FF, FF_CHUNK):
            cols = slice(c0, c0 + FF_CHUNK)
            gt = _dot_nt(h, wg_ref[cols, :])
            up = _dot_nt(h, wu_ref[cols, :])
            gt_s[:, cols] = gt.astype(BF16)
            up_s[:, cols] = up.astype(BF16)
            act_ref[:, cols] = (gt * _sigmoid(gt) * up).astype(BF16)
        x2 = x1 + _dot(act_ref[...], wd_ref[...])
        gf_ = gf_ref[...]
        diff = _rms_fwd(x2, gf_) - t_ref[...]
        lrow = 0.5 * jnp.sum(_colsum8(diff * diff), axis=1, keepdims=True) * (1.0 / D_MODEL)
        _acc(loss_ref, jnp.broadcast_to(lrow, (8, 128)), first)
        dx2, dgr = _rms_bwd(diff * (1.0 / D_MODEL), x2, gf_)
        _acc(dgf_ref, _colsum8(dgr), first)
        dx2b = dx2.astype(BF16)
        dx2b_ref[...] = dx2b
        for c0 in range(0, D_FF, FF_CHUNK):
            cols = slice(c0, c0 + FF_CHUNK)
            dact = _dot_nt(dx2b, wd_ref[cols, :])
            gt = gt_s[:, cols].astype(F32)
            sg = _sigmoid(gt)
            dgt_ref[:, cols] = (dact * up_s[:, cols].astype(F32) * sg * (1.0 + gt * (1.0 - sg))).astype(BF16)
            dup_ref[:, cols] = (dact * gt * sg).astype(BF16)
        dh = _dot(dgt_ref[...], wg_ref[...]) + _dot(dup_ref[...], wu_ref[...])
        dxn, dgr2 = _rms_bwd(dh, x1, g2_ref[...])
        _acc(dg2_ref, _colsum8(dgr2), first)
        dx1 = dx2 + dxn
        dx1_ref[...] = dx1
        dx1b = dx1.astype(BF16)
        dx1b_ref[...] = dx1b
        dy = _dot_nt(dx1b, wo_ref[...])
        dya_ref[...] = dy[:, :RW]
        dyb_ref[...] = dy[:, RW:]

    row = pl.BlockSpec((tm, D_MODEL), lambda i: (i, 0))
    wide = pl.BlockSpec((tm, D_FF), lambda i: (i, 0))
    half = pl.BlockSpec((tm, RW), lambda i: (i, 0))
    wsp = pl.BlockSpec((D_FF, D_MODEL), lambda i: (0, 0))
    vec = pl.BlockSpec((1, D_MODEL), lambda i: (0, 0))
    part = pl.BlockSpec((8, D_MODEL), lambda i: (0, 0))
    bf = lambda n: jax.ShapeDtypeStruct((t, n), BF16)
    return pl.pallas_call(
        body, name="ffn_all", grid=(t // tm,),
        in_specs=[row, row, wsp, wsp, wsp, pl.BlockSpec((D_MODEL, D_MODEL), lambda i: (0, 0)), vec, vec, row],
        out_specs=[row, wide, row, wide, wide, row, row, half, half, pl.BlockSpec((8, 128), lambda i: (0, 0)), part, part],
        out_shape=[bf(D_MODEL), bf(D_FF), bf(D_MODEL), bf(D_FF), bf(D_FF), bf(D_MODEL),
                   jax.ShapeDtypeStruct((t, D_MODEL), F32), jax.ShapeDtypeStruct((t, RW), F32),
                   jax.ShapeDtypeStruct((t, RW), F32), jax.ShapeDtypeStruct((8, 128), F32),
                   jax.ShapeDtypeStruct((8, D_MODEL), F32), jax.ShapeDtypeStruct((8, D_MODEL), F32)],
        scratch_shapes=[pltpu.VMEM((tm, D_FF), BF16), pltpu.VMEM((tm, D_FF), BF16)],
        compiler_params=_params(("arbitrary",)),
    )(x, ycat, wg, wu, wd, wout, g2, gf, tgt)


def _wgrad(a, b, tk, tn, name):
    t, kdim = a.shape
    ndim = b.shape[1]

    def body(a_ref, b_ref, o_ref):
        o_ref[...] = _dot_tn(a_ref[...], b_ref[...]).astype(BF16)

    return pl.pallas_call(
        body, name=name, grid=(kdim // tk, ndim // tn),
        in_specs=[pl.BlockSpec((t, tk), lambda i, j: (0, i)), pl.BlockSpec((t, tn), lambda i, j: (0, j))],
        out_specs=pl.BlockSpec((tk, tn), lambda i, j: (i, j)),
        out_shape=jax.ShapeDtypeStruct((kdim, ndim), BF16),
        compiler_params=_params(("parallel", "parallel")),
    )(a, b)


def _post_bwd(dya, y, r, k2, v, g, lnw, lnb, rk, after):
    t = y.shape[0]
    tm = _POST_TM

    def body(d_ref, y_ref, r_ref, k_ref, v_ref, g_ref, lnw_ref, lnb_ref, rk_ref, _,
             dy_ref, dr_ref, dk_ref, dv_ref, dg_ref, dlnw_ref, dlnb_ref, drk_ref):
        first = pl.program_id(0) == 0
        ones = jnp.ones((tm, 1), F32)
        prim = (y_ref[...], r_ref[...], k_ref[...], v_ref[...], g_ref[...],
                ones * lnw_ref[...], ones * lnb_ref[...], ones * rk_ref[...])
        _, vjp = jax.vjp(_post_fn, *prim)
        dy, dr, dk, dv, dg, dlnw, dlnb, drk = vjp(d_ref[...])
        dy_ref[...] = dy
        dr_ref[...] = dr
        dk_ref[...] = dk
        dv_ref[...] = dv
        dg_ref[...] = dg
        _acc(dlnw_ref, _colsum8(dlnw), first)
        _acc(dlnb_ref, _colsum8(dlnb), first)
        _acc(drk_ref, _colsum8(drk), first)

    row = pl.BlockSpec((tm, RW), lambda i: (i, 0))
    vec = pl.BlockSpec((1, RW), lambda i: (0, 0))
    part = pl.BlockSpec((8, RW), lambda i: (0, 0))
    return pl.pallas_call(
        body, name="rwkv_post_bwd", grid=(t // tm,),
        in_specs=[row] * 6 + [vec] * 3 + [ANY], out_specs=[row] * 5 + [part] * 3,
        out_shape=[jax.ShapeDtypeStruct((t, RW), F32)] * 5 + [jax.ShapeDtypeStruct((8, RW), F32)] * 3,
        compiler_params=_params(("arbitrary",)),
    )(dya, y, r, k2, v, g, lnw, lnb, rk, after)


def _wkv_bwd(dy, s0s, r, lw, k2, v, kk, a):
    t = r.shape[0]
    nc = t // CHUNK

    def body(dy_ref, s_ref, r_ref, lw_ref, k_ref, v_ref, kk_ref, a_ref,
             dr_ref, dlw_ref, dk_ref, dv_ref, dkk_ref, da_ref, ds):
        @pl.when(pl.program_id(0) == 0)
        def _():
            ds[...] = jnp.zeros_like(ds)

        _, vjp = jax.vjp(_wkv_chunk_fn, s_ref[0],
                         *[_pairs(ref) for ref in (r_ref, lw_ref, k_ref, v_ref, kk_ref, a_ref)])
        res = vjp((_pairs(dy_ref), ds[...]))
        ds[...] = res[0]
        for ref, val in zip((dr_ref, dlw_ref, dk_ref, dv_ref, dkk_ref, da_ref), res[1:]):
            for p in range(N_PAIR):
                ref[:, 128 * p:128 * (p + 1)] = val[p]

    blk = pl.BlockSpec((CHUNK, RW), lambda c: (nc - 1 - c, 0))
    return pl.pallas_call(
        body, name="wkv_bwd", grid=(nc,),
        in_specs=[blk, pl.BlockSpec((1, N_PAIR, 128, 128), lambda c: (nc - 1 - c, 0, 0, 0))] + [blk] * 6,
        out_specs=[blk] * 6,
        out_shape=[jax.ShapeDtypeStruct((t, RW), F32)] * 6,
        scratch_shapes=[pltpu.VMEM((N_PAIR, 128, 128), F32)],
        compiler_params=_params(("arbitrary",)),
    )(dy, s0s, r, lw, k2, v, kk, a)


def _prep_in_proj_bwd(proj, pw, douts, dq, dk, dv, win, x, g1, dx1):
    t = proj.shape[0]
    tm = _PREP_TM
    nt = t // tm

    def body(p_ref, l8_ref, mu, w0, w2p, a0, a2p, g2, k_k, k_a, dr, dr2, dlw, dk2, dk22, dv, dv2, dkk, da, dg,
             dq_ref, dkq_ref, dvq_ref, w_ref, x_ref, g1_ref, dx1_ref,
             dproj_ref, dx_ref, dg1_ref, dmu_ref, dw0_ref, dw2_ref, da0_ref, da2_ref, dg2_ref, dkk_ref, dka_ref, carry):
        i = pl.program_id(0)
        first = i == 0

        @pl.when(first)
        def _():
            carry[...] = jnp.zeros_like(carry)

        p = p_ref[...]
        pprev = _shifted(p, l8_ref[...], i == nt - 1)
        ones = jnp.ones((tm, 1), F32)
        prim = (p, pprev, ones * mu[...], ones * w0[...], w2p[...], ones * a0[...], a2p[...], g2[...],
                ones * k_k[...], ones * k_a[...])
        _, vjp = jax.vjp(_prep_fn, *prim)
        dp, dpp, dmu, dw0, dw2, da0, da2, dg2, dkk_, dka = vjp(
            (dr[...] + dr2[...], dlw[...], dk2[...] + dk22[...], dv[...] + dv2[...], dkk[...], da[...], dg[...]))
        up = pltpu.roll(dpp, tm - 1, axis=0)
        rid = lax.broadcasted_iota(jnp.int32, dpp.shape, 0)
        dpa = dp + jnp.where(rid == tm - 1, carry[0:1, :], up)
        carry[...] = jnp.broadcast_to(dpp[0:1, :], carry.shape)
        _acc(dmu_ref, _colsum8(dmu), first)
        _acc(dw0_ref, _colsum8(dw0), first)
        _acc(dw2_ref, dw2, first)
        _acc(da0_ref, _colsum8(da0), first)
        _acc(da2_ref, da2, first)
        _acc(dg2_ref, dg2, first)
        _acc(dkk_ref, _colsum8(dkk_), first)
        _acc(dka_ref, _colsum8(dka), first)
        parts = [dpa] + [ref[pr] for ref in (dq_ref, dkq_ref, dvq_ref) for pr in range(N_PAIR)]
        dproj = jnp.concatenate([z.astype(BF16) for z in parts], axis=1)
        dproj_ref[...] = dproj
        dxn, dgr = _rms_bwd(_dot(dproj, w_ref[...]), x_ref[...], g1_ref[...])
        dx_ref[...] = dx1_ref[...] + dxn
        _acc(dg1_ref, _colsum8(dgr), first)

    rev = lambda i: (nt - 1 - i, 0)
    row = pl.BlockSpec((tm, RW), rev)
    wide = pl.BlockSpec((tm, D_MODEL), rev)
    pair = pl.BlockSpec((N_PAIR, tm, 128), lambda i: (0, nt - 1 - i, 0))
    part = lambda n: pl.BlockSpec((8, n), lambda i: (0, 0))
    mat = pl.BlockSpec((128, RW), lambda i: (0, 0))
    return pl.pallas_call(
        body, name="prep_in_proj_bwd", grid=(nt,),
        in_specs=[pl.BlockSpec((tm, SHIFT_COLS), rev),
                  pl.BlockSpec((8, SHIFT_COLS), lambda i: (jnp.maximum((nt - 1 - i) * (tm // 8) - 1, 0), 0))]
                 + _prep_specs(tm) + [row] * 10
                 + [pair] * 3 + [pl.BlockSpec((IN_COLS, D_MODEL), lambda i: (0, 0)), wide,
                                 pl.BlockSpec((1, D_MODEL), lambda i: (0, 0)), wide],
        out_specs=[pl.BlockSpec((tm, IN_COLS), rev), wide, part(D_MODEL), part(SHIFT_COLS), part(RW), mat, part(RW), mat,
                   mat, part(RW), part(RW)],
        out_shape=[jax.ShapeDtypeStruct((t, IN_COLS), BF16), jax.ShapeDtypeStruct((t, D_MODEL), F32),
                   jax.ShapeDtypeStruct((8, D_MODEL), F32), jax.ShapeDtypeStruct((8, SHIFT_COLS), F32),
                   jax.ShapeDtypeStruct((8, RW), F32), jax.ShapeDtypeStruct((128, RW), F32),
                   jax.ShapeDtypeStruct((8, RW), F32), jax.ShapeDtypeStruct((128, RW), F32),
                   jax.ShapeDtypeStruct((128, RW), F32), jax.ShapeDtypeStruct((8, RW), F32),
                   jax.ShapeDtypeStruct((8, RW), F32)],
        scratch_shapes=[pltpu.VMEM((8, SHIFT_COLS), F32)],
        compiler_params=_params(("arbitrary",)),
    )(proj, proj, *pw, *douts, dq, dk, dv, win, x, g1, dx1)


def _combine_bwd(dyb, o, l, og):
    t = dyb.shape[0]
    tm = _COMB_TM

    def body(d_ref, o_ref, l_ref, og_ref, do_ref, dl_ref, dog_ref):
        ones = jnp.ones((tm, 1), F32)
        dog = []
        for p in range(N_PAIR):
            cols = slice(128 * p, 128 * (p + 1))
            _, vjp = jax.vjp(_combine_fn, o_ref[0, p], o_ref[1, p], o_ref[2, p], l_ref[0, p], l_ref[1, p], l_ref[2, p],
                             ones * og_ref[:, cols])
            res = vjp(d_ref[:, cols])
            for b in range(3):
                do_ref[b, p] = res[b]
                dl_ref[b, p] = res[3 + b]
            dog.append(_colsum8(res[6]))
        _acc(dog_ref, jnp.concatenate(dog, axis=1), pl.program_id(0) == 0)

    blk = pl.BlockSpec((3, N_PAIR, tm, 128), lambda i: (0, 0, i, 0))
    return pl.pallas_call(
        body, name="attn_combine_bwd", grid=(t // tm,),
        in_specs=[pl.BlockSpec((tm, RW), lambda i: (i, 0)), blk, blk, pl.BlockSpec((1, RW), lambda i: (0, 0))],
        out_specs=[blk, blk, pl.BlockSpec((8, RW), lambda i: (0, 0))],
        out_shape=[jax.ShapeDtypeStruct((3, N_PAIR, t, 128), F32)] * 2 + [jax.ShapeDtypeStruct((8, RW), F32)],
        compiler_params=_params(("arbitrary",)),
    )(dyb, o, l, og)


def _attn_bwd(do, dl, o, lse, qkv):
    t = qkv.shape[2]

    def body(do_ref, dl_ref, o_ref, l_ref, q_ref, k_ref, v_ref, dq_ref, dk_ref, dv_ref):
        @pl.when(pl.program_id(1) == 0)
        def _():
            for ref in (dq_ref, dk_ref, dv_ref):
                ref[...] = jnp.zeros_like(ref)

        def unit(di, places, has_prev):
            cur = [_dilated_rows(d, r, n) for d, r, n in places]
            q, kc, vc = [_take(ref, (0,), cur) for ref in (q_ref, k_ref, v_ref)]
            kp = vp = None
            if has_prev:
                prv = [_dilated_rows(d, r, n - 1) for d, r, n in places]
                kp, vp = [_take(ref, (0,), prv) for ref in (k_ref, v_ref)]
            res = _attn_block_bwd(q, kc, vc, kp, vp, *[_take(ref, (0,), cur) for ref in (o_ref, l_ref, do_ref, dl_ref)])
            _put(dq_ref, (), cur, res[0], add=True)
            _put(dk_ref, (), cur, res[1], add=True)
            _put(dv_ref, (), cur, res[2], add=True)
            if has_prev:
                _put(dk_ref, (), prv, res[3], add=True)
                _put(dv_ref, (), prv, res[4], add=True)

        _for_each_sequence(t, unit)

    spec = lambda j: pl.BlockSpec((1, ATTN_GROUP, t, 128), lambda i, b: (j, i, 0, 0))
    branch = pl.BlockSpec((1, ATTN_GROUP, t, 128), lambda i, b: (b, i, 0, 0))
    out = pl.BlockSpec((ATTN_GROUP, t, 128), lambda i, b: (i, 0, 0))
    return pl.pallas_call(
        body, name="attn_bwd", grid=(N_PAIR // ATTN_GROUP, len(DILATIONS)),
        in_specs=[branch] * 4 + [spec(0), spec(1), spec(2)], out_specs=[out] * 3,
        out_shape=[jax.ShapeDtypeStruct((N_PAIR, t, 128), F32)] * 3,
        compiler_params=_params(("parallel", "arbitrary")),
    )(do, dl, o, lse, qkv, qkv, qkv)


def _unstack_lora(lora_all):
    def body(l_ref, w_ref, a_ref, g_ref):
        z = jnp.zeros((64, 128), F32)
        for p in range(N_CHIP):
            cols = slice(128 * p, 128 * (p + 1))
            w_ref[0:64, cols] = l_ref[p, 0:64, :]
            w_ref[64:128, cols] = z
            a_ref[0:64, cols] = z
            a_ref[64:128, cols] = l_ref[p, 64:128, :]
            g_ref[:, cols] = l_ref[p, 128:256, :]

    return pl.pallas_call(body, name="unstack_lora", out_shape=[jax.ShapeDtypeStruct((128, RW), F32)] * 3)(lora_all)


def _local_step(x, tgt, win, vecs, w2p, a2p, g2m, get_rest, send_rest):
    pw = (vecs["mu_shift"], vecs["decay_w0"], w2p, vecs["iclr_a0"], a2p, g2m, vecs["k_k"], vecs["k_a"])
    h, proj, qkv, r, lw, k2, v, kk, a, g = _in_proj_prep(x, vecs["mix_norm_g"], win, pw)
    y, s0s = _wkv_fwd(r, lw, k2, v, kk, a)
    o_att, l_att = _attn_fwd(qkv)
    ycat = _mixers_out(y, r, k2, v, g, vecs["ln_x_w"], vecs["ln_x_b"], vecs["r_k"], o_att, l_att, vecs["attn_out_g"])
    wout, wg, wu, wd = get_rest(ycat)
    h2, act, dx2b, dgt, dup, dx1b, dx1, dya, dyb, loss8, dgf, dg2n = _ffn_all(
        x, ycat, wg, wu, wd, wout, vecs["ffn_norm_g"], vecs["final_norm_g"], tgt)
    gw = {
        "w_down": _wgrad(act, dx2b, 1408, 1024, "wgrad_down"),
        "w_gate": _wgrad(dgt, h2, 1408, 1024, "wgrad_gate"),
        "w_up": _wgrad(dup, h2, 1408, 1024, "wgrad_up"),
        "w_out": _wgrad(ycat, dx1b, 1024, 1024, "wgrad_out"),
    }

    dy, dr_p, dk2_p, dv_p, dg, dlnw, dlnb, drk = _post_bwd(dya, y, r, k2, v, g, vecs["ln_x_w"], vecs["ln_x_b"], vecs["r_k"],
                                                           after=send_rest(gw))
    dr_s, dlw, dk2_s, dv_s, dkk, da = _wkv_bwd(dy, s0s, r, lw, k2, v, kk, a)
    do_att, dl_att, dog = _combine_bwd(dyb, o_att, l_att, vecs["attn_out_g"])
    dq, dk, dv = _attn_bwd(do_att, dl_att, o_att, l_att, qkv)
    dproj, dx, dg1, dmu, dw0, dw2p, da0, da2p, dg2m, dk_k, dk_a = _prep_in_proj_bwd(
        proj, pw, (dr_p, dr_s, dlw, dk2_p, dk2_s, dv_p, dv_s, dkk, da, dg), dq, dk, dv, win, x, vecs["mix_norm_g"], dx1)
    gw["w_in"] = _wgrad(dproj, h, 1664, 1024, "wgrad_in")
    gw["decay_w2"] = dw2p[:64]
    gw["iclr_a2"] = da2p[64:]
    gw["gate_g2"] = dg2m
    gv = {"mix_norm_g": dg1, "mu_shift": dmu, "decay_w0": dw0, "iclr_a0": da0, "k_k": dk_k, "k_a": dk_a, "r_k": drk,
          "ln_x_w": dlnw, "ln_x_b": dlnb, "attn_out_g": dog, "ffn_norm_g": dg2n, "final_norm_g": dgf}
    return loss8, dx, gw, gv


N_CHIP = 4
N_DEV = 8
MATS = ("w_in", "w_out", "w_gate", "w_up", "w_down")
LORAS = ("decay_w2", "iclr_a2", "gate_g2")
VECS = (("mix_norm_g", 1024), ("mu_shift", 1792), ("decay_w0", 512), ("iclr_a0", 512), ("k_k", 512), ("k_a", 512),
        ("r_k", 512), ("ln_x_w", 512), ("ln_x_b", 512), ("attn_out_g", 512), ("ffn_norm_g", 1024),
        ("final_norm_g", 1024))
N_VEC = sum(n for _, n in VECS)
N_SMALL = N_VEC + 128
ANY = pl.BlockSpec(memory_space=pl.ANY)


def _flip(v, f):
    return 1 - v if f else v


class _Me:
    def __init__(self, mode):
        x, y, c = lax.axis_index("x"), lax.axis_index("y"), lax.axis_index("c")
        self.core, self.chip, self.dev = c, 2 * x + y, 4 * x + 2 * y + c
        self.sibling = (x, y, 1 - c)
        if mode == "chips":
            self.peers = [(px, py, c) for px, py in ((1 - x, y), (x, 1 - y), (1 - x, 1 - y))]
        else:
            self.peers = [(_flip(x, k & 4), _flip(y, k & 2), _flip(c, k & 1)) for k in range(1, N_DEV)]


def _half(core, rows):
    h = rows // 2
    return pl.ds(pl.multiple_of(core * h, h), h)


_BY_CHIP = ("gather", "whole", "chipsum")


def _peer_copy(srcs, dsts, kinds, send_sems, recv_sems, me, j, i, incoming):
    px, py, pc = me.peers[j]
    pchip, pdev = 2 * px + py, 4 * px + 2 * py + pc
    src, dst, kind = srcs[i], dsts[i], kinds[i]
    if kind in ("gather", "whole"):
        rows = _half(me.core, src.shape[1]) if kind == "gather" else pl.ds(0, src.shape[1])
        src, dst = src.at[me.chip, rows], dst.at[pchip if incoming else me.chip, rows]
    elif kind == "scatter":
        src, dst = src.at[pchip, _half(pc, src.shape[1])], dst.at[pdev if incoming else me.dev]
    elif kind == "chipsum":
        src, dst = src.at[pchip], dst.at[pchip if incoming else me.chip]
    else:
        dst = dst.at[pdev if incoming else me.dev]
    n = len(srcs)
    return pltpu.make_async_remote_copy(src_ref=src, dst_ref=dst, send_sem=send_sems.at[n * j + i],
                                        recv_sem=recv_sems.at[n * j + i], device_id=(px, py, pc), device_id_type=MESH)


def _mode(kinds):
    return "chips" if kinds[0] in _BY_CHIP else "devs"


def _npeer(kinds):
    return N_CHIP - 1 if kinds[0] in _BY_CHIP else N_DEV - 1


def _sibling_halves(gs, name):
    n = len(gs)

    def body(*refs):
        srcs, dsts, send_sems, recv_sems = refs[:n], refs[n:2 * n], refs[2 * n], refs[2 * n + 1]
        me = _Me("chips")

        def copy(i, p):
            return pltpu.make_async_remote_copy(
                src_ref=srcs[i].at[p, _half(1 - me.core, srcs[i].shape[1])], dst_ref=dsts[i].at[p],
                send_sem=send_sems.at[N_CHIP * i + p], recv_sem=recv_sems.at[N_CHIP * i + p],
                device_id=me.sibling, device_id_type=MESH)

        copies = [copy(i, p) for i in range(n) for p in range(N_CHIP)]
        for cp in copies:
            cp.start()
        for cp in copies:
            cp.wait()

    return pl.pallas_call(
        body, name=name, in_specs=[ANY] * n, out_specs=[ANY] * n,
        out_shape=[jax.ShapeDtypeStruct((N_CHIP, g.shape[1] // 2, g.shape[2]), g.dtype) for g in gs],
        scratch_shapes=[pltpu.SemaphoreType.DMA((N_CHIP * n,)), pltpu.SemaphoreType.DMA((N_CHIP * n,))],
    )(*gs)


def _add_halves(g, other, core, tr, name):
    _, h, cols = other.shape

    def body(core_ref, g_ref, o_ref, out_ref):
        out_ref[...] = (g_ref[...].astype(F32) + o_ref[...].astype(F32)).astype(BF16)

    blk = lambda off: pl.BlockSpec((1, tr, cols), lambda p, i, core_ref: (p, core_ref[0] * (h // tr) * off + i, 0))
    return pl.pallas_call(
        body, name=name,
        grid_spec=pltpu.PrefetchScalarGridSpec(num_scalar_prefetch=1, grid=(N_CHIP, h // tr),
                                               in_specs=[blk(1), blk(0)], out_specs=blk(0)),
        out_shape=jax.ShapeDtypeStruct(other.shape, BF16),
        compiler_params=_params(("parallel", "parallel")),
    )(core, g, other)


def _swap_gathered(lands, name):
    n = len(lands)

    def body(*refs):
        dsts, send_sems, recv_sems = refs[n:2 * n], refs[2 * n], refs[2 * n + 1]
        me = _Me("chips")

        def copy(j, i, incoming):
            px, py, _ = me.peers[j]
            rows_out, rows_in = _half(me.core, dsts[i].shape[1]), _half(1 - me.core, dsts[i].shape[1])
            return pltpu.make_async_remote_copy(
                src_ref=dsts[i].at[2 * px + py, rows_out], dst_ref=dsts[i].at[2 * px + py, rows_in if incoming else rows_out],
                send_sem=send_sems.at[n * j + i], recv_sem=recv_sems.at[n * j + i], device_id=me.sibling, device_id_type=MESH)

        sends = [copy(j, i, False) for j in range(3) for i in range(n)]
        for cp in sends:
            cp.start()
        for j in range(3):
            for i in range(n):
                copy(j, i, True).wait_recv()
        for cp in sends:
            cp.wait_send()

    return pl.pallas_call(
        body, name=name, in_specs=[ANY] * n, out_specs=[ANY] * n,
        out_shape=[jax.ShapeDtypeStruct(l.shape, l.dtype) for l in lands],
        input_output_aliases={i: i for i in range(n)},
        scratch_shapes=[pltpu.SemaphoreType.DMA((3 * n,)), pltpu.SemaphoreType.DMA((3 * n,))],
    )(*lands)


def _join_halves(sums, name):
    n = len(sums)

    def body(*refs):
        dsts, send_sems, recv_sems = refs[n:2 * n], refs[2 * n], refs[2 * n + 1]
        me = _Me("chips")

        def copy(i, incoming):
            mine, other = _half(me.core, dsts[i].shape[0]), _half(1 - me.core, dsts[i].shape[0])
            return pltpu.make_async_remote_copy(src_ref=dsts[i].at[mine], dst_ref=dsts[i].at[other if incoming else mine],
                                                send_sem=send_sems.at[i], recv_sem=recv_sems.at[i],
                                                device_id=me.sibling, device_id_type=MESH)

        sends = [copy(i, False) for i in range(n)]
        for cp in sends:
            cp.start()
        for i in range(n):
            copy(i, True).wait_recv()
        for cp in sends:
            cp.wait_send()

    return pl.pallas_call(
        body, name=name, in_specs=[ANY] * n, out_specs=[ANY] * n,
        out_shape=[jax.ShapeDtypeStruct(s.shape, s.dtype) for s in sums],
        input_output_aliases={i: i for i in range(n)},
        scratch_shapes=[pltpu.SemaphoreType.DMA((n,)), pltpu.SemaphoreType.DMA((n,))],
    )(*sums)


HBM = pl.BlockSpec(memory_space=pltpu.HBM)
SEM = pl.BlockSpec(memory_space=pltpu.SEMAPHORE)
EFFECT = pltpu.SideEffectType.DATAFLOW_SIDE_EFFECTING


def _swap_start(arrs, lands, kinds, name):
    n = len(lands)
    ops = list(lands) if arrs is None else [*arrs, *lands]
    k = len(ops)

    def body(*refs):
        srcs, dsts, send_sems, recv_sems, token = refs[:n], refs[k - n:k], refs[k], refs[k + 1], refs[-1]
        me = _Me(_mode(kinds))
        for j in range(len(me.peers)):
            for i in range(n):
                _peer_copy(srcs, dsts, kinds, send_sems, recv_sems, me, j, i, False).start()
        token[...] = jnp.zeros_like(token)

    ns = _npeer(kinds) * n
    outs = pl.pallas_call(
        body, name=name,
        out_shape=(pltpu.SemaphoreType.DMA((ns,)), pltpu.SemaphoreType.DMA((ns,)),
                   *[pltpu.HBM(a.shape, a.dtype) for a in ops], jax.ShapeDtypeStruct((8, 128), F32)),
        in_specs=[HBM] * k, out_specs=(SEM, SEM, *[HBM] * k, pl.BlockSpec(memory_space=pltpu.VMEM)),
        input_output_aliases={i: 2 + i for i in range(k)},
        compiler_params=pltpu.CompilerParams(has_side_effects=EFFECT),
    )(*[pltpu.with_memory_space_constraint(a, pltpu.HBM) for a in ops])
    return outs[0], outs[1], outs[2:2 + k - n], outs[2 + k - n:2 + k], outs[-1]


def _swap_wait(send_sems, recv_sems, srcs_thru, lands_thru, after, kinds, name):
    n = len(lands_thru)
    ops = [*srcs_thru, *lands_thru]
    k = len(ops)

    def body(*refs):
        srcs, dsts, s_sems, r_sems = refs[:n], refs[k - n:k], refs[k], refs[k + 1]
        me = _Me(_mode(kinds))
        for j in range(len(me.peers)):
            for i in range(n):
                cp = _peer_copy(srcs, dsts, kinds, s_sems, r_sems, me, j, i, True)
                cp.wait_send()
                cp.wait_recv()

    outs = pl.pallas_call(
        body, name=name,
        out_shape=tuple(pltpu.HBM(a.shape, a.dtype) for a in ops),
        in_specs=[HBM] * k + [SEM, SEM, ANY], out_specs=tuple([HBM] * k),
        input_output_aliases={i: i for i in range(k)},
        compiler_params=pltpu.CompilerParams(has_side_effects=EFFECT),
    )(*ops, send_sems, recv_sems, after)
    return outs[k - n:]


def _adamw(w, g, m, v):
    m = ADAM_B1 * m + (1.0 - ADAM_B1) * g
    v = ADAM_B2 * v + (1.0 - ADAM_B2) * (g * g)
    m_hat = m / (1.0 - ADAM_B1 ** ADAM_STEP)
    v_hat = v / (1.0 - ADAM_B2 ** ADAM_STEP)
    delta = -ADAM_LR * (m_hat / (jnp.sqrt(v_hat) + ADAM_EPS) + ADAM_WD * w)
    return delta, m, v


def _reduce8(rbuf, core, tr, name):
    slots, h, cols = rbuf.shape

    def body(core_ref, r_ref, g_ref):
        g = r_ref[0].astype(F32)
        for s in range(1, slots):
            g = g + r_ref[s].astype(F32)
        g_ref[...] = g

    return pl.pallas_call(
        body, name=name,
        grid_spec=pltpu.PrefetchScalarGridSpec(
            num_scalar_prefetch=1, grid=(h // tr,),
            in_specs=[pl.BlockSpec((slots, tr, cols), lambda i, core_ref: (0, i, 0))],
            out_specs=pl.BlockSpec((tr, cols), lambda i, core_ref: (core_ref[0] * (h // tr) + i, 0))),
        out_shape=jax.ShapeDtypeStruct((2 * h, cols), F32),
        compiler_params=_params(("parallel",)),
    )(core, rbuf)


def _adamw_call(g, w, m, v, tr, name):
    _, rows, cols = w.shape

    def body(g_in, w_ref, m_ref, v_ref, g_ref, d_ref, nm_ref, nv_ref):
        g = g_in[...]
        g_ref[0] = g
        d_ref[0], nm_ref[0], nv_ref[0] = _adamw(w_ref[0], g, m_ref[0], v_ref[0])

    row = pl.BlockSpec((1, tr, cols), lambda i: (0, i, 0))
    return pl.pallas_call(
        body, name=name, grid=(rows // tr,),
        in_specs=[pl.BlockSpec((tr, cols), lambda i: (i, 0)), row, row, row], out_specs=[row] * 4,
        out_shape=[jax.ShapeDtypeStruct(w.shape, F32)] * 4,
        compiler_params=_params(("parallel",)),
    )(g, w, m, v)


def _adamw_lora(g, ws3, ms3, vs3):
    def body(g_in, *refs):
        ins, outs = refs[:9], refs[9:]
        r0 = 0
        for i in range(3):
            rows = ins[i].shape[1]
            g = g_in[r0:r0 + rows, :]
            outs[i][0] = g
            outs[3 + i][0], outs[6 + i][0], outs[9 + i][0] = _adamw(ins[i][0], g, ins[3 + i][0], ins[6 + i][0])
            r0 += rows

    return pl.pallas_call(body, name="adamw_lora",
                          out_shape=[jax.ShapeDtypeStruct(a.shape, F32) for a in ws3] * 4)(g, *ws3, *ms3, *vs3)


def _rowsum_small(parts, loss8, after):
    def body(*refs):
        out = refs[-1]
        c0 = 0
        for ref in refs[:-2]:
            n = ref.shape[1]
            out[:, c0:c0 + n] = jnp.sum(ref[...], axis=0, keepdims=True)
            c0 += n

    k = len(parts) + 1
    return pl.pallas_call(body, name="rowsum_small", in_specs=[pl.BlockSpec(memory_space=pltpu.VMEM)] * k + [ANY],
                          out_shape=jax.ShapeDtypeStruct((1, N_SMALL), F32))(*parts, loss8, after)


def _reduce_adamw_small(sbuf, ws, ms, vs):
    nv = len(ws)

    def body(*refs):
        s_ref, ins, outs = refs[0], refs[1:1 + 3 * nv], refs[1 + 3 * nv:]
        tot = s_ref[0]
        for s in range(1, N_DEV):
            tot = tot + s_ref[s]
        c0 = 0
        for i in range(nv):
            rows, cols = ins[i].shape
            n = rows * cols
            for r in range(rows):
                outs[i][r:r + 1, :] = tot[:, c0 + cols * r:c0 + cols * (r + 1)]
            g = outs[i][...]
            outs[nv + i][...], outs[2 * nv + i][...], outs[3 * nv + i][...] = _adamw(
                ins[i][...], g, ins[nv + i][...], ins[2 * nv + i][...])
            c0 += n
        outs[-1][...] = tot[:, c0:]

    return pl.pallas_call(
        body, name="reduce_adamw_small",
        out_shape=[jax.ShapeDtypeStruct(a.shape, F32) for a in ws] * 4 + [jax.ShapeDtypeStruct((1, 128), F32)],
    )(sbuf, *ws, *ms, *vs)


_TRANSPOSED = ("w_in", "w_gate", "w_up")
_ROW_STACKED = MATS
_ADAM_TILE = {"w_in": 208, "w_out": 256, "w_gate": 176, "w_up": 176, "w_down": 176, "lora": 256}
_SUM_TILE = {"w_in": 208, "w_out": 128, "w_gate": 176, "w_up": 176, "w_down": 176, "lora": 128}


def _full(n, stacked):
    p, r, c = stacked.shape
    if n in _ROW_STACKED:
        return stacked.reshape(p * r, c)
    return jnp.transpose(stacked, (1, 0, 2)).reshape(r, p * c)


def _by_chip(n, full):
    if n in _ROW_STACKED:
        return full.reshape(N_CHIP, full.shape[0] // N_CHIP, full.shape[1])
    r, c = full.shape
    return jnp.transpose(full.reshape(r, N_CHIP, c // N_CHIP), (1, 0, 2))


def _with_own(land_shape, dtype, own, slot):
    return lax.dynamic_update_slice(lax.empty(land_shape, dtype), own[None], (slot,) + (0,) * own.ndim)


def _cast_into_slot(a, chip, tr, name, after=None):
    rows, cols = a.shape

    def body(chip_ref, a_ref, *rest):
        rest[-1][0] = a_ref[...].astype(BF16)

    extra = [] if after is None else [after]
    return pl.pallas_call(
        body, name=name,
        grid_spec=pltpu.PrefetchScalarGridSpec(
            num_scalar_prefetch=1, grid=(rows // tr,),
            in_specs=[pl.BlockSpec((tr, cols), lambda i, chip_ref: (i, 0))] + [ANY] * len(extra),
            out_specs=pl.BlockSpec((1, tr, cols), lambda i, chip_ref: (chip_ref[0], i, 0))),
        out_shape=jax.ShapeDtypeStruct((N_CHIP, rows, cols), BF16),
        compiler_params=_params(("parallel",)),
    )(chip, a, *extra)


def kernel(x, mix_norm_g, w_in, mu_shift, decay_w0, decay_w2, iclr_a0, iclr_a2, gate_g2, k_k, k_a, r_k, ln_x_w, ln_x_b, attn_out_g, w_out, ffn_norm_g, w_gate, w_up, w_down, final_norm_g, loss_target, m_mix_norm_g, m_w_in, m_mu_shift, m_decay_w0, m_decay_w2, m_iclr_a0, m_iclr_a2, m_gate_g2, m_k_k, m_k_a, m_r_k, m_ln_x_w, m_ln_x_b, m_attn_out_g, m_w_out, m_ffn_norm_g, m_w_gate, m_w_up, m_w_down, m_final_norm_g, v_mix_norm_g, v_w_in, v_mu_shift, v_decay_w0, v_decay_w2, v_iclr_a0, v_iclr_a2, v_gate_g2, v_k_k, v_k_a, v_r_k, v_ln_x_w, v_ln_x_b, v_attn_out_g, v_w_out, v_ffn_norm_g, v_w_gate, v_w_up, v_w_down, v_final_norm_g):
    names = ("mix_norm_g", "w_in", "mu_shift", "decay_w0", "decay_w2", "iclr_a0", "iclr_a2", "gate_g2", "k_k", "k_a",
             "r_k", "ln_x_w", "ln_x_b", "attn_out_g", "w_out", "ffn_norm_g", "w_gate", "w_up", "w_down", "final_norm_g")
    w = dict(zip(names, (mix_norm_g, w_in, mu_shift, decay_w0, decay_w2, iclr_a0, iclr_a2, gate_g2, k_k, k_a, r_k,
                         ln_x_w, ln_x_b, attn_out_g, w_out, ffn_norm_g, w_gate, w_up, w_down, final_norm_g)))
    m = dict(zip(names, (m_mix_norm_g, m_w_in, m_mu_shift, m_decay_w0, m_decay_w2, m_iclr_a0, m_iclr_a2, m_gate_g2,
                         m_k_k, m_k_a, m_r_k, m_ln_x_w, m_ln_x_b, m_attn_out_g, m_w_out, m_ffn_norm_g, m_w_gate,
                         m_w_up, m_w_down, m_final_norm_g)))
    v = dict(zip(names, (v_mix_norm_g, v_w_in, v_mu_shift, v_decay_w0, v_decay_w2, v_iclr_a0, v_iclr_a2, v_gate_g2,
                         v_k_k, v_k_a, v_r_k, v_ln_x_w, v_ln_x_b, v_attn_out_g, v_w_out, v_ffn_norm_g, v_w_gate,
                         v_w_up, v_w_down, v_final_norm_g)))
    first = ("w_in", "lora")
    rest = ("w_out", "w_gate", "w_up", "w_down")
    xi, yi, ci = lax.axis_index("x"), lax.axis_index("y"), lax.axis_index("c")
    my_chip, my_dev = 2 * xi + yi, 4 * xi + 2 * yi + ci
    gather, scatter = ("gather",) * 4, ("scatter",) * 4

    def stored(d):
        out = {n: jnp.transpose(d[n][0]) if n in _TRANSPOSED else d[n][0] for n in MATS}
        out["lora"] = jnp.concatenate([d[n][0] for n in LORAS], axis=0)
        return out

    ws, ms, vs = stored(w), stored(m), stored(v)
    chip = jnp.reshape(my_chip, (1,)).astype(jnp.int32)
    early = _swap_start(None, [_cast_into_slot(ws["w_in"], chip, _ADAM_TILE["w_in"], "cast_w_in"),
                               _with_own((N_CHIP,) + ws["lora"].shape, F32, ws["lora"], my_chip)], gather[:2],
                        "gather_first_start")
    lands = [_cast_into_slot(ws[n], chip, _ADAM_TILE[n], "cast_" + n, after=early[4]) for n in rest]
    gather_rest = ("gather", "gather", "whole", "whole")
    ssem, rsem, srcs_thru, lands_thru, tok = _swap_start(None, lands, gather_rest, "gather_rest_start")
    got = _swap_wait(early[0], early[1], early[2], early[3], tok, gather[:2], "gather_first_wait")
    win_all, lora_all = _swap_gathered(got, "gather_first_halves")
    win = _full("w_in", win_all)
    w2p, a2p, g2m = _unstack_lora(lora_all)

    vecs = {n: w[n].reshape(1, sz) for n, sz in VECS}

    def get_rest(after):
        got_rest = _swap_wait(ssem, rsem, srcs_thru, lands_thru, after, gather_rest, "gather_rest_wait")
        swapped = _swap_gathered(got_rest[:2], "gather_rest_halves")
        return [_full(n, z) for n, z in zip(rest, [*swapped, *got_rest[2:]])]

    flight = []

    def my_half(g):
        h = g.shape[1] // 2
        return lax.dynamic_slice(g, (my_chip, ci * h, 0), (1, h, g.shape[2]))[0]

    def send_rest(gw):
        gs = [_by_chip(n, gw[n]) for n in rest]
        into = [_with_own((N_DEV,) + my_half(g).shape, BF16, my_half(g), my_dev) for g in gs]
        flight.extend(_swap_start(gs, into, scatter, "exchange_rest_start"))
        return flight[4]

    loss8, dx, gw, gv = _local_step(x[0], loss_target[0], win, vecs, w2p, a2p, g2m, get_rest, send_rest)

    core = jnp.reshape(ci, (1,)).astype(jnp.int32)
    gs = [_by_chip("w_in", gw["w_in"]),
          jnp.concatenate([_by_chip(n, gw[n]) for n in LORAS], axis=1).astype(BF16)]
    theirs = _sibling_halves(gs, "presum_halves")
    sums = [_add_halves(g, o, core, _SUM_TILE[n], "chipsum_" + n) for n, g, o in zip(first, gs, theirs)]
    own = [lax.dynamic_index_in_dim(s, my_chip, 0, keepdims=False) for s in sums]
    last = _swap_start(sums, [_with_own(s.shape, BF16, o, my_chip) for s, o in zip(sums, own)], ("chipsum",) * 2,
                       "exchange_first_start")
    small = _rowsum_small([gv[n] for n, _ in VECS], loss8, after=last[4])
    vecs_out = _swap_start([small], [_with_own((N_DEV,) + small.shape, F32, small, my_dev)], ("all",),
                           "exchange_vectors_start")


    def update(group, rbufs, tag):
        sums = [_reduce8(rb, core, _SUM_TILE[n], "reduce_" + n) for n, rb in zip(group, rbufs)]
        gsum = _join_halves(sums, "join_halves_" + tag)
        out = {}
        for n, g in zip(group, gsum):
            if n == "lora":
                r = _adamw_lora(g, *[[d[k] for k in LORAS] for d in (w, m, v)])
                for i, name in enumerate(LORAS):
                    out[name] = r[i::3]
            else:
                r = _adamw_call(g, ws[n][None], ms[n][None], vs[n][None], _ADAM_TILE[n], "adamw_" + n)
                out[n] = [jnp.transpose(z[0])[None] for z in r] if n in _TRANSPOSED else r
        return out, r[1]

    res, done = update(rest, _swap_wait(flight[0], flight[1], flight[2], flight[3], vecs_out[4], scatter,
                                        "exchange_rest_wait"), "rest")
    got = _swap_wait(last[0], last[1], last[2], last[3], done, ("chipsum",) * 2, "exchange_first_wait")
    res_first, done = update(first, got, "first")
    res.update(res_first)
    sbuf = _swap_wait(vecs_out[0], vecs_out[1], vecs_out[2], vecs_out[3], done, ("all",), "exchange_vectors_wait")[0]
    rows = lambda d: [d[n].reshape(-1, d[n].shape[-1]) for n, _ in VECS]
    small_res = _reduce_adamw_small(sbuf, rows(w), rows(m), rows(v))

    outs = []
    for k in range(4):
        piece = {n: r[k] for n, r in res.items()}
        for i, (n, _) in enumerate(VECS):
            piece[n] = small_res[k * len(VECS) + i].reshape(w[n].shape)
        outs.extend(piece[n] for n in names)
    return (small_res[-1][0, 0], dx[None], *outs)
```

```python
import jax
import jax.numpy as jnp
from jax import lax
from jax.experimental import pallas as pl
from jax.experimental.pallas import tpu as pltpu

F32 = jnp.float32
BF16 = jnp.bfloat16

D_MODEL = 1024
HEAD_DIM = 64
RW = 512
N_PAIR = RW // 128
SHIFT_COLS = 1792
IN_COLS = 3328
D_FF = 2816
FF_CHUNK = 256
NORM_EPS = 1e-6
GN_EPS = 64e-5
CHUNK = 64
SUB = 16
WKV_PASSES = 1
ATTN_PASSES = 1
ATTN_BLOCK = 128
DILATIONS = (1, 4, 16)
NEG = -1e30
ADAM_LR, ADAM_B1, ADAM_B2, ADAM_EPS, ADAM_WD, ADAM_STEP = 0.001, 0.9, 0.999, 1e-08, 0.01, 10
VMEM_LIMIT = 56 * 1024 * 1024
MESH = pl.DeviceIdType.MESH


def _params(sem=None, **kw):
    return pltpu.CompilerParams(dimension_semantics=sem, vmem_limit_bytes=VMEM_LIMIT, **kw)


def _dot(a, b):
    return lax.dot_general(a, b, (((1,), (0,)), ((), ())), preferred_element_type=F32)


def _dot_nt(a, b):
    return lax.dot_general(a, b, (((1,), (1,)), ((), ())), preferred_element_type=F32)


def _dot_tn(a, b):
    return lax.dot_general(a, b, (((0,), (0,)), ((), ())), preferred_element_type=F32)


_FORMS = {"nn": ((1,), (0,)), "nt": ((1,), (1,)), "tn": ((0,), (0,))}


def _dg(a, b, form):
    if a.ndim == 3 or b.ndim == 3:
        nb = a.shape[0] if a.ndim == 3 else b.shape[0]
        return jnp.stack([_dg(a[i] if a.ndim == 3 else a, b[i] if b.ndim == 3 else b, form) for i in range(nb)], axis=0)
    return lax.dot_general(a, b, (_FORMS[form], ((), ())), preferred_element_type=F32)


def _split2(x):
    hi = x.astype(BF16)
    return hi, (x - hi.astype(F32)).astype(BF16)


def _split3(x):
    hi = x.astype(BF16)
    rest = x - hi.astype(F32)
    mid = rest.astype(BF16)
    return hi, mid, (rest - mid.astype(F32)).astype(BF16)


def _mm_raw(a, b, form, mode):
    if mode == 1:
        return _dg(a.astype(BF16), b.astype(BF16), form)
    if mode == 3:
        ah, al = _split2(a)
        bh, bl = _split2(b)
        return _dg(ah, bh, form) + (_dg(ah, bl, form) + _dg(al, bh, form))
    if mode == "L3":
        ab = a.astype(BF16)
        b1, b2, b3 = _split3(b)
        if form == "nn":
            n = b.shape[-1]
            wide = _dg(ab, jnp.concatenate([b1, b2, b3], axis=-1), form)
            return wide[..., :n] + (wide[..., n:2 * n] + wide[..., 2 * n:])
        return _dg(ab, b1, form) + (_dg(ab, b2, form) + _dg(ab, b3, form))
    assert mode == "R3", mode
    bb = b.astype(BF16)
    a1, a2, a3 = _split3(a)
    if form in ("nn", "nt"):
        m = a.shape[-2]
        tall = _dg(jnp.concatenate([a1, a2, a3], axis=-2), bb, form)
        return tall[..., :m, :] + (tall[..., m:2 * m, :] + tall[..., 2 * m:, :])
    return _dg(a1, bb, form) + (_dg(a2, bb, form) + _dg(a3, bb, form))


def _mm(a, b, form, mode):
    @jax.custom_vjp
    def f(a, b):
        return _mm_raw(a, b, form, mode)

    def fwd(a, b):
        return _mm_raw(a, b, form, mode), (a, b)

    def bwd(res, ct):
        a, b = res
        la = {1: 1, 3: 3, "L3": None, "R3": "R3"}[mode]
        lb = {1: 1, 3: 3, "L3": "L3", "R3": None}[mode]
        if form == "nn":
            da = None if la is None else _mm_raw(ct, b, "nt", la)
            db = None if lb is None else _mm_raw(a, ct, "tn", lb)
        elif form == "nt":
            da = None if la is None else _mm_raw(ct, b, "nn", la)
            db = None if lb is None else _mm_raw(ct, a, "tn", "R3" if lb == "L3" else lb)
        else:
            da = None if la is None else _mm_raw(b, ct, "nt", "L3" if la == "R3" else la)
            db = None if lb is None else _mm_raw(a, ct, "nn", lb)
        return (jnp.zeros_like(a) if da is None else da, jnp.zeros_like(b) if db is None else db)

    f.defvjp(fwd, bwd)
    return f(a, b)


def _seg_ones(n):
    r = lax.broadcasted_iota(jnp.int32, (n, n), 0) // HEAD_DIM
    c = lax.broadcasted_iota(jnp.int32, (n, n), 1) // HEAD_DIM
    return (r == c).astype(F32)


def _segsum(x, seg):
    return _mm(x, seg, "nn", "R3")


def _rms_fwd(x, g):
    rstd = lax.rsqrt(jnp.mean(x * x, axis=-1, keepdims=True) + NORM_EPS)
    return x * rstd * g


def _rms_bwd(dy, x, g):
    rstd = lax.rsqrt(jnp.mean(x * x, axis=-1, keepdims=True) + NORM_EPS)
    xn = x * rstd
    dxn = dy * g
    dx = rstd * (dxn - xn * jnp.mean(dxn * xn, axis=-1, keepdims=True))
    return dx, dy * xn


def _sigmoid(x):
    return 1.0 / (1.0 + jnp.exp(-x))


def _softplus(x):
    return jnp.maximum(x, 0.0) + jnp.log(1.0 + jnp.exp(-jnp.abs(x)))


def _acc(ref, val, first):
    @pl.when(first)
    def _():
        ref[...] = val

    @pl.when(jnp.logical_not(first))
    def _():
        ref[...] += val


def _colsum8(v):
    rows, n = v.shape
    return jnp.sum(v.reshape(rows // 8, 8, n), axis=0)


def _prep_fn(p, pprev, mu, w0, w2p, a0, a2p, g2, k_k, k_a):
    seg = _seg_ones(RW)
    ps = p + (pprev - p) * mu
    r = ps[:, 0:RW]
    k = ps[:, RW:2 * RW]
    v = ps[:, 2 * RW:3 * RW]
    xwa = ps[:, 3 * RW:3 * RW + 128]
    xg = ps[:, 3 * RW + 128:3 * RW + 256]
    wraw = -_softplus(-(w0 + _mm(jnp.tanh(xwa), w2p, "nn", 3))) - 0.5
    lw = -jnp.exp(wraw)
    a = _sigmoid(a0 + _mm(xwa, a2p, "nn", 3))
    g = _mm(_sigmoid(xg), g2, "nn", 3)
    kk = k * k_k
    kk = kk / jnp.maximum(jnp.sqrt(_segsum(kk * kk, seg)), 1e-12)
    k2 = k * (1.0 + (a - 1.0) * k_a)
    return r, lw, k2, v, kk, a, g


def _transposed(z):
    return jnp.stack([z[i].T for i in range(z.shape[0])], axis=0) if z.ndim == 3 else z.T


def _solve_unit_lower(lmat, rhs):
    c = lmat.shape[-1]
    row = lax.broadcasted_iota(jnp.int32, (c, c), 0)
    col = lax.broadcasted_iota(jnp.int32, (c, c), 1)
    eye = (row == col).astype(F32)
    ld = jnp.where(row // SUB == col // SUB, lmat, 0.0)
    lo = lmat - ld
    x = eye + ld
    m = ld
    mm = lambda p, q: _mm(p, q, "nn", WKV_PASSES)
    cat = jnp.concatenate
    m = mm(m, m)
    for _ in range(2):
        mx = mm(m, cat([m, x], axis=-1))
        m, x = mx[..., :c], x + mx[..., c:]
    x = x + mm(m, x)
    gw = mm(x, cat([lo, rhs], axis=-1))
    g, w = gw[..., :c], gw[..., c:]
    gg = mm(g, cat([g, w], axis=-1))
    w = w + gg[..., c:]
    return w + mm(gg[..., :c], w)


def _wkv_chunk_fn(s0, r, lw, k, v, kk, a):
    c = r.shape[-2]
    n = 2 * c
    row = lax.broadcasted_iota(jnp.int32, (n, n), 0)
    col = lax.broadcasted_iota(jnp.int32, (n, n), 1)
    same = (row // c) == (col // c)
    incl = jnp.logical_and(row >= col, same)
    strict = jnp.logical_and(row > col, same)
    sel = (lax.broadcasted_iota(jnp.int32, (n, 128), 0) // c) == (lax.broadcasted_iota(jnp.int32, (n, 128), 1) // HEAD_DIM)
    two = lambda z: jnp.concatenate([z, z], axis=-2)
    lw2 = two(lw)
    mm = lambda p_, q_, form: _mm(p_, q_, form, WKV_PASSES)
    cl = _mm(incl.astype(F32), lw2, "nn", "L3")
    p = jnp.exp(cl)
    pinv = jnp.exp(-cl)
    pprev = jnp.exp(cl - lw2)
    kk2 = two(kk)
    at = jnp.where(sel, -kk2 * pprev, 0.0)
    bt = jnp.where(sel, kk2 * two(a) * pinv, 0.0)
    kt = jnp.where(sel, two(k) * pinv, 0.0)
    rt = jnp.where(sel, two(r) * p, 0.0)
    vt = jnp.where(sel, two(v), 0.0)
    cat = jnp.concatenate
    bk = cat([bt, kt], axis=-2)
    arbk = mm(cat([at, rt], axis=-2), bk, "nt")
    ab, ak = jnp.where(strict, arbk[..., :n, :n], 0.0), jnp.where(strict, arbk[..., :n, n:], 0.0)
    rb, rk = jnp.where(incl, arbk[..., n:, :n], 0.0), jnp.where(incl, arbk[..., n:, n:], 0.0)
    s0t = _transposed(s0)
    u = _solve_unit_lower(ab, mm(cat([at, ak], axis=-1), cat([s0t, vt], axis=-2), "nn"))
    y2 = mm(cat([rt, rb, rk], axis=-1), cat([s0t, u, vt], axis=-2), "nn")
    plast = jnp.exp(jnp.sum(lw, axis=-2, keepdims=True))
    s1 = (s0 + mm(cat([u, vt], axis=-2), bk, "tn")) * plast
    r2 = lax.broadcasted_iota(jnp.int32, (128, 128), 0) // HEAD_DIM
    c2 = lax.broadcasted_iota(jnp.int32, (128, 128), 1) // HEAD_DIM
    return y2[..., :c, :] + y2[..., c:, :], jnp.where(r2 == c2, s1, 0.0)


def _post_fn(y, r, k2, v, g, lnw, lnb, rk):
    seg = _seg_ones(RW)
    mean = _segsum(y, seg) * (1.0 / HEAD_DIM)
    yc = y - mean
    var = _segsum(yc * yc, seg) * (1.0 / HEAD_DIM)
    yn = yc * lax.rsqrt(var + GN_EPS)
    out = yn * lnw + lnb + _segsum(r * k2 * rk, seg) * v
    return out * g


def _attn_block_fn(q, kc, vc, kp=None, vp=None):
    n = ATTN_BLOCK
    qi = lax.broadcasted_iota(jnp.int32, (n, n), 0)
    kj = lax.broadcasted_iota(jnp.int32, (n, n), 1)
    lane = lax.broadcasted_iota(jnp.int32, (1, 128), 1)
    scale = HEAD_DIM ** -0.5
    valid = kj <= qi
    keys, vals = kc, vc
    if kp is not None:
        valid = jnp.concatenate([valid, kj >= qi], axis=-1)
        keys, vals = jnp.concatenate([kc, kp], axis=-2), jnp.concatenate([vc, vp], axis=-2)
    m0 = (lane // HEAD_DIM) == 0
    q2 = jnp.concatenate([jnp.where(m0, q, 0.0), jnp.where(m0, 0.0, q)], axis=-2)
    valid2 = jnp.concatenate([valid, valid], axis=-2)
    s = jnp.where(valid2, _mm(q2, keys, "nt", ATTN_PASSES) * scale, NEG)
    m = jnp.max(s, axis=-1, keepdims=True)
    p = jnp.exp(s - m)
    den = jnp.sum(p, axis=-1, keepdims=True)
    o2 = _mm(p, vals, "nn", ATTN_PASSES) / den
    l2 = m + jnp.log(den)
    return jnp.where(m0, o2[..., :n, :], o2[..., n:, :]), jnp.where(m0, l2[..., :n, :], l2[..., n:, :])


def _attn_block_bwd(q, kc, vc, kp, vp, o, lse, do, dl):
    n = ATTN_BLOCK
    cat = jnp.concatenate
    qi = lax.broadcasted_iota(jnp.int32, (n, n), 0)
    kj = lax.broadcasted_iota(jnp.int32, (n, n), 1)
    m0 = (lax.broadcasted_iota(jnp.int32, (1, 128), 1) // HEAD_DIM) == 0
    scale = HEAD_DIM ** -0.5
    valid = kj <= qi
    keys, vals = kc, vc
    if kp is not None:
        valid = cat([valid, kj >= qi], axis=-1)
        keys, vals = cat([kc, kp], axis=-2), cat([vc, vp], axis=-2)
    stack = lambda z: cat([jnp.where(m0, z, 0.0), jnp.where(m0, 0.0, z)], axis=-2)
    q2, do2 = stack(q), stack(do)
    lse2 = cat([jnp.max(jnp.where(m0, lse, NEG), axis=-1, keepdims=True),
                jnp.max(jnp.where(m0, NEG, lse), axis=-1, keepdims=True)], axis=-2)
    delta = jnp.sum(do2 * cat([o, o], axis=-2), axis=-1, keepdims=True)
    dlse = jnp.sum(stack(dl), axis=-1, keepdims=True)
    mm = lambda a, b, form: _mm_raw(a, b, form, ATTN_PASSES)
    s = jnp.where(cat([valid, valid], axis=-2), mm(q2, keys, "nt") * scale, NEG)
    p = jnp.exp(s - lse2)
    ds = p * (mm(do2, vals, "nt") - delta + dlse)
    dq2 = mm(ds, keys, "nn") * scale
    dq = jnp.where(m0, dq2[..., :n, :], dq2[..., n:, :])
    dkeys = mm(ds, q2, "tn") * scale
    dvals = mm(p, do2, "tn")
    if kp is None:
        return dq, dkeys, dvals
    return dq, dkeys[..., :n, :], dvals[..., :n, :], dkeys[..., n:, :], dvals[..., n:, :]


def _combine_fn(o1, o2, o3, l1, l2, l3, og):
    seg = _seg_ones(o1.shape[-1])
    m = jnp.maximum(jnp.maximum(l1, l2), l3)
    e1, e2, e3 = jnp.exp(l1 - m), jnp.exp(l2 - m), jnp.exp(l3 - m)
    o = (e1 * o1 + e2 * o2 + e3 * o3) / (e1 + e2 + e3)
    o = o * lax.rsqrt(_segsum(o * o, seg) * (1.0 / HEAD_DIM) + NORM_EPS)
    return o * og


def _shifted(p, last8, first):
    prow = jnp.where(first, 0.0, last8[7:8, :])
    rolled = pltpu.roll(p, 1, axis=0)
    rid = lax.broadcasted_iota(jnp.int32, p.shape, 0)
    return jnp.where(rid == 0, prow, rolled)


_PREP_TM = 256


def _prep_specs(tm):
    vec = lambda n: pl.BlockSpec((1, n), lambda i: (0, 0))
    mat = lambda r, n: pl.BlockSpec((r, n), lambda i: (0, 0))
    return [vec(SHIFT_COLS), vec(RW), mat(128, RW), vec(RW), mat(128, RW), mat(128, RW), vec(RW), vec(RW)]


def _in_proj_prep(x, g1, win, pw):
    t = x.shape[0]
    tm = _PREP_TM

    def body(x_ref, g_ref, w_ref, mu, w0, w2p, a0, a2p, g2, k_k, k_a, h_ref, pa_ref, qkv_ref, *rest):
        outs, carry = rest[:7], rest[7]

        @pl.when(pl.program_id(0) == 0)
        def _():
            carry[...] = jnp.zeros_like(carry)

        h = _rms_fwd(x_ref[...], g_ref[...]).astype(BF16)
        h_ref[...] = h
        proj = _dot_nt(h, w_ref[...])
        p = proj[:, :SHIFT_COLS]
        pa_ref[...] = p
        for j in range(3):
            for pr in range(N_PAIR):
                c0 = SHIFT_COLS + j * RW + pr * 128
                qkv_ref[j, pr] = proj[:, c0:c0 + 128]
        pprev = _shifted(p, carry[...], pl.program_id(0) == 0)
        carry[...] = p[tm - 8:, :]
        res = _prep_fn(p, pprev, mu[...], w0[...], w2p[...], a0[...], a2p[...], g2[...], k_k[...], k_a[...])
        for o_ref, val in zip(outs, res):
            o_ref[...] = val

    row = pl.BlockSpec((tm, RW), lambda i: (i, 0))
    return pl.pallas_call(
        body, name="in_proj_prep", grid=(t // tm,),
        in_specs=[pl.BlockSpec((tm, D_MODEL), lambda i: (i, 0)), pl.BlockSpec((1, D_MODEL), lambda i: (0, 0)),
                  pl.BlockSpec((IN_COLS, D_MODEL), lambda i: (0, 0))] + _prep_specs(tm),
        out_specs=[pl.BlockSpec((tm, D_MODEL), lambda i: (i, 0)), pl.BlockSpec((tm, SHIFT_COLS), lambda i: (i, 0)),
                   pl.BlockSpec((3, N_PAIR, tm, 128), lambda i: (0, 0, i, 0))] + [row] * 7,
        out_shape=[jax.ShapeDtypeStruct((t, D_MODEL), BF16), jax.ShapeDtypeStruct((t, SHIFT_COLS), F32),
                   jax.ShapeDtypeStruct((3, N_PAIR, t, 128), F32)] + [jax.ShapeDtypeStruct((t, RW), F32)] * 7,
        scratch_shapes=[pltpu.VMEM((8, SHIFT_COLS), F32)],
        compiler_params=_params(("arbitrary",)),
    )(x, g1, win, *pw)


def _pairs(ref):
    return jnp.stack([ref[:, 128 * p:128 * (p + 1)] for p in range(N_PAIR)], axis=0)


def _wkv_fwd(r, lw, k2, v, kk, a):
    t = r.shape[0]
    nc = t // CHUNK

    def body(r_ref, lw_ref, k_ref, v_ref, kk_ref, a_ref, y_ref, s_ref, st):
        @pl.when(pl.program_id(0) == 0)
        def _():
            st[...] = jnp.zeros_like(st)

        s0 = st[...]
        s_ref[0] = s0
        y, s1 = _wkv_chunk_fn(s0, *[_pairs(ref) for ref in (r_ref, lw_ref, k_ref, v_ref, kk_ref, a_ref)])
        for p in range(N_PAIR):
            y_ref[:, 128 * p:128 * (p + 1)] = y[p]
        st[...] = s1

    blk = pl.BlockSpec((CHUNK, RW), lambda c: (c, 0))
    return pl.pallas_call(
        body, name="wkv_fwd", grid=(nc,),
        in_specs=[blk] * 6,
        out_specs=[blk, pl.BlockSpec((1, N_PAIR, 128, 128), lambda c: (c, 0, 0, 0))],
        out_shape=[jax.ShapeDtypeStruct((t, RW), F32), jax.ShapeDtypeStruct((nc, N_PAIR, 128, 128), F32)],
        scratch_shapes=[pltpu.VMEM((N_PAIR, 128, 128), F32)],
        compiler_params=_params(("arbitrary",)),
    )(r, lw, k2, v, kk, a)


_POST_TM = 512


ATTN_GROUP = 2


def _dilated_rows(d, r, n):
    if d == 1:
        return pl.ds(pl.multiple_of(n * ATTN_BLOCK, ATTN_BLOCK), ATTN_BLOCK)
    return pl.ds(r + n * (ATTN_BLOCK * d), ATTN_BLOCK, stride=d)


def _for_each_sequence(t, unit):
    for di, d in enumerate(DILATIONS):

        @pl.when(pl.program_id(1) == di)
        def _(di=di, d=d):
            nb = t // (ATTN_BLOCK * d)
            if d == 1:
                unit(di, [(d, 0, 0)], False)
                unit(di, [(d, 0, 1)], True)
                lax.fori_loop(1, nb // 2, lambda k, c: (unit(di, [(d, 0, 2 * k), (d, 0, 2 * k + 1)], True), c)[1], 0)
            else:

                def residues(r, carry):
                    unit(di, [(d, r, 0), (d, r + d // 2, 0)], False)
                    if nb > 1:
                        lax.fori_loop(1, nb, lambda n, c: (unit(di, [(d, r, n), (d, r + d // 2, n)], True), c)[1], 0)
                    return carry

                lax.fori_loop(0, d // 2, residues, 0)


def _take(ref, lead, rows_list):
    return jnp.stack([ref.at[(*lead, g)][rows, :] for rows in rows_list for g in range(ref.shape[len(lead)])], axis=0)


def _put(ref, lead, rows_list, val, add=False):
    k = 0
    for rows in rows_list:
        for g in range(ref.shape[len(lead)]):
            if add:
                ref.at[(*lead, g)][rows, :] += val[k]
            else:
                ref.at[(*lead, g)][rows, :] = val[k]
            k += 1


def _attn_fwd(qkv):
    t = qkv.shape[2]

    def body(q_ref, k_ref, v_ref, o_ref, l_ref):
        def unit(di, places, has_prev):
            cur = [_dilated_rows(d, r, n) for d, r, n in places]
            args = [_take(ref, (0,), cur) for ref in (q_ref, k_ref, v_ref)]
            if has_prev:
                prv = [_dilated_rows(d, r, n - 1) for d, r, n in places]
                args += [_take(ref, (0,), prv) for ref in (k_ref, v_ref)]
            o, lse = _attn_block_fn(*args)
            _put(o_ref, (0,), cur, o)
            _put(l_ref, (0,), cur, lse)

        _for_each_sequence(t, unit)

    spec = lambda j: pl.BlockSpec((1, ATTN_GROUP, t, 128), lambda i, b: (j, i, 0, 0))
    out = pl.BlockSpec((1, ATTN_GROUP, t, 128), lambda i, b: (b, i, 0, 0))
    return pl.pallas_call(
        body, name="attn_fwd", grid=(N_PAIR // ATTN_GROUP, len(DILATIONS)),
        in_specs=[spec(0), spec(1), spec(2)], out_specs=[out, out],
        out_shape=[jax.ShapeDtypeStruct((3, N_PAIR, t, 128), F32)] * 2,
        compiler_params=_params(("parallel", "arbitrary")),
    )(qkv, qkv, qkv)


_COMB_TM = 512


def _mixers_out(y, r, k2, v, g, lnw, lnb, rk, o, l, og):
    t = y.shape[0]
    tm = _COMB_TM

    def body(y_ref, r_ref, k_ref, v_ref, g_ref, lnw_ref, lnb_ref, rk_ref, o_ref, l_ref, og_ref, out_ref):
        out_ref[:, :RW] = _post_fn(y_ref[...], r_ref[...], k_ref[...], v_ref[...], g_ref[...],
                                   lnw_ref[...], lnb_ref[...], rk_ref[...]).astype(BF16)
        for p in range(N_PAIR):
            cols = slice(128 * p, 128 * (p + 1))
            out_ref[:, RW + 128 * p:RW + 128 * (p + 1)] = _combine_fn(
                o_ref[0, p], o_ref[1, p], o_ref[2, p], l_ref[0, p], l_ref[1, p], l_ref[2, p], og_ref[:, cols]).astype(BF16)

    row = pl.BlockSpec((tm, RW), lambda i: (i, 0))
    vec = pl.BlockSpec((1, RW), lambda i: (0, 0))
    blk = pl.BlockSpec((3, N_PAIR, tm, 128), lambda i: (0, 0, i, 0))
    return pl.pallas_call(
        body, name="mixers_out", grid=(t // tm,),
        in_specs=[row] * 5 + [vec] * 3 + [blk, blk, vec], out_specs=pl.BlockSpec((tm, D_MODEL), lambda i: (i, 0)),
        out_shape=jax.ShapeDtypeStruct((t, D_MODEL), BF16),
        compiler_params=_params(("parallel",)),
    )(y, r, k2, v, g, lnw, lnb, rk, o, l, og)


def _ffn_all(x, ycat, wg, wu, wd, wout, g2, gf, tgt):
    t = x.shape[0]
    tm = 256

    def body(x_ref, y_ref, wg_ref, wu_ref, wd_ref, wo_ref, g2_ref, gf_ref, t_ref,
             h_ref, act_ref, dx2b_ref, dgt_ref, dup_ref, dx1b_ref, dx1_ref, dya_ref, dyb_ref, loss_ref, dgf_ref, dg2_ref,
             gt_s, up_s):
        first = pl.program_id(0) == 0
        x1 = x_ref[...] + _dot(y_ref[...], wo_ref[...])
        h = _rms_fwd(x1, g2_ref[...]).astype(BF16)
        h_ref[...] = h
        for c0 in range(0, D_FF, FF_CHUNK):
            cols = slice(c0, c0 + FF_CHUNK)
            gt = _dot_nt(h, wg_ref[cols, :])
            up = _dot_nt(h, wu_ref[cols, :])
            gt_s[:, cols] = gt.astype(BF16)
            up_s[:, cols] = up.astype(BF16)
            act_ref[:, cols] = (gt * _sigmoid(gt) * up).astype(BF16)
        x2 = x1 + _dot(act_ref[...], wd_ref[...])
        gf_ = gf_ref[...]
        diff = _rms_fwd(x2, gf_) - t_ref[...]
        lrow = 0.5 * jnp.sum(_colsum8(diff * diff), axis=1, keepdims=True) * (1.0 / D_MODEL)
        _acc(loss_ref, jnp.broadcast_to(lrow, (8, 128)), first)
        dx2, dgr = _rms_bwd(diff * (1.0 / D_MODEL), x2, gf_)
        _acc(dgf_ref, _colsum8(dgr), first)
        dx2b = dx2.astype(BF16)
        dx2b_ref[...] = dx2b
        for c0 in range(0, D_FF, FF_CHUNK):
            cols = slice(c0, c0 + FF_CHUNK)
            dact = _dot_nt(dx2b, wd_ref[cols, :])
            gt = gt_s[:, cols].astype(F32)
            sg = _sigmoid(gt)
            dgt_ref[:, cols] = (dact * up_s[:, cols].astype(F32) * sg * (1.0 + gt * (1.0 - sg))).astype(BF16)
            dup_ref[:, cols] = (dact * gt * sg).astype(BF16)
        dh = _dot(dgt_ref[...], wg_ref[...]) + _dot(dup_ref[...], wu_ref[...])
        dxn, dgr2 = _rms_bwd(dh, x1, g2_ref[...])
        _acc(dg2_ref, _colsum8(dgr2), first)
        dx1 = dx2 + dxn
        dx1_ref[...] = dx1
        dx1b = dx1.astype(BF16)
        dx1b_ref[...] = dx1b
        dy = _dot_nt(dx1b, wo_ref[...])
        dya_ref[...] = dy[:, :RW]
        dyb_ref[...] = dy[:, RW:]

    row = pl.BlockSpec((tm, D_MODEL), lambda i: (i, 0))
    wide = pl.BlockSpec((tm, D_FF), lambda i: (i, 0))
    half = pl.BlockSpec((tm, RW), lambda i: (i, 0))
    wsp = pl.BlockSpec((D_FF, D_MODEL), lambda i: (0, 0))
    vec = pl.BlockSpec((1, D_MODEL), lambda i: (0, 0))
    part = pl.BlockSpec((8, D_MODEL), lambda i: (0, 0))
    bf = lambda n: jax.ShapeDtypeStruct((t, n), BF16)
    return pl.pallas_call(
        body, name="ffn_all", grid=(t // tm,),
        in_specs=[row, row, wsp, wsp, wsp, pl.BlockSpec((D_MODEL, D_MODEL), lambda i: (0, 0)), vec, vec, row],
        out_specs=[row, wide, row, wide, wide, row, row, half, half, pl.BlockSpec((8, 128), lambda i: (0, 0)), part, part],
        out_shape=[bf(D_MODEL), bf(D_FF), bf(D_MODEL), bf(D_FF), bf(D_FF), bf(D_MODEL),
                   jax.ShapeDtypeStruct((t, D_MODEL), F32), jax.ShapeDtypeStruct((t, RW), F32),
                   jax.ShapeDtypeStruct((t, RW), F32), jax.ShapeDtypeStruct((8, 128), F32),
                   jax.ShapeDtypeStruct((8, D_MODEL), F32), jax.ShapeDtypeStruct((8, D_MODEL), F32)],
        scratch_shapes=[pltpu.VMEM((tm, D_FF), BF16), pltpu.VMEM((tm, D_FF), BF16)],
        compiler_params=_params(("arbitrary",)),
    )(x, ycat, wg, wu, wd, wout, g2, gf, tgt)


def _wgrad(a, b, tk, tn, name):
    t, kdim = a.shape
    ndim = b.shape[1]

    def body(a_ref, b_ref, o_ref):
        o_ref[...] = _dot_tn(a_ref[...], b_ref[...]).astype(BF16)

    return pl.pallas_call(
        body, name=name, grid=(kdim // tk, ndim // tn),
        in_specs=[pl.BlockSpec((t, tk), lambda i, j: (0, i)), pl.BlockSpec((t, tn), lambda i, j: (0, j))],
        out_specs=pl.BlockSpec((tk, tn), lambda i, j: (i, j)),
        out_shape=jax.ShapeDtypeStruct((kdim, ndim), BF16),
        compiler_params=_params(("parallel", "parallel")),
    )(a, b)


def _post_bwd(dya, y, r, k2, v, g, lnw, lnb, rk, after):
    t = y.shape[0]
    tm = _POST_TM

    def body(d_ref, y_ref, r_ref, k_ref, v_ref, g_ref, lnw_ref, lnb_ref, rk_ref, _,
             dy_ref, dr_ref, dk_ref, dv_ref, dg_ref, dlnw_ref, dlnb_ref, drk_ref):
        first = pl.program_id(0) == 0
        ones = jnp.ones((tm, 1), F32)
        prim = (y_ref[...], r_ref[...], k_ref[...], v_ref[...], g_ref[...],
                ones * lnw_ref[...], ones * lnb_ref[...], ones * rk_ref[...])
        _, vjp = jax.vjp(_post_fn, *prim)
        dy, dr, dk, dv, dg, dlnw, dlnb, drk = vjp(d_ref[...])
        dy_ref[...] = dy
        dr_ref[...] = dr
        dk_ref[...] = dk
        dv_ref[...] = dv
        dg_ref[...] = dg
        _acc(dlnw_ref, _colsum8(dlnw), first)
        _acc(dlnb_ref, _colsum8(dlnb), first)
        _acc(drk_ref, _colsum8(drk), first)

    row = pl.BlockSpec((tm, RW), lambda i: (i, 0))
    vec = pl.BlockSpec((1, RW), lambda i: (0, 0))
    part = pl.BlockSpec((8, RW), lambda i: (0, 0))
    return pl.pallas_call(
        body, name="rwkv_post_bwd", grid=(t // tm,),
        in_specs=[row] * 6 + [vec] * 3 + [ANY], out_specs=[row] * 5 + [part] * 3,
        out_shape=[jax.ShapeDtypeStruct((t, RW), F32)] * 5 + [jax.ShapeDtypeStruct((8, RW), F32)] * 3,
        compiler_params=_params(("arbitrary",)),
    )(dya, y, r, k2, v, g, lnw, lnb, rk, after)


def _wkv_bwd(dy, s0s, r, lw, k2, v, kk, a):
    t = r.shape[0]
    nc = t // CHUNK

    def body(dy_ref, s_ref, r_ref, lw_ref, k_ref, v_ref, kk_ref, a_ref,
             dr_ref, dlw_ref, dk_ref, dv_ref, dkk_ref, da_ref, ds):
        @pl.when(pl.program_id(0) == 0)
        def _():
            ds[...] = jnp.zeros_like(ds)

        _, vjp = jax.vjp(_wkv_chunk_fn, s_ref[0],
                         *[_pairs(ref) for ref in (r_ref, lw_ref, k_ref, v_ref, kk_ref, a_ref)])
        res = vjp((_pairs(dy_ref), ds[...]))
        ds[...] = res[0]
        for ref, val in zip((dr_ref, dlw_ref, dk_ref, dv_ref, dkk_ref, da_ref), res[1:]):
            for p in range(N_PAIR):
                ref[:, 128 * p:128 * (p + 1)] = val[p]

    blk = pl.BlockSpec((CHUNK, RW), lambda c: (nc - 1 - c, 0))
    return pl.pallas_call(
        body, name="wkv_bwd", grid=(nc,),
        in_specs=[blk, pl.BlockSpec((1, N_PAIR, 128, 128), lambda c: (nc - 1 - c, 0, 0, 0))] + [blk] * 6,
        out_specs=[blk] * 6,
        out_shape=[jax.ShapeDtypeStruct((t, RW), F32)] * 6,
        scratch_shapes=[pltpu.VMEM((N_PAIR, 128, 128), F32)],
        compiler_params=_params(("arbitrary",)),
    )(dy, s0s, r, lw, k2, v, kk, a)


def _prep_in_proj_bwd(proj, pw, douts, dq, dk, dv, win, x, g1, dx1):
    t = proj.shape[0]
    tm = _PREP_TM
    nt = t // tm

    def body(p_ref, l8_ref, mu, w0, w2p, a0, a2p, g2, k_k, k_a, dr, dr2, dlw, dk2, dk22, dv, dv2, dkk, da, dg,
             dq_ref, dkq_ref, dvq_ref, w_ref, x_ref, g1_ref, dx1_ref,
             dproj_ref, dx_ref, dg1_ref, dmu_ref, dw0_ref, dw2_ref, da0_ref, da2_ref, dg2_ref, dkk_ref, dka_ref, carry):
        i = pl.program_id(0)
        first = i == 0

        @pl.when(first)
        def _():
            carry[...] = jnp.zeros_like(carry)

        p = p_ref[...]
        pprev = _shifted(p, l8_ref[...], i == nt - 1)
        ones = jnp.ones((tm, 1), F32)
        prim = (p, pprev, ones * mu[...], ones * w0[...], w2p[...], ones * a0[...], a2p[...], g2[...],
                ones * k_k[...], ones * k_a[...])
        _, vjp = jax.vjp(_prep_fn, *prim)
        dp, dpp, dmu, dw0, dw2, da0, da2, dg2, dkk_, dka = vjp(
            (dr[...] + dr2[...], dlw[...], dk2[...] + dk22[...], dv[...] + dv2[...], dkk[...], da[...], dg[...]))
        up = pltpu.roll(dpp, tm - 1, axis=0)
        rid = lax.broadcasted_iota(jnp.int32, dpp.shape, 0)
        dpa = dp + jnp.where(rid == tm - 1, carry[0:1, :], up)
        carry[...] = jnp.broadcast_to(dpp[0:1, :], carry.shape)
        _acc(dmu_ref, _colsum8(dmu), first)
        _acc(dw0_ref, _colsum8(dw0), first)
        _acc(dw2_ref, dw2, first)
        _acc(da0_ref, _colsum8(da0), first)
        _acc(da2_ref, da2, first)
        _acc(dg2_ref, dg2, first)
        _acc(dkk_ref, _colsum8(dkk_), first)
        _acc(dka_ref, _colsum8(dka), first)
        parts = [dpa] + [ref[pr] for ref in (dq_ref, dkq_ref, dvq_ref) for pr in range(N_PAIR)]
        dproj = jnp.concatenate([z.astype(BF16) for z in parts], axis=1)
        dproj_ref[...] = dproj
        dxn, dgr = _rms_bwd(_dot(dproj, w_ref[...]), x_ref[...], g1_ref[...])
        dx_ref[...] = dx1_ref[...] + dxn
        _acc(dg1_ref, _colsum8(dgr), first)

    rev = lambda i: (nt - 1 - i, 0)
    row = pl.BlockSpec((tm, RW), rev)
    wide = pl.BlockSpec((tm, D_MODEL), rev)
    pair = pl.BlockSpec((N_PAIR, tm, 128), lambda i: (0, nt - 1 - i, 0))
    part = lambda n: pl.BlockSpec((8, n), lambda i: (0, 0))
    mat = pl.BlockSpec((128, RW), lambda i: (0, 0))
    return pl.pallas_call(
        body, name="prep_in_proj_bwd", grid=(nt,),
        in_specs=[pl.BlockSpec((tm, SHIFT_COLS), rev),
                  pl.BlockSpec((8, SHIFT_COLS), lambda i: (jnp.maximum((nt - 1 - i) * (tm // 8) - 1, 0), 0))]
                 + _prep_specs(tm) + [row] * 10
                 + [pair] * 3 + [pl.BlockSpec((IN_COLS, D_MODEL), lambda i: (0, 0)), wide,
                                 pl.BlockSpec((1, D_MODEL), lambda i: (0, 0)), wide],
        out_specs=[pl.BlockSpec((tm, IN_COLS), rev), wide, part(D_MODEL), part(SHIFT_COLS), part(RW), mat, part(RW), mat,
                   mat, part(RW), part(RW)],
        out_shape=[jax.ShapeDtypeStruct((t, IN_COLS), BF16), jax.ShapeDtypeStruct((t, D_MODEL), F32),
                   jax.ShapeDtypeStruct((8, D_MODEL), F32), jax.ShapeDtypeStruct((8, SHIFT_COLS), F32),
                   jax.ShapeDtypeStruct((8, RW), F32), jax.ShapeDtypeStruct((128, RW), F32),
                   jax.ShapeDtypeStruct((8, RW), F32), jax.ShapeDtypeStruct((128, RW), F32),
                   jax.ShapeDtypeStruct((128, RW), F32), jax.ShapeDtypeStruct((8, RW), F32),
                   jax.ShapeDtypeStruct((8, RW), F32)],
        scratch_shapes=[pltpu.VMEM((8, SHIFT_COLS), F32)],
        compiler_params=_params(("arbitrary",)),
    )(proj, proj, *pw, *douts, dq, dk, dv, win, x, g1, dx1)


def _combine_bwd(dyb, o, l, og):
    t = dyb.shape[0]
    tm = _COMB_TM

    def body(d_ref, o_ref, l_ref, og_ref, do_ref, dl_ref, dog_ref):
        ones = jnp.ones((tm, 1), F32)
        dog = []
        for p in range(N_PAIR):
            cols = slice(128 * p, 128 * (p + 1))
            _, vjp = jax.vjp(_combine_fn, o_ref[0, p], o_ref[1, p], o_ref[2, p], l_ref[0, p], l_ref[1, p], l_ref[2, p],
                             ones * og_ref[:, cols])
            res = vjp(d_ref[:, cols])
            for b in range(3):
                do_ref[b, p] = res[b]
                dl_ref[b, p] = res[3 + b]
            dog.append(_colsum8(res[6]))
        _acc(dog_ref, jnp.concatenate(dog, axis=1), pl.program_id(0) == 0)

    blk = pl.BlockSpec((3, N_PAIR, tm, 128), lambda i: (0, 0, i, 0))
    return pl.pallas_call(
        body, name="attn_combine_bwd", grid=(t // tm,),
        in_specs=[pl.BlockSpec((tm, RW), lambda i: (i, 0)), blk, blk, pl.BlockSpec((1, RW), lambda i: (0, 0))],
        out_specs=[blk, blk, pl.BlockSpec((8, RW), lambda i: (0, 0))],
        out_shape=[jax.ShapeDtypeStruct((3, N_PAIR, t, 128), F32)] * 2 + [jax.ShapeDtypeStruct((8, RW), F32)],
        compiler_params=_params(("arbitrary",)),
    )(dyb, o, l, og)


def _attn_bwd(do, dl, o, lse, qkv):
    t = qkv.shape[2]

    def body(do_ref, dl_ref, o_ref, l_ref, q_ref, k_ref, v_ref, dq_ref, dk_ref, dv_ref):
        @pl.when(pl.program_id(1) == 0)
        def _():
            for ref in (dq_ref, dk_ref, dv_ref):
                ref[...] = jnp.zeros_like(ref)

        def unit(di, places, has_prev):
            cur = [_dilated_rows(d, r, n) for d, r, n in places]
            q, kc, vc = [_take(ref, (0,), cur) for ref in (q_ref, k_ref, v_ref)]
            kp = vp = None
            if has_prev:
                prv = [_dilated_rows(d, r, n - 1) for d, r, n in places]
                kp, vp = [_take(ref, (0,), prv) for ref in (k_ref, v_ref)]
            res = _attn_block_bwd(q, kc, vc, kp, vp, *[_take(ref, (0,), cur) for ref in (o_ref, l_ref, do_ref, dl_ref)])
            _put(dq_ref, (), cur, res[0], add=True)
            _put(dk_ref, (), cur, res[1], add=True)
            _put(dv_ref, (), cur, res[2], add=True)
            if has_prev:
                _put(dk_ref, (), prv, res[3], add=True)
                _put(dv_ref, (), prv, res[4], add=True)

        _for_each_sequence(t, unit)

    spec = lambda j: pl.BlockSpec((1, ATTN_GROUP, t, 128), lambda i, b: (j, i, 0, 0))
    branch = pl.BlockSpec((1, ATTN_GROUP, t, 128), lambda i, b: (b, i, 0, 0))
    out = pl.BlockSpec((ATTN_GROUP, t, 128), lambda i, b: (i, 0, 0))
    return pl.pallas_call(
        body, name="attn_bwd", grid=(N_PAIR // ATTN_GROUP, len(DILATIONS)),
        in_specs=[branch] * 4 + [spec(0), spec(1), spec(2)], out_specs=[out] * 3,
        out_shape=[jax.ShapeDtypeStruct((N_PAIR, t, 128), F32)] * 3,
        compiler_params=_params(("parallel", "arbitrary")),
    )(do, dl, o, lse, qkv, qkv, qkv)


def _unstack_lora(lora_all):
    def body(l_ref, w_ref, a_ref, g_ref):
        z = jnp.zeros((64, 128), F32)
        for p in range(N_CHIP):
            cols = slice(128 * p, 128 * (p + 1))
            w_ref[0:64, cols] = l_ref[p, 0:64, :]
            w_ref[64:128, cols] = z
            a_ref[0:64, cols] = z
            a_ref[64:128, cols] = l_ref[p, 64:128, :]
            g_ref[:, cols] = l_ref[p, 128:256, :]

    return pl.pallas_call(body, name="unstack_lora", out_shape=[jax.ShapeDtypeStruct((128, RW), F32)] * 3)(lora_all)


def _stack_lora_grads(dw2p, da2p, dg2m):
    def body(w_ref, a_ref, g_ref, out_ref):
        for p in range(N_CHIP):
            cols = slice(128 * p, 128 * (p + 1))
            out_ref[p, 0:64, :] = w_ref[0:64, cols].astype(BF16)
            out_ref[p, 64:128, :] = a_ref[64:128, cols].astype(BF16)
            out_ref[p, 128:256, :] = g_ref[:, cols].astype(BF16)

    return pl.pallas_call(body, name="stack_lora_grads",
                          out_shape=jax.ShapeDtypeStruct((N_CHIP, 256, 128), BF16))(dw2p, da2p, dg2m)


def _local_step(x, tgt, win, vecs, w2p, a2p, g2m, get_rest, send_rest):
    pw = (vecs["mu_shift"], vecs["decay_w0"], w2p, vecs["iclr_a0"], a2p, g2m, vecs["k_k"], vecs["k_a"])
    h, proj, qkv, r, lw, k2, v, kk, a, g = _in_proj_prep(x, vecs["mix_norm_g"], win, pw)
    y, s0s = _wkv_fwd(r, lw, k2, v, kk, a)
    o_att, l_att = _attn_fwd(qkv)
    ycat = _mixers_out(y, r, k2, v, g, vecs["ln_x_w"], vecs["ln_x_b"], vecs["r_k"], o_att, l_att, vecs["attn_out_g"])
    wout, wg, wu, wd = get_rest(ycat)
    h2, act, dx2b, dgt, dup, dx1b, dx1, dya, dyb, loss8, dgf, dg2n = _ffn_all(
        x, ycat, wg, wu, wd, wout, vecs["ffn_norm_g"], vecs["final_norm_g"], tgt)
    gw = {
        "w_down": _wgrad(act, dx2b, 1408, 1024, "wgrad_down"),
        "w_gate": _wgrad(dgt, h2, 1408, 1024, "wgrad_gate"),
        "w_up": _wgrad(dup, h2, 1408, 1024, "wgrad_up"),
        "w_out": _wgrad(ycat, dx1b, 1024, 1024, "wgrad_out"),
    }

    dy, dr_p, dk2_p, dv_p, dg, dlnw, dlnb, drk = _post_bwd(dya, y, r, k2, v, g, vecs["ln_x_w"], vecs["ln_x_b"], vecs["r_k"],
                                                           after=send_rest(gw))
    dr_s, dlw, dk2_s, dv_s, dkk, da = _wkv_bwd(dy, s0s, r, lw, k2, v, kk, a)
    do_att, dl_att, dog = _combine_bwd(dyb, o_att, l_att, vecs["attn_out_g"])
    dq, dk, dv = _attn_bwd(do_att, dl_att, o_att, l_att, qkv)
    dproj, dx, dg1, dmu, dw0, dw2p, da0, da2p, dg2m, dk_k, dk_a = _prep_in_proj_bwd(
        proj, pw, (dr_p, dr_s, dlw, dk2_p, dk2_s, dv_p, dv_s, dkk, da, dg), dq, dk, dv, win, x, vecs["mix_norm_g"], dx1)
    gw["w_in"] = _wgrad(dproj, h, 1664, 1024, "wgrad_in")
    gw["lora"] = _stack_lora_grads(dw2p, da2p, dg2m)
    gv ={"mix_norm_g": dg1, "mu_shift": dmu, "decay_w0": dw0, "iclr_a0": da0, "k_k": dk_k, "k_a": dk_a, "r_k": drk,
          "ln_x_w": dlnw, "ln_x_b": dlnb, "attn_out_g": dog, "ffn_norm_g": dg2n, "final_norm_g": dgf}
    return loss8, dx, gw, gv


N_CHIP = 4
N_DEV = 8
MATS = ("w_in", "w_out", "w_gate", "w_up", "w_down")
LORAS = ("decay_w2", "iclr_a2", "gate_g2")
VECS = (("mix_norm_g", 1024), ("mu_shift", 1792), ("decay_w0", 512), ("iclr_a0", 512), ("k_k", 512), ("k_a", 512),
        ("r_k", 512), ("ln_x_w", 512), ("ln_x_b", 512), ("attn_out_g", 512), ("ffn_norm_g", 1024),
        ("final_norm_g", 1024))
N_VEC = sum(n for _, n in VECS)
N_SMALL = N_VEC + 128
ANY = pl.BlockSpec(memory_space=pl.ANY)


def _flip(v, f):
    return 1 - v if f else v


class _Me:
    def __init__(self, mode):
        x, y, c = lax.axis_index("x"), lax.axis_index("y"), lax.axis_index("c")
        self.core, self.chip, self.dev = c, 2 * x + y, 4 * x + 2 * y + c
        self.sibling = (x, y, 1 - c)
        if mode == "chips":
            self.peers = [(px, py, c) for px, py in ((1 - x, y), (x, 1 - y), (1 - x, 1 - y))]
        else:
            self.peers = [(_flip(x, k & 4), _flip(y, k & 2), _flip(c, k & 1)) for k in range(1, N_DEV)]


def _half(core, rows):
    h = rows // 2
    return pl.ds(pl.multiple_of(core * h, h), h)


_BY_CHIP = ("gather", "whole", "chipsum")


def _peer_copy(srcs, dsts, kinds, send_sems, recv_sems, me, j, i, incoming):
    px, py, pc = me.peers[j]
    pchip, pdev = 2 * px + py, 4 * px + 2 * py + pc
    src, dst, kind = srcs[i], dsts[i], kinds[i]
    if kind in ("gather", "whole"):
        rows = _half(me.core, src.shape[1]) if kind == "gather" else pl.ds(0, src.shape[1])
        src, dst = src.at[me.chip, rows], dst.at[pchip if incoming else me.chip, rows]
    elif kind == "scatter":
        src, dst = src.at[pchip, _half(pc, src.shape[1])], dst.at[pdev if incoming else me.dev]
    elif kind == "chipsum":
        src, dst = src.at[pchip], dst.at[pchip if incoming else me.chip]
    else:
        dst = dst.at[pdev if incoming else me.dev]
    n = len(srcs)
    return pltpu.make_async_remote_copy(src_ref=src, dst_ref=dst, send_sem=send_sems.at[n * j + i],
                                        recv_sem=recv_sems.at[n * j + i], device_id=(px, py, pc), device_id_type=MESH)


def _mode(kinds):
    return "chips" if kinds[0] in _BY_CHIP else "devs"


def _npeer(kinds):
    return N_CHIP - 1 if kinds[0] in _BY_CHIP else N_DEV - 1


def _sibling_halves(gs, name):
    n = len(gs)

    def body(*refs):
        srcs, dsts, send_sems, recv_sems = refs[:n], refs[n:2 * n], refs[2 * n], refs[2 * n + 1]
        me = _Me("chips")

        def copy(i, p):
            return pltpu.make_async_remote_copy(
                src_ref=srcs[i].at[p, _half(1 - me.core, srcs[i].shape[1])], dst_ref=dsts[i].at[p],
                send_sem=send_sems.at[N_CHIP * i + p], recv_sem=recv_sems.at[N_CHIP * i + p],
                device_id=me.sibling, device_id_type=MESH)

        copies = [copy(i, p) for i in range(n) for p in range(N_CHIP)]
        for cp in copies:
            cp.start()
        for cp in copies:
            cp.wait()

    return pl.pallas_call(
        body, name=name, in_specs=[ANY] * n, out_specs=[ANY] * n,
        out_shape=[jax.ShapeDtypeStruct((N_CHIP, g.shape[1] // 2, g.shape[2]), g.dtype) for g in gs],
        scratch_shapes=[pltpu.SemaphoreType.DMA((N_CHIP * n,)), pltpu.SemaphoreType.DMA((N_CHIP * n,))],
    )(*gs)


def _add_halves(g, other, core, tr, name):
    _, h, cols = other.shape

    def body(core_ref, g_ref, o_ref, out_ref):
        out_ref[...] = (g_ref[...].astype(F32) + o_ref[...].astype(F32)).astype(BF16)

    blk = lambda off: pl.BlockSpec((1, tr, cols), lambda p, i, core_ref: (p, core_ref[0] * (h // tr) * off + i, 0))
    return pl.pallas_call(
        body, name=name,
        grid_spec=pltpu.PrefetchScalarGridSpec(num_scalar_prefetch=1, grid=(N_CHIP, h // tr),
                                               in_specs=[blk(1), blk(0)], out_specs=blk(0)),
        out_shape=jax.ShapeDtypeStruct(other.shape, BF16),
        compiler_params=_params(("parallel", "parallel")),
    )(core, g, other)


def _swap_gathered(lands, name):
    n = len(lands)

    def body(*refs):
        dsts, send_sems, recv_sems = refs[n:2 * n], refs[2 * n], refs[2 * n + 1]
        me = _Me("chips")

        def copy(j, i, incoming):
            px, py, _ = me.peers[j]
            rows_out, rows_in = _half(me.core, dsts[i].shape[1]), _half(1 - me.core, dsts[i].shape[1])
            return pltpu.make_async_remote_copy(
                src_ref=dsts[i].at[2 * px + py, rows_out], dst_ref=dsts[i].at[2 * px + py, rows_in if incoming else rows_out],
                send_sem=send_sems.at[n * j + i], recv_sem=recv_sems.at[n * j + i], device_id=me.sibling, device_id_type=MESH)

        sends = [copy(j, i, False) for j in range(3) for i in range(n)]
        for cp in sends:
            cp.start()
        for j in range(3):
            for i in range(n):
                copy(j, i, True).wait_recv()
        for cp in sends:
            cp.wait_send()

    return pl.pallas_call(
        body, name=name, in_specs=[ANY] * n, out_specs=[ANY] * n,
        out_shape=[jax.ShapeDtypeStruct(l.shape, l.dtype) for l in lands],
        input_output_aliases={i: i for i in range(n)},
        scratch_shapes=[pltpu.SemaphoreType.DMA((3 * n,)), pltpu.SemaphoreType.DMA((3 * n,))],
    )(*lands)


def _join_halves(sums, name):
    n = len(sums)

    def body(*refs):
        dsts, send_sems, recv_sems = refs[n:2 * n], refs[2 * n], refs[2 * n + 1]
        me = _Me("chips")

        def copy(i, incoming):
            mine, other = _half(me.core, dsts[i].shape[0]), _half(1 - me.core, dsts[i].shape[0])
            return pltpu.make_async_remote_copy(src_ref=dsts[i].at[mine], dst_ref=dsts[i].at[other if incoming else mine],
                                                send_sem=send_sems.at[i], recv_sem=recv_sems.at[i],
                                                device_id=me.sibling, device_id_type=MESH)

        sends = [copy(i, False) for i in range(n)]
        for cp in sends:
            cp.start()
        for i in range(n):
            copy(i, True).wait_recv()
        for cp in sends:
            cp.wait_send()

    return pl.pallas_call(
        body, name=name, in_specs=[ANY] * n, out_specs=[ANY] * n,
        out_shape=[jax.ShapeDtypeStruct(s.shape, s.dtype) for s in sums],
        input_output_aliases={i: i for i in range(n)},
        scratch_shapes=[pltpu.SemaphoreType.DMA((n,)), pltpu.SemaphoreType.DMA((n,))],
    )(*sums)


HBM = pl.BlockSpec(memory_space=pltpu.HBM)
SEM = pl.BlockSpec(memory_space=pltpu.SEMAPHORE)
EFFECT = pltpu.SideEffectType.DATAFLOW_SIDE_EFFECTING


def _swap_start(arrs, lands, kinds, name):
    n = len(lands)
    ops = list(lands) if arrs is None else [*arrs, *lands]
    k = len(ops)

    def body(*refs):
        srcs, dsts, send_sems, recv_sems, token = refs[:n], refs[k - n:k], refs[k], refs[k + 1], refs[-1]
        me = _Me(_mode(kinds))
        for j in range(len(me.peers)):
            for i in range(n):
                _peer_copy(srcs, dsts, kinds, send_sems, recv_sems, me, j, i, False).start()
        token[...] = jnp.zeros_like(token)

    ns = _npeer(kinds) * n
    outs = pl.pallas_call(
        body, name=name,
        out_shape=(pltpu.SemaphoreType.DMA((ns,)), pltpu.SemaphoreType.DMA((ns,)),
                   *[pltpu.HBM(a.shape, a.dtype) for a in ops], jax.ShapeDtypeStruct((8, 128), F32)),
        in_specs=[HBM] * k, out_specs=(SEM, SEM, *[HBM] * k, pl.BlockSpec(memory_space=pltpu.VMEM)),
        input_output_aliases={i: 2 + i for i in range(k)},
        compiler_params=pltpu.CompilerParams(has_side_effects=EFFECT),
    )(*[pltpu.with_memory_space_constraint(a, pltpu.HBM) for a in ops])
    return outs[0], outs[1], outs[2:2 + k - n], outs[2 + k - n:2 + k], outs[-1]


def _swap_wait(send_sems, recv_sems, srcs_thru, lands_thru, after, kinds, name):
    n = len(lands_thru)
    ops = [*srcs_thru, *lands_thru]
    k = len(ops)

    def body(*refs):
        srcs, dsts, s_sems, r_sems = refs[:n], refs[k - n:k], refs[k], refs[k + 1]
        me = _Me(_mode(kinds))
        for j in range(len(me.peers)):
            for i in range(n):
                cp = _peer_copy(srcs, dsts, kinds, s_sems, r_sems, me, j, i, True)
                cp.wait_send()
                cp.wait_recv()

    outs = pl.pallas_call(
        body, name=name,
        out_shape=tuple(pltpu.HBM(a.shape, a.dtype) for a in ops),
        in_specs=[HBM] * k + [SEM, SEM, ANY], out_specs=tuple([HBM] * k),
        input_output_aliases={i: i for i in range(k)},
        compiler_params=pltpu.CompilerParams(has_side_effects=EFFECT),
    )(*ops, send_sems, recv_sems, after)
    return outs[k - n:]


def _adamw(w, g, m, v):
    m = ADAM_B1 * m + (1.0 - ADAM_B1) * g
    v = ADAM_B2 * v + (1.0 - ADAM_B2) * (g * g)
    m_hat = m / (1.0 - ADAM_B1 ** ADAM_STEP)
    v_hat = v / (1.0 - ADAM_B2 ** ADAM_STEP)
    delta = -ADAM_LR * (m_hat / (jnp.sqrt(v_hat) + ADAM_EPS) + ADAM_WD * w)
    return delta, m, v


def _reduce8(rbuf, core, tr, name):
    slots, h, cols = rbuf.shape

    def body(core_ref, r_ref, g_ref):
        g = r_ref[0].astype(F32)
        for s in range(1, slots):
            g = g + r_ref[s].astype(F32)
        g_ref[...] = g

    return pl.pallas_call(
        body, name=name,
        grid_spec=pltpu.PrefetchScalarGridSpec(
            num_scalar_prefetch=1, grid=(h // tr,),
            in_specs=[pl.BlockSpec((slots, tr, cols), lambda i, core_ref: (0, i, 0))],
            out_specs=pl.BlockSpec((tr, cols), lambda i, core_ref: (core_ref[0] * (h // tr) + i, 0))),
        out_shape=jax.ShapeDtypeStruct((2 * h, cols), F32),
        compiler_params=_params(("parallel",)),
    )(core, rbuf)


def _adamw_call(g, w, m, v, tr, name):
    _, rows, cols = w.shape

    def body(g_in, w_ref, m_ref, v_ref, g_ref, d_ref, nm_ref, nv_ref):
        g = g_in[...]
        g_ref[0] = g
        d_ref[0], nm_ref[0], nv_ref[0] = _adamw(w_ref[0], g, m_ref[0], v_ref[0])

    row = pl.BlockSpec((1, tr, cols), lambda i: (0, i, 0))
    return pl.pallas_call(
        body, name=name, grid=(rows // tr,),
        in_specs=[pl.BlockSpec((tr, cols), lambda i: (i, 0)), row, row, row], out_specs=[row] * 4,
        out_shape=[jax.ShapeDtypeStruct(w.shape, F32)] * 4,
        compiler_params=_params(("parallel",)),
    )(g, w, m, v)


def _adamw_lora(g, ws3, ms3, vs3):
    def body(g_in, *refs):
        ins, outs = refs[:9], refs[9:]
        r0 = 0
        for i in range(3):
            rows = ins[i].shape[1]
            g = g_in[r0:r0 + rows, :]
            outs[i][0] = g
            outs[3 + i][0], outs[6 + i][0], outs[9 + i][0] = _adamw(ins[i][0], g, ins[3 + i][0], ins[6 + i][0])
            r0 += rows

    return pl.pallas_call(body, name="adamw_lora",
                          out_shape=[jax.ShapeDtypeStruct(a.shape, F32) for a in ws3] * 4)(g, *ws3, *ms3, *vs3)


def _rowsum_small(parts, loss8, after):
    def body(*refs):
        out = refs[-1]
        c0 = 0
        for ref in refs[:-2]:
            n = ref.shape[1]
            out[:, c0:c0 + n] = jnp.sum(ref[...], axis=0, keepdims=True)
            c0 += n

    k = len(parts) + 1
    return pl.pallas_call(body, name="rowsum_small", in_specs=[pl.BlockSpec(memory_space=pltpu.VMEM)] * k + [ANY],
                          out_shape=jax.ShapeDtypeStruct((1, N_SMALL), F32))(*parts, loss8, after)


def _reduce_adamw_small(sbuf, ws, ms, vs):
    nv = len(ws)

    def body(*refs):
        s_ref, ins, outs = refs[0], refs[1:1 + 3 * nv], refs[1 + 3 * nv:]
        tot = s_ref[0]
        for s in range(1, N_DEV):
            tot = tot + s_ref[s]
        c0 = 0
        for i in range(nv):
            rows, cols = ins[i].shape
            n = rows * cols
            for r in range(rows):
                outs[i][r:r + 1, :] = tot[:, c0 + cols * r:c0 + cols * (r + 1)]
            g = outs[i][...]
            outs[nv + i][...], outs[2 * nv + i][...], outs[3 * nv + i][...] = _adamw(
                ins[i][...], g, ins[nv + i][...], ins[2 * nv + i][...])
            c0 += n
        outs[-1][...] = tot[:, c0:]

    return pl.pallas_call(
        body, name="reduce_adamw_small",
        out_shape=[jax.ShapeDtypeStruct(a.shape, F32) for a in ws] * 4 + [jax.ShapeDtypeStruct((1, 128), F32)],
    )(sbuf, *ws, *ms, *vs)


_TRANSPOSED = ("w_in", "w_gate", "w_up")
_ROW_STACKED = MATS
_ADAM_TILE = {"w_in": 208, "w_out": 256, "w_gate": 176, "w_up": 176, "w_down": 176, "lora": 256}
_SUM_TILE = {"w_in": 208, "w_out": 128, "w_gate": 176, "w_up": 176, "w_down": 176, "lora": 128}


def _full(n, stacked):
    p, r, c = stacked.shape
    if n in _ROW_STACKED:
        return stacked.reshape(p * r, c)
    return jnp.transpose(stacked, (1, 0, 2)).reshape(r, p * c)


def _by_chip(n, full):
    if n in _ROW_STACKED:
        return full.reshape(N_CHIP, full.shape[0] // N_CHIP, full.shape[1])
    r, c = full.shape
    return jnp.transpose(full.reshape(r, N_CHIP, c // N_CHIP), (1, 0, 2))


def _with_own(land_shape, dtype, own, slot):
    return lax.dynamic_update_slice(lax.empty(land_shape, dtype), own[None], (slot,) + (0,) * own.ndim)


def _cast_into_slot(a, chip, tr, name, after=None):
    rows, cols = a.shape

    def body(chip_ref, a_ref, *rest):
        rest[-1][0] = a_ref[...].astype(BF16)

    extra = [] if after is None else [after]
    return pl.pallas_call(
        body, name=name,
        grid_spec=pltpu.PrefetchScalarGridSpec(
            num_scalar_prefetch=1, grid=(rows // tr,),
            in_specs=[pl.BlockSpec((tr, cols), lambda i, chip_ref: (i, 0))] + [ANY] * len(extra),
            out_specs=pl.BlockSpec((1, tr, cols), lambda i, chip_ref: (chip_ref[0], i, 0))),
        out_shape=jax.ShapeDtypeStruct((N_CHIP, rows, cols), BF16),
        compiler_params=_params(("parallel",)),
    )(chip, a, *extra)


def kernel(x, mix_norm_g, w_in, mu_shift, decay_w0, decay_w2, iclr_a0, iclr_a2, gate_g2, k_k, k_a, r_k, ln_x_w, ln_x_b, attn_out_g, w_out, ffn_norm_g, w_gate, w_up, w_down, final_norm_g, loss_target, m_mix_norm_g, m_w_in, m_mu_shift, m_decay_w0, m_decay_w2, m_iclr_a0, m_iclr_a2, m_gate_g2, m_k_k, m_k_a, m_r_k, m_ln_x_w, m_ln_x_b, m_attn_out_g, m_w_out, m_ffn_norm_g, m_w_gate, m_w_up, m_w_down, m_final_norm_g, v_mix_norm_g, v_w_in, v_mu_shift, v_decay_w0, v_decay_w2, v_iclr_a0, v_iclr_a2, v_gate_g2, v_k_k, v_k_a, v_r_k, v_ln_x_w, v_ln_x_b, v_attn_out_g, v_w_out, v_ffn_norm_g, v_w_gate, v_w_up, v_w_down, v_final_norm_g):
    names = ("mix_norm_g", "w_in", "mu_shift", "decay_w0", "decay_w2", "iclr_a0", "iclr_a2", "gate_g2", "k_k", "k_a",
             "r_k", "ln_x_w", "ln_x_b", "attn_out_g", "w_out", "ffn_norm_g", "w_gate", "w_up", "w_down", "final_norm_g")
    w = dict(zip(names, (mix_norm_g, w_in, mu_shift, decay_w0, decay_w2, iclr_a0, iclr_a2, gate_g2, k_k, k_a, r_k,
                         ln_x_w, ln_x_b, attn_out_g, w_out, ffn_norm_g, w_gate, w_up, w_down, final_norm_g)))
    m = dict(zip(names, (m_mix_norm_g, m_w_in, m_mu_shift, m_decay_w0, m_decay_w2, m_iclr_a0, m_iclr_a2, m_gate_g2,
                         m_k_k, m_k_a, m_r_k, m_ln_x_w, m_ln_x_b, m_attn_out_g, m_w_out, m_ffn_norm_g, m_w_gate,
                         m_w_up, m_w_down, m_final_norm_g)))
    v = dict(zip(names, (v_mix_norm_g, v_w_in, v_mu_shift, v_decay_w0, v_decay_w2, v_iclr_a0, v_iclr_a2, v_gate_g2,
                         v_k_k, v_k_a, v_r_k, v_ln_x_w, v_ln_x_b, v_attn_out_g, v_w_out, v_ffn_norm_g, v_w_gate,
                         v_w_up, v_w_down, v_final_norm_g)))
    first = ("w_in", "lora")
    rest = ("w_out", "w_gate", "w_up", "w_down")
    xi, yi, ci = lax.axis_index("x"), lax.axis_index("y"), lax.axis_index("c")
    my_chip, my_dev = 2 * xi + yi, 4 * xi + 2 * yi + ci
    gather, scatter = ("gather",) * 4, ("scatter",) * 4

    def stored(d):
        out = {n: jnp.transpose(d[n][0]) if n in _TRANSPOSED else d[n][0] for n in MATS}
        out["lora"] = jnp.concatenate([d[n][0] for n in LORAS], axis=0)
        return out

    ws, ms, vs = stored(w), stored(m), stored(v)
    chip = jnp.reshape(my_chip, (1,)).astype(jnp.int32)
    early = _swap_start(None, [_cast_into_slot(ws["w_in"], chip, _ADAM_TILE["w_in"], "cast_w_in"),
                               _with_own((N_CHIP,) + ws["lora"].shape, F32, ws["lora"], my_chip)], gather[:2],
                        "gather_first_start")
    lands = [_cast_into_slot(ws[n], chip, _ADAM_TILE[n], "cast_" + n, after=early[4]) for n in rest]
    gather_rest = ("gather", "gather", "whole", "whole")
    ssem, rsem, srcs_thru, lands_thru, tok = _swap_start(None, lands, gather_rest, "gather_rest_start")
    got = _swap_wait(early[0], early[1], early[2], early[3], tok, gather[:2], "gather_first_wait")
    win_all, lora_all = _swap_gathered(got, "gather_first_halves")
    win = _full("w_in", win_all)
    w2p, a2p, g2m = _unstack_lora(lora_all)

    vecs = {n: w[n].reshape(1, sz) for n, sz in VECS}

    def get_rest(after):
        got_rest = _swap_wait(ssem, rsem, srcs_thru, lands_thru, after, gather_rest, "gather_rest_wait")
        swapped = _swap_gathered(got_rest[:2], "gather_rest_halves")
        return [_full(n, z) for n, z in zip(rest, [*swapped, *got_rest[2:]])]

    flight = []

    def my_half(g):
        h = g.shape[1] // 2
        return lax.dynamic_slice(g, (my_chip, ci * h, 0), (1, h, g.shape[2]))[0]

    def send_rest(gw):
        gs = [_by_chip(n, gw[n]) for n in rest]
        into = [_with_own((N_DEV,) + my_half(g).shape, BF16, my_half(g), my_dev) for g in gs]
        flight.extend(_swap_start(gs, into, scatter, "exchange_rest_start"))
        return flight[4]

    loss8, dx, gw, gv = _local_step(x[0], loss_target[0], win, vecs, w2p, a2p, g2m, get_rest, send_rest)

    core = jnp.reshape(ci, (1,)).astype(jnp.int32)
    gs = [_by_chip("w_in", gw["w_in"]), gw["lora"]]
    theirs = _sibling_halves(gs, "presum_halves")
    sums = [_add_halves(g, o, core, _SUM_TILE[n], "chipsum_" + n) for n, g, o in zip(first, gs, theirs)]
    own = [lax.dynamic_index_in_dim(s, my_chip, 0, keepdims=False) for s in sums]
    last = _swap_start(sums, [_with_own(s.shape, BF16, o, my_chip) for s, o in zip(sums, own)], ("chipsum",) * 2,
                       "exchange_first_start")
    small = _rowsum_small([gv[n] for n, _ in VECS], loss8, after=last[4])
    vecs_out = _swap_start([small], [_with_own((N_DEV,) + small.shape, F32, small, my_dev)], ("all",),
                           "exchange_vectors_start")


    def update(group, rbufs, tag):
        sums = [_reduce8(rb, core, _SUM_TILE[n], "reduce_" + n) for n, rb in zip(group, rbufs)]
        gsum = _join_halves(sums, "join_halves_" + tag)
        out = {}
        for n, g in zip(group, gsum):
            if n == "lora":
                r = _adamw_lora(g, *[[d[k] for k in LORAS] for d in (w, m, v)])
                for i, name in enumerate(LORAS):
                    out[name] = r[i::3]
            else:
                r = _adamw_call(g, ws[n][None], ms[n][None], vs[n][None], _ADAM_TILE[n], "adamw_" + n)
                out[n] = [jnp.transpose(z[0])[None] for z in r] if n in _TRANSPOSED else r
        return out, r[1]

    res, done = update(rest, _swap_wait(flight[0], flight[1], flight[2], flight[3], vecs_out[4], scatter,
                                        "exchange_rest_wait"), "rest")
    got = _swap_wait(last[0], last[1], last[2], last[3], done, ("chipsum",) * 2, "exchange_first_wait")
    res_first, done = update(first, got, "first")
    res.update(res_first)
    sbuf = _swap_wait(vecs_out[0], vecs_out[1], vecs_out[2], vecs_out[3], done, ("all",), "exchange_vectors_wait")[0]
    rows = lambda d: [d[n].reshape(-1, d[n].shape[-1]) for n, _ in VECS]
    small_res = _reduce_adamw_small(sbuf, rows(w), rows(m), rows(v))

    outs = []
    for k in range(4):
        piece = {n: r[k] for n, r in res.items()}
        for i, (n, _) in enumerate(VECS):
            piece[n] = small_res[k * len(VECS) + i].reshape(w[n].shape)
        outs.extend(piece[n] for n in names)
    return (small_res[-1][0, 0], dx[None], *outs)
```

```python
import jax
import jax.numpy as jnp
from jax import lax
from jax.experimental import pallas as pl
from jax.experimental.pallas import tpu as pltpu

F32 = jnp.float32
BF16 = jnp.bfloat16

D_MODEL = 1024
HEAD_DIM = 64
RW = 512
N_PAIR = RW // 128
SHIFT_COLS = 1792
IN_COLS = 3328
D_FF = 2816
FF_CHUNK = 256
NORM_EPS = 1e-6
GN_EPS = 64e-5
CHUNK = 64
SUB = 16
WKV_PASSES = 1
ATTN_PASSES = 1
ATTN_BLOCK = 128
DILATIONS = (1, 4, 16)
NEG = -1e30
ADAM_LR, ADAM_B1, ADAM_B2, ADAM_EPS, ADAM_WD, ADAM_STEP = 0.001, 0.9, 0.999, 1e-08, 0.01, 10
VMEM_LIMIT = 56 * 1024 * 1024
MESH = pl.DeviceIdType.MESH


def _params(sem=None, **kw):
    return pltpu.CompilerParams(dimension_semantics=sem, vmem_limit_bytes=VMEM_LIMIT, **kw)


def _dot(a, b):
    return lax.dot_general(a, b, (((1,), (0,)), ((), ())), preferred_element_type=F32)


def _dot_nt(a, b):
    return lax.dot_general(a, b, (((1,), (1,)), ((), ())), preferred_element_type=F32)


def _dot_tn(a, b):
    return lax.dot_general(a, b, (((0,), (0,)), ((), ())), preferred_element_type=F32)


_FORMS = {"nn": ((1,), (0,)), "nt": ((1,), (1,)), "tn": ((0,), (0,))}


def _dg(a, b, form):
    if a.ndim == 3 or b.ndim == 3:
        nb = a.shape[0] if a.ndim == 3 else b.shape[0]
        return jnp.stack([_dg(a[i] if a.ndim == 3 else a, b[i] if b.ndim == 3 else b, form) for i in range(nb)], axis=0)
    return lax.dot_general(a, b, (_FORMS[form], ((), ())), preferred_element_type=F32)


def _split2(x):
    hi = x.astype(BF16)
    return hi, (x - hi.astype(F32)).astype(BF16)


def _split3(x):
    hi = x.astype(BF16)
    rest = x - hi.astype(F32)
    mid = rest.astype(BF16)
    return hi, mid, (rest - mid.astype(F32)).astype(BF16)


def _mm_raw(a, b, form, mode):
    if mode == 1:
        return _dg(a.astype(BF16), b.astype(BF16), form)
    if mode == 3:
        ah, al = _split2(a)
        bh, bl = _split2(b)
        return _dg(ah, bh, form) + (_dg(ah, bl, form) + _dg(al, bh, form))
    if mode == "L3":
        ab = a.astype(BF16)
        b1, b2, b3 = _split3(b)
        if form == "nn":
            n = b.shape[-1]
            wide = _dg(ab, jnp.concatenate([b1, b2, b3], axis=-1), form)
            return wide[..., :n] + (wide[..., n:2 * n] + wide[..., 2 * n:])
        return _dg(ab, b1, form) + (_dg(ab, b2, form) + _dg(ab, b3, form))
    assert mode == "R3", mode
    bb = b.astype(BF16)
    a1, a2, a3 = _split3(a)
    if form in ("nn", "nt"):
        m = a.shape[-2]
        tall = _dg(jnp.concatenate([a1, a2, a3], axis=-2), bb, form)
        return tall[..., :m, :] + (tall[..., m:2 * m, :] + tall[..., 2 * m:, :])
    return _dg(a1, bb, form) + (_dg(a2, bb, form) + _dg(a3, bb, form))


def _mm(a, b, form, mode):
    @jax.custom_vjp
    def f(a, b):
        return _mm_raw(a, b, form, mode)

    def fwd(a, b):
        return _mm_raw(a, b, form, mode), (a, b)

    def bwd(res, ct):
        a, b = res
        la = {1: 1, 3: 3, "L3": None, "R3": "R3"}[mode]
        lb = {1: 1, 3: 3, "L3": "L3", "R3": None}[mode]
        if form == "nn":
            da = None if la is None else _mm_raw(ct, b, "nt", la)
            db = None if lb is None else _mm_raw(a, ct, "tn", lb)
        elif form == "nt":
            da = None if la is None else _mm_raw(ct, b, "nn", la)
            db = None if lb is None else _mm_raw(ct, a, "tn", "R3" if lb == "L3" else lb)
        else:
            da = None if la is None else _mm_raw(b, ct, "nt", "L3" if la == "R3" else la)
            db = None if lb is None else _mm_raw(a, ct, "nn", lb)
        return (jnp.zeros_like(a) if da is None else da, jnp.zeros_like(b) if db is None else db)

    f.defvjp(fwd, bwd)
    return f(a, b)


def _seg_ones(n):
    r = lax.broadcasted_iota(jnp.int32, (n, n), 0) // HEAD_DIM
    c = lax.broadcasted_iota(jnp.int32, (n, n), 1) // HEAD_DIM
    return (r == c).astype(F32)


def _segsum(x, seg):
    return _mm(x, seg, "nn", "R3")


def _rms_fwd(x, g):
    rstd = lax.rsqrt(jnp.mean(x * x, axis=-1, keepdims=True) + NORM_EPS)
    return x * rstd * g


def _rms_bwd(dy, x, g):
    rstd = lax.rsqrt(jnp.mean(x * x, axis=-1, keepdims=True) + NORM_EPS)
    xn = x * rstd
    dxn = dy * g
    dx = rstd * (dxn - xn * jnp.mean(dxn * xn, axis=-1, keepdims=True))
    return dx, dy * xn


def _sigmoid(x):
    return 1.0 / (1.0 + jnp.exp(-x))


def _softplus(x):
    return jnp.maximum(x, 0.0) + jnp.log(1.0 + jnp.exp(-jnp.abs(x)))


def _acc(ref, val, first):
    @pl.when(first)
    def _():
        ref[...] = val

    @pl.when(jnp.logical_not(first))
    def _():
        ref[...] += val


def _colsum8(v):
    rows, n = v.shape
    return jnp.sum(v.reshape(rows // 8, 8, n), axis=0)


def _prep_fn(p, pprev, mu, w0, w2p, a0, a2p, g2, k_k, k_a):
    seg = _seg_ones(RW)
    ps = p + (pprev - p) * mu
    r = ps[:, 0:RW]
    k = ps[:, RW:2 * RW]
    v = ps[:, 2 * RW:3 * RW]
    xwa = ps[:, 3 * RW:3 * RW + 128]
    xg = ps[:, 3 * RW + 128:3 * RW + 256]
    wraw = -_softplus(-(w0 + _mm(jnp.tanh(xwa), w2p, "nn", 3))) - 0.5
    lw = -jnp.exp(wraw)
    a = _sigmoid(a0 + _mm(xwa, a2p, "nn", 3))
    g = _mm(_sigmoid(xg), g2, "nn", 3)
    kk = k * k_k
    kk = kk / jnp.maximum(jnp.sqrt(_segsum(kk * kk, seg)), 1e-12)
    k2 = k * (1.0 + (a - 1.0) * k_a)
    return r, lw, k2, v, kk, a, g


def _transposed(z):
    return jnp.stack([z[i].T for i in range(z.shape[0])], axis=0) if z.ndim == 3 else z.T


def _solve_unit_lower(lmat, rhs):
    c = lmat.shape[-1]
    row = lax.broadcasted_iota(jnp.int32, (c, c), 0)
    col = lax.broadcasted_iota(jnp.int32, (c, c), 1)
    eye = (row == col).astype(F32)
    ld = jnp.where(row // SUB == col // SUB, lmat, 0.0)
    lo = lmat - ld
    x = eye + ld
    m = ld
    mm = lambda p, q: _mm(p, q, "nn", WKV_PASSES)
    cat = jnp.concatenate
    m = mm(m, m)
    for _ in range(2):
        mx = mm(m, cat([m, x], axis=-1))
        m, x = mx[..., :c], x + mx[..., c:]
    x = x + mm(m, x)
    gw = mm(x, cat([lo, rhs], axis=-1))
    g, w = gw[..., :c], gw[..., c:]
    gg = mm(g, cat([g, w], axis=-1))
    w = w + gg[..., c:]
    return w + mm(gg[..., :c], w)


def _wkv_chunk_fn(s0, r, lw, k, v, kk, a):
    c = r.shape[-2]
    n = 2 * c
    row = lax.broadcasted_iota(jnp.int32, (n, n), 0)
    col = lax.broadcasted_iota(jnp.int32, (n, n), 1)
    same = (row // c) == (col // c)
    incl = jnp.logical_and(row >= col, same)
    strict = jnp.logical_and(row > col, same)
    sel = (lax.broadcasted_iota(jnp.int32, (n, 128), 0) // c) == (lax.broadcasted_iota(jnp.int32, (n, 128), 1) // HEAD_DIM)
    two = lambda z: jnp.concatenate([z, z], axis=-2)
    lw2 = two(lw)
    mm = lambda p_, q_, form: _mm(p_, q_, form, WKV_PASSES)
    cl = _mm(incl.astype(F32), lw2, "nn", "L3")
    p = jnp.exp(cl)
    pinv = jnp.exp(-cl)
    pprev = jnp.exp(cl - lw2)
    kk2 = two(kk)
    at = jnp.where(sel, -kk2 * pprev, 0.0)
    bt = jnp.where(sel, kk2 * two(a) * pinv, 0.0)
    kt = jnp.where(sel, two(k) * pinv, 0.0)
    rt = jnp.where(sel, two(r) * p, 0.0)
    vt = jnp.where(sel, two(v), 0.0)
    cat = jnp.concatenate
    bk = cat([bt, kt], axis=-2)
    arbk = mm(cat([at, rt], axis=-2), bk, "nt")
    ab, ak = jnp.where(strict, arbk[..., :n, :n], 0.0), jnp.where(strict, arbk[..., :n, n:], 0.0)
    rb, rk = jnp.where(incl, arbk[..., n:, :n], 0.0), jnp.where(incl, arbk[..., n:, n:], 0.0)
    s0t = _transposed(s0)
    u = _solve_unit_lower(ab, mm(cat([at, ak], axis=-1), cat([s0t, vt], axis=-2), "nn"))
    y2 = mm(cat([rt, rb, rk], axis=-1), cat([s0t, u, vt], axis=-2), "nn")
    plast = jnp.exp(jnp.sum(lw, axis=-2, keepdims=True))
    s1 = (s0 + mm(cat([u, vt], axis=-2), bk, "tn")) * plast
    r2 = lax.broadcasted_iota(jnp.int32, (128, 128), 0) // HEAD_DIM
    c2 = lax.broadcasted_iota(jnp.int32, (128, 128), 1) // HEAD_DIM
    return y2[..., :c, :] + y2[..., c:, :], jnp.where(r2 == c2, s1, 0.0)


def _post_fn(y, r, k2, v, g, lnw, lnb, rk):
    seg = _seg_ones(RW)
    mean = _segsum(y, seg) * (1.0 / HEAD_DIM)
    yc = y - mean
    var = _segsum(yc * yc, seg) * (1.0 / HEAD_DIM)
    yn = yc * lax.rsqrt(var + GN_EPS)
    out = yn * lnw + lnb + _segsum(r * k2 * rk, seg) * v
    return out * g


def _attn_block_fn(q, kc, vc, kp=None, vp=None):
    n = ATTN_BLOCK
    qi = lax.broadcasted_iota(jnp.int32, (n, n), 0)
    kj = lax.broadcasted_iota(jnp.int32, (n, n), 1)
    lane = lax.broadcasted_iota(jnp.int32, (1, 128), 1)
    scale = HEAD_DIM ** -0.5
    valid = kj <= qi
    keys, vals = kc, vc
    if kp is not None:
        valid = jnp.concatenate([valid, kj >= qi], axis=-1)
        keys, vals = jnp.concatenate([kc, kp], axis=-2), jnp.concatenate([vc, vp], axis=-2)
    m0 = (lane // HEAD_DIM) == 0
    q2 = jnp.concatenate([jnp.where(m0, q, 0.0), jnp.where(m0, 0.0, q)], axis=-2)
    valid2 = jnp.concatenate([valid, valid], axis=-2)
    s = jnp.where(valid2, _mm(q2, keys, "nt", ATTN_PASSES) * scale, NEG)
    m = jnp.max(s, axis=-1, keepdims=True)
    p = jnp.exp(s - m)
    den = jnp.sum(p, axis=-1, keepdims=True)
    o2 = _mm(p, vals, "nn", ATTN_PASSES) / den
    l2 = m + jnp.log(den)
    return jnp.where(m0, o2[..., :n, :], o2[..., n:, :]), jnp.where(m0, l2[..., :n, :], l2[..., n:, :])


def _attn_block_bwd(q, kc, vc, kp, vp, o, lse, do, dl):
    n = ATTN_BLOCK
    cat = jnp.concatenate
    qi = lax.broadcasted_iota(jnp.int32, (n, n), 0)
    kj = lax.broadcasted_iota(jnp.int32, (n, n), 1)
    m0 = (lax.broadcasted_iota(jnp.int32, (1, 128), 1) // HEAD_DIM) == 0
    scale = HEAD_DIM ** -0.5
    valid = kj <= qi
    keys, vals = kc, vc
    if kp is not None:
        valid = cat([valid, kj >= qi], axis=-1)
        keys, vals = cat([kc, kp], axis=-2), cat([vc, vp], axis=-2)
    stack = lambda z: cat([jnp.where(m0, z, 0.0), jnp.where(m0, 0.0, z)], axis=-2)
    q2, do2 = stack(q), stack(do)
    lse2 = cat([jnp.max(jnp.where(m0, lse, NEG), axis=-1, keepdims=True),
                jnp.max(jnp.where(m0, NEG, lse), axis=-1, keepdims=True)], axis=-2)
    delta = jnp.sum(do2 * cat([o, o], axis=-2), axis=-1, keepdims=True)
    dlse = jnp.sum(stack(dl), axis=-1, keepdims=True)
    mm = lambda a, b, form: _mm_raw(a, b, form, ATTN_PASSES)
    s = jnp.where(cat([valid, valid], axis=-2), mm(q2, keys, "nt") * scale, NEG)
    p = jnp.exp(s - lse2)
    ds = p * (mm(do2, vals, "nt") - delta + dlse)
    dq2 = mm(ds, keys, "nn") * scale
    dq = jnp.where(m0, dq2[..., :n, :], dq2[..., n:, :])
    dkeys = mm(ds, q2, "tn") * scale
    dvals = mm(p, do2, "tn")
    if kp is None:
        return dq, dkeys, dvals
    return dq, dkeys[..., :n, :], dvals[..., :n, :], dkeys[..., n:, :], dvals[..., n:, :]


def _combine_fn(o1, o2, o3, l1, l2, l3, og):
    seg = _seg_ones(o1.shape[-1])
    m = jnp.maximum(jnp.maximum(l1, l2), l3)
    e1, e2, e3 = jnp.exp(l1 - m), jnp.exp(l2 - m), jnp.exp(l3 - m)
    o = (e1 * o1 + e2 * o2 + e3 * o3) / (e1 + e2 + e3)
    o = o * lax.rsqrt(_segsum(o * o, seg) * (1.0 / HEAD_DIM) + NORM_EPS)
    return o * og


def _shifted(p, last8, first):
    prow = jnp.where(first, 0.0, last8[7:8, :])
    rolled = pltpu.roll(p, 1, axis=0)
    rid = lax.broadcasted_iota(jnp.int32, p.shape, 0)
    return jnp.where(rid == 0, prow, rolled)


_PREP_TM = 256


def _prep_specs(tm):
    vec = lambda n: pl.BlockSpec((1, n), lambda i: (0, 0))
    mat = lambda r, n: pl.BlockSpec((r, n), lambda i: (0, 0))
    return [vec(SHIFT_COLS), vec(RW), mat(128, RW), vec(RW), mat(128, RW), mat(128, RW), vec(RW), vec(RW)]


def _in_proj_prep(x, g1, win, pw):
    t = x.shape[0]
    tm = _PREP_TM

    def body(x_ref, g_ref, w_ref, mu, w0, w2p, a0, a2p, g2, k_k, k_a, h_ref, pa_ref, qkv_ref, *rest):
        outs, carry = rest[:7], rest[7]

        @pl.when(pl.program_id(0) == 0)
        def _():
            carry[...] = jnp.zeros_like(carry)

        h = _rms_fwd(x_ref[...], g_ref[...]).astype(BF16)
        h_ref[...] = h
        proj = _dot_nt(h, w_ref[...])
        p = proj[:, :SHIFT_COLS]
        pa_ref[...] = p
        for j in range(3):
            for pr in range(N_PAIR):
                c0 = SHIFT_COLS + j * RW + pr * 128
                qkv_ref[j, pr] = proj[:, c0:c0 + 128]
        pprev = _shifted(p, carry[...], pl.program_id(0) == 0)
        carry[...] = p[tm - 8:, :]
        res = _prep_fn(p, pprev, mu[...], w0[...], w2p[...], a0[...], a2p[...], g2[...], k_k[...], k_a[...])
        for o_ref, val in zip(outs, res):
            o_ref[...] = val

    row = pl.BlockSpec((tm, RW), lambda i: (i, 0))
    return pl.pallas_call(
        body, name="in_proj_prep", grid=(t // tm,),
        in_specs=[pl.BlockSpec((tm, D_MODEL), lambda i: (i, 0)), pl.BlockSpec((1, D_MODEL), lambda i: (0, 0)),
                  pl.BlockSpec((IN_COLS, D_MODEL), lambda i: (0, 0))] + _prep_specs(tm),
        out_specs=[pl.BlockSpec((tm, D_MODEL), lambda i: (i, 0)), pl.BlockSpec((tm, SHIFT_COLS), lambda i: (i, 0)),
                   pl.BlockSpec((3, N_PAIR, tm, 128), lambda i: (0, 0, i, 0))] + [row] * 7,
        out_shape=[jax.ShapeDtypeStruct((t, D_MODEL), BF16), jax.ShapeDtypeStruct((t, SHIFT_COLS), F32),
                   jax.ShapeDtypeStruct((3, N_PAIR, t, 128), F32)] + [jax.ShapeDtypeStruct((t, RW), F32)] * 7,
        scratch_shapes=[pltpu.VMEM((8, SHIFT_COLS), F32)],
        compiler_params=_params(("arbitrary",)),
    )(x, g1, win, *pw)


def _pairs(ref):
    return jnp.stack([ref[:, 128 * p:128 * (p + 1)] for p in range(N_PAIR)], axis=0)


def _wkv_fwd(r, lw, k2, v, kk, a):
    t = r.shape[0]
    nc = t // CHUNK

    def body(r_ref, lw_ref, k_ref, v_ref, kk_ref, a_ref, y_ref, s_ref, st):
        @pl.when(pl.program_id(0) == 0)
        def _():
            st[...] = jnp.zeros_like(st)

        s0 = st[...]
        s_ref[0] = s0
        y, s1 = _wkv_chunk_fn(s0, *[_pairs(ref) for ref in (r_ref, lw_ref, k_ref, v_ref, kk_ref, a_ref)])
        for p in range(N_PAIR):
            y_ref[:, 128 * p:128 * (p + 1)] = y[p]
        st[...] = s1

    blk = pl.BlockSpec((CHUNK, RW), lambda c: (c, 0))
    return pl.pallas_call(
        body, name="wkv_fwd", grid=(nc,),
        in_specs=[blk] * 6,
        out_specs=[blk, pl.BlockSpec((1, N_PAIR, 128, 128), lambda c: (c, 0, 0, 0))],
        out_shape=[jax.ShapeDtypeStruct((t, RW), F32), jax.ShapeDtypeStruct((nc, N_PAIR, 128, 128), F32)],
        scratch_shapes=[pltpu.VMEM((N_PAIR, 128, 128), F32)],
        compiler_params=_params(("arbitrary",)),
    )(r, lw, k2, v, kk, a)


_POST_TM = 512


ATTN_GROUP = 2


def _dilated_rows(d, r, n):
    if d == 1:
        return pl.ds(pl.multiple_of(n * ATTN_BLOCK, ATTN_BLOCK), ATTN_BLOCK)
    return pl.ds(r + n * (ATTN_BLOCK * d), ATTN_BLOCK, stride=d)


def _for_each_sequence(t, unit):
    for di, d in enumerate(DILATIONS):

        @pl.when(pl.program_id(1) == di)
        def _(di=di, d=d):
            nb = t // (ATTN_BLOCK * d)
            if d == 1:
                unit(di, [(d, 0, 0)], False)
                unit(di, [(d, 0, 1)], True)
                lax.fori_loop(1, nb // 2, lambda k, c: (unit(di, [(d, 0, 2 * k), (d, 0, 2 * k + 1)], True), c)[1], 0)
            else:

                def residues(r, carry):
                    unit(di, [(d, r, 0), (d, r + d // 2, 0)], False)
                    if nb > 1:
                        lax.fori_loop(1, nb, lambda n, c: (unit(di, [(d, r, n), (d, r + d // 2, n)], True), c)[1], 0)
                    return carry

                lax.fori_loop(0, d // 2, residues, 0)


def _take(ref, lead, rows_list):
    return jnp.stack([ref.at[(*lead, g)][rows, :] for rows in rows_list for g in range(ref.shape[len(lead)])], axis=0)


def _put(ref, lead, rows_list, val, add=False):
    k = 0
    for rows in rows_list:
        for g in range(ref.shape[len(lead)]):
            if add:
                ref.at[(*lead, g)][rows, :] += val[k]
            else:
                ref.at[(*lead, g)][rows, :] = val[k]
            k += 1


def _attn_fwd(qkv):
    t = qkv.shape[2]

    def body(q_ref, k_ref, v_ref, o_ref, l_ref):
        def unit(di, places, has_prev):
            cur = [_dilated_rows(d, r, n) for d, r, n in places]
            args = [_take(ref, (0,), cur) for ref in (q_ref, k_ref, v_ref)]
            if has_prev:
                prv = [_dilated_rows(d, r, n - 1) for d, r, n in places]
                args += [_take(ref, (0,), prv) for ref in (k_ref, v_ref)]
            o, lse = _attn_block_fn(*args)
            _put(o_ref, (0,), cur, o)
            _put(l_ref, (0,), cur, lse)

        _for_each_sequence(t, unit)

    spec = lambda j: pl.BlockSpec((1, ATTN_GROUP, t, 128), lambda i, b: (j, i, 0, 0))
    out = pl.BlockSpec((1, ATTN_GROUP, t, 128), lambda i, b: (b, i, 0, 0))
    return pl.pallas_call(
        body, name="attn_fwd", grid=(N_PAIR // ATTN_GROUP, len(DILATIONS)),
        in_specs=[spec(0), spec(1), spec(2)], out_specs=[out, out],
        out_shape=[jax.ShapeDtypeStruct((3, N_PAIR, t, 128), F32)] * 2,
        compiler_params=_params(("parallel", "arbitrary")),
    )(qkv, qkv, qkv)


_COMB_TM = 512


def _mixers_out(y, r, k2, v, g, lnw, lnb, rk, o, l, og):
    t = y.shape[0]
    tm = _COMB_TM

    def body(y_ref, r_ref, k_ref, v_ref, g_ref, lnw_ref, lnb_ref, rk_ref, o_ref, l_ref, og_ref, out_ref):
        out_ref[:, :RW] = _post_fn(y_ref[...], r_ref[...], k_ref[...], v_ref[...], g_ref[...],
                                   lnw_ref[...], lnb_ref[...], rk_ref[...]).astype(BF16)
        for p in range(N_PAIR):
            cols = slice(128 * p, 128 * (p + 1))
            out_ref[:, RW + 128 * p:RW + 128 * (p + 1)] = _combine_fn(
                o_ref[0, p], o_ref[1, p], o_ref[2, p], l_ref[0, p], l_ref[1, p], l_ref[2, p], og_ref[:, cols]).astype(BF16)

    row = pl.BlockSpec((tm, RW), lambda i: (i, 0))
    vec = pl.BlockSpec((1, RW), lambda i: (0, 0))
    blk = pl.BlockSpec((3, N_PAIR, tm, 128), lambda i: (0, 0, i, 0))
    return pl.pallas_call(
        body, name="mixers_out", grid=(t // tm,),
        in_specs=[row] * 5 + [vec] * 3 + [blk, blk, vec], out_specs=pl.BlockSpec((tm, D_MODEL), lambda i: (i, 0)),
        out_shape=jax.ShapeDtypeStruct((t, D_MODEL), BF16),
        compiler_params=_params(("parallel",)),
    )(y, r, k2, v, g, lnw, lnb, rk, o, l, og)


def _ffn_all(x, ycat, wg, wu, wd, wout, g2, gf, tgt):
    t = x.shape[0]
    tm = 256

    def body(x_ref, y_ref, wg_ref, wu_ref, wd_ref, wo_ref, g2_ref, gf_ref, t_ref,
             h_ref, act_ref, dx2b_ref, dgt_ref, dup_ref, dx1b_ref, dx1_ref, dya_ref, dyb_ref, loss_ref, dgf_ref, dg2_ref,
             gt_s, up_s):
        first = pl.program_id(0) == 0
        x1 = x_ref[...] + _dot(y_ref[...], wo_ref[...])
        h = _rms_fwd(x1, g2_ref[...]).astype(BF16)
        h_ref[...] = h
        for c0 in range(0, D_FF, FF_CHUNK):
            cols = slice(c0, c0 + FF_CHUNK)
            gt = _dot_nt(h, wg_ref[cols, :])
            up = _dot_nt(h, wu_ref[cols, :])
            gt_s[:, cols] = gt.astype(BF16)
            up_s[:, cols] = up.astype(BF16)
            act_ref[:, cols] = (gt * _sigmoid(gt) * up).astype(BF16)
        x2 = x1 + _dot(act_ref[...], wd_ref[...])
        gf_ = gf_ref[...]
        diff = _rms_fwd(x2, gf_) - t_ref[...]
        lrow = 0.5 * jnp.sum(_colsum8(diff * diff), axis=1, keepdims=True) * (1.0 / D_MODEL)
        _acc(loss_ref, jnp.broadcast_to(lrow, (8, 128)), first)
        dx2, dgr = _rms_bwd(diff * (1.0 / D_MODEL), x2, gf_)
        _acc(dgf_ref, _colsum8(dgr), first)
        dx2b = dx2.astype(BF16)
        dx2b_ref[...] = dx2b
        for c0 in range(0, D_FF, FF_CHUNK):
            cols = slice(c0, c0 + FF_CHUNK)
            dact = _dot_nt(dx2b, wd_ref[cols, :])
            gt = gt_s[:, cols].astype(F32)
            sg = _sigmoid(gt)
            dgt_ref[:, cols] = (dact * up_s[:, cols].astype(F32) * sg * (1.0 + gt * (1.0 - sg))).astype(BF16)
            dup_ref[:, cols] = (dact * gt * sg).astype(BF16)
        dh = _dot(dgt_ref[...], wg_ref[...]) + _dot(dup_ref[...], wu_ref[...])
        dxn, dgr2 = _rms_bwd(dh, x1, g2_ref[...])
        _acc(dg2_ref, _colsum8(dgr2), first)
        dx1 = dx2 + dxn
        dx1_ref[...] = dx1
        dx1b = dx1.astype(BF16)
        dx1b_ref[...] = dx1b
        dy = _dot_nt(dx1b, wo_ref[...])
        dya_ref[...] = dy[:, :RW]
        dyb_ref[...] = dy[:, RW:]

    row = pl.BlockSpec((tm, D_MODEL), lambda i: (i, 0))
    wide = pl.BlockSpec((tm, D_FF), lambda i: (i, 0))
    half = pl.BlockSpec((tm, RW), lambda i: (i, 0))
    wsp = pl.BlockSpec((D_FF, D_MODEL), lambda i: (0, 0))
    vec = pl.BlockSpec((1, D_MODEL), lambda i: (0, 0))
    part = pl.BlockSpec((8, D_MODEL), lambda i: (0, 0))
    bf = lambda n: jax.ShapeDtypeStruct((t, n), BF16)
    return pl.pallas_call(
        body, name="ffn_all", grid=(t // tm,),
        in_specs=[row, row, wsp, wsp, wsp, pl.BlockSpec((D_MODEL, D_MODEL), lambda i: (0, 0)), vec, vec, row],
        out_specs=[row, wide, row, wide, wide, row, row, half, half, pl.BlockSpec((8, 128), lambda i: (0, 0)), part, part],
        out_shape=[bf(D_MODEL), bf(D_FF), bf(D_MODEL), bf(D_FF), bf(D_FF), bf(D_MODEL),
                   jax.ShapeDtypeStruct((t, D_MODEL), F32), jax.ShapeDtypeStruct((t, RW), F32),
                   jax.ShapeDtypeStruct((t, RW), F32), jax.ShapeDtypeStruct((8, 128), F32),
                   jax.ShapeDtypeStruct((8, D_MODEL), F32), jax.ShapeDtypeStruct((8, D_MODEL), F32)],
        scratch_shapes=[pltpu.VMEM((tm, D_FF), BF16), pltpu.VMEM((tm, D_FF), BF16)],
        compiler_params=_params(("arbitrary",)),
    )(x, ycat, wg, wu, wd, wout, g2, gf, tgt)


def _wgrad(a, b, tk, tn, name):
    t, kdim = a.shape
    ndim = b.shape[1]

    def body(a_ref, b_ref, o_ref):
        o_ref[...] = _dot_tn(a_ref[...], b_ref[...]).astype(BF16)

    return pl.pallas_call(
        body, name=name, grid=(kdim // tk, ndim // tn),
        in_specs=[pl.BlockSpec((t, tk), lambda i, j: (0, i)), pl.BlockSpec((t, tn), lambda i, j: (0, j))],
        out_specs=pl.BlockSpec((tk, tn), lambda i, j: (i, j)),
        out_shape=jax.ShapeDtypeStruct((kdim, ndim), BF16),
        compiler_params=_params(("parallel", "parallel")),
    )(a, b)


def _post_bwd(dya, y, r, k2, v, g, lnw, lnb, rk, after):
    t = y.shape[0]
    tm = _POST_TM

    def body(d_ref, y_ref, r_ref, k_ref, v_ref, g_ref, lnw_ref, lnb_ref, rk_ref, _,
             dy_ref, dr_ref, dk_ref, dv_ref, dg_ref, dlnw_ref, dlnb_ref, drk_ref):
        first = pl.program_id(0) == 0
        ones = jnp.ones((tm, 1), F32)
        prim = (y_ref[...], r_ref[...], k_ref[...], v_ref[...], g_ref[...],
                ones * lnw_ref[...], ones * lnb_ref[...], ones * rk_ref[...])
        _, vjp = jax.vjp(_post_fn, *prim)
        dy, dr, dk, dv, dg, dlnw, dlnb, drk = vjp(d_ref[...])
        dy_ref[...] = dy
        dr_ref[...] = dr
        dk_ref[...] = dk
        dv_ref[...] = dv
        dg_ref[...] = dg
        _acc(dlnw_ref, _colsum8(dlnw), first)
        _acc(dlnb_ref, _colsum8(dlnb), first)
        _acc(drk_ref, _colsum8(drk), first)

    row = pl.BlockSpec((tm, RW), lambda i: (i, 0))
    vec = pl.BlockSpec((1, RW), lambda i: (0, 0))
    part = pl.BlockSpec((8, RW), lambda i: (0, 0))
    return pl.pallas_call(
        body, name="rwkv_post_bwd", grid=(t // tm,),
        in_specs=[row] * 6 + [vec] * 3 + [ANY], out_specs=[row] * 5 + [part] * 3,
        out_shape=[jax.ShapeDtypeStruct((t, RW), F32)] * 5 + [jax.ShapeDtypeStruct((8, RW), F32)] * 3,
        compiler_params=_params(("arbitrary",)),
    )(dya, y, r, k2, v, g, lnw, lnb, rk, after)


def _wkv_bwd(dy, s0s, r, lw, k2, v, kk, a):
    t = r.shape[0]
    nc = t // CHUNK

    def body(dy_ref, s_ref, r_ref, lw_ref, k_ref, v_ref, kk_ref, a_ref,
             dr_ref, dlw_ref, dk_ref, dv_ref, dkk_ref, da_ref, ds):
        @pl.when(pl.program_id(0) == 0)
        def _():
            ds[...] = jnp.zeros_like(ds)

        _, vjp = jax.vjp(_wkv_chunk_fn, s_ref[0],
                         *[_pairs(ref) for ref in (r_ref, lw_ref, k_ref, v_ref, kk_ref, a_ref)])
        res = vjp((_pairs(dy_ref), ds[...]))
        ds[...] = res[0]
        for ref, val in zip((dr_ref, dlw_ref, dk_ref, dv_ref, dkk_ref, da_ref), res[1:]):
            for p in range(N_PAIR):
                ref[:, 128 * p:128 * (p + 1)] = val[p]

    blk = pl.BlockSpec((CHUNK, RW), lambda c: (nc - 1 - c, 0))
    return pl.pallas_call(
        body, name="wkv_bwd", grid=(nc,),
        in_specs=[blk, pl.BlockSpec((1, N_PAIR, 128, 128), lambda c: (nc - 1 - c, 0, 0, 0))] + [blk] * 6,
        out_specs=[blk] * 6,
        out_shape=[jax.ShapeDtypeStruct((t, RW), F32)] * 6,
        scratch_shapes=[pltpu.VMEM((N_PAIR, 128, 128), F32)],
        compiler_params=_params(("arbitrary",)),
    )(dy, s0s, r, lw, k2, v, kk, a)


def _prep_in_proj_bwd(proj, pw, douts, dq, dk, dv, win, x, g1, dx1):
    t = proj.shape[0]
    tm = _PREP_TM
    nt = t // tm

    def body(p_ref, l8_ref, mu, w0, w2p, a0, a2p, g2, k_k, k_a, dr, dr2, dlw, dk2, dk22, dv, dv2, dkk, da, dg,
             dq_ref, dkq_ref, dvq_ref, w_ref, x_ref, g1_ref, dx1_ref,
             dproj_ref, dx_ref, dg1_ref, dmu_ref, dw0_ref, dw2_ref, da0_ref, da2_ref, dg2_ref, dkk_ref, dka_ref, carry):
        i = pl.program_id(0)
        first = i == 0

        @pl.when(first)
        def _():
            carry[...] = jnp.zeros_like(carry)

        p = p_ref[...]
        pprev = _shifted(p, l8_ref[...], i == nt - 1)
        ones = jnp.ones((tm, 1), F32)
        prim = (p, pprev, ones * mu[...], ones * w0[...], w2p[...], ones * a0[...], a2p[...], g2[...],
                ones * k_k[...], ones * k_a[...])
        _, vjp = jax.vjp(_prep_fn, *prim)
        dp, dpp, dmu, dw0, dw2, da0, da2, dg2, dkk_, dka = vjp(
            (dr[...] + dr2[...], dlw[...], dk2[...] + dk22[...], dv[...] + dv2[...], dkk[...], da[...], dg[...]))
        up = pltpu.roll(dpp, tm - 1, axis=0)
        rid = lax.broadcasted_iota(jnp.int32, dpp.shape, 0)
        dpa = dp + jnp.where(rid == tm - 1, carry[0:1, :], up)
        carry[...] = jnp.broadcast_to(dpp[0:1, :], carry.shape)
        _acc(dmu_ref, _colsum8(dmu), first)
        _acc(dw0_ref, _colsum8(dw0), first)
        _acc(dw2_ref, dw2, first)
        _acc(da0_ref, _colsum8(da0), first)
        _acc(da2_ref, da2, first)
        _acc(dg2_ref, dg2, first)
        _acc(dkk_ref, _colsum8(dkk_), first)
        _acc(dka_ref, _colsum8(dka), first)
        parts = [dpa] + [ref[pr] for ref in (dq_ref, dkq_ref, dvq_ref) for pr in range(N_PAIR)]
        dproj = jnp.concatenate([z.astype(BF16) for z in parts], axis=1)
        dproj_ref[...] = dproj
        dxn, dgr = _rms_bwd(_dot(dproj, w_ref[...]), x_ref[...], g1_ref[...])
        dx_ref[...] = dx1_ref[...] + dxn
        _acc(dg1_ref, _colsum8(dgr), first)

    rev = lambda i: (nt - 1 - i, 0)
    row = pl.BlockSpec((tm, RW), rev)
    wide = pl.BlockSpec((tm, D_MODEL), rev)
    pair = pl.BlockSpec((N_PAIR, tm, 128), lambda i: (0, nt - 1 - i, 0))
    part = lambda n: pl.BlockSpec((8, n), lambda i: (0, 0))
    mat = pl.BlockSpec((128, RW), lambda i: (0, 0))
    return pl.pallas_call(
        body, name="prep_in_proj_bwd", grid=(nt,),
        in_specs=[pl.BlockSpec((tm, SHIFT_COLS), rev),
                  pl.BlockSpec((8, SHIFT_COLS), lambda i: (jnp.maximum((nt - 1 - i) * (tm // 8) - 1, 0), 0))]
                 + _prep_specs(tm) + [row] * 10
                 + [pair] * 3 + [pl.BlockSpec((IN_COLS, D_MODEL), lambda i: (0, 0)), wide,
                                 pl.BlockSpec((1, D_MODEL), lambda i: (0, 0)), wide],
        out_specs=[pl.BlockSpec((tm, IN_COLS), rev), wide, part(D_MODEL), part(SHIFT_COLS), part(RW), mat, part(RW), mat,
                   mat, part(RW), part(RW)],
        out_shape=[jax.ShapeDtypeStruct((t, IN_COLS), BF16), jax.ShapeDtypeStruct((t, D_MODEL), F32),
                   jax.ShapeDtypeStruct((8, D_MODEL), F32), jax.ShapeDtypeStruct((8, SHIFT_COLS), F32),
                   jax.ShapeDtypeStruct((8, RW), F32), jax.ShapeDtypeStruct((128, RW), F32),
                   jax.ShapeDtypeStruct((8, RW), F32), jax.ShapeDtypeStruct((128, RW), F32),
                   jax.ShapeDtypeStruct((128, RW), F32), jax.ShapeDtypeStruct((8, RW), F32),
                   jax.ShapeDtypeStruct((8, RW), F32)],
        scratch_shapes=[pltpu.VMEM((8, SHIFT_COLS), F32)],
        compiler_params=_params(("arbitrary",)),
    )(proj, proj, *pw, *douts, dq, dk, dv, win, x, g1, dx1)


def _combine_bwd(dyb, o, l, og):
    t = dyb.shape[0]
    tm = _COMB_TM

    def body(d_ref, o_ref, l_ref, og_ref, do_ref, dl_ref, dog_ref):
        ones = jnp.ones((tm, 1), F32)
        dog = []
        for p in range(N_PAIR):
            cols = slice(128 * p, 128 * (p + 1))
            _, vjp = jax.vjp(_combine_fn, o_ref[0, p], o_ref[1, p], o_ref[2, p], l_ref[0, p], l_ref[1, p], l_ref[2, p],
                             ones * og_ref[:, cols])
            res = vjp(d_ref[:, cols])
            for b in range(3):
                do_ref[b, p] = res[b]
                dl_ref[b, p] = res[3 + b]
            dog.append(_colsum8(res[6]))
        _acc(dog_ref, jnp.concatenate(dog, axis=1), pl.program_id(0) == 0)

    blk = pl.BlockSpec((3, N_PAIR, tm, 128), lambda i: (0, 0, i, 0))
    return pl.pallas_call(
        body, name="attn_combine_bwd", grid=(t // tm,),
        in_specs=[pl.BlockSpec((tm, RW), lambda i: (i, 0)), blk, blk, pl.BlockSpec((1, RW), lambda i: (0, 0))],
        out_specs=[blk, blk, pl.BlockSpec((8, RW), lambda i: (0, 0))],
        out_shape=[jax.ShapeDtypeStruct((3, N_PAIR, t, 128), F32)] * 2 + [jax.ShapeDtypeStruct((8, RW), F32)],
        compiler_params=_params(("arbitrary",)),
    )(dyb, o, l, og)


def _attn_bwd(do, dl, o, lse, qkv):
    t = qkv.shape[2]

    def body(do_ref, dl_ref, o_ref, l_ref, q_ref, k_ref, v_ref, dq_ref, dk_ref, dv_ref):
        @pl.when(pl.program_id(1) == 0)
        def _():
            for ref in (dq_ref, dk_ref, dv_ref):
                ref[...] = jnp.zeros_like(ref)

        def unit(di, places, has_prev):
            cur = [_dilated_rows(d, r, n) for d, r, n in places]
            q, kc, vc = [_take(ref, (0,), cur) for ref in (q_ref, k_ref, v_ref)]
            kp = vp = None
            if has_prev:
                prv = [_dilated_rows(d, r, n - 1) for d, r, n in places]
                kp, vp = [_take(ref, (0,), prv) for ref in (k_ref, v_ref)]
            res = _attn_block_bwd(q, kc, vc, kp, vp, *[_take(ref, (0,), cur) for ref in (o_ref, l_ref, do_ref, dl_ref)])
            _put(dq_ref, (), cur, res[0], add=True)
            _put(dk_ref, (), cur, res[1], add=True)
            _put(dv_ref, (), cur, res[2], add=True)
            if has_prev:
                _put(dk_ref, (), prv, res[3], add=True)
                _put(dv_ref, (), prv, res[4], add=True)

        _for_each_sequence(t, unit)

    spec = lambda j: pl.BlockSpec((1, ATTN_GROUP, t, 128), lambda i, b: (j, i, 0, 0))
    branch = pl.BlockSpec((1, ATTN_GROUP, t, 128), lambda i, b: (b, i, 0, 0))
    out = pl.BlockSpec((ATTN_GROUP, t, 128), lambda i, b: (i, 0, 0))
    return pl.pallas_call(
        body, name="attn_bwd", grid=(N_PAIR // ATTN_GROUP, len(DILATIONS)),
        in_specs=[branch] * 4 + [spec(0), spec(1), spec(2)], out_specs=[out] * 3,
        out_shape=[jax.ShapeDtypeStruct((N_PAIR, t, 128), F32)] * 3,
        compiler_params=_params(("parallel", "arbitrary")),
    )(do, dl, o, lse, qkv, qkv, qkv)


def _unstack_lora(lora_all):
    def body(l_ref, w_ref, a_ref, g_ref):
        z = jnp.zeros((64, 128), F32)
        for p in range(N_CHIP):
            cols = slice(128 * p, 128 * (p + 1))
            w_ref[0:64, cols] = l_ref[p, 0:64, :]
            w_ref[64:128, cols] = z
            a_ref[0:64, cols] = z
            a_ref[64:128, cols] = l_ref[p, 64:128, :]
            g_ref[:, cols] = l_ref[p, 128:256, :]

    return pl.pallas_call(body, name="unstack_lora", out_shape=[jax.ShapeDtypeStruct((128, RW), F32)] * 3)(lora_all)


def _local_step(x, tgt, win, vecs, w2p, a2p, g2m, get_rest, send_rest):
    pw = (vecs["mu_shift"], vecs["decay_w0"], w2p, vecs["iclr_a0"], a2p, g2m, vecs["k_k"], vecs["k_a"])
    h, proj, qkv, r, lw, k2, v, kk, a, g = _in_proj_prep(x, vecs["mix_norm_g"], win, pw)
    y, s0s = _wkv_fwd(r, lw, k2, v, kk, a)
    o_att, l_att = _attn_fwd(qkv)
    ycat = _mixers_out(y, r, k2, v, g, vecs["ln_x_w"], vecs["ln_x_b"], vecs["r_k"], o_att, l_att, vecs["attn_out_g"])
    wout, wg, wu, wd = get_rest(ycat)
    h2, act, dx2b, dgt, dup, dx1b, dx1, dya, dyb, loss8, dgf, dg2n = _ffn_all(
        x, ycat, wg, wu, wd, wout, vecs["ffn_norm_g"], vecs["final_norm_g"], tgt)
    gw = {
        "w_down": _wgrad(act, dx2b, 1408, 1024, "wgrad_down"),
        "w_gate": _wgrad(dgt, h2, 1408, 1024, "wgrad_gate"),
        "w_up": _wgrad(dup, h2, 1408, 1024, "wgrad_up"),
        "w_out": _wgrad(ycat, dx1b, 1024, 1024, "wgrad_out"),
    }

    dy, dr_p, dk2_p, dv_p, dg, dlnw, dlnb, drk = _post_bwd(dya, y, r, k2, v, g, vecs["ln_x_w"], vecs["ln_x_b"], vecs["r_k"],
                                                           after=send_rest(gw))
    dr_s, dlw, dk2_s, dv_s, dkk, da = _wkv_bwd(dy, s0s, r, lw, k2, v, kk, a)
    do_att, dl_att, dog = _combine_bwd(dyb, o_att, l_att, vecs["attn_out_g"])
    dq, dk, dv = _attn_bwd(do_att, dl_att, o_att, l_att, qkv)
    dproj, dx, dg1, dmu, dw0, dw2p, da0, da2p, dg2m, dk_k, dk_a = _prep_in_proj_bwd(
        proj, pw, (dr_p, dr_s, dlw, dk2_p, dk2_s, dv_p, dv_s, dkk, da, dg), dq, dk, dv, win, x, vecs["mix_norm_g"], dx1)
    gw["w_in"] = _wgrad(dproj, h, 1664, 1024, "wgrad_in")
    gw["decay_w2"] = dw2p[:64]
    gw["iclr_a2"] = da2p[64:]
    gw["gate_g2"] = dg2m
    gv = {"mix_norm_g": dg1, "mu_shift": dmu, "decay_w0": dw0, "iclr_a0": da0, "k_k": dk_k, "k_a": dk_a, "r_k": drk,
          "ln_x_w": dlnw, "ln_x_b": dlnb, "attn_out_g": dog, "ffn_norm_g": dg2n, "final_norm_g": dgf}
    return loss8, dx, gw, gv


N_CHIP = 4
N_DEV = 8
MATS = ("w_in", "w_out", "w_gate", "w_up", "w_down")
LORAS = ("decay_w2", "iclr_a2", "gate_g2")
VECS = (("mix_norm_g", 1024), ("mu_shift", 1792), ("decay_w0", 512), ("iclr_a0", 512), ("k_k", 512), ("k_a", 512),
        ("r_k", 512), ("ln_x_w", 512), ("ln_x_b", 512), ("attn_out_g", 512), ("ffn_norm_g", 1024),
        ("final_norm_g", 1024))
N_VEC = sum(n for _, n in VECS)
N_SMALL = N_VEC + 128
ANY = pl.BlockSpec(memory_space=pl.ANY)


def _flip(v, f):
    return 1 - v if f else v


class _Me:
    def __init__(self, mode):
        x, y, c = lax.axis_index("x"), lax.axis_index("y"), lax.axis_index("c")
        self.core, self.chip, self.dev = c, 2 * x + y, 4 * x + 2 * y + c
        self.sibling = (x, y, 1 - c)
        if mode == "chips":
            self.peers = [(px, py, c) for px, py in ((1 - x, y), (x, 1 - y), (1 - x, 1 - y))]
        else:
            self.peers = [(_flip(x, k & 4), _flip(y, k & 2), _flip(c, k & 1)) for k in range(1, N_DEV)]


def _half(core, rows):
    h = rows // 2
    return pl.ds(pl.multiple_of(core * h, h), h)


_BY_CHIP = ("gather", "whole", "chipsum")


def _peer_copy(srcs, dsts, kinds, send_sems, recv_sems, me, j, i, incoming):
    px, py, pc = me.peers[j]
    pchip, pdev = 2 * px + py, 4 * px + 2 * py + pc
    src, dst, kind = srcs[i], dsts[i], kinds[i]
    if kind in ("gather", "whole"):
        rows = _half(me.core, src.shape[1]) if kind == "gather" else pl.ds(0, src.shape[1])
        src, dst = src.at[me.chip, rows], dst.at[pchip if incoming else me.chip, rows]
    elif kind == "scatter":
        src, dst = src.at[pchip, _half(pc, src.shape[1])], dst.at[pdev if incoming else me.dev]
    elif kind == "chipsum":
        src, dst = src.at[pchip], dst.at[pchip if incoming else me.chip]
    else:
        dst = dst.at[pdev if incoming else me.dev]
    n = len(srcs)
    return pltpu.make_async_remote_copy(src_ref=src, dst_ref=dst, send_sem=send_sems.at[n * j + i],
                                        recv_sem=recv_sems.at[n * j + i], device_id=(px, py, pc), device_id_type=MESH)


def _mode(kinds):
    return "chips" if kinds[0] in _BY_CHIP else "devs"


def _npeer(kinds):
    return N_CHIP - 1 if kinds[0] in _BY_CHIP else N_DEV - 1


def _sibling_halves(gs, name):
    n = len(gs)

    def body(*refs):
        srcs, dsts, send_sems, recv_sems = refs[:n], refs[n:2 * n], refs[2 * n], refs[2 * n + 1]
        me = _Me("chips")

        def copy(i, p):
            return pltpu.make_async_remote_copy(
                src_ref=srcs[i].at[p, _half(1 - me.core, srcs[i].shape[1])], dst_ref=dsts[i].at[p],
                send_sem=send_sems.at[N_CHIP * i + p], recv_sem=recv_sems.at[N_CHIP * i + p],
                device_id=me.sibling, device_id_type=MESH)

        copies = [copy(i, p) for i in range(n) for p in range(N_CHIP)]
        for cp in copies:
            cp.start()
        for cp in copies:
            cp.wait()

    return pl.pallas_call(
        body, name=name, in_specs=[ANY] * n, out_specs=[ANY] * n,
        out_shape=[jax.ShapeDtypeStruct((N_CHIP, g.shape[1] // 2, g.shape[2]), g.dtype) for g in gs],
        scratch_shapes=[pltpu.SemaphoreType.DMA((N_CHIP * n,)), pltpu.SemaphoreType.DMA((N_CHIP * n,))],
    )(*gs)


def _add_halves(g, other, core, tr, name):
    _, h, cols = other.shape

    def body(core_ref, g_ref, o_ref, out_ref):
        out_ref[...] = (g_ref[...].astype(F32) + o_ref[...].astype(F32)).astype(BF16)

    blk = lambda off: pl.BlockSpec((1, tr, cols), lambda p, i, core_ref: (p, core_ref[0] * (h // tr) * off + i, 0))
    return pl.pallas_call(
        body, name=name,
        grid_spec=pltpu.PrefetchScalarGridSpec(num_scalar_prefetch=1, grid=(N_CHIP, h // tr),
                                               in_specs=[blk(1), blk(0)], out_specs=blk(0)),
        out_shape=jax.ShapeDtypeStruct(other.shape, BF16),
        compiler_params=_params(("parallel", "parallel")),
    )(core, g, other)


def _swap_gathered(lands, name):
    n = len(lands)

    def body(*refs):
        dsts, send_sems, recv_sems = refs[n:2 * n], refs[2 * n], refs[2 * n + 1]
        me = _Me("chips")

        def copy(j, i, incoming):
            px, py, _ = me.peers[j]
            rows_out, rows_in = _half(me.core, dsts[i].shape[1]), _half(1 - me.core, dsts[i].shape[1])
            return pltpu.make_async_remote_copy(
                src_ref=dsts[i].at[2 * px + py, rows_out], dst_ref=dsts[i].at[2 * px + py, rows_in if incoming else rows_out],
                send_sem=send_sems.at[n * j + i], recv_sem=recv_sems.at[n * j + i], device_id=me.sibling, device_id_type=MESH)

        sends = [copy(j, i, False) for j in range(3) for i in range(n)]
        for cp in sends:
            cp.start()
        for j in range(3):
            for i in range(n):
                copy(j, i, True).wait_recv()
        for cp in sends:
            cp.wait_send()

    return pl.pallas_call(
        body, name=name, in_specs=[ANY] * n, out_specs=[ANY] * n,
        out_shape=[jax.ShapeDtypeStruct(l.shape, l.dtype) for l in lands],
        input_output_aliases={i: i for i in range(n)},
        scratch_shapes=[pltpu.SemaphoreType.DMA((3 * n,)), pltpu.SemaphoreType.DMA((3 * n,))],
    )(*lands)


def _join_halves(sums, name):
    n = len(sums)

    def body(*refs):
        dsts, send_sems, recv_sems = refs[n:2 * n], refs[2 * n], refs[2 * n + 1]
        me = _Me("chips")

        def copy(i, incoming):
            mine, other = _half(me.core, dsts[i].shape[0]), _half(1 - me.core, dsts[i].shape[0])
            return pltpu.make_async_remote_copy(src_ref=dsts[i].at[mine], dst_ref=dsts[i].at[other if incoming else mine],
                                                send_sem=send_sems.at[i], recv_sem=recv_sems.at[i],
                                                device_id=me.sibling, device_id_type=MESH)

        sends = [copy(i, False) for i in range(n)]
        for cp in sends:
            cp.start()
        for i in range(n):
            copy(i, True).wait_recv()
        for cp in sends:
            cp.wait_send()

    return pl.pallas_call(
        body, name=name, in_specs=[ANY] * n, out_specs=[ANY] * n,
        out_shape=[jax.ShapeDtypeStruct(s.shape, s.dtype) for s in sums],
        input_output_aliases={i: i for i in range(n)},
        scratch_shapes=[pltpu.SemaphoreType.DMA((n,)), pltpu.SemaphoreType.DMA((n,))],
    )(*sums)


HBM = pl.BlockSpec(memory_space=pltpu.HBM)
SEM = pl.BlockSpec(memory_space=pltpu.SEMAPHORE)
EFFECT = pltpu.SideEffectType.DATAFLOW_SIDE_EFFECTING


def _swap_start(arrs, lands, kinds, name):
    n = len(lands)
    ops = list(lands) if arrs is None else [*arrs, *lands]
    k = len(ops)

    def body(*refs):
        srcs, dsts, send_sems, recv_sems, token = refs[:n], refs[k - n:k], refs[k], refs[k + 1], refs[-1]
        me = _Me(_mode(kinds))
        for j in range(len(me.peers)):
            for i in range(n):
                _peer_copy(srcs, dsts, kinds, send_sems, recv_sems, me, j, i, False).start()
        token[...] = jnp.zeros_like(token)

    ns = _npeer(kinds) * n
    outs = pl.pallas_call(
        body, name=name,
        out_shape=(pltpu.SemaphoreType.DMA((ns,)), pltpu.SemaphoreType.DMA((ns,)),
                   *[pltpu.HBM(a.shape, a.dtype) for a in ops], jax.ShapeDtypeStruct((8, 128), F32)),
        in_specs=[HBM] * k, out_specs=(SEM, SEM, *[HBM] * k, pl.BlockSpec(memory_space=pltpu.VMEM)),
        input_output_aliases={i: 2 + i for i in range(k)},
        compiler_params=pltpu.CompilerParams(has_side_effects=EFFECT),
    )(*[pltpu.with_memory_space_constraint(a, pltpu.HBM) for a in ops])
    return outs[0], outs[1], outs[2:2 + k - n], outs[2 + k - n:2 + k], outs[-1]


def _swap_wait(send_sems, recv_sems, srcs_thru, lands_thru, after, kinds, name):
    n = len(lands_thru)
    ops = [*srcs_thru, *lands_thru]
    k = len(ops)

    def body(*refs):
        srcs, dsts, s_sems, r_sems = refs[:n], refs[k - n:k], refs[k], refs[k + 1]
        me = _Me(_mode(kinds))
        for j in range(len(me.peers)):
            for i in range(n):
                cp = _peer_copy(srcs, dsts, kinds, s_sems, r_sems, me, j, i, True)
                cp.wait_send()
                cp.wait_recv()

    outs = pl.pallas_call(
        body, name=name,
        out_shape=tuple(pltpu.HBM(a.shape, a.dtype) for a in ops),
        in_specs=[HBM] * k + [SEM, SEM, ANY], out_specs=tuple([HBM] * k),
        input_output_aliases={i: i for i in range(k)},
        compiler_params=pltpu.CompilerParams(has_side_effects=EFFECT),
    )(*ops, send_sems, recv_sems, after)
    return outs[k - n:]


def _wait_and_swap(send_sems, recv_sems, lands_thru, after, kinds, name):
    n = len(lands_thru)

    pairs = [(j, i) for j in range(3) for i in range(n)]

    def swap(dsts, d_send, d_recv, me, j, i, incoming):
        px, py, _ = me.peers[j]
        rows_out, rows_in = _half(me.core, dsts[i].shape[1]), _half(1 - me.core, dsts[i].shape[1])
        return pltpu.make_async_remote_copy(
            src_ref=dsts[i].at[2 * px + py, rows_out], dst_ref=dsts[i].at[2 * px + py, rows_in if incoming else rows_out],
            send_sem=d_send.at[n * j + i], recv_sem=d_recv.at[n * j + i], device_id=me.sibling, device_id_type=MESH)

    def arrive_and_start(*refs):
        dsts, s_sems, r_sems, d_send, d_recv = refs[:n], refs[n], refs[n + 1], refs[n + 3], refs[n + 4]
        me = _Me("chips")
        for j in range(3):
            for i in range(n):
                cp = _peer_copy(dsts, dsts, kinds, s_sems, r_sems, me, j, i, True)
                cp.wait_recv()
                cp.wait_send()
            for i in range(n):
                swap(dsts, d_send, d_recv, me, j, i, False).start()

    outs = pl.pallas_call(
        arrive_and_start, name=name + "_start",
        out_shape=(pltpu.SemaphoreType.DMA((3 * n,)), pltpu.SemaphoreType.DMA((3 * n,)),
                   *[pltpu.HBM(a.shape, a.dtype) for a in lands_thru]),
        in_specs=[HBM] * n + [SEM, SEM, ANY], out_specs=(SEM, SEM, *[HBM] * n),
        input_output_aliases={i: 2 + i for i in range(n)},
        compiler_params=pltpu.CompilerParams(has_side_effects=EFFECT),
    )(*lands_thru, send_sems, recv_sems, after)

    def swapped(*refs):
        dsts, d_send, d_recv = refs[:n], refs[n], refs[n + 1]
        me = _Me("chips")
        for j, i in pairs:
            cp = swap(dsts, d_send, d_recv, me, j, i, True)
            cp.wait_recv()
            cp.wait_send()

    return list(pl.pallas_call(
        swapped, name=name + "_wait",
        out_shape=tuple(pltpu.HBM(a.shape, a.dtype) for a in lands_thru),
        in_specs=[HBM] * n + [SEM, SEM], out_specs=tuple([HBM] * n),
        input_output_aliases={i: i for i in range(n)},
        compiler_params=pltpu.CompilerParams(has_side_effects=EFFECT),
    )(*outs[2:], outs[0], outs[1]))


def _adamw(w, g, m, v):
    m = ADAM_B1 * m + (1.0 - ADAM_B1) * g
    v = ADAM_B2 * v + (1.0 - ADAM_B2) * (g * g)
    m_hat = m / (1.0 - ADAM_B1 ** ADAM_STEP)
    v_hat = v / (1.0 - ADAM_B2 ** ADAM_STEP)
    delta = -ADAM_LR * (m_hat / (jnp.sqrt(v_hat) + ADAM_EPS) + ADAM_WD * w)
    return delta, m, v


def _reduce8(rbuf, core, tr, name):
    slots, h, cols = rbuf.shape

    def body(core_ref, r_ref, g_ref):
        g = r_ref[0].astype(F32)
        for s in range(1, slots):
            g = g + r_ref[s].astype(F32)
        g_ref[...] = g

    return pl.pallas_call(
        body, name=name,
        grid_spec=pltpu.PrefetchScalarGridSpec(
            num_scalar_prefetch=1, grid=(h // tr,),
            in_specs=[pl.BlockSpec((slots, tr, cols), lambda i, core_ref: (0, i, 0))],
            out_specs=pl.BlockSpec((tr, cols), lambda i, core_ref: (core_ref[0] * (h // tr) + i, 0))),
        out_shape=jax.ShapeDtypeStruct((2 * h, cols), F32),
        compiler_params=_params(("parallel",)),
    )(core, rbuf)


def _adamw_call(g, w, m, v, tr, name):
    _, rows, cols = w.shape

    def body(g_in, w_ref, m_ref, v_ref, g_ref, d_ref, nm_ref, nv_ref):
        g = g_in[...]
        g_ref[0] = g
        d_ref[0], nm_ref[0], nv_ref[0] = _adamw(w_ref[0], g, m_ref[0], v_ref[0])

    row = pl.BlockSpec((1, tr, cols), lambda i: (0, i, 0))
    return pl.pallas_call(
        body, name=name, grid=(rows // tr,),
        in_specs=[pl.BlockSpec((tr, cols), lambda i: (i, 0)), row, row, row], out_specs=[row] * 4,
        out_shape=[jax.ShapeDtypeStruct(w.shape, F32)] * 4,
        compiler_params=_params(("parallel",)),
    )(g, w, m, v)


def _adamw_lora(g, ws3, ms3, vs3):
    def body(g_in, *refs):
        ins, outs = refs[:9], refs[9:]
        r0 = 0
        for i in range(3):
            rows = ins[i].shape[1]
            g = g_in[r0:r0 + rows, :]
            outs[i][0] = g
            outs[3 + i][0], outs[6 + i][0], outs[9 + i][0] = _adamw(ins[i][0], g, ins[3 + i][0], ins[6 + i][0])
            r0 += rows

    return pl.pallas_call(body, name="adamw_lora",
                          out_shape=[jax.ShapeDtypeStruct(a.shape, F32) for a in ws3] * 4)(g, *ws3, *ms3, *vs3)


def _rowsum_small(parts, loss8, after):
    def body(*refs):
        out = refs[-1]
        c0 = 0
        for ref in refs[:-2]:
            n = ref.shape[1]
            out[:, c0:c0 + n] = jnp.sum(ref[...], axis=0, keepdims=True)
            c0 += n

    k = len(parts) + 1
    return pl.pallas_call(body, name="rowsum_small", in_specs=[pl.BlockSpec(memory_space=pltpu.VMEM)] * k + [ANY],
                          out_shape=jax.ShapeDtypeStruct((1, N_SMALL), F32))(*parts, loss8, after)


def _reduce_adamw_small(sbuf, ws, ms, vs):
    nv = len(ws)

    def body(*refs):
        s_ref, ins, outs = refs[0], refs[1:1 + 3 * nv], refs[1 + 3 * nv:]
        tot = s_ref[0]
        for s in range(1, N_DEV):
            tot = tot + s_ref[s]
        c0 = 0
        for i in range(nv):
            rows, cols = ins[i].shape
            n = rows * cols
            for r in range(rows):
                outs[i][r:r + 1, :] = tot[:, c0 + cols * r:c0 + cols * (r + 1)]
            g = outs[i][...]
            outs[nv + i][...], outs[2 * nv + i][...], outs[3 * nv + i][...] = _adamw(
                ins[i][...], g, ins[nv + i][...], ins[2 * nv + i][...])
            c0 += n
        outs[-1][...] = tot[:, c0:]

    return pl.pallas_call(
        body, name="reduce_adamw_small",
        out_shape=[jax.ShapeDtypeStruct(a.shape, F32) for a in ws] * 4 + [jax.ShapeDtypeStruct((1, 128), F32)],
    )(sbuf, *ws, *ms, *vs)


_TRANSPOSED = ("w_in", "w_gate", "w_up")
_ROW_STACKED = MATS
_ADAM_TILE = {"w_in": 208, "w_out": 256, "w_gate": 176, "w_up": 176, "w_down": 176, "lora": 256}
_SUM_TILE = {"w_in": 208, "w_out": 128, "w_gate": 176, "w_up": 176, "w_down": 176, "lora": 128}


def _full(n, stacked):
    p, r, c = stacked.shape
    if n in _ROW_STACKED:
        return stacked.reshape(p * r, c)
    return jnp.transpose(stacked, (1, 0, 2)).reshape(r, p * c)


def _by_chip(n, full):
    if n in _ROW_STACKED:
        return full.reshape(N_CHIP, full.shape[0] // N_CHIP, full.shape[1])
    r, c = full.shape
    return jnp.transpose(full.reshape(r, N_CHIP, c // N_CHIP), (1, 0, 2))


def _with_own(land_shape, dtype, own, slot):
    return lax.dynamic_update_slice(lax.empty(land_shape, dtype), own[None], (slot,) + (0,) * own.ndim)


def _cast_into_slot(a, chip, tr, name, after=None):
    rows, cols = a.shape

    def body(chip_ref, a_ref, *rest):
        rest[-1][0] = a_ref[...].astype(BF16)

    extra = [] if after is None else [after]
    return pl.pallas_call(
        body, name=name,
        grid_spec=pltpu.PrefetchScalarGridSpec(
            num_scalar_prefetch=1, grid=(rows // tr,),
            in_specs=[pl.BlockSpec((tr, cols), lambda i, chip_ref: (i, 0))] + [ANY] * len(extra),
            out_specs=pl.BlockSpec((1, tr, cols), lambda i, chip_ref: (chip_ref[0], i, 0))),
        out_shape=jax.ShapeDtypeStruct((N_CHIP, rows, cols), BF16),
        compiler_params=_params(("parallel",)),
    )(chip, a, *extra)


def kernel(x, mix_norm_g, w_in, mu_shift, decay_w0, decay_w2, iclr_a0, iclr_a2, gate_g2, k_k, k_a, r_k, ln_x_w, ln_x_b, attn_out_g, w_out, ffn_norm_g, w_gate, w_up, w_down, final_norm_g, loss_target, m_mix_norm_g, m_w_in, m_mu_shift, m_decay_w0, m_decay_w2, m_iclr_a0, m_iclr_a2, m_gate_g2, m_k_k, m_k_a, m_r_k, m_ln_x_w, m_ln_x_b, m_attn_out_g, m_w_out, m_ffn_norm_g, m_w_gate, m_w_up, m_w_down, m_final_norm_g, v_mix_norm_g, v_w_in, v_mu_shift, v_decay_w0, v_decay_w2, v_iclr_a0, v_iclr_a2, v_gate_g2, v_k_k, v_k_a, v_r_k, v_ln_x_w, v_ln_x_b, v_attn_out_g, v_w_out, v_ffn_norm_g, v_w_gate, v_w_up, v_w_down, v_final_norm_g):
    names = ("mix_norm_g", "w_in", "mu_shift", "decay_w0", "decay_w2", "iclr_a0", "iclr_a2", "gate_g2", "k_k", "k_a",
             "r_k", "ln_x_w", "ln_x_b", "attn_out_g", "w_out", "ffn_norm_g", "w_gate", "w_up", "w_down", "final_norm_g")
    w = dict(zip(names, (mix_norm_g, w_in, mu_shift, decay_w0, decay_w2, iclr_a0, iclr_a2, gate_g2, k_k, k_a, r_k,
                         ln_x_w, ln_x_b, attn_out_g, w_out, ffn_norm_g, w_gate, w_up, w_down, final_norm_g)))
    m = dict(zip(names, (m_mix_norm_g, m_w_in, m_mu_shift, m_decay_w0, m_decay_w2, m_iclr_a0, m_iclr_a2, m_gate_g2,
                         m_k_k, m_k_a, m_r_k, m_ln_x_w, m_ln_x_b, m_attn_out_g, m_w_out, m_ffn_norm_g, m_w_gate,
                         m_w_up, m_w_down, m_final_norm_g)))
    v = dict(zip(names, (v_mix_norm_g, v_w_in, v_mu_shift, v_decay_w0, v_decay_w2, v_iclr_a0, v_iclr_a2, v_gate_g2,
                         v_k_k, v_k_a, v_r_k, v_ln_x_w, v_ln_x_b, v_attn_out_g, v_w_out, v_ffn_norm_g, v_w_gate,
                         v_w_up, v_w_down, v_final_norm_g)))
    first = ("w_in", "lora")
    rest = ("w_out", "w_gate", "w_up", "w_down")
    xi, yi, ci = lax.axis_index("x"), lax.axis_index("y"), lax.axis_index("c")
    my_chip, my_dev = 2 * xi + yi, 4 * xi + 2 * yi + ci
    gather, scatter = ("gather",) * 4, ("scatter",) * 4

    def stored(d):
        out = {n: jnp.transpose(d[n][0]) if n in _TRANSPOSED else d[n][0] for n in MATS}
        out["lora"] = jnp.concatenate([d[n][0] for n in LORAS], axis=0)
        return out

    ws, ms, vs = stored(w), stored(m), stored(v)
    chip = jnp.reshape(my_chip, (1,)).astype(jnp.int32)
    early = _swap_start(None, [_cast_into_slot(ws["w_in"], chip, _ADAM_TILE["w_in"], "cast_w_in"),
                               _with_own((N_CHIP,) + ws["lora"].shape, F32, ws["lora"], my_chip)], gather[:2],
                        "gather_first_start")
    lands = [_cast_into_slot(ws[n], chip, _ADAM_TILE[n], "cast_" + n, after=early[4]) for n in rest]
    gather_rest = ("gather", "gather", "whole", "whole")
    ssem, rsem, srcs_thru, lands_thru, tok = _swap_start(None, lands, gather_rest, "gather_rest_start")
    win_all, lora_all = _wait_and_swap(early[0], early[1], early[3], tok, gather[:2], "gather_first_wait_halves")
    win = _full("w_in", win_all)
    w2p, a2p, g2m = _unstack_lora(lora_all)

    vecs = {n: w[n].reshape(1, sz) for n, sz in VECS}

    def get_rest(after):
        got_rest = _swap_wait(ssem, rsem, srcs_thru, lands_thru, after, gather_rest, "gather_rest_wait")
        swapped = _swap_gathered(got_rest[:2], "gather_rest_halves")
        return [_full(n, z) for n, z in zip(rest, [*swapped, *got_rest[2:]])]

    flight = []

    def my_half(g):
        h = g.shape[1] // 2
        return lax.dynamic_slice(g, (my_chip, ci * h, 0), (1, h, g.shape[2]))[0]

    def send_rest(gw):
        gs = [_by_chip(n, gw[n]) for n in rest]
        into = [_with_own((N_DEV,) + my_half(g).shape, BF16, my_half(g), my_dev) for g in gs]
        flight.extend(_swap_start(gs, into, scatter, "exchange_rest_start"))
        return flight[4]

    loss8, dx, gw, gv = _local_step(x[0], loss_target[0], win, vecs, w2p, a2p, g2m, get_rest, send_rest)

    core = jnp.reshape(ci, (1,)).astype(jnp.int32)
    gs = [_by_chip("w_in", gw["w_in"]),
          jnp.concatenate([_by_chip(n, gw[n]) for n in LORAS], axis=1).astype(BF16)]
    theirs = _sibling_halves(gs, "presum_halves")
    sums = [_add_halves(g, o, core, _SUM_TILE[n], "chipsum_" + n) for n, g, o in zip(first, gs, theirs)]
    own = [lax.dynamic_index_in_dim(s, my_chip, 0, keepdims=False) for s in sums]
    last = _swap_start(sums, [_with_own(s.shape, BF16, o, my_chip) for s, o in zip(sums, own)], ("chipsum",) * 2,
                       "exchange_first_start")
    small = _rowsum_small([gv[n] for n, _ in VECS], loss8, after=last[4])
    vecs_out = _swap_start([small], [_with_own((N_DEV,) + small.shape, F32, small, my_dev)], ("all",),
                           "exchange_vectors_start")


    def update(group, rbufs, tag):
        sums = [_reduce8(rb, core, _SUM_TILE[n], "reduce_" + n) for n, rb in zip(group, rbufs)]
        gsum = _join_halves(sums, "join_halves_" + tag)
        out = {}
        for n, g in zip(group, gsum):
            if n == "lora":
                r = _adamw_lora(g, *[[d[k] for k in LORAS] for d in (w, m, v)])
                for i, name in enumerate(LORAS):
                    out[name] = r[i::3]
            else:
                r = _adamw_call(g, ws[n][None], ms[n][None], vs[n][None], _ADAM_TILE[n], "adamw_" + n)
                out[n] = [jnp.transpose(z[0])[None] for z in r] if n in _TRANSPOSED else r
        return out, r[1]

    res, done = update(rest, _swap_wait(flight[0], flight[1], flight[2], flight[3], vecs_out[4], scatter,
                                        "exchange_rest_wait"), "rest")
    got = _swap_wait(last[0], last[1], last[2], last[3], done, ("chipsum",) * 2, "exchange_first_wait")
    res_first, done = update(first, got, "first")
    res.update(res_first)
    sbuf = _swap_wait(vecs_out[0], vecs_out[1], vecs_out[2], vecs_out[3], done, ("all",), "exchange_vectors_wait")[0]
    rows = lambda d: [d[n].reshape(-1, d[n].shape[-1]) for n, _ in VECS]
    small_res = _reduce_adamw_small(sbuf, rows(w), rows(m), rows(v))

    outs = []
    for k in range(4):
        piece = {n: r[k] for n, r in res.items()}
        for i, (n, _) in enumerate(VECS):
            piece[n] = small_res[k * len(VECS) + i].reshape(w[n].shape)
        outs.extend(piece[n] for n in names)
    return (small_res[-1][0, 0], dx[None], *outs)
```

```python
import jax
import jax.numpy as jnp
from jax import lax
from jax.experimental import pallas as pl
from jax.experimental.pallas import tpu as pltpu

F32 = jnp.float32
BF16 = jnp.bfloat16

D_MODEL = 1024
HEAD_DIM = 64
RW = 512
N_PAIR = RW // 128
SHIFT_COLS = 1792
IN_COLS = 3328
D_FF = 2816
FF_CHUNK = 256
NORM_EPS = 1e-6
GN_EPS = 64e-5
CHUNK = 64
SUB = 16
WKV_PASSES = 1
ATTN_PASSES = 1
ATTN_BLOCK = 128
DILATIONS = (1, 4, 16)
NEG = -1e30
ADAM_LR, ADAM_B1, ADAM_B2, ADAM_EPS, ADAM_WD, ADAM_STEP = 0.001, 0.9, 0.999, 1e-08, 0.01, 10
VMEM_LIMIT = 56 * 1024 * 1024
MESH = pl.DeviceIdType.MESH


def _params(sem=None, **kw):
    return pltpu.CompilerParams(dimension_semantics=sem, vmem_limit_bytes=VMEM_LIMIT, **kw)


def _dot(a, b):
    return lax.dot_general(a, b, (((1,), (0,)), ((), ())), preferred_element_type=F32)


def _dot_nt(a, b):
    return lax.dot_general(a, b, (((1,), (1,)), ((), ())), preferred_element_type=F32)


def _dot_tn(a, b):
    return lax.dot_general(a, b, (((0,), (0,)), ((), ())), preferred_element_type=F32)


_FORMS = {"nn": ((1,), (0,)), "nt": ((1,), (1,)), "tn": ((0,), (0,))}


def _dg(a, b, form):
    if a.ndim == 3 or b.ndim == 3:
        nb = a.shape[0] if a.ndim == 3 else b.shape[0]
        return jnp.stack([_dg(a[i] if a.ndim == 3 else a, b[i] if b.ndim == 3 else b, form) for i in range(nb)], axis=0)
    return lax.dot_general(a, b, (_FORMS[form], ((), ())), preferred_element_type=F32)


def _split2(x):
    hi = x.astype(BF16)
    return hi, (x - hi.astype(F32)).astype(BF16)


def _split3(x):
    hi = x.astype(BF16)
    rest = x - hi.astype(F32)
    mid = rest.astype(BF16)
    return hi, mid, (rest - mid.astype(F32)).astype(BF16)


def _mm_raw(a, b, form, mode):
    if mode == 1:
        return _dg(a.astype(BF16), b.astype(BF16), form)
    if mode == 3:
        ah, al = _split2(a)
        bh, bl = _split2(b)
        return _dg(ah, bh, form) + (_dg(ah, bl, form) + _dg(al, bh, form))
    if mode == "L3":
        ab = a.astype(BF16)
        b1, b2, b3 = _split3(b)
        if form == "nn":
            n = b.shape[-1]
            wide = _dg(ab, jnp.concatenate([b1, b2, b3], axis=-1), form)
            return wide[..., :n] + (wide[..., n:2 * n] + wide[..., 2 * n:])
        return _dg(ab, b1, form) + (_dg(ab, b2, form) + _dg(ab, b3, form))
    assert mode == "R3", mode
    bb = b.astype(BF16)
    a1, a2, a3 = _split3(a)
    if form in ("nn", "nt"):
        m = a.shape[-2]
        tall = _dg(jnp.concatenate([a1, a2, a3], axis=-2), bb, form)
        return tall[..., :m, :] + (tall[..., m:2 * m, :] + tall[..., 2 * m:, :])
    return _dg(a1, bb, form) + (_dg(a2, bb, form) + _dg(a3, bb, form))


def _mm(a, b, form, mode):
    @jax.custom_vjp
    def f(a, b):
        return _mm_raw(a, b, form, mode)

    def fwd(a, b):
        return _mm_raw(a, b, form, mode), (a, b)

    def bwd(res, ct):
        a, b = res
        la = {1: 1, 3: 3, "L3": None, "R3": "R3"}[mode]
        lb = {1: 1, 3: 3, "L3": "L3", "R3": None}[mode]
        if form == "nn":
            da = None if la is None else _mm_raw(ct, b, "nt", la)
            db = None if lb is None else _mm_raw(a, ct, "tn", lb)
        elif form == "nt":
            da = None if la is None else _mm_raw(ct, b, "nn", la)
            db = None if lb is None else _mm_raw(ct, a, "tn", "R3" if lb == "L3" else lb)
        else:
            da = None if la is None else _mm_raw(b, ct, "nt", "L3" if la == "R3" else la)
            db = None if lb is None else _mm_raw(a, ct, "nn", lb)
        return (jnp.zeros_like(a) if da is None else da, jnp.zeros_like(b) if db is None else db)

    f.defvjp(fwd, bwd)
    return f(a, b)


def _seg_ones(n):
    r = lax.broadcasted_iota(jnp.int32, (n, n), 0) // HEAD_DIM
    c = lax.broadcasted_iota(jnp.int32, (n, n), 1) // HEAD_DIM
    return (r == c).astype(F32)


def _segsum(x, seg):
    return _mm(x, seg, "nn", "R3")


def _rms_fwd(x, g):
    rstd = lax.rsqrt(jnp.mean(x * x, axis=-1, keepdims=True) + NORM_EPS)
    return x * rstd * g


def _rms_bwd(dy, x, g):
    rstd = lax.rsqrt(jnp.mean(x * x, axis=-1, keepdims=True) + NORM_EPS)
    xn = x * rstd
    dxn = dy * g
    dx = rstd * (dxn - xn * jnp.mean(dxn * xn, axis=-1, keepdims=True))
    return dx, dy * xn


def _sigmoid(x):
    return 1.0 / (1.0 + jnp.exp(-x))


def _softplus(x):
    return jnp.maximum(x, 0.0) + jnp.log(1.0 + jnp.exp(-jnp.abs(x)))


def _acc(ref, val, first):
    @pl.when(first)
    def _():
        ref[...] = val

    @pl.when(jnp.logical_not(first))
    def _():
        ref[...] += val


def _colsum8(v):
    rows, n = v.shape
    return jnp.sum(v.reshape(rows // 8, 8, n), axis=0)


def _prep_fn(p, pprev, mu, w0, w2p, a0, a2p, g2, k_k, k_a):
    seg = _seg_ones(RW)
    ps = p + (pprev - p) * mu
    r = ps[:, 0:RW]
    k = ps[:, RW:2 * RW]
    v = ps[:, 2 * RW:3 * RW]
    xwa = ps[:, 3 * RW:3 * RW + 128]
    xg = ps[:, 3 * RW + 128:3 * RW + 256]
    wraw = -_softplus(-(w0 + _mm(jnp.tanh(xwa), w2p, "nn", 3))) - 0.5
    lw = -jnp.exp(wraw)
    a = _sigmoid(a0 + _mm(xwa, a2p, "nn", 3))
    g = _mm(_sigmoid(xg), g2, "nn", 3)
    kk = k * k_k
    kk = kk / jnp.maximum(jnp.sqrt(_segsum(kk * kk, seg)), 1e-12)
    k2 = k * (1.0 + (a - 1.0) * k_a)
    return r, lw, k2, v, kk, a, g


def _transposed(z):
    return jnp.stack([z[i].T for i in range(z.shape[0])], axis=0) if z.ndim == 3 else z.T


def _solve_unit_lower(lmat, rhs):
    c = lmat.shape[-1]
    row = lax.broadcasted_iota(jnp.int32, (c, c), 0)
    col = lax.broadcasted_iota(jnp.int32, (c, c), 1)
    eye = (row == col).astype(F32)
    ld = jnp.where(row // SUB == col // SUB, lmat, 0.0)
    lo = lmat - ld
    x = eye + ld
    m = ld
    mm = lambda p, q: _mm(p, q, "nn", WKV_PASSES)
    cat = jnp.concatenate
    m = mm(m, m)
    for _ in range(2):
        mx = mm(m, cat([m, x], axis=-1))
        m, x = mx[..., :c], x + mx[..., c:]
    x = x + mm(m, x)
    gw = mm(x, cat([lo, rhs], axis=-1))
    g, w = gw[..., :c], gw[..., c:]
    gg = mm(g, cat([g, w], axis=-1))
    w = w + gg[..., c:]
    return w + mm(gg[..., :c], w)


def _wkv_chunk_fn(s0, r, lw, k, v, kk, a):
    c = r.shape[-2]
    n = 2 * c
    row = lax.broadcasted_iota(jnp.int32, (n, n), 0)
    col = lax.broadcasted_iota(jnp.int32, (n, n), 1)
    same = (row // c) == (col // c)
    incl = jnp.logical_and(row >= col, same)
    strict = jnp.logical_and(row > col, same)
    sel = (lax.broadcasted_iota(jnp.int32, (n, 128), 0) // c) == (lax.broadcasted_iota(jnp.int32, (n, 128), 1) // HEAD_DIM)
    two = lambda z: jnp.concatenate([z, z], axis=-2)
    lw2 = two(lw)
    mm = lambda p_, q_, form: _mm(p_, q_, form, WKV_PASSES)
    cl = _mm(incl.astype(F32), lw2, "nn", "L3")
    p = jnp.exp(cl)
    pinv = jnp.exp(-cl)
    pprev = jnp.exp(cl - lw2)
    kk2 = two(kk)
    at = jnp.where(sel, -kk2 * pprev, 0.0)
    bt = jnp.where(sel, kk2 * two(a) * pinv, 0.0)
    kt = jnp.where(sel, two(k) * pinv, 0.0)
    rt = jnp.where(sel, two(r) * p, 0.0)
    vt = jnp.where(sel, two(v), 0.0)
    cat = jnp.concatenate
    bk = cat([bt, kt], axis=-2)
    arbk = mm(cat([at, rt], axis=-2), bk, "nt")
    ab, ak = jnp.where(strict, arbk[..., :n, :n], 0.0), jnp.where(strict, arbk[..., :n, n:], 0.0)
    rb, rk = jnp.where(incl, arbk[..., n:, :n], 0.0), jnp.where(incl, arbk[..., n:, n:], 0.0)
    s0t = _transposed(s0)
    u = _solve_unit_lower(ab, mm(cat([at, ak], axis=-1), cat([s0t, vt], axis=-2), "nn"))
    y2 = mm(cat([rt, rb, rk], axis=-1), cat([s0t, u, vt], axis=-2), "nn")
    plast = jnp.exp(jnp.sum(lw, axis=-2, keepdims=True))
    s1 = (s0 + mm(cat([u, vt], axis=-2), bk, "tn")) * plast
    r2 = lax.broadcasted_iota(jnp.int32, (128, 128), 0) // HEAD_DIM
    c2 = lax.broadcasted_iota(jnp.int32, (128, 128), 1) // HEAD_DIM
    return y2[..., :c, :] + y2[..., c:, :], jnp.where(r2 == c2, s1, 0.0)


def _post_fn(y, r, k2, v, g, lnw, lnb, rk):
    seg = _seg_ones(RW)
    mean = _segsum(y, seg) * (1.0 / HEAD_DIM)
    yc = y - mean
    var = _segsum(yc * yc, seg) * (1.0 / HEAD_DIM)
    yn = yc * lax.rsqrt(var + GN_EPS)
    out = yn * lnw + lnb + _segsum(r * k2 * rk, seg) * v
    return out * g


def _attn_block_fn(q, kc, vc, kp=None, vp=None):
    n = ATTN_BLOCK
    qi = lax.broadcasted_iota(jnp.int32, (n, n), 0)
    kj = lax.broadcasted_iota(jnp.int32, (n, n), 1)
    lane = lax.broadcasted_iota(jnp.int32, (1, 128), 1)
    scale = HEAD_DIM ** -0.5
    valid = kj <= qi
    keys, vals = kc, vc
    if kp is not None:
        valid = jnp.concatenate([valid, kj >= qi], axis=-1)
        keys, vals = jnp.concatenate([kc, kp], axis=-2), jnp.concatenate([vc, vp], axis=-2)
    m0 = (lane // HEAD_DIM) == 0
    q2 = jnp.concatenate([jnp.where(m0, q, 0.0), jnp.where(m0, 0.0, q)], axis=-2)
    valid2 = jnp.concatenate([valid, valid], axis=-2)
    s = jnp.where(valid2, _mm(q2, keys, "nt", ATTN_PASSES) * scale, NEG)
    m = jnp.max(s, axis=-1, keepdims=True)
    p = jnp.exp(s - m)
    den = jnp.sum(p, axis=-1, keepdims=True)
    o2 = _mm(p, vals, "nn", ATTN_PASSES) / den
    l2 = m + jnp.log(den)
    return jnp.where(m0, o2[..., :n, :], o2[..., n:, :]), jnp.where(m0, l2[..., :n, :], l2[..., n:, :])


def _attn_block_bwd(q, kc, vc, kp, vp, o, lse, do, dl):
    n = ATTN_BLOCK
    cat = jnp.concatenate
    qi = lax.broadcasted_iota(jnp.int32, (n, n), 0)
    kj = lax.broadcasted_iota(jnp.int32, (n, n), 1)
    m0 = (lax.broadcasted_iota(jnp.int32, (1, 128), 1) // HEAD_DIM) == 0
    scale = HEAD_DIM ** -0.5
    valid = kj <= qi
    keys, vals = kc, vc
    if kp is not None:
        valid = cat([valid, kj >= qi], axis=-1)
        keys, vals = cat([kc, kp], axis=-2), cat([vc, vp], axis=-2)
    stack = lambda z: cat([jnp.where(m0, z, 0.0), jnp.where(m0, 0.0, z)], axis=-2)
    q2, do2 = stack(q), stack(do)
    lse2 = cat([jnp.max(jnp.where(m0, lse, NEG), axis=-1, keepdims=True),
                jnp.max(jnp.where(m0, NEG, lse), axis=-1, keepdims=True)], axis=-2)
    delta = jnp.sum(do2 * cat([o, o], axis=-2), axis=-1, keepdims=True)
    dlse = jnp.sum(stack(dl), axis=-1, keepdims=True)
    mm = lambda a, b, form: _mm_raw(a, b, form, ATTN_PASSES)
    s = jnp.where(cat([valid, valid], axis=-2), mm(q2, keys, "nt") * scale, NEG)
    p = jnp.exp(s - lse2)
    ds = p * (mm(do2, vals, "nt") - delta + dlse)
    dq2 = mm(ds, keys, "nn") * scale
    dq = jnp.where(m0, dq2[..., :n, :], dq2[..., n:, :])
    dkeys = mm(ds, q2, "tn") * scale
    dvals = mm(p, do2, "tn")
    if kp is None:
        return dq, dkeys, dvals
    return dq, dkeys[..., :n, :], dvals[..., :n, :], dkeys[..., n:, :], dvals[..., n:, :]


def _combine_fn(o1, o2, o3, l1, l2, l3, og):
    seg = _seg_ones(o1.shape[-1])
    m = jnp.maximum(jnp.maximum(l1, l2), l3)
    e1, e2, e3 = jnp.exp(l1 - m), jnp.exp(l2 - m), jnp.exp(l3 - m)
    o = (e1 * o1 + e2 * o2 + e3 * o3) / (e1 + e2 + e3)
    o = o * lax.rsqrt(_segsum(o * o, seg) * (1.0 / HEAD_DIM) + NORM_EPS)
    return o * og


def _shifted(p, last8, first):
    prow = jnp.where(first, 0.0, last8[7:8, :])
    rolled = pltpu.roll(p, 1, axis=0)
    rid = lax.broadcasted_iota(jnp.int32, p.shape, 0)
    return jnp.where(rid == 0, prow, rolled)


_PREP_TM = 256


def _prep_specs(tm):
    vec = lambda n: pl.BlockSpec((1, n), lambda i: (0, 0))
    mat = lambda r, n: pl.BlockSpec((r, n), lambda i: (0, 0))
    return [vec(SHIFT_COLS), vec(RW), mat(128, RW), vec(RW), mat(128, RW), mat(128, RW), vec(RW), vec(RW)]


def _in_proj_prep(x, g1, win, pw):
    t = x.shape[0]
    tm = _PREP_TM

    def body(x_ref, g_ref, w_ref, mu, w0, w2p, a0, a2p, g2, k_k, k_a, h_ref, pa_ref, qkv_ref, *rest):
        outs, carry = rest[:7], rest[7]

        @pl.when(pl.program_id(0) == 0)
        def _():
            carry[...] = jnp.zeros_like(carry)

        h = _rms_fwd(x_ref[...], g_ref[...]).astype(BF16)
        h_ref[...] = h
        proj = _dot_nt(h, w_ref[...])
        p = proj[:, :SHIFT_COLS]
        pa_ref[...] = p
        for j in range(3):
            for pr in range(N_PAIR):
                c0 = SHIFT_COLS + j * RW + pr * 128
                qkv_ref[j, pr] = proj[:, c0:c0 + 128]
        pprev = _shifted(p, carry[...], pl.program_id(0) == 0)
        carry[...] = p[tm - 8:, :]
        res = _prep_fn(p, pprev, mu[...], w0[...], w2p[...], a0[...], a2p[...], g2[...], k_k[...], k_a[...])
        for o_ref, val in zip(outs, res):
            o_ref[...] = val

    row = pl.BlockSpec((tm, RW), lambda i: (i, 0))
    return pl.pallas_call(
        body, name="in_proj_prep", grid=(t // tm,),
        in_specs=[pl.BlockSpec((tm, D_MODEL), lambda i: (i, 0)), pl.BlockSpec((1, D_MODEL), lambda i: (0, 0)),
                  pl.BlockSpec((IN_COLS, D_MODEL), lambda i: (0, 0))] + _prep_specs(tm),
        out_specs=[pl.BlockSpec((tm, D_MODEL), lambda i: (i, 0)), pl.BlockSpec((tm, SHIFT_COLS), lambda i: (i, 0)),
                   pl.BlockSpec((3, N_PAIR, tm, 128), lambda i: (0, 0, i, 0))] + [row] * 7,
        out_shape=[jax.ShapeDtypeStruct((t, D_MODEL), BF16), jax.ShapeDtypeStruct((t, SHIFT_COLS), F32),
                   jax.ShapeDtypeStruct((3, N_PAIR, t, 128), F32)] + [jax.ShapeDtypeStruct((t, RW), F32)] * 7,
        scratch_shapes=[pltpu.VMEM((8, SHIFT_COLS), F32)],
        compiler_params=_params(("arbitrary",)),
    )(x, g1, win, *pw)


def _pairs(ref):
    return jnp.stack([ref[:, 128 * p:128 * (p + 1)] for p in range(N_PAIR)], axis=0)


def _wkv_fwd(r, lw, k2, v, kk, a):
    t = r.shape[0]
    nc = t // CHUNK

    def body(r_ref, lw_ref, k_ref, v_ref, kk_ref, a_ref, y_ref, s_ref, st):
        @pl.when(pl.program_id(0) == 0)
        def _():
            st[...] = jnp.zeros_like(st)

        s0 = st[...]
        s_ref[0] = s0
        y, s1 = _wkv_chunk_fn(s0, *[_pairs(ref) for ref in (r_ref, lw_ref, k_ref, v_ref, kk_ref, a_ref)])
        for p in range(N_PAIR):
            y_ref[:, 128 * p:128 * (p + 1)] = y[p]
        st[...] = s1

    blk = pl.BlockSpec((CHUNK, RW), lambda c: (c, 0))
    return pl.pallas_call(
        body, name="wkv_fwd", grid=(nc,),
        in_specs=[blk] * 6,
        out_specs=[blk, pl.BlockSpec((1, N_PAIR, 128, 128), lambda c: (c, 0, 0, 0))],
        out_shape=[jax.ShapeDtypeStruct((t, RW), F32), jax.ShapeDtypeStruct((nc, N_PAIR, 128, 128), F32)],
        scratch_shapes=[pltpu.VMEM((N_PAIR, 128, 128), F32)],
        compiler_params=_params(("arbitrary",)),
    )(r, lw, k2, v, kk, a)


_POST_TM = 512


ATTN_GROUP = 2


def _dilated_rows(d, r, n):
    if d == 1:
        return pl.ds(pl.multiple_of(n * ATTN_BLOCK, ATTN_BLOCK), ATTN_BLOCK)
    return pl.ds(r + n * (ATTN_BLOCK * d), ATTN_BLOCK, stride=d)


def _for_each_sequence(t, unit):
    for di, d in enumerate(DILATIONS):

        @pl.when(pl.program_id(1) == di)
        def _(di=di, d=d):
            nb = t // (ATTN_BLOCK * d)
            if d == 1:
                unit(di, [(d, 0, 0)], False)
                unit(di, [(d, 0, 1)], True)
                lax.fori_loop(1, nb // 2, lambda k, c: (unit(di, [(d, 0, 2 * k), (d, 0, 2 * k + 1)], True), c)[1], 0)
            else:

                def residues(r, carry):
                    unit(di, [(d, r, 0), (d, r + d // 2, 0)], False)
                    if nb > 1:
                        lax.fori_loop(1, nb, lambda n, c: (unit(di, [(d, r, n), (d, r + d // 2, n)], True), c)[1], 0)
                    return carry

                lax.fori_loop(0, d // 2, residues, 0)


def _take(ref, lead, rows_list):
    return jnp.stack([ref.at[(*lead, g)][rows, :] for rows in rows_list for g in range(ref.shape[len(lead)])], axis=0)


def _put(ref, lead, rows_list, val, add=False):
    k = 0
    for rows in rows_list:
        for g in range(ref.shape[len(lead)]):
            if add:
                ref.at[(*lead, g)][rows, :] += val[k]
            else:
                ref.at[(*lead, g)][rows, :] = val[k]
            k += 1


def _attn_fwd(qkv):
    t = qkv.shape[2]

    def body(q_ref, k_ref, v_ref, o_ref, l_ref):
        def unit(di, places, has_prev):
            cur = [_dilated_rows(d, r, n) for d, r, n in places]
            args = [_take(ref, (0,), cur) for ref in (q_ref, k_ref, v_ref)]
            if has_prev:
                prv = [_dilated_rows(d, r, n - 1) for d, r, n in places]
                args += [_take(ref, (0,), prv) for ref in (k_ref, v_ref)]
            o, lse = _attn_block_fn(*args)
            _put(o_ref, (0,), cur, o)
            _put(l_ref, (0,), cur, lse)

        _for_each_sequence(t, unit)

    spec = lambda j: pl.BlockSpec((1, ATTN_GROUP, t, 128), lambda i, b: (j, i, 0, 0))
    out = pl.BlockSpec((1, ATTN_GROUP, t, 128), lambda i, b: (b, i, 0, 0))
    return pl.pallas_call(
        body, name="attn_fwd", grid=(N_PAIR // ATTN_GROUP, len(DILATIONS)),
        in_specs=[spec(0), spec(1), spec(2)], out_specs=[out, out],
        out_shape=[jax.ShapeDtypeStruct((3, N_PAIR, t, 128), F32)] * 2,
        compiler_params=_params(("parallel", "arbitrary")),
    )(qkv, qkv, qkv)


_COMB_TM = 512


def _mixers_out(y, r, k2, v, g, lnw, lnb, rk, o, l, og):
    t = y.shape[0]
    tm = _COMB_TM

    def body(y_ref, r_ref, k_ref, v_ref, g_ref, lnw_ref, lnb_ref, rk_ref, o_ref, l_ref, og_ref, out_ref):
        out_ref[:, :RW] = _post_fn(y_ref[...], r_ref[...], k_ref[...], v_ref[...], g_ref[...],
                                   lnw_ref[...], lnb_ref[...], rk_ref[...]).astype(BF16)
        for p in range(N_PAIR):
            cols = slice(128 * p, 128 * (p + 1))
            out_ref[:, RW + 128 * p:RW + 128 * (p + 1)] = _combine_fn(
                o_ref[0, p], o_ref[1, p], o_ref[2, p], l_ref[0, p], l_ref[1, p], l_ref[2, p], og_ref[:, cols]).astype(BF16)

    row = pl.BlockSpec((tm, RW), lambda i: (i, 0))
    vec = pl.BlockSpec((1, RW), lambda i: (0, 0))
    blk = pl.BlockSpec((3, N_PAIR, tm, 128), lambda i: (0, 0, i, 0))
    return pl.pallas_call(
        body, name="mixers_out", grid=(t // tm,),
        in_specs=[row] * 5 + [vec] * 3 + [blk, blk, vec], out_specs=pl.BlockSpec((tm, D_MODEL), lambda i: (i, 0)),
        out_shape=jax.ShapeDtypeStruct((t, D_MODEL), BF16),
        compiler_params=_params(("parallel",)),
    )(y, r, k2, v, g, lnw, lnb, rk, o, l, og)


def _ffn_all(x, ycat, wg, wu, wd, wout, g2, gf, tgt):
    t = x.shape[0]
    tm = 256

    def body(x_ref, y_ref, wg_ref, wu_ref, wd_ref, wo_ref, g2_ref, gf_ref, t_ref,
             h_ref, act_ref, dx2b_ref, dgt_ref, dup_ref, dx1b_ref, dx1_ref, dya_ref, dyb_ref, loss_ref, dgf_ref, dg2_ref,
             gt_s, up_s):
        first = pl.program_id(0) == 0
        x1 = x_ref[...] + _dot(y_ref[...], wo_ref[...])
        h = _rms_fwd(x1, g2_ref[...]).astype(BF16)
        h_ref[...] = h
        for c0 in range(0, D_FF, FF_CHUNK):
            cols = slice(c0, c0 + FF_CHUNK)
            gt = _dot_nt(h, wg_ref[cols, :])
            up = _dot_nt(h, wu_ref[cols, :])
            gt_s[:, cols] = gt.astype(BF16)
            up_s[:, cols] = up.astype(BF16)
            act_ref[:, cols] = (gt * _sigmoid(gt) * up).astype(BF16)
        x2 = x1 + _dot(act_ref[...], wd_ref[...])
        gf_ = gf_ref[...]
        diff = _rms_fwd(x2, gf_) - t_ref[...]
        lrow = 0.5 * jnp.sum(_colsum8(diff * diff), axis=1, keepdims=True) * (1.0 / D_MODEL)
        _acc(loss_ref, jnp.broadcast_to(lrow, (8, 128)), first)
        dx2, dgr = _rms_bwd(diff * (1.0 / D_MODEL), x2, gf_)
        _acc(dgf_ref, _colsum8(dgr), first)
        dx2b = dx2.astype(BF16)
        dx2b_ref[...] = dx2b
        for c0 in range(0, D_FF, FF_CHUNK):
            cols = slice(c0, c0 + FF_CHUNK)
            dact = _dot_nt(dx2b, wd_ref[cols, :])
            gt = gt_s[:, cols].astype(F32)
            sg = _sigmoid(gt)
            dgt_ref[:, cols] = (dact * up_s[:, cols].astype(F32) * sg * (1.0 + gt * (1.0 - sg))).astype(BF16)
            dup_ref[:, cols] = (dact * gt * sg).astype(BF16)
        dh = _dot(dgt_ref[...], wg_ref[...]) + _dot(dup_ref[...], wu_ref[...])
        dxn, dgr2 = _rms_bwd(dh, x1, g2_ref[...])
        _acc(dg2_ref, _colsum8(dgr2), first)
        dx1 = dx2 + dxn
        dx1_ref[...] = dx1
        dx1b = dx1.astype(BF16)
        dx1b_ref[...] = dx1b
        dy = _dot_nt(dx1b, wo_ref[...])
        dya_ref[...] = dy[:, :RW]
        dyb_ref[...] = dy[:, RW:]

    row = pl.BlockSpec((tm, D_MODEL), lambda i: (i, 0))
    wide = pl.BlockSpec((tm, D_FF), lambda i: (i, 0))
    half = pl.BlockSpec((tm, RW), lambda i: (i, 0))
    wsp = pl.BlockSpec((D_FF, D_MODEL), lambda i: (0, 0))
    vec = pl.BlockSpec((1, D_MODEL), lambda i: (0, 0))
    part = pl.BlockSpec((8, D_MODEL), lambda i: (0, 0))
    bf = lambda n: jax.ShapeDtypeStruct((t, n), BF16)
    return pl.pallas_call(
        body, name="ffn_all", grid=(t // tm,),
        in_specs=[row, row, wsp, wsp, wsp, pl.BlockSpec((D_MODEL, D_MODEL), lambda i: (0, 0)), vec, vec, row],
        out_specs=[row, wide, row, wide, wide, row, row, half, half, pl.BlockSpec((8, 128), lambda i: (0, 0)), part, part],
        out_shape=[bf(D_MODEL), bf(D_FF), bf(D_MODEL), bf(D_FF), bf(D_FF), bf(D_MODEL),
                   jax.ShapeDtypeStruct((t, D_MODEL), F32), jax.ShapeDtypeStruct((t, RW), F32),
                   jax.ShapeDtypeStruct((t, RW), F32), jax.ShapeDtypeStruct((8, 128), F32),
                   jax.ShapeDtypeStruct((8, D_MODEL), F32), jax.ShapeDtypeStruct((8, D_MODEL), F32)],
        scratch_shapes=[pltpu.VMEM((tm, D_FF), BF16), pltpu.VMEM((tm, D_FF), BF16)],
        compiler_params=_params(("arbitrary",)),
    )(x, ycat, wg, wu, wd, wout, g2, gf, tgt)


def _wgrad(a, b, tk, tn, name):
    t, kdim = a.shape
    ndim = b.shape[1]

    def body(a_ref, b_ref, o_ref):
        o_ref[...] = _dot_tn(a_ref[...], b_ref[...]).astype(BF16)

    return pl.pallas_call(
        body, name=name, grid=(kdim // tk, ndim // tn),
        in_specs=[pl.BlockSpec((t, tk), lambda i, j: (0, i)), pl.BlockSpec((t, tn), lambda i, j: (0, j))],
        out_specs=pl.BlockSpec((tk, tn), lambda i, j: (i, j)),
        out_shape=jax.ShapeDtypeStruct((kdim, ndim), BF16),
        compiler_params=_params(("parallel", "parallel")),
    )(a, b)


def _post_bwd(dya, y, r, k2, v, g, lnw, lnb, rk, after):
    t = y.shape[0]
    tm = _POST_TM

    def body(d_ref, y_ref, r_ref, k_ref, v_ref, g_ref, lnw_ref, lnb_ref, rk_ref, _,
             dy_ref, dr_ref, dk_ref, dv_ref, dg_ref, dlnw_ref, dlnb_ref, drk_ref):
        first = pl.program_id(0) == 0
        ones = jnp.ones((tm, 1), F32)
        prim = (y_ref[...], r_ref[...], k_ref[...], v_ref[...], g_ref[...],
                ones * lnw_ref[...], ones * lnb_ref[...], ones * rk_ref[...])
        _, vjp = jax.vjp(_post_fn, *prim)
        dy, dr, dk, dv, dg, dlnw, dlnb, drk = vjp(d_ref[...])
        dy_ref[...] = dy
        dr_ref[...] = dr
        dk_ref[...] = dk
        dv_ref[...] = dv
        dg_ref[...] = dg
        _acc(dlnw_ref, _colsum8(dlnw), first)
        _acc(dlnb_ref, _colsum8(dlnb), first)
        _acc(drk_ref, _colsum8(drk), first)

    row = pl.BlockSpec((tm, RW), lambda i: (i, 0))
    vec = pl.BlockSpec((1, RW), lambda i: (0, 0))
    part = pl.BlockSpec((8, RW), lambda i: (0, 0))
    return pl.pallas_call(
        body, name="rwkv_post_bwd", grid=(t // tm,),
        in_specs=[row] * 6 + [vec] * 3 + [ANY], out_specs=[row] * 5 + [part] * 3,
        out_shape=[jax.ShapeDtypeStruct((t, RW), F32)] * 5 + [jax.ShapeDtypeStruct((8, RW), F32)] * 3,
        compiler_params=_params(("arbitrary",)),
    )(dya, y, r, k2, v, g, lnw, lnb, rk, after)


def _wkv_bwd(dy, s0s, r, lw, k2, v, kk, a):
    t = r.shape[0]
    nc = t // CHUNK

    def body(dy_ref, s_ref, r_ref, lw_ref, k_ref, v_ref, kk_ref, a_ref,
             dr_ref, dlw_ref, dk_ref, dv_ref, dkk_ref, da_ref, ds):
        @pl.when(pl.program_id(0) == 0)
        def _():
            ds[...] = jnp.zeros_like(ds)

        _, vjp = jax.vjp(_wkv_chunk_fn, s_ref[0],
                         *[_pairs(ref) for ref in (r_ref, lw_ref, k_ref, v_ref, kk_ref, a_ref)])
        res = vjp((_pairs(dy_ref), ds[...]))
        ds[...] = res[0]
        for ref, val in zip((dr_ref, dlw_ref, dk_ref, dv_ref, dkk_ref, da_ref), res[1:]):
            for p in range(N_PAIR):
                ref[:, 128 * p:128 * (p + 1)] = val[p]

    blk = pl.BlockSpec((CHUNK, RW), lambda c: (nc - 1 - c, 0))
    return pl.pallas_call(
        body, name="wkv_bwd", grid=(nc,),
        in_specs=[blk, pl.BlockSpec((1, N_PAIR, 128, 128), lambda c: (nc - 1 - c, 0, 0, 0))] + [blk] * 6,
        out_specs=[blk] * 6,
        out_shape=[jax.ShapeDtypeStruct((t, RW), F32)] * 6,
        scratch_shapes=[pltpu.VMEM((N_PAIR, 128, 128), F32)],
        compiler_params=_params(("arbitrary",)),
    )(dy, s0s, r, lw, k2, v, kk, a)


def _prep_in_proj_bwd(proj, pw, douts, dq, dk, dv, win, x, g1, dx1):
    t = proj.shape[0]
    tm = _PREP_TM
    nt = t // tm

    def body(p_ref, l8_ref, mu, w0, w2p, a0, a2p, g2, k_k, k_a, dr, dr2, dlw, dk2, dk22, dv, dv2, dkk, da, dg,
             dq_ref, dkq_ref, dvq_ref, w_ref, x_ref, g1_ref, dx1_ref,
             dproj_ref, dx_ref, dg1_ref, dmu_ref, dw0_ref, dw2_ref, da0_ref, da2_ref, dg2_ref, dkk_ref, dka_ref, carry):
        i = pl.program_id(0)
        first = i == 0

        @pl.when(first)
        def _():
            carry[...] = jnp.zeros_like(carry)

        p = p_ref[...]
        pprev = _shifted(p, l8_ref[...], i == nt - 1)
        ones = jnp.ones((tm, 1), F32)
        prim = (p, pprev, ones * mu[...], ones * w0[...], w2p[...], ones * a0[...], a2p[...], g2[...],
                ones * k_k[...], ones * k_a[...])
        _, vjp = jax.vjp(_prep_fn, *prim)
        dp, dpp, dmu, dw0, dw2, da0, da2, dg2, dkk_, dka = vjp(
            (dr[...] + dr2[...], dlw[...], dk2[...] + dk22[...], dv[...] + dv2[...], dkk[...], da[...], dg[...]))
        up = pltpu.roll(dpp, tm - 1, axis=0)
        rid = lax.broadcasted_iota(jnp.int32, dpp.shape, 0)
        dpa = dp + jnp.where(rid == tm - 1, carry[0:1, :], up)
        carry[...] = jnp.broadcast_to(dpp[0:1, :], carry.shape)
        _acc(dmu_ref, _colsum8(dmu), first)
        _acc(dw0_ref, _colsum8(dw0), first)
        _acc(dw2_ref, dw2, first)
        _acc(da0_ref, _colsum8(da0), first)
        _acc(da2_ref, da2, first)
        _acc(dg2_ref, dg2, first)
        _acc(dkk_ref, _colsum8(dkk_), first)
        _acc(dka_ref, _colsum8(dka), first)
        parts = [dpa] + [ref[pr] for ref in (dq_ref, dkq_ref, dvq_ref) for pr in range(N_PAIR)]
        dproj = jnp.concatenate([z.astype(BF16) for z in parts], axis=1)
        dproj_ref[...] = dproj
        dxn, dgr = _rms_bwd(_dot(dproj, w_ref[...]), x_ref[...], g1_ref[...])
        dx_ref[...] = dx1_ref[...] + dxn
        _acc(dg1_ref, _colsum8(dgr), first)

    rev = lambda i: (nt - 1 - i, 0)
    row = pl.BlockSpec((tm, RW), rev)
    wide = pl.BlockSpec((tm, D_MODEL), rev)
    pair = pl.BlockSpec((N_PAIR, tm, 128), lambda i: (0, nt - 1 - i, 0))
    part = lambda n: pl.BlockSpec((8, n), lambda i: (0, 0))
    mat = pl.BlockSpec((128, RW), lambda i: (0, 0))
    return pl.pallas_call(
        body, name="prep_in_proj_bwd", grid=(nt,),
        in_specs=[pl.BlockSpec((tm, SHIFT_COLS), rev),
                  pl.BlockSpec((8, SHIFT_COLS), lambda i: (jnp.maximum((nt - 1 - i) * (tm // 8) - 1, 0), 0))]
                 + _prep_specs(tm) + [row] * 10
                 + [pair] * 3 + [pl.BlockSpec((IN_COLS, D_MODEL), lambda i: (0, 0)), wide,
                                 pl.BlockSpec((1, D_MODEL), lambda i: (0, 0)), wide],
        out_specs=[pl.BlockSpec((tm, IN_COLS), rev), wide, part(D_MODEL), part(SHIFT_COLS), part(RW), mat, part(RW), mat,
                   mat, part(RW), part(RW)],
        out_shape=[jax.ShapeDtypeStruct((t, IN_COLS), BF16), jax.ShapeDtypeStruct((t, D_MODEL), F32),
                   jax.ShapeDtypeStruct((8, D_MODEL), F32), jax.ShapeDtypeStruct((8, SHIFT_COLS), F32),
                   jax.ShapeDtypeStruct((8, RW), F32), jax.ShapeDtypeStruct((128, RW), F32),
                   jax.ShapeDtypeStruct((8, RW), F32), jax.ShapeDtypeStruct((128, RW), F32),
                   jax.ShapeDtypeStruct((128, RW), F32), jax.ShapeDtypeStruct((8, RW), F32),
                   jax.ShapeDtypeStruct((8, RW), F32)],
        scratch_shapes=[pltpu.VMEM((8, SHIFT_COLS), F32)],
        compiler_params=_params(("arbitrary",)),
    )(proj, proj, *pw, *douts, dq, dk, dv, win, x, g1, dx1)


def _combine_bwd(dyb, o, l, og):
    t = dyb.shape[0]
    tm = _COMB_TM

    def body(d_ref, o_ref, l_ref, og_ref, do_ref, dl_ref, dog_ref):
        ones = jnp.ones((tm, 1), F32)
        dog = []
        for p in range(N_PAIR):
            cols = slice(128 * p, 128 * (p + 1))
            _, vjp = jax.vjp(_combine_fn, o_ref[0, p], o_ref[1, p], o_ref[2, p], l_ref[0, p], l_ref[1, p], l_ref[2, p],
                             ones * og_ref[:, cols])
            res = vjp(d_ref[:, cols])
            for b in range(3):
                do_ref[b, p] = res[b]
                dl_ref[b, p] = res[3 + b]
            dog.append(_colsum8(res[6]))
        _acc(dog_ref, jnp.concatenate(dog, axis=1), pl.program_id(0) == 0)

    blk = pl.BlockSpec((3, N_PAIR, tm, 128), lambda i: (0, 0, i, 0))
    return pl.pallas_call(
        body, name="attn_combine_bwd", grid=(t // tm,),
        in_specs=[pl.BlockSpec((tm, RW), lambda i: (i, 0)), blk, blk, pl.BlockSpec((1, RW), lambda i: (0, 0))],
        out_specs=[blk, blk, pl.BlockSpec((8, RW), lambda i: (0, 0))],
        out_shape=[jax.ShapeDtypeStruct((3, N_PAIR, t, 128), F32)] * 2 + [jax.ShapeDtypeStruct((8, RW), F32)],
        compiler_params=_params(("arbitrary",)),
    )(dyb, o, l, og)


def _attn_bwd(do, dl, o, lse, qkv):
    t = qkv.shape[2]

    def body(do_ref, dl_ref, o_ref, l_ref, q_ref, k_ref, v_ref, dq_ref, dk_ref, dv_ref):
        @pl.when(pl.program_id(1) == 0)
        def _():
            for ref in (dq_ref, dk_ref, dv_ref):
                ref[...] = jnp.zeros_like(ref)

        def unit(di, places, has_prev):
            cur = [_dilated_rows(d, r, n) for d, r, n in places]
            q, kc, vc = [_take(ref, (0,), cur) for ref in (q_ref, k_ref, v_ref)]
            kp = vp = None
            if has_prev:
                prv = [_dilated_rows(d, r, n - 1) for d, r, n in places]
                kp, vp = [_take(ref, (0,), prv) for ref in (k_ref, v_ref)]
            res = _attn_block_bwd(q, kc, vc, kp, vp, *[_take(ref, (0,), cur) for ref in (o_ref, l_ref, do_ref, dl_ref)])
            _put(dq_ref, (), cur, res[0], add=True)
            _put(dk_ref, (), cur, res[1], add=True)
            _put(dv_ref, (), cur, res[2], add=True)
            if has_prev:
                _put(dk_ref, (), prv, res[3], add=True)
                _put(dv_ref, (), prv, res[4], add=True)

        _for_each_sequence(t, unit)

    spec = lambda j: pl.BlockSpec((1, ATTN_GROUP, t, 128), lambda i, b: (j, i, 0, 0))
    branch = pl.BlockSpec((1, ATTN_GROUP, t, 128), lambda i, b: (b, i, 0, 0))
    out = pl.BlockSpec((ATTN_GROUP, t, 128), lambda i, b: (i, 0, 0))
    return pl.pallas_call(
        body, name="attn_bwd", grid=(N_PAIR // ATTN_GROUP, len(DILATIONS)),
        in_specs=[branch] * 4 + [spec(0), spec(1), spec(2)], out_specs=[out] * 3,
        out_shape=[jax.ShapeDtypeStruct((N_PAIR, t, 128), F32)] * 3,
        compiler_params=_params(("parallel", "arbitrary")),
    )(do, dl, o, lse, qkv, qkv, qkv)


def _unstack_lora(lora_all):
    def body(l_ref, w_ref, a_ref, g_ref):
        z = jnp.zeros((64, 128), F32)
        for p in range(N_CHIP):
            cols = slice(128 * p, 128 * (p + 1))
            w_ref[0:64, cols] = l_ref[p, 0:64, :]
            w_ref[64:128, cols] = z
            a_ref[0:64, cols] = z
            a_ref[64:128, cols] = l_ref[p, 64:128, :]
            g_ref[:, cols] = l_ref[p, 128:256, :]

    return pl.pallas_call(body, name="unstack_lora", out_shape=[jax.ShapeDtypeStruct((128, RW), F32)] * 3)(lora_all)


def _local_step(x, tgt, win, vecs, w2p, a2p, g2m, get_rest, send_rest):
    pw = (vecs["mu_shift"], vecs["decay_w0"], w2p, vecs["iclr_a0"], a2p, g2m, vecs["k_k"], vecs["k_a"])
    h, proj, qkv, r, lw, k2, v, kk, a, g = _in_proj_prep(x, vecs["mix_norm_g"], win, pw)
    y, s0s = _wkv_fwd(r, lw, k2, v, kk, a)
    o_att, l_att = _attn_fwd(qkv)
    ycat = _mixers_out(y, r, k2, v, g, vecs["ln_x_w"], vecs["ln_x_b"], vecs["r_k"], o_att, l_att, vecs["attn_out_g"])
    wout, wg, wu, wd = get_rest(ycat)
    h2, act, dx2b, dgt, dup, dx1b, dx1, dya, dyb, loss8, dgf, dg2n = _ffn_all(
        x, ycat, wg, wu, wd, wout, vecs["ffn_norm_g"], vecs["final_norm_g"], tgt)
    gw = {
        "w_down": _wgrad(act, dx2b, 1408, 1024, "wgrad_down"),
        "w_gate": _wgrad(dgt, h2, 1408, 1024, "wgrad_gate"),
        "w_up": _wgrad(dup, h2, 1408, 1024, "wgrad_up"),
        "w_out": _wgrad(ycat, dx1b, 1024, 1024, "wgrad_out"),
    }

    dy, dr_p, dk2_p, dv_p, dg, dlnw, dlnb, drk = _post_bwd(dya, y, r, k2, v, g, vecs["ln_x_w"], vecs["ln_x_b"], vecs["r_k"],
                                                           after=send_rest(gw))
    dr_s, dlw, dk2_s, dv_s, dkk, da = _wkv_bwd(dy, s0s, r, lw, k2, v, kk, a)
    do_att, dl_att, dog = _combine_bwd(dyb, o_att, l_att, vecs["attn_out_g"])
    dq, dk, dv = _attn_bwd(do_att, dl_att, o_att, l_att, qkv)
    dproj, dx, dg1, dmu, dw0, dw2p, da0, da2p, dg2m, dk_k, dk_a = _prep_in_proj_bwd(
        proj, pw, (dr_p, dr_s, dlw, dk2_p, dk2_s, dv_p, dv_s, dkk, da, dg), dq, dk, dv, win, x, vecs["mix_norm_g"], dx1)
    gw["w_in"] = _wgrad(dproj, h, 1664, 1024, "wgrad_in")
    gw["decay_w2"] = dw2p[:64]
    gw["iclr_a2"] = da2p[64:]
    gw["gate_g2"] = dg2m
    gv = {"mix_norm_g": dg1, "mu_shift": dmu, "decay_w0": dw0, "iclr_a0": da0, "k_k": dk_k, "k_a": dk_a, "r_k": drk,
          "ln_x_w": dlnw, "ln_x_b": dlnb, "attn_out_g": dog, "ffn_norm_g": dg2n, "final_norm_g": dgf}
    return loss8, dx, gw, gv


N_CHIP = 4
N_DEV = 8
MATS = ("w_in", "w_out", "w_gate", "w_up", "w_down")
LORAS = ("decay_w2", "iclr_a2", "gate_g2")
VECS = (("mix_norm_g", 1024), ("mu_shift", 1792), ("decay_w0", 512), ("iclr_a0", 512), ("k_k", 512), ("k_a", 512),
        ("r_k", 512), ("ln_x_w", 512), ("ln_x_b", 512), ("attn_out_g", 512), ("ffn_norm_g", 1024),
        ("final_norm_g", 1024))
N_VEC = sum(n for _, n in VECS)
N_SMALL = N_VEC + 128
ANY = pl.BlockSpec(memory_space=pl.ANY)


def _flip(v, f):
    return 1 - v if f else v


class _Me:
    def __init__(self, mode):
        x, y, c = lax.axis_index("x"), lax.axis_index("y"), lax.axis_index("c")
        self.core, self.chip, self.dev = c, 2 * x + y, 4 * x + 2 * y + c
        self.sibling = (x, y, 1 - c)
        if mode == "chips":
            self.peers = [(px, py, c) for px, py in ((1 - x, y), (x, 1 - y), (1 - x, 1 - y))]
        else:
            self.peers = [(_flip(x, k & 4), _flip(y, k & 2), _flip(c, k & 1)) for k in range(1, N_DEV)]


def _half(core, rows):
    h = rows // 2
    return pl.ds(pl.multiple_of(core * h, h), h)


_BY_CHIP = ("gather", "whole", "chipsum")


def _peer_copy(srcs, dsts, kinds, send_sems, recv_sems, me, j, i, incoming):
    px, py, pc = me.peers[j]
    pchip, pdev = 2 * px + py, 4 * px + 2 * py + pc
    src, dst, kind = srcs[i], dsts[i], kinds[i]
    if kind in ("gather", "whole"):
        rows = _half(me.core, src.shape[1]) if kind == "gather" else pl.ds(0, src.shape[1])
        src, dst = src.at[me.chip, rows], dst.at[pchip if incoming else me.chip, rows]
    elif kind == "scatter":
        src, dst = src.at[pchip, _half(pc, src.shape[1])], dst.at[pdev if incoming else me.dev]
    elif kind == "chipsum":
        src, dst = src.at[pchip], dst.at[pchip if incoming else me.chip]
    else:
        dst = dst.at[pdev if incoming else me.dev]
    n = len(srcs)
    return pltpu.make_async_remote_copy(src_ref=src, dst_ref=dst, send_sem=send_sems.at[n * j + i],
                                        recv_sem=recv_sems.at[n * j + i], device_id=(px, py, pc), device_id_type=MESH)


def _mode(kinds):
    return "chips" if kinds[0] in _BY_CHIP else "devs"


def _npeer(kinds):
    return N_CHIP - 1 if kinds[0] in _BY_CHIP else N_DEV - 1


def _sibling_halves(gs, name):
    n = len(gs)

    def body(*refs):
        srcs, dsts, send_sems, recv_sems = refs[:n], refs[n:2 * n], refs[2 * n], refs[2 * n + 1]
        me = _Me("chips")

        def copy(i, p):
            return pltpu.make_async_remote_copy(
                src_ref=srcs[i].at[p, _half(1 - me.core, srcs[i].shape[1])], dst_ref=dsts[i].at[p],
                send_sem=send_sems.at[N_CHIP * i + p], recv_sem=recv_sems.at[N_CHIP * i + p],
                device_id=me.sibling, device_id_type=MESH)

        copies = [copy(i, p) for i in range(n) for p in range(N_CHIP)]
        for cp in copies:
            cp.start()
        for cp in copies:
            cp.wait()

    return pl.pallas_call(
        body, name=name, in_specs=[ANY] * n, out_specs=[ANY] * n,
        out_shape=[jax.ShapeDtypeStruct((N_CHIP, g.shape[1] // 2, g.shape[2]), g.dtype) for g in gs],
        scratch_shapes=[pltpu.SemaphoreType.DMA((N_CHIP * n,)), pltpu.SemaphoreType.DMA((N_CHIP * n,))],
    )(*gs)


def _add_halves(g, other, core, tr, name):
    _, h, cols = other.shape

    def body(core_ref, g_ref, o_ref, out_ref):
        out_ref[...] = (g_ref[...].astype(F32) + o_ref[...].astype(F32)).astype(BF16)

    blk = lambda off: pl.BlockSpec((1, tr, cols), lambda p, i, core_ref: (p, core_ref[0] * (h // tr) * off + i, 0))
    return pl.pallas_call(
        body, name=name,
        grid_spec=pltpu.PrefetchScalarGridSpec(num_scalar_prefetch=1, grid=(N_CHIP, h // tr),
                                               in_specs=[blk(1), blk(0)], out_specs=blk(0)),
        out_shape=jax.ShapeDtypeStruct(other.shape, BF16),
        compiler_params=_params(("parallel", "parallel")),
    )(core, g, other)


def _swap_gathered(lands, name):
    n = len(lands)

    def body(*refs):
        dsts, send_sems, recv_sems = refs[n:2 * n], refs[2 * n], refs[2 * n + 1]
        me = _Me("chips")

        def copy(j, i, incoming):
            px, py, _ = me.peers[j]
            rows_out, rows_in = _half(me.core, dsts[i].shape[1]), _half(1 - me.core, dsts[i].shape[1])
            return pltpu.make_async_remote_copy(
                src_ref=dsts[i].at[2 * px + py, rows_out], dst_ref=dsts[i].at[2 * px + py, rows_in if incoming else rows_out],
                send_sem=send_sems.at[n * j + i], recv_sem=recv_sems.at[n * j + i], device_id=me.sibling, device_id_type=MESH)

        sends = [copy(j, i, False) for j in range(3) for i in range(n)]
        for cp in sends:
            cp.start()
        for j in range(3):
            for i in range(n):
                copy(j, i, True).wait_recv()
        for cp in sends:
            cp.wait_send()

    return pl.pallas_call(
        body, name=name, in_specs=[ANY] * n, out_specs=[ANY] * n,
        out_shape=[jax.ShapeDtypeStruct(l.shape, l.dtype) for l in lands],
        input_output_aliases={i: i for i in range(n)},
        scratch_shapes=[pltpu.SemaphoreType.DMA((3 * n,)), pltpu.SemaphoreType.DMA((3 * n,))],
    )(*lands)


def _join_halves(sums, name):
    n = len(sums)

    def body(*refs):
        dsts, send_sems, recv_sems = refs[n:2 * n], refs[2 * n], refs[2 * n + 1]
        me = _Me("chips")

        def copy(i, incoming):
            mine, other = _half(me.core, dsts[i].shape[0]), _half(1 - me.core, dsts[i].shape[0])
            return pltpu.make_async_remote_copy(src_ref=dsts[i].at[mine], dst_ref=dsts[i].at[other if incoming else mine],
                                                send_sem=send_sems.at[i], recv_sem=recv_sems.at[i],
                                                device_id=me.sibling, device_id_type=MESH)

        sends = [copy(i, False) for i in range(n)]
        for cp in sends:
            cp.start()
        for i in range(n):
            copy(i, True).wait_recv()
        for cp in sends:
            cp.wait_send()

    return pl.pallas_call(
        body, name=name, in_specs=[ANY] * n, out_specs=[ANY] * n,
        out_shape=[jax.ShapeDtypeStruct(s.shape, s.dtype) for s in sums],
        input_output_aliases={i: i for i in range(n)},
        scratch_shapes=[pltpu.SemaphoreType.DMA((n,)), pltpu.SemaphoreType.DMA((n,))],
    )(*sums)


HBM = pl.BlockSpec(memory_space=pltpu.HBM)
SEM = pl.BlockSpec(memory_space=pltpu.SEMAPHORE)
EFFECT = pltpu.SideEffectType.DATAFLOW_SIDE_EFFECTING


def _swap_start(arrs, lands, kinds, name):
    n = len(lands)
    ops = list(lands) if arrs is None else [*arrs, *lands]
    k = len(ops)

    def body(*refs):
        srcs, dsts, send_sems, recv_sems, token = refs[:n], refs[k - n:k], refs[k], refs[k + 1], refs[-1]
        me = _Me(_mode(kinds))
        for j in range(len(me.peers)):
            for i in range(n):
                _peer_copy(srcs, dsts, kinds, send_sems, recv_sems, me, j, i, False).start()
        token[...] = jnp.zeros_like(token)

    ns = _npeer(kinds) * n
    outs = pl.pallas_call(
        body, name=name,
        out_shape=(pltpu.SemaphoreType.DMA((ns,)), pltpu.SemaphoreType.DMA((ns,)),
                   *[pltpu.HBM(a.shape, a.dtype) for a in ops], jax.ShapeDtypeStruct((8, 128), F32)),
        in_specs=[HBM] * k, out_specs=(SEM, SEM, *[HBM] * k, pl.BlockSpec(memory_space=pltpu.VMEM)),
        input_output_aliases={i: 2 + i for i in range(k)},
        compiler_params=pltpu.CompilerParams(has_side_effects=EFFECT),
    )(*[pltpu.with_memory_space_constraint(a, pltpu.HBM) for a in ops])
    return outs[0], outs[1], outs[2:2 + k - n], outs[2 + k - n:2 + k], outs[-1]


def _swap_wait(send_sems, recv_sems, srcs_thru, lands_thru, after, kinds, name):
    n = len(lands_thru)
    ops = [*srcs_thru, *lands_thru]
    k = len(ops)

    def body(*refs):
        srcs, dsts, s_sems, r_sems = refs[:n], refs[k - n:k], refs[k], refs[k + 1]
        me = _Me(_mode(kinds))
        for j in range(len(me.peers)):
            for i in range(n):
                cp = _peer_copy(srcs, dsts, kinds, s_sems, r_sems, me, j, i, True)
                cp.wait_send()
                cp.wait_recv()

    outs = pl.pallas_call(
        body, name=name,
        out_shape=tuple(pltpu.HBM(a.shape, a.dtype) for a in ops),
        in_specs=[HBM] * k + [SEM, SEM, ANY], out_specs=tuple([HBM] * k),
        input_output_aliases={i: i for i in range(k)},
        compiler_params=pltpu.CompilerParams(has_side_effects=EFFECT),
    )(*ops, send_sems, recv_sems, after)
    return outs[k - n:]


def _wait_and_swap(send_sems, recv_sems, lands_thru, after, kinds, name):
    n = len(lands_thru)

    pairs = [(j, i) for j in range(3) for i in range(n)]

    def swap(dsts, d_send, d_recv, me, j, i, incoming):
        px, py, _ = me.peers[j]
        rows_out, rows_in = _half(me.core, dsts[i].shape[1]), _half(1 - me.core, dsts[i].shape[1])
        return pltpu.make_async_remote_copy(
            src_ref=dsts[i].at[2 * px + py, rows_out], dst_ref=dsts[i].at[2 * px + py, rows_in if incoming else rows_out],
            send_sem=d_send.at[n * j + i], recv_sem=d_recv.at[n * j + i], device_id=me.sibling, device_id_type=MESH)

    def arrive_and_start(*refs):
        dsts, s_sems, r_sems, d_send, d_recv = refs[:n], refs[n], refs[n + 1], refs[n + 3], refs[n + 4]
        me = _Me("chips")
        for j in range(3):
            for i in range(n):
                cp = _peer_copy(dsts, dsts, kinds, s_sems, r_sems, me, j, i, True)
                cp.wait_recv()
                cp.wait_send()
            for i in range(n):
                swap(dsts, d_send, d_recv, me, j, i, False).start()

    outs = pl.pallas_call(
        arrive_and_start, name=name + "_start",
        out_shape=(pltpu.SemaphoreType.DMA((3 * n,)), pltpu.SemaphoreType.DMA((3 * n,)),
                   *[pltpu.HBM(a.shape, a.dtype) for a in lands_thru]),
        in_specs=[HBM] * n + [SEM, SEM, ANY], out_specs=(SEM, SEM, *[HBM] * n),
        input_output_aliases={i: 2 + i for i in range(n)},
        compiler_params=pltpu.CompilerParams(has_side_effects=EFFECT),
    )(*lands_thru, send_sems, recv_sems, after)

    def swapped(*refs):
        dsts, d_send, d_recv = refs[:n], refs[n], refs[n + 1]
        me = _Me("chips")
        for j, i in pairs:
            cp = swap(dsts, d_send, d_recv, me, j, i, True)
            cp.wait_recv()
            cp.wait_send()

    return list(pl.pallas_call(
        swapped, name=name + "_wait",
        out_shape=tuple(pltpu.HBM(a.shape, a.dtype) for a in lands_thru),
        in_specs=[HBM] * n + [SEM, SEM], out_specs=tuple([HBM] * n),
        input_output_aliases={i: i for i in range(n)},
        compiler_params=pltpu.CompilerParams(has_side_effects=EFFECT),
    )(*outs[2:], outs[0], outs[1]))


def _adamw(w, g, m, v):
    m = ADAM_B1 * m + (1.0 - ADAM_B1) * g
    v = ADAM_B2 * v + (1.0 - ADAM_B2) * (g * g)
    m_hat = m / (1.0 - ADAM_B1 ** ADAM_STEP)
    v_hat = v / (1.0 - ADAM_B2 ** ADAM_STEP)
    delta = -ADAM_LR * (m_hat / (jnp.sqrt(v_hat) + ADAM_EPS) + ADAM_WD * w)
    return delta, m, v


def _reduce8(rbuf, core, tr, name):
    slots, h, cols = rbuf.shape

    def body(core_ref, r_ref, g_ref):
        g = r_ref[0].astype(F32)
        for s in range(1, slots):
            g = g + r_ref[s].astype(F32)
        g_ref[...] = g

    return pl.pallas_call(
        body, name=name,
        grid_spec=pltpu.PrefetchScalarGridSpec(
            num_scalar_prefetch=1, grid=(h // tr,),
            in_specs=[pl.BlockSpec((slots, tr, cols), lambda i, core_ref: (0, i, 0))],
            out_specs=pl.BlockSpec((tr, cols), lambda i, core_ref: (core_ref[0] * (h // tr) + i, 0))),
        out_shape=jax.ShapeDtypeStruct((2 * h, cols), F32),
        compiler_params=_params(("parallel",)),
    )(core, rbuf)


def _adamw_call(g, w, m, v, tr, name):
    _, rows, cols = w.shape

    def body(g_in, w_ref, m_ref, v_ref, g_ref, d_ref, nm_ref, nv_ref):
        g = g_in[...]
        g_ref[0] = g
        d_ref[0], nm_ref[0], nv_ref[0] = _adamw(w_ref[0], g, m_ref[0], v_ref[0])

    row = pl.BlockSpec((1, tr, cols), lambda i: (0, i, 0))
    return pl.pallas_call(
        body, name=name, grid=(rows // tr,),
        in_specs=[pl.BlockSpec((tr, cols), lambda i: (i, 0)), row, row, row], out_specs=[row] * 4,
        out_shape=[jax.ShapeDtypeStruct(w.shape, F32)] * 4,
        compiler_params=_params(("parallel",)),
    )(g, w, m, v)


def _adamw_lora(g, ws3, ms3, vs3):
    def body(g_in, *refs):
        ins, outs = refs[:9], refs[9:]
        r0 = 0
        for i in range(3):
            rows = ins[i].shape[1]
            g = g_in[r0:r0 + rows, :]
            outs[i][0] = g
            outs[3 + i][0], outs[6 + i][0], outs[9 + i][0] = _adamw(ins[i][0], g, ins[3 + i][0], ins[6 + i][0])
            r0 += rows

    return pl.pallas_call(body, name="adamw_lora",
                          out_shape=[jax.ShapeDtypeStruct(a.shape, F32) for a in ws3] * 4)(g, *ws3, *ms3, *vs3)


def _rowsum_small(parts, loss8, after):
    def body(*refs):
        out = refs[-1]
        c0 = 0
        for ref in refs[:-2]:
            n = ref.shape[1]
            out[:, c0:c0 + n] = jnp.sum(ref[...], axis=0, keepdims=True)
            c0 += n

    k = len(parts) + 1
    return pl.pallas_call(body, name="rowsum_small", in_specs=[pl.BlockSpec(memory_space=pltpu.VMEM)] * k + [ANY],
                          out_shape=jax.ShapeDtypeStruct((1, N_SMALL), F32))(*parts, loss8, after)


def _reduce_adamw_small(sbuf, ws, ms, vs):
    nv = len(ws)

    def body(*refs):
        s_ref, ins, outs = refs[0], refs[1:1 + 3 * nv], refs[1 + 3 * nv:]
        tot = s_ref[0]
        for s in range(1, N_DEV):
            tot = tot + s_ref[s]
        c0 = 0
        for i in range(nv):
            rows, cols = ins[i].shape
            n = rows * cols
            for r in range(rows):
                outs[i][r:r + 1, :] = tot[:, c0 + cols * r:c0 + cols * (r + 1)]
            g = outs[i][...]
            outs[nv + i][...], outs[2 * nv + i][...], outs[3 * nv + i][...] = _adamw(
                ins[i][...], g, ins[nv + i][...], ins[2 * nv + i][...])
            c0 += n
        outs[-1][...] = tot[:, c0:]

    return pl.pallas_call(
        body, name="reduce_adamw_small",
        out_shape=[jax.ShapeDtypeStruct(a.shape, F32) for a in ws] * 4 + [jax.ShapeDtypeStruct((1, 128), F32)],
    )(sbuf, *ws, *ms, *vs)


_TRANSPOSED = ("w_in", "w_gate", "w_up")
_ROW_STACKED = MATS
_ADAM_TILE = {"w_in": 208, "w_out": 256, "w_gate": 176, "w_up": 176, "w_down": 176, "lora": 256}
_SUM_TILE = {"w_in": 208, "w_out": 128, "w_gate": 176, "w_up": 176, "w_down": 176, "lora": 128}


def _full(n, stacked):
    p, r, c = stacked.shape
    if n in _ROW_STACKED:
        return stacked.reshape(p * r, c)
    return jnp.transpose(stacked, (1, 0, 2)).reshape(r, p * c)


def _by_chip(n, full):
    if n in _ROW_STACKED:
        return full.reshape(N_CHIP, full.shape[0] // N_CHIP, full.shape[1])
    r, c = full.shape
    return jnp.transpose(full.reshape(r, N_CHIP, c // N_CHIP), (1, 0, 2))


def _with_own(land_shape, dtype, own, slot):
    return lax.dynamic_update_slice(lax.empty(land_shape, dtype), own[None], (slot,) + (0,) * own.ndim)


def _cast_into_slot(a, chip, tr, name, after=None):
    rows, cols = a.shape

    def body(chip_ref, a_ref, *rest):
        rest[-1][0] = a_ref[...].astype(BF16)

    extra = [] if after is None else [after]
    return pl.pallas_call(
        body, name=name,
        grid_spec=pltpu.PrefetchScalarGridSpec(
            num_scalar_prefetch=1, grid=(rows // tr,),
            in_specs=[pl.BlockSpec((tr, cols), lambda i, chip_ref: (i, 0))] + [ANY] * len(extra),
            out_specs=pl.BlockSpec((1, tr, cols), lambda i, chip_ref: (chip_ref[0], i, 0))),
        out_shape=jax.ShapeDtypeStruct((N_CHIP, rows, cols), BF16),
        compiler_params=_params(("parallel",)),
    )(chip, a, *extra)


def kernel(x, mix_norm_g, w_in, mu_shift, decay_w0, decay_w2, iclr_a0, iclr_a2, gate_g2, k_k, k_a, r_k, ln_x_w, ln_x_b, attn_out_g, w_out, ffn_norm_g, w_gate, w_up, w_down, final_norm_g, loss_target, m_mix_norm_g, m_w_in, m_mu_shift, m_decay_w0, m_decay_w2, m_iclr_a0, m_iclr_a2, m_gate_g2, m_k_k, m_k_a, m_r_k, m_ln_x_w, m_ln_x_b, m_attn_out_g, m_w_out, m_ffn_norm_g, m_w_gate, m_w_up, m_w_down, m_final_norm_g, v_mix_norm_g, v_w_in, v_mu_shift, v_decay_w0, v_decay_w2, v_iclr_a0, v_iclr_a2, v_gate_g2, v_k_k, v_k_a, v_r_k, v_ln_x_w, v_ln_x_b, v_attn_out_g, v_w_out, v_ffn_norm_g, v_w_gate, v_w_up, v_w_down, v_final_norm_g):
    names = ("mix_norm_g", "w_in", "mu_shift", "decay_w0", "decay_w2", "iclr_a0", "iclr_a2", "gate_g2", "k_k", "k_a",
             "r_k", "ln_x_w", "ln_x_b", "attn_out_g", "w_out", "ffn_norm_g", "w_gate", "w_up", "w_down", "final_norm_g")
    w = dict(zip(names, (mix_norm_g, w_in, mu_shift, decay_w0, decay_w2, iclr_a0, iclr_a2, gate_g2, k_k, k_a, r_k,
                         ln_x_w, ln_x_b, attn_out_g, w_out, ffn_norm_g, w_gate, w_up, w_down, final_norm_g)))
    m = dict(zip(names, (m_mix_norm_g, m_w_in, m_mu_shift, m_decay_w0, m_decay_w2, m_iclr_a0, m_iclr_a2, m_gate_g2,
                         m_k_k, m_k_a, m_r_k, m_ln_x_w, m_ln_x_b, m_attn_out_g, m_w_out, m_ffn_norm_g, m_w_gate,
                         m_w_up, m_w_down, m_final_norm_g)))
    v = dict(zip(names, (v_mix_norm_g, v_w_in, v_mu_shift, v_decay_w0, v_decay_w2, v_iclr_a0, v_iclr_a2, v_gate_g2,
                         v_k_k, v_k_a, v_r_k, v_ln_x_w, v_ln_x_b, v_attn_out_g, v_w_out, v_ffn_norm_g, v_w_gate,
                         v_w_up, v_w_down, v_final_norm_g)))
    first = ("w_in", "lora")
    rest = ("w_out", "w_gate", "w_up", "w_down")
    xi, yi, ci = lax.axis_index("x"), lax.axis_index("y"), lax.axis_index("c")
    my_chip, my_dev = 2 * xi + yi, 4 * xi + 2 * yi + ci
    gather, scatter = ("gather",) * 4, ("scatter",) * 4

    def stored(d):
        out = {n: jnp.transpose(d[n][0]) if n in _TRANSPOSED else d[n][0] for n in MATS}
        out["lora"] = jnp.concatenate([d[n][0] for n in LORAS], axis=0)
        return out

    ws, ms, vs = stored(w), stored(m), stored(v)
    chip = jnp.reshape(my_chip, (1,)).astype(jnp.int32)
    early = _swap_start(None, [_cast_into_slot(ws["w_in"], chip, _ADAM_TILE["w_in"], "cast_w_in")], gather[:1],
                        "gather_first_start")
    early_lora = _swap_start(None, [_with_own((N_CHIP,) + ws["lora"].shape, F32, ws["lora"], my_chip)], gather[:1],
                             "gather_lora_start")
    lands = [_cast_into_slot(ws[n], chip, _ADAM_TILE[n], "cast_" + n, after=early[4]) for n in rest]
    gather_rest = ("gather", "gather", "whole", "whole")
    ssem, rsem, srcs_thru, lands_thru, tok = _swap_start(None, lands, gather_rest, "gather_rest_start")
    lora_all, = _wait_and_swap(early_lora[0], early_lora[1], early_lora[3], tok, gather[:1], "gather_lora_wait_halves")
    w2p, a2p, g2m = _unstack_lora(lora_all)
    win_all, = _wait_and_swap(early[0], early[1], early[3], w2p, gather[:1], "gather_first_wait_halves")
    win = _full("w_in", win_all)

    vecs = {n: w[n].reshape(1, sz) for n, sz in VECS}

    def get_rest(after):
        got_rest = _swap_wait(ssem, rsem, srcs_thru, lands_thru, after, gather_rest, "gather_rest_wait")
        swapped = _swap_gathered(got_rest[:2], "gather_rest_halves")
        return [_full(n, z) for n, z in zip(rest, [*swapped, *got_rest[2:]])]

    flight = []

    def my_half(g):
        h = g.shape[1] // 2
        return lax.dynamic_slice(g, (my_chip, ci * h, 0), (1, h, g.shape[2]))[0]

    def send_rest(gw):
        gs = [_by_chip(n, gw[n]) for n in rest]
        into = [_with_own((N_DEV,) + my_half(g).shape, BF16, my_half(g), my_dev) for g in gs]
        flight.extend(_swap_start(gs, into, scatter, "exchange_rest_start"))
        return flight[4]

    loss8, dx, gw, gv = _local_step(x[0], loss_target[0], win, vecs, w2p, a2p, g2m, get_rest, send_rest)

    core = jnp.reshape(ci, (1,)).astype(jnp.int32)
    gs = [_by_chip("w_in", gw["w_in"]),
          jnp.concatenate([_by_chip(n, gw[n]) for n in LORAS], axis=1).astype(BF16)]
    theirs = _sibling_halves(gs, "presum_halves")
    sums = [_add_halves(g, o, core, _SUM_TILE[n], "chipsum_" + n) for n, g, o in zip(first, gs, theirs)]
    own = [lax.dynamic_index_in_dim(s, my_chip, 0, keepdims=False) for s in sums]
    last = _swap_start(sums, [_with_own(s.shape, BF16, o, my_chip) for s, o in zip(sums, own)], ("chipsum",) * 2,
                       "exchange_first_start")
    small = _rowsum_small([gv[n] for n, _ in VECS], loss8, after=last[4])
    vecs_out = _swap_start([small], [_with_own((N_DEV,) + small.shape, F32, small, my_dev)], ("all",),
                           "exchange_vectors_start")


    def update(group, rbufs, tag):
        sums = [_reduce8(rb, core, _SUM_TILE[n], "reduce_" + n) for n, rb in zip(group, rbufs)]
        gsum = _join_halves(sums, "join_halves_" + tag)
        out = {}
        for n, g in zip(group, gsum):
            if n == "lora":
                r = _adamw_lora(g, *[[d[k] for k in LORAS] for d in (w, m, v)])
                for i, name in enumerate(LORAS):
                    out[name] = r[i::3]
            else:
                r = _adamw_call(g, ws[n][None], ms[n][None], vs[n][None], _ADAM_TILE[n], "adamw_" + n)
                out[n] = [jnp.transpose(z[0])[None] for z in r] if n in _TRANSPOSED else r
        return out, r[1]

    res, done = update(rest, _swap_wait(flight[0], flight[1], flight[2], flight[3], vecs_out[4], scatter,
                                        "exchange_rest_wait"), "rest")
    got = _swap_wait(last[0], last[1], last[2], last[3], done, ("chipsum",) * 2, "exchange_first_wait")
    res_first, done = update(first, got, "first")
    res.update(res_first)
    sbuf = _swap_wait(vecs_out[0], vecs_out[1], vecs_out[2], vecs_out[3], done, ("all",), "exchange_vectors_wait")[0]
    rows = lambda d: [d[n].reshape(-1, d[n].shape[-1]) for n, _ in VECS]
    small_res = _reduce_adamw_small(sbuf, rows(w), rows(m), rows(v))

    outs = []
    for k in range(4):
        piece = {n: r[k] for n, r in res.items()}
        for i, (n, _) in enumerate(VECS):
            piece[n] = small_res[k * len(VECS) + i].reshape(w[n].shape)
        outs.extend(piece[n] for n in names)
    return (small_res[-1][0, 0], dx[None], *outs)
```

```python
import jax
import jax.numpy as jnp
from jax import lax
from jax.experimental import pallas as pl
from jax.experimental.pallas import tpu as pltpu

F32 = jnp.float32
BF16 = jnp.bfloat16

D_MODEL = 1024
HEAD_DIM = 64
RW = 512
N_PAIR = RW // 128
SHIFT_COLS = 1792
IN_COLS = 3328
D_FF = 2816
FF_CHUNK = 256
NORM_EPS = 1e-6
GN_EPS = 64e-5
CHUNK = 64
SUB = 16
WKV_PASSES = 1
ATTN_PASSES = 1
ATTN_BLOCK = 128
DILATIONS = (1, 4, 16)
NEG = -1e30
ADAM_LR, ADAM_B1, ADAM_B2, ADAM_EPS, ADAM_WD, ADAM_STEP = 0.001, 0.9, 0.999, 1e-08, 0.01, 10
VMEM_LIMIT = 56 * 1024 * 1024
MESH = pl.DeviceIdType.MESH


def _params(sem=None, **kw):
    return pltpu.CompilerParams(dimension_semantics=sem, vmem_limit_bytes=VMEM_LIMIT, **kw)


def _dot(a, b):
    return lax.dot_general(a, b, (((1,), (0,)), ((), ())), preferred_element_type=F32)


def _dot_nt(a, b):
    return lax.dot_general(a, b, (((1,), (1,)), ((), ())), preferred_element_type=F32)


def _dot_tn(a, b):
    return lax.dot_general(a, b, (((0,), (0,)), ((), ())), preferred_element_type=F32)


_FORMS = {"nn": ((1,), (0,)), "nt": ((1,), (1,)), "tn": ((0,), (0,))}


def _dg(a, b, form):
    if a.ndim == 3 or b.ndim == 3:
        nb = a.shape[0] if a.ndim == 3 else b.shape[0]
        return jnp.stack([_dg(a[i] if a.ndim == 3 else a, b[i] if b.ndim == 3 else b, form) for i in range(nb)], axis=0)
    return lax.dot_general(a, b, (_FORMS[form], ((), ())), preferred_element_type=F32)


def _split2(x):
    hi = x.astype(BF16)
    return hi, (x - hi.astype(F32)).astype(BF16)


def _split3(x):
    hi = x.astype(BF16)
    rest = x - hi.astype(F32)
    mid = rest.astype(BF16)
    return hi, mid, (rest - mid.astype(F32)).astype(BF16)


def _mm_raw(a, b, form, mode):
    if mode == 1:
        return _dg(a.astype(BF16), b.astype(BF16), form)
    if mode == 3:
        ah, al = _split2(a)
        bh, bl = _split2(b)
        return _dg(ah, bh, form) + (_dg(ah, bl, form) + _dg(al, bh, form))
    if mode == "L3":
        ab = a.astype(BF16)
        b1, b2, b3 = _split3(b)
        if form == "nn":
            n = b.shape[-1]
            wide = _dg(ab, jnp.concatenate([b1, b2, b3], axis=-1), form)
            return wide[..., :n] + (wide[..., n:2 * n] + wide[..., 2 * n:])
        return _dg(ab, b1, form) + (_dg(ab, b2, form) + _dg(ab, b3, form))
    assert mode == "R3", mode
    bb = b.astype(BF16)
    a1, a2, a3 = _split3(a)
    if form in ("nn", "nt"):
        m = a.shape[-2]
        tall = _dg(jnp.concatenate([a1, a2, a3], axis=-2), bb, form)
        return tall[..., :m, :] + (tall[..., m:2 * m, :] + tall[..., 2 * m:, :])
    return _dg(a1, bb, form) + (_dg(a2, bb, form) + _dg(a3, bb, form))


def _mm(a, b, form, mode):
    @jax.custom_vjp
    def f(a, b):
        return _mm_raw(a, b, form, mode)

    def fwd(a, b):
        return _mm_raw(a, b, form, mode), (a, b)

    def bwd(res, ct):
        a, b = res
        la = {1: 1, 3: 3, "L3": None, "R3": "R3"}[mode]
        lb = {1: 1, 3: 3, "L3": "L3", "R3": None}[mode]
        if form == "nn":
            da = None if la is None else _mm_raw(ct, b, "nt", la)
            db = None if lb is None else _mm_raw(a, ct, "tn", lb)
        elif form == "nt":
            da = None if la is None else _mm_raw(ct, b, "nn", la)
            db = None if lb is None else _mm_raw(ct, a, "tn", "R3" if lb == "L3" else lb)
        else:
            da = None if la is None else _mm_raw(b, ct, "nt", "L3" if la == "R3" else la)
            db = None if lb is None else _mm_raw(a, ct, "nn", lb)
        return (jnp.zeros_like(a) if da is None else da, jnp.zeros_like(b) if db is None else db)

    f.defvjp(fwd, bwd)
    return f(a, b)


def _seg_ones(n):
    r = lax.broadcasted_iota(jnp.int32, (n, n), 0) // HEAD_DIM
    c = lax.broadcasted_iota(jnp.int32, (n, n), 1) // HEAD_DIM
    return (r == c).astype(F32)


def _segsum(x, seg):
    return _mm(x, seg, "nn", "R3")


def _rms_fwd(x, g):
    rstd = lax.rsqrt(jnp.mean(x * x, axis=-1, keepdims=True) + NORM_EPS)
    return x * rstd * g


def _rms_bwd(dy, x, g):
    rstd = lax.rsqrt(jnp.mean(x * x, axis=-1, keepdims=True) + NORM_EPS)
    xn = x * rstd
    dxn = dy * g
    dx = rstd * (dxn - xn * jnp.mean(dxn * xn, axis=-1, keepdims=True))
    return dx, dy * xn


def _sigmoid(x):
    return 1.0 / (1.0 + jnp.exp(-x))


def _softplus(x):
    return jnp.maximum(x, 0.0) + jnp.log(1.0 + jnp.exp(-jnp.abs(x)))


def _acc(ref, val, first):
    @pl.when(first)
    def _():
        ref[...] = val

    @pl.when(jnp.logical_not(first))
    def _():
        ref[...] += val


def _colsum8(v):
    rows, n = v.shape
    return jnp.sum(v.reshape(rows // 8, 8, n), axis=0)


def _prep_fn(p, pprev, mu, w0, w2p, a0, a2p, g2, k_k, k_a):
    seg = _seg_ones(RW)
    ps = p + (pprev - p) * mu
    r = ps[:, 0:RW]
    k = ps[:, RW:2 * RW]
    v = ps[:, 2 * RW:3 * RW]
    xwa = ps[:, 3 * RW:3 * RW + 128]
    xg = ps[:, 3 * RW + 128:3 * RW + 256]
    wraw = -_softplus(-(w0 + _mm(jnp.tanh(xwa), w2p, "nn", 3))) - 0.5
    lw = -jnp.exp(wraw)
    a = _sigmoid(a0 + _mm(xwa, a2p, "nn", 3))
    g = _mm(_sigmoid(xg), g2, "nn", 3)
    kk = k * k_k
    kk = kk / jnp.maximum(jnp.sqrt(_segsum(kk * kk, seg)), 1e-12)
    k2 = k * (1.0 + (a - 1.0) * k_a)
    return r, lw, k2, v, kk, a, g


def _transposed(z):
    return jnp.stack([z[i].T for i in range(z.shape[0])], axis=0) if z.ndim == 3 else z.T


def _solve_unit_lower(lmat, rhs):
    c = lmat.shape[-1]
    row = lax.broadcasted_iota(jnp.int32, (c, c), 0)
    col = lax.broadcasted_iota(jnp.int32, (c, c), 1)
    eye = (row == col).astype(F32)
    ld = jnp.where(row // SUB == col // SUB, lmat, 0.0)
    lo = lmat - ld
    x = eye + ld
    m = ld
    mm = lambda p, q: _mm(p, q, "nn", WKV_PASSES)
    cat = jnp.concatenate
    m = mm(m, m)
    for _ in range(2):
        mx = mm(m, cat([m, x], axis=-1))
        m, x = mx[..., :c], x + mx[..., c:]
    x = x + mm(m, x)
    gw = mm(x, cat([lo, rhs], axis=-1))
    g, w = gw[..., :c], gw[..., c:]
    gg = mm(g, cat([g, w], axis=-1))
    w = w + gg[..., c:]
    return w + mm(gg[..., :c], w)


def _wkv_chunk_fn(s0, r, lw, k, v, kk, a):
    c = r.shape[-2]
    n = 2 * c
    row = lax.broadcasted_iota(jnp.int32, (n, n), 0)
    col = lax.broadcasted_iota(jnp.int32, (n, n), 1)
    same = (row // c) == (col // c)
    incl = jnp.logical_and(row >= col, same)
    strict = jnp.logical_and(row > col, same)
    sel = (lax.broadcasted_iota(jnp.int32, (n, 128), 0) // c) == (lax.broadcasted_iota(jnp.int32, (n, 128), 1) // HEAD_DIM)
    two = lambda z: jnp.concatenate([z, z], axis=-2)
    lw2 = two(lw)
    mm = lambda p_, q_, form: _mm(p_, q_, form, WKV_PASSES)
    cl = _mm(incl.astype(F32), lw2, "nn", "L3")
    p = jnp.exp(cl)
    pinv = jnp.exp(-cl)
    pprev = jnp.exp(cl - lw2)
    kk2 = two(kk)
    at = jnp.where(sel, -kk2 * pprev, 0.0)
    bt = jnp.where(sel, kk2 * two(a) * pinv, 0.0)
    kt = jnp.where(sel, two(k) * pinv, 0.0)
    rt = jnp.where(sel, two(r) * p, 0.0)
    vt = jnp.where(sel, two(v), 0.0)
    cat = jnp.concatenate
    bk = cat([bt, kt], axis=-2)
    arbk = mm(cat([at, rt], axis=-2), bk, "nt")
    ab, ak = jnp.where(strict, arbk[..., :n, :n], 0.0), jnp.where(strict, arbk[..., :n, n:], 0.0)
    rb, rk = jnp.where(incl, arbk[..., n:, :n], 0.0), jnp.where(incl, arbk[..., n:, n:], 0.0)
    s0t = _transposed(s0)
    u = _solve_unit_lower(ab, mm(cat([at, ak], axis=-1), cat([s0t, vt], axis=-2), "nn"))
    y2 = mm(cat([rt, rb, rk], axis=-1), cat([s0t, u, vt], axis=-2), "nn")
    plast = jnp.exp(jnp.sum(lw, axis=-2, keepdims=True))
    s1 = (s0 + mm(cat([u, vt], axis=-2), bk, "tn")) * plast
    r2 = lax.broadcasted_iota(jnp.int32, (128, 128), 0) // HEAD_DIM
    c2 = lax.broadcasted_iota(jnp.int32, (128, 128), 1) // HEAD_DIM
    return y2[..., :c, :] + y2[..., c:, :], jnp.where(r2 == c2, s1, 0.0)


def _post_fn(y, r, k2, v, g, lnw, lnb, rk):
    seg = _seg_ones(RW)
    mean = _segsum(y, seg) * (1.0 / HEAD_DIM)
    yc = y - mean
    var = _segsum(yc * yc, seg) * (1.0 / HEAD_DIM)
    yn = yc * lax.rsqrt(var + GN_EPS)
    out = yn * lnw + lnb + _segsum(r * k2 * rk, seg) * v
    return out * g


def _attn_block_fn(q, kc, vc, kp=None, vp=None):
    n = ATTN_BLOCK
    qi = lax.broadcasted_iota(jnp.int32, (n, n), 0)
    kj = lax.broadcasted_iota(jnp.int32, (n, n), 1)
    lane = lax.broadcasted_iota(jnp.int32, (1, 128), 1)
    scale = HEAD_DIM ** -0.5
    valid = kj <= qi
    keys, vals = kc, vc
    if kp is not None:
        valid = jnp.concatenate([valid, kj >= qi], axis=-1)
        keys, vals = jnp.concatenate([kc, kp], axis=-2), jnp.concatenate([vc, vp], axis=-2)
    m0 = (lane // HEAD_DIM) == 0
    q2 = jnp.concatenate([jnp.where(m0, q, 0.0), jnp.where(m0, 0.0, q)], axis=-2)
    valid2 = jnp.concatenate([valid, valid], axis=-2)
    s = jnp.where(valid2, _mm(q2, keys, "nt", ATTN_PASSES) * scale, NEG)
    m = jnp.max(s, axis=-1, keepdims=True)
    p = jnp.exp(s - m)
    den = jnp.sum(p, axis=-1, keepdims=True)
    o2 = _mm(p, vals, "nn", ATTN_PASSES) / den
    l2 = m + jnp.log(den)
    return jnp.where(m0, o2[..., :n, :], o2[..., n:, :]), jnp.where(m0, l2[..., :n, :], l2[..., n:, :])


def _attn_block_bwd(q, kc, vc, kp, vp, o, lse, do, dl):
    n = ATTN_BLOCK
    cat = jnp.concatenate
    qi = lax.broadcasted_iota(jnp.int32, (n, n), 0)
    kj = lax.broadcasted_iota(jnp.int32, (n, n), 1)
    m0 = (lax.broadcasted_iota(jnp.int32, (1, 128), 1) // HEAD_DIM) == 0
    scale = HEAD_DIM ** -0.5
    valid = kj <= qi
    keys, vals = kc, vc
    if kp is not None:
        valid = cat([valid, kj >= qi], axis=-1)
        keys, vals = cat([kc, kp], axis=-2), cat([vc, vp], axis=-2)
    stack = lambda z: cat([jnp.where(m0, z, 0.0), jnp.where(m0, 0.0, z)], axis=-2)
    q2, do2 = stack(q), stack(do)
    lse2 = cat([jnp.max(jnp.where(m0, lse, NEG), axis=-1, keepdims=True),
                jnp.max(jnp.where(m0, NEG, lse), axis=-1, keepdims=True)], axis=-2)
    delta = jnp.sum(do2 * cat([o, o], axis=-2), axis=-1, keepdims=True)
    dlse = jnp.sum(stack(dl), axis=-1, keepdims=True)
    mm = lambda a, b, form: _mm_raw(a, b, form, ATTN_PASSES)
    s = jnp.where(cat([valid, valid], axis=-2), mm(q2, keys, "nt") * scale, NEG)
    p = jnp.exp(s - lse2)
    ds = p * (mm(do2, vals, "nt") - delta + dlse)
    dq2 = mm(ds, keys, "nn") * scale
    dq = jnp.where(m0, dq2[..., :n, :], dq2[..., n:, :])
    dkeys = mm(ds, q2, "tn") * scale
    dvals = mm(p, do2, "tn")
    if kp is None:
        return dq, dkeys, dvals
    return dq, dkeys[..., :n, :], dvals[..., :n, :], dkeys[..., n:, :], dvals[..., n:, :]


def _combine_fn(o1, o2, o3, l1, l2, l3, og):
    seg = _seg_ones(o1.shape[-1])
    m = jnp.maximum(jnp.maximum(l1, l2), l3)
    e1, e2, e3 = jnp.exp(l1 - m), jnp.exp(l2 - m), jnp.exp(l3 - m)
    o = (e1 * o1 + e2 * o2 + e3 * o3) / (e1 + e2 + e3)
    o = o * lax.rsqrt(_segsum(o * o, seg) * (1.0 / HEAD_DIM) + NORM_EPS)
    return o * og


def _shifted(p, last8, first):
    prow = jnp.where(first, 0.0, last8[7:8, :])
    rolled = pltpu.roll(p, 1, axis=0)
    rid = lax.broadcasted_iota(jnp.int32, p.shape, 0)
    return jnp.where(rid == 0, prow, rolled)


_PREP_TM = 256


def _prep_specs(tm):
    vec = lambda n: pl.BlockSpec((1, n), lambda i: (0, 0))
    mat = lambda r, n: pl.BlockSpec((r, n), lambda i: (0, 0))
    return [vec(SHIFT_COLS), vec(RW), mat(128, RW), vec(RW), mat(128, RW), mat(128, RW), vec(RW), vec(RW)]


def _in_proj_prep(x, g1, win, pw):
    t = x.shape[0]
    tm = _PREP_TM

    def body(x_ref, g_ref, w_ref, mu, w0, w2p, a0, a2p, g2, k_k, k_a, h_ref, pa_ref, qkv_ref, *rest):
        outs, carry = rest[:7], rest[7]

        @pl.when(pl.program_id(0) == 0)
        def _():
            carry[...] = jnp.zeros_like(carry)

        h = _rms_fwd(x_ref[...], g_ref[...]).astype(BF16)
        h_ref[...] = h
        proj = _dot_nt(h, w_ref[...])
        p = proj[:, :SHIFT_COLS]
        pa_ref[...] = p
        for j in range(3):
            for pr in range(N_PAIR):
                c0 = SHIFT_COLS + j * RW + pr * 128
                qkv_ref[j, pr] = proj[:, c0:c0 + 128]
        pprev = _shifted(p, carry[...], pl.program_id(0) == 0)
        carry[...] = p[tm - 8:, :]
        res = _prep_fn(p, pprev, mu[...], w0[...], w2p[...], a0[...], a2p[...], g2[...], k_k[...], k_a[...])
        for o_ref, val in zip(outs, res):
            o_ref[...] = val

    row = pl.BlockSpec((tm, RW), lambda i: (i, 0))
    return pl.pallas_call(
        body, name="in_proj_prep", grid=(t // tm,),
        in_specs=[pl.BlockSpec((tm, D_MODEL), lambda i: (i, 0)), pl.BlockSpec((1, D_MODEL), lambda i: (0, 0)),
                  pl.BlockSpec((IN_COLS, D_MODEL), lambda i: (0, 0))] + _prep_specs(tm),
        out_specs=[pl.BlockSpec((tm, D_MODEL), lambda i: (i, 0)), pl.BlockSpec((tm, SHIFT_COLS), lambda i: (i, 0)),
                   pl.BlockSpec((3, N_PAIR, tm, 128), lambda i: (0, 0, i, 0))] + [row] * 7,
        out_shape=[jax.ShapeDtypeStruct((t, D_MODEL), BF16), jax.ShapeDtypeStruct((t, SHIFT_COLS), F32),
                   jax.ShapeDtypeStruct((3, N_PAIR, t, 128), F32)] + [jax.ShapeDtypeStruct((t, RW), F32)] * 7,
        scratch_shapes=[pltpu.VMEM((8, SHIFT_COLS), F32)],
        compiler_params=_params(("arbitrary",)),
    )(x, g1, win, *pw)


def _pairs(ref):
    return jnp.stack([ref[:, 128 * p:128 * (p + 1)] for p in range(N_PAIR)], axis=0)


def _wkv_fwd(r, lw, k2, v, kk, a):
    t = r.shape[0]
    nc = t // CHUNK

    def body(r_ref, lw_ref, k_ref, v_ref, kk_ref, a_ref, y_ref, s_ref, st):
        @pl.when(pl.program_id(0) == 0)
        def _():
            st[...] = jnp.zeros_like(st)

        s0 = st[...]
        s_ref[0] = s0
        y, s1 = _wkv_chunk_fn(s0, *[_pairs(ref) for ref in (r_ref, lw_ref, k_ref, v_ref, kk_ref, a_ref)])
        for p in range(N_PAIR):
            y_ref[:, 128 * p:128 * (p + 1)] = y[p]
        st[...] = s1

    blk = pl.BlockSpec((CHUNK, RW), lambda c: (c, 0))
    return pl.pallas_call(
        body, name="wkv_fwd", grid=(nc,),
        in_specs=[blk] * 6,
        out_specs=[blk, pl.BlockSpec((1, N_PAIR, 128, 128), lambda c: (c, 0, 0, 0))],
        out_shape=[jax.ShapeDtypeStruct((t, RW), F32), jax.ShapeDtypeStruct((nc, N_PAIR, 128, 128), F32)],
        scratch_shapes=[pltpu.VMEM((N_PAIR, 128, 128), F32)],
        compiler_params=_params(("arbitrary",)),
    )(r, lw, k2, v, kk, a)


_POST_TM = 512


ATTN_GROUP = 2


def _dilated_rows(d, r, n):
    if d == 1:
        return pl.ds(pl.multiple_of(n * ATTN_BLOCK, ATTN_BLOCK), ATTN_BLOCK)
    return pl.ds(r + n * (ATTN_BLOCK * d), ATTN_BLOCK, stride=d)


def _for_each_sequence(t, unit):
    for di, d in enumerate(DILATIONS):

        @pl.when(pl.program_id(1) == di)
        def _(di=di, d=d):
            nb = t // (ATTN_BLOCK * d)
            if d == 1:
                unit(di, [(d, 0, 0)], False)
                unit(di, [(d, 0, 1)], True)
                lax.fori_loop(1, nb // 2, lambda k, c: (unit(di, [(d, 0, 2 * k), (d, 0, 2 * k + 1)], True), c)[1], 0)
            else:

                def residues(r, carry):
                    unit(di, [(d, r, 0), (d, r + d // 2, 0)], False)
                    if nb > 1:
                        lax.fori_loop(1, nb, lambda n, c: (unit(di, [(d, r, n), (d, r + d // 2, n)], True), c)[1], 0)
                    return carry

                lax.fori_loop(0, d // 2, residues, 0)


def _take(ref, lead, rows_list):
    return jnp.stack([ref.at[(*lead, g)][rows, :] for rows in rows_list for g in range(ref.shape[len(lead)])], axis=0)


def _put(ref, lead, rows_list, val, add=False):
    k = 0
    for rows in rows_list:
        for g in range(ref.shape[len(lead)]):
            if add:
                ref.at[(*lead, g)][rows, :] += val[k]
            else:
                ref.at[(*lead, g)][rows, :] = val[k]
            k += 1


def _attn_fwd(qkv):
    t = qkv.shape[2]

    def body(q_ref, k_ref, v_ref, o_ref, l_ref):
        def unit(di, places, has_prev):
            cur = [_dilated_rows(d, r, n) for d, r, n in places]
            args = [_take(ref, (0,), cur) for ref in (q_ref, k_ref, v_ref)]
            if has_prev:
                prv = [_dilated_rows(d, r, n - 1) for d, r, n in places]
                args += [_take(ref, (0,), prv) for ref in (k_ref, v_ref)]
            o, lse = _attn_block_fn(*args)
            _put(o_ref, (0,), cur, o)
            _put(l_ref, (0,), cur, lse)

        _for_each_sequence(t, unit)

    spec = lambda j: pl.BlockSpec((1, ATTN_GROUP, t, 128), lambda i, b: (j, i, 0, 0))
    out = pl.BlockSpec((1, ATTN_GROUP, t, 128), lambda i, b: (b, i, 0, 0))
    return pl.pallas_call(
        body, name="attn_fwd", grid=(N_PAIR // ATTN_GROUP, len(DILATIONS)),
        in_specs=[spec(0), spec(1), spec(2)], out_specs=[out, out],
        out_shape=[jax.ShapeDtypeStruct((3, N_PAIR, t, 128), F32)] * 2,
        compiler_params=_params(("parallel", "arbitrary")),
    )(qkv, qkv, qkv)


_COMB_TM = 512


def _mixers_out(y, r, k2, v, g, lnw, lnb, rk, o, l, og):
    t = y.shape[0]
    tm = _COMB_TM

    def body(y_ref, r_ref, k_ref, v_ref, g_ref, lnw_ref, lnb_ref, rk_ref, o_ref, l_ref, og_ref, out_ref):
        out_ref[:, :RW] = _post_fn(y_ref[...], r_ref[...], k_ref[...], v_ref[...], g_ref[...],
                                   lnw_ref[...], lnb_ref[...], rk_ref[...]).astype(BF16)
        for p in range(N_PAIR):
            cols = slice(128 * p, 128 * (p + 1))
            out_ref[:, RW + 128 * p:RW + 128 * (p + 1)] = _combine_fn(
                o_ref[0, p], o_ref[1, p], o_ref[2, p], l_ref[0, p], l_ref[1, p], l_ref[2, p], og_ref[:, cols]).astype(BF16)

    row = pl.BlockSpec((tm, RW), lambda i: (i, 0))
    vec = pl.BlockSpec((1, RW), lambda i: (0, 0))
    blk = pl.BlockSpec((3, N_PAIR, tm, 128), lambda i: (0, 0, i, 0))
    return pl.pallas_call(
        body, name="mixers_out", grid=(t // tm,),
        in_specs=[row] * 5 + [vec] * 3 + [blk, blk, vec], out_specs=pl.BlockSpec((tm, D_MODEL), lambda i: (i, 0)),
        out_shape=jax.ShapeDtypeStruct((t, D_MODEL), BF16),
        compiler_params=_params(("parallel",)),
    )(y, r, k2, v, g, lnw, lnb, rk, o, l, og)


def _ffn_all(x, ycat, wg, wu, wd, wout, g2, gf, tgt):
    t = x.shape[0]
    tm = 256

    def body(x_ref, y_ref, wg_ref, wu_ref, wd_ref, wo_ref, g2_ref, gf_ref, t_ref,
             h_ref, act_ref, dx2b_ref, dgt_ref, dup_ref, dx1b_ref, dx1_ref, dya_ref, dyb_ref, loss_ref, dgf_ref, dg2_ref,
             gt_s, up_s):
        first = pl.program_id(0) == 0
        x1 = x_ref[...] + _dot(y_ref[...], wo_ref[...])
        h = _rms_fwd(x1, g2_ref[...]).astype(BF16)
        h_ref[...] = h
        for c0 in range(0, D_FF, FF_CHUNK):
            cols = slice(c0, c0 + FF_CHUNK)
            gt = _dot_nt(h, wg_ref[cols, :])
            up = _dot_nt(h, wu_ref[cols, :])
            gt_s[:, cols] = gt.astype(BF16)
            up_s[:, cols] = up.astype(BF16)
            act_ref[:, cols] = (gt * _sigmoid(gt) * up).astype(BF16)
        x2 = x1 + _dot(act_ref[...], wd_ref[...])
        gf_ = gf_ref[...]
        diff = _rms_fwd(x2, gf_) - t_ref[...]
        lrow = 0.5 * jnp.sum(_colsum8(diff * diff), axis=1, keepdims=True) * (1.0 / D_MODEL)
        _acc(loss_ref, jnp.broadcast_to(lrow, (8, 128)), first)
        dx2, dgr = _rms_bwd(diff * (1.0 / D_MODEL), x2, gf_)
        _acc(dgf_ref, _colsum8(dgr), first)
        dx2b = dx2.astype(BF16)
        dx2b_ref[...] = dx2b
        for c0 in range(0, D_FF, FF_CHUNK):
            cols = slice(c0, c0 + FF_CHUNK)
            dact = _dot_nt(dx2b, wd_ref[cols, :])
            gt = gt_s[:, cols].astype(F32)
            sg = _sigmoid(gt)
            dgt_ref[:, cols] = (dact * up_s[:, cols].astype(F32) * sg * (1.0 + gt * (1.0 - sg))).astype(BF16)
            dup_ref[:, cols] = (dact * gt * sg).astype(BF16)
        dh = _dot(dgt_ref[...], wg_ref[...]) + _dot(dup_ref[...], wu_ref[...])
        dxn, dgr2 = _rms_bwd(dh, x1, g2_ref[...])
        _acc(dg2_ref, _colsum8(dgr2), first)
        dx1 = dx2 + dxn
        dx1_ref[...] = dx1
        dx1b = dx1.astype(BF16)
        dx1b_ref[...] = dx1b
        dy = _dot_nt(dx1b, wo_ref[...])
        dya_ref[...] = dy[:, :RW]
        dyb_ref[...] = dy[:, RW:]

    row = pl.BlockSpec((tm, D_MODEL), lambda i: (i, 0))
    wide = pl.BlockSpec((tm, D_FF), lambda i: (i, 0))
    half = pl.BlockSpec((tm, RW), lambda i: (i, 0))
    wsp = pl.BlockSpec((D_FF, D_MODEL), lambda i: (0, 0))
    vec = pl.BlockSpec((1, D_MODEL), lambda i: (0, 0))
    part = pl.BlockSpec((8, D_MODEL), lambda i: (0, 0))
    bf = lambda n: jax.ShapeDtypeStruct((t, n), BF16)
    return pl.pallas_call(
        body, name="ffn_all", grid=(t // tm,),
        in_specs=[row, row, wsp, wsp, wsp, pl.BlockSpec((D_MODEL, D_MODEL), lambda i: (0, 0)), vec, vec, row],
        out_specs=[row, wide, row, wide, wide, row, row, half, half, pl.BlockSpec((8, 128), lambda i: (0, 0)), part, part],
        out_shape=[bf(D_MODEL), bf(D_FF), bf(D_MODEL), bf(D_FF), bf(D_FF), bf(D_MODEL),
                   jax.ShapeDtypeStruct((t, D_MODEL), F32), jax.ShapeDtypeStruct((t, RW), F32),
                   jax.ShapeDtypeStruct((t, RW), F32), jax.ShapeDtypeStruct((8, 128), F32),
                   jax.ShapeDtypeStruct((8, D_MODEL), F32), jax.ShapeDtypeStruct((8, D_MODEL), F32)],
        scratch_shapes=[pltpu.VMEM((tm, D_FF), BF16), pltpu.VMEM((tm, D_FF), BF16)],
        compiler_params=_params(("arbitrary",)),
    )(x, ycat, wg, wu, wd, wout, g2, gf, tgt)


def _wgrad(a, b, tk, tn, name):
    t, kdim = a.shape
    ndim = b.shape[1]

    def body(a_ref, b_ref, o_ref):
        o_ref[...] = _dot_tn(a_ref[...], b_ref[...]).astype(BF16)

    return pl.pallas_call(
        body, name=name, grid=(kdim // tk, ndim // tn),
        in_specs=[pl.BlockSpec((t, tk), lambda i, j: (0, i)), pl.BlockSpec((t, tn), lambda i, j: (0, j))],
        out_specs=pl.BlockSpec((tk, tn), lambda i, j: (i, j)),
        out_shape=jax.ShapeDtypeStruct((kdim, ndim), BF16),
        compiler_params=_params(("parallel", "parallel")),
    )(a, b)


def _post_bwd(dya, y, r, k2, v, g, lnw, lnb, rk, after):
    t = y.shape[0]
    tm = _POST_TM

    def body(d_ref, y_ref, r_ref, k_ref, v_ref, g_ref, lnw_ref, lnb_ref, rk_ref, _,
             dy_ref, dr_ref, dk_ref, dv_ref, dg_ref, dlnw_ref, dlnb_ref, drk_ref):
        first = pl.program_id(0) == 0
        ones = jnp.ones((tm, 1), F32)
        prim = (y_ref[...], r_ref[...], k_ref[...], v_ref[...], g_ref[...],
                ones * lnw_ref[...], ones * lnb_ref[...], ones * rk_ref[...])
        _, vjp = jax.vjp(_post_fn, *prim)
        dy, dr, dk, dv, dg, dlnw, dlnb, drk = vjp(d_ref[...])
        dy_ref[...] = dy
        dr_ref[...] = dr
        dk_ref[...] = dk
        dv_ref[...] = dv
        dg_ref[...] = dg
        _acc(dlnw_ref, _colsum8(dlnw), first)
        _acc(dlnb_ref, _colsum8(dlnb), first)
        _acc(drk_ref, _colsum8(drk), first)

    row = pl.BlockSpec((tm, RW), lambda i: (i, 0))
    vec = pl.BlockSpec((1, RW), lambda i: (0, 0))
    part = pl.BlockSpec((8, RW), lambda i: (0, 0))
    return pl.pallas_call(
        body, name="rwkv_post_bwd", grid=(t // tm,),
        in_specs=[row] * 6 + [vec] * 3 + [ANY], out_specs=[row] * 5 + [part] * 3,
        out_shape=[jax.ShapeDtypeStruct((t, RW), F32)] * 5 + [jax.ShapeDtypeStruct((8, RW), F32)] * 3,
        compiler_params=_params(("arbitrary",)),
    )(dya, y, r, k2, v, g, lnw, lnb, rk, after)


def _wkv_bwd(dy, s0s, r, lw, k2, v, kk, a):
    t = r.shape[0]
    nc = t // CHUNK

    def body(dy_ref, s_ref, r_ref, lw_ref, k_ref, v_ref, kk_ref, a_ref,
             dr_ref, dlw_ref, dk_ref, dv_ref, dkk_ref, da_ref, ds):
        @pl.when(pl.program_id(0) == 0)
        def _():
            ds[...] = jnp.zeros_like(ds)

        _, vjp = jax.vjp(_wkv_chunk_fn, s_ref[0],
                         *[_pairs(ref) for ref in (r_ref, lw_ref, k_ref, v_ref, kk_ref, a_ref)])
        res = vjp((_pairs(dy_ref), ds[...]))
        ds[...] = res[0]
        for ref, val in zip((dr_ref, dlw_ref, dk_ref, dv_ref, dkk_ref, da_ref), res[1:]):
            for p in range(N_PAIR):
                ref[:, 128 * p:128 * (p + 1)] = val[p]

    blk = pl.BlockSpec((CHUNK, RW), lambda c: (nc - 1 - c, 0))
    return pl.pallas_call(
        body, name="wkv_bwd", grid=(nc,),
        in_specs=[blk, pl.BlockSpec((1, N_PAIR, 128, 128), lambda c: (nc - 1 - c, 0, 0, 0))] + [blk] * 6,
        out_specs=[blk] * 6,
        out_shape=[jax.ShapeDtypeStruct((t, RW), F32)] * 6,
        scratch_shapes=[pltpu.VMEM((N_PAIR, 128, 128), F32)],
        compiler_params=_params(("arbitrary",)),
    )(dy, s0s, r, lw, k2, v, kk, a)


def _prep_in_proj_bwd(proj, pw, douts, dq, dk, dv, win, x, g1, dx1):
    t = proj.shape[0]
    tm = _PREP_TM
    nt = t // tm

    def body(p_ref, l8_ref, mu, w0, w2p, a0, a2p, g2, k_k, k_a, dr, dr2, dlw, dk2, dk22, dv, dv2, dkk, da, dg,
             dq_ref, dkq_ref, dvq_ref, w_ref, x_ref, g1_ref, dx1_ref,
             dproj_ref, dx_ref, dg1_ref, dmu_ref, dw0_ref, dw2_ref, da0_ref, da2_ref, dg2_ref, dkk_ref, dka_ref, carry):
        i = pl.program_id(0)
        first = i == 0

        @pl.when(first)
        def _():
            carry[...] = jnp.zeros_like(carry)

        p = p_ref[...]
        pprev = _shifted(p, l8_ref[...], i == nt - 1)
        ones = jnp.ones((tm, 1), F32)
        prim = (p, pprev, ones * mu[...], ones * w0[...], w2p[...], ones * a0[...], a2p[...], g2[...],
                ones * k_k[...], ones * k_a[...])
        _, vjp = jax.vjp(_prep_fn, *prim)
        dp, dpp, dmu, dw0, dw2, da0, da2, dg2, dkk_, dka = vjp(
            (dr[...] + dr2[...], dlw[...], dk2[...] + dk22[...], dv[...] + dv2[...], dkk[...], da[...], dg[...]))
        up = pltpu.roll(dpp, tm - 1, axis=0)
        rid = lax.broadcasted_iota(jnp.int32, dpp.shape, 0)
        dpa = dp + jnp.where(rid == tm - 1, carry[0:1, :], up)
        carry[...] = jnp.broadcast_to(dpp[0:1, :], carry.shape)
        _acc(dmu_ref, _colsum8(dmu), first)
        _acc(dw0_ref, _colsum8(dw0), first)
        _acc(dw2_ref, dw2, first)
        _acc(da0_ref, _colsum8(da0), first)
        _acc(da2_ref, da2, first)
        _acc(dg2_ref, dg2, first)
        _acc(dkk_ref, _colsum8(dkk_), first)
        _acc(dka_ref, _colsum8(dka), first)
        parts = [dpa] + [ref[pr] for ref in (dq_ref, dkq_ref, dvq_ref) for pr in range(N_PAIR)]
        dproj = jnp.concatenate([z.astype(BF16) for z in parts], axis=1)
        dproj_ref[...] = dproj
        dxn, dgr = _rms_bwd(_dot(dproj, w_ref[...]), x_ref[...], g1_ref[...])
        dx_ref[...] = dx1_ref[...] + dxn
        _acc(dg1_ref, _colsum8(dgr), first)

    rev = lambda i: (nt - 1 - i, 0)
    row = pl.BlockSpec((tm, RW), rev)
    wide = pl.BlockSpec((tm, D_MODEL), rev)
    pair = pl.BlockSpec((N_PAIR, tm, 128), lambda i: (0, nt - 1 - i, 0))
    part = lambda n: pl.BlockSpec((8, n), lambda i: (0, 0))
    mat = pl.BlockSpec((128, RW), lambda i: (0, 0))
    return pl.pallas_call(
        body, name="prep_in_proj_bwd", grid=(nt,),
        in_specs=[pl.BlockSpec((tm, SHIFT_COLS), rev),
                  pl.BlockSpec((8, SHIFT_COLS), lambda i: (jnp.maximum((nt - 1 - i) * (tm // 8) - 1, 0), 0))]
                 + _prep_specs(tm) + [row] * 10
                 + [pair] * 3 + [pl.BlockSpec((IN_COLS, D_MODEL), lambda i: (0, 0)), wide,
                                 pl.BlockSpec((1, D_MODEL), lambda i: (0, 0)), wide],
        out_specs=[pl.BlockSpec((tm, IN_COLS), rev), wide, part(D_MODEL), part(SHIFT_COLS), part(RW), mat, part(RW), mat,
                   mat, part(RW), part(RW)],
        out_shape=[jax.ShapeDtypeStruct((t, IN_COLS), BF16), jax.ShapeDtypeStruct((t, D_MODEL), F32),
                   jax.ShapeDtypeStruct((8, D_MODEL), F32), jax.ShapeDtypeStruct((8, SHIFT_COLS), F32),
                   jax.ShapeDtypeStruct((8, RW), F32), jax.ShapeDtypeStruct((128, RW), F32),
                   jax.ShapeDtypeStruct((8, RW), F32), jax.ShapeDtypeStruct((128, RW), F32),
                   jax.ShapeDtypeStruct((128, RW), F32), jax.ShapeDtypeStruct((8, RW), F32),
                   jax.ShapeDtypeStruct((8, RW), F32)],
        scratch_shapes=[pltpu.VMEM((8, SHIFT_COLS), F32)],
        compiler_params=_params(("arbitrary",)),
    )(proj, proj, *pw, *douts, dq, dk, dv, win, x, g1, dx1)


def _combine_bwd(dyb, o, l, og):
    t = dyb.shape[0]
    tm = _COMB_TM

    def body(d_ref, o_ref, l_ref, og_ref, do_ref, dl_ref, dog_ref):
        ones = jnp.ones((tm, 1), F32)
        dog = []
        for p in range(N_PAIR):
            cols = slice(128 * p, 128 * (p + 1))
            _, vjp = jax.vjp(_combine_fn, o_ref[0, p], o_ref[1, p], o_ref[2, p], l_ref[0, p], l_ref[1, p], l_ref[2, p],
                             ones * og_ref[:, cols])
            res = vjp(d_ref[:, cols])
            for b in range(3):
                do_ref[b, p] = res[b]
                dl_ref[b, p] = res[3 + b]
            dog.append(_colsum8(res[6]))
        _acc(dog_ref, jnp.concatenate(dog, axis=1), pl.program_id(0) == 0)

    blk = pl.BlockSpec((3, N_PAIR, tm, 128), lambda i: (0, 0, i, 0))
    return pl.pallas_call(
        body, name="attn_combine_bwd", grid=(t // tm,),
        in_specs=[pl.BlockSpec((tm, RW), lambda i: (i, 0)), blk, blk, pl.BlockSpec((1, RW), lambda i: (0, 0))],
        out_specs=[blk, blk, pl.BlockSpec((8, RW), lambda i: (0, 0))],
        out_shape=[jax.ShapeDtypeStruct((3, N_PAIR, t, 128), F32)] * 2 + [jax.ShapeDtypeStruct((8, RW), F32)],
        compiler_params=_params(("arbitrary",)),
    )(dyb, o, l, og)


def _attn_bwd(do, dl, o, lse, qkv):
    t = qkv.shape[2]

    def body(do_ref, dl_ref, o_ref, l_ref, q_ref, k_ref, v_ref, dq_ref, dk_ref, dv_ref):
        @pl.when(pl.program_id(1) == 0)
        def _():
            for ref in (dq_ref, dk_ref, dv_ref):
                ref[...] = jnp.zeros_like(ref)

        def unit(di, places, has_prev):
            cur = [_dilated_rows(d, r, n) for d, r, n in places]
            q, kc, vc = [_take(ref, (0,), cur) for ref in (q_ref, k_ref, v_ref)]
            kp = vp = None
            if has_prev:
                prv = [_dilated_rows(d, r, n - 1) for d, r, n in places]
                kp, vp = [_take(ref, (0,), prv) for ref in (k_ref, v_ref)]
            res = _attn_block_bwd(q, kc, vc, kp, vp, *[_take(ref, (0,), cur) for ref in (o_ref, l_ref, do_ref, dl_ref)])
            _put(dq_ref, (), cur, res[0], add=True)
            _put(dk_ref, (), cur, res[1], add=True)
            _put(dv_ref, (), cur, res[2], add=True)
            if has_prev:
                _put(dk_ref, (), prv, res[3], add=True)
                _put(dv_ref, (), prv, res[4], add=True)

        _for_each_sequence(t, unit)

    spec = lambda j: pl.BlockSpec((1, ATTN_GROUP, t, 128), lambda i, b: (j, i, 0, 0))
    branch = pl.BlockSpec((1, ATTN_GROUP, t, 128), lambda i, b: (b, i, 0, 0))
    out = pl.BlockSpec((ATTN_GROUP, t, 128), lambda i, b: (i, 0, 0))
    return pl.pallas_call(
        body, name="attn_bwd", grid=(N_PAIR // ATTN_GROUP, len(DILATIONS)),
        in_specs=[branch] * 4 + [spec(0), spec(1), spec(2)], out_specs=[out] * 3,
        out_shape=[jax.ShapeDtypeStruct((N_PAIR, t, 128), F32)] * 3,
        compiler_params=_params(("parallel", "arbitrary")),
    )(do, dl, o, lse, qkv, qkv, qkv)


def _unstack_lora(lora_all):
    def body(l_ref, w_ref, a_ref, g_ref):
        z = jnp.zeros((64, 128), F32)
        for p in range(N_CHIP):
            cols = slice(128 * p, 128 * (p + 1))
            w_ref[0:64, cols] = l_ref[p, 0:64, :]
            w_ref[64:128, cols] = z
            a_ref[0:64, cols] = z
            a_ref[64:128, cols] = l_ref[p, 64:128, :]
            g_ref[:, cols] = l_ref[p, 128:256, :]

    return pl.pallas_call(body, name="unstack_lora", out_shape=[jax.ShapeDtypeStruct((128, RW), F32)] * 3)(lora_all)


def _local_step(x, tgt, win, vecs, w2p, a2p, g2m, get_rest, send_rest):
    pw = (vecs["mu_shift"], vecs["decay_w0"], w2p, vecs["iclr_a0"], a2p, g2m, vecs["k_k"], vecs["k_a"])
    h, proj, qkv, r, lw, k2, v, kk, a, g = _in_proj_prep(x, vecs["mix_norm_g"], win, pw)
    y, s0s = _wkv_fwd(r, lw, k2, v, kk, a)
    o_att, l_att = _attn_fwd(qkv)
    ycat = _mixers_out(y, r, k2, v, g, vecs["ln_x_w"], vecs["ln_x_b"], vecs["r_k"], o_att, l_att, vecs["attn_out_g"])
    wout, wg, wu, wd = get_rest(ycat)
    h2, act, dx2b, dgt, dup, dx1b, dx1, dya, dyb, loss8, dgf, dg2n = _ffn_all(
        x, ycat, wg, wu, wd, wout, vecs["ffn_norm_g"], vecs["final_norm_g"], tgt)
    gw = {
        "w_down": _wgrad(act, dx2b, 1408, 1024, "wgrad_down"),
        "w_gate": _wgrad(dgt, h2, 1408, 1024, "wgrad_gate"),
        "w_up": _wgrad(dup, h2, 1408, 1024, "wgrad_up"),
        "w_out": _wgrad(ycat, dx1b, 1024, 1024, "wgrad_out"),
    }

    dy, dr_p, dk2_p, dv_p, dg, dlnw, dlnb, drk = _post_bwd(dya, y, r, k2, v, g, vecs["ln_x_w"], vecs["ln_x_b"], vecs["r_k"],
                                                           after=send_rest(gw))
    dr_s, dlw, dk2_s, dv_s, dkk, da = _wkv_bwd(dy, s0s, r, lw, k2, v, kk, a)
    do_att, dl_att, dog = _combine_bwd(dyb, o_att, l_att, vecs["attn_out_g"])
    dq, dk, dv = _attn_bwd(do_att, dl_att, o_att, l_att, qkv)
    dproj, dx, dg1, dmu, dw0, dw2p, da0, da2p, dg2m, dk_k, dk_a = _prep_in_proj_bwd(
        proj, pw, (dr_p, dr_s, dlw, dk2_p, dk2_s, dv_p, dv_s, dkk, da, dg), dq, dk, dv, win, x, vecs["mix_norm_g"], dx1)
    gw["w_in"] = _wgrad(dproj, h, 1664, 1024, "wgrad_in")
    gw["decay_w2"] = dw2p[:64]
    gw["iclr_a2"] = da2p[64:]
    gw["gate_g2"] = dg2m
    gv = {"mix_norm_g": dg1, "mu_shift": dmu, "decay_w0": dw0, "iclr_a0": da0, "k_k": dk_k, "k_a": dk_a, "r_k": drk,
          "ln_x_w": dlnw, "ln_x_b": dlnb, "attn_out_g": dog, "ffn_norm_g": dg2n, "final_norm_g": dgf}
    return loss8, dx, gw, gv


N_CHIP = 4
N_DEV = 8
MATS = ("w_in", "w_out", "w_gate", "w_up", "w_down")
LORAS = ("decay_w2", "iclr_a2", "gate_g2")
VECS = (("mix_norm_g", 1024), ("mu_shift", 1792), ("decay_w0", 512), ("iclr_a0", 512), ("k_k", 512), ("k_a", 512),
        ("r_k", 512), ("ln_x_w", 512), ("ln_x_b", 512), ("attn_out_g", 512), ("ffn_norm_g", 1024),
        ("final_norm_g", 1024))
N_VEC = sum(n for _, n in VECS)
N_SMALL = N_VEC + 128
ANY = pl.BlockSpec(memory_space=pl.ANY)


def _flip(v, f):
    return 1 - v if f else v


class _Me:
    def __init__(self, mode):
        x, y, c = lax.axis_index("x"), lax.axis_index("y"), lax.axis_index("c")
        self.core, self.chip, self.dev = c, 2 * x + y, 4 * x + 2 * y + c
        self.sibling = (x, y, 1 - c)
        if mode == "chips":
            self.peers = [(px, py, c) for px, py in ((1 - x, y), (x, 1 - y), (1 - x, 1 - y))]
        else:
            self.peers = [(_flip(x, k & 4), _flip(y, k & 2), _flip(c, k & 1)) for k in range(1, N_DEV)]


def _half(core, rows):
    h = rows // 2
    return pl.ds(pl.multiple_of(core * h, h), h)


_BY_CHIP = ("gather", "whole", "chipsum")


def _peer_copy(srcs, dsts, kinds, send_sems, recv_sems, me, j, i, incoming):
    px, py, pc = me.peers[j]
    pchip, pdev = 2 * px + py, 4 * px + 2 * py + pc
    src, dst, kind = srcs[i], dsts[i], kinds[i]
    if kind in ("gather", "whole"):
        rows = _half(me.core, src.shape[1]) if kind == "gather" else pl.ds(0, src.shape[1])
        src, dst = src.at[me.chip, rows], dst.at[pchip if incoming else me.chip, rows]
    elif kind == "scatter":
        src, dst = src.at[pchip, _half(pc, src.shape[1])], dst.at[pdev if incoming else me.dev]
    elif kind == "chipsum":
        src, dst = src.at[pchip], dst.at[pchip if incoming else me.chip]
    else:
        dst = dst.at[pdev if incoming else me.dev]
    n = len(srcs)
    return pltpu.make_async_remote_copy(src_ref=src, dst_ref=dst, send_sem=send_sems.at[n * j + i],
                                        recv_sem=recv_sems.at[n * j + i], device_id=(px, py, pc), device_id_type=MESH)


def _mode(kinds):
    return "chips" if kinds[0] in _BY_CHIP else "devs"


def _npeer(kinds):
    return N_CHIP - 1 if kinds[0] in _BY_CHIP else N_DEV - 1


def _sibling_halves(gs, name):
    n = len(gs)

    def body(*refs):
        srcs, dsts, send_sems, recv_sems = refs[:n], refs[n:2 * n], refs[2 * n], refs[2 * n + 1]
        me = _Me("chips")

        def copy(i, p):
            return pltpu.make_async_remote_copy(
                src_ref=srcs[i].at[p, _half(1 - me.core, srcs[i].shape[1])], dst_ref=dsts[i].at[p],
                send_sem=send_sems.at[N_CHIP * i + p], recv_sem=recv_sems.at[N_CHIP * i + p],
                device_id=me.sibling, device_id_type=MESH)

        copies = [copy(i, p) for i in range(n) for p in range(N_CHIP)]
        for cp in copies:
            cp.start()
        for cp in copies:
            cp.wait()

    return pl.pallas_call(
        body, name=name, in_specs=[ANY] * n, out_specs=[ANY] * n,
        out_shape=[jax.ShapeDtypeStruct((N_CHIP, g.shape[1] // 2, g.shape[2]), g.dtype) for g in gs],
        scratch_shapes=[pltpu.SemaphoreType.DMA((N_CHIP * n,)), pltpu.SemaphoreType.DMA((N_CHIP * n,))],
    )(*gs)


def _add_halves(g, other, core, tr, name):
    _, h, cols = other.shape

    def body(core_ref, g_ref, o_ref, out_ref):
        out_ref[...] = (g_ref[...].astype(F32) + o_ref[...].astype(F32)).astype(BF16)

    blk = lambda off: pl.BlockSpec((1, tr, cols), lambda p, i, core_ref: (p, core_ref[0] * (h // tr) * off + i, 0))
    return pl.pallas_call(
        body, name=name,
        grid_spec=pltpu.PrefetchScalarGridSpec(num_scalar_prefetch=1, grid=(N_CHIP, h // tr),
                                               in_specs=[blk(1), blk(0)], out_specs=blk(0)),
        out_shape=jax.ShapeDtypeStruct(other.shape, BF16),
        compiler_params=_params(("parallel", "parallel")),
    )(core, g, other)


def _swap_gathered(lands, name):
    n = len(lands)

    def body(*refs):
        dsts, send_sems, recv_sems = refs[n:2 * n], refs[2 * n], refs[2 * n + 1]
        me = _Me("chips")

        def copy(j, i, incoming):
            px, py, _ = me.peers[j]
            rows_out, rows_in = _half(me.core, dsts[i].shape[1]), _half(1 - me.core, dsts[i].shape[1])
            return pltpu.make_async_remote_copy(
                src_ref=dsts[i].at[2 * px + py, rows_out], dst_ref=dsts[i].at[2 * px + py, rows_in if incoming else rows_out],
                send_sem=send_sems.at[n * j + i], recv_sem=recv_sems.at[n * j + i], device_id=me.sibling, device_id_type=MESH)

        sends = [copy(j, i, False) for j in range(3) for i in range(n)]
        for cp in sends:
            cp.start()
        for j in range(3):
            for i in range(n):
                copy(j, i, True).wait_recv()
        for cp in sends:
            cp.wait_send()

    return pl.pallas_call(
        body, name=name, in_specs=[ANY] * n, out_specs=[ANY] * n,
        out_shape=[jax.ShapeDtypeStruct(l.shape, l.dtype) for l in lands],
        input_output_aliases={i: i for i in range(n)},
        scratch_shapes=[pltpu.SemaphoreType.DMA((3 * n,)), pltpu.SemaphoreType.DMA((3 * n,))],
    )(*lands)


def _join_halves(sums, name):
    n = len(sums)

    def body(*refs):
        dsts, send_sems, recv_sems = refs[n:2 * n], refs[2 * n], refs[2 * n + 1]
        me = _Me("chips")

        def copy(i, incoming):
            mine, other = _half(me.core, dsts[i].shape[0]), _half(1 - me.core, dsts[i].shape[0])
            return pltpu.make_async_remote_copy(src_ref=dsts[i].at[mine], dst_ref=dsts[i].at[other if incoming else mine],
                                                send_sem=send_sems.at[i], recv_sem=recv_sems.at[i],
                                                device_id=me.sibling, device_id_type=MESH)

        sends = [copy(i, False) for i in range(n)]
        for cp in sends:
            cp.start()
        for i in range(n):
            copy(i, True).wait_recv()
        for cp in sends:
            cp.wait_send()

    return pl.pallas_call(
        body, name=name, in_specs=[ANY] * n, out_specs=[ANY] * n,
        out_shape=[jax.ShapeDtypeStruct(s.shape, s.dtype) for s in sums],
        input_output_aliases={i: i for i in range(n)},
        scratch_shapes=[pltpu.SemaphoreType.DMA((n,)), pltpu.SemaphoreType.DMA((n,))],
    )(*sums)


HBM = pl.BlockSpec(memory_space=pltpu.HBM)
SEM = pl.BlockSpec(memory_space=pltpu.SEMAPHORE)
EFFECT = pltpu.SideEffectType.DATAFLOW_SIDE_EFFECTING


def _swap_start(arrs, lands, kinds, name):
    n = len(lands)
    ops = list(lands) if arrs is None else [*arrs, *lands]
    k = len(ops)

    def body(*refs):
        srcs, dsts, send_sems, recv_sems, token = refs[:n], refs[k - n:k], refs[k], refs[k + 1], refs[-1]
        me = _Me(_mode(kinds))
        for j in range(len(me.peers)):
            for i in range(n):
                _peer_copy(srcs, dsts, kinds, send_sems, recv_sems, me, j, i, False).start()
        token[...] = jnp.zeros_like(token)

    ns = _npeer(kinds) * n
    outs = pl.pallas_call(
        body, name=name,
        out_shape=(pltpu.SemaphoreType.DMA((ns,)), pltpu.SemaphoreType.DMA((ns,)),
                   *[pltpu.HBM(a.shape, a.dtype) for a in ops], jax.ShapeDtypeStruct((8, 128), F32)),
        in_specs=[HBM] * k, out_specs=(SEM, SEM, *[HBM] * k, pl.BlockSpec(memory_space=pltpu.VMEM)),
        input_output_aliases={i: 2 + i for i in range(k)},
        compiler_params=pltpu.CompilerParams(has_side_effects=EFFECT),
    )(*[pltpu.with_memory_space_constraint(a, pltpu.HBM) for a in ops])
    return outs[0], outs[1], outs[2:2 + k - n], outs[2 + k - n:2 + k], outs[-1]


def _swap_wait(send_sems, recv_sems, srcs_thru, lands_thru, after, kinds, name):
    n = len(lands_thru)
    ops = [*srcs_thru, *lands_thru]
    k = len(ops)

    def body(*refs):
        srcs, dsts, s_sems, r_sems = refs[:n], refs[k - n:k], refs[k], refs[k + 1]
        me = _Me(_mode(kinds))
        for j in range(len(me.peers)):
            for i in range(n):
                cp = _peer_copy(srcs, dsts, kinds, s_sems, r_sems, me, j, i, True)
                cp.wait_send()
                cp.wait_recv()

    outs = pl.pallas_call(
        body, name=name,
        out_shape=tuple(pltpu.HBM(a.shape, a.dtype) for a in ops),
        in_specs=[HBM] * k + [SEM, SEM, ANY], out_specs=tuple([HBM] * k),
        input_output_aliases={i: i for i in range(k)},
        compiler_params=pltpu.CompilerParams(has_side_effects=EFFECT),
    )(*ops, send_sems, recv_sems, after)
    return outs[k - n:]


def _wait_and_swap(send_sems, recv_sems, lands_thru, after, kinds, name):
    n = len(lands_thru)

    pairs = [(j, i) for j in range(3) for i in range(n)]

    def swap(dsts, d_send, d_recv, me, j, i, incoming):
        px, py, _ = me.peers[j]
        rows_out, rows_in = _half(me.core, dsts[i].shape[1]), _half(1 - me.core, dsts[i].shape[1])
        return pltpu.make_async_remote_copy(
            src_ref=dsts[i].at[2 * px + py, rows_out], dst_ref=dsts[i].at[2 * px + py, rows_in if incoming else rows_out],
            send_sem=d_send.at[n * j + i], recv_sem=d_recv.at[n * j + i], device_id=me.sibling, device_id_type=MESH)

    def arrive_and_start(*refs):
        dsts, s_sems, r_sems, d_send, d_recv = refs[:n], refs[n], refs[n + 1], refs[n + 3], refs[n + 4]
        me = _Me("chips")
        for j in range(3):
            for i in range(n):
                cp = _peer_copy(dsts, dsts, kinds, s_sems, r_sems, me, j, i, True)
                cp.wait_recv()
                cp.wait_send()
            for i in range(n):
                swap(dsts, d_send, d_recv, me, j, i, False).start()

    outs = pl.pallas_call(
        arrive_and_start, name=name + "_start",
        out_shape=(pltpu.SemaphoreType.DMA((3 * n,)), pltpu.SemaphoreType.DMA((3 * n,)),
                   *[pltpu.HBM(a.shape, a.dtype) for a in lands_thru]),
        in_specs=[HBM] * n + [SEM, SEM, ANY], out_specs=(SEM, SEM, *[HBM] * n),
        input_output_aliases={i: 2 + i for i in range(n)},
        compiler_params=pltpu.CompilerParams(has_side_effects=EFFECT),
    )(*lands_thru, send_sems, recv_sems, after)

    def swapped(*refs):
        dsts, d_send, d_recv = refs[:n], refs[n], refs[n + 1]
        me = _Me("chips")
        for j, i in pairs:
            cp = swap(dsts, d_send, d_recv, me, j, i, True)
            cp.wait_recv()
            cp.wait_send()

    return list(pl.pallas_call(
        swapped, name=name + "_wait",
        out_shape=tuple(pltpu.HBM(a.shape, a.dtype) for a in lands_thru),
        in_specs=[HBM] * n + [SEM, SEM], out_specs=tuple([HBM] * n),
        input_output_aliases={i: i for i in range(n)},
        compiler_params=pltpu.CompilerParams(has_side_effects=EFFECT),
    )(*outs[2:], outs[0], outs[1]))


def _adamw(w, g, m, v):
    m = ADAM_B1 * m + (1.0 - ADAM_B1) * g
    v = ADAM_B2 * v + (1.0 - ADAM_B2) * (g * g)
    m_hat = m / (1.0 - ADAM_B1 ** ADAM_STEP)
    v_hat = v / (1.0 - ADAM_B2 ** ADAM_STEP)
    delta = -ADAM_LR * (m_hat / (jnp.sqrt(v_hat) + ADAM_EPS) + ADAM_WD * w)
    return delta, m, v


def _reduce8(rbuf, core, tr, name):
    slots, h, cols = rbuf.shape

    def body(core_ref, r_ref, g_ref):
        g = r_ref[0].astype(F32)
        for s in range(1, slots):
            g = g + r_ref[s].astype(F32)
        g_ref[...] = g

    return pl.pallas_call(
        body, name=name,
        grid_spec=pltpu.PrefetchScalarGridSpec(
            num_scalar_prefetch=1, grid=(h // tr,),
            in_specs=[pl.BlockSpec((slots, tr, cols), lambda i, core_ref: (0, i, 0))],
            out_specs=pl.BlockSpec((tr, cols), lambda i, core_ref: (core_ref[0] * (h // tr) + i, 0))),
        out_shape=jax.ShapeDtypeStruct((2 * h, cols), F32),
        compiler_params=_params(("parallel",)),
    )(core, rbuf)


def _adamw_call(g, w, m, v, tr, name):
    _, rows, cols = w.shape

    def body(g_in, w_ref, m_ref, v_ref, g_ref, d_ref, nm_ref, nv_ref):
        g = g_in[...]
        g_ref[0] = g
        d_ref[0], nm_ref[0], nv_ref[0] = _adamw(w_ref[0], g, m_ref[0], v_ref[0])

    row = pl.BlockSpec((1, tr, cols), lambda i: (0, i, 0))
    return pl.pallas_call(
        body, name=name, grid=(rows // tr,),
        in_specs=[pl.BlockSpec((tr, cols), lambda i: (i, 0)), row, row, row], out_specs=[row] * 4,
        out_shape=[jax.ShapeDtypeStruct(w.shape, F32)] * 4,
        compiler_params=_params(("parallel",)),
    )(g, w, m, v)


def _adamw_lora(g, ws3, ms3, vs3):
    def body(g_in, *refs):
        ins, outs = refs[:9], refs[9:]
        r0 = 0
        for i in range(3):
            rows = ins[i].shape[1]
            g = g_in[r0:r0 + rows, :]
            outs[i][0] = g
            outs[3 + i][0], outs[6 + i][0], outs[9 + i][0] = _adamw(ins[i][0], g, ins[3 + i][0], ins[6 + i][0])
            r0 += rows

    return pl.pallas_call(body, name="adamw_lora",
                          out_shape=[jax.ShapeDtypeStruct(a.shape, F32) for a in ws3] * 4)(g, *ws3, *ms3, *vs3)


def _rowsum_small(parts, loss8, after):
    def body(*refs):
        out = refs[-1]
        c0 = 0
        for ref in refs[:-2]:
            n = ref.shape[1]
            out[:, c0:c0 + n] = jnp.sum(ref[...], axis=0, keepdims=True)
            c0 += n

    k = len(parts) + 1
    return pl.pallas_call(body, name="rowsum_small", in_specs=[pl.BlockSpec(memory_space=pltpu.VMEM)] * k + [ANY],
                          out_shape=jax.ShapeDtypeStruct((1, N_SMALL), F32))(*parts, loss8, after)


def _reduce_adamw_small(sbuf, ws, ms, vs):
    nv = len(ws)

    def body(*refs):
        s_ref, ins, outs = refs[0], refs[1:1 + 3 * nv], refs[1 + 3 * nv:]
        tot = s_ref[0]
        for s in range(1, N_DEV):
            tot = tot + s_ref[s]
        c0 = 0
        for i in range(nv):
            rows, cols = ins[i].shape
            n = rows * cols
            for r in range(rows):
                outs[i][r:r + 1, :] = tot[:, c0 + cols * r:c0 + cols * (r + 1)]
            g = outs[i][...]
            outs[nv + i][...], outs[2 * nv + i][...], outs[3 * nv + i][...] = _adamw(
                ins[i][...], g, ins[nv + i][...], ins[2 * nv + i][...])
            c0 += n
        outs[-1][...] = tot[:, c0:]

    return pl.pallas_call(
        body, name="reduce_adamw_small",
        out_shape=[jax.ShapeDtypeStruct(a.shape, F32) for a in ws] * 4 + [jax.ShapeDtypeStruct((1, 128), F32)],
    )(sbuf, *ws, *ms, *vs)


_TRANSPOSED = ("w_in", "w_gate", "w_up")
_ROW_STACKED = MATS
_ADAM_TILE = {"w_in": 208, "w_out": 256, "w_gate": 176, "w_up": 176, "w_down": 176, "lora": 256}
_SUM_TILE = {"w_in": 208, "w_out": 128, "w_gate": 176, "w_up": 176, "w_down": 176, "lora": 128}


def _full(n, stacked):
    p, r, c = stacked.shape
    if n in _ROW_STACKED:
        return stacked.reshape(p * r, c)
    return jnp.transpose(stacked, (1, 0, 2)).reshape(r, p * c)


def _by_chip(n, full):
    if n in _ROW_STACKED:
        return full.reshape(N_CHIP, full.shape[0] // N_CHIP, full.shape[1])
    r, c = full.shape
    return jnp.transpose(full.reshape(r, N_CHIP, c // N_CHIP), (1, 0, 2))


def _with_own(land_shape, dtype, own, slot):
    return lax.dynamic_update_slice(lax.empty(land_shape, dtype), own[None], (slot,) + (0,) * own.ndim)


def _cast_into_slot(a, chip, tr, name, after=None):
    rows, cols = a.shape

    def body(chip_ref, a_ref, *rest):
        rest[-1][0] = a_ref[...].astype(BF16)

    extra = [] if after is None else [after]
    return pl.pallas_call(
        body, name=name,
        grid_spec=pltpu.PrefetchScalarGridSpec(
            num_scalar_prefetch=1, grid=(rows // tr,),
            in_specs=[pl.BlockSpec((tr, cols), lambda i, chip_ref: (i, 0))] + [ANY] * len(extra),
            out_specs=pl.BlockSpec((1, tr, cols), lambda i, chip_ref: (chip_ref[0], i, 0))),
        out_shape=jax.ShapeDtypeStruct((N_CHIP, rows, cols), BF16),
        compiler_params=_params(("parallel",)),
    )(chip, a, *extra)


def kernel(x, mix_norm_g, w_in, mu_shift, decay_w0, decay_w2, iclr_a0, iclr_a2, gate_g2, k_k, k_a, r_k, ln_x_w, ln_x_b, attn_out_g, w_out, ffn_norm_g, w_gate, w_up, w_down, final_norm_g, loss_target, m_mix_norm_g, m_w_in, m_mu_shift, m_decay_w0, m_decay_w2, m_iclr_a0, m_iclr_a2, m_gate_g2, m_k_k, m_k_a, m_r_k, m_ln_x_w, m_ln_x_b, m_attn_out_g, m_w_out, m_ffn_norm_g, m_w_gate, m_w_up, m_w_down, m_final_norm_g, v_mix_norm_g, v_w_in, v_mu_shift, v_decay_w0, v_decay_w2, v_iclr_a0, v_iclr_a2, v_gate_g2, v_k_k, v_k_a, v_r_k, v_ln_x_w, v_ln_x_b, v_attn_out_g, v_w_out, v_ffn_norm_g, v_w_gate, v_w_up, v_w_down, v_final_norm_g):
    names = ("mix_norm_g", "w_in", "mu_shift", "decay_w0", "decay_w2", "iclr_a0", "iclr_a2", "gate_g2", "k_k", "k_a",
             "r_k", "ln_x_w", "ln_x_b", "attn_out_g", "w_out", "ffn_norm_g", "w_gate", "w_up", "w_down", "final_norm_g")
    w = dict(zip(names, (mix_norm_g, w_in, mu_shift, decay_w0, decay_w2, iclr_a0, iclr_a2, gate_g2, k_k, k_a, r_k,
                         ln_x_w, ln_x_b, attn_out_g, w_out, ffn_norm_g, w_gate, w_up, w_down, final_norm_g)))
    m = dict(zip(names, (m_mix_norm_g, m_w_in, m_mu_shift, m_decay_w0, m_decay_w2, m_iclr_a0, m_iclr_a2, m_gate_g2,
                         m_k_k, m_k_a, m_r_k, m_ln_x_w, m_ln_x_b, m_attn_out_g, m_w_out, m_ffn_norm_g, m_w_gate,
                         m_w_up, m_w_down, m_final_norm_g)))
    v = dict(zip(names, (v_mix_norm_g, v_w_in, v_mu_shift, v_decay_w0, v_decay_w2, v_iclr_a0, v_iclr_a2, v_gate_g2,
                         v_k_k, v_k_a, v_r_k, v_ln_x_w, v_ln_x_b, v_attn_out_g, v_w_out, v_ffn_norm_g, v_w_gate,
                         v_w_up, v_w_down, v_final_norm_g)))
    first = ("w_in", "lora")
    rest = ("w_out", "w_gate", "w_up", "w_down")
    xi, yi, ci = lax.axis_index("x"), lax.axis_index("y"), lax.axis_index("c")
    my_chip, my_dev = 2 * xi + yi, 4 * xi + 2 * yi + ci
    gather, scatter = ("gather",) * 4, ("scatter",) * 4

    def stored(d):
        out = {n: jnp.transpose(d[n][0]) if n in _TRANSPOSED else d[n][0] for n in MATS}
        out["lora"] = jnp.concatenate([d[n][0] for n in LORAS], axis=0)
        return out

    ws, ms, vs = stored(w), stored(m), stored(v)
    chip = jnp.reshape(my_chip, (1,)).astype(jnp.int32)
    early = _swap_start(None, [_cast_into_slot(ws["w_in"], chip, _ADAM_TILE["w_in"], "cast_w_in")], gather[:1],
                        "gather_first_start")
    lora_own, _ = lax.optimization_barrier((ws["lora"], early[4]))
    early_lora = _swap_start(None, [_with_own((N_CHIP,) + ws["lora"].shape, F32, lora_own, my_chip)], gather[:1],
                             "gather_lora_start")
    lands = [_cast_into_slot(ws[n], chip, _ADAM_TILE[n], "cast_" + n, after=early[4]) for n in rest]
    gather_rest = ("gather", "gather", "whole", "whole")
    ssem, rsem, srcs_thru, lands_thru, tok = _swap_start(None, lands, gather_rest, "gather_rest_start")
    lora_all, = _wait_and_swap(early_lora[0], early_lora[1], early_lora[3], tok, gather[:1], "gather_lora_wait_halves")
    w2p, a2p, g2m = _unstack_lora(lora_all)
    win_all, = _wait_and_swap(early[0], early[1], early[3], w2p, gather[:1], "gather_first_wait_halves")
    win = _full("w_in", win_all)

    vecs = {n: w[n].reshape(1, sz) for n, sz in VECS}

    def get_rest(after):
        got_rest = _swap_wait(ssem, rsem, srcs_thru, lands_thru, after, gather_rest, "gather_rest_wait")
        swapped = _swap_gathered(got_rest[:2], "gather_rest_halves")
        return [_full(n, z) for n, z in zip(rest, [*swapped, *got_rest[2:]])]

    flight = []

    def my_half(g):
        h = g.shape[1] // 2
        return lax.dynamic_slice(g, (my_chip, ci * h, 0), (1, h, g.shape[2]))[0]

    def send_rest(gw):
        gs = [_by_chip(n, gw[n]) for n in rest]
        into = [_with_own((N_DEV,) + my_half(g).shape, BF16, my_half(g), my_dev) for g in gs]
        flight.extend(_swap_start(gs, into, scatter, "exchange_rest_start"))
        return flight[4]

    loss8, dx, gw, gv = _local_step(x[0], loss_target[0], win, vecs, w2p, a2p, g2m, get_rest, send_rest)

    core = jnp.reshape(ci, (1,)).astype(jnp.int32)
    gs = [_by_chip("w_in", gw["w_in"]),
          jnp.concatenate([_by_chip(n, gw[n]) for n in LORAS], axis=1).astype(BF16)]
    theirs = _sibling_halves(gs, "presum_halves")
    sums = [_add_halves(g, o, core, _SUM_TILE[n], "chipsum_" + n) for n, g, o in zip(first, gs, theirs)]
    own = [lax.dynamic_index_in_dim(s, my_chip, 0, keepdims=False) for s in sums]
    last = _swap_start(sums, [_with_own(s.shape, BF16, o, my_chip) for s, o in zip(sums, own)], ("chipsum",) * 2,
                       "exchange_first_start")
    small = _rowsum_small([gv[n] for n, _ in VECS], loss8, after=last[4])
    vecs_out = _swap_start([small], [_with_own((N_DEV,) + small.shape, F32, small, my_dev)], ("all",),
                           "exchange_vectors_start")


    def update(group, rbufs, tag):
        sums = [_reduce8(rb, core, _SUM_TILE[n], "reduce_" + n) for n, rb in zip(group, rbufs)]
        gsum = _join_halves(sums, "join_halves_" + tag)
        out = {}
        for n, g in zip(group, gsum):
            if n == "lora":
                r = _adamw_lora(g, *[[d[k] for k in LORAS] for d in (w, m, v)])
                for i, name in enumerate(LORAS):
                    out[name] = r[i::3]
            else:
                r = _adamw_call(g, ws[n][None], ms[n][None], vs[n][None], _ADAM_TILE[n], "adamw_" + n)
                out[n] = [jnp.transpose(z[0])[None] for z in r] if n in _TRANSPOSED else r
        return out, r[1]

    res, done = update(rest, _swap_wait(flight[0], flight[1], flight[2], flight[3], vecs_out[4], scatter,
                                        "exchange_rest_wait"), "rest")
    got = _swap_wait(last[0], last[1], last[2], last[3], done, ("chipsum",) * 2, "exchange_first_wait")
    res_first, done = update(first, got, "first")
    res.update(res_first)
    sbuf = _swap_wait(vecs_out[0], vecs_out[1], vecs_out[2], vecs_out[3], done, ("all",), "exchange_vectors_wait")[0]
    rows = lambda d: [d[n].reshape(-1, d[n].shape[-1]) for n, _ in VECS]
    small_res = _reduce_adamw_small(sbuf, rows(w), rows(m), rows(v))

    outs = []
    for k in range(4):
        piece = {n: r[k] for n, r in res.items()}
        for i, (n, _) in enumerate(VECS):
            piece[n] = small_res[k * len(VECS) + i].reshape(w[n].shape)
        outs.extend(piece[n] for n in names)
    return (small_res[-1][0, 0], dx[None], *outs)
```
